```python
import math
import jax, jax.numpy as jnp
from jax import lax
import numpy as np

D_MODEL = 1024
BATCH = 8
SEQ = 4096
DEPTH = 1

HEAD_DIM = 64
CONV_WIDTH = D_MODEL // 2
CONV_GROUPS = CONV_WIDTH // HEAD_DIM
CONV_K = 3
LRU_WIDTH = D_MODEL
LRU_HEADS = LRU_WIDTH // HEAD_DIM
LRU_CONV_K = 4
LRU_C = 8.0
MIX_WIDTH = CONV_WIDTH + LRU_WIDTH
IN_COLS = 3 * CONV_WIDTH + 2 * LRU_WIDTH
D_FF = 4 * D_MODEL
EPS = 1e-6

kernel_name = "hymba_style_shortconv_rglru_block"


def rmsnorm(x, g):
    xf = x.astype(jnp.float32)
    y = xf * lax.rsqrt(jnp.mean(xf * xf, axis=-1, keepdims=True) + EPS)
    return (y * g.astype(jnp.float32)).astype(x.dtype)


def causal_dwconv(x, w):
    k_len = w.shape[0]
    s = x.shape[1]
    xp = jnp.pad(x, ((0, 0), (k_len - 1, 0), (0, 0)))
    y = w[0] * xp[:, 0:s]
    for k in range(1, k_len):
        y = y + w[k] * xp[:, k:k + s]
    return y


def block_diag_linear(x, w, b):
    bt, s, _ = x.shape
    xh = x.reshape(bt, s, LRU_HEADS, HEAD_DIM)
    y = jnp.einsum('bshi,hij->bshj', xh, w).reshape(bt, s, LRU_WIDTH)
    return y + b


def rg_lru(x, w_a, b_a, w_x, b_x, lam):
    r = jax.nn.sigmoid(block_diag_linear(x, w_a, b_a).astype(jnp.float32))
    i = jax.nn.sigmoid(block_diag_linear(x, w_x, b_x).astype(jnp.float32))
    log_a = -LRU_C * r * jax.nn.softplus(-lam.astype(jnp.float32))
    a = jnp.exp(log_a)
    mult = jnp.sqrt(-jnp.expm1(2.0 * log_a))
    bx = mult * (i * x.astype(jnp.float32))

    def combine(lhs, rhs):
        a1, b1 = lhs
        a2, b2 = rhs
        return a1 * a2, a2 * b1 + b2

    _, h = lax.associative_scan(combine, (a, bx), axis=1)
    return h.astype(x.dtype)


def _fwd_setup_inputs(seed: int = 0) -> dict:
    key = jax.random.key(seed)
    ks = jax.random.split(key, 20)
    L = DEPTH
    nrm = jax.random.normal
    x = nrm(ks[0], (BATCH, SEQ, D_MODEL), jnp.float32)
    norm_mix_g = 1.0 + 0.02 * nrm(ks[1], (L, D_MODEL), jnp.float32)
    w_in = nrm(ks[2], (L, D_MODEL, IN_COLS), jnp.float32) * D_MODEL ** -0.5
    conv_w = nrm(ks[3], (L, CONV_K, CONV_WIDTH), jnp.float32) * CONV_K ** -0.5
    rnn_conv_w = nrm(ks[4], (L, LRU_CONV_K, LRU_WIDTH), jnp.float32) * LRU_CONV_K ** -0.5
    rnn_conv_b = 0.01 * nrm(ks[5], (L, LRU_WIDTH), jnp.float32)
    w_a = nrm(ks[6], (L, LRU_HEADS, HEAD_DIM, HEAD_DIM), jnp.float32) * HEAD_DIM ** -0.5
    b_a = 0.01 * nrm(ks[7], (L, LRU_WIDTH), jnp.float32)
    w_x = nrm(ks[8], (L, LRU_HEADS, HEAD_DIM, HEAD_DIM), jnp.float32) * HEAD_DIM ** -0.5
    b_x = 0.01 * nrm(ks[9], (L, LRU_WIDTH), jnp.float32)
    a_c = jax.random.uniform(ks[10], (L, LRU_WIDTH), jnp.float32, 0.9, 0.999)
    s = a_c ** (1.0 / LRU_C)
    lru_lambda = jnp.log(s) - jnp.log1p(-s)
    g_norm_conv = 1.0 + 0.02 * nrm(ks[11], (L, CONV_WIDTH), jnp.float32)
    g_norm_rnn = 1.0 + 0.02 * nrm(ks[12], (L, LRU_WIDTH), jnp.float32)
    w_out = nrm(ks[13], (L, MIX_WIDTH, D_MODEL), jnp.float32) * MIX_WIDTH ** -0.5
    norm_mlp_g = 1.0 + 0.02 * nrm(ks[14], (L, D_MODEL), jnp.float32)
    w_mlp_in = nrm(ks[15], (L, D_MODEL, D_FF), jnp.float32) * D_MODEL ** -0.5
    w_mlp_out = nrm(ks[16], (L, D_FF, D_MODEL), jnp.float32) * D_FF ** -0.5
    final_norm_g = 1.0 + 0.02 * nrm(ks[17], (D_MODEL,), jnp.float32)
    return {"x": x, "norm_mix_g": norm_mix_g, "w_in": w_in, "conv_w": conv_w,
            "rnn_conv_w": rnn_conv_w, "rnn_conv_b": rnn_conv_b,
            "w_a": w_a, "b_a": b_a, "w_x": w_x, "b_x": b_x,
            "lru_lambda": lru_lambda, "g_norm_conv": g_norm_conv,
            "g_norm_rnn": g_norm_rnn, "w_out": w_out, "norm_mlp_g": norm_mlp_g,
            "w_mlp_in": w_mlp_in, "w_mlp_out": w_mlp_out,
            "final_norm_g": final_norm_g}


def _fwd_reference(x, norm_mix_g, w_in, conv_w, rnn_conv_w, rnn_conv_b, w_a, b_a,
              w_x, b_x, lru_lambda, g_norm_conv, g_norm_rnn, w_out,
              norm_mlp_g, w_mlp_in, w_mlp_out, final_norm_g):
    split_pts = [CONV_WIDTH, 2 * CONV_WIDTH, 3 * CONV_WIDTH,
                 3 * CONV_WIDTH + LRU_WIDTH]
    for l in range(DEPTH):
        h = rmsnorm(x, norm_mix_g[l])
        u = jnp.einsum('bsd,dc->bsc', h, w_in[l])
        gate_b, gate_c, v, x_r, g = jnp.split(u, split_pts, axis=-1)
        y_conv = gate_b * causal_dwconv(gate_c * v, conv_w[l])
        xr = causal_dwconv(x_r, rnn_conv_w[l]) + rnn_conv_b[l]
        y_rnn = rg_lru(xr, w_a[l], b_a[l], w_x[l], b_x[l], lru_lambda[l])
        y_rnn = y_rnn * jax.nn.gelu(g)
        y = jnp.concatenate([rmsnorm(y_conv, g_norm_conv[l]),
                             rmsnorm(y_rnn, g_norm_rnn[l])], axis=-1)
        x = x + jnp.einsum('bsc,cd->bsd', y, w_out[l])
        h = rmsnorm(x, norm_mlp_g[l])
        z = jnp.square(jax.nn.relu(jnp.einsum('bsd,df->bsf', h, w_mlp_in[l])))
        x = x + jnp.einsum('bsf,fd->bsd', z, w_mlp_out[l])
    return rmsnorm(x, final_norm_g)


import jax as _jax
import jax.numpy as _jnp

TWIN_FORMAT = 'train_step'
FWD_PARAMS = ['x', 'norm_mix_g', 'w_in', 'conv_w', 'rnn_conv_w', 'rnn_conv_b', 'w_a', 'b_a', 'w_x', 'b_x', 'lru_lambda', 'g_norm_conv', 'g_norm_rnn', 'w_out', 'norm_mlp_g', 'w_mlp_in', 'w_mlp_out', 'final_norm_g']
TWIN_WEIGHTS = ['norm_mix_g', 'w_in', 'conv_w', 'rnn_conv_w', 'rnn_conv_b', 'w_a', 'b_a', 'w_x', 'b_x', 'lru_lambda', 'g_norm_conv', 'g_norm_rnn', 'w_out', 'norm_mlp_g', 'w_mlp_in', 'w_mlp_out', 'final_norm_g']
TWIN_DIFF_INPUT = 'x'
TWIN_INPUTS = ['x', 'norm_mix_g', 'w_in', 'conv_w', 'rnn_conv_w', 'rnn_conv_b', 'w_a', 'b_a', 'w_x', 'b_x', 'lru_lambda', 'g_norm_conv', 'g_norm_rnn', 'w_out', 'norm_mlp_g', 'w_mlp_in', 'w_mlp_out', 'final_norm_g', 'loss_target', 'm_norm_mix_g', 'm_w_in', 'm_conv_w', 'm_rnn_conv_w', 'm_rnn_conv_b', 'm_w_a', 'm_b_a', 'm_w_x', 'm_b_x', 'm_lru_lambda', 'm_g_norm_conv', 'm_g_norm_rnn', 'm_w_out', 'm_norm_mlp_g', 'm_w_mlp_in', 'm_w_mlp_out', 'm_final_norm_g', 'v_norm_mix_g', 'v_w_in', 'v_conv_w', 'v_rnn_conv_w', 'v_rnn_conv_b', 'v_w_a', 'v_b_a', 'v_w_x', 'v_b_x', 'v_lru_lambda', 'v_g_norm_conv', 'v_g_norm_rnn', 'v_w_out', 'v_norm_mlp_g', 'v_w_mlp_in', 'v_w_mlp_out', 'v_final_norm_g']
TWIN_OUTPUTS = ['loss', 'grad_x', 'grad_norm_mix_g', 'grad_w_in', 'grad_conv_w', 'grad_rnn_conv_w', 'grad_rnn_conv_b', 'grad_w_a', 'grad_b_a', 'grad_w_x', 'grad_b_x', 'grad_lru_lambda', 'grad_g_norm_conv', 'grad_g_norm_rnn', 'grad_w_out', 'grad_norm_mlp_g', 'grad_w_mlp_in', 'grad_w_mlp_out', 'grad_final_norm_g', 'delta_norm_mix_g', 'delta_w_in', 'delta_conv_w', 'delta_rnn_conv_w', 'delta_rnn_conv_b', 'delta_w_a', 'delta_b_a', 'delta_w_x', 'delta_b_x', 'delta_lru_lambda', 'delta_g_norm_conv', 'delta_g_norm_rnn', 'delta_w_out', 'delta_norm_mlp_g', 'delta_w_mlp_in', 'delta_w_mlp_out', 'delta_final_norm_g', 'new_m_norm_mix_g', 'new_m_w_in', 'new_m_conv_w', 'new_m_rnn_conv_w', 'new_m_rnn_conv_b', 'new_m_w_a', 'new_m_b_a', 'new_m_w_x', 'new_m_b_x', 'new_m_lru_lambda', 'new_m_g_norm_conv', 'new_m_g_norm_rnn', 'new_m_w_out', 'new_m_norm_mlp_g', 'new_m_w_mlp_in', 'new_m_w_mlp_out', 'new_m_final_norm_g', 'new_v_norm_mix_g', 'new_v_w_in', 'new_v_conv_w', 'new_v_rnn_conv_w', 'new_v_rnn_conv_b', 'new_v_w_a', 'new_v_b_a', 'new_v_w_x', 'new_v_b_x', 'new_v_lru_lambda', 'new_v_g_norm_conv', 'new_v_g_norm_rnn', 'new_v_w_out', 'new_v_norm_mlp_g', 'new_v_w_mlp_in', 'new_v_w_mlp_out', 'new_v_final_norm_g']
TWIN_LEAF_KINDS = {'loss': 'loss', 'grad_x': 'grad_x', 'grad_norm_mix_g': 'grad_w', 'grad_w_in': 'grad_w', 'grad_conv_w': 'grad_w', 'grad_rnn_conv_w': 'grad_w', 'grad_rnn_conv_b': 'grad_w', 'grad_w_a': 'grad_w', 'grad_b_a': 'grad_w', 'grad_w_x': 'grad_w', 'grad_b_x': 'grad_w', 'grad_lru_lambda': 'grad_w', 'grad_g_norm_conv': 'grad_w', 'grad_g_norm_rnn': 'grad_w', 'grad_w_out': 'grad_w', 'grad_norm_mlp_g': 'grad_w', 'grad_w_mlp_in': 'grad_w', 'grad_w_mlp_out': 'grad_w', 'grad_final_norm_g': 'grad_w', 'delta_norm_mix_g': 'delta_w', 'delta_w_in': 'delta_w', 'delta_conv_w': 'delta_w', 'delta_rnn_conv_w': 'delta_w', 'delta_rnn_conv_b': 'delta_w', 'delta_w_a': 'delta_w', 'delta_b_a': 'delta_w', 'delta_w_x': 'delta_w', 'delta_b_x': 'delta_w', 'delta_lru_lambda': 'delta_w', 'delta_g_norm_conv': 'delta_w', 'delta_g_norm_rnn': 'delta_w', 'delta_w_out': 'delta_w', 'delta_norm_mlp_g': 'delta_w', 'delta_w_mlp_in': 'delta_w', 'delta_w_mlp_out': 'delta_w', 'delta_final_norm_g': 'delta_w', 'new_m_norm_mix_g': 'new_m', 'new_m_w_in': 'new_m', 'new_m_conv_w': 'new_m', 'new_m_rnn_conv_w': 'new_m', 'new_m_rnn_conv_b': 'new_m', 'new_m_w_a': 'new_m', 'new_m_b_a': 'new_m', 'new_m_w_x': 'new_m', 'new_m_b_x': 'new_m', 'new_m_lru_lambda': 'new_m', 'new_m_g_norm_conv': 'new_m', 'new_m_g_norm_rnn': 'new_m', 'new_m_w_out': 'new_m', 'new_m_norm_mlp_g': 'new_m', 'new_m_w_mlp_in': 'new_m', 'new_m_w_mlp_out': 'new_m', 'new_m_final_norm_g': 'new_m', 'new_v_norm_mix_g': 'new_v', 'new_v_w_in': 'new_v', 'new_v_conv_w': 'new_v', 'new_v_rnn_conv_w': 'new_v', 'new_v_rnn_conv_b': 'new_v', 'new_v_w_a': 'new_v', 'new_v_b_a': 'new_v', 'new_v_w_x': 'new_v', 'new_v_b_x': 'new_v', 'new_v_lru_lambda': 'new_v', 'new_v_g_norm_conv': 'new_v', 'new_v_g_norm_rnn': 'new_v', 'new_v_w_out': 'new_v', 'new_v_norm_mlp_g': 'new_v', 'new_v_w_mlp_in': 'new_v', 'new_v_w_mlp_out': 'new_v', 'new_v_final_norm_g': 'new_v'}


def _forward(args):
    return _fwd_reference(*[args[k] for k in FWD_PARAMS])


def _output_shape():
    out = _jax.eval_shape(lambda: _forward(_fwd_setup_inputs(0)))
    return out.shape, out.dtype

N_MICROBATCH = 1
ADAM_LR = 0.001
ADAM_B1 = 0.9
ADAM_B2 = 0.999
ADAM_EPS = 1e-08
ADAM_WD = 0.01
ADAM_STEP = 10
PER_EXAMPLE_BATCH_AXIS = {'x': 0, 'loss_target': 0}
SHARED_INPUTS = []
_WEIGHT_DTYPES = {'norm_mix_g': _jnp.float32, 'w_in': _jnp.float32, 'conv_w': _jnp.float32, 'rnn_conv_w': _jnp.float32, 'rnn_conv_b': _jnp.float32, 'w_a': _jnp.float32, 'b_a': _jnp.float32, 'w_x': _jnp.float32, 'b_x': _jnp.float32, 'lru_lambda': _jnp.float32, 'g_norm_conv': _jnp.float32, 'g_norm_rnn': _jnp.float32, 'w_out': _jnp.float32, 'norm_mlp_g': _jnp.float32, 'w_mlp_in': _jnp.float32, 'w_mlp_out': _jnp.float32, 'final_norm_g': _jnp.float32}
MOMENT_SCALE = {'norm_mix_g': 2.107293e-01, 'w_in': 1.105397e-01, 'conv_w': 1.197603e-01, 'rnn_conv_w': 1.147009e-01, 'rnn_conv_b': 1.231331e+00, 'w_a': 4.229340e-02, 'b_a': 3.206370e-02, 'w_x': 7.611773e-02, 'b_x': 3.774828e-02, 'lru_lambda': 5.704777e-02, 'g_norm_conv': 1.178068e-01, 'g_norm_rnn': 1.140540e-01, 'w_out': 1.371852e-01, 'norm_mlp_g': 1.571532e-01, 'w_mlp_in': 6.759541e-02, 'w_mlp_out': 1.294041e-01, 'final_norm_g': 3.224579e+01}


def _to_microbatches(a, axis):
    t = _jnp.moveaxis(a, axis, 0)
    t = t.reshape((N_MICROBATCH, t.shape[0] // N_MICROBATCH) + t.shape[1:])
    return _jnp.moveaxis(t, 1, axis + 1)


def setup_inputs(seed: int = 0) -> dict:
    inp = _fwd_setup_inputs(seed)
    key = _jax.random.fold_in(_jax.random.key(seed), 7919)
    shape, _ = _output_shape()
    out = dict(inp)
    out["loss_target"] = _jax.random.normal(_jax.random.fold_in(key, 0), shape, _jnp.float32)
    for i, name in enumerate(TWIN_WEIGHTS):
        w = inp[name].astype(_jnp.float32)
        if MOMENT_SCALE is None:
            s = _jnp.sqrt(_jnp.mean(_jnp.square(w)) + 1e-30)
        else:
            s = MOMENT_SCALE[name]
        km, kv = _jax.random.split(_jax.random.fold_in(key, i + 1))
        out[name] = w
        out["m_" + name] = s * _jax.random.normal(km, w.shape, _jnp.float32)
        out["v_" + name] = (s * s) * _jax.random.uniform(kv, w.shape, _jnp.float32, 0.5, 1.5)
    if N_MICROBATCH > 1:
        for name, axis in PER_EXAMPLE_BATCH_AXIS.items():
            out[name] = _to_microbatches(out[name], axis)
    return {'x': out['x'], 'norm_mix_g': out['norm_mix_g'], 'w_in': out['w_in'], 'conv_w': out['conv_w'], 'rnn_conv_w': out['rnn_conv_w'], 'rnn_conv_b': out['rnn_conv_b'], 'w_a': out['w_a'], 'b_a': out['b_a'], 'w_x': out['w_x'], 'b_x': out['b_x'], 'lru_lambda': out['lru_lambda'], 'g_norm_conv': out['g_norm_conv'], 'g_norm_rnn': out['g_norm_rnn'], 'w_out': out['w_out'], 'norm_mlp_g': out['norm_mlp_g'], 'w_mlp_in': out['w_mlp_in'], 'w_mlp_out': out['w_mlp_out'], 'final_norm_g': out['final_norm_g'], 'loss_target': out['loss_target'], 'm_norm_mix_g': out['m_norm_mix_g'], 'm_w_in': out['m_w_in'], 'm_conv_w': out['m_conv_w'], 'm_rnn_conv_w': out['m_rnn_conv_w'], 'm_rnn_conv_b': out['m_rnn_conv_b'], 'm_w_a': out['m_w_a'], 'm_b_a': out['m_b_a'], 'm_w_x': out['m_w_x'], 'm_b_x': out['m_b_x'], 'm_lru_lambda': out['m_lru_lambda'], 'm_g_norm_conv': out['m_g_norm_conv'], 'm_g_norm_rnn': out['m_g_norm_rnn'], 'm_w_out': out['m_w_out'], 'm_norm_mlp_g': out['m_norm_mlp_g'], 'm_w_mlp_in': out['m_w_mlp_in'], 'm_w_mlp_out': out['m_w_mlp_out'], 'm_final_norm_g': out['m_final_norm_g'], 'v_norm_mix_g': out['v_norm_mix_g'], 'v_w_in': out['v_w_in'], 'v_conv_w': out['v_conv_w'], 'v_rnn_conv_w': out['v_rnn_conv_w'], 'v_rnn_conv_b': out['v_rnn_conv_b'], 'v_w_a': out['v_w_a'], 'v_b_a': out['v_b_a'], 'v_w_x': out['v_w_x'], 'v_b_x': out['v_b_x'], 'v_lru_lambda': out['v_lru_lambda'], 'v_g_norm_conv': out['v_g_norm_conv'], 'v_g_norm_rnn': out['v_g_norm_rnn'], 'v_w_out': out['v_w_out'], 'v_norm_mlp_g': out['v_norm_mlp_g'], 'v_w_mlp_in': out['v_w_mlp_in'], 'v_w_mlp_out': out['v_w_mlp_out'], 'v_final_norm_g': out['v_final_norm_g']}


def _loss(weights, diff, rest, loss_target):
    with _jax.named_scope("forward"):
        args = {**rest, TWIN_DIFF_INPUT: diff, **{k: w.astype(_WEIGHT_DTYPES[k]) for k, w in weights.items()}}
        y = _forward(args)
    with _jax.named_scope("loss_head"):
        err = _jnp.square(y.astype(_jnp.float32) - loss_target)
        return 0.5 * _jnp.sum(_jnp.mean(err, axis=-1)) if err.ndim else 0.5 * err


def _adamw(w, g, m, v):
    m = ADAM_B1 * m + (1.0 - ADAM_B1) * g
    v = ADAM_B2 * v + (1.0 - ADAM_B2) * _jnp.square(g)
    m_hat = m / (1.0 - ADAM_B1 ** ADAM_STEP)
    v_hat = v / (1.0 - ADAM_B2 ** ADAM_STEP)
    delta = -ADAM_LR * (m_hat / (_jnp.sqrt(v_hat) + ADAM_EPS) + ADAM_WD * w)
    return delta, m, v


def reference(x, norm_mix_g, w_in, conv_w, rnn_conv_w, rnn_conv_b, w_a, b_a, w_x, b_x, lru_lambda, g_norm_conv, g_norm_rnn, w_out, norm_mlp_g, w_mlp_in, w_mlp_out, final_norm_g, loss_target, m_norm_mix_g, m_w_in, m_conv_w, m_rnn_conv_w, m_rnn_conv_b, m_w_a, m_b_a, m_w_x, m_b_x, m_lru_lambda, m_g_norm_conv, m_g_norm_rnn, m_w_out, m_norm_mlp_g, m_w_mlp_in, m_w_mlp_out, m_final_norm_g, v_norm_mix_g, v_w_in, v_conv_w, v_rnn_conv_w, v_rnn_conv_b, v_w_a, v_b_a, v_w_x, v_b_x, v_lru_lambda, v_g_norm_conv, v_g_norm_rnn, v_w_out, v_norm_mlp_g, v_w_mlp_in, v_w_mlp_out, v_final_norm_g):
    given = dict(x=x, norm_mix_g=norm_mix_g, w_in=w_in, conv_w=conv_w, rnn_conv_w=rnn_conv_w, rnn_conv_b=rnn_conv_b, w_a=w_a, b_a=b_a, w_x=w_x, b_x=b_x, lru_lambda=lru_lambda, g_norm_conv=g_norm_conv, g_norm_rnn=g_norm_rnn, w_out=w_out, norm_mlp_g=norm_mlp_g, w_mlp_in=w_mlp_in, w_mlp_out=w_mlp_out, final_norm_g=final_norm_g, loss_target=loss_target, m_norm_mix_g=m_norm_mix_g, m_w_in=m_w_in, m_conv_w=m_conv_w, m_rnn_conv_w=m_rnn_conv_w, m_rnn_conv_b=m_rnn_conv_b, m_w_a=m_w_a, m_b_a=m_b_a, m_w_x=m_w_x, m_b_x=m_b_x, m_lru_lambda=m_lru_lambda, m_g_norm_conv=m_g_norm_conv, m_g_norm_rnn=m_g_norm_rnn, m_w_out=m_w_out, m_norm_mlp_g=m_norm_mlp_g, m_w_mlp_in=m_w_mlp_in, m_w_mlp_out=m_w_mlp_out, m_final_norm_g=m_final_norm_g, v_norm_mix_g=v_norm_mix_g, v_w_in=v_w_in, v_conv_w=v_conv_w, v_rnn_conv_w=v_rnn_conv_w, v_rnn_conv_b=v_rnn_conv_b, v_w_a=v_w_a, v_b_a=v_b_a, v_w_x=v_w_x, v_b_x=v_b_x, v_lru_lambda=v_lru_lambda, v_g_norm_conv=v_g_norm_conv, v_g_norm_rnn=v_g_norm_rnn, v_w_out=v_w_out, v_norm_mlp_g=v_norm_mlp_g, v_w_mlp_in=v_w_mlp_in, v_w_mlp_out=v_w_mlp_out, v_final_norm_g=v_final_norm_g)
    weights = {n: given[n] for n in TWIN_WEIGHTS}
    shared = {n: given[n] for n in SHARED_INPUTS}
    per_example = {n: given[n] for n in ['x']}
    grad_fn = _jax.value_and_grad(_loss, argnums=(0, 1))

    def one_microbatch(ex, loss_target):
        ex = dict(ex)
        diff = ex.pop(TWIN_DIFF_INPUT)
        return grad_fn(weights, diff, {**shared, **ex}, loss_target)

    if N_MICROBATCH == 1:
        loss, (grad_w, grad_x) = one_microbatch(per_example, given["loss_target"])
    else:
        def body(carry, xs):
            loss_sum, grad_sum = carry
            l_k, (gw_k, gx_k) = one_microbatch(xs[0], xs[1])
            with _jax.named_scope("update"):
                return (loss_sum + l_k, _jax.tree.map(_jnp.add, grad_sum, gw_k)), gx_k

        init = (_jnp.zeros((), _jnp.float32), _jax.tree.map(_jnp.zeros_like, weights))
        (loss, grad_w), grad_x = _jax.lax.scan(body, init, (per_example, given["loss_target"]))
    with _jax.named_scope("update"):
        delta_w, new_m, new_v = {}, {}, {}
        for n in TWIN_WEIGHTS:
            delta_w[n], new_m[n], new_v[n] = _adamw(weights[n], grad_w[n], given["m_" + n], given["v_" + n])
    return (loss, grad_x, *[grad_w[n] for n in TWIN_WEIGHTS], *[delta_w[n] for n in TWIN_WEIGHTS],
            *[new_m[n] for n in TWIN_WEIGHTS], *[new_v[n] for n in TWIN_WEIGHTS])
```

```python
import functools

import jax
import jax.numpy as jnp
from jax import lax
from jax.experimental import pallas as pl
from jax.experimental.pallas import tpu as pltpu

F32 = jnp.float32
BF16 = jnp.bfloat16

D_MODEL = 1024
HEAD_DIM = 64
CONV_WIDTH = 512
LRU_WIDTH = 1024
MIX_WIDTH = CONV_WIDTH + LRU_WIDTH
IN_COLS = 3 * CONV_WIDTH + 2 * LRU_WIDTH
D_FF = 4 * D_MODEL
GROUP = 256
EPS = 1e-6
LRU_C = 8.0
N_DEV = 8
SUB = 8

OFF_GB, OFF_GC, OFF_V, OFF_XR, OFF_G = 0, 512, 1024, 1536, 2560

ADAM_LR, ADAM_B1, ADAM_B2, ADAM_EPS, ADAM_WD, ADAM_STEP = 0.001, 0.9, 0.999, 1e-08, 0.01, 10
BC1 = 1.0 - ADAM_B1 ** ADAM_STEP
BC2 = 1.0 - ADAM_B2 ** ADAM_STEP

MIB = 1024 * 1024
MESH = pl.DeviceIdType.MESH

VEC_ROWS = 32
ROW_GF, ROW_GMLP, ROW_LOSS = 0, 1, 2
ROW_GNC, ROW_GNR, ROW_BR, ROW_BA, ROW_BX, ROW_LAM, ROW_CW, ROW_RW = 8, 9, 10, 11, 12, 13, 14, 17
ROW_GMIX = 24
ACC_GNC, ACC_GNR, ACC_BR, ACC_BA, ACC_BX, ACC_SP, ACC_CW, ACC_RW, N_ACC = 0, 1, 2, 3, 4, 5, 6, 9, 13


def _params(semantics=None, vmem_mib=48):
    return pltpu.CompilerParams(dimension_semantics=semantics, vmem_limit_bytes=vmem_mib * MIB)


def _rms(x):
    return lax.rsqrt(jnp.mean(x * x, axis=-1, keepdims=True) + EPS)


def _rms_bwd(dy, xhat, r, g):
    dyh = dy * g
    return r * (dyh - xhat * jnp.mean(dyh * xhat, axis=-1, keepdims=True))


def _sigmoid(x):
    return 1.0 / (1.0 + jnp.exp(-x))


def _gelu(x):
    c0, c1 = 0.7978845608028654, 0.044715
    t = jnp.tanh(c0 * (x + c1 * x * x * x))
    ge = 0.5 * x * (1.0 + t)
    dge = 0.5 * (1.0 + t) + 0.5 * x * (1.0 - t * t) * c0 * (1.0 + 3.0 * c1 * x * x)
    return ge, dge


def _softplus_neg(lam):
    z = -lam
    e = jnp.exp(-jnp.abs(z))
    return jnp.maximum(z, 0.0) + jnp.where(e < 1e-4, e * (1.0 - 0.5 * e), jnp.log(1.0 + e))


def _lru_gates(pa, px, sp_c):
    ra = _sigmoid(pa)
    ii = _sigmoid(px)
    la = -ra * sp_c
    a = jnp.exp(la)
    x2 = 2.0 * la
    series = -x2 * (1.0 + x2 * (0.5 + x2 * (1.0 / 6.0 + x2 * (1.0 / 24.0))))
    mult = jnp.sqrt(jnp.where(x2 > -0.01, series, 1.0 - a * a))
    return ra, ii, a, mult


def _down(cur, prev, s, row):
    return jnp.where(row >= s, pltpu.roll(cur, s, 0), pltpu.roll(prev, s, 0))


def _up(cur, nxt, s, row):
    return jnp.where(row < SUB - s, pltpu.roll(cur, SUB - s, 0), pltpu.roll(nxt, SUB - s, 0))


def _scan8_fwd(a, b, row):
    for s in (1, 2, 4):
        m = row >= s
        a_sh = pltpu.roll(a, s, 0)
        b_sh = pltpu.roll(b, s, 0)
        b = jnp.where(m, a * b_sh + b, b)
        a = jnp.where(m, a * a_sh, a)
    return a, b


def _scan8_rev(a, b, row):
    for s in (1, 2, 4):
        m = row < SUB - s
        a_sh = pltpu.roll(a, SUB - s, 0)
        b_sh = pltpu.roll(b, SUB - s, 0)
        b = jnp.where(m, a * b_sh + b, b)
        a = jnp.where(m, a * a_sh, a)
    return a, b


def _group_mask(shape):
    r = lax.broadcasted_iota(jnp.int32, shape, 0)
    c = lax.broadcasted_iota(jnp.int32, shape, 1)
    return ((r % GROUP) // HEAD_DIM) == (c // HEAD_DIM)


def _expand_heads(w):
    j = lax.broadcasted_iota(jnp.int32, (HEAD_DIM, GROUP), 0)
    c = lax.broadcasted_iota(jnp.int32, (HEAD_DIM, GROUP), 1)
    spread = (c % HEAD_DIM == j).astype(BF16)
    e = jnp.dot(w.astype(BF16), spread, preferred_element_type=F32)
    return jnp.where(_group_mask(e.shape), e, 0.0).astype(BF16)


def _fold_heads(p):
    p = jnp.where(_group_mask(p.shape), p, 0.0)
    c = lax.broadcasted_iota(jnp.int32, (GROUP, HEAD_DIM), 0)
    j = lax.broadcasted_iota(jnp.int32, (GROUP, HEAD_DIM), 1)
    fold = (c % HEAD_DIM == j).astype(BF16)
    hi = p.astype(BF16)
    rest = p - hi.astype(F32)
    mid = rest.astype(BF16)
    lo = (rest - mid.astype(F32)).astype(BF16)
    dot = functools.partial(jnp.dot, preferred_element_type=F32)
    return dot(hi, fold) + dot(mid, fold) + dot(lo, fold)


def _block_diag_apply(xb, wbd_ref):
    parts = [jnp.dot(xb[:, g * GROUP:(g + 1) * GROUP], wbd_ref[g * GROUP:(g + 1) * GROUP, :],
                     preferred_element_type=F32) for g in range(LRU_WIDTH // GROUP)]
    return jnp.concatenate(parts, axis=1)


def _block_diag_apply_t(db, wbd_ref):
    parts = [lax.dot_general(db[:, g * GROUP:(g + 1) * GROUP], wbd_ref[g * GROUP:(g + 1) * GROUP, :],
                             (((1,), (1,)), ((), ())), preferred_element_type=F32)
             for g in range(LRU_WIDTH // GROUP)]
    return jnp.concatenate(parts, axis=1)


def _dot_nt(a, b):
    return lax.dot_general(a, b, (((1,), (1,)), ((), ())), preferred_element_type=F32)


def _dot_tn(a, b):
    return lax.dot_general(a, b, (((0,), (0,)), ((), ())), preferred_element_type=F32)


def _place():
    x, y, c = lax.axis_index("x"), lax.axis_index("y"), lax.axis_index("c")
    return x, y, c


def _all_gather_weights(w_in, w_out, w_mlp_in, w_mlp_out, conv_w, rnn_conv_w):
    n_in = w_in.shape[1]
    n_arr = 5

    def body(win_ref, wout_ref, w1_ref, w2_ref, cw_ref, rw_ref,
             o_win, o_wout, o_w1, o_w2, o_cp, padbuf, send_sems, recv_sems):
        x, y, c = _place()
        me = (x, y, c)
        my_id = 4 * x + 2 * y + c
        sibling = (x, y, 1 - c)
        chips = [(1 - x, y), (x, 1 - y), (1 - x, 1 - y)]
        outs = [o_win, o_wout, o_w1, o_w2, o_cp]

        padbuf[...] = jnp.zeros(padbuf.shape, F32)
        padbuf[:, 0:n_in] = win_ref[...]
        o_win[my_id] = padbuf[...].T[0:n_in, :].astype(BF16)
        o_wout[my_id] = wout_ref[...].astype(BF16)
        o_w1[my_id] = w1_ref[...].astype(BF16)
        o_w2[my_id] = w2_ref[...].astype(BF16)
        o_cp[my_id] = jnp.zeros(o_cp.shape[1:], F32)
        o_cp[my_id, 0:3, 0:64] = cw_ref[...]
        o_cp[my_id, 3:7, :] = rw_ref[...]

        def copy(arr, k, block, to):
            blk = outs[arr].at[4 * block[0] + 2 * block[1] + block[2]]
            return pltpu.make_async_remote_copy(
                src_ref=blk, dst_ref=blk, send_sem=send_sems.at[arr, k], recv_sem=recv_sems.at[arr, k],
                device_id=to, device_id_type=MESH)

        first = []
        for arr in range(n_arr):
            first.append(copy(arr, 0, me, sibling))
            first += [copy(arr, 1 + j, me, (*chip, c)) for j, chip in enumerate(chips)]
        for cp in first:
            cp.start()
        passed = []
        for j, chip in enumerate(chips):
            for arr in range(n_arr):
                copy(arr, 1 + j, (*chip, c), me).wait_recv()
                fwd = copy(arr, 4 + j, (*chip, c), sibling)
                fwd.start()
                passed.append(fwd)
        for arr in range(n_arr):
            copy(arr, 0, sibling, me).wait_recv()
            for j, chip in enumerate(chips):
                copy(arr, 4 + j, (*chip, 1 - c), me).wait_recv()
        for cp in first + passed:
            cp.wait_send()

    vm = pl.BlockSpec(memory_space=pltpu.VMEM)
    shapes = (
        jax.ShapeDtypeStruct((N_DEV, n_in, D_MODEL), BF16),
        jax.ShapeDtypeStruct((N_DEV,) + w_out.shape, BF16),
        jax.ShapeDtypeStruct((N_DEV,) + w_mlp_in.shape, BF16),
        jax.ShapeDtypeStruct((N_DEV,) + w_mlp_out.shape, BF16),
        jax.ShapeDtypeStruct((N_DEV, 8, 128), F32),
    )
    return pl.pallas_call(
        body, out_shape=shapes, in_specs=[vm] * 6, out_specs=[vm] * 5,
        scratch_shapes=[pltpu.VMEM((D_MODEL, 512), F32),
                        pltpu.SemaphoreType.DMA((n_arr, 7)), pltpu.SemaphoreType.DMA((n_arr, 7))],
        compiler_params=_params(vmem_mib=56), name="all_gather_weights",
    )(w_in, w_out, w_mlp_in, w_mlp_out, conv_w, rnn_conv_w)


def _reduce_scatter(g, name):
    _, rows, cols = g.shape

    def body(g_ref, o_ref, sib, hsend, hrecv, d_send, d_recv, i_send, i_recv):
        x, y, c = _place()
        sibling = (x, y, 1 - c)
        chips = [(x, y), (1 - x, y), (x, 1 - y), (1 - x, 1 - y)]

        def gid(chip, core):
            return 4 * chip[0] + 2 * chip[1] + core

        def d2d(q):
            return pltpu.make_async_remote_copy(
                src_ref=g_ref.at[gid(chips[q], 1 - c)], dst_ref=sib.at[q],
                send_sem=d_send.at[q], recv_sem=d_recv.at[q], device_id=sibling, device_id_type=MESH)

        def ici(q):
            return pltpu.make_async_remote_copy(
                src_ref=hsend.at[q - 1], dst_ref=hrecv.at[q - 1],
                send_sem=i_send.at[q - 1], recv_sem=i_recv.at[q - 1],
                device_id=(*chips[q], c), device_id_type=MESH)

        for q in (1, 2, 3, 0):
            d2d(q).start()
        for q in (1, 2, 3):
            d2d(q).wait_recv()
            hsend[q - 1] = (g_ref[gid(chips[q], c)].astype(F32) + sib[q].astype(F32)).astype(BF16)
            ici(q).start()
        d2d(0).wait_recv()
        acc = g_ref[gid(chips[0], c)].astype(F32) + sib[0].astype(F32)
        for q in (1, 2, 3):
            ici(q).wait_recv()
            acc = acc + hrecv[q - 1].astype(F32)
        o_ref[...] = acc
        for q in range(4):
            d2d(q).wait_send()
        for q in (1, 2, 3):
            ici(q).wait_send()

    vm = pl.BlockSpec(memory_space=pltpu.VMEM)
    return pl.pallas_call(
        body, out_shape=jax.ShapeDtypeStruct((rows, cols), F32), in_specs=[vm], out_specs=vm,
        scratch_shapes=[pltpu.VMEM((4, rows, cols), BF16), pltpu.VMEM((3, rows, cols), BF16),
                        pltpu.VMEM((3, rows, cols), BF16),
                        pltpu.SemaphoreType.DMA((4,)), pltpu.SemaphoreType.DMA((4,)),
                        pltpu.SemaphoreType.DMA((3,)), pltpu.SemaphoreType.DMA((3,))],
        compiler_params=_params(vmem_mib=48), name=name,
    )(g)


def _all_reduce_small(vec_m, vec_b, vec_x, wab):
    wrows = wab.shape[0] // N_DEV

    def body(vm_ref, vb_ref, vx_ref, w_ref, o_vec, o_w, vpack, vrecv, wrecv, wred,
             v_send, v_recv, w_send, w_recv, b_send, b_recv):
        x, y, c = _place()
        my_id = 4 * x + 2 * y + c

        def peer(k):
            return (x ^ ((k >> 2) & 1), y ^ ((k >> 1) & 1), c ^ (k & 1))

        def pid(k):
            p = peer(k)
            return 4 * p[0] + 2 * p[1] + p[2]

        vpack[0:8, :] = vm_ref[...]
        vpack[8:24, :] = vb_ref[...]
        vpack[24:32, :] = vx_ref[...]
        vrecv[my_id] = vpack[...]

        def vcopy(k):
            return pltpu.make_async_remote_copy(
                src_ref=vpack, dst_ref=vrecv.at[my_id], send_sem=v_send.at[k], recv_sem=v_recv.at[k],
                device_id=peer(k), device_id_type=MESH)

        def wcopy(k):
            return pltpu.make_async_remote_copy(
                src_ref=w_ref.at[pl.ds(pl.multiple_of(pid(k) * wrows, SUB), wrows), :], dst_ref=wrecv.at[k],
                send_sem=w_send.at[k], recv_sem=w_recv.at[k], device_id=peer(k), device_id_type=MESH)

        def bcopy(k):
            mine = o_w.at[pl.ds(pl.multiple_of(my_id * wrows, SUB), wrows), :]
            return pltpu.make_async_remote_copy(
                src_ref=wred, dst_ref=mine, send_sem=b_send.at[k], recv_sem=b_recv.at[k],
                device_id=peer(k), device_id_type=MESH)

        for k in range(1, N_DEV):
            vcopy(k).start()
            wcopy(k).start()
        red = w_ref[pl.ds(pl.multiple_of(my_id * wrows, SUB), wrows), :]
        for k in range(1, N_DEV):
            wcopy(k).wait_recv()
            red = red + wrecv[k]
        wred[...] = red
        o_w[pl.ds(pl.multiple_of(my_id * wrows, SUB), wrows), :] = red
        for k in range(1, N_DEV):
            bcopy(k).start()
        for k in range(1, N_DEV):
            vcopy(k).wait_recv()
        tot = vrecv[0]
        for s in range(1, N_DEV):
            tot = tot + vrecv[s]
        o_vec[...] = tot
        for k in range(1, N_DEV):
            bcopy(k).wait_recv()
        for k in range(1, N_DEV):
            vcopy(k).wait_send()
            wcopy(k).wait_send()
            bcopy(k).wait_send()

    vm = pl.BlockSpec(memory_space=pltpu.VMEM)
    dma7 = pltpu.SemaphoreType.DMA((N_DEV,))
    return pl.pallas_call(
        body, out_shape=(jax.ShapeDtypeStruct((VEC_ROWS, D_MODEL), F32), jax.ShapeDtypeStruct(wab.shape, F32)),
        in_specs=[vm] * 4, out_specs=[vm] * 2,
        scratch_shapes=[pltpu.VMEM((VEC_ROWS, D_MODEL), F32), pltpu.VMEM((N_DEV, VEC_ROWS, D_MODEL), F32),
                        pltpu.VMEM((N_DEV, wrows, HEAD_DIM), F32), pltpu.VMEM((wrows, HEAD_DIM), F32),
                        dma7, dma7, dma7, dma7, dma7, dma7],
        compiler_params=_params(vmem_mib=32), name="all_reduce_small",
    )(vec_m, vec_b, vec_x, wab)


def _in_proj(x, g_mix, win_t, tm):
    t_len = x.shape[0]

    def body(x_ref, g_ref, w_ref, u_ref, h_ref):
        xv = x_ref[...]
        h = (xv * _rms(xv) * g_ref[...]).astype(BF16)
        h_ref[...] = h
        u_ref[...] = _dot_nt(h, w_ref[...])

    return pl.pallas_call(
        body, grid=(t_len // tm,),
        in_specs=[pl.BlockSpec((tm, D_MODEL), lambda i: (i, 0)), pl.BlockSpec((1, D_MODEL), lambda i: (0, 0)),
                  pl.BlockSpec((IN_COLS, D_MODEL), lambda i: (0, 0))],
        out_specs=[pl.BlockSpec((tm, IN_COLS), lambda i: (i, 0)), pl.BlockSpec((tm, D_MODEL), lambda i: (i, 0))],
        out_shape=[jax.ShapeDtypeStruct((t_len, IN_COLS), F32), jax.ShapeDtypeStruct((t_len, D_MODEL), BF16)],
        compiler_params=_params(("arbitrary",), 56), name="in_proj",
    )(x, g_mix, win_t)


def _conv3_chunk(u_ref, r, cv_prev, cw, row):
    gb = u_ref[pl.ds(r, SUB), OFF_GB:OFF_GB + CONV_WIDTH]
    gc = u_ref[pl.ds(r, SUB), OFF_GC:OFF_GC + CONV_WIDTH]
    v = u_ref[pl.ds(r, SUB), OFF_V:OFF_V + CONV_WIDTH]
    cv = gc * v
    cv_m1 = _down(cv, cv_prev, 1, row)
    cv_m2 = _down(cv, cv_prev, 2, row)
    cq = cw[2:3, :] * cv + cw[1:2, :] * cv_m1 + cw[0:1, :] * cv_m2
    return gb, gc, v, cv, cv_m1, cv_m2, cq


def _conv4_chunk(u_ref, r, xin_prev, rw, rb, row):
    xin = u_ref[pl.ds(r, SUB), OFF_XR:OFF_XR + LRU_WIDTH]
    m1 = _down(xin, xin_prev, 1, row)
    m2 = _down(xin, xin_prev, 2, row)
    m3 = _down(xin, xin_prev, 3, row)
    xr = rw[3:4, :] * xin + rw[2:3, :] * m1 + rw[1:2, :] * m2 + rw[0:1, :] * m3 + rb
    return xin, m1, m2, m3, xr


def _mixer_fwd(u, x, conv_w, rnn_conv_w, rnn_conv_b, wa, b_a, wx, b_x, lam, gnc, gnr, w_out, tm):
    t_len = x.shape[0]
    n_chunks = tm // SUB

    def body(u_ref, x_ref, cw_ref, rw_ref, rb_ref, wa_ref, ba_ref, wx_ref, bx_ref, lam_ref, gnc_ref, gnr_ref,
             wout_ref, x1_ref, hs_ref, y_ref,
             y_s, xr_s, pa_s, px_s, wabd, wxbd, cv_car, xin_car, h_car):
        @pl.when(pl.program_id(0) == 0)
        def _():
            cv_car[...] = jnp.zeros(cv_car.shape, F32)
            xin_car[...] = jnp.zeros(xin_car.shape, F32)
            h_car[...] = jnp.zeros(h_car.shape, F32)
            wabd[...] = _expand_heads(wa_ref[...])
            wxbd[...] = _expand_heads(wx_ref[...])

        row_c = lax.broadcasted_iota(jnp.int32, (SUB, CONV_WIDTH), 0)
        row_r = lax.broadcasted_iota(jnp.int32, (SUB, LRU_WIDTH), 0)
        cw = cw_ref[...]
        rw = rw_ref[...]
        rb = rb_ref[...]
        g_c = gnc_ref[...]
        g_r = gnr_ref[...]
        sp_c = LRU_C * _softplus_neg(lam_ref[...])

        def convs(i, carry):
            cv_prev, xin_prev = carry
            r = pl.multiple_of(i * SUB, SUB)
            gb, _, _, cv, _, _, cq = _conv3_chunk(u_ref, r, cv_prev, cw, row_c)
            y_c = gb * cq
            y_s[pl.ds(r, SUB), 0:CONV_WIDTH] = y_c * _rms(y_c) * g_c
            xin, _, _, _, xr = _conv4_chunk(u_ref, r, xin_prev, rw, rb, row_r)
            xr_s[pl.ds(r, SUB), :] = xr
            return cv, xin

        cv_last, xin_last = lax.fori_loop(0, n_chunks, convs, (cv_car[...], xin_car[...]))
        cv_car[...] = cv_last
        xin_car[...] = xin_last

        xrb = xr_s[...].astype(BF16)
        pa_s[...] = _block_diag_apply(xrb, wabd) + ba_ref[...]
        px_s[...] = _block_diag_apply(xrb, wxbd) + bx_ref[...]

        def recur(i, h_prev):
            r = pl.multiple_of(i * SUB, SUB)
            xr = xr_s[pl.ds(r, SUB), :]
            _, ii, a, mult = _lru_gates(pa_s[pl.ds(r, SUB), :], px_s[pl.ds(r, SUB), :], sp_c)
            a_cum, b_cum = _scan8_fwd(a, mult * ii * xr, row_r)
            h = a_cum * h_prev + b_cum
            hs_ref[pl.ds(r, SUB), :] = h
            ge, _ = _gelu(u_ref[pl.ds(r, SUB), OFF_G:OFF_G + LRU_WIDTH])
            y_r = h * ge
            y_s[pl.ds(r, SUB), CONV_WIDTH:MIX_WIDTH] = y_r * _rms(y_r) * g_r
            return h[SUB - 1:SUB, :]

        h_car[...] = lax.fori_loop(0, n_chunks, recur, h_car[...])

        yb = y_s[...].astype(BF16)
        y_ref[...] = yb
        x1_ref[...] = x_ref[...] + jnp.dot(yb, wout_ref[...], preferred_element_type=F32)

    row_tile = lambda w: pl.BlockSpec((tm, w), lambda i: (i, 0))
    whole = lambda a: pl.BlockSpec(a.shape, lambda i: (0,) * a.ndim)
    smalls = (conv_w, rnn_conv_w, rnn_conv_b, wa, b_a, wx, b_x, lam, gnc, gnr, w_out)
    return pl.pallas_call(
        body, grid=(t_len // tm,),
        in_specs=[row_tile(IN_COLS), row_tile(D_MODEL)] + [whole(a) for a in smalls],
        out_specs=[row_tile(D_MODEL), row_tile(LRU_WIDTH), row_tile(MIX_WIDTH)],
        out_shape=[jax.ShapeDtypeStruct((t_len, D_MODEL), F32), jax.ShapeDtypeStruct((t_len, LRU_WIDTH), F32),
                   jax.ShapeDtypeStruct((t_len, MIX_WIDTH), BF16)],
        scratch_shapes=[pltpu.VMEM((tm, MIX_WIDTH), F32), pltpu.VMEM((tm, LRU_WIDTH), F32),
                        pltpu.VMEM((tm, LRU_WIDTH), F32), pltpu.VMEM((tm, LRU_WIDTH), F32),
                        pltpu.VMEM((LRU_WIDTH, GROUP), BF16), pltpu.VMEM((LRU_WIDTH, GROUP), BF16),
                        pltpu.VMEM((SUB, CONV_WIDTH), F32), pltpu.VMEM((SUB, LRU_WIDTH), F32),
                        pltpu.VMEM((1, LRU_WIDTH), F32)],
        compiler_params=_params(("arbitrary",), 56), name="mixer_fwd",
    )(u, x, *smalls)


def _mlp_fwd_bwd(x1, target, g_mlp, g_f, w1, w2, tm):
    t_len = x1.shape[0]
    n_blk, _, blk = w1.shape

    def body(x1_ref, tg_ref, gm_ref, gf_ref, w1_hbm, w2_hbm,
             dx1_ref, z_ref, dpre_ref, h2_ref, dx2_ref, vec_ref, w1_s, w2_s, rp_s, sem):
        @pl.when(pl.program_id(0) == 0)
        def _():
            c1 = pltpu.make_async_copy(w1_hbm, w1_s, sem.at[0])
            c2 = pltpu.make_async_copy(w2_hbm, w2_s, sem.at[1])
            c1.start()
            c2.start()
            vec_ref[...] = jnp.zeros(vec_ref.shape, F32)
            c1.wait()
            c2.wait()

        x1v = x1_ref[...]
        g_m = gm_ref[...]
        g_o = gf_ref[...]
        r2 = _rms(x1v)
        x1h = x1v * r2
        h2 = (x1h * g_m).astype(BF16)
        h2_ref[...] = h2
        x2 = x1v
        for k in range(n_blk):
            rp = jnp.maximum(jnp.dot(h2, w1_s[k], preferred_element_type=F32), 0.0)
            rp_s[:, k * blk:(k + 1) * blk] = rp
            zb = (rp * rp).astype(BF16)
            z_ref[:, k * blk:(k + 1) * blk] = zb
            x2 = x2 + jnp.dot(zb, w2_s[k * blk:(k + 1) * blk, :], preferred_element_type=F32)
        r3 = _rms(x2)
        x2h = x2 * r3
        err = x2h * g_o - tg_ref[...]
        dout = err * (1.0 / D_MODEL)
        vec_ref[ROW_LOSS:ROW_LOSS + 1, :] += (0.5 / D_MODEL) * jnp.sum(err * err, axis=0, keepdims=True)
        vec_ref[ROW_GF:ROW_GF + 1, :] += jnp.sum(dout * x2h, axis=0, keepdims=True)
        dx2 = _rms_bwd(dout, x2h, r3, g_o)
        dx2b = dx2.astype(BF16)
        dx2_ref[...] = dx2b
        dh2 = jnp.zeros((tm, D_MODEL), F32)
        for k in range(n_blk):
            dz = _dot_nt(dx2b, w2_s[k * blk:(k + 1) * blk, :])
            dpb = (dz * 2.0 * rp_s[:, k * blk:(k + 1) * blk]).astype(BF16)
            dpre_ref[:, k * blk:(k + 1) * blk] = dpb
            dh2 = dh2 + _dot_nt(dpb, w1_s[k])
        vec_ref[ROW_GMLP:ROW_GMLP + 1, :] += jnp.sum(dh2 * x1h, axis=0, keepdims=True)
        dx1_ref[...] = dx2 + _rms_bwd(dh2, x1h, r2, g_m)

    row_tile = lambda w: pl.BlockSpec((tm, w), lambda i: (i, 0))
    vec_spec = pl.BlockSpec((1, D_MODEL), lambda i: (0, 0))
    hbm = pl.BlockSpec(memory_space=pl.ANY)
    return pl.pallas_call(
        body, grid=(t_len // tm,),
        in_specs=[row_tile(D_MODEL), row_tile(D_MODEL), vec_spec, vec_spec, hbm, hbm],
        out_specs=[row_tile(D_MODEL), row_tile(D_FF), row_tile(D_FF), row_tile(D_MODEL), row_tile(D_MODEL),
                   pl.BlockSpec((SUB, D_MODEL), lambda i: (0, 0))],
        out_shape=[jax.ShapeDtypeStruct((t_len, D_MODEL), F32), jax.ShapeDtypeStruct((t_len, D_FF), BF16),
                   jax.ShapeDtypeStruct((t_len, D_FF), BF16), jax.ShapeDtypeStruct((t_len, D_MODEL), BF16),
                   jax.ShapeDtypeStruct((t_len, D_MODEL), BF16), jax.ShapeDtypeStruct((SUB, D_MODEL), F32)],
        scratch_shapes=[pltpu.VMEM(w1.shape, BF16), pltpu.VMEM(w2.shape, BF16), pltpu.VMEM((tm, D_FF), F32),
                        pltpu.SemaphoreType.DMA((2,))],
        compiler_params=_params(("arbitrary",), 56), name="mlp_fwd_bwd",
    )(x1, target, g_mlp, g_f, w1, w2)


def _mlp_weight_grads(h2, dpre, z, dx2, tk):
    t_len = h2.shape[0]
    blk = D_FF // N_DEV

    def body(h2_ref, dp_ref, z_ref, dx2_ref, g1_ref, g2_ref, acc1, acc2):
        j = pl.program_id(1)

        @pl.when(j == 0)
        def _():
            acc1[...] = jnp.zeros(acc1.shape, F32)
            acc2[...] = jnp.zeros(acc2.shape, F32)

        acc1[...] += _dot_tn(h2_ref[...], dp_ref[...])
        acc2[...] += _dot_tn(z_ref[...], dx2_ref[...])

        @pl.when(j == pl.num_programs(1) - 1)
        def _():
            g1_ref[0] = acc1[...].astype(BF16)
            g2_ref[0] = acc2[...].astype(BF16)

    return pl.pallas_call(
        body, grid=(N_DEV, t_len // tk),
        in_specs=[pl.BlockSpec((tk, D_MODEL), lambda k, j: (j, 0)), pl.BlockSpec((tk, blk), lambda k, j: (j, k)),
                  pl.BlockSpec((tk, blk), lambda k, j: (j, k)), pl.BlockSpec((tk, D_MODEL), lambda k, j: (j, 0))],
        out_specs=[pl.BlockSpec((1, D_MODEL, blk), lambda k, j: (k, 0, 0)),
                   pl.BlockSpec((1, blk, D_MODEL), lambda k, j: (k, 0, 0))],
        out_shape=[jax.ShapeDtypeStruct((N_DEV, D_MODEL, blk), BF16), jax.ShapeDtypeStruct((N_DEV, blk, D_MODEL), BF16)],
        scratch_shapes=[pltpu.VMEM((D_MODEL, blk), F32), pltpu.VMEM((blk, D_MODEL), F32)],
        compiler_params=_params(("arbitrary", "arbitrary"), 40), name="mlp_weight_grads",
    )(h2, dpre, z, dx2)


def _mixer_bwd(u, hs, dx1, conv_w, rnn_conv_w, rnn_conv_b, wa, b_a, wx, b_x, lam, gnc, gnr, w_out, tm):
    t_len = u.shape[0]
    n_tiles = t_len // tm
    n_chunks = tm // SUB
    per_tile = tm // SUB

    def body(u_ref, up_ref, hs_ref, hp_ref, dx1_ref, cw_ref, rw_ref, rb_ref, wa_ref, ba_ref, wx_ref, bx_ref,
             lam_ref, gnc_ref, gnr_ref, wout_ref, du_ref, vec_ref, wab_ref,
             du_s, dy_s, xr_s, pa_s, px_s, dpa_s, dpx_s, dxr_s, wabd, wxbd, acc, dwa_acc, dwx_acc,
             a_car, dh_car, dcq_car, dxr_car):
        step = pl.program_id(0)
        has_prev = (step < n_tiles - 1).astype(F32)

        @pl.when(step == 0)
        def _():
            acc[...] = jnp.zeros(acc.shape, F32)
            dwa_acc[...] = jnp.zeros(dwa_acc.shape, F32)
            dwx_acc[...] = jnp.zeros(dwx_acc.shape, F32)
            a_car[...] = jnp.ones(a_car.shape, F32)
            dh_car[...] = jnp.zeros(dh_car.shape, F32)
            dcq_car[...] = jnp.zeros(dcq_car.shape, F32)
            dxr_car[...] = jnp.zeros(dxr_car.shape, F32)
            wabd[...] = _expand_heads(wa_ref[...])
            wxbd[...] = _expand_heads(wx_ref[...])

        row_c = lax.broadcasted_iota(jnp.int32, (SUB, CONV_WIDTH), 0)
        row_r = lax.broadcasted_iota(jnp.int32, (SUB, LRU_WIDTH), 0)
        cw = cw_ref[...]
        rw = rw_ref[...]
        rb = rb_ref[...]
        g_c = gnc_ref[...]
        g_r = gnr_ref[...]
        sp_c = LRU_C * _softplus_neg(lam_ref[...])

        up = up_ref[...] * has_prev
        cv_before = up[:, OFF_GC:OFF_GC + CONV_WIDTH] * up[:, OFF_V:OFF_V + CONV_WIDTH]
        xin_before = up[:, OFF_XR:OFF_XR + LRU_WIDTH]
        hs_before = hp_ref[...] * has_prev

        dy_s[...] = _dot_nt(dx1_ref[...].astype(BF16), wout_ref[...])

        def conv4_fwd(i, xin_prev):
            r = pl.multiple_of(i * SUB, SUB)
            xin, _, _, _, xr = _conv4_chunk(u_ref, r, xin_prev, rw, rb, row_r)
            xr_s[pl.ds(r, SUB), :] = xr
            return xin

        lax.fori_loop(0, n_chunks, conv4_fwd, xin_before)
        xrb = xr_s[...].astype(BF16)
        pa_s[...] = _block_diag_apply(xrb, wabd) + ba_ref[...]
        px_s[...] = _block_diag_apply(xrb, wxbd) + bx_ref[...]

        def recur_bwd(j, carry):
            a_later, dh_later = carry
            i = n_chunks - 1 - j
            r = pl.multiple_of(i * SUB, SUB)
            rp = pl.multiple_of(jnp.maximum(i - 1, 0) * SUB, SUB)
            xr = xr_s[pl.ds(r, SUB), :]
            hs_c = hs_ref[pl.ds(r, SUB), :]
            hs_prev = jnp.where(i == 0, hs_before, hs_ref[pl.ds(rp, SUB), :])
            h_m1 = _down(hs_c, hs_prev, 1, row_r)
            ra, ii, a, mult = _lru_gates(pa_s[pl.ds(r, SUB), :], px_s[pl.ds(r, SUB), :], sp_c)
            ge, dge = _gelu(u_ref[pl.ds(r, SUB), OFF_G:OFF_G + LRU_WIDTH])
            y_r = hs_c * ge
            rr = _rms(y_r)
            yhat = y_r * rr
            dyn = dy_s[pl.ds(r, SUB), CONV_WIDTH:MIX_WIDTH]
            acc[ACC_GNR] += dyn * yhat
            dy_r = _rms_bwd(dyn, yhat, rr, g_r)
            du_s[pl.ds(r, SUB), OFF_G:OFF_G + LRU_WIDTH] = dy_r * hs_c * dge
            a_cum, d_cum = _scan8_rev(_up(a, a_later, 1, row_r), dy_r * ge, row_r)
            dh = a_cum * dh_later + d_cum
            dmult = dh * ii * xr
            dii = dh * mult * xr
            dxr_s[pl.ds(r, SUB), :] = dh * mult * ii
            dla = dh * h_m1 * a - dmult * a * a / mult
            acc[ACC_SP] += -dla * ra
            dpa = -dla * sp_c * ra * (1.0 - ra)
            dpx = dii * ii * (1.0 - ii)
            acc[ACC_BA] += dpa
            acc[ACC_BX] += dpx
            dpa_s[pl.ds(r, SUB), :] = dpa
            dpx_s[pl.ds(r, SUB), :] = dpx
            return a, dh[0:1, :]

        a_first, dh_first = lax.fori_loop(0, n_chunks, recur_bwd, (a_car[...], dh_car[...]))
        a_car[...] = a_first
        dh_car[...] = dh_first

        dpab = dpa_s[...].astype(BF16)
        dpxb = dpx_s[...].astype(BF16)
        dxr_s[...] += _block_diag_apply_t(dpab, wabd) + _block_diag_apply_t(dpxb, wxbd)
        for g in range(LRU_WIDTH // GROUP):
            cols = slice(g * GROUP, (g + 1) * GROUP)
            dwa_acc[cols, :] += _dot_tn(xrb[:, cols], dpab[:, cols])
            dwx_acc[cols, :] += _dot_tn(xrb[:, cols], dpxb[:, cols])

        def convs_bwd(j, carry):
            dcq_later, dxr_later = carry
            i = n_chunks - 1 - j
            r = pl.multiple_of(i * SUB, SUB)
            rp = pl.multiple_of(jnp.maximum(i - 1, 0) * SUB, SUB)
            cv_prev = jnp.where(i == 0, cv_before,
                                u_ref[pl.ds(rp, SUB), OFF_GC:OFF_GC + CONV_WIDTH]
                                * u_ref[pl.ds(rp, SUB), OFF_V:OFF_V + CONV_WIDTH])
            gb, gc, v, cv, cv_m1, cv_m2, cq = _conv3_chunk(u_ref, r, cv_prev, cw, row_c)
            y_c = gb * cq
            rc = _rms(y_c)
            yhat = y_c * rc
            dyn = dy_s[pl.ds(r, SUB), 0:CONV_WIDTH]
            acc[ACC_GNC, :, 0:CONV_WIDTH] += dyn * yhat
            dy_c = _rms_bwd(dyn, yhat, rc, g_c)
            dcq = dy_c * gb
            dcv = (cw[2:3, :] * dcq + cw[1:2, :] * _up(dcq, dcq_later, 1, row_c)
                   + cw[0:1, :] * _up(dcq, dcq_later, 2, row_c))
            acc[ACC_CW + 2, :, 0:CONV_WIDTH] += dcq * cv
            acc[ACC_CW + 1, :, 0:CONV_WIDTH] += dcq * cv_m1
            acc[ACC_CW + 0, :, 0:CONV_WIDTH] += dcq * cv_m2
            du_s[pl.ds(r, SUB), OFF_GB:OFF_GB + CONV_WIDTH] = dy_c * cq
            du_s[pl.ds(r, SUB), OFF_GC:OFF_GC + CONV_WIDTH] = dcv * v
            du_s[pl.ds(r, SUB), OFF_V:OFF_V + CONV_WIDTH] = dcv * gc

            xin_prev = jnp.where(i == 0, xin_before, u_ref[pl.ds(rp, SUB), OFF_XR:OFF_XR + LRU_WIDTH])
            xin, m1, m2, m3, _ = _conv4_chunk(u_ref, r, xin_prev, rw, rb, row_r)
            dxr = dxr_s[pl.ds(r, SUB), :]
            du_s[pl.ds(r, SUB), OFF_XR:OFF_XR + LRU_WIDTH] = (
                rw[3:4, :] * dxr + rw[2:3, :] * _up(dxr, dxr_later, 1, row_r)
                + rw[1:2, :] * _up(dxr, dxr_later, 2, row_r) + rw[0:1, :] * _up(dxr, dxr_later, 3, row_r))
            acc[ACC_RW + 3] += dxr * xin
            acc[ACC_RW + 2] += dxr * m1
            acc[ACC_RW + 1] += dxr * m2
            acc[ACC_RW + 0] += dxr * m3
            acc[ACC_BR] += dxr
            return dcq, dxr

        dcq_first, dxr_first = lax.fori_loop(0, n_chunks, convs_bwd, (dcq_car[...], dxr_car[...]))
        dcq_car[...] = dcq_first
        dxr_car[...] = dxr_first

        du_ref[...] = du_s[...].astype(BF16)

        @pl.when(step == n_tiles - 1)
        def _():
            vec_ref[...] = jnp.zeros(vec_ref.shape, F32)
            rows = {ACC_GNC: ROW_GNC, ACC_GNR: ROW_GNR, ACC_BR: ROW_BR, ACC_BA: ROW_BA, ACC_BX: ROW_BX}
            for k in range(3):
                rows[ACC_CW + k] = ROW_CW + k
            for k in range(4):
                rows[ACC_RW + k] = ROW_RW + k
            for slot, out_row in rows.items():
                o = out_row - ROW_GNC
                vec_ref[o:o + 1, :] = jnp.sum(acc[slot], axis=0, keepdims=True)
            lam_v = lam_ref[...]
            dsp = jnp.sum(acc[ACC_SP], axis=0, keepdims=True)
            o = ROW_LAM - ROW_GNC
            vec_ref[o:o + 1, :] = -dsp * LRU_C / (1.0 + jnp.exp(lam_v))
            wab_ref[0:LRU_WIDTH, :] = _fold_heads(dwa_acc[...])
            wab_ref[LRU_WIDTH:2 * LRU_WIDTH, :] = _fold_heads(dwx_acc[...])

    rev = lambda w: pl.BlockSpec((tm, w), lambda s: (n_tiles - 1 - s, 0))
    before = lambda w: pl.BlockSpec((SUB, w), lambda s: (jnp.maximum((n_tiles - 1 - s) * per_tile - 1, 0), 0))
    whole = lambda a: pl.BlockSpec(a.shape, lambda s: (0,) * a.ndim)
    smalls = (conv_w, rnn_conv_w, rnn_conv_b, wa, b_a, wx, b_x, lam, gnc, gnr, w_out)
    full = lambda w: pltpu.VMEM((tm, w), F32)
    return pl.pallas_call(
        body, grid=(n_tiles,),
        in_specs=[rev(IN_COLS), before(IN_COLS), rev(LRU_WIDTH), before(LRU_WIDTH), rev(D_MODEL)]
        + [whole(a) for a in smalls],
        out_specs=[rev(IN_COLS), pl.BlockSpec((16, D_MODEL), lambda s: (0, 0)),
                   pl.BlockSpec((2 * LRU_WIDTH, HEAD_DIM), lambda s: (0, 0))],
        out_shape=[jax.ShapeDtypeStruct((t_len, IN_COLS), BF16), jax.ShapeDtypeStruct((16, D_MODEL), F32),
                   jax.ShapeDtypeStruct((2 * LRU_WIDTH, HEAD_DIM), F32)],
        scratch_shapes=[full(IN_COLS), full(MIX_WIDTH), full(LRU_WIDTH), full(LRU_WIDTH), full(LRU_WIDTH),
                        full(LRU_WIDTH), full(LRU_WIDTH), full(LRU_WIDTH),
                        pltpu.VMEM((LRU_WIDTH, GROUP), BF16), pltpu.VMEM((LRU_WIDTH, GROUP), BF16),
                        pltpu.VMEM((N_ACC, SUB, LRU_WIDTH), F32),
                        pltpu.VMEM((LRU_WIDTH, GROUP), F32), pltpu.VMEM((LRU_WIDTH, GROUP), F32),
                        pltpu.VMEM((SUB, LRU_WIDTH), F32), pltpu.VMEM((1, LRU_WIDTH), F32),
                        pltpu.VMEM((SUB, CONV_WIDTH), F32), pltpu.VMEM((SUB, LRU_WIDTH), F32)],
        compiler_params=_params(("arbitrary",), 56), name="mixer_bwd",
    )(u, u, hs, hs, dx1, *smalls)


def _in_proj_bwd(du, dx1, x, g_mix, win_t, tm):
    t_len = x.shape[0]

    def body(du_ref, dx1_ref, x_ref, g_ref, w_ref, dx_ref, vec_ref):
        @pl.when(pl.program_id(0) == 0)
        def _():
            vec_ref[...] = jnp.zeros(vec_ref.shape, F32)

        dh = jnp.dot(du_ref[...], w_ref[...], preferred_element_type=F32)
        xv = x_ref[...]
        r1 = _rms(xv)
        xh = xv * r1
        vec_ref[0:1, :] += jnp.sum(dh * xh, axis=0, keepdims=True)
        dx_ref[...] = dx1_ref[...] + _rms_bwd(dh, xh, r1, g_ref[...])

    row_tile = lambda w: pl.BlockSpec((tm, w), lambda i: (i, 0))
    return pl.pallas_call(
        body, grid=(t_len // tm,),
        in_specs=[row_tile(IN_COLS), row_tile(D_MODEL), row_tile(D_MODEL), pl.BlockSpec((1, D_MODEL), lambda i: (0, 0)),
                  pl.BlockSpec((IN_COLS, D_MODEL), lambda i: (0, 0))],
        out_specs=[row_tile(D_MODEL), pl.BlockSpec((SUB, D_MODEL), lambda i: (0, 0))],
        out_shape=[jax.ShapeDtypeStruct((t_len, D_MODEL), F32), jax.ShapeDtypeStruct((SUB, D_MODEL), F32)],
        compiler_params=_params(("arbitrary",), 56), name="in_proj_bwd",
    )(du, dx1, x, g_mix, win_t)


def _tn_weight_grad(a, b, tk, name):
    t_len, m = a.shape
    n = b.shape[1]

    def body(a_ref, b_ref, o_ref, acc):
        j = pl.program_id(0)

        @pl.when(j == 0)
        def _():
            acc[...] = jnp.zeros(acc.shape, F32)

        acc[...] += _dot_tn(a_ref[...].astype(BF16), b_ref[...].astype(BF16))

        @pl.when(j == pl.num_programs(0) - 1)
        def _():
            o_ref[...] = acc[...].astype(BF16)

    return pl.pallas_call(
        body, grid=(t_len // tk,),
        in_specs=[pl.BlockSpec((tk, m), lambda j: (j, 0)), pl.BlockSpec((tk, n), lambda j: (j, 0))],
        out_specs=pl.BlockSpec((m, n), lambda j: (0, 0)),
        out_shape=jax.ShapeDtypeStruct((m, n), BF16),
        scratch_shapes=[pltpu.VMEM((m, n), F32)],
        compiler_params=_params(("arbitrary",), 56), name=name,
    )(a, b)


def _adamw(w, g, m, v):
    m = ADAM_B1 * m + (1.0 - ADAM_B1) * g
    v = ADAM_B2 * v + (1.0 - ADAM_B2) * (g * g)
    delta = -ADAM_LR * ((m / BC1) / (jnp.sqrt(v / BC2) + ADAM_EPS) + ADAM_WD * w)
    return delta, m, v


def _update_sharded(g, w, m, v, rows_blk, name, transposed=False):
    rows, cols = w.shape
    pad_cols = -(-cols // 128) * 128

    def body(g_ref, w_ref, m_ref, v_ref, og, od, om, ov, *scratch):
        if transposed:
            padbuf, turned = scratch
            padbuf[...] = jnp.zeros(padbuf.shape, F32)
            padbuf[0:cols, :] = g_ref[...]
            turned[...] = padbuf[...].T
            gv = turned[:, 0:cols]
        else:
            gv = g_ref[...]
        delta, mn, vn = _adamw(w_ref[...], gv, m_ref[...], v_ref[...])
        og[...] = gv
        od[...] = delta
        om[...] = mn
        ov[...] = vn

    blk = pl.BlockSpec((rows_blk, cols), lambda i: (i, 0))
    g_spec = pl.BlockSpec((cols, rows_blk), lambda i: (0, i)) if transposed else blk
    shape = jax.ShapeDtypeStruct((rows, cols), F32)
    return pl.pallas_call(
        body, grid=(rows // rows_blk,), in_specs=[g_spec, blk, blk, blk], out_specs=[blk] * 4, out_shape=[shape] * 4,
        scratch_shapes=[pltpu.VMEM((pad_cols, rows_blk), F32), pltpu.VMEM((rows_blk, pad_cols), F32)] if transposed else [],
        compiler_params=_params(("arbitrary",), 32), name=name,
    )(g, w, m, v)


def _update_small(vsum, wsum, g_cw, g_rw, weights, moments_m, moments_v):
    n = len(weights)

    def body(*refs):
        vs, ws, gcw, grw = refs[0:4]
        w_refs = refs[4:4 + n]
        m_refs = refs[4 + n:4 + 2 * n]
        v_refs = refs[4 + 2 * n:4 + 3 * n]
        outs = refs[4 + 3 * n:]
        loss_ref = outs[0]
        loss_ref[...] = jnp.sum(vs[ROW_LOSS:ROW_LOSS + 1, :], axis=1, keepdims=True)
        grads = [
            vs[ROW_GMIX:ROW_GMIX + 1, :], gcw[...], grw[...], vs[ROW_BR:ROW_BR + 1, :],
            ws[0:LRU_WIDTH, :], vs[ROW_BA:ROW_BA + 1, :], ws[LRU_WIDTH:2 * LRU_WIDTH, :], vs[ROW_BX:ROW_BX + 1, :],
            vs[ROW_LAM:ROW_LAM + 1, :], vs[ROW_GNC:ROW_GNC + 1, 0:CONV_WIDTH], vs[ROW_GNR:ROW_GNR + 1, :],
            vs[ROW_GMLP:ROW_GMLP + 1, :], vs[ROW_GF:ROW_GF + 1, :],
        ]
        for k in range(n):
            gk = grads[k]
            delta, mn, vn = _adamw(w_refs[k][...], gk, m_refs[k][...], v_refs[k][...])
            outs[1 + 4 * k][...] = gk
            outs[2 + 4 * k][...] = delta
            outs[3 + 4 * k][...] = mn
            outs[4 + 4 * k][...] = vn

    vm = pl.BlockSpec(memory_space=pltpu.VMEM)
    out_shape = [jax.ShapeDtypeStruct((1, 1), F32)]
    for w in weights:
        out_shape += [jax.ShapeDtypeStruct(w.shape, F32)] * 4
    args = (vsum, wsum, g_cw, g_rw, *weights, *moments_m, *moments_v)
    return pl.pallas_call(
        body, out_shape=out_shape, in_specs=[vm] * len(args), out_specs=[vm] * len(out_shape),
        compiler_params=_params(vmem_mib=32), name="update_small",
    )(*args)


def kernel(x, norm_mix_g, w_in, conv_w, rnn_conv_w, rnn_conv_b, w_a, b_a, w_x, b_x, lru_lambda, g_norm_conv, g_norm_rnn, w_out, norm_mlp_g, w_mlp_in, w_mlp_out, final_norm_g, loss_target, m_norm_mix_g, m_w_in, m_conv_w, m_rnn_conv_w, m_rnn_conv_b, m_w_a, m_b_a, m_w_x, m_b_x, m_lru_lambda, m_g_norm_conv, m_g_norm_rnn, m_w_out, m_norm_mlp_g, m_w_mlp_in, m_w_mlp_out, m_final_norm_g, v_norm_mix_g, v_w_in, v_conv_w, v_rnn_conv_w, v_rnn_conv_b, v_w_a, v_b_a, v_w_x, v_b_x, v_lru_lambda, v_g_norm_conv, v_g_norm_rnn, v_w_out, v_norm_mlp_g, v_w_mlp_in, v_w_mlp_out, v_final_norm_g):
    t_len = x.shape[1]
    my_id = 4 * lax.axis_index("x") + 2 * lax.axis_index("y") + lax.axis_index("c")
    tm = min(256, t_len)
    tk = min(512, t_len)

    xs = x.reshape(t_len, D_MODEL)
    tgt = loss_target.reshape(t_len, D_MODEL)
    flat = lambda a: a.reshape(a.shape[-2:]) if a.ndim == 3 else a.reshape(1, -1)
    heads = lambda a: a.reshape(LRU_WIDTH, HEAD_DIM)

    win_blk, wout_blk, w1_blk, w2_blk, cpack = _all_gather_weights(
        flat(w_in), flat(w_out), flat(w_mlp_in), flat(w_mlp_out), flat(conv_w), flat(rnn_conv_w))
    win_t = win_blk.reshape(IN_COLS, D_MODEL)
    wout_f = wout_blk.reshape(MIX_WIDTH, D_MODEL)
    w2_f = w2_blk.reshape(D_FF, D_MODEL)
    conv_full = jnp.transpose(cpack[:, 0:3, 0:64], (1, 0, 2)).reshape(3, CONV_WIDTH)
    rnn_full = jnp.transpose(cpack[:, 3:7, :], (1, 0, 2)).reshape(4, LRU_WIDTH)

    mixer_small = (conv_full, rnn_full, flat(rnn_conv_b), heads(w_a), flat(b_a), heads(w_x), flat(b_x),
                   flat(lru_lambda), flat(g_norm_conv), flat(g_norm_rnn), wout_f)

    u, h = _in_proj(xs, flat(norm_mix_g), win_t, tm)
    x1, hs, y = _mixer_fwd(u, xs, *mixer_small, tm)
    dx1, z, dpre, h2, dx2, vec_m = _mlp_fwd_bwd(x1, tgt, flat(norm_mlp_g), flat(final_norm_g), w1_blk, w2_f, tm)
    g_w1, g_w2 = _mlp_weight_grads(h2, dpre, z, dx2, tk)
    g_wout = _tn_weight_grad(y, dx1, tk, "w_out_grad")
    du, vec_b, wab = _mixer_bwd(u, hs, dx1, *mixer_small, tm)
    grad_x, vec_x = _in_proj_bwd(du, dx1, xs, flat(norm_mix_g), win_t, tm)
    g_win_t = _tn_weight_grad(du, h, tk, "w_in_grad")

    r_w2 = _reduce_scatter(g_w2, "reduce_scatter_w_mlp_out")
    r_w1 = _reduce_scatter(g_w1, "reduce_scatter_w_mlp_in")
    r_wout = _reduce_scatter(g_wout.reshape(N_DEV, MIX_WIDTH // N_DEV, D_MODEL), "reduce_scatter_w_out")
    r_win_t = _reduce_scatter(g_win_t.reshape(N_DEV, IN_COLS // N_DEV, D_MODEL), "reduce_scatter_w_in")
    vsum, wsum = _all_reduce_small(vec_m, vec_b, vec_x, wab)

    up_win = _update_sharded(r_win_t, flat(w_in), flat(m_w_in), flat(v_w_in), 256, "update_w_in", transposed=True)
    up_wout = _update_sharded(r_wout, flat(w_out), flat(m_w_out), flat(v_w_out), 96, "update_w_out")
    up_w1 = _update_sharded(r_w1, flat(w_mlp_in), flat(m_w_mlp_in), flat(v_w_mlp_in), 256, "update_w_mlp_in")
    up_w2 = _update_sharded(r_w2, flat(w_mlp_out), flat(m_w_mlp_out), flat(v_w_mlp_out), 256, "update_w_mlp_out")

    g_cw = lax.dynamic_slice(vsum, (ROW_CW, 64 * my_id), (3, 64))
    g_rw = lax.dynamic_slice(vsum, (ROW_RW, 128 * my_id), (4, 128))
    small_w = (norm_mix_g, conv_w, rnn_conv_w, rnn_conv_b, w_a, b_a, w_x, b_x, lru_lambda, g_norm_conv, g_norm_rnn,
               norm_mlp_g, final_norm_g)
    small_m = (m_norm_mix_g, m_conv_w, m_rnn_conv_w, m_rnn_conv_b, m_w_a, m_b_a, m_w_x, m_b_x, m_lru_lambda,
               m_g_norm_conv, m_g_norm_rnn, m_norm_mlp_g, m_final_norm_g)
    small_v = (v_norm_mix_g, v_conv_w, v_rnn_conv_w, v_rnn_conv_b, v_w_a, v_b_a, v_w_x, v_b_x, v_lru_lambda,
               v_g_norm_conv, v_g_norm_rnn, v_norm_mlp_g, v_final_norm_g)
    is_heads = (False, False, False, False, True, False, True, False, False, False, False, False, False)
    as2d = lambda arrs: [heads(a) if hd else flat(a) for a, hd in zip(arrs, is_heads)]
    small_out = _update_small(vsum, wsum, g_cw, g_rw, as2d(small_w), as2d(small_m), as2d(small_v))
    loss = small_out[0].reshape(())

    names = ["norm_mix_g", "w_in", "conv_w", "rnn_conv_w", "rnn_conv_b", "w_a", "b_a", "w_x", "b_x", "lru_lambda",
             "g_norm_conv", "g_norm_rnn", "w_out", "norm_mlp_g", "w_mlp_in", "w_mlp_out", "final_norm_g"]
    originals = dict(zip(names, (norm_mix_g, w_in, conv_w, rnn_conv_w, rnn_conv_b, w_a, b_a, w_x, b_x, lru_lambda,
                                 g_norm_conv, g_norm_rnn, w_out, norm_mlp_g, w_mlp_in, w_mlp_out, final_norm_g)))
    results = {"w_in": up_win, "w_out": up_wout, "w_mlp_in": up_w1, "w_mlp_out": up_w2}
    small_names = ["norm_mix_g", "conv_w", "rnn_conv_w", "rnn_conv_b", "w_a", "b_a", "w_x", "b_x", "lru_lambda",
                   "g_norm_conv", "g_norm_rnn", "norm_mlp_g", "final_norm_g"]
    for k, nm in enumerate(small_names):
        results[nm] = small_out[1 + 4 * k:5 + 4 * k]
    out = [loss, grad_x.reshape(x.shape)]
    for kind in range(4):
        out += [results[nm][kind].reshape(originals[nm].shape) for nm in names]
    return tuple(out)
```

```python
import functools

import jax
import jax.numpy as jnp
from jax import lax
from jax.experimental import pallas as pl
from jax.experimental.pallas import tpu as pltpu

F32 = jnp.float32
BF16 = jnp.bfloat16

D_MODEL = 1024
HEAD_DIM = 64
CONV_WIDTH = 512
LRU_WIDTH = 1024
MIX_WIDTH = CONV_WIDTH + LRU_WIDTH
IN_COLS = 3 * CONV_WIDTH + 2 * LRU_WIDTH
D_FF = 4 * D_MODEL
GROUP = 256
EPS = 1e-6
LRU_C = 8.0
N_DEV = 8
SUB = 8

OFF_GB, OFF_GC, OFF_V, OFF_XR, OFF_G = 0, 512, 1024, 1536, 2560

ADAM_LR, ADAM_B1, ADAM_B2, ADAM_EPS, ADAM_WD, ADAM_STEP = 0.001, 0.9, 0.999, 1e-08, 0.01, 10
BC1 = 1.0 - ADAM_B1 ** ADAM_STEP
BC2 = 1.0 - ADAM_B2 ** ADAM_STEP

MIB = 1024 * 1024
MESH = pl.DeviceIdType.MESH

VEC_ROWS = 32
ROW_GF, ROW_GMLP, ROW_LOSS = 0, 1, 2
ROW_GNC, ROW_GNR, ROW_BR, ROW_BA, ROW_BX, ROW_LAM, ROW_CW, ROW_RW = 8, 9, 10, 11, 12, 13, 14, 17
ROW_GMIX = 24
ACC_GNC, ACC_GNR, ACC_BR, ACC_BA, ACC_BX, ACC_SP, ACC_CW, ACC_RW, N_ACC = 0, 1, 2, 3, 4, 5, 6, 9, 13


def _params(semantics=None, vmem_mib=48):
    return pltpu.CompilerParams(dimension_semantics=semantics, vmem_limit_bytes=vmem_mib * MIB)


def _rms(x):
    return lax.rsqrt(jnp.mean(x * x, axis=-1, keepdims=True) + EPS)


def _rms_bwd(dy, xhat, r, g):
    dyh = dy * g
    return r * (dyh - xhat * jnp.mean(dyh * xhat, axis=-1, keepdims=True))


def _sigmoid(x):
    return 0.5 + 0.5 * jnp.tanh(0.5 * x)


def _gelu(x):
    c0, c1 = 0.7978845608028654, 0.044715
    t = jnp.tanh(c0 * (x + c1 * x * x * x))
    ge = 0.5 * x * (1.0 + t)
    dge = 0.5 * (1.0 + t) + 0.5 * x * (1.0 - t * t) * c0 * (1.0 + 3.0 * c1 * x * x)
    return ge, dge


def _softplus_neg(lam):
    z = -lam
    e = jnp.exp(-jnp.abs(z))
    return jnp.maximum(z, 0.0) + jnp.where(e < 1e-4, e * (1.0 - 0.5 * e), jnp.log(1.0 + e))


def _lru_gates(pa, px, sp_c):
    ra = _sigmoid(pa)
    ii = _sigmoid(px)
    la = -ra * sp_c
    a = jnp.exp(la)
    x2 = 2.0 * la
    series = -x2 * (1.0 + x2 * (0.5 + x2 * (1.0 / 6.0 + x2 * (1.0 / 24.0))))
    m2 = jnp.where(x2 > -0.01, series, 1.0 - a * a)
    inv_mult = lax.rsqrt(m2)
    mult = jnp.where(m2 > 0.0, m2 * inv_mult, 0.0)
    return ra, ii, a, mult, inv_mult


def _down(cur, prev, s, row):
    return jnp.where(row >= s, pltpu.roll(cur, s, 0), pltpu.roll(prev, s, 0))


def _up(cur, nxt, s, row):
    return jnp.where(row < SUB - s, pltpu.roll(cur, SUB - s, 0), pltpu.roll(nxt, SUB - s, 0))


def _scan8_fwd(a, b, row):
    for s in (1, 2, 4):
        m = row >= s
        a_sh = pltpu.roll(a, s, 0)
        b_sh = pltpu.roll(b, s, 0)
        b = jnp.where(m, a * b_sh + b, b)
        a = jnp.where(m, a * a_sh, a)
    return a, b


def _scan8_rev(a, b, row):
    for s in (1, 2, 4):
        m = row < SUB - s
        a_sh = pltpu.roll(a, SUB - s, 0)
        b_sh = pltpu.roll(b, SUB - s, 0)
        b = jnp.where(m, a * b_sh + b, b)
        a = jnp.where(m, a * a_sh, a)
    return a, b


def _group_mask(shape):
    r = lax.broadcasted_iota(jnp.int32, shape, 0)
    c = lax.broadcasted_iota(jnp.int32, shape, 1)
    return ((r % GROUP) // HEAD_DIM) == (c // HEAD_DIM)


def _expand_heads(w):
    j = lax.broadcasted_iota(jnp.int32, (HEAD_DIM, GROUP), 0)
    c = lax.broadcasted_iota(jnp.int32, (HEAD_DIM, GROUP), 1)
    spread = (c % HEAD_DIM == j).astype(BF16)
    e = jnp.dot(w.astype(BF16), spread, preferred_element_type=F32)
    return jnp.where(_group_mask(e.shape), e, 0.0).astype(BF16)


def _fold_heads(p):
    p = jnp.where(_group_mask(p.shape), p, 0.0)
    c = lax.broadcasted_iota(jnp.int32, (GROUP, HEAD_DIM), 0)
    j = lax.broadcasted_iota(jnp.int32, (GROUP, HEAD_DIM), 1)
    fold = (c % HEAD_DIM == j).astype(BF16)
    hi = p.astype(BF16)
    rest = p - hi.astype(F32)
    mid = rest.astype(BF16)
    lo = (rest - mid.astype(F32)).astype(BF16)
    dot = functools.partial(jnp.dot, preferred_element_type=F32)
    return dot(hi, fold) + dot(mid, fold) + dot(lo, fold)


def _block_diag_apply(xb, wbd_ref):
    parts = [jnp.dot(xb[:, g * GROUP:(g + 1) * GROUP], wbd_ref[g * GROUP:(g + 1) * GROUP, :],
                     preferred_element_type=F32) for g in range(LRU_WIDTH // GROUP)]
    return jnp.concatenate(parts, axis=1)


def _block_diag_apply_t(db, wbd_ref):
    parts = [lax.dot_general(db[:, g * GROUP:(g + 1) * GROUP], wbd_ref[g * GROUP:(g + 1) * GROUP, :],
                             (((1,), (1,)), ((), ())), preferred_element_type=F32)
             for g in range(LRU_WIDTH // GROUP)]
    return jnp.concatenate(parts, axis=1)


def _dot_nt(a, b):
    return lax.dot_general(a, b, (((1,), (1,)), ((), ())), preferred_element_type=F32)


def _dot_tn(a, b):
    return lax.dot_general(a, b, (((0,), (0,)), ((), ())), preferred_element_type=F32)


CHUNKS_IN_FLIGHT = 4


def _chunk_loop(n_chunks, chunk, init):
    def body(k, carry):
        for j in range(CHUNKS_IN_FLIGHT):
            carry = chunk(k * CHUNKS_IN_FLIGHT + j, carry)
        return carry

    return lax.fori_loop(0, n_chunks // CHUNKS_IN_FLIGHT, body, init)


def _place():
    x, y, c = lax.axis_index("x"), lax.axis_index("y"), lax.axis_index("c")
    return x, y, c


def _all_gather_weights(w_in, w_out, w_mlp_in, w_mlp_out, conv_w, rnn_conv_w):
    n_in = w_in.shape[1]
    n_arr = 5

    def body(win_ref, wout_ref, w1_ref, w2_ref, cw_ref, rw_ref,
             o_win, o_wout, o_w1, o_w2, o_cp, padbuf, send_sems, recv_sems):
        x, y, c = _place()
        me = (x, y, c)
        my_id = 4 * x + 2 * y + c
        sibling = (x, y, 1 - c)
        chips = [(1 - x, y), (x, 1 - y), (1 - x, 1 - y)]
        outs = [o_win, o_wout, o_w1, o_w2, o_cp]

        padbuf[...] = jnp.zeros(padbuf.shape, F32)
        padbuf[:, 0:n_in] = win_ref[...]
        o_win[my_id] = padbuf[...].T[0:n_in, :].astype(BF16)
        o_wout[my_id] = wout_ref[...].astype(BF16)
        o_w1[my_id] = w1_ref[...].astype(BF16)
        o_w2[my_id] = w2_ref[...].astype(BF16)
        o_cp[my_id] = jnp.zeros(o_cp.shape[1:], F32)
        o_cp[my_id, 0:3, 0:64] = cw_ref[...]
        o_cp[my_id, 3:7, :] = rw_ref[...]

        def copy(arr, k, block, to):
            blk = outs[arr].at[4 * block[0] + 2 * block[1] + block[2]]
            return pltpu.make_async_remote_copy(
                src_ref=blk, dst_ref=blk, send_sem=send_sems.at[arr, k], recv_sem=recv_sems.at[arr, k],
                device_id=to, device_id_type=MESH)

        first = []
        for arr in range(n_arr):
            first.append(copy(arr, 0, me, sibling))
            first += [copy(arr, 1 + j, me, (*chip, c)) for j, chip in enumerate(chips)]
        for cp in first:
            cp.start()
        passed = []
        for j, chip in enumerate(chips):
            for arr in range(n_arr):
                copy(arr, 1 + j, (*chip, c), me).wait_recv()
                fwd = copy(arr, 4 + j, (*chip, c), sibling)
                fwd.start()
                passed.append(fwd)
        for arr in range(n_arr):
            copy(arr, 0, sibling, me).wait_recv()
            for j, chip in enumerate(chips):
                copy(arr, 4 + j, (*chip, 1 - c), me).wait_recv()
        for cp in first + passed:
            cp.wait_send()

    vm = pl.BlockSpec(memory_space=pltpu.VMEM)
    shapes = (
        jax.ShapeDtypeStruct((N_DEV, n_in, D_MODEL), BF16),
        jax.ShapeDtypeStruct((N_DEV,) + w_out.shape, BF16),
        jax.ShapeDtypeStruct((N_DEV,) + w_mlp_in.shape, BF16),
        jax.ShapeDtypeStruct((N_DEV,) + w_mlp_out.shape, BF16),
        jax.ShapeDtypeStruct((N_DEV, 8, 128), F32),
    )
    return pl.pallas_call(
        body, out_shape=shapes, in_specs=[vm] * 6, out_specs=[vm] * 5,
        scratch_shapes=[pltpu.VMEM((D_MODEL, 512), F32),
                        pltpu.SemaphoreType.DMA((n_arr, 7)), pltpu.SemaphoreType.DMA((n_arr, 7))],
        compiler_params=_params(vmem_mib=56), name="all_gather_weights",
    )(w_in, w_out, w_mlp_in, w_mlp_out, conv_w, rnn_conv_w)


def _reduce_scatter(g, name):
    _, rows, cols = g.shape

    def body(g_ref, o_ref, sib, hsend, hrecv, d_send, d_recv, i_send, i_recv):
        x, y, c = _place()
        sibling = (x, y, 1 - c)
        chips = [(x, y), (1 - x, y), (x, 1 - y), (1 - x, 1 - y)]

        def gid(chip, core):
            return 4 * chip[0] + 2 * chip[1] + core

        def d2d(q):
            return pltpu.make_async_remote_copy(
                src_ref=g_ref.at[gid(chips[q], 1 - c)], dst_ref=sib.at[q],
                send_sem=d_send.at[q], recv_sem=d_recv.at[q], device_id=sibling, device_id_type=MESH)

        def ici(q):
            return pltpu.make_async_remote_copy(
                src_ref=hsend.at[q - 1], dst_ref=hrecv.at[q - 1],
                send_sem=i_send.at[q - 1], recv_sem=i_recv.at[q - 1],
                device_id=(*chips[q], c), device_id_type=MESH)

        for q in (1, 2, 3, 0):
            d2d(q).start()
        for q in (1, 2, 3):
            d2d(q).wait_recv()
            hsend[q - 1] = (g_ref[gid(chips[q], c)].astype(F32) + sib[q].astype(F32)).astype(BF16)
            ici(q).start()
        d2d(0).wait_recv()
        acc = g_ref[gid(chips[0], c)].astype(F32) + sib[0].astype(F32)
        for q in (1, 2, 3):
            ici(q).wait_recv()
            acc = acc + hrecv[q - 1].astype(F32)
        o_ref[...] = acc
        for q in range(4):
            d2d(q).wait_send()
        for q in (1, 2, 3):
            ici(q).wait_send()

    vm = pl.BlockSpec(memory_space=pltpu.VMEM)
    return pl.pallas_call(
        body, out_shape=jax.ShapeDtypeStruct((rows, cols), F32), in_specs=[vm], out_specs=vm,
        scratch_shapes=[pltpu.VMEM((4, rows, cols), BF16), pltpu.VMEM((3, rows, cols), BF16),
                        pltpu.VMEM((3, rows, cols), BF16),
                        pltpu.SemaphoreType.DMA((4,)), pltpu.SemaphoreType.DMA((4,)),
                        pltpu.SemaphoreType.DMA((3,)), pltpu.SemaphoreType.DMA((3,))],
        compiler_params=_params(vmem_mib=48), name=name,
    )(g)


def _all_reduce_small(vec_m, vec_b, vec_x, wab):
    wrows = wab.shape[0] // N_DEV

    def body(vm_ref, vb_ref, vx_ref, w_ref, o_vec, o_w, vpack, vrecv, wrecv, wred,
             v_send, v_recv, w_send, w_recv, b_send, b_recv):
        x, y, c = _place()
        my_id = 4 * x + 2 * y + c

        def peer(k):
            return (x ^ ((k >> 2) & 1), y ^ ((k >> 1) & 1), c ^ (k & 1))

        def pid(k):
            p = peer(k)
            return 4 * p[0] + 2 * p[1] + p[2]

        vpack[0:8, :] = vm_ref[...]
        vpack[8:24, :] = vb_ref[...]
        vpack[24:32, :] = vx_ref[...]
        vrecv[my_id] = vpack[...]

        def vcopy(k):
            return pltpu.make_async_remote_copy(
                src_ref=vpack, dst_ref=vrecv.at[my_id], send_sem=v_send.at[k], recv_sem=v_recv.at[k],
                device_id=peer(k), device_id_type=MESH)

        def wcopy(k):
            return pltpu.make_async_remote_copy(
                src_ref=w_ref.at[pl.ds(pl.multiple_of(pid(k) * wrows, SUB), wrows), :], dst_ref=wrecv.at[k],
                send_sem=w_send.at[k], recv_sem=w_recv.at[k], device_id=peer(k), device_id_type=MESH)

        def bcopy(k):
            mine = o_w.at[pl.ds(pl.multiple_of(my_id * wrows, SUB), wrows), :]
            return pltpu.make_async_remote_copy(
                src_ref=wred, dst_ref=mine, send_sem=b_send.at[k], recv_sem=b_recv.at[k],
                device_id=peer(k), device_id_type=MESH)

        for k in range(1, N_DEV):
            vcopy(k).start()
            wcopy(k).start()
        red = w_ref[pl.ds(pl.multiple_of(my_id * wrows, SUB), wrows), :]
        for k in range(1, N_DEV):
            wcopy(k).wait_recv()
            red = red + wrecv[k]
        wred[...] = red
        o_w[pl.ds(pl.multiple_of(my_id * wrows, SUB), wrows), :] = red
        for k in range(1, N_DEV):
            bcopy(k).start()
        for k in range(1, N_DEV):
            vcopy(k).wait_recv()
        tot = vrecv[0]
        for s in range(1, N_DEV):
            tot = tot + vrecv[s]
        o_vec[...] = tot
        for k in range(1, N_DEV):
            bcopy(k).wait_recv()
        for k in range(1, N_DEV):
            vcopy(k).wait_send()
            wcopy(k).wait_send()
            bcopy(k).wait_send()

    vm = pl.BlockSpec(memory_space=pltpu.VMEM)
    dma7 = pltpu.SemaphoreType.DMA((N_DEV,))
    return pl.pallas_call(
        body, out_shape=(jax.ShapeDtypeStruct((VEC_ROWS, D_MODEL), F32), jax.ShapeDtypeStruct(wab.shape, F32)),
        in_specs=[vm] * 4, out_specs=[vm] * 2,
        scratch_shapes=[pltpu.VMEM((VEC_ROWS, D_MODEL), F32), pltpu.VMEM((N_DEV, VEC_ROWS, D_MODEL), F32),
                        pltpu.VMEM((N_DEV, wrows, HEAD_DIM), F32), pltpu.VMEM((wrows, HEAD_DIM), F32),
                        dma7, dma7, dma7, dma7, dma7, dma7],
        compiler_params=_params(vmem_mib=32), name="all_reduce_small",
    )(vec_m, vec_b, vec_x, wab)


def _in_proj(x, g_mix, win_t, tm):
    t_len = x.shape[0]

    def body(x_ref, g_ref, w_ref, u_ref, h_ref):
        xv = x_ref[...]
        h = (xv * _rms(xv) * g_ref[...]).astype(BF16)
        h_ref[...] = h
        u_ref[...] = _dot_nt(h, w_ref[...])

    return pl.pallas_call(
        body, grid=(t_len // tm,),
        in_specs=[pl.BlockSpec((tm, D_MODEL), lambda i: (i, 0)), pl.BlockSpec((1, D_MODEL), lambda i: (0, 0)),
                  pl.BlockSpec((IN_COLS, D_MODEL), lambda i: (0, 0))],
        out_specs=[pl.BlockSpec((tm, IN_COLS), lambda i: (i, 0)), pl.BlockSpec((tm, D_MODEL), lambda i: (i, 0))],
        out_shape=[jax.ShapeDtypeStruct((t_len, IN_COLS), F32), jax.ShapeDtypeStruct((t_len, D_MODEL), BF16)],
        compiler_params=_params(("arbitrary",), 56), name="in_proj",
    )(x, g_mix, win_t)


def _conv3_chunk(u_ref, r, cv_prev, cw, row):
    gb = u_ref[pl.ds(r, SUB), OFF_GB:OFF_GB + CONV_WIDTH]
    gc = u_ref[pl.ds(r, SUB), OFF_GC:OFF_GC + CONV_WIDTH]
    v = u_ref[pl.ds(r, SUB), OFF_V:OFF_V + CONV_WIDTH]
    cv = gc * v
    cv_m1 = _down(cv, cv_prev, 1, row)
    cv_m2 = _down(cv, cv_prev, 2, row)
    cq = cw[2:3, :] * cv + cw[1:2, :] * cv_m1 + cw[0:1, :] * cv_m2
    return gb, gc, v, cv, cv_m1, cv_m2, cq


def _conv4_chunk(u_ref, r, xin_prev, rw, rb, row):
    xin = u_ref[pl.ds(r, SUB), OFF_XR:OFF_XR + LRU_WIDTH]
    m1 = _down(xin, xin_prev, 1, row)
    m2 = _down(xin, xin_prev, 2, row)
    m3 = _down(xin, xin_prev, 3, row)
    xr = rw[3:4, :] * xin + rw[2:3, :] * m1 + rw[1:2, :] * m2 + rw[0:1, :] * m3 + rb
    return xin, m1, m2, m3, xr


def _mixer_fwd(u, x, conv_w, rnn_conv_w, rnn_conv_b, wa, b_a, wx, b_x, lam, gnc, gnr, w_out, tm):
    t_len = x.shape[0]
    n_chunks = tm // SUB

    def body(u_ref, x_ref, cw_ref, rw_ref, rb_ref, wa_ref, ba_ref, wx_ref, bx_ref, lam_ref, gnc_ref, gnr_ref,
             wout_ref, x1_ref, hs_ref, y_ref,
             y_s, xr_s, pa_s, px_s, wabd, wxbd, cv_car, xin_car, h_car):
        @pl.when(pl.program_id(0) == 0)
        def _():
            cv_car[...] = jnp.zeros(cv_car.shape, F32)
            xin_car[...] = jnp.zeros(xin_car.shape, F32)
            h_car[...] = jnp.zeros(h_car.shape, F32)
            wabd[...] = _expand_heads(wa_ref[...])
            wxbd[...] = _expand_heads(wx_ref[...])

        row_c = lax.broadcasted_iota(jnp.int32, (SUB, CONV_WIDTH), 0)
        row_r = lax.broadcasted_iota(jnp.int32, (SUB, LRU_WIDTH), 0)
        cw = cw_ref[...]
        rw = rw_ref[...]
        rb = rb_ref[...]
        g_c = gnc_ref[...]
        g_r = gnr_ref[...]
        sp_c = LRU_C * _softplus_neg(lam_ref[...])

        def convs(i, carry):
            cv_prev, xin_prev = carry
            r = pl.multiple_of(i * SUB, SUB)
            gb, _, _, cv, _, _, cq = _conv3_chunk(u_ref, r, cv_prev, cw, row_c)
            y_c = gb * cq
            y_s[pl.ds(r, SUB), 0:CONV_WIDTH] = y_c * _rms(y_c) * g_c
            xin, _, _, _, xr = _conv4_chunk(u_ref, r, xin_prev, rw, rb, row_r)
            xr_s[pl.ds(r, SUB), :] = xr
            return cv, xin

        cv_last, xin_last = _chunk_loop(n_chunks, convs, (cv_car[...], xin_car[...]))
        cv_car[...] = cv_last
        xin_car[...] = xin_last

        xrb = xr_s[...].astype(BF16)
        pa_s[...] = _block_diag_apply(xrb, wabd) + ba_ref[...]
        px_s[...] = _block_diag_apply(xrb, wxbd) + bx_ref[...]

        def recur(i, h_prev):
            r = pl.multiple_of(i * SUB, SUB)
            xr = xr_s[pl.ds(r, SUB), :]
            _, ii, a, mult, _ = _lru_gates(pa_s[pl.ds(r, SUB), :], px_s[pl.ds(r, SUB), :], sp_c)
            a_cum, b_cum = _scan8_fwd(a, mult * ii * xr, row_r)
            h = a_cum * h_prev + b_cum
            hs_ref[pl.ds(r, SUB), :] = h
            ge, _ = _gelu(u_ref[pl.ds(r, SUB), OFF_G:OFF_G + LRU_WIDTH])
            y_r = h * ge
            y_s[pl.ds(r, SUB), CONV_WIDTH:MIX_WIDTH] = y_r * _rms(y_r) * g_r
            return h[SUB - 1:SUB, :]

        h_car[...] = _chunk_loop(n_chunks, recur, h_car[...])

        yb = y_s[...].astype(BF16)
        y_ref[...] = yb
        x1_ref[...] = x_ref[...] + jnp.dot(yb, wout_ref[...], preferred_element_type=F32)

    row_tile = lambda w: pl.BlockSpec((tm, w), lambda i: (i, 0))
    whole = lambda a: pl.BlockSpec(a.shape, lambda i: (0,) * a.ndim)
    smalls = (conv_w, rnn_conv_w, rnn_conv_b, wa, b_a, wx, b_x, lam, gnc, gnr, w_out)
    return pl.pallas_call(
        body, grid=(t_len // tm,),
        in_specs=[row_tile(IN_COLS), row_tile(D_MODEL)] + [whole(a) for a in smalls],
        out_specs=[row_tile(D_MODEL), row_tile(LRU_WIDTH), row_tile(MIX_WIDTH)],
        out_shape=[jax.ShapeDtypeStruct((t_len, D_MODEL), F32), jax.ShapeDtypeStruct((t_len, LRU_WIDTH), F32),
                   jax.ShapeDtypeStruct((t_len, MIX_WIDTH), BF16)],
        scratch_shapes=[pltpu.VMEM((tm, MIX_WIDTH), F32), pltpu.VMEM((tm, LRU_WIDTH), F32),
                        pltpu.VMEM((tm, LRU_WIDTH), F32), pltpu.VMEM((tm, LRU_WIDTH), F32),
                        pltpu.VMEM((LRU_WIDTH, GROUP), BF16), pltpu.VMEM((LRU_WIDTH, GROUP), BF16),
                        pltpu.VMEM((SUB, CONV_WIDTH), F32), pltpu.VMEM((SUB, LRU_WIDTH), F32),
                        pltpu.VMEM((1, LRU_WIDTH), F32)],
        compiler_params=_params(("arbitrary",), 56), name="mixer_fwd",
    )(u, x, *smalls)


def _mlp_fwd_bwd(x1, target, g_mlp, g_f, w1, w2, tm):
    t_len = x1.shape[0]
    n_blk, _, blk = w1.shape

    def body(x1_ref, tg_ref, gm_ref, gf_ref, w1_hbm, w2_hbm,
             dx1_ref, z_ref, dpre_ref, h2_ref, dx2_ref, vec_ref, w1_s, w2_s, rp_s, sem):
        @pl.when(pl.program_id(0) == 0)
        def _():
            c1 = pltpu.make_async_copy(w1_hbm, w1_s, sem.at[0])
            c2 = pltpu.make_async_copy(w2_hbm, w2_s, sem.at[1])
            c1.start()
            c2.start()
            vec_ref[...] = jnp.zeros(vec_ref.shape, F32)
            c1.wait()
            c2.wait()

        x1v = x1_ref[...]
        g_m = gm_ref[...]
        g_o = gf_ref[...]
        r2 = _rms(x1v)
        x1h = x1v * r2
        h2 = (x1h * g_m).astype(BF16)
        h2_ref[...] = h2
        x2 = x1v
        for k in range(n_blk):
            rp = jnp.maximum(jnp.dot(h2, w1_s[k], preferred_element_type=F32), 0.0)
            rp_s[:, k * blk:(k + 1) * blk] = rp
            zb = (rp * rp).astype(BF16)
            z_ref[:, k * blk:(k + 1) * blk] = zb
            x2 = x2 + jnp.dot(zb, w2_s[k * blk:(k + 1) * blk, :], preferred_element_type=F32)
        r3 = _rms(x2)
        x2h = x2 * r3
        err = x2h * g_o - tg_ref[...]
        dout = err * (1.0 / D_MODEL)
        vec_ref[ROW_LOSS:ROW_LOSS + 1, :] += (0.5 / D_MODEL) * jnp.sum(err * err, axis=0, keepdims=True)
        vec_ref[ROW_GF:ROW_GF + 1, :] += jnp.sum(dout * x2h, axis=0, keepdims=True)
        dx2 = _rms_bwd(dout, x2h, r3, g_o)
        dx2b = dx2.astype(BF16)
        dx2_ref[...] = dx2b
        dh2 = jnp.zeros((tm, D_MODEL), F32)
        for k in range(n_blk):
            dz = _dot_nt(dx2b, w2_s[k * blk:(k + 1) * blk, :])
            dpb = (dz * 2.0 * rp_s[:, k * blk:(k + 1) * blk]).astype(BF16)
            dpre_ref[:, k * blk:(k + 1) * blk] = dpb
            dh2 = dh2 + _dot_nt(dpb, w1_s[k])
        vec_ref[ROW_GMLP:ROW_GMLP + 1, :] += jnp.sum(dh2 * x1h, axis=0, keepdims=True)
        dx1_ref[...] = dx2 + _rms_bwd(dh2, x1h, r2, g_m)

    row_tile = lambda w: pl.BlockSpec((tm, w), lambda i: (i, 0))
    vec_spec = pl.BlockSpec((1, D_MODEL), lambda i: (0, 0))
    hbm = pl.BlockSpec(memory_space=pl.ANY)
    return pl.pallas_call(
        body, grid=(t_len // tm,),
        in_specs=[row_tile(D_MODEL), row_tile(D_MODEL), vec_spec, vec_spec, hbm, hbm],
        out_specs=[row_tile(D_MODEL), row_tile(D_FF), row_tile(D_FF), row_tile(D_MODEL), row_tile(D_MODEL),
                   pl.BlockSpec((SUB, D_MODEL), lambda i: (0, 0))],
        out_shape=[jax.ShapeDtypeStruct((t_len, D_MODEL), F32), jax.ShapeDtypeStruct((t_len, D_FF), BF16),
                   jax.ShapeDtypeStruct((t_len, D_FF), BF16), jax.ShapeDtypeStruct((t_len, D_MODEL), BF16),
                   jax.ShapeDtypeStruct((t_len, D_MODEL), BF16), jax.ShapeDtypeStruct((SUB, D_MODEL), F32)],
        scratch_shapes=[pltpu.VMEM(w1.shape, BF16), pltpu.VMEM(w2.shape, BF16), pltpu.VMEM((tm, D_FF), F32),
                        pltpu.SemaphoreType.DMA((2,))],
        compiler_params=_params(("arbitrary",), 56), name="mlp_fwd_bwd",
    )(x1, target, g_mlp, g_f, w1, w2)


def _mlp_weight_grads(h2, dpre, z, dx2, tk):
    t_len = h2.shape[0]
    blk = D_FF // N_DEV

    def body(h2_ref, dp_ref, z_ref, dx2_ref, g1_ref, g2_ref, acc1, acc2):
        j = pl.program_id(1)

        @pl.when(j == 0)
        def _():
            acc1[...] = jnp.zeros(acc1.shape, F32)
            acc2[...] = jnp.zeros(acc2.shape, F32)

        acc1[...] += _dot_tn(h2_ref[...], dp_ref[...])
        acc2[...] += _dot_tn(z_ref[...], dx2_ref[...])

        @pl.when(j == pl.num_programs(1) - 1)
        def _():
            g1_ref[0] = acc1[...].astype(BF16)
            g2_ref[0] = acc2[...].astype(BF16)

    return pl.pallas_call(
        body, grid=(N_DEV, t_len // tk),
        in_specs=[pl.BlockSpec((tk, D_MODEL), lambda k, j: (j, 0)), pl.BlockSpec((tk, blk), lambda k, j: (j, k)),
                  pl.BlockSpec((tk, blk), lambda k, j: (j, k)), pl.BlockSpec((tk, D_MODEL), lambda k, j: (j, 0))],
        out_specs=[pl.BlockSpec((1, D_MODEL, blk), lambda k, j: (k, 0, 0)),
                   pl.BlockSpec((1, blk, D_MODEL), lambda k, j: (k, 0, 0))],
        out_shape=[jax.ShapeDtypeStruct((N_DEV, D_MODEL, blk), BF16), jax.ShapeDtypeStruct((N_DEV, blk, D_MODEL), BF16)],
        scratch_shapes=[pltpu.VMEM((D_MODEL, blk), F32), pltpu.VMEM((blk, D_MODEL), F32)],
        compiler_params=_params(("arbitrary", "arbitrary"), 40), name="mlp_weight_grads",
    )(h2, dpre, z, dx2)


def _mixer_bwd(u, hs, dx1, conv_w, rnn_conv_w, rnn_conv_b, wa, b_a, wx, b_x, lam, gnc, gnr, w_out, tm):
    t_len = u.shape[0]
    n_tiles = t_len // tm
    n_chunks = tm // SUB
    per_tile = tm // SUB

    def body(u_ref, up_ref, hs_ref, hp_ref, dx1_ref, cw_ref, rw_ref, rb_ref, wa_ref, ba_ref, wx_ref, bx_ref,
             lam_ref, gnc_ref, gnr_ref, wout_ref, du_ref, vec_ref, wab_ref,
             du_s, dy_s, xr_s, pa_s, px_s, dpa_s, dpx_s, dxr_s, wabd, wxbd, acc, dwa_acc, dwx_acc,
             a_car, dh_car, dcq_car, dxr_car):
        step = pl.program_id(0)
        has_prev = (step < n_tiles - 1).astype(F32)

        @pl.when(step == 0)
        def _():
            acc[...] = jnp.zeros(acc.shape, F32)
            dwa_acc[...] = jnp.zeros(dwa_acc.shape, F32)
            dwx_acc[...] = jnp.zeros(dwx_acc.shape, F32)
            a_car[...] = jnp.ones(a_car.shape, F32)
            dh_car[...] = jnp.zeros(dh_car.shape, F32)
            dcq_car[...] = jnp.zeros(dcq_car.shape, F32)
            dxr_car[...] = jnp.zeros(dxr_car.shape, F32)
            wabd[...] = _expand_heads(wa_ref[...])
            wxbd[...] = _expand_heads(wx_ref[...])

        row_c = lax.broadcasted_iota(jnp.int32, (SUB, CONV_WIDTH), 0)
        row_r = lax.broadcasted_iota(jnp.int32, (SUB, LRU_WIDTH), 0)
        cw = cw_ref[...]
        rw = rw_ref[...]
        rb = rb_ref[...]
        g_c = gnc_ref[...]
        g_r = gnr_ref[...]
        sp_c = LRU_C * _softplus_neg(lam_ref[...])

        up = up_ref[...] * has_prev
        cv_before = up[:, OFF_GC:OFF_GC + CONV_WIDTH] * up[:, OFF_V:OFF_V + CONV_WIDTH]
        xin_before = up[:, OFF_XR:OFF_XR + LRU_WIDTH]
        hs_before = hp_ref[...] * has_prev

        dy_s[...] = _dot_nt(dx1_ref[...].astype(BF16), wout_ref[...])

        def conv4_fwd(i, xin_prev):
            r = pl.multiple_of(i * SUB, SUB)
            xin, _, _, _, xr = _conv4_chunk(u_ref, r, xin_prev, rw, rb, row_r)
            xr_s[pl.ds(r, SUB), :] = xr
            return xin

        _chunk_loop(n_chunks, conv4_fwd, xin_before)
        xrb = xr_s[...].astype(BF16)
        pa_s[...] = _block_diag_apply(xrb, wabd) + ba_ref[...]
        px_s[...] = _block_diag_apply(xrb, wxbd) + bx_ref[...]

        def recur_bwd(j, carry):
            a_later, dh_later = carry
            i = n_chunks - 1 - j
            r = pl.multiple_of(i * SUB, SUB)
            rp = pl.multiple_of(jnp.maximum(i - 1, 0) * SUB, SUB)
            xr = xr_s[pl.ds(r, SUB), :]
            hs_c = hs_ref[pl.ds(r, SUB), :]
            hs_prev = jnp.where(i == 0, hs_before, hs_ref[pl.ds(rp, SUB), :])
            h_m1 = _down(hs_c, hs_prev, 1, row_r)
            ra, ii, a, mult, inv_mult = _lru_gates(pa_s[pl.ds(r, SUB), :], px_s[pl.ds(r, SUB), :], sp_c)
            ge, dge = _gelu(u_ref[pl.ds(r, SUB), OFF_G:OFF_G + LRU_WIDTH])
            y_r = hs_c * ge
            rr = _rms(y_r)
            yhat = y_r * rr
            dyn = dy_s[pl.ds(r, SUB), CONV_WIDTH:MIX_WIDTH]
            acc[ACC_GNR] += dyn * yhat
            dy_r = _rms_bwd(dyn, yhat, rr, g_r)
            du_s[pl.ds(r, SUB), OFF_G:OFF_G + LRU_WIDTH] = dy_r * hs_c * dge
            a_cum, d_cum = _scan8_rev(_up(a, a_later, 1, row_r), dy_r * ge, row_r)
            dh = a_cum * dh_later + d_cum
            dmult = dh * ii * xr
            dii = dh * mult * xr
            dxr_s[pl.ds(r, SUB), :] = dh * mult * ii
            dla = dh * h_m1 * a - dmult * a * a * inv_mult
            acc[ACC_SP] += -dla * ra
            dpa = -dla * sp_c * ra * (1.0 - ra)
            dpx = dii * ii * (1.0 - ii)
            acc[ACC_BA] += dpa
            acc[ACC_BX] += dpx
            dpa_s[pl.ds(r, SUB), :] = dpa
            dpx_s[pl.ds(r, SUB), :] = dpx
            return a, dh[0:1, :]

        a_first, dh_first = _chunk_loop(n_chunks, recur_bwd, (a_car[...], dh_car[...]))
        a_car[...] = a_first
        dh_car[...] = dh_first

        dpab = dpa_s[...].astype(BF16)
        dpxb = dpx_s[...].astype(BF16)
        dxr_s[...] += _block_diag_apply_t(dpab, wabd) + _block_diag_apply_t(dpxb, wxbd)
        for g in range(LRU_WIDTH // GROUP):
            cols = slice(g * GROUP, (g + 1) * GROUP)
            dwa_acc[cols, :] += _dot_tn(xrb[:, cols], dpab[:, cols])
            dwx_acc[cols, :] += _dot_tn(xrb[:, cols], dpxb[:, cols])

        def convs_bwd(j, carry):
            dcq_later, dxr_later = carry
            i = n_chunks - 1 - j
            r = pl.multiple_of(i * SUB, SUB)
            rp = pl.multiple_of(jnp.maximum(i - 1, 0) * SUB, SUB)
            cv_prev = jnp.where(i == 0, cv_before,
                                u_ref[pl.ds(rp, SUB), OFF_GC:OFF_GC + CONV_WIDTH]
                                * u_ref[pl.ds(rp, SUB), OFF_V:OFF_V + CONV_WIDTH])
            gb, gc, v, cv, cv_m1, cv_m2, cq = _conv3_chunk(u_ref, r, cv_prev, cw, row_c)
            y_c = gb * cq
            rc = _rms(y_c)
            yhat = y_c * rc
            dyn = dy_s[pl.ds(r, SUB), 0:CONV_WIDTH]
            acc[ACC_GNC, :, 0:CONV_WIDTH] += dyn * yhat
            dy_c = _rms_bwd(dyn, yhat, rc, g_c)
            dcq = dy_c * gb
            dcv = (cw[2:3, :] * dcq + cw[1:2, :] * _up(dcq, dcq_later, 1, row_c)
                   + cw[0:1, :] * _up(dcq, dcq_later, 2, row_c))
            acc[ACC_CW + 2, :, 0:CONV_WIDTH] += dcq * cv
            acc[ACC_CW + 1, :, 0:CONV_WIDTH] += dcq * cv_m1
            acc[ACC_CW + 0, :, 0:CONV_WIDTH] += dcq * cv_m2
            du_s[pl.ds(r, SUB), OFF_GB:OFF_GB + CONV_WIDTH] = dy_c * cq
            du_s[pl.ds(r, SUB), OFF_GC:OFF_GC + CONV_WIDTH] = dcv * v
            du_s[pl.ds(r, SUB), OFF_V:OFF_V + CONV_WIDTH] = dcv * gc

            xin_prev = jnp.where(i == 0, xin_before, u_ref[pl.ds(rp, SUB), OFF_XR:OFF_XR + LRU_WIDTH])
            xin, m1, m2, m3, _ = _conv4_chunk(u_ref, r, xin_prev, rw, rb, row_r)
            dxr = dxr_s[pl.ds(r, SUB), :]
            du_s[pl.ds(r, SUB), OFF_XR:OFF_XR + LRU_WIDTH] = (
                rw[3:4, :] * dxr + rw[2:3, :] * _up(dxr, dxr_later, 1, row_r)
                + rw[1:2, :] * _up(dxr, dxr_later, 2, row_r) + rw[0:1, :] * _up(dxr, dxr_later, 3, row_r))
            acc[ACC_RW + 3] += dxr * xin
            acc[ACC_RW + 2] += dxr * m1
            acc[ACC_RW + 1] += dxr * m2
            acc[ACC_RW + 0] += dxr * m3
            acc[ACC_BR] += dxr
            return dcq, dxr

        dcq_first, dxr_first = _chunk_loop(n_chunks, convs_bwd, (dcq_car[...], dxr_car[...]))
        dcq_car[...] = dcq_first
        dxr_car[...] = dxr_first

        du_ref[...] = du_s[...].astype(BF16)

        @pl.when(step == n_tiles - 1)
        def _():
            vec_ref[...] = jnp.zeros(vec_ref.shape, F32)
            rows = {ACC_GNC: ROW_GNC, ACC_GNR: ROW_GNR, ACC_BR: ROW_BR, ACC_BA: ROW_BA, ACC_BX: ROW_BX}
            for k in range(3):
                rows[ACC_CW + k] = ROW_CW + k
            for k in range(4):
                rows[ACC_RW + k] = ROW_RW + k
            for slot, out_row in rows.items():
                o = out_row - ROW_GNC
                vec_ref[o:o + 1, :] = jnp.sum(acc[slot], axis=0, keepdims=True)
            lam_v = lam_ref[...]
            dsp = jnp.sum(acc[ACC_SP], axis=0, keepdims=True)
            o = ROW_LAM - ROW_GNC
            vec_ref[o:o + 1, :] = -dsp * LRU_C / (1.0 + jnp.exp(lam_v))
            wab_ref[0:LRU_WIDTH, :] = _fold_heads(dwa_acc[...])
            wab_ref[LRU_WIDTH:2 * LRU_WIDTH, :] = _fold_heads(dwx_acc[...])

    rev = lambda w: pl.BlockSpec((tm, w), lambda s: (n_tiles - 1 - s, 0))
    before = lambda w: pl.BlockSpec((SUB, w), lambda s: (jnp.maximum((n_tiles - 1 - s) * per_tile - 1, 0), 0))
    whole = lambda a: pl.BlockSpec(a.shape, lambda s: (0,) * a.ndim)
    smalls = (conv_w, rnn_conv_w, rnn_conv_b, wa, b_a, wx, b_x, lam, gnc, gnr, w_out)
    full = lambda w: pltpu.VMEM((tm, w), F32)
    return pl.pallas_call(
        body, grid=(n_tiles,),
        in_specs=[rev(IN_COLS), before(IN_COLS), rev(LRU_WIDTH), before(LRU_WIDTH), rev(D_MODEL)]
        + [whole(a) for a in smalls],
        out_specs=[rev(IN_COLS), pl.BlockSpec((16, D_MODEL), lambda s: (0, 0)),
                   pl.BlockSpec((2 * LRU_WIDTH, HEAD_DIM), lambda s: (0, 0))],
        out_shape=[jax.ShapeDtypeStruct((t_len, IN_COLS), BF16), jax.ShapeDtypeStruct((16, D_MODEL), F32),
                   jax.ShapeDtypeStruct((2 * LRU_WIDTH, HEAD_DIM), F32)],
        scratch_shapes=[full(IN_COLS), full(MIX_WIDTH), full(LRU_WIDTH), full(LRU_WIDTH), full(LRU_WIDTH),
                        full(LRU_WIDTH), full(LRU_WIDTH), full(LRU_WIDTH),
                        pltpu.VMEM((LRU_WIDTH, GROUP), BF16), pltpu.VMEM((LRU_WIDTH, GROUP), BF16),
                        pltpu.VMEM((N_ACC, SUB, LRU_WIDTH), F32),
                        pltpu.VMEM((LRU_WIDTH, GROUP), F32), pltpu.VMEM((LRU_WIDTH, GROUP), F32),
                        pltpu.VMEM((SUB, LRU_WIDTH), F32), pltpu.VMEM((1, LRU_WIDTH), F32),
                        pltpu.VMEM((SUB, CONV_WIDTH), F32), pltpu.VMEM((SUB, LRU_WIDTH), F32)],
        compiler_params=_params(("arbitrary",), 56), name="mixer_bwd",
    )(u, u, hs, hs, dx1, *smalls)


def _in_proj_bwd(du, dx1, x, g_mix, win_t, tm):
    t_len = x.shape[0]

    def body(du_ref, dx1_ref, x_ref, g_ref, w_ref, dx_ref, vec_ref):
        @pl.when(pl.program_id(0) == 0)
        def _():
            vec_ref[...] = jnp.zeros(vec_ref.shape, F32)

        dh = jnp.dot(du_ref[...], w_ref[...], preferred_element_type=F32)
        xv = x_ref[...]
        r1 = _rms(xv)
        xh = xv * r1
        vec_ref[0:1, :] += jnp.sum(dh * xh, axis=0, keepdims=True)
        dx_ref[...] = dx1_ref[...] + _rms_bwd(dh, xh, r1, g_ref[...])

    row_tile = lambda w: pl.BlockSpec((tm, w), lambda i: (i, 0))
    return pl.pallas_call(
        body, grid=(t_len // tm,),
        in_specs=[row_tile(IN_COLS), row_tile(D_MODEL), row_tile(D_MODEL), pl.BlockSpec((1, D_MODEL), lambda i: (0, 0)),
                  pl.BlockSpec((IN_COLS, D_MODEL), lambda i: (0, 0))],
        out_specs=[row_tile(D_MODEL), pl.BlockSpec((SUB, D_MODEL), lambda i: (0, 0))],
        out_shape=[jax.ShapeDtypeStruct((t_len, D_MODEL), F32), jax.ShapeDtypeStruct((SUB, D_MODEL), F32)],
        compiler_params=_params(("arbitrary",), 56), name="in_proj_bwd",
    )(du, dx1, x, g_mix, win_t)


def _tn_weight_grad(a, b, tk, name):
    t_len, m = a.shape
    n = b.shape[1]

    def body(a_ref, b_ref, o_ref, acc):
        j = pl.program_id(0)

        @pl.when(j == 0)
        def _():
            acc[...] = jnp.zeros(acc.shape, F32)

        acc[...] += _dot_tn(a_ref[...].astype(BF16), b_ref[...].astype(BF16))

        @pl.when(j == pl.num_programs(0) - 1)
        def _():
            o_ref[...] = acc[...].astype(BF16)

    return pl.pallas_call(
        body, grid=(t_len // tk,),
        in_specs=[pl.BlockSpec((tk, m), lambda j: (j, 0)), pl.BlockSpec((tk, n), lambda j: (j, 0))],
        out_specs=pl.BlockSpec((m, n), lambda j: (0, 0)),
        out_shape=jax.ShapeDtypeStruct((m, n), BF16),
        scratch_shapes=[pltpu.VMEM((m, n), F32)],
        compiler_params=_params(("arbitrary",), 56), name=name,
    )(a, b)


def _adamw(w, g, m, v):
    m = ADAM_B1 * m + (1.0 - ADAM_B1) * g
    v = ADAM_B2 * v + (1.0 - ADAM_B2) * (g * g)
    delta = -ADAM_LR * ((m / BC1) / (jnp.sqrt(v / BC2) + ADAM_EPS) + ADAM_WD * w)
    return delta, m, v


def _update_sharded(g, w, m, v, rows_blk, name, transposed=False):
    rows, cols = w.shape
    pad_cols = -(-cols // 128) * 128

    def body(g_ref, w_ref, m_ref, v_ref, og, od, om, ov, *scratch):
        if transposed:
            padbuf, turned = scratch
            padbuf[...] = jnp.zeros(padbuf.shape, F32)
            padbuf[0:cols, :] = g_ref[...]
            turned[...] = padbuf[...].T
            gv = turned[:, 0:cols]
        else:
            gv = g_ref[...]
        delta, mn, vn = _adamw(w_ref[...], gv, m_ref[...], v_ref[...])
        og[...] = gv
        od[...] = delta
        om[...] = mn
        ov[...] = vn

    blk = pl.BlockSpec((rows_blk, cols), lambda i: (i, 0))
    g_spec = pl.BlockSpec((cols, rows_blk), lambda i: (0, i)) if transposed else blk
    shape = jax.ShapeDtypeStruct((rows, cols), F32)
    return pl.pallas_call(
        body, grid=(rows // rows_blk,), in_specs=[g_spec, blk, blk, blk], out_specs=[blk] * 4, out_shape=[shape] * 4,
        scratch_shapes=[pltpu.VMEM((pad_cols, rows_blk), F32), pltpu.VMEM((rows_blk, pad_cols), F32)] if transposed else [],
        compiler_params=_params(("arbitrary",), 32), name=name,
    )(g, w, m, v)


def _update_small(vsum, wsum, g_cw, g_rw, weights, moments_m, moments_v):
    n = len(weights)

    def body(*refs):
        vs, ws, gcw, grw = refs[0:4]
        w_refs = refs[4:4 + n]
        m_refs = refs[4 + n:4 + 2 * n]
        v_refs = refs[4 + 2 * n:4 + 3 * n]
        outs = refs[4 + 3 * n:]
        loss_ref = outs[0]
        loss_ref[...] = jnp.sum(vs[ROW_LOSS:ROW_LOSS + 1, :], axis=1, keepdims=True)
        grads = [
            vs[ROW_GMIX:ROW_GMIX + 1, :], gcw[...], grw[...], vs[ROW_BR:ROW_BR + 1, :],
            ws[0:LRU_WIDTH, :], vs[ROW_BA:ROW_BA + 1, :], ws[LRU_WIDTH:2 * LRU_WIDTH, :], vs[ROW_BX:ROW_BX + 1, :],
            vs[ROW_LAM:ROW_LAM + 1, :], vs[ROW_GNC:ROW_GNC + 1, 0:CONV_WIDTH], vs[ROW_GNR:ROW_GNR + 1, :],
            vs[ROW_GMLP:ROW_GMLP + 1, :], vs[ROW_GF:ROW_GF + 1, :],
        ]
        for k in range(n):
            gk = grads[k]
            delta, mn, vn = _adamw(w_refs[k][...], gk, m_refs[k][...], v_refs[k][...])
            outs[1 + 4 * k][...] = gk
            outs[2 + 4 * k][...] = delta
            outs[3 + 4 * k][...] = mn
            outs[4 + 4 * k][...] = vn

    vm = pl.BlockSpec(memory_space=pltpu.VMEM)
    out_shape = [jax.ShapeDtypeStruct((1, 1), F32)]
    for w in weights:
        out_shape += [jax.ShapeDtypeStruct(w.shape, F32)] * 4
    args = (vsum, wsum, g_cw, g_rw, *weights, *moments_m, *moments_v)
    return pl.pallas_call(
        body, out_shape=out_shape, in_specs=[vm] * len(args), out_specs=[vm] * len(out_shape),
        compiler_params=_params(vmem_mib=32), name="update_small",
    )(*args)


def kernel(x, norm_mix_g, w_in, conv_w, rnn_conv_w, rnn_conv_b, w_a, b_a, w_x, b_x, lru_lambda, g_norm_conv, g_norm_rnn, w_out, norm_mlp_g, w_mlp_in, w_mlp_out, final_norm_g, loss_target, m_norm_mix_g, m_w_in, m_conv_w, m_rnn_conv_w, m_rnn_conv_b, m_w_a, m_b_a, m_w_x, m_b_x, m_lru_lambda, m_g_norm_conv, m_g_norm_rnn, m_w_out, m_norm_mlp_g, m_w_mlp_in, m_w_mlp_out, m_final_norm_g, v_norm_mix_g, v_w_in, v_conv_w, v_rnn_conv_w, v_rnn_conv_b, v_w_a, v_b_a, v_w_x, v_b_x, v_lru_lambda, v_g_norm_conv, v_g_norm_rnn, v_w_out, v_norm_mlp_g, v_w_mlp_in, v_w_mlp_out, v_final_norm_g):
    t_len = x.shape[1]
    my_id = 4 * lax.axis_index("x") + 2 * lax.axis_index("y") + lax.axis_index("c")
    tm = min(256, t_len)
    tk = min(512, t_len)

    xs = x.reshape(t_len, D_MODEL)
    tgt = loss_target.reshape(t_len, D_MODEL)
    flat = lambda a: a.reshape(a.shape[-2:]) if a.ndim == 3 else a.reshape(1, -1)
    heads = lambda a: a.reshape(LRU_WIDTH, HEAD_DIM)

    win_blk, wout_blk, w1_blk, w2_blk, cpack = _all_gather_weights(
        flat(w_in), flat(w_out), flat(w_mlp_in), flat(w_mlp_out), flat(conv_w), flat(rnn_conv_w))
    win_t = win_blk.reshape(IN_COLS, D_MODEL)
    wout_f = wout_blk.reshape(MIX_WIDTH, D_MODEL)
    w2_f = w2_blk.reshape(D_FF, D_MODEL)
    conv_full = jnp.transpose(cpack[:, 0:3, 0:64], (1, 0, 2)).reshape(3, CONV_WIDTH)
    rnn_full = jnp.transpose(cpack[:, 3:7, :], (1, 0, 2)).reshape(4, LRU_WIDTH)

    mixer_small = (conv_full, rnn_full, flat(rnn_conv_b), heads(w_a), flat(b_a), heads(w_x), flat(b_x),
                   flat(lru_lambda), flat(g_norm_conv), flat(g_norm_rnn), wout_f)

    u, h = _in_proj(xs, flat(norm_mix_g), win_t, tm)
    x1, hs, y = _mixer_fwd(u, xs, *mixer_small, tm)
    dx1, z, dpre, h2, dx2, vec_m = _mlp_fwd_bwd(x1, tgt, flat(norm_mlp_g), flat(final_norm_g), w1_blk, w2_f, tm)
    g_w1, g_w2 = _mlp_weight_grads(h2, dpre, z, dx2, tk)
    g_wout = _tn_weight_grad(y, dx1, tk, "w_out_grad")
    du, vec_b, wab = _mixer_bwd(u, hs, dx1, *mixer_small, tm)
    grad_x, vec_x = _in_proj_bwd(du, dx1, xs, flat(norm_mix_g), win_t, tm)
    g_win_t = _tn_weight_grad(du, h, tk, "w_in_grad")

    r_w2 = _reduce_scatter(g_w2, "reduce_scatter_w_mlp_out")
    r_w1 = _reduce_scatter(g_w1, "reduce_scatter_w_mlp_in")
    r_wout = _reduce_scatter(g_wout.reshape(N_DEV, MIX_WIDTH // N_DEV, D_MODEL), "reduce_scatter_w_out")
    r_win_t = _reduce_scatter(g_win_t.reshape(N_DEV, IN_COLS // N_DEV, D_MODEL), "reduce_scatter_w_in")
    vsum, wsum = _all_reduce_small(vec_m, vec_b, vec_x, wab)

    up_win = _update_sharded(r_win_t, flat(w_in), flat(m_w_in), flat(v_w_in), 256, "update_w_in", transposed=True)
    up_wout = _update_sharded(r_wout, flat(w_out), flat(m_w_out), flat(v_w_out), 96, "update_w_out")
    up_w1 = _update_sharded(r_w1, flat(w_mlp_in), flat(m_w_mlp_in), flat(v_w_mlp_in), 256, "update_w_mlp_in")
    up_w2 = _update_sharded(r_w2, flat(w_mlp_out), flat(m_w_mlp_out), flat(v_w_mlp_out), 256, "update_w_mlp_out")

    g_cw = lax.dynamic_slice(vsum, (ROW_CW, 64 * my_id), (3, 64))
    g_rw = lax.dynamic_slice(vsum, (ROW_RW, 128 * my_id), (4, 128))
    small_w = (norm_mix_g, conv_w, rnn_conv_w, rnn_conv_b, w_a, b_a, w_x, b_x, lru_lambda, g_norm_conv, g_norm_rnn,
               norm_mlp_g, final_norm_g)
    small_m = (m_norm_mix_g, m_conv_w, m_rnn_conv_w, m_rnn_conv_b, m_w_a, m_b_a, m_w_x, m_b_x, m_lru_lambda,
               m_g_norm_conv, m_g_norm_rnn, m_norm_mlp_g, m_final_norm_g)
    small_v = (v_norm_mix_g, v_conv_w, v_rnn_conv_w, v_rnn_conv_b, v_w_a, v_b_a, v_w_x, v_b_x, v_lru_lambda,
               v_g_norm_conv, v_g_norm_rnn, v_norm_mlp_g, v_final_norm_g)
    is_heads = (False, False, False, False, True, False, True, False, False, False, False, False, False)
    as2d = lambda arrs: [heads(a) if hd else flat(a) for a, hd in zip(arrs, is_heads)]
    small_out = _update_small(vsum, wsum, g_cw, g_rw, as2d(small_w), as2d(small_m), as2d(small_v))
    loss = small_out[0].reshape(())

    names = ["norm_mix_g", "w_in", "conv_w", "rnn_conv_w", "rnn_conv_b", "w_a", "b_a", "w_x", "b_x", "lru_lambda",
             "g_norm_conv", "g_norm_rnn", "w_out", "norm_mlp_g", "w_mlp_in", "w_mlp_out", "final_norm_g"]
    originals = dict(zip(names, (norm_mix_g, w_in, conv_w, rnn_conv_w, rnn_conv_b, w_a, b_a, w_x, b_x, lru_lambda,
                                 g_norm_conv, g_norm_rnn, w_out, norm_mlp_g, w_mlp_in, w_mlp_out, final_norm_g)))
    results = {"w_in": up_win, "w_out": up_wout, "w_mlp_in": up_w1, "w_mlp_out": up_w2}
    small_names = ["norm_mix_g", "conv_w", "rnn_conv_w", "rnn_conv_b", "w_a", "b_a", "w_x", "b_x", "lru_lambda",
                   "g_norm_conv", "g_norm_rnn", "norm_mlp_g", "final_norm_g"]
    for k, nm in enumerate(small_names):
        results[nm] = small_out[1 + 4 * k:5 + 4 * k]
    out = [loss, grad_x.reshape(x.shape)]
    for kind in range(4):
        out += [results[nm][kind].reshape(originals[nm].shape) for nm in names]
    return tuple(out)
```

```python
import functools

import jax
import jax.numpy as jnp
from jax import lax
from jax.experimental import pallas as pl
from jax.experimental.pallas import tpu as pltpu

F32 = jnp.float32
BF16 = jnp.bfloat16

D_MODEL = 1024
HEAD_DIM = 64
CONV_WIDTH = 512
LRU_WIDTH = 1024
MIX_WIDTH = CONV_WIDTH + LRU_WIDTH
IN_COLS = 3 * CONV_WIDTH + 2 * LRU_WIDTH
D_FF = 4 * D_MODEL
GROUP = 256
EPS = 1e-6
LRU_C = 8.0
N_DEV = 8
SUB = 8

OFF_GB, OFF_GC, OFF_V, OFF_XR, OFF_G = 0, 512, 1024, 1536, 2560

ADAM_LR, ADAM_B1, ADAM_B2, ADAM_EPS, ADAM_WD, ADAM_STEP = 0.001, 0.9, 0.999, 1e-08, 0.01, 10
BC1 = 1.0 - ADAM_B1 ** ADAM_STEP
BC2 = 1.0 - ADAM_B2 ** ADAM_STEP

MIB = 1024 * 1024
MESH = pl.DeviceIdType.MESH

VEC_ROWS = 32
ROW_GF, ROW_GMLP, ROW_LOSS = 0, 1, 2
ROW_GNC, ROW_GNR, ROW_BR, ROW_BA, ROW_BX, ROW_LAM, ROW_CW, ROW_RW = 8, 9, 10, 11, 12, 13, 14, 17
ROW_GMIX = 24
ACC_GNC, ACC_GNR, ACC_BR, ACC_BA, ACC_BX, ACC_SP, ACC_CW, ACC_RW, N_ACC = 0, 1, 2, 3, 4, 5, 6, 9, 13


def _params(semantics=None, vmem_mib=48):
    return pltpu.CompilerParams(dimension_semantics=semantics, vmem_limit_bytes=vmem_mib * MIB)


def _rms(x):
    return lax.rsqrt(jnp.mean(x * x, axis=-1, keepdims=True) + EPS)


def _rms_bwd(dy, xhat, r, g):
    dyh = dy * g
    return r * (dyh - xhat * jnp.mean(dyh * xhat, axis=-1, keepdims=True))


def _sigmoid(x):
    return 0.5 + 0.5 * jnp.tanh(0.5 * x)


def _gelu(x):
    c0, c1 = 0.7978845608028654, 0.044715
    t = jnp.tanh(c0 * (x + c1 * x * x * x))
    ge = 0.5 * x * (1.0 + t)
    dge = 0.5 * (1.0 + t) + 0.5 * x * (1.0 - t * t) * c0 * (1.0 + 3.0 * c1 * x * x)
    return ge, dge


def _softplus_neg(lam):
    z = -lam
    e = jnp.exp(-jnp.abs(z))
    return jnp.maximum(z, 0.0) + jnp.where(e < 1e-4, e * (1.0 - 0.5 * e), jnp.log(1.0 + e))


def _lru_gates(pa, px, sp_c):
    ra = _sigmoid(pa)
    ii = _sigmoid(px)
    la = -ra * sp_c
    a = jnp.exp(la)
    x2 = 2.0 * la
    series = -x2 * (1.0 + x2 * (0.5 + x2 * (1.0 / 6.0 + x2 * (1.0 / 24.0))))
    m2 = jnp.where(x2 > -0.01, series, 1.0 - a * a)
    inv_mult = lax.rsqrt(m2)
    mult = jnp.where(m2 > 0.0, m2 * inv_mult, 0.0)
    return ra, ii, a, mult, inv_mult


def _down(cur, prev, s, row):
    return jnp.where(row >= s, pltpu.roll(cur, s, 0), pltpu.roll(prev, s, 0))


def _up(cur, nxt, s, row):
    return jnp.where(row < SUB - s, pltpu.roll(cur, SUB - s, 0), pltpu.roll(nxt, SUB - s, 0))


def _scan8_fwd(a, b, row):
    for s in (1, 2, 4):
        m = row >= s
        a_sh = pltpu.roll(a, s, 0)
        b_sh = pltpu.roll(b, s, 0)
        b = jnp.where(m, a * b_sh + b, b)
        a = jnp.where(m, a * a_sh, a)
    return a, b


def _scan8_rev(a, b, row):
    for s in (1, 2, 4):
        m = row < SUB - s
        a_sh = pltpu.roll(a, SUB - s, 0)
        b_sh = pltpu.roll(b, SUB - s, 0)
        b = jnp.where(m, a * b_sh + b, b)
        a = jnp.where(m, a * a_sh, a)
    return a, b


def _group_mask(shape):
    r = lax.broadcasted_iota(jnp.int32, shape, 0)
    c = lax.broadcasted_iota(jnp.int32, shape, 1)
    return ((r % GROUP) // HEAD_DIM) == (c // HEAD_DIM)


def _expand_heads(w):
    j = lax.broadcasted_iota(jnp.int32, (HEAD_DIM, GROUP), 0)
    c = lax.broadcasted_iota(jnp.int32, (HEAD_DIM, GROUP), 1)
    spread = (c % HEAD_DIM == j).astype(BF16)
    e = jnp.dot(w.astype(BF16), spread, preferred_element_type=F32)
    return jnp.where(_group_mask(e.shape), e, 0.0).astype(BF16)


def _fold_heads(p):
    p = jnp.where(_group_mask(p.shape), p, 0.0)
    c = lax.broadcasted_iota(jnp.int32, (GROUP, HEAD_DIM), 0)
    j = lax.broadcasted_iota(jnp.int32, (GROUP, HEAD_DIM), 1)
    fold = (c % HEAD_DIM == j).astype(BF16)
    hi = p.astype(BF16)
    rest = p - hi.astype(F32)
    mid = rest.astype(BF16)
    lo = (rest - mid.astype(F32)).astype(BF16)
    dot = functools.partial(jnp.dot, preferred_element_type=F32)
    return dot(hi, fold) + dot(mid, fold) + dot(lo, fold)


def _block_diag_apply(xb, wbd_ref):
    parts = [jnp.dot(xb[:, g * GROUP:(g + 1) * GROUP], wbd_ref[g * GROUP:(g + 1) * GROUP, :],
                     preferred_element_type=F32) for g in range(LRU_WIDTH // GROUP)]
    return jnp.concatenate(parts, axis=1)


def _block_diag_apply_t(db, wbd_ref):
    parts = [lax.dot_general(db[:, g * GROUP:(g + 1) * GROUP], wbd_ref[g * GROUP:(g + 1) * GROUP, :],
                             (((1,), (1,)), ((), ())), preferred_element_type=F32)
             for g in range(LRU_WIDTH // GROUP)]
    return jnp.concatenate(parts, axis=1)


def _dot_nt(a, b):
    return lax.dot_general(a, b, (((1,), (1,)), ((), ())), preferred_element_type=F32)


def _dot_tn(a, b):
    return lax.dot_general(a, b, (((0,), (0,)), ((), ())), preferred_element_type=F32)


CHUNKS_IN_FLIGHT = 4


def _chunk_loop(n_chunks, chunk, init):
    def body(k, carry):
        for j in range(CHUNKS_IN_FLIGHT):
            carry = chunk(k * CHUNKS_IN_FLIGHT + j, carry)
        return carry

    return lax.fori_loop(0, n_chunks // CHUNKS_IN_FLIGHT, body, init)


def _place():
    x, y, c = lax.axis_index("x"), lax.axis_index("y"), lax.axis_index("c")
    return x, y, c


def _block_id(chip, core):
    return 4 * chip[0] + 2 * chip[1] + core


def _other_chips(x, y):
    return [(1 - x, y), (x, 1 - y), (1 - x, 1 - y)]


def _remote_copy(src, dst, send_sem, recv_sem, to):
    return pltpu.make_async_remote_copy(src_ref=src, dst_ref=dst, send_sem=send_sem, recv_sem=recv_sem,
                                        device_id=to, device_id_type=MESH)


HBM_SPEC = pl.BlockSpec(memory_space=pl.ANY)


def _all_gather_w_in(w_in, w_out, w_mlp_in, w_mlp_out, conv_w, rnn_conv_w):
    n_in = w_in.shape[1]
    n_arr = 2

    def body(win_ref, wout_ref, w1_ref, w2_ref, cw_ref, rw_ref,
             o_win, o_cp, o_wout, o_w1, o_w2, padbuf, send_sems, recv_sems):
        x, y, c = _place()
        me = (x, y, c)
        my_id = _block_id((x, y), c)
        sibling = (x, y, 1 - c)
        chips = _other_chips(x, y)
        outs = [o_win, o_cp]

        padbuf[...] = jnp.zeros(padbuf.shape, F32)
        padbuf[:, 0:n_in] = win_ref[...]
        o_win[my_id] = padbuf[...].T[0:n_in, :].astype(BF16)
        o_cp[my_id] = jnp.zeros(o_cp.shape[1:], F32)
        o_cp[my_id, 0:3, 0:64] = cw_ref[...]
        o_cp[my_id, 3:7, :] = rw_ref[...]

        def copy(arr, k, block, to):
            blk = outs[arr].at[block]
            return _remote_copy(blk, blk, send_sems.at[arr, k], recv_sems.at[arr, k], to)

        first = []
        for arr in range(n_arr):
            first.append(copy(arr, 0, my_id, sibling))
            first += [copy(arr, 1 + j, my_id, (*chip, c)) for j, chip in enumerate(chips)]
        for cp in first:
            cp.start()
        o_wout[...] = wout_ref[...].astype(BF16)
        o_w1[...] = w1_ref[...].astype(BF16)
        o_w2[...] = w2_ref[...].astype(BF16)
        passed = []
        for j, chip in enumerate(chips):
            for arr in range(n_arr):
                copy(arr, 1 + j, _block_id(chip, c), me).wait_recv()
                fwd = copy(arr, 4 + j, _block_id(chip, c), sibling)
                fwd.start()
                passed.append(fwd)
        for arr in range(n_arr):
            copy(arr, 0, _block_id((x, y), 1 - c), me).wait_recv()
            for j, chip in enumerate(chips):
                copy(arr, 4 + j, _block_id(chip, 1 - c), me).wait_recv()
        for cp in first + passed:
            cp.wait_send()

    vm = pl.BlockSpec(memory_space=pltpu.VMEM)
    shapes = (
        jax.ShapeDtypeStruct((N_DEV, n_in, D_MODEL), BF16),
        jax.ShapeDtypeStruct((N_DEV, 8, 128), F32),
        jax.ShapeDtypeStruct(w_out.shape, BF16),
        jax.ShapeDtypeStruct(w_mlp_in.shape, BF16),
        jax.ShapeDtypeStruct(w_mlp_out.shape, BF16),
    )
    return pl.pallas_call(
        body, out_shape=shapes, in_specs=[vm] * 6, out_specs=[vm] * 5,
        scratch_shapes=[pltpu.VMEM((D_MODEL, 512), F32),
                        pltpu.SemaphoreType.DMA((n_arr, 7)), pltpu.SemaphoreType.DMA((n_arr, 7))],
        compiler_params=_params(vmem_mib=40), name="all_gather_w_in",
    )(w_in, w_out, w_mlp_in, w_mlp_out, conv_w, rnn_conv_w)


def _host_all_gather(step, n_steps, shards, fulls, send_sems, recv_sems, local_sems):
    x, y, c = _place()
    me = (x, y, c)
    my_id = _block_id((x, y), c)
    sibling = (x, y, 1 - c)
    chips = _other_chips(x, y)
    n_arr = len(shards)

    def copy(arr, k, block, to, src=None):
        dst = fulls[arr].at[block]
        return _remote_copy(dst if src is None else src, dst, send_sems.at[arr, k], recv_sems.at[arr, k], to)

    def local(arr):
        return pltpu.make_async_copy(shards[arr], fulls[arr].at[my_id], local_sems.at[arr])

    @pl.when(step == 0)
    def _():
        for arr in range(n_arr):
            local(arr).start()
            copy(arr, 0, my_id, sibling, shards[arr]).start()
            for j, chip in enumerate(chips):
                copy(arr, 1 + j, my_id, (*chip, c), shards[arr]).start()

    @pl.when(step == max(n_steps - 2, 0))
    def _():
        for j, chip in enumerate(chips):
            for arr in range(n_arr):
                copy(arr, 1 + j, _block_id(chip, c), me).wait_recv()
                copy(arr, 4 + j, _block_id(chip, c), sibling).start()

    @pl.when(step == n_steps - 1)
    def _():
        for arr in range(n_arr):
            copy(arr, 0, _block_id((x, y), 1 - c), me).wait_recv()
            for j, chip in enumerate(chips):
                copy(arr, 4 + j, _block_id(chip, 1 - c), me).wait_recv()
            for k in range(4):
                copy(arr, k, my_id, me, shards[arr]).wait_send()
            for j, chip in enumerate(chips):
                copy(arr, 4 + j, _block_id(chip, c), me).wait_send()
            local(arr).wait()


def _host_pair_exchange(step, n_steps, gs, sibs, send_sems, recv_sems):
    x, y, c = _place()
    sibling = (x, y, 1 - c)
    chips = [(x, y)] + _other_chips(x, y)

    def d2d(arr, q):
        return _remote_copy(gs[arr].at[_block_id(chips[q], 1 - c)], sibs[arr].at[q],
                            send_sems.at[arr, q], recv_sems.at[arr, q], sibling)

    @pl.when(step == 0)
    def _():
        for arr in range(len(gs)):
            for q in (1, 2, 3, 0):
                d2d(arr, q).start()

    @pl.when(step == n_steps - 1)
    def _():
        for arr in range(len(gs)):
            for q in range(4):
                d2d(arr, q).wait()


def _host_chip_exchange(step, n_steps, hsends, hrecvs, send_sems, recv_sems):
    x, y, c = _place()
    chips = _other_chips(x, y)

    def ici(arr, j):
        return _remote_copy(hsends[arr].at[j], hrecvs[arr].at[j], send_sems.at[arr, j], recv_sems.at[arr, j],
                            (*chips[j], c))

    @pl.when(step == 0)
    def _():
        for arr in range(len(hsends)):
            for j in range(3):
                ici(arr, j).start()

    @pl.when(step == n_steps - 1)
    def _():
        for arr in range(len(hsends)):
            for j in range(3):
                ici(arr, j).wait()


def _pair_sum(g, sib, name):
    _, rows, cols = g.shape

    def body(g_ref, sib_ref, hs_ref, own_ref):
        x, y, c = _place()
        chips = [(x, y)] + _other_chips(x, y)
        for q in (1, 2, 3):
            hs_ref[q - 1] = (g_ref[_block_id(chips[q], c)].astype(F32) + sib_ref[q].astype(F32)).astype(BF16)
        own_ref[...] = g_ref[_block_id(chips[0], c)].astype(F32) + sib_ref[0].astype(F32)

    vm = pl.BlockSpec(memory_space=pltpu.VMEM)
    return pl.pallas_call(
        body, out_shape=(jax.ShapeDtypeStruct((3, rows, cols), BF16), jax.ShapeDtypeStruct((rows, cols), F32)),
        in_specs=[vm, vm], out_specs=[vm, vm], compiler_params=_params(vmem_mib=40), name=name,
    )(g, sib)


def _exchange_scratch(n_arr, n_copies):
    return [pltpu.SemaphoreType.DMA((n_arr, n_copies)), pltpu.SemaphoreType.DMA((n_arr, n_copies))]


def _reduce_scatter(g, name):
    _, rows, cols = g.shape

    def body(g_ref, o_ref, sib, hsend, hrecv, d_send, d_recv, i_send, i_recv):
        x, y, c = _place()
        sibling = (x, y, 1 - c)
        chips = [(x, y), (1 - x, y), (x, 1 - y), (1 - x, 1 - y)]

        def gid(chip, core):
            return 4 * chip[0] + 2 * chip[1] + core

        def d2d(q):
            return pltpu.make_async_remote_copy(
                src_ref=g_ref.at[gid(chips[q], 1 - c)], dst_ref=sib.at[q],
                send_sem=d_send.at[q], recv_sem=d_recv.at[q], device_id=sibling, device_id_type=MESH)

        def ici(q):
            return pltpu.make_async_remote_copy(
                src_ref=hsend.at[q - 1], dst_ref=hrecv.at[q - 1],
                send_sem=i_send.at[q - 1], recv_sem=i_recv.at[q - 1],
                device_id=(*chips[q], c), device_id_type=MESH)

        for q in (1, 2, 3, 0):
            d2d(q).start()
        for q in (1, 2, 3):
            d2d(q).wait_recv()
            hsend[q - 1] = (g_ref[gid(chips[q], c)].astype(F32) + sib[q].astype(F32)).astype(BF16)
            ici(q).start()
        d2d(0).wait_recv()
        acc = g_ref[gid(chips[0], c)].astype(F32) + sib[0].astype(F32)
        for q in (1, 2, 3):
            ici(q).wait_recv()
            acc = acc + hrecv[q - 1].astype(F32)
        o_ref[...] = acc
        for q in range(4):
            d2d(q).wait_send()
        for q in (1, 2, 3):
            ici(q).wait_send()

    vm = pl.BlockSpec(memory_space=pltpu.VMEM)
    return pl.pallas_call(
        body, out_shape=jax.ShapeDtypeStruct((rows, cols), F32), in_specs=[vm], out_specs=vm,
        scratch_shapes=[pltpu.VMEM((4, rows, cols), BF16), pltpu.VMEM((3, rows, cols), BF16),
                        pltpu.VMEM((3, rows, cols), BF16),
                        pltpu.SemaphoreType.DMA((4,)), pltpu.SemaphoreType.DMA((4,)),
                        pltpu.SemaphoreType.DMA((3,)), pltpu.SemaphoreType.DMA((3,))],
        compiler_params=_params(vmem_mib=48), name=name,
    )(g)


def _all_reduce_small(vec_m, vec_b, vec_x, wab):
    wrows = wab.shape[0] // N_DEV

    def body(vm_ref, vb_ref, vx_ref, w_ref, o_vec, o_w, vpack, vrecv, wrecv, wred,
             v_send, v_recv, w_send, w_recv, b_send, b_recv):
        x, y, c = _place()
        my_id = 4 * x + 2 * y + c

        def peer(k):
            return (x ^ ((k >> 2) & 1), y ^ ((k >> 1) & 1), c ^ (k & 1))

        def pid(k):
            p = peer(k)
            return 4 * p[0] + 2 * p[1] + p[2]

        vpack[0:8, :] = vm_ref[...]
        vpack[8:24, :] = vb_ref[...]
        vpack[24:32, :] = vx_ref[...]
        vrecv[my_id] = vpack[...]

        def vcopy(k):
            return pltpu.make_async_remote_copy(
                src_ref=vpack, dst_ref=vrecv.at[my_id], send_sem=v_send.at[k], recv_sem=v_recv.at[k],
                device_id=peer(k), device_id_type=MESH)

        def wcopy(k):
            return pltpu.make_async_remote_copy(
                src_ref=w_ref.at[pl.ds(pl.multiple_of(pid(k) * wrows, SUB), wrows), :], dst_ref=wrecv.at[k],
                send_sem=w_send.at[k], recv_sem=w_recv.at[k], device_id=peer(k), device_id_type=MESH)

        def bcopy(k):
            mine = o_w.at[pl.ds(pl.multiple_of(my_id * wrows, SUB), wrows), :]
            return pltpu.make_async_remote_copy(
                src_ref=wred, dst_ref=mine, send_sem=b_send.at[k], recv_sem=b_recv.at[k],
                device_id=peer(k), device_id_type=MESH)

        for k in range(1, N_DEV):
            vcopy(k).start()
            wcopy(k).start()
        red = w_ref[pl.ds(pl.multiple_of(my_id * wrows, SUB), wrows), :]
        for k in range(1, N_DEV):
            wcopy(k).wait_recv()
            red = red + wrecv[k]
        wred[...] = red
        o_w[pl.ds(pl.multiple_of(my_id * wrows, SUB), wrows), :] = red
        for k in range(1, N_DEV):
            bcopy(k).start()
        for k in range(1, N_DEV):
            vcopy(k).wait_recv()
        tot = vrecv[0]
        for s in range(1, N_DEV):
            tot = tot + vrecv[s]
        o_vec[...] = tot
        for k in range(1, N_DEV):
            bcopy(k).wait_recv()
        for k in range(1, N_DEV):
            vcopy(k).wait_send()
            wcopy(k).wait_send()
            bcopy(k).wait_send()

    vm = pl.BlockSpec(memory_space=pltpu.VMEM)
    dma7 = pltpu.SemaphoreType.DMA((N_DEV,))
    return pl.pallas_call(
        body, out_shape=(jax.ShapeDtypeStruct((VEC_ROWS, D_MODEL), F32), jax.ShapeDtypeStruct(wab.shape, F32)),
        in_specs=[vm] * 4, out_specs=[vm] * 2,
        scratch_shapes=[pltpu.VMEM((VEC_ROWS, D_MODEL), F32), pltpu.VMEM((N_DEV, VEC_ROWS, D_MODEL), F32),
                        pltpu.VMEM((N_DEV, wrows, HEAD_DIM), F32), pltpu.VMEM((wrows, HEAD_DIM), F32),
                        dma7, dma7, dma7, dma7, dma7, dma7],
        compiler_params=_params(vmem_mib=32), name="all_reduce_small",
    )(vec_m, vec_b, vec_x, wab)


def _in_proj(x, g_mix, win_t, wout_shard, tm):
    t_len = x.shape[0]
    n_steps = t_len // tm

    def body(x_ref, g_ref, w_ref, wout_ref, u_ref, h_ref, wout_full, send_sems, recv_sems, local_sems):
        _host_all_gather(pl.program_id(0), n_steps, [wout_ref], [wout_full], send_sems, recv_sems, local_sems)
        xv = x_ref[...]
        h = (xv * _rms(xv) * g_ref[...]).astype(BF16)
        h_ref[...] = h
        u_ref[...] = _dot_nt(h, w_ref[...])

    return pl.pallas_call(
        body, grid=(n_steps,),
        in_specs=[pl.BlockSpec((tm, D_MODEL), lambda i: (i, 0)), pl.BlockSpec((1, D_MODEL), lambda i: (0, 0)),
                  pl.BlockSpec((IN_COLS, D_MODEL), lambda i: (0, 0)), HBM_SPEC],
        out_specs=[pl.BlockSpec((tm, IN_COLS), lambda i: (i, 0)), pl.BlockSpec((tm, D_MODEL), lambda i: (i, 0)),
                   HBM_SPEC],
        out_shape=[jax.ShapeDtypeStruct((t_len, IN_COLS), F32), jax.ShapeDtypeStruct((t_len, D_MODEL), BF16),
                   jax.ShapeDtypeStruct((N_DEV,) + wout_shard.shape, BF16)],
        scratch_shapes=_exchange_scratch(1, 7) + [pltpu.SemaphoreType.DMA((1,))],
        compiler_params=_params(("arbitrary",), 56), name="in_proj",
    )(x, g_mix, win_t, wout_shard)


def _conv3_chunk(u_ref, r, cv_prev, cw, row):
    gb = u_ref[pl.ds(r, SUB), OFF_GB:OFF_GB + CONV_WIDTH]
    gc = u_ref[pl.ds(r, SUB), OFF_GC:OFF_GC + CONV_WIDTH]
    v = u_ref[pl.ds(r, SUB), OFF_V:OFF_V + CONV_WIDTH]
    cv = gc * v
    cv_m1 = _down(cv, cv_prev, 1, row)
    cv_m2 = _down(cv, cv_prev, 2, row)
    cq = cw[2:3, :] * cv + cw[1:2, :] * cv_m1 + cw[0:1, :] * cv_m2
    return gb, gc, v, cv, cv_m1, cv_m2, cq


def _conv4_chunk(u_ref, r, xin_prev, rw, rb, row):
    xin = u_ref[pl.ds(r, SUB), OFF_XR:OFF_XR + LRU_WIDTH]
    m1 = _down(xin, xin_prev, 1, row)
    m2 = _down(xin, xin_prev, 2, row)
    m3 = _down(xin, xin_prev, 3, row)
    xr = rw[3:4, :] * xin + rw[2:3, :] * m1 + rw[1:2, :] * m2 + rw[0:1, :] * m3 + rb
    return xin, m1, m2, m3, xr


def _mixer_fwd(u, conv_w, rnn_conv_w, rnn_conv_b, wa, b_a, wx, b_x, lam, gnc, gnr, w1_shard, w2_shard, tm):
    t_len = u.shape[0]
    n_steps = t_len // tm
    n_chunks = tm // SUB

    def body(u_ref, cw_ref, rw_ref, rb_ref, wa_ref, ba_ref, wx_ref, bx_ref, lam_ref, gnc_ref, gnr_ref,
             w1_shard, w2_shard, hs_ref, y_ref, w1_full, w2_full,
             y_s, xr_s, pa_s, px_s, wabd, wxbd, cv_car, xin_car, h_car, send_sems, recv_sems, local_sems):
        _host_all_gather(pl.program_id(0), n_steps, [w1_shard, w2_shard], [w1_full, w2_full],
                         send_sems, recv_sems, local_sems)

        @pl.when(pl.program_id(0) == 0)
        def _():
            cv_car[...] = jnp.zeros(cv_car.shape, F32)
            xin_car[...] = jnp.zeros(xin_car.shape, F32)
            h_car[...] = jnp.zeros(h_car.shape, F32)
            wabd[...] = _expand_heads(wa_ref[...])
            wxbd[...] = _expand_heads(wx_ref[...])

        row_c = lax.broadcasted_iota(jnp.int32, (SUB, CONV_WIDTH), 0)
        row_r = lax.broadcasted_iota(jnp.int32, (SUB, LRU_WIDTH), 0)
        cw = cw_ref[...]
        rw = rw_ref[...]
        rb = rb_ref[...]
        g_c = gnc_ref[...]
        g_r = gnr_ref[...]
        sp_c = LRU_C * _softplus_neg(lam_ref[...])

        def convs(i, carry):
            cv_prev, xin_prev = carry
            r = pl.multiple_of(i * SUB, SUB)
            gb, _, _, cv, _, _, cq = _conv3_chunk(u_ref, r, cv_prev, cw, row_c)
            y_c = gb * cq
            y_s[pl.ds(r, SUB), 0:CONV_WIDTH] = y_c * _rms(y_c) * g_c
            xin, _, _, _, xr = _conv4_chunk(u_ref, r, xin_prev, rw, rb, row_r)
            xr_s[pl.ds(r, SUB), :] = xr
            return cv, xin

        cv_last, xin_last = _chunk_loop(n_chunks, convs, (cv_car[...], xin_car[...]))
        cv_car[...] = cv_last
        xin_car[...] = xin_last

        xrb = xr_s[...].astype(BF16)
        pa_s[...] = _block_diag_apply(xrb, wabd) + ba_ref[...]
        px_s[...] = _block_diag_apply(xrb, wxbd) + bx_ref[...]

        def recur(i, h_prev):
            r = pl.multiple_of(i * SUB, SUB)
            xr = xr_s[pl.ds(r, SUB), :]
            _, ii, a, mult, _ = _lru_gates(pa_s[pl.ds(r, SUB), :], px_s[pl.ds(r, SUB), :], sp_c)
            a_cum, b_cum = _scan8_fwd(a, mult * ii * xr, row_r)
            h = a_cum * h_prev + b_cum
            hs_ref[pl.ds(r, SUB), :] = h
            ge, _ = _gelu(u_ref[pl.ds(r, SUB), OFF_G:OFF_G + LRU_WIDTH])
            y_r = h * ge
            y_s[pl.ds(r, SUB), CONV_WIDTH:MIX_WIDTH] = y_r * _rms(y_r) * g_r
            return h[SUB - 1:SUB, :]

        h_car[...] = _chunk_loop(n_chunks, recur, h_car[...])

        y_ref[...] = y_s[...].astype(BF16)

    row_tile = lambda w: pl.BlockSpec((tm, w), lambda i: (i, 0))
    whole = lambda a: pl.BlockSpec(a.shape, lambda i: (0,) * a.ndim)
    smalls = (conv_w, rnn_conv_w, rnn_conv_b, wa, b_a, wx, b_x, lam, gnc, gnr)
    return pl.pallas_call(
        body, grid=(n_steps,),
        in_specs=[row_tile(IN_COLS)] + [whole(a) for a in smalls] + [HBM_SPEC, HBM_SPEC],
        out_specs=[row_tile(LRU_WIDTH), row_tile(MIX_WIDTH), HBM_SPEC, HBM_SPEC],
        out_shape=[jax.ShapeDtypeStruct((t_len, LRU_WIDTH), F32), jax.ShapeDtypeStruct((t_len, MIX_WIDTH), BF16),
                   jax.ShapeDtypeStruct((N_DEV,) + w1_shard.shape, BF16),
                   jax.ShapeDtypeStruct((N_DEV,) + w2_shard.shape, BF16)],
        scratch_shapes=[pltpu.VMEM((tm, MIX_WIDTH), F32), pltpu.VMEM((tm, LRU_WIDTH), F32),
                        pltpu.VMEM((tm, LRU_WIDTH), F32), pltpu.VMEM((tm, LRU_WIDTH), F32),
                        pltpu.VMEM((LRU_WIDTH, GROUP), BF16), pltpu.VMEM((LRU_WIDTH, GROUP), BF16),
                        pltpu.VMEM((SUB, CONV_WIDTH), F32), pltpu.VMEM((SUB, LRU_WIDTH), F32),
                        pltpu.VMEM((1, LRU_WIDTH), F32)] + _exchange_scratch(2, 7) + [pltpu.SemaphoreType.DMA((2,))],
        compiler_params=_params(("arbitrary",), 56), name="mixer_fwd",
    )(u, *smalls, w1_shard, w2_shard)


def _mlp_fwd_bwd(x, y, target, g_mlp, g_f, w_out, w1, w2, tm):
    t_len = x.shape[0]
    n_blk, _, blk = w1.shape

    def body(x_ref, y_ref, tg_ref, gm_ref, gf_ref, wout_hbm, w1_hbm, w2_hbm,
             dx1_ref, z_ref, dpre_ref, h2_ref, dx2_ref, vec_ref, wout_s, w1_s, w2_s, rp_s, sem):
        @pl.when(pl.program_id(0) == 0)
        def _():
            loads = [pltpu.make_async_copy(src, dst, sem.at[k])
                     for k, (src, dst) in enumerate(((wout_hbm, wout_s), (w1_hbm, w1_s), (w2_hbm, w2_s)))]
            for cp in loads:
                cp.start()
            vec_ref[...] = jnp.zeros(vec_ref.shape, F32)
            for cp in loads:
                cp.wait()

        x1v = x_ref[...] + jnp.dot(y_ref[...], wout_s[...], preferred_element_type=F32)
        g_m = gm_ref[...]
        g_o = gf_ref[...]
        r2 = _rms(x1v)
        x1h = x1v * r2
        h2 = (x1h * g_m).astype(BF16)
        h2_ref[...] = h2
        x2 = x1v
        for k in range(n_blk):
            rp = jnp.maximum(jnp.dot(h2, w1_s[k], preferred_element_type=F32), 0.0)
            rp_s[:, k * blk:(k + 1) * blk] = rp
            zb = (rp * rp).astype(BF16)
            z_ref[:, k * blk:(k + 1) * blk] = zb
            x2 = x2 + jnp.dot(zb, w2_s[k * blk:(k + 1) * blk, :], preferred_element_type=F32)
        r3 = _rms(x2)
        x2h = x2 * r3
        err = x2h * g_o - tg_ref[...]
        dout = err * (1.0 / D_MODEL)
        vec_ref[ROW_LOSS:ROW_LOSS + 1, :] += (0.5 / D_MODEL) * jnp.sum(err * err, axis=0, keepdims=True)
        vec_ref[ROW_GF:ROW_GF + 1, :] += jnp.sum(dout * x2h, axis=0, keepdims=True)
        dx2 = _rms_bwd(dout, x2h, r3, g_o)
        dx2b = dx2.astype(BF16)
        dx2_ref[...] = dx2b
        dh2 = jnp.zeros((tm, D_MODEL), F32)
        for k in range(n_blk):
            dz = _dot_nt(dx2b, w2_s[k * blk:(k + 1) * blk, :])
            dpb = (dz * 2.0 * rp_s[:, k * blk:(k + 1) * blk]).astype(BF16)
            dpre_ref[:, k * blk:(k + 1) * blk] = dpb
            dh2 = dh2 + _dot_nt(dpb, w1_s[k])
        vec_ref[ROW_GMLP:ROW_GMLP + 1, :] += jnp.sum(dh2 * x1h, axis=0, keepdims=True)
        dx1_ref[...] = dx2 + _rms_bwd(dh2, x1h, r2, g_m)

    row_tile = lambda w: pl.BlockSpec((tm, w), lambda i: (i, 0))
    vec_spec = pl.BlockSpec((1, D_MODEL), lambda i: (0, 0))
    return pl.pallas_call(
        body, grid=(t_len // tm,),
        in_specs=[row_tile(D_MODEL), row_tile(MIX_WIDTH), row_tile(D_MODEL), vec_spec, vec_spec,
                  HBM_SPEC, HBM_SPEC, HBM_SPEC],
        out_specs=[row_tile(D_MODEL), row_tile(D_FF), row_tile(D_FF), row_tile(D_MODEL), row_tile(D_MODEL),
                   pl.BlockSpec((SUB, D_MODEL), lambda i: (0, 0))],
        out_shape=[jax.ShapeDtypeStruct((t_len, D_MODEL), F32), jax.ShapeDtypeStruct((t_len, D_FF), BF16),
                   jax.ShapeDtypeStruct((t_len, D_FF), BF16), jax.ShapeDtypeStruct((t_len, D_MODEL), BF16),
                   jax.ShapeDtypeStruct((t_len, D_MODEL), BF16), jax.ShapeDtypeStruct((SUB, D_MODEL), F32)],
        scratch_shapes=[pltpu.VMEM(w_out.shape, BF16), pltpu.VMEM(w1.shape, BF16), pltpu.VMEM(w2.shape, BF16),
                        pltpu.VMEM((tm, D_FF), F32), pltpu.SemaphoreType.DMA((3,))],
        compiler_params=_params(("arbitrary",), 58), name="mlp_fwd_bwd",
    )(x, y, target, g_mlp, g_f, w_out, w1, w2)


def _mlp_weight_grads(h2, dpre, z, dx2, tk):
    t_len = h2.shape[0]
    blk = D_FF // N_DEV

    def body(h2_ref, dp_ref, z_ref, dx2_ref, g1_ref, g2_ref, acc1, acc2):
        j = pl.program_id(1)

        @pl.when(j == 0)
        def _():
            acc1[...] = jnp.zeros(acc1.shape, F32)
            acc2[...] = jnp.zeros(acc2.shape, F32)

        acc1[...] += _dot_tn(h2_ref[...], dp_ref[...])
        acc2[...] += _dot_tn(z_ref[...], dx2_ref[...])

        @pl.when(j == pl.num_programs(1) - 1)
        def _():
            g1_ref[0] = acc1[...].astype(BF16)
            g2_ref[0] = acc2[...].astype(BF16)

    return pl.pallas_call(
        body, grid=(N_DEV, t_len // tk),
        in_specs=[pl.BlockSpec((tk, D_MODEL), lambda k, j: (j, 0)), pl.BlockSpec((tk, blk), lambda k, j: (j, k)),
                  pl.BlockSpec((tk, blk), lambda k, j: (j, k)), pl.BlockSpec((tk, D_MODEL), lambda k, j: (j, 0))],
        out_specs=[pl.BlockSpec((1, D_MODEL, blk), lambda k, j: (k, 0, 0)),
                   pl.BlockSpec((1, blk, D_MODEL), lambda k, j: (k, 0, 0))],
        out_shape=[jax.ShapeDtypeStruct((N_DEV, D_MODEL, blk), BF16), jax.ShapeDtypeStruct((N_DEV, blk, D_MODEL), BF16)],
        scratch_shapes=[pltpu.VMEM((D_MODEL, blk), F32), pltpu.VMEM((blk, D_MODEL), F32)],
        compiler_params=_params(("arbitrary", "arbitrary"), 40), name="mlp_weight_grads",
    )(h2, dpre, z, dx2)


def _mixer_bwd(u, hs, dx1, conv_w, rnn_conv_w, rnn_conv_b, wa, b_a, wx, b_x, lam, gnc, gnr, w_out,
               chip_sums, g_wout, tm):
    t_len = u.shape[0]
    n_tiles = t_len // tm
    n_chunks = tm // SUB
    per_tile = tm // SUB
    n_sums = len(chip_sums)

    def body(u_ref, up_ref, hs_ref, hp_ref, dx1_ref, cw_ref, rw_ref, rb_ref, wa_ref, ba_ref, wx_ref, bx_ref,
             lam_ref, gnc_ref, gnr_ref, wout_ref, *rest):
        hsends = rest[0:n_sums]
        gwout_ref = rest[n_sums]
        du_ref, vec_ref, wab_ref = rest[n_sums + 1:n_sums + 4]
        hrecvs = rest[n_sums + 4:2 * n_sums + 4]
        sib_wout = rest[2 * n_sums + 4]
        (du_s, dy_s, xr_s, pa_s, px_s, dpa_s, dpx_s, dxr_s, wabd, wxbd, acc, dwa_acc, dwx_acc,
         a_car, dh_car, dcq_car, dxr_car, i_send, i_recv, d_send, d_recv) = rest[2 * n_sums + 5:]
        step = pl.program_id(0)
        _host_chip_exchange(step, n_tiles, hsends, hrecvs, i_send, i_recv)
        _host_pair_exchange(step, n_tiles, [gwout_ref], [sib_wout], d_send, d_recv)
        has_prev = (step < n_tiles - 1).astype(F32)

        @pl.when(step == 0)
        def _():
            acc[...] = jnp.zeros(acc.shape, F32)
            dwa_acc[...] = jnp.zeros(dwa_acc.shape, F32)
            dwx_acc[...] = jnp.zeros(dwx_acc.shape, F32)
            a_car[...] = jnp.ones(a_car.shape, F32)
            dh_car[...] = jnp.zeros(dh_car.shape, F32)
            dcq_car[...] = jnp.zeros(dcq_car.shape, F32)
            dxr_car[...] = jnp.zeros(dxr_car.shape, F32)
            wabd[...] = _expand_heads(wa_ref[...])
            wxbd[...] = _expand_heads(wx_ref[...])

        row_c = lax.broadcasted_iota(jnp.int32, (SUB, CONV_WIDTH), 0)
        row_r = lax.broadcasted_iota(jnp.int32, (SUB, LRU_WIDTH), 0)
        cw = cw_ref[...]
        rw = rw_ref[...]
        rb = rb_ref[...]
        g_c = gnc_ref[...]
        g_r = gnr_ref[...]
        sp_c = LRU_C * _softplus_neg(lam_ref[...])

        up = up_ref[...] * has_prev
        cv_before = up[:, OFF_GC:OFF_GC + CONV_WIDTH] * up[:, OFF_V:OFF_V + CONV_WIDTH]
        xin_before = up[:, OFF_XR:OFF_XR + LRU_WIDTH]
        hs_before = hp_ref[...] * has_prev

        dy_s[...] = _dot_nt(dx1_ref[...].astype(BF16), wout_ref[...])

        def conv4_fwd(i, xin_prev):
            r = pl.multiple_of(i * SUB, SUB)
            xin, _, _, _, xr = _conv4_chunk(u_ref, r, xin_prev, rw, rb, row_r)
            xr_s[pl.ds(r, SUB), :] = xr
            return xin

        _chunk_loop(n_chunks, conv4_fwd, xin_before)
        xrb = xr_s[...].astype(BF16)
        pa_s[...] = _block_diag_apply(xrb, wabd) + ba_ref[...]
        px_s[...] = _block_diag_apply(xrb, wxbd) + bx_ref[...]

        def recur_bwd(j, carry):
            a_later, dh_later = carry
            i = n_chunks - 1 - j
            r = pl.multiple_of(i * SUB, SUB)
            rp = pl.multiple_of(jnp.maximum(i - 1, 0) * SUB, SUB)
            xr = xr_s[pl.ds(r, SUB), :]
            hs_c = hs_ref[pl.ds(r, SUB), :]
            hs_prev = jnp.where(i == 0, hs_before, hs_ref[pl.ds(rp, SUB), :])
            h_m1 = _down(hs_c, hs_prev, 1, row_r)
            ra, ii, a, mult, inv_mult = _lru_gates(pa_s[pl.ds(r, SUB), :], px_s[pl.ds(r, SUB), :], sp_c)
            ge, dge = _gelu(u_ref[pl.ds(r, SUB), OFF_G:OFF_G + LRU_WIDTH])
            y_r = hs_c * ge
            rr = _rms(y_r)
            yhat = y_r * rr
            dyn = dy_s[pl.ds(r, SUB), CONV_WIDTH:MIX_WIDTH]
            acc[ACC_GNR] += dyn * yhat
            dy_r = _rms_bwd(dyn, yhat, rr, g_r)
            du_s[pl.ds(r, SUB), OFF_G:OFF_G + LRU_WIDTH] = dy_r * hs_c * dge
            a_cum, d_cum = _scan8_rev(_up(a, a_later, 1, row_r), dy_r * ge, row_r)
            dh = a_cum * dh_later + d_cum
            dmult = dh * ii * xr
            dii = dh * mult * xr
            dxr_s[pl.ds(r, SUB), :] = dh * mult * ii
            dla = dh * h_m1 * a - dmult * a * a * inv_mult
            acc[ACC_SP] += -dla * ra
            dpa = -dla * sp_c * ra * (1.0 - ra)
            dpx = dii * ii * (1.0 - ii)
            acc[ACC_BA] += dpa
            acc[ACC_BX] += dpx
            dpa_s[pl.ds(r, SUB), :] = dpa
            dpx_s[pl.ds(r, SUB), :] = dpx
            return a, dh[0:1, :]

        a_first, dh_first = _chunk_loop(n_chunks, recur_bwd, (a_car[...], dh_car[...]))
        a_car[...] = a_first
        dh_car[...] = dh_first

        dpab = dpa_s[...].astype(BF16)
        dpxb = dpx_s[...].astype(BF16)
        dxr_s[...] += _block_diag_apply_t(dpab, wabd) + _block_diag_apply_t(dpxb, wxbd)
        for g in range(LRU_WIDTH // GROUP):
            cols = slice(g * GROUP, (g + 1) * GROUP)
            dwa_acc[cols, :] += _dot_tn(xrb[:, cols], dpab[:, cols])
            dwx_acc[cols, :] += _dot_tn(xrb[:, cols], dpxb[:, cols])

        def convs_bwd(j, carry):
            dcq_later, dxr_later = carry
            i = n_chunks - 1 - j
            r = pl.multiple_of(i * SUB, SUB)
            rp = pl.multiple_of(jnp.maximum(i - 1, 0) * SUB, SUB)
            cv_prev = jnp.where(i == 0, cv_before,
                                u_ref[pl.ds(rp, SUB), OFF_GC:OFF_GC + CONV_WIDTH]
                                * u_ref[pl.ds(rp, SUB), OFF_V:OFF_V + CONV_WIDTH])
            gb, gc, v, cv, cv_m1, cv_m2, cq = _conv3_chunk(u_ref, r, cv_prev, cw, row_c)
            y_c = gb * cq
            rc = _rms(y_c)
            yhat = y_c * rc
            dyn = dy_s[pl.ds(r, SUB), 0:CONV_WIDTH]
            acc[ACC_GNC, :, 0:CONV_WIDTH] += dyn * yhat
            dy_c = _rms_bwd(dyn, yhat, rc, g_c)
            dcq = dy_c * gb
            dcv = (cw[2:3, :] * dcq + cw[1:2, :] * _up(dcq, dcq_later, 1, row_c)
                   + cw[0:1, :] * _up(dcq, dcq_later, 2, row_c))
            acc[ACC_CW + 2, :, 0:CONV_WIDTH] += dcq * cv
            acc[ACC_CW + 1, :, 0:CONV_WIDTH] += dcq * cv_m1
            acc[ACC_CW + 0, :, 0:CONV_WIDTH] += dcq * cv_m2
            du_s[pl.ds(r, SUB), OFF_GB:OFF_GB + CONV_WIDTH] = dy_c * cq
            du_s[pl.ds(r, SUB), OFF_GC:OFF_GC + CONV_WIDTH] = dcv * v
            du_s[pl.ds(r, SUB), OFF_V:OFF_V + CONV_WIDTH] = dcv * gc

            xin_prev = jnp.where(i == 0, xin_before, u_ref[pl.ds(rp, SUB), OFF_XR:OFF_XR + LRU_WIDTH])
            xin, m1, m2, m3, _ = _conv4_chunk(u_ref, r, xin_prev, rw, rb, row_r)
            dxr = dxr_s[pl.ds(r, SUB), :]
            du_s[pl.ds(r, SUB), OFF_XR:OFF_XR + LRU_WIDTH] = (
                rw[3:4, :] * dxr + rw[2:3, :] * _up(dxr, dxr_later, 1, row_r)
                + rw[1:2, :] * _up(dxr, dxr_later, 2, row_r) + rw[0:1, :] * _up(dxr, dxr_later, 3, row_r))
            acc[ACC_RW + 3] += dxr * xin
            acc[ACC_RW + 2] += dxr * m1
            acc[ACC_RW + 1] += dxr * m2
            acc[ACC_RW + 0] += dxr * m3
            acc[ACC_BR] += dxr
            return dcq, dxr

        dcq_first, dxr_first = _chunk_loop(n_chunks, convs_bwd, (dcq_car[...], dxr_car[...]))
        dcq_car[...] = dcq_first
        dxr_car[...] = dxr_first

        du_ref[...] = du_s[...].astype(BF16)

        @pl.when(step == n_tiles - 1)
        def _():
            vec_ref[...] = jnp.zeros(vec_ref.shape, F32)
            rows = {ACC_GNC: ROW_GNC, ACC_GNR: ROW_GNR, ACC_BR: ROW_BR, ACC_BA: ROW_BA, ACC_BX: ROW_BX}
            for k in range(3):
                rows[ACC_CW + k] = ROW_CW + k
            for k in range(4):
                rows[ACC_RW + k] = ROW_RW + k
            for slot, out_row in rows.items():
                o = out_row - ROW_GNC
                vec_ref[o:o + 1, :] = jnp.sum(acc[slot], axis=0, keepdims=True)
            lam_v = lam_ref[...]
            dsp = jnp.sum(acc[ACC_SP], axis=0, keepdims=True)
            o = ROW_LAM - ROW_GNC
            vec_ref[o:o + 1, :] = -dsp * LRU_C / (1.0 + jnp.exp(lam_v))
            wab_ref[0:LRU_WIDTH, :] = _fold_heads(dwa_acc[...])
            wab_ref[LRU_WIDTH:2 * LRU_WIDTH, :] = _fold_heads(dwx_acc[...])

    rev = lambda w: pl.BlockSpec((tm, w), lambda s: (n_tiles - 1 - s, 0))
    before = lambda w: pl.BlockSpec((SUB, w), lambda s: (jnp.maximum((n_tiles - 1 - s) * per_tile - 1, 0), 0))
    whole = lambda a: pl.BlockSpec(a.shape, lambda s: (0,) * a.ndim)
    smalls = (conv_w, rnn_conv_w, rnn_conv_b, wa, b_a, wx, b_x, lam, gnc, gnr, w_out)
    full = lambda w: pltpu.VMEM((tm, w), F32)
    return pl.pallas_call(
        body, grid=(n_tiles,),
        in_specs=[rev(IN_COLS), before(IN_COLS), rev(LRU_WIDTH), before(LRU_WIDTH), rev(D_MODEL)]
        + [whole(a) for a in smalls] + [HBM_SPEC] * (n_sums + 1),
        out_specs=[rev(IN_COLS), pl.BlockSpec((16, D_MODEL), lambda s: (0, 0)),
                   pl.BlockSpec((2 * LRU_WIDTH, HEAD_DIM), lambda s: (0, 0))] + [HBM_SPEC] * (n_sums + 1),
        out_shape=[jax.ShapeDtypeStruct((t_len, IN_COLS), BF16), jax.ShapeDtypeStruct((16, D_MODEL), F32),
                   jax.ShapeDtypeStruct((2 * LRU_WIDTH, HEAD_DIM), F32)]
        + [jax.ShapeDtypeStruct(s.shape, BF16) for s in chip_sums]
        + [jax.ShapeDtypeStruct((4,) + g_wout.shape[1:], BF16)],
        scratch_shapes=[full(IN_COLS), full(MIX_WIDTH), full(LRU_WIDTH), full(LRU_WIDTH), full(LRU_WIDTH),
                        full(LRU_WIDTH), full(LRU_WIDTH), full(LRU_WIDTH),
                        pltpu.VMEM((LRU_WIDTH, GROUP), BF16), pltpu.VMEM((LRU_WIDTH, GROUP), BF16),
                        pltpu.VMEM((N_ACC, SUB, LRU_WIDTH), F32),
                        pltpu.VMEM((LRU_WIDTH, GROUP), F32), pltpu.VMEM((LRU_WIDTH, GROUP), F32),
                        pltpu.VMEM((SUB, LRU_WIDTH), F32), pltpu.VMEM((1, LRU_WIDTH), F32),
                        pltpu.VMEM((SUB, CONV_WIDTH), F32), pltpu.VMEM((SUB, LRU_WIDTH), F32)]
        + _exchange_scratch(n_sums, 3) + _exchange_scratch(1, 4),
        compiler_params=_params(("arbitrary",), 56), name="mixer_bwd",
    )(u, u, hs, hs, dx1, *smalls, *chip_sums, g_wout)


def _in_proj_bwd(du, dx1, x, g_mix, win_t, tm):
    t_len = x.shape[0]

    def body(du_ref, dx1_ref, x_ref, g_ref, w_ref, dx_ref, vec_ref):
        @pl.when(pl.program_id(0) == 0)
        def _():
            vec_ref[...] = jnp.zeros(vec_ref.shape, F32)

        dh = jnp.dot(du_ref[...], w_ref[...], preferred_element_type=F32)
        xv = x_ref[...]
        r1 = _rms(xv)
        xh = xv * r1
        vec_ref[0:1, :] += jnp.sum(dh * xh, axis=0, keepdims=True)
        dx_ref[...] = dx1_ref[...] + _rms_bwd(dh, xh, r1, g_ref[...])

    row_tile = lambda w: pl.BlockSpec((tm, w), lambda i: (i, 0))
    return pl.pallas_call(
        body, grid=(t_len // tm,),
        in_specs=[row_tile(IN_COLS), row_tile(D_MODEL), row_tile(D_MODEL), pl.BlockSpec((1, D_MODEL), lambda i: (0, 0)),
                  pl.BlockSpec((IN_COLS, D_MODEL), lambda i: (0, 0))],
        out_specs=[row_tile(D_MODEL), pl.BlockSpec((SUB, D_MODEL), lambda i: (0, 0))],
        out_shape=[jax.ShapeDtypeStruct((t_len, D_MODEL), F32), jax.ShapeDtypeStruct((SUB, D_MODEL), F32)],
        compiler_params=_params(("arbitrary",), 56), name="in_proj_bwd",
    )(du, dx1, x, g_mix, win_t)


def _tn_weight_grad(a, b, tk, name, pair=(), chip=()):
    t_len, m = a.shape
    n = b.shape[1]
    n_steps = t_len // tk
    sent = tuple(pair) + tuple(chip)
    n_sent = len(sent)

    def body(a_ref, b_ref, *rest):
        srcs = rest[0:n_sent]
        o_ref = rest[n_sent]
        dsts = rest[n_sent + 1:2 * n_sent + 1]
        acc = rest[2 * n_sent + 1]
        sems = rest[2 * n_sent + 2:]
        j = pl.program_id(0)
        if pair:
            _host_pair_exchange(j, n_steps, srcs, dsts, *sems)
        if chip:
            _host_chip_exchange(j, n_steps, srcs, dsts, *sems)

        @pl.when(j == 0)
        def _():
            acc[...] = jnp.zeros(acc.shape, F32)

        acc[...] += _dot_tn(a_ref[...].astype(BF16), b_ref[...].astype(BF16))

        @pl.when(j == n_steps - 1)
        def _():
            o_ref[...] = acc[...].astype(BF16)

    landed = [jax.ShapeDtypeStruct((4,) + g.shape[1:], BF16) for g in pair]
    landed += [jax.ShapeDtypeStruct(s.shape, BF16) for s in chip]
    scratch = [pltpu.VMEM((m, n), F32)]
    if n_sent:
        scratch += _exchange_scratch(n_sent, 4 if pair else 3)
    return pl.pallas_call(
        body, grid=(n_steps,),
        in_specs=[pl.BlockSpec((tk, m), lambda j: (j, 0)), pl.BlockSpec((tk, n), lambda j: (j, 0))]
        + [HBM_SPEC] * n_sent,
        out_specs=[pl.BlockSpec((m, n), lambda j: (0, 0))] + [HBM_SPEC] * n_sent,
        out_shape=[jax.ShapeDtypeStruct((m, n), BF16)] + landed,
        scratch_shapes=scratch,
        compiler_params=_params(("arbitrary",), 56), name=name,
    )(a, b, *sent)


def _adamw(w, g, m, v):
    m = ADAM_B1 * m + (1.0 - ADAM_B1) * g
    v = ADAM_B2 * v + (1.0 - ADAM_B2) * (g * g)
    delta = -ADAM_LR * ((m / BC1) / (jnp.sqrt(v / BC2) + ADAM_EPS) + ADAM_WD * w)
    return delta, m, v


def _update_sharded(g, w, m, v, rows_blk, name, landed=None, transposed=False):
    rows, cols = w.shape
    pad_cols = -(-cols // 128) * 128

    def body(g_ref, *rest):
        if landed is not None:
            l_ref, rest = rest[0], rest[1:]
        w_ref, m_ref, v_ref, og, od, om, ov = rest[0:7]
        if transposed:
            padbuf, turned = rest[7:]
            padbuf[...] = jnp.zeros(padbuf.shape, F32)
            padbuf[0:cols, :] = g_ref[...]
            turned[...] = padbuf[...].T
            gv = turned[:, 0:cols]
        else:
            gv = g_ref[...]
        if landed is not None:
            for j in range(3):
                gv = gv + l_ref[j].astype(F32)
        delta, mn, vn = _adamw(w_ref[...], gv, m_ref[...], v_ref[...])
        og[...] = gv
        od[...] = delta
        om[...] = mn
        ov[...] = vn

    blk = pl.BlockSpec((rows_blk, cols), lambda i: (i, 0))
    g_spec = pl.BlockSpec((cols, rows_blk), lambda i: (0, i)) if transposed else blk
    extra_specs = [] if landed is None else [pl.BlockSpec((3, rows_blk, cols), lambda i: (0, i, 0))]
    extra_args = [] if landed is None else [landed]
    shape = jax.ShapeDtypeStruct((rows, cols), F32)
    return pl.pallas_call(
        body, grid=(rows // rows_blk,), in_specs=[g_spec] + extra_specs + [blk, blk, blk], out_specs=[blk] * 4,
        out_shape=[shape] * 4,
        scratch_shapes=[pltpu.VMEM((pad_cols, rows_blk), F32), pltpu.VMEM((rows_blk, pad_cols), F32)] if transposed else [],
        compiler_params=_params(("arbitrary",), 32), name=name,
    )(g, *extra_args, w, m, v)


def _update_small(vsum, wsum, g_cw, g_rw, weights, moments_m, moments_v):
    n = len(weights)

    def body(*refs):
        vs, ws, gcw, grw = refs[0:4]
        w_refs = refs[4:4 + n]
        m_refs = refs[4 + n:4 + 2 * n]
        v_refs = refs[4 + 2 * n:4 + 3 * n]
        outs = refs[4 + 3 * n:]
        loss_ref = outs[0]
        loss_ref[...] = jnp.sum(vs[ROW_LOSS:ROW_LOSS + 1, :], axis=1, keepdims=True)
        grads = [
            vs[ROW_GMIX:ROW_GMIX + 1, :], gcw[...], grw[...], vs[ROW_BR:ROW_BR + 1, :],
            ws[0:LRU_WIDTH, :], vs[ROW_BA:ROW_BA + 1, :], ws[LRU_WIDTH:2 * LRU_WIDTH, :], vs[ROW_BX:ROW_BX + 1, :],
            vs[ROW_LAM:ROW_LAM + 1, :], vs[ROW_GNC:ROW_GNC + 1, 0:CONV_WIDTH], vs[ROW_GNR:ROW_GNR + 1, :],
            vs[ROW_GMLP:ROW_GMLP + 1, :], vs[ROW_GF:ROW_GF + 1, :],
        ]
        for k in range(n):
            gk = grads[k]
            delta, mn, vn = _adamw(w_refs[k][...], gk, m_refs[k][...], v_refs[k][...])
            outs[1 + 4 * k][...] = gk
            outs[2 + 4 * k][...] = delta
            outs[3 + 4 * k][...] = mn
            outs[4 + 4 * k][...] = vn

    vm = pl.BlockSpec(memory_space=pltpu.VMEM)
    out_shape = [jax.ShapeDtypeStruct((1, 1), F32)]
    for w in weights:
        out_shape += [jax.ShapeDtypeStruct(w.shape, F32)] * 4
    args = (vsum, wsum, g_cw, g_rw, *weights, *moments_m, *moments_v)
    return pl.pallas_call(
        body, out_shape=out_shape, in_specs=[vm] * len(args), out_specs=[vm] * len(out_shape),
        compiler_params=_params(vmem_mib=32), name="update_small",
    )(*args)


def kernel(x, norm_mix_g, w_in, conv_w, rnn_conv_w, rnn_conv_b, w_a, b_a, w_x, b_x, lru_lambda, g_norm_conv, g_norm_rnn, w_out, norm_mlp_g, w_mlp_in, w_mlp_out, final_norm_g, loss_target, m_norm_mix_g, m_w_in, m_conv_w, m_rnn_conv_w, m_rnn_conv_b, m_w_a, m_b_a, m_w_x, m_b_x, m_lru_lambda, m_g_norm_conv, m_g_norm_rnn, m_w_out, m_norm_mlp_g, m_w_mlp_in, m_w_mlp_out, m_final_norm_g, v_norm_mix_g, v_w_in, v_conv_w, v_rnn_conv_w, v_rnn_conv_b, v_w_a, v_b_a, v_w_x, v_b_x, v_lru_lambda, v_g_norm_conv, v_g_norm_rnn, v_w_out, v_norm_mlp_g, v_w_mlp_in, v_w_mlp_out, v_final_norm_g):
    t_len = x.shape[1]
    my_id = 4 * lax.axis_index("x") + 2 * lax.axis_index("y") + lax.axis_index("c")
    tm = min(256, t_len)
    tk = min(512, t_len)

    xs = x.reshape(t_len, D_MODEL)
    tgt = loss_target.reshape(t_len, D_MODEL)
    flat = lambda a: a.reshape(a.shape[-2:]) if a.ndim == 3 else a.reshape(1, -1)
    heads = lambda a: a.reshape(LRU_WIDTH, HEAD_DIM)

    win_blk, cpack, wout_shard, w1_shard, w2_shard = _all_gather_w_in(
        flat(w_in), flat(w_out), flat(w_mlp_in), flat(w_mlp_out), flat(conv_w), flat(rnn_conv_w))
    win_t = win_blk.reshape(IN_COLS, D_MODEL)
    conv_full = jnp.transpose(cpack[:, 0:3, 0:64], (1, 0, 2)).reshape(3, CONV_WIDTH)
    rnn_full = jnp.transpose(cpack[:, 3:7, :], (1, 0, 2)).reshape(4, LRU_WIDTH)
    mixer_small = (conv_full, rnn_full, flat(rnn_conv_b), heads(w_a), flat(b_a), heads(w_x), flat(b_x),
                   flat(lru_lambda), flat(g_norm_conv), flat(g_norm_rnn))

    u, h, wout_blk = _in_proj(xs, flat(norm_mix_g), win_t, wout_shard, tm)
    wout_f = wout_blk.reshape(MIX_WIDTH, D_MODEL)
    hs, y, w1_blk, w2_blk = _mixer_fwd(u, *mixer_small, w1_shard, w2_shard, tm)
    dx1, z, dpre, h2, dx2, vec_m = _mlp_fwd_bwd(xs, y, tgt, flat(norm_mlp_g), flat(final_norm_g), wout_f, w1_blk,
                                                w2_blk.reshape(D_FF, D_MODEL), tm)
    g_w1, g_w2 = _mlp_weight_grads(h2, dpre, z, dx2, tk)
    g_wout, sib_w1, sib_w2 = _tn_weight_grad(y, dx1, tk, "w_out_grad", pair=(g_w1, g_w2))
    g_wout = g_wout.reshape(N_DEV, MIX_WIDTH // N_DEV, D_MODEL)
    hsend_w1, own_w1 = _pair_sum(g_w1, sib_w1, "pair_sum_w_mlp_in")
    hsend_w2, own_w2 = _pair_sum(g_w2, sib_w2, "pair_sum_w_mlp_out")
    du, vec_b, wab, landed_w1, landed_w2, sib_wout = _mixer_bwd(
        u, hs, dx1, *mixer_small, wout_f, (hsend_w1, hsend_w2), g_wout, tm)
    hsend_wout, own_wout = _pair_sum(g_wout, sib_wout, "pair_sum_w_out")
    grad_x, vec_x = _in_proj_bwd(du, dx1, xs, flat(norm_mix_g), win_t, tm)
    g_win_t, landed_wout = _tn_weight_grad(du, h, tk, "w_in_grad", chip=(hsend_wout,))

    r_win_t = _reduce_scatter(g_win_t.reshape(N_DEV, IN_COLS // N_DEV, D_MODEL), "reduce_scatter_w_in")
    vsum, wsum = _all_reduce_small(vec_m, vec_b, vec_x, wab)

    up_win = _update_sharded(r_win_t, flat(w_in), flat(m_w_in), flat(v_w_in), 256, "update_w_in", transposed=True)
    up_wout = _update_sharded(own_wout, flat(w_out), flat(m_w_out), flat(v_w_out), 96, "update_w_out",
                              landed=landed_wout)
    up_w1 = _update_sharded(own_w1, flat(w_mlp_in), flat(m_w_mlp_in), flat(v_w_mlp_in), 256, "update_w_mlp_in",
                            landed=landed_w1)
    up_w2 = _update_sharded(own_w2, flat(w_mlp_out), flat(m_w_mlp_out), flat(v_w_mlp_out), 256, "update_w_mlp_out",
                            landed=landed_w2)

    g_cw = lax.dynamic_slice(vsum, (ROW_CW, 64 * my_id), (3, 64))
    g_rw = lax.dynamic_slice(vsum, (ROW_RW, 128 * my_id), (4, 128))
    small_w = (norm_mix_g, conv_w, rnn_conv_w, rnn_conv_b, w_a, b_a, w_x, b_x, lru_lambda, g_norm_conv, g_norm_rnn,
               norm_mlp_g, final_norm_g)
    small_m = (m_norm_mix_g, m_conv_w, m_rnn_conv_w, m_rnn_conv_b, m_w_a, m_b_a, m_w_x, m_b_x, m_lru_lambda,
               m_g_norm_conv, m_g_norm_rnn, m_norm_mlp_g, m_final_norm_g)
    small_v = (v_norm_mix_g, v_conv_w, v_rnn_conv_w, v_rnn_conv_b, v_w_a, v_b_a, v_w_x, v_b_x, v_lru_lambda,
               v_g_norm_conv, v_g_norm_rnn, v_norm_mlp_g, v_final_norm_g)
    is_heads = (False, False, False, False, True, False, True, False, False, False, False, False, False)
    as2d = lambda arrs: [heads(a) if hd else flat(a) for a, hd in zip(arrs, is_heads)]
    small_out = _update_small(vsum, wsum, g_cw, g_rw, as2d(small_w), as2d(small_m), as2d(small_v))
    loss = small_out[0].reshape(())

    names = ["norm_mix_g", "w_in", "conv_w", "rnn_conv_w", "rnn_conv_b", "w_a", "b_a", "w_x", "b_x", "lru_lambda",
             "g_norm_conv", "g_norm_rnn", "w_out", "norm_mlp_g", "w_mlp_in", "w_mlp_out", "final_norm_g"]
    originals = dict(zip(names, (norm_mix_g, w_in, conv_w, rnn_conv_w, rnn_conv_b, w_a, b_a, w_x, b_x, lru_lambda,
                                 g_norm_conv, g_norm_rnn, w_out, norm_mlp_g, w_mlp_in, w_mlp_out, final_norm_g)))
    results = {"w_in": up_win, "w_out": up_wout, "w_mlp_in": up_w1, "w_mlp_out": up_w2}
    small_names = ["norm_mix_g", "conv_w", "rnn_conv_w", "rnn_conv_b", "w_a", "b_a", "w_x", "b_x", "lru_lambda",
                   "g_norm_conv", "g_norm_rnn", "norm_mlp_g", "final_norm_g"]
    for k, nm in enumerate(small_names):
        results[nm] = small_out[1 + 4 * k:5 + 4 * k]
    out = [loss, grad_x.reshape(x.shape)]
    for kind in range(4):
        out += [results[nm][kind].reshape(originals[nm].shape) for nm in names]
    return tuple(out)
```

```python
import functools

import jax
import jax.numpy as jnp
from jax import lax
from jax.experimental import pallas as pl
from jax.experimental.pallas import tpu as pltpu

F32 = jnp.float32
BF16 = jnp.bfloat16

D_MODEL = 1024
HEAD_DIM = 64
CONV_WIDTH = 512
LRU_WIDTH = 1024
MIX_WIDTH = CONV_WIDTH + LRU_WIDTH
IN_COLS = 3 * CONV_WIDTH + 2 * LRU_WIDTH
D_FF = 4 * D_MODEL
GROUP = 256
EPS = 1e-6
LRU_C = 8.0
N_DEV = 8
SUB = 8

OFF_GB, OFF_GC, OFF_V, OFF_XR, OFF_G = 0, 512, 1024, 1536, 2560

ADAM_LR, ADAM_B1, ADAM_B2, ADAM_EPS, ADAM_WD, ADAM_STEP = 0.001, 0.9, 0.999, 1e-08, 0.01, 10
BC1 = 1.0 - ADAM_B1 ** ADAM_STEP
BC2 = 1.0 - ADAM_B2 ** ADAM_STEP

MIB = 1024 * 1024
MESH = pl.DeviceIdType.MESH

VEC_ROWS = 32
ROW_GF, ROW_GMLP, ROW_LOSS = 0, 1, 2
ROW_GNC, ROW_GNR, ROW_BR, ROW_BA, ROW_BX, ROW_LAM, ROW_CW, ROW_RW = 8, 9, 10, 11, 12, 13, 14, 17
ROW_GMIX = 24
ACC_GNC, ACC_GNR, ACC_BR, ACC_BA, ACC_BX, ACC_SP, ACC_CW, ACC_RW, N_ACC = 0, 1, 2, 3, 4, 5, 6, 9, 13


def _params(semantics=None, vmem_mib=48):
    return pltpu.CompilerParams(dimension_semantics=semantics, vmem_limit_bytes=vmem_mib * MIB)


def _rms(x):
    return lax.rsqrt(jnp.mean(x * x, axis=-1, keepdims=True) + EPS)


def _rms_bwd(dy, xhat, r, g):
    dyh = dy * g
    return r * (dyh - xhat * jnp.mean(dyh * xhat, axis=-1, keepdims=True))


def _sigmoid(x):
    return 0.5 + 0.5 * jnp.tanh(0.5 * x)


def _gelu(x):
    c0, c1 = 0.7978845608028654, 0.044715
    t = jnp.tanh(c0 * (x + c1 * x * x * x))
    ge = 0.5 * x * (1.0 + t)
    dge = 0.5 * (1.0 + t) + 0.5 * x * (1.0 - t * t) * c0 * (1.0 + 3.0 * c1 * x * x)
    return ge, dge


def _softplus_neg(lam):
    z = -lam
    e = jnp.exp(-jnp.abs(z))
    return jnp.maximum(z, 0.0) + jnp.where(e < 1e-4, e * (1.0 - 0.5 * e), jnp.log(1.0 + e))


def _lru_gates(pa, px, sp_c):
    ra = _sigmoid(pa)
    ii = _sigmoid(px)
    la = -ra * sp_c
    a = jnp.exp(la)
    x2 = 2.0 * la
    series = -x2 * (1.0 + x2 * (0.5 + x2 * (1.0 / 6.0 + x2 * (1.0 / 24.0))))
    m2 = jnp.where(x2 > -0.01, series, 1.0 - a * a)
    inv_mult = lax.rsqrt(m2)
    mult = jnp.where(m2 > 0.0, m2 * inv_mult, 0.0)
    return ra, ii, a, mult, inv_mult


def _down(cur, prev, s, row):
    return jnp.where(row >= s, pltpu.roll(cur, s, 0), pltpu.roll(prev, s, 0))


def _up(cur, nxt, s, row):
    return jnp.where(row < SUB - s, pltpu.roll(cur, SUB - s, 0), pltpu.roll(nxt, SUB - s, 0))


def _scan8_fwd(a, b, row):
    for s in (1, 2, 4):
        m = row >= s
        a_sh = pltpu.roll(a, s, 0)
        b_sh = pltpu.roll(b, s, 0)
        b = jnp.where(m, a * b_sh + b, b)
        a = jnp.where(m, a * a_sh, a)
    return a, b


def _scan8_rev(a, b, row):
    for s in (1, 2, 4):
        m = row < SUB - s
        a_sh = pltpu.roll(a, SUB - s, 0)
        b_sh = pltpu.roll(b, SUB - s, 0)
        b = jnp.where(m, a * b_sh + b, b)
        a = jnp.where(m, a * a_sh, a)
    return a, b


def _group_mask(shape):
    r = lax.broadcasted_iota(jnp.int32, shape, 0)
    c = lax.broadcasted_iota(jnp.int32, shape, 1)
    return ((r % GROUP) // HEAD_DIM) == (c // HEAD_DIM)


def _expand_heads(w):
    j = lax.broadcasted_iota(jnp.int32, (HEAD_DIM, GROUP), 0)
    c = lax.broadcasted_iota(jnp.int32, (HEAD_DIM, GROUP), 1)
    spread = (c % HEAD_DIM == j).astype(BF16)
    e = jnp.dot(w.astype(BF16), spread, preferred_element_type=F32)
    return jnp.where(_group_mask(e.shape), e, 0.0).astype(BF16)


def _fold_heads(p):
    p = jnp.where(_group_mask(p.shape), p, 0.0)
    c = lax.broadcasted_iota(jnp.int32, (GROUP, HEAD_DIM), 0)
    j = lax.broadcasted_iota(jnp.int32, (GROUP, HEAD_DIM), 1)
    fold = (c % HEAD_DIM == j).astype(BF16)
    hi = p.astype(BF16)
    rest = p - hi.astype(F32)
    mid = rest.astype(BF16)
    lo = (rest - mid.astype(F32)).astype(BF16)
    dot = functools.partial(jnp.dot, preferred_element_type=F32)
    return dot(hi, fold) + dot(mid, fold) + dot(lo, fold)


def _block_diag_apply(xb, wbd_ref):
    parts = [jnp.dot(xb[:, g * GROUP:(g + 1) * GROUP], wbd_ref[g * GROUP:(g + 1) * GROUP, :],
                     preferred_element_type=F32) for g in range(LRU_WIDTH // GROUP)]
    return jnp.concatenate(parts, axis=1)


def _block_diag_apply_t(db, wbd_ref):
    parts = [lax.dot_general(db[:, g * GROUP:(g + 1) * GROUP], wbd_ref[g * GROUP:(g + 1) * GROUP, :],
                             (((1,), (1,)), ((), ())), preferred_element_type=F32)
             for g in range(LRU_WIDTH // GROUP)]
    return jnp.concatenate(parts, axis=1)


def _dot_nt(a, b):
    return lax.dot_general(a, b, (((1,), (1,)), ((), ())), preferred_element_type=F32)


def _dot_tn(a, b):
    return lax.dot_general(a, b, (((0,), (0,)), ((), ())), preferred_element_type=F32)


CHUNKS_IN_FLIGHT = 4


def _chunk_loop(n_chunks, chunk, init):
    def body(k, carry):
        for j in range(CHUNKS_IN_FLIGHT):
            carry = chunk(k * CHUNKS_IN_FLIGHT + j, carry)
        return carry

    return lax.fori_loop(0, n_chunks // CHUNKS_IN_FLIGHT, body, init)


def _place():
    x, y, c = lax.axis_index("x"), lax.axis_index("y"), lax.axis_index("c")
    return x, y, c


def _block_id(chip, core):
    return 4 * chip[0] + 2 * chip[1] + core


def _other_chips(x, y):
    return [(1 - x, y), (x, 1 - y), (1 - x, 1 - y)]


def _remote_copy(src, dst, send_sem, recv_sem, to):
    return pltpu.make_async_remote_copy(src_ref=src, dst_ref=dst, send_sem=send_sem, recv_sem=recv_sem,
                                        device_id=to, device_id_type=MESH)


HBM_SPEC = pl.BlockSpec(memory_space=pl.ANY)


def _all_gather_w_in(w_in, w_out, w_mlp_in, w_mlp_out, conv_w, rnn_conv_w):
    n_in = w_in.shape[1]
    n_arr = 2

    def body(win_ref, wout_ref, w1_ref, w2_ref, cw_ref, rw_ref,
             o_win, o_cp, o_wout, o_w1, o_w2, padbuf, send_sems, recv_sems):
        x, y, c = _place()
        me = (x, y, c)
        my_id = _block_id((x, y), c)
        sibling = (x, y, 1 - c)
        chips = _other_chips(x, y)
        outs = [o_win, o_cp]

        padbuf[...] = jnp.zeros(padbuf.shape, F32)
        padbuf[:, 0:n_in] = win_ref[...]
        o_win[my_id] = padbuf[...].T[0:n_in, :].astype(BF16)
        o_cp[my_id] = jnp.zeros(o_cp.shape[1:], F32)
        o_cp[my_id, 0:3, 0:64] = cw_ref[...]
        o_cp[my_id, 3:7, :] = rw_ref[...]

        def copy(arr, k, block, to):
            blk = outs[arr].at[block]
            return _remote_copy(blk, blk, send_sems.at[arr, k], recv_sems.at[arr, k], to)

        first = []
        for arr in range(n_arr):
            first.append(copy(arr, 0, my_id, sibling))
            first += [copy(arr, 1 + j, my_id, (*chip, c)) for j, chip in enumerate(chips)]
        for cp in first:
            cp.start()
        o_wout[...] = wout_ref[...].astype(BF16)
        o_w1[...] = w1_ref[...].astype(BF16)
        o_w2[...] = w2_ref[...].astype(BF16)
        passed = []
        for j, chip in enumerate(chips):
            for arr in range(n_arr):
                copy(arr, 1 + j, _block_id(chip, c), me).wait_recv()
                fwd = copy(arr, 4 + j, _block_id(chip, c), sibling)
                fwd.start()
                passed.append(fwd)
        for arr in range(n_arr):
            copy(arr, 0, _block_id((x, y), 1 - c), me).wait_recv()
            for j, chip in enumerate(chips):
                copy(arr, 4 + j, _block_id(chip, 1 - c), me).wait_recv()
        for cp in first + passed:
            cp.wait_send()

    vm = pl.BlockSpec(memory_space=pltpu.VMEM)
    shapes = (
        jax.ShapeDtypeStruct((N_DEV, n_in, D_MODEL), BF16),
        jax.ShapeDtypeStruct((N_DEV, 8, 128), F32),
        jax.ShapeDtypeStruct(w_out.shape, BF16),
        jax.ShapeDtypeStruct(w_mlp_in.shape, BF16),
        jax.ShapeDtypeStruct(w_mlp_out.shape, BF16),
    )
    return pl.pallas_call(
        body, out_shape=shapes, in_specs=[vm] * 6, out_specs=[vm] * 5,
        scratch_shapes=[pltpu.VMEM((D_MODEL, 512), F32),
                        pltpu.SemaphoreType.DMA((n_arr, 7)), pltpu.SemaphoreType.DMA((n_arr, 7))],
        compiler_params=_params(vmem_mib=40), name="all_gather_w_in",
    )(w_in, w_out, w_mlp_in, w_mlp_out, conv_w, rnn_conv_w)


def _host_all_gather(step, n_steps, shards, fulls, send_sems, recv_sems, local_sems):
    x, y, c = _place()
    me = (x, y, c)
    my_id = _block_id((x, y), c)
    sibling = (x, y, 1 - c)
    chips = _other_chips(x, y)
    n_arr = len(shards)

    def copy(arr, k, block, to, src=None):
        dst = fulls[arr].at[block]
        return _remote_copy(dst if src is None else src, dst, send_sems.at[arr, k], recv_sems.at[arr, k], to)

    def local(arr):
        return pltpu.make_async_copy(shards[arr], fulls[arr].at[my_id], local_sems.at[arr])

    @pl.when(step == 0)
    def _():
        for arr in range(n_arr):
            local(arr).start()
            copy(arr, 0, my_id, sibling, shards[arr]).start()
            for j, chip in enumerate(chips):
                copy(arr, 1 + j, my_id, (*chip, c), shards[arr]).start()

    @pl.when(step == max(n_steps - 2, 0))
    def _():
        for j, chip in enumerate(chips):
            for arr in range(n_arr):
                copy(arr, 1 + j, _block_id(chip, c), me).wait_recv()
                copy(arr, 4 + j, _block_id(chip, c), sibling).start()

    @pl.when(step == n_steps - 1)
    def _():
        for arr in range(n_arr):
            copy(arr, 0, _block_id((x, y), 1 - c), me).wait_recv()
            for j, chip in enumerate(chips):
                copy(arr, 4 + j, _block_id(chip, 1 - c), me).wait_recv()
            for k in range(4):
                copy(arr, k, my_id, me, shards[arr]).wait_send()
            for j, chip in enumerate(chips):
                copy(arr, 4 + j, _block_id(chip, c), me).wait_send()
            local(arr).wait()


def _host_pair_exchange(step, n_steps, gs, sibs, send_sems, recv_sems):
    x, y, c = _place()
    sibling = (x, y, 1 - c)
    chips = [(x, y)] + _other_chips(x, y)

    def d2d(arr, q):
        return _remote_copy(gs[arr].at[_block_id(chips[q], 1 - c)], sibs[arr].at[q],
                            send_sems.at[arr, q], recv_sems.at[arr, q], sibling)

    @pl.when(step == 0)
    def _():
        for arr in range(len(gs)):
            for q in (1, 2, 3, 0):
                d2d(arr, q).start()

    @pl.when(step == n_steps - 1)
    def _():
        for arr in range(len(gs)):
            for q in range(4):
                d2d(arr, q).wait()


def _host_chip_exchange(step, n_steps, hsends, hrecvs, send_sems, recv_sems):
    x, y, c = _place()
    chips = _other_chips(x, y)

    def ici(arr, j):
        return _remote_copy(hsends[arr].at[j], hrecvs[arr].at[j], send_sems.at[arr, j], recv_sems.at[arr, j],
                            (*chips[j], c))

    @pl.when(step == 0)
    def _():
        for arr in range(len(hsends)):
            for j in range(3):
                ici(arr, j).start()

    @pl.when(step == n_steps - 1)
    def _():
        for arr in range(len(hsends)):
            for j in range(3):
                ici(arr, j).wait()


def _pair_sum(g, sib, name):
    _, rows, cols = g.shape
    x, y, c = _place()
    slots = jnp.stack([_block_id(chip, c) for chip in [(x, y)] + _other_chips(x, y)]).astype(jnp.int32)

    def body(slots_ref, g_ref, sib_ref, hs_ref, own_ref):
        q = pl.program_id(0)
        both = g_ref[0].astype(F32) + sib_ref[0].astype(F32)

        @pl.when(q == 0)
        def _():
            own_ref[...] = both

        @pl.when(q > 0)
        def _():
            hs_ref[0] = both.astype(BF16)

    block = (1, rows, cols)
    grid_spec = pltpu.PrefetchScalarGridSpec(
        num_scalar_prefetch=1, grid=(4,),
        in_specs=[pl.BlockSpec(block, lambda q, s: (s[q], 0, 0)), pl.BlockSpec(block, lambda q, s: (q, 0, 0))],
        out_specs=[pl.BlockSpec(block, lambda q, s: (jnp.maximum(q - 1, 0), 0, 0)),
                   pl.BlockSpec((rows, cols), lambda q, s: (0, 0))])
    return pl.pallas_call(
        body, grid_spec=grid_spec,
        out_shape=(jax.ShapeDtypeStruct((3, rows, cols), BF16), jax.ShapeDtypeStruct((rows, cols), F32)),
        compiler_params=_params(("arbitrary",), 32), name=name,
    )(slots, g, sib)


def _exchange_scratch(n_arr, n_copies):
    return [pltpu.SemaphoreType.DMA((n_arr, n_copies)), pltpu.SemaphoreType.DMA((n_arr, n_copies))]


def _final_exchange(g, vec_m, vec_b, vec_x, wab):
    _, rows, cols = g.shape
    wrows = wab.shape[0] // N_DEV

    def body(g_ref, vm_ref, vb_ref, vx_ref, w_ref, o_ref, o_vec, o_w,
             sib, hsend, hrecv, vpack, vrecv, wrecv, wred,
             d_send, d_recv, i_send, i_recv, v_send, v_recv, w_send, w_recv, b_send, b_recv):
        x, y, c = _place()
        my_id = _block_id((x, y), c)
        sibling = (x, y, 1 - c)
        chips = [(x, y)] + _other_chips(x, y)

        def d2d(q):
            return _remote_copy(g_ref.at[_block_id(chips[q], 1 - c)], sib.at[q], d_send.at[q], d_recv.at[q], sibling)

        def ici(q):
            return _remote_copy(hsend.at[q - 1], hrecv.at[q - 1], i_send.at[q - 1], i_recv.at[q - 1], (*chips[q], c))

        def peer(k):
            return (x ^ ((k >> 2) & 1), y ^ ((k >> 1) & 1), c ^ (k & 1))

        def pid(k):
            p = peer(k)
            return 4 * p[0] + 2 * p[1] + p[2]

        vpack[0:8, :] = vm_ref[...]
        vpack[8:24, :] = vb_ref[...]
        vpack[24:32, :] = vx_ref[...]
        vrecv[my_id] = vpack[...]

        def vcopy(k):
            return pltpu.make_async_remote_copy(
                src_ref=vpack, dst_ref=vrecv.at[my_id], send_sem=v_send.at[k], recv_sem=v_recv.at[k],
                device_id=peer(k), device_id_type=MESH)

        def wcopy(k):
            return pltpu.make_async_remote_copy(
                src_ref=w_ref.at[pl.ds(pl.multiple_of(pid(k) * wrows, SUB), wrows), :], dst_ref=wrecv.at[k],
                send_sem=w_send.at[k], recv_sem=w_recv.at[k], device_id=peer(k), device_id_type=MESH)

        def bcopy(k):
            mine = o_w.at[pl.ds(pl.multiple_of(my_id * wrows, SUB), wrows), :]
            return pltpu.make_async_remote_copy(
                src_ref=wred, dst_ref=mine, send_sem=b_send.at[k], recv_sem=b_recv.at[k],
                device_id=peer(k), device_id_type=MESH)

        for q in (1, 2, 3, 0):
            d2d(q).start()
        for k in range(1, N_DEV):
            vcopy(k).start()
            wcopy(k).start()
        for q in (1, 2, 3):
            d2d(q).wait_recv()
            hsend[q - 1] = (g_ref[_block_id(chips[q], c)].astype(F32) + sib[q].astype(F32)).astype(BF16)
            ici(q).start()
        red = w_ref[pl.ds(pl.multiple_of(my_id * wrows, SUB), wrows), :]
        for k in range(1, N_DEV):
            wcopy(k).wait_recv()
            red = red + wrecv[k]
        wred[...] = red
        o_w[pl.ds(pl.multiple_of(my_id * wrows, SUB), wrows), :] = red
        for k in range(1, N_DEV):
            bcopy(k).start()
        d2d(0).wait_recv()
        acc = g_ref[_block_id(chips[0], c)].astype(F32) + sib[0].astype(F32)
        for k in range(1, N_DEV):
            vcopy(k).wait_recv()
        tot = vrecv[0]
        for s in range(1, N_DEV):
            tot = tot + vrecv[s]
        o_vec[...] = tot
        for q in (1, 2, 3):
            ici(q).wait_recv()
            acc = acc + hrecv[q - 1].astype(F32)
        o_ref[...] = acc
        for k in range(1, N_DEV):
            bcopy(k).wait_recv()
        for k in range(1, N_DEV):
            vcopy(k).wait_send()
            wcopy(k).wait_send()
            bcopy(k).wait_send()
        for q in range(4):
            d2d(q).wait_send()
        for q in (1, 2, 3):
            ici(q).wait_send()

    vm = pl.BlockSpec(memory_space=pltpu.VMEM)
    dma8 = pltpu.SemaphoreType.DMA((N_DEV,))
    dma4 = pltpu.SemaphoreType.DMA((4,))
    dma3 = pltpu.SemaphoreType.DMA((3,))
    return pl.pallas_call(
        body, out_shape=(jax.ShapeDtypeStruct((rows, cols), F32), jax.ShapeDtypeStruct((VEC_ROWS, D_MODEL), F32),
                         jax.ShapeDtypeStruct(wab.shape, F32)),
        in_specs=[vm] * 5, out_specs=[vm] * 3,
        scratch_shapes=[pltpu.VMEM((4, rows, cols), BF16), pltpu.VMEM((3, rows, cols), BF16),
                        pltpu.VMEM((3, rows, cols), BF16),
                        pltpu.VMEM((VEC_ROWS, D_MODEL), F32), pltpu.VMEM((N_DEV, VEC_ROWS, D_MODEL), F32),
                        pltpu.VMEM((N_DEV, wrows, HEAD_DIM), F32), pltpu.VMEM((wrows, HEAD_DIM), F32),
                        dma4, dma4, dma3, dma3, dma8, dma8, dma8, dma8, dma8, dma8],
        compiler_params=_params(vmem_mib=48), name="final_exchange",
    )(g, vec_m, vec_b, vec_x, wab)


def _in_proj(x, g_mix, win_t, wout_shard, tm):
    t_len = x.shape[0]
    n_steps = t_len // tm

    def body(x_ref, g_ref, w_ref, wout_ref, u_ref, h_ref, wout_full, send_sems, recv_sems, local_sems):
        _host_all_gather(pl.program_id(0), n_steps, [wout_ref], [wout_full], send_sems, recv_sems, local_sems)
        xv = x_ref[...]
        h = (xv * _rms(xv) * g_ref[...]).astype(BF16)
        h_ref[...] = h
        u_ref[...] = _dot_nt(h, w_ref[...])

    return pl.pallas_call(
        body, grid=(n_steps,),
        in_specs=[pl.BlockSpec((tm, D_MODEL), lambda i: (i, 0)), pl.BlockSpec((1, D_MODEL), lambda i: (0, 0)),
                  pl.BlockSpec((IN_COLS, D_MODEL), lambda i: (0, 0)), HBM_SPEC],
        out_specs=[pl.BlockSpec((tm, IN_COLS), lambda i: (i, 0)), pl.BlockSpec((tm, D_MODEL), lambda i: (i, 0)),
                   HBM_SPEC],
        out_shape=[jax.ShapeDtypeStruct((t_len, IN_COLS), F32), jax.ShapeDtypeStruct((t_len, D_MODEL), BF16),
                   jax.ShapeDtypeStruct((N_DEV,) + wout_shard.shape, BF16)],
        scratch_shapes=_exchange_scratch(1, 7) + [pltpu.SemaphoreType.DMA((1,))],
        compiler_params=_params(("arbitrary",), 56), name="in_proj",
    )(x, g_mix, win_t, wout_shard)


def _conv3_chunk(u_ref, r, cv_prev, cw, row):
    gb = u_ref[pl.ds(r, SUB), OFF_GB:OFF_GB + CONV_WIDTH]
    gc = u_ref[pl.ds(r, SUB), OFF_GC:OFF_GC + CONV_WIDTH]
    v = u_ref[pl.ds(r, SUB), OFF_V:OFF_V + CONV_WIDTH]
    cv = gc * v
    cv_m1 = _down(cv, cv_prev, 1, row)
    cv_m2 = _down(cv, cv_prev, 2, row)
    cq = cw[2:3, :] * cv + cw[1:2, :] * cv_m1 + cw[0:1, :] * cv_m2
    return gb, gc, v, cv, cv_m1, cv_m2, cq


def _conv4_chunk(u_ref, r, xin_prev, rw, rb, row):
    xin = u_ref[pl.ds(r, SUB), OFF_XR:OFF_XR + LRU_WIDTH]
    m1 = _down(xin, xin_prev, 1, row)
    m2 = _down(xin, xin_prev, 2, row)
    m3 = _down(xin, xin_prev, 3, row)
    xr = rw[3:4, :] * xin + rw[2:3, :] * m1 + rw[1:2, :] * m2 + rw[0:1, :] * m3 + rb
    return xin, m1, m2, m3, xr


def _mixer_fwd(u, conv_w, rnn_conv_w, rnn_conv_b, wa, b_a, wx, b_x, lam, gnc, gnr, w1_shard, w2_shard, tm):
    t_len = u.shape[0]
    n_steps = t_len // tm
    n_chunks = tm // SUB

    def body(u_ref, cw_ref, rw_ref, rb_ref, wa_ref, ba_ref, wx_ref, bx_ref, lam_ref, gnc_ref, gnr_ref,
             w1_shard, w2_shard, hs_ref, y_ref, w1_full, w2_full,
             y_s, xr_s, pa_s, px_s, wabd, wxbd, cv_car, xin_car, h_car, send_sems, recv_sems, local_sems):
        _host_all_gather(pl.program_id(0), n_steps, [w1_shard, w2_shard], [w1_full, w2_full],
                         send_sems, recv_sems, local_sems)

        @pl.when(pl.program_id(0) == 0)
        def _():
            cv_car[...] = jnp.zeros(cv_car.shape, F32)
            xin_car[...] = jnp.zeros(xin_car.shape, F32)
            h_car[...] = jnp.zeros(h_car.shape, F32)
            wabd[...] = _expand_heads(wa_ref[...])
            wxbd[...] = _expand_heads(wx_ref[...])

        row_c = lax.broadcasted_iota(jnp.int32, (SUB, CONV_WIDTH), 0)
        row_r = lax.broadcasted_iota(jnp.int32, (SUB, LRU_WIDTH), 0)
        cw = cw_ref[...]
        rw = rw_ref[...]
        rb = rb_ref[...]
        g_c = gnc_ref[...]
        g_r = gnr_ref[...]
        sp_c = LRU_C * _softplus_neg(lam_ref[...])

        def convs(i, carry):
            cv_prev, xin_prev = carry
            r = pl.multiple_of(i * SUB, SUB)
            gb, _, _, cv, _, _, cq = _conv3_chunk(u_ref, r, cv_prev, cw, row_c)
            y_c = gb * cq
            y_s[pl.ds(r, SUB), 0:CONV_WIDTH] = y_c * _rms(y_c) * g_c
            xin, _, _, _, xr = _conv4_chunk(u_ref, r, xin_prev, rw, rb, row_r)
            xr_s[pl.ds(r, SUB), :] = xr
            return cv, xin

        cv_last, xin_last = _chunk_loop(n_chunks, convs, (cv_car[...], xin_car[...]))
        cv_car[...] = cv_last
        xin_car[...] = xin_last

        xrb = xr_s[...].astype(BF16)
        pa_s[...] = _block_diag_apply(xrb, wabd) + ba_ref[...]
        px_s[...] = _block_diag_apply(xrb, wxbd) + bx_ref[...]

        def recur(i, h_prev):
            r = pl.multiple_of(i * SUB, SUB)
            xr = xr_s[pl.ds(r, SUB), :]
            _, ii, a, mult, _ = _lru_gates(pa_s[pl.ds(r, SUB), :], px_s[pl.ds(r, SUB), :], sp_c)
            a_cum, b_cum = _scan8_fwd(a, mult * ii * xr, row_r)
            h = a_cum * h_prev + b_cum
            hs_ref[pl.ds(r, SUB), :] = h
            ge, _ = _gelu(u_ref[pl.ds(r, SUB), OFF_G:OFF_G + LRU_WIDTH])
            y_r = h * ge
            y_s[pl.ds(r, SUB), CONV_WIDTH:MIX_WIDTH] = y_r * _rms(y_r) * g_r
            return h[SUB - 1:SUB, :]

        h_car[...] = _chunk_loop(n_chunks, recur, h_car[...])

        y_ref[...] = y_s[...].astype(BF16)

    row_tile = lambda w: pl.BlockSpec((tm, w), lambda i: (i, 0))
    whole = lambda a: pl.BlockSpec(a.shape, lambda i: (0,) * a.ndim)
    smalls = (conv_w, rnn_conv_w, rnn_conv_b, wa, b_a, wx, b_x, lam, gnc, gnr)
    return pl.pallas_call(
        body, grid=(n_steps,),
        in_specs=[row_tile(IN_COLS)] + [whole(a) for a in smalls] + [HBM_SPEC, HBM_SPEC],
        out_specs=[row_tile(LRU_WIDTH), row_tile(MIX_WIDTH), HBM_SPEC, HBM_SPEC],
        out_shape=[jax.ShapeDtypeStruct((t_len, LRU_WIDTH), F32), jax.ShapeDtypeStruct((t_len, MIX_WIDTH), BF16),
                   jax.ShapeDtypeStruct((N_DEV,) + w1_shard.shape, BF16),
                   jax.ShapeDtypeStruct((N_DEV,) + w2_shard.shape, BF16)],
        scratch_shapes=[pltpu.VMEM((tm, MIX_WIDTH), F32), pltpu.VMEM((tm, LRU_WIDTH), F32),
                        pltpu.VMEM((tm, LRU_WIDTH), F32), pltpu.VMEM((tm, LRU_WIDTH), F32),
                        pltpu.VMEM((LRU_WIDTH, GROUP), BF16), pltpu.VMEM((LRU_WIDTH, GROUP), BF16),
                        pltpu.VMEM((SUB, CONV_WIDTH), F32), pltpu.VMEM((SUB, LRU_WIDTH), F32),
                        pltpu.VMEM((1, LRU_WIDTH), F32)] + _exchange_scratch(2, 7) + [pltpu.SemaphoreType.DMA((2,))],
        compiler_params=_params(("arbitrary",), 56), name="mixer_fwd",
    )(u, *smalls, w1_shard, w2_shard)


def _mlp_fwd_bwd(x, y, target, g_mlp, g_f, w_out, w1, w2, tm):
    t_len = x.shape[0]
    n_blk, _, blk = w1.shape

    def body(x_ref, y_ref, tg_ref, gm_ref, gf_ref, wout_hbm, w1_hbm, w2_hbm,
             dx1_ref, z_ref, dpre_ref, h2_ref, dx2_ref, vec_ref, wout_s, w1_s, w2_s, rp_s, sem):
        @pl.when(pl.program_id(0) == 0)
        def _():
            loads = [pltpu.make_async_copy(src, dst, sem.at[k])
                     for k, (src, dst) in enumerate(((wout_hbm, wout_s), (w1_hbm, w1_s), (w2_hbm, w2_s)))]
            for cp in loads:
                cp.start()
            vec_ref[...] = jnp.zeros(vec_ref.shape, F32)
            for cp in loads:
                cp.wait()

        x1v = x_ref[...] + jnp.dot(y_ref[...], wout_s[...], preferred_element_type=F32)
        g_m = gm_ref[...]
        g_o = gf_ref[...]
        r2 = _rms(x1v)
        x1h = x1v * r2
        h2 = (x1h * g_m).astype(BF16)
        h2_ref[...] = h2
        x2 = x1v
        for k in range(n_blk):
            rp = jnp.maximum(jnp.dot(h2, w1_s[k], preferred_element_type=F32), 0.0)
            rp_s[:, k * blk:(k + 1) * blk] = rp
            zb = (rp * rp).astype(BF16)
            z_ref[:, k * blk:(k + 1) * blk] = zb
            x2 = x2 + jnp.dot(zb, w2_s[k * blk:(k + 1) * blk, :], preferred_element_type=F32)
        r3 = _rms(x2)
        x2h = x2 * r3
        err = x2h * g_o - tg_ref[...]
        dout = err * (1.0 / D_MODEL)
        vec_ref[ROW_LOSS:ROW_LOSS + 1, :] += (0.5 / D_MODEL) * jnp.sum(err * err, axis=0, keepdims=True)
        vec_ref[ROW_GF:ROW_GF + 1, :] += jnp.sum(dout * x2h, axis=0, keepdims=True)
        dx2 = _rms_bwd(dout, x2h, r3, g_o)
        dx2b = dx2.astype(BF16)
        dx2_ref[...] = dx2b
        dh2 = jnp.zeros((tm, D_MODEL), F32)
        for k in range(n_blk):
            dz = _dot_nt(dx2b, w2_s[k * blk:(k + 1) * blk, :])
            dpb = (dz * 2.0 * rp_s[:, k * blk:(k + 1) * blk]).astype(BF16)
            dpre_ref[:, k * blk:(k + 1) * blk] = dpb
            dh2 = dh2 + _dot_nt(dpb, w1_s[k])
        vec_ref[ROW_GMLP:ROW_GMLP + 1, :] += jnp.sum(dh2 * x1h, axis=0, keepdims=True)
        dx1_ref[...] = dx2 + _rms_bwd(dh2, x1h, r2, g_m)

    row_tile = lambda w: pl.BlockSpec((tm, w), lambda i: (i, 0))
    vec_spec = pl.BlockSpec((1, D_MODEL), lambda i: (0, 0))
    return pl.pallas_call(
        body, grid=(t_len // tm,),
        in_specs=[row_tile(D_MODEL), row_tile(MIX_WIDTH), row_tile(D_MODEL), vec_spec, vec_spec,
                  HBM_SPEC, HBM_SPEC, HBM_SPEC],
        out_specs=[row_tile(D_MODEL), row_tile(D_FF), row_tile(D_FF), row_tile(D_MODEL), row_tile(D_MODEL),
                   pl.BlockSpec((SUB, D_MODEL), lambda i: (0, 0))],
        out_shape=[jax.ShapeDtypeStruct((t_len, D_MODEL), F32), jax.ShapeDtypeStruct((t_len, D_FF), BF16),
                   jax.ShapeDtypeStruct((t_len, D_FF), BF16), jax.ShapeDtypeStruct((t_len, D_MODEL), BF16),
                   jax.ShapeDtypeStruct((t_len, D_MODEL), BF16), jax.ShapeDtypeStruct((SUB, D_MODEL), F32)],
        scratch_shapes=[pltpu.VMEM(w_out.shape, BF16), pltpu.VMEM(w1.shape, BF16), pltpu.VMEM(w2.shape, BF16),
                        pltpu.VMEM((tm, D_FF), F32), pltpu.SemaphoreType.DMA((3,))],
        compiler_params=_params(("arbitrary",), 58), name="mlp_fwd_bwd",
    )(x, y, target, g_mlp, g_f, w_out, w1, w2)


def _mixer_bwd(u, hs, dx1, conv_w, rnn_conv_w, rnn_conv_b, wa, b_a, wx, b_x, lam, gnc, gnr, w_out,
               chip_sums, g_wout, tm):
    t_len = u.shape[0]
    n_tiles = t_len // tm
    n_chunks = tm // SUB
    per_tile = tm // SUB
    n_sums = len(chip_sums)

    def body(u_ref, up_ref, hs_ref, hp_ref, dx1_ref, cw_ref, rw_ref, rb_ref, wa_ref, ba_ref, wx_ref, bx_ref,
             lam_ref, gnc_ref, gnr_ref, wout_ref, *rest):
        hsends = rest[0:n_sums]
        gwout_ref = rest[n_sums]
        du_ref, vec_ref, wab_ref = rest[n_sums + 1:n_sums + 4]
        hrecvs = rest[n_sums + 4:2 * n_sums + 4]
        sib_wout = rest[2 * n_sums + 4]
        (du_s, dy_s, xr_s, pa_s, px_s, dpa_s, dpx_s, dxr_s, wabd, wxbd, acc, dwa_acc, dwx_acc,
         a_car, dh_car, dcq_car, dxr_car, i_send, i_recv, d_send, d_recv) = rest[2 * n_sums + 5:]
        step = pl.program_id(0)
        _host_chip_exchange(step, n_tiles, hsends, hrecvs, i_send, i_recv)
        _host_pair_exchange(step, n_tiles, [gwout_ref], [sib_wout], d_send, d_recv)
        has_prev = (step < n_tiles - 1).astype(F32)

        @pl.when(step == 0)
        def _():
            acc[...] = jnp.zeros(acc.shape, F32)
            dwa_acc[...] = jnp.zeros(dwa_acc.shape, F32)
            dwx_acc[...] = jnp.zeros(dwx_acc.shape, F32)
            a_car[...] = jnp.ones(a_car.shape, F32)
            dh_car[...] = jnp.zeros(dh_car.shape, F32)
            dcq_car[...] = jnp.zeros(dcq_car.shape, F32)
            dxr_car[...] = jnp.zeros(dxr_car.shape, F32)
            wabd[...] = _expand_heads(wa_ref[...])
            wxbd[...] = _expand_heads(wx_ref[...])

        row_c = lax.broadcasted_iota(jnp.int32, (SUB, CONV_WIDTH), 0)
        row_r = lax.broadcasted_iota(jnp.int32, (SUB, LRU_WIDTH), 0)
        cw = cw_ref[...]
        rw = rw_ref[...]
        rb = rb_ref[...]
        g_c = gnc_ref[...]
        g_r = gnr_ref[...]
        sp_c = LRU_C * _softplus_neg(lam_ref[...])

        up = up_ref[...] * has_prev
        cv_before = up[:, OFF_GC:OFF_GC + CONV_WIDTH] * up[:, OFF_V:OFF_V + CONV_WIDTH]
        xin_before = up[:, OFF_XR:OFF_XR + LRU_WIDTH]
        hs_before = hp_ref[...] * has_prev

        dy_s[...] = _dot_nt(dx1_ref[...].astype(BF16), wout_ref[...])

        def conv4_fwd(i, xin_prev):
            r = pl.multiple_of(i * SUB, SUB)
            xin, _, _, _, xr = _conv4_chunk(u_ref, r, xin_prev, rw, rb, row_r)
            xr_s[pl.ds(r, SUB), :] = xr
            return xin

        _chunk_loop(n_chunks, conv4_fwd, xin_before)
        xrb = xr_s[...].astype(BF16)
        pa_s[...] = _block_diag_apply(xrb, wabd) + ba_ref[...]
        px_s[...] = _block_diag_apply(xrb, wxbd) + bx_ref[...]

        def recur_bwd(j, carry):
            a_later, dh_later = carry
            i = n_chunks - 1 - j
            r = pl.multiple_of(i * SUB, SUB)
            rp = pl.multiple_of(jnp.maximum(i - 1, 0) * SUB, SUB)
            xr = xr_s[pl.ds(r, SUB), :]
            hs_c = hs_ref[pl.ds(r, SUB), :]
            hs_prev = jnp.where(i == 0, hs_before, hs_ref[pl.ds(rp, SUB), :])
            h_m1 = _down(hs_c, hs_prev, 1, row_r)
            ra, ii, a, mult, inv_mult = _lru_gates(pa_s[pl.ds(r, SUB), :], px_s[pl.ds(r, SUB), :], sp_c)
            ge, dge = _gelu(u_ref[pl.ds(r, SUB), OFF_G:OFF_G + LRU_WIDTH])
            y_r = hs_c * ge
            rr = _rms(y_r)
            yhat = y_r * rr
            dyn = dy_s[pl.ds(r, SUB), CONV_WIDTH:MIX_WIDTH]
            acc[ACC_GNR] += dyn * yhat
            dy_r = _rms_bwd(dyn, yhat, rr, g_r)
            du_s[pl.ds(r, SUB), OFF_G:OFF_G + LRU_WIDTH] = dy_r * hs_c * dge
            a_cum, d_cum = _scan8_rev(_up(a, a_later, 1, row_r), dy_r * ge, row_r)
            dh = a_cum * dh_later + d_cum
            dmult = dh * ii * xr
            dii = dh * mult * xr
            dxr_s[pl.ds(r, SUB), :] = dh * mult * ii
            dla = dh * h_m1 * a - dmult * a * a * inv_mult
            acc[ACC_SP] += -dla * ra
            dpa = -dla * sp_c * ra * (1.0 - ra)
            dpx = dii * ii * (1.0 - ii)
            acc[ACC_BA] += dpa
            acc[ACC_BX] += dpx
            dpa_s[pl.ds(r, SUB), :] = dpa
            dpx_s[pl.ds(r, SUB), :] = dpx
            return a, dh[0:1, :]

        a_first, dh_first = _chunk_loop(n_chunks, recur_bwd, (a_car[...], dh_car[...]))
        a_car[...] = a_first
        dh_car[...] = dh_first

        dpab = dpa_s[...].astype(BF16)
        dpxb = dpx_s[...].astype(BF16)
        dxr_s[...] += _block_diag_apply_t(dpab, wabd) + _block_diag_apply_t(dpxb, wxbd)
        for g in range(LRU_WIDTH // GROUP):
            cols = slice(g * GROUP, (g + 1) * GROUP)
            dwa_acc[cols, :] += _dot_tn(xrb[:, cols], dpab[:, cols])
            dwx_acc[cols, :] += _dot_tn(xrb[:, cols], dpxb[:, cols])

        def convs_bwd(j, carry):
            dcq_later, dxr_later = carry
            i = n_chunks - 1 - j
            r = pl.multiple_of(i * SUB, SUB)
            rp = pl.multiple_of(jnp.maximum(i - 1, 0) * SUB, SUB)
            cv_prev = jnp.where(i == 0, cv_before,
                                u_ref[pl.ds(rp, SUB), OFF_GC:OFF_GC + CONV_WIDTH]
                                * u_ref[pl.ds(rp, SUB), OFF_V:OFF_V + CONV_WIDTH])
            gb, gc, v, cv, cv_m1, cv_m2, cq = _conv3_chunk(u_ref, r, cv_prev, cw, row_c)
            y_c = gb * cq
            rc = _rms(y_c)
            yhat = y_c * rc
            dyn = dy_s[pl.ds(r, SUB), 0:CONV_WIDTH]
            acc[ACC_GNC, :, 0:CONV_WIDTH] += dyn * yhat
            dy_c = _rms_bwd(dyn, yhat, rc, g_c)
            dcq = dy_c * gb
            dcv = (cw[2:3, :] * dcq + cw[1:2, :] * _up(dcq, dcq_later, 1, row_c)
                   + cw[0:1, :] * _up(dcq, dcq_later, 2, row_c))
            acc[ACC_CW + 2, :, 0:CONV_WIDTH] += dcq * cv
            acc[ACC_CW + 1, :, 0:CONV_WIDTH] += dcq * cv_m1
            acc[ACC_CW + 0, :, 0:CONV_WIDTH] += dcq * cv_m2
            du_s[pl.ds(r, SUB), OFF_GB:OFF_GB + CONV_WIDTH] = dy_c * cq
            du_s[pl.ds(r, SUB), OFF_GC:OFF_GC + CONV_WIDTH] = dcv * v
            du_s[pl.ds(r, SUB), OFF_V:OFF_V + CONV_WIDTH] = dcv * gc

            xin_prev = jnp.where(i == 0, xin_before, u_ref[pl.ds(rp, SUB), OFF_XR:OFF_XR + LRU_WIDTH])
            xin, m1, m2, m3, _ = _conv4_chunk(u_ref, r, xin_prev, rw, rb, row_r)
            dxr = dxr_s[pl.ds(r, SUB), :]
            du_s[pl.ds(r, SUB), OFF_XR:OFF_XR + LRU_WIDTH] = (
                rw[3:4, :] * dxr + rw[2:3, :] * _up(dxr, dxr_later, 1, row_r)
                + rw[1:2, :] * _up(dxr, dxr_later, 2, row_r) + rw[0:1, :] * _up(dxr, dxr_later, 3, row_r))
            acc[ACC_RW + 3] += dxr * xin
            acc[ACC_RW + 2] += dxr * m1
            acc[ACC_RW + 1] += dxr * m2
            acc[ACC_RW + 0] += dxr * m3
            acc[ACC_BR] += dxr
            return dcq, dxr

        dcq_first, dxr_first = _chunk_loop(n_chunks, convs_bwd, (dcq_car[...], dxr_car[...]))
        dcq_car[...] = dcq_first
        dxr_car[...] = dxr_first

        du_ref[...] = du_s[...].astype(BF16)

        @pl.when(step == n_tiles - 1)
        def _():
            vec_ref[...] = jnp.zeros(vec_ref.shape, F32)
            rows = {ACC_GNC: ROW_GNC, ACC_GNR: ROW_GNR, ACC_BR: ROW_BR, ACC_BA: ROW_BA, ACC_BX: ROW_BX}
            for k in range(3):
                rows[ACC_CW + k] = ROW_CW + k
            for k in range(4):
                rows[ACC_RW + k] = ROW_RW + k
            for slot, out_row in rows.items():
                o = out_row - ROW_GNC
                vec_ref[o:o + 1, :] = jnp.sum(acc[slot], axis=0, keepdims=True)
            lam_v = lam_ref[...]
            dsp = jnp.sum(acc[ACC_SP], axis=0, keepdims=True)
            o = ROW_LAM - ROW_GNC
            vec_ref[o:o + 1, :] = -dsp * LRU_C / (1.0 + jnp.exp(lam_v))
            wab_ref[0:LRU_WIDTH, :] = _fold_heads(dwa_acc[...])
            wab_ref[LRU_WIDTH:2 * LRU_WIDTH, :] = _fold_heads(dwx_acc[...])

    rev = lambda w: pl.BlockSpec((tm, w), lambda s: (n_tiles - 1 - s, 0))
    before = lambda w: pl.BlockSpec((SUB, w), lambda s: (jnp.maximum((n_tiles - 1 - s) * per_tile - 1, 0), 0))
    whole = lambda a: pl.BlockSpec(a.shape, lambda s: (0,) * a.ndim)
    smalls = (conv_w, rnn_conv_w, rnn_conv_b, wa, b_a, wx, b_x, lam, gnc, gnr, w_out)
    full = lambda w: pltpu.VMEM((tm, w), F32)
    return pl.pallas_call(
        body, grid=(n_tiles,),
        in_specs=[rev(IN_COLS), before(IN_COLS), rev(LRU_WIDTH), before(LRU_WIDTH), rev(D_MODEL)]
        + [whole(a) for a in smalls] + [HBM_SPEC] * (n_sums + 1),
        out_specs=[rev(IN_COLS), pl.BlockSpec((16, D_MODEL), lambda s: (0, 0)),
                   pl.BlockSpec((2 * LRU_WIDTH, HEAD_DIM), lambda s: (0, 0))] + [HBM_SPEC] * (n_sums + 1),
        out_shape=[jax.ShapeDtypeStruct((t_len, IN_COLS), BF16), jax.ShapeDtypeStruct((16, D_MODEL), F32),
                   jax.ShapeDtypeStruct((2 * LRU_WIDTH, HEAD_DIM), F32)]
        + [jax.ShapeDtypeStruct(s.shape, BF16) for s in chip_sums]
        + [jax.ShapeDtypeStruct((4,) + g_wout.shape[1:], BF16)],
        scratch_shapes=[full(IN_COLS), full(MIX_WIDTH), full(LRU_WIDTH), full(LRU_WIDTH), full(LRU_WIDTH),
                        full(LRU_WIDTH), full(LRU_WIDTH), full(LRU_WIDTH),
                        pltpu.VMEM((LRU_WIDTH, GROUP), BF16), pltpu.VMEM((LRU_WIDTH, GROUP), BF16),
                        pltpu.VMEM((N_ACC, SUB, LRU_WIDTH), F32),
                        pltpu.VMEM((LRU_WIDTH, GROUP), F32), pltpu.VMEM((LRU_WIDTH, GROUP), F32),
                        pltpu.VMEM((SUB, LRU_WIDTH), F32), pltpu.VMEM((1, LRU_WIDTH), F32),
                        pltpu.VMEM((SUB, CONV_WIDTH), F32), pltpu.VMEM((SUB, LRU_WIDTH), F32)]
        + _exchange_scratch(n_sums, 3) + _exchange_scratch(1, 4),
        compiler_params=_params(("arbitrary",), 56), name="mixer_bwd",
    )(u, u, hs, hs, dx1, *smalls, *chip_sums, g_wout)


def _in_proj_bwd(du, dx1, x, g_mix, win_t, tm):
    t_len = x.shape[0]

    def body(du_ref, dx1_ref, x_ref, g_ref, w_ref, dx_ref, vec_ref):
        @pl.when(pl.program_id(0) == 0)
        def _():
            vec_ref[...] = jnp.zeros(vec_ref.shape, F32)

        dh = jnp.dot(du_ref[...], w_ref[...], preferred_element_type=F32)
        xv = x_ref[...]
        r1 = _rms(xv)
        xh = xv * r1
        vec_ref[0:1, :] += jnp.sum(dh * xh, axis=0, keepdims=True)
        dx_ref[...] = dx1_ref[...] + _rms_bwd(dh, xh, r1, g_ref[...])

    row_tile = lambda w: pl.BlockSpec((tm, w), lambda i: (i, 0))
    return pl.pallas_call(
        body, grid=(t_len // tm,),
        in_specs=[row_tile(IN_COLS), row_tile(D_MODEL), row_tile(D_MODEL), pl.BlockSpec((1, D_MODEL), lambda i: (0, 0)),
                  pl.BlockSpec((IN_COLS, D_MODEL), lambda i: (0, 0))],
        out_specs=[row_tile(D_MODEL), pl.BlockSpec((SUB, D_MODEL), lambda i: (0, 0))],
        out_shape=[jax.ShapeDtypeStruct((t_len, D_MODEL), F32), jax.ShapeDtypeStruct((SUB, D_MODEL), F32)],
        compiler_params=_params(("arbitrary",), 56), name="in_proj_bwd",
    )(du, dx1, x, g_mix, win_t)


def _tn_weight_grad(a, b, tk, name, pair=(), chip=(), col_blocks=1):
    t_len, m = a.shape
    n = b.shape[1]
    n_steps = t_len // tk
    sent = tuple(pair) + tuple(chip)
    n_sent = len(sent)

    def body(a_ref, b_ref, *rest):
        srcs = rest[0:n_sent]
        o_ref = rest[n_sent]
        dsts = rest[n_sent + 1:2 * n_sent + 1]
        acc = rest[2 * n_sent + 1]
        sems = rest[2 * n_sent + 2:]
        j = pl.program_id(0)
        if pair:
            _host_pair_exchange(j, n_steps, srcs, dsts, *sems)
        if chip:
            _host_chip_exchange(j, n_steps, srcs, dsts, *sems)

        @pl.when(j == 0)
        def _():
            acc[...] = jnp.zeros(acc.shape, F32)

        acc[...] += _dot_tn(a_ref[...].astype(BF16), b_ref[...].astype(BF16))

        @pl.when(j == n_steps - 1)
        def _():
            if col_blocks == 1:
                o_ref[...] = acc[...].astype(BF16)
            else:
                for k in range(col_blocks):
                    o_ref[k] = acc[:, k * nb:(k + 1) * nb].astype(BF16)

    nb = n // col_blocks
    out_dims = (m, n) if col_blocks == 1 else (col_blocks, m, nb)
    landed = [jax.ShapeDtypeStruct((4,) + g.shape[1:], BF16) for g in pair]
    landed += [jax.ShapeDtypeStruct(s.shape, BF16) for s in chip]
    scratch = [pltpu.VMEM((m, n), F32)]
    if n_sent:
        scratch += _exchange_scratch(n_sent, 4 if pair else 3)
    return pl.pallas_call(
        body, grid=(n_steps,),
        in_specs=[pl.BlockSpec((tk, m), lambda j: (j, 0)), pl.BlockSpec((tk, n), lambda j: (j, 0))]
        + [HBM_SPEC] * n_sent,
        out_specs=[pl.BlockSpec(out_dims, lambda j: (0,) * len(out_dims))] + [HBM_SPEC] * n_sent,
        out_shape=[jax.ShapeDtypeStruct(out_dims, BF16)] + landed,
        scratch_shapes=scratch,
        compiler_params=_params(("arbitrary",), 56), name=name,
    )(a, b, *sent)


def _adamw(w, g, m, v):
    m = ADAM_B1 * m + (1.0 - ADAM_B1) * g
    v = ADAM_B2 * v + (1.0 - ADAM_B2) * (g * g)
    delta = -ADAM_LR * ((m / BC1) / (jnp.sqrt(v / BC2) + ADAM_EPS) + ADAM_WD * w)
    return delta, m, v


def _update_sharded(g, w, m, v, rows_blk, name, landed=None, transposed=False):
    rows, cols = w.shape
    pad_cols = -(-cols // 128) * 128

    def body(g_ref, *rest):
        if landed is not None:
            l_ref, rest = rest[0], rest[1:]
        w_ref, m_ref, v_ref, og, od, om, ov = rest[0:7]
        if transposed:
            padbuf, turned = rest[7:]
            padbuf[...] = jnp.zeros(padbuf.shape, F32)
            padbuf[0:cols, :] = g_ref[...]
            turned[...] = padbuf[...].T
            gv = turned[:, 0:cols]
        else:
            gv = g_ref[...]
        if landed is not None:
            for j in range(3):
                gv = gv + l_ref[j].astype(F32)
        delta, mn, vn = _adamw(w_ref[...], gv, m_ref[...], v_ref[...])
        og[...] = gv
        od[...] = delta
        om[...] = mn
        ov[...] = vn

    blk = pl.BlockSpec((rows_blk, cols), lambda i: (i, 0))
    g_spec = pl.BlockSpec((cols, rows_blk), lambda i: (0, i)) if transposed else blk
    extra_specs = [] if landed is None else [pl.BlockSpec((3, rows_blk, cols), lambda i: (0, i, 0))]
    extra_args = [] if landed is None else [landed]
    shape = jax.ShapeDtypeStruct((rows, cols), F32)
    return pl.pallas_call(
        body, grid=(rows // rows_blk,), in_specs=[g_spec] + extra_specs + [blk, blk, blk], out_specs=[blk] * 4,
        out_shape=[shape] * 4,
        scratch_shapes=[pltpu.VMEM((pad_cols, rows_blk), F32), pltpu.VMEM((rows_blk, pad_cols), F32)] if transposed else [],
        compiler_params=_params(("arbitrary",), 32), name=name,
    )(g, *extra_args, w, m, v)


def _update_small(vsum, wsum, g_cw, g_rw, weights, moments_m, moments_v):
    n = len(weights)

    def body(*refs):
        vs, ws, gcw, grw = refs[0:4]
        w_refs = refs[4:4 + n]
        m_refs = refs[4 + n:4 + 2 * n]
        v_refs = refs[4 + 2 * n:4 + 3 * n]
        outs = refs[4 + 3 * n:]
        loss_ref = outs[0]
        loss_ref[...] = jnp.sum(vs[ROW_LOSS:ROW_LOSS + 1, :], axis=1, keepdims=True)
        grads = [
            vs[ROW_GMIX:ROW_GMIX + 1, :], gcw[...], grw[...], vs[ROW_BR:ROW_BR + 1, :],
            ws[0:LRU_WIDTH, :], vs[ROW_BA:ROW_BA + 1, :], ws[LRU_WIDTH:2 * LRU_WIDTH, :], vs[ROW_BX:ROW_BX + 1, :],
            vs[ROW_LAM:ROW_LAM + 1, :], vs[ROW_GNC:ROW_GNC + 1, 0:CONV_WIDTH], vs[ROW_GNR:ROW_GNR + 1, :],
            vs[ROW_GMLP:ROW_GMLP + 1, :], vs[ROW_GF:ROW_GF + 1, :],
        ]
        for k in range(n):
            gk = grads[k]
            delta, mn, vn = _adamw(w_refs[k][...], gk, m_refs[k][...], v_refs[k][...])
            outs[1 + 4 * k][...] = gk
            outs[2 + 4 * k][...] = delta
            outs[3 + 4 * k][...] = mn
            outs[4 + 4 * k][...] = vn

    whole = lambda a: pl.BlockSpec(a.shape, lambda i: (0,) * len(a.shape))
    out_shape = [jax.ShapeDtypeStruct((1, 1), F32)]
    for w in weights:
        out_shape += [jax.ShapeDtypeStruct(w.shape, F32)] * 4
    args = (vsum, wsum, g_cw, g_rw, *weights, *moments_m, *moments_v)
    return pl.pallas_call(
        body, grid=(1,), out_shape=out_shape, in_specs=[whole(a) for a in args], out_specs=[whole(s) for s in out_shape],
        compiler_params=_params(("arbitrary",), 32), name="update_small",
    )(*args)


def kernel(x, norm_mix_g, w_in, conv_w, rnn_conv_w, rnn_conv_b, w_a, b_a, w_x, b_x, lru_lambda, g_norm_conv, g_norm_rnn, w_out, norm_mlp_g, w_mlp_in, w_mlp_out, final_norm_g, loss_target, m_norm_mix_g, m_w_in, m_conv_w, m_rnn_conv_w, m_rnn_conv_b, m_w_a, m_b_a, m_w_x, m_b_x, m_lru_lambda, m_g_norm_conv, m_g_norm_rnn, m_w_out, m_norm_mlp_g, m_w_mlp_in, m_w_mlp_out, m_final_norm_g, v_norm_mix_g, v_w_in, v_conv_w, v_rnn_conv_w, v_rnn_conv_b, v_w_a, v_b_a, v_w_x, v_b_x, v_lru_lambda, v_g_norm_conv, v_g_norm_rnn, v_w_out, v_norm_mlp_g, v_w_mlp_in, v_w_mlp_out, v_final_norm_g):
    t_len = x.shape[1]
    my_id = 4 * lax.axis_index("x") + 2 * lax.axis_index("y") + lax.axis_index("c")
    tm = min(256, t_len)
    tk = min(512, t_len)

    xs = x.reshape(t_len, D_MODEL)
    tgt = loss_target.reshape(t_len, D_MODEL)
    flat = lambda a: a.reshape(a.shape[-2:]) if a.ndim == 3 else a.reshape(1, -1)
    heads = lambda a: a.reshape(LRU_WIDTH, HEAD_DIM)

    win_blk, cpack, wout_shard, w1_shard, w2_shard = _all_gather_w_in(
        flat(w_in), flat(w_out), flat(w_mlp_in), flat(w_mlp_out), flat(conv_w), flat(rnn_conv_w))
    win_t = win_blk.reshape(IN_COLS, D_MODEL)
    conv_full = jnp.transpose(cpack[:, 0:3, 0:64], (1, 0, 2)).reshape(3, CONV_WIDTH)
    rnn_full = jnp.transpose(cpack[:, 3:7, :], (1, 0, 2)).reshape(4, LRU_WIDTH)
    mixer_small = (conv_full, rnn_full, flat(rnn_conv_b), heads(w_a), flat(b_a), heads(w_x), flat(b_x),
                   flat(lru_lambda), flat(g_norm_conv), flat(g_norm_rnn))

    u, h, wout_blk = _in_proj(xs, flat(norm_mix_g), win_t, wout_shard, tm)
    wout_f = wout_blk.reshape(MIX_WIDTH, D_MODEL)
    hs, y, w1_blk, w2_blk = _mixer_fwd(u, *mixer_small, w1_shard, w2_shard, tm)
    dx1, z, dpre, h2, dx2, vec_m = _mlp_fwd_bwd(xs, y, tgt, flat(norm_mlp_g), flat(final_norm_g), wout_f, w1_blk,
                                                w2_blk.reshape(D_FF, D_MODEL), tm)
    (g_w1,) = _tn_weight_grad(h2, dpre, tk, "w_mlp_in_grad", col_blocks=N_DEV)
    (g_w2,) = _tn_weight_grad(z, dx2, tk, "w_mlp_out_grad")
    g_w2 = g_w2.reshape(N_DEV, D_FF // N_DEV, D_MODEL)
    g_wout, sib_w1, sib_w2 = _tn_weight_grad(y, dx1, tk, "w_out_grad", pair=(g_w1, g_w2))
    g_wout = g_wout.reshape(N_DEV, MIX_WIDTH // N_DEV, D_MODEL)
    hsend_w1, own_w1 = _pair_sum(g_w1, sib_w1, "pair_sum_w_mlp_in")
    hsend_w2, own_w2 = _pair_sum(g_w2, sib_w2, "pair_sum_w_mlp_out")
    du, vec_b, wab, landed_w1, landed_w2, sib_wout = _mixer_bwd(
        u, hs, dx1, *mixer_small, wout_f, (hsend_w1, hsend_w2), g_wout, tm)
    hsend_wout, own_wout = _pair_sum(g_wout, sib_wout, "pair_sum_w_out")
    grad_x, vec_x = _in_proj_bwd(du, dx1, xs, flat(norm_mix_g), win_t, tm)
    g_win_t, landed_wout = _tn_weight_grad(du, h, tk, "w_in_grad", chip=(hsend_wout,))

    r_win_t, vsum, wsum = _final_exchange(g_win_t.reshape(N_DEV, IN_COLS // N_DEV, D_MODEL), vec_m, vec_b, vec_x, wab)

    up_win = _update_sharded(r_win_t, flat(w_in), flat(m_w_in), flat(v_w_in), 256, "update_w_in", transposed=True)
    up_wout = _update_sharded(own_wout, flat(w_out), flat(m_w_out), flat(v_w_out), 96, "update_w_out",
                              landed=landed_wout)
    up_w1 = _update_sharded(own_w1, flat(w_mlp_in), flat(m_w_mlp_in), flat(v_w_mlp_in), 256, "update_w_mlp_in",
                            landed=landed_w1)
    up_w2 = _update_sharded(own_w2, flat(w_mlp_out), flat(m_w_mlp_out), flat(v_w_mlp_out), 256, "update_w_mlp_out",
                            landed=landed_w2)

    g_cw = lax.dynamic_slice(vsum, (ROW_CW, 64 * my_id), (3, 64))
    g_rw = lax.dynamic_slice(vsum, (ROW_RW, 128 * my_id), (4, 128))
    small_w = (norm_mix_g, conv_w, rnn_conv_w, rnn_conv_b, w_a, b_a, w_x, b_x, lru_lambda, g_norm_conv, g_norm_rnn,
               norm_mlp_g, final_norm_g)
    small_m = (m_norm_mix_g, m_conv_w, m_rnn_conv_w, m_rnn_conv_b, m_w_a, m_b_a, m_w_x, m_b_x, m_lru_lambda,
               m_g_norm_conv, m_g_norm_rnn, m_norm_mlp_g, m_final_norm_g)
    small_v = (v_norm_mix_g, v_conv_w, v_rnn_conv_w, v_rnn_conv_b, v_w_a, v_b_a, v_w_x, v_b_x, v_lru_lambda,
               v_g_norm_conv, v_g_norm_rnn, v_norm_mlp_g, v_final_norm_g)
    is_heads = (False, False, False, False, True, False, True, False, False, False, False, False, False)
    as2d = lambda arrs: [heads(a) if hd else flat(a) for a, hd in zip(arrs, is_heads)]
    small_out = _update_small(vsum, wsum, g_cw, g_rw, as2d(small_w), as2d(small_m), as2d(small_v))
    loss = small_out[0].reshape(())

    names = ["norm_mix_g", "w_in", "conv_w", "rnn_conv_w", "rnn_conv_b", "w_a", "b_a", "w_x", "b_x", "lru_lambda",
             "g_norm_conv", "g_norm_rnn", "w_out", "norm_mlp_g", "w_mlp_in", "w_mlp_out", "final_norm_g"]
    originals = dict(zip(names, (norm_mix_g, w_in, conv_w, rnn_conv_w, rnn_conv_b, w_a, b_a, w_x, b_x, lru_lambda,
                                 g_norm_conv, g_norm_rnn, w_out, norm_mlp_g, w_mlp_in, w_mlp_out, final_norm_g)))
    results = {"w_in": up_win, "w_out": up_wout, "w_mlp_in": up_w1, "w_mlp_out": up_w2}
    small_names = ["norm_mix_g", "conv_w", "rnn_conv_w", "rnn_conv_b", "w_a", "b_a", "w_x", "b_x", "lru_lambda",
                   "g_norm_conv", "g_norm_rnn", "norm_mlp_g", "final_norm_g"]
    for k, nm in enumerate(small_names):
        results[nm] = small_out[1 + 4 * k:5 + 4 * k]
    out = [loss, grad_x.reshape(x.shape)]
    for kind in range(4):
        out += [results[nm][kind].reshape(originals[nm].shape) for nm in names]
    return tuple(out)
```

```python
import functools

import jax
import jax.numpy as jnp
from jax import lax
from jax.experimental import pallas as pl
from jax.experimental.pallas import tpu as pltpu

F32 = jnp.float32
BF16 = jnp.bfloat16

D_MODEL = 1024
HEAD_DIM = 64
CONV_WIDTH = 512
LRU_WIDTH = 1024
MIX_WIDTH = CONV_WIDTH + LRU_WIDTH
IN_COLS = 3 * CONV_WIDTH + 2 * LRU_WIDTH
D_FF = 4 * D_MODEL
GROUP = 256
EPS = 1e-6
LRU_C = 8.0
N_DEV = 8
SUB = 8

OFF_GB, OFF_GC, OFF_V, OFF_XR, OFF_G = 0, 512, 1024, 1536, 2560

ADAM_LR, ADAM_B1, ADAM_B2, ADAM_EPS, ADAM_WD, ADAM_STEP = 0.001, 0.9, 0.999, 1e-08, 0.01, 10
BC1 = 1.0 - ADAM_B1 ** ADAM_STEP
BC2 = 1.0 - ADAM_B2 ** ADAM_STEP

MIB = 1024 * 1024
MESH = pl.DeviceIdType.MESH

VEC_ROWS = 32
ROW_GF, ROW_GMLP, ROW_LOSS = 0, 1, 2
ROW_GNC, ROW_GNR, ROW_BR, ROW_BA, ROW_BX, ROW_LAM, ROW_CW, ROW_RW = 8, 9, 10, 11, 12, 13, 14, 17
ROW_GMIX = 24
ACC_GNC, ACC_GNR, ACC_BR, ACC_BA, ACC_BX, ACC_SP, ACC_CW, ACC_RW, N_ACC = 0, 1, 2, 3, 4, 5, 6, 9, 13


def _params(semantics=None, vmem_mib=48):
    return pltpu.CompilerParams(dimension_semantics=semantics, vmem_limit_bytes=vmem_mib * MIB)


def _rms(x):
    return lax.rsqrt(jnp.mean(x * x, axis=-1, keepdims=True) + EPS)


def _rms_bwd(dy, xhat, r, g):
    dyh = dy * g
    return r * (dyh - xhat * jnp.mean(dyh * xhat, axis=-1, keepdims=True))


def _sigmoid(x):
    return 0.5 + 0.5 * jnp.tanh(0.5 * x)


def _gelu(x):
    c0, c1 = 0.7978845608028654, 0.044715
    t = jnp.tanh(c0 * (x + c1 * x * x * x))
    ge = 0.5 * x * (1.0 + t)
    dge = 0.5 * (1.0 + t) + 0.5 * x * (1.0 - t * t) * c0 * (1.0 + 3.0 * c1 * x * x)
    return ge, dge


def _softplus_neg(lam):
    z = -lam
    e = jnp.exp(-jnp.abs(z))
    return jnp.maximum(z, 0.0) + jnp.where(e < 1e-4, e * (1.0 - 0.5 * e), jnp.log(1.0 + e))


def _lru_gates(pa, px, sp_c):
    ra = _sigmoid(pa)
    ii = _sigmoid(px)
    la = -ra * sp_c
    a = jnp.exp(la)
    x2 = 2.0 * la
    series = -x2 * (1.0 + x2 * (0.5 + x2 * (1.0 / 6.0 + x2 * (1.0 / 24.0))))
    m2 = jnp.where(x2 > -0.01, series, 1.0 - a * a)
    inv_mult = lax.rsqrt(m2)
    mult = jnp.where(m2 > 0.0, m2 * inv_mult, 0.0)
    return ra, ii, a, mult, inv_mult


def _down(cur, prev, s, row):
    return jnp.where(row >= s, pltpu.roll(cur, s, 0), pltpu.roll(prev, s, 0))


def _up(cur, nxt, s, row):
    return jnp.where(row < SUB - s, pltpu.roll(cur, SUB - s, 0), pltpu.roll(nxt, SUB - s, 0))


def _scan8_fwd(a, b, row):
    for s in (1, 2, 4):
        m = row >= s
        a_sh = pltpu.roll(a, s, 0)
        b_sh = pltpu.roll(b, s, 0)
        b = jnp.where(m, a * b_sh + b, b)
        a = jnp.where(m, a * a_sh, a)
    return a, b


def _scan8_rev(a, b, row):
    for s in (1, 2, 4):
        m = row < SUB - s
        a_sh = pltpu.roll(a, SUB - s, 0)
        b_sh = pltpu.roll(b, SUB - s, 0)
        b = jnp.where(m, a * b_sh + b, b)
        a = jnp.where(m, a * a_sh, a)
    return a, b


def _group_mask(shape):
    r = lax.broadcasted_iota(jnp.int32, shape, 0)
    c = lax.broadcasted_iota(jnp.int32, shape, 1)
    return ((r % GROUP) // HEAD_DIM) == (c // HEAD_DIM)


def _expand_heads(w):
    j = lax.broadcasted_iota(jnp.int32, (HEAD_DIM, GROUP), 0)
    c = lax.broadcasted_iota(jnp.int32, (HEAD_DIM, GROUP), 1)
    spread = (c % HEAD_DIM == j).astype(BF16)
    e = jnp.dot(w.astype(BF16), spread, preferred_element_type=F32)
    return jnp.where(_group_mask(e.shape), e, 0.0).astype(BF16)


def _fold_heads(p):
    p = jnp.where(_group_mask(p.shape), p, 0.0)
    c = lax.broadcasted_iota(jnp.int32, (GROUP, HEAD_DIM), 0)
    j = lax.broadcasted_iota(jnp.int32, (GROUP, HEAD_DIM), 1)
    fold = (c % HEAD_DIM == j).astype(BF16)
    hi = p.astype(BF16)
    rest = p - hi.astype(F32)
    mid = rest.astype(BF16)
    lo = (rest - mid.astype(F32)).astype(BF16)
    dot = functools.partial(jnp.dot, preferred_element_type=F32)
    return dot(hi, fold) + dot(mid, fold) + dot(lo, fold)


def _block_diag_apply(xb, wbd_ref):
    parts = [jnp.dot(xb[:, g * GROUP:(g + 1) * GROUP], wbd_ref[g * GROUP:(g + 1) * GROUP, :],
                     preferred_element_type=F32) for g in range(LRU_WIDTH // GROUP)]
    return jnp.concatenate(parts, axis=1)


def _block_diag_apply_t(db, wbd_ref):
    parts = [lax.dot_general(db[:, g * GROUP:(g + 1) * GROUP], wbd_ref[g * GROUP:(g + 1) * GROUP, :],
                             (((1,), (1,)), ((), ())), preferred_element_type=F32)
             for g in range(LRU_WIDTH // GROUP)]
    return jnp.concatenate(parts, axis=1)


def _dot_nt(a, b):
    return lax.dot_general(a, b, (((1,), (1,)), ((), ())), preferred_element_type=F32)


def _dot_tn(a, b):
    return lax.dot_general(a, b, (((0,), (0,)), ((), ())), preferred_element_type=F32)


CHUNKS_IN_FLIGHT = 4


def _chunk_loop(n_chunks, chunk, init):
    def body(k, carry):
        for j in range(CHUNKS_IN_FLIGHT):
            carry = chunk(k * CHUNKS_IN_FLIGHT + j, carry)
        return carry

    return lax.fori_loop(0, n_chunks // CHUNKS_IN_FLIGHT, body, init)


def _place():
    x, y, c = lax.axis_index("x"), lax.axis_index("y"), lax.axis_index("c")
    return x, y, c


def _block_id(chip, core):
    return 4 * chip[0] + 2 * chip[1] + core


def _other_chips(x, y):
    return [(1 - x, y), (x, 1 - y), (1 - x, 1 - y)]


def _remote_copy(src, dst, send_sem, recv_sem, to):
    return pltpu.make_async_remote_copy(src_ref=src, dst_ref=dst, send_sem=send_sem, recv_sem=recv_sem,
                                        device_id=to, device_id_type=MESH)


HBM_SPEC = pl.BlockSpec(memory_space=pl.ANY)


def _in_hbm(*arrays):
    return [pltpu.with_memory_space_constraint(a, pltpu.HBM) for a in arrays]


def _all_gather_w_in(w_in, w_out, w_mlp_in, w_mlp_out, conv_w, rnn_conv_w):
    n_in = w_in.shape[1]
    n_arr = 2

    def body(win_ref, wout_ref, w1_ref, w2_ref, cw_ref, rw_ref,
             o_win, o_cp, o_wout, o_w1, o_w2, padbuf, send_sems, recv_sems):
        x, y, c = _place()
        me = (x, y, c)
        my_id = _block_id((x, y), c)
        sibling = (x, y, 1 - c)
        chips = _other_chips(x, y)
        outs = [o_win, o_cp]

        padbuf[...] = jnp.zeros(padbuf.shape, F32)
        padbuf[:, 0:n_in] = win_ref[...]
        o_win[my_id] = padbuf[...].T[0:n_in, :].astype(BF16)
        o_cp[my_id] = jnp.zeros(o_cp.shape[1:], F32)
        o_cp[my_id, 0:3, 0:64] = cw_ref[...]
        o_cp[my_id, 3:7, :] = rw_ref[...]

        def copy(arr, k, block, to):
            blk = outs[arr].at[block]
            return _remote_copy(blk, blk, send_sems.at[arr, k], recv_sems.at[arr, k], to)

        first = []
        for arr in range(n_arr):
            first.append(copy(arr, 0, my_id, sibling))
            first += [copy(arr, 1 + j, my_id, (*chip, c)) for j, chip in enumerate(chips)]
        for cp in first:
            cp.start()
        o_wout[...] = wout_ref[...].astype(BF16)
        o_w1[...] = w1_ref[...].astype(BF16)
        o_w2[...] = w2_ref[...].astype(BF16)
        passed = []
        for j, chip in enumerate(chips):
            for arr in range(n_arr):
                copy(arr, 1 + j, _block_id(chip, c), me).wait_recv()
                fwd = copy(arr, 4 + j, _block_id(chip, c), sibling)
                fwd.start()
                passed.append(fwd)
        for arr in range(n_arr):
            copy(arr, 0, _block_id((x, y), 1 - c), me).wait_recv()
            for j, chip in enumerate(chips):
                copy(arr, 4 + j, _block_id(chip, 1 - c), me).wait_recv()
        for cp in first + passed:
            cp.wait_send()

    vm = pl.BlockSpec(memory_space=pltpu.VMEM)
    shapes = (
        jax.ShapeDtypeStruct((N_DEV, n_in, D_MODEL), BF16),
        jax.ShapeDtypeStruct((N_DEV, 8, 128), F32),
        jax.ShapeDtypeStruct(w_out.shape, BF16),
        jax.ShapeDtypeStruct(w_mlp_in.shape, BF16),
        jax.ShapeDtypeStruct(w_mlp_out.shape, BF16),
    )
    return pl.pallas_call(
        body, out_shape=shapes, in_specs=[vm] * 6, out_specs=[vm] * 5,
        scratch_shapes=[pltpu.VMEM((D_MODEL, 512), F32),
                        pltpu.SemaphoreType.DMA((n_arr, 7)), pltpu.SemaphoreType.DMA((n_arr, 7))],
        compiler_params=_params(vmem_mib=40), name="all_gather_w_in",
    )(w_in, w_out, w_mlp_in, w_mlp_out, conv_w, rnn_conv_w)


def _host_all_gather(step, n_steps, shards, fulls, send_sems, recv_sems, local_sems):
    x, y, c = _place()
    me = (x, y, c)
    my_id = _block_id((x, y), c)
    sibling = (x, y, 1 - c)
    chips = _other_chips(x, y)
    n_arr = len(shards)

    def copy(arr, k, block, to, src=None):
        dst = fulls[arr].at[block]
        return _remote_copy(dst if src is None else src, dst, send_sems.at[arr, k], recv_sems.at[arr, k], to)

    def local(arr):
        return pltpu.make_async_copy(shards[arr], fulls[arr].at[my_id], local_sems.at[arr])

    @pl.when(step == 0)
    def _():
        for arr in range(n_arr):
            local(arr).start()
            copy(arr, 0, my_id, sibling, shards[arr]).start()
            for j, chip in enumerate(chips):
                copy(arr, 1 + j, my_id, (*chip, c), shards[arr]).start()

    @pl.when(step == max(n_steps - 2, 0))
    def _():
        for j, chip in enumerate(chips):
            for arr in range(n_arr):
                copy(arr, 1 + j, _block_id(chip, c), me).wait_recv()
                copy(arr, 4 + j, _block_id(chip, c), sibling).start()

    @pl.when(step == n_steps - 1)
    def _():
        for arr in range(n_arr):
            copy(arr, 0, _block_id((x, y), 1 - c), me).wait_recv()
            for j, chip in enumerate(chips):
                copy(arr, 4 + j, _block_id(chip, 1 - c), me).wait_recv()
            for k in range(4):
                copy(arr, k, my_id, me, shards[arr]).wait_send()
            for j, chip in enumerate(chips):
                copy(arr, 4 + j, _block_id(chip, c), me).wait_send()
            local(arr).wait()


def _host_pair_exchange(step, n_steps, gs, sibs, send_sems, recv_sems):
    x, y, c = _place()
    sibling = (x, y, 1 - c)
    chips = [(x, y)] + _other_chips(x, y)

    def d2d(arr, q):
        return _remote_copy(gs[arr].at[_block_id(chips[q], 1 - c)], sibs[arr].at[q],
                            send_sems.at[arr, q], recv_sems.at[arr, q], sibling)

    @pl.when(step == 0)
    def _():
        for arr in range(len(gs)):
            for q in (1, 2, 3, 0):
                d2d(arr, q).start()

    @pl.when(step == n_steps - 1)
    def _():
        for arr in range(len(gs)):
            for q in range(4):
                d2d(arr, q).wait()


def _host_chip_exchange(step, n_steps, hsends, hrecvs, send_sems, recv_sems):
    x, y, c = _place()
    chips = _other_chips(x, y)

    def ici(arr, j):
        return _remote_copy(hsends[arr].at[j], hrecvs[arr].at[j], send_sems.at[arr, j], recv_sems.at[arr, j],
                            (*chips[j], c))

    @pl.when(step == 0)
    def _():
        for arr in range(len(hsends)):
            for j in range(3):
                ici(arr, j).start()

    @pl.when(step == n_steps - 1)
    def _():
        for arr in range(len(hsends)):
            for j in range(3):
                ici(arr, j).wait()


def _pair_sum(g, sib, name):
    _, rows, cols = g.shape
    x, y, c = _place()
    slots = jnp.stack([_block_id(chip, c) for chip in [(x, y)] + _other_chips(x, y)]).astype(jnp.int32)

    def body(slots_ref, g_ref, sib_ref, hs_ref, own_ref):
        q = pl.program_id(0)
        both = g_ref[0].astype(F32) + sib_ref[0].astype(F32)

        @pl.when(q == 0)
        def _():
            own_ref[...] = both

        @pl.when(q > 0)
        def _():
            hs_ref[0] = both.astype(BF16)

    block = (1, rows, cols)
    grid_spec = pltpu.PrefetchScalarGridSpec(
        num_scalar_prefetch=1, grid=(4,),
        in_specs=[pl.BlockSpec(block, lambda q, s: (s[q], 0, 0)), pl.BlockSpec(block, lambda q, s: (q, 0, 0))],
        out_specs=[pl.BlockSpec(block, lambda q, s: (jnp.maximum(q - 1, 0), 0, 0)),
                   pl.BlockSpec((rows, cols), lambda q, s: (0, 0))])
    return pl.pallas_call(
        body, grid_spec=grid_spec,
        out_shape=(pltpu.HBM((3, rows, cols), BF16), pltpu.HBM((rows, cols), F32)),
        compiler_params=_params(("arbitrary",), 32), name=name,
    )(slots, *_in_hbm(g, sib))


def _exchange_scratch(n_arr, n_copies):
    return [pltpu.SemaphoreType.DMA((n_arr, n_copies)), pltpu.SemaphoreType.DMA((n_arr, n_copies))]


def _final_exchange(g, vec_m, vec_b, vec_x, wab):
    _, rows, cols = g.shape
    wrows = wab.shape[0] // N_DEV

    def body(g_ref, vm_ref, vb_ref, vx_ref, w_ref, o_ref, o_vec, o_w,
             sib, hsend, hrecv, vpack, vrecv, wrecv, wred,
             d_send, d_recv, i_send, i_recv, v_send, v_recv, w_send, w_recv, b_send, b_recv):
        x, y, c = _place()
        my_id = _block_id((x, y), c)
        sibling = (x, y, 1 - c)
        chips = [(x, y)] + _other_chips(x, y)

        def d2d(q):
            return _remote_copy(g_ref.at[_block_id(chips[q], 1 - c)], sib.at[q], d_send.at[q], d_recv.at[q], sibling)

        def ici(q):
            return _remote_copy(hsend.at[q - 1], hrecv.at[q - 1], i_send.at[q - 1], i_recv.at[q - 1], (*chips[q], c))

        def peer(k):
            return (x ^ ((k >> 2) & 1), y ^ ((k >> 1) & 1), c ^ (k & 1))

        def pid(k):
            p = peer(k)
            return 4 * p[0] + 2 * p[1] + p[2]

        vpack[0:8, :] = vm_ref[...]
        vpack[8:24, :] = vb_ref[...]
        vpack[24:32, :] = vx_ref[...]
        vrecv[my_id] = vpack[...]

        def vcopy(k):
            return pltpu.make_async_remote_copy(
                src_ref=vpack, dst_ref=vrecv.at[my_id], send_sem=v_send.at[k], recv_sem=v_recv.at[k],
                device_id=peer(k), device_id_type=MESH)

        def wcopy(k):
            return pltpu.make_async_remote_copy(
                src_ref=w_ref.at[pl.ds(pl.multiple_of(pid(k) * wrows, SUB), wrows), :], dst_ref=wrecv.at[k],
                send_sem=w_send.at[k], recv_sem=w_recv.at[k], device_id=peer(k), device_id_type=MESH)

        def bcopy(k):
            mine = o_w.at[pl.ds(pl.multiple_of(my_id * wrows, SUB), wrows), :]
            return pltpu.make_async_remote_copy(
                src_ref=wred, dst_ref=mine, send_sem=b_send.at[k], recv_sem=b_recv.at[k],
                device_id=peer(k), device_id_type=MESH)

        for q in (1, 2, 3, 0):
            d2d(q).start()
        for k in range(1, N_DEV):
            vcopy(k).start()
            wcopy(k).start()
        for q in (1, 2, 3):
            d2d(q).wait_recv()
            hsend[q - 1] = (g_ref[_block_id(chips[q], c)].astype(F32) + sib[q].astype(F32)).astype(BF16)
            ici(q).start()
        red = w_ref[pl.ds(pl.multiple_of(my_id * wrows, SUB), wrows), :]
        for k in range(1, N_DEV):
            wcopy(k).wait_recv()
            red = red + wrecv[k]
        wred[...] = red
        o_w[pl.ds(pl.multiple_of(my_id * wrows, SUB), wrows), :] = red
        for k in range(1, N_DEV):
            bcopy(k).start()
        d2d(0).wait_recv()
        acc = g_ref[_block_id(chips[0], c)].astype(F32) + sib[0].astype(F32)
        for k in range(1, N_DEV):
            vcopy(k).wait_recv()
        tot = vrecv[0]
        for s in range(1, N_DEV):
            tot = tot + vrecv[s]
        o_vec[...] = tot
        for q in (1, 2, 3):
            ici(q).wait_recv()
            acc = acc + hrecv[q - 1].astype(F32)
        o_ref[...] = acc
        for k in range(1, N_DEV):
            bcopy(k).wait_recv()
        for k in range(1, N_DEV):
            vcopy(k).wait_send()
            wcopy(k).wait_send()
            bcopy(k).wait_send()
        for q in range(4):
            d2d(q).wait_send()
        for q in (1, 2, 3):
            ici(q).wait_send()

    vm = pl.BlockSpec(memory_space=pltpu.VMEM)
    dma8 = pltpu.SemaphoreType.DMA((N_DEV,))
    dma4 = pltpu.SemaphoreType.DMA((4,))
    dma3 = pltpu.SemaphoreType.DMA((3,))
    return pl.pallas_call(
        body, out_shape=(jax.ShapeDtypeStruct((rows, cols), F32), jax.ShapeDtypeStruct((VEC_ROWS, D_MODEL), F32),
                         jax.ShapeDtypeStruct(wab.shape, F32)),
        in_specs=[vm] * 5, out_specs=[vm] * 3,
        scratch_shapes=[pltpu.VMEM((4, rows, cols), BF16), pltpu.VMEM((3, rows, cols), BF16),
                        pltpu.VMEM((3, rows, cols), BF16),
                        pltpu.VMEM((VEC_ROWS, D_MODEL), F32), pltpu.VMEM((N_DEV, VEC_ROWS, D_MODEL), F32),
                        pltpu.VMEM((N_DEV, wrows, HEAD_DIM), F32), pltpu.VMEM((wrows, HEAD_DIM), F32),
                        dma4, dma4, dma3, dma3, dma8, dma8, dma8, dma8, dma8, dma8],
        compiler_params=_params(vmem_mib=48), name="final_exchange",
    )(g, vec_m, vec_b, vec_x, wab)


def _in_proj(x, g_mix, win_t, wout_shard, tm):
    t_len = x.shape[0]
    n_steps = t_len // tm

    def body(x_ref, g_ref, w_ref, wout_ref, u_ref, h_ref, wout_full, send_sems, recv_sems, local_sems):
        _host_all_gather(pl.program_id(0), n_steps, [wout_ref], [wout_full], send_sems, recv_sems, local_sems)
        xv = x_ref[...]
        h = (xv * _rms(xv) * g_ref[...]).astype(BF16)
        h_ref[...] = h
        u_ref[...] = _dot_nt(h, w_ref[...])

    return pl.pallas_call(
        body, grid=(n_steps,),
        in_specs=[pl.BlockSpec((tm, D_MODEL), lambda i: (i, 0)), pl.BlockSpec((1, D_MODEL), lambda i: (0, 0)),
                  pl.BlockSpec((IN_COLS, D_MODEL), lambda i: (0, 0)), HBM_SPEC],
        out_specs=[pl.BlockSpec((tm, IN_COLS), lambda i: (i, 0)), pl.BlockSpec((tm, D_MODEL), lambda i: (i, 0)),
                   HBM_SPEC],
        out_shape=[jax.ShapeDtypeStruct((t_len, IN_COLS), F32), jax.ShapeDtypeStruct((t_len, D_MODEL), BF16),
                   jax.ShapeDtypeStruct((N_DEV,) + wout_shard.shape, BF16)],
        scratch_shapes=_exchange_scratch(1, 7) + [pltpu.SemaphoreType.DMA((1,))],
        compiler_params=_params(("arbitrary",), 56), name="in_proj",
    )(x, g_mix, win_t, wout_shard)


def _conv3_chunk(u_ref, r, cv_prev, cw, row):
    gb = u_ref[pl.ds(r, SUB), OFF_GB:OFF_GB + CONV_WIDTH]
    gc = u_ref[pl.ds(r, SUB), OFF_GC:OFF_GC + CONV_WIDTH]
    v = u_ref[pl.ds(r, SUB), OFF_V:OFF_V + CONV_WIDTH]
    cv = gc * v
    cv_m1 = _down(cv, cv_prev, 1, row)
    cv_m2 = _down(cv, cv_prev, 2, row)
    cq = cw[2:3, :] * cv + cw[1:2, :] * cv_m1 + cw[0:1, :] * cv_m2
    return gb, gc, v, cv, cv_m1, cv_m2, cq


def _conv4_chunk(u_ref, r, xin_prev, rw, rb, row):
    xin = u_ref[pl.ds(r, SUB), OFF_XR:OFF_XR + LRU_WIDTH]
    m1 = _down(xin, xin_prev, 1, row)
    m2 = _down(xin, xin_prev, 2, row)
    m3 = _down(xin, xin_prev, 3, row)
    xr = rw[3:4, :] * xin + rw[2:3, :] * m1 + rw[1:2, :] * m2 + rw[0:1, :] * m3 + rb
    return xin, m1, m2, m3, xr


def _mixer_fwd(u, conv_w, rnn_conv_w, rnn_conv_b, wa, b_a, wx, b_x, lam, gnc, gnr, w1_shard, w2_shard, tm):
    t_len = u.shape[0]
    n_steps = t_len // tm
    n_chunks = tm // SUB

    def body(u_ref, cw_ref, rw_ref, rb_ref, wa_ref, ba_ref, wx_ref, bx_ref, lam_ref, gnc_ref, gnr_ref,
             w1_shard, w2_shard, hs_ref, y_ref, w1_full, w2_full,
             y_s, xr_s, pa_s, px_s, wabd, wxbd, cv_car, xin_car, h_car, send_sems, recv_sems, local_sems):
        _host_all_gather(pl.program_id(0), n_steps, [w1_shard, w2_shard], [w1_full, w2_full],
                         send_sems, recv_sems, local_sems)

        @pl.when(pl.program_id(0) == 0)
        def _():
            cv_car[...] = jnp.zeros(cv_car.shape, F32)
            xin_car[...] = jnp.zeros(xin_car.shape, F32)
            h_car[...] = jnp.zeros(h_car.shape, F32)
            wabd[...] = _expand_heads(wa_ref[...])
            wxbd[...] = _expand_heads(wx_ref[...])

        row_c = lax.broadcasted_iota(jnp.int32, (SUB, CONV_WIDTH), 0)
        row_r = lax.broadcasted_iota(jnp.int32, (SUB, LRU_WIDTH), 0)
        cw = cw_ref[...]
        rw = rw_ref[...]
        rb = rb_ref[...]
        g_c = gnc_ref[...]
        g_r = gnr_ref[...]
        sp_c = LRU_C * _softplus_neg(lam_ref[...])

        def convs(i, carry):
            cv_prev, xin_prev = carry
            r = pl.multiple_of(i * SUB, SUB)
            gb, _, _, cv, _, _, cq = _conv3_chunk(u_ref, r, cv_prev, cw, row_c)
            y_c = gb * cq
            y_s[pl.ds(r, SUB), 0:CONV_WIDTH] = y_c * _rms(y_c) * g_c
            xin, _, _, _, xr = _conv4_chunk(u_ref, r, xin_prev, rw, rb, row_r)
            xr_s[pl.ds(r, SUB), :] = xr
            return cv, xin

        cv_last, xin_last = _chunk_loop(n_chunks, convs, (cv_car[...], xin_car[...]))
        cv_car[...] = cv_last
        xin_car[...] = xin_last

        xrb = xr_s[...].astype(BF16)
        pa_s[...] = _block_diag_apply(xrb, wabd) + ba_ref[...]
        px_s[...] = _block_diag_apply(xrb, wxbd) + bx_ref[...]

        def recur(i, h_prev):
            r = pl.multiple_of(i * SUB, SUB)
            xr = xr_s[pl.ds(r, SUB), :]
            _, ii, a, mult, _ = _lru_gates(pa_s[pl.ds(r, SUB), :], px_s[pl.ds(r, SUB), :], sp_c)
            a_cum, b_cum = _scan8_fwd(a, mult * ii * xr, row_r)
            h = a_cum * h_prev + b_cum
            hs_ref[pl.ds(r, SUB), :] = h
            ge, _ = _gelu(u_ref[pl.ds(r, SUB), OFF_G:OFF_G + LRU_WIDTH])
            y_r = h * ge
            y_s[pl.ds(r, SUB), CONV_WIDTH:MIX_WIDTH] = y_r * _rms(y_r) * g_r
            return h[SUB - 1:SUB, :]

        h_car[...] = _chunk_loop(n_chunks, recur, h_car[...])

        y_ref[...] = y_s[...].astype(BF16)

    row_tile = lambda w: pl.BlockSpec((tm, w), lambda i: (i, 0))
    whole = lambda a: pl.BlockSpec(a.shape, lambda i: (0,) * a.ndim)
    smalls = (conv_w, rnn_conv_w, rnn_conv_b, wa, b_a, wx, b_x, lam, gnc, gnr)
    return pl.pallas_call(
        body, grid=(n_steps,),
        in_specs=[row_tile(IN_COLS)] + [whole(a) for a in smalls] + [HBM_SPEC, HBM_SPEC],
        out_specs=[row_tile(LRU_WIDTH), row_tile(MIX_WIDTH), HBM_SPEC, HBM_SPEC],
        out_shape=[jax.ShapeDtypeStruct((t_len, LRU_WIDTH), F32), jax.ShapeDtypeStruct((t_len, MIX_WIDTH), BF16),
                   jax.ShapeDtypeStruct((N_DEV,) + w1_shard.shape, BF16),
                   jax.ShapeDtypeStruct((N_DEV,) + w2_shard.shape, BF16)],
        scratch_shapes=[pltpu.VMEM((tm, MIX_WIDTH), F32), pltpu.VMEM((tm, LRU_WIDTH), F32),
                        pltpu.VMEM((tm, LRU_WIDTH), F32), pltpu.VMEM((tm, LRU_WIDTH), F32),
                        pltpu.VMEM((LRU_WIDTH, GROUP), BF16), pltpu.VMEM((LRU_WIDTH, GROUP), BF16),
                        pltpu.VMEM((SUB, CONV_WIDTH), F32), pltpu.VMEM((SUB, LRU_WIDTH), F32),
                        pltpu.VMEM((1, LRU_WIDTH), F32)] + _exchange_scratch(2, 7) + [pltpu.SemaphoreType.DMA((2,))],
        compiler_params=_params(("arbitrary",), 56), name="mixer_fwd",
    )(u, *smalls, w1_shard, w2_shard)


def _mlp_fwd_bwd(x, y, target, g_mlp, g_f, w_out, w1, w2, tm):
    t_len = x.shape[0]
    n_steps = t_len // tm
    n_blk, _, blk = w1.shape

    def body(x_ref, y_ref, tg_ref, gm_ref, gf_ref, wout_hbm, w1_hbm, w2_hbm,
             dx1_ref, h2_ref, dx2_ref, vec_ref, z_hbm, dpre_hbm,
             wout_s, w1_s, w2_s, rp_s, z_s, dp_s, sem, out_sem):
        step = pl.program_id(0)
        rows = pl.ds(pl.multiple_of(step * tm, tm), tm)
        z_out = pltpu.make_async_copy(z_s, z_hbm.at[rows, :], out_sem.at[0])
        dp_out = pltpu.make_async_copy(dp_s, dpre_hbm.at[rows, :], out_sem.at[1])

        @pl.when(step == 0)
        def _():
            loads = [pltpu.make_async_copy(src, dst, sem.at[k])
                     for k, (src, dst) in enumerate(((wout_hbm, wout_s), (w1_hbm, w1_s), (w2_hbm, w2_s)))]
            for cp in loads:
                cp.start()
            vec_ref[...] = jnp.zeros(vec_ref.shape, F32)
            for cp in loads:
                cp.wait()

        x1v = x_ref[...] + jnp.dot(y_ref[...], wout_s[...], preferred_element_type=F32)
        g_m = gm_ref[...]
        g_o = gf_ref[...]
        r2 = _rms(x1v)
        x1h = x1v * r2
        h2 = (x1h * g_m).astype(BF16)
        h2_ref[...] = h2
        x2 = x1v

        @pl.when(step > 0)
        def _():
            z_out.wait()

        for k in range(n_blk):
            rp = jnp.maximum(jnp.dot(h2, w1_s[k], preferred_element_type=F32), 0.0)
            rp_s[:, k * blk:(k + 1) * blk] = rp.astype(BF16)
            zb = (rp * rp).astype(BF16)
            z_s[:, k * blk:(k + 1) * blk] = zb
            x2 = x2 + jnp.dot(zb, w2_s[k * blk:(k + 1) * blk, :], preferred_element_type=F32)
        z_out.start()
        r3 = _rms(x2)
        x2h = x2 * r3
        err = x2h * g_o - tg_ref[...]
        dout = err * (1.0 / D_MODEL)
        vec_ref[ROW_LOSS:ROW_LOSS + 1, :] += (0.5 / D_MODEL) * jnp.sum(err * err, axis=0, keepdims=True)
        vec_ref[ROW_GF:ROW_GF + 1, :] += jnp.sum(dout * x2h, axis=0, keepdims=True)
        dx2 = _rms_bwd(dout, x2h, r3, g_o)
        dx2b = dx2.astype(BF16)
        dx2_ref[...] = dx2b
        dh2 = jnp.zeros((tm, D_MODEL), F32)

        @pl.when(step > 0)
        def _():
            dp_out.wait()

        for k in range(n_blk):
            dz = _dot_nt(dx2b, w2_s[k * blk:(k + 1) * blk, :])
            dpb = (dz * 2.0 * rp_s[:, k * blk:(k + 1) * blk].astype(F32)).astype(BF16)
            dp_s[:, k * blk:(k + 1) * blk] = dpb
            dh2 = dh2 + _dot_nt(dpb, w1_s[k])
        dp_out.start()
        vec_ref[ROW_GMLP:ROW_GMLP + 1, :] += jnp.sum(dh2 * x1h, axis=0, keepdims=True)
        dx1_ref[...] = dx2 + _rms_bwd(dh2, x1h, r2, g_m)

        @pl.when(step == n_steps - 1)
        def _():
            z_out.wait()
            dp_out.wait()

    row_tile = lambda w: pl.BlockSpec((tm, w), lambda i: (i, 0))
    vec_spec = pl.BlockSpec((1, D_MODEL), lambda i: (0, 0))
    outs = pl.pallas_call(
        body, grid=(n_steps,),
        in_specs=[row_tile(D_MODEL), row_tile(MIX_WIDTH), row_tile(D_MODEL), vec_spec, vec_spec,
                  HBM_SPEC, HBM_SPEC, HBM_SPEC],
        out_specs=[row_tile(D_MODEL), row_tile(D_MODEL), row_tile(D_MODEL),
                   pl.BlockSpec((SUB, D_MODEL), lambda i: (0, 0)), HBM_SPEC, HBM_SPEC],
        out_shape=[jax.ShapeDtypeStruct((t_len, D_MODEL), F32), jax.ShapeDtypeStruct((t_len, D_MODEL), BF16),
                   jax.ShapeDtypeStruct((t_len, D_MODEL), BF16), jax.ShapeDtypeStruct((SUB, D_MODEL), F32),
                   jax.ShapeDtypeStruct((t_len, D_FF), BF16), jax.ShapeDtypeStruct((t_len, D_FF), BF16)],
        scratch_shapes=[pltpu.VMEM(w_out.shape, BF16), pltpu.VMEM(w1.shape, BF16), pltpu.VMEM(w2.shape, BF16),
                        pltpu.VMEM((tm, D_FF), BF16), pltpu.VMEM((tm, D_FF), BF16), pltpu.VMEM((tm, D_FF), BF16),
                        pltpu.SemaphoreType.DMA((3,)), pltpu.SemaphoreType.DMA((2,))],
        compiler_params=_params(("arbitrary",), 58), name="mlp_fwd_bwd",
    )(x, y, target, g_mlp, g_f, w_out, w1, w2)
    dx1, h2, dx2, vec, z, dpre = outs
    return dx1, z, dpre, h2, dx2, vec


def _mixer_bwd(u, hs, dx1, conv_w, rnn_conv_w, rnn_conv_b, wa, b_a, wx, b_x, lam, gnc, gnr, w_out,
               chip_sums, g_wout, tm):
    t_len = u.shape[0]
    n_tiles = t_len // tm
    n_chunks = tm // SUB
    per_tile = tm // SUB
    n_sums = len(chip_sums)

    def body(u_ref, up_ref, hs_ref, hp_ref, dx1_ref, cw_ref, rw_ref, rb_ref, wa_ref, ba_ref, wx_ref, bx_ref,
             lam_ref, gnc_ref, gnr_ref, wout_ref, *rest):
        hsends = rest[0:n_sums]
        gwout_ref = rest[n_sums]
        du_ref, vec_ref, wab_ref = rest[n_sums + 1:n_sums + 4]
        hrecvs = rest[n_sums + 4:2 * n_sums + 4]
        sib_wout = rest[2 * n_sums + 4]
        (du_s, dy_s, xr_s, pa_s, px_s, dpa_s, dpx_s, dxr_s, wabd, wxbd, acc, dwa_acc, dwx_acc,
         a_car, dh_car, dcq_car, dxr_car, i_send, i_recv, d_send, d_recv) = rest[2 * n_sums + 5:]
        step = pl.program_id(0)
        _host_chip_exchange(step, n_tiles, hsends, hrecvs, i_send, i_recv)
        _host_pair_exchange(step, n_tiles, [gwout_ref], [sib_wout], d_send, d_recv)
        has_prev = (step < n_tiles - 1).astype(F32)

        @pl.when(step == 0)
        def _():
            acc[...] = jnp.zeros(acc.shape, F32)
            dwa_acc[...] = jnp.zeros(dwa_acc.shape, F32)
            dwx_acc[...] = jnp.zeros(dwx_acc.shape, F32)
            a_car[...] = jnp.ones(a_car.shape, F32)
            dh_car[...] = jnp.zeros(dh_car.shape, F32)
            dcq_car[...] = jnp.zeros(dcq_car.shape, F32)
            dxr_car[...] = jnp.zeros(dxr_car.shape, F32)
            wabd[...] = _expand_heads(wa_ref[...])
            wxbd[...] = _expand_heads(wx_ref[...])

        row_c = lax.broadcasted_iota(jnp.int32, (SUB, CONV_WIDTH), 0)
        row_r = lax.broadcasted_iota(jnp.int32, (SUB, LRU_WIDTH), 0)
        cw = cw_ref[...]
        rw = rw_ref[...]
        rb = rb_ref[...]
        g_c = gnc_ref[...]
        g_r = gnr_ref[...]
        sp_c = LRU_C * _softplus_neg(lam_ref[...])

        up = up_ref[...] * has_prev
        cv_before = up[:, OFF_GC:OFF_GC + CONV_WIDTH] * up[:, OFF_V:OFF_V + CONV_WIDTH]
        xin_before = up[:, OFF_XR:OFF_XR + LRU_WIDTH]
        hs_before = hp_ref[...] * has_prev

        dy_s[...] = _dot_nt(dx1_ref[...].astype(BF16), wout_ref[...])

        def conv4_fwd(i, xin_prev):
            r = pl.multiple_of(i * SUB, SUB)
            xin, _, _, _, xr = _conv4_chunk(u_ref, r, xin_prev, rw, rb, row_r)
            xr_s[pl.ds(r, SUB), :] = xr
            return xin

        _chunk_loop(n_chunks, conv4_fwd, xin_before)
        xrb = xr_s[...].astype(BF16)
        pa_s[...] = _block_diag_apply(xrb, wabd) + ba_ref[...]
        px_s[...] = _block_diag_apply(xrb, wxbd) + bx_ref[...]

        def recur_bwd(j, carry):
            a_later, dh_later = carry
            i = n_chunks - 1 - j
            r = pl.multiple_of(i * SUB, SUB)
            rp = pl.multiple_of(jnp.maximum(i - 1, 0) * SUB, SUB)
            xr = xr_s[pl.ds(r, SUB), :]
            hs_c = hs_ref[pl.ds(r, SUB), :]
            hs_prev = jnp.where(i == 0, hs_before, hs_ref[pl.ds(rp, SUB), :])
            h_m1 = _down(hs_c, hs_prev, 1, row_r)
            ra, ii, a, mult, inv_mult = _lru_gates(pa_s[pl.ds(r, SUB), :], px_s[pl.ds(r, SUB), :], sp_c)
            ge, dge = _gelu(u_ref[pl.ds(r, SUB), OFF_G:OFF_G + LRU_WIDTH])
            y_r = hs_c * ge
            rr = _rms(y_r)
            yhat = y_r * rr
            dyn = dy_s[pl.ds(r, SUB), CONV_WIDTH:MIX_WIDTH]
            acc[ACC_GNR] += dyn * yhat
            dy_r = _rms_bwd(dyn, yhat, rr, g_r)
            du_s[pl.ds(r, SUB), OFF_G:OFF_G + LRU_WIDTH] = dy_r * hs_c * dge
            a_cum, d_cum = _scan8_rev(_up(a, a_later, 1, row_r), dy_r * ge, row_r)
            dh = a_cum * dh_later + d_cum
            dmult = dh * ii * xr
            dii = dh * mult * xr
            dxr_s[pl.ds(r, SUB), :] = dh * mult * ii
            dla = dh * h_m1 * a - dmult * a * a * inv_mult
            acc[ACC_SP] += -dla * ra
            dpa = -dla * sp_c * ra * (1.0 - ra)
            dpx = dii * ii * (1.0 - ii)
            acc[ACC_BA] += dpa
            acc[ACC_BX] += dpx
            dpa_s[pl.ds(r, SUB), :] = dpa
            dpx_s[pl.ds(r, SUB), :] = dpx
            return a, dh[0:1, :]

        a_first, dh_first = _chunk_loop(n_chunks, recur_bwd, (a_car[...], dh_car[...]))
        a_car[...] = a_first
        dh_car[...] = dh_first

        dpab = dpa_s[...].astype(BF16)
        dpxb = dpx_s[...].astype(BF16)
        dxr_s[...] += _block_diag_apply_t(dpab, wabd) + _block_diag_apply_t(dpxb, wxbd)
        for g in range(LRU_WIDTH // GROUP):
            cols = slice(g * GROUP, (g + 1) * GROUP)
            dwa_acc[cols, :] += _dot_tn(xrb[:, cols], dpab[:, cols])
            dwx_acc[cols, :] += _dot_tn(xrb[:, cols], dpxb[:, cols])

        def convs_bwd(j, carry):
            dcq_later, dxr_later = carry
            i = n_chunks - 1 - j
            r = pl.multiple_of(i * SUB, SUB)
            rp = pl.multiple_of(jnp.maximum(i - 1, 0) * SUB, SUB)
            cv_prev = jnp.where(i == 0, cv_before,
                                u_ref[pl.ds(rp, SUB), OFF_GC:OFF_GC + CONV_WIDTH]
                                * u_ref[pl.ds(rp, SUB), OFF_V:OFF_V + CONV_WIDTH])
            gb, gc, v, cv, cv_m1, cv_m2, cq = _conv3_chunk(u_ref, r, cv_prev, cw, row_c)
            y_c = gb * cq
            rc = _rms(y_c)
            yhat = y_c * rc
            dyn = dy_s[pl.ds(r, SUB), 0:CONV_WIDTH]
            acc[ACC_GNC, :, 0:CONV_WIDTH] += dyn * yhat
            dy_c = _rms_bwd(dyn, yhat, rc, g_c)
            dcq = dy_c * gb
            dcv = (cw[2:3, :] * dcq + cw[1:2, :] * _up(dcq, dcq_later, 1, row_c)
                   + cw[0:1, :] * _up(dcq, dcq_later, 2, row_c))
            acc[ACC_CW + 2, :, 0:CONV_WIDTH] += dcq * cv
            acc[ACC_CW + 1, :, 0:CONV_WIDTH] += dcq * cv_m1
            acc[ACC_CW + 0, :, 0:CONV_WIDTH] += dcq * cv_m2
            du_s[pl.ds(r, SUB), OFF_GB:OFF_GB + CONV_WIDTH] = dy_c * cq
            du_s[pl.ds(r, SUB), OFF_GC:OFF_GC + CONV_WIDTH] = dcv * v
            du_s[pl.ds(r, SUB), OFF_V:OFF_V + CONV_WIDTH] = dcv * gc

            xin_prev = jnp.where(i == 0, xin_before, u_ref[pl.ds(rp, SUB), OFF_XR:OFF_XR + LRU_WIDTH])
            xin, m1, m2, m3, _ = _conv4_chunk(u_ref, r, xin_prev, rw, rb, row_r)
            dxr = dxr_s[pl.ds(r, SUB), :]
            du_s[pl.ds(r, SUB), OFF_XR:OFF_XR + LRU_WIDTH] = (
                rw[3:4, :] * dxr + rw[2:3, :] * _up(dxr, dxr_later, 1, row_r)
                + rw[1:2, :] * _up(dxr, dxr_later, 2, row_r) + rw[0:1, :] * _up(dxr, dxr_later, 3, row_r))
            acc[ACC_RW + 3] += dxr * xin
            acc[ACC_RW + 2] += dxr * m1
            acc[ACC_RW + 1] += dxr * m2
            acc[ACC_RW + 0] += dxr * m3
            acc[ACC_BR] += dxr
            return dcq, dxr

        dcq_first, dxr_first = _chunk_loop(n_chunks, convs_bwd, (dcq_car[...], dxr_car[...]))
        dcq_car[...] = dcq_first
        dxr_car[...] = dxr_first

        du_ref[...] = du_s[...].astype(BF16)

        @pl.when(step == n_tiles - 1)
        def _():
            vec_ref[...] = jnp.zeros(vec_ref.shape, F32)
            rows = {ACC_GNC: ROW_GNC, ACC_GNR: ROW_GNR, ACC_BR: ROW_BR, ACC_BA: ROW_BA, ACC_BX: ROW_BX}
            for k in range(3):
                rows[ACC_CW + k] = ROW_CW + k
            for k in range(4):
                rows[ACC_RW + k] = ROW_RW + k
            for slot, out_row in rows.items():
                o = out_row - ROW_GNC
                vec_ref[o:o + 1, :] = jnp.sum(acc[slot], axis=0, keepdims=True)
            lam_v = lam_ref[...]
            dsp = jnp.sum(acc[ACC_SP], axis=0, keepdims=True)
            o = ROW_LAM - ROW_GNC
            vec_ref[o:o + 1, :] = -dsp * LRU_C / (1.0 + jnp.exp(lam_v))
            wab_ref[0:LRU_WIDTH, :] = _fold_heads(dwa_acc[...])
            wab_ref[LRU_WIDTH:2 * LRU_WIDTH, :] = _fold_heads(dwx_acc[...])

    rev = lambda w: pl.BlockSpec((tm, w), lambda s: (n_tiles - 1 - s, 0))
    before = lambda w: pl.BlockSpec((SUB, w), lambda s: (jnp.maximum((n_tiles - 1 - s) * per_tile - 1, 0), 0))
    whole = lambda a: pl.BlockSpec(a.shape, lambda s: (0,) * a.ndim)
    smalls = (conv_w, rnn_conv_w, rnn_conv_b, wa, b_a, wx, b_x, lam, gnc, gnr, w_out)
    full = lambda w: pltpu.VMEM((tm, w), F32)
    return pl.pallas_call(
        body, grid=(n_tiles,),
        in_specs=[rev(IN_COLS), before(IN_COLS), rev(LRU_WIDTH), before(LRU_WIDTH), rev(D_MODEL)]
        + [whole(a) for a in smalls] + [HBM_SPEC] * (n_sums + 1),
        out_specs=[rev(IN_COLS), pl.BlockSpec((16, D_MODEL), lambda s: (0, 0)),
                   pl.BlockSpec((2 * LRU_WIDTH, HEAD_DIM), lambda s: (0, 0))] + [HBM_SPEC] * (n_sums + 1),
        out_shape=[jax.ShapeDtypeStruct((t_len, IN_COLS), BF16), jax.ShapeDtypeStruct((16, D_MODEL), F32),
                   jax.ShapeDtypeStruct((2 * LRU_WIDTH, HEAD_DIM), F32)]
        + [jax.ShapeDtypeStruct(s.shape, BF16) for s in chip_sums]
        + [jax.ShapeDtypeStruct((4,) + g_wout.shape[1:], BF16)],
        scratch_shapes=[full(IN_COLS), full(MIX_WIDTH), full(LRU_WIDTH), full(LRU_WIDTH), full(LRU_WIDTH),
                        full(LRU_WIDTH), full(LRU_WIDTH), full(LRU_WIDTH),
                        pltpu.VMEM((LRU_WIDTH, GROUP), BF16), pltpu.VMEM((LRU_WIDTH, GROUP), BF16),
                        pltpu.VMEM((N_ACC, SUB, LRU_WIDTH), F32),
                        pltpu.VMEM((LRU_WIDTH, GROUP), F32), pltpu.VMEM((LRU_WIDTH, GROUP), F32),
                        pltpu.VMEM((SUB, LRU_WIDTH), F32), pltpu.VMEM((1, LRU_WIDTH), F32),
                        pltpu.VMEM((SUB, CONV_WIDTH), F32), pltpu.VMEM((SUB, LRU_WIDTH), F32)]
        + _exchange_scratch(n_sums, 3) + _exchange_scratch(1, 4),
        compiler_params=_params(("arbitrary",), 56), name="mixer_bwd",
    )(u, u, hs, hs, dx1, *smalls, *chip_sums, g_wout)


def _in_proj_bwd(du, dx1, x, g_mix, win_t, tm):
    t_len = x.shape[0]

    def body(du_ref, dx1_ref, x_ref, g_ref, w_ref, dx_ref, vec_ref):
        @pl.when(pl.program_id(0) == 0)
        def _():
            vec_ref[...] = jnp.zeros(vec_ref.shape, F32)

        dh = jnp.dot(du_ref[...], w_ref[...], preferred_element_type=F32)
        xv = x_ref[...]
        r1 = _rms(xv)
        xh = xv * r1
        vec_ref[0:1, :] += jnp.sum(dh * xh, axis=0, keepdims=True)
        dx_ref[...] = dx1_ref[...] + _rms_bwd(dh, xh, r1, g_ref[...])

    row_tile = lambda w: pl.BlockSpec((tm, w), lambda i: (i, 0))
    return pl.pallas_call(
        body, grid=(t_len // tm,),
        in_specs=[row_tile(IN_COLS), row_tile(D_MODEL), row_tile(D_MODEL), pl.BlockSpec((1, D_MODEL), lambda i: (0, 0)),
                  pl.BlockSpec((IN_COLS, D_MODEL), lambda i: (0, 0))],
        out_specs=[row_tile(D_MODEL), pl.BlockSpec((SUB, D_MODEL), lambda i: (0, 0))],
        out_shape=[jax.ShapeDtypeStruct((t_len, D_MODEL), F32), jax.ShapeDtypeStruct((SUB, D_MODEL), F32)],
        compiler_params=_params(("arbitrary",), 56), name="in_proj_bwd",
    )(du, dx1, x, g_mix, win_t)


def _tn_weight_grad(a, b, tk, name, pair=(), chip=(), col_blocks=1):
    t_len, m = a.shape
    n = b.shape[1]
    n_steps = t_len // tk
    sent = tuple(pair) + tuple(chip)
    n_sent = len(sent)

    def body(a_ref, b_ref, *rest):
        srcs = rest[0:n_sent]
        o_ref = rest[n_sent]
        dsts = rest[n_sent + 1:2 * n_sent + 1]
        acc = rest[2 * n_sent + 1]
        sems = rest[2 * n_sent + 2:]
        j = pl.program_id(0)
        if pair:
            _host_pair_exchange(j, n_steps, srcs, dsts, *sems)
        if chip:
            _host_chip_exchange(j, n_steps, srcs, dsts, *sems)

        @pl.when(j == 0)
        def _():
            acc[...] = jnp.zeros(acc.shape, F32)

        acc[...] += _dot_tn(a_ref[...].astype(BF16), b_ref[...].astype(BF16))

        @pl.when(j == n_steps - 1)
        def _():
            if col_blocks == 1:
                o_ref[...] = acc[...].astype(BF16)
            else:
                for k in range(col_blocks):
                    o_ref[k] = acc[:, k * nb:(k + 1) * nb].astype(BF16)

    nb = n // col_blocks
    out_dims = (m, n) if col_blocks == 1 else (col_blocks, m, nb)
    landed = [jax.ShapeDtypeStruct((4,) + g.shape[1:], BF16) for g in pair]
    landed += [jax.ShapeDtypeStruct(s.shape, BF16) for s in chip]
    scratch = [pltpu.VMEM((m, n), F32)]
    if n_sent:
        scratch += _exchange_scratch(n_sent, 4 if pair else 3)
    return pl.pallas_call(
        body, grid=(n_steps,),
        in_specs=[pl.BlockSpec((tk, m), lambda j: (j, 0)), pl.BlockSpec((tk, n), lambda j: (j, 0))]
        + [HBM_SPEC] * n_sent,
        out_specs=[pl.BlockSpec(out_dims, lambda j: (0,) * len(out_dims))] + [HBM_SPEC] * n_sent,
        out_shape=[jax.ShapeDtypeStruct(out_dims, BF16)] + landed,
        scratch_shapes=scratch,
        compiler_params=_params(("arbitrary",), 56), name=name,
    )(a, b, *sent)


def _adamw(w, g, m, v):
    m = ADAM_B1 * m + (1.0 - ADAM_B1) * g
    v = ADAM_B2 * v + (1.0 - ADAM_B2) * (g * g)
    delta = -ADAM_LR * ((m / BC1) / (jnp.sqrt(v / BC2) + ADAM_EPS) + ADAM_WD * w)
    return delta, m, v


def _update_sharded(g, w, m, v, rows_blk, name, landed=None, transposed=False):
    rows, cols = w.shape
    pad_cols = -(-cols // 128) * 128

    def body(g_ref, *rest):
        if landed is not None:
            l_ref, rest = rest[0], rest[1:]
        w_ref, m_ref, v_ref, og, od, om, ov = rest[0:7]
        if transposed:
            padbuf, turned = rest[7:]
            padbuf[...] = jnp.zeros(padbuf.shape, F32)
            padbuf[0:cols, :] = g_ref[...]
            turned[...] = padbuf[...].T
            gv = turned[:, 0:cols]
        else:
            gv = g_ref[...]
        if landed is not None:
            for j in range(3):
                gv = gv + l_ref[j].astype(F32)
        delta, mn, vn = _adamw(w_ref[...], gv, m_ref[...], v_ref[...])
        og[...] = gv
        od[...] = delta
        om[...] = mn
        ov[...] = vn

    blk = pl.BlockSpec((rows_blk, cols), lambda i: (i, 0))
    g_spec = pl.BlockSpec((cols, rows_blk), lambda i: (0, i)) if transposed else blk
    extra_specs = [] if landed is None else [pl.BlockSpec((3, rows_blk, cols), lambda i: (0, i, 0))]
    extra_args = [] if landed is None else [landed]
    shape = pltpu.HBM((rows, cols), F32)
    return pl.pallas_call(
        body, grid=(rows // rows_blk,), in_specs=[g_spec] + extra_specs + [blk, blk, blk], out_specs=[blk] * 4,
        out_shape=[shape] * 4,
        scratch_shapes=[pltpu.VMEM((pad_cols, rows_blk), F32), pltpu.VMEM((rows_blk, pad_cols), F32)] if transposed else [],
        compiler_params=_params(("arbitrary",), 32), name=name,
    )(*_in_hbm(g, *extra_args, w, m, v))


def _update_small(vsum, wsum, g_cw, g_rw, weights, moments_m, moments_v):
    n = len(weights)

    def body(*refs):
        vs, ws, gcw, grw = refs[0:4]
        w_refs = refs[4:4 + n]
        m_refs = refs[4 + n:4 + 2 * n]
        v_refs = refs[4 + 2 * n:4 + 3 * n]
        outs = refs[4 + 3 * n:]
        loss_ref = outs[0]
        loss_ref[...] = jnp.sum(vs[ROW_LOSS:ROW_LOSS + 1, :], axis=1, keepdims=True)
        grads = [
            vs[ROW_GMIX:ROW_GMIX + 1, :], gcw[...], grw[...], vs[ROW_BR:ROW_BR + 1, :],
            ws[0:LRU_WIDTH, :], vs[ROW_BA:ROW_BA + 1, :], ws[LRU_WIDTH:2 * LRU_WIDTH, :], vs[ROW_BX:ROW_BX + 1, :],
            vs[ROW_LAM:ROW_LAM + 1, :], vs[ROW_GNC:ROW_GNC + 1, 0:CONV_WIDTH], vs[ROW_GNR:ROW_GNR + 1, :],
            vs[ROW_GMLP:ROW_GMLP + 1, :], vs[ROW_GF:ROW_GF + 1, :],
        ]
        for k in range(n):
            gk = grads[k]
            delta, mn, vn = _adamw(w_refs[k][...], gk, m_refs[k][...], v_refs[k][...])
            outs[1 + 4 * k][...] = gk
            outs[2 + 4 * k][...] = delta
            outs[3 + 4 * k][...] = mn
            outs[4 + 4 * k][...] = vn

    whole = lambda a: pl.BlockSpec(a.shape, lambda i: (0,) * len(a.shape))
    out_shape = [jax.ShapeDtypeStruct((1, 1), F32)]
    for w in weights:
        out_shape += [jax.ShapeDtypeStruct(w.shape, F32)] * 4
    args = (vsum, wsum, g_cw, g_rw, *weights, *moments_m, *moments_v)
    return pl.pallas_call(
        body, grid=(1,), out_shape=out_shape, in_specs=[whole(a) for a in args], out_specs=[whole(s) for s in out_shape],
        compiler_params=_params(("arbitrary",), 32), name="update_small",
    )(*args)


def kernel(x, norm_mix_g, w_in, conv_w, rnn_conv_w, rnn_conv_b, w_a, b_a, w_x, b_x, lru_lambda, g_norm_conv, g_norm_rnn, w_out, norm_mlp_g, w_mlp_in, w_mlp_out, final_norm_g, loss_target, m_norm_mix_g, m_w_in, m_conv_w, m_rnn_conv_w, m_rnn_conv_b, m_w_a, m_b_a, m_w_x, m_b_x, m_lru_lambda, m_g_norm_conv, m_g_norm_rnn, m_w_out, m_norm_mlp_g, m_w_mlp_in, m_w_mlp_out, m_final_norm_g, v_norm_mix_g, v_w_in, v_conv_w, v_rnn_conv_w, v_rnn_conv_b, v_w_a, v_b_a, v_w_x, v_b_x, v_lru_lambda, v_g_norm_conv, v_g_norm_rnn, v_w_out, v_norm_mlp_g, v_w_mlp_in, v_w_mlp_out, v_final_norm_g):
    t_len = x.shape[1]
    my_id = 4 * lax.axis_index("x") + 2 * lax.axis_index("y") + lax.axis_index("c")
    tm = min(256, t_len)
    tb = min(512, t_len)
    tk = min(512, t_len)

    xs = x.reshape(t_len, D_MODEL)
    tgt = loss_target.reshape(t_len, D_MODEL)
    flat = lambda a: a.reshape(a.shape[-2:]) if a.ndim == 3 else a.reshape(1, -1)
    heads = lambda a: a.reshape(LRU_WIDTH, HEAD_DIM)

    win_blk, cpack, wout_shard, w1_shard, w2_shard = _all_gather_w_in(
        flat(w_in), flat(w_out), flat(w_mlp_in), flat(w_mlp_out), flat(conv_w), flat(rnn_conv_w))
    win_t = win_blk.reshape(IN_COLS, D_MODEL)
    conv_full = jnp.transpose(cpack[:, 0:3, 0:64], (1, 0, 2)).reshape(3, CONV_WIDTH)
    rnn_full = jnp.transpose(cpack[:, 3:7, :], (1, 0, 2)).reshape(4, LRU_WIDTH)
    mixer_small = (conv_full, rnn_full, flat(rnn_conv_b), heads(w_a), flat(b_a), heads(w_x), flat(b_x),
                   flat(lru_lambda), flat(g_norm_conv), flat(g_norm_rnn))

    u, h, wout_blk = _in_proj(xs, flat(norm_mix_g), win_t, wout_shard, tb)
    wout_f = wout_blk.reshape(MIX_WIDTH, D_MODEL)
    hs, y, w1_blk, w2_blk = _mixer_fwd(u, *mixer_small, w1_shard, w2_shard, tm)
    dx1, z, dpre, h2, dx2, vec_m = _mlp_fwd_bwd(xs, y, tgt, flat(norm_mlp_g), flat(final_norm_g), wout_f, w1_blk,
                                                w2_blk.reshape(D_FF, D_MODEL), tb)
    (g_w1,) = _tn_weight_grad(h2, dpre, tk, "w_mlp_in_grad", col_blocks=N_DEV)
    (g_w2,) = _tn_weight_grad(z, dx2, tk, "w_mlp_out_grad")
    g_w2 = g_w2.reshape(N_DEV, D_FF // N_DEV, D_MODEL)
    g_wout, sib_w1, sib_w2 = _tn_weight_grad(y, dx1, tk, "w_out_grad", pair=(g_w1, g_w2))
    g_wout = g_wout.reshape(N_DEV, MIX_WIDTH // N_DEV, D_MODEL)
    hsend_w1, own_w1 = _pair_sum(g_w1, sib_w1, "pair_sum_w_mlp_in")
    hsend_w2, own_w2 = _pair_sum(g_w2, sib_w2, "pair_sum_w_mlp_out")
    du, vec_b, wab, landed_w1, landed_w2, sib_wout = _mixer_bwd(
        u, hs, dx1, *mixer_small, wout_f, (hsend_w1, hsend_w2), g_wout, tm)
    hsend_wout, own_wout = _pair_sum(g_wout, sib_wout, "pair_sum_w_out")
    grad_x, vec_x = _in_proj_bwd(du, dx1, xs, flat(norm_mix_g), win_t, tm)
    g_win_t, landed_wout = _tn_weight_grad(du, h, tk, "w_in_grad", chip=(hsend_wout,))

    r_win_t, vsum, wsum = _final_exchange(g_win_t.reshape(N_DEV, IN_COLS // N_DEV, D_MODEL), vec_m, vec_b, vec_x, wab)

    up_win = _update_sharded(r_win_t, flat(w_in), flat(m_w_in), flat(v_w_in), 256, "update_w_in", transposed=True)
    up_wout = _update_sharded(own_wout, flat(w_out), flat(m_w_out), flat(v_w_out), 96, "update_w_out",
                              landed=landed_wout)
    up_w1 = _update_sharded(own_w1, flat(w_mlp_in), flat(m_w_mlp_in), flat(v_w_mlp_in), 256, "update_w_mlp_in",
                            landed=landed_w1)
    up_w2 = _update_sharded(own_w2, flat(w_mlp_out), flat(m_w_mlp_out), flat(v_w_mlp_out), 256, "update_w_mlp_out",
                            landed=landed_w2)

    g_cw = lax.dynamic_slice(vsum, (ROW_CW, 64 * my_id), (3, 64))
    g_rw = lax.dynamic_slice(vsum, (ROW_RW, 128 * my_id), (4, 128))
    small_w = (norm_mix_g, conv_w, rnn_conv_w, rnn_conv_b, w_a, b_a, w_x, b_x, lru_lambda, g_norm_conv, g_norm_rnn,
               norm_mlp_g, final_norm_g)
    small_m = (m_norm_mix_g, m_conv_w, m_rnn_conv_w, m_rnn_conv_b, m_w_a, m_b_a, m_w_x, m_b_x, m_lru_lambda,
               m_g_norm_conv, m_g_norm_rnn, m_norm_mlp_g, m_final_norm_g)
    small_v = (v_norm_mix_g, v_conv_w, v_rnn_conv_w, v_rnn_conv_b, v_w_a, v_b_a, v_w_x, v_b_x, v_lru_lambda,
               v_g_norm_conv, v_g_norm_rnn, v_norm_mlp_g, v_final_norm_g)
    is_heads = (False, False, False, False, True, False, True, False, False, False, False, False, False)
    as2d = lambda arrs: [heads(a) if hd else flat(a) for a, hd in zip(arrs, is_heads)]
    small_out = _update_small(vsum, wsum, g_cw, g_rw, as2d(small_w), as2d(small_m), as2d(small_v))
    loss = small_out[0].reshape(())

    names = ["norm_mix_g", "w_in", "conv_w", "rnn_conv_w", "rnn_conv_b", "w_a", "b_a", "w_x", "b_x", "lru_lambda",
             "g_norm_conv", "g_norm_rnn", "w_out", "norm_mlp_g", "w_mlp_in", "w_mlp_out", "final_norm_g"]
    originals = dict(zip(names, (norm_mix_g, w_in, conv_w, rnn_conv_w, rnn_conv_b, w_a, b_a, w_x, b_x, lru_lambda,
                                 g_norm_conv, g_norm_rnn, w_out, norm_mlp_g, w_mlp_in, w_mlp_out, final_norm_g)))
    results = {"w_in": up_win, "w_out": up_wout, "w_mlp_in": up_w1, "w_mlp_out": up_w2}
    small_names = ["norm_mix_g", "conv_w", "rnn_conv_w", "rnn_conv_b", "w_a", "b_a", "w_x", "b_x", "lru_lambda",
                   "g_norm_conv", "g_norm_rnn", "norm_mlp_g", "final_norm_g"]
    for k, nm in enumerate(small_names):
        results[nm] = small_out[1 + 4 * k:5 + 4 * k]
    out = [loss, grad_x.reshape(x.shape)]
    for kind in range(4):
        out += [results[nm][kind].reshape(originals[nm].shape) for nm in names]
    return tuple(out)
```

```python
import functools

import jax
import jax.numpy as jnp
from jax import lax
from jax.experimental import pallas as pl
from jax.experimental.pallas import tpu as pltpu

F32 = jnp.float32
BF16 = jnp.bfloat16

D_MODEL = 1024
HEAD_DIM = 64
CONV_WIDTH = 512
LRU_WIDTH = 1024
MIX_WIDTH = CONV_WIDTH + LRU_WIDTH
IN_COLS = 3 * CONV_WIDTH + 2 * LRU_WIDTH
D_FF = 4 * D_MODEL
GROUP = 256
EPS = 1e-6
LRU_C = 8.0
N_DEV = 8
SUB = 8

OFF_GB, OFF_GC, OFF_V, OFF_XR, OFF_G = 0, 512, 1024, 1536, 2560

ADAM_LR, ADAM_B1, ADAM_B2, ADAM_EPS, ADAM_WD, ADAM_STEP = 0.001, 0.9, 0.999, 1e-08, 0.01, 10
BC1 = 1.0 - ADAM_B1 ** ADAM_STEP
BC2 = 1.0 - ADAM_B2 ** ADAM_STEP

MIB = 1024 * 1024
MESH = pl.DeviceIdType.MESH

VEC_ROWS = 32
ROW_GF, ROW_GMLP, ROW_LOSS = 0, 1, 2
ROW_GNC, ROW_GNR, ROW_BR, ROW_BA, ROW_BX, ROW_LAM, ROW_CW, ROW_RW = 8, 9, 10, 11, 12, 13, 14, 17
ROW_GMIX = 24
ACC_GNC, ACC_GNR, ACC_BR, ACC_BA, ACC_BX, ACC_SP, ACC_CW, ACC_RW, N_ACC = 0, 1, 2, 3, 4, 5, 6, 9, 13


def _params(semantics=None, vmem_mib=48):
    return pltpu.CompilerParams(dimension_semantics=semantics, vmem_limit_bytes=vmem_mib * MIB)


def _rms(x):
    return lax.rsqrt(jnp.mean(x * x, axis=-1, keepdims=True) + EPS)


def _rms_bwd(dy, xhat, r, g):
    dyh = dy * g
    return r * (dyh - xhat * jnp.mean(dyh * xhat, axis=-1, keepdims=True))


def _sigmoid(x):
    return 0.5 + 0.5 * jnp.tanh(0.5 * x)


def _gelu(x):
    c0, c1 = 0.7978845608028654, 0.044715
    t = jnp.tanh(c0 * (x + c1 * x * x * x))
    ge = 0.5 * x * (1.0 + t)
    dge = 0.5 * (1.0 + t) + 0.5 * x * (1.0 - t * t) * c0 * (1.0 + 3.0 * c1 * x * x)
    return ge, dge


def _softplus_neg(lam):
    z = -lam
    e = jnp.exp(-jnp.abs(z))
    return jnp.maximum(z, 0.0) + jnp.where(e < 1e-4, e * (1.0 - 0.5 * e), jnp.log(1.0 + e))


def _lru_gates(pa, px, sp_c):
    ra = _sigmoid(pa)
    ii = _sigmoid(px)
    la = -ra * sp_c
    a = jnp.exp(la)
    x2 = 2.0 * la
    series = -x2 * (1.0 + x2 * (0.5 + x2 * (1.0 / 6.0 + x2 * (1.0 / 24.0))))
    m2 = jnp.where(x2 > -0.01, series, 1.0 - a * a)
    inv_mult = lax.rsqrt(m2)
    mult = jnp.where(m2 > 0.0, m2 * inv_mult, 0.0)
    return ra, ii, a, mult, inv_mult


def _down(cur, prev, s, row):
    return jnp.where(row >= s, pltpu.roll(cur, s, 0), pltpu.roll(prev, s, 0))


def _up(cur, nxt, s, row):
    return jnp.where(row < SUB - s, pltpu.roll(cur, SUB - s, 0), pltpu.roll(nxt, SUB - s, 0))


def _scan8_fwd(a, b, row):
    for s in (1, 2, 4):
        m = row >= s
        a_sh = pltpu.roll(a, s, 0)
        b_sh = pltpu.roll(b, s, 0)
        b = jnp.where(m, a * b_sh + b, b)
        a = jnp.where(m, a * a_sh, a)
    return a, b


def _scan8_rev(a, b, row):
    for s in (1, 2, 4):
        m = row < SUB - s
        a_sh = pltpu.roll(a, SUB - s, 0)
        b_sh = pltpu.roll(b, SUB - s, 0)
        b = jnp.where(m, a * b_sh + b, b)
        a = jnp.where(m, a * a_sh, a)
    return a, b


def _group_mask(shape):
    r = lax.broadcasted_iota(jnp.int32, shape, 0)
    c = lax.broadcasted_iota(jnp.int32, shape, 1)
    return ((r % GROUP) // HEAD_DIM) == (c // HEAD_DIM)


def _expand_heads(w):
    j = lax.broadcasted_iota(jnp.int32, (HEAD_DIM, GROUP), 0)
    c = lax.broadcasted_iota(jnp.int32, (HEAD_DIM, GROUP), 1)
    spread = (c % HEAD_DIM == j).astype(BF16)
    e = jnp.dot(w.astype(BF16), spread, preferred_element_type=F32)
    return jnp.where(_group_mask(e.shape), e, 0.0).astype(BF16)


def _fold_heads(p):
    p = jnp.where(_group_mask(p.shape), p, 0.0)
    c = lax.broadcasted_iota(jnp.int32, (GROUP, HEAD_DIM), 0)
    j = lax.broadcasted_iota(jnp.int32, (GROUP, HEAD_DIM), 1)
    fold = (c % HEAD_DIM == j).astype(BF16)
    hi = p.astype(BF16)
    rest = p - hi.astype(F32)
    mid = rest.astype(BF16)
    lo = (rest - mid.astype(F32)).astype(BF16)
    dot = functools.partial(jnp.dot, preferred_element_type=F32)
    return dot(hi, fold) + dot(mid, fold) + dot(lo, fold)


def _block_diag_apply(xb, wbd_ref):
    parts = [jnp.dot(xb[:, g * GROUP:(g + 1) * GROUP], wbd_ref[g * GROUP:(g + 1) * GROUP, :],
                     preferred_element_type=F32) for g in range(LRU_WIDTH // GROUP)]
    return jnp.concatenate(parts, axis=1)


def _block_diag_apply_t(db, wbd_ref):
    parts = [lax.dot_general(db[:, g * GROUP:(g + 1) * GROUP], wbd_ref[g * GROUP:(g + 1) * GROUP, :],
                             (((1,), (1,)), ((), ())), preferred_element_type=F32)
             for g in range(LRU_WIDTH // GROUP)]
    return jnp.concatenate(parts, axis=1)


def _dot_nt(a, b):
    return lax.dot_general(a, b, (((1,), (1,)), ((), ())), preferred_element_type=F32)


def _dot_tn(a, b):
    return lax.dot_general(a, b, (((0,), (0,)), ((), ())), preferred_element_type=F32)


CHUNKS_IN_FLIGHT = 4


def _chunk_loop(n_chunks, chunk, init):
    def body(k, carry):
        for j in range(CHUNKS_IN_FLIGHT):
            carry = chunk(k * CHUNKS_IN_FLIGHT + j, carry)
        return carry

    return lax.fori_loop(0, n_chunks // CHUNKS_IN_FLIGHT, body, init)


def _place():
    x, y, c = lax.axis_index("x"), lax.axis_index("y"), lax.axis_index("c")
    return x, y, c


def _block_id(chip, core):
    return 4 * chip[0] + 2 * chip[1] + core


def _other_chips(x, y):
    return [(1 - x, y), (x, 1 - y), (1 - x, 1 - y)]


def _remote_copy(src, dst, send_sem, recv_sem, to):
    return pltpu.make_async_remote_copy(src_ref=src, dst_ref=dst, send_sem=send_sem, recv_sem=recv_sem,
                                        device_id=to, device_id_type=MESH)


HBM_SPEC = pl.BlockSpec(memory_space=pl.ANY)


def _in_hbm(*arrays):
    return [pltpu.with_memory_space_constraint(a, pltpu.HBM) for a in arrays]


def _all_gather_w_in(w_in, w_out, w_mlp_in, w_mlp_out, conv_w, rnn_conv_w):
    n_in = w_in.shape[1]
    n_arr = 2

    def body(win_ref, wout_ref, w1_ref, w2_ref, cw_ref, rw_ref,
             o_win, o_cp, o_wout, o_w1, o_w2, padbuf, send_sems, recv_sems):
        x, y, c = _place()
        me = (x, y, c)
        my_id = _block_id((x, y), c)
        sibling = (x, y, 1 - c)
        chips = _other_chips(x, y)
        outs = [o_win, o_cp]

        padbuf[...] = jnp.zeros(padbuf.shape, F32)
        padbuf[:, 0:n_in] = win_ref[...]
        o_win[my_id] = padbuf[...].T[0:n_in, :].astype(BF16)
        o_cp[my_id] = jnp.zeros(o_cp.shape[1:], F32)
        o_cp[my_id, 0:3, 0:64] = cw_ref[...]
        o_cp[my_id, 3:7, :] = rw_ref[...]

        def copy(arr, k, block, to):
            blk = outs[arr].at[block]
            return _remote_copy(blk, blk, send_sems.at[arr, k], recv_sems.at[arr, k], to)

        first = []
        for arr in range(n_arr):
            first.append(copy(arr, 0, my_id, sibling))
            first += [copy(arr, 1 + j, my_id, (*chip, c)) for j, chip in enumerate(chips)]
        for cp in first:
            cp.start()
        o_wout[...] = wout_ref[...].astype(BF16)
        o_w1[...] = w1_ref[...].astype(BF16)
        o_w2[...] = w2_ref[...].astype(BF16)
        passed = []
        for j, chip in enumerate(chips):
            for arr in range(n_arr):
                copy(arr, 1 + j, _block_id(chip, c), me).wait_recv()
                fwd = copy(arr, 4 + j, _block_id(chip, c), sibling)
                fwd.start()
                passed.append(fwd)
        for arr in range(n_arr):
            copy(arr, 0, _block_id((x, y), 1 - c), me).wait_recv()
            for j, chip in enumerate(chips):
                copy(arr, 4 + j, _block_id(chip, 1 - c), me).wait_recv()
        for cp in first + passed:
            cp.wait_send()

    vm = pl.BlockSpec(memory_space=pltpu.VMEM)
    shapes = (
        jax.ShapeDtypeStruct((N_DEV, n_in, D_MODEL), BF16),
        jax.ShapeDtypeStruct((N_DEV, 8, 128), F32),
        jax.ShapeDtypeStruct(w_out.shape, BF16),
        jax.ShapeDtypeStruct(w_mlp_in.shape, BF16),
        jax.ShapeDtypeStruct(w_mlp_out.shape, BF16),
    )
    return pl.pallas_call(
        body, out_shape=shapes, in_specs=[vm] * 6, out_specs=[vm] * 5,
        scratch_shapes=[pltpu.VMEM((D_MODEL, 512), F32),
                        pltpu.SemaphoreType.DMA((n_arr, 7)), pltpu.SemaphoreType.DMA((n_arr, 7))],
        compiler_params=_params(vmem_mib=40), name="all_gather_w_in",
    )(w_in, w_out, w_mlp_in, w_mlp_out, conv_w, rnn_conv_w)


def _host_all_gather(step, n_steps, shards, fulls, send_sems, recv_sems, local_sems):
    x, y, c = _place()
    me = (x, y, c)
    my_id = _block_id((x, y), c)
    sibling = (x, y, 1 - c)
    chips = _other_chips(x, y)
    n_arr = len(shards)

    def copy(arr, k, block, to, src=None):
        dst = fulls[arr].at[block]
        return _remote_copy(dst if src is None else src, dst, send_sems.at[arr, k], recv_sems.at[arr, k], to)

    def local(arr):
        return pltpu.make_async_copy(shards[arr], fulls[arr].at[my_id], local_sems.at[arr])

    @pl.when(step == 0)
    def _():
        for arr in range(n_arr):
            local(arr).start()
            copy(arr, 0, my_id, sibling, shards[arr]).start()
            for j, chip in enumerate(chips):
                copy(arr, 1 + j, my_id, (*chip, c), shards[arr]).start()

    @pl.when(step == max(n_steps - 2, 0))
    def _():
        for j, chip in enumerate(chips):
            for arr in range(n_arr):
                copy(arr, 1 + j, _block_id(chip, c), me).wait_recv()
                copy(arr, 4 + j, _block_id(chip, c), sibling).start()

    @pl.when(step == n_steps - 1)
    def _():
        for arr in range(n_arr):
            copy(arr, 0, _block_id((x, y), 1 - c), me).wait_recv()
            for j, chip in enumerate(chips):
                copy(arr, 4 + j, _block_id(chip, 1 - c), me).wait_recv()
            for k in range(4):
                copy(arr, k, my_id, me, shards[arr]).wait_send()
            for j, chip in enumerate(chips):
                copy(arr, 4 + j, _block_id(chip, c), me).wait_send()
            local(arr).wait()


def _host_pair_exchange(step, n_steps, gs, sibs, send_sems, recv_sems):
    x, y, c = _place()
    sibling = (x, y, 1 - c)
    chips = [(x, y)] + _other_chips(x, y)

    def d2d(arr, q):
        return _remote_copy(gs[arr].at[_block_id(chips[q], 1 - c)], sibs[arr].at[q],
                            send_sems.at[arr, q], recv_sems.at[arr, q], sibling)

    @pl.when(step == 0)
    def _():
        for arr in range(len(gs)):
            for q in (1, 2, 3, 0):
                d2d(arr, q).start()

    @pl.when(step == n_steps - 1)
    def _():
        for arr in range(len(gs)):
            for q in range(4):
                d2d(arr, q).wait()


def _host_chip_exchange(step, n_steps, hsends, hrecvs, send_sems, recv_sems):
    x, y, c = _place()
    chips = _other_chips(x, y)

    def ici(arr, j):
        return _remote_copy(hsends[arr].at[j], hrecvs[arr].at[j], send_sems.at[arr, j], recv_sems.at[arr, j],
                            (*chips[j], c))

    @pl.when(step == 0)
    def _():
        for arr in range(len(hsends)):
            for j in range(3):
                ici(arr, j).start()

    @pl.when(step == n_steps - 1)
    def _():
        for arr in range(len(hsends)):
            for j in range(3):
                ici(arr, j).wait()


def _host_half_exchange(step, n_steps, parts, sibs, send_sems, recv_sems):
    x, y, c = _place()
    n_q, rows2, _ = parts.shape
    half = rows2 // 2

    def d2d(q):
        src = parts.at[q, pl.ds(pl.multiple_of((1 - c) * half, 16), half), :]
        return _remote_copy(src, sibs.at[q], send_sems.at[q], recv_sems.at[q], (x, y, 1 - c))

    @pl.when(step == 0)
    def _():
        for q in range(n_q):
            d2d(q).start()

    @pl.when(step == n_steps - 1)
    def _():
        for q in range(n_q):
            d2d(q).wait()


def _peer(x, y, c, k):
    return (x ^ ((k >> 2) & 1), y ^ ((k >> 1) & 1), c ^ (k & 1))


def _host_small_exchange(step, n_steps, vec_m, vec_b, wab, vrecv_m, vrecv_b, wrecv, send_sems, recv_sems, local_sems):
    x, y, c = _place()
    my_id = _block_id((x, y), c)
    wrows = wab.shape[0] // N_DEV

    def copies(k):
        to = _peer(x, y, c, k)
        block = wab.at[pl.ds(pl.multiple_of(_block_id(to[0:2], to[2]) * wrows, SUB), wrows), :]
        return [_remote_copy(vec_m, vrecv_m.at[my_id], send_sems.at[0, k], recv_sems.at[0, k], to),
                _remote_copy(vec_b, vrecv_b.at[my_id], send_sems.at[1, k], recv_sems.at[1, k], to),
                _remote_copy(block, wrecv.at[k], send_sems.at[2, k], recv_sems.at[2, k], to)]

    mine = [pltpu.make_async_copy(vec_m, vrecv_m.at[my_id], local_sems.at[0]),
            pltpu.make_async_copy(vec_b, vrecv_b.at[my_id], local_sems.at[1])]

    @pl.when(step == 0)
    def _():
        for cp in mine:
            cp.start()
        for k in range(1, N_DEV):
            for cp in copies(k):
                cp.start()

    @pl.when(step == n_steps - 1)
    def _():
        for k in range(1, N_DEV):
            for cp in copies(k):
                cp.wait()
        for cp in mine:
            cp.wait()


def _pair_sum_parts(parts, sibs, core):
    n_q, rows2, cols = parts.shape
    half = rows2 // 2

    def body(core_ref, g_ref, s_ref, o_ref):
        o_ref[0] = (g_ref[0, 0].astype(F32) + s_ref[0].astype(F32)).astype(BF16)

    block = (1, half, cols)
    grid_spec = pltpu.PrefetchScalarGridSpec(
        num_scalar_prefetch=1, grid=(n_q,),
        in_specs=[pl.BlockSpec((1, 1, half, cols), lambda q, cr: (q, cr[0], 0, 0)),
                  pl.BlockSpec(block, lambda q, cr: (q, 0, 0))],
        out_specs=pl.BlockSpec(block, lambda q, cr: (q, 0, 0)))
    return pl.pallas_call(
        body, grid_spec=grid_spec, out_shape=pltpu.HBM((n_q, half, cols), BF16),
        compiler_params=_params(("arbitrary",), 32), name="pair_sum_w_in",
    )(core, *_in_hbm(parts.reshape(n_q, 2, half, cols), sibs))


def _pair_sum(g, sib, name):
    _, rows, cols = g.shape
    x, y, c = _place()
    slots = jnp.stack([_block_id(chip, c) for chip in [(x, y)] + _other_chips(x, y)]).astype(jnp.int32)

    def body(slots_ref, g_ref, sib_ref, hs_ref, own_ref):
        q = pl.program_id(0)
        both = g_ref[0].astype(F32) + sib_ref[0].astype(F32)

        @pl.when(q == 0)
        def _():
            own_ref[...] = both

        @pl.when(q > 0)
        def _():
            hs_ref[0] = both.astype(BF16)

    block = (1, rows, cols)
    grid_spec = pltpu.PrefetchScalarGridSpec(
        num_scalar_prefetch=1, grid=(4,),
        in_specs=[pl.BlockSpec(block, lambda q, s: (s[q], 0, 0)), pl.BlockSpec(block, lambda q, s: (q, 0, 0))],
        out_specs=[pl.BlockSpec(block, lambda q, s: (jnp.maximum(q - 1, 0), 0, 0)),
                   pl.BlockSpec((rows, cols), lambda q, s: (0, 0))])
    return pl.pallas_call(
        body, grid_spec=grid_spec,
        out_shape=(pltpu.HBM((3, rows, cols), BF16), pltpu.HBM((rows, cols), F32)),
        compiler_params=_params(("arbitrary",), 32), name=name,
    )(slots, *_in_hbm(g, sib))


def _exchange_scratch(n_arr, n_copies):
    return [pltpu.SemaphoreType.DMA((n_arr, n_copies)), pltpu.SemaphoreType.DMA((n_arr, n_copies))]


def _final_small(vrecv_m, vrecv_b, wab, wrecv, vec_x):
    wrows = wab.shape[0] // N_DEV

    def body(vm_ref, vb_ref, w_ref, wr_ref, vx_ref, o_vec, o_w, xrecv, wred, x_send, x_recv, b_send, b_recv):
        x, y, c = _place()
        my_id = _block_id((x, y), c)
        my_rows = pl.ds(pl.multiple_of(my_id * wrows, SUB), wrows)

        def xcopy(k):
            return _remote_copy(vx_ref, xrecv.at[my_id], x_send.at[k], x_recv.at[k], _peer(x, y, c, k))

        def bcopy(k):
            return _remote_copy(wred, o_w.at[my_rows, :], b_send.at[k], b_recv.at[k], _peer(x, y, c, k))

        xrecv[my_id] = vx_ref[...]
        for k in range(1, N_DEV):
            xcopy(k).start()
        red = w_ref[my_rows, :]
        for k in range(1, N_DEV):
            red = red + wr_ref[k]
        wred[...] = red
        o_w[my_rows, :] = red
        for k in range(1, N_DEV):
            bcopy(k).start()
        for k in range(1, N_DEV):
            xcopy(k).wait_recv()
        for rows, ref in ((slice(0, 8), vm_ref), (slice(8, 24), vb_ref), (slice(24, 32), xrecv)):
            tot = ref[0]
            for s in range(1, N_DEV):
                tot = tot + ref[s]
            o_vec[rows, :] = tot
        for k in range(1, N_DEV):
            bcopy(k).wait_recv()
        for k in range(1, N_DEV):
            xcopy(k).wait_send()
            bcopy(k).wait_send()

    vm = pl.BlockSpec(memory_space=pltpu.VMEM)
    dma8 = pltpu.SemaphoreType.DMA((N_DEV,))
    return pl.pallas_call(
        body, out_shape=(jax.ShapeDtypeStruct((VEC_ROWS, D_MODEL), F32), jax.ShapeDtypeStruct(wab.shape, F32)),
        in_specs=[vm] * 5, out_specs=[vm] * 2,
        scratch_shapes=[pltpu.VMEM((N_DEV, SUB, D_MODEL), F32), pltpu.VMEM((wrows, HEAD_DIM), F32),
                        dma8, dma8, dma8, dma8],
        compiler_params=_params(vmem_mib=32), name="final_small",
    )(vrecv_m, vrecv_b, wab, wrecv, vec_x)


def _in_proj(x, g_mix, win_t, wout_shard, tm):
    t_len = x.shape[0]
    n_steps = t_len // tm

    def body(x_ref, g_ref, w_ref, wout_ref, u_ref, h_ref, wout_full, send_sems, recv_sems, local_sems):
        _host_all_gather(pl.program_id(0), n_steps, [wout_ref], [wout_full], send_sems, recv_sems, local_sems)
        xv = x_ref[...]
        h = (xv * _rms(xv) * g_ref[...]).astype(BF16)
        h_ref[...] = h
        u_ref[...] = _dot_nt(h, w_ref[...])

    return pl.pallas_call(
        body, grid=(n_steps,),
        in_specs=[pl.BlockSpec((tm, D_MODEL), lambda i: (i, 0)), pl.BlockSpec((1, D_MODEL), lambda i: (0, 0)),
                  pl.BlockSpec((IN_COLS, D_MODEL), lambda i: (0, 0)), HBM_SPEC],
        out_specs=[pl.BlockSpec((tm, IN_COLS), lambda i: (i, 0)), pl.BlockSpec((tm, D_MODEL), lambda i: (i, 0)),
                   HBM_SPEC],
        out_shape=[jax.ShapeDtypeStruct((t_len, IN_COLS), F32), jax.ShapeDtypeStruct((t_len, D_MODEL), BF16),
                   jax.ShapeDtypeStruct((N_DEV,) + wout_shard.shape, BF16)],
        scratch_shapes=_exchange_scratch(1, 7) + [pltpu.SemaphoreType.DMA((1,))],
        compiler_params=_params(("arbitrary",), 56), name="in_proj",
    )(x, g_mix, win_t, wout_shard)


def _conv3_chunk(u_ref, r, cv_prev, cw, row):
    gb = u_ref[pl.ds(r, SUB), OFF_GB:OFF_GB + CONV_WIDTH]
    gc = u_ref[pl.ds(r, SUB), OFF_GC:OFF_GC + CONV_WIDTH]
    v = u_ref[pl.ds(r, SUB), OFF_V:OFF_V + CONV_WIDTH]
    cv = gc * v
    cv_m1 = _down(cv, cv_prev, 1, row)
    cv_m2 = _down(cv, cv_prev, 2, row)
    cq = cw[2:3, :] * cv + cw[1:2, :] * cv_m1 + cw[0:1, :] * cv_m2
    return gb, gc, v, cv, cv_m1, cv_m2, cq


def _conv4_chunk(u_ref, r, xin_prev, rw, rb, row):
    xin = u_ref[pl.ds(r, SUB), OFF_XR:OFF_XR + LRU_WIDTH]
    m1 = _down(xin, xin_prev, 1, row)
    m2 = _down(xin, xin_prev, 2, row)
    m3 = _down(xin, xin_prev, 3, row)
    xr = rw[3:4, :] * xin + rw[2:3, :] * m1 + rw[1:2, :] * m2 + rw[0:1, :] * m3 + rb
    return xin, m1, m2, m3, xr


def _mixer_fwd(u, conv_w, rnn_conv_w, rnn_conv_b, wa, b_a, wx, b_x, lam, gnc, gnr, w1_shard, w2_shard, tm):
    t_len = u.shape[0]
    n_steps = t_len // tm
    n_chunks = tm // SUB

    def body(u_ref, cw_ref, rw_ref, rb_ref, wa_ref, ba_ref, wx_ref, bx_ref, lam_ref, gnc_ref, gnr_ref,
             w1_shard, w2_shard, hs_ref, y_ref, w1_full, w2_full,
             y_s, xr_s, pa_s, px_s, wabd, wxbd, cv_car, xin_car, h_car, send_sems, recv_sems, local_sems):
        _host_all_gather(pl.program_id(0), n_steps, [w1_shard, w2_shard], [w1_full, w2_full],
                         send_sems, recv_sems, local_sems)

        @pl.when(pl.program_id(0) == 0)
        def _():
            cv_car[...] = jnp.zeros(cv_car.shape, F32)
            xin_car[...] = jnp.zeros(xin_car.shape, F32)
            h_car[...] = jnp.zeros(h_car.shape, F32)
            wabd[...] = _expand_heads(wa_ref[...])
            wxbd[...] = _expand_heads(wx_ref[...])

        row_c = lax.broadcasted_iota(jnp.int32, (SUB, CONV_WIDTH), 0)
        row_r = lax.broadcasted_iota(jnp.int32, (SUB, LRU_WIDTH), 0)
        cw = cw_ref[...]
        rw = rw_ref[...]
        rb = rb_ref[...]
        g_c = gnc_ref[...]
        g_r = gnr_ref[...]
        sp_c = LRU_C * _softplus_neg(lam_ref[...])

        def convs(i, carry):
            cv_prev, xin_prev = carry
            r = pl.multiple_of(i * SUB, SUB)
            gb, _, _, cv, _, _, cq = _conv3_chunk(u_ref, r, cv_prev, cw, row_c)
            y_c = gb * cq
            y_s[pl.ds(r, SUB), 0:CONV_WIDTH] = y_c * _rms(y_c) * g_c
            xin, _, _, _, xr = _conv4_chunk(u_ref, r, xin_prev, rw, rb, row_r)
            xr_s[pl.ds(r, SUB), :] = xr
            return cv, xin

        cv_last, xin_last = _chunk_loop(n_chunks, convs, (cv_car[...], xin_car[...]))
        cv_car[...] = cv_last
        xin_car[...] = xin_last

        xrb = xr_s[...].astype(BF16)
        pa_s[...] = _block_diag_apply(xrb, wabd) + ba_ref[...]
        px_s[...] = _block_diag_apply(xrb, wxbd) + bx_ref[...]

        def recur(i, h_prev):
            r = pl.multiple_of(i * SUB, SUB)
            xr = xr_s[pl.ds(r, SUB), :]
            _, ii, a, mult, _ = _lru_gates(pa_s[pl.ds(r, SUB), :], px_s[pl.ds(r, SUB), :], sp_c)
            a_cum, b_cum = _scan8_fwd(a, mult * ii * xr, row_r)
            h = a_cum * h_prev + b_cum
            hs_ref[pl.ds(r, SUB), :] = h
            ge, _ = _gelu(u_ref[pl.ds(r, SUB), OFF_G:OFF_G + LRU_WIDTH])
            y_r = h * ge
            y_s[pl.ds(r, SUB), CONV_WIDTH:MIX_WIDTH] = y_r * _rms(y_r) * g_r
            return h[SUB - 1:SUB, :]

        h_car[...] = _chunk_loop(n_chunks, recur, h_car[...])

        y_ref[...] = y_s[...].astype(BF16)

    row_tile = lambda w: pl.BlockSpec((tm, w), lambda i: (i, 0))
    whole = lambda a: pl.BlockSpec(a.shape, lambda i: (0,) * a.ndim)
    smalls = (conv_w, rnn_conv_w, rnn_conv_b, wa, b_a, wx, b_x, lam, gnc, gnr)
    return pl.pallas_call(
        body, grid=(n_steps,),
        in_specs=[row_tile(IN_COLS)] + [whole(a) for a in smalls] + [HBM_SPEC, HBM_SPEC],
        out_specs=[row_tile(LRU_WIDTH), row_tile(MIX_WIDTH), HBM_SPEC, HBM_SPEC],
        out_shape=[jax.ShapeDtypeStruct((t_len, LRU_WIDTH), F32), jax.ShapeDtypeStruct((t_len, MIX_WIDTH), BF16),
                   jax.ShapeDtypeStruct((N_DEV,) + w1_shard.shape, BF16),
                   jax.ShapeDtypeStruct((N_DEV,) + w2_shard.shape, BF16)],
        scratch_shapes=[pltpu.VMEM((tm, MIX_WIDTH), F32), pltpu.VMEM((tm, LRU_WIDTH), F32),
                        pltpu.VMEM((tm, LRU_WIDTH), F32), pltpu.VMEM((tm, LRU_WIDTH), F32),
                        pltpu.VMEM((LRU_WIDTH, GROUP), BF16), pltpu.VMEM((LRU_WIDTH, GROUP), BF16),
                        pltpu.VMEM((SUB, CONV_WIDTH), F32), pltpu.VMEM((SUB, LRU_WIDTH), F32),
                        pltpu.VMEM((1, LRU_WIDTH), F32)] + _exchange_scratch(2, 7) + [pltpu.SemaphoreType.DMA((2,))],
        compiler_params=_params(("arbitrary",), 56), name="mixer_fwd",
    )(u, *smalls, w1_shard, w2_shard)


def _mlp_fwd_bwd(x, y, target, g_mlp, g_f, w_out, w1, w2, tm):
    t_len = x.shape[0]
    n_steps = t_len // tm
    n_blk, _, blk = w1.shape

    def body(x_ref, y_ref, tg_ref, gm_ref, gf_ref, wout_hbm, w1_hbm, w2_hbm,
             dx1_ref, h2_ref, dx2_ref, vec_ref, z_hbm, dpre_hbm,
             wout_s, w1_s, w2_s, rp_s, z_s, dp_s, sem, out_sem):
        step = pl.program_id(0)
        rows = pl.ds(pl.multiple_of(step * tm, tm), tm)
        z_out = pltpu.make_async_copy(z_s, z_hbm.at[rows, :], out_sem.at[0])
        dp_out = pltpu.make_async_copy(dp_s, dpre_hbm.at[rows, :], out_sem.at[1])

        @pl.when(step == 0)
        def _():
            loads = [pltpu.make_async_copy(src, dst, sem.at[k])
                     for k, (src, dst) in enumerate(((wout_hbm, wout_s), (w1_hbm, w1_s), (w2_hbm, w2_s)))]
            for cp in loads:
                cp.start()
            vec_ref[...] = jnp.zeros(vec_ref.shape, F32)
            for cp in loads:
                cp.wait()

        x1v = x_ref[...] + jnp.dot(y_ref[...], wout_s[...], preferred_element_type=F32)
        g_m = gm_ref[...]
        g_o = gf_ref[...]
        r2 = _rms(x1v)
        x1h = x1v * r2
        h2 = (x1h * g_m).astype(BF16)
        h2_ref[...] = h2
        x2 = x1v

        @pl.when(step > 0)
        def _():
            z_out.wait()

        for k in range(n_blk):
            rp = jnp.maximum(jnp.dot(h2, w1_s[k], preferred_element_type=F32), 0.0)
            rp_s[:, k * blk:(k + 1) * blk] = rp.astype(BF16)
            zb = (rp * rp).astype(BF16)
            z_s[:, k * blk:(k + 1) * blk] = zb
            x2 = x2 + jnp.dot(zb, w2_s[k * blk:(k + 1) * blk, :], preferred_element_type=F32)
        z_out.start()
        r3 = _rms(x2)
        x2h = x2 * r3
        err = x2h * g_o - tg_ref[...]
        dout = err * (1.0 / D_MODEL)
        vec_ref[ROW_LOSS:ROW_LOSS + 1, :] += (0.5 / D_MODEL) * jnp.sum(err * err, axis=0, keepdims=True)
        vec_ref[ROW_GF:ROW_GF + 1, :] += jnp.sum(dout * x2h, axis=0, keepdims=True)
        dx2 = _rms_bwd(dout, x2h, r3, g_o)
        dx2b = dx2.astype(BF16)
        dx2_ref[...] = dx2b
        dh2 = jnp.zeros((tm, D_MODEL), F32)

        @pl.when(step > 0)
        def _():
            dp_out.wait()

        for k in range(n_blk):
            dz = _dot_nt(dx2b, w2_s[k * blk:(k + 1) * blk, :])
            dpb = (dz * 2.0 * rp_s[:, k * blk:(k + 1) * blk].astype(F32)).astype(BF16)
            dp_s[:, k * blk:(k + 1) * blk] = dpb
            dh2 = dh2 + _dot_nt(dpb, w1_s[k])
        dp_out.start()
        vec_ref[ROW_GMLP:ROW_GMLP + 1, :] += jnp.sum(dh2 * x1h, axis=0, keepdims=True)
        dx1_ref[...] = dx2 + _rms_bwd(dh2, x1h, r2, g_m)

        @pl.when(step == n_steps - 1)
        def _():
            z_out.wait()
            dp_out.wait()

    row_tile = lambda w: pl.BlockSpec((tm, w), lambda i: (i, 0))
    vec_spec = pl.BlockSpec((1, D_MODEL), lambda i: (0, 0))
    outs = pl.pallas_call(
        body, grid=(n_steps,),
        in_specs=[row_tile(D_MODEL), row_tile(MIX_WIDTH), row_tile(D_MODEL), vec_spec, vec_spec,
                  HBM_SPEC, HBM_SPEC, HBM_SPEC],
        out_specs=[row_tile(D_MODEL), row_tile(D_MODEL), row_tile(D_MODEL),
                   pl.BlockSpec((SUB, D_MODEL), lambda i: (0, 0)), HBM_SPEC, HBM_SPEC],
        out_shape=[jax.ShapeDtypeStruct((t_len, D_MODEL), F32), jax.ShapeDtypeStruct((t_len, D_MODEL), BF16),
                   jax.ShapeDtypeStruct((t_len, D_MODEL), BF16), jax.ShapeDtypeStruct((SUB, D_MODEL), F32),
                   jax.ShapeDtypeStruct((t_len, D_FF), BF16), jax.ShapeDtypeStruct((t_len, D_FF), BF16)],
        scratch_shapes=[pltpu.VMEM(w_out.shape, BF16), pltpu.VMEM(w1.shape, BF16), pltpu.VMEM(w2.shape, BF16),
                        pltpu.VMEM((tm, D_FF), BF16), pltpu.VMEM((tm, D_FF), BF16), pltpu.VMEM((tm, D_FF), BF16),
                        pltpu.SemaphoreType.DMA((3,)), pltpu.SemaphoreType.DMA((2,))],
        compiler_params=_params(("arbitrary",), 58), name="mlp_fwd_bwd",
    )(x, y, target, g_mlp, g_f, w_out, w1, w2)
    dx1, h2, dx2, vec, z, dpre = outs
    return dx1, z, dpre, h2, dx2, vec


def _mixer_bwd(u, hs, dx1, conv_w, rnn_conv_w, rnn_conv_b, wa, b_a, wx, b_x, lam, gnc, gnr, w_out,
               chip_sums, g_wout, tm):
    t_len = u.shape[0]
    n_tiles = t_len // tm
    n_chunks = tm // SUB
    per_tile = tm // SUB
    n_sums = len(chip_sums)

    def body(u_ref, up_ref, hs_ref, hp_ref, dx1_ref, cw_ref, rw_ref, rb_ref, wa_ref, ba_ref, wx_ref, bx_ref,
             lam_ref, gnc_ref, gnr_ref, wout_ref, *rest):
        hsends = rest[0:n_sums]
        gwout_ref = rest[n_sums]
        du_ref, vec_ref, wab_ref = rest[n_sums + 1:n_sums + 4]
        hrecvs = rest[n_sums + 4:2 * n_sums + 4]
        sib_wout = rest[2 * n_sums + 4]
        (du_s, dy_s, xr_s, pa_s, px_s, dpa_s, dpx_s, dxr_s, wabd, wxbd, acc, dwa_acc, dwx_acc,
         a_car, dh_car, dcq_car, dxr_car, i_send, i_recv, d_send, d_recv) = rest[2 * n_sums + 5:]
        step = pl.program_id(0)
        _host_chip_exchange(step, n_tiles, hsends, hrecvs, i_send, i_recv)
        _host_pair_exchange(step, n_tiles, [gwout_ref], [sib_wout], d_send, d_recv)
        has_prev = (step < n_tiles - 1).astype(F32)

        @pl.when(step == 0)
        def _():
            acc[...] = jnp.zeros(acc.shape, F32)
            dwa_acc[...] = jnp.zeros(dwa_acc.shape, F32)
            dwx_acc[...] = jnp.zeros(dwx_acc.shape, F32)
            a_car[...] = jnp.ones(a_car.shape, F32)
            dh_car[...] = jnp.zeros(dh_car.shape, F32)
            dcq_car[...] = jnp.zeros(dcq_car.shape, F32)
            dxr_car[...] = jnp.zeros(dxr_car.shape, F32)
            wabd[...] = _expand_heads(wa_ref[...])
            wxbd[...] = _expand_heads(wx_ref[...])

        row_c = lax.broadcasted_iota(jnp.int32, (SUB, CONV_WIDTH), 0)
        row_r = lax.broadcasted_iota(jnp.int32, (SUB, LRU_WIDTH), 0)
        cw = cw_ref[...]
        rw = rw_ref[...]
        rb = rb_ref[...]
        g_c = gnc_ref[...]
        g_r = gnr_ref[...]
        sp_c = LRU_C * _softplus_neg(lam_ref[...])

        up = up_ref[...] * has_prev
        cv_before = up[:, OFF_GC:OFF_GC + CONV_WIDTH] * up[:, OFF_V:OFF_V + CONV_WIDTH]
        xin_before = up[:, OFF_XR:OFF_XR + LRU_WIDTH]
        hs_before = hp_ref[...] * has_prev

        dy_s[...] = _dot_nt(dx1_ref[...].astype(BF16), wout_ref[...])

        def conv4_fwd(i, xin_prev):
            r = pl.multiple_of(i * SUB, SUB)
            xin, _, _, _, xr = _conv4_chunk(u_ref, r, xin_prev, rw, rb, row_r)
            xr_s[pl.ds(r, SUB), :] = xr
            return xin

        _chunk_loop(n_chunks, conv4_fwd, xin_before)
        xrb = xr_s[...].astype(BF16)
        pa_s[...] = _block_diag_apply(xrb, wabd) + ba_ref[...]
        px_s[...] = _block_diag_apply(xrb, wxbd) + bx_ref[...]

        def recur_bwd(j, carry):
            a_later, dh_later = carry
            i = n_chunks - 1 - j
            r = pl.multiple_of(i * SUB, SUB)
            rp = pl.multiple_of(jnp.maximum(i - 1, 0) * SUB, SUB)
            xr = xr_s[pl.ds(r, SUB), :]
            hs_c = hs_ref[pl.ds(r, SUB), :]
            hs_prev = jnp.where(i == 0, hs_before, hs_ref[pl.ds(rp, SUB), :])
            h_m1 = _down(hs_c, hs_prev, 1, row_r)
            ra, ii, a, mult, inv_mult = _lru_gates(pa_s[pl.ds(r, SUB), :], px_s[pl.ds(r, SUB), :], sp_c)
            ge, dge = _gelu(u_ref[pl.ds(r, SUB), OFF_G:OFF_G + LRU_WIDTH])
            y_r = hs_c * ge
            rr = _rms(y_r)
            yhat = y_r * rr
            dyn = dy_s[pl.ds(r, SUB), CONV_WIDTH:MIX_WIDTH]
            acc[ACC_GNR] += dyn * yhat
            dy_r = _rms_bwd(dyn, yhat, rr, g_r)
            du_s[pl.ds(r, SUB), OFF_G:OFF_G + LRU_WIDTH] = dy_r * hs_c * dge
            a_cum, d_cum = _scan8_rev(_up(a, a_later, 1, row_r), dy_r * ge, row_r)
            dh = a_cum * dh_later + d_cum
            dmult = dh * ii * xr
            dii = dh * mult * xr
            dxr_s[pl.ds(r, SUB), :] = dh * mult * ii
            dla = dh * h_m1 * a - dmult * a * a * inv_mult
            acc[ACC_SP] += -dla * ra
            dpa = -dla * sp_c * ra * (1.0 - ra)
            dpx = dii * ii * (1.0 - ii)
            acc[ACC_BA] += dpa
            acc[ACC_BX] += dpx
            dpa_s[pl.ds(r, SUB), :] = dpa
            dpx_s[pl.ds(r, SUB), :] = dpx
            return a, dh[0:1, :]

        a_first, dh_first = _chunk_loop(n_chunks, recur_bwd, (a_car[...], dh_car[...]))
        a_car[...] = a_first
        dh_car[...] = dh_first

        dpab = dpa_s[...].astype(BF16)
        dpxb = dpx_s[...].astype(BF16)
        dxr_s[...] += _block_diag_apply_t(dpab, wabd) + _block_diag_apply_t(dpxb, wxbd)
        for g in range(LRU_WIDTH // GROUP):
            cols = slice(g * GROUP, (g + 1) * GROUP)
            dwa_acc[cols, :] += _dot_tn(xrb[:, cols], dpab[:, cols])
            dwx_acc[cols, :] += _dot_tn(xrb[:, cols], dpxb[:, cols])

        def convs_bwd(j, carry):
            dcq_later, dxr_later = carry
            i = n_chunks - 1 - j
            r = pl.multiple_of(i * SUB, SUB)
            rp = pl.multiple_of(jnp.maximum(i - 1, 0) * SUB, SUB)
            cv_prev = jnp.where(i == 0, cv_before,
                                u_ref[pl.ds(rp, SUB), OFF_GC:OFF_GC + CONV_WIDTH]
                                * u_ref[pl.ds(rp, SUB), OFF_V:OFF_V + CONV_WIDTH])
            gb, gc, v, cv, cv_m1, cv_m2, cq = _conv3_chunk(u_ref, r, cv_prev, cw, row_c)
            y_c = gb * cq
            rc = _rms(y_c)
            yhat = y_c * rc
            dyn = dy_s[pl.ds(r, SUB), 0:CONV_WIDTH]
            acc[ACC_GNC, :, 0:CONV_WIDTH] += dyn * yhat
            dy_c = _rms_bwd(dyn, yhat, rc, g_c)
            dcq = dy_c * gb
            dcv = (cw[2:3, :] * dcq + cw[1:2, :] * _up(dcq, dcq_later, 1, row_c)
                   + cw[0:1, :] * _up(dcq, dcq_later, 2, row_c))
            acc[ACC_CW + 2, :, 0:CONV_WIDTH] += dcq * cv
            acc[ACC_CW + 1, :, 0:CONV_WIDTH] += dcq * cv_m1
            acc[ACC_CW + 0, :, 0:CONV_WIDTH] += dcq * cv_m2
            du_s[pl.ds(r, SUB), OFF_GB:OFF_GB + CONV_WIDTH] = dy_c * cq
            du_s[pl.ds(r, SUB), OFF_GC:OFF_GC + CONV_WIDTH] = dcv * v
            du_s[pl.ds(r, SUB), OFF_V:OFF_V + CONV_WIDTH] = dcv * gc

            xin_prev = jnp.where(i == 0, xin_before, u_ref[pl.ds(rp, SUB), OFF_XR:OFF_XR + LRU_WIDTH])
            xin, m1, m2, m3, _ = _conv4_chunk(u_ref, r, xin_prev, rw, rb, row_r)
            dxr = dxr_s[pl.ds(r, SUB), :]
            du_s[pl.ds(r, SUB), OFF_XR:OFF_XR + LRU_WIDTH] = (
                rw[3:4, :] * dxr + rw[2:3, :] * _up(dxr, dxr_later, 1, row_r)
                + rw[1:2, :] * _up(dxr, dxr_later, 2, row_r) + rw[0:1, :] * _up(dxr, dxr_later, 3, row_r))
            acc[ACC_RW + 3] += dxr * xin
            acc[ACC_RW + 2] += dxr * m1
            acc[ACC_RW + 1] += dxr * m2
            acc[ACC_RW + 0] += dxr * m3
            acc[ACC_BR] += dxr
            return dcq, dxr

        dcq_first, dxr_first = _chunk_loop(n_chunks, convs_bwd, (dcq_car[...], dxr_car[...]))
        dcq_car[...] = dcq_first
        dxr_car[...] = dxr_first

        du_ref[...] = du_s[...].astype(BF16)

        @pl.when(step == n_tiles - 1)
        def _():
            vec_ref[...] = jnp.zeros(vec_ref.shape, F32)
            rows = {ACC_GNC: ROW_GNC, ACC_GNR: ROW_GNR, ACC_BR: ROW_BR, ACC_BA: ROW_BA, ACC_BX: ROW_BX}
            for k in range(3):
                rows[ACC_CW + k] = ROW_CW + k
            for k in range(4):
                rows[ACC_RW + k] = ROW_RW + k
            for slot, out_row in rows.items():
                o = out_row - ROW_GNC
                vec_ref[o:o + 1, :] = jnp.sum(acc[slot], axis=0, keepdims=True)
            lam_v = lam_ref[...]
            dsp = jnp.sum(acc[ACC_SP], axis=0, keepdims=True)
            o = ROW_LAM - ROW_GNC
            vec_ref[o:o + 1, :] = -dsp * LRU_C / (1.0 + jnp.exp(lam_v))
            wab_ref[0:LRU_WIDTH, :] = _fold_heads(dwa_acc[...])
            wab_ref[LRU_WIDTH:2 * LRU_WIDTH, :] = _fold_heads(dwx_acc[...])

    rev = lambda w: pl.BlockSpec((tm, w), lambda s: (n_tiles - 1 - s, 0))
    before = lambda w: pl.BlockSpec((SUB, w), lambda s: (jnp.maximum((n_tiles - 1 - s) * per_tile - 1, 0), 0))
    whole = lambda a: pl.BlockSpec(a.shape, lambda s: (0,) * a.ndim)
    smalls = (conv_w, rnn_conv_w, rnn_conv_b, wa, b_a, wx, b_x, lam, gnc, gnr, w_out)
    full = lambda w: pltpu.VMEM((tm, w), F32)
    return pl.pallas_call(
        body, grid=(n_tiles,),
        in_specs=[rev(IN_COLS), before(IN_COLS), rev(LRU_WIDTH), before(LRU_WIDTH), rev(D_MODEL)]
        + [whole(a) for a in smalls] + [HBM_SPEC] * (n_sums + 1),
        out_specs=[rev(IN_COLS), pl.BlockSpec((16, D_MODEL), lambda s: (0, 0)),
                   pl.BlockSpec((2 * LRU_WIDTH, HEAD_DIM), lambda s: (0, 0))] + [HBM_SPEC] * (n_sums + 1),
        out_shape=[jax.ShapeDtypeStruct((t_len, IN_COLS), BF16), jax.ShapeDtypeStruct((16, D_MODEL), F32),
                   jax.ShapeDtypeStruct((2 * LRU_WIDTH, HEAD_DIM), F32)]
        + [jax.ShapeDtypeStruct(s.shape, BF16) for s in chip_sums]
        + [jax.ShapeDtypeStruct((4,) + g_wout.shape[1:], BF16)],
        scratch_shapes=[full(IN_COLS), full(MIX_WIDTH), full(LRU_WIDTH), full(LRU_WIDTH), full(LRU_WIDTH),
                        full(LRU_WIDTH), full(LRU_WIDTH), full(LRU_WIDTH),
                        pltpu.VMEM((LRU_WIDTH, GROUP), BF16), pltpu.VMEM((LRU_WIDTH, GROUP), BF16),
                        pltpu.VMEM((N_ACC, SUB, LRU_WIDTH), F32),
                        pltpu.VMEM((LRU_WIDTH, GROUP), F32), pltpu.VMEM((LRU_WIDTH, GROUP), F32),
                        pltpu.VMEM((SUB, LRU_WIDTH), F32), pltpu.VMEM((1, LRU_WIDTH), F32),
                        pltpu.VMEM((SUB, CONV_WIDTH), F32), pltpu.VMEM((SUB, LRU_WIDTH), F32)]
        + _exchange_scratch(n_sums, 3) + _exchange_scratch(1, 4),
        compiler_params=_params(("arbitrary",), 56), name="mixer_bwd",
    )(u, u, hs, hs, dx1, *smalls, *chip_sums, g_wout)


def _in_proj_bwd(du, dx1, x, g_mix, win_t, tm, chip_sums, g_own, vec_m, vec_b, wab):
    t_len = x.shape[0]
    n_steps = t_len // tm
    wrows = wab.shape[0] // N_DEV

    def body(du_ref, dx1_ref, x_ref, g_ref, w_ref, hs_ref, gown_ref, vm_ref, vb_ref, wab_ref,
             dx_ref, vec_ref, landed_ref, sib_ref, vrm_ref, vrb_ref, wr_ref,
             i_send, i_recv, d_send, d_recv, s_send, s_recv, s_local):
        step = pl.program_id(0)
        _host_chip_exchange(step, n_steps, [hs_ref], [landed_ref], i_send, i_recv)
        _host_half_exchange(step, n_steps, gown_ref, sib_ref, d_send, d_recv)
        _host_small_exchange(step, n_steps, vm_ref, vb_ref, wab_ref, vrm_ref, vrb_ref, wr_ref, s_send, s_recv, s_local)

        @pl.when(step == 0)
        def _():
            vec_ref[...] = jnp.zeros(vec_ref.shape, F32)

        dh = jnp.dot(du_ref[...], w_ref[...], preferred_element_type=F32)
        xv = x_ref[...]
        r1 = _rms(xv)
        xh = xv * r1
        vec_ref[0:1, :] += jnp.sum(dh * xh, axis=0, keepdims=True)
        dx_ref[...] = dx1_ref[...] + _rms_bwd(dh, xh, r1, g_ref[...])

    row_tile = lambda w: pl.BlockSpec((tm, w), lambda i: (i, 0))
    half_shape = (g_own.shape[0], g_own.shape[1] // 2, g_own.shape[2])
    return pl.pallas_call(
        body, grid=(n_steps,),
        in_specs=[row_tile(IN_COLS), row_tile(D_MODEL), row_tile(D_MODEL), pl.BlockSpec((1, D_MODEL), lambda i: (0, 0)),
                  pl.BlockSpec((IN_COLS, D_MODEL), lambda i: (0, 0))] + [HBM_SPEC] * 5,
        out_specs=[row_tile(D_MODEL), pl.BlockSpec((SUB, D_MODEL), lambda i: (0, 0))] + [HBM_SPEC] * 5,
        out_shape=[jax.ShapeDtypeStruct((t_len, D_MODEL), F32), jax.ShapeDtypeStruct((SUB, D_MODEL), F32),
                   jax.ShapeDtypeStruct(chip_sums.shape, BF16), jax.ShapeDtypeStruct(half_shape, BF16),
                   jax.ShapeDtypeStruct((N_DEV,) + vec_m.shape, F32), jax.ShapeDtypeStruct((N_DEV,) + vec_b.shape, F32),
                   jax.ShapeDtypeStruct((N_DEV, wrows, wab.shape[1]), F32)],
        scratch_shapes=_exchange_scratch(1, 3) + [pltpu.SemaphoreType.DMA((1,)), pltpu.SemaphoreType.DMA((1,))]
        + _exchange_scratch(3, N_DEV) + [pltpu.SemaphoreType.DMA((2,))],
        compiler_params=_params(("arbitrary",), 56), name="in_proj_bwd",
    )(du, dx1, x, g_mix, win_t, chip_sums, g_own, vec_m, vec_b, wab)


def _tn_weight_grad(a, b, tk, name, pair=(), col_blocks=1):
    t_len, m = a.shape
    n = b.shape[1]
    n_steps = t_len // tk
    sent = tuple(pair)
    n_sent = len(sent)

    def body(a_ref, b_ref, *rest):
        srcs = rest[0:n_sent]
        o_ref = rest[n_sent]
        dsts = rest[n_sent + 1:2 * n_sent + 1]
        acc = rest[2 * n_sent + 1]
        sems = rest[2 * n_sent + 2:]
        j = pl.program_id(0)
        if pair:
            _host_pair_exchange(j, n_steps, srcs, dsts, *sems)

        @pl.when(j == 0)
        def _():
            acc[...] = jnp.zeros(acc.shape, F32)

        acc[...] += _dot_tn(a_ref[...].astype(BF16), b_ref[...].astype(BF16))

        @pl.when(j == n_steps - 1)
        def _():
            if col_blocks == 1:
                o_ref[...] = acc[...].astype(BF16)
            else:
                for k in range(col_blocks):
                    o_ref[k] = acc[:, k * nb:(k + 1) * nb].astype(BF16)

    nb = n // col_blocks
    out_dims = (m, n) if col_blocks == 1 else (col_blocks, m, nb)
    landed = [jax.ShapeDtypeStruct((4,) + g.shape[1:], BF16) for g in pair]
    scratch = [pltpu.VMEM((m, n), F32)]
    if n_sent:
        scratch += _exchange_scratch(n_sent, 4)
    return pl.pallas_call(
        body, grid=(n_steps,),
        in_specs=[pl.BlockSpec((tk, m), lambda j: (j, 0)), pl.BlockSpec((tk, n), lambda j: (j, 0))]
        + [HBM_SPEC] * n_sent,
        out_specs=[pl.BlockSpec(out_dims, lambda j: (0,) * len(out_dims))] + [HBM_SPEC] * n_sent,
        out_shape=[jax.ShapeDtypeStruct(out_dims, BF16)] + landed,
        scratch_shapes=scratch,
        compiler_params=_params(("arbitrary",), 56), name=name,
    )(a, b, *sent)


def _w_in_grad_part(du, h, tk, name, chip_ids, chip=(), halves=None):
    t_len = du.shape[0]
    n_t = t_len // tk
    n_q = chip_ids.shape[0]
    width = 2 * (IN_COLS // N_DEV)
    n_steps = n_q * n_t
    sent = tuple(chip) + (() if halves is None else (halves,))
    n_sent = len(sent)

    def body(ids_ref, a_ref, b_ref, *rest):
        srcs = rest[0:n_sent]
        o_ref = rest[n_sent]
        dsts = rest[n_sent + 1:2 * n_sent + 1]
        acc = rest[2 * n_sent + 1]
        sems = rest[2 * n_sent + 2:]
        j = pl.program_id(1)
        step = pl.program_id(0) * n_t + j
        if chip:
            _host_chip_exchange(step, n_steps, srcs, dsts, *sems)
        if halves is not None:
            _host_half_exchange(step, n_steps, srcs[0], dsts[0], *sems)

        @pl.when(j == 0)
        def _():
            acc[...] = jnp.zeros(acc.shape, F32)

        acc[...] += _dot_tn(a_ref[...], b_ref[...])

        @pl.when(j == n_t - 1)
        def _():
            o_ref[0] = acc[...].astype(BF16)

    landed = [jax.ShapeDtypeStruct(s.shape, BF16) for s in chip]
    scratch = [pltpu.VMEM((width, D_MODEL), F32)]
    if chip:
        scratch += _exchange_scratch(len(chip), 3)
    if halves is not None:
        landed.append(jax.ShapeDtypeStruct((halves.shape[0], halves.shape[1] // 2, halves.shape[2]), BF16))
        scratch += [pltpu.SemaphoreType.DMA((halves.shape[0],)), pltpu.SemaphoreType.DMA((halves.shape[0],))]
    grid_spec = pltpu.PrefetchScalarGridSpec(
        num_scalar_prefetch=1, grid=(n_q, n_t),
        in_specs=[pl.BlockSpec((tk, width), lambda q, j, ids: (j, ids[q])),
                  pl.BlockSpec((tk, D_MODEL), lambda q, j, ids: (j, 0))] + [HBM_SPEC] * n_sent,
        out_specs=[pl.BlockSpec((1, width, D_MODEL), lambda q, j, ids: (q, 0, 0))] + [HBM_SPEC] * n_sent,
        scratch_shapes=scratch)
    return pl.pallas_call(
        body, grid_spec=grid_spec, out_shape=[jax.ShapeDtypeStruct((n_q, width, D_MODEL), BF16)] + landed,
        compiler_params=_params(("arbitrary", "arbitrary"), 40), name=name,
    )(chip_ids, du, h, *sent)


def _adamw(w, g, m, v):
    m = ADAM_B1 * m + (1.0 - ADAM_B1) * g
    v = ADAM_B2 * v + (1.0 - ADAM_B2) * (g * g)
    delta = -ADAM_LR * ((m / BC1) / (jnp.sqrt(v / BC2) + ADAM_EPS) + ADAM_WD * w)
    return delta, m, v


def _update_sharded(g, landed, w, m, v, rows_blk, name):
    rows, cols = w.shape

    def body(g_ref, l_ref, w_ref, m_ref, v_ref, og, od, om, ov):
        gv = g_ref[...]
        for j in range(3):
            gv = gv + l_ref[j].astype(F32)
        delta, mn, vn = _adamw(w_ref[...], gv, m_ref[...], v_ref[...])
        og[...] = gv
        od[...] = delta
        om[...] = mn
        ov[...] = vn

    blk = pl.BlockSpec((rows_blk, cols), lambda i: (i, 0))
    shape = pltpu.HBM((rows, cols), F32)
    return pl.pallas_call(
        body, grid=(rows // rows_blk,),
        in_specs=[blk, pl.BlockSpec((3, rows_blk, cols), lambda i: (0, i, 0)), blk, blk, blk],
        out_specs=[blk] * 4, out_shape=[shape] * 4,
        compiler_params=_params(("arbitrary",), 32), name=name,
    )(*_in_hbm(g, landed, w, m, v))


def _update_w_in(g_own, sib_own, landed, w, m, v, core, rows_blk):
    rows, cols = w.shape
    pad_cols = -(-cols // 128) * 128

    def body(core_ref, g_ref, s_ref, l_ref, w_ref, m_ref, v_ref, og, od, om, ov, padbuf, turned):
        gt = g_ref[0, 0].astype(F32) + s_ref[0].astype(F32)
        for j in range(3):
            gt = gt + l_ref[j].astype(F32)
        padbuf[...] = jnp.zeros(padbuf.shape, F32)
        padbuf[0:cols, :] = gt
        turned[...] = padbuf[...].T
        gv = turned[:, 0:cols]
        delta, mn, vn = _adamw(w_ref[...], gv, m_ref[...], v_ref[...])
        og[...] = gv
        od[...] = delta
        om[...] = mn
        ov[...] = vn

    blk = pl.BlockSpec((rows_blk, cols), lambda i, cr: (i, 0))
    grid_spec = pltpu.PrefetchScalarGridSpec(
        num_scalar_prefetch=1, grid=(rows // rows_blk,),
        in_specs=[pl.BlockSpec((1, 1, cols, rows_blk), lambda i, cr: (0, cr[0], 0, i)),
                  pl.BlockSpec((1, cols, rows_blk), lambda i, cr: (0, 0, i)),
                  pl.BlockSpec((3, cols, rows_blk), lambda i, cr: (0, 0, i)), blk, blk, blk],
        out_specs=[blk] * 4,
        scratch_shapes=[pltpu.VMEM((pad_cols, rows_blk), F32), pltpu.VMEM((rows_blk, pad_cols), F32)])
    return pl.pallas_call(
        body, grid_spec=grid_spec, out_shape=[pltpu.HBM((rows, cols), F32)] * 4,
        compiler_params=_params(("arbitrary",), 32), name="update_w_in",
    )(core, *_in_hbm(g_own.reshape(1, 2, cols, rows), sib_own, landed, w, m, v))


def _update_small(vsum, wsum, g_cw, g_rw, weights, moments_m, moments_v):
    n = len(weights)

    def body(*refs):
        vs, ws, gcw, grw = refs[0:4]
        w_refs = refs[4:4 + n]
        m_refs = refs[4 + n:4 + 2 * n]
        v_refs = refs[4 + 2 * n:4 + 3 * n]
        outs = refs[4 + 3 * n:]
        loss_ref = outs[0]
        loss_ref[...] = jnp.sum(vs[ROW_LOSS:ROW_LOSS + 1, :], axis=1, keepdims=True)
        grads = [
            vs[ROW_GMIX:ROW_GMIX + 1, :], gcw[...], grw[...], vs[ROW_BR:ROW_BR + 1, :],
            ws[0:LRU_WIDTH, :], vs[ROW_BA:ROW_BA + 1, :], ws[LRU_WIDTH:2 * LRU_WIDTH, :], vs[ROW_BX:ROW_BX + 1, :],
            vs[ROW_LAM:ROW_LAM + 1, :], vs[ROW_GNC:ROW_GNC + 1, 0:CONV_WIDTH], vs[ROW_GNR:ROW_GNR + 1, :],
            vs[ROW_GMLP:ROW_GMLP + 1, :], vs[ROW_GF:ROW_GF + 1, :],
        ]
        for k in range(n):
            gk = grads[k]
            delta, mn, vn = _adamw(w_refs[k][...], gk, m_refs[k][...], v_refs[k][...])
            outs[1 + 4 * k][...] = gk
            outs[2 + 4 * k][...] = delta
            outs[3 + 4 * k][...] = mn
            outs[4 + 4 * k][...] = vn

    whole = lambda a: pl.BlockSpec(a.shape, lambda i: (0,) * len(a.shape))
    out_shape = [jax.ShapeDtypeStruct((1, 1), F32)]
    for w in weights:
        out_shape += [jax.ShapeDtypeStruct(w.shape, F32)] * 4
    args = (vsum, wsum, g_cw, g_rw, *weights, *moments_m, *moments_v)
    return pl.pallas_call(
        body, grid=(1,), out_shape=out_shape, in_specs=[whole(a) for a in args], out_specs=[whole(s) for s in out_shape],
        compiler_params=_params(("arbitrary",), 32), name="update_small",
    )(*args)


def kernel(x, norm_mix_g, w_in, conv_w, rnn_conv_w, rnn_conv_b, w_a, b_a, w_x, b_x, lru_lambda, g_norm_conv, g_norm_rnn, w_out, norm_mlp_g, w_mlp_in, w_mlp_out, final_norm_g, loss_target, m_norm_mix_g, m_w_in, m_conv_w, m_rnn_conv_w, m_rnn_conv_b, m_w_a, m_b_a, m_w_x, m_b_x, m_lru_lambda, m_g_norm_conv, m_g_norm_rnn, m_w_out, m_norm_mlp_g, m_w_mlp_in, m_w_mlp_out, m_final_norm_g, v_norm_mix_g, v_w_in, v_conv_w, v_rnn_conv_w, v_rnn_conv_b, v_w_a, v_b_a, v_w_x, v_b_x, v_lru_lambda, v_g_norm_conv, v_g_norm_rnn, v_w_out, v_norm_mlp_g, v_w_mlp_in, v_w_mlp_out, v_final_norm_g):
    t_len = x.shape[1]
    my_id = 4 * lax.axis_index("x") + 2 * lax.axis_index("y") + lax.axis_index("c")
    tm = min(256, t_len)
    tb = min(512, t_len)
    tk = min(512, t_len)

    xs = x.reshape(t_len, D_MODEL)
    tgt = loss_target.reshape(t_len, D_MODEL)
    flat = lambda a: a.reshape(a.shape[-2:]) if a.ndim == 3 else a.reshape(1, -1)
    heads = lambda a: a.reshape(LRU_WIDTH, HEAD_DIM)

    win_blk, cpack, wout_shard, w1_shard, w2_shard = _all_gather_w_in(
        flat(w_in), flat(w_out), flat(w_mlp_in), flat(w_mlp_out), flat(conv_w), flat(rnn_conv_w))
    win_t = win_blk.reshape(IN_COLS, D_MODEL)
    conv_full = jnp.transpose(cpack[:, 0:3, 0:64], (1, 0, 2)).reshape(3, CONV_WIDTH)
    rnn_full = jnp.transpose(cpack[:, 3:7, :], (1, 0, 2)).reshape(4, LRU_WIDTH)
    mixer_small = (conv_full, rnn_full, flat(rnn_conv_b), heads(w_a), flat(b_a), heads(w_x), flat(b_x),
                   flat(lru_lambda), flat(g_norm_conv), flat(g_norm_rnn))

    u, h, wout_blk = _in_proj(xs, flat(norm_mix_g), win_t, wout_shard, tb)
    wout_f = wout_blk.reshape(MIX_WIDTH, D_MODEL)
    hs, y, w1_blk, w2_blk = _mixer_fwd(u, *mixer_small, w1_shard, w2_shard, tm)
    dx1, z, dpre, h2, dx2, vec_m = _mlp_fwd_bwd(xs, y, tgt, flat(norm_mlp_g), flat(final_norm_g), wout_f, w1_blk,
                                                w2_blk.reshape(D_FF, D_MODEL), tb)
    (g_w1,) = _tn_weight_grad(h2, dpre, tk, "w_mlp_in_grad", col_blocks=N_DEV)
    (g_w2,) = _tn_weight_grad(z, dx2, tk, "w_mlp_out_grad")
    g_w2 = g_w2.reshape(N_DEV, D_FF // N_DEV, D_MODEL)
    g_wout, sib_w1, sib_w2 = _tn_weight_grad(y, dx1, tk, "w_out_grad", pair=(g_w1, g_w2))
    g_wout = g_wout.reshape(N_DEV, MIX_WIDTH // N_DEV, D_MODEL)
    hsend_w1, own_w1 = _pair_sum(g_w1, sib_w1, "pair_sum_w_mlp_in")
    hsend_w2, own_w2 = _pair_sum(g_w2, sib_w2, "pair_sum_w_mlp_out")
    du, vec_b, wab, landed_w1, landed_w2, sib_wout = _mixer_bwd(
        u, hs, dx1, *mixer_small, wout_f, (hsend_w1, hsend_w2), g_wout, tm)
    hsend_wout, own_wout = _pair_sum(g_wout, sib_wout, "pair_sum_w_out")
    ax, ay, ac = lax.axis_index("x"), lax.axis_index("y"), lax.axis_index("c")
    chip_ids = jnp.stack([2 * cx + cy for cx, cy in [(ax, ay)] + _other_chips(ax, ay)]).astype(jnp.int32)
    core = jnp.reshape(ac, (1,)).astype(jnp.int32)
    tw = min(1024, t_len)
    g_others, landed_wout = _w_in_grad_part(du, h, tw, "w_in_grad_others", chip_ids[1:4], chip=(hsend_wout,))
    g_own, sib_others = _w_in_grad_part(du, h, tw, "w_in_grad_own", chip_ids[0:1], halves=g_others)
    hsend_win = _pair_sum_parts(g_others, sib_others, core)
    grad_x, vec_x, landed_win, sib_own, vrecv_m, vrecv_b, wrecv = _in_proj_bwd(
        du, dx1, xs, flat(norm_mix_g), win_t, tm, hsend_win, g_own, vec_m, vec_b, wab)

    vsum, wsum = _final_small(vrecv_m, vrecv_b, wab, wrecv, vec_x)

    up_win = _update_w_in(g_own, sib_own, landed_win, flat(w_in), flat(m_w_in), flat(v_w_in), core, 256)
    up_wout = _update_sharded(own_wout, landed_wout, flat(w_out), flat(m_w_out), flat(v_w_out), 96, "update_w_out")
    up_w1 = _update_sharded(own_w1, landed_w1, flat(w_mlp_in), flat(m_w_mlp_in), flat(v_w_mlp_in), 256,
                            "update_w_mlp_in")
    up_w2 = _update_sharded(own_w2, landed_w2, flat(w_mlp_out), flat(m_w_mlp_out), flat(v_w_mlp_out), 256,
                            "update_w_mlp_out")

    g_cw = lax.dynamic_slice(vsum, (ROW_CW, 64 * my_id), (3, 64))
    g_rw = lax.dynamic_slice(vsum, (ROW_RW, 128 * my_id), (4, 128))
    small_w = (norm_mix_g, conv_w, rnn_conv_w, rnn_conv_b, w_a, b_a, w_x, b_x, lru_lambda, g_norm_conv, g_norm_rnn,
               norm_mlp_g, final_norm_g)
    small_m = (m_norm_mix_g, m_conv_w, m_rnn_conv_w, m_rnn_conv_b, m_w_a, m_b_a, m_w_x, m_b_x, m_lru_lambda,
               m_g_norm_conv, m_g_norm_rnn, m_norm_mlp_g, m_final_norm_g)
    small_v = (v_norm_mix_g, v_conv_w, v_rnn_conv_w, v_rnn_conv_b, v_w_a, v_b_a, v_w_x, v_b_x, v_lru_lambda,
               v_g_norm_conv, v_g_norm_rnn, v_norm_mlp_g, v_final_norm_g)
    is_heads = (False, False, False, False, True, False, True, False, False, False, False, False, False)
    as2d = lambda arrs: [heads(a) if hd else flat(a) for a, hd in zip(arrs, is_heads)]
    small_out = _update_small(vsum, wsum, g_cw, g_rw, as2d(small_w), as2d(small_m), as2d(small_v))
    loss = small_out[0].reshape(())

    names = ["norm_mix_g", "w_in", "conv_w", "rnn_conv_w", "rnn_conv_b", "w_a", "b_a", "w_x", "b_x", "lru_lambda",
             "g_norm_conv", "g_norm_rnn", "w_out", "norm_mlp_g", "w_mlp_in", "w_mlp_out", "final_norm_g"]
    originals = dict(zip(names, (norm_mix_g, w_in, conv_w, rnn_conv_w, rnn_conv_b, w_a, b_a, w_x, b_x, lru_lambda,
                                 g_norm_conv, g_norm_rnn, w_out, norm_mlp_g, w_mlp_in, w_mlp_out, final_norm_g)))
    results = {"w_in": up_win, "w_out": up_wout, "w_mlp_in": up_w1, "w_mlp_out": up_w2}
    small_names = ["norm_mix_g", "conv_w", "rnn_conv_w", "rnn_conv_b", "w_a", "b_a", "w_x", "b_x", "lru_lambda",
                   "g_norm_conv", "g_norm_rnn", "norm_mlp_g", "final_norm_g"]
    for k, nm in enumerate(small_names):
        results[nm] = small_out[1 + 4 * k:5 + 4 * k]
    out = [loss, grad_x.reshape(x.shape)]
    for kind in range(4):
        out += [results[nm][kind].reshape(originals[nm].shape) for nm in names]
    return tuple(out)
```

```python
import functools

import jax
import jax.numpy as jnp
from jax import lax
from jax.experimental import pallas as pl
from jax.experimental.pallas import tpu as pltpu

F32 = jnp.float32
BF16 = jnp.bfloat16

D_MODEL = 1024
HEAD_DIM = 64
CONV_WIDTH = 512
LRU_WIDTH = 1024
MIX_WIDTH = CONV_WIDTH + LRU_WIDTH
IN_COLS = 3 * CONV_WIDTH + 2 * LRU_WIDTH
D_FF = 4 * D_MODEL
GROUP = 256
EPS = 1e-6
LRU_C = 8.0
N_DEV = 8
SUB = 8

OFF_GB, OFF_GC, OFF_V, OFF_XR, OFF_G = 0, 512, 1024, 1536, 2560

ADAM_LR, ADAM_B1, ADAM_B2, ADAM_EPS, ADAM_WD, ADAM_STEP = 0.001, 0.9, 0.999, 1e-08, 0.01, 10
BC1 = 1.0 - ADAM_B1 ** ADAM_STEP
BC2 = 1.0 - ADAM_B2 ** ADAM_STEP

MIB = 1024 * 1024
MESH = pl.DeviceIdType.MESH

VEC_ROWS = 32
ROW_GF, ROW_GMLP, ROW_LOSS = 0, 1, 2
ROW_GNC, ROW_GNR, ROW_BR, ROW_BA, ROW_BX, ROW_LAM, ROW_CW, ROW_RW = 8, 9, 10, 11, 12, 13, 14, 17
ROW_GMIX = 24
ACC_GNC, ACC_GNR, ACC_BR, ACC_BA, ACC_BX, ACC_SP, ACC_CW, ACC_RW, N_ACC = 0, 1, 2, 3, 4, 5, 6, 9, 13


def _params(semantics=None, vmem_mib=48):
    return pltpu.CompilerParams(dimension_semantics=semantics, vmem_limit_bytes=vmem_mib * MIB)


def _rms(x):
    return lax.rsqrt(jnp.mean(x * x, axis=-1, keepdims=True) + EPS)


def _rms_bwd(dy, xhat, r, g):
    dyh = dy * g
    return r * (dyh - xhat * jnp.mean(dyh * xhat, axis=-1, keepdims=True))


def _sigmoid(x):
    return 0.5 + 0.5 * jnp.tanh(0.5 * x)


def _gelu(x):
    c0, c1 = 0.7978845608028654, 0.044715
    t = jnp.tanh(c0 * (x + c1 * x * x * x))
    ge = 0.5 * x * (1.0 + t)
    dge = 0.5 * (1.0 + t) + 0.5 * x * (1.0 - t * t) * c0 * (1.0 + 3.0 * c1 * x * x)
    return ge, dge


def _softplus_neg(lam):
    z = -lam
    e = jnp.exp(-jnp.abs(z))
    return jnp.maximum(z, 0.0) + jnp.where(e < 1e-4, e * (1.0 - 0.5 * e), jnp.log(1.0 + e))


def _lru_gates(pa, px, sp_c):
    ra = _sigmoid(pa)
    ii = _sigmoid(px)
    la = -ra * sp_c
    a = jnp.exp(la)
    x2 = 2.0 * la
    series = -x2 * (1.0 + x2 * (0.5 + x2 * (1.0 / 6.0 + x2 * (1.0 / 24.0))))
    m2 = jnp.where(x2 > -0.01, series, 1.0 - a * a)
    inv_mult = lax.rsqrt(m2)
    mult = jnp.where(m2 > 0.0, m2 * inv_mult, 0.0)
    return ra, ii, a, mult, inv_mult


def _down(cur, prev, s, row):
    return jnp.where(row >= s, pltpu.roll(cur, s, 0), pltpu.roll(prev, s, 0))


def _up(cur, nxt, s, row):
    return jnp.where(row < SUB - s, pltpu.roll(cur, SUB - s, 0), pltpu.roll(nxt, SUB - s, 0))


def _scan8_fwd(a, b, row):
    for s in (1, 2, 4):
        m = row >= s
        a_sh = pltpu.roll(a, s, 0)
        b_sh = pltpu.roll(b, s, 0)
        b = jnp.where(m, a * b_sh + b, b)
        a = jnp.where(m, a * a_sh, a)
    return a, b


def _scan8_rev(a, b, row):
    for s in (1, 2, 4):
        m = row < SUB - s
        a_sh = pltpu.roll(a, SUB - s, 0)
        b_sh = pltpu.roll(b, SUB - s, 0)
        b = jnp.where(m, a * b_sh + b, b)
        a = jnp.where(m, a * a_sh, a)
    return a, b


def _group_mask(shape):
    r = lax.broadcasted_iota(jnp.int32, shape, 0)
    c = lax.broadcasted_iota(jnp.int32, shape, 1)
    return ((r % GROUP) // HEAD_DIM) == (c // HEAD_DIM)


def _expand_heads(w):
    j = lax.broadcasted_iota(jnp.int32, (HEAD_DIM, GROUP), 0)
    c = lax.broadcasted_iota(jnp.int32, (HEAD_DIM, GROUP), 1)
    spread = (c % HEAD_DIM == j).astype(BF16)
    e = jnp.dot(w.astype(BF16), spread, preferred_element_type=F32)
    return jnp.where(_group_mask(e.shape), e, 0.0).astype(BF16)


def _fold_heads(p):
    p = jnp.where(_group_mask(p.shape), p, 0.0)
    c = lax.broadcasted_iota(jnp.int32, (GROUP, HEAD_DIM), 0)
    j = lax.broadcasted_iota(jnp.int32, (GROUP, HEAD_DIM), 1)
    fold = (c % HEAD_DIM == j).astype(BF16)
    hi = p.astype(BF16)
    rest = p - hi.astype(F32)
    mid = rest.astype(BF16)
    lo = (rest - mid.astype(F32)).astype(BF16)
    dot = functools.partial(jnp.dot, preferred_element_type=F32)
    return dot(hi, fold) + dot(mid, fold) + dot(lo, fold)


def _block_diag_apply(xb, wbd_ref):
    parts = [jnp.dot(xb[:, g * GROUP:(g + 1) * GROUP], wbd_ref[g * GROUP:(g + 1) * GROUP, :],
                     preferred_element_type=F32) for g in range(LRU_WIDTH // GROUP)]
    return jnp.concatenate(parts, axis=1)


def _block_diag_apply_t(db, wbd_ref):
    parts = [lax.dot_general(db[:, g * GROUP:(g + 1) * GROUP], wbd_ref[g * GROUP:(g + 1) * GROUP, :],
                             (((1,), (1,)), ((), ())), preferred_element_type=F32)
             for g in range(LRU_WIDTH // GROUP)]
    return jnp.concatenate(parts, axis=1)


def _dot_nt(a, b):
    return lax.dot_general(a, b, (((1,), (1,)), ((), ())), preferred_element_type=F32)


def _dot_tn(a, b):
    return lax.dot_general(a, b, (((0,), (0,)), ((), ())), preferred_element_type=F32)


CHUNKS_IN_FLIGHT = 4


def _chunk_loop(n_chunks, chunk, init):
    def body(k, carry):
        for j in range(CHUNKS_IN_FLIGHT):
            carry = chunk(k * CHUNKS_IN_FLIGHT + j, carry)
        return carry

    return lax.fori_loop(0, n_chunks // CHUNKS_IN_FLIGHT, body, init)


def _place():
    x, y, c = lax.axis_index("x"), lax.axis_index("y"), lax.axis_index("c")
    return x, y, c


def _block_id(chip, core):
    return 4 * chip[0] + 2 * chip[1] + core


def _other_chips(x, y):
    return [(1 - x, y), (x, 1 - y), (1 - x, 1 - y)]


def _remote_copy(src, dst, send_sem, recv_sem, to):
    return pltpu.make_async_remote_copy(src_ref=src, dst_ref=dst, send_sem=send_sem, recv_sem=recv_sem,
                                        device_id=to, device_id_type=MESH)


HBM_SPEC = pl.BlockSpec(memory_space=pl.ANY)


def _in_hbm(*arrays):
    return [pltpu.with_memory_space_constraint(a, pltpu.HBM) for a in arrays]


def _all_gather_w_in(w_in, w_out, w_mlp_in, w_mlp_out, conv_w, rnn_conv_w):
    n_in = w_in.shape[1]
    n_arr = 2

    def body(win_ref, wout_ref, w1_ref, w2_ref, cw_ref, rw_ref,
             o_win, o_cp, o_wout, o_w1, o_w2, padbuf, send_sems, recv_sems):
        x, y, c = _place()
        me = (x, y, c)
        my_id = _block_id((x, y), c)
        sibling = (x, y, 1 - c)
        chips = _other_chips(x, y)
        outs = [o_win, o_cp]

        padbuf[...] = jnp.zeros(padbuf.shape, F32)
        padbuf[:, 0:n_in] = win_ref[...]
        o_win[my_id] = padbuf[...].T[0:n_in, :].astype(BF16)
        o_cp[my_id] = jnp.zeros(o_cp.shape[1:], F32)
        o_cp[my_id, 0:3, 0:64] = cw_ref[...]
        o_cp[my_id, 3:7, :] = rw_ref[...]

        def copy(arr, k, block, to):
            blk = outs[arr].at[block]
            return _remote_copy(blk, blk, send_sems.at[arr, k], recv_sems.at[arr, k], to)

        first = []
        for arr in range(n_arr):
            first.append(copy(arr, 0, my_id, sibling))
            first += [copy(arr, 1 + j, my_id, (*chip, c)) for j, chip in enumerate(chips)]
        for cp in first:
            cp.start()
        o_wout[...] = wout_ref[...].astype(BF16)
        o_w1[...] = w1_ref[...].astype(BF16)
        o_w2[...] = w2_ref[...].astype(BF16)
        passed = []
        for j, chip in enumerate(chips):
            for arr in range(n_arr):
                copy(arr, 1 + j, _block_id(chip, c), me).wait_recv()
                fwd = copy(arr, 4 + j, _block_id(chip, c), sibling)
                fwd.start()
                passed.append(fwd)
        for arr in range(n_arr):
            copy(arr, 0, _block_id((x, y), 1 - c), me).wait_recv()
            for j, chip in enumerate(chips):
                copy(arr, 4 + j, _block_id(chip, 1 - c), me).wait_recv()
        for cp in first + passed:
            cp.wait_send()

    vm = pl.BlockSpec(memory_space=pltpu.VMEM)
    shapes = (
        jax.ShapeDtypeStruct((N_DEV, n_in, D_MODEL), BF16),
        jax.ShapeDtypeStruct((N_DEV, 8, 128), F32),
        jax.ShapeDtypeStruct(w_out.shape, BF16),
        jax.ShapeDtypeStruct(w_mlp_in.shape, BF16),
        jax.ShapeDtypeStruct(w_mlp_out.shape, BF16),
    )
    return pl.pallas_call(
        body, out_shape=shapes, in_specs=[vm] * 6, out_specs=[vm] * 5,
        scratch_shapes=[pltpu.VMEM((D_MODEL, 512), F32),
                        pltpu.SemaphoreType.DMA((n_arr, 7)), pltpu.SemaphoreType.DMA((n_arr, 7))],
        compiler_params=_params(vmem_mib=40), name="all_gather_w_in",
    )(w_in, w_out, w_mlp_in, w_mlp_out, conv_w, rnn_conv_w)


def _host_all_gather(step, n_steps, shards, fulls, send_sems, recv_sems, local_sems):
    x, y, c = _place()
    me = (x, y, c)
    my_id = _block_id((x, y), c)
    sibling = (x, y, 1 - c)
    chips = _other_chips(x, y)
    n_arr = len(shards)

    def copy(arr, k, block, to, src=None):
        dst = fulls[arr].at[block]
        return _remote_copy(dst if src is None else src, dst, send_sems.at[arr, k], recv_sems.at[arr, k], to)

    def local(arr):
        return pltpu.make_async_copy(shards[arr], fulls[arr].at[my_id], local_sems.at[arr])

    @pl.when(step == 0)
    def _():
        for arr in range(n_arr):
            local(arr).start()
            copy(arr, 0, my_id, sibling, shards[arr]).start()
            for j, chip in enumerate(chips):
                copy(arr, 1 + j, my_id, (*chip, c), shards[arr]).start()

    @pl.when(step == max(n_steps - 2, 0))
    def _():
        for j, chip in enumerate(chips):
            for arr in range(n_arr):
                copy(arr, 1 + j, _block_id(chip, c), me).wait_recv()
                copy(arr, 4 + j, _block_id(chip, c), sibling).start()

    @pl.when(step == n_steps - 1)
    def _():
        for arr in range(n_arr):
            copy(arr, 0, _block_id((x, y), 1 - c), me).wait_recv()
            for j, chip in enumerate(chips):
                copy(arr, 4 + j, _block_id(chip, 1 - c), me).wait_recv()
            for k in range(4):
                copy(arr, k, my_id, me, shards[arr]).wait_send()
            for j, chip in enumerate(chips):
                copy(arr, 4 + j, _block_id(chip, c), me).wait_send()
            local(arr).wait()


def _host_pair_exchange(step, n_steps, gs, sibs, send_sems, recv_sems):
    x, y, c = _place()
    sibling = (x, y, 1 - c)
    chips = [(x, y)] + _other_chips(x, y)

    def d2d(arr, q):
        return _remote_copy(gs[arr].at[_block_id(chips[q], 1 - c)], sibs[arr].at[q],
                            send_sems.at[arr, q], recv_sems.at[arr, q], sibling)

    @pl.when(step == 0)
    def _():
        for arr in range(len(gs)):
            for q in (1, 2, 3, 0):
                d2d(arr, q).start()

    @pl.when(step == n_steps - 1)
    def _():
        for arr in range(len(gs)):
            for q in range(4):
                d2d(arr, q).wait()


def _host_chip_exchange(step, n_steps, hsends, hrecvs, send_sems, recv_sems):
    x, y, c = _place()
    chips = _other_chips(x, y)

    def ici(arr, j):
        return _remote_copy(hsends[arr].at[j], hrecvs[arr].at[j], send_sems.at[arr, j], recv_sems.at[arr, j],
                            (*chips[j], c))

    @pl.when(step == 0)
    def _():
        for arr in range(len(hsends)):
            for j in range(3):
                ici(arr, j).start()

    @pl.when(step == n_steps - 1)
    def _():
        for arr in range(len(hsends)):
            for j in range(3):
                ici(arr, j).wait()


def _host_half_exchange(step, n_steps, parts, sibs, send_sems, recv_sems):
    x, y, c = _place()
    n_q, rows2, _ = parts.shape
    half = rows2 // 2

    def d2d(q):
        src = parts.at[q, pl.ds(pl.multiple_of((1 - c) * half, 16), half), :]
        return _remote_copy(src, sibs.at[q], send_sems.at[q], recv_sems.at[q], (x, y, 1 - c))

    @pl.when(step == 0)
    def _():
        for q in range(n_q):
            d2d(q).start()

    @pl.when(step == n_steps - 1)
    def _():
        for q in range(n_q):
            d2d(q).wait()


def _peer(x, y, c, k):
    return (x ^ ((k >> 2) & 1), y ^ ((k >> 1) & 1), c ^ (k & 1))


def _host_small_exchange(step, n_steps, vec_m, vec_b, wab, vrecv_m, vrecv_b, wrecv, send_sems, recv_sems, local_sems):
    x, y, c = _place()
    my_id = _block_id((x, y), c)
    wrows = wab.shape[0] // N_DEV

    def copies(k):
        to = _peer(x, y, c, k)
        block = wab.at[pl.ds(pl.multiple_of(_block_id(to[0:2], to[2]) * wrows, SUB), wrows), :]
        return [_remote_copy(vec_m, vrecv_m.at[my_id], send_sems.at[0, k], recv_sems.at[0, k], to),
                _remote_copy(vec_b, vrecv_b.at[my_id], send_sems.at[1, k], recv_sems.at[1, k], to),
                _remote_copy(block, wrecv.at[k], send_sems.at[2, k], recv_sems.at[2, k], to)]

    mine = [pltpu.make_async_copy(vec_m, vrecv_m.at[my_id], local_sems.at[0]),
            pltpu.make_async_copy(vec_b, vrecv_b.at[my_id], local_sems.at[1])]

    @pl.when(step == 0)
    def _():
        for cp in mine:
            cp.start()
        for k in range(1, N_DEV):
            for cp in copies(k):
                cp.start()

    @pl.when(step == n_steps - 1)
    def _():
        for k in range(1, N_DEV):
            for cp in copies(k):
                cp.wait()
        for cp in mine:
            cp.wait()


def _pair_sum_parts(parts, sibs, core):
    n_q, rows2, cols = parts.shape
    half = rows2 // 2

    def body(core_ref, g_ref, s_ref, o_ref):
        o_ref[0] = (g_ref[0, 0].astype(F32) + s_ref[0].astype(F32)).astype(BF16)

    block = (1, half, cols)
    grid_spec = pltpu.PrefetchScalarGridSpec(
        num_scalar_prefetch=1, grid=(n_q,),
        in_specs=[pl.BlockSpec((1, 1, half, cols), lambda q, cr: (q, cr[0], 0, 0)),
                  pl.BlockSpec(block, lambda q, cr: (q, 0, 0))],
        out_specs=pl.BlockSpec(block, lambda q, cr: (q, 0, 0)))
    return pl.pallas_call(
        body, grid_spec=grid_spec, out_shape=pltpu.HBM((n_q, half, cols), BF16),
        compiler_params=_params(("arbitrary",), 32), name="pair_sum_w_in",
    )(core, *_in_hbm(parts.reshape(n_q, 2, half, cols), sibs))


def _pair_sum(g, sib, name):
    _, rows, cols = g.shape
    x, y, c = _place()
    slots = jnp.stack([_block_id(chip, c) for chip in [(x, y)] + _other_chips(x, y)]).astype(jnp.int32)

    def body(slots_ref, g_ref, sib_ref, hs_ref, own_ref):
        q = pl.program_id(0)
        both = g_ref[0].astype(F32) + sib_ref[0].astype(F32)

        @pl.when(q == 0)
        def _():
            own_ref[...] = both

        @pl.when(q > 0)
        def _():
            hs_ref[0] = both.astype(BF16)

    block = (1, rows, cols)
    grid_spec = pltpu.PrefetchScalarGridSpec(
        num_scalar_prefetch=1, grid=(4,),
        in_specs=[pl.BlockSpec(block, lambda q, s: (s[q], 0, 0)), pl.BlockSpec(block, lambda q, s: (q, 0, 0))],
        out_specs=[pl.BlockSpec(block, lambda q, s: (jnp.maximum(q - 1, 0), 0, 0)),
                   pl.BlockSpec((rows, cols), lambda q, s: (0, 0))])
    return pl.pallas_call(
        body, grid_spec=grid_spec,
        out_shape=(pltpu.HBM((3, rows, cols), BF16), pltpu.HBM((rows, cols), F32)),
        compiler_params=_params(("arbitrary",), 32), name=name,
    )(slots, *_in_hbm(g, sib))


def _exchange_scratch(n_arr, n_copies):
    return [pltpu.SemaphoreType.DMA((n_arr, n_copies)), pltpu.SemaphoreType.DMA((n_arr, n_copies))]


def _final_small(vrecv_m, vrecv_b, wab, wrecv, vec_x):
    wrows = wab.shape[0] // N_DEV

    def body(vm_ref, vb_ref, w_ref, wr_ref, vx_ref, o_vec, o_w, xrecv, wred, x_send, x_recv, b_send, b_recv):
        x, y, c = _place()
        my_id = _block_id((x, y), c)
        my_rows = pl.ds(pl.multiple_of(my_id * wrows, SUB), wrows)

        def xcopy(k):
            return _remote_copy(vx_ref, xrecv.at[my_id], x_send.at[k], x_recv.at[k], _peer(x, y, c, k))

        def bcopy(k):
            return _remote_copy(wred, o_w.at[my_rows, :], b_send.at[k], b_recv.at[k], _peer(x, y, c, k))

        xrecv[my_id] = vx_ref[...]
        for k in range(1, N_DEV):
            xcopy(k).start()
        red = w_ref[my_rows, :]
        for k in range(1, N_DEV):
            red = red + wr_ref[k]
        wred[...] = red
        o_w[my_rows, :] = red
        for k in range(1, N_DEV):
            bcopy(k).start()
        for k in range(1, N_DEV):
            xcopy(k).wait_recv()
        for rows, ref in ((slice(0, 8), vm_ref), (slice(8, 24), vb_ref), (slice(24, 32), xrecv)):
            tot = ref[0]
            for s in range(1, N_DEV):
                tot = tot + ref[s]
            o_vec[rows, :] = tot
        for k in range(1, N_DEV):
            bcopy(k).wait_recv()
        for k in range(1, N_DEV):
            xcopy(k).wait_send()
            bcopy(k).wait_send()

    vm = pl.BlockSpec(memory_space=pltpu.VMEM)
    dma8 = pltpu.SemaphoreType.DMA((N_DEV,))
    return pl.pallas_call(
        body, out_shape=(jax.ShapeDtypeStruct((VEC_ROWS, D_MODEL), F32), jax.ShapeDtypeStruct(wab.shape, F32)),
        in_specs=[vm] * 5, out_specs=[vm] * 2,
        scratch_shapes=[pltpu.VMEM((N_DEV, SUB, D_MODEL), F32), pltpu.VMEM((wrows, HEAD_DIM), F32),
                        dma8, dma8, dma8, dma8],
        compiler_params=_params(vmem_mib=32), name="final_small",
    )(vrecv_m, vrecv_b, wab, wrecv, vec_x)


def _in_proj(x, g_mix, win_t, wout_shard, tm):
    t_len = x.shape[0]
    n_steps = t_len // tm

    def body(x_ref, g_ref, w_ref, wout_ref, u_ref, h_ref, wout_full, send_sems, recv_sems, local_sems):
        _host_all_gather(pl.program_id(0), n_steps, [wout_ref], [wout_full], send_sems, recv_sems, local_sems)
        xv = x_ref[...]
        h = (xv * _rms(xv) * g_ref[...]).astype(BF16)
        h_ref[...] = h
        u_ref[...] = _dot_nt(h, w_ref[...])

    return pl.pallas_call(
        body, grid=(n_steps,),
        in_specs=[pl.BlockSpec((tm, D_MODEL), lambda i: (i, 0)), pl.BlockSpec((1, D_MODEL), lambda i: (0, 0)),
                  pl.BlockSpec((IN_COLS, D_MODEL), lambda i: (0, 0)), HBM_SPEC],
        out_specs=[pl.BlockSpec((tm, IN_COLS), lambda i: (i, 0)), pl.BlockSpec((tm, D_MODEL), lambda i: (i, 0)),
                   HBM_SPEC],
        out_shape=[jax.ShapeDtypeStruct((t_len, IN_COLS), F32), jax.ShapeDtypeStruct((t_len, D_MODEL), BF16),
                   jax.ShapeDtypeStruct((N_DEV,) + wout_shard.shape, BF16)],
        scratch_shapes=_exchange_scratch(1, 7) + [pltpu.SemaphoreType.DMA((1,))],
        compiler_params=_params(("arbitrary",), 56), name="in_proj",
    )(x, g_mix, win_t, wout_shard)


def _conv3_chunk(u_ref, r, cv_prev, cw, row):
    gb = u_ref[pl.ds(r, SUB), OFF_GB:OFF_GB + CONV_WIDTH]
    gc = u_ref[pl.ds(r, SUB), OFF_GC:OFF_GC + CONV_WIDTH]
    v = u_ref[pl.ds(r, SUB), OFF_V:OFF_V + CONV_WIDTH]
    cv = gc * v
    cv_m1 = _down(cv, cv_prev, 1, row)
    cv_m2 = _down(cv, cv_prev, 2, row)
    cq = cw[2:3, :] * cv + cw[1:2, :] * cv_m1 + cw[0:1, :] * cv_m2
    return gb, gc, v, cv, cv_m1, cv_m2, cq


def _conv4_chunk(u_ref, r, xin_prev, rw, rb, row):
    xin = u_ref[pl.ds(r, SUB), OFF_XR:OFF_XR + LRU_WIDTH]
    m1 = _down(xin, xin_prev, 1, row)
    m2 = _down(xin, xin_prev, 2, row)
    m3 = _down(xin, xin_prev, 3, row)
    xr = rw[3:4, :] * xin + rw[2:3, :] * m1 + rw[1:2, :] * m2 + rw[0:1, :] * m3 + rb
    return xin, m1, m2, m3, xr


def _mixer_fwd(u, conv_w, rnn_conv_w, rnn_conv_b, wa, b_a, wx, b_x, lam, gnc, gnr, w1_shard, w2_shard, tm):
    t_len = u.shape[0]
    n_steps = t_len // tm
    n_chunks = tm // SUB

    def body(u_ref, cw_ref, rw_ref, rb_ref, wa_ref, ba_ref, wx_ref, bx_ref, lam_ref, gnc_ref, gnr_ref,
             w1_shard, w2_shard, hs_ref, y_ref, w1_full, w2_full,
             y_s, xr_s, pa_s, px_s, wabd, wxbd, cv_car, xin_car, h_car, send_sems, recv_sems, local_sems):
        _host_all_gather(pl.program_id(0), n_steps, [w1_shard, w2_shard], [w1_full, w2_full],
                         send_sems, recv_sems, local_sems)

        @pl.when(pl.program_id(0) == 0)
        def _():
            cv_car[...] = jnp.zeros(cv_car.shape, F32)
            xin_car[...] = jnp.zeros(xin_car.shape, F32)
            h_car[...] = jnp.zeros(h_car.shape, F32)
            wabd[...] = _expand_heads(wa_ref[...])
            wxbd[...] = _expand_heads(wx_ref[...])

        row_c = lax.broadcasted_iota(jnp.int32, (SUB, CONV_WIDTH), 0)
        row_r = lax.broadcasted_iota(jnp.int32, (SUB, LRU_WIDTH), 0)
        cw = cw_ref[...]
        rw = rw_ref[...]
        rb = rb_ref[...]
        g_c = gnc_ref[...]
        g_r = gnr_ref[...]
        sp_c = LRU_C * _softplus_neg(lam_ref[...])

        def convs(i, carry):
            cv_prev, xin_prev = carry
            r = pl.multiple_of(i * SUB, SUB)
            gb, _, _, cv, _, _, cq = _conv3_chunk(u_ref, r, cv_prev, cw, row_c)
            y_c = gb * cq
            y_s[pl.ds(r, SUB), 0:CONV_WIDTH] = y_c * _rms(y_c) * g_c
            xin, _, _, _, xr = _conv4_chunk(u_ref, r, xin_prev, rw, rb, row_r)
            xr_s[pl.ds(r, SUB), :] = xr
            return cv, xin

        cv_last, xin_last = _chunk_loop(n_chunks, convs, (cv_car[...], xin_car[...]))
        cv_car[...] = cv_last
        xin_car[...] = xin_last

        xrb = xr_s[...].astype(BF16)
        pa_s[...] = _block_diag_apply(xrb, wabd) + ba_ref[...]
        px_s[...] = _block_diag_apply(xrb, wxbd) + bx_ref[...]

        def recur(i, h_prev):
            r = pl.multiple_of(i * SUB, SUB)
            xr = xr_s[pl.ds(r, SUB), :]
            _, ii, a, mult, _ = _lru_gates(pa_s[pl.ds(r, SUB), :], px_s[pl.ds(r, SUB), :], sp_c)
            a_cum, b_cum = _scan8_fwd(a, mult * ii * xr, row_r)
            h = a_cum * h_prev + b_cum
            hs_ref[pl.ds(r, SUB), :] = h
            ge, _ = _gelu(u_ref[pl.ds(r, SUB), OFF_G:OFF_G + LRU_WIDTH])
            y_r = h * ge
            y_s[pl.ds(r, SUB), CONV_WIDTH:MIX_WIDTH] = y_r * _rms(y_r) * g_r
            return h[SUB - 1:SUB, :]

        h_car[...] = _chunk_loop(n_chunks, recur, h_car[...])

        y_ref[...] = y_s[...].astype(BF16)

    row_tile = lambda w: pl.BlockSpec((tm, w), lambda i: (i, 0))
    whole = lambda a: pl.BlockSpec(a.shape, lambda i: (0,) * a.ndim)
    smalls = (conv_w, rnn_conv_w, rnn_conv_b, wa, b_a, wx, b_x, lam, gnc, gnr)
    return pl.pallas_call(
        body, grid=(n_steps,),
        in_specs=[row_tile(IN_COLS)] + [whole(a) for a in smalls] + [HBM_SPEC, HBM_SPEC],
        out_specs=[row_tile(LRU_WIDTH), row_tile(MIX_WIDTH), HBM_SPEC, HBM_SPEC],
        out_shape=[jax.ShapeDtypeStruct((t_len, LRU_WIDTH), F32), jax.ShapeDtypeStruct((t_len, MIX_WIDTH), BF16),
                   jax.ShapeDtypeStruct((N_DEV,) + w1_shard.shape, BF16),
                   jax.ShapeDtypeStruct((N_DEV,) + w2_shard.shape, BF16)],
        scratch_shapes=[pltpu.VMEM((tm, MIX_WIDTH), F32), pltpu.VMEM((tm, LRU_WIDTH), F32),
                        pltpu.VMEM((tm, LRU_WIDTH), F32), pltpu.VMEM((tm, LRU_WIDTH), F32),
                        pltpu.VMEM((LRU_WIDTH, GROUP), BF16), pltpu.VMEM((LRU_WIDTH, GROUP), BF16),
                        pltpu.VMEM((SUB, CONV_WIDTH), F32), pltpu.VMEM((SUB, LRU_WIDTH), F32),
                        pltpu.VMEM((1, LRU_WIDTH), F32)] + _exchange_scratch(2, 7) + [pltpu.SemaphoreType.DMA((2,))],
        compiler_params=_params(("arbitrary",), 56), name="mixer_fwd",
    )(u, *smalls, w1_shard, w2_shard)


def _mlp_fwd_bwd(x, y, target, g_mlp, g_f, w_out, w1, w2, tm):
    t_len = x.shape[0]
    n_steps = t_len // tm
    n_blk, _, blk = w1.shape

    def body(x_ref, y_ref, tg_ref, gm_ref, gf_ref, wout_hbm, w1_hbm, w2_hbm,
             dx1_ref, h2_ref, dx2_ref, vec_ref, z_hbm, dpre_hbm,
             wout_s, w1_s, w2_s, rp_s, z_s, dp_s, sem, out_sem):
        step = pl.program_id(0)
        rows = pl.ds(pl.multiple_of(step * tm, tm), tm)
        z_out = pltpu.make_async_copy(z_s, z_hbm.at[rows, :], out_sem.at[0])
        dp_out = pltpu.make_async_copy(dp_s, dpre_hbm.at[rows, :], out_sem.at[1])

        @pl.when(step == 0)
        def _():
            loads = [pltpu.make_async_copy(src, dst, sem.at[k])
                     for k, (src, dst) in enumerate(((wout_hbm, wout_s), (w1_hbm, w1_s), (w2_hbm, w2_s)))]
            for cp in loads:
                cp.start()
            vec_ref[...] = jnp.zeros(vec_ref.shape, F32)
            for cp in loads:
                cp.wait()

        x1v = x_ref[...] + jnp.dot(y_ref[...], wout_s[...], preferred_element_type=F32)
        g_m = gm_ref[...]
        g_o = gf_ref[...]
        r2 = _rms(x1v)
        x1h = x1v * r2
        h2 = (x1h * g_m).astype(BF16)
        h2_ref[...] = h2
        x2 = x1v

        @pl.when(step > 0)
        def _():
            z_out.wait()

        for k in range(n_blk):
            rp = jnp.maximum(jnp.dot(h2, w1_s[k], preferred_element_type=F32), 0.0)
            rp_s[:, k * blk:(k + 1) * blk] = rp.astype(BF16)
            zb = (rp * rp).astype(BF16)
            z_s[:, k * blk:(k + 1) * blk] = zb
            x2 = x2 + jnp.dot(zb, w2_s[k * blk:(k + 1) * blk, :], preferred_element_type=F32)
        z_out.start()
        r3 = _rms(x2)
        x2h = x2 * r3
        err = x2h * g_o - tg_ref[...]
        dout = err * (1.0 / D_MODEL)
        vec_ref[ROW_LOSS:ROW_LOSS + 1, :] += (0.5 / D_MODEL) * jnp.sum(err * err, axis=0, keepdims=True)
        vec_ref[ROW_GF:ROW_GF + 1, :] += jnp.sum(dout * x2h, axis=0, keepdims=True)
        dx2 = _rms_bwd(dout, x2h, r3, g_o)
        dx2b = dx2.astype(BF16)
        dx2_ref[...] = dx2b
        dh2 = jnp.zeros((tm, D_MODEL), F32)

        @pl.when(step > 0)
        def _():
            dp_out.wait()

        for k in range(n_blk):
            dz = _dot_nt(dx2b, w2_s[k * blk:(k + 1) * blk, :])
            dpb = (dz * 2.0 * rp_s[:, k * blk:(k + 1) * blk].astype(F32)).astype(BF16)
            dp_s[:, k * blk:(k + 1) * blk] = dpb
            dh2 = dh2 + _dot_nt(dpb, w1_s[k])
        dp_out.start()
        vec_ref[ROW_GMLP:ROW_GMLP + 1, :] += jnp.sum(dh2 * x1h, axis=0, keepdims=True)
        dx1_ref[...] = dx2 + _rms_bwd(dh2, x1h, r2, g_m)

        @pl.when(step == n_steps - 1)
        def _():
            z_out.wait()
            dp_out.wait()

    row_tile = lambda w: pl.BlockSpec((tm, w), lambda i: (i, 0))
    vec_spec = pl.BlockSpec((1, D_MODEL), lambda i: (0, 0))
    outs = pl.pallas_call(
        body, grid=(n_steps,),
        in_specs=[row_tile(D_MODEL), row_tile(MIX_WIDTH), row_tile(D_MODEL), vec_spec, vec_spec,
                  HBM_SPEC, HBM_SPEC, HBM_SPEC],
        out_specs=[row_tile(D_MODEL), row_tile(D_MODEL), row_tile(D_MODEL),
                   pl.BlockSpec((SUB, D_MODEL), lambda i: (0, 0)), HBM_SPEC, HBM_SPEC],
        out_shape=[jax.ShapeDtypeStruct((t_len, D_MODEL), F32), jax.ShapeDtypeStruct((t_len, D_MODEL), BF16),
                   jax.ShapeDtypeStruct((t_len, D_MODEL), BF16), jax.ShapeDtypeStruct((SUB, D_MODEL), F32),
                   jax.ShapeDtypeStruct((t_len, D_FF), BF16), jax.ShapeDtypeStruct((t_len, D_FF), BF16)],
        scratch_shapes=[pltpu.VMEM(w_out.shape, BF16), pltpu.VMEM(w1.shape, BF16), pltpu.VMEM(w2.shape, BF16),
                        pltpu.VMEM((tm, D_FF), BF16), pltpu.VMEM((tm, D_FF), BF16), pltpu.VMEM((tm, D_FF), BF16),
                        pltpu.SemaphoreType.DMA((3,)), pltpu.SemaphoreType.DMA((2,))],
        compiler_params=_params(("arbitrary",), 58), name="mlp_fwd_bwd",
    )(x, y, target, g_mlp, g_f, w_out, w1, w2)
    dx1, h2, dx2, vec, z, dpre = outs
    return dx1, z, dpre, h2, dx2, vec


def _mixer_bwd(u, hs, dx1, conv_w, rnn_conv_w, rnn_conv_b, wa, b_a, wx, b_x, lam, gnc, gnr, w_out,
               chip_sums, g_wout, tm):
    t_len = u.shape[0]
    n_tiles = t_len // tm
    n_chunks = tm // SUB
    per_tile = tm // SUB
    n_sums = len(chip_sums)

    def body(u_ref, up_ref, hs_ref, hp_ref, dx1_ref, cw_ref, rw_ref, rb_ref, wa_ref, ba_ref, wx_ref, bx_ref,
             lam_ref, gnc_ref, gnr_ref, wout_ref, *rest):
        hsends = rest[0:n_sums]
        gwout_ref = rest[n_sums]
        du_ref, vec_ref, wab_ref = rest[n_sums + 1:n_sums + 4]
        hrecvs = rest[n_sums + 4:2 * n_sums + 4]
        sib_wout = rest[2 * n_sums + 4]
        (du_s, dy_s, xr_s, pa_s, px_s, dpa_s, dpx_s, dxr_s, wabd, wxbd, acc, dwa_acc, dwx_acc,
         a_car, dh_car, dcq_car, dxr_car, i_send, i_recv, d_send, d_recv) = rest[2 * n_sums + 5:]
        step = pl.program_id(0)
        _host_chip_exchange(step, n_tiles, hsends, hrecvs, i_send, i_recv)
        _host_pair_exchange(step, n_tiles, [gwout_ref], [sib_wout], d_send, d_recv)
        has_prev = (step < n_tiles - 1).astype(F32)

        @pl.when(step == 0)
        def _():
            acc[...] = jnp.zeros(acc.shape, F32)
            dwa_acc[...] = jnp.zeros(dwa_acc.shape, F32)
            dwx_acc[...] = jnp.zeros(dwx_acc.shape, F32)
            a_car[...] = jnp.ones(a_car.shape, F32)
            dh_car[...] = jnp.zeros(dh_car.shape, F32)
            dcq_car[...] = jnp.zeros(dcq_car.shape, F32)
            dxr_car[...] = jnp.zeros(dxr_car.shape, F32)
            wabd[...] = _expand_heads(wa_ref[...])
            wxbd[...] = _expand_heads(wx_ref[...])

        row_c = lax.broadcasted_iota(jnp.int32, (SUB, CONV_WIDTH), 0)
        row_r = lax.broadcasted_iota(jnp.int32, (SUB, LRU_WIDTH), 0)
        cw = cw_ref[...]
        rw = rw_ref[...]
        rb = rb_ref[...]
        g_c = gnc_ref[...]
        g_r = gnr_ref[...]
        sp_c = LRU_C * _softplus_neg(lam_ref[...])

        up = up_ref[...] * has_prev
        cv_before = up[:, OFF_GC:OFF_GC + CONV_WIDTH] * up[:, OFF_V:OFF_V + CONV_WIDTH]
        xin_before = up[:, OFF_XR:OFF_XR + LRU_WIDTH]
        hs_before = hp_ref[...] * has_prev

        dy_s[...] = _dot_nt(dx1_ref[...].astype(BF16), wout_ref[...])

        def conv4_fwd(i, xin_prev):
            r = pl.multiple_of(i * SUB, SUB)
            xin, _, _, _, xr = _conv4_chunk(u_ref, r, xin_prev, rw, rb, row_r)
            xr_s[pl.ds(r, SUB), :] = xr
            return xin

        _chunk_loop(n_chunks, conv4_fwd, xin_before)
        xrb = xr_s[...].astype(BF16)
        pa_s[...] = _block_diag_apply(xrb, wabd) + ba_ref[...]
        px_s[...] = _block_diag_apply(xrb, wxbd) + bx_ref[...]

        def recur_bwd(j, carry):
            a_later, dh_later = carry
            i = n_chunks - 1 - j
            r = pl.multiple_of(i * SUB, SUB)
            rp = pl.multiple_of(jnp.maximum(i - 1, 0) * SUB, SUB)
            xr = xr_s[pl.ds(r, SUB), :]
            hs_c = hs_ref[pl.ds(r, SUB), :]
            hs_prev = jnp.where(i == 0, hs_before, hs_ref[pl.ds(rp, SUB), :])
            h_m1 = _down(hs_c, hs_prev, 1, row_r)
            ra, ii, a, mult, inv_mult = _lru_gates(pa_s[pl.ds(r, SUB), :], px_s[pl.ds(r, SUB), :], sp_c)
            ge, dge = _gelu(u_ref[pl.ds(r, SUB), OFF_G:OFF_G + LRU_WIDTH])
            y_r = hs_c * ge
            rr = _rms(y_r)
            yhat = y_r * rr
            dyn = dy_s[pl.ds(r, SUB), CONV_WIDTH:MIX_WIDTH]
            acc[ACC_GNR] += dyn * yhat
            dy_r = _rms_bwd(dyn, yhat, rr, g_r)
            du_s[pl.ds(r, SUB), OFF_G:OFF_G + LRU_WIDTH] = dy_r * hs_c * dge
            a_cum, d_cum = _scan8_rev(_up(a, a_later, 1, row_r), dy_r * ge, row_r)
            dh = a_cum * dh_later + d_cum
            dmult = dh * ii * xr
            dii = dh * mult * xr
            dxr_s[pl.ds(r, SUB), :] = dh * mult * ii
            dla = dh * h_m1 * a - dmult * a * a * inv_mult
            acc[ACC_SP] += -dla * ra
            dpa = -dla * sp_c * ra * (1.0 - ra)
            dpx = dii * ii * (1.0 - ii)
            acc[ACC_BA] += dpa
            acc[ACC_BX] += dpx
            dpa_s[pl.ds(r, SUB), :] = dpa
            dpx_s[pl.ds(r, SUB), :] = dpx
            return a, dh[0:1, :]

        a_first, dh_first = _chunk_loop(n_chunks, recur_bwd, (a_car[...], dh_car[...]))
        a_car[...] = a_first
        dh_car[...] = dh_first

        dpab = dpa_s[...].astype(BF16)
        dpxb = dpx_s[...].astype(BF16)
        dxr_s[...] += _block_diag_apply_t(dpab, wabd) + _block_diag_apply_t(dpxb, wxbd)
        for g in range(LRU_WIDTH // GROUP):
            cols = slice(g * GROUP, (g + 1) * GROUP)
            dwa_acc[cols, :] += _dot_tn(xrb[:, cols], dpab[:, cols])
            dwx_acc[cols, :] += _dot_tn(xrb[:, cols], dpxb[:, cols])

        def convs_bwd(j, carry):
            dcq_later, dxr_later = carry
            i = n_chunks - 1 - j
            r = pl.multiple_of(i * SUB, SUB)
            rp = pl.multiple_of(jnp.maximum(i - 1, 0) * SUB, SUB)
            cv_prev = jnp.where(i == 0, cv_before,
                                u_ref[pl.ds(rp, SUB), OFF_GC:OFF_GC + CONV_WIDTH]
                                * u_ref[pl.ds(rp, SUB), OFF_V:OFF_V + CONV_WIDTH])
            gb, gc, v, cv, cv_m1, cv_m2, cq = _conv3_chunk(u_ref, r, cv_prev, cw, row_c)
            y_c = gb * cq
            rc = _rms(y_c)
            yhat = y_c * rc
            dyn = dy_s[pl.ds(r, SUB), 0:CONV_WIDTH]
            acc[ACC_GNC, :, 0:CONV_WIDTH] += dyn * yhat
            dy_c = _rms_bwd(dyn, yhat, rc, g_c)
            dcq = dy_c * gb
            dcv = (cw[2:3, :] * dcq + cw[1:2, :] * _up(dcq, dcq_later, 1, row_c)
                   + cw[0:1, :] * _up(dcq, dcq_later, 2, row_c))
            acc[ACC_CW + 2, :, 0:CONV_WIDTH] += dcq * cv
            acc[ACC_CW + 1, :, 0:CONV_WIDTH] += dcq * cv_m1
            acc[ACC_CW + 0, :, 0:CONV_WIDTH] += dcq * cv_m2
            du_s[pl.ds(r, SUB), OFF_GB:OFF_GB + CONV_WIDTH] = dy_c * cq
            du_s[pl.ds(r, SUB), OFF_GC:OFF_GC + CONV_WIDTH] = dcv * v
            du_s[pl.ds(r, SUB), OFF_V:OFF_V + CONV_WIDTH] = dcv * gc

            xin_prev = jnp.where(i == 0, xin_before, u_ref[pl.ds(rp, SUB), OFF_XR:OFF_XR + LRU_WIDTH])
            xin, m1, m2, m3, _ = _conv4_chunk(u_ref, r, xin_prev, rw, rb, row_r)
            dxr = dxr_s[pl.ds(r, SUB), :]
            du_s[pl.ds(r, SUB), OFF_XR:OFF_XR + LRU_WIDTH] = (
                rw[3:4, :] * dxr + rw[2:3, :] * _up(dxr, dxr_later, 1, row_r)
                + rw[1:2, :] * _up(dxr, dxr_later, 2, row_r) + rw[0:1, :] * _up(dxr, dxr_later, 3, row_r))
            acc[ACC_RW + 3] += dxr * xin
            acc[ACC_RW + 2] += dxr * m1
            acc[ACC_RW + 1] += dxr * m2
            acc[ACC_RW + 0] += dxr * m3
            acc[ACC_BR] += dxr
            return dcq, dxr

        dcq_first, dxr_first = _chunk_loop(n_chunks, convs_bwd, (dcq_car[...], dxr_car[...]))
        dcq_car[...] = dcq_first
        dxr_car[...] = dxr_first

        du_ref[...] = du_s[...].astype(BF16)

        @pl.when(step == n_tiles - 1)
        def _():
            vec_ref[...] = jnp.zeros(vec_ref.shape, F32)
            rows = {ACC_GNC: ROW_GNC, ACC_GNR: ROW_GNR, ACC_BR: ROW_BR, ACC_BA: ROW_BA, ACC_BX: ROW_BX}
            for k in range(3):
                rows[ACC_CW + k] = ROW_CW + k
            for k in range(4):
                rows[ACC_RW + k] = ROW_RW + k
            for slot, out_row in rows.items():
                o = out_row - ROW_GNC
                vec_ref[o:o + 1, :] = jnp.sum(acc[slot], axis=0, keepdims=True)
            lam_v = lam_ref[...]
            dsp = jnp.sum(acc[ACC_SP], axis=0, keepdims=True)
            o = ROW_LAM - ROW_GNC
            vec_ref[o:o + 1, :] = -dsp * LRU_C / (1.0 + jnp.exp(lam_v))
            wab_ref[0:LRU_WIDTH, :] = _fold_heads(dwa_acc[...])
            wab_ref[LRU_WIDTH:2 * LRU_WIDTH, :] = _fold_heads(dwx_acc[...])

    rev = lambda w: pl.BlockSpec((tm, w), lambda s: (n_tiles - 1 - s, 0))
    before = lambda w: pl.BlockSpec((SUB, w), lambda s: (jnp.maximum((n_tiles - 1 - s) * per_tile - 1, 0), 0))
    whole = lambda a: pl.BlockSpec(a.shape, lambda s: (0,) * a.ndim)
    smalls = (conv_w, rnn_conv_w, rnn_conv_b, wa, b_a, wx, b_x, lam, gnc, gnr, w_out)
    full = lambda w: pltpu.VMEM((tm, w), F32)
    return pl.pallas_call(
        body, grid=(n_tiles,),
        in_specs=[rev(IN_COLS), before(IN_COLS), rev(LRU_WIDTH), before(LRU_WIDTH), rev(D_MODEL)]
        + [whole(a) for a in smalls] + [HBM_SPEC] * (n_sums + 1),
        out_specs=[rev(IN_COLS), pl.BlockSpec((16, D_MODEL), lambda s: (0, 0)),
                   pl.BlockSpec((2 * LRU_WIDTH, HEAD_DIM), lambda s: (0, 0))] + [HBM_SPEC] * (n_sums + 1),
        out_shape=[jax.ShapeDtypeStruct((t_len, IN_COLS), BF16), jax.ShapeDtypeStruct((16, D_MODEL), F32),
                   jax.ShapeDtypeStruct((2 * LRU_WIDTH, HEAD_DIM), F32)]
        + [jax.ShapeDtypeStruct(s.shape, BF16) for s in chip_sums]
        + [jax.ShapeDtypeStruct((4,) + g_wout.shape[1:], BF16)],
        scratch_shapes=[full(IN_COLS), full(MIX_WIDTH), full(LRU_WIDTH), full(LRU_WIDTH), full(LRU_WIDTH),
                        full(LRU_WIDTH), full(LRU_WIDTH), full(LRU_WIDTH),
                        pltpu.VMEM((LRU_WIDTH, GROUP), BF16), pltpu.VMEM((LRU_WIDTH, GROUP), BF16),
                        pltpu.VMEM((N_ACC, SUB, LRU_WIDTH), F32),
                        pltpu.VMEM((LRU_WIDTH, GROUP), F32), pltpu.VMEM((LRU_WIDTH, GROUP), F32),
                        pltpu.VMEM((SUB, LRU_WIDTH), F32), pltpu.VMEM((1, LRU_WIDTH), F32),
                        pltpu.VMEM((SUB, CONV_WIDTH), F32), pltpu.VMEM((SUB, LRU_WIDTH), F32)]
        + _exchange_scratch(n_sums, 3) + _exchange_scratch(1, 4),
        compiler_params=_params(("arbitrary",), 56), name="mixer_bwd",
    )(u, u, hs, hs, dx1, *smalls, *chip_sums, g_wout)


def _in_proj_bwd(du, dx1, x, g_mix, win_t, tm, chip_sums, g_own):
    t_len = x.shape[0]
    n_steps = t_len // tm

    def body(du_ref, dx1_ref, x_ref, g_ref, w_ref, hs_ref, gown_ref,
             dx_ref, vec_ref, landed_ref, sib_ref, i_send, i_recv, d_send, d_recv):
        step = pl.program_id(0)
        _host_chip_exchange(step, n_steps, [hs_ref], [landed_ref], i_send, i_recv)
        _host_half_exchange(step, n_steps, gown_ref, sib_ref, d_send, d_recv)

        @pl.when(step == 0)
        def _():
            vec_ref[...] = jnp.zeros(vec_ref.shape, F32)

        dh = jnp.dot(du_ref[...], w_ref[...], preferred_element_type=F32)
        xv = x_ref[...]
        r1 = _rms(xv)
        xh = xv * r1
        vec_ref[0:1, :] += jnp.sum(dh * xh, axis=0, keepdims=True)
        dx_ref[...] = dx1_ref[...] + _rms_bwd(dh, xh, r1, g_ref[...])

    row_tile = lambda w: pl.BlockSpec((tm, w), lambda i: (i, 0))
    half_shape = (g_own.shape[0], g_own.shape[1] // 2, g_own.shape[2])
    return pl.pallas_call(
        body, grid=(n_steps,),
        in_specs=[row_tile(IN_COLS), row_tile(D_MODEL), row_tile(D_MODEL), pl.BlockSpec((1, D_MODEL), lambda i: (0, 0)),
                  pl.BlockSpec((IN_COLS, D_MODEL), lambda i: (0, 0))] + [HBM_SPEC] * 2,
        out_specs=[row_tile(D_MODEL), pl.BlockSpec((SUB, D_MODEL), lambda i: (0, 0))] + [HBM_SPEC] * 2,
        out_shape=[jax.ShapeDtypeStruct((t_len, D_MODEL), F32), jax.ShapeDtypeStruct((SUB, D_MODEL), F32),
                   jax.ShapeDtypeStruct(chip_sums.shape, BF16), jax.ShapeDtypeStruct(half_shape, BF16)],
        scratch_shapes=_exchange_scratch(1, 3) + [pltpu.SemaphoreType.DMA((1,)), pltpu.SemaphoreType.DMA((1,))],
        compiler_params=_params(("arbitrary",), 56), name="in_proj_bwd",
    )(du, dx1, x, g_mix, win_t, chip_sums, g_own)


def _tn_weight_grad(a, b, tk, name, pair=(), col_blocks=1):
    t_len, m = a.shape
    n = b.shape[1]
    n_steps = t_len // tk
    sent = tuple(pair)
    n_sent = len(sent)

    def body(a_ref, b_ref, *rest):
        srcs = rest[0:n_sent]
        o_ref = rest[n_sent]
        dsts = rest[n_sent + 1:2 * n_sent + 1]
        acc = rest[2 * n_sent + 1]
        sems = rest[2 * n_sent + 2:]
        j = pl.program_id(0)
        if pair:
            _host_pair_exchange(j, n_steps, srcs, dsts, *sems)

        @pl.when(j == 0)
        def _():
            acc[...] = jnp.zeros(acc.shape, F32)

        acc[...] += _dot_tn(a_ref[...].astype(BF16), b_ref[...].astype(BF16))

        @pl.when(j == n_steps - 1)
        def _():
            if col_blocks == 1:
                o_ref[...] = acc[...].astype(BF16)
            else:
                for k in range(col_blocks):
                    o_ref[k] = acc[:, k * nb:(k + 1) * nb].astype(BF16)

    nb = n // col_blocks
    out_dims = (m, n) if col_blocks == 1 else (col_blocks, m, nb)
    landed = [jax.ShapeDtypeStruct((4,) + g.shape[1:], BF16) for g in pair]
    scratch = [pltpu.VMEM((m, n), F32)]
    if n_sent:
        scratch += _exchange_scratch(n_sent, 4)
    return pl.pallas_call(
        body, grid=(n_steps,),
        in_specs=[pl.BlockSpec((tk, m), lambda j: (j, 0)), pl.BlockSpec((tk, n), lambda j: (j, 0))]
        + [HBM_SPEC] * n_sent,
        out_specs=[pl.BlockSpec(out_dims, lambda j: (0,) * len(out_dims))] + [HBM_SPEC] * n_sent,
        out_shape=[jax.ShapeDtypeStruct(out_dims, BF16)] + landed,
        scratch_shapes=scratch,
        compiler_params=_params(("arbitrary",), 56), name=name,
    )(a, b, *sent)


def _w_in_grad_part(du, h, tk, name, chip_ids, chip=(), halves=None, small=None):
    t_len = du.shape[0]
    n_t = t_len // tk
    n_q = chip_ids.shape[0]
    width = 2 * (IN_COLS // N_DEV)
    n_steps = n_q * n_t
    n_chip = len(chip)
    sent = tuple(chip) + (() if halves is None else (halves,)) + (() if small is None else tuple(small))
    n_sent = len(sent)

    def body(ids_ref, a_ref, b_ref, *rest):
        srcs = rest[0:n_sent]
        o_ref = rest[n_sent]
        dsts = rest[n_sent + 1:2 * n_sent + 1]
        acc = rest[2 * n_sent + 1]
        sems = list(rest[2 * n_sent + 2:])
        j = pl.program_id(1)
        step = pl.program_id(0) * n_t + j
        if chip:
            _host_chip_exchange(step, n_steps, srcs[0:n_chip], dsts[0:n_chip], sems.pop(0), sems.pop(0))
        if halves is not None:
            _host_half_exchange(step, n_steps, srcs[n_chip], dsts[n_chip], sems.pop(0), sems.pop(0))
        if small is not None:
            _host_small_exchange(step, n_steps, *srcs[n_sent - 3:], *dsts[n_sent - 3:], *sems)

        @pl.when(j == 0)
        def _():
            acc[...] = jnp.zeros(acc.shape, F32)

        acc[...] += _dot_tn(a_ref[...], b_ref[...])

        @pl.when(j == n_t - 1)
        def _():
            o_ref[0] = acc[...].astype(BF16)

    landed = [jax.ShapeDtypeStruct(s.shape, BF16) for s in chip]
    scratch = [pltpu.VMEM((width, D_MODEL), F32)]
    if chip:
        scratch += _exchange_scratch(len(chip), 3)
    if halves is not None:
        landed.append(jax.ShapeDtypeStruct((halves.shape[0], halves.shape[1] // 2, halves.shape[2]), BF16))
        scratch += [pltpu.SemaphoreType.DMA((halves.shape[0],)), pltpu.SemaphoreType.DMA((halves.shape[0],))]
    if small is not None:
        vec_m, vec_b, wab = small
        landed += [jax.ShapeDtypeStruct((N_DEV,) + vec_m.shape, F32), jax.ShapeDtypeStruct((N_DEV,) + vec_b.shape, F32),
                   jax.ShapeDtypeStruct((N_DEV, wab.shape[0] // N_DEV, wab.shape[1]), F32)]
        scratch += _exchange_scratch(3, N_DEV) + [pltpu.SemaphoreType.DMA((2,))]
    grid_spec = pltpu.PrefetchScalarGridSpec(
        num_scalar_prefetch=1, grid=(n_q, n_t),
        in_specs=[pl.BlockSpec((tk, width), lambda q, j, ids: (j, ids[q])),
                  pl.BlockSpec((tk, D_MODEL), lambda q, j, ids: (j, 0))] + [HBM_SPEC] * n_sent,
        out_specs=[pl.BlockSpec((1, width, D_MODEL), lambda q, j, ids: (q, 0, 0))] + [HBM_SPEC] * n_sent,
        scratch_shapes=scratch)
    return pl.pallas_call(
        body, grid_spec=grid_spec, out_shape=[jax.ShapeDtypeStruct((n_q, width, D_MODEL), BF16)] + landed,
        compiler_params=_params(("arbitrary", "arbitrary"), 40), name=name,
    )(chip_ids, du, h, *sent)


def _adamw(w, g, m, v):
    m = ADAM_B1 * m + (1.0 - ADAM_B1) * g
    v = ADAM_B2 * v + (1.0 - ADAM_B2) * (g * g)
    delta = -ADAM_LR * ((m / BC1) / (jnp.sqrt(v / BC2) + ADAM_EPS) + ADAM_WD * w)
    return delta, m, v


def _update_sharded(g, landed, w, m, v, rows_blk, name):
    rows, cols = w.shape

    def body(g_ref, l_ref, w_ref, m_ref, v_ref, og, od, om, ov):
        gv = g_ref[...]
        for j in range(3):
            gv = gv + l_ref[j].astype(F32)
        delta, mn, vn = _adamw(w_ref[...], gv, m_ref[...], v_ref[...])
        og[...] = gv
        od[...] = delta
        om[...] = mn
        ov[...] = vn

    blk = pl.BlockSpec((rows_blk, cols), lambda i: (i, 0))
    shape = pltpu.HBM((rows, cols), F32)
    return pl.pallas_call(
        body, grid=(rows // rows_blk,),
        in_specs=[blk, pl.BlockSpec((3, rows_blk, cols), lambda i: (0, i, 0)), blk, blk, blk],
        out_specs=[blk] * 4, out_shape=[shape] * 4,
        compiler_params=_params(("arbitrary",), 32), name=name,
    )(*_in_hbm(g, landed, w, m, v))


def _update_w_in(g_own, sib_own, landed, w, m, v, core, rows_blk):
    rows, cols = w.shape
    pad_cols = -(-cols // 128) * 128

    def body(core_ref, g_ref, s_ref, l_ref, w_ref, m_ref, v_ref, og, od, om, ov, padbuf, turned):
        gt = g_ref[0, 0].astype(F32) + s_ref[0].astype(F32)
        for j in range(3):
            gt = gt + l_ref[j].astype(F32)
        padbuf[...] = jnp.zeros(padbuf.shape, F32)
        padbuf[0:cols, :] = gt
        turned[...] = padbuf[...].T
        gv = turned[:, 0:cols]
        delta, mn, vn = _adamw(w_ref[...], gv, m_ref[...], v_ref[...])
        og[...] = gv
        od[...] = delta
        om[...] = mn
        ov[...] = vn

    blk = pl.BlockSpec((rows_blk, cols), lambda i, cr: (i, 0))
    grid_spec = pltpu.PrefetchScalarGridSpec(
        num_scalar_prefetch=1, grid=(rows // rows_blk,),
        in_specs=[pl.BlockSpec((1, 1, cols, rows_blk), lambda i, cr: (0, cr[0], 0, i)),
                  pl.BlockSpec((1, cols, rows_blk), lambda i, cr: (0, 0, i)),
                  pl.BlockSpec((3, cols, rows_blk), lambda i, cr: (0, 0, i)), blk, blk, blk],
        out_specs=[blk] * 4,
        scratch_shapes=[pltpu.VMEM((pad_cols, rows_blk), F32), pltpu.VMEM((rows_blk, pad_cols), F32)])
    return pl.pallas_call(
        body, grid_spec=grid_spec, out_shape=[pltpu.HBM((rows, cols), F32)] * 4,
        compiler_params=_params(("arbitrary",), 32), name="update_w_in",
    )(core, *_in_hbm(g_own.reshape(1, 2, cols, rows), sib_own, landed, w, m, v))


def _update_small(vsum, wsum, g_cw, g_rw, weights, moments_m, moments_v):
    n = len(weights)

    def body(*refs):
        vs, ws, gcw, grw = refs[0:4]
        w_refs = refs[4:4 + n]
        m_refs = refs[4 + n:4 + 2 * n]
        v_refs = refs[4 + 2 * n:4 + 3 * n]
        outs = refs[4 + 3 * n:]
        loss_ref = outs[0]
        loss_ref[...] = jnp.sum(vs[ROW_LOSS:ROW_LOSS + 1, :], axis=1, keepdims=True)
        grads = [
            vs[ROW_GMIX:ROW_GMIX + 1, :], gcw[...], grw[...], vs[ROW_BR:ROW_BR + 1, :],
            ws[0:LRU_WIDTH, :], vs[ROW_BA:ROW_BA + 1, :], ws[LRU_WIDTH:2 * LRU_WIDTH, :], vs[ROW_BX:ROW_BX + 1, :],
            vs[ROW_LAM:ROW_LAM + 1, :], vs[ROW_GNC:ROW_GNC + 1, 0:CONV_WIDTH], vs[ROW_GNR:ROW_GNR + 1, :],
            vs[ROW_GMLP:ROW_GMLP + 1, :], vs[ROW_GF:ROW_GF + 1, :],
        ]
        for k in range(n):
            gk = grads[k]
            delta, mn, vn = _adamw(w_refs[k][...], gk, m_refs[k][...], v_refs[k][...])
            outs[1 + 4 * k][...] = gk
            outs[2 + 4 * k][...] = delta
            outs[3 + 4 * k][...] = mn
            outs[4 + 4 * k][...] = vn

    whole = lambda a: pl.BlockSpec(a.shape, lambda i: (0,) * len(a.shape))
    out_shape = [jax.ShapeDtypeStruct((1, 1), F32)]
    for w in weights:
        out_shape += [jax.ShapeDtypeStruct(w.shape, F32)] * 4
    args = (vsum, wsum, g_cw, g_rw, *weights, *moments_m, *moments_v)
    return pl.pallas_call(
        body, grid=(1,), out_shape=out_shape, in_specs=[whole(a) for a in args], out_specs=[whole(s) for s in out_shape],
        compiler_params=_params(("arbitrary",), 32), name="update_small",
    )(*args)


def kernel(x, norm_mix_g, w_in, conv_w, rnn_conv_w, rnn_conv_b, w_a, b_a, w_x, b_x, lru_lambda, g_norm_conv, g_norm_rnn, w_out, norm_mlp_g, w_mlp_in, w_mlp_out, final_norm_g, loss_target, m_norm_mix_g, m_w_in, m_conv_w, m_rnn_conv_w, m_rnn_conv_b, m_w_a, m_b_a, m_w_x, m_b_x, m_lru_lambda, m_g_norm_conv, m_g_norm_rnn, m_w_out, m_norm_mlp_g, m_w_mlp_in, m_w_mlp_out, m_final_norm_g, v_norm_mix_g, v_w_in, v_conv_w, v_rnn_conv_w, v_rnn_conv_b, v_w_a, v_b_a, v_w_x, v_b_x, v_lru_lambda, v_g_norm_conv, v_g_norm_rnn, v_w_out, v_norm_mlp_g, v_w_mlp_in, v_w_mlp_out, v_final_norm_g):
    t_len = x.shape[1]
    my_id = 4 * lax.axis_index("x") + 2 * lax.axis_index("y") + lax.axis_index("c")
    tm = min(256, t_len)
    tb = min(512, t_len)
    tk = min(512, t_len)

    xs = x.reshape(t_len, D_MODEL)
    tgt = loss_target.reshape(t_len, D_MODEL)
    flat = lambda a: a.reshape(a.shape[-2:]) if a.ndim == 3 else a.reshape(1, -1)
    heads = lambda a: a.reshape(LRU_WIDTH, HEAD_DIM)

    win_blk, cpack, wout_shard, w1_shard, w2_shard = _all_gather_w_in(
        flat(w_in), flat(w_out), flat(w_mlp_in), flat(w_mlp_out), flat(conv_w), flat(rnn_conv_w))
    win_t = win_blk.reshape(IN_COLS, D_MODEL)
    conv_full = jnp.transpose(cpack[:, 0:3, 0:64], (1, 0, 2)).reshape(3, CONV_WIDTH)
    rnn_full = jnp.transpose(cpack[:, 3:7, :], (1, 0, 2)).reshape(4, LRU_WIDTH)
    mixer_small = (conv_full, rnn_full, flat(rnn_conv_b), heads(w_a), flat(b_a), heads(w_x), flat(b_x),
                   flat(lru_lambda), flat(g_norm_conv), flat(g_norm_rnn))

    u, h, wout_blk = _in_proj(xs, flat(norm_mix_g), win_t, wout_shard, tb)
    wout_f = wout_blk.reshape(MIX_WIDTH, D_MODEL)
    hs, y, w1_blk, w2_blk = _mixer_fwd(u, *mixer_small, w1_shard, w2_shard, tm)
    dx1, z, dpre, h2, dx2, vec_m = _mlp_fwd_bwd(xs, y, tgt, flat(norm_mlp_g), flat(final_norm_g), wout_f, w1_blk,
                                                w2_blk.reshape(D_FF, D_MODEL), tb)
    (g_w1,) = _tn_weight_grad(h2, dpre, tk, "w_mlp_in_grad", col_blocks=N_DEV)
    (g_w2,) = _tn_weight_grad(z, dx2, tk, "w_mlp_out_grad")
    g_w2 = g_w2.reshape(N_DEV, D_FF // N_DEV, D_MODEL)
    g_wout, sib_w1, sib_w2 = _tn_weight_grad(y, dx1, tk, "w_out_grad", pair=(g_w1, g_w2))
    g_wout = g_wout.reshape(N_DEV, MIX_WIDTH // N_DEV, D_MODEL)
    hsend_w1, own_w1 = _pair_sum(g_w1, sib_w1, "pair_sum_w_mlp_in")
    hsend_w2, own_w2 = _pair_sum(g_w2, sib_w2, "pair_sum_w_mlp_out")
    du, vec_b, wab, landed_w1, landed_w2, sib_wout = _mixer_bwd(
        u, hs, dx1, *mixer_small, wout_f, (hsend_w1, hsend_w2), g_wout, tm)
    hsend_wout, own_wout = _pair_sum(g_wout, sib_wout, "pair_sum_w_out")
    ax, ay, ac = lax.axis_index("x"), lax.axis_index("y"), lax.axis_index("c")
    chip_ids = jnp.stack([2 * cx + cy for cx, cy in [(ax, ay)] + _other_chips(ax, ay)]).astype(jnp.int32)
    core = jnp.reshape(ac, (1,)).astype(jnp.int32)
    tw = min(1024, t_len)
    g_others, landed_wout, vrecv_m, vrecv_b, wrecv = _w_in_grad_part(
        du, h, tw, "w_in_grad_others", chip_ids[1:4], chip=(hsend_wout,), small=(vec_m, vec_b, wab))
    g_own, sib_others = _w_in_grad_part(du, h, tw, "w_in_grad_own", chip_ids[0:1], halves=g_others)
    hsend_win = _pair_sum_parts(g_others, sib_others, core)
    grad_x, vec_x, landed_win, sib_own = _in_proj_bwd(du, dx1, xs, flat(norm_mix_g), win_t, tm, hsend_win, g_own)

    vsum, wsum = _final_small(vrecv_m, vrecv_b, wab, wrecv, vec_x)

    up_win = _update_w_in(g_own, sib_own, landed_win, flat(w_in), flat(m_w_in), flat(v_w_in), core, 256)
    up_wout = _update_sharded(own_wout, landed_wout, flat(w_out), flat(m_w_out), flat(v_w_out), 96, "update_w_out")
    up_w1 = _update_sharded(own_w1, landed_w1, flat(w_mlp_in), flat(m_w_mlp_in), flat(v_w_mlp_in), 256,
                            "update_w_mlp_in")
    up_w2 = _update_sharded(own_w2, landed_w2, flat(w_mlp_out), flat(m_w_mlp_out), flat(v_w_mlp_out), 256,
                            "update_w_mlp_out")

    g_cw = lax.dynamic_slice(vsum, (ROW_CW, 64 * my_id), (3, 64))
    g_rw = lax.dynamic_slice(vsum, (ROW_RW, 128 * my_id), (4, 128))
    small_w = (norm_mix_g, conv_w, rnn_conv_w, rnn_conv_b, w_a, b_a, w_x, b_x, lru_lambda, g_norm_conv, g_norm_rnn,
               norm_mlp_g, final_norm_g)
    small_m = (m_norm_mix_g, m_conv_w, m_rnn_conv_w, m_rnn_conv_b, m_w_a, m_b_a, m_w_x, m_b_x, m_lru_lambda,
               m_g_norm_conv, m_g_norm_rnn, m_norm_mlp_g, m_final_norm_g)
    small_v = (v_norm_mix_g, v_conv_w, v_rnn_conv_w, v_rnn_conv_b, v_w_a, v_b_a, v_w_x, v_b_x, v_lru_lambda,
               v_g_norm_conv, v_g_norm_rnn, v_norm_mlp_g, v_final_norm_g)
    is_heads = (False, False, False, False, True, False, True, False, False, False, False, False, False)
    as2d = lambda arrs: [heads(a) if hd else flat(a) for a, hd in zip(arrs, is_heads)]
    small_out = _update_small(vsum, wsum, g_cw, g_rw, as2d(small_w), as2d(small_m), as2d(small_v))
    loss = small_out[0].reshape(())

    names = ["norm_mix_g", "w_in", "conv_w", "rnn_conv_w", "rnn_conv_b", "w_a", "b_a", "w_x", "b_x", "lru_lambda",
             "g_norm_conv", "g_norm_rnn", "w_out", "norm_mlp_g", "w_mlp_in", "w_mlp_out", "final_norm_g"]
    originals = dict(zip(names, (norm_mix_g, w_in, conv_w, rnn_conv_w, rnn_conv_b, w_a, b_a, w_x, b_x, lru_lambda,
                                 g_norm_conv, g_norm_rnn, w_out, norm_mlp_g, w_mlp_in, w_mlp_out, final_norm_g)))
    results = {"w_in": up_win, "w_out": up_wout, "w_mlp_in": up_w1, "w_mlp_out": up_w2}
    small_names = ["norm_mix_g", "conv_w", "rnn_conv_w", "rnn_conv_b", "w_a", "b_a", "w_x", "b_x", "lru_lambda",
                   "g_norm_conv", "g_norm_rnn", "norm_mlp_g", "final_norm_g"]
    for k, nm in enumerate(small_names):
        results[nm] = small_out[1 + 4 * k:5 + 4 * k]
    out = [loss, grad_x.reshape(x.shape)]
    for kind in range(4):
        out += [results[nm][kind].reshape(originals[nm].shape) for nm in names]
    return tuple(out)
```

```python
import functools

import jax
import jax.numpy as jnp
from jax import lax
from jax.experimental import pallas as pl
from jax.experimental.pallas import tpu as pltpu

F32 = jnp.float32
BF16 = jnp.bfloat16

D_MODEL = 1024
HEAD_DIM = 64
CONV_WIDTH = 512
LRU_WIDTH = 1024
MIX_WIDTH = CONV_WIDTH + LRU_WIDTH
IN_COLS = 3 * CONV_WIDTH + 2 * LRU_WIDTH
D_FF = 4 * D_MODEL
GROUP = 256
EPS = 1e-6
LRU_C = 8.0
N_DEV = 8
SUB = 8

OFF_GB, OFF_GC, OFF_V, OFF_XR, OFF_G = 0, 512, 1024, 1536, 2560

ADAM_LR, ADAM_B1, ADAM_B2, ADAM_EPS, ADAM_WD, ADAM_STEP = 0.001, 0.9, 0.999, 1e-08, 0.01, 10
BC1 = 1.0 - ADAM_B1 ** ADAM_STEP
BC2 = 1.0 - ADAM_B2 ** ADAM_STEP

MIB = 1024 * 1024
MESH = pl.DeviceIdType.MESH

VEC_ROWS = 32
ROW_GF, ROW_GMLP, ROW_LOSS = 0, 1, 2
ROW_GNC, ROW_GNR, ROW_BR, ROW_BA, ROW_BX, ROW_LAM, ROW_CW, ROW_RW = 8, 9, 10, 11, 12, 13, 14, 17
ROW_GMIX = 24
ACC_GNC, ACC_GNR, ACC_BR, ACC_BA, ACC_BX, ACC_SP, ACC_CW, ACC_RW, N_ACC = 0, 1, 2, 3, 4, 5, 6, 9, 13


def _params(semantics=None, vmem_mib=48):
    return pltpu.CompilerParams(dimension_semantics=semantics, vmem_limit_bytes=vmem_mib * MIB)


def _rms(x):
    return lax.rsqrt(jnp.mean(x * x, axis=-1, keepdims=True) + EPS)


def _rms_bwd(dy, xhat, r, g):
    dyh = dy * g
    return r * (dyh - xhat * jnp.mean(dyh * xhat, axis=-1, keepdims=True))


def _sigmoid(x):
    return 0.5 + 0.5 * jnp.tanh(0.5 * x)


def _gelu(x):
    c0, c1 = 0.7978845608028654, 0.044715
    t = jnp.tanh(c0 * (x + c1 * x * x * x))
    ge = 0.5 * x * (1.0 + t)
    dge = 0.5 * (1.0 + t) + 0.5 * x * (1.0 - t * t) * c0 * (1.0 + 3.0 * c1 * x * x)
    return ge, dge


def _softplus_neg(lam):
    z = -lam
    e = jnp.exp(-jnp.abs(z))
    return jnp.maximum(z, 0.0) + jnp.where(e < 1e-4, e * (1.0 - 0.5 * e), jnp.log(1.0 + e))


def _lru_gates(pa, px, sp_c):
    ra = _sigmoid(pa)
    ii = _sigmoid(px)
    la = -ra * sp_c
    a = jnp.exp(la)
    x2 = 2.0 * la
    series = -x2 * (1.0 + x2 * (0.5 + x2 * (1.0 / 6.0 + x2 * (1.0 / 24.0))))
    m2 = jnp.where(x2 > -0.01, series, 1.0 - a * a)
    inv_mult = lax.rsqrt(m2)
    mult = jnp.where(m2 > 0.0, m2 * inv_mult, 0.0)
    return ra, ii, a, mult, inv_mult


def _down(cur, prev, s, row):
    return jnp.where(row >= s, pltpu.roll(cur, s, 0), pltpu.roll(prev, s, 0))


def _up(cur, nxt, s, row):
    return jnp.where(row < SUB - s, pltpu.roll(cur, SUB - s, 0), pltpu.roll(nxt, SUB - s, 0))


def _scan8_fwd(a, b, row):
    for s in (1, 2, 4):
        m = row >= s
        a_sh = pltpu.roll(a, s, 0)
        b_sh = pltpu.roll(b, s, 0)
        b = jnp.where(m, a * b_sh + b, b)
        a = jnp.where(m, a * a_sh, a)
    return a, b


def _scan8_rev(a, b, row):
    for s in (1, 2, 4):
        m = row < SUB - s
        a_sh = pltpu.roll(a, SUB - s, 0)
        b_sh = pltpu.roll(b, SUB - s, 0)
        b = jnp.where(m, a * b_sh + b, b)
        a = jnp.where(m, a * a_sh, a)
    return a, b


def _group_mask(shape):
    r = lax.broadcasted_iota(jnp.int32, shape, 0)
    c = lax.broadcasted_iota(jnp.int32, shape, 1)
    return ((r % GROUP) // HEAD_DIM) == (c // HEAD_DIM)


def _expand_heads(w):
    j = lax.broadcasted_iota(jnp.int32, (HEAD_DIM, GROUP), 0)
    c = lax.broadcasted_iota(jnp.int32, (HEAD_DIM, GROUP), 1)
    spread = (c % HEAD_DIM == j).astype(BF16)
    e = jnp.dot(w.astype(BF16), spread, preferred_element_type=F32)
    return jnp.where(_group_mask(e.shape), e, 0.0).astype(BF16)


def _fold_heads(p):
    p = jnp.where(_group_mask(p.shape), p, 0.0)
    c = lax.broadcasted_iota(jnp.int32, (GROUP, HEAD_DIM), 0)
    j = lax.broadcasted_iota(jnp.int32, (GROUP, HEAD_DIM), 1)
    fold = (c % HEAD_DIM == j).astype(BF16)
    hi = p.astype(BF16)
    rest = p - hi.astype(F32)
    mid = rest.astype(BF16)
    lo = (rest - mid.astype(F32)).astype(BF16)
    dot = functools.partial(jnp.dot, preferred_element_type=F32)
    return dot(hi, fold) + dot(mid, fold) + dot(lo, fold)


def _block_diag_apply(xb, wbd_ref):
    parts = [jnp.dot(xb[:, g * GROUP:(g + 1) * GROUP], wbd_ref[g * GROUP:(g + 1) * GROUP, :],
                     preferred_element_type=F32) for g in range(LRU_WIDTH // GROUP)]
    return jnp.concatenate(parts, axis=1)


def _block_diag_apply_t(db, wbd_ref):
    parts = [lax.dot_general(db[:, g * GROUP:(g + 1) * GROUP], wbd_ref[g * GROUP:(g + 1) * GROUP, :],
                             (((1,), (1,)), ((), ())), preferred_element_type=F32)
             for g in range(LRU_WIDTH // GROUP)]
    return jnp.concatenate(parts, axis=1)


def _dot_nt(a, b):
    return lax.dot_general(a, b, (((1,), (1,)), ((), ())), preferred_element_type=F32)


def _dot_tn(a, b):
    return lax.dot_general(a, b, (((0,), (0,)), ((), ())), preferred_element_type=F32)


CHUNKS_IN_FLIGHT = 4


def _chunk_loop(n_chunks, chunk, init):
    def body(k, carry):
        for j in range(CHUNKS_IN_FLIGHT):
            carry = chunk(k * CHUNKS_IN_FLIGHT + j, carry)
        return carry

    return lax.fori_loop(0, n_chunks // CHUNKS_IN_FLIGHT, body, init)


def _place():
    x, y, c = lax.axis_index("x"), lax.axis_index("y"), lax.axis_index("c")
    return x, y, c


def _block_id(chip, core):
    return 4 * chip[0] + 2 * chip[1] + core


def _other_chips(x, y):
    return [(1 - x, y), (x, 1 - y), (1 - x, 1 - y)]


def _remote_copy(src, dst, send_sem, recv_sem, to):
    return pltpu.make_async_remote_copy(src_ref=src, dst_ref=dst, send_sem=send_sem, recv_sem=recv_sem,
                                        device_id=to, device_id_type=MESH)


HBM_SPEC = pl.BlockSpec(memory_space=pl.ANY)


def _in_hbm(*arrays):
    return [pltpu.with_memory_space_constraint(a, pltpu.HBM) for a in arrays]


def _prep_shards(w_in, w_out, w_mlp_in, w_mlp_out, conv_w, rnn_conv_w):
    n_in = w_in.shape[1]

    def body(win_ref, wout_ref, w1_ref, w2_ref, cw_ref, rw_ref, o_win, o_wout, o_w1, o_w2, o_cp, padbuf):
        padbuf[...] = jnp.zeros(padbuf.shape, F32)
        padbuf[:, 0:n_in] = win_ref[...]
        o_win[...] = padbuf[...].T[0:n_in, :].astype(BF16)
        o_wout[...] = wout_ref[...].astype(BF16)
        o_w1[...] = w1_ref[...].astype(BF16)
        o_w2[...] = w2_ref[...].astype(BF16)
        o_cp[...] = jnp.zeros(o_cp.shape, F32)
        o_cp[0:3, 0:64] = cw_ref[...]
        o_cp[3:7, :] = rw_ref[...]

    whole = lambda shape: pl.BlockSpec(shape, lambda i: (0,) * len(shape))
    args = (w_in, w_out, w_mlp_in, w_mlp_out, conv_w, rnn_conv_w)
    shapes = [((n_in, D_MODEL), BF16), (w_out.shape, BF16), (w_mlp_in.shape, BF16), (w_mlp_out.shape, BF16),
              ((8, 128), F32)]
    return pl.pallas_call(
        body, grid=(1,), out_shape=[jax.ShapeDtypeStruct(s, d) for s, d in shapes],
        in_specs=[whole(a.shape) for a in args], out_specs=[whole(s) for s, _ in shapes],
        scratch_shapes=[pltpu.VMEM((D_MODEL, 512), F32)],
        compiler_params=_params(("arbitrary",), 40), name="prep_shards",
    )(*args)


def _host_all_gather(step, n_steps, shards, fulls, send_sems, recv_sems, local_sems):
    x, y, c = _place()
    me = (x, y, c)
    my_id = _block_id((x, y), c)
    sibling = (x, y, 1 - c)
    chips = _other_chips(x, y)
    n_arr = len(shards)

    def copy(arr, k, block, to, src=None):
        dst = fulls[arr].at[block]
        return _remote_copy(dst if src is None else src, dst, send_sems.at[arr, k], recv_sems.at[arr, k], to)

    def local(arr):
        return pltpu.make_async_copy(shards[arr], fulls[arr].at[my_id], local_sems.at[arr])

    @pl.when(step == 0)
    def _():
        for arr in range(n_arr):
            local(arr).start()
            copy(arr, 0, my_id, sibling, shards[arr]).start()
            for j, chip in enumerate(chips):
                copy(arr, 1 + j, my_id, (*chip, c), shards[arr]).start()

    @pl.when(step == max(n_steps - 2, 0))
    def _():
        for j, chip in enumerate(chips):
            for arr in range(n_arr):
                copy(arr, 1 + j, _block_id(chip, c), me).wait_recv()
                copy(arr, 4 + j, _block_id(chip, c), sibling).start()

    @pl.when(step == n_steps - 1)
    def _():
        for arr in range(n_arr):
            copy(arr, 0, _block_id((x, y), 1 - c), me).wait_recv()
            for j, chip in enumerate(chips):
                copy(arr, 4 + j, _block_id(chip, 1 - c), me).wait_recv()
            for k in range(4):
                copy(arr, k, my_id, me, shards[arr]).wait_send()
            for j, chip in enumerate(chips):
                copy(arr, 4 + j, _block_id(chip, c), me).wait_send()
            local(arr).wait()


def _host_pair_exchange(step, n_steps, gs, sibs, send_sems, recv_sems):
    x, y, c = _place()
    sibling = (x, y, 1 - c)
    chips = [(x, y)] + _other_chips(x, y)

    def d2d(arr, q):
        return _remote_copy(gs[arr].at[_block_id(chips[q], 1 - c)], sibs[arr].at[q],
                            send_sems.at[arr, q], recv_sems.at[arr, q], sibling)

    @pl.when(step == 0)
    def _():
        for arr in range(len(gs)):
            for q in (1, 2, 3, 0):
                d2d(arr, q).start()

    @pl.when(step == n_steps - 1)
    def _():
        for arr in range(len(gs)):
            for q in range(4):
                d2d(arr, q).wait()


def _host_chip_exchange(step, n_steps, hsends, hrecvs, send_sems, recv_sems):
    x, y, c = _place()
    chips = _other_chips(x, y)

    def ici(arr, j):
        return _remote_copy(hsends[arr].at[j], hrecvs[arr].at[j], send_sems.at[arr, j], recv_sems.at[arr, j],
                            (*chips[j], c))

    @pl.when(step == 0)
    def _():
        for arr in range(len(hsends)):
            for j in range(3):
                ici(arr, j).start()

    @pl.when(step == n_steps - 1)
    def _():
        for arr in range(len(hsends)):
            for j in range(3):
                ici(arr, j).wait()


def _host_half_exchange(step, n_steps, parts, sibs, send_sems, recv_sems):
    x, y, c = _place()
    n_q, rows2, _ = parts.shape
    half = rows2 // 2

    def d2d(q):
        src = parts.at[q, pl.ds(pl.multiple_of((1 - c) * half, 16), half), :]
        return _remote_copy(src, sibs.at[q], send_sems.at[q], recv_sems.at[q], (x, y, 1 - c))

    @pl.when(step == 0)
    def _():
        for q in range(n_q):
            d2d(q).start()

    @pl.when(step == n_steps - 1)
    def _():
        for q in range(n_q):
            d2d(q).wait()


def _peer(x, y, c, k):
    return (x ^ ((k >> 2) & 1), y ^ ((k >> 1) & 1), c ^ (k & 1))


def _host_small_exchange(step, n_steps, vec_m, vec_b, wab, vrecv_m, vrecv_b, wrecv, send_sems, recv_sems, local_sems):
    x, y, c = _place()
    my_id = _block_id((x, y), c)
    wrows = wab.shape[0] // N_DEV

    def copies(k):
        to = _peer(x, y, c, k)
        block = wab.at[pl.ds(pl.multiple_of(_block_id(to[0:2], to[2]) * wrows, SUB), wrows), :]
        return [_remote_copy(vec_m, vrecv_m.at[my_id], send_sems.at[0, k], recv_sems.at[0, k], to),
                _remote_copy(vec_b, vrecv_b.at[my_id], send_sems.at[1, k], recv_sems.at[1, k], to),
                _remote_copy(block, wrecv.at[k], send_sems.at[2, k], recv_sems.at[2, k], to)]

    mine = [pltpu.make_async_copy(vec_m, vrecv_m.at[my_id], local_sems.at[0]),
            pltpu.make_async_copy(vec_b, vrecv_b.at[my_id], local_sems.at[1])]

    @pl.when(step == 0)
    def _():
        for cp in mine:
            cp.start()
        for k in range(1, N_DEV):
            for cp in copies(k):
                cp.start()

    @pl.when(step == n_steps - 1)
    def _():
        for k in range(1, N_DEV):
            for cp in copies(k):
                cp.wait()
        for cp in mine:
            cp.wait()


def _pair_sum_parts(parts, sibs, core):
    n_q, rows2, cols = parts.shape
    half = rows2 // 2

    def body(core_ref, g_ref, s_ref, o_ref):
        o_ref[0] = (g_ref[0, 0].astype(F32) + s_ref[0].astype(F32)).astype(BF16)

    block = (1, half, cols)
    grid_spec = pltpu.PrefetchScalarGridSpec(
        num_scalar_prefetch=1, grid=(n_q,),
        in_specs=[pl.BlockSpec((1, 1, half, cols), lambda q, cr: (q, cr[0], 0, 0)),
                  pl.BlockSpec(block, lambda q, cr: (q, 0, 0))],
        out_specs=pl.BlockSpec(block, lambda q, cr: (q, 0, 0)))
    return pl.pallas_call(
        body, grid_spec=grid_spec, out_shape=pltpu.HBM((n_q, half, cols), BF16),
        compiler_params=_params(("arbitrary",), 32), name="pair_sum_w_in",
    )(core, *_in_hbm(parts.reshape(n_q, 2, half, cols), sibs))


def _pair_sum(g, sib, name):
    _, rows, cols = g.shape
    x, y, c = _place()
    slots = jnp.stack([_block_id(chip, c) for chip in [(x, y)] + _other_chips(x, y)]).astype(jnp.int32)

    def body(slots_ref, g_ref, sib_ref, hs_ref, own_ref):
        q = pl.program_id(0)
        both = g_ref[0].astype(F32) + sib_ref[0].astype(F32)

        @pl.when(q == 0)
        def _():
            own_ref[...] = both

        @pl.when(q > 0)
        def _():
            hs_ref[0] = both.astype(BF16)

    block = (1, rows, cols)
    grid_spec = pltpu.PrefetchScalarGridSpec(
        num_scalar_prefetch=1, grid=(4,),
        in_specs=[pl.BlockSpec(block, lambda q, s: (s[q], 0, 0)), pl.BlockSpec(block, lambda q, s: (q, 0, 0))],
        out_specs=[pl.BlockSpec(block, lambda q, s: (jnp.maximum(q - 1, 0), 0, 0)),
                   pl.BlockSpec((rows, cols), lambda q, s: (0, 0))])
    return pl.pallas_call(
        body, grid_spec=grid_spec,
        out_shape=(pltpu.HBM((3, rows, cols), BF16), pltpu.HBM((rows, cols), F32)),
        compiler_params=_params(("arbitrary",), 32), name=name,
    )(slots, *_in_hbm(g, sib))


def _exchange_scratch(n_arr, n_copies):
    return [pltpu.SemaphoreType.DMA((n_arr, n_copies)), pltpu.SemaphoreType.DMA((n_arr, n_copies))]


def _final_small(vrecv_m, vrecv_b, wab, wrecv, vec_x):
    wrows = wab.shape[0] // N_DEV

    def body(vm_ref, vb_ref, w_ref, wr_ref, vx_ref, o_vec, o_w, xrecv, wred, x_send, x_recv, b_send, b_recv):
        x, y, c = _place()
        my_id = _block_id((x, y), c)
        my_rows = pl.ds(pl.multiple_of(my_id * wrows, SUB), wrows)

        def xcopy(k):
            return _remote_copy(vx_ref, xrecv.at[my_id], x_send.at[k], x_recv.at[k], _peer(x, y, c, k))

        def bcopy(k):
            return _remote_copy(wred, o_w.at[my_rows, :], b_send.at[k], b_recv.at[k], _peer(x, y, c, k))

        xrecv[my_id] = vx_ref[...]
        for k in range(1, N_DEV):
            xcopy(k).start()
        red = w_ref[my_rows, :]
        for k in range(1, N_DEV):
            red = red + wr_ref[k]
        wred[...] = red
        o_w[my_rows, :] = red
        for k in range(1, N_DEV):
            bcopy(k).start()
        for k in range(1, N_DEV):
            xcopy(k).wait_recv()
        for rows, ref in ((slice(0, 8), vm_ref), (slice(8, 24), vb_ref), (slice(24, 32), xrecv)):
            tot = ref[0]
            for s in range(1, N_DEV):
                tot = tot + ref[s]
            o_vec[rows, :] = tot
        for k in range(1, N_DEV):
            bcopy(k).wait_recv()
        for k in range(1, N_DEV):
            xcopy(k).wait_send()
            bcopy(k).wait_send()

    vm = pl.BlockSpec(memory_space=pltpu.VMEM)
    dma8 = pltpu.SemaphoreType.DMA((N_DEV,))
    return pl.pallas_call(
        body, out_shape=(jax.ShapeDtypeStruct((VEC_ROWS, D_MODEL), F32), jax.ShapeDtypeStruct(wab.shape, F32)),
        in_specs=[vm] * 5, out_specs=[vm] * 2,
        scratch_shapes=[pltpu.VMEM((N_DEV, SUB, D_MODEL), F32), pltpu.VMEM((wrows, HEAD_DIM), F32),
                        dma8, dma8, dma8, dma8],
        compiler_params=_params(vmem_mib=32), name="final_small",
    )(vrecv_m, vrecv_b, wab, wrecv, vec_x)


def _in_proj(x, g_mix, shards, tm):
    t_len = x.shape[0]
    n_t = t_len // tm
    n_arr = len(shards)
    rows = [s.shape[0] for s in shards]
    width = 2 * rows[0]
    ax, ay = lax.axis_index("x"), lax.axis_index("y")
    order = jnp.stack([2 * cx + cy for cx, cy in [(ax, ay)] + _other_chips(ax, ay)]).astype(jnp.int32)

    def body(order_ref, x_ref, g_ref, *rest):
        shard_refs = rest[0:n_arr]
        u_ref, h_ref = rest[n_arr:n_arr + 2]
        fulls = rest[n_arr + 2:2 * n_arr + 2]
        h_s, wbuf, send_sems, recv_sems, local_sems, load_sem = rest[2 * n_arr + 2:]
        p = pl.program_id(0)
        i = pl.program_id(1)
        x_, y_, c = _place()
        me = (x_, y_, c)
        my_id = _block_id((x_, y_), c)
        sibling = (x_, y_, 1 - c)
        chips = _other_chips(x_, y_)

        def block(arr, blk):
            return fulls[arr].at[pl.ds(pl.multiple_of(blk * rows[arr], rows[arr]), rows[arr]), :]

        def copy(arr, k, blk, to, src=None):
            dst = block(arr, blk)
            return _remote_copy(dst if src is None else src, dst, send_sems.at[arr, k], recv_sems.at[arr, k], to)

        def local(arr):
            return pltpu.make_async_copy(shard_refs[arr], block(arr, my_id), local_sems.at[arr])

        def load_chip(chip):
            start = pl.multiple_of((2 * chip[0] + chip[1]) * width, width)
            cp = pltpu.make_async_copy(fulls[0].at[pl.ds(start, width), :], wbuf, load_sem.at[0])
            cp.start()
            cp.wait()

        @pl.when((p == 0) & (i == 0))
        def _():
            for arr in range(n_arr):
                local(arr).start()
                copy(arr, 0, my_id, sibling, shard_refs[arr]).start()
                for j, chip in enumerate(chips):
                    copy(arr, 1 + j, my_id, (*chip, c), shard_refs[arr]).start()
            for arr in range(n_arr):
                local(arr).wait()
                copy(arr, 0, _block_id((x_, y_), 1 - c), me).wait_recv()
            load_chip((x_, y_))

        for j, chip in enumerate(chips):
            @pl.when((p == j + 1) & (i == 0))
            def _(j=j, chip=chip):
                for arr in range(n_arr):
                    copy(arr, 1 + j, _block_id(chip, c), me).wait_recv()
                    copy(arr, 4 + j, _block_id(chip, c), sibling).start()
                for arr in range(n_arr):
                    copy(arr, 4 + j, _block_id(chip, 1 - c), me).wait_recv()
                load_chip(chip)

        @pl.when((p == 3) & (i == n_t - 1))
        def _():
            for arr in range(n_arr):
                for k in range(4):
                    copy(arr, k, my_id, me, shard_refs[arr]).wait_send()
                for j, chip in enumerate(chips):
                    copy(arr, 4 + j, _block_id(chip, c), me).wait_send()

        tile = pl.ds(pl.multiple_of(i * tm, tm), tm)

        @pl.when(p == 0)
        def _():
            xv = x_ref[...]
            h = (xv * _rms(xv) * g_ref[...]).astype(BF16)
            h_ref[...] = h
            h_s[tile, :] = h

        u_ref[...] = _dot_nt(h_s[tile, :], wbuf[...])

    first_pass = lambda p, i, o: (jnp.where(p == 0, i, n_t - 1), 0)
    grid_spec = pltpu.PrefetchScalarGridSpec(
        num_scalar_prefetch=1, grid=(4, n_t),
        in_specs=[pl.BlockSpec((tm, D_MODEL), first_pass), pl.BlockSpec((1, D_MODEL), lambda p, i, o: (0, 0))]
        + [HBM_SPEC] * n_arr,
        out_specs=[pl.BlockSpec((tm, width), lambda p, i, o: (i, o[p])), pl.BlockSpec((tm, D_MODEL), first_pass)]
        + [HBM_SPEC] * n_arr,
        scratch_shapes=[pltpu.VMEM((t_len, D_MODEL), BF16), pltpu.VMEM((width, D_MODEL), BF16)]
        + _exchange_scratch(n_arr, 7) + [pltpu.SemaphoreType.DMA((n_arr,)), pltpu.SemaphoreType.DMA((1,))])
    return pl.pallas_call(
        body, grid_spec=grid_spec,
        out_shape=[jax.ShapeDtypeStruct((t_len, IN_COLS), F32), jax.ShapeDtypeStruct((t_len, D_MODEL), BF16)]
        + [jax.ShapeDtypeStruct((N_DEV * s.shape[0], s.shape[1]), s.dtype) for s in shards],
        compiler_params=_params(("arbitrary", "arbitrary"), 48), name="in_proj",
    )(order, x, g_mix, *shards)


def _conv3_chunk(u_ref, r, cv_prev, cw, row):
    gb = u_ref[pl.ds(r, SUB), OFF_GB:OFF_GB + CONV_WIDTH]
    gc = u_ref[pl.ds(r, SUB), OFF_GC:OFF_GC + CONV_WIDTH]
    v = u_ref[pl.ds(r, SUB), OFF_V:OFF_V + CONV_WIDTH]
    cv = gc * v
    cv_m1 = _down(cv, cv_prev, 1, row)
    cv_m2 = _down(cv, cv_prev, 2, row)
    cq = cw[2:3, :] * cv + cw[1:2, :] * cv_m1 + cw[0:1, :] * cv_m2
    return gb, gc, v, cv, cv_m1, cv_m2, cq


def _conv4_chunk(u_ref, r, xin_prev, rw, rb, row):
    xin = u_ref[pl.ds(r, SUB), OFF_XR:OFF_XR + LRU_WIDTH]
    m1 = _down(xin, xin_prev, 1, row)
    m2 = _down(xin, xin_prev, 2, row)
    m3 = _down(xin, xin_prev, 3, row)
    xr = rw[3:4, :] * xin + rw[2:3, :] * m1 + rw[1:2, :] * m2 + rw[0:1, :] * m3 + rb
    return xin, m1, m2, m3, xr


def _mixer_fwd(u, conv_w, rnn_conv_w, rnn_conv_b, wa, b_a, wx, b_x, lam, gnc, gnr, w1_shard, w2_shard, tm):
    t_len = u.shape[0]
    n_steps = t_len // tm
    n_chunks = tm // SUB

    def body(u_ref, cw_ref, rw_ref, rb_ref, wa_ref, ba_ref, wx_ref, bx_ref, lam_ref, gnc_ref, gnr_ref,
             w1_shard, w2_shard, hs_ref, y_ref, w1_full, w2_full,
             y_s, xr_s, pa_s, px_s, wabd, wxbd, cv_car, xin_car, h_car, send_sems, recv_sems, local_sems):
        _host_all_gather(pl.program_id(0), n_steps, [w1_shard, w2_shard], [w1_full, w2_full],
                         send_sems, recv_sems, local_sems)

        @pl.when(pl.program_id(0) == 0)
        def _():
            cv_car[...] = jnp.zeros(cv_car.shape, F32)
            xin_car[...] = jnp.zeros(xin_car.shape, F32)
            h_car[...] = jnp.zeros(h_car.shape, F32)
            wabd[...] = _expand_heads(wa_ref[...])
            wxbd[...] = _expand_heads(wx_ref[...])

        row_c = lax.broadcasted_iota(jnp.int32, (SUB, CONV_WIDTH), 0)
        row_r = lax.broadcasted_iota(jnp.int32, (SUB, LRU_WIDTH), 0)
        cw = cw_ref[...]
        rw = rw_ref[...]
        rb = rb_ref[...]
        g_c = gnc_ref[...]
        g_r = gnr_ref[...]
        sp_c = LRU_C * _softplus_neg(lam_ref[...])

        def convs(i, carry):
            cv_prev, xin_prev = carry
            r = pl.multiple_of(i * SUB, SUB)
            gb, _, _, cv, _, _, cq = _conv3_chunk(u_ref, r, cv_prev, cw, row_c)
            y_c = gb * cq
            y_s[pl.ds(r, SUB), 0:CONV_WIDTH] = y_c * _rms(y_c) * g_c
            xin, _, _, _, xr = _conv4_chunk(u_ref, r, xin_prev, rw, rb, row_r)
            xr_s[pl.ds(r, SUB), :] = xr
            return cv, xin

        cv_last, xin_last = _chunk_loop(n_chunks, convs, (cv_car[...], xin_car[...]))
        cv_car[...] = cv_last
        xin_car[...] = xin_last

        xrb = xr_s[...].astype(BF16)
        pa_s[...] = _block_diag_apply(xrb, wabd) + ba_ref[...]
        px_s[...] = _block_diag_apply(xrb, wxbd) + bx_ref[...]

        def recur(i, h_prev):
            r = pl.multiple_of(i * SUB, SUB)
            xr = xr_s[pl.ds(r, SUB), :]
            _, ii, a, mult, _ = _lru_gates(pa_s[pl.ds(r, SUB), :], px_s[pl.ds(r, SUB), :], sp_c)
            a_cum, b_cum = _scan8_fwd(a, mult * ii * xr, row_r)
            h = a_cum * h_prev + b_cum
            hs_ref[pl.ds(r, SUB), :] = h
            ge, _ = _gelu(u_ref[pl.ds(r, SUB), OFF_G:OFF_G + LRU_WIDTH])
            y_r = h * ge
            y_s[pl.ds(r, SUB), CONV_WIDTH:MIX_WIDTH] = y_r * _rms(y_r) * g_r
            return h[SUB - 1:SUB, :]

        h_car[...] = _chunk_loop(n_chunks, recur, h_car[...])

        y_ref[...] = y_s[...].astype(BF16)

    row_tile = lambda w: pl.BlockSpec((tm, w), lambda i: (i, 0))
    whole = lambda a: pl.BlockSpec(a.shape, lambda i: (0,) * a.ndim)
    smalls = (conv_w, rnn_conv_w, rnn_conv_b, wa, b_a, wx, b_x, lam, gnc, gnr)
    return pl.pallas_call(
        body, grid=(n_steps,),
        in_specs=[row_tile(IN_COLS)] + [whole(a) for a in smalls] + [HBM_SPEC, HBM_SPEC],
        out_specs=[row_tile(LRU_WIDTH), row_tile(MIX_WIDTH), HBM_SPEC, HBM_SPEC],
        out_shape=[jax.ShapeDtypeStruct((t_len, LRU_WIDTH), F32), jax.ShapeDtypeStruct((t_len, MIX_WIDTH), BF16),
                   jax.ShapeDtypeStruct((N_DEV,) + w1_shard.shape, BF16),
                   jax.ShapeDtypeStruct((N_DEV,) + w2_shard.shape, BF16)],
        scratch_shapes=[pltpu.VMEM((tm, MIX_WIDTH), F32), pltpu.VMEM((tm, LRU_WIDTH), F32),
                        pltpu.VMEM((tm, LRU_WIDTH), F32), pltpu.VMEM((tm, LRU_WIDTH), F32),
                        pltpu.VMEM((LRU_WIDTH, GROUP), BF16), pltpu.VMEM((LRU_WIDTH, GROUP), BF16),
                        pltpu.VMEM((SUB, CONV_WIDTH), F32), pltpu.VMEM((SUB, LRU_WIDTH), F32),
                        pltpu.VMEM((1, LRU_WIDTH), F32)] + _exchange_scratch(2, 7) + [pltpu.SemaphoreType.DMA((2,))],
        compiler_params=_params(("arbitrary",), 56), name="mixer_fwd",
    )(u, *smalls, w1_shard, w2_shard)


def _mlp_fwd_bwd(x, y, target, g_mlp, g_f, w_out, w1, w2, tm):
    t_len = x.shape[0]
    n_steps = t_len // tm
    n_blk, _, blk = w1.shape

    def body(x_ref, y_ref, tg_ref, gm_ref, gf_ref, wout_hbm, w1_hbm, w2_hbm,
             dx1_ref, h2_ref, dx2_ref, vec_ref, z_hbm, dpre_hbm,
             wout_s, w1_s, w2_s, rp_s, z_s, dp_s, sem, out_sem):
        step = pl.program_id(0)
        rows = pl.ds(pl.multiple_of(step * tm, tm), tm)
        z_out = pltpu.make_async_copy(z_s, z_hbm.at[rows, :], out_sem.at[0])
        dp_out = pltpu.make_async_copy(dp_s, dpre_hbm.at[rows, :], out_sem.at[1])

        @pl.when(step == 0)
        def _():
            loads = [pltpu.make_async_copy(src, dst, sem.at[k])
                     for k, (src, dst) in enumerate(((wout_hbm, wout_s), (w1_hbm, w1_s), (w2_hbm, w2_s)))]
            for cp in loads:
                cp.start()
            vec_ref[...] = jnp.zeros(vec_ref.shape, F32)
            for cp in loads:
                cp.wait()

        x1v = x_ref[...] + jnp.dot(y_ref[...], wout_s[...], preferred_element_type=F32)
        g_m = gm_ref[...]
        g_o = gf_ref[...]
        r2 = _rms(x1v)
        x1h = x1v * r2
        h2 = (x1h * g_m).astype(BF16)
        h2_ref[...] = h2
        x2 = x1v

        @pl.when(step > 0)
        def _():
            z_out.wait()

        for k in range(n_blk):
            rp = jnp.maximum(jnp.dot(h2, w1_s[k], preferred_element_type=F32), 0.0)
            rp_s[:, k * blk:(k + 1) * blk] = rp.astype(BF16)
            zb = (rp * rp).astype(BF16)
            z_s[:, k * blk:(k + 1) * blk] = zb
            x2 = x2 + jnp.dot(zb, w2_s[k * blk:(k + 1) * blk, :], preferred_element_type=F32)
        z_out.start()
        r3 = _rms(x2)
        x2h = x2 * r3
        err = x2h * g_o - tg_ref[...]
        dout = err * (1.0 / D_MODEL)
        vec_ref[ROW_LOSS:ROW_LOSS + 1, :] += (0.5 / D_MODEL) * jnp.sum(err * err, axis=0, keepdims=True)
        vec_ref[ROW_GF:ROW_GF + 1, :] += jnp.sum(dout * x2h, axis=0, keepdims=True)
        dx2 = _rms_bwd(dout, x2h, r3, g_o)
        dx2b = dx2.astype(BF16)
        dx2_ref[...] = dx2b
        dh2 = jnp.zeros((tm, D_MODEL), F32)

        @pl.when(step > 0)
        def _():
            dp_out.wait()

        for k in range(n_blk):
            dz = _dot_nt(dx2b, w2_s[k * blk:(k + 1) * blk, :])
            dpb = (dz * 2.0 * rp_s[:, k * blk:(k + 1) * blk].astype(F32)).astype(BF16)
            dp_s[:, k * blk:(k + 1) * blk] = dpb
            dh2 = dh2 + _dot_nt(dpb, w1_s[k])
        dp_out.start()
        vec_ref[ROW_GMLP:ROW_GMLP + 1, :] += jnp.sum(dh2 * x1h, axis=0, keepdims=True)
        dx1_ref[...] = dx2 + _rms_bwd(dh2, x1h, r2, g_m)

        @pl.when(step == n_steps - 1)
        def _():
            z_out.wait()
            dp_out.wait()

    row_tile = lambda w: pl.BlockSpec((tm, w), lambda i: (i, 0))
    vec_spec = pl.BlockSpec((1, D_MODEL), lambda i: (0, 0))
    outs = pl.pallas_call(
        body, grid=(n_steps,),
        in_specs=[row_tile(D_MODEL), row_tile(MIX_WIDTH), row_tile(D_MODEL), vec_spec, vec_spec,
                  HBM_SPEC, HBM_SPEC, HBM_SPEC],
        out_specs=[row_tile(D_MODEL), row_tile(D_MODEL), row_tile(D_MODEL),
                   pl.BlockSpec((SUB, D_MODEL), lambda i: (0, 0)), HBM_SPEC, HBM_SPEC],
        out_shape=[jax.ShapeDtypeStruct((t_len, D_MODEL), F32), jax.ShapeDtypeStruct((t_len, D_MODEL), BF16),
                   jax.ShapeDtypeStruct((t_len, D_MODEL), BF16), jax.ShapeDtypeStruct((SUB, D_MODEL), F32),
                   jax.ShapeDtypeStruct((t_len, D_FF), BF16), jax.ShapeDtypeStruct((t_len, D_FF), BF16)],
        scratch_shapes=[pltpu.VMEM(w_out.shape, BF16), pltpu.VMEM(w1.shape, BF16), pltpu.VMEM(w2.shape, BF16),
                        pltpu.VMEM((tm, D_FF), BF16), pltpu.VMEM((tm, D_FF), BF16), pltpu.VMEM((tm, D_FF), BF16),
                        pltpu.SemaphoreType.DMA((3,)), pltpu.SemaphoreType.DMA((2,))],
        compiler_params=_params(("arbitrary",), 58), name="mlp_fwd_bwd",
    )(x, y, target, g_mlp, g_f, w_out, w1, w2)
    dx1, h2, dx2, vec, z, dpre = outs
    return dx1, z, dpre, h2, dx2, vec


def _mixer_bwd(u, hs, dx1, conv_w, rnn_conv_w, rnn_conv_b, wa, b_a, wx, b_x, lam, gnc, gnr, w_out,
               chip_sums, g_wout, tm):
    t_len = u.shape[0]
    n_tiles = t_len // tm
    n_chunks = tm // SUB
    per_tile = tm // SUB
    n_sums = len(chip_sums)

    def body(u_ref, up_ref, hs_ref, hp_ref, dx1_ref, cw_ref, rw_ref, rb_ref, wa_ref, ba_ref, wx_ref, bx_ref,
             lam_ref, gnc_ref, gnr_ref, wout_ref, *rest):
        hsends = rest[0:n_sums]
        gwout_ref = rest[n_sums]
        du_ref, vec_ref, wab_ref = rest[n_sums + 1:n_sums + 4]
        hrecvs = rest[n_sums + 4:2 * n_sums + 4]
        sib_wout = rest[2 * n_sums + 4]
        (du_s, dy_s, xr_s, pa_s, px_s, dpa_s, dpx_s, dxr_s, wabd, wxbd, acc, dwa_acc, dwx_acc,
         a_car, dh_car, dcq_car, dxr_car, i_send, i_recv, d_send, d_recv) = rest[2 * n_sums + 5:]
        step = pl.program_id(0)
        _host_chip_exchange(step, n_tiles, hsends, hrecvs, i_send, i_recv)
        _host_pair_exchange(step, n_tiles, [gwout_ref], [sib_wout], d_send, d_recv)
        has_prev = (step < n_tiles - 1).astype(F32)

        @pl.when(step == 0)
        def _():
            acc[...] = jnp.zeros(acc.shape, F32)
            dwa_acc[...] = jnp.zeros(dwa_acc.shape, F32)
            dwx_acc[...] = jnp.zeros(dwx_acc.shape, F32)
            a_car[...] = jnp.ones(a_car.shape, F32)
            dh_car[...] = jnp.zeros(dh_car.shape, F32)
            dcq_car[...] = jnp.zeros(dcq_car.shape, F32)
            dxr_car[...] = jnp.zeros(dxr_car.shape, F32)
            wabd[...] = _expand_heads(wa_ref[...])
            wxbd[...] = _expand_heads(wx_ref[...])

        row_c = lax.broadcasted_iota(jnp.int32, (SUB, CONV_WIDTH), 0)
        row_r = lax.broadcasted_iota(jnp.int32, (SUB, LRU_WIDTH), 0)
        cw = cw_ref[...]
        rw = rw_ref[...]
        rb = rb_ref[...]
        g_c = gnc_ref[...]
        g_r = gnr_ref[...]
        sp_c = LRU_C * _softplus_neg(lam_ref[...])

        up = up_ref[...] * has_prev
        cv_before = up[:, OFF_GC:OFF_GC + CONV_WIDTH] * up[:, OFF_V:OFF_V + CONV_WIDTH]
        xin_before = up[:, OFF_XR:OFF_XR + LRU_WIDTH]
        hs_before = hp_ref[...] * has_prev

        dy_s[...] = _dot_nt(dx1_ref[...].astype(BF16), wout_ref[...])

        def conv4_fwd(i, xin_prev):
            r = pl.multiple_of(i * SUB, SUB)
            xin, _, _, _, xr = _conv4_chunk(u_ref, r, xin_prev, rw, rb, row_r)
            xr_s[pl.ds(r, SUB), :] = xr
            return xin

        _chunk_loop(n_chunks, conv4_fwd, xin_before)
        xrb = xr_s[...].astype(BF16)
        pa_s[...] = _block_diag_apply(xrb, wabd) + ba_ref[...]
        px_s[...] = _block_diag_apply(xrb, wxbd) + bx_ref[...]

        def recur_bwd(j, carry):
            a_later, dh_later = carry
            i = n_chunks - 1 - j
            r = pl.multiple_of(i * SUB, SUB)
            rp = pl.multiple_of(jnp.maximum(i - 1, 0) * SUB, SUB)
            xr = xr_s[pl.ds(r, SUB), :]
            hs_c = hs_ref[pl.ds(r, SUB), :]
            hs_prev = jnp.where(i == 0, hs_before, hs_ref[pl.ds(rp, SUB), :])
            h_m1 = _down(hs_c, hs_prev, 1, row_r)
            ra, ii, a, mult, inv_mult = _lru_gates(pa_s[pl.ds(r, SUB), :], px_s[pl.ds(r, SUB), :], sp_c)
            ge, dge = _gelu(u_ref[pl.ds(r, SUB), OFF_G:OFF_G + LRU_WIDTH])
            y_r = hs_c * ge
            rr = _rms(y_r)
            yhat = y_r * rr
            dyn = dy_s[pl.ds(r, SUB), CONV_WIDTH:MIX_WIDTH]
            acc[ACC_GNR] += dyn * yhat
            dy_r = _rms_bwd(dyn, yhat, rr, g_r)
            du_s[pl.ds(r, SUB), OFF_G:OFF_G + LRU_WIDTH] = dy_r * hs_c * dge
            a_cum, d_cum = _scan8_rev(_up(a, a_later, 1, row_r), dy_r * ge, row_r)
            dh = a_cum * dh_later + d_cum
            dmult = dh * ii * xr
            dii = dh * mult * xr
            dxr_s[pl.ds(r, SUB), :] = dh * mult * ii
            dla = dh * h_m1 * a - dmult * a * a * inv_mult
            acc[ACC_SP] += -dla * ra
            dpa = -dla * sp_c * ra * (1.0 - ra)
            dpx = dii * ii * (1.0 - ii)
            acc[ACC_BA] += dpa
            acc[ACC_BX] += dpx
            dpa_s[pl.ds(r, SUB), :] = dpa
            dpx_s[pl.ds(r, SUB), :] = dpx
            return a, dh[0:1, :]

        a_first, dh_first = _chunk_loop(n_chunks, recur_bwd, (a_car[...], dh_car[...]))
        a_car[...] = a_first
        dh_car[...] = dh_first

        dpab = dpa_s[...].astype(BF16)
        dpxb = dpx_s[...].astype(BF16)
        dxr_s[...] += _block_diag_apply_t(dpab, wabd) + _block_diag_apply_t(dpxb, wxbd)
        for g in range(LRU_WIDTH // GROUP):
            cols = slice(g * GROUP, (g + 1) * GROUP)
            dwa_acc[cols, :] += _dot_tn(xrb[:, cols], dpab[:, cols])
            dwx_acc[cols, :] += _dot_tn(xrb[:, cols], dpxb[:, cols])

        def convs_bwd(j, carry):
            dcq_later, dxr_later = carry
            i = n_chunks - 1 - j
            r = pl.multiple_of(i * SUB, SUB)
            rp = pl.multiple_of(jnp.maximum(i - 1, 0) * SUB, SUB)
            cv_prev = jnp.where(i == 0, cv_before,
                                u_ref[pl.ds(rp, SUB), OFF_GC:OFF_GC + CONV_WIDTH]
                                * u_ref[pl.ds(rp, SUB), OFF_V:OFF_V + CONV_WIDTH])
            gb, gc, v, cv, cv_m1, cv_m2, cq = _conv3_chunk(u_ref, r, cv_prev, cw, row_c)
            y_c = gb * cq
            rc = _rms(y_c)
            yhat = y_c * rc
            dyn = dy_s[pl.ds(r, SUB), 0:CONV_WIDTH]
            acc[ACC_GNC, :, 0:CONV_WIDTH] += dyn * yhat
            dy_c = _rms_bwd(dyn, yhat, rc, g_c)
            dcq = dy_c * gb
            dcv = (cw[2:3, :] * dcq + cw[1:2, :] * _up(dcq, dcq_later, 1, row_c)
                   + cw[0:1, :] * _up(dcq, dcq_later, 2, row_c))
            acc[ACC_CW + 2, :, 0:CONV_WIDTH] += dcq * cv
            acc[ACC_CW + 1, :, 0:CONV_WIDTH] += dcq * cv_m1
            acc[ACC_CW + 0, :, 0:CONV_WIDTH] += dcq * cv_m2
            du_s[pl.ds(r, SUB), OFF_GB:OFF_GB + CONV_WIDTH] = dy_c * cq
            du_s[pl.ds(r, SUB), OFF_GC:OFF_GC + CONV_WIDTH] = dcv * v
            du_s[pl.ds(r, SUB), OFF_V:OFF_V + CONV_WIDTH] = dcv * gc

            xin_prev = jnp.where(i == 0, xin_before, u_ref[pl.ds(rp, SUB), OFF_XR:OFF_XR + LRU_WIDTH])
            xin, m1, m2, m3, _ = _conv4_chunk(u_ref, r, xin_prev, rw, rb, row_r)
            dxr = dxr_s[pl.ds(r, SUB), :]
            du_s[pl.ds(r, SUB), OFF_XR:OFF_XR + LRU_WIDTH] = (
                rw[3:4, :] * dxr + rw[2:3, :] * _up(dxr, dxr_later, 1, row_r)
                + rw[1:2, :] * _up(dxr, dxr_later, 2, row_r) + rw[0:1, :] * _up(dxr, dxr_later, 3, row_r))
            acc[ACC_RW + 3] += dxr * xin
            acc[ACC_RW + 2] += dxr * m1
            acc[ACC_RW + 1] += dxr * m2
            acc[ACC_RW + 0] += dxr * m3
            acc[ACC_BR] += dxr
            return dcq, dxr

        dcq_first, dxr_first = _chunk_loop(n_chunks, convs_bwd, (dcq_car[...], dxr_car[...]))
        dcq_car[...] = dcq_first
        dxr_car[...] = dxr_first

        du_ref[...] = du_s[...].astype(BF16)

        @pl.when(step == n_tiles - 1)
        def _():
            vec_ref[...] = jnp.zeros(vec_ref.shape, F32)
            rows = {ACC_GNC: ROW_GNC, ACC_GNR: ROW_GNR, ACC_BR: ROW_BR, ACC_BA: ROW_BA, ACC_BX: ROW_BX}
            for k in range(3):
                rows[ACC_CW + k] = ROW_CW + k
            for k in range(4):
                rows[ACC_RW + k] = ROW_RW + k
            for slot, out_row in rows.items():
                o = out_row - ROW_GNC
                vec_ref[o:o + 1, :] = jnp.sum(acc[slot], axis=0, keepdims=True)
            lam_v = lam_ref[...]
            dsp = jnp.sum(acc[ACC_SP], axis=0, keepdims=True)
            o = ROW_LAM - ROW_GNC
            vec_ref[o:o + 1, :] = -dsp * LRU_C / (1.0 + jnp.exp(lam_v))
            wab_ref[0:LRU_WIDTH, :] = _fold_heads(dwa_acc[...])
            wab_ref[LRU_WIDTH:2 * LRU_WIDTH, :] = _fold_heads(dwx_acc[...])

    rev = lambda w: pl.BlockSpec((tm, w), lambda s: (n_tiles - 1 - s, 0))
    before = lambda w: pl.BlockSpec((SUB, w), lambda s: (jnp.maximum((n_tiles - 1 - s) * per_tile - 1, 0), 0))
    whole = lambda a: pl.BlockSpec(a.shape, lambda s: (0,) * a.ndim)
    smalls = (conv_w, rnn_conv_w, rnn_conv_b, wa, b_a, wx, b_x, lam, gnc, gnr, w_out)
    full = lambda w: pltpu.VMEM((tm, w), F32)
    return pl.pallas_call(
        body, grid=(n_tiles,),
        in_specs=[rev(IN_COLS), before(IN_COLS), rev(LRU_WIDTH), before(LRU_WIDTH), rev(D_MODEL)]
        + [whole(a) for a in smalls] + [HBM_SPEC] * (n_sums + 1),
        out_specs=[rev(IN_COLS), pl.BlockSpec((16, D_MODEL), lambda s: (0, 0)),
                   pl.BlockSpec((2 * LRU_WIDTH, HEAD_DIM), lambda s: (0, 0))] + [HBM_SPEC] * (n_sums + 1),
        out_shape=[jax.ShapeDtypeStruct((t_len, IN_COLS), BF16), jax.ShapeDtypeStruct((16, D_MODEL), F32),
                   jax.ShapeDtypeStruct((2 * LRU_WIDTH, HEAD_DIM), F32)]
        + [jax.ShapeDtypeStruct(s.shape, BF16) for s in chip_sums]
        + [jax.ShapeDtypeStruct((4,) + g_wout.shape[1:], BF16)],
        scratch_shapes=[full(IN_COLS), full(MIX_WIDTH), full(LRU_WIDTH), full(LRU_WIDTH), full(LRU_WIDTH),
                        full(LRU_WIDTH), full(LRU_WIDTH), full(LRU_WIDTH),
                        pltpu.VMEM((LRU_WIDTH, GROUP), BF16), pltpu.VMEM((LRU_WIDTH, GROUP), BF16),
                        pltpu.VMEM((N_ACC, SUB, LRU_WIDTH), F32),
                        pltpu.VMEM((LRU_WIDTH, GROUP), F32), pltpu.VMEM((LRU_WIDTH, GROUP), F32),
                        pltpu.VMEM((SUB, LRU_WIDTH), F32), pltpu.VMEM((1, LRU_WIDTH), F32),
                        pltpu.VMEM((SUB, CONV_WIDTH), F32), pltpu.VMEM((SUB, LRU_WIDTH), F32)]
        + _exchange_scratch(n_sums, 3) + _exchange_scratch(1, 4),
        compiler_params=_params(("arbitrary",), 56), name="mixer_bwd",
    )(u, u, hs, hs, dx1, *smalls, *chip_sums, g_wout)


def _in_proj_bwd(du, dx1, x, g_mix, win_t, tm, chip_sums, g_own):
    t_len = x.shape[0]
    n_steps = t_len // tm

    def body(du_ref, dx1_ref, x_ref, g_ref, w_ref, hs_ref, gown_ref,
             dx_ref, vec_ref, landed_ref, sib_ref, i_send, i_recv, d_send, d_recv):
        step = pl.program_id(0)
        _host_chip_exchange(step, n_steps, [hs_ref], [landed_ref], i_send, i_recv)
        _host_half_exchange(step, n_steps, gown_ref, sib_ref, d_send, d_recv)

        @pl.when(step == 0)
        def _():
            vec_ref[...] = jnp.zeros(vec_ref.shape, F32)

        dh = jnp.dot(du_ref[...], w_ref[...], preferred_element_type=F32)
        xv = x_ref[...]
        r1 = _rms(xv)
        xh = xv * r1
        vec_ref[0:1, :] += jnp.sum(dh * xh, axis=0, keepdims=True)
        dx_ref[...] = dx1_ref[...] + _rms_bwd(dh, xh, r1, g_ref[...])

    row_tile = lambda w: pl.BlockSpec((tm, w), lambda i: (i, 0))
    half_shape = (g_own.shape[0], g_own.shape[1] // 2, g_own.shape[2])
    return pl.pallas_call(
        body, grid=(n_steps,),
        in_specs=[row_tile(IN_COLS), row_tile(D_MODEL), row_tile(D_MODEL), pl.BlockSpec((1, D_MODEL), lambda i: (0, 0)),
                  pl.BlockSpec((IN_COLS, D_MODEL), lambda i: (0, 0))] + [HBM_SPEC] * 2,
        out_specs=[row_tile(D_MODEL), pl.BlockSpec((SUB, D_MODEL), lambda i: (0, 0))] + [HBM_SPEC] * 2,
        out_shape=[jax.ShapeDtypeStruct((t_len, D_MODEL), F32), jax.ShapeDtypeStruct((SUB, D_MODEL), F32),
                   jax.ShapeDtypeStruct(chip_sums.shape, BF16), jax.ShapeDtypeStruct(half_shape, BF16)],
        scratch_shapes=_exchange_scratch(1, 3) + [pltpu.SemaphoreType.DMA((1,)), pltpu.SemaphoreType.DMA((1,))],
        compiler_params=_params(("arbitrary",), 56), name="in_proj_bwd",
    )(du, dx1, x, g_mix, win_t, chip_sums, g_own)


def _tn_weight_grad(a, b, tk, name, pair=(), col_blocks=1):
    t_len, m = a.shape
    n = b.shape[1]
    n_steps = t_len // tk
    sent = tuple(pair)
    n_sent = len(sent)

    def body(a_ref, b_ref, *rest):
        srcs = rest[0:n_sent]
        o_ref = rest[n_sent]
        dsts = rest[n_sent + 1:2 * n_sent + 1]
        acc = rest[2 * n_sent + 1]
        sems = rest[2 * n_sent + 2:]
        j = pl.program_id(0)
        if pair:
            _host_pair_exchange(j, n_steps, srcs, dsts, *sems)

        @pl.when(j == 0)
        def _():
            acc[...] = jnp.zeros(acc.shape, F32)

        acc[...] += _dot_tn(a_ref[...].astype(BF16), b_ref[...].astype(BF16))

        @pl.when(j == n_steps - 1)
        def _():
            if col_blocks == 1:
                o_ref[...] = acc[...].astype(BF16)
            else:
                for k in range(col_blocks):
                    o_ref[k] = acc[:, k * nb:(k + 1) * nb].astype(BF16)

    nb = n // col_blocks
    out_dims = (m, n) if col_blocks == 1 else (col_blocks, m, nb)
    landed = [jax.ShapeDtypeStruct((4,) + g.shape[1:], BF16) for g in pair]
    scratch = [pltpu.VMEM((m, n), F32)]
    if n_sent:
        scratch += _exchange_scratch(n_sent, 4)
    return pl.pallas_call(
        body, grid=(n_steps,),
        in_specs=[pl.BlockSpec((tk, m), lambda j: (j, 0)), pl.BlockSpec((tk, n), lambda j: (j, 0))]
        + [HBM_SPEC] * n_sent,
        out_specs=[pl.BlockSpec(out_dims, lambda j: (0,) * len(out_dims))] + [HBM_SPEC] * n_sent,
        out_shape=[jax.ShapeDtypeStruct(out_dims, BF16)] + landed,
        scratch_shapes=scratch,
        compiler_params=_params(("arbitrary",), 56), name=name,
    )(a, b, *sent)


def _w_in_grad_part(du, h, tk, name, chip_ids, chip=(), halves=None, small=None):
    t_len = du.shape[0]
    n_t = t_len // tk
    n_q = chip_ids.shape[0]
    width = 2 * (IN_COLS // N_DEV)
    n_steps = n_q * n_t
    n_chip = len(chip)
    sent = tuple(chip) + (() if halves is None else (halves,)) + (() if small is None else tuple(small))
    n_sent = len(sent)

    def body(ids_ref, a_ref, b_ref, *rest):
        srcs = rest[0:n_sent]
        o_ref = rest[n_sent]
        dsts = rest[n_sent + 1:2 * n_sent + 1]
        acc = rest[2 * n_sent + 1]
        sems = list(rest[2 * n_sent + 2:])
        j = pl.program_id(1)
        step = pl.program_id(0) * n_t + j
        if chip:
            _host_chip_exchange(step, n_steps, srcs[0:n_chip], dsts[0:n_chip], sems.pop(0), sems.pop(0))
        if halves is not None:
            _host_half_exchange(step, n_steps, srcs[n_chip], dsts[n_chip], sems.pop(0), sems.pop(0))
        if small is not None:
            _host_small_exchange(step, n_steps, *srcs[n_sent - 3:], *dsts[n_sent - 3:], *sems)

        @pl.when(j == 0)
        def _():
            acc[...] = jnp.zeros(acc.shape, F32)

        acc[...] += _dot_tn(a_ref[...], b_ref[...])

        @pl.when(j == n_t - 1)
        def _():
            o_ref[0] = acc[...].astype(BF16)

    landed = [jax.ShapeDtypeStruct(s.shape, BF16) for s in chip]
    scratch = [pltpu.VMEM((width, D_MODEL), F32)]
    if chip:
        scratch += _exchange_scratch(len(chip), 3)
    if halves is not None:
        landed.append(jax.ShapeDtypeStruct((halves.shape[0], halves.shape[1] // 2, halves.shape[2]), BF16))
        scratch += [pltpu.SemaphoreType.DMA((halves.shape[0],)), pltpu.SemaphoreType.DMA((halves.shape[0],))]
    if small is not None:
        vec_m, vec_b, wab = small
        landed += [jax.ShapeDtypeStruct((N_DEV,) + vec_m.shape, F32), jax.ShapeDtypeStruct((N_DEV,) + vec_b.shape, F32),
                   jax.ShapeDtypeStruct((N_DEV, wab.shape[0] // N_DEV, wab.shape[1]), F32)]
        scratch += _exchange_scratch(3, N_DEV) + [pltpu.SemaphoreType.DMA((2,))]
    grid_spec = pltpu.PrefetchScalarGridSpec(
        num_scalar_prefetch=1, grid=(n_q, n_t),
        in_specs=[pl.BlockSpec((tk, width), lambda q, j, ids: (j, ids[q])),
                  pl.BlockSpec((tk, D_MODEL), lambda q, j, ids: (j, 0))] + [HBM_SPEC] * n_sent,
        out_specs=[pl.BlockSpec((1, width, D_MODEL), lambda q, j, ids: (q, 0, 0))] + [HBM_SPEC] * n_sent,
        scratch_shapes=scratch)
    return pl.pallas_call(
        body, grid_spec=grid_spec, out_shape=[jax.ShapeDtypeStruct((n_q, width, D_MODEL), BF16)] + landed,
        compiler_params=_params(("arbitrary", "arbitrary"), 40), name=name,
    )(chip_ids, du, h, *sent)


def _adamw(w, g, m, v):
    m = ADAM_B1 * m + (1.0 - ADAM_B1) * g
    v = ADAM_B2 * v + (1.0 - ADAM_B2) * (g * g)
    delta = -ADAM_LR * ((m / BC1) / (jnp.sqrt(v / BC2) + ADAM_EPS) + ADAM_WD * w)
    return delta, m, v


def _update_sharded(g, landed, w, m, v, rows_blk, name):
    rows, cols = w.shape

    def body(g_ref, l_ref, w_ref, m_ref, v_ref, og, od, om, ov):
        gv = g_ref[...]
        for j in range(3):
            gv = gv + l_ref[j].astype(F32)
        delta, mn, vn = _adamw(w_ref[...], gv, m_ref[...], v_ref[...])
        og[...] = gv
        od[...] = delta
        om[...] = mn
        ov[...] = vn

    blk = pl.BlockSpec((rows_blk, cols), lambda i: (i, 0))
    shape = pltpu.HBM((rows, cols), F32)
    return pl.pallas_call(
        body, grid=(rows // rows_blk,),
        in_specs=[blk, pl.BlockSpec((3, rows_blk, cols), lambda i: (0, i, 0)), blk, blk, blk],
        out_specs=[blk] * 4, out_shape=[shape] * 4,
        compiler_params=_params(("arbitrary",), 32), name=name,
    )(*_in_hbm(g, landed, w, m, v))


def _update_w_in(g_own, sib_own, landed, w, m, v, core, rows_blk):
    rows, cols = w.shape
    pad_cols = -(-cols // 128) * 128

    def body(core_ref, g_ref, s_ref, l_ref, w_ref, m_ref, v_ref, og, od, om, ov, padbuf, turned):
        gt = g_ref[0, 0].astype(F32) + s_ref[0].astype(F32)
        for j in range(3):
            gt = gt + l_ref[j].astype(F32)
        padbuf[...] = jnp.zeros(padbuf.shape, F32)
        padbuf[0:cols, :] = gt
        turned[...] = padbuf[...].T
        gv = turned[:, 0:cols]
        delta, mn, vn = _adamw(w_ref[...], gv, m_ref[...], v_ref[...])
        og[...] = gv
        od[...] = delta
        om[...] = mn
        ov[...] = vn

    blk = pl.BlockSpec((rows_blk, cols), lambda i, cr: (i, 0))
    grid_spec = pltpu.PrefetchScalarGridSpec(
        num_scalar_prefetch=1, grid=(rows // rows_blk,),
        in_specs=[pl.BlockSpec((1, 1, cols, rows_blk), lambda i, cr: (0, cr[0], 0, i)),
                  pl.BlockSpec((1, cols, rows_blk), lambda i, cr: (0, 0, i)),
                  pl.BlockSpec((3, cols, rows_blk), lambda i, cr: (0, 0, i)), blk, blk, blk],
        out_specs=[blk] * 4,
        scratch_shapes=[pltpu.VMEM((pad_cols, rows_blk), F32), pltpu.VMEM((rows_blk, pad_cols), F32)])
    return pl.pallas_call(
        body, grid_spec=grid_spec, out_shape=[pltpu.HBM((rows, cols), F32)] * 4,
        compiler_params=_params(("arbitrary",), 32), name="update_w_in",
    )(core, *_in_hbm(g_own.reshape(1, 2, cols, rows), sib_own, landed, w, m, v))


def _update_small(vsum, wsum, g_cw, g_rw, weights, moments_m, moments_v):
    n = len(weights)

    def body(*refs):
        vs, ws, gcw, grw = refs[0:4]
        w_refs = refs[4:4 + n]
        m_refs = refs[4 + n:4 + 2 * n]
        v_refs = refs[4 + 2 * n:4 + 3 * n]
        outs = refs[4 + 3 * n:]
        loss_ref = outs[0]
        loss_ref[...] = jnp.sum(vs[ROW_LOSS:ROW_LOSS + 1, :], axis=1, keepdims=True)
        grads = [
            vs[ROW_GMIX:ROW_GMIX + 1, :], gcw[...], grw[...], vs[ROW_BR:ROW_BR + 1, :],
            ws[0:LRU_WIDTH, :], vs[ROW_BA:ROW_BA + 1, :], ws[LRU_WIDTH:2 * LRU_WIDTH, :], vs[ROW_BX:ROW_BX + 1, :],
            vs[ROW_LAM:ROW_LAM + 1, :], vs[ROW_GNC:ROW_GNC + 1, 0:CONV_WIDTH], vs[ROW_GNR:ROW_GNR + 1, :],
            vs[ROW_GMLP:ROW_GMLP + 1, :], vs[ROW_GF:ROW_GF + 1, :],
        ]
        for k in range(n):
            gk = grads[k]
            delta, mn, vn = _adamw(w_refs[k][...], gk, m_refs[k][...], v_refs[k][...])
            outs[1 + 4 * k][...] = gk
            outs[2 + 4 * k][...] = delta
            outs[3 + 4 * k][...] = mn
            outs[4 + 4 * k][...] = vn

    whole = lambda a: pl.BlockSpec(a.shape, lambda i: (0,) * len(a.shape))
    out_shape = [jax.ShapeDtypeStruct((1, 1), F32)]
    for w in weights:
        out_shape += [jax.ShapeDtypeStruct(w.shape, F32)] * 4
    args = (vsum, wsum, g_cw, g_rw, *weights, *moments_m, *moments_v)
    return pl.pallas_call(
        body, grid=(1,), out_shape=out_shape, in_specs=[whole(a) for a in args], out_specs=[whole(s) for s in out_shape],
        compiler_params=_params(("arbitrary",), 32), name="update_small",
    )(*args)


def kernel(x, norm_mix_g, w_in, conv_w, rnn_conv_w, rnn_conv_b, w_a, b_a, w_x, b_x, lru_lambda, g_norm_conv, g_norm_rnn, w_out, norm_mlp_g, w_mlp_in, w_mlp_out, final_norm_g, loss_target, m_norm_mix_g, m_w_in, m_conv_w, m_rnn_conv_w, m_rnn_conv_b, m_w_a, m_b_a, m_w_x, m_b_x, m_lru_lambda, m_g_norm_conv, m_g_norm_rnn, m_w_out, m_norm_mlp_g, m_w_mlp_in, m_w_mlp_out, m_final_norm_g, v_norm_mix_g, v_w_in, v_conv_w, v_rnn_conv_w, v_rnn_conv_b, v_w_a, v_b_a, v_w_x, v_b_x, v_lru_lambda, v_g_norm_conv, v_g_norm_rnn, v_w_out, v_norm_mlp_g, v_w_mlp_in, v_w_mlp_out, v_final_norm_g):
    t_len = x.shape[1]
    my_id = 4 * lax.axis_index("x") + 2 * lax.axis_index("y") + lax.axis_index("c")
    tm = min(256, t_len)
    tb = min(512, t_len)
    tk = min(512, t_len)

    xs = x.reshape(t_len, D_MODEL)
    tgt = loss_target.reshape(t_len, D_MODEL)
    flat = lambda a: a.reshape(a.shape[-2:]) if a.ndim == 3 else a.reshape(1, -1)
    heads = lambda a: a.reshape(LRU_WIDTH, HEAD_DIM)

    win_shard, wout_shard, w1_shard, w2_shard, cp_shard = _prep_shards(
        flat(w_in), flat(w_out), flat(w_mlp_in), flat(w_mlp_out), flat(conv_w), flat(rnn_conv_w))

    u, h, win_t, wout_f, cp_full = _in_proj(xs, flat(norm_mix_g), (win_shard, wout_shard, cp_shard), tb)
    cpack = cp_full.reshape(N_DEV, 8, 128)
    conv_full = jnp.transpose(cpack[:, 0:3, 0:64], (1, 0, 2)).reshape(3, CONV_WIDTH)
    rnn_full = jnp.transpose(cpack[:, 3:7, :], (1, 0, 2)).reshape(4, LRU_WIDTH)
    mixer_small = (conv_full, rnn_full, flat(rnn_conv_b), heads(w_a), flat(b_a), heads(w_x), flat(b_x),
                   flat(lru_lambda), flat(g_norm_conv), flat(g_norm_rnn))
    hs, y, w1_blk, w2_blk = _mixer_fwd(u, *mixer_small, w1_shard, w2_shard, tm)
    dx1, z, dpre, h2, dx2, vec_m = _mlp_fwd_bwd(xs, y, tgt, flat(norm_mlp_g), flat(final_norm_g), wout_f, w1_blk,
                                                w2_blk.reshape(D_FF, D_MODEL), tb)
    (g_w1,) = _tn_weight_grad(h2, dpre, tk, "w_mlp_in_grad", col_blocks=N_DEV)
    (g_w2,) = _tn_weight_grad(z, dx2, tk, "w_mlp_out_grad")
    g_w2 = g_w2.reshape(N_DEV, D_FF // N_DEV, D_MODEL)
    g_wout, sib_w1, sib_w2 = _tn_weight_grad(y, dx1, tk, "w_out_grad", pair=(g_w1, g_w2))
    g_wout = g_wout.reshape(N_DEV, MIX_WIDTH // N_DEV, D_MODEL)
    hsend_w1, own_w1 = _pair_sum(g_w1, sib_w1, "pair_sum_w_mlp_in")
    hsend_w2, own_w2 = _pair_sum(g_w2, sib_w2, "pair_sum_w_mlp_out")
    du, vec_b, wab, landed_w1, landed_w2, sib_wout = _mixer_bwd(
        u, hs, dx1, *mixer_small, wout_f, (hsend_w1, hsend_w2), g_wout, tm)
    hsend_wout, own_wout = _pair_sum(g_wout, sib_wout, "pair_sum_w_out")
    ax, ay, ac = lax.axis_index("x"), lax.axis_index("y"), lax.axis_index("c")
    chip_ids = jnp.stack([2 * cx + cy for cx, cy in [(ax, ay)] + _other_chips(ax, ay)]).astype(jnp.int32)
    core = jnp.reshape(ac, (1,)).astype(jnp.int32)
    tw = min(1024, t_len)
    g_others, landed_wout, vrecv_m, vrecv_b, wrecv = _w_in_grad_part(
        du, h, tw, "w_in_grad_others", chip_ids[1:4], chip=(hsend_wout,), small=(vec_m, vec_b, wab))
    g_own, sib_others = _w_in_grad_part(du, h, tw, "w_in_grad_own", chip_ids[0:1], halves=g_others)
    hsend_win = _pair_sum_parts(g_others, sib_others, core)
    grad_x, vec_x, landed_win, sib_own = _in_proj_bwd(du, dx1, xs, flat(norm_mix_g), win_t, tm, hsend_win, g_own)

    vsum, wsum = _final_small(vrecv_m, vrecv_b, wab, wrecv, vec_x)

    up_win = _update_w_in(g_own, sib_own, landed_win, flat(w_in), flat(m_w_in), flat(v_w_in), core, 256)
    up_wout = _update_sharded(own_wout, landed_wout, flat(w_out), flat(m_w_out), flat(v_w_out), 96, "update_w_out")
    up_w1 = _update_sharded(own_w1, landed_w1, flat(w_mlp_in), flat(m_w_mlp_in), flat(v_w_mlp_in), 256,
                            "update_w_mlp_in")
    up_w2 = _update_sharded(own_w2, landed_w2, flat(w_mlp_out), flat(m_w_mlp_out), flat(v_w_mlp_out), 256,
                            "update_w_mlp_out")

    g_cw = lax.dynamic_slice(vsum, (ROW_CW, 64 * my_id), (3, 64))
    g_rw = lax.dynamic_slice(vsum, (ROW_RW, 128 * my_id), (4, 128))
    small_w = (norm_mix_g, conv_w, rnn_conv_w, rnn_conv_b, w_a, b_a, w_x, b_x, lru_lambda, g_norm_conv, g_norm_rnn,
               norm_mlp_g, final_norm_g)
    small_m = (m_norm_mix_g, m_conv_w, m_rnn_conv_w, m_rnn_conv_b, m_w_a, m_b_a, m_w_x, m_b_x, m_lru_lambda,
               m_g_norm_conv, m_g_norm_rnn, m_norm_mlp_g, m_final_norm_g)
    small_v = (v_norm_mix_g, v_conv_w, v_rnn_conv_w, v_rnn_conv_b, v_w_a, v_b_a, v_w_x, v_b_x, v_lru_lambda,
               v_g_norm_conv, v_g_norm_rnn, v_norm_mlp_g, v_final_norm_g)
    is_heads = (False, False, False, False, True, False, True, False, False, False, False, False, False)
    as2d = lambda arrs: [heads(a) if hd else flat(a) for a, hd in zip(arrs, is_heads)]
    small_out = _update_small(vsum, wsum, g_cw, g_rw, as2d(small_w), as2d(small_m), as2d(small_v))
    loss = small_out[0].reshape(())

    names = ["norm_mix_g", "w_in", "conv_w", "rnn_conv_w", "rnn_conv_b", "w_a", "b_a", "w_x", "b_x", "lru_lambda",
             "g_norm_conv", "g_norm_rnn", "w_out", "norm_mlp_g", "w_mlp_in", "w_mlp_out", "final_norm_g"]
    originals = dict(zip(names, (norm_mix_g, w_in, conv_w, rnn_conv_w, rnn_conv_b, w_a, b_a, w_x, b_x, lru_lambda,
                                 g_norm_conv, g_norm_rnn, w_out, norm_mlp_g, w_mlp_in, w_mlp_out, final_norm_g)))
    results = {"w_in": up_win, "w_out": up_wout, "w_mlp_in": up_w1, "w_mlp_out": up_w2}
    small_names = ["norm_mix_g", "conv_w", "rnn_conv_w", "rnn_conv_b", "w_a", "b_a", "w_x", "b_x", "lru_lambda",
                   "g_norm_conv", "g_norm_rnn", "norm_mlp_g", "final_norm_g"]
    for k, nm in enumerate(small_names):
        results[nm] = small_out[1 + 4 * k:5 + 4 * k]
    out = [loss, grad_x.reshape(x.shape)]
    for kind in range(4):
        out += [results[nm][kind].reshape(originals[nm].shape) for nm in names]
    return tuple(out)
```

```python
import functools

import jax
import jax.numpy as jnp
from jax import lax
from jax.experimental import pallas as pl
from jax.experimental.pallas import tpu as pltpu

F32 = jnp.float32
BF16 = jnp.bfloat16

D_MODEL = 1024
HEAD_DIM = 64
CONV_WIDTH = 512
LRU_WIDTH = 1024
MIX_WIDTH = CONV_WIDTH + LRU_WIDTH
IN_COLS = 3 * CONV_WIDTH + 2 * LRU_WIDTH
D_FF = 4 * D_MODEL
GROUP = 256
EPS = 1e-6
LRU_C = 8.0
N_DEV = 8
SUB = 8

OFF_GB, OFF_GC, OFF_V, OFF_XR, OFF_G = 0, 512, 1024, 1536, 2560

ADAM_LR, ADAM_B1, ADAM_B2, ADAM_EPS, ADAM_WD, ADAM_STEP = 0.001, 0.9, 0.999, 1e-08, 0.01, 10
BC1 = 1.0 - ADAM_B1 ** ADAM_STEP
BC2 = 1.0 - ADAM_B2 ** ADAM_STEP

MIB = 1024 * 1024
MESH = pl.DeviceIdType.MESH

VEC_ROWS = 32
ROW_GF, ROW_GMLP, ROW_LOSS = 0, 1, 2
ROW_GNC, ROW_GNR, ROW_BR, ROW_BA, ROW_BX, ROW_LAM, ROW_CW, ROW_RW = 8, 9, 10, 11, 12, 13, 14, 17
ROW_GMIX = 24
ACC_GNC, ACC_GNR, ACC_BR, ACC_BA, ACC_BX, ACC_SP, ACC_CW, ACC_RW, N_ACC = 0, 1, 2, 3, 4, 5, 6, 9, 13


def _params(semantics=None, vmem_mib=48):
    return pltpu.CompilerParams(dimension_semantics=semantics, vmem_limit_bytes=vmem_mib * MIB)


def _rms(x):
    return lax.rsqrt(jnp.mean(x * x, axis=-1, keepdims=True) + EPS)


def _rms_bwd(dy, xhat, r, g):
    dyh = dy * g
    return r * (dyh - xhat * jnp.mean(dyh * xhat, axis=-1, keepdims=True))


def _sigmoid(x):
    return 0.5 + 0.5 * jnp.tanh(0.5 * x)


def _gelu(x):
    c0, c1 = 0.7978845608028654, 0.044715
    t = jnp.tanh(c0 * (x + c1 * x * x * x))
    ge = 0.5 * x * (1.0 + t)
    dge = 0.5 * (1.0 + t) + 0.5 * x * (1.0 - t * t) * c0 * (1.0 + 3.0 * c1 * x * x)
    return ge, dge


def _softplus_neg(lam):
    z = -lam
    e = jnp.exp(-jnp.abs(z))
    return jnp.maximum(z, 0.0) + jnp.where(e < 1e-4, e * (1.0 - 0.5 * e), jnp.log(1.0 + e))


def _lru_gates(pa, px, sp_c):
    ra = _sigmoid(pa)
    ii = _sigmoid(px)
    la = -ra * sp_c
    a = jnp.exp(la)
    x2 = 2.0 * la
    series = -x2 * (1.0 + x2 * (0.5 + x2 * (1.0 / 6.0 + x2 * (1.0 / 24.0))))
    m2 = jnp.where(x2 > -0.01, series, 1.0 - a * a)
    inv_mult = lax.rsqrt(m2)
    mult = jnp.where(m2 > 0.0, m2 * inv_mult, 0.0)
    return ra, ii, a, mult, inv_mult


def _down(cur, prev, s, row):
    return jnp.where(row >= s, pltpu.roll(cur, s, 0), pltpu.roll(prev, s, 0))


def _up(cur, nxt, s, row):
    return jnp.where(row < SUB - s, pltpu.roll(cur, SUB - s, 0), pltpu.roll(nxt, SUB - s, 0))


def _scan8_fwd(a, b, row):
    for s in (1, 2, 4):
        m = row >= s
        a_sh = pltpu.roll(a, s, 0)
        b_sh = pltpu.roll(b, s, 0)
        b = jnp.where(m, a * b_sh + b, b)
        a = jnp.where(m, a * a_sh, a)
    return a, b


def _scan8_rev(a, b, row):
    for s in (1, 2, 4):
        m = row < SUB - s
        a_sh = pltpu.roll(a, SUB - s, 0)
        b_sh = pltpu.roll(b, SUB - s, 0)
        b = jnp.where(m, a * b_sh + b, b)
        a = jnp.where(m, a * a_sh, a)
    return a, b


def _group_mask(shape):
    r = lax.broadcasted_iota(jnp.int32, shape, 0)
    c = lax.broadcasted_iota(jnp.int32, shape, 1)
    return ((r % GROUP) // HEAD_DIM) == (c // HEAD_DIM)


def _expand_heads(w):
    j = lax.broadcasted_iota(jnp.int32, (HEAD_DIM, GROUP), 0)
    c = lax.broadcasted_iota(jnp.int32, (HEAD_DIM, GROUP), 1)
    spread = (c % HEAD_DIM == j).astype(BF16)
    e = jnp.dot(w.astype(BF16), spread, preferred_element_type=F32)
    return jnp.where(_group_mask(e.shape), e, 0.0).astype(BF16)


def _fold_heads(p):
    p = jnp.where(_group_mask(p.shape), p, 0.0)
    c = lax.broadcasted_iota(jnp.int32, (GROUP, HEAD_DIM), 0)
    j = lax.broadcasted_iota(jnp.int32, (GROUP, HEAD_DIM), 1)
    fold = (c % HEAD_DIM == j).astype(BF16)
    hi = p.astype(BF16)
    rest = p - hi.astype(F32)
    mid = rest.astype(BF16)
    lo = (rest - mid.astype(F32)).astype(BF16)
    dot = functools.partial(jnp.dot, preferred_element_type=F32)
    return dot(hi, fold) + dot(mid, fold) + dot(lo, fold)


def _block_diag_apply(xb, wbd_ref):
    parts = [jnp.dot(xb[:, g * GROUP:(g + 1) * GROUP], wbd_ref[g * GROUP:(g + 1) * GROUP, :],
                     preferred_element_type=F32) for g in range(LRU_WIDTH // GROUP)]
    return jnp.concatenate(parts, axis=1)


def _block_diag_apply_t(db, wbd_ref):
    parts = [lax.dot_general(db[:, g * GROUP:(g + 1) * GROUP], wbd_ref[g * GROUP:(g + 1) * GROUP, :],
                             (((1,), (1,)), ((), ())), preferred_element_type=F32)
             for g in range(LRU_WIDTH // GROUP)]
    return jnp.concatenate(parts, axis=1)


def _dot_nt(a, b):
    return lax.dot_general(a, b, (((1,), (1,)), ((), ())), preferred_element_type=F32)


def _dot_tn(a, b):
    return lax.dot_general(a, b, (((0,), (0,)), ((), ())), preferred_element_type=F32)


CHUNKS_IN_FLIGHT = 4


def _chunk_loop(n_chunks, chunk, init):
    def body(k, carry):
        for j in range(CHUNKS_IN_FLIGHT):
            carry = chunk(k * CHUNKS_IN_FLIGHT + j, carry)
        return carry

    return lax.fori_loop(0, n_chunks // CHUNKS_IN_FLIGHT, body, init)


def _place():
    x, y, c = lax.axis_index("x"), lax.axis_index("y"), lax.axis_index("c")
    return x, y, c


def _block_id(chip, core):
    return 4 * chip[0] + 2 * chip[1] + core


def _other_chips(x, y):
    return [(1 - x, y), (x, 1 - y), (1 - x, 1 - y)]


def _remote_copy(src, dst, send_sem, recv_sem, to):
    return pltpu.make_async_remote_copy(src_ref=src, dst_ref=dst, send_sem=send_sem, recv_sem=recv_sem,
                                        device_id=to, device_id_type=MESH)


HBM_SPEC = pl.BlockSpec(memory_space=pl.ANY)


def _in_hbm(*arrays):
    return [pltpu.with_memory_space_constraint(a, pltpu.HBM) for a in arrays]


def _prep_shards(w_in, w_out, w_mlp_in, w_mlp_out, conv_w, rnn_conv_w):
    n_in = w_in.shape[1]

    def body(win_ref, wout_ref, w1_ref, w2_ref, cw_ref, rw_ref, o_win, o_wout, o_w1, o_w2, o_cp, padbuf):
        padbuf[...] = jnp.zeros(padbuf.shape, F32)
        padbuf[:, 0:n_in] = win_ref[...]
        o_win[...] = padbuf[...].T[0:n_in, :].astype(BF16)
        o_wout[...] = wout_ref[...].astype(BF16)
        o_w1[...] = w1_ref[...].astype(BF16)
        o_w2[...] = w2_ref[...].astype(BF16)
        o_cp[...] = jnp.zeros(o_cp.shape, F32)
        o_cp[0:3, 0:64] = cw_ref[...]
        o_cp[3:7, :] = rw_ref[...]

    whole = lambda shape: pl.BlockSpec(shape, lambda i: (0,) * len(shape))
    args = (w_in, w_out, w_mlp_in, w_mlp_out, conv_w, rnn_conv_w)
    shapes = [((n_in, D_MODEL), BF16), (w_out.shape, BF16), (w_mlp_in.shape, BF16), (w_mlp_out.shape, BF16),
              ((8, 128), F32)]
    return pl.pallas_call(
        body, grid=(1,), out_shape=[jax.ShapeDtypeStruct(s, d) for s, d in shapes],
        in_specs=[whole(a.shape) for a in args], out_specs=[whole(s) for s, _ in shapes],
        scratch_shapes=[pltpu.VMEM((D_MODEL, 512), F32)],
        compiler_params=_params(("arbitrary",), 40), name="prep_shards",
    )(*args)


def _host_all_gather(step, n_steps, shards, fulls, send_sems, recv_sems, local_sems):
    x, y, c = _place()
    me = (x, y, c)
    my_id = _block_id((x, y), c)
    sibling = (x, y, 1 - c)
    chips = _other_chips(x, y)
    n_arr = len(shards)

    def copy(arr, k, block, to, src=None):
        dst = fulls[arr].at[block]
        return _remote_copy(dst if src is None else src, dst, send_sems.at[arr, k], recv_sems.at[arr, k], to)

    def local(arr):
        return pltpu.make_async_copy(shards[arr], fulls[arr].at[my_id], local_sems.at[arr])

    @pl.when(step == 0)
    def _():
        for arr in range(n_arr):
            local(arr).start()
            copy(arr, 0, my_id, sibling, shards[arr]).start()
            for j, chip in enumerate(chips):
                copy(arr, 1 + j, my_id, (*chip, c), shards[arr]).start()

    @pl.when(step == max(n_steps - 2, 0))
    def _():
        for j, chip in enumerate(chips):
            for arr in range(n_arr):
                copy(arr, 1 + j, _block_id(chip, c), me).wait_recv()
                copy(arr, 4 + j, _block_id(chip, c), sibling).start()

    @pl.when(step == n_steps - 1)
    def _():
        for arr in range(n_arr):
            copy(arr, 0, _block_id((x, y), 1 - c), me).wait_recv()
            for j, chip in enumerate(chips):
                copy(arr, 4 + j, _block_id(chip, 1 - c), me).wait_recv()
            for k in range(4):
                copy(arr, k, my_id, me, shards[arr]).wait_send()
            for j, chip in enumerate(chips):
                copy(arr, 4 + j, _block_id(chip, c), me).wait_send()
            local(arr).wait()


def _host_pair_exchange(step, n_steps, gs, sibs, send_sems, recv_sems):
    x, y, c = _place()
    sibling = (x, y, 1 - c)
    chips = [(x, y)] + _other_chips(x, y)

    def d2d(arr, q):
        return _remote_copy(gs[arr].at[_block_id(chips[q], 1 - c)], sibs[arr].at[q],
                            send_sems.at[arr, q], recv_sems.at[arr, q], sibling)

    @pl.when(step == 0)
    def _():
        for arr in range(len(gs)):
            for q in (1, 2, 3, 0):
                d2d(arr, q).start()

    @pl.when(step == n_steps - 1)
    def _():
        for arr in range(len(gs)):
            for q in range(4):
                d2d(arr, q).wait()


def _host_chip_exchange(step, n_steps, hsends, hrecvs, send_sems, recv_sems):
    x, y, c = _place()
    chips = _other_chips(x, y)

    def ici(arr, j):
        return _remote_copy(hsends[arr].at[j], hrecvs[arr].at[j], send_sems.at[arr, j], recv_sems.at[arr, j],
                            (*chips[j], c))

    @pl.when(step == 0)
    def _():
        for arr in range(len(hsends)):
            for j in range(3):
                ici(arr, j).start()

    @pl.when(step == n_steps - 1)
    def _():
        for arr in range(len(hsends)):
            for j in range(3):
                ici(arr, j).wait()


def _host_half_exchange(step, n_steps, parts, sibs, send_sems, recv_sems):
    x, y, c = _place()
    n_q, rows2, _ = parts.shape
    half = rows2 // 2

    def d2d(q):
        src = parts.at[q, pl.ds(pl.multiple_of((1 - c) * half, 16), half), :]
        return _remote_copy(src, sibs.at[q], send_sems.at[q], recv_sems.at[q], (x, y, 1 - c))

    @pl.when(step == 0)
    def _():
        for q in range(n_q):
            d2d(q).start()

    @pl.when(step == n_steps - 1)
    def _():
        for q in range(n_q):
            d2d(q).wait()


def _peer(x, y, c, k):
    return (x ^ ((k >> 2) & 1), y ^ ((k >> 1) & 1), c ^ (k & 1))


def _host_small_exchange(step, n_steps, vec_m, vec_b, wab, vrecv_m, vrecv_b, wrecv, send_sems, recv_sems, local_sems):
    x, y, c = _place()
    my_id = _block_id((x, y), c)
    wrows = wab.shape[0] // N_DEV

    def copies(k):
        to = _peer(x, y, c, k)
        block = wab.at[pl.ds(pl.multiple_of(_block_id(to[0:2], to[2]) * wrows, SUB), wrows), :]
        return [_remote_copy(vec_m, vrecv_m.at[my_id], send_sems.at[0, k], recv_sems.at[0, k], to),
                _remote_copy(vec_b, vrecv_b.at[my_id], send_sems.at[1, k], recv_sems.at[1, k], to),
                _remote_copy(block, wrecv.at[k], send_sems.at[2, k], recv_sems.at[2, k], to)]

    mine = [pltpu.make_async_copy(vec_m, vrecv_m.at[my_id], local_sems.at[0]),
            pltpu.make_async_copy(vec_b, vrecv_b.at[my_id], local_sems.at[1])]

    @pl.when(step == 0)
    def _():
        for cp in mine:
            cp.start()
        for k in range(1, N_DEV):
            for cp in copies(k):
                cp.start()

    @pl.when(step == n_steps - 1)
    def _():
        for k in range(1, N_DEV):
            for cp in copies(k):
                cp.wait()
        for cp in mine:
            cp.wait()


def _pair_sum_parts(parts, sibs, core):
    n_q, rows2, cols = parts.shape
    half = rows2 // 2

    def body(core_ref, g_ref, s_ref, o_ref):
        o_ref[0] = (g_ref[0, 0].astype(F32) + s_ref[0].astype(F32)).astype(BF16)

    block = (1, half, cols)
    grid_spec = pltpu.PrefetchScalarGridSpec(
        num_scalar_prefetch=1, grid=(n_q,),
        in_specs=[pl.BlockSpec((1, 1, half, cols), lambda q, cr: (q, cr[0], 0, 0)),
                  pl.BlockSpec(block, lambda q, cr: (q, 0, 0))],
        out_specs=pl.BlockSpec(block, lambda q, cr: (q, 0, 0)))
    return pl.pallas_call(
        body, grid_spec=grid_spec, out_shape=pltpu.HBM((n_q, half, cols), BF16),
        compiler_params=_params(("arbitrary",), 32), name="pair_sum_w_in",
    )(core, *_in_hbm(parts.reshape(n_q, 2, half, cols), sibs))


def _pair_sum(g, sib, name):
    _, rows, cols = g.shape
    x, y, c = _place()
    slots = jnp.stack([_block_id(chip, c) for chip in [(x, y)] + _other_chips(x, y)]).astype(jnp.int32)

    def body(slots_ref, g_ref, sib_ref, hs_ref, own_ref):
        q = pl.program_id(0)
        both = g_ref[0].astype(F32) + sib_ref[0].astype(F32)

        @pl.when(q == 0)
        def _():
            own_ref[...] = both

        @pl.when(q > 0)
        def _():
            hs_ref[0] = both.astype(BF16)

    block = (1, rows, cols)
    grid_spec = pltpu.PrefetchScalarGridSpec(
        num_scalar_prefetch=1, grid=(4,),
        in_specs=[pl.BlockSpec(block, lambda q, s: (s[q], 0, 0)), pl.BlockSpec(block, lambda q, s: (q, 0, 0))],
        out_specs=[pl.BlockSpec(block, lambda q, s: (jnp.maximum(q - 1, 0), 0, 0)),
                   pl.BlockSpec((rows, cols), lambda q, s: (0, 0))])
    return pl.pallas_call(
        body, grid_spec=grid_spec,
        out_shape=(pltpu.HBM((3, rows, cols), BF16), pltpu.HBM((rows, cols), F32)),
        compiler_params=_params(("arbitrary",), 32), name=name,
    )(slots, *_in_hbm(g, sib))


def _exchange_scratch(n_arr, n_copies):
    return [pltpu.SemaphoreType.DMA((n_arr, n_copies)), pltpu.SemaphoreType.DMA((n_arr, n_copies))]


def _final_small(vrecv_m, vrecv_b, wab, wrecv, vec_x):
    wrows = wab.shape[0] // N_DEV

    def body(vm_ref, vb_ref, w_ref, wr_ref, vx_ref, o_vec, o_w, xrecv, wred, x_send, x_recv, b_send, b_recv):
        x, y, c = _place()
        my_id = _block_id((x, y), c)
        my_rows = pl.ds(pl.multiple_of(my_id * wrows, SUB), wrows)

        def xcopy(k):
            return _remote_copy(vx_ref, xrecv.at[my_id], x_send.at[k], x_recv.at[k], _peer(x, y, c, k))

        def bcopy(k):
            return _remote_copy(wred, o_w.at[my_rows, :], b_send.at[k], b_recv.at[k], _peer(x, y, c, k))

        xrecv[my_id] = vx_ref[...]
        for k in range(1, N_DEV):
            xcopy(k).start()
        red = w_ref[my_rows, :]
        for k in range(1, N_DEV):
            red = red + wr_ref[k]
        wred[...] = red
        o_w[my_rows, :] = red
        for k in range(1, N_DEV):
            bcopy(k).start()
        for k in range(1, N_DEV):
            xcopy(k).wait_recv()
        for rows, ref in ((slice(0, 8), vm_ref), (slice(8, 24), vb_ref), (slice(24, 32), xrecv)):
            tot = ref[0]
            for s in range(1, N_DEV):
                tot = tot + ref[s]
            o_vec[rows, :] = tot
        for k in range(1, N_DEV):
            bcopy(k).wait_recv()
        for k in range(1, N_DEV):
            xcopy(k).wait_send()
            bcopy(k).wait_send()

    vm = pl.BlockSpec(memory_space=pltpu.VMEM)
    dma8 = pltpu.SemaphoreType.DMA((N_DEV,))
    return pl.pallas_call(
        body, out_shape=(jax.ShapeDtypeStruct((VEC_ROWS, D_MODEL), F32), jax.ShapeDtypeStruct(wab.shape, F32)),
        in_specs=[vm] * 5, out_specs=[vm] * 2,
        scratch_shapes=[pltpu.VMEM((N_DEV, SUB, D_MODEL), F32), pltpu.VMEM((wrows, HEAD_DIM), F32),
                        dma8, dma8, dma8, dma8],
        compiler_params=_params(vmem_mib=32), name="final_small",
    )(vrecv_m, vrecv_b, wab, wrecv, vec_x)


def _in_proj(x, g_mix, shards, tm):
    t_len = x.shape[0]
    n_t = t_len // tm
    n_arr = len(shards)
    rows = [s.shape[0] for s in shards]
    width = 2 * rows[0]
    ax, ay = lax.axis_index("x"), lax.axis_index("y")
    order = jnp.stack([2 * cx + cy for cx, cy in [(ax, ay)] + _other_chips(ax, ay)]).astype(jnp.int32)

    def body(order_ref, x_ref, g_ref, *rest):
        shard_refs = rest[0:n_arr]
        u_ref, h_ref = rest[n_arr:n_arr + 2]
        fulls = rest[n_arr + 2:2 * n_arr + 2]
        h_s, wbuf, send_sems, recv_sems, local_sems, load_sem = rest[2 * n_arr + 2:]
        p = pl.program_id(0)
        i = pl.program_id(1)
        x_, y_, c = _place()
        me = (x_, y_, c)
        my_id = _block_id((x_, y_), c)
        sibling = (x_, y_, 1 - c)
        chips = _other_chips(x_, y_)

        def block(arr, blk):
            return fulls[arr].at[pl.ds(pl.multiple_of(blk * rows[arr], rows[arr]), rows[arr]), :]

        def copy(arr, k, blk, to, src=None):
            dst = block(arr, blk)
            return _remote_copy(dst if src is None else src, dst, send_sems.at[arr, k], recv_sems.at[arr, k], to)

        def local(arr):
            return pltpu.make_async_copy(shard_refs[arr], block(arr, my_id), local_sems.at[arr])

        def load_chip(chip):
            start = pl.multiple_of((2 * chip[0] + chip[1]) * width, width)
            cp = pltpu.make_async_copy(fulls[0].at[pl.ds(start, width), :], wbuf, load_sem.at[0])
            cp.start()
            cp.wait()

        @pl.when((p == 0) & (i == 0))
        def _():
            for arr in range(n_arr):
                local(arr).start()
                copy(arr, 0, my_id, sibling, shard_refs[arr]).start()
                for j in (0, 1):
                    copy(arr, 1 + j, my_id, (*chips[j], c), shard_refs[arr]).start()
            for arr in range(n_arr):
                local(arr).wait()
                copy(arr, 0, _block_id((x_, y_), 1 - c), me).wait_recv()
            load_chip((x_, y_))

        for j, chip in enumerate(chips):
            @pl.when((p == j + 1) & (i == 0))
            def _(j=j, chip=chip):
                for arr in range(n_arr):
                    copy(arr, 1 + j, _block_id(chip, c), me).wait_recv()
                    copy(arr, 4 + j, _block_id(chip, c), sibling).start()
                    if j == 0:
                        copy(arr, 3, my_id, (*chips[2], c), shard_refs[arr]).start()
                for arr in range(n_arr):
                    copy(arr, 4 + j, _block_id(chip, 1 - c), me).wait_recv()
                load_chip(chip)

        @pl.when((p == 3) & (i == n_t - 1))
        def _():
            for arr in range(n_arr):
                for k in range(4):
                    copy(arr, k, my_id, me, shard_refs[arr]).wait_send()
                for j, chip in enumerate(chips):
                    copy(arr, 4 + j, _block_id(chip, c), me).wait_send()

        tile = pl.ds(pl.multiple_of(i * tm, tm), tm)

        @pl.when(p == 0)
        def _():
            xv = x_ref[...]
            h = (xv * _rms(xv) * g_ref[...]).astype(BF16)
            h_ref[...] = h
            h_s[tile, :] = h

        u_ref[...] = _dot_nt(h_s[tile, :], wbuf[...])

    first_pass = lambda p, i, o: (jnp.where(p == 0, i, n_t - 1), 0)
    grid_spec = pltpu.PrefetchScalarGridSpec(
        num_scalar_prefetch=1, grid=(4, n_t),
        in_specs=[pl.BlockSpec((tm, D_MODEL), first_pass), pl.BlockSpec((1, D_MODEL), lambda p, i, o: (0, 0))]
        + [HBM_SPEC] * n_arr,
        out_specs=[pl.BlockSpec((tm, width), lambda p, i, o: (i, o[p])), pl.BlockSpec((tm, D_MODEL), first_pass)]
        + [HBM_SPEC] * n_arr,
        scratch_shapes=[pltpu.VMEM((t_len, D_MODEL), BF16), pltpu.VMEM((width, D_MODEL), BF16)]
        + _exchange_scratch(n_arr, 7) + [pltpu.SemaphoreType.DMA((n_arr,)), pltpu.SemaphoreType.DMA((1,))])
    return pl.pallas_call(
        body, grid_spec=grid_spec,
        out_shape=[jax.ShapeDtypeStruct((t_len, IN_COLS), F32), jax.ShapeDtypeStruct((t_len, D_MODEL), BF16)]
        + [jax.ShapeDtypeStruct((N_DEV * s.shape[0], s.shape[1]), s.dtype) for s in shards],
        compiler_params=_params(("arbitrary", "arbitrary"), 48), name="in_proj",
    )(order, x, g_mix, *shards)


def _conv3_chunk(u_ref, r, cv_prev, cw, row):
    gb = u_ref[pl.ds(r, SUB), OFF_GB:OFF_GB + CONV_WIDTH]
    gc = u_ref[pl.ds(r, SUB), OFF_GC:OFF_GC + CONV_WIDTH]
    v = u_ref[pl.ds(r, SUB), OFF_V:OFF_V + CONV_WIDTH]
    cv = gc * v
    cv_m1 = _down(cv, cv_prev, 1, row)
    cv_m2 = _down(cv, cv_prev, 2, row)
    cq = cw[2:3, :] * cv + cw[1:2, :] * cv_m1 + cw[0:1, :] * cv_m2
    return gb, gc, v, cv, cv_m1, cv_m2, cq


def _conv4_chunk(u_ref, r, xin_prev, rw, rb, row):
    xin = u_ref[pl.ds(r, SUB), OFF_XR:OFF_XR + LRU_WIDTH]
    m1 = _down(xin, xin_prev, 1, row)
    m2 = _down(xin, xin_prev, 2, row)
    m3 = _down(xin, xin_prev, 3, row)
    xr = rw[3:4, :] * xin + rw[2:3, :] * m1 + rw[1:2, :] * m2 + rw[0:1, :] * m3 + rb
    return xin, m1, m2, m3, xr


def _mixer_fwd(u, conv_w, rnn_conv_w, rnn_conv_b, wa, b_a, wx, b_x, lam, gnc, gnr, shards, tm):
    t_len = u.shape[0]
    n_steps = t_len // tm
    n_chunks = tm // SUB
    n_arr = len(shards)

    def body(u_ref, cw_ref, rw_ref, rb_ref, wa_ref, ba_ref, wx_ref, bx_ref, lam_ref, gnc_ref, gnr_ref, *rest):
        shard_refs = rest[0:n_arr]
        hs_ref, y_ref = rest[n_arr:n_arr + 2]
        fulls = rest[n_arr + 2:2 * n_arr + 2]
        (y_s, xr_s, pa_s, px_s, wabd, wxbd, cv_car, xin_car, h_car,
         send_sems, recv_sems, local_sems) = rest[2 * n_arr + 2:]
        _host_all_gather(pl.program_id(0), n_steps, shard_refs, fulls, send_sems, recv_sems, local_sems)

        @pl.when(pl.program_id(0) == 0)
        def _():
            cv_car[...] = jnp.zeros(cv_car.shape, F32)
            xin_car[...] = jnp.zeros(xin_car.shape, F32)
            h_car[...] = jnp.zeros(h_car.shape, F32)
            wabd[...] = _expand_heads(wa_ref[...])
            wxbd[...] = _expand_heads(wx_ref[...])

        row_c = lax.broadcasted_iota(jnp.int32, (SUB, CONV_WIDTH), 0)
        row_r = lax.broadcasted_iota(jnp.int32, (SUB, LRU_WIDTH), 0)
        cw = cw_ref[...]
        rw = rw_ref[...]
        rb = rb_ref[...]
        g_c = gnc_ref[...]
        g_r = gnr_ref[...]
        sp_c = LRU_C * _softplus_neg(lam_ref[...])

        def convs(i, carry):
            cv_prev, xin_prev = carry
            r = pl.multiple_of(i * SUB, SUB)
            gb, _, _, cv, _, _, cq = _conv3_chunk(u_ref, r, cv_prev, cw, row_c)
            y_c = gb * cq
            y_s[pl.ds(r, SUB), 0:CONV_WIDTH] = y_c * _rms(y_c) * g_c
            xin, _, _, _, xr = _conv4_chunk(u_ref, r, xin_prev, rw, rb, row_r)
            xr_s[pl.ds(r, SUB), :] = xr
            return cv, xin

        cv_last, xin_last = _chunk_loop(n_chunks, convs, (cv_car[...], xin_car[...]))
        cv_car[...] = cv_last
        xin_car[...] = xin_last

        xrb = xr_s[...].astype(BF16)
        pa_s[...] = _block_diag_apply(xrb, wabd) + ba_ref[...]
        px_s[...] = _block_diag_apply(xrb, wxbd) + bx_ref[...]

        def recur(i, h_prev):
            r = pl.multiple_of(i * SUB, SUB)
            xr = xr_s[pl.ds(r, SUB), :]
            _, ii, a, mult, _ = _lru_gates(pa_s[pl.ds(r, SUB), :], px_s[pl.ds(r, SUB), :], sp_c)
            a_cum, b_cum = _scan8_fwd(a, mult * ii * xr, row_r)
            h = a_cum * h_prev + b_cum
            hs_ref[pl.ds(r, SUB), :] = h
            ge, _ = _gelu(u_ref[pl.ds(r, SUB), OFF_G:OFF_G + LRU_WIDTH])
            y_r = h * ge
            y_s[pl.ds(r, SUB), CONV_WIDTH:MIX_WIDTH] = y_r * _rms(y_r) * g_r
            return h[SUB - 1:SUB, :]

        h_car[...] = _chunk_loop(n_chunks, recur, h_car[...])

        y_ref[...] = y_s[...].astype(BF16)

    row_tile = lambda w: pl.BlockSpec((tm, w), lambda i: (i, 0))
    whole = lambda a: pl.BlockSpec(a.shape, lambda i: (0,) * a.ndim)
    smalls = (conv_w, rnn_conv_w, rnn_conv_b, wa, b_a, wx, b_x, lam, gnc, gnr)
    return pl.pallas_call(
        body, grid=(n_steps,),
        in_specs=[row_tile(IN_COLS)] + [whole(a) for a in smalls] + [HBM_SPEC] * n_arr,
        out_specs=[row_tile(LRU_WIDTH), row_tile(MIX_WIDTH)] + [HBM_SPEC] * n_arr,
        out_shape=[jax.ShapeDtypeStruct((t_len, LRU_WIDTH), F32), jax.ShapeDtypeStruct((t_len, MIX_WIDTH), BF16)]
        + [jax.ShapeDtypeStruct((N_DEV,) + s.shape, BF16) for s in shards],
        scratch_shapes=[pltpu.VMEM((tm, MIX_WIDTH), F32), pltpu.VMEM((tm, LRU_WIDTH), F32),
                        pltpu.VMEM((tm, LRU_WIDTH), F32), pltpu.VMEM((tm, LRU_WIDTH), F32),
                        pltpu.VMEM((LRU_WIDTH, GROUP), BF16), pltpu.VMEM((LRU_WIDTH, GROUP), BF16),
                        pltpu.VMEM((SUB, CONV_WIDTH), F32), pltpu.VMEM((SUB, LRU_WIDTH), F32),
                        pltpu.VMEM((1, LRU_WIDTH), F32)]
        + _exchange_scratch(n_arr, 7) + [pltpu.SemaphoreType.DMA((n_arr,))],
        compiler_params=_params(("arbitrary",), 56), name="mixer_fwd",
    )(u, *smalls, *shards)


def _mlp_fwd_bwd(x, y, target, g_mlp, g_f, w_out, w1, w2, tm):
    t_len = x.shape[0]
    n_steps = t_len // tm
    n_blk, _, blk = w1.shape

    def body(x_ref, y_ref, tg_ref, gm_ref, gf_ref, wout_hbm, w1_hbm, w2_hbm,
             dx1_ref, h2_ref, dx2_ref, vec_ref, z_hbm, dpre_hbm,
             wout_s, w1_s, w2_s, rp_s, z_s, dp_s, sem, out_sem):
        step = pl.program_id(0)
        rows = pl.ds(pl.multiple_of(step * tm, tm), tm)
        z_out = pltpu.make_async_copy(z_s, z_hbm.at[rows, :], out_sem.at[0])
        dp_out = pltpu.make_async_copy(dp_s, dpre_hbm.at[rows, :], out_sem.at[1])

        @pl.when(step == 0)
        def _():
            loads = [pltpu.make_async_copy(src, dst, sem.at[k])
                     for k, (src, dst) in enumerate(((wout_hbm, wout_s), (w1_hbm, w1_s), (w2_hbm, w2_s)))]
            for cp in loads:
                cp.start()
            vec_ref[...] = jnp.zeros(vec_ref.shape, F32)
            for cp in loads:
                cp.wait()

        x1v = x_ref[...] + jnp.dot(y_ref[...], wout_s[...], preferred_element_type=F32)
        g_m = gm_ref[...]
        g_o = gf_ref[...]
        r2 = _rms(x1v)
        x1h = x1v * r2
        h2 = (x1h * g_m).astype(BF16)
        h2_ref[...] = h2
        x2 = x1v

        @pl.when(step > 0)
        def _():
            z_out.wait()

        for k in range(n_blk):
            rp = jnp.maximum(jnp.dot(h2, w1_s[k], preferred_element_type=F32), 0.0)
            rp_s[:, k * blk:(k + 1) * blk] = rp.astype(BF16)
            zb = (rp * rp).astype(BF16)
            z_s[:, k * blk:(k + 1) * blk] = zb
            x2 = x2 + jnp.dot(zb, w2_s[k * blk:(k + 1) * blk, :], preferred_element_type=F32)
        z_out.start()
        r3 = _rms(x2)
        x2h = x2 * r3
        err = x2h * g_o - tg_ref[...]
        dout = err * (1.0 / D_MODEL)
        vec_ref[ROW_LOSS:ROW_LOSS + 1, :] += (0.5 / D_MODEL) * jnp.sum(err * err, axis=0, keepdims=True)
        vec_ref[ROW_GF:ROW_GF + 1, :] += jnp.sum(dout * x2h, axis=0, keepdims=True)
        dx2 = _rms_bwd(dout, x2h, r3, g_o)
        dx2b = dx2.astype(BF16)
        dx2_ref[...] = dx2b
        dh2 = jnp.zeros((tm, D_MODEL), F32)

        @pl.when(step > 0)
        def _():
            dp_out.wait()

        for k in range(n_blk):
            dz = _dot_nt(dx2b, w2_s[k * blk:(k + 1) * blk, :])
            dpb = (dz * 2.0 * rp_s[:, k * blk:(k + 1) * blk].astype(F32)).astype(BF16)
            dp_s[:, k * blk:(k + 1) * blk] = dpb
            dh2 = dh2 + _dot_nt(dpb, w1_s[k])
        dp_out.start()
        vec_ref[ROW_GMLP:ROW_GMLP + 1, :] += jnp.sum(dh2 * x1h, axis=0, keepdims=True)
        dx1_ref[...] = dx2 + _rms_bwd(dh2, x1h, r2, g_m)

        @pl.when(step == n_steps - 1)
        def _():
            z_out.wait()
            dp_out.wait()

    row_tile = lambda w: pl.BlockSpec((tm, w), lambda i: (i, 0))
    vec_spec = pl.BlockSpec((1, D_MODEL), lambda i: (0, 0))
    outs = pl.pallas_call(
        body, grid=(n_steps,),
        in_specs=[row_tile(D_MODEL), row_tile(MIX_WIDTH), row_tile(D_MODEL), vec_spec, vec_spec,
                  HBM_SPEC, HBM_SPEC, HBM_SPEC],
        out_specs=[row_tile(D_MODEL), row_tile(D_MODEL), row_tile(D_MODEL),
                   pl.BlockSpec((SUB, D_MODEL), lambda i: (0, 0)), HBM_SPEC, HBM_SPEC],
        out_shape=[jax.ShapeDtypeStruct((t_len, D_MODEL), F32), jax.ShapeDtypeStruct((t_len, D_MODEL), BF16),
                   jax.ShapeDtypeStruct((t_len, D_MODEL), BF16), jax.ShapeDtypeStruct((SUB, D_MODEL), F32),
                   jax.ShapeDtypeStruct((t_len, D_FF), BF16), jax.ShapeDtypeStruct((t_len, D_FF), BF16)],
        scratch_shapes=[pltpu.VMEM(w_out.shape, BF16), pltpu.VMEM(w1.shape, BF16), pltpu.VMEM(w2.shape, BF16),
                        pltpu.VMEM((tm, D_FF), BF16), pltpu.VMEM((tm, D_FF), BF16), pltpu.VMEM((tm, D_FF), BF16),
                        pltpu.SemaphoreType.DMA((3,)), pltpu.SemaphoreType.DMA((2,))],
        compiler_params=_params(("arbitrary",), 58), name="mlp_fwd_bwd",
    )(x, y, target, g_mlp, g_f, w_out, w1, w2)
    dx1, h2, dx2, vec, z, dpre = outs
    return dx1, z, dpre, h2, dx2, vec


def _mixer_bwd(u, hs, dx1, conv_w, rnn_conv_w, rnn_conv_b, wa, b_a, wx, b_x, lam, gnc, gnr, w_out,
               chip_sums, g_wout, tm):
    t_len = u.shape[0]
    n_tiles = t_len // tm
    n_chunks = tm // SUB
    per_tile = tm // SUB
    n_sums = len(chip_sums)

    def body(u_ref, up_ref, hs_ref, hp_ref, dx1_ref, cw_ref, rw_ref, rb_ref, wa_ref, ba_ref, wx_ref, bx_ref,
             lam_ref, gnc_ref, gnr_ref, wout_ref, *rest):
        hsends = rest[0:n_sums]
        gwout_ref = rest[n_sums]
        du_ref, vec_ref, wab_ref = rest[n_sums + 1:n_sums + 4]
        hrecvs = rest[n_sums + 4:2 * n_sums + 4]
        sib_wout = rest[2 * n_sums + 4]
        (du_s, dy_s, xr_s, pa_s, px_s, dpa_s, dpx_s, dxr_s, wabd, wxbd, acc, dwa_acc, dwx_acc,
         a_car, dh_car, dcq_car, dxr_car, i_send, i_recv, d_send, d_recv) = rest[2 * n_sums + 5:]
        step = pl.program_id(0)
        _host_chip_exchange(step, n_tiles, hsends, hrecvs, i_send, i_recv)
        _host_pair_exchange(step, n_tiles, [gwout_ref], [sib_wout], d_send, d_recv)
        has_prev = (step < n_tiles - 1).astype(F32)

        @pl.when(step == 0)
        def _():
            acc[...] = jnp.zeros(acc.shape, F32)
            dwa_acc[...] = jnp.zeros(dwa_acc.shape, F32)
            dwx_acc[...] = jnp.zeros(dwx_acc.shape, F32)
            a_car[...] = jnp.ones(a_car.shape, F32)
            dh_car[...] = jnp.zeros(dh_car.shape, F32)
            dcq_car[...] = jnp.zeros(dcq_car.shape, F32)
            dxr_car[...] = jnp.zeros(dxr_car.shape, F32)
            wabd[...] = _expand_heads(wa_ref[...])
            wxbd[...] = _expand_heads(wx_ref[...])

        row_c = lax.broadcasted_iota(jnp.int32, (SUB, CONV_WIDTH), 0)
        row_r = lax.broadcasted_iota(jnp.int32, (SUB, LRU_WIDTH), 0)
        cw = cw_ref[...]
        rw = rw_ref[...]
        rb = rb_ref[...]
        g_c = gnc_ref[...]
        g_r = gnr_ref[...]
        sp_c = LRU_C * _softplus_neg(lam_ref[...])

        up = up_ref[...] * has_prev
        cv_before = up[:, OFF_GC:OFF_GC + CONV_WIDTH] * up[:, OFF_V:OFF_V + CONV_WIDTH]
        xin_before = up[:, OFF_XR:OFF_XR + LRU_WIDTH]
        hs_before = hp_ref[...] * has_prev

        dy_s[...] = _dot_nt(dx1_ref[...].astype(BF16), wout_ref[...])

        def conv4_fwd(i, xin_prev):
            r = pl.multiple_of(i * SUB, SUB)
            xin, _, _, _, xr = _conv4_chunk(u_ref, r, xin_prev, rw, rb, row_r)
            xr_s[pl.ds(r, SUB), :] = xr
            return xin

        _chunk_loop(n_chunks, conv4_fwd, xin_before)
        xrb = xr_s[...].astype(BF16)
        pa_s[...] = _block_diag_apply(xrb, wabd) + ba_ref[...]
        px_s[...] = _block_diag_apply(xrb, wxbd) + bx_ref[...]

        def recur_bwd(j, carry):
            a_later, dh_later = carry
            i = n_chunks - 1 - j
            r = pl.multiple_of(i * SUB, SUB)
            rp = pl.multiple_of(jnp.maximum(i - 1, 0) * SUB, SUB)
            xr = xr_s[pl.ds(r, SUB), :]
            hs_c = hs_ref[pl.ds(r, SUB), :]
            hs_prev = jnp.where(i == 0, hs_before, hs_ref[pl.ds(rp, SUB), :])
            h_m1 = _down(hs_c, hs_prev, 1, row_r)
            ra, ii, a, mult, inv_mult = _lru_gates(pa_s[pl.ds(r, SUB), :], px_s[pl.ds(r, SUB), :], sp_c)
            ge, dge = _gelu(u_ref[pl.ds(r, SUB), OFF_G:OFF_G + LRU_WIDTH])
            y_r = hs_c * ge
            rr = _rms(y_r)
            yhat = y_r * rr
            dyn = dy_s[pl.ds(r, SUB), CONV_WIDTH:MIX_WIDTH]
            acc[ACC_GNR] += dyn * yhat
            dy_r = _rms_bwd(dyn, yhat, rr, g_r)
            du_s[pl.ds(r, SUB), OFF_G:OFF_G + LRU_WIDTH] = dy_r * hs_c * dge
            a_cum, d_cum = _scan8_rev(_up(a, a_later, 1, row_r), dy_r * ge, row_r)
            dh = a_cum * dh_later + d_cum
            dmult = dh * ii * xr
            dii = dh * mult * xr
            dxr_s[pl.ds(r, SUB), :] = dh * mult * ii
            dla = dh * h_m1 * a - dmult * a * a * inv_mult
            acc[ACC_SP] += -dla * ra
            dpa = -dla * sp_c * ra * (1.0 - ra)
            dpx = dii * ii * (1.0 - ii)
            acc[ACC_BA] += dpa
            acc[ACC_BX] += dpx
            dpa_s[pl.ds(r, SUB), :] = dpa
            dpx_s[pl.ds(r, SUB), :] = dpx
            return a, dh[0:1, :]

        a_first, dh_first = _chunk_loop(n_chunks, recur_bwd, (a_car[...], dh_car[...]))
        a_car[...] = a_first
        dh_car[...] = dh_first

        dpab = dpa_s[...].astype(BF16)
        dpxb = dpx_s[...].astype(BF16)
        dxr_s[...] += _block_diag_apply_t(dpab, wabd) + _block_diag_apply_t(dpxb, wxbd)
        for g in range(LRU_WIDTH // GROUP):
            cols = slice(g * GROUP, (g + 1) * GROUP)
            dwa_acc[cols, :] += _dot_tn(xrb[:, cols], dpab[:, cols])
            dwx_acc[cols, :] += _dot_tn(xrb[:, cols], dpxb[:, cols])

        def convs_bwd(j, carry):
            dcq_later, dxr_later = carry
            i = n_chunks - 1 - j
            r = pl.multiple_of(i * SUB, SUB)
            rp = pl.multiple_of(jnp.maximum(i - 1, 0) * SUB, SUB)
            cv_prev = jnp.where(i == 0, cv_before,
                                u_ref[pl.ds(rp, SUB), OFF_GC:OFF_GC + CONV_WIDTH]
                                * u_ref[pl.ds(rp, SUB), OFF_V:OFF_V + CONV_WIDTH])
            gb, gc, v, cv, cv_m1, cv_m2, cq = _conv3_chunk(u_ref, r, cv_prev, cw, row_c)
            y_c = gb * cq
            rc = _rms(y_c)
            yhat = y_c * rc
            dyn = dy_s[pl.ds(r, SUB), 0:CONV_WIDTH]
            acc[ACC_GNC, :, 0:CONV_WIDTH] += dyn * yhat
            dy_c = _rms_bwd(dyn, yhat, rc, g_c)
            dcq = dy_c * gb
            dcv = (cw[2:3, :] * dcq + cw[1:2, :] * _up(dcq, dcq_later, 1, row_c)
                   + cw[0:1, :] * _up(dcq, dcq_later, 2, row_c))
            acc[ACC_CW + 2, :, 0:CONV_WIDTH] += dcq * cv
            acc[ACC_CW + 1, :, 0:CONV_WIDTH] += dcq * cv_m1
            acc[ACC_CW + 0, :, 0:CONV_WIDTH] += dcq * cv_m2
            du_s[pl.ds(r, SUB), OFF_GB:OFF_GB + CONV_WIDTH] = dy_c * cq
            du_s[pl.ds(r, SUB), OFF_GC:OFF_GC + CONV_WIDTH] = dcv * v
            du_s[pl.ds(r, SUB), OFF_V:OFF_V + CONV_WIDTH] = dcv * gc

            xin_prev = jnp.where(i == 0, xin_before, u_ref[pl.ds(rp, SUB), OFF_XR:OFF_XR + LRU_WIDTH])
            xin, m1, m2, m3, _ = _conv4_chunk(u_ref, r, xin_prev, rw, rb, row_r)
            dxr = dxr_s[pl.ds(r, SUB), :]
            du_s[pl.ds(r, SUB), OFF_XR:OFF_XR + LRU_WIDTH] = (
                rw[3:4, :] * dxr + rw[2:3, :] * _up(dxr, dxr_later, 1, row_r)
                + rw[1:2, :] * _up(dxr, dxr_later, 2, row_r) + rw[0:1, :] * _up(dxr, dxr_later, 3, row_r))
            acc[ACC_RW + 3] += dxr * xin
            acc[ACC_RW + 2] += dxr * m1
            acc[ACC_RW + 1] += dxr * m2
            acc[ACC_RW + 0] += dxr * m3
            acc[ACC_BR] += dxr
            return dcq, dxr

        dcq_first, dxr_first = _chunk_loop(n_chunks, convs_bwd, (dcq_car[...], dxr_car[...]))
        dcq_car[...] = dcq_first
        dxr_car[...] = dxr_first

        du_ref[...] = du_s[...].astype(BF16)

        @pl.when(step == n_tiles - 1)
        def _():
            vec_ref[...] = jnp.zeros(vec_ref.shape, F32)
            rows = {ACC_GNC: ROW_GNC, ACC_GNR: ROW_GNR, ACC_BR: ROW_BR, ACC_BA: ROW_BA, ACC_BX: ROW_BX}
            for k in range(3):
                rows[ACC_CW + k] = ROW_CW + k
            for k in range(4):
                rows[ACC_RW + k] = ROW_RW + k
            for slot, out_row in rows.items():
                o = out_row - ROW_GNC
                vec_ref[o:o + 1, :] = jnp.sum(acc[slot], axis=0, keepdims=True)
            lam_v = lam_ref[...]
            dsp = jnp.sum(acc[ACC_SP], axis=0, keepdims=True)
            o = ROW_LAM - ROW_GNC
            vec_ref[o:o + 1, :] = -dsp * LRU_C / (1.0 + jnp.exp(lam_v))
            wab_ref[0:LRU_WIDTH, :] = _fold_heads(dwa_acc[...])
            wab_ref[LRU_WIDTH:2 * LRU_WIDTH, :] = _fold_heads(dwx_acc[...])

    rev = lambda w: pl.BlockSpec((tm, w), lambda s: (n_tiles - 1 - s, 0))
    before = lambda w: pl.BlockSpec((SUB, w), lambda s: (jnp.maximum((n_tiles - 1 - s) * per_tile - 1, 0), 0))
    whole = lambda a: pl.BlockSpec(a.shape, lambda s: (0,) * a.ndim)
    smalls = (conv_w, rnn_conv_w, rnn_conv_b, wa, b_a, wx, b_x, lam, gnc, gnr, w_out)
    full = lambda w: pltpu.VMEM((tm, w), F32)
    return pl.pallas_call(
        body, grid=(n_tiles,),
        in_specs=[rev(IN_COLS), before(IN_COLS), rev(LRU_WIDTH), before(LRU_WIDTH), rev(D_MODEL)]
        + [whole(a) for a in smalls] + [HBM_SPEC] * (n_sums + 1),
        out_specs=[rev(IN_COLS), pl.BlockSpec((16, D_MODEL), lambda s: (0, 0)),
                   pl.BlockSpec((2 * LRU_WIDTH, HEAD_DIM), lambda s: (0, 0))] + [HBM_SPEC] * (n_sums + 1),
        out_shape=[jax.ShapeDtypeStruct((t_len, IN_COLS), BF16), jax.ShapeDtypeStruct((16, D_MODEL), F32),
                   jax.ShapeDtypeStruct((2 * LRU_WIDTH, HEAD_DIM), F32)]
        + [jax.ShapeDtypeStruct(s.shape, BF16) for s in chip_sums]
        + [jax.ShapeDtypeStruct((4,) + g_wout.shape[1:], BF16)],
        scratch_shapes=[full(IN_COLS), full(MIX_WIDTH), full(LRU_WIDTH), full(LRU_WIDTH), full(LRU_WIDTH),
                        full(LRU_WIDTH), full(LRU_WIDTH), full(LRU_WIDTH),
                        pltpu.VMEM((LRU_WIDTH, GROUP), BF16), pltpu.VMEM((LRU_WIDTH, GROUP), BF16),
                        pltpu.VMEM((N_ACC, SUB, LRU_WIDTH), F32),
                        pltpu.VMEM((LRU_WIDTH, GROUP), F32), pltpu.VMEM((LRU_WIDTH, GROUP), F32),
                        pltpu.VMEM((SUB, LRU_WIDTH), F32), pltpu.VMEM((1, LRU_WIDTH), F32),
                        pltpu.VMEM((SUB, CONV_WIDTH), F32), pltpu.VMEM((SUB, LRU_WIDTH), F32)]
        + _exchange_scratch(n_sums, 3) + _exchange_scratch(1, 4),
        compiler_params=_params(("arbitrary",), 56), name="mixer_bwd",
    )(u, u, hs, hs, dx1, *smalls, *chip_sums, g_wout)


def _in_proj_bwd(du, dx1, x, g_mix, win_t, tm, chip_sums, g_own):
    t_len = x.shape[0]
    n_steps = t_len // tm

    def body(du_ref, dx1_ref, x_ref, g_ref, w_ref, hs_ref, gown_ref,
             dx_ref, vec_ref, landed_ref, sib_ref, i_send, i_recv, d_send, d_recv):
        step = pl.program_id(0)
        _host_chip_exchange(step, n_steps, [hs_ref], [landed_ref], i_send, i_recv)
        _host_half_exchange(step, n_steps, gown_ref, sib_ref, d_send, d_recv)

        @pl.when(step == 0)
        def _():
            vec_ref[...] = jnp.zeros(vec_ref.shape, F32)

        dh = jnp.dot(du_ref[...], w_ref[...], preferred_element_type=F32)
        xv = x_ref[...]
        r1 = _rms(xv)
        xh = xv * r1
        vec_ref[0:1, :] += jnp.sum(dh * xh, axis=0, keepdims=True)
        dx_ref[...] = dx1_ref[...] + _rms_bwd(dh, xh, r1, g_ref[...])

    row_tile = lambda w: pl.BlockSpec((tm, w), lambda i: (i, 0))
    half_shape = (g_own.shape[0], g_own.shape[1] // 2, g_own.shape[2])
    return pl.pallas_call(
        body, grid=(n_steps,),
        in_specs=[row_tile(IN_COLS), row_tile(D_MODEL), row_tile(D_MODEL), pl.BlockSpec((1, D_MODEL), lambda i: (0, 0)),
                  pl.BlockSpec((IN_COLS, D_MODEL), lambda i: (0, 0))] + [HBM_SPEC] * 2,
        out_specs=[row_tile(D_MODEL), pl.BlockSpec((SUB, D_MODEL), lambda i: (0, 0))] + [HBM_SPEC] * 2,
        out_shape=[jax.ShapeDtypeStruct((t_len, D_MODEL), F32), jax.ShapeDtypeStruct((SUB, D_MODEL), F32),
                   jax.ShapeDtypeStruct(chip_sums.shape, BF16), jax.ShapeDtypeStruct(half_shape, BF16)],
        scratch_shapes=_exchange_scratch(1, 3) + [pltpu.SemaphoreType.DMA((1,)), pltpu.SemaphoreType.DMA((1,))],
        compiler_params=_params(("arbitrary",), 56), name="in_proj_bwd",
    )(du, dx1, x, g_mix, win_t, chip_sums, g_own)


def _tn_weight_grad(a, b, tk, name, pair=(), col_blocks=1):
    t_len, m = a.shape
    n = b.shape[1]
    n_steps = t_len // tk
    sent = tuple(pair)
    n_sent = len(sent)

    def body(a_ref, b_ref, *rest):
        srcs = rest[0:n_sent]
        o_ref = rest[n_sent]
        dsts = rest[n_sent + 1:2 * n_sent + 1]
        acc = rest[2 * n_sent + 1]
        sems = rest[2 * n_sent + 2:]
        j = pl.program_id(0)
        if pair:
            _host_pair_exchange(j, n_steps, srcs, dsts, *sems)

        @pl.when(j == 0)
        def _():
            acc[...] = jnp.zeros(acc.shape, F32)

        acc[...] += _dot_tn(a_ref[...].astype(BF16), b_ref[...].astype(BF16))

        @pl.when(j == n_steps - 1)
        def _():
            if col_blocks == 1:
                o_ref[...] = acc[...].astype(BF16)
            else:
                for k in range(col_blocks):
                    o_ref[k] = acc[:, k * nb:(k + 1) * nb].astype(BF16)

    nb = n // col_blocks
    out_dims = (m, n) if col_blocks == 1 else (col_blocks, m, nb)
    landed = [jax.ShapeDtypeStruct((4,) + g.shape[1:], BF16) for g in pair]
    scratch = [pltpu.VMEM((m, n), F32)]
    if n_sent:
        scratch += _exchange_scratch(n_sent, 4)
    return pl.pallas_call(
        body, grid=(n_steps,),
        in_specs=[pl.BlockSpec((tk, m), lambda j: (j, 0)), pl.BlockSpec((tk, n), lambda j: (j, 0))]
        + [HBM_SPEC] * n_sent,
        out_specs=[pl.BlockSpec(out_dims, lambda j: (0,) * len(out_dims))] + [HBM_SPEC] * n_sent,
        out_shape=[jax.ShapeDtypeStruct(out_dims, BF16)] + landed,
        scratch_shapes=scratch,
        compiler_params=_params(("arbitrary",), 56), name=name,
    )(a, b, *sent)


def _w_in_grad_part(du, h, tk, name, chip_ids, chip=(), halves=None, small=None):
    t_len = du.shape[0]
    n_t = t_len // tk
    n_q = chip_ids.shape[0]
    width = 2 * (IN_COLS // N_DEV)
    n_steps = n_q * n_t
    n_chip = len(chip)
    sent = tuple(chip) + (() if halves is None else (halves,)) + (() if small is None else tuple(small))
    n_sent = len(sent)

    def body(ids_ref, a_ref, b_ref, *rest):
        srcs = rest[0:n_sent]
        o_ref = rest[n_sent]
        dsts = rest[n_sent + 1:2 * n_sent + 1]
        acc = rest[2 * n_sent + 1]
        sems = list(rest[2 * n_sent + 2:])
        j = pl.program_id(1)
        step = pl.program_id(0) * n_t + j
        if chip:
            _host_chip_exchange(step, n_steps, srcs[0:n_chip], dsts[0:n_chip], sems.pop(0), sems.pop(0))
        if halves is not None:
            _host_half_exchange(step, n_steps, srcs[n_chip], dsts[n_chip], sems.pop(0), sems.pop(0))
        if small is not None:
            _host_small_exchange(step, n_steps, *srcs[n_sent - 3:], *dsts[n_sent - 3:], *sems)

        @pl.when(j == 0)
        def _():
            acc[...] = jnp.zeros(acc.shape, F32)

        acc[...] += _dot_tn(a_ref[...], b_ref[...])

        @pl.when(j == n_t - 1)
        def _():
            o_ref[0] = acc[...].astype(BF16)

    landed = [jax.ShapeDtypeStruct(s.shape, BF16) for s in chip]
    scratch = [pltpu.VMEM((width, D_MODEL), F32)]
    if chip:
        scratch += _exchange_scratch(len(chip), 3)
    if halves is not None:
        landed.append(jax.ShapeDtypeStruct((halves.shape[0], halves.shape[1] // 2, halves.shape[2]), BF16))
        scratch += [pltpu.SemaphoreType.DMA((halves.shape[0],)), pltpu.SemaphoreType.DMA((halves.shape[0],))]
    if small is not None:
        vec_m, vec_b, wab = small
        landed += [jax.ShapeDtypeStruct((N_DEV,) + vec_m.shape, F32), jax.ShapeDtypeStruct((N_DEV,) + vec_b.shape, F32),
                   jax.ShapeDtypeStruct((N_DEV, wab.shape[0] // N_DEV, wab.shape[1]), F32)]
        scratch += _exchange_scratch(3, N_DEV) + [pltpu.SemaphoreType.DMA((2,))]
    grid_spec = pltpu.PrefetchScalarGridSpec(
        num_scalar_prefetch=1, grid=(n_q, n_t),
        in_specs=[pl.BlockSpec((tk, width), lambda q, j, ids: (j, ids[q])),
                  pl.BlockSpec((tk, D_MODEL), lambda q, j, ids: (j, 0))] + [HBM_SPEC] * n_sent,
        out_specs=[pl.BlockSpec((1, width, D_MODEL), lambda q, j, ids: (q, 0, 0))] + [HBM_SPEC] * n_sent,
        scratch_shapes=scratch)
    return pl.pallas_call(
        body, grid_spec=grid_spec, out_shape=[jax.ShapeDtypeStruct((n_q, width, D_MODEL), BF16)] + landed,
        compiler_params=_params(("arbitrary", "arbitrary"), 40), name=name,
    )(chip_ids, du, h, *sent)


def _adamw(w, g, m, v):
    m = ADAM_B1 * m + (1.0 - ADAM_B1) * g
    v = ADAM_B2 * v + (1.0 - ADAM_B2) * (g * g)
    delta = -ADAM_LR * ((m / BC1) / (jnp.sqrt(v / BC2) + ADAM_EPS) + ADAM_WD * w)
    return delta, m, v


def _update_sharded(g, landed, w, m, v, rows_blk, name):
    rows, cols = w.shape

    def body(g_ref, l_ref, w_ref, m_ref, v_ref, og, od, om, ov):
        gv = g_ref[...]
        for j in range(3):
            gv = gv + l_ref[j].astype(F32)
        delta, mn, vn = _adamw(w_ref[...], gv, m_ref[...], v_ref[...])
        og[...] = gv
        od[...] = delta
        om[...] = mn
        ov[...] = vn

    blk = pl.BlockSpec((rows_blk, cols), lambda i: (i, 0))
    shape = pltpu.HBM((rows, cols), F32)
    return pl.pallas_call(
        body, grid=(rows // rows_blk,),
        in_specs=[blk, pl.BlockSpec((3, rows_blk, cols), lambda i: (0, i, 0)), blk, blk, blk],
        out_specs=[blk] * 4, out_shape=[shape] * 4,
        compiler_params=_params(("arbitrary",), 32), name=name,
    )(*_in_hbm(g, landed, w, m, v))


def _update_w_in(g_own, sib_own, landed, w, m, v, core, rows_blk):
    rows, cols = w.shape
    pad_cols = -(-cols // 128) * 128

    def body(core_ref, g_ref, s_ref, l_ref, w_ref, m_ref, v_ref, og, od, om, ov, padbuf, turned):
        gt = g_ref[0, 0].astype(F32) + s_ref[0].astype(F32)
        for j in range(3):
            gt = gt + l_ref[j].astype(F32)
        padbuf[...] = jnp.zeros(padbuf.shape, F32)
        padbuf[0:cols, :] = gt
        turned[...] = padbuf[...].T
        gv = turned[:, 0:cols]
        delta, mn, vn = _adamw(w_ref[...], gv, m_ref[...], v_ref[...])
        og[...] = gv
        od[...] = delta
        om[...] = mn
        ov[...] = vn

    blk = pl.BlockSpec((rows_blk, cols), lambda i, cr: (i, 0))
    grid_spec = pltpu.PrefetchScalarGridSpec(
        num_scalar_prefetch=1, grid=(rows // rows_blk,),
        in_specs=[pl.BlockSpec((1, 1, cols, rows_blk), lambda i, cr: (0, cr[0], 0, i)),
                  pl.BlockSpec((1, cols, rows_blk), lambda i, cr: (0, 0, i)),
                  pl.BlockSpec((3, cols, rows_blk), lambda i, cr: (0, 0, i)), blk, blk, blk],
        out_specs=[blk] * 4,
        scratch_shapes=[pltpu.VMEM((pad_cols, rows_blk), F32), pltpu.VMEM((rows_blk, pad_cols), F32)])
    return pl.pallas_call(
        body, grid_spec=grid_spec, out_shape=[pltpu.HBM((rows, cols), F32)] * 4,
        compiler_params=_params(("arbitrary",), 32), name="update_w_in",
    )(core, *_in_hbm(g_own.reshape(1, 2, cols, rows), sib_own, landed, w, m, v))


def _update_small(vsum, wsum, g_cw, g_rw, weights, moments_m, moments_v):
    n = len(weights)

    def body(*refs):
        vs, ws, gcw, grw = refs[0:4]
        w_refs = refs[4:4 + n]
        m_refs = refs[4 + n:4 + 2 * n]
        v_refs = refs[4 + 2 * n:4 + 3 * n]
        outs = refs[4 + 3 * n:]
        loss_ref = outs[0]
        loss_ref[...] = jnp.sum(vs[ROW_LOSS:ROW_LOSS + 1, :], axis=1, keepdims=True)
        grads = [
            vs[ROW_GMIX:ROW_GMIX + 1, :], gcw[...], grw[...], vs[ROW_BR:ROW_BR + 1, :],
            ws[0:LRU_WIDTH, :], vs[ROW_BA:ROW_BA + 1, :], ws[LRU_WIDTH:2 * LRU_WIDTH, :], vs[ROW_BX:ROW_BX + 1, :],
            vs[ROW_LAM:ROW_LAM + 1, :], vs[ROW_GNC:ROW_GNC + 1, 0:CONV_WIDTH], vs[ROW_GNR:ROW_GNR + 1, :],
            vs[ROW_GMLP:ROW_GMLP + 1, :], vs[ROW_GF:ROW_GF + 1, :],
        ]
        for k in range(n):
            gk = grads[k]
            delta, mn, vn = _adamw(w_refs[k][...], gk, m_refs[k][...], v_refs[k][...])
            outs[1 + 4 * k][...] = gk
            outs[2 + 4 * k][...] = delta
            outs[3 + 4 * k][...] = mn
            outs[4 + 4 * k][...] = vn

    whole = lambda a: pl.BlockSpec(a.shape, lambda i: (0,) * len(a.shape))
    out_shape = [jax.ShapeDtypeStruct((1, 1), F32)]
    for w in weights:
        out_shape += [jax.ShapeDtypeStruct(w.shape, F32)] * 4
    args = (vsum, wsum, g_cw, g_rw, *weights, *moments_m, *moments_v)
    return pl.pallas_call(
        body, grid=(1,), out_shape=out_shape, in_specs=[whole(a) for a in args], out_specs=[whole(s) for s in out_shape],
        compiler_params=_params(("arbitrary",), 32), name="update_small",
    )(*args)


def kernel(x, norm_mix_g, w_in, conv_w, rnn_conv_w, rnn_conv_b, w_a, b_a, w_x, b_x, lru_lambda, g_norm_conv, g_norm_rnn, w_out, norm_mlp_g, w_mlp_in, w_mlp_out, final_norm_g, loss_target, m_norm_mix_g, m_w_in, m_conv_w, m_rnn_conv_w, m_rnn_conv_b, m_w_a, m_b_a, m_w_x, m_b_x, m_lru_lambda, m_g_norm_conv, m_g_norm_rnn, m_w_out, m_norm_mlp_g, m_w_mlp_in, m_w_mlp_out, m_final_norm_g, v_norm_mix_g, v_w_in, v_conv_w, v_rnn_conv_w, v_rnn_conv_b, v_w_a, v_b_a, v_w_x, v_b_x, v_lru_lambda, v_g_norm_conv, v_g_norm_rnn, v_w_out, v_norm_mlp_g, v_w_mlp_in, v_w_mlp_out, v_final_norm_g):
    t_len = x.shape[1]
    my_id = 4 * lax.axis_index("x") + 2 * lax.axis_index("y") + lax.axis_index("c")
    tm = min(256, t_len)
    tb = min(512, t_len)
    tk = min(512, t_len)

    xs = x.reshape(t_len, D_MODEL)
    tgt = loss_target.reshape(t_len, D_MODEL)
    flat = lambda a: a.reshape(a.shape[-2:]) if a.ndim == 3 else a.reshape(1, -1)
    heads = lambda a: a.reshape(LRU_WIDTH, HEAD_DIM)

    win_shard, wout_shard, w1_shard, w2_shard, cp_shard = _prep_shards(
        flat(w_in), flat(w_out), flat(w_mlp_in), flat(w_mlp_out), flat(conv_w), flat(rnn_conv_w))

    u, h, win_t, cp_full = _in_proj(xs, flat(norm_mix_g), (win_shard, cp_shard), tb)
    cpack = cp_full.reshape(N_DEV, 8, 128)
    conv_full = jnp.transpose(cpack[:, 0:3, 0:64], (1, 0, 2)).reshape(3, CONV_WIDTH)
    rnn_full = jnp.transpose(cpack[:, 3:7, :], (1, 0, 2)).reshape(4, LRU_WIDTH)
    mixer_small = (conv_full, rnn_full, flat(rnn_conv_b), heads(w_a), flat(b_a), heads(w_x), flat(b_x),
                   flat(lru_lambda), flat(g_norm_conv), flat(g_norm_rnn))
    hs, y, w1_blk, w2_blk, wout_blk = _mixer_fwd(u, *mixer_small, (w1_shard, w2_shard, wout_shard), tm)
    wout_f = wout_blk.reshape(MIX_WIDTH, D_MODEL)
    dx1, z, dpre, h2, dx2, vec_m = _mlp_fwd_bwd(xs, y, tgt, flat(norm_mlp_g), flat(final_norm_g), wout_f, w1_blk,
                                                w2_blk.reshape(D_FF, D_MODEL), tb)
    (g_w1,) = _tn_weight_grad(h2, dpre, tk, "w_mlp_in_grad", col_blocks=N_DEV)
    (g_w2,) = _tn_weight_grad(z, dx2, tk, "w_mlp_out_grad")
    g_w2 = g_w2.reshape(N_DEV, D_FF // N_DEV, D_MODEL)
    g_wout, sib_w1, sib_w2 = _tn_weight_grad(y, dx1, tk, "w_out_grad", pair=(g_w1, g_w2))
    g_wout = g_wout.reshape(N_DEV, MIX_WIDTH // N_DEV, D_MODEL)
    hsend_w1, own_w1 = _pair_sum(g_w1, sib_w1, "pair_sum_w_mlp_in")
    hsend_w2, own_w2 = _pair_sum(g_w2, sib_w2, "pair_sum_w_mlp_out")
    du, vec_b, wab, landed_w1, landed_w2, sib_wout = _mixer_bwd(
        u, hs, dx1, *mixer_small, wout_f, (hsend_w1, hsend_w2), g_wout, tm)
    hsend_wout, own_wout = _pair_sum(g_wout, sib_wout, "pair_sum_w_out")
    ax, ay, ac = lax.axis_index("x"), lax.axis_index("y"), lax.axis_index("c")
    chip_ids = jnp.stack([2 * cx + cy for cx, cy in [(ax, ay)] + _other_chips(ax, ay)]).astype(jnp.int32)
    core = jnp.reshape(ac, (1,)).astype(jnp.int32)
    tw = min(1024, t_len)
    g_others, landed_wout, vrecv_m, vrecv_b, wrecv = _w_in_grad_part(
        du, h, tw, "w_in_grad_others", chip_ids[1:4], chip=(hsend_wout,), small=(vec_m, vec_b, wab))
    g_own, sib_others = _w_in_grad_part(du, h, tw, "w_in_grad_own", chip_ids[0:1], halves=g_others)
    hsend_win = _pair_sum_parts(g_others, sib_others, core)
    grad_x, vec_x, landed_win, sib_own = _in_proj_bwd(du, dx1, xs, flat(norm_mix_g), win_t, tm, hsend_win, g_own)

    vsum, wsum = _final_small(vrecv_m, vrecv_b, wab, wrecv, vec_x)

    up_win = _update_w_in(g_own, sib_own, landed_win, flat(w_in), flat(m_w_in), flat(v_w_in), core, 256)
    up_wout = _update_sharded(own_wout, landed_wout, flat(w_out), flat(m_w_out), flat(v_w_out), 96, "update_w_out")
    up_w1 = _update_sharded(own_w1, landed_w1, flat(w_mlp_in), flat(m_w_mlp_in), flat(v_w_mlp_in), 256,
                            "update_w_mlp_in")
    up_w2 = _update_sharded(own_w2, landed_w2, flat(w_mlp_out), flat(m_w_mlp_out), flat(v_w_mlp_out), 256,
                            "update_w_mlp_out")

    g_cw = lax.dynamic_slice(vsum, (ROW_CW, 64 * my_id), (3, 64))
    g_rw = lax.dynamic_slice(vsum, (ROW_RW, 128 * my_id), (4, 128))
    small_w = (norm_mix_g, conv_w, rnn_conv_w, rnn_conv_b, w_a, b_a, w_x, b_x, lru_lambda, g_norm_conv, g_norm_rnn,
               norm_mlp_g, final_norm_g)
    small_m = (m_norm_mix_g, m_conv_w, m_rnn_conv_w, m_rnn_conv_b, m_w_a, m_b_a, m_w_x, m_b_x, m_lru_lambda,
               m_g_norm_conv, m_g_norm_rnn, m_norm_mlp_g, m_final_norm_g)
    small_v = (v_norm_mix_g, v_conv_w, v_rnn_conv_w, v_rnn_conv_b, v_w_a, v_b_a, v_w_x, v_b_x, v_lru_lambda,
               v_g_norm_conv, v_g_norm_rnn, v_norm_mlp_g, v_final_norm_g)
    is_heads = (False, False, False, False, True, False, True, False, False, False, False, False, False)
    as2d = lambda arrs: [heads(a) if hd else flat(a) for a, hd in zip(arrs, is_heads)]
    small_out = _update_small(vsum, wsum, g_cw, g_rw, as2d(small_w), as2d(small_m), as2d(small_v))
    loss = small_out[0].reshape(())

    names = ["norm_mix_g", "w_in", "conv_w", "rnn_conv_w", "rnn_conv_b", "w_a", "b_a", "w_x", "b_x", "lru_lambda",
             "g_norm_conv", "g_norm_rnn", "w_out", "norm_mlp_g", "w_mlp_in", "w_mlp_out", "final_norm_g"]
    originals = dict(zip(names, (norm_mix_g, w_in, conv_w, rnn_conv_w, rnn_conv_b, w_a, b_a, w_x, b_x, lru_lambda,
                                 g_norm_conv, g_norm_rnn, w_out, norm_mlp_g, w_mlp_in, w_mlp_out, final_norm_g)))
    results = {"w_in": up_win, "w_out": up_wout, "w_mlp_in": up_w1, "w_mlp_out": up_w2}
    small_names = ["norm_mix_g", "conv_w", "rnn_conv_w", "rnn_conv_b", "w_a", "b_a", "w_x", "b_x", "lru_lambda",
                   "g_norm_conv", "g_norm_rnn", "norm_mlp_g", "final_norm_g"]
    for k, nm in enumerate(small_names):
        results[nm] = small_out[1 + 4 * k:5 + 4 * k]
    out = [loss, grad_x.reshape(x.shape)]
    for kind in range(4):
        out += [results[nm][kind].reshape(originals[nm].shape) for nm in names]
    return tuple(out)
```

```python
import functools

import jax
import jax.numpy as jnp
from jax import lax
from jax.experimental import pallas as pl
from jax.experimental.pallas import tpu as pltpu

F32 = jnp.float32
BF16 = jnp.bfloat16

D_MODEL = 1024
HEAD_DIM = 64
CONV_WIDTH = 512
LRU_WIDTH = 1024
MIX_WIDTH = CONV_WIDTH + LRU_WIDTH
IN_COLS = 3 * CONV_WIDTH + 2 * LRU_WIDTH
D_FF = 4 * D_MODEL
GROUP = 256
EPS = 1e-6
LRU_C = 8.0
N_DEV = 8
SUB = 8

OFF_GB, OFF_GC, OFF_V, OFF_XR, OFF_G = 0, 512, 1024, 1536, 2560

ADAM_LR, ADAM_B1, ADAM_B2, ADAM_EPS, ADAM_WD, ADAM_STEP = 0.001, 0.9, 0.999, 1e-08, 0.01, 10
BC1 = 1.0 - ADAM_B1 ** ADAM_STEP
BC2 = 1.0 - ADAM_B2 ** ADAM_STEP

MIB = 1024 * 1024
MESH = pl.DeviceIdType.MESH

VEC_ROWS = 32
ROW_GF, ROW_GMLP, ROW_LOSS = 0, 1, 2
ROW_GNC, ROW_GNR, ROW_BR, ROW_BA, ROW_BX, ROW_LAM, ROW_CW, ROW_RW = 8, 9, 10, 11, 12, 13, 14, 17
ROW_GMIX = 24
ACC_GNC, ACC_GNR, ACC_BR, ACC_BA, ACC_BX, ACC_SP, ACC_CW, ACC_RW, N_ACC = 0, 1, 2, 3, 4, 5, 6, 9, 13


def _params(semantics=None, vmem_mib=48):
    return pltpu.CompilerParams(dimension_semantics=semantics, vmem_limit_bytes=vmem_mib * MIB)


def _rms(x):
    return lax.rsqrt(jnp.mean(x * x, axis=-1, keepdims=True) + EPS)


def _rms_bwd(dy, xhat, r, g):
    dyh = dy * g
    return r * (dyh - xhat * jnp.mean(dyh * xhat, axis=-1, keepdims=True))


def _sigmoid(x):
    return 0.5 + 0.5 * jnp.tanh(0.5 * x)


def _gelu(x):
    c0, c1 = 0.7978845608028654, 0.044715
    x2 = x * x
    t = jnp.tanh(x * (c0 + (c0 * c1) * x2))
    half = 0.5 + 0.5 * t
    ge = x * half
    dge = half + (0.5 * x) * (1.0 - t * t) * (c0 + (3.0 * c0 * c1) * x2)
    return ge, dge


def _softplus_neg(lam):
    z = -lam
    e = jnp.exp(-jnp.abs(z))
    return jnp.maximum(z, 0.0) + jnp.where(e < 1e-4, e * (1.0 - 0.5 * e), jnp.log(1.0 + e))


def _lru_gates(pa, px, sp_c):
    ra = _sigmoid(pa)
    ii = _sigmoid(px)
    la = -ra * sp_c
    a = jnp.exp(la)
    x2 = 2.0 * la
    series = -x2 * (1.0 + x2 * (0.5 + x2 * (1.0 / 6.0 + x2 * (1.0 / 24.0))))
    m2 = jnp.where(x2 > -0.01, series, 1.0 - a * a)
    inv_mult = lax.rsqrt(m2)
    mult = jnp.where(m2 > 0.0, m2 * inv_mult, 0.0)
    return ra, ii, a, mult, inv_mult


def _down(cur, prev, s, row):
    return jnp.where(row >= s, pltpu.roll(cur, s, 0), pltpu.roll(prev, s, 0))


def _up(cur, nxt, s, row):
    return jnp.where(row < SUB - s, pltpu.roll(cur, SUB - s, 0), pltpu.roll(nxt, SUB - s, 0))


def _scan8_fwd(a, b, row):
    for s in (1, 2, 4):
        m = row >= s
        a_sh = pltpu.roll(a, s, 0)
        b_sh = pltpu.roll(b, s, 0)
        b = jnp.where(m, a * b_sh + b, b)
        a = jnp.where(m, a * a_sh, a)
    return a, b


def _scan8_rev(a, b, row):
    for s in (1, 2, 4):
        m = row < SUB - s
        a_sh = pltpu.roll(a, SUB - s, 0)
        b_sh = pltpu.roll(b, SUB - s, 0)
        b = jnp.where(m, a * b_sh + b, b)
        a = jnp.where(m, a * a_sh, a)
    return a, b


def _group_mask(shape):
    r = lax.broadcasted_iota(jnp.int32, shape, 0)
    c = lax.broadcasted_iota(jnp.int32, shape, 1)
    return ((r % GROUP) // HEAD_DIM) == (c // HEAD_DIM)


def _expand_heads(w):
    j = lax.broadcasted_iota(jnp.int32, (HEAD_DIM, GROUP), 0)
    c = lax.broadcasted_iota(jnp.int32, (HEAD_DIM, GROUP), 1)
    spread = (c % HEAD_DIM == j).astype(BF16)
    e = jnp.dot(w.astype(BF16), spread, preferred_element_type=F32)
    return jnp.where(_group_mask(e.shape), e, 0.0).astype(BF16)


def _fold_heads(p):
    p = jnp.where(_group_mask(p.shape), p, 0.0)
    c = lax.broadcasted_iota(jnp.int32, (GROUP, HEAD_DIM), 0)
    j = lax.broadcasted_iota(jnp.int32, (GROUP, HEAD_DIM), 1)
    fold = (c % HEAD_DIM == j).astype(BF16)
    hi = p.astype(BF16)
    rest = p - hi.astype(F32)
    mid = rest.astype(BF16)
    lo = (rest - mid.astype(F32)).astype(BF16)
    dot = functools.partial(jnp.dot, preferred_element_type=F32)
    return dot(hi, fold) + dot(mid, fold) + dot(lo, fold)


def _block_diag_apply(xb, wbd_ref):
    parts = [jnp.dot(xb[:, g * GROUP:(g + 1) * GROUP], wbd_ref[g * GROUP:(g + 1) * GROUP, :],
                     preferred_element_type=F32) for g in range(LRU_WIDTH // GROUP)]
    return jnp.concatenate(parts, axis=1)


def _block_diag_apply_t(db, wbd_ref):
    parts = [lax.dot_general(db[:, g * GROUP:(g + 1) * GROUP], wbd_ref[g * GROUP:(g + 1) * GROUP, :],
                             (((1,), (1,)), ((), ())), preferred_element_type=F32)
             for g in range(LRU_WIDTH // GROUP)]
    return jnp.concatenate(parts, axis=1)


def _dot_nt(a, b):
    return lax.dot_general(a, b, (((1,), (1,)), ((), ())), preferred_element_type=F32)


def _dot_tn(a, b):
    return lax.dot_general(a, b, (((0,), (0,)), ((), ())), preferred_element_type=F32)


CHUNKS_IN_FLIGHT = 4

def _chunk_loop(n_chunks, chunk, init):
    def body(k, carry):
        for j in range(CHUNKS_IN_FLIGHT):
            carry = chunk(k * CHUNKS_IN_FLIGHT + j, carry)
        return carry

    return lax.fori_loop(0, n_chunks // CHUNKS_IN_FLIGHT, body, init)


def _place():
    x, y, c = lax.axis_index("x"), lax.axis_index("y"), lax.axis_index("c")
    return x, y, c


def _block_id(chip, core):
    return 4 * chip[0] + 2 * chip[1] + core


def _other_chips(x, y):
    return [(1 - x, y), (x, 1 - y), (1 - x, 1 - y)]


def _remote_copy(src, dst, send_sem, recv_sem, to):
    return pltpu.make_async_remote_copy(src_ref=src, dst_ref=dst, send_sem=send_sem, recv_sem=recv_sem,
                                        device_id=to, device_id_type=MESH)


HBM_SPEC = pl.BlockSpec(memory_space=pl.ANY)


def _in_hbm(*arrays):
    return [pltpu.with_memory_space_constraint(a, pltpu.HBM) for a in arrays]


def _prep_shards(w_in, w_out, w_mlp_in, w_mlp_out, conv_w, rnn_conv_w):
    n_in = w_in.shape[1]

    def body(win_ref, wout_ref, w1_ref, w2_ref, cw_ref, rw_ref, o_win, o_wout, o_w1, o_w2, o_cp, padbuf):
        padbuf[...] = jnp.zeros(padbuf.shape, F32)
        padbuf[:, 0:n_in] = win_ref[...]
        o_win[...] = padbuf[...].T[0:n_in, :].astype(BF16)
        o_wout[...] = wout_ref[...].astype(BF16)
        o_w1[...] = w1_ref[...].astype(BF16)
        o_w2[...] = w2_ref[...].astype(BF16)
        o_cp[...] = jnp.zeros(o_cp.shape, F32)
        o_cp[0:3, 0:64] = cw_ref[...]
        o_cp[3:7, :] = rw_ref[...]

    whole = lambda shape: pl.BlockSpec(shape, lambda i: (0,) * len(shape))
    args = (w_in, w_out, w_mlp_in, w_mlp_out, conv_w, rnn_conv_w)
    shapes = [((n_in, D_MODEL), BF16), (w_out.shape, BF16), (w_mlp_in.shape, BF16), (w_mlp_out.shape, BF16),
              ((8, 128), F32)]
    return pl.pallas_call(
        body, grid=(1,), out_shape=[jax.ShapeDtypeStruct(s, d) for s, d in shapes],
        in_specs=[whole(a.shape) for a in args], out_specs=[whole(s) for s, _ in shapes],
        scratch_shapes=[pltpu.VMEM((D_MODEL, 512), F32)],
        compiler_params=_params(("arbitrary",), 40), name="prep_shards",
    )(*args)


def _host_all_gather(step, n_steps, shards, fulls, send_sems, recv_sems, local_sems):
    x, y, c = _place()
    me = (x, y, c)
    my_id = _block_id((x, y), c)
    sibling = (x, y, 1 - c)
    chips = _other_chips(x, y)
    n_arr = len(shards)

    def copy(arr, k, block, to, src=None):
        dst = fulls[arr].at[block]
        return _remote_copy(dst if src is None else src, dst, send_sems.at[arr, k], recv_sems.at[arr, k], to)

    def local(arr):
        return pltpu.make_async_copy(shards[arr], fulls[arr].at[my_id], local_sems.at[arr])

    @pl.when(step == 0)
    def _():
        for arr in range(n_arr):
            local(arr).start()
            copy(arr, 0, my_id, sibling, shards[arr]).start()
            for j, chip in enumerate(chips):
                copy(arr, 1 + j, my_id, (*chip, c), shards[arr]).start()

    @pl.when(step == max(n_steps - 2, 0))
    def _():
        for j, chip in enumerate(chips):
            for arr in range(n_arr):
                copy(arr, 1 + j, _block_id(chip, c), me).wait_recv()
                copy(arr, 4 + j, _block_id(chip, c), sibling).start()

    @pl.when(step == n_steps - 1)
    def _():
        for arr in range(n_arr):
            copy(arr, 0, _block_id((x, y), 1 - c), me).wait_recv()
            for j, chip in enumerate(chips):
                copy(arr, 4 + j, _block_id(chip, 1 - c), me).wait_recv()
            for k in range(4):
                copy(arr, k, my_id, me, shards[arr]).wait_send()
            for j, chip in enumerate(chips):
                copy(arr, 4 + j, _block_id(chip, c), me).wait_send()
            local(arr).wait()


def _host_pair_exchange(step, n_steps, gs, sibs, send_sems, recv_sems):
    x, y, c = _place()
    sibling = (x, y, 1 - c)
    chips = [(x, y)] + _other_chips(x, y)

    def d2d(arr, q):
        return _remote_copy(gs[arr].at[_block_id(chips[q], 1 - c)], sibs[arr].at[q],
                            send_sems.at[arr, q], recv_sems.at[arr, q], sibling)

    @pl.when(step == 0)
    def _():
        for arr in range(len(gs)):
            for q in (1, 2, 3, 0):
                d2d(arr, q).start()

    @pl.when(step == n_steps - 1)
    def _():
        for arr in range(len(gs)):
            for q in range(4):
                d2d(arr, q).wait()


def _host_chip_exchange(step, n_steps, hsends, hrecvs, send_sems, recv_sems):
    x, y, c = _place()
    chips = _other_chips(x, y)

    def ici(arr, j):
        return _remote_copy(hsends[arr].at[j], hrecvs[arr].at[j], send_sems.at[arr, j], recv_sems.at[arr, j],
                            (*chips[j], c))

    @pl.when(step == 0)
    def _():
        for arr in range(len(hsends)):
            for j in range(3):
                ici(arr, j).start()

    @pl.when(step == n_steps - 1)
    def _():
        for arr in range(len(hsends)):
            for j in range(3):
                ici(arr, j).wait()


def _host_half_exchange(step, n_steps, parts, sibs, send_sems, recv_sems):
    x, y, c = _place()
    n_q, rows2, _ = parts.shape
    half = rows2 // 2

    def d2d(q):
        src = parts.at[q, pl.ds(pl.multiple_of((1 - c) * half, 16), half), :]
        return _remote_copy(src, sibs.at[q], send_sems.at[q], recv_sems.at[q], (x, y, 1 - c))

    @pl.when(step == 0)
    def _():
        for q in range(n_q):
            d2d(q).start()

    @pl.when(step == n_steps - 1)
    def _():
        for q in range(n_q):
            d2d(q).wait()


def _peer(x, y, c, k):
    return (x ^ ((k >> 2) & 1), y ^ ((k >> 1) & 1), c ^ (k & 1))


def _host_small_exchange(step, n_steps, vec_m, vec_b, wab, vrecv_m, vrecv_b, wrecv, send_sems, recv_sems, local_sems):
    x, y, c = _place()
    my_id = _block_id((x, y), c)
    wrows = wab.shape[0] // N_DEV

    def copies(k):
        to = _peer(x, y, c, k)
        block = wab.at[pl.ds(pl.multiple_of(_block_id(to[0:2], to[2]) * wrows, SUB), wrows), :]
        return [_remote_copy(vec_m, vrecv_m.at[my_id], send_sems.at[0, k], recv_sems.at[0, k], to),
                _remote_copy(vec_b, vrecv_b.at[my_id], send_sems.at[1, k], recv_sems.at[1, k], to),
                _remote_copy(block, wrecv.at[k], send_sems.at[2, k], recv_sems.at[2, k], to)]

    mine = [pltpu.make_async_copy(vec_m, vrecv_m.at[my_id], local_sems.at[0]),
            pltpu.make_async_copy(vec_b, vrecv_b.at[my_id], local_sems.at[1])]

    @pl.when(step == 0)
    def _():
        for cp in mine:
            cp.start()
        for k in range(1, N_DEV):
            for cp in copies(k):
                cp.start()

    @pl.when(step == n_steps - 1)
    def _():
        for k in range(1, N_DEV):
            for cp in copies(k):
                cp.wait()
        for cp in mine:
            cp.wait()


def _pair_sum_parts(parts, sibs, core):
    n_q, rows2, cols = parts.shape
    half = rows2 // 2

    def body(core_ref, g_ref, s_ref, o_ref):
        o_ref[0] = (g_ref[0, 0].astype(F32) + s_ref[0].astype(F32)).astype(BF16)

    block = (1, half, cols)
    grid_spec = pltpu.PrefetchScalarGridSpec(
        num_scalar_prefetch=1, grid=(n_q,),
        in_specs=[pl.BlockSpec((1, 1, half, cols), lambda q, cr: (q, cr[0], 0, 0)),
                  pl.BlockSpec(block, lambda q, cr: (q, 0, 0))],
        out_specs=pl.BlockSpec(block, lambda q, cr: (q, 0, 0)))
    return pl.pallas_call(
        body, grid_spec=grid_spec, out_shape=pltpu.HBM((n_q, half, cols), BF16),
        compiler_params=_params(("arbitrary",), 32), name="pair_sum_w_in",
    )(core, *_in_hbm(parts.reshape(n_q, 2, half, cols), sibs))


def _pair_sum(g, sib, name):
    _, rows, cols = g.shape
    x, y, c = _place()
    slots = jnp.stack([_block_id(chip, c) for chip in [(x, y)] + _other_chips(x, y)]).astype(jnp.int32)

    def body(slots_ref, g_ref, sib_ref, hs_ref, own_ref):
        q = pl.program_id(0)
        both = g_ref[0].astype(F32) + sib_ref[0].astype(F32)

        @pl.when(q == 0)
        def _():
            own_ref[...] = both

        @pl.when(q > 0)
        def _():
            hs_ref[0] = both.astype(BF16)

    block = (1, rows, cols)
    grid_spec = pltpu.PrefetchScalarGridSpec(
        num_scalar_prefetch=1, grid=(4,),
        in_specs=[pl.BlockSpec(block, lambda q, s: (s[q], 0, 0)), pl.BlockSpec(block, lambda q, s: (q, 0, 0))],
        out_specs=[pl.BlockSpec(block, lambda q, s: (jnp.maximum(q - 1, 0), 0, 0)),
                   pl.BlockSpec((rows, cols), lambda q, s: (0, 0))])
    return pl.pallas_call(
        body, grid_spec=grid_spec,
        out_shape=(pltpu.HBM((3, rows, cols), BF16), pltpu.HBM((rows, cols), F32)),
        compiler_params=_params(("arbitrary",), 32), name=name,
    )(slots, *_in_hbm(g, sib))


def _exchange_scratch(n_arr, n_copies):
    return [pltpu.SemaphoreType.DMA((n_arr, n_copies)), pltpu.SemaphoreType.DMA((n_arr, n_copies))]


def _final_small(vrecv_m, vrecv_b, wab, wrecv, vec_x):
    wrows = wab.shape[0] // N_DEV

    def body(vm_ref, vb_ref, w_ref, wr_ref, vx_ref, o_vec, o_w, xrecv, wred, x_send, x_recv, b_send, b_recv):
        x, y, c = _place()
        my_id = _block_id((x, y), c)
        my_rows = pl.ds(pl.multiple_of(my_id * wrows, SUB), wrows)

        def xcopy(k):
            return _remote_copy(vx_ref, xrecv.at[my_id], x_send.at[k], x_recv.at[k], _peer(x, y, c, k))

        def bcopy(k):
            return _remote_copy(wred, o_w.at[my_rows, :], b_send.at[k], b_recv.at[k], _peer(x, y, c, k))

        xrecv[my_id] = vx_ref[...]
        for k in range(1, N_DEV):
            xcopy(k).start()
        red = w_ref[my_rows, :]
        for k in range(1, N_DEV):
            red = red + wr_ref[k]
        wred[...] = red
        o_w[my_rows, :] = red
        for k in range(1, N_DEV):
            bcopy(k).start()
        for k in range(1, N_DEV):
            xcopy(k).wait_recv()
        for rows, ref in ((slice(0, 8), vm_ref), (slice(8, 24), vb_ref), (slice(24, 32), xrecv)):
            tot = ref[0]
            for s in range(1, N_DEV):
                tot = tot + ref[s]
            o_vec[rows, :] = tot
        for k in range(1, N_DEV):
            bcopy(k).wait_recv()
        for k in range(1, N_DEV):
            xcopy(k).wait_send()
            bcopy(k).wait_send()

    vm = pl.BlockSpec(memory_space=pltpu.VMEM)
    dma8 = pltpu.SemaphoreType.DMA((N_DEV,))
    return pl.pallas_call(
        body, out_shape=(jax.ShapeDtypeStruct((VEC_ROWS, D_MODEL), F32), jax.ShapeDtypeStruct(wab.shape, F32)),
        in_specs=[vm] * 5, out_specs=[vm] * 2,
        scratch_shapes=[pltpu.VMEM((N_DEV, SUB, D_MODEL), F32), pltpu.VMEM((wrows, HEAD_DIM), F32),
                        dma8, dma8, dma8, dma8],
        compiler_params=_params(vmem_mib=32), name="final_small",
    )(vrecv_m, vrecv_b, wab, wrecv, vec_x)


def _in_proj(x, g_mix, shards, tm):
    t_len = x.shape[0]
    n_t = t_len // tm
    n_arr = len(shards)
    rows = [s.shape[0] for s in shards]
    width = 2 * rows[0]
    ax, ay = lax.axis_index("x"), lax.axis_index("y")
    order = jnp.stack([2 * cx + cy for cx, cy in [(ax, ay)] + _other_chips(ax, ay)]).astype(jnp.int32)

    def body(order_ref, x_ref, g_ref, *rest):
        shard_refs = rest[0:n_arr]
        u_ref, h_ref = rest[n_arr:n_arr + 2]
        fulls = rest[n_arr + 2:2 * n_arr + 2]
        h_s, wbuf, send_sems, recv_sems, local_sems, load_sem = rest[2 * n_arr + 2:]
        p = pl.program_id(0)
        i = pl.program_id(1)
        x_, y_, c = _place()
        me = (x_, y_, c)
        my_id = _block_id((x_, y_), c)
        sibling = (x_, y_, 1 - c)
        chips = _other_chips(x_, y_)

        def block(arr, blk):
            return fulls[arr].at[pl.ds(pl.multiple_of(blk * rows[arr], rows[arr]), rows[arr]), :]

        def copy(arr, k, blk, to, src=None):
            dst = block(arr, blk)
            return _remote_copy(dst if src is None else src, dst, send_sems.at[arr, k], recv_sems.at[arr, k], to)

        def local(arr):
            return pltpu.make_async_copy(shard_refs[arr], block(arr, my_id), local_sems.at[arr])

        def load_chip(chip):
            start = pl.multiple_of((2 * chip[0] + chip[1]) * width, width)
            cp = pltpu.make_async_copy(fulls[0].at[pl.ds(start, width), :], wbuf, load_sem.at[0])
            cp.start()
            cp.wait()

        @pl.when((p == 0) & (i == 0))
        def _():
            for arr in range(n_arr):
                local(arr).start()
                copy(arr, 0, my_id, sibling, shard_refs[arr]).start()
                for j in (0, 1):
                    copy(arr, 1 + j, my_id, (*chips[j], c), shard_refs[arr]).start()
            for arr in range(n_arr):
                local(arr).wait()
                copy(arr, 0, _block_id((x_, y_), 1 - c), me).wait_recv()
            load_chip((x_, y_))

        for j, chip in enumerate(chips):
            @pl.when((p == j + 1) & (i == 0))
            def _(j=j, chip=chip):
                for arr in range(n_arr):
                    copy(arr, 1 + j, _block_id(chip, c), me).wait_recv()
                    copy(arr, 4 + j, _block_id(chip, c), sibling).start()
                    if j == 0:
                        copy(arr, 3, my_id, (*chips[2], c), shard_refs[arr]).start()
                for arr in range(n_arr):
                    copy(arr, 4 + j, _block_id(chip, 1 - c), me).wait_recv()
                load_chip(chip)

        @pl.when((p == 3) & (i == n_t - 1))
        def _():
            for arr in range(n_arr):
                for k in range(4):
                    copy(arr, k, my_id, me, shard_refs[arr]).wait_send()
                for j, chip in enumerate(chips):
                    copy(arr, 4 + j, _block_id(chip, c), me).wait_send()

        tile = pl.ds(pl.multiple_of(i * tm, tm), tm)

        @pl.when(p == 0)
        def _():
            xv = x_ref[...]
            h = (xv * _rms(xv) * g_ref[...]).astype(BF16)
            h_ref[...] = h
            h_s[tile, :] = h

        u_ref[...] = _dot_nt(h_s[tile, :], wbuf[...])

    first_pass = lambda p, i, o: (jnp.where(p == 0, i, n_t - 1), 0)
    grid_spec = pltpu.PrefetchScalarGridSpec(
        num_scalar_prefetch=1, grid=(4, n_t),
        in_specs=[pl.BlockSpec((tm, D_MODEL), first_pass), pl.BlockSpec((1, D_MODEL), lambda p, i, o: (0, 0))]
        + [HBM_SPEC] * n_arr,
        out_specs=[pl.BlockSpec((tm, width), lambda p, i, o: (i, o[p])), pl.BlockSpec((tm, D_MODEL), first_pass)]
        + [HBM_SPEC] * n_arr,
        scratch_shapes=[pltpu.VMEM((t_len, D_MODEL), BF16), pltpu.VMEM((width, D_MODEL), BF16)]
        + _exchange_scratch(n_arr, 7) + [pltpu.SemaphoreType.DMA((n_arr,)), pltpu.SemaphoreType.DMA((1,))])
    return pl.pallas_call(
        body, grid_spec=grid_spec,
        out_shape=[jax.ShapeDtypeStruct((t_len, IN_COLS), F32), jax.ShapeDtypeStruct((t_len, D_MODEL), BF16)]
        + [jax.ShapeDtypeStruct((N_DEV * s.shape[0], s.shape[1]), s.dtype) for s in shards],
        compiler_params=_params(("arbitrary", "arbitrary"), 48), name="in_proj",
    )(order, x, g_mix, *shards)


def _conv3_chunk(u_ref, r, cv_prev, cw, row):
    gb = u_ref[pl.ds(r, SUB), OFF_GB:OFF_GB + CONV_WIDTH]
    gc = u_ref[pl.ds(r, SUB), OFF_GC:OFF_GC + CONV_WIDTH]
    v = u_ref[pl.ds(r, SUB), OFF_V:OFF_V + CONV_WIDTH]
    cv = gc * v
    cv_m1 = _down(cv, cv_prev, 1, row)
    cv_m2 = _down(cv, cv_prev, 2, row)
    cq = cw[2:3, :] * cv + cw[1:2, :] * cv_m1 + cw[0:1, :] * cv_m2
    return gb, gc, v, cv, cv_m1, cv_m2, cq


def _conv4_chunk(u_ref, r, xin_prev, rw, rb, row):
    xin = u_ref[pl.ds(r, SUB), OFF_XR:OFF_XR + LRU_WIDTH]
    m1 = _down(xin, xin_prev, 1, row)
    m2 = _down(xin, xin_prev, 2, row)
    m3 = _down(xin, xin_prev, 3, row)
    xr = rw[3:4, :] * xin + rw[2:3, :] * m1 + rw[1:2, :] * m2 + rw[0:1, :] * m3 + rb
    return xin, m1, m2, m3, xr


def _mixer_fwd(u, conv_w, rnn_conv_w, rnn_conv_b, wa, b_a, wx, b_x, lam, gnc, gnr, shards, tm):
    t_len = u.shape[0]
    n_steps = t_len // tm
    n_chunks = tm // SUB
    n_arr = len(shards)

    def body(u_ref, cw_ref, rw_ref, rb_ref, wa_ref, ba_ref, wx_ref, bx_ref, lam_ref, gnc_ref, gnr_ref, *rest):
        shard_refs = rest[0:n_arr]
        hs_ref, y_ref, xr_s, ra_ref, ii_ref, mult_ref = rest[n_arr:n_arr + 6]
        fulls = rest[n_arr + 6:2 * n_arr + 6]
        (y_s, pa_s, px_s, wabd, wxbd, cv_car, xin_car, h_car,
         send_sems, recv_sems, local_sems) = rest[2 * n_arr + 6:]
        _host_all_gather(pl.program_id(0), n_steps, shard_refs, fulls, send_sems, recv_sems, local_sems)

        @pl.when(pl.program_id(0) == 0)
        def _():
            cv_car[...] = jnp.zeros(cv_car.shape, F32)
            xin_car[...] = jnp.zeros(xin_car.shape, F32)
            h_car[...] = jnp.zeros(h_car.shape, F32)
            wabd[...] = _expand_heads(wa_ref[...])
            wxbd[...] = _expand_heads(wx_ref[...])

        row_c = lax.broadcasted_iota(jnp.int32, (SUB, CONV_WIDTH), 0)
        row_r = lax.broadcasted_iota(jnp.int32, (SUB, LRU_WIDTH), 0)
        cw = cw_ref[...]
        rw = rw_ref[...]
        rb = rb_ref[...]
        g_c = gnc_ref[...]
        g_r = gnr_ref[...]
        sp_c = LRU_C * _softplus_neg(lam_ref[...])

        def convs(i, carry):
            cv_prev, xin_prev = carry
            r = pl.multiple_of(i * SUB, SUB)
            gb, _, _, cv, _, _, cq = _conv3_chunk(u_ref, r, cv_prev, cw, row_c)
            y_c = gb * cq
            y_s[pl.ds(r, SUB), 0:CONV_WIDTH] = y_c * _rms(y_c) * g_c
            xin, _, _, _, xr = _conv4_chunk(u_ref, r, xin_prev, rw, rb, row_r)
            xr_s[pl.ds(r, SUB), :] = xr
            return cv, xin

        cv_last, xin_last = _chunk_loop(n_chunks, convs, (cv_car[...], xin_car[...]))
        cv_car[...] = cv_last
        xin_car[...] = xin_last

        xrb = xr_s[...].astype(BF16)
        pa_s[...] = _block_diag_apply(xrb, wabd) + ba_ref[...]
        px_s[...] = _block_diag_apply(xrb, wxbd) + bx_ref[...]

        def recur(i, h_prev):
            r = pl.multiple_of(i * SUB, SUB)
            xr = xr_s[pl.ds(r, SUB), :]
            ra, ii, a, mult, _ = _lru_gates(pa_s[pl.ds(r, SUB), :], px_s[pl.ds(r, SUB), :], sp_c)
            ra_ref[pl.ds(r, SUB), :] = ra
            ii_ref[pl.ds(r, SUB), :] = ii
            mult_ref[pl.ds(r, SUB), :] = mult
            a_cum, b_cum = _scan8_fwd(a, mult * ii * xr, row_r)
            h = a_cum * h_prev + b_cum
            hs_ref[pl.ds(r, SUB), :] = h
            ge, _ = _gelu(u_ref[pl.ds(r, SUB), OFF_G:OFF_G + LRU_WIDTH])
            y_r = h * ge
            y_s[pl.ds(r, SUB), CONV_WIDTH:MIX_WIDTH] = y_r * _rms(y_r) * g_r
            return h[SUB - 1:SUB, :]

        h_car[...] = _chunk_loop(n_chunks, recur, h_car[...])

        y_ref[...] = y_s[...].astype(BF16)

    row_tile = lambda w: pl.BlockSpec((tm, w), lambda i: (i, 0))
    whole = lambda a: pl.BlockSpec(a.shape, lambda i: (0,) * a.ndim)
    smalls = (conv_w, rnn_conv_w, rnn_conv_b, wa, b_a, wx, b_x, lam, gnc, gnr)
    return pl.pallas_call(
        body, grid=(n_steps,),
        in_specs=[row_tile(IN_COLS)] + [whole(a) for a in smalls] + [HBM_SPEC] * n_arr,
        out_specs=[row_tile(LRU_WIDTH), row_tile(MIX_WIDTH)] + [row_tile(LRU_WIDTH)] * 4 + [HBM_SPEC] * n_arr,
        out_shape=[jax.ShapeDtypeStruct((t_len, LRU_WIDTH), F32), jax.ShapeDtypeStruct((t_len, MIX_WIDTH), BF16)]
        + [jax.ShapeDtypeStruct((t_len, LRU_WIDTH), F32)] * 4
        + [jax.ShapeDtypeStruct((N_DEV,) + s.shape, BF16) for s in shards],
        scratch_shapes=[pltpu.VMEM((tm, MIX_WIDTH), F32),
                        pltpu.VMEM((tm, LRU_WIDTH), F32), pltpu.VMEM((tm, LRU_WIDTH), F32),
                        pltpu.VMEM((LRU_WIDTH, GROUP), BF16), pltpu.VMEM((LRU_WIDTH, GROUP), BF16),
                        pltpu.VMEM((SUB, CONV_WIDTH), F32), pltpu.VMEM((SUB, LRU_WIDTH), F32),
                        pltpu.VMEM((1, LRU_WIDTH), F32)]
        + _exchange_scratch(n_arr, 7) + [pltpu.SemaphoreType.DMA((n_arr,))],
        compiler_params=_params(("arbitrary",), 56), name="mixer_fwd",
    )(u, *smalls, *shards)


def _mlp_fwd_bwd(x, y, target, g_mlp, g_f, w_out, w1, w2, tm):
    t_len = x.shape[0]
    n_steps = t_len // tm
    n_blk, _, blk = w1.shape

    def body(x_ref, y_ref, tg_ref, gm_ref, gf_ref, wout_hbm, w1_hbm, w2_hbm,
             dx1_ref, h2_ref, dx2_ref, vec_ref, z_hbm, dpre_hbm,
             wout_s, w1_s, w2_s, rp_s, z_s, dp_s, sem, out_sem):
        step = pl.program_id(0)
        rows = pl.ds(pl.multiple_of(step * tm, tm), tm)
        z_out = pltpu.make_async_copy(z_s, z_hbm.at[rows, :], out_sem.at[0])
        dp_out = pltpu.make_async_copy(dp_s, dpre_hbm.at[rows, :], out_sem.at[1])

        @pl.when(step == 0)
        def _():
            loads = [pltpu.make_async_copy(src, dst, sem.at[k])
                     for k, (src, dst) in enumerate(((wout_hbm, wout_s), (w1_hbm, w1_s), (w2_hbm, w2_s)))]
            for cp in loads:
                cp.start()
            vec_ref[...] = jnp.zeros(vec_ref.shape, F32)
            for cp in loads:
                cp.wait()

        x1v = x_ref[...] + jnp.dot(y_ref[...], wout_s[...], preferred_element_type=F32)
        g_m = gm_ref[...]
        g_o = gf_ref[...]
        r2 = _rms(x1v)
        x1h = x1v * r2
        h2 = (x1h * g_m).astype(BF16)
        h2_ref[...] = h2
        x2 = x1v

        @pl.when(step > 0)
        def _():
            z_out.wait()

        for k in range(n_blk):
            rp = jnp.maximum(jnp.dot(h2, w1_s[k], preferred_element_type=F32), 0.0)
            rp_s[:, k * blk:(k + 1) * blk] = rp.astype(BF16)
            zb = (rp * rp).astype(BF16)
            z_s[:, k * blk:(k + 1) * blk] = zb
            x2 = x2 + jnp.dot(zb, w2_s[k * blk:(k + 1) * blk, :], preferred_element_type=F32)
        z_out.start()
        r3 = _rms(x2)
        x2h = x2 * r3
        err = x2h * g_o - tg_ref[...]
        dout = err * (1.0 / D_MODEL)
        vec_ref[ROW_LOSS:ROW_LOSS + 1, :] += (0.5 / D_MODEL) * jnp.sum(err * err, axis=0, keepdims=True)
        vec_ref[ROW_GF:ROW_GF + 1, :] += jnp.sum(dout * x2h, axis=0, keepdims=True)
        dx2 = _rms_bwd(dout, x2h, r3, g_o)
        dx2b = dx2.astype(BF16)
        dx2_ref[...] = dx2b
        dh2 = jnp.zeros((tm, D_MODEL), F32)

        @pl.when(step > 0)
        def _():
            dp_out.wait()

        for k in range(n_blk):
            dz = _dot_nt(dx2b, w2_s[k * blk:(k + 1) * blk, :])
            dpb = (dz * 2.0 * rp_s[:, k * blk:(k + 1) * blk].astype(F32)).astype(BF16)
            dp_s[:, k * blk:(k + 1) * blk] = dpb
            dh2 = dh2 + _dot_nt(dpb, w1_s[k])
        dp_out.start()
        vec_ref[ROW_GMLP:ROW_GMLP + 1, :] += jnp.sum(dh2 * x1h, axis=0, keepdims=True)
        dx1_ref[...] = dx2 + _rms_bwd(dh2, x1h, r2, g_m)

        @pl.when(step == n_steps - 1)
        def _():
            z_out.wait()
            dp_out.wait()

    row_tile = lambda w: pl.BlockSpec((tm, w), lambda i: (i, 0))
    vec_spec = pl.BlockSpec((1, D_MODEL), lambda i: (0, 0))
    outs = pl.pallas_call(
        body, grid=(n_steps,),
        in_specs=[row_tile(D_MODEL), row_tile(MIX_WIDTH), row_tile(D_MODEL), vec_spec, vec_spec,
                  HBM_SPEC, HBM_SPEC, HBM_SPEC],
        out_specs=[row_tile(D_MODEL), row_tile(D_MODEL), row_tile(D_MODEL),
                   pl.BlockSpec((SUB, D_MODEL), lambda i: (0, 0)), HBM_SPEC, HBM_SPEC],
        out_shape=[jax.ShapeDtypeStruct((t_len, D_MODEL), F32), jax.ShapeDtypeStruct((t_len, D_MODEL), BF16),
                   jax.ShapeDtypeStruct((t_len, D_MODEL), BF16), jax.ShapeDtypeStruct((SUB, D_MODEL), F32),
                   jax.ShapeDtypeStruct((t_len, D_FF), BF16), jax.ShapeDtypeStruct((t_len, D_FF), BF16)],
        scratch_shapes=[pltpu.VMEM(w_out.shape, BF16), pltpu.VMEM(w1.shape, BF16), pltpu.VMEM(w2.shape, BF16),
                        pltpu.VMEM((tm, D_FF), BF16), pltpu.VMEM((tm, D_FF), BF16), pltpu.VMEM((tm, D_FF), BF16),
                        pltpu.SemaphoreType.DMA((3,)), pltpu.SemaphoreType.DMA((2,))],
        compiler_params=_params(("arbitrary",), 58), name="mlp_fwd_bwd",
    )(x, y, target, g_mlp, g_f, w_out, w1, w2)
    dx1, h2, dx2, vec, z, dpre = outs
    return dx1, z, dpre, h2, dx2, vec


def _mixer_bwd(u, hs, dx1, saved, conv_w, rnn_conv_w, rnn_conv_b, wa, b_a, wx, b_x, lam, gnc, gnr, w_out,
               chip_sums, g_wout, tm):
    t_len = u.shape[0]
    n_tiles = t_len // tm
    n_chunks = tm // SUB
    per_tile = tm // SUB
    n_sums = len(chip_sums)

    def body(u_ref, up_ref, hs_ref, hp_ref, dx1_ref, xr_ref, ra_ref, ii_ref, mult_ref,
             cw_ref, rw_ref, rb_ref, wa_ref, ba_ref, wx_ref, bx_ref, lam_ref, gnc_ref, gnr_ref, wout_ref, *rest):
        hsends = rest[0:n_sums]
        gwout_ref = rest[n_sums]
        du_ref, vec_ref, wab_ref = rest[n_sums + 1:n_sums + 4]
        hrecvs = rest[n_sums + 4:2 * n_sums + 4]
        sib_wout = rest[2 * n_sums + 4]
        (du_s, dy_s, dpa_s, dpx_s, dxr_s, wabd, wxbd, acc, dwa_acc, dwx_acc,
         a_car, dh_car, dcq_car, dxr_car, i_send, i_recv, d_send, d_recv) = rest[2 * n_sums + 5:]
        step = pl.program_id(0)
        _host_chip_exchange(step, n_tiles, hsends, hrecvs, i_send, i_recv)
        _host_pair_exchange(step, n_tiles, [gwout_ref], [sib_wout], d_send, d_recv)
        has_prev = (step < n_tiles - 1).astype(F32)

        @pl.when(step == 0)
        def _():
            acc[...] = jnp.zeros(acc.shape, F32)
            dwa_acc[...] = jnp.zeros(dwa_acc.shape, F32)
            dwx_acc[...] = jnp.zeros(dwx_acc.shape, F32)
            a_car[...] = jnp.ones(a_car.shape, F32)
            dh_car[...] = jnp.zeros(dh_car.shape, F32)
            dcq_car[...] = jnp.zeros(dcq_car.shape, F32)
            dxr_car[...] = jnp.zeros(dxr_car.shape, F32)
            wabd[...] = _expand_heads(wa_ref[...])
            wxbd[...] = _expand_heads(wx_ref[...])

        row_c = lax.broadcasted_iota(jnp.int32, (SUB, CONV_WIDTH), 0)
        row_r = lax.broadcasted_iota(jnp.int32, (SUB, LRU_WIDTH), 0)
        cw = cw_ref[...]
        rw = rw_ref[...]
        rb = rb_ref[...]
        g_c = gnc_ref[...]
        g_r = gnr_ref[...]
        sp_c = LRU_C * _softplus_neg(lam_ref[...])

        up = up_ref[...] * has_prev
        cv_before = up[:, OFF_GC:OFF_GC + CONV_WIDTH] * up[:, OFF_V:OFF_V + CONV_WIDTH]
        xin_before = up[:, OFF_XR:OFF_XR + LRU_WIDTH]
        hs_before = hp_ref[...] * has_prev

        dy_s[...] = _dot_nt(dx1_ref[...].astype(BF16), wout_ref[...])

        xrb = xr_ref[...].astype(BF16)

        def recur_bwd(j, carry):
            a_later, dh_later = carry
            i = n_chunks - 1 - j
            r = pl.multiple_of(i * SUB, SUB)
            rp = pl.multiple_of(jnp.maximum(i - 1, 0) * SUB, SUB)
            xr = xr_ref[pl.ds(r, SUB), :]
            hs_c = hs_ref[pl.ds(r, SUB), :]
            hs_prev = jnp.where(i == 0, hs_before, hs_ref[pl.ds(rp, SUB), :])
            h_m1 = _down(hs_c, hs_prev, 1, row_r)
            ra = ra_ref[pl.ds(r, SUB), :]
            ii = ii_ref[pl.ds(r, SUB), :]
            mult = mult_ref[pl.ds(r, SUB), :]
            a = jnp.exp(-ra * sp_c)
            inv_mult = lax.rsqrt(mult * mult)
            ge, dge = _gelu(u_ref[pl.ds(r, SUB), OFF_G:OFF_G + LRU_WIDTH])
            y_r = hs_c * ge
            rr = _rms(y_r)
            yhat = y_r * rr
            dyn = dy_s[pl.ds(r, SUB), CONV_WIDTH:MIX_WIDTH]
            acc[ACC_GNR] += dyn * yhat
            dy_r = _rms_bwd(dyn, yhat, rr, g_r)
            du_s[pl.ds(r, SUB), OFF_G:OFF_G + LRU_WIDTH] = dy_r * hs_c * dge
            a_cum, d_cum = _scan8_rev(_up(a, a_later, 1, row_r), dy_r * ge, row_r)
            dh = a_cum * dh_later + d_cum
            dmult = dh * ii * xr
            dii = dh * mult * xr
            dxr_s[pl.ds(r, SUB), :] = dh * mult * ii
            dla = dh * h_m1 * a - dmult * a * a * inv_mult
            acc[ACC_SP] += -dla * ra
            dpa = -dla * sp_c * ra * (1.0 - ra)
            dpx = dii * ii * (1.0 - ii)
            acc[ACC_BA] += dpa
            acc[ACC_BX] += dpx
            dpa_s[pl.ds(r, SUB), :] = dpa
            dpx_s[pl.ds(r, SUB), :] = dpx
            return a, dh[0:1, :]

        a_first, dh_first = _chunk_loop(n_chunks, recur_bwd, (a_car[...], dh_car[...]))
        a_car[...] = a_first
        dh_car[...] = dh_first

        dpab = dpa_s[...].astype(BF16)
        dpxb = dpx_s[...].astype(BF16)
        dxr_s[...] += _block_diag_apply_t(dpab, wabd) + _block_diag_apply_t(dpxb, wxbd)
        for g in range(LRU_WIDTH // GROUP):
            cols = slice(g * GROUP, (g + 1) * GROUP)
            dwa_acc[cols, :] += _dot_tn(xrb[:, cols], dpab[:, cols])
            dwx_acc[cols, :] += _dot_tn(xrb[:, cols], dpxb[:, cols])

        def convs_bwd(j, carry):
            dcq_later, dxr_later = carry
            i = n_chunks - 1 - j
            r = pl.multiple_of(i * SUB, SUB)
            rp = pl.multiple_of(jnp.maximum(i - 1, 0) * SUB, SUB)
            cv_prev = jnp.where(i == 0, cv_before,
                                u_ref[pl.ds(rp, SUB), OFF_GC:OFF_GC + CONV_WIDTH]
                                * u_ref[pl.ds(rp, SUB), OFF_V:OFF_V + CONV_WIDTH])
            gb, gc, v, cv, cv_m1, cv_m2, cq = _conv3_chunk(u_ref, r, cv_prev, cw, row_c)
            y_c = gb * cq
            rc = _rms(y_c)
            yhat = y_c * rc
            dyn = dy_s[pl.ds(r, SUB), 0:CONV_WIDTH]
            acc[ACC_GNC, :, 0:CONV_WIDTH] += dyn * yhat
            dy_c = _rms_bwd(dyn, yhat, rc, g_c)
            dcq = dy_c * gb
            dcv = (cw[2:3, :] * dcq + cw[1:2, :] * _up(dcq, dcq_later, 1, row_c)
                   + cw[0:1, :] * _up(dcq, dcq_later, 2, row_c))
            acc[ACC_CW + 2, :, 0:CONV_WIDTH] += dcq * cv
            acc[ACC_CW + 1, :, 0:CONV_WIDTH] += dcq * cv_m1
            acc[ACC_CW + 0, :, 0:CONV_WIDTH] += dcq * cv_m2
            du_s[pl.ds(r, SUB), OFF_GB:OFF_GB + CONV_WIDTH] = dy_c * cq
            du_s[pl.ds(r, SUB), OFF_GC:OFF_GC + CONV_WIDTH] = dcv * v
            du_s[pl.ds(r, SUB), OFF_V:OFF_V + CONV_WIDTH] = dcv * gc

            xin_prev = jnp.where(i == 0, xin_before, u_ref[pl.ds(rp, SUB), OFF_XR:OFF_XR + LRU_WIDTH])
            xin, m1, m2, m3, _ = _conv4_chunk(u_ref, r, xin_prev, rw, rb, row_r)
            dxr = dxr_s[pl.ds(r, SUB), :]
            du_s[pl.ds(r, SUB), OFF_XR:OFF_XR + LRU_WIDTH] = (
                rw[3:4, :] * dxr + rw[2:3, :] * _up(dxr, dxr_later, 1, row_r)
                + rw[1:2, :] * _up(dxr, dxr_later, 2, row_r) + rw[0:1, :] * _up(dxr, dxr_later, 3, row_r))
            acc[ACC_RW + 3] += dxr * xin
            acc[ACC_RW + 2] += dxr * m1
            acc[ACC_RW + 1] += dxr * m2
            acc[ACC_RW + 0] += dxr * m3
            acc[ACC_BR] += dxr
            return dcq, dxr

        dcq_first, dxr_first = _chunk_loop(n_chunks, convs_bwd, (dcq_car[...], dxr_car[...]))
        dcq_car[...] = dcq_first
        dxr_car[...] = dxr_first

        du_ref[...] = du_s[...].astype(BF16)

        @pl.when(step == n_tiles - 1)
        def _():
            vec_ref[...] = jnp.zeros(vec_ref.shape, F32)
            rows = {ACC_GNC: ROW_GNC, ACC_GNR: ROW_GNR, ACC_BR: ROW_BR, ACC_BA: ROW_BA, ACC_BX: ROW_BX}
            for k in range(3):
                rows[ACC_CW + k] = ROW_CW + k
            for k in range(4):
                rows[ACC_RW + k] = ROW_RW + k
            for slot, out_row in rows.items():
                o = out_row - ROW_GNC
                vec_ref[o:o + 1, :] = jnp.sum(acc[slot], axis=0, keepdims=True)
            lam_v = lam_ref[...]
            dsp = jnp.sum(acc[ACC_SP], axis=0, keepdims=True)
            o = ROW_LAM - ROW_GNC
            vec_ref[o:o + 1, :] = -dsp * LRU_C / (1.0 + jnp.exp(lam_v))
            wab_ref[0:LRU_WIDTH, :] = _fold_heads(dwa_acc[...])
            wab_ref[LRU_WIDTH:2 * LRU_WIDTH, :] = _fold_heads(dwx_acc[...])

    rev = lambda w: pl.BlockSpec((tm, w), lambda s: (n_tiles - 1 - s, 0))
    before = lambda w: pl.BlockSpec((SUB, w), lambda s: (jnp.maximum((n_tiles - 1 - s) * per_tile - 1, 0), 0))
    whole = lambda a: pl.BlockSpec(a.shape, lambda s: (0,) * a.ndim)
    smalls = (conv_w, rnn_conv_w, rnn_conv_b, wa, b_a, wx, b_x, lam, gnc, gnr, w_out)
    full = lambda w: pltpu.VMEM((tm, w), F32)
    return pl.pallas_call(
        body, grid=(n_tiles,),
        in_specs=[rev(IN_COLS), before(IN_COLS), rev(LRU_WIDTH), before(LRU_WIDTH), rev(D_MODEL)]
        + [rev(LRU_WIDTH)] * len(saved) + [whole(a) for a in smalls] + [HBM_SPEC] * (n_sums + 1),
        out_specs=[rev(IN_COLS), pl.BlockSpec((16, D_MODEL), lambda s: (0, 0)),
                   pl.BlockSpec((2 * LRU_WIDTH, HEAD_DIM), lambda s: (0, 0))] + [HBM_SPEC] * (n_sums + 1),
        out_shape=[jax.ShapeDtypeStruct((t_len, IN_COLS), BF16), jax.ShapeDtypeStruct((16, D_MODEL), F32),
                   jax.ShapeDtypeStruct((2 * LRU_WIDTH, HEAD_DIM), F32)]
        + [jax.ShapeDtypeStruct(s.shape, BF16) for s in chip_sums]
        + [jax.ShapeDtypeStruct((4,) + g_wout.shape[1:], BF16)],
        scratch_shapes=[full(IN_COLS), full(MIX_WIDTH), full(LRU_WIDTH), full(LRU_WIDTH), full(LRU_WIDTH),
                        pltpu.VMEM((LRU_WIDTH, GROUP), BF16), pltpu.VMEM((LRU_WIDTH, GROUP), BF16),
                        pltpu.VMEM((N_ACC, SUB, LRU_WIDTH), F32),
                        pltpu.VMEM((LRU_WIDTH, GROUP), F32), pltpu.VMEM((LRU_WIDTH, GROUP), F32),
                        pltpu.VMEM((SUB, LRU_WIDTH), F32), pltpu.VMEM((1, LRU_WIDTH), F32),
                        pltpu.VMEM((SUB, CONV_WIDTH), F32), pltpu.VMEM((SUB, LRU_WIDTH), F32)]
        + _exchange_scratch(n_sums, 3) + _exchange_scratch(1, 4),
        compiler_params=_params(("arbitrary",), 56), name="mixer_bwd",
    )(u, u, hs, hs, dx1, *saved, *smalls, *chip_sums, g_wout)


def _in_proj_bwd(du, dx1, x, g_mix, win_t, tm, chip_sums, g_own):
    t_len = x.shape[0]
    n_steps = t_len // tm

    def body(du_ref, dx1_ref, x_ref, g_ref, w_ref, hs_ref, gown_ref,
             dx_ref, vec_ref, landed_ref, sib_ref, i_send, i_recv, d_send, d_recv):
        step = pl.program_id(0)
        _host_chip_exchange(step, n_steps, [hs_ref], [landed_ref], i_send, i_recv)
        _host_half_exchange(step, n_steps, gown_ref, sib_ref, d_send, d_recv)

        @pl.when(step == 0)
        def _():
            vec_ref[...] = jnp.zeros(vec_ref.shape, F32)

        dh = jnp.dot(du_ref[...], w_ref[...], preferred_element_type=F32)
        xv = x_ref[...]
        r1 = _rms(xv)
        xh = xv * r1
        vec_ref[0:1, :] += jnp.sum(dh * xh, axis=0, keepdims=True)
        dx_ref[...] = dx1_ref[...] + _rms_bwd(dh, xh, r1, g_ref[...])

    row_tile = lambda w: pl.BlockSpec((tm, w), lambda i: (i, 0))
    half_shape = (g_own.shape[0], g_own.shape[1] // 2, g_own.shape[2])
    return pl.pallas_call(
        body, grid=(n_steps,),
        in_specs=[row_tile(IN_COLS), row_tile(D_MODEL), row_tile(D_MODEL), pl.BlockSpec((1, D_MODEL), lambda i: (0, 0)),
                  pl.BlockSpec((IN_COLS, D_MODEL), lambda i: (0, 0))] + [HBM_SPEC] * 2,
        out_specs=[row_tile(D_MODEL), pl.BlockSpec((SUB, D_MODEL), lambda i: (0, 0))] + [HBM_SPEC] * 2,
        out_shape=[jax.ShapeDtypeStruct((t_len, D_MODEL), F32), jax.ShapeDtypeStruct((SUB, D_MODEL), F32),
                   jax.ShapeDtypeStruct(chip_sums.shape, BF16), jax.ShapeDtypeStruct(half_shape, BF16)],
        scratch_shapes=_exchange_scratch(1, 3) + [pltpu.SemaphoreType.DMA((1,)), pltpu.SemaphoreType.DMA((1,))],
        compiler_params=_params(("arbitrary",), 56), name="in_proj_bwd",
    )(du, dx1, x, g_mix, win_t, chip_sums, g_own)


def _tn_weight_grad(a, b, tk, name, pair=(), col_blocks=1):
    t_len, m = a.shape
    n = b.shape[1]
    n_steps = t_len // tk
    sent = tuple(pair)
    n_sent = len(sent)

    def body(a_ref, b_ref, *rest):
        srcs = rest[0:n_sent]
        o_ref = rest[n_sent]
        dsts = rest[n_sent + 1:2 * n_sent + 1]
        acc = rest[2 * n_sent + 1]
        sems = rest[2 * n_sent + 2:]
        j = pl.program_id(0)
        if pair:
            _host_pair_exchange(j, n_steps, srcs, dsts, *sems)

        @pl.when(j == 0)
        def _():
            acc[...] = jnp.zeros(acc.shape, F32)

        acc[...] += _dot_tn(a_ref[...].astype(BF16), b_ref[...].astype(BF16))

        @pl.when(j == n_steps - 1)
        def _():
            if col_blocks == 1:
                o_ref[...] = acc[...].astype(BF16)
            else:
                for k in range(col_blocks):
                    o_ref[k] = acc[:, k * nb:(k + 1) * nb].astype(BF16)

    nb = n // col_blocks
    out_dims = (m, n) if col_blocks == 1 else (col_blocks, m, nb)
    landed = [jax.ShapeDtypeStruct((4,) + g.shape[1:], BF16) for g in pair]
    scratch = [pltpu.VMEM((m, n), F32)]
    if n_sent:
        scratch += _exchange_scratch(n_sent, 4)
    return pl.pallas_call(
        body, grid=(n_steps,),
        in_specs=[pl.BlockSpec((tk, m), lambda j: (j, 0)), pl.BlockSpec((tk, n), lambda j: (j, 0))]
        + [HBM_SPEC] * n_sent,
        out_specs=[pl.BlockSpec(out_dims, lambda j: (0,) * len(out_dims))] + [HBM_SPEC] * n_sent,
        out_shape=[jax.ShapeDtypeStruct(out_dims, BF16)] + landed,
        scratch_shapes=scratch,
        compiler_params=_params(("arbitrary",), 56), name=name,
    )(a, b, *sent)


def _w_in_grad_part(du, h, tk, name, chip_ids, chip=(), halves=None, small=None):
    t_len = du.shape[0]
    n_t = t_len // tk
    n_q = chip_ids.shape[0]
    width = 2 * (IN_COLS // N_DEV)
    n_steps = n_q * n_t
    n_chip = len(chip)
    sent = tuple(chip) + (() if halves is None else (halves,)) + (() if small is None else tuple(small))
    n_sent = len(sent)

    def body(ids_ref, a_ref, b_ref, *rest):
        srcs = rest[0:n_sent]
        o_ref = rest[n_sent]
        dsts = rest[n_sent + 1:2 * n_sent + 1]
        acc = rest[2 * n_sent + 1]
        sems = list(rest[2 * n_sent + 2:])
        j = pl.program_id(1)
        step = pl.program_id(0) * n_t + j
        if chip:
            _host_chip_exchange(step, n_steps, srcs[0:n_chip], dsts[0:n_chip], sems.pop(0), sems.pop(0))
        if halves is not None:
            _host_half_exchange(step, n_steps, srcs[n_chip], dsts[n_chip], sems.pop(0), sems.pop(0))
        if small is not None:
            _host_small_exchange(step, n_steps, *srcs[n_sent - 3:], *dsts[n_sent - 3:], *sems)

        @pl.when(j == 0)
        def _():
            acc[...] = jnp.zeros(acc.shape, F32)

        acc[...] += _dot_tn(a_ref[...], b_ref[...])

        @pl.when(j == n_t - 1)
        def _():
            o_ref[0] = acc[...].astype(BF16)

    landed = [jax.ShapeDtypeStruct(s.shape, BF16) for s in chip]
    scratch = [pltpu.VMEM((width, D_MODEL), F32)]
    if chip:
        scratch += _exchange_scratch(len(chip), 3)
    if halves is not None:
        landed.append(jax.ShapeDtypeStruct((halves.shape[0], halves.shape[1] // 2, halves.shape[2]), BF16))
        scratch += [pltpu.SemaphoreType.DMA((halves.shape[0],)), pltpu.SemaphoreType.DMA((halves.shape[0],))]
    if small is not None:
        vec_m, vec_b, wab = small
        landed += [jax.ShapeDtypeStruct((N_DEV,) + vec_m.shape, F32), jax.ShapeDtypeStruct((N_DEV,) + vec_b.shape, F32),
                   jax.ShapeDtypeStruct((N_DEV, wab.shape[0] // N_DEV, wab.shape[1]), F32)]
        scratch += _exchange_scratch(3, N_DEV) + [pltpu.SemaphoreType.DMA((2,))]
    grid_spec = pltpu.PrefetchScalarGridSpec(
        num_scalar_prefetch=1, grid=(n_q, n_t),
        in_specs=[pl.BlockSpec((tk, width), lambda q, j, ids: (j, ids[q])),
                  pl.BlockSpec((tk, D_MODEL), lambda q, j, ids: (j, 0))] + [HBM_SPEC] * n_sent,
        out_specs=[pl.BlockSpec((1, width, D_MODEL), lambda q, j, ids: (q, 0, 0))] + [HBM_SPEC] * n_sent,
        scratch_shapes=scratch)
    return pl.pallas_call(
        body, grid_spec=grid_spec, out_shape=[jax.ShapeDtypeStruct((n_q, width, D_MODEL), BF16)] + landed,
        compiler_params=_params(("arbitrary", "arbitrary"), 40), name=name,
    )(chip_ids, du, h, *sent)


def _adamw(w, g, m, v):
    m = ADAM_B1 * m + (1.0 - ADAM_B1) * g
    v = ADAM_B2 * v + (1.0 - ADAM_B2) * (g * g)
    delta = -ADAM_LR * ((m / BC1) / (jnp.sqrt(v / BC2) + ADAM_EPS) + ADAM_WD * w)
    return delta, m, v


def _update_sharded(g, landed, w, m, v, rows_blk, name):
    rows, cols = w.shape

    def body(g_ref, l_ref, w_ref, m_ref, v_ref, og, od, om, ov):
        gv = g_ref[...]
        for j in range(3):
            gv = gv + l_ref[j].astype(F32)
        delta, mn, vn = _adamw(w_ref[...], gv, m_ref[...], v_ref[...])
        og[...] = gv
        od[...] = delta
        om[...] = mn
        ov[...] = vn

    blk = pl.BlockSpec((rows_blk, cols), lambda i: (i, 0))
    shape = pltpu.HBM((rows, cols), F32)
    return pl.pallas_call(
        body, grid=(rows // rows_blk,),
        in_specs=[blk, pl.BlockSpec((3, rows_blk, cols), lambda i: (0, i, 0)), blk, blk, blk],
        out_specs=[blk] * 4, out_shape=[shape] * 4,
        compiler_params=_params(("arbitrary",), 32), name=name,
    )(*_in_hbm(g, landed, w, m, v))


def _update_w_in(g_own, sib_own, landed, w, m, v, core, rows_blk):
    rows, cols = w.shape
    pad_cols = -(-cols // 128) * 128

    def body(core_ref, g_ref, s_ref, l_ref, w_ref, m_ref, v_ref, og, od, om, ov, padbuf, turned):
        gt = g_ref[0, 0].astype(F32) + s_ref[0].astype(F32)
        for j in range(3):
            gt = gt + l_ref[j].astype(F32)
        padbuf[...] = jnp.zeros(padbuf.shape, F32)
        padbuf[0:cols, :] = gt
        turned[...] = padbuf[...].T
        gv = turned[:, 0:cols]
        delta, mn, vn = _adamw(w_ref[...], gv, m_ref[...], v_ref[...])
        og[...] = gv
        od[...] = delta
        om[...] = mn
        ov[...] = vn

    blk = pl.BlockSpec((rows_blk, cols), lambda i, cr: (i, 0))
    grid_spec = pltpu.PrefetchScalarGridSpec(
        num_scalar_prefetch=1, grid=(rows // rows_blk,),
        in_specs=[pl.BlockSpec((1, 1, cols, rows_blk), lambda i, cr: (0, cr[0], 0, i)),
                  pl.BlockSpec((1, cols, rows_blk), lambda i, cr: (0, 0, i)),
                  pl.BlockSpec((3, cols, rows_blk), lambda i, cr: (0, 0, i)), blk, blk, blk],
        out_specs=[blk] * 4,
        scratch_shapes=[pltpu.VMEM((pad_cols, rows_blk), F32), pltpu.VMEM((rows_blk, pad_cols), F32)])
    return pl.pallas_call(
        body, grid_spec=grid_spec, out_shape=[pltpu.HBM((rows, cols), F32)] * 4,
        compiler_params=_params(("arbitrary",), 32), name="update_w_in",
    )(core, *_in_hbm(g_own.reshape(1, 2, cols, rows), sib_own, landed, w, m, v))


def _update_small(vsum, wsum, g_cw, g_rw, weights, moments_m, moments_v):
    n = len(weights)

    def body(*refs):
        vs, ws, gcw, grw = refs[0:4]
        w_refs = refs[4:4 + n]
        m_refs = refs[4 + n:4 + 2 * n]
        v_refs = refs[4 + 2 * n:4 + 3 * n]
        outs = refs[4 + 3 * n:]
        loss_ref = outs[0]
        loss_ref[...] = jnp.sum(vs[ROW_LOSS:ROW_LOSS + 1, :], axis=1, keepdims=True)
        grads = [
            vs[ROW_GMIX:ROW_GMIX + 1, :], gcw[...], grw[...], vs[ROW_BR:ROW_BR + 1, :],
            ws[0:LRU_WIDTH, :], vs[ROW_BA:ROW_BA + 1, :], ws[LRU_WIDTH:2 * LRU_WIDTH, :], vs[ROW_BX:ROW_BX + 1, :],
            vs[ROW_LAM:ROW_LAM + 1, :], vs[ROW_GNC:ROW_GNC + 1, 0:CONV_WIDTH], vs[ROW_GNR:ROW_GNR + 1, :],
            vs[ROW_GMLP:ROW_GMLP + 1, :], vs[ROW_GF:ROW_GF + 1, :],
        ]
        for k in range(n):
            gk = grads[k]
            delta, mn, vn = _adamw(w_refs[k][...], gk, m_refs[k][...], v_refs[k][...])
            outs[1 + 4 * k][...] = gk
            outs[2 + 4 * k][...] = delta
            outs[3 + 4 * k][...] = mn
            outs[4 + 4 * k][...] = vn

    whole = lambda a: pl.BlockSpec(a.shape, lambda i: (0,) * len(a.shape))
    out_shape = [jax.ShapeDtypeStruct((1, 1), F32)]
    for w in weights:
        out_shape += [jax.ShapeDtypeStruct(w.shape, F32)] * 4
    args = (vsum, wsum, g_cw, g_rw, *weights, *moments_m, *moments_v)
    return pl.pallas_call(
        body, grid=(1,), out_shape=out_shape, in_specs=[whole(a) for a in args], out_specs=[whole(s) for s in out_shape],
        compiler_params=_params(("arbitrary",), 32), name="update_small",
    )(*args)


def kernel(x, norm_mix_g, w_in, conv_w, rnn_conv_w, rnn_conv_b, w_a, b_a, w_x, b_x, lru_lambda, g_norm_conv, g_norm_rnn, w_out, norm_mlp_g, w_mlp_in, w_mlp_out, final_norm_g, loss_target, m_norm_mix_g, m_w_in, m_conv_w, m_rnn_conv_w, m_rnn_conv_b, m_w_a, m_b_a, m_w_x, m_b_x, m_lru_lambda, m_g_norm_conv, m_g_norm_rnn, m_w_out, m_norm_mlp_g, m_w_mlp_in, m_w_mlp_out, m_final_norm_g, v_norm_mix_g, v_w_in, v_conv_w, v_rnn_conv_w, v_rnn_conv_b, v_w_a, v_b_a, v_w_x, v_b_x, v_lru_lambda, v_g_norm_conv, v_g_norm_rnn, v_w_out, v_norm_mlp_g, v_w_mlp_in, v_w_mlp_out, v_final_norm_g):
    t_len = x.shape[1]
    my_id = 4 * lax.axis_index("x") + 2 * lax.axis_index("y") + lax.axis_index("c")
    tm = min(256, t_len)
    tb = min(512, t_len)
    tk = min(512, t_len)

    xs = x.reshape(t_len, D_MODEL)
    tgt = loss_target.reshape(t_len, D_MODEL)
    flat = lambda a: a.reshape(a.shape[-2:]) if a.ndim == 3 else a.reshape(1, -1)
    heads = lambda a: a.reshape(LRU_WIDTH, HEAD_DIM)

    win_shard, wout_shard, w1_shard, w2_shard, cp_shard = _prep_shards(
        flat(w_in), flat(w_out), flat(w_mlp_in), flat(w_mlp_out), flat(conv_w), flat(rnn_conv_w))

    u, h, win_t, cp_full = _in_proj(xs, flat(norm_mix_g), (win_shard, cp_shard), tb)
    cpack = cp_full.reshape(N_DEV, 8, 128)
    conv_full = jnp.transpose(cpack[:, 0:3, 0:64], (1, 0, 2)).reshape(3, CONV_WIDTH)
    rnn_full = jnp.transpose(cpack[:, 3:7, :], (1, 0, 2)).reshape(4, LRU_WIDTH)
    mixer_small = (conv_full, rnn_full, flat(rnn_conv_b), heads(w_a), flat(b_a), heads(w_x), flat(b_x),
                   flat(lru_lambda), flat(g_norm_conv), flat(g_norm_rnn))
    hs, y, xr, gate_r, gate_i, mult, w1_blk, w2_blk, wout_blk = _mixer_fwd(
        u, *mixer_small, (w1_shard, w2_shard, wout_shard), tm)
    wout_f = wout_blk.reshape(MIX_WIDTH, D_MODEL)
    dx1, z, dpre, h2, dx2, vec_m = _mlp_fwd_bwd(xs, y, tgt, flat(norm_mlp_g), flat(final_norm_g), wout_f, w1_blk,
                                                w2_blk.reshape(D_FF, D_MODEL), tb)
    (g_w1,) = _tn_weight_grad(h2, dpre, tk, "w_mlp_in_grad", col_blocks=N_DEV)
    (g_w2,) = _tn_weight_grad(z, dx2, tk, "w_mlp_out_grad")
    g_w2 = g_w2.reshape(N_DEV, D_FF // N_DEV, D_MODEL)
    g_wout, sib_w1, sib_w2 = _tn_weight_grad(y, dx1, tk, "w_out_grad", pair=(g_w1, g_w2))
    g_wout = g_wout.reshape(N_DEV, MIX_WIDTH // N_DEV, D_MODEL)
    hsend_w1, own_w1 = _pair_sum(g_w1, sib_w1, "pair_sum_w_mlp_in")
    hsend_w2, own_w2 = _pair_sum(g_w2, sib_w2, "pair_sum_w_mlp_out")
    du, vec_b, wab, landed_w1, landed_w2, sib_wout = _mixer_bwd(
        u, hs, dx1, (xr, gate_r, gate_i, mult), *mixer_small, wout_f, (hsend_w1, hsend_w2), g_wout, tm)
    hsend_wout, own_wout = _pair_sum(g_wout, sib_wout, "pair_sum_w_out")
    ax, ay, ac = lax.axis_index("x"), lax.axis_index("y"), lax.axis_index("c")
    chip_ids = jnp.stack([2 * cx + cy for cx, cy in [(ax, ay)] + _other_chips(ax, ay)]).astype(jnp.int32)
    core = jnp.reshape(ac, (1,)).astype(jnp.int32)
    tw = min(1024, t_len)
    g_others, landed_wout, vrecv_m, vrecv_b, wrecv = _w_in_grad_part(
        du, h, tw, "w_in_grad_others", chip_ids[1:4], chip=(hsend_wout,), small=(vec_m, vec_b, wab))
    g_own, sib_others = _w_in_grad_part(du, h, tw, "w_in_grad_own", chip_ids[0:1], halves=g_others)
    hsend_win = _pair_sum_parts(g_others, sib_others, core)
    grad_x, vec_x, landed_win, sib_own = _in_proj_bwd(du, dx1, xs, flat(norm_mix_g), win_t, tm, hsend_win, g_own)

    vsum, wsum = _final_small(vrecv_m, vrecv_b, wab, wrecv, vec_x)

    up_win = _update_w_in(g_own, sib_own, landed_win, flat(w_in), flat(m_w_in), flat(v_w_in), core, 256)
    up_wout = _update_sharded(own_wout, landed_wout, flat(w_out), flat(m_w_out), flat(v_w_out), 96, "update_w_out")
    up_w1 = _update_sharded(own_w1, landed_w1, flat(w_mlp_in), flat(m_w_mlp_in), flat(v_w_mlp_in), 256,
                            "update_w_mlp_in")
    up_w2 = _update_sharded(own_w2, landed_w2, flat(w_mlp_out), flat(m_w_mlp_out), flat(v_w_mlp_out), 256,
                            "update_w_mlp_out")

    g_cw = lax.dynamic_slice(vsum, (ROW_CW, 64 * my_id), (3, 64))
    g_rw = lax.dynamic_slice(vsum, (ROW_RW, 128 * my_id), (4, 128))
    small_w = (norm_mix_g, conv_w, rnn_conv_w, rnn_conv_b, w_a, b_a, w_x, b_x, lru_lambda, g_norm_conv, g_norm_rnn,
               norm_mlp_g, final_norm_g)
    small_m = (m_norm_mix_g, m_conv_w, m_rnn_conv_w, m_rnn_conv_b, m_w_a, m_b_a, m_w_x, m_b_x, m_lru_lambda,
               m_g_norm_conv, m_g_norm_rnn, m_norm_mlp_g, m_final_norm_g)
    small_v = (v_norm_mix_g, v_conv_w, v_rnn_conv_w, v_rnn_conv_b, v_w_a, v_b_a, v_w_x, v_b_x, v_lru_lambda,
               v_g_norm_conv, v_g_norm_rnn, v_norm_mlp_g, v_final_norm_g)
    is_heads = (False, False, False, False, True, False, True, False, False, False, False, False, False)
    as2d = lambda arrs: [heads(a) if hd else flat(a) for a, hd in zip(arrs, is_heads)]
    small_out = _update_small(vsum, wsum, g_cw, g_rw, as2d(small_w), as2d(small_m), as2d(small_v))
    loss = small_out[0].reshape(())

    names = ["norm_mix_g", "w_in", "conv_w", "rnn_conv_w", "rnn_conv_b", "w_a", "b_a", "w_x", "b_x", "lru_lambda",
             "g_norm_conv", "g_norm_rnn", "w_out", "norm_mlp_g", "w_mlp_in", "w_mlp_out", "final_norm_g"]
    originals = dict(zip(names, (norm_mix_g, w_in, conv_w, rnn_conv_w, rnn_conv_b, w_a, b_a, w_x, b_x, lru_lambda,
                                 g_norm_conv, g_norm_rnn, w_out, norm_mlp_g, w_mlp_in, w_mlp_out, final_norm_g)))
    results = {"w_in": up_win, "w_out": up_wout, "w_mlp_in": up_w1, "w_mlp_out": up_w2}
    small_names = ["norm_mix_g", "conv_w", "rnn_conv_w", "rnn_conv_b", "w_a", "b_a", "w_x", "b_x", "lru_lambda",
                   "g_norm_conv", "g_norm_rnn", "norm_mlp_g", "final_norm_g"]
    for k, nm in enumerate(small_names):
        results[nm] = small_out[1 + 4 * k:5 + 4 * k]
    out = [loss, grad_x.reshape(x.shape)]
    for kind in range(4):
        out += [results[nm][kind].reshape(originals[nm].shape) for nm in names]
    return tuple(out)
```

```python
import functools

import jax
import jax.numpy as jnp
from jax import lax
from jax.experimental import pallas as pl
from jax.experimental.pallas import tpu as pltpu

F32 = jnp.float32
BF16 = jnp.bfloat16

D_MODEL = 1024
HEAD_DIM = 64
CONV_WIDTH = 512
LRU_WIDTH = 1024
MIX_WIDTH = CONV_WIDTH + LRU_WIDTH
IN_COLS = 3 * CONV_WIDTH + 2 * LRU_WIDTH
D_FF = 4 * D_MODEL
GROUP = 256
EPS = 1e-6
LRU_C = 8.0
N_DEV = 8
SUB = 8

OFF_GB, OFF_GC, OFF_V, OFF_XR, OFF_G = 0, 512, 1024, 1536, 2560

ADAM_LR, ADAM_B1, ADAM_B2, ADAM_EPS, ADAM_WD, ADAM_STEP = 0.001, 0.9, 0.999, 1e-08, 0.01, 10
BC1 = 1.0 - ADAM_B1 ** ADAM_STEP
BC2 = 1.0 - ADAM_B2 ** ADAM_STEP

MIB = 1024 * 1024
MESH = pl.DeviceIdType.MESH

VEC_ROWS = 32
ROW_GF, ROW_GMLP, ROW_LOSS = 0, 1, 2
ROW_GNC, ROW_GNR, ROW_BR, ROW_BA, ROW_BX, ROW_LAM, ROW_CW, ROW_RW = 8, 9, 10, 11, 12, 13, 14, 17
ROW_GMIX = 24
ACC_GNC, ACC_GNR, ACC_BR, ACC_BA, ACC_BX, ACC_SP, ACC_CW, ACC_RW, N_ACC = 0, 1, 2, 3, 4, 5, 6, 9, 13


def _params(semantics=None, vmem_mib=48):
    return pltpu.CompilerParams(dimension_semantics=semantics, vmem_limit_bytes=vmem_mib * MIB)


def _rms(x):
    return lax.rsqrt(jnp.mean(x * x, axis=-1, keepdims=True) + EPS)


def _rms_bwd(dy, xhat, r, g):
    dyh = dy * g
    return r * (dyh - xhat * jnp.mean(dyh * xhat, axis=-1, keepdims=True))


def _sigmoid(x):
    return 0.5 + 0.5 * jnp.tanh(0.5 * x)


def _gelu(x):
    c0, c1 = 0.7978845608028654, 0.044715
    x2 = x * x
    t = jnp.tanh(x * (c0 + (c0 * c1) * x2))
    half = 0.5 + 0.5 * t
    ge = x * half
    dge = half + (0.5 * x) * (1.0 - t * t) * (c0 + (3.0 * c0 * c1) * x2)
    return ge, dge


def _softplus_neg(lam):
    z = -lam
    e = jnp.exp(-jnp.abs(z))
    return jnp.maximum(z, 0.0) + jnp.where(e < 1e-4, e * (1.0 - 0.5 * e), jnp.log(1.0 + e))


def _lru_gates(pa, px, sp_c):
    ra = _sigmoid(pa)
    ii = _sigmoid(px)
    la = -ra * sp_c
    a = jnp.exp(la)
    x2 = 2.0 * la
    series = -x2 * (1.0 + x2 * (0.5 + x2 * (1.0 / 6.0 + x2 * (1.0 / 24.0))))
    m2 = jnp.where(x2 > -0.01, series, 1.0 - a * a)
    inv_mult = lax.rsqrt(m2)
    mult = jnp.where(m2 > 0.0, m2 * inv_mult, 0.0)
    return ra, ii, a, mult, inv_mult


def _down(cur, prev, s, row):
    return jnp.where(row >= s, pltpu.roll(cur, s, 0), pltpu.roll(prev, s, 0))


def _up(cur, nxt, s, row):
    return jnp.where(row < SUB - s, pltpu.roll(cur, SUB - s, 0), pltpu.roll(nxt, SUB - s, 0))


def _scan8_fwd(a, b, row):
    for s in (1, 2, 4):
        m = row >= s
        a_sh = pltpu.roll(a, s, 0)
        b_sh = pltpu.roll(b, s, 0)
        b = jnp.where(m, a * b_sh + b, b)
        a = jnp.where(m, a * a_sh, a)
    return a, b


def _scan8_rev(a, b, row):
    for s in (1, 2, 4):
        m = row < SUB - s
        a_sh = pltpu.roll(a, SUB - s, 0)
        b_sh = pltpu.roll(b, SUB - s, 0)
        b = jnp.where(m, a * b_sh + b, b)
        a = jnp.where(m, a * a_sh, a)
    return a, b


def _group_mask(shape):
    r = lax.broadcasted_iota(jnp.int32, shape, 0)
    c = lax.broadcasted_iota(jnp.int32, shape, 1)
    return ((r % GROUP) // HEAD_DIM) == (c // HEAD_DIM)


def _expand_heads(w):
    j = lax.broadcasted_iota(jnp.int32, (HEAD_DIM, GROUP), 0)
    c = lax.broadcasted_iota(jnp.int32, (HEAD_DIM, GROUP), 1)
    spread = (c % HEAD_DIM == j).astype(BF16)
    e = jnp.dot(w.astype(BF16), spread, preferred_element_type=F32)
    return jnp.where(_group_mask(e.shape), e, 0.0).astype(BF16)


def _fold_heads(p):
    p = jnp.where(_group_mask(p.shape), p, 0.0)
    c = lax.broadcasted_iota(jnp.int32, (GROUP, HEAD_DIM), 0)
    j = lax.broadcasted_iota(jnp.int32, (GROUP, HEAD_DIM), 1)
    fold = (c % HEAD_DIM == j).astype(BF16)
    hi = p.astype(BF16)
    rest = p - hi.astype(F32)
    mid = rest.astype(BF16)
    lo = (rest - mid.astype(F32)).astype(BF16)
    dot = functools.partial(jnp.dot, preferred_element_type=F32)
    return dot(hi, fold) + dot(mid, fold) + dot(lo, fold)


def _block_diag_apply(xb, wbd_ref):
    parts = [jnp.dot(xb[:, g * GROUP:(g + 1) * GROUP], wbd_ref[g * GROUP:(g + 1) * GROUP, :],
                     preferred_element_type=F32) for g in range(LRU_WIDTH // GROUP)]
    return jnp.concatenate(parts, axis=1)


def _block_diag_apply_t(db, wbd_ref):
    parts = [lax.dot_general(db[:, g * GROUP:(g + 1) * GROUP], wbd_ref[g * GROUP:(g + 1) * GROUP, :],
                             (((1,), (1,)), ((), ())), preferred_element_type=F32)
             for g in range(LRU_WIDTH // GROUP)]
    return jnp.concatenate(parts, axis=1)


def _dot_nt(a, b):
    return lax.dot_general(a, b, (((1,), (1,)), ((), ())), preferred_element_type=F32)


def _dot_tn(a, b):
    return lax.dot_general(a, b, (((0,), (0,)), ((), ())), preferred_element_type=F32)


def _chunk_loop(n_chunks, chunk, init, in_flight=4):
    def body(k, carry):
        for j in range(in_flight):
            carry = chunk(k * in_flight + j, carry)
        return carry

    return lax.fori_loop(0, n_chunks // in_flight, body, init)


def _place():
    x, y, c = lax.axis_index("x"), lax.axis_index("y"), lax.axis_index("c")
    return x, y, c


def _block_id(chip, core):
    return 4 * chip[0] + 2 * chip[1] + core


def _other_chips(x, y):
    return [(1 - x, y), (x, 1 - y), (1 - x, 1 - y)]


def _remote_copy(src, dst, send_sem, recv_sem, to):
    return pltpu.make_async_remote_copy(src_ref=src, dst_ref=dst, send_sem=send_sem, recv_sem=recv_sem,
                                        device_id=to, device_id_type=MESH)


HBM_SPEC = pl.BlockSpec(memory_space=pl.ANY)


def _in_hbm(*arrays):
    return [pltpu.with_memory_space_constraint(a, pltpu.HBM) for a in arrays]


def _prep_shards(w_in, w_out, w_mlp_in, w_mlp_out, conv_w, rnn_conv_w):
    n_in = w_in.shape[1]

    def body(win_ref, wout_ref, w1_ref, w2_ref, cw_ref, rw_ref, o_win, o_wout, o_w1, o_w2, o_cp, padbuf):
        padbuf[...] = jnp.zeros(padbuf.shape, F32)
        padbuf[:, 0:n_in] = win_ref[...]
        o_win[...] = padbuf[...].T[0:n_in, :].astype(BF16)
        o_wout[...] = wout_ref[...].astype(BF16)
        o_w1[...] = w1_ref[...].astype(BF16)
        o_w2[...] = w2_ref[...].astype(BF16)
        o_cp[...] = jnp.zeros(o_cp.shape, F32)
        o_cp[0:3, 0:64] = cw_ref[...]
        o_cp[3:7, :] = rw_ref[...]

    whole = lambda shape: pl.BlockSpec(shape, lambda i: (0,) * len(shape))
    args = (w_in, w_out, w_mlp_in, w_mlp_out, conv_w, rnn_conv_w)
    shapes = [((n_in, D_MODEL), BF16), (w_out.shape, BF16), (w_mlp_in.shape, BF16), (w_mlp_out.shape, BF16),
              ((8, 128), F32)]
    return pl.pallas_call(
        body, grid=(1,), out_shape=[jax.ShapeDtypeStruct(s, d) for s, d in shapes],
        in_specs=[whole(a.shape) for a in args], out_specs=[whole(s) for s, _ in shapes],
        scratch_shapes=[pltpu.VMEM((D_MODEL, 512), F32)],
        compiler_params=_params(("arbitrary",), 40), name="prep_shards",
    )(*args)


def _host_all_gather(step, n_steps, shards, fulls, send_sems, recv_sems, local_sems):
    x, y, c = _place()
    me = (x, y, c)
    my_id = _block_id((x, y), c)
    sibling = (x, y, 1 - c)
    chips = _other_chips(x, y)
    n_arr = len(shards)

    def copy(arr, k, block, to, src=None):
        dst = fulls[arr].at[block]
        return _remote_copy(dst if src is None else src, dst, send_sems.at[arr, k], recv_sems.at[arr, k], to)

    def local(arr):
        return pltpu.make_async_copy(shards[arr], fulls[arr].at[my_id], local_sems.at[arr])

    @pl.when(step == 0)
    def _():
        for arr in range(n_arr):
            local(arr).start()
            copy(arr, 0, my_id, sibling, shards[arr]).start()
            for j, chip in enumerate(chips):
                copy(arr, 1 + j, my_id, (*chip, c), shards[arr]).start()

    @pl.when(step == max(n_steps - 2, 0))
    def _():
        for j, chip in enumerate(chips):
            for arr in range(n_arr):
                copy(arr, 1 + j, _block_id(chip, c), me).wait_recv()
                copy(arr, 4 + j, _block_id(chip, c), sibling).start()

    @pl.when(step == n_steps - 1)
    def _():
        for arr in range(n_arr):
            copy(arr, 0, _block_id((x, y), 1 - c), me).wait_recv()
            for j, chip in enumerate(chips):
                copy(arr, 4 + j, _block_id(chip, 1 - c), me).wait_recv()
            for k in range(4):
                copy(arr, k, my_id, me, shards[arr]).wait_send()
            for j, chip in enumerate(chips):
                copy(arr, 4 + j, _block_id(chip, c), me).wait_send()
            local(arr).wait()


def _host_pair_exchange(step, n_steps, gs, sibs, send_sems, recv_sems):
    x, y, c = _place()
    sibling = (x, y, 1 - c)
    chips = [(x, y)] + _other_chips(x, y)

    def d2d(arr, q):
        return _remote_copy(gs[arr].at[_block_id(chips[q], 1 - c)], sibs[arr].at[q],
                            send_sems.at[arr, q], recv_sems.at[arr, q], sibling)

    @pl.when(step == 0)
    def _():
        for arr in range(len(gs)):
            for q in (1, 2, 3, 0):
                d2d(arr, q).start()

    @pl.when(step == n_steps - 1)
    def _():
        for arr in range(len(gs)):
            for q in range(4):
                d2d(arr, q).wait()


def _host_chip_exchange(step, n_steps, hsends, hrecvs, send_sems, recv_sems):
    x, y, c = _place()
    chips = _other_chips(x, y)

    def ici(arr, j):
        return _remote_copy(hsends[arr].at[j], hrecvs[arr].at[j], send_sems.at[arr, j], recv_sems.at[arr, j],
                            (*chips[j], c))

    @pl.when(step == 0)
    def _():
        for arr in range(len(hsends)):
            for j in range(3):
                ici(arr, j).start()

    @pl.when(step == n_steps - 1)
    def _():
        for arr in range(len(hsends)):
            for j in range(3):
                ici(arr, j).wait()


def _host_half_exchange(step, n_steps, parts, sibs, send_sems, recv_sems):
    x, y, c = _place()
    n_q, rows2, _ = parts.shape
    half = rows2 // 2

    def d2d(q):
        src = parts.at[q, pl.ds(pl.multiple_of((1 - c) * half, 16), half), :]
        return _remote_copy(src, sibs.at[q], send_sems.at[q], recv_sems.at[q], (x, y, 1 - c))

    @pl.when(step == 0)
    def _():
        for q in range(n_q):
            d2d(q).start()

    @pl.when(step == n_steps - 1)
    def _():
        for q in range(n_q):
            d2d(q).wait()


def _peer(x, y, c, k):
    return (x ^ ((k >> 2) & 1), y ^ ((k >> 1) & 1), c ^ (k & 1))


def _host_small_exchange(step, n_steps, vec_m, vec_b, wab, vrecv_m, vrecv_b, wrecv, send_sems, recv_sems, local_sems):
    x, y, c = _place()
    my_id = _block_id((x, y), c)
    wrows = wab.shape[0] // N_DEV

    def copies(k):
        to = _peer(x, y, c, k)
        block = wab.at[pl.ds(pl.multiple_of(_block_id(to[0:2], to[2]) * wrows, SUB), wrows), :]
        return [_remote_copy(vec_m, vrecv_m.at[my_id], send_sems.at[0, k], recv_sems.at[0, k], to),
                _remote_copy(vec_b, vrecv_b.at[my_id], send_sems.at[1, k], recv_sems.at[1, k], to),
                _remote_copy(block, wrecv.at[k], send_sems.at[2, k], recv_sems.at[2, k], to)]

    mine = [pltpu.make_async_copy(vec_m, vrecv_m.at[my_id], local_sems.at[0]),
            pltpu.make_async_copy(vec_b, vrecv_b.at[my_id], local_sems.at[1])]

    @pl.when(step == 0)
    def _():
        for cp in mine:
            cp.start()
        for k in range(1, N_DEV):
            for cp in copies(k):
                cp.start()

    @pl.when(step == n_steps - 1)
    def _():
        for k in range(1, N_DEV):
            for cp in copies(k):
                cp.wait()
        for cp in mine:
            cp.wait()


def _pair_sum_parts(parts, sibs, core):
    n_q, rows2, cols = parts.shape
    half = rows2 // 2

    def body(core_ref, g_ref, s_ref, o_ref):
        o_ref[0] = (g_ref[0, 0].astype(F32) + s_ref[0].astype(F32)).astype(BF16)

    block = (1, half, cols)
    grid_spec = pltpu.PrefetchScalarGridSpec(
        num_scalar_prefetch=1, grid=(n_q,),
        in_specs=[pl.BlockSpec((1, 1, half, cols), lambda q, cr: (q, cr[0], 0, 0)),
                  pl.BlockSpec(block, lambda q, cr: (q, 0, 0))],
        out_specs=pl.BlockSpec(block, lambda q, cr: (q, 0, 0)))
    return pl.pallas_call(
        body, grid_spec=grid_spec, out_shape=pltpu.HBM((n_q, half, cols), BF16),
        compiler_params=_params(("arbitrary",), 32), name="pair_sum_w_in",
    )(core, *_in_hbm(parts.reshape(n_q, 2, half, cols), sibs))


def _pair_sum(g, sib, name):
    _, rows, cols = g.shape
    x, y, c = _place()
    slots = jnp.stack([_block_id(chip, c) for chip in [(x, y)] + _other_chips(x, y)]).astype(jnp.int32)

    def body(slots_ref, g_ref, sib_ref, hs_ref, own_ref):
        q = pl.program_id(0)
        both = g_ref[0].astype(F32) + sib_ref[0].astype(F32)

        @pl.when(q == 0)
        def _():
            own_ref[...] = both

        @pl.when(q > 0)
        def _():
            hs_ref[0] = both.astype(BF16)

    block = (1, rows, cols)
    grid_spec = pltpu.PrefetchScalarGridSpec(
        num_scalar_prefetch=1, grid=(4,),
        in_specs=[pl.BlockSpec(block, lambda q, s: (s[q], 0, 0)), pl.BlockSpec(block, lambda q, s: (q, 0, 0))],
        out_specs=[pl.BlockSpec(block, lambda q, s: (jnp.maximum(q - 1, 0), 0, 0)),
                   pl.BlockSpec((rows, cols), lambda q, s: (0, 0))])
    return pl.pallas_call(
        body, grid_spec=grid_spec,
        out_shape=(pltpu.HBM((3, rows, cols), BF16), pltpu.HBM((rows, cols), F32)),
        compiler_params=_params(("arbitrary",), 32), name=name,
    )(slots, *_in_hbm(g, sib))


def _exchange_scratch(n_arr, n_copies):
    return [pltpu.SemaphoreType.DMA((n_arr, n_copies)), pltpu.SemaphoreType.DMA((n_arr, n_copies))]


def _final_small(vrecv_m, vrecv_b, wab, wrecv, vec_x):
    wrows = wab.shape[0] // N_DEV

    def body(vm_ref, vb_ref, w_ref, wr_ref, vx_ref, o_vec, o_w, xrecv, wred, x_send, x_recv, b_send, b_recv):
        x, y, c = _place()
        my_id = _block_id((x, y), c)
        my_rows = pl.ds(pl.multiple_of(my_id * wrows, SUB), wrows)

        def xcopy(k):
            return _remote_copy(vx_ref, xrecv.at[my_id], x_send.at[k], x_recv.at[k], _peer(x, y, c, k))

        def bcopy(k):
            return _remote_copy(wred, o_w.at[my_rows, :], b_send.at[k], b_recv.at[k], _peer(x, y, c, k))

        xrecv[my_id] = vx_ref[...]
        for k in range(1, N_DEV):
            xcopy(k).start()
        red = w_ref[my_rows, :]
        for k in range(1, N_DEV):
            red = red + wr_ref[k]
        wred[...] = red
        o_w[my_rows, :] = red
        for k in range(1, N_DEV):
            bcopy(k).start()
        for k in range(1, N_DEV):
            xcopy(k).wait_recv()
        for rows, ref in ((slice(0, 8), vm_ref), (slice(8, 24), vb_ref), (slice(24, 32), xrecv)):
            tot = ref[0]
            for s in range(1, N_DEV):
                tot = tot + ref[s]
            o_vec[rows, :] = tot
        for k in range(1, N_DEV):
            bcopy(k).wait_recv()
        for k in range(1, N_DEV):
            xcopy(k).wait_send()
            bcopy(k).wait_send()

    vm = pl.BlockSpec(memory_space=pltpu.VMEM)
    dma8 = pltpu.SemaphoreType.DMA((N_DEV,))
    return pl.pallas_call(
        body, out_shape=(jax.ShapeDtypeStruct((VEC_ROWS, D_MODEL), F32), jax.ShapeDtypeStruct(wab.shape, F32)),
        in_specs=[vm] * 5, out_specs=[vm] * 2,
        scratch_shapes=[pltpu.VMEM((N_DEV, SUB, D_MODEL), F32), pltpu.VMEM((wrows, HEAD_DIM), F32),
                        dma8, dma8, dma8, dma8],
        compiler_params=_params(vmem_mib=32), name="final_small",
    )(vrecv_m, vrecv_b, wab, wrecv, vec_x)


def _in_proj(x, g_mix, shards, tm):
    t_len = x.shape[0]
    n_t = t_len // tm
    n_arr = len(shards)
    rows = [s.shape[0] for s in shards]
    width = 2 * rows[0]
    ax, ay = lax.axis_index("x"), lax.axis_index("y")
    order = jnp.stack([2 * cx + cy for cx, cy in [(ax, ay)] + _other_chips(ax, ay)]).astype(jnp.int32)

    def body(order_ref, x_ref, g_ref, *rest):
        shard_refs = rest[0:n_arr]
        u_ref, h_ref = rest[n_arr:n_arr + 2]
        fulls = rest[n_arr + 2:2 * n_arr + 2]
        h_s, wbuf, send_sems, recv_sems, local_sems, load_sem = rest[2 * n_arr + 2:]
        p = pl.program_id(0)
        i = pl.program_id(1)
        x_, y_, c = _place()
        me = (x_, y_, c)
        my_id = _block_id((x_, y_), c)
        sibling = (x_, y_, 1 - c)
        chips = _other_chips(x_, y_)

        def block(arr, blk):
            return fulls[arr].at[pl.ds(pl.multiple_of(blk * rows[arr], rows[arr]), rows[arr]), :]

        def copy(arr, k, blk, to, src=None):
            dst = block(arr, blk)
            return _remote_copy(dst if src is None else src, dst, send_sems.at[arr, k], recv_sems.at[arr, k], to)

        def local(arr):
            return pltpu.make_async_copy(shard_refs[arr], block(arr, my_id), local_sems.at[arr])

        def load_chip(chip):
            start = pl.multiple_of((2 * chip[0] + chip[1]) * width, width)
            cp = pltpu.make_async_copy(fulls[0].at[pl.ds(start, width), :], wbuf, load_sem.at[0])
            cp.start()
            cp.wait()

        @pl.when((p == 0) & (i == 0))
        def _():
            for arr in range(n_arr):
                local(arr).start()
                copy(arr, 0, my_id, sibling, shard_refs[arr]).start()
                for j in (0, 1):
                    copy(arr, 1 + j, my_id, (*chips[j], c), shard_refs[arr]).start()
            for arr in range(n_arr):
                local(arr).wait()
                copy(arr, 0, _block_id((x_, y_), 1 - c), me).wait_recv()
            load_chip((x_, y_))

        for j, chip in enumerate(chips):
            @pl.when((p == j + 1) & (i == 0))
            def _(j=j, chip=chip):
                for arr in range(n_arr):
                    copy(arr, 1 + j, _block_id(chip, c), me).wait_recv()
                    copy(arr, 4 + j, _block_id(chip, c), sibling).start()
                    if j == 0:
                        copy(arr, 3, my_id, (*chips[2], c), shard_refs[arr]).start()
                for arr in range(n_arr):
                    copy(arr, 4 + j, _block_id(chip, 1 - c), me).wait_recv()
                load_chip(chip)

        @pl.when((p == 3) & (i == n_t - 1))
        def _():
            for arr in range(n_arr):
                for k in range(4):
                    copy(arr, k, my_id, me, shard_refs[arr]).wait_send()
                for j, chip in enumerate(chips):
                    copy(arr, 4 + j, _block_id(chip, c), me).wait_send()

        tile = pl.ds(pl.multiple_of(i * tm, tm), tm)

        @pl.when(p == 0)
        def _():
            xv = x_ref[...]
            h = (xv * _rms(xv) * g_ref[...]).astype(BF16)
            h_ref[...] = h
            h_s[tile, :] = h

        u_ref[...] = _dot_nt(h_s[tile, :], wbuf[...])

    first_pass = lambda p, i, o: (jnp.where(p == 0, i, n_t - 1), 0)
    grid_spec = pltpu.PrefetchScalarGridSpec(
        num_scalar_prefetch=1, grid=(4, n_t),
        in_specs=[pl.BlockSpec((tm, D_MODEL), first_pass), pl.BlockSpec((1, D_MODEL), lambda p, i, o: (0, 0))]
        + [HBM_SPEC] * n_arr,
        out_specs=[pl.BlockSpec((tm, width), lambda p, i, o: (i, o[p])), pl.BlockSpec((tm, D_MODEL), first_pass)]
        + [HBM_SPEC] * n_arr,
        scratch_shapes=[pltpu.VMEM((t_len, D_MODEL), BF16), pltpu.VMEM((width, D_MODEL), BF16)]
        + _exchange_scratch(n_arr, 7) + [pltpu.SemaphoreType.DMA((n_arr,)), pltpu.SemaphoreType.DMA((1,))])
    return pl.pallas_call(
        body, grid_spec=grid_spec,
        out_shape=[jax.ShapeDtypeStruct((t_len, IN_COLS), F32), jax.ShapeDtypeStruct((t_len, D_MODEL), BF16)]
        + [jax.ShapeDtypeStruct((N_DEV * s.shape[0], s.shape[1]), s.dtype) for s in shards],
        compiler_params=_params(("arbitrary", "arbitrary"), 48), name="in_proj",
    )(order, x, g_mix, *shards)


def _conv3_chunk(u_ref, r, cv_prev, cw, row):
    gb = u_ref[pl.ds(r, SUB), OFF_GB:OFF_GB + CONV_WIDTH]
    gc = u_ref[pl.ds(r, SUB), OFF_GC:OFF_GC + CONV_WIDTH]
    v = u_ref[pl.ds(r, SUB), OFF_V:OFF_V + CONV_WIDTH]
    cv = gc * v
    cv_m1 = _down(cv, cv_prev, 1, row)
    cv_m2 = _down(cv, cv_prev, 2, row)
    cq = cw[2:3, :] * cv + cw[1:2, :] * cv_m1 + cw[0:1, :] * cv_m2
    return gb, gc, v, cv, cv_m1, cv_m2, cq


def _conv4_chunk(u_ref, r, xin_prev, rw, rb, row):
    xin = u_ref[pl.ds(r, SUB), OFF_XR:OFF_XR + LRU_WIDTH]
    m1 = _down(xin, xin_prev, 1, row)
    m2 = _down(xin, xin_prev, 2, row)
    m3 = _down(xin, xin_prev, 3, row)
    xr = rw[3:4, :] * xin + rw[2:3, :] * m1 + rw[1:2, :] * m2 + rw[0:1, :] * m3 + rb
    return xin, m1, m2, m3, xr


def _mixer_fwd(u, conv_w, rnn_conv_w, rnn_conv_b, wa, b_a, wx, b_x, lam, gnc, gnr, shards, tm):
    t_len = u.shape[0]
    n_steps = t_len // tm
    n_chunks = tm // SUB
    n_arr = len(shards)

    def body(u_ref, cw_ref, rw_ref, rb_ref, wa_ref, ba_ref, wx_ref, bx_ref, lam_ref, gnc_ref, gnr_ref, *rest):
        shard_refs = rest[0:n_arr]
        hs_ref, y_ref, xr_s, ra_ref, ii_ref, mult_ref = rest[n_arr:n_arr + 6]
        fulls = rest[n_arr + 6:2 * n_arr + 6]
        (y_s, pa_s, px_s, wabd, wxbd, cv_car, xin_car, h_car,
         send_sems, recv_sems, local_sems) = rest[2 * n_arr + 6:]
        _host_all_gather(pl.program_id(0), n_steps, shard_refs, fulls, send_sems, recv_sems, local_sems)

        @pl.when(pl.program_id(0) == 0)
        def _():
            cv_car[...] = jnp.zeros(cv_car.shape, F32)
            xin_car[...] = jnp.zeros(xin_car.shape, F32)
            h_car[...] = jnp.zeros(h_car.shape, F32)
            wabd[...] = _expand_heads(wa_ref[...])
            wxbd[...] = _expand_heads(wx_ref[...])

        row_c = lax.broadcasted_iota(jnp.int32, (SUB, CONV_WIDTH), 0)
        row_r = lax.broadcasted_iota(jnp.int32, (SUB, LRU_WIDTH), 0)
        cw = cw_ref[...]
        rw = rw_ref[...]
        rb = rb_ref[...]
        g_c = gnc_ref[...]
        g_r = gnr_ref[...]
        sp_c = LRU_C * _softplus_neg(lam_ref[...])

        def convs(i, carry):
            cv_prev, xin_prev = carry
            r = pl.multiple_of(i * SUB, SUB)
            gb, _, _, cv, _, _, cq = _conv3_chunk(u_ref, r, cv_prev, cw, row_c)
            y_c = gb * cq
            y_s[pl.ds(r, SUB), 0:CONV_WIDTH] = y_c * _rms(y_c) * g_c
            xin, _, _, _, xr = _conv4_chunk(u_ref, r, xin_prev, rw, rb, row_r)
            xr_s[pl.ds(r, SUB), :] = xr
            return cv, xin

        cv_last, xin_last = _chunk_loop(n_chunks, convs, (cv_car[...], xin_car[...]))
        cv_car[...] = cv_last
        xin_car[...] = xin_last

        xrb = xr_s[...].astype(BF16)
        pa_s[...] = _block_diag_apply(xrb, wabd) + ba_ref[...]
        px_s[...] = _block_diag_apply(xrb, wxbd) + bx_ref[...]

        def recur(i, h_prev):
            r = pl.multiple_of(i * SUB, SUB)
            xr = xr_s[pl.ds(r, SUB), :]
            ra, ii, a, mult, _ = _lru_gates(pa_s[pl.ds(r, SUB), :], px_s[pl.ds(r, SUB), :], sp_c)
            ra_ref[pl.ds(r, SUB), :] = ra
            ii_ref[pl.ds(r, SUB), :] = ii
            mult_ref[pl.ds(r, SUB), :] = mult
            a_cum, b_cum = _scan8_fwd(a, mult * ii * xr, row_r)
            h = a_cum * h_prev + b_cum
            hs_ref[pl.ds(r, SUB), :] = h
            ge, _ = _gelu(u_ref[pl.ds(r, SUB), OFF_G:OFF_G + LRU_WIDTH])
            y_r = h * ge
            y_s[pl.ds(r, SUB), CONV_WIDTH:MIX_WIDTH] = y_r * _rms(y_r) * g_r
            return h[SUB - 1:SUB, :]

        h_car[...] = _chunk_loop(n_chunks, recur, h_car[...])

        y_ref[...] = y_s[...].astype(BF16)

    row_tile = lambda w: pl.BlockSpec((tm, w), lambda i: (i, 0))
    whole = lambda a: pl.BlockSpec(a.shape, lambda i: (0,) * a.ndim)
    smalls = (conv_w, rnn_conv_w, rnn_conv_b, wa, b_a, wx, b_x, lam, gnc, gnr)
    return pl.pallas_call(
        body, grid=(n_steps,),
        in_specs=[row_tile(IN_COLS)] + [whole(a) for a in smalls] + [HBM_SPEC] * n_arr,
        out_specs=[row_tile(LRU_WIDTH), row_tile(MIX_WIDTH)] + [row_tile(LRU_WIDTH)] * 4 + [HBM_SPEC] * n_arr,
        out_shape=[jax.ShapeDtypeStruct((t_len, LRU_WIDTH), F32), jax.ShapeDtypeStruct((t_len, MIX_WIDTH), BF16)]
        + [jax.ShapeDtypeStruct((t_len, LRU_WIDTH), F32)] * 4
        + [jax.ShapeDtypeStruct((N_DEV,) + s.shape, BF16) for s in shards],
        scratch_shapes=[pltpu.VMEM((tm, MIX_WIDTH), F32),
                        pltpu.VMEM((tm, LRU_WIDTH), F32), pltpu.VMEM((tm, LRU_WIDTH), F32),
                        pltpu.VMEM((LRU_WIDTH, GROUP), BF16), pltpu.VMEM((LRU_WIDTH, GROUP), BF16),
                        pltpu.VMEM((SUB, CONV_WIDTH), F32), pltpu.VMEM((SUB, LRU_WIDTH), F32),
                        pltpu.VMEM((1, LRU_WIDTH), F32)]
        + _exchange_scratch(n_arr, 7) + [pltpu.SemaphoreType.DMA((n_arr,))],
        compiler_params=_params(("arbitrary",), 56), name="mixer_fwd",
    )(u, *smalls, *shards)


def _mlp_fwd_bwd(x, y, target, g_mlp, g_f, w_out, w1, w2, tm):
    t_len = x.shape[0]
    n_steps = t_len // tm
    n_blk, _, blk = w1.shape

    def body(x_ref, y_ref, tg_ref, gm_ref, gf_ref, wout_hbm, w1_hbm, w2_hbm,
             dx1_ref, h2_ref, dx2_ref, vec_ref, z_hbm, dpre_hbm,
             wout_s, w1_s, w2_s, rp_s, z_s, dp_s, sem, out_sem):
        step = pl.program_id(0)
        rows = pl.ds(pl.multiple_of(step * tm, tm), tm)
        z_out = pltpu.make_async_copy(z_s, z_hbm.at[rows, :], out_sem.at[0])
        dp_out = pltpu.make_async_copy(dp_s, dpre_hbm.at[rows, :], out_sem.at[1])

        @pl.when(step == 0)
        def _():
            loads = [pltpu.make_async_copy(src, dst, sem.at[k])
                     for k, (src, dst) in enumerate(((wout_hbm, wout_s), (w1_hbm, w1_s), (w2_hbm, w2_s)))]
            for cp in loads:
                cp.start()
            vec_ref[...] = jnp.zeros(vec_ref.shape, F32)
            for cp in loads:
                cp.wait()

        x1v = x_ref[...] + jnp.dot(y_ref[...], wout_s[...], preferred_element_type=F32)
        g_m = gm_ref[...]
        g_o = gf_ref[...]
        r2 = _rms(x1v)
        x1h = x1v * r2
        h2 = (x1h * g_m).astype(BF16)
        h2_ref[...] = h2
        x2 = x1v

        @pl.when(step > 0)
        def _():
            z_out.wait()

        for k in range(n_blk):
            rp = jnp.maximum(jnp.dot(h2, w1_s[k], preferred_element_type=F32), 0.0)
            rp_s[:, k * blk:(k + 1) * blk] = rp.astype(BF16)
            zb = (rp * rp).astype(BF16)
            z_s[:, k * blk:(k + 1) * blk] = zb
            x2 = x2 + jnp.dot(zb, w2_s[k * blk:(k + 1) * blk, :], preferred_element_type=F32)
        z_out.start()
        r3 = _rms(x2)
        x2h = x2 * r3
        err = x2h * g_o - tg_ref[...]
        dout = err * (1.0 / D_MODEL)
        vec_ref[ROW_LOSS:ROW_LOSS + 1, :] += (0.5 / D_MODEL) * jnp.sum(err * err, axis=0, keepdims=True)
        vec_ref[ROW_GF:ROW_GF + 1, :] += jnp.sum(dout * x2h, axis=0, keepdims=True)
        dx2 = _rms_bwd(dout, x2h, r3, g_o)
        dx2b = dx2.astype(BF16)
        dx2_ref[...] = dx2b
        dh2 = jnp.zeros((tm, D_MODEL), F32)

        @pl.when(step > 0)
        def _():
            dp_out.wait()

        for k in range(n_blk):
            dz = _dot_nt(dx2b, w2_s[k * blk:(k + 1) * blk, :])
            dpb = (dz * 2.0 * rp_s[:, k * blk:(k + 1) * blk].astype(F32)).astype(BF16)
            dp_s[:, k * blk:(k + 1) * blk] = dpb
            dh2 = dh2 + _dot_nt(dpb, w1_s[k])
        dp_out.start()
        vec_ref[ROW_GMLP:ROW_GMLP + 1, :] += jnp.sum(dh2 * x1h, axis=0, keepdims=True)
        dx1_ref[...] = dx2 + _rms_bwd(dh2, x1h, r2, g_m)

        @pl.when(step == n_steps - 1)
        def _():
            z_out.wait()
            dp_out.wait()

    row_tile = lambda w: pl.BlockSpec((tm, w), lambda i: (i, 0))
    vec_spec = pl.BlockSpec((1, D_MODEL), lambda i: (0, 0))
    outs = pl.pallas_call(
        body, grid=(n_steps,),
        in_specs=[row_tile(D_MODEL), row_tile(MIX_WIDTH), row_tile(D_MODEL), vec_spec, vec_spec,
                  HBM_SPEC, HBM_SPEC, HBM_SPEC],
        out_specs=[row_tile(D_MODEL), row_tile(D_MODEL), row_tile(D_MODEL),
                   pl.BlockSpec((SUB, D_MODEL), lambda i: (0, 0)), HBM_SPEC, HBM_SPEC],
        out_shape=[jax.ShapeDtypeStruct((t_len, D_MODEL), F32), jax.ShapeDtypeStruct((t_len, D_MODEL), BF16),
                   jax.ShapeDtypeStruct((t_len, D_MODEL), BF16), jax.ShapeDtypeStruct((SUB, D_MODEL), F32),
                   jax.ShapeDtypeStruct((t_len, D_FF), BF16), jax.ShapeDtypeStruct((t_len, D_FF), BF16)],
        scratch_shapes=[pltpu.VMEM(w_out.shape, BF16), pltpu.VMEM(w1.shape, BF16), pltpu.VMEM(w2.shape, BF16),
                        pltpu.VMEM((tm, D_FF), BF16), pltpu.VMEM((tm, D_FF), BF16), pltpu.VMEM((tm, D_FF), BF16),
                        pltpu.SemaphoreType.DMA((3,)), pltpu.SemaphoreType.DMA((2,))],
        compiler_params=_params(("arbitrary",), 58), name="mlp_fwd_bwd",
    )(x, y, target, g_mlp, g_f, w_out, w1, w2)
    dx1, h2, dx2, vec, z, dpre = outs
    return dx1, z, dpre, h2, dx2, vec


def _mixer_bwd(u, hs, dx1, saved, conv_w, rnn_conv_w, rnn_conv_b, wa, b_a, wx, b_x, lam, gnc, gnr, w_out,
               chip_sums, g_wout, tm):
    t_len = u.shape[0]
    n_tiles = t_len // tm
    n_chunks = tm // SUB
    per_tile = tm // SUB
    n_sums = len(chip_sums)

    def body(u_ref, up_ref, hs_ref, hp_ref, dx1_ref, xr_ref, ra_ref, ii_ref, mult_ref,
             cw_ref, rw_ref, rb_ref, wa_ref, ba_ref, wx_ref, bx_ref, lam_ref, gnc_ref, gnr_ref, wout_ref, *rest):
        hsends = rest[0:n_sums]
        gwout_ref = rest[n_sums]
        du_ref, vec_ref, wab_ref = rest[n_sums + 1:n_sums + 4]
        hrecvs = rest[n_sums + 4:2 * n_sums + 4]
        sib_wout = rest[2 * n_sums + 4]
        (du_s, dy_s, dpa_s, dpx_s, dxr_s, wabd, wxbd, acc, dwa_acc, dwx_acc,
         a_car, dh_car, dcq_car, dxr_car, i_send, i_recv, d_send, d_recv) = rest[2 * n_sums + 5:]
        step = pl.program_id(0)
        _host_chip_exchange(step, n_tiles, hsends, hrecvs, i_send, i_recv)
        _host_pair_exchange(step, n_tiles, [gwout_ref], [sib_wout], d_send, d_recv)
        has_prev = (step < n_tiles - 1).astype(F32)

        @pl.when(step == 0)
        def _():
            acc[...] = jnp.zeros(acc.shape, F32)
            dwa_acc[...] = jnp.zeros(dwa_acc.shape, F32)
            dwx_acc[...] = jnp.zeros(dwx_acc.shape, F32)
            a_car[...] = jnp.ones(a_car.shape, F32)
            dh_car[...] = jnp.zeros(dh_car.shape, F32)
            dcq_car[...] = jnp.zeros(dcq_car.shape, F32)
            dxr_car[...] = jnp.zeros(dxr_car.shape, F32)
            wabd[...] = _expand_heads(wa_ref[...])
            wxbd[...] = _expand_heads(wx_ref[...])

        row_c = lax.broadcasted_iota(jnp.int32, (SUB, CONV_WIDTH), 0)
        row_r = lax.broadcasted_iota(jnp.int32, (SUB, LRU_WIDTH), 0)
        cw = cw_ref[...]
        rw = rw_ref[...]
        rb = rb_ref[...]
        g_c = gnc_ref[...]
        g_r = gnr_ref[...]
        sp_c = LRU_C * _softplus_neg(lam_ref[...])

        up = up_ref[...] * has_prev
        cv_before = up[:, OFF_GC:OFF_GC + CONV_WIDTH] * up[:, OFF_V:OFF_V + CONV_WIDTH]
        xin_before = up[:, OFF_XR:OFF_XR + LRU_WIDTH]
        hs_before = hp_ref[...] * has_prev

        dy_s[...] = _dot_nt(dx1_ref[...].astype(BF16), wout_ref[...])

        xrb = xr_ref[...].astype(BF16)

        def recur_bwd(j, carry):
            a_later, dh_later = carry
            i = n_chunks - 1 - j
            r = pl.multiple_of(i * SUB, SUB)
            rp = pl.multiple_of(jnp.maximum(i - 1, 0) * SUB, SUB)
            xr = xr_ref[pl.ds(r, SUB), :]
            hs_c = hs_ref[pl.ds(r, SUB), :]
            hs_prev = jnp.where(i == 0, hs_before, hs_ref[pl.ds(rp, SUB), :])
            h_m1 = _down(hs_c, hs_prev, 1, row_r)
            ra = ra_ref[pl.ds(r, SUB), :]
            ii = ii_ref[pl.ds(r, SUB), :]
            mult = mult_ref[pl.ds(r, SUB), :]
            a = jnp.exp(-ra * sp_c)
            inv_mult = lax.rsqrt(mult * mult)
            ge, dge = _gelu(u_ref[pl.ds(r, SUB), OFF_G:OFF_G + LRU_WIDTH])
            y_r = hs_c * ge
            rr = _rms(y_r)
            yhat = y_r * rr
            dyn = dy_s[pl.ds(r, SUB), CONV_WIDTH:MIX_WIDTH]
            acc[ACC_GNR] += dyn * yhat
            dy_r = _rms_bwd(dyn, yhat, rr, g_r)
            du_s[pl.ds(r, SUB), OFF_G:OFF_G + LRU_WIDTH] = dy_r * hs_c * dge
            a_cum, d_cum = _scan8_rev(_up(a, a_later, 1, row_r), dy_r * ge, row_r)
            dh = a_cum * dh_later + d_cum
            dmult = dh * ii * xr
            dii = dh * mult * xr
            dxr_s[pl.ds(r, SUB), :] = dh * mult * ii
            dla = dh * h_m1 * a - dmult * a * a * inv_mult
            acc[ACC_SP] += -dla * ra
            dpa = -dla * sp_c * ra * (1.0 - ra)
            dpx = dii * ii * (1.0 - ii)
            acc[ACC_BA] += dpa
            acc[ACC_BX] += dpx
            dpa_s[pl.ds(r, SUB), :] = dpa
            dpx_s[pl.ds(r, SUB), :] = dpx
            return a, dh[0:1, :]

        a_first, dh_first = _chunk_loop(n_chunks, recur_bwd, (a_car[...], dh_car[...]), in_flight=8)
        a_car[...] = a_first
        dh_car[...] = dh_first

        dpab = dpa_s[...].astype(BF16)
        dpxb = dpx_s[...].astype(BF16)
        dxr_s[...] += _block_diag_apply_t(dpab, wabd) + _block_diag_apply_t(dpxb, wxbd)
        for g in range(LRU_WIDTH // GROUP):
            cols = slice(g * GROUP, (g + 1) * GROUP)
            dwa_acc[cols, :] += _dot_tn(xrb[:, cols], dpab[:, cols])
            dwx_acc[cols, :] += _dot_tn(xrb[:, cols], dpxb[:, cols])

        def convs_bwd(j, carry):
            dcq_later, dxr_later = carry
            i = n_chunks - 1 - j
            r = pl.multiple_of(i * SUB, SUB)
            rp = pl.multiple_of(jnp.maximum(i - 1, 0) * SUB, SUB)
            cv_prev = jnp.where(i == 0, cv_before,
                                u_ref[pl.ds(rp, SUB), OFF_GC:OFF_GC + CONV_WIDTH]
                                * u_ref[pl.ds(rp, SUB), OFF_V:OFF_V + CONV_WIDTH])
            gb, gc, v, cv, cv_m1, cv_m2, cq = _conv3_chunk(u_ref, r, cv_prev, cw, row_c)
            y_c = gb * cq
            rc = _rms(y_c)
            yhat = y_c * rc
            dyn = dy_s[pl.ds(r, SUB), 0:CONV_WIDTH]
            acc[ACC_GNC, :, 0:CONV_WIDTH] += dyn * yhat
            dy_c = _rms_bwd(dyn, yhat, rc, g_c)
            dcq = dy_c * gb
            dcv = (cw[2:3, :] * dcq + cw[1:2, :] * _up(dcq, dcq_later, 1, row_c)
                   + cw[0:1, :] * _up(dcq, dcq_later, 2, row_c))
            acc[ACC_CW + 2, :, 0:CONV_WIDTH] += dcq * cv
            acc[ACC_CW + 1, :, 0:CONV_WIDTH] += dcq * cv_m1
            acc[ACC_CW + 0, :, 0:CONV_WIDTH] += dcq * cv_m2
            du_s[pl.ds(r, SUB), OFF_GB:OFF_GB + CONV_WIDTH] = dy_c * cq
            du_s[pl.ds(r, SUB), OFF_GC:OFF_GC + CONV_WIDTH] = dcv * v
            du_s[pl.ds(r, SUB), OFF_V:OFF_V + CONV_WIDTH] = dcv * gc

            xin_prev = jnp.where(i == 0, xin_before, u_ref[pl.ds(rp, SUB), OFF_XR:OFF_XR + LRU_WIDTH])
            xin, m1, m2, m3, _ = _conv4_chunk(u_ref, r, xin_prev, rw, rb, row_r)
            dxr = dxr_s[pl.ds(r, SUB), :]
            du_s[pl.ds(r, SUB), OFF_XR:OFF_XR + LRU_WIDTH] = (
                rw[3:4, :] * dxr + rw[2:3, :] * _up(dxr, dxr_later, 1, row_r)
                + rw[1:2, :] * _up(dxr, dxr_later, 2, row_r) + rw[0:1, :] * _up(dxr, dxr_later, 3, row_r))
            acc[ACC_RW + 3] += dxr * xin
            acc[ACC_RW + 2] += dxr * m1
            acc[ACC_RW + 1] += dxr * m2
            acc[ACC_RW + 0] += dxr * m3
            acc[ACC_BR] += dxr
            return dcq, dxr

        dcq_first, dxr_first = _chunk_loop(n_chunks, convs_bwd, (dcq_car[...], dxr_car[...]), in_flight=8)
        dcq_car[...] = dcq_first
        dxr_car[...] = dxr_first

        du_ref[...] = du_s[...].astype(BF16)

        @pl.when(step == n_tiles - 1)
        def _():
            vec_ref[...] = jnp.zeros(vec_ref.shape, F32)
            rows = {ACC_GNC: ROW_GNC, ACC_GNR: ROW_GNR, ACC_BR: ROW_BR, ACC_BA: ROW_BA, ACC_BX: ROW_BX}
            for k in range(3):
                rows[ACC_CW + k] = ROW_CW + k
            for k in range(4):
                rows[ACC_RW + k] = ROW_RW + k
            for slot, out_row in rows.items():
                o = out_row - ROW_GNC
                vec_ref[o:o + 1, :] = jnp.sum(acc[slot], axis=0, keepdims=True)
            lam_v = lam_ref[...]
            dsp = jnp.sum(acc[ACC_SP], axis=0, keepdims=True)
            o = ROW_LAM - ROW_GNC
            vec_ref[o:o + 1, :] = -dsp * LRU_C / (1.0 + jnp.exp(lam_v))
            wab_ref[0:LRU_WIDTH, :] = _fold_heads(dwa_acc[...])
            wab_ref[LRU_WIDTH:2 * LRU_WIDTH, :] = _fold_heads(dwx_acc[...])

    rev = lambda w: pl.BlockSpec((tm, w), lambda s: (n_tiles - 1 - s, 0))
    before = lambda w: pl.BlockSpec((SUB, w), lambda s: (jnp.maximum((n_tiles - 1 - s) * per_tile - 1, 0), 0))
    whole = lambda a: pl.BlockSpec(a.shape, lambda s: (0,) * a.ndim)
    smalls = (conv_w, rnn_conv_w, rnn_conv_b, wa, b_a, wx, b_x, lam, gnc, gnr, w_out)
    full = lambda w: pltpu.VMEM((tm, w), F32)
    return pl.pallas_call(
        body, grid=(n_tiles,),
        in_specs=[rev(IN_COLS), before(IN_COLS), rev(LRU_WIDTH), before(LRU_WIDTH), rev(D_MODEL)]
        + [rev(LRU_WIDTH)] * len(saved) + [whole(a) for a in smalls] + [HBM_SPEC] * (n_sums + 1),
        out_specs=[rev(IN_COLS), pl.BlockSpec((16, D_MODEL), lambda s: (0, 0)),
                   pl.BlockSpec((2 * LRU_WIDTH, HEAD_DIM), lambda s: (0, 0))] + [HBM_SPEC] * (n_sums + 1),
        out_shape=[jax.ShapeDtypeStruct((t_len, IN_COLS), BF16), jax.ShapeDtypeStruct((16, D_MODEL), F32),
                   jax.ShapeDtypeStruct((2 * LRU_WIDTH, HEAD_DIM), F32)]
        + [jax.ShapeDtypeStruct(s.shape, BF16) for s in chip_sums]
        + [jax.ShapeDtypeStruct((4,) + g_wout.shape[1:], BF16)],
        scratch_shapes=[full(IN_COLS), full(MIX_WIDTH), full(LRU_WIDTH), full(LRU_WIDTH), full(LRU_WIDTH),
                        pltpu.VMEM((LRU_WIDTH, GROUP), BF16), pltpu.VMEM((LRU_WIDTH, GROUP), BF16),
                        pltpu.VMEM((N_ACC, SUB, LRU_WIDTH), F32),
                        pltpu.VMEM((LRU_WIDTH, GROUP), F32), pltpu.VMEM((LRU_WIDTH, GROUP), F32),
                        pltpu.VMEM((SUB, LRU_WIDTH), F32), pltpu.VMEM((1, LRU_WIDTH), F32),
                        pltpu.VMEM((SUB, CONV_WIDTH), F32), pltpu.VMEM((SUB, LRU_WIDTH), F32)]
        + _exchange_scratch(n_sums, 3) + _exchange_scratch(1, 4),
        compiler_params=_params(("arbitrary",), 56), name="mixer_bwd",
    )(u, u, hs, hs, dx1, *saved, *smalls, *chip_sums, g_wout)


def _in_proj_bwd(du, dx1, x, g_mix, win_t, tm, chip_sums, g_own):
    t_len = x.shape[0]
    n_steps = t_len // tm

    def body(du_ref, dx1_ref, x_ref, g_ref, w_ref, hs_ref, gown_ref,
             dx_ref, vec_ref, landed_ref, sib_ref, i_send, i_recv, d_send, d_recv):
        step = pl.program_id(0)
        _host_chip_exchange(step, n_steps, [hs_ref], [landed_ref], i_send, i_recv)
        _host_half_exchange(step, n_steps, gown_ref, sib_ref, d_send, d_recv)

        @pl.when(step == 0)
        def _():
            vec_ref[...] = jnp.zeros(vec_ref.shape, F32)

        dh = jnp.dot(du_ref[...], w_ref[...], preferred_element_type=F32)
        xv = x_ref[...]
        r1 = _rms(xv)
        xh = xv * r1
        vec_ref[0:1, :] += jnp.sum(dh * xh, axis=0, keepdims=True)
        dx_ref[...] = dx1_ref[...] + _rms_bwd(dh, xh, r1, g_ref[...])

    row_tile = lambda w: pl.BlockSpec((tm, w), lambda i: (i, 0))
    half_shape = (g_own.shape[0], g_own.shape[1] // 2, g_own.shape[2])
    return pl.pallas_call(
        body, grid=(n_steps,),
        in_specs=[row_tile(IN_COLS), row_tile(D_MODEL), row_tile(D_MODEL), pl.BlockSpec((1, D_MODEL), lambda i: (0, 0)),
                  pl.BlockSpec((IN_COLS, D_MODEL), lambda i: (0, 0))] + [HBM_SPEC] * 2,
        out_specs=[row_tile(D_MODEL), pl.BlockSpec((SUB, D_MODEL), lambda i: (0, 0))] + [HBM_SPEC] * 2,
        out_shape=[jax.ShapeDtypeStruct((t_len, D_MODEL), F32), jax.ShapeDtypeStruct((SUB, D_MODEL), F32),
                   jax.ShapeDtypeStruct(chip_sums.shape, BF16), jax.ShapeDtypeStruct(half_shape, BF16)],
        scratch_shapes=_exchange_scratch(1, 3) + [pltpu.SemaphoreType.DMA((1,)), pltpu.SemaphoreType.DMA((1,))],
        compiler_params=_params(("arbitrary",), 56), name="in_proj_bwd",
    )(du, dx1, x, g_mix, win_t, chip_sums, g_own)


def _tn_weight_grad(a, b, tk, name, pair=(), col_blocks=1):
    t_len, m = a.shape
    n = b.shape[1]
    n_steps = t_len // tk
    sent = tuple(pair)
    n_sent = len(sent)

    def body(a_ref, b_ref, *rest):
        srcs = rest[0:n_sent]
        o_ref = rest[n_sent]
        dsts = rest[n_sent + 1:2 * n_sent + 1]
        acc = rest[2 * n_sent + 1]
        sems = rest[2 * n_sent + 2:]
        j = pl.program_id(0)
        if pair:
            _host_pair_exchange(j, n_steps, srcs, dsts, *sems)

        @pl.when(j == 0)
        def _():
            acc[...] = jnp.zeros(acc.shape, F32)

        acc[...] += _dot_tn(a_ref[...].astype(BF16), b_ref[...].astype(BF16))

        @pl.when(j == n_steps - 1)
        def _():
            if col_blocks == 1:
                o_ref[...] = acc[...].astype(BF16)
            else:
                for k in range(col_blocks):
                    o_ref[k] = acc[:, k * nb:(k + 1) * nb].astype(BF16)

    nb = n // col_blocks
    out_dims = (m, n) if col_blocks == 1 else (col_blocks, m, nb)
    landed = [jax.ShapeDtypeStruct((4,) + g.shape[1:], BF16) for g in pair]
    scratch = [pltpu.VMEM((m, n), F32)]
    if n_sent:
        scratch += _exchange_scratch(n_sent, 4)
    return pl.pallas_call(
        body, grid=(n_steps,),
        in_specs=[pl.BlockSpec((tk, m), lambda j: (j, 0)), pl.BlockSpec((tk, n), lambda j: (j, 0))]
        + [HBM_SPEC] * n_sent,
        out_specs=[pl.BlockSpec(out_dims, lambda j: (0,) * len(out_dims))] + [HBM_SPEC] * n_sent,
        out_shape=[jax.ShapeDtypeStruct(out_dims, BF16)] + landed,
        scratch_shapes=scratch,
        compiler_params=_params(("arbitrary",), 56), name=name,
    )(a, b, *sent)


def _w_in_grad_part(du, h, tk, name, chip_ids, chip=(), halves=None, small=None):
    t_len = du.shape[0]
    n_t = t_len // tk
    n_q = chip_ids.shape[0]
    width = 2 * (IN_COLS // N_DEV)
    n_steps = n_q * n_t
    n_chip = len(chip)
    sent = tuple(chip) + (() if halves is None else (halves,)) + (() if small is None else tuple(small))
    n_sent = len(sent)

    def body(ids_ref, a_ref, b_ref, *rest):
        srcs = rest[0:n_sent]
        o_ref = rest[n_sent]
        dsts = rest[n_sent + 1:2 * n_sent + 1]
        acc = rest[2 * n_sent + 1]
        sems = list(rest[2 * n_sent + 2:])
        j = pl.program_id(1)
        step = pl.program_id(0) * n_t + j
        if chip:
            _host_chip_exchange(step, n_steps, srcs[0:n_chip], dsts[0:n_chip], sems.pop(0), sems.pop(0))
        if halves is not None:
            _host_half_exchange(step, n_steps, srcs[n_chip], dsts[n_chip], sems.pop(0), sems.pop(0))
        if small is not None:
            _host_small_exchange(step, n_steps, *srcs[n_sent - 3:], *dsts[n_sent - 3:], *sems)

        @pl.when(j == 0)
        def _():
            acc[...] = jnp.zeros(acc.shape, F32)

        acc[...] += _dot_tn(a_ref[...], b_ref[...])

        @pl.when(j == n_t - 1)
        def _():
            o_ref[0] = acc[...].astype(BF16)

    landed = [jax.ShapeDtypeStruct(s.shape, BF16) for s in chip]
    scratch = [pltpu.VMEM((width, D_MODEL), F32)]
    if chip:
        scratch += _exchange_scratch(len(chip), 3)
    if halves is not None:
        landed.append(jax.ShapeDtypeStruct((halves.shape[0], halves.shape[1] // 2, halves.shape[2]), BF16))
        scratch += [pltpu.SemaphoreType.DMA((halves.shape[0],)), pltpu.SemaphoreType.DMA((halves.shape[0],))]
    if small is not None:
        vec_m, vec_b, wab = small
        landed += [jax.ShapeDtypeStruct((N_DEV,) + vec_m.shape, F32), jax.ShapeDtypeStruct((N_DEV,) + vec_b.shape, F32),
                   jax.ShapeDtypeStruct((N_DEV, wab.shape[0] // N_DEV, wab.shape[1]), F32)]
        scratch += _exchange_scratch(3, N_DEV) + [pltpu.SemaphoreType.DMA((2,))]
    grid_spec = pltpu.PrefetchScalarGridSpec(
        num_scalar_prefetch=1, grid=(n_q, n_t),
        in_specs=[pl.BlockSpec((tk, width), lambda q, j, ids: (j, ids[q])),
                  pl.BlockSpec((tk, D_MODEL), lambda q, j, ids: (j, 0))] + [HBM_SPEC] * n_sent,
        out_specs=[pl.BlockSpec((1, width, D_MODEL), lambda q, j, ids: (q, 0, 0))] + [HBM_SPEC] * n_sent,
        scratch_shapes=scratch)
    return pl.pallas_call(
        body, grid_spec=grid_spec, out_shape=[jax.ShapeDtypeStruct((n_q, width, D_MODEL), BF16)] + landed,
        compiler_params=_params(("arbitrary", "arbitrary"), 40), name=name,
    )(chip_ids, du, h, *sent)


def _adamw(w, g, m, v):
    m = ADAM_B1 * m + (1.0 - ADAM_B1) * g
    v = ADAM_B2 * v + (1.0 - ADAM_B2) * (g * g)
    delta = -ADAM_LR * ((m / BC1) / (jnp.sqrt(v / BC2) + ADAM_EPS) + ADAM_WD * w)
    return delta, m, v


def _update_sharded(g, landed, w, m, v, rows_blk, name):
    rows, cols = w.shape

    def body(g_ref, l_ref, w_ref, m_ref, v_ref, og, od, om, ov):
        gv = g_ref[...]
        for j in range(3):
            gv = gv + l_ref[j].astype(F32)
        delta, mn, vn = _adamw(w_ref[...], gv, m_ref[...], v_ref[...])
        og[...] = gv
        od[...] = delta
        om[...] = mn
        ov[...] = vn

    blk = pl.BlockSpec((rows_blk, cols), lambda i: (i, 0))
    shape = pltpu.HBM((rows, cols), F32)
    return pl.pallas_call(
        body, grid=(rows // rows_blk,),
        in_specs=[blk, pl.BlockSpec((3, rows_blk, cols), lambda i: (0, i, 0)), blk, blk, blk],
        out_specs=[blk] * 4, out_shape=[shape] * 4,
        compiler_params=_params(("arbitrary",), 32), name=name,
    )(*_in_hbm(g, landed, w, m, v))


def _update_w_in(g_own, sib_own, landed, w, m, v, core, rows_blk):
    rows, cols = w.shape
    pad_cols = -(-cols // 128) * 128

    def body(core_ref, g_ref, s_ref, l_ref, w_ref, m_ref, v_ref, og, od, om, ov, padbuf, turned):
        gt = g_ref[0, 0].astype(F32) + s_ref[0].astype(F32)
        for j in range(3):
            gt = gt + l_ref[j].astype(F32)
        padbuf[...] = jnp.zeros(padbuf.shape, F32)
        padbuf[0:cols, :] = gt
        turned[...] = padbuf[...].T
        gv = turned[:, 0:cols]
        delta, mn, vn = _adamw(w_ref[...], gv, m_ref[...], v_ref[...])
        og[...] = gv
        od[...] = delta
        om[...] = mn
        ov[...] = vn

    blk = pl.BlockSpec((rows_blk, cols), lambda i, cr: (i, 0))
    grid_spec = pltpu.PrefetchScalarGridSpec(
        num_scalar_prefetch=1, grid=(rows // rows_blk,),
        in_specs=[pl.BlockSpec((1, 1, cols, rows_blk), lambda i, cr: (0, cr[0], 0, i)),
                  pl.BlockSpec((1, cols, rows_blk), lambda i, cr: (0, 0, i)),
                  pl.BlockSpec((3, cols, rows_blk), lambda i, cr: (0, 0, i)), blk, blk, blk],
        out_specs=[blk] * 4,
        scratch_shapes=[pltpu.VMEM((pad_cols, rows_blk), F32), pltpu.VMEM((rows_blk, pad_cols), F32)])
    return pl.pallas_call(
        body, grid_spec=grid_spec, out_shape=[pltpu.HBM((rows, cols), F32)] * 4,
        compiler_params=_params(("arbitrary",), 32), name="update_w_in",
    )(core, *_in_hbm(g_own.reshape(1, 2, cols, rows), sib_own, landed, w, m, v))


def _update_small(vsum, wsum, g_cw, g_rw, weights, moments_m, moments_v):
    n = len(weights)

    def body(*refs):
        vs, ws, gcw, grw = refs[0:4]
        w_refs = refs[4:4 + n]
        m_refs = refs[4 + n:4 + 2 * n]
        v_refs = refs[4 + 2 * n:4 + 3 * n]
        outs = refs[4 + 3 * n:]
        loss_ref = outs[0]
        loss_ref[...] = jnp.sum(vs[ROW_LOSS:ROW_LOSS + 1, :], axis=1, keepdims=True)
        grads = [
            vs[ROW_GMIX:ROW_GMIX + 1, :], gcw[...], grw[...], vs[ROW_BR:ROW_BR + 1, :],
            ws[0:LRU_WIDTH, :], vs[ROW_BA:ROW_BA + 1, :], ws[LRU_WIDTH:2 * LRU_WIDTH, :], vs[ROW_BX:ROW_BX + 1, :],
            vs[ROW_LAM:ROW_LAM + 1, :], vs[ROW_GNC:ROW_GNC + 1, 0:CONV_WIDTH], vs[ROW_GNR:ROW_GNR + 1, :],
            vs[ROW_GMLP:ROW_GMLP + 1, :], vs[ROW_GF:ROW_GF + 1, :],
        ]
        for k in range(n):
            gk = grads[k]
            delta, mn, vn = _adamw(w_refs[k][...], gk, m_refs[k][...], v_refs[k][...])
            outs[1 + 4 * k][...] = gk
            outs[2 + 4 * k][...] = delta
            outs[3 + 4 * k][...] = mn
            outs[4 + 4 * k][...] = vn

    whole = lambda a: pl.BlockSpec(a.shape, lambda i: (0,) * len(a.shape))
    out_shape = [jax.ShapeDtypeStruct((1, 1), F32)]
    for w in weights:
        out_shape += [jax.ShapeDtypeStruct(w.shape, F32)] * 4
    args = (vsum, wsum, g_cw, g_rw, *weights, *moments_m, *moments_v)
    return pl.pallas_call(
        body, grid=(1,), out_shape=out_shape, in_specs=[whole(a) for a in args], out_specs=[whole(s) for s in out_shape],
        compiler_params=_params(("arbitrary",), 32), name="update_small",
    )(*args)


def kernel(x, norm_mix_g, w_in, conv_w, rnn_conv_w, rnn_conv_b, w_a, b_a, w_x, b_x, lru_lambda, g_norm_conv, g_norm_rnn, w_out, norm_mlp_g, w_mlp_in, w_mlp_out, final_norm_g, loss_target, m_norm_mix_g, m_w_in, m_conv_w, m_rnn_conv_w, m_rnn_conv_b, m_w_a, m_b_a, m_w_x, m_b_x, m_lru_lambda, m_g_norm_conv, m_g_norm_rnn, m_w_out, m_norm_mlp_g, m_w_mlp_in, m_w_mlp_out, m_final_norm_g, v_norm_mix_g, v_w_in, v_conv_w, v_rnn_conv_w, v_rnn_conv_b, v_w_a, v_b_a, v_w_x, v_b_x, v_lru_lambda, v_g_norm_conv, v_g_norm_rnn, v_w_out, v_norm_mlp_g, v_w_mlp_in, v_w_mlp_out, v_final_norm_g):
    t_len = x.shape[1]
    my_id = 4 * lax.axis_index("x") + 2 * lax.axis_index("y") + lax.axis_index("c")
    tm = min(256, t_len)
    tb = min(512, t_len)
    tk = min(512, t_len)

    xs = x.reshape(t_len, D_MODEL)
    tgt = loss_target.reshape(t_len, D_MODEL)
    flat = lambda a: a.reshape(a.shape[-2:]) if a.ndim == 3 else a.reshape(1, -1)
    heads = lambda a: a.reshape(LRU_WIDTH, HEAD_DIM)

    win_shard, wout_shard, w1_shard, w2_shard, cp_shard = _prep_shards(
        flat(w_in), flat(w_out), flat(w_mlp_in), flat(w_mlp_out), flat(conv_w), flat(rnn_conv_w))

    u, h, win_t, cp_full = _in_proj(xs, flat(norm_mix_g), (win_shard, cp_shard), tb)
    cpack = cp_full.reshape(N_DEV, 8, 128)
    conv_full = jnp.transpose(cpack[:, 0:3, 0:64], (1, 0, 2)).reshape(3, CONV_WIDTH)
    rnn_full = jnp.transpose(cpack[:, 3:7, :], (1, 0, 2)).reshape(4, LRU_WIDTH)
    mixer_small = (conv_full, rnn_full, flat(rnn_conv_b), heads(w_a), flat(b_a), heads(w_x), flat(b_x),
                   flat(lru_lambda), flat(g_norm_conv), flat(g_norm_rnn))
    hs, y, xr, gate_r, gate_i, mult, w1_blk, w2_blk, wout_blk = _mixer_fwd(
        u, *mixer_small, (w1_shard, w2_shard, wout_shard), tm)
    wout_f = wout_blk.reshape(MIX_WIDTH, D_MODEL)
    dx1, z, dpre, h2, dx2, vec_m = _mlp_fwd_bwd(xs, y, tgt, flat(norm_mlp_g), flat(final_norm_g), wout_f, w1_blk,
                                                w2_blk.reshape(D_FF, D_MODEL), tb)
    (g_w1,) = _tn_weight_grad(h2, dpre, tk, "w_mlp_in_grad", col_blocks=N_DEV)
    (g_w2,) = _tn_weight_grad(z, dx2, tk, "w_mlp_out_grad")
    g_w2 = g_w2.reshape(N_DEV, D_FF // N_DEV, D_MODEL)
    g_wout, sib_w1, sib_w2 = _tn_weight_grad(y, dx1, tk, "w_out_grad", pair=(g_w1, g_w2))
    g_wout = g_wout.reshape(N_DEV, MIX_WIDTH // N_DEV, D_MODEL)
    hsend_w1, own_w1 = _pair_sum(g_w1, sib_w1, "pair_sum_w_mlp_in")
    hsend_w2, own_w2 = _pair_sum(g_w2, sib_w2, "pair_sum_w_mlp_out")
    du, vec_b, wab, landed_w1, landed_w2, sib_wout = _mixer_bwd(
        u, hs, dx1, (xr, gate_r, gate_i, mult), *mixer_small, wout_f, (hsend_w1, hsend_w2), g_wout, tm)
    hsend_wout, own_wout = _pair_sum(g_wout, sib_wout, "pair_sum_w_out")
    ax, ay, ac = lax.axis_index("x"), lax.axis_index("y"), lax.axis_index("c")
    chip_ids = jnp.stack([2 * cx + cy for cx, cy in [(ax, ay)] + _other_chips(ax, ay)]).astype(jnp.int32)
    core = jnp.reshape(ac, (1,)).astype(jnp.int32)
    tw = min(1024, t_len)
    g_others, landed_wout, vrecv_m, vrecv_b, wrecv = _w_in_grad_part(
        du, h, tw, "w_in_grad_others", chip_ids[1:4], chip=(hsend_wout,), small=(vec_m, vec_b, wab))
    g_own, sib_others = _w_in_grad_part(du, h, tw, "w_in_grad_own", chip_ids[0:1], halves=g_others)
    hsend_win = _pair_sum_parts(g_others, sib_others, core)
    grad_x, vec_x, landed_win, sib_own = _in_proj_bwd(du, dx1, xs, flat(norm_mix_g), win_t, tm, hsend_win, g_own)

    vsum, wsum = _final_small(vrecv_m, vrecv_b, wab, wrecv, vec_x)

    up_win = _update_w_in(g_own, sib_own, landed_win, flat(w_in), flat(m_w_in), flat(v_w_in), core, 256)
    up_wout = _update_sharded(own_wout, landed_wout, flat(w_out), flat(m_w_out), flat(v_w_out), 96, "update_w_out")
    up_w1 = _update_sharded(own_w1, landed_w1, flat(w_mlp_in), flat(m_w_mlp_in), flat(v_w_mlp_in), 256,
                            "update_w_mlp_in")
    up_w2 = _update_sharded(own_w2, landed_w2, flat(w_mlp_out), flat(m_w_mlp_out), flat(v_w_mlp_out), 256,
                            "update_w_mlp_out")

    g_cw = lax.dynamic_slice(vsum, (ROW_CW, 64 * my_id), (3, 64))
    g_rw = lax.dynamic_slice(vsum, (ROW_RW, 128 * my_id), (4, 128))
    small_w = (norm_mix_g, conv_w, rnn_conv_w, rnn_conv_b, w_a, b_a, w_x, b_x, lru_lambda, g_norm_conv, g_norm_rnn,
               norm_mlp_g, final_norm_g)
    small_m = (m_norm_mix_g, m_conv_w, m_rnn_conv_w, m_rnn_conv_b, m_w_a, m_b_a, m_w_x, m_b_x, m_lru_lambda,
               m_g_norm_conv, m_g_norm_rnn, m_norm_mlp_g, m_final_norm_g)
    small_v = (v_norm_mix_g, v_conv_w, v_rnn_conv_w, v_rnn_conv_b, v_w_a, v_b_a, v_w_x, v_b_x, v_lru_lambda,
               v_g_norm_conv, v_g_norm_rnn, v_norm_mlp_g, v_final_norm_g)
    is_heads = (False, False, False, False, True, False, True, False, False, False, False, False, False)
    as2d = lambda arrs: [heads(a) if hd else flat(a) for a, hd in zip(arrs, is_heads)]
    small_out = _update_small(vsum, wsum, g_cw, g_rw, as2d(small_w), as2d(small_m), as2d(small_v))
    loss = small_out[0].reshape(())

    names = ["norm_mix_g", "w_in", "conv_w", "rnn_conv_w", "rnn_conv_b", "w_a", "b_a", "w_x", "b_x", "lru_lambda",
             "g_norm_conv", "g_norm_rnn", "w_out", "norm_mlp_g", "w_mlp_in", "w_mlp_out", "final_norm_g"]
    originals = dict(zip(names, (norm_mix_g, w_in, conv_w, rnn_conv_w, rnn_conv_b, w_a, b_a, w_x, b_x, lru_lambda,
                                 g_norm_conv, g_norm_rnn, w_out, norm_mlp_g, w_mlp_in, w_mlp_out, final_norm_g)))
    results = {"w_in": up_win, "w_out": up_wout, "w_mlp_in": up_w1, "w_mlp_out": up_w2}
    small_names = ["norm_mix_g", "conv_w", "rnn_conv_w", "rnn_conv_b", "w_a", "b_a", "w_x", "b_x", "lru_lambda",
                   "g_norm_conv", "g_norm_rnn", "norm_mlp_g", "final_norm_g"]
    for k, nm in enumerate(small_names):
        results[nm] = small_out[1 + 4 * k:5 + 4 * k]
    out = [loss, grad_x.reshape(x.shape)]
    for kind in range(4):
        out += [results[nm][kind].reshape(originals[nm].shape) for nm in names]
    return tuple(out)
```

```python
import functools

import jax
import jax.numpy as jnp
from jax import lax
from jax.experimental import pallas as pl
from jax.experimental.pallas import tpu as pltpu

F32 = jnp.float32
BF16 = jnp.bfloat16

D_MODEL = 1024
HEAD_DIM = 64
CONV_WIDTH = 512
LRU_WIDTH = 1024
MIX_WIDTH = CONV_WIDTH + LRU_WIDTH
IN_COLS = 3 * CONV_WIDTH + 2 * LRU_WIDTH
D_FF = 4 * D_MODEL
GROUP = 256
EPS = 1e-6
LRU_C = 8.0
N_DEV = 8
SUB = 8

OFF_GB, OFF_GC, OFF_V, OFF_XR, OFF_G = 0, 512, 1024, 1536, 2560

ADAM_LR, ADAM_B1, ADAM_B2, ADAM_EPS, ADAM_WD, ADAM_STEP = 0.001, 0.9, 0.999, 1e-08, 0.01, 10
BC1 = 1.0 - ADAM_B1 ** ADAM_STEP
BC2 = 1.0 - ADAM_B2 ** ADAM_STEP

MIB = 1024 * 1024
MESH = pl.DeviceIdType.MESH

VEC_ROWS = 32
ROW_GF, ROW_GMLP, ROW_LOSS = 0, 1, 2
ROW_GNC, ROW_GNR, ROW_BR, ROW_BA, ROW_BX, ROW_LAM, ROW_CW, ROW_RW = 8, 9, 10, 11, 12, 13, 14, 17
ROW_GMIX = 24
ACC_GNC, ACC_GNR, ACC_BR, ACC_BA, ACC_BX, ACC_SP, ACC_CW, ACC_RW, N_ACC = 0, 1, 2, 3, 4, 5, 6, 9, 13


def _params(semantics=None, vmem_mib=48):
    return pltpu.CompilerParams(dimension_semantics=semantics, vmem_limit_bytes=vmem_mib * MIB)


def _rms(x):
    return lax.rsqrt(jnp.mean(x * x, axis=-1, keepdims=True) + EPS)


def _rms_bwd(dy, xhat, r, g):
    dyh = dy * g
    return r * (dyh - xhat * jnp.mean(dyh * xhat, axis=-1, keepdims=True))


def _sigmoid(x):
    return 0.5 + 0.5 * jnp.tanh(0.5 * x)


def _gelu(x):
    c0, c1 = 0.7978845608028654, 0.044715
    x2 = x * x
    t = jnp.tanh(x * (c0 + (c0 * c1) * x2))
    half = 0.5 + 0.5 * t
    ge = x * half
    dge = half + (0.5 * x) * (1.0 - t * t) * (c0 + (3.0 * c0 * c1) * x2)
    return ge, dge


def _softplus_neg(lam):
    z = -lam
    e = jnp.exp(-jnp.abs(z))
    return jnp.maximum(z, 0.0) + jnp.where(e < 1e-4, e * (1.0 - 0.5 * e), jnp.log(1.0 + e))


def _lru_gates(pa, px, sp_c):
    ra = _sigmoid(pa)
    ii = _sigmoid(px)
    la = -ra * sp_c
    a = jnp.exp(la)
    x2 = 2.0 * la
    series = -x2 * (1.0 + x2 * (0.5 + x2 * (1.0 / 6.0 + x2 * (1.0 / 24.0))))
    m2 = jnp.where(x2 > -0.01, series, 1.0 - a * a)
    inv_mult = lax.rsqrt(m2)
    mult = jnp.where(m2 > 0.0, m2 * inv_mult, 0.0)
    return ra, ii, a, mult, inv_mult


def _down(cur, prev, s, row):
    return jnp.where(row >= s, pltpu.roll(cur, s, 0), pltpu.roll(prev, s, 0))


def _up(cur, nxt, s, row):
    return jnp.where(row < SUB - s, pltpu.roll(cur, SUB - s, 0), pltpu.roll(nxt, SUB - s, 0))


def _scan8_fwd(a, b, row):
    for s in (1, 2, 4):
        m = row >= s
        a_sh = pltpu.roll(a, s, 0)
        b_sh = pltpu.roll(b, s, 0)
        b = jnp.where(m, a * b_sh + b, b)
        a = jnp.where(m, a * a_sh, a)
    return a, b


def _scan8_rev(a, b, row):
    for s in (1, 2, 4):
        m = row < SUB - s
        a_sh = pltpu.roll(a, SUB - s, 0)
        b_sh = pltpu.roll(b, SUB - s, 0)
        b = jnp.where(m, a * b_sh + b, b)
        a = jnp.where(m, a * a_sh, a)
    return a, b


def _group_mask(shape):
    r = lax.broadcasted_iota(jnp.int32, shape, 0)
    c = lax.broadcasted_iota(jnp.int32, shape, 1)
    return ((r % GROUP) // HEAD_DIM) == (c // HEAD_DIM)


def _expand_heads(w):
    j = lax.broadcasted_iota(jnp.int32, (HEAD_DIM, GROUP), 0)
    c = lax.broadcasted_iota(jnp.int32, (HEAD_DIM, GROUP), 1)
    spread = (c % HEAD_DIM == j).astype(BF16)
    e = jnp.dot(w.astype(BF16), spread, preferred_element_type=F32)
    return jnp.where(_group_mask(e.shape), e, 0.0).astype(BF16)


def _fold_heads(p):
    p = jnp.where(_group_mask(p.shape), p, 0.0)
    c = lax.broadcasted_iota(jnp.int32, (GROUP, HEAD_DIM), 0)
    j = lax.broadcasted_iota(jnp.int32, (GROUP, HEAD_DIM), 1)
    fold = (c % HEAD_DIM == j).astype(BF16)
    hi = p.astype(BF16)
    rest = p - hi.astype(F32)
    mid = rest.astype(BF16)
    lo = (rest - mid.astype(F32)).astype(BF16)
    dot = functools.partial(jnp.dot, preferred_element_type=F32)
    return dot(hi, fold) + dot(mid, fold) + dot(lo, fold)


def _block_diag_apply(xb, wbd_ref):
    parts = [jnp.dot(xb[:, g * GROUP:(g + 1) * GROUP], wbd_ref[g * GROUP:(g + 1) * GROUP, :],
                     preferred_element_type=F32) for g in range(LRU_WIDTH // GROUP)]
    return jnp.concatenate(parts, axis=1)


def _block_diag_apply_t(db, wbd_ref):
    parts = [lax.dot_general(db[:, g * GROUP:(g + 1) * GROUP], wbd_ref[g * GROUP:(g + 1) * GROUP, :],
                             (((1,), (1,)), ((), ())), preferred_element_type=F32)
             for g in range(LRU_WIDTH // GROUP)]
    return jnp.concatenate(parts, axis=1)


def _dot_nt(a, b):
    return lax.dot_general(a, b, (((1,), (1,)), ((), ())), preferred_element_type=F32)


def _dot_tn(a, b):
    return lax.dot_general(a, b, (((0,), (0,)), ((), ())), preferred_element_type=F32)


def _chunk_loop(n_chunks, chunk, init, in_flight=4):
    def body(k, carry):
        for j in range(in_flight):
            carry = chunk(k * in_flight + j, carry)
        return carry

    return lax.fori_loop(0, n_chunks // in_flight, body, init)


def _place():
    x, y, c = lax.axis_index("x"), lax.axis_index("y"), lax.axis_index("c")
    return x, y, c


def _block_id(chip, core):
    return 4 * chip[0] + 2 * chip[1] + core


def _other_chips(x, y):
    return [(1 - x, y), (x, 1 - y), (1 - x, 1 - y)]


def _remote_copy(src, dst, send_sem, recv_sem, to):
    return pltpu.make_async_remote_copy(src_ref=src, dst_ref=dst, send_sem=send_sem, recv_sem=recv_sem,
                                        device_id=to, device_id_type=MESH)


HBM_SPEC = pl.BlockSpec(memory_space=pl.ANY)


def _in_hbm(*arrays):
    return [pltpu.with_memory_space_constraint(a, pltpu.HBM) for a in arrays]


def _prep_shards(w_in, w_out, w_mlp_in, w_mlp_out, conv_w, rnn_conv_w):
    n_in = w_in.shape[1]

    def body(win_ref, wout_ref, w1_ref, w2_ref, cw_ref, rw_ref, o_win, o_wout, o_w1, o_w2, o_cp, padbuf):
        padbuf[...] = jnp.zeros(padbuf.shape, F32)
        padbuf[:, 0:n_in] = win_ref[...]
        o_win[...] = padbuf[...].T[0:n_in, :].astype(BF16)
        o_wout[...] = wout_ref[...].astype(BF16)
        o_w1[...] = w1_ref[...].astype(BF16)
        o_w2[...] = w2_ref[...].astype(BF16)
        o_cp[...] = jnp.zeros(o_cp.shape, F32)
        o_cp[0:3, 0:64] = cw_ref[...]
        o_cp[3:7, :] = rw_ref[...]

    whole = lambda shape: pl.BlockSpec(shape, lambda i: (0,) * len(shape))
    args = (w_in, w_out, w_mlp_in, w_mlp_out, conv_w, rnn_conv_w)
    shapes = [((n_in, D_MODEL), BF16), (w_out.shape, BF16), (w_mlp_in.shape, BF16), (w_mlp_out.shape, BF16),
              ((8, 128), F32)]
    return pl.pallas_call(
        body, grid=(1,), out_shape=[jax.ShapeDtypeStruct(s, d) for s, d in shapes],
        in_specs=[whole(a.shape) for a in args], out_specs=[whole(s) for s, _ in shapes],
        scratch_shapes=[pltpu.VMEM((D_MODEL, 512), F32)],
        compiler_params=_params(("arbitrary",), 40), name="prep_shards",
    )(*args)


def _host_all_gather(step, n_steps, shards, fulls, send_sems, recv_sems, local_sems):
    x, y, c = _place()
    me = (x, y, c)
    my_id = _block_id((x, y), c)
    sibling = (x, y, 1 - c)
    chips = _other_chips(x, y)
    n_arr = len(shards)

    def copy(arr, k, block, to, src=None):
        dst = fulls[arr].at[block]
        return _remote_copy(dst if src is None else src, dst, send_sems.at[arr, k], recv_sems.at[arr, k], to)

    def local(arr):
        return pltpu.make_async_copy(shards[arr], fulls[arr].at[my_id], local_sems.at[arr])

    @pl.when(step == 0)
    def _():
        for arr in range(n_arr):
            local(arr).start()
            copy(arr, 0, my_id, sibling, shards[arr]).start()
            for j, chip in enumerate(chips):
                copy(arr, 1 + j, my_id, (*chip, c), shards[arr]).start()

    @pl.when(step == max(n_steps - 2, 0))
    def _():
        for j, chip in enumerate(chips):
            for arr in range(n_arr):
                copy(arr, 1 + j, _block_id(chip, c), me).wait_recv()
                copy(arr, 4 + j, _block_id(chip, c), sibling).start()

    @pl.when(step == n_steps - 1)
    def _():
        for arr in range(n_arr):
            copy(arr, 0, _block_id((x, y), 1 - c), me).wait_recv()
            for j, chip in enumerate(chips):
                copy(arr, 4 + j, _block_id(chip, 1 - c), me).wait_recv()
            for k in range(4):
                copy(arr, k, my_id, me, shards[arr]).wait_send()
            for j, chip in enumerate(chips):
                copy(arr, 4 + j, _block_id(chip, c), me).wait_send()
            local(arr).wait()


def _host_pair_exchange(step, n_steps, gs, sibs, send_sems, recv_sems):
    x, y, c = _place()
    sibling = (x, y, 1 - c)
    chips = [(x, y)] + _other_chips(x, y)

    def d2d(arr, q):
        return _remote_copy(gs[arr].at[_block_id(chips[q], 1 - c)], sibs[arr].at[q],
                            send_sems.at[arr, q], recv_sems.at[arr, q], sibling)

    @pl.when(step == 0)
    def _():
        for arr in range(len(gs)):
            for q in (1, 2, 3, 0):
                d2d(arr, q).start()

    @pl.when(step == n_steps - 1)
    def _():
        for arr in range(len(gs)):
            for q in range(4):
                d2d(arr, q).wait()


def _host_chip_exchange(step, n_steps, hsends, hrecvs, send_sems, recv_sems):
    x, y, c = _place()
    chips = _other_chips(x, y)

    def ici(arr, j):
        return _remote_copy(hsends[arr].at[j], hrecvs[arr].at[j], send_sems.at[arr, j], recv_sems.at[arr, j],
                            (*chips[j], c))

    @pl.when(step == 0)
    def _():
        for arr in range(len(hsends)):
            for j in range(3):
                ici(arr, j).start()

    @pl.when(step == n_steps - 1)
    def _():
        for arr in range(len(hsends)):
            for j in range(3):
                ici(arr, j).wait()


def _host_half_exchange(step, n_steps, parts, sibs, send_sems, recv_sems):
    x, y, c = _place()
    n_q, rows2, _ = parts.shape
    half = rows2 // 2

    def d2d(q):
        src = parts.at[q, pl.ds(pl.multiple_of((1 - c) * half, 16), half), :]
        return _remote_copy(src, sibs.at[q], send_sems.at[q], recv_sems.at[q], (x, y, 1 - c))

    @pl.when(step == 0)
    def _():
        for q in range(n_q):
            d2d(q).start()

    @pl.when(step == n_steps - 1)
    def _():
        for q in range(n_q):
            d2d(q).wait()


def _peer(x, y, c, k):
    return (x ^ ((k >> 2) & 1), y ^ ((k >> 1) & 1), c ^ (k & 1))


def _host_small_exchange(step, n_steps, vec_m, vec_b, wab, vrecv_m, vrecv_b, wrecv, send_sems, recv_sems, local_sems):
    x, y, c = _place()
    my_id = _block_id((x, y), c)
    wrows = wab.shape[0] // N_DEV

    def copies(k):
        to = _peer(x, y, c, k)
        block = wab.at[pl.ds(pl.multiple_of(_block_id(to[0:2], to[2]) * wrows, SUB), wrows), :]
        return [_remote_copy(vec_m, vrecv_m.at[my_id], send_sems.at[0, k], recv_sems.at[0, k], to),
                _remote_copy(vec_b, vrecv_b.at[my_id], send_sems.at[1, k], recv_sems.at[1, k], to),
                _remote_copy(block, wrecv.at[k], send_sems.at[2, k], recv_sems.at[2, k], to)]

    mine = [pltpu.make_async_copy(vec_m, vrecv_m.at[my_id], local_sems.at[0]),
            pltpu.make_async_copy(vec_b, vrecv_b.at[my_id], local_sems.at[1])]

    @pl.when(step == 0)
    def _():
        for cp in mine:
            cp.start()
        for k in range(1, N_DEV):
            for cp in copies(k):
                cp.start()

    @pl.when(step == n_steps - 1)
    def _():
        for k in range(1, N_DEV):
            for cp in copies(k):
                cp.wait()
        for cp in mine:
            cp.wait()


def _pair_sum_parts(parts, sibs, core):
    n_q, rows2, cols = parts.shape
    half = rows2 // 2

    def body(core_ref, g_ref, s_ref, o_ref):
        o_ref[0] = (g_ref[0, 0].astype(F32) + s_ref[0].astype(F32)).astype(BF16)

    block = (1, half, cols)
    grid_spec = pltpu.PrefetchScalarGridSpec(
        num_scalar_prefetch=1, grid=(n_q,),
        in_specs=[pl.BlockSpec((1, 1, half, cols), lambda q, cr: (q, cr[0], 0, 0)),
                  pl.BlockSpec(block, lambda q, cr: (q, 0, 0))],
        out_specs=pl.BlockSpec(block, lambda q, cr: (q, 0, 0)))
    return pl.pallas_call(
        body, grid_spec=grid_spec, out_shape=pltpu.HBM((n_q, half, cols), BF16),
        compiler_params=_params(("arbitrary",), 32), name="pair_sum_w_in",
    )(core, *_in_hbm(parts.reshape(n_q, 2, half, cols), sibs))


def _pair_sum(g, sib, name):
    _, rows, cols = g.shape
    x, y, c = _place()
    slots = jnp.stack([_block_id(chip, c) for chip in [(x, y)] + _other_chips(x, y)]).astype(jnp.int32)

    def body(slots_ref, g_ref, sib_ref, hs_ref, own_ref):
        q = pl.program_id(0)
        both = g_ref[0].astype(F32) + sib_ref[0].astype(F32)

        @pl.when(q == 0)
        def _():
            own_ref[...] = both

        @pl.when(q > 0)
        def _():
            hs_ref[0] = both.astype(BF16)

    block = (1, rows, cols)
    grid_spec = pltpu.PrefetchScalarGridSpec(
        num_scalar_prefetch=1, grid=(4,),
        in_specs=[pl.BlockSpec(block, lambda q, s: (s[q], 0, 0)), pl.BlockSpec(block, lambda q, s: (q, 0, 0))],
        out_specs=[pl.BlockSpec(block, lambda q, s: (jnp.maximum(q - 1, 0), 0, 0)),
                   pl.BlockSpec((rows, cols), lambda q, s: (0, 0))])
    return pl.pallas_call(
        body, grid_spec=grid_spec,
        out_shape=(pltpu.HBM((3, rows, cols), BF16), pltpu.HBM((rows, cols), F32)),
        compiler_params=_params(("arbitrary",), 32), name=name,
    )(slots, *_in_hbm(g, sib))


def _exchange_scratch(n_arr, n_copies):
    return [pltpu.SemaphoreType.DMA((n_arr, n_copies)), pltpu.SemaphoreType.DMA((n_arr, n_copies))]


def _final_small(vrecv_m, vrecv_b, wab, wrecv, vec_x):
    wrows = wab.shape[0] // N_DEV

    def body(vm_ref, vb_ref, w_ref, wr_ref, vx_ref, o_vec, o_w, xrecv, wred, x_send, x_recv, b_send, b_recv):
        x, y, c = _place()
        my_id = _block_id((x, y), c)
        my_rows = pl.ds(pl.multiple_of(my_id * wrows, SUB), wrows)

        def xcopy(k):
            return _remote_copy(vx_ref, xrecv.at[my_id], x_send.at[k], x_recv.at[k], _peer(x, y, c, k))

        def bcopy(k):
            return _remote_copy(wred, o_w.at[my_rows, :], b_send.at[k], b_recv.at[k], _peer(x, y, c, k))

        xrecv[my_id] = vx_ref[...]
        for k in range(1, N_DEV):
            xcopy(k).start()
        red = w_ref[my_rows, :]
        for k in range(1, N_DEV):
            red = red + wr_ref[k]
        wred[...] = red
        o_w[my_rows, :] = red
        for k in range(1, N_DEV):
            bcopy(k).start()
        for k in range(1, N_DEV):
            xcopy(k).wait_recv()
        for rows, ref in ((slice(0, 8), vm_ref), (slice(8, 24), vb_ref), (slice(24, 32), xrecv)):
            tot = ref[0]
            for s in range(1, N_DEV):
                tot = tot + ref[s]
            o_vec[rows, :] = tot
        for k in range(1, N_DEV):
            bcopy(k).wait_recv()
        for k in range(1, N_DEV):
            xcopy(k).wait_send()
            bcopy(k).wait_send()

    vm = pl.BlockSpec(memory_space=pltpu.VMEM)
    dma8 = pltpu.SemaphoreType.DMA((N_DEV,))
    return pl.pallas_call(
        body, out_shape=(jax.ShapeDtypeStruct((VEC_ROWS, D_MODEL), F32), jax.ShapeDtypeStruct(wab.shape, F32)),
        in_specs=[vm] * 5, out_specs=[vm] * 2,
        scratch_shapes=[pltpu.VMEM((N_DEV, SUB, D_MODEL), F32), pltpu.VMEM((wrows, HEAD_DIM), F32),
                        dma8, dma8, dma8, dma8],
        compiler_params=_params(vmem_mib=32), name="final_small",
    )(vrecv_m, vrecv_b, wab, wrecv, vec_x)


def _in_proj(x, g_mix, shards, tm):
    t_len = x.shape[0]
    n_t = t_len // tm
    n_arr = len(shards)
    rows = [s.shape[0] for s in shards]
    width = 2 * rows[0]
    ax, ay = lax.axis_index("x"), lax.axis_index("y")
    order = jnp.stack([2 * cx + cy for cx, cy in [(ax, ay)] + _other_chips(ax, ay)]).astype(jnp.int32)

    def body(order_ref, x_ref, g_ref, *rest):
        shard_refs = rest[0:n_arr]
        u_ref, h_ref = rest[n_arr:n_arr + 2]
        fulls = rest[n_arr + 2:2 * n_arr + 2]
        h_s, wbuf, send_sems, recv_sems, local_sems, load_sem = rest[2 * n_arr + 2:]
        p = pl.program_id(0)
        i = pl.program_id(1)
        x_, y_, c = _place()
        me = (x_, y_, c)
        my_id = _block_id((x_, y_), c)
        sibling = (x_, y_, 1 - c)
        chips = _other_chips(x_, y_)

        def block(arr, blk):
            return fulls[arr].at[pl.ds(pl.multiple_of(blk * rows[arr], rows[arr]), rows[arr]), :]

        def copy(arr, k, blk, to, src=None):
            dst = block(arr, blk)
            return _remote_copy(dst if src is None else src, dst, send_sems.at[arr, k], recv_sems.at[arr, k], to)

        def local(arr):
            return pltpu.make_async_copy(shard_refs[arr], block(arr, my_id), local_sems.at[arr])

        def load_chip(chip):
            start = pl.multiple_of((2 * chip[0] + chip[1]) * width, width)
            cp = pltpu.make_async_copy(fulls[0].at[pl.ds(start, width), :], wbuf, load_sem.at[0])
            cp.start()
            cp.wait()

        @pl.when((p == 0) & (i == 0))
        def _():
            for arr in range(n_arr):
                local(arr).start()
                copy(arr, 0, my_id, sibling, shard_refs[arr]).start()
                for j in (0, 1):
                    copy(arr, 1 + j, my_id, (*chips[j], c), shard_refs[arr]).start()
            for arr in range(n_arr):
                local(arr).wait()
                copy(arr, 0, _block_id((x_, y_), 1 - c), me).wait_recv()
            load_chip((x_, y_))

        for j, chip in enumerate(chips):
            @pl.when((p == j + 1) & (i == 0))
            def _(j=j, chip=chip):
                for arr in range(n_arr):
                    copy(arr, 1 + j, _block_id(chip, c), me).wait_recv()
                    copy(arr, 4 + j, _block_id(chip, c), sibling).start()
                    if j == 0:
                        copy(arr, 3, my_id, (*chips[2], c), shard_refs[arr]).start()
                for arr in range(n_arr):
                    copy(arr, 4 + j, _block_id(chip, 1 - c), me).wait_recv()
                load_chip(chip)

        @pl.when((p == 3) & (i == n_t - 1))
        def _():
            for arr in range(n_arr):
                for k in range(4):
                    copy(arr, k, my_id, me, shard_refs[arr]).wait_send()
                for j, chip in enumerate(chips):
                    copy(arr, 4 + j, _block_id(chip, c), me).wait_send()

        tile = pl.ds(pl.multiple_of(i * tm, tm), tm)

        @pl.when(p == 0)
        def _():
            xv = x_ref[...]
            h = (xv * _rms(xv) * g_ref[...]).astype(BF16)
            h_ref[...] = h
            h_s[tile, :] = h

        u_ref[...] = _dot_nt(h_s[tile, :], wbuf[...])

    first_pass = lambda p, i, o: (jnp.where(p == 0, i, n_t - 1), 0)
    grid_spec = pltpu.PrefetchScalarGridSpec(
        num_scalar_prefetch=1, grid=(4, n_t),
        in_specs=[pl.BlockSpec((tm, D_MODEL), first_pass), pl.BlockSpec((1, D_MODEL), lambda p, i, o: (0, 0))]
        + [HBM_SPEC] * n_arr,
        out_specs=[pl.BlockSpec((tm, width), lambda p, i, o: (i, o[p])), pl.BlockSpec((tm, D_MODEL), first_pass)]
        + [HBM_SPEC] * n_arr,
        scratch_shapes=[pltpu.VMEM((t_len, D_MODEL), BF16), pltpu.VMEM((width, D_MODEL), BF16)]
        + _exchange_scratch(n_arr, 7) + [pltpu.SemaphoreType.DMA((n_arr,)), pltpu.SemaphoreType.DMA((1,))])
    return pl.pallas_call(
        body, grid_spec=grid_spec,
        out_shape=[jax.ShapeDtypeStruct((t_len, IN_COLS), F32), jax.ShapeDtypeStruct((t_len, D_MODEL), BF16)]
        + [jax.ShapeDtypeStruct((N_DEV * s.shape[0], s.shape[1]), s.dtype) for s in shards],
        compiler_params=_params(("arbitrary", "arbitrary"), 48), name="in_proj",
    )(order, x, g_mix, *shards)


def _conv3_chunk(u_ref, r, cv_prev, cw, row):
    gb = u_ref[pl.ds(r, SUB), OFF_GB:OFF_GB + CONV_WIDTH]
    gc = u_ref[pl.ds(r, SUB), OFF_GC:OFF_GC + CONV_WIDTH]
    v = u_ref[pl.ds(r, SUB), OFF_V:OFF_V + CONV_WIDTH]
    cv = gc * v
    cv_m1 = _down(cv, cv_prev, 1, row)
    cv_m2 = _down(cv, cv_prev, 2, row)
    cq = cw[2:3, :] * cv + cw[1:2, :] * cv_m1 + cw[0:1, :] * cv_m2
    return gb, gc, v, cv, cv_m1, cv_m2, cq


def _conv4_chunk(u_ref, r, xin_prev, rw, rb, row):
    xin = u_ref[pl.ds(r, SUB), OFF_XR:OFF_XR + LRU_WIDTH]
    m1 = _down(xin, xin_prev, 1, row)
    m2 = _down(xin, xin_prev, 2, row)
    m3 = _down(xin, xin_prev, 3, row)
    xr = rw[3:4, :] * xin + rw[2:3, :] * m1 + rw[1:2, :] * m2 + rw[0:1, :] * m3 + rb
    return xin, m1, m2, m3, xr


def _mixer_fwd(u, conv_w, rnn_conv_w, rnn_conv_b, wa, b_a, wx, b_x, lam, gnc, gnr, shards, tm):
    t_len = u.shape[0]
    n_steps = t_len // tm
    n_chunks = tm // SUB
    n_arr = len(shards)

    def body(u_ref, cw_ref, rw_ref, rb_ref, wa_ref, ba_ref, wx_ref, bx_ref, lam_ref, gnc_ref, gnr_ref, *rest):
        shard_refs = rest[0:n_arr]
        hs_ref, y_ref, xr_s, ra_ref, ii_ref, mult_ref = rest[n_arr:n_arr + 6]
        fulls = rest[n_arr + 6:2 * n_arr + 6]
        (y_s, pa_s, px_s, wabd, wxbd, cv_car, xin_car, h_car,
         send_sems, recv_sems, local_sems) = rest[2 * n_arr + 6:]
        _host_all_gather(pl.program_id(0), n_steps, shard_refs, fulls, send_sems, recv_sems, local_sems)

        @pl.when(pl.program_id(0) == 0)
        def _():
            cv_car[...] = jnp.zeros(cv_car.shape, F32)
            xin_car[...] = jnp.zeros(xin_car.shape, F32)
            h_car[...] = jnp.zeros(h_car.shape, F32)
            wabd[...] = _expand_heads(wa_ref[...])
            wxbd[...] = _expand_heads(wx_ref[...])

        row_c = lax.broadcasted_iota(jnp.int32, (SUB, CONV_WIDTH), 0)
        row_r = lax.broadcasted_iota(jnp.int32, (SUB, LRU_WIDTH), 0)
        cw = cw_ref[...]
        rw = rw_ref[...]
        rb = rb_ref[...]
        g_c = gnc_ref[...]
        g_r = gnr_ref[...]
        sp_c = LRU_C * _softplus_neg(lam_ref[...])

        def convs(i, carry):
            cv_prev, xin_prev = carry
            r = pl.multiple_of(i * SUB, SUB)
            gb, _, _, cv, _, _, cq = _conv3_chunk(u_ref, r, cv_prev, cw, row_c)
            y_c = gb * cq
            y_s[pl.ds(r, SUB), 0:CONV_WIDTH] = y_c * _rms(y_c) * g_c
            xin, _, _, _, xr = _conv4_chunk(u_ref, r, xin_prev, rw, rb, row_r)
            xr_s[pl.ds(r, SUB), :] = xr
            return cv, xin

        cv_last, xin_last = _chunk_loop(n_chunks, convs, (cv_car[...], xin_car[...]))
        cv_car[...] = cv_last
        xin_car[...] = xin_last

        xrb = xr_s[...].astype(BF16)
        pa_s[...] = _block_diag_apply(xrb, wabd) + ba_ref[...]
        px_s[...] = _block_diag_apply(xrb, wxbd) + bx_ref[...]

        def recur(i, h_prev):
            r = pl.multiple_of(i * SUB, SUB)
            xr = xr_s[pl.ds(r, SUB), :]
            ra, ii, a, mult, _ = _lru_gates(pa_s[pl.ds(r, SUB), :], px_s[pl.ds(r, SUB), :], sp_c)
            ra_ref[pl.ds(r, SUB), :] = ra
            ii_ref[pl.ds(r, SUB), :] = ii
            mult_ref[pl.ds(r, SUB), :] = mult
            a_cum, b_cum = _scan8_fwd(a, mult * ii * xr, row_r)
            h = a_cum * h_prev + b_cum
            hs_ref[pl.ds(r, SUB), :] = h
            ge, _ = _gelu(u_ref[pl.ds(r, SUB), OFF_G:OFF_G + LRU_WIDTH])
            y_r = h * ge
            y_s[pl.ds(r, SUB), CONV_WIDTH:MIX_WIDTH] = y_r * _rms(y_r) * g_r
            return h[SUB - 1:SUB, :]

        h_car[...] = _chunk_loop(n_chunks, recur, h_car[...])

        y_ref[...] = y_s[...].astype(BF16)

    row_tile = lambda w: pl.BlockSpec((tm, w), lambda i: (i, 0))
    whole = lambda a: pl.BlockSpec(a.shape, lambda i: (0,) * a.ndim)
    smalls = (conv_w, rnn_conv_w, rnn_conv_b, wa, b_a, wx, b_x, lam, gnc, gnr)
    return pl.pallas_call(
        body, grid=(n_steps,),
        in_specs=[row_tile(IN_COLS)] + [whole(a) for a in smalls] + [HBM_SPEC] * n_arr,
        out_specs=[row_tile(LRU_WIDTH), row_tile(MIX_WIDTH)] + [row_tile(LRU_WIDTH)] * 4 + [HBM_SPEC] * n_arr,
        out_shape=[jax.ShapeDtypeStruct((t_len, LRU_WIDTH), F32), jax.ShapeDtypeStruct((t_len, MIX_WIDTH), BF16)]
        + [jax.ShapeDtypeStruct((t_len, LRU_WIDTH), F32)] * 4
        + [jax.ShapeDtypeStruct((N_DEV,) + s.shape, BF16) for s in shards],
        scratch_shapes=[pltpu.VMEM((tm, MIX_WIDTH), F32),
                        pltpu.VMEM((tm, LRU_WIDTH), F32), pltpu.VMEM((tm, LRU_WIDTH), F32),
                        pltpu.VMEM((LRU_WIDTH, GROUP), BF16), pltpu.VMEM((LRU_WIDTH, GROUP), BF16),
                        pltpu.VMEM((SUB, CONV_WIDTH), F32), pltpu.VMEM((SUB, LRU_WIDTH), F32),
                        pltpu.VMEM((1, LRU_WIDTH), F32)]
        + _exchange_scratch(n_arr, 7) + [pltpu.SemaphoreType.DMA((n_arr,))],
        compiler_params=_params(("arbitrary",), 56), name="mixer_fwd",
    )(u, *smalls, *shards)


def _mlp_up(x, y, g_mlp, w_out, w1, w2_shard, tm):
    t_len = x.shape[0]
    n_steps = t_len // tm
    n_blk, _, blk = w1.shape

    def body(x_ref, y_ref, gm_ref, wout_hbm, w1_hbm, w2_ref, x1_ref, h2_ref, z_ref, w2_full,
             wout_s, w1_s, sem, send_sems, recv_sems, local_sems):
        step = pl.program_id(0)
        _host_all_gather(step, n_steps, [w2_ref], [w2_full], send_sems, recv_sems, local_sems)

        @pl.when(step == 0)
        def _():
            loads = [pltpu.make_async_copy(wout_hbm, wout_s, sem.at[0]), pltpu.make_async_copy(w1_hbm, w1_s, sem.at[1])]
            for cp in loads:
                cp.start()
            for cp in loads:
                cp.wait()

        x1v = x_ref[...] + jnp.dot(y_ref[...], wout_s[...], preferred_element_type=F32)
        x1_ref[...] = x1v
        h2 = (x1v * _rms(x1v) * gm_ref[...]).astype(BF16)
        h2_ref[...] = h2
        for k in range(n_blk):
            rp = jnp.maximum(jnp.dot(h2, w1_s[k], preferred_element_type=F32), 0.0)
            z_ref[:, k * blk:(k + 1) * blk] = (rp * rp).astype(BF16)

    row_tile = lambda w: pl.BlockSpec((tm, w), lambda i: (i, 0))
    return pl.pallas_call(
        body, grid=(n_steps,),
        in_specs=[row_tile(D_MODEL), row_tile(MIX_WIDTH), pl.BlockSpec((1, D_MODEL), lambda i: (0, 0)),
                  HBM_SPEC, HBM_SPEC, HBM_SPEC],
        out_specs=[row_tile(D_MODEL), row_tile(D_MODEL), row_tile(D_FF), HBM_SPEC],
        out_shape=[jax.ShapeDtypeStruct((t_len, D_MODEL), F32), jax.ShapeDtypeStruct((t_len, D_MODEL), BF16),
                   jax.ShapeDtypeStruct((t_len, D_FF), BF16), jax.ShapeDtypeStruct((N_DEV,) + w2_shard.shape, BF16)],
        scratch_shapes=[pltpu.VMEM(w_out.shape, BF16), pltpu.VMEM(w1.shape, BF16), pltpu.SemaphoreType.DMA((2,))]
        + _exchange_scratch(1, 7) + [pltpu.SemaphoreType.DMA((1,))],
        compiler_params=_params(("arbitrary",), 48), name="mlp_up",
    )(x, y, g_mlp, w_out, w1, w2_shard)


def _mlp_down_bwd(x1, z, target, g_mlp, g_f, w1, w2, tm):
    t_len = x1.shape[0]
    n_steps = t_len // tm
    n_blk, _, blk = w1.shape

    def body(x1_ref, z_ref, tg_ref, gm_ref, gf_ref, w1_hbm, w2_hbm, dx1_ref, dx2_ref, vec_ref, dpre_hbm,
             w1_s, w2_s, dp_s, sem, out_sem):
        step = pl.program_id(0)
        rows = pl.ds(pl.multiple_of(step * tm, tm), tm)
        dp_out = pltpu.make_async_copy(dp_s, dpre_hbm.at[rows, :], out_sem.at[0])

        @pl.when(step == 0)
        def _():
            loads = [pltpu.make_async_copy(w1_hbm, w1_s, sem.at[0]), pltpu.make_async_copy(w2_hbm, w2_s, sem.at[1])]
            for cp in loads:
                cp.start()
            vec_ref[...] = jnp.zeros(vec_ref.shape, F32)
            for cp in loads:
                cp.wait()

        x1v = x1_ref[...]
        g_m = gm_ref[...]
        g_o = gf_ref[...]
        r2 = _rms(x1v)
        x1h = x1v * r2
        x2 = x1v + jnp.dot(z_ref[...], w2_s[...], preferred_element_type=F32)
        r3 = _rms(x2)
        x2h = x2 * r3
        err = x2h * g_o - tg_ref[...]
        dout = err * (1.0 / D_MODEL)
        vec_ref[ROW_LOSS:ROW_LOSS + 1, :] += (0.5 / D_MODEL) * jnp.sum(err * err, axis=0, keepdims=True)
        vec_ref[ROW_GF:ROW_GF + 1, :] += jnp.sum(dout * x2h, axis=0, keepdims=True)
        dx2 = _rms_bwd(dout, x2h, r3, g_o)
        dx2b = dx2.astype(BF16)
        dx2_ref[...] = dx2b
        dh2 = jnp.zeros((tm, D_MODEL), F32)

        @pl.when(step > 0)
        def _():
            dp_out.wait()

        for k in range(n_blk):
            cols = slice(k * blk, (k + 1) * blk)
            dz = _dot_nt(dx2b, w2_s[cols, :])
            dpb = (dz * 2.0 * jnp.sqrt(z_ref[:, cols].astype(F32))).astype(BF16)
            dp_s[:, cols] = dpb
            dh2 = dh2 + _dot_nt(dpb, w1_s[k])
        dp_out.start()
        vec_ref[ROW_GMLP:ROW_GMLP + 1, :] += jnp.sum(dh2 * x1h, axis=0, keepdims=True)
        dx1_ref[...] = dx2 + _rms_bwd(dh2, x1h, r2, g_m)

        @pl.when(step == n_steps - 1)
        def _():
            dp_out.wait()

    row_tile = lambda w: pl.BlockSpec((tm, w), lambda i: (i, 0))
    vec_spec = pl.BlockSpec((1, D_MODEL), lambda i: (0, 0))
    return pl.pallas_call(
        body, grid=(n_steps,),
        in_specs=[row_tile(D_MODEL), row_tile(D_FF), row_tile(D_MODEL), vec_spec, vec_spec, HBM_SPEC, HBM_SPEC],
        out_specs=[row_tile(D_MODEL), row_tile(D_MODEL), pl.BlockSpec((SUB, D_MODEL), lambda i: (0, 0)), HBM_SPEC],
        out_shape=[jax.ShapeDtypeStruct((t_len, D_MODEL), F32), jax.ShapeDtypeStruct((t_len, D_MODEL), BF16),
                   jax.ShapeDtypeStruct((SUB, D_MODEL), F32), jax.ShapeDtypeStruct((t_len, D_FF), BF16)],
        scratch_shapes=[pltpu.VMEM(w1.shape, BF16), pltpu.VMEM(w2.shape, BF16), pltpu.VMEM((tm, D_FF), BF16),
                        pltpu.SemaphoreType.DMA((2,)), pltpu.SemaphoreType.DMA((1,))],
        compiler_params=_params(("arbitrary",), 56), name="mlp_down_bwd",
    )(x1, z, target, g_mlp, g_f, w1, w2)


def _mixer_bwd(u, hs, dx1, saved, conv_w, rnn_conv_w, rnn_conv_b, wa, b_a, wx, b_x, lam, gnc, gnr, w_out,
               chip_sums, g_wout, tm):
    t_len = u.shape[0]
    n_tiles = t_len // tm
    n_chunks = tm // SUB
    per_tile = tm // SUB
    n_sums = len(chip_sums)

    def body(u_ref, up_ref, hs_ref, hp_ref, dx1_ref, xr_ref, ra_ref, ii_ref, mult_ref,
             cw_ref, rw_ref, rb_ref, wa_ref, ba_ref, wx_ref, bx_ref, lam_ref, gnc_ref, gnr_ref, wout_ref, *rest):
        hsends = rest[0:n_sums]
        gwout_ref = rest[n_sums]
        du_ref, vec_ref, wab_ref = rest[n_sums + 1:n_sums + 4]
        hrecvs = rest[n_sums + 4:2 * n_sums + 4]
        sib_wout = rest[2 * n_sums + 4]
        (du_s, dy_s, dpa_s, dpx_s, dxr_s, wabd, wxbd, acc, dwa_acc, dwx_acc,
         a_car, dh_car, dcq_car, dxr_car, i_send, i_recv, d_send, d_recv) = rest[2 * n_sums + 5:]
        step = pl.program_id(0)
        _host_chip_exchange(step, n_tiles, hsends, hrecvs, i_send, i_recv)
        _host_pair_exchange(step, n_tiles, [gwout_ref], [sib_wout], d_send, d_recv)
        has_prev = (step < n_tiles - 1).astype(F32)

        @pl.when(step == 0)
        def _():
            acc[...] = jnp.zeros(acc.shape, F32)
            dwa_acc[...] = jnp.zeros(dwa_acc.shape, F32)
            dwx_acc[...] = jnp.zeros(dwx_acc.shape, F32)
            a_car[...] = jnp.ones(a_car.shape, F32)
            dh_car[...] = jnp.zeros(dh_car.shape, F32)
            dcq_car[...] = jnp.zeros(dcq_car.shape, F32)
            dxr_car[...] = jnp.zeros(dxr_car.shape, F32)
            wabd[...] = _expand_heads(wa_ref[...])
            wxbd[...] = _expand_heads(wx_ref[...])

        row_c = lax.broadcasted_iota(jnp.int32, (SUB, CONV_WIDTH), 0)
        row_r = lax.broadcasted_iota(jnp.int32, (SUB, LRU_WIDTH), 0)
        cw = cw_ref[...]
        rw = rw_ref[...]
        rb = rb_ref[...]
        g_c = gnc_ref[...]
        g_r = gnr_ref[...]
        sp_c = LRU_C * _softplus_neg(lam_ref[...])

        up = up_ref[...] * has_prev
        cv_before = up[:, OFF_GC:OFF_GC + CONV_WIDTH] * up[:, OFF_V:OFF_V + CONV_WIDTH]
        xin_before = up[:, OFF_XR:OFF_XR + LRU_WIDTH]
        hs_before = hp_ref[...] * has_prev

        dy_s[...] = _dot_nt(dx1_ref[...].astype(BF16), wout_ref[...])

        xrb = xr_ref[...].astype(BF16)

        def recur_bwd(j, carry):
            a_later, dh_later = carry
            i = n_chunks - 1 - j
            r = pl.multiple_of(i * SUB, SUB)
            rp = pl.multiple_of(jnp.maximum(i - 1, 0) * SUB, SUB)
            xr = xr_ref[pl.ds(r, SUB), :]
            hs_c = hs_ref[pl.ds(r, SUB), :]
            hs_prev = jnp.where(i == 0, hs_before, hs_ref[pl.ds(rp, SUB), :])
            h_m1 = _down(hs_c, hs_prev, 1, row_r)
            ra = ra_ref[pl.ds(r, SUB), :]
            ii = ii_ref[pl.ds(r, SUB), :]
            mult = mult_ref[pl.ds(r, SUB), :]
            a = jnp.exp(-ra * sp_c)
            inv_mult = lax.rsqrt(mult * mult)
            ge, dge = _gelu(u_ref[pl.ds(r, SUB), OFF_G:OFF_G + LRU_WIDTH])
            y_r = hs_c * ge
            rr = _rms(y_r)
            yhat = y_r * rr
            dyn = dy_s[pl.ds(r, SUB), CONV_WIDTH:MIX_WIDTH]
            acc[ACC_GNR] += dyn * yhat
            dy_r = _rms_bwd(dyn, yhat, rr, g_r)
            du_s[pl.ds(r, SUB), OFF_G:OFF_G + LRU_WIDTH] = dy_r * hs_c * dge
            a_cum, d_cum = _scan8_rev(_up(a, a_later, 1, row_r), dy_r * ge, row_r)
            dh = a_cum * dh_later + d_cum
            dmult = dh * ii * xr
            dii = dh * mult * xr
            dxr_s[pl.ds(r, SUB), :] = dh * mult * ii
            dla = dh * h_m1 * a - dmult * a * a * inv_mult
            acc[ACC_SP] += -dla * ra
            dpa = -dla * sp_c * ra * (1.0 - ra)
            dpx = dii * ii * (1.0 - ii)
            acc[ACC_BA] += dpa
            acc[ACC_BX] += dpx
            dpa_s[pl.ds(r, SUB), :] = dpa
            dpx_s[pl.ds(r, SUB), :] = dpx
            return a, dh[0:1, :]

        a_first, dh_first = _chunk_loop(n_chunks, recur_bwd, (a_car[...], dh_car[...]), in_flight=8)
        a_car[...] = a_first
        dh_car[...] = dh_first

        dpab = dpa_s[...].astype(BF16)
        dpxb = dpx_s[...].astype(BF16)
        dxr_s[...] += _block_diag_apply_t(dpab, wabd) + _block_diag_apply_t(dpxb, wxbd)
        for g in range(LRU_WIDTH // GROUP):
            cols = slice(g * GROUP, (g + 1) * GROUP)
            dwa_acc[cols, :] += _dot_tn(xrb[:, cols], dpab[:, cols])
            dwx_acc[cols, :] += _dot_tn(xrb[:, cols], dpxb[:, cols])

        def convs_bwd(j, carry):
            dcq_later, dxr_later = carry
            i = n_chunks - 1 - j
            r = pl.multiple_of(i * SUB, SUB)
            rp = pl.multiple_of(jnp.maximum(i - 1, 0) * SUB, SUB)
            cv_prev = jnp.where(i == 0, cv_before,
                                u_ref[pl.ds(rp, SUB), OFF_GC:OFF_GC + CONV_WIDTH]
                                * u_ref[pl.ds(rp, SUB), OFF_V:OFF_V + CONV_WIDTH])
            gb, gc, v, cv, cv_m1, cv_m2, cq = _conv3_chunk(u_ref, r, cv_prev, cw, row_c)
            y_c = gb * cq
            rc = _rms(y_c)
            yhat = y_c * rc
            dyn = dy_s[pl.ds(r, SUB), 0:CONV_WIDTH]
            acc[ACC_GNC, :, 0:CONV_WIDTH] += dyn * yhat
            dy_c = _rms_bwd(dyn, yhat, rc, g_c)
            dcq = dy_c * gb
            dcv = (cw[2:3, :] * dcq + cw[1:2, :] * _up(dcq, dcq_later, 1, row_c)
                   + cw[0:1, :] * _up(dcq, dcq_later, 2, row_c))
            acc[ACC_CW + 2, :, 0:CONV_WIDTH] += dcq * cv
            acc[ACC_CW + 1, :, 0:CONV_WIDTH] += dcq * cv_m1
            acc[ACC_CW + 0, :, 0:CONV_WIDTH] += dcq * cv_m2
            du_s[pl.ds(r, SUB), OFF_GB:OFF_GB + CONV_WIDTH] = dy_c * cq
            du_s[pl.ds(r, SUB), OFF_GC:OFF_GC + CONV_WIDTH] = dcv * v
            du_s[pl.ds(r, SUB), OFF_V:OFF_V + CONV_WIDTH] = dcv * gc

            xin_prev = jnp.where(i == 0, xin_before, u_ref[pl.ds(rp, SUB), OFF_XR:OFF_XR + LRU_WIDTH])
            xin, m1, m2, m3, _ = _conv4_chunk(u_ref, r, xin_prev, rw, rb, row_r)
            dxr = dxr_s[pl.ds(r, SUB), :]
            du_s[pl.ds(r, SUB), OFF_XR:OFF_XR + LRU_WIDTH] = (
                rw[3:4, :] * dxr + rw[2:3, :] * _up(dxr, dxr_later, 1, row_r)
                + rw[1:2, :] * _up(dxr, dxr_later, 2, row_r) + rw[0:1, :] * _up(dxr, dxr_later, 3, row_r))
            acc[ACC_RW + 3] += dxr * xin
            acc[ACC_RW + 2] += dxr * m1
            acc[ACC_RW + 1] += dxr * m2
            acc[ACC_RW + 0] += dxr * m3
            acc[ACC_BR] += dxr
            return dcq, dxr

        dcq_first, dxr_first = _chunk_loop(n_chunks, convs_bwd, (dcq_car[...], dxr_car[...]), in_flight=8)
        dcq_car[...] = dcq_first
        dxr_car[...] = dxr_first

        du_ref[...] = du_s[...].astype(BF16)

        @pl.when(step == n_tiles - 1)
        def _():
            vec_ref[...] = jnp.zeros(vec_ref.shape, F32)
            rows = {ACC_GNC: ROW_GNC, ACC_GNR: ROW_GNR, ACC_BR: ROW_BR, ACC_BA: ROW_BA, ACC_BX: ROW_BX}
            for k in range(3):
                rows[ACC_CW + k] = ROW_CW + k
            for k in range(4):
                rows[ACC_RW + k] = ROW_RW + k
            for slot, out_row in rows.items():
                o = out_row - ROW_GNC
                vec_ref[o:o + 1, :] = jnp.sum(acc[slot], axis=0, keepdims=True)
            lam_v = lam_ref[...]
            dsp = jnp.sum(acc[ACC_SP], axis=0, keepdims=True)
            o = ROW_LAM - ROW_GNC
            vec_ref[o:o + 1, :] = -dsp * LRU_C / (1.0 + jnp.exp(lam_v))
            wab_ref[0:LRU_WIDTH, :] = _fold_heads(dwa_acc[...])
            wab_ref[LRU_WIDTH:2 * LRU_WIDTH, :] = _fold_heads(dwx_acc[...])

    rev = lambda w: pl.BlockSpec((tm, w), lambda s: (n_tiles - 1 - s, 0))
    before = lambda w: pl.BlockSpec((SUB, w), lambda s: (jnp.maximum((n_tiles - 1 - s) * per_tile - 1, 0), 0))
    whole = lambda a: pl.BlockSpec(a.shape, lambda s: (0,) * a.ndim)
    smalls = (conv_w, rnn_conv_w, rnn_conv_b, wa, b_a, wx, b_x, lam, gnc, gnr, w_out)
    full = lambda w: pltpu.VMEM((tm, w), F32)
    return pl.pallas_call(
        body, grid=(n_tiles,),
        in_specs=[rev(IN_COLS), before(IN_COLS), rev(LRU_WIDTH), before(LRU_WIDTH), rev(D_MODEL)]
        + [rev(LRU_WIDTH)] * len(saved) + [whole(a) for a in smalls] + [HBM_SPEC] * (n_sums + 1),
        out_specs=[rev(IN_COLS), pl.BlockSpec((16, D_MODEL), lambda s: (0, 0)),
                   pl.BlockSpec((2 * LRU_WIDTH, HEAD_DIM), lambda s: (0, 0))] + [HBM_SPEC] * (n_sums + 1),
        out_shape=[jax.ShapeDtypeStruct((t_len, IN_COLS), BF16), jax.ShapeDtypeStruct((16, D_MODEL), F32),
                   jax.ShapeDtypeStruct((2 * LRU_WIDTH, HEAD_DIM), F32)]
        + [jax.ShapeDtypeStruct(s.shape, BF16) for s in chip_sums]
        + [jax.ShapeDtypeStruct((4,) + g_wout.shape[1:], BF16)],
        scratch_shapes=[full(IN_COLS), full(MIX_WIDTH), full(LRU_WIDTH), full(LRU_WIDTH), full(LRU_WIDTH),
                        pltpu.VMEM((LRU_WIDTH, GROUP), BF16), pltpu.VMEM((LRU_WIDTH, GROUP), BF16),
                        pltpu.VMEM((N_ACC, SUB, LRU_WIDTH), F32),
                        pltpu.VMEM((LRU_WIDTH, GROUP), F32), pltpu.VMEM((LRU_WIDTH, GROUP), F32),
                        pltpu.VMEM((SUB, LRU_WIDTH), F32), pltpu.VMEM((1, LRU_WIDTH), F32),
                        pltpu.VMEM((SUB, CONV_WIDTH), F32), pltpu.VMEM((SUB, LRU_WIDTH), F32)]
        + _exchange_scratch(n_sums, 3) + _exchange_scratch(1, 4),
        compiler_params=_params(("arbitrary",), 56), name="mixer_bwd",
    )(u, u, hs, hs, dx1, *saved, *smalls, *chip_sums, g_wout)


def _in_proj_bwd(du, dx1, x, g_mix, win_t, tm, chip_sums, g_own):
    t_len = x.shape[0]
    n_steps = t_len // tm

    def body(du_ref, dx1_ref, x_ref, g_ref, w_ref, hs_ref, gown_ref,
             dx_ref, vec_ref, landed_ref, sib_ref, i_send, i_recv, d_send, d_recv):
        step = pl.program_id(0)
        _host_chip_exchange(step, n_steps, [hs_ref], [landed_ref], i_send, i_recv)
        _host_half_exchange(step, n_steps, gown_ref, sib_ref, d_send, d_recv)

        @pl.when(step == 0)
        def _():
            vec_ref[...] = jnp.zeros(vec_ref.shape, F32)

        dh = jnp.dot(du_ref[...], w_ref[...], preferred_element_type=F32)
        xv = x_ref[...]
        r1 = _rms(xv)
        xh = xv * r1
        vec_ref[0:1, :] += jnp.sum(dh * xh, axis=0, keepdims=True)
        dx_ref[...] = dx1_ref[...] + _rms_bwd(dh, xh, r1, g_ref[...])

    row_tile = lambda w: pl.BlockSpec((tm, w), lambda i: (i, 0))
    half_shape = (g_own.shape[0], g_own.shape[1] // 2, g_own.shape[2])
    return pl.pallas_call(
        body, grid=(n_steps,),
        in_specs=[row_tile(IN_COLS), row_tile(D_MODEL), row_tile(D_MODEL), pl.BlockSpec((1, D_MODEL), lambda i: (0, 0)),
                  pl.BlockSpec((IN_COLS, D_MODEL), lambda i: (0, 0))] + [HBM_SPEC] * 2,
        out_specs=[row_tile(D_MODEL), pl.BlockSpec((SUB, D_MODEL), lambda i: (0, 0))] + [HBM_SPEC] * 2,
        out_shape=[jax.ShapeDtypeStruct((t_len, D_MODEL), F32), jax.ShapeDtypeStruct((SUB, D_MODEL), F32),
                   jax.ShapeDtypeStruct(chip_sums.shape, BF16), jax.ShapeDtypeStruct(half_shape, BF16)],
        scratch_shapes=_exchange_scratch(1, 3) + [pltpu.SemaphoreType.DMA((1,)), pltpu.SemaphoreType.DMA((1,))],
        compiler_params=_params(("arbitrary",), 56), name="in_proj_bwd",
    )(du, dx1, x, g_mix, win_t, chip_sums, g_own)


def _tn_weight_grad(a, b, tk, name, pair=(), col_blocks=1):
    t_len, m = a.shape
    n = b.shape[1]
    n_steps = t_len // tk
    sent = tuple(pair)
    n_sent = len(sent)

    def body(a_ref, b_ref, *rest):
        srcs = rest[0:n_sent]
        o_ref = rest[n_sent]
        dsts = rest[n_sent + 1:2 * n_sent + 1]
        acc = rest[2 * n_sent + 1]
        sems = rest[2 * n_sent + 2:]
        j = pl.program_id(0)
        if pair:
            _host_pair_exchange(j, n_steps, srcs, dsts, *sems)

        @pl.when(j == 0)
        def _():
            acc[...] = jnp.zeros(acc.shape, F32)

        acc[...] += _dot_tn(a_ref[...].astype(BF16), b_ref[...].astype(BF16))

        @pl.when(j == n_steps - 1)
        def _():
            if col_blocks == 1:
                o_ref[...] = acc[...].astype(BF16)
            else:
                for k in range(col_blocks):
                    o_ref[k] = acc[:, k * nb:(k + 1) * nb].astype(BF16)

    nb = n // col_blocks
    out_dims = (m, n) if col_blocks == 1 else (col_blocks, m, nb)
    landed = [jax.ShapeDtypeStruct((4,) + g.shape[1:], BF16) for g in pair]
    scratch = [pltpu.VMEM((m, n), F32)]
    if n_sent:
        scratch += _exchange_scratch(n_sent, 4)
    return pl.pallas_call(
        body, grid=(n_steps,),
        in_specs=[pl.BlockSpec((tk, m), lambda j: (j, 0)), pl.BlockSpec((tk, n), lambda j: (j, 0))]
        + [HBM_SPEC] * n_sent,
        out_specs=[pl.BlockSpec(out_dims, lambda j: (0,) * len(out_dims))] + [HBM_SPEC] * n_sent,
        out_shape=[jax.ShapeDtypeStruct(out_dims, BF16)] + landed,
        scratch_shapes=scratch,
        compiler_params=_params(("arbitrary",), 56), name=name,
    )(a, b, *sent)


def _w_in_grad_part(du, h, tk, name, chip_ids, chip=(), halves=None, small=None):
    t_len = du.shape[0]
    n_t = t_len // tk
    n_q = chip_ids.shape[0]
    width = 2 * (IN_COLS // N_DEV)
    n_steps = n_q * n_t
    n_chip = len(chip)
    sent = tuple(chip) + (() if halves is None else (halves,)) + (() if small is None else tuple(small))
    n_sent = len(sent)

    def body(ids_ref, a_ref, b_ref, *rest):
        srcs = rest[0:n_sent]
        o_ref = rest[n_sent]
        dsts = rest[n_sent + 1:2 * n_sent + 1]
        acc = rest[2 * n_sent + 1]
        sems = list(rest[2 * n_sent + 2:])
        j = pl.program_id(1)
        step = pl.program_id(0) * n_t + j
        if chip:
            _host_chip_exchange(step, n_steps, srcs[0:n_chip], dsts[0:n_chip], sems.pop(0), sems.pop(0))
        if halves is not None:
            _host_half_exchange(step, n_steps, srcs[n_chip], dsts[n_chip], sems.pop(0), sems.pop(0))
        if small is not None:
            _host_small_exchange(step, n_steps, *srcs[n_sent - 3:], *dsts[n_sent - 3:], *sems)

        @pl.when(j == 0)
        def _():
            acc[...] = jnp.zeros(acc.shape, F32)

        acc[...] += _dot_tn(a_ref[...], b_ref[...])

        @pl.when(j == n_t - 1)
        def _():
            o_ref[0] = acc[...].astype(BF16)

    landed = [jax.ShapeDtypeStruct(s.shape, BF16) for s in chip]
    scratch = [pltpu.VMEM((width, D_MODEL), F32)]
    if chip:
        scratch += _exchange_scratch(len(chip), 3)
    if halves is not None:
        landed.append(jax.ShapeDtypeStruct((halves.shape[0], halves.shape[1] // 2, halves.shape[2]), BF16))
        scratch += [pltpu.SemaphoreType.DMA((halves.shape[0],)), pltpu.SemaphoreType.DMA((halves.shape[0],))]
    if small is not None:
        vec_m, vec_b, wab = small
        landed += [jax.ShapeDtypeStruct((N_DEV,) + vec_m.shape, F32), jax.ShapeDtypeStruct((N_DEV,) + vec_b.shape, F32),
                   jax.ShapeDtypeStruct((N_DEV, wab.shape[0] // N_DEV, wab.shape[1]), F32)]
        scratch += _exchange_scratch(3, N_DEV) + [pltpu.SemaphoreType.DMA((2,))]
    grid_spec = pltpu.PrefetchScalarGridSpec(
        num_scalar_prefetch=1, grid=(n_q, n_t),
        in_specs=[pl.BlockSpec((tk, width), lambda q, j, ids: (j, ids[q])),
                  pl.BlockSpec((tk, D_MODEL), lambda q, j, ids: (j, 0))] + [HBM_SPEC] * n_sent,
        out_specs=[pl.BlockSpec((1, width, D_MODEL), lambda q, j, ids: (q, 0, 0))] + [HBM_SPEC] * n_sent,
        scratch_shapes=scratch)
    return pl.pallas_call(
        body, grid_spec=grid_spec, out_shape=[jax.ShapeDtypeStruct((n_q, width, D_MODEL), BF16)] + landed,
        compiler_params=_params(("arbitrary", "arbitrary"), 40), name=name,
    )(chip_ids, du, h, *sent)


def _adamw(w, g, m, v):
    m = ADAM_B1 * m + (1.0 - ADAM_B1) * g
    v = ADAM_B2 * v + (1.0 - ADAM_B2) * (g * g)
    delta = -ADAM_LR * ((m / BC1) / (jnp.sqrt(v / BC2) + ADAM_EPS) + ADAM_WD * w)
    return delta, m, v


def _update_sharded(g, landed, w, m, v, rows_blk, name):
    rows, cols = w.shape

    def body(g_ref, l_ref, w_ref, m_ref, v_ref, og, od, om, ov):
        gv = g_ref[...]
        for j in range(3):
            gv = gv + l_ref[j].astype(F32)
        delta, mn, vn = _adamw(w_ref[...], gv, m_ref[...], v_ref[...])
        og[...] = gv
        od[...] = delta
        om[...] = mn
        ov[...] = vn

    blk = pl.BlockSpec((rows_blk, cols), lambda i: (i, 0))
    shape = pltpu.HBM((rows, cols), F32)
    return pl.pallas_call(
        body, grid=(rows // rows_blk,),
        in_specs=[blk, pl.BlockSpec((3, rows_blk, cols), lambda i: (0, i, 0)), blk, blk, blk],
        out_specs=[blk] * 4, out_shape=[shape] * 4,
        compiler_params=_params(("arbitrary",), 32), name=name,
    )(*_in_hbm(g, landed, w, m, v))


def _update_w_in(g_own, sib_own, landed, w, m, v, core, rows_blk):
    rows, cols = w.shape
    pad_cols = -(-cols // 128) * 128

    def body(core_ref, g_ref, s_ref, l_ref, w_ref, m_ref, v_ref, og, od, om, ov, padbuf, turned):
        gt = g_ref[0, 0].astype(F32) + s_ref[0].astype(F32)
        for j in range(3):
            gt = gt + l_ref[j].astype(F32)
        padbuf[...] = jnp.zeros(padbuf.shape, F32)
        padbuf[0:cols, :] = gt
        turned[...] = padbuf[...].T
        gv = turned[:, 0:cols]
        delta, mn, vn = _adamw(w_ref[...], gv, m_ref[...], v_ref[...])
        og[...] = gv
        od[...] = delta
        om[...] = mn
        ov[...] = vn

    blk = pl.BlockSpec((rows_blk, cols), lambda i, cr: (i, 0))
    grid_spec = pltpu.PrefetchScalarGridSpec(
        num_scalar_prefetch=1, grid=(rows // rows_blk,),
        in_specs=[pl.BlockSpec((1, 1, cols, rows_blk), lambda i, cr: (0, cr[0], 0, i)),
                  pl.BlockSpec((1, cols, rows_blk), lambda i, cr: (0, 0, i)),
                  pl.BlockSpec((3, cols, rows_blk), lambda i, cr: (0, 0, i)), blk, blk, blk],
        out_specs=[blk] * 4,
        scratch_shapes=[pltpu.VMEM((pad_cols, rows_blk), F32), pltpu.VMEM((rows_blk, pad_cols), F32)])
    return pl.pallas_call(
        body, grid_spec=grid_spec, out_shape=[pltpu.HBM((rows, cols), F32)] * 4,
        compiler_params=_params(("arbitrary",), 32), name="update_w_in",
    )(core, *_in_hbm(g_own.reshape(1, 2, cols, rows), sib_own, landed, w, m, v))


def _update_small(vsum, wsum, g_cw, g_rw, weights, moments_m, moments_v):
    n = len(weights)

    def body(*refs):
        vs, ws, gcw, grw = refs[0:4]
        w_refs = refs[4:4 + n]
        m_refs = refs[4 + n:4 + 2 * n]
        v_refs = refs[4 + 2 * n:4 + 3 * n]
        outs = refs[4 + 3 * n:]
        loss_ref = outs[0]
        loss_ref[...] = jnp.sum(vs[ROW_LOSS:ROW_LOSS + 1, :], axis=1, keepdims=True)
        grads = [
            vs[ROW_GMIX:ROW_GMIX + 1, :], gcw[...], grw[...], vs[ROW_BR:ROW_BR + 1, :],
            ws[0:LRU_WIDTH, :], vs[ROW_BA:ROW_BA + 1, :], ws[LRU_WIDTH:2 * LRU_WIDTH, :], vs[ROW_BX:ROW_BX + 1, :],
            vs[ROW_LAM:ROW_LAM + 1, :], vs[ROW_GNC:ROW_GNC + 1, 0:CONV_WIDTH], vs[ROW_GNR:ROW_GNR + 1, :],
            vs[ROW_GMLP:ROW_GMLP + 1, :], vs[ROW_GF:ROW_GF + 1, :],
        ]
        for k in range(n):
            gk = grads[k]
            delta, mn, vn = _adamw(w_refs[k][...], gk, m_refs[k][...], v_refs[k][...])
            outs[1 + 4 * k][...] = gk
            outs[2 + 4 * k][...] = delta
            outs[3 + 4 * k][...] = mn
            outs[4 + 4 * k][...] = vn

    whole = lambda a: pl.BlockSpec(a.shape, lambda i: (0,) * len(a.shape))
    out_shape = [jax.ShapeDtypeStruct((1, 1), F32)]
    for w in weights:
        out_shape += [jax.ShapeDtypeStruct(w.shape, F32)] * 4
    args = (vsum, wsum, g_cw, g_rw, *weights, *moments_m, *moments_v)
    return pl.pallas_call(
        body, grid=(1,), out_shape=out_shape, in_specs=[whole(a) for a in args], out_specs=[whole(s) for s in out_shape],
        compiler_params=_params(("arbitrary",), 32), name="update_small",
    )(*args)


def kernel(x, norm_mix_g, w_in, conv_w, rnn_conv_w, rnn_conv_b, w_a, b_a, w_x, b_x, lru_lambda, g_norm_conv, g_norm_rnn, w_out, norm_mlp_g, w_mlp_in, w_mlp_out, final_norm_g, loss_target, m_norm_mix_g, m_w_in, m_conv_w, m_rnn_conv_w, m_rnn_conv_b, m_w_a, m_b_a, m_w_x, m_b_x, m_lru_lambda, m_g_norm_conv, m_g_norm_rnn, m_w_out, m_norm_mlp_g, m_w_mlp_in, m_w_mlp_out, m_final_norm_g, v_norm_mix_g, v_w_in, v_conv_w, v_rnn_conv_w, v_rnn_conv_b, v_w_a, v_b_a, v_w_x, v_b_x, v_lru_lambda, v_g_norm_conv, v_g_norm_rnn, v_w_out, v_norm_mlp_g, v_w_mlp_in, v_w_mlp_out, v_final_norm_g):
    t_len = x.shape[1]
    my_id = 4 * lax.axis_index("x") + 2 * lax.axis_index("y") + lax.axis_index("c")
    tm = min(256, t_len)
    tb = min(512, t_len)
    tk = min(512, t_len)

    xs = x.reshape(t_len, D_MODEL)
    tgt = loss_target.reshape(t_len, D_MODEL)
    flat = lambda a: a.reshape(a.shape[-2:]) if a.ndim == 3 else a.reshape(1, -1)
    heads = lambda a: a.reshape(LRU_WIDTH, HEAD_DIM)

    win_shard, wout_shard, w1_shard, w2_shard, cp_shard = _prep_shards(
        flat(w_in), flat(w_out), flat(w_mlp_in), flat(w_mlp_out), flat(conv_w), flat(rnn_conv_w))

    u, h, win_t, cp_full = _in_proj(xs, flat(norm_mix_g), (win_shard, cp_shard), tb)
    cpack = cp_full.reshape(N_DEV, 8, 128)
    conv_full = jnp.transpose(cpack[:, 0:3, 0:64], (1, 0, 2)).reshape(3, CONV_WIDTH)
    rnn_full = jnp.transpose(cpack[:, 3:7, :], (1, 0, 2)).reshape(4, LRU_WIDTH)
    mixer_small = (conv_full, rnn_full, flat(rnn_conv_b), heads(w_a), flat(b_a), heads(w_x), flat(b_x),
                   flat(lru_lambda), flat(g_norm_conv), flat(g_norm_rnn))
    hs, y, xr, gate_r, gate_i, mult, w1_blk, wout_blk = _mixer_fwd(u, *mixer_small, (w1_shard, wout_shard), tm)
    wout_f = wout_blk.reshape(MIX_WIDTH, D_MODEL)
    x1, h2, z, w2_blk = _mlp_up(xs, y, flat(norm_mlp_g), wout_f, w1_blk, w2_shard, tb)
    dx1, dx2, vec_m, dpre = _mlp_down_bwd(x1, z, tgt, flat(norm_mlp_g), flat(final_norm_g), w1_blk,
                                          w2_blk.reshape(D_FF, D_MODEL), tb)
    (g_w1,) = _tn_weight_grad(h2, dpre, tk, "w_mlp_in_grad", col_blocks=N_DEV)
    (g_w2,) = _tn_weight_grad(z, dx2, tk, "w_mlp_out_grad")
    g_w2 = g_w2.reshape(N_DEV, D_FF // N_DEV, D_MODEL)
    g_wout, sib_w1, sib_w2 = _tn_weight_grad(y, dx1, tk, "w_out_grad", pair=(g_w1, g_w2))
    g_wout = g_wout.reshape(N_DEV, MIX_WIDTH // N_DEV, D_MODEL)
    hsend_w1, own_w1 = _pair_sum(g_w1, sib_w1, "pair_sum_w_mlp_in")
    hsend_w2, own_w2 = _pair_sum(g_w2, sib_w2, "pair_sum_w_mlp_out")
    du, vec_b, wab, landed_w1, landed_w2, sib_wout = _mixer_bwd(
        u, hs, dx1, (xr, gate_r, gate_i, mult), *mixer_small, wout_f, (hsend_w1, hsend_w2), g_wout, tm)
    hsend_wout, own_wout = _pair_sum(g_wout, sib_wout, "pair_sum_w_out")
    ax, ay, ac = lax.axis_index("x"), lax.axis_index("y"), lax.axis_index("c")
    chip_ids = jnp.stack([2 * cx + cy for cx, cy in [(ax, ay)] + _other_chips(ax, ay)]).astype(jnp.int32)
    core = jnp.reshape(ac, (1,)).astype(jnp.int32)
    tw = min(1024, t_len)
    g_others, landed_wout, vrecv_m, vrecv_b, wrecv = _w_in_grad_part(
        du, h, tw, "w_in_grad_others", chip_ids[1:4], chip=(hsend_wout,), small=(vec_m, vec_b, wab))
    g_own, sib_others = _w_in_grad_part(du, h, tw, "w_in_grad_own", chip_ids[0:1], halves=g_others)
    hsend_win = _pair_sum_parts(g_others, sib_others, core)
    grad_x, vec_x, landed_win, sib_own = _in_proj_bwd(du, dx1, xs, flat(norm_mix_g), win_t, tm, hsend_win, g_own)

    vsum, wsum = _final_small(vrecv_m, vrecv_b, wab, wrecv, vec_x)

    up_win = _update_w_in(g_own, sib_own, landed_win, flat(w_in), flat(m_w_in), flat(v_w_in), core, 256)
    up_wout = _update_sharded(own_wout, landed_wout, flat(w_out), flat(m_w_out), flat(v_w_out), 96, "update_w_out")
    up_w1 = _update_sharded(own_w1, landed_w1, flat(w_mlp_in), flat(m_w_mlp_in), flat(v_w_mlp_in), 256,
                            "update_w_mlp_in")
    up_w2 = _update_sharded(own_w2, landed_w2, flat(w_mlp_out), flat(m_w_mlp_out), flat(v_w_mlp_out), 256,
                            "update_w_mlp_out")

    g_cw = lax.dynamic_slice(vsum, (ROW_CW, 64 * my_id), (3, 64))
    g_rw = lax.dynamic_slice(vsum, (ROW_RW, 128 * my_id), (4, 128))
    small_w = (norm_mix_g, conv_w, rnn_conv_w, rnn_conv_b, w_a, b_a, w_x, b_x, lru_lambda, g_norm_conv, g_norm_rnn,
               norm_mlp_g, final_norm_g)
    small_m = (m_norm_mix_g, m_conv_w, m_rnn_conv_w, m_rnn_conv_b, m_w_a, m_b_a, m_w_x, m_b_x, m_lru_lambda,
               m_g_norm_conv, m_g_norm_rnn, m_norm_mlp_g, m_final_norm_g)
    small_v = (v_norm_mix_g, v_conv_w, v_rnn_conv_w, v_rnn_conv_b, v_w_a, v_b_a, v_w_x, v_b_x, v_lru_lambda,
               v_g_norm_conv, v_g_norm_rnn, v_norm_mlp_g, v_final_norm_g)
    is_heads = (False, False, False, False, True, False, True, False, False, False, False, False, False)
    as2d = lambda arrs: [heads(a) if hd else flat(a) for a, hd in zip(arrs, is_heads)]
    small_out = _update_small(vsum, wsum, g_cw, g_rw, as2d(small_w), as2d(small_m), as2d(small_v))
    loss = small_out[0].reshape(())

    names = ["norm_mix_g", "w_in", "conv_w", "rnn_conv_w", "rnn_conv_b", "w_a", "b_a", "w_x", "b_x", "lru_lambda",
             "g_norm_conv", "g_norm_rnn", "w_out", "norm_mlp_g", "w_mlp_in", "w_mlp_out", "final_norm_g"]
    originals = dict(zip(names, (norm_mix_g, w_in, conv_w, rnn_conv_w, rnn_conv_b, w_a, b_a, w_x, b_x, lru_lambda,
                                 g_norm_conv, g_norm_rnn, w_out, norm_mlp_g, w_mlp_in, w_mlp_out, final_norm_g)))
    results = {"w_in": up_win, "w_out": up_wout, "w_mlp_in": up_w1, "w_mlp_out": up_w2}
    small_names = ["norm_mix_g", "conv_w", "rnn_conv_w", "rnn_conv_b", "w_a", "b_a", "w_x", "b_x", "lru_lambda",
                   "g_norm_conv", "g_norm_rnn", "norm_mlp_g", "final_norm_g"]
    for k, nm in enumerate(small_names):
        results[nm] = small_out[1 + 4 * k:5 + 4 * k]
    out = [loss, grad_x.reshape(x.shape)]
    for kind in range(4):
        out += [results[nm][kind].reshape(originals[nm].shape) for nm in names]
    return tuple(out)
```

```python
import functools

import jax
import jax.numpy as jnp
from jax import lax
from jax.experimental import pallas as pl
from jax.experimental.pallas import tpu as pltpu

F32 = jnp.float32
BF16 = jnp.bfloat16

D_MODEL = 1024
HEAD_DIM = 64
CONV_WIDTH = 512
LRU_WIDTH = 1024
MIX_WIDTH = CONV_WIDTH + LRU_WIDTH
IN_COLS = 3 * CONV_WIDTH + 2 * LRU_WIDTH
D_FF = 4 * D_MODEL
GROUP = 256
EPS = 1e-6
LRU_C = 8.0
N_DEV = 8
SUB = 8

OFF_GB, OFF_GC, OFF_V, OFF_XR, OFF_G = 0, 512, 1024, 1536, 2560

ADAM_LR, ADAM_B1, ADAM_B2, ADAM_EPS, ADAM_WD, ADAM_STEP = 0.001, 0.9, 0.999, 1e-08, 0.01, 10
BC1 = 1.0 - ADAM_B1 ** ADAM_STEP
BC2 = 1.0 - ADAM_B2 ** ADAM_STEP

MIB = 1024 * 1024
MESH = pl.DeviceIdType.MESH

VEC_ROWS = 32
ROW_GF, ROW_GMLP, ROW_LOSS = 0, 1, 2
ROW_GNC, ROW_GNR, ROW_BR, ROW_BA, ROW_BX, ROW_LAM, ROW_CW, ROW_RW = 8, 9, 10, 11, 12, 13, 14, 17
ROW_GMIX = 24
ACC_GNC, ACC_GNR, ACC_BR, ACC_BA, ACC_BX, ACC_SP, ACC_CW, ACC_RW, N_ACC = 0, 1, 2, 3, 4, 5, 6, 9, 13


def _params(semantics=None, vmem_mib=48):
    return pltpu.CompilerParams(dimension_semantics=semantics, vmem_limit_bytes=vmem_mib * MIB)


def _rms(x):
    return lax.rsqrt(jnp.mean(x * x, axis=-1, keepdims=True) + EPS)


def _rms_bwd(dy, xhat, r, g):
    dyh = dy * g
    return r * (dyh - xhat * jnp.mean(dyh * xhat, axis=-1, keepdims=True))


def _sigmoid(x):
    return 0.5 + 0.5 * jnp.tanh(0.5 * x)


def _gelu(x):
    c0, c1 = 0.7978845608028654, 0.044715
    x2 = x * x
    t = jnp.tanh(x * (c0 + (c0 * c1) * x2))
    half = 0.5 + 0.5 * t
    ge = x * half
    dge = half + (0.5 * x) * (1.0 - t * t) * (c0 + (3.0 * c0 * c1) * x2)
    return ge, dge


def _softplus_neg(lam):
    z = -lam
    e = jnp.exp(-jnp.abs(z))
    return jnp.maximum(z, 0.0) + jnp.where(e < 1e-4, e * (1.0 - 0.5 * e), jnp.log(1.0 + e))


def _lru_gates(pa, px, sp_c):
    ra = _sigmoid(pa)
    ii = _sigmoid(px)
    la = -ra * sp_c
    a = jnp.exp(la)
    x2 = 2.0 * la
    series = -x2 * (1.0 + x2 * (0.5 + x2 * (1.0 / 6.0 + x2 * (1.0 / 24.0))))
    m2 = jnp.where(x2 > -0.01, series, 1.0 - a * a)
    inv_mult = lax.rsqrt(m2)
    mult = jnp.where(m2 > 0.0, m2 * inv_mult, 0.0)
    return ra, ii, a, mult, inv_mult


def _down(cur, prev, s, row):
    return jnp.where(row >= s, pltpu.roll(cur, s, 0), pltpu.roll(prev, s, 0))


def _up(cur, nxt, s, row):
    return jnp.where(row < SUB - s, pltpu.roll(cur, SUB - s, 0), pltpu.roll(nxt, SUB - s, 0))


def _scan8_fwd(a, b, row):
    for s in (1, 2, 4):
        m = row >= s
        a_sh = pltpu.roll(a, s, 0)
        b_sh = pltpu.roll(b, s, 0)
        b = jnp.where(m, a * b_sh + b, b)
        a = jnp.where(m, a * a_sh, a)
    return a, b


def _scan8_rev(a, b, row):
    for s in (1, 2, 4):
        m = row < SUB - s
        a_sh = pltpu.roll(a, SUB - s, 0)
        b_sh = pltpu.roll(b, SUB - s, 0)
        b = jnp.where(m, a * b_sh + b, b)
        a = jnp.where(m, a * a_sh, a)
    return a, b


def _group_mask(shape):
    r = lax.broadcasted_iota(jnp.int32, shape, 0)
    c = lax.broadcasted_iota(jnp.int32, shape, 1)
    return ((r % GROUP) // HEAD_DIM) == (c // HEAD_DIM)


def _expand_heads(w):
    j = lax.broadcasted_iota(jnp.int32, (HEAD_DIM, GROUP), 0)
    c = lax.broadcasted_iota(jnp.int32, (HEAD_DIM, GROUP), 1)
    spread = (c % HEAD_DIM == j).astype(BF16)
    e = jnp.dot(w.astype(BF16), spread, preferred_element_type=F32)
    return jnp.where(_group_mask(e.shape), e, 0.0).astype(BF16)


def _fold_heads(p):
    p = jnp.where(_group_mask(p.shape), p, 0.0)
    c = lax.broadcasted_iota(jnp.int32, (GROUP, HEAD_DIM), 0)
    j = lax.broadcasted_iota(jnp.int32, (GROUP, HEAD_DIM), 1)
    fold = (c % HEAD_DIM == j).astype(BF16)
    hi = p.astype(BF16)
    rest = p - hi.astype(F32)
    mid = rest.astype(BF16)
    lo = (rest - mid.astype(F32)).astype(BF16)
    dot = functools.partial(jnp.dot, preferred_element_type=F32)
    return dot(hi, fold) + dot(mid, fold) + dot(lo, fold)


def _block_diag_apply(xb, wbd_ref):
    parts = [jnp.dot(xb[:, g * GROUP:(g + 1) * GROUP], wbd_ref[g * GROUP:(g + 1) * GROUP, :],
                     preferred_element_type=F32) for g in range(LRU_WIDTH // GROUP)]
    return jnp.concatenate(parts, axis=1)


def _block_diag_apply_t(db, wbd_ref):
    parts = [lax.dot_general(db[:, g * GROUP:(g + 1) * GROUP], wbd_ref[g * GROUP:(g + 1) * GROUP, :],
                             (((1,), (1,)), ((), ())), preferred_element_type=F32)
             for g in range(LRU_WIDTH // GROUP)]
    return jnp.concatenate(parts, axis=1)


def _dot_nt(a, b):
    return lax.dot_general(a, b, (((1,), (1,)), ((), ())), preferred_element_type=F32)


def _dot_tn(a, b):
    return lax.dot_general(a, b, (((0,), (0,)), ((), ())), preferred_element_type=F32)


def _chunk_loop(n_chunks, chunk, init, in_flight=4):
    def body(k, carry):
        for j in range(in_flight):
            carry = chunk(k * in_flight + j, carry)
        return carry

    return lax.fori_loop(0, n_chunks // in_flight, body, init)


def _place():
    x, y, c = lax.axis_index("x"), lax.axis_index("y"), lax.axis_index("c")
    return x, y, c


def _block_id(chip, core):
    return 4 * chip[0] + 2 * chip[1] + core


def _other_chips(x, y):
    return [(1 - x, y), (x, 1 - y), (1 - x, 1 - y)]


def _remote_copy(src, dst, send_sem, recv_sem, to):
    return pltpu.make_async_remote_copy(src_ref=src, dst_ref=dst, send_sem=send_sem, recv_sem=recv_sem,
                                        device_id=to, device_id_type=MESH)


HBM_SPEC = pl.BlockSpec(memory_space=pl.ANY)


def _in_hbm(*arrays):
    return [pltpu.with_memory_space_constraint(a, pltpu.HBM) for a in arrays]


def _prep_shards(w_in_t, w_out, w_mlp_in, w_mlp_out, conv_w, rnn_conv_w):
    def body(win_ref, wout_ref, w1_ref, w2_ref, cw_ref, rw_ref, o_win, o_wout, o_w1, o_w2, o_cp):
        o_win[...] = win_ref[...].astype(BF16)
        o_wout[...] = wout_ref[...].astype(BF16)
        o_w1[...] = w1_ref[...].astype(BF16)
        o_w2[...] = w2_ref[...].astype(BF16)
        o_cp[...] = jnp.zeros(o_cp.shape, F32)
        o_cp[0:3, 0:64] = cw_ref[...]
        o_cp[3:7, :] = rw_ref[...]

    whole = lambda shape: pl.BlockSpec(shape, lambda i: (0,) * len(shape))
    args = (w_in_t, w_out, w_mlp_in, w_mlp_out, conv_w, rnn_conv_w)
    shapes = [(w_in_t.shape, BF16), (w_out.shape, BF16), (w_mlp_in.shape, BF16), (w_mlp_out.shape, BF16),
              ((8, 128), F32)]
    return pl.pallas_call(
        body, grid=(1,), out_shape=[jax.ShapeDtypeStruct(s, d) for s, d in shapes],
        in_specs=[whole(a.shape) for a in args], out_specs=[whole(s) for s, _ in shapes],
        compiler_params=_params(("arbitrary",), 40), name="prep_shards",
    )(*args)


def _host_all_gather(step, n_steps, shards, fulls, send_sems, recv_sems, local_sems):
    x, y, c = _place()
    me = (x, y, c)
    my_id = _block_id((x, y), c)
    sibling = (x, y, 1 - c)
    chips = _other_chips(x, y)
    n_arr = len(shards)

    def copy(arr, k, block, to, src=None):
        dst = fulls[arr].at[block]
        return _remote_copy(dst if src is None else src, dst, send_sems.at[arr, k], recv_sems.at[arr, k], to)

    def local(arr):
        return pltpu.make_async_copy(shards[arr], fulls[arr].at[my_id], local_sems.at[arr])

    @pl.when(step == 0)
    def _():
        for arr in range(n_arr):
            local(arr).start()
            copy(arr, 0, my_id, sibling, shards[arr]).start()
            for j, chip in enumerate(chips):
                copy(arr, 1 + j, my_id, (*chip, c), shards[arr]).start()

    @pl.when(step == max(n_steps - 2, 0))
    def _():
        for j, chip in enumerate(chips):
            for arr in range(n_arr):
                copy(arr, 1 + j, _block_id(chip, c), me).wait_recv()
                copy(arr, 4 + j, _block_id(chip, c), sibling).start()

    @pl.when(step == n_steps - 1)
    def _():
        for arr in range(n_arr):
            copy(arr, 0, _block_id((x, y), 1 - c), me).wait_recv()
            for j, chip in enumerate(chips):
                copy(arr, 4 + j, _block_id(chip, 1 - c), me).wait_recv()
            for k in range(4):
                copy(arr, k, my_id, me, shards[arr]).wait_send()
            for j, chip in enumerate(chips):
                copy(arr, 4 + j, _block_id(chip, c), me).wait_send()
            local(arr).wait()


def _host_pair_exchange(step, n_steps, gs, sibs, send_sems, recv_sems):
    x, y, c = _place()
    sibling = (x, y, 1 - c)
    chips = [(x, y)] + _other_chips(x, y)

    def d2d(arr, q):
        return _remote_copy(gs[arr].at[_block_id(chips[q], 1 - c)], sibs[arr].at[q],
                            send_sems.at[arr, q], recv_sems.at[arr, q], sibling)

    @pl.when(step == 0)
    def _():
        for arr in range(len(gs)):
            for q in (1, 2, 3, 0):
                d2d(arr, q).start()

    @pl.when(step == n_steps - 1)
    def _():
        for arr in range(len(gs)):
            for q in range(4):
                d2d(arr, q).wait()


def _host_chip_exchange(step, n_steps, hsends, hrecvs, send_sems, recv_sems):
    x, y, c = _place()
    chips = _other_chips(x, y)

    def ici(arr, j):
        return _remote_copy(hsends[arr].at[j], hrecvs[arr].at[j], send_sems.at[arr, j], recv_sems.at[arr, j],
                            (*chips[j], c))

    @pl.when(step == 0)
    def _():
        for arr in range(len(hsends)):
            for j in range(3):
                ici(arr, j).start()

    @pl.when(step == n_steps - 1)
    def _():
        for arr in range(len(hsends)):
            for j in range(3):
                ici(arr, j).wait()


def _host_half_exchange(step, n_steps, parts, sibs, send_sems, recv_sems):
    x, y, c = _place()
    n_q, rows2, _ = parts.shape
    half = rows2 // 2

    def d2d(q):
        src = parts.at[q, pl.ds(pl.multiple_of((1 - c) * half, 16), half), :]
        return _remote_copy(src, sibs.at[q], send_sems.at[q], recv_sems.at[q], (x, y, 1 - c))

    @pl.when(step == 0)
    def _():
        for q in range(n_q):
            d2d(q).start()

    @pl.when(step == n_steps - 1)
    def _():
        for q in range(n_q):
            d2d(q).wait()


def _peer(x, y, c, k):
    return (x ^ ((k >> 2) & 1), y ^ ((k >> 1) & 1), c ^ (k & 1))


def _host_small_exchange(step, n_steps, vec_m, vec_b, wab, vrecv_m, vrecv_b, wrecv, send_sems, recv_sems, local_sems):
    x, y, c = _place()
    my_id = _block_id((x, y), c)
    wrows = wab.shape[0] // N_DEV

    def copies(k):
        to = _peer(x, y, c, k)
        block = wab.at[pl.ds(pl.multiple_of(_block_id(to[0:2], to[2]) * wrows, SUB), wrows), :]
        return [_remote_copy(vec_m, vrecv_m.at[my_id], send_sems.at[0, k], recv_sems.at[0, k], to),
                _remote_copy(vec_b, vrecv_b.at[my_id], send_sems.at[1, k], recv_sems.at[1, k], to),
                _remote_copy(block, wrecv.at[k], send_sems.at[2, k], recv_sems.at[2, k], to)]

    mine = [pltpu.make_async_copy(vec_m, vrecv_m.at[my_id], local_sems.at[0]),
            pltpu.make_async_copy(vec_b, vrecv_b.at[my_id], local_sems.at[1])]

    @pl.when(step == 0)
    def _():
        for cp in mine:
            cp.start()
        for k in range(1, N_DEV):
            for cp in copies(k):
                cp.start()

    @pl.when(step == n_steps - 1)
    def _():
        for k in range(1, N_DEV):
            for cp in copies(k):
                cp.wait()
        for cp in mine:
            cp.wait()


def _pair_sum_parts(parts, sibs, core):
    n_q, rows2, cols = parts.shape
    half = rows2 // 2

    def body(core_ref, g_ref, s_ref, o_ref):
        o_ref[0] = (g_ref[0, 0].astype(F32) + s_ref[0].astype(F32)).astype(BF16)

    block = (1, half, cols)
    grid_spec = pltpu.PrefetchScalarGridSpec(
        num_scalar_prefetch=1, grid=(n_q,),
        in_specs=[pl.BlockSpec((1, 1, half, cols), lambda q, cr: (q, cr[0], 0, 0)),
                  pl.BlockSpec(block, lambda q, cr: (q, 0, 0))],
        out_specs=pl.BlockSpec(block, lambda q, cr: (q, 0, 0)))
    return pl.pallas_call(
        body, grid_spec=grid_spec, out_shape=pltpu.HBM((n_q, half, cols), BF16),
        compiler_params=_params(("arbitrary",), 32), name="pair_sum_w_in",
    )(core, *_in_hbm(parts.reshape(n_q, 2, half, cols), sibs))


def _pair_sum(g, sib, name):
    _, rows, cols = g.shape
    x, y, c = _place()
    slots = jnp.stack([_block_id(chip, c) for chip in [(x, y)] + _other_chips(x, y)]).astype(jnp.int32)

    def body(slots_ref, g_ref, sib_ref, hs_ref, own_ref):
        q = pl.program_id(0)
        both = g_ref[0].astype(F32) + sib_ref[0].astype(F32)

        @pl.when(q == 0)
        def _():
            own_ref[...] = both

        @pl.when(q > 0)
        def _():
            hs_ref[0] = both.astype(BF16)

    block = (1, rows, cols)
    grid_spec = pltpu.PrefetchScalarGridSpec(
        num_scalar_prefetch=1, grid=(4,),
        in_specs=[pl.BlockSpec(block, lambda q, s: (s[q], 0, 0)), pl.BlockSpec(block, lambda q, s: (q, 0, 0))],
        out_specs=[pl.BlockSpec(block, lambda q, s: (jnp.maximum(q - 1, 0), 0, 0)),
                   pl.BlockSpec((rows, cols), lambda q, s: (0, 0))])
    return pl.pallas_call(
        body, grid_spec=grid_spec,
        out_shape=(pltpu.HBM((3, rows, cols), BF16), pltpu.HBM((rows, cols), F32)),
        compiler_params=_params(("arbitrary",), 32), name=name,
    )(slots, *_in_hbm(g, sib))


def _exchange_scratch(n_arr, n_copies):
    return [pltpu.SemaphoreType.DMA((n_arr, n_copies)), pltpu.SemaphoreType.DMA((n_arr, n_copies))]


def _final_small(vrecv_m, vrecv_b, wab, wrecv, vec_x):
    wrows = wab.shape[0] // N_DEV

    def body(vm_ref, vb_ref, w_ref, wr_ref, vx_ref, o_vec, o_w, xrecv, wred, x_send, x_recv, b_send, b_recv):
        x, y, c = _place()
        my_id = _block_id((x, y), c)
        my_rows = pl.ds(pl.multiple_of(my_id * wrows, SUB), wrows)

        def xcopy(k):
            return _remote_copy(vx_ref, xrecv.at[my_id], x_send.at[k], x_recv.at[k], _peer(x, y, c, k))

        def bcopy(k):
            return _remote_copy(wred, o_w.at[my_rows, :], b_send.at[k], b_recv.at[k], _peer(x, y, c, k))

        xrecv[my_id] = vx_ref[...]
        for k in range(1, N_DEV):
            xcopy(k).start()
        red = w_ref[my_rows, :]
        for k in range(1, N_DEV):
            red = red + wr_ref[k]
        wred[...] = red
        o_w[my_rows, :] = red
        for k in range(1, N_DEV):
            bcopy(k).start()
        for k in range(1, N_DEV):
            xcopy(k).wait_recv()
        for rows, ref in ((slice(0, 8), vm_ref), (slice(8, 24), vb_ref), (slice(24, 32), xrecv)):
            tot = ref[0]
            for s in range(1, N_DEV):
                tot = tot + ref[s]
            o_vec[rows, :] = tot
        for k in range(1, N_DEV):
            bcopy(k).wait_recv()
        for k in range(1, N_DEV):
            xcopy(k).wait_send()
            bcopy(k).wait_send()

    vm = pl.BlockSpec(memory_space=pltpu.VMEM)
    dma8 = pltpu.SemaphoreType.DMA((N_DEV,))
    return pl.pallas_call(
        body, out_shape=(jax.ShapeDtypeStruct((VEC_ROWS, D_MODEL), F32), jax.ShapeDtypeStruct(wab.shape, F32)),
        in_specs=[vm] * 5, out_specs=[vm] * 2,
        scratch_shapes=[pltpu.VMEM((N_DEV, SUB, D_MODEL), F32), pltpu.VMEM((wrows, HEAD_DIM), F32),
                        dma8, dma8, dma8, dma8],
        compiler_params=_params(vmem_mib=32), name="final_small",
    )(vrecv_m, vrecv_b, wab, wrecv, vec_x)


def _in_proj(x, g_mix, shards, tm):
    t_len = x.shape[0]
    n_t = t_len // tm
    n_arr = len(shards)
    rows = [s.shape[0] for s in shards]
    width = 2 * rows[0]
    ax, ay = lax.axis_index("x"), lax.axis_index("y")
    order = jnp.stack([2 * cx + cy for cx, cy in [(ax, ay)] + _other_chips(ax, ay)]).astype(jnp.int32)

    def body(order_ref, x_ref, g_ref, *rest):
        shard_refs = rest[0:n_arr]
        u_ref, h_ref = rest[n_arr:n_arr + 2]
        fulls = rest[n_arr + 2:2 * n_arr + 2]
        h_s, wbuf, send_sems, recv_sems, local_sems, load_sem = rest[2 * n_arr + 2:]
        p = pl.program_id(0)
        i = pl.program_id(1)
        x_, y_, c = _place()
        me = (x_, y_, c)
        my_id = _block_id((x_, y_), c)
        sibling = (x_, y_, 1 - c)
        chips = _other_chips(x_, y_)

        def block(arr, blk):
            return fulls[arr].at[pl.ds(pl.multiple_of(blk * rows[arr], rows[arr]), rows[arr]), :]

        def copy(arr, k, blk, to, src=None):
            dst = block(arr, blk)
            return _remote_copy(dst if src is None else src, dst, send_sems.at[arr, k], recv_sems.at[arr, k], to)

        def local(arr):
            return pltpu.make_async_copy(shard_refs[arr], block(arr, my_id), local_sems.at[arr])

        def load_chip(chip):
            start = pl.multiple_of((2 * chip[0] + chip[1]) * width, width)
            cp = pltpu.make_async_copy(fulls[0].at[pl.ds(start, width), :], wbuf, load_sem.at[0])
            cp.start()
            cp.wait()

        @pl.when((p == 0) & (i == 0))
        def _():
            for arr in range(n_arr):
                local(arr).start()
                copy(arr, 0, my_id, sibling, shard_refs[arr]).start()
                for j in (0, 1):
                    copy(arr, 1 + j, my_id, (*chips[j], c), shard_refs[arr]).start()
            for arr in range(n_arr):
                local(arr).wait()
                copy(arr, 0, _block_id((x_, y_), 1 - c), me).wait_recv()
            load_chip((x_, y_))

        for j, chip in enumerate(chips):
            @pl.when((p == j + 1) & (i == 0))
            def _(j=j, chip=chip):
                for arr in range(n_arr):
                    copy(arr, 1 + j, _block_id(chip, c), me).wait_recv()
                    copy(arr, 4 + j, _block_id(chip, c), sibling).start()
                    if j == 0:
                        copy(arr, 3, my_id, (*chips[2], c), shard_refs[arr]).start()
                for arr in range(n_arr):
                    copy(arr, 4 + j, _block_id(chip, 1 - c), me).wait_recv()
                load_chip(chip)

        @pl.when((p == 3) & (i == n_t - 1))
        def _():
            for arr in range(n_arr):
                for k in range(4):
                    copy(arr, k, my_id, me, shard_refs[arr]).wait_send()
                for j, chip in enumerate(chips):
                    copy(arr, 4 + j, _block_id(chip, c), me).wait_send()

        tile = pl.ds(pl.multiple_of(i * tm, tm), tm)

        @pl.when(p == 0)
        def _():
            xv = x_ref[...]
            h = (xv * _rms(xv) * g_ref[...]).astype(BF16)
            h_ref[...] = h
            h_s[tile, :] = h

        u_ref[...] = _dot_nt(h_s[tile, :], wbuf[...])

    first_pass = lambda p, i, o: (jnp.where(p == 0, i, n_t - 1), 0)
    grid_spec = pltpu.PrefetchScalarGridSpec(
        num_scalar_prefetch=1, grid=(4, n_t),
        in_specs=[pl.BlockSpec((tm, D_MODEL), first_pass), pl.BlockSpec((1, D_MODEL), lambda p, i, o: (0, 0))]
        + [HBM_SPEC] * n_arr,
        out_specs=[pl.BlockSpec((tm, width), lambda p, i, o: (i, o[p])), pl.BlockSpec((tm, D_MODEL), first_pass)]
        + [HBM_SPEC] * n_arr,
        scratch_shapes=[pltpu.VMEM((t_len, D_MODEL), BF16), pltpu.VMEM((width, D_MODEL), BF16)]
        + _exchange_scratch(n_arr, 7) + [pltpu.SemaphoreType.DMA((n_arr,)), pltpu.SemaphoreType.DMA((1,))])
    return pl.pallas_call(
        body, grid_spec=grid_spec,
        out_shape=[jax.ShapeDtypeStruct((t_len, IN_COLS), F32), jax.ShapeDtypeStruct((t_len, D_MODEL), BF16)]
        + [jax.ShapeDtypeStruct((N_DEV * s.shape[0], s.shape[1]), s.dtype) for s in shards],
        compiler_params=_params(("arbitrary", "arbitrary"), 48), name="in_proj",
    )(order, x, g_mix, *shards)


def _conv3_chunk(u_ref, r, cv_prev, cw, row):
    gb = u_ref[pl.ds(r, SUB), OFF_GB:OFF_GB + CONV_WIDTH]
    gc = u_ref[pl.ds(r, SUB), OFF_GC:OFF_GC + CONV_WIDTH]
    v = u_ref[pl.ds(r, SUB), OFF_V:OFF_V + CONV_WIDTH]
    cv = gc * v
    cv_m1 = _down(cv, cv_prev, 1, row)
    cv_m2 = _down(cv, cv_prev, 2, row)
    cq = cw[2:3, :] * cv + cw[1:2, :] * cv_m1 + cw[0:1, :] * cv_m2
    return gb, gc, v, cv, cv_m1, cv_m2, cq


def _conv4_chunk(u_ref, r, xin_prev, rw, rb, row):
    xin = u_ref[pl.ds(r, SUB), OFF_XR:OFF_XR + LRU_WIDTH]
    m1 = _down(xin, xin_prev, 1, row)
    m2 = _down(xin, xin_prev, 2, row)
    m3 = _down(xin, xin_prev, 3, row)
    xr = rw[3:4, :] * xin + rw[2:3, :] * m1 + rw[1:2, :] * m2 + rw[0:1, :] * m3 + rb
    return xin, m1, m2, m3, xr


def _mixer_fwd(u, conv_w, rnn_conv_w, rnn_conv_b, wa, b_a, wx, b_x, lam, gnc, gnr, shards, tm):
    t_len = u.shape[0]
    n_steps = t_len // tm
    n_chunks = tm // SUB
    n_arr = len(shards)

    def body(u_ref, cw_ref, rw_ref, rb_ref, wa_ref, ba_ref, wx_ref, bx_ref, lam_ref, gnc_ref, gnr_ref, *rest):
        shard_refs = rest[0:n_arr]
        hs_ref, y_ref, xr_s, ra_ref, ii_ref, mult_ref = rest[n_arr:n_arr + 6]
        fulls = rest[n_arr + 6:2 * n_arr + 6]
        (y_s, pa_s, px_s, wabd, wxbd, cv_car, xin_car, h_car,
         send_sems, recv_sems, local_sems) = rest[2 * n_arr + 6:]
        _host_all_gather(pl.program_id(0), n_steps, shard_refs, fulls, send_sems, recv_sems, local_sems)

        @pl.when(pl.program_id(0) == 0)
        def _():
            cv_car[...] = jnp.zeros(cv_car.shape, F32)
            xin_car[...] = jnp.zeros(xin_car.shape, F32)
            h_car[...] = jnp.zeros(h_car.shape, F32)
            wabd[...] = _expand_heads(wa_ref[...])
            wxbd[...] = _expand_heads(wx_ref[...])

        row_c = lax.broadcasted_iota(jnp.int32, (SUB, CONV_WIDTH), 0)
        row_r = lax.broadcasted_iota(jnp.int32, (SUB, LRU_WIDTH), 0)
        cw = cw_ref[...]
        rw = rw_ref[...]
        rb = rb_ref[...]
        g_c = gnc_ref[...]
        g_r = gnr_ref[...]
        sp_c = LRU_C * _softplus_neg(lam_ref[...])

        def convs(i, carry):
            cv_prev, xin_prev = carry
            r = pl.multiple_of(i * SUB, SUB)
            gb, _, _, cv, _, _, cq = _conv3_chunk(u_ref, r, cv_prev, cw, row_c)
            y_c = gb * cq
            y_s[pl.ds(r, SUB), 0:CONV_WIDTH] = y_c * _rms(y_c) * g_c
            xin, _, _, _, xr = _conv4_chunk(u_ref, r, xin_prev, rw, rb, row_r)
            xr_s[pl.ds(r, SUB), :] = xr
            return cv, xin

        cv_last, xin_last = _chunk_loop(n_chunks, convs, (cv_car[...], xin_car[...]))
        cv_car[...] = cv_last
        xin_car[...] = xin_last

        xrb = xr_s[...].astype(BF16)
        pa_s[...] = _block_diag_apply(xrb, wabd) + ba_ref[...]
        px_s[...] = _block_diag_apply(xrb, wxbd) + bx_ref[...]

        def recur(i, h_prev):
            r = pl.multiple_of(i * SUB, SUB)
            xr = xr_s[pl.ds(r, SUB), :]
            ra, ii, a, mult, _ = _lru_gates(pa_s[pl.ds(r, SUB), :], px_s[pl.ds(r, SUB), :], sp_c)
            ra_ref[pl.ds(r, SUB), :] = ra
            ii_ref[pl.ds(r, SUB), :] = ii
            mult_ref[pl.ds(r, SUB), :] = mult
            a_cum, b_cum = _scan8_fwd(a, mult * ii * xr, row_r)
            h = a_cum * h_prev + b_cum
            hs_ref[pl.ds(r, SUB), :] = h
            ge, _ = _gelu(u_ref[pl.ds(r, SUB), OFF_G:OFF_G + LRU_WIDTH])
            y_r = h * ge
            y_s[pl.ds(r, SUB), CONV_WIDTH:MIX_WIDTH] = y_r * _rms(y_r) * g_r
            return h[SUB - 1:SUB, :]

        h_car[...] = _chunk_loop(n_chunks, recur, h_car[...])

        y_ref[...] = y_s[...].astype(BF16)

    row_tile = lambda w: pl.BlockSpec((tm, w), lambda i: (i, 0))
    whole = lambda a: pl.BlockSpec(a.shape, lambda i: (0,) * a.ndim)
    smalls = (conv_w, rnn_conv_w, rnn_conv_b, wa, b_a, wx, b_x, lam, gnc, gnr)
    return pl.pallas_call(
        body, grid=(n_steps,),
        in_specs=[row_tile(IN_COLS)] + [whole(a) for a in smalls] + [HBM_SPEC] * n_arr,
        out_specs=[row_tile(LRU_WIDTH), row_tile(MIX_WIDTH)] + [row_tile(LRU_WIDTH)] * 4 + [HBM_SPEC] * n_arr,
        out_shape=[jax.ShapeDtypeStruct((t_len, LRU_WIDTH), F32), jax.ShapeDtypeStruct((t_len, MIX_WIDTH), BF16)]
        + [jax.ShapeDtypeStruct((t_len, LRU_WIDTH), F32)] * 4
        + [jax.ShapeDtypeStruct((N_DEV,) + s.shape, BF16) for s in shards],
        scratch_shapes=[pltpu.VMEM((tm, MIX_WIDTH), F32),
                        pltpu.VMEM((tm, LRU_WIDTH), F32), pltpu.VMEM((tm, LRU_WIDTH), F32),
                        pltpu.VMEM((LRU_WIDTH, GROUP), BF16), pltpu.VMEM((LRU_WIDTH, GROUP), BF16),
                        pltpu.VMEM((SUB, CONV_WIDTH), F32), pltpu.VMEM((SUB, LRU_WIDTH), F32),
                        pltpu.VMEM((1, LRU_WIDTH), F32)]
        + _exchange_scratch(n_arr, 7) + [pltpu.SemaphoreType.DMA((n_arr,))],
        compiler_params=_params(("arbitrary",), 56), name="mixer_fwd",
    )(u, *smalls, *shards)


def _mlp_up(x, y, g_mlp, w_out, w1, w2_shard, tm):
    t_len = x.shape[0]
    n_steps = t_len // tm
    n_blk, _, blk = w1.shape

    def body(x_ref, y_ref, gm_ref, wout_hbm, w1_hbm, w2_ref, x1_ref, h2_ref, z_ref, w2_full,
             wout_s, w1_s, sem, send_sems, recv_sems, local_sems):
        step = pl.program_id(0)
        _host_all_gather(step, n_steps, [w2_ref], [w2_full], send_sems, recv_sems, local_sems)

        load_wout = pltpu.make_async_copy(wout_hbm, wout_s, sem.at[0])
        load_w1 = pltpu.make_async_copy(w1_hbm, w1_s, sem.at[1])

        @pl.when(step == 0)
        def _():
            load_wout.start()
            load_w1.start()
            load_wout.wait()

        x1v = x_ref[...] + jnp.dot(y_ref[...], wout_s[...], preferred_element_type=F32)
        x1_ref[...] = x1v
        h2 = (x1v * _rms(x1v) * gm_ref[...]).astype(BF16)
        h2_ref[...] = h2

        @pl.when(step == 0)
        def _():
            load_w1.wait()

        for k in range(n_blk):
            rp = jnp.maximum(jnp.dot(h2, w1_s[k], preferred_element_type=F32), 0.0)
            z_ref[:, k * blk:(k + 1) * blk] = (rp * rp).astype(BF16)

    row_tile = lambda w: pl.BlockSpec((tm, w), lambda i: (i, 0))
    return pl.pallas_call(
        body, grid=(n_steps,),
        in_specs=[row_tile(D_MODEL), row_tile(MIX_WIDTH), pl.BlockSpec((1, D_MODEL), lambda i: (0, 0)),
                  HBM_SPEC, HBM_SPEC, HBM_SPEC],
        out_specs=[row_tile(D_MODEL), row_tile(D_MODEL), row_tile(D_FF), HBM_SPEC],
        out_shape=[jax.ShapeDtypeStruct((t_len, D_MODEL), F32), jax.ShapeDtypeStruct((t_len, D_MODEL), BF16),
                   jax.ShapeDtypeStruct((t_len, D_FF), BF16), jax.ShapeDtypeStruct((N_DEV,) + w2_shard.shape, BF16)],
        scratch_shapes=[pltpu.VMEM(w_out.shape, BF16), pltpu.VMEM(w1.shape, BF16), pltpu.SemaphoreType.DMA((2,))]
        + _exchange_scratch(1, 7) + [pltpu.SemaphoreType.DMA((1,))],
        compiler_params=_params(("arbitrary",), 48), name="mlp_up",
    )(x, y, g_mlp, w_out, w1, w2_shard)


def _mlp_down_bwd(x1, z, target, g_mlp, g_f, w1, w2, tm):
    t_len = x1.shape[0]
    n_steps = t_len // tm
    n_blk, _, blk = w1.shape

    def body(x1_ref, z_ref, tg_ref, gm_ref, gf_ref, w1_hbm, w2_hbm, dx1_ref, dx2_ref, vec_ref, dpre_hbm,
             w1_s, w2_s, dp_s, sem, out_sem):
        step = pl.program_id(0)
        rows = pl.ds(pl.multiple_of(step * tm, tm), tm)
        dp_out = pltpu.make_async_copy(dp_s, dpre_hbm.at[rows, :], out_sem.at[0])

        load_w1 = pltpu.make_async_copy(w1_hbm, w1_s, sem.at[0])
        load_w2 = pltpu.make_async_copy(w2_hbm, w2_s, sem.at[1])

        @pl.when(step == 0)
        def _():
            load_w2.start()
            load_w1.start()
            vec_ref[...] = jnp.zeros(vec_ref.shape, F32)
            load_w2.wait()

        x1v = x1_ref[...]
        g_m = gm_ref[...]
        g_o = gf_ref[...]
        r2 = _rms(x1v)
        x1h = x1v * r2
        x2 = x1v + jnp.dot(z_ref[...], w2_s[...], preferred_element_type=F32)
        r3 = _rms(x2)
        x2h = x2 * r3
        err = x2h * g_o - tg_ref[...]
        dout = err * (1.0 / D_MODEL)
        vec_ref[ROW_LOSS:ROW_LOSS + 1, :] += (0.5 / D_MODEL) * jnp.sum(err * err, axis=0, keepdims=True)
        vec_ref[ROW_GF:ROW_GF + 1, :] += jnp.sum(dout * x2h, axis=0, keepdims=True)
        dx2 = _rms_bwd(dout, x2h, r3, g_o)
        dx2b = dx2.astype(BF16)
        dx2_ref[...] = dx2b
        dh2 = jnp.zeros((tm, D_MODEL), F32)

        @pl.when(step > 0)
        def _():
            dp_out.wait()

        @pl.when(step == 0)
        def _():
            load_w1.wait()

        for k in range(n_blk):
            cols = slice(k * blk, (k + 1) * blk)
            dz = _dot_nt(dx2b, w2_s[cols, :])
            dpb = (dz * 2.0 * jnp.sqrt(z_ref[:, cols].astype(F32))).astype(BF16)
            dp_s[:, cols] = dpb
            dh2 = dh2 + _dot_nt(dpb, w1_s[k])
        dp_out.start()
        vec_ref[ROW_GMLP:ROW_GMLP + 1, :] += jnp.sum(dh2 * x1h, axis=0, keepdims=True)
        dx1_ref[...] = dx2 + _rms_bwd(dh2, x1h, r2, g_m)

        @pl.when(step == n_steps - 1)
        def _():
            dp_out.wait()

    row_tile = lambda w: pl.BlockSpec((tm, w), lambda i: (i, 0))
    vec_spec = pl.BlockSpec((1, D_MODEL), lambda i: (0, 0))
    return pl.pallas_call(
        body, grid=(n_steps,),
        in_specs=[row_tile(D_MODEL), row_tile(D_FF), row_tile(D_MODEL), vec_spec, vec_spec, HBM_SPEC, HBM_SPEC],
        out_specs=[row_tile(D_MODEL), row_tile(D_MODEL), pl.BlockSpec((SUB, D_MODEL), lambda i: (0, 0)), HBM_SPEC],
        out_shape=[jax.ShapeDtypeStruct((t_len, D_MODEL), F32), jax.ShapeDtypeStruct((t_len, D_MODEL), BF16),
                   jax.ShapeDtypeStruct((SUB, D_MODEL), F32), jax.ShapeDtypeStruct((t_len, D_FF), BF16)],
        scratch_shapes=[pltpu.VMEM(w1.shape, BF16), pltpu.VMEM(w2.shape, BF16), pltpu.VMEM((tm, D_FF), BF16),
                        pltpu.SemaphoreType.DMA((2,)), pltpu.SemaphoreType.DMA((1,))],
        compiler_params=_params(("arbitrary",), 56), name="mlp_down_bwd",
    )(x1, z, target, g_mlp, g_f, w1, w2)


def _mixer_bwd(u, hs, dx1, saved, conv_w, rnn_conv_w, rnn_conv_b, wa, b_a, wx, b_x, lam, gnc, gnr, w_out,
               chip_sums, g_wout, tm):
    t_len = u.shape[0]
    n_tiles = t_len // tm
    n_chunks = tm // SUB
    per_tile = tm // SUB
    n_sums = len(chip_sums)

    def body(u_ref, up_ref, hs_ref, hp_ref, dx1_ref, xr_ref, ra_ref, ii_ref, mult_ref,
             cw_ref, rw_ref, rb_ref, wa_ref, ba_ref, wx_ref, bx_ref, lam_ref, gnc_ref, gnr_ref, wout_ref, *rest):
        hsends = rest[0:n_sums]
        gwout_ref = rest[n_sums]
        du_ref, vec_ref, wab_ref = rest[n_sums + 1:n_sums + 4]
        hrecvs = rest[n_sums + 4:2 * n_sums + 4]
        sib_wout = rest[2 * n_sums + 4]
        (du_s, dy_s, dpa_s, dpx_s, dxr_s, wabd, wxbd, acc, dwa_acc, dwx_acc,
         a_car, dh_car, dcq_car, dxr_car, i_send, i_recv, d_send, d_recv) = rest[2 * n_sums + 5:]
        step = pl.program_id(0)
        _host_chip_exchange(step, n_tiles, hsends, hrecvs, i_send, i_recv)
        _host_pair_exchange(step, n_tiles, [gwout_ref], [sib_wout], d_send, d_recv)
        has_prev = (step < n_tiles - 1).astype(F32)

        @pl.when(step == 0)
        def _():
            acc[...] = jnp.zeros(acc.shape, F32)
            dwa_acc[...] = jnp.zeros(dwa_acc.shape, F32)
            dwx_acc[...] = jnp.zeros(dwx_acc.shape, F32)
            a_car[...] = jnp.ones(a_car.shape, F32)
            dh_car[...] = jnp.zeros(dh_car.shape, F32)
            dcq_car[...] = jnp.zeros(dcq_car.shape, F32)
            dxr_car[...] = jnp.zeros(dxr_car.shape, F32)
            wabd[...] = _expand_heads(wa_ref[...])
            wxbd[...] = _expand_heads(wx_ref[...])

        row_c = lax.broadcasted_iota(jnp.int32, (SUB, CONV_WIDTH), 0)
        row_r = lax.broadcasted_iota(jnp.int32, (SUB, LRU_WIDTH), 0)
        cw = cw_ref[...]
        rw = rw_ref[...]
        rb = rb_ref[...]
        g_c = gnc_ref[...]
        g_r = gnr_ref[...]
        sp_c = LRU_C * _softplus_neg(lam_ref[...])

        up = up_ref[...] * has_prev
        cv_before = up[:, OFF_GC:OFF_GC + CONV_WIDTH] * up[:, OFF_V:OFF_V + CONV_WIDTH]
        xin_before = up[:, OFF_XR:OFF_XR + LRU_WIDTH]
        hs_before = hp_ref[...] * has_prev

        dy_s[...] = _dot_nt(dx1_ref[...].astype(BF16), wout_ref[...])

        xrb = xr_ref[...].astype(BF16)

        def recur_bwd(j, carry):
            a_later, dh_later = carry
            i = n_chunks - 1 - j
            r = pl.multiple_of(i * SUB, SUB)
            rp = pl.multiple_of(jnp.maximum(i - 1, 0) * SUB, SUB)
            xr = xr_ref[pl.ds(r, SUB), :]
            hs_c = hs_ref[pl.ds(r, SUB), :]
            hs_prev = jnp.where(i == 0, hs_before, hs_ref[pl.ds(rp, SUB), :])
            h_m1 = _down(hs_c, hs_prev, 1, row_r)
            ra = ra_ref[pl.ds(r, SUB), :]
            ii = ii_ref[pl.ds(r, SUB), :]
            mult = mult_ref[pl.ds(r, SUB), :]
            a = jnp.exp(-ra * sp_c)
            inv_mult = lax.rsqrt(mult * mult)
            ge, dge = _gelu(u_ref[pl.ds(r, SUB), OFF_G:OFF_G + LRU_WIDTH])
            y_r = hs_c * ge
            rr = _rms(y_r)
            yhat = y_r * rr
            dyn = dy_s[pl.ds(r, SUB), CONV_WIDTH:MIX_WIDTH]
            acc[ACC_GNR] += dyn * yhat
            dy_r = _rms_bwd(dyn, yhat, rr, g_r)
            du_s[pl.ds(r, SUB), OFF_G:OFF_G + LRU_WIDTH] = dy_r * hs_c * dge
            a_cum, d_cum = _scan8_rev(_up(a, a_later, 1, row_r), dy_r * ge, row_r)
            dh = a_cum * dh_later + d_cum
            dmult = dh * ii * xr
            dii = dh * mult * xr
            dxr_s[pl.ds(r, SUB), :] = dh * mult * ii
            dla = dh * h_m1 * a - dmult * a * a * inv_mult
            acc[ACC_SP] += -dla * ra
            dpa = -dla * sp_c * ra * (1.0 - ra)
            dpx = dii * ii * (1.0 - ii)
            acc[ACC_BA] += dpa
            acc[ACC_BX] += dpx
            dpa_s[pl.ds(r, SUB), :] = dpa
            dpx_s[pl.ds(r, SUB), :] = dpx
            return a, dh[0:1, :]

        a_first, dh_first = _chunk_loop(n_chunks, recur_bwd, (a_car[...], dh_car[...]), in_flight=8)
        a_car[...] = a_first
        dh_car[...] = dh_first

        dpab = dpa_s[...].astype(BF16)
        dpxb = dpx_s[...].astype(BF16)
        dxr_s[...] += _block_diag_apply_t(dpab, wabd) + _block_diag_apply_t(dpxb, wxbd)
        for g in range(LRU_WIDTH // GROUP):
            cols = slice(g * GROUP, (g + 1) * GROUP)
            dwa_acc[cols, :] += _dot_tn(xrb[:, cols], dpab[:, cols])
            dwx_acc[cols, :] += _dot_tn(xrb[:, cols], dpxb[:, cols])

        def convs_bwd(j, carry):
            dcq_later, dxr_later = carry
            i = n_chunks - 1 - j
            r = pl.multiple_of(i * SUB, SUB)
            rp = pl.multiple_of(jnp.maximum(i - 1, 0) * SUB, SUB)
            cv_prev = jnp.where(i == 0, cv_before,
                                u_ref[pl.ds(rp, SUB), OFF_GC:OFF_GC + CONV_WIDTH]
                                * u_ref[pl.ds(rp, SUB), OFF_V:OFF_V + CONV_WIDTH])
            gb, gc, v, cv, cv_m1, cv_m2, cq = _conv3_chunk(u_ref, r, cv_prev, cw, row_c)
            y_c = gb * cq
            rc = _rms(y_c)
            yhat = y_c * rc
            dyn = dy_s[pl.ds(r, SUB), 0:CONV_WIDTH]
            acc[ACC_GNC, :, 0:CONV_WIDTH] += dyn * yhat
            dy_c = _rms_bwd(dyn, yhat, rc, g_c)
            dcq = dy_c * gb
            dcv = (cw[2:3, :] * dcq + cw[1:2, :] * _up(dcq, dcq_later, 1, row_c)
                   + cw[0:1, :] * _up(dcq, dcq_later, 2, row_c))
            acc[ACC_CW + 2, :, 0:CONV_WIDTH] += dcq * cv
            acc[ACC_CW + 1, :, 0:CONV_WIDTH] += dcq * cv_m1
            acc[ACC_CW + 0, :, 0:CONV_WIDTH] += dcq * cv_m2
            du_s[pl.ds(r, SUB), OFF_GB:OFF_GB + CONV_WIDTH] = dy_c * cq
            du_s[pl.ds(r, SUB), OFF_GC:OFF_GC + CONV_WIDTH] = dcv * v
            du_s[pl.ds(r, SUB), OFF_V:OFF_V + CONV_WIDTH] = dcv * gc

            xin_prev = jnp.where(i == 0, xin_before, u_ref[pl.ds(rp, SUB), OFF_XR:OFF_XR + LRU_WIDTH])
            xin, m1, m2, m3, _ = _conv4_chunk(u_ref, r, xin_prev, rw, rb, row_r)
            dxr = dxr_s[pl.ds(r, SUB), :]
            du_s[pl.ds(r, SUB), OFF_XR:OFF_XR + LRU_WIDTH] = (
                rw[3:4, :] * dxr + rw[2:3, :] * _up(dxr, dxr_later, 1, row_r)
                + rw[1:2, :] * _up(dxr, dxr_later, 2, row_r) + rw[0:1, :] * _up(dxr, dxr_later, 3, row_r))
            acc[ACC_RW + 3] += dxr * xin
            acc[ACC_RW + 2] += dxr * m1
            acc[ACC_RW + 1] += dxr * m2
            acc[ACC_RW + 0] += dxr * m3
            acc[ACC_BR] += dxr
            return dcq, dxr

        dcq_first, dxr_first = _chunk_loop(n_chunks, convs_bwd, (dcq_car[...], dxr_car[...]), in_flight=8)
        dcq_car[...] = dcq_first
        dxr_car[...] = dxr_first

        du_ref[...] = du_s[...].astype(BF16)

        @pl.when(step == n_tiles - 1)
        def _():
            vec_ref[...] = jnp.zeros(vec_ref.shape, F32)
            rows = {ACC_GNC: ROW_GNC, ACC_GNR: ROW_GNR, ACC_BR: ROW_BR, ACC_BA: ROW_BA, ACC_BX: ROW_BX}
            for k in range(3):
                rows[ACC_CW + k] = ROW_CW + k
            for k in range(4):
                rows[ACC_RW + k] = ROW_RW + k
            for slot, out_row in rows.items():
                o = out_row - ROW_GNC
                vec_ref[o:o + 1, :] = jnp.sum(acc[slot], axis=0, keepdims=True)
            lam_v = lam_ref[...]
            dsp = jnp.sum(acc[ACC_SP], axis=0, keepdims=True)
            o = ROW_LAM - ROW_GNC
            vec_ref[o:o + 1, :] = -dsp * LRU_C / (1.0 + jnp.exp(lam_v))
            wab_ref[0:LRU_WIDTH, :] = _fold_heads(dwa_acc[...])
            wab_ref[LRU_WIDTH:2 * LRU_WIDTH, :] = _fold_heads(dwx_acc[...])

    rev = lambda w: pl.BlockSpec((tm, w), lambda s: (n_tiles - 1 - s, 0))
    before = lambda w: pl.BlockSpec((SUB, w), lambda s: (jnp.maximum((n_tiles - 1 - s) * per_tile - 1, 0), 0))
    whole = lambda a: pl.BlockSpec(a.shape, lambda s: (0,) * a.ndim)
    smalls = (conv_w, rnn_conv_w, rnn_conv_b, wa, b_a, wx, b_x, lam, gnc, gnr, w_out)
    full = lambda w: pltpu.VMEM((tm, w), F32)
    return pl.pallas_call(
        body, grid=(n_tiles,),
        in_specs=[rev(IN_COLS), before(IN_COLS), rev(LRU_WIDTH), before(LRU_WIDTH), rev(D_MODEL)]
        + [rev(LRU_WIDTH)] * len(saved) + [whole(a) for a in smalls] + [HBM_SPEC] * (n_sums + 1),
        out_specs=[rev(IN_COLS), pl.BlockSpec((16, D_MODEL), lambda s: (0, 0)),
                   pl.BlockSpec((2 * LRU_WIDTH, HEAD_DIM), lambda s: (0, 0))] + [HBM_SPEC] * (n_sums + 1),
        out_shape=[jax.ShapeDtypeStruct((t_len, IN_COLS), BF16), jax.ShapeDtypeStruct((16, D_MODEL), F32),
                   jax.ShapeDtypeStruct((2 * LRU_WIDTH, HEAD_DIM), F32)]
        + [jax.ShapeDtypeStruct(s.shape, BF16) for s in chip_sums]
        + [jax.ShapeDtypeStruct((4,) + g_wout.shape[1:], BF16)],
        scratch_shapes=[full(IN_COLS), full(MIX_WIDTH), full(LRU_WIDTH), full(LRU_WIDTH), full(LRU_WIDTH),
                        pltpu.VMEM((LRU_WIDTH, GROUP), BF16), pltpu.VMEM((LRU_WIDTH, GROUP), BF16),
                        pltpu.VMEM((N_ACC, SUB, LRU_WIDTH), F32),
                        pltpu.VMEM((LRU_WIDTH, GROUP), F32), pltpu.VMEM((LRU_WIDTH, GROUP), F32),
                        pltpu.VMEM((SUB, LRU_WIDTH), F32), pltpu.VMEM((1, LRU_WIDTH), F32),
                        pltpu.VMEM((SUB, CONV_WIDTH), F32), pltpu.VMEM((SUB, LRU_WIDTH), F32)]
        + _exchange_scratch(n_sums, 3) + _exchange_scratch(1, 4),
        compiler_params=_params(("arbitrary",), 56), name="mixer_bwd",
    )(u, u, hs, hs, dx1, *saved, *smalls, *chip_sums, g_wout)


def _in_proj_bwd(du, dx1, x, g_mix, win_t, tm, chip_sums, g_own):
    t_len = x.shape[0]
    n_steps = t_len // tm

    def body(du_ref, dx1_ref, x_ref, g_ref, w_ref, hs_ref, gown_ref,
             dx_ref, vec_ref, landed_ref, sib_ref, i_send, i_recv, d_send, d_recv):
        step = pl.program_id(0)
        _host_chip_exchange(step, n_steps, [hs_ref], [landed_ref], i_send, i_recv)
        _host_half_exchange(step, n_steps, gown_ref, sib_ref, d_send, d_recv)

        @pl.when(step == 0)
        def _():
            vec_ref[...] = jnp.zeros(vec_ref.shape, F32)

        dh = jnp.dot(du_ref[...], w_ref[...], preferred_element_type=F32)
        xv = x_ref[...]
        r1 = _rms(xv)
        xh = xv * r1
        vec_ref[0:1, :] += jnp.sum(dh * xh, axis=0, keepdims=True)
        dx_ref[...] = dx1_ref[...] + _rms_bwd(dh, xh, r1, g_ref[...])

    row_tile = lambda w: pl.BlockSpec((tm, w), lambda i: (i, 0))
    half_shape = (g_own.shape[0], g_own.shape[1] // 2, g_own.shape[2])
    return pl.pallas_call(
        body, grid=(n_steps,),
        in_specs=[row_tile(IN_COLS), row_tile(D_MODEL), row_tile(D_MODEL), pl.BlockSpec((1, D_MODEL), lambda i: (0, 0)),
                  pl.BlockSpec((IN_COLS, D_MODEL), lambda i: (0, 0))] + [HBM_SPEC] * 2,
        out_specs=[row_tile(D_MODEL), pl.BlockSpec((SUB, D_MODEL), lambda i: (0, 0))] + [HBM_SPEC] * 2,
        out_shape=[jax.ShapeDtypeStruct((t_len, D_MODEL), F32), jax.ShapeDtypeStruct((SUB, D_MODEL), F32),
                   jax.ShapeDtypeStruct(chip_sums.shape, BF16), jax.ShapeDtypeStruct(half_shape, BF16)],
        scratch_shapes=_exchange_scratch(1, 3) + [pltpu.SemaphoreType.DMA((1,)), pltpu.SemaphoreType.DMA((1,))],
        compiler_params=_params(("arbitrary",), 56), name="in_proj_bwd",
    )(du, dx1, x, g_mix, win_t, chip_sums, g_own)


def _tn_weight_grad(a, b, tk, name, pair=(), col_blocks=1):
    t_len, m = a.shape
    n = b.shape[1]
    n_steps = t_len // tk
    sent = tuple(pair)
    n_sent = len(sent)

    def body(a_ref, b_ref, *rest):
        srcs = rest[0:n_sent]
        o_ref = rest[n_sent]
        dsts = rest[n_sent + 1:2 * n_sent + 1]
        acc = rest[2 * n_sent + 1]
        sems = rest[2 * n_sent + 2:]
        j = pl.program_id(0)
        if pair:
            _host_pair_exchange(j, n_steps, srcs, dsts, *sems)

        @pl.when(j == 0)
        def _():
            acc[...] = jnp.zeros(acc.shape, F32)

        acc[...] += _dot_tn(a_ref[...].astype(BF16), b_ref[...].astype(BF16))

        @pl.when(j == n_steps - 1)
        def _():
            if col_blocks == 1:
                o_ref[...] = acc[...].astype(BF16)
            else:
                for k in range(col_blocks):
                    o_ref[k] = acc[:, k * nb:(k + 1) * nb].astype(BF16)

    nb = n // col_blocks
    out_dims = (m, n) if col_blocks == 1 else (col_blocks, m, nb)
    landed = [jax.ShapeDtypeStruct((4,) + g.shape[1:], BF16) for g in pair]
    scratch = [pltpu.VMEM((m, n), F32)]
    if n_sent:
        scratch += _exchange_scratch(n_sent, 4)
    return pl.pallas_call(
        body, grid=(n_steps,),
        in_specs=[pl.BlockSpec((tk, m), lambda j: (j, 0)), pl.BlockSpec((tk, n), lambda j: (j, 0))]
        + [HBM_SPEC] * n_sent,
        out_specs=[pl.BlockSpec(out_dims, lambda j: (0,) * len(out_dims))] + [HBM_SPEC] * n_sent,
        out_shape=[jax.ShapeDtypeStruct(out_dims, BF16)] + landed,
        scratch_shapes=scratch,
        compiler_params=_params(("arbitrary",), 56), name=name,
    )(a, b, *sent)


def _w_in_grad_part(du, h, tk, name, chip_ids, chip=(), halves=None, small=None):
    t_len = du.shape[0]
    n_t = t_len // tk
    n_q = chip_ids.shape[0]
    width = 2 * (IN_COLS // N_DEV)
    n_steps = n_q * n_t
    n_chip = len(chip)
    sent = tuple(chip) + (() if halves is None else (halves,)) + (() if small is None else tuple(small))
    n_sent = len(sent)

    def body(ids_ref, a_ref, b_ref, *rest):
        srcs = rest[0:n_sent]
        o_ref = rest[n_sent]
        dsts = rest[n_sent + 1:2 * n_sent + 1]
        acc = rest[2 * n_sent + 1]
        sems = list(rest[2 * n_sent + 2:])
        j = pl.program_id(1)
        step = pl.program_id(0) * n_t + j
        if chip:
            _host_chip_exchange(step, n_steps, srcs[0:n_chip], dsts[0:n_chip], sems.pop(0), sems.pop(0))
        if halves is not None:
            _host_half_exchange(step, n_steps, srcs[n_chip], dsts[n_chip], sems.pop(0), sems.pop(0))
        if small is not None:
            _host_small_exchange(step, n_steps, *srcs[n_sent - 3:], *dsts[n_sent - 3:], *sems)

        @pl.when(j == 0)
        def _():
            acc[...] = jnp.zeros(acc.shape, F32)

        acc[...] += _dot_tn(a_ref[...], b_ref[...])

        @pl.when(j == n_t - 1)
        def _():
            o_ref[0] = acc[...].astype(BF16)

    landed = [jax.ShapeDtypeStruct(s.shape, BF16) for s in chip]
    scratch = [pltpu.VMEM((width, D_MODEL), F32)]
    if chip:
        scratch += _exchange_scratch(len(chip), 3)
    if halves is not None:
        landed.append(jax.ShapeDtypeStruct((halves.shape[0], halves.shape[1] // 2, halves.shape[2]), BF16))
        scratch += [pltpu.SemaphoreType.DMA((halves.shape[0],)), pltpu.SemaphoreType.DMA((halves.shape[0],))]
    if small is not None:
        vec_m, vec_b, wab = small
        landed += [jax.ShapeDtypeStruct((N_DEV,) + vec_m.shape, F32), jax.ShapeDtypeStruct((N_DEV,) + vec_b.shape, F32),
                   jax.ShapeDtypeStruct((N_DEV, wab.shape[0] // N_DEV, wab.shape[1]), F32)]
        scratch += _exchange_scratch(3, N_DEV) + [pltpu.SemaphoreType.DMA((2,))]
    grid_spec = pltpu.PrefetchScalarGridSpec(
        num_scalar_prefetch=1, grid=(n_q, n_t),
        in_specs=[pl.BlockSpec((tk, width), lambda q, j, ids: (j, ids[q])),
                  pl.BlockSpec((tk, D_MODEL), lambda q, j, ids: (j, 0))] + [HBM_SPEC] * n_sent,
        out_specs=[pl.BlockSpec((1, width, D_MODEL), lambda q, j, ids: (q, 0, 0))] + [HBM_SPEC] * n_sent,
        scratch_shapes=scratch)
    return pl.pallas_call(
        body, grid_spec=grid_spec, out_shape=[jax.ShapeDtypeStruct((n_q, width, D_MODEL), BF16)] + landed,
        compiler_params=_params(("arbitrary", "arbitrary"), 40), name=name,
    )(chip_ids, du, h, *sent)


def _adamw(w, g, m, v):
    m = ADAM_B1 * m + (1.0 - ADAM_B1) * g
    v = ADAM_B2 * v + (1.0 - ADAM_B2) * (g * g)
    delta = -ADAM_LR * ((m / BC1) / (jnp.sqrt(v / BC2) + ADAM_EPS) + ADAM_WD * w)
    return delta, m, v


def _update_sharded(g, landed, w, m, v, rows_blk, name):
    rows, cols = w.shape

    def body(g_ref, l_ref, w_ref, m_ref, v_ref, og, od, om, ov):
        gv = g_ref[...]
        for j in range(3):
            gv = gv + l_ref[j].astype(F32)
        delta, mn, vn = _adamw(w_ref[...], gv, m_ref[...], v_ref[...])
        og[...] = gv
        od[...] = delta
        om[...] = mn
        ov[...] = vn

    blk = pl.BlockSpec((rows_blk, cols), lambda i: (i, 0))
    shape = pltpu.HBM((rows, cols), F32)
    return pl.pallas_call(
        body, grid=(rows // rows_blk,),
        in_specs=[blk, pl.BlockSpec((3, rows_blk, cols), lambda i: (0, i, 0)), blk, blk, blk],
        out_specs=[blk] * 4, out_shape=[shape] * 4,
        compiler_params=_params(("arbitrary",), 32), name=name,
    )(*_in_hbm(g, landed, w, m, v))


def _update_w_in(g_own, sib_own, landed, w_t, m_t, v_t, core, cols_blk):
    rows, cols = w_t.shape

    def body(core_ref, g_ref, s_ref, l_ref, w_ref, m_ref, v_ref, og, od, om, ov):
        gv = g_ref[0, 0].astype(F32) + s_ref[0].astype(F32)
        for j in range(3):
            gv = gv + l_ref[j].astype(F32)
        delta, mn, vn = _adamw(w_ref[...], gv, m_ref[...], v_ref[...])
        og[...] = gv
        od[...] = delta
        om[...] = mn
        ov[...] = vn

    blk = pl.BlockSpec((rows, cols_blk), lambda i, cr: (0, i))
    grid_spec = pltpu.PrefetchScalarGridSpec(
        num_scalar_prefetch=1, grid=(cols // cols_blk,),
        in_specs=[pl.BlockSpec((1, 1, rows, cols_blk), lambda i, cr: (0, cr[0], 0, i)),
                  pl.BlockSpec((1, rows, cols_blk), lambda i, cr: (0, 0, i)),
                  pl.BlockSpec((3, rows, cols_blk), lambda i, cr: (0, 0, i)), blk, blk, blk],
        out_specs=[blk] * 4)
    return pl.pallas_call(
        body, grid_spec=grid_spec, out_shape=[pltpu.HBM((rows, cols), F32)] * 4,
        compiler_params=_params(("arbitrary",), 32), name="update_w_in",
    )(core, *_in_hbm(g_own.reshape(1, 2, rows, cols), sib_own, landed, w_t, m_t, v_t))


def _update_small(vsum, wsum, g_cw, g_rw, weights, moments_m, moments_v):
    n = len(weights)

    def body(*refs):
        vs, ws, gcw, grw = refs[0:4]
        w_refs = refs[4:4 + n]
        m_refs = refs[4 + n:4 + 2 * n]
        v_refs = refs[4 + 2 * n:4 + 3 * n]
        outs = refs[4 + 3 * n:]
        loss_ref = outs[0]
        loss_ref[...] = jnp.sum(vs[ROW_LOSS:ROW_LOSS + 1, :], axis=1, keepdims=True)
        grads = [
            vs[ROW_GMIX:ROW_GMIX + 1, :], gcw[...], grw[...], vs[ROW_BR:ROW_BR + 1, :],
            ws[0:LRU_WIDTH, :], vs[ROW_BA:ROW_BA + 1, :], ws[LRU_WIDTH:2 * LRU_WIDTH, :], vs[ROW_BX:ROW_BX + 1, :],
            vs[ROW_LAM:ROW_LAM + 1, :], vs[ROW_GNC:ROW_GNC + 1, 0:CONV_WIDTH], vs[ROW_GNR:ROW_GNR + 1, :],
            vs[ROW_GMLP:ROW_GMLP + 1, :], vs[ROW_GF:ROW_GF + 1, :],
        ]
        for k in range(n):
            gk = grads[k]
            delta, mn, vn = _adamw(w_refs[k][...], gk, m_refs[k][...], v_refs[k][...])
            outs[1 + 4 * k][...] = gk
            outs[2 + 4 * k][...] = delta
            outs[3 + 4 * k][...] = mn
            outs[4 + 4 * k][...] = vn

    whole = lambda a: pl.BlockSpec(a.shape, lambda i: (0,) * len(a.shape))
    out_shape = [jax.ShapeDtypeStruct((1, 1), F32)]
    for w in weights:
        out_shape += [jax.ShapeDtypeStruct(w.shape, F32)] * 4
    args = (vsum, wsum, g_cw, g_rw, *weights, *moments_m, *moments_v)
    return pl.pallas_call(
        body, grid=(1,), out_shape=out_shape, in_specs=[whole(a) for a in args], out_specs=[whole(s) for s in out_shape],
        compiler_params=_params(("arbitrary",), 32), name="update_small",
    )(*args)


def kernel(x, norm_mix_g, w_in, conv_w, rnn_conv_w, rnn_conv_b, w_a, b_a, w_x, b_x, lru_lambda, g_norm_conv, g_norm_rnn, w_out, norm_mlp_g, w_mlp_in, w_mlp_out, final_norm_g, loss_target, m_norm_mix_g, m_w_in, m_conv_w, m_rnn_conv_w, m_rnn_conv_b, m_w_a, m_b_a, m_w_x, m_b_x, m_lru_lambda, m_g_norm_conv, m_g_norm_rnn, m_w_out, m_norm_mlp_g, m_w_mlp_in, m_w_mlp_out, m_final_norm_g, v_norm_mix_g, v_w_in, v_conv_w, v_rnn_conv_w, v_rnn_conv_b, v_w_a, v_b_a, v_w_x, v_b_x, v_lru_lambda, v_g_norm_conv, v_g_norm_rnn, v_w_out, v_norm_mlp_g, v_w_mlp_in, v_w_mlp_out, v_final_norm_g):
    t_len = x.shape[1]
    my_id = 4 * lax.axis_index("x") + 2 * lax.axis_index("y") + lax.axis_index("c")
    tm = min(256, t_len)
    tb = min(512, t_len)
    tk = min(512, t_len)

    xs = x.reshape(t_len, D_MODEL)
    tgt = loss_target.reshape(t_len, D_MODEL)
    flat = lambda a: a.reshape(a.shape[-2:]) if a.ndim == 3 else a.reshape(1, -1)
    heads = lambda a: a.reshape(LRU_WIDTH, HEAD_DIM)

    turned = lambda a: jnp.transpose(flat(a))
    win_shard, wout_shard, w1_shard, w2_shard, cp_shard = _prep_shards(
        turned(w_in), flat(w_out), flat(w_mlp_in), flat(w_mlp_out), flat(conv_w), flat(rnn_conv_w))

    u, h, win_t, cp_full = _in_proj(xs, flat(norm_mix_g), (win_shard, cp_shard), tb)
    cpack = cp_full.reshape(N_DEV, 8, 128)
    conv_full = jnp.transpose(cpack[:, 0:3, 0:64], (1, 0, 2)).reshape(3, CONV_WIDTH)
    rnn_full = jnp.transpose(cpack[:, 3:7, :], (1, 0, 2)).reshape(4, LRU_WIDTH)
    mixer_small = (conv_full, rnn_full, flat(rnn_conv_b), heads(w_a), flat(b_a), heads(w_x), flat(b_x),
                   flat(lru_lambda), flat(g_norm_conv), flat(g_norm_rnn))
    hs, y, xr, gate_r, gate_i, mult, w1_blk, wout_blk = _mixer_fwd(u, *mixer_small, (w1_shard, wout_shard), tm)
    wout_f = wout_blk.reshape(MIX_WIDTH, D_MODEL)
    x1, h2, z, w2_blk = _mlp_up(xs, y, flat(norm_mlp_g), wout_f, w1_blk, w2_shard, tb)
    dx1, dx2, vec_m, dpre = _mlp_down_bwd(x1, z, tgt, flat(norm_mlp_g), flat(final_norm_g), w1_blk,
                                          w2_blk.reshape(D_FF, D_MODEL), tb)
    (g_w1,) = _tn_weight_grad(h2, dpre, tk, "w_mlp_in_grad", col_blocks=N_DEV)
    (g_w2,) = _tn_weight_grad(z, dx2, tk, "w_mlp_out_grad")
    g_w2 = g_w2.reshape(N_DEV, D_FF // N_DEV, D_MODEL)
    g_wout, sib_w1, sib_w2 = _tn_weight_grad(y, dx1, tk, "w_out_grad", pair=(g_w1, g_w2))
    g_wout = g_wout.reshape(N_DEV, MIX_WIDTH // N_DEV, D_MODEL)
    hsend_w1, own_w1 = _pair_sum(g_w1, sib_w1, "pair_sum_w_mlp_in")
    hsend_w2, own_w2 = _pair_sum(g_w2, sib_w2, "pair_sum_w_mlp_out")
    du, vec_b, wab, landed_w1, landed_w2, sib_wout = _mixer_bwd(
        u, hs, dx1, (xr, gate_r, gate_i, mult), *mixer_small, wout_f, (hsend_w1, hsend_w2), g_wout, tm)
    hsend_wout, own_wout = _pair_sum(g_wout, sib_wout, "pair_sum_w_out")
    ax, ay, ac = lax.axis_index("x"), lax.axis_index("y"), lax.axis_index("c")
    chip_ids = jnp.stack([2 * cx + cy for cx, cy in [(ax, ay)] + _other_chips(ax, ay)]).astype(jnp.int32)
    core = jnp.reshape(ac, (1,)).astype(jnp.int32)
    tw = min(1024, t_len)
    g_others, landed_wout, vrecv_m, vrecv_b, wrecv = _w_in_grad_part(
        du, h, tw, "w_in_grad_others", chip_ids[1:4], chip=(hsend_wout,), small=(vec_m, vec_b, wab))
    g_own, sib_others = _w_in_grad_part(du, h, tw, "w_in_grad_own", chip_ids[0:1], halves=g_others)
    hsend_win = _pair_sum_parts(g_others, sib_others, core)
    grad_x, vec_x, landed_win, sib_own = _in_proj_bwd(du, dx1, xs, flat(norm_mix_g), win_t, tm, hsend_win, g_own)

    vsum, wsum = _final_small(vrecv_m, vrecv_b, wab, wrecv, vec_x)

    up_win = _update_w_in(g_own, sib_own, landed_win, turned(w_in), turned(m_w_in), turned(v_w_in), core, 256)
    up_win = [jnp.transpose(a) for a in up_win]
    up_wout = _update_sharded(own_wout, landed_wout, flat(w_out), flat(m_w_out), flat(v_w_out), 96, "update_w_out")
    up_w1 = _update_sharded(own_w1, landed_w1, flat(w_mlp_in), flat(m_w_mlp_in), flat(v_w_mlp_in), 256,
                            "update_w_mlp_in")
    up_w2 = _update_sharded(own_w2, landed_w2, flat(w_mlp_out), flat(m_w_mlp_out), flat(v_w_mlp_out), 256,
                            "update_w_mlp_out")

    g_cw = lax.dynamic_slice(vsum, (ROW_CW, 64 * my_id), (3, 64))
    g_rw = lax.dynamic_slice(vsum, (ROW_RW, 128 * my_id), (4, 128))
    small_w = (norm_mix_g, conv_w, rnn_conv_w, rnn_conv_b, w_a, b_a, w_x, b_x, lru_lambda, g_norm_conv, g_norm_rnn,
               norm_mlp_g, final_norm_g)
    small_m = (m_norm_mix_g, m_conv_w, m_rnn_conv_w, m_rnn_conv_b, m_w_a, m_b_a, m_w_x, m_b_x, m_lru_lambda,
               m_g_norm_conv, m_g_norm_rnn, m_norm_mlp_g, m_final_norm_g)
    small_v = (v_norm_mix_g, v_conv_w, v_rnn_conv_w, v_rnn_conv_b, v_w_a, v_b_a, v_w_x, v_b_x, v_lru_lambda,
               v_g_norm_conv, v_g_norm_rnn, v_norm_mlp_g, v_final_norm_g)
    is_heads = (False, False, False, False, True, False, True, False, False, False, False, False, False)
    as2d = lambda arrs: [heads(a) if hd else flat(a) for a, hd in zip(arrs, is_heads)]
    small_out = _update_small(vsum, wsum, g_cw, g_rw, as2d(small_w), as2d(small_m), as2d(small_v))
    loss = small_out[0].reshape(())

    names = ["norm_mix_g", "w_in", "conv_w", "rnn_conv_w", "rnn_conv_b", "w_a", "b_a", "w_x", "b_x", "lru_lambda",
             "g_norm_conv", "g_norm_rnn", "w_out", "norm_mlp_g", "w_mlp_in", "w_mlp_out", "final_norm_g"]
    originals = dict(zip(names, (norm_mix_g, w_in, conv_w, rnn_conv_w, rnn_conv_b, w_a, b_a, w_x, b_x, lru_lambda,
                                 g_norm_conv, g_norm_rnn, w_out, norm_mlp_g, w_mlp_in, w_mlp_out, final_norm_g)))
    results = {"w_in": up_win, "w_out": up_wout, "w_mlp_in": up_w1, "w_mlp_out": up_w2}
    small_names = ["norm_mix_g", "conv_w", "rnn_conv_w", "rnn_conv_b", "w_a", "b_a", "w_x", "b_x", "lru_lambda",
                   "g_norm_conv", "g_norm_rnn", "norm_mlp_g", "final_norm_g"]
    for k, nm in enumerate(small_names):
        results[nm] = small_out[1 + 4 * k:5 + 4 * k]
    out = [loss, grad_x.reshape(x.shape)]
    for kind in range(4):
        out += [results[nm][kind].reshape(originals[nm].shape) for nm in names]
    return tuple(out)
```

```python
import functools

import jax
import jax.numpy as jnp
from jax import lax
from jax.experimental import pallas as pl
from jax.experimental.pallas import tpu as pltpu

F32 = jnp.float32
BF16 = jnp.bfloat16

D_MODEL = 1024
HEAD_DIM = 64
CONV_WIDTH = 512
LRU_WIDTH = 1024
MIX_WIDTH = CONV_WIDTH + LRU_WIDTH
IN_COLS = 3 * CONV_WIDTH + 2 * LRU_WIDTH
D_FF = 4 * D_MODEL
GROUP = 256
EPS = 1e-6
LRU_C = 8.0
N_DEV = 8
SUB = 8

OFF_GB, OFF_GC, OFF_V, OFF_XR, OFF_G = 0, 512, 1024, 1536, 2560

ADAM_LR, ADAM_B1, ADAM_B2, ADAM_EPS, ADAM_WD, ADAM_STEP = 0.001, 0.9, 0.999, 1e-08, 0.01, 10
BC1 = 1.0 - ADAM_B1 ** ADAM_STEP
BC2 = 1.0 - ADAM_B2 ** ADAM_STEP

MIB = 1024 * 1024
MESH = pl.DeviceIdType.MESH

VEC_ROWS = 32
ROW_GF, ROW_GMLP, ROW_LOSS = 0, 1, 2
ROW_GNC, ROW_GNR, ROW_BR, ROW_BA, ROW_BX, ROW_LAM, ROW_CW, ROW_RW = 8, 9, 10, 11, 12, 13, 14, 17
ROW_GMIX = 24
ACC_GNC, ACC_GNR, ACC_BR, ACC_BA, ACC_BX, ACC_SP, ACC_CW, ACC_RW, N_ACC = 0, 1, 2, 3, 4, 5, 6, 9, 13


def _params(semantics=None, vmem_mib=48):
    return pltpu.CompilerParams(dimension_semantics=semantics, vmem_limit_bytes=vmem_mib * MIB)


def _rms(x):
    return lax.rsqrt(jnp.mean(x * x, axis=-1, keepdims=True) + EPS)


def _rms_bwd(dy, xhat, r, g):
    dyh = dy * g
    return r * (dyh - xhat * jnp.mean(dyh * xhat, axis=-1, keepdims=True))


def _sigmoid(x):
    return 0.5 + 0.5 * jnp.tanh(0.5 * x)


def _gelu(x):
    c0, c1 = 0.7978845608028654, 0.044715
    x2 = x * x
    t = jnp.tanh(x * (c0 + (c0 * c1) * x2))
    half = 0.5 + 0.5 * t
    ge = x * half
    dge = half + (0.5 * x) * (1.0 - t * t) * (c0 + (3.0 * c0 * c1) * x2)
    return ge, dge


def _softplus_neg(lam):
    z = -lam
    e = jnp.exp(-jnp.abs(z))
    return jnp.maximum(z, 0.0) + jnp.where(e < 1e-4, e * (1.0 - 0.5 * e), jnp.log(1.0 + e))


def _lru_gates(pa, px, sp_c):
    ra = _sigmoid(pa)
    ii = _sigmoid(px)
    la = -ra * sp_c
    a = jnp.exp(la)
    x2 = 2.0 * la
    series = -x2 * (1.0 + x2 * (0.5 + x2 * (1.0 / 6.0 + x2 * (1.0 / 24.0))))
    m2 = jnp.where(x2 > -0.01, series, 1.0 - a * a)
    inv_mult = lax.rsqrt(m2)
    mult = jnp.where(m2 > 0.0, m2 * inv_mult, 0.0)
    return ra, ii, a, mult, inv_mult


def _down(cur, prev, s, row):
    return jnp.where(row >= s, pltpu.roll(cur, s, 0), pltpu.roll(prev, s, 0))


def _up(cur, nxt, s, row):
    return jnp.where(row < SUB - s, pltpu.roll(cur, SUB - s, 0), pltpu.roll(nxt, SUB - s, 0))


def _scan8_fwd(a, b, row):
    for s in (1, 2, 4):
        m = row >= s
        a_sh = pltpu.roll(a, s, 0)
        b_sh = pltpu.roll(b, s, 0)
        b = jnp.where(m, a * b_sh + b, b)
        a = jnp.where(m, a * a_sh, a)
    return a, b


def _scan8_rev(a, b, row):
    for s in (1, 2, 4):
        m = row < SUB - s
        a_sh = pltpu.roll(a, SUB - s, 0)
        b_sh = pltpu.roll(b, SUB - s, 0)
        b = jnp.where(m, a * b_sh + b, b)
        a = jnp.where(m, a * a_sh, a)
    return a, b


def _group_mask(shape):
    r = lax.broadcasted_iota(jnp.int32, shape, 0)
    c = lax.broadcasted_iota(jnp.int32, shape, 1)
    return ((r % GROUP) // HEAD_DIM) == (c // HEAD_DIM)


def _expand_heads(w):
    j = lax.broadcasted_iota(jnp.int32, (HEAD_DIM, GROUP), 0)
    c = lax.broadcasted_iota(jnp.int32, (HEAD_DIM, GROUP), 1)
    spread = (c % HEAD_DIM == j).astype(BF16)
    e = jnp.dot(w.astype(BF16), spread, preferred_element_type=F32)
    return jnp.where(_group_mask(e.shape), e, 0.0).astype(BF16)


def _fold_heads(p):
    p = jnp.where(_group_mask(p.shape), p, 0.0)
    c = lax.broadcasted_iota(jnp.int32, (GROUP, HEAD_DIM), 0)
    j = lax.broadcasted_iota(jnp.int32, (GROUP, HEAD_DIM), 1)
    fold = (c % HEAD_DIM == j).astype(BF16)
    hi = p.astype(BF16)
    rest = p - hi.astype(F32)
    mid = rest.astype(BF16)
    lo = (rest - mid.astype(F32)).astype(BF16)
    dot = functools.partial(jnp.dot, preferred_element_type=F32)
    return dot(hi, fold) + dot(mid, fold) + dot(lo, fold)


def _block_diag_apply(xb, wbd_ref):
    parts = [jnp.dot(xb[:, g * GROUP:(g + 1) * GROUP], wbd_ref[g * GROUP:(g + 1) * GROUP, :],
                     preferred_element_type=F32) for g in range(LRU_WIDTH // GROUP)]
    return jnp.concatenate(parts, axis=1)


def _block_diag_apply_t(db, wbd_ref):
    parts = [lax.dot_general(db[:, g * GROUP:(g + 1) * GROUP], wbd_ref[g * GROUP:(g + 1) * GROUP, :],
                             (((1,), (1,)), ((), ())), preferred_element_type=F32)
             for g in range(LRU_WIDTH // GROUP)]
    return jnp.concatenate(parts, axis=1)


def _dot_nt(a, b):
    return lax.dot_general(a, b, (((1,), (1,)), ((), ())), preferred_element_type=F32)


def _dot_tn(a, b):
    return lax.dot_general(a, b, (((0,), (0,)), ((), ())), preferred_element_type=F32)


def _chunk_loop(n_chunks, chunk, init, in_flight=4):
    def body(k, carry):
        for j in range(in_flight):
            carry = chunk(k * in_flight + j, carry)
        return carry

    return lax.fori_loop(0, n_chunks // in_flight, body, init)


def _place():
    x, y, c = lax.axis_index("x"), lax.axis_index("y"), lax.axis_index("c")
    return x, y, c


def _block_id(chip, core):
    return 4 * chip[0] + 2 * chip[1] + core


def _other_chips(x, y):
    return [(1 - x, y), (x, 1 - y), (1 - x, 1 - y)]


def _remote_copy(src, dst, send_sem, recv_sem, to):
    return pltpu.make_async_remote_copy(src_ref=src, dst_ref=dst, send_sem=send_sem, recv_sem=recv_sem,
                                        device_id=to, device_id_type=MESH)


HBM_SPEC = pl.BlockSpec(memory_space=pl.ANY)


def _in_hbm(*arrays):
    return [pltpu.with_memory_space_constraint(a, pltpu.HBM) for a in arrays]


def _prep_shards(w_in_t, w_out, w_mlp_in, w_mlp_out, conv_w, rnn_conv_w):
    def body(win_ref, wout_ref, w1_ref, w2_ref, cw_ref, rw_ref, o_win, o_wout, o_w1, o_w2, o_cp):
        o_win[...] = win_ref[...].astype(BF16)
        o_wout[...] = wout_ref[...].astype(BF16)
        o_w1[...] = w1_ref[...].astype(BF16)
        o_w2[...] = w2_ref[...].astype(BF16)
        o_cp[...] = jnp.zeros(o_cp.shape, F32)
        o_cp[0:3, 0:64] = cw_ref[...]
        o_cp[3:7, :] = rw_ref[...]

    whole = lambda shape: pl.BlockSpec(shape, lambda i: (0,) * len(shape))
    args = (w_in_t, w_out, w_mlp_in, w_mlp_out, conv_w, rnn_conv_w)
    shapes = [(w_in_t.shape, BF16), (w_out.shape, BF16), (w_mlp_in.shape, BF16), (w_mlp_out.shape, BF16),
              ((8, 128), F32)]
    return pl.pallas_call(
        body, grid=(1,), out_shape=[jax.ShapeDtypeStruct(s, d) for s, d in shapes],
        in_specs=[whole(a.shape) for a in args], out_specs=[whole(s) for s, _ in shapes],
        compiler_params=_params(("arbitrary",), 40), name="prep_shards",
    )(*args)


def _host_all_gather(step, n_steps, shards, fulls, send_sems, recv_sems, local_sems):
    x, y, c = _place()
    me = (x, y, c)
    my_id = _block_id((x, y), c)
    sibling = (x, y, 1 - c)
    chips = _other_chips(x, y)
    n_arr = len(shards)

    def copy(arr, k, block, to, src=None):
        dst = fulls[arr].at[block]
        return _remote_copy(dst if src is None else src, dst, send_sems.at[arr, k], recv_sems.at[arr, k], to)

    def local(arr):
        return pltpu.make_async_copy(shards[arr], fulls[arr].at[my_id], local_sems.at[arr])

    @pl.when(step == 0)
    def _():
        for arr in range(n_arr):
            local(arr).start()
            copy(arr, 0, my_id, sibling, shards[arr]).start()
            for j, chip in enumerate(chips):
                copy(arr, 1 + j, my_id, (*chip, c), shards[arr]).start()

    @pl.when(step == max(n_steps - 2, 0))
    def _():
        for j, chip in enumerate(chips):
            for arr in range(n_arr):
                copy(arr, 1 + j, _block_id(chip, c), me).wait_recv()
                copy(arr, 4 + j, _block_id(chip, c), sibling).start()

    @pl.when(step == n_steps - 1)
    def _():
        for arr in range(n_arr):
            copy(arr, 0, _block_id((x, y), 1 - c), me).wait_recv()
            for j, chip in enumerate(chips):
                copy(arr, 4 + j, _block_id(chip, 1 - c), me).wait_recv()
            for k in range(4):
                copy(arr, k, my_id, me, shards[arr]).wait_send()
            for j, chip in enumerate(chips):
                copy(arr, 4 + j, _block_id(chip, c), me).wait_send()
            local(arr).wait()


def _host_pair_exchange(step, n_steps, gs, sibs, send_sems, recv_sems):
    x, y, c = _place()
    sibling = (x, y, 1 - c)
    chips = [(x, y)] + _other_chips(x, y)

    def d2d(arr, q):
        return _remote_copy(gs[arr].at[_block_id(chips[q], 1 - c)], sibs[arr].at[q],
                            send_sems.at[arr, q], recv_sems.at[arr, q], sibling)

    @pl.when(step == 0)
    def _():
        for arr in range(len(gs)):
            for q in (1, 2, 3, 0):
                d2d(arr, q).start()

    @pl.when(step == n_steps - 1)
    def _():
        for arr in range(len(gs)):
            for q in range(4):
                d2d(arr, q).wait()


def _host_chip_exchange(step, n_steps, hsends, hrecvs, send_sems, recv_sems):
    x, y, c = _place()
    chips = _other_chips(x, y)

    def ici(arr, j):
        return _remote_copy(hsends[arr].at[j], hrecvs[arr].at[j], send_sems.at[arr, j], recv_sems.at[arr, j],
                            (*chips[j], c))

    @pl.when(step == 0)
    def _():
        for arr in range(len(hsends)):
            for j in range(3):
                ici(arr, j).start()

    @pl.when(step == n_steps - 1)
    def _():
        for arr in range(len(hsends)):
            for j in range(3):
                ici(arr, j).wait()


def _host_half_exchange(step, n_steps, parts, sibs, send_sems, recv_sems):
    x, y, c = _place()
    n_q, rows2, _ = parts.shape
    half = rows2 // 2

    def d2d(q):
        src = parts.at[q, pl.ds(pl.multiple_of((1 - c) * half, 16), half), :]
        return _remote_copy(src, sibs.at[q], send_sems.at[q], recv_sems.at[q], (x, y, 1 - c))

    @pl.when(step == 0)
    def _():
        for q in range(n_q):
            d2d(q).start()

    @pl.when(step == n_steps - 1)
    def _():
        for q in range(n_q):
            d2d(q).wait()


def _peer(x, y, c, k):
    return (x ^ ((k >> 2) & 1), y ^ ((k >> 1) & 1), c ^ (k & 1))


def _host_small_exchange(step, n_steps, vec_m, vec_b, wab, vrecv_m, vrecv_b, wrecv, send_sems, recv_sems, local_sems):
    x, y, c = _place()
    my_id = _block_id((x, y), c)
    wrows = wab.shape[0] // N_DEV

    def copies(k):
        to = _peer(x, y, c, k)
        block = wab.at[pl.ds(pl.multiple_of(_block_id(to[0:2], to[2]) * wrows, SUB), wrows), :]
        return [_remote_copy(vec_m, vrecv_m.at[my_id], send_sems.at[0, k], recv_sems.at[0, k], to),
                _remote_copy(vec_b, vrecv_b.at[my_id], send_sems.at[1, k], recv_sems.at[1, k], to),
                _remote_copy(block, wrecv.at[k], send_sems.at[2, k], recv_sems.at[2, k], to)]

    mine = [pltpu.make_async_copy(vec_m, vrecv_m.at[my_id], local_sems.at[0]),
            pltpu.make_async_copy(vec_b, vrecv_b.at[my_id], local_sems.at[1])]

    @pl.when(step == 0)
    def _():
        for cp in mine:
            cp.start()
        for k in range(1, N_DEV):
            for cp in copies(k):
                cp.start()

    @pl.when(step == n_steps - 1)
    def _():
        for k in range(1, N_DEV):
            for cp in copies(k):
                cp.wait()
        for cp in mine:
            cp.wait()


def _pair_sum_parts(parts, sibs, core):
    n_q, rows2, cols = parts.shape
    half = rows2 // 2

    def body(core_ref, g_ref, s_ref, o_ref):
        o_ref[0] = (g_ref[0, 0].astype(F32) + s_ref[0].astype(F32)).astype(BF16)

    block = (1, half, cols)
    grid_spec = pltpu.PrefetchScalarGridSpec(
        num_scalar_prefetch=1, grid=(n_q,),
        in_specs=[pl.BlockSpec((1, 1, half, cols), lambda q, cr: (q, cr[0], 0, 0)),
                  pl.BlockSpec(block, lambda q, cr: (q, 0, 0))],
        out_specs=pl.BlockSpec(block, lambda q, cr: (q, 0, 0)))
    return pl.pallas_call(
        body, grid_spec=grid_spec, out_shape=pltpu.HBM((n_q, half, cols), BF16),
        compiler_params=_params(("arbitrary",), 32), name="pair_sum_w_in",
    )(core, *_in_hbm(parts.reshape(n_q, 2, half, cols), sibs))


def _pair_sum(g, sib, name):
    _, rows, cols = g.shape
    x, y, c = _place()
    slots = jnp.stack([_block_id(chip, c) for chip in [(x, y)] + _other_chips(x, y)]).astype(jnp.int32)

    def body(slots_ref, g_ref, sib_ref, hs_ref, own_ref):
        q = pl.program_id(0)
        both = g_ref[0].astype(F32) + sib_ref[0].astype(F32)

        @pl.when(q == 0)
        def _():
            own_ref[...] = both

        @pl.when(q > 0)
        def _():
            hs_ref[0] = both.astype(BF16)

    block = (1, rows, cols)
    grid_spec = pltpu.PrefetchScalarGridSpec(
        num_scalar_prefetch=1, grid=(4,),
        in_specs=[pl.BlockSpec(block, lambda q, s: (s[q], 0, 0)), pl.BlockSpec(block, lambda q, s: (q, 0, 0))],
        out_specs=[pl.BlockSpec(block, lambda q, s: (jnp.maximum(q - 1, 0), 0, 0)),
                   pl.BlockSpec((rows, cols), lambda q, s: (0, 0))])
    return pl.pallas_call(
        body, grid_spec=grid_spec,
        out_shape=(pltpu.HBM((3, rows, cols), BF16), pltpu.HBM((rows, cols), F32)),
        compiler_params=_params(("arbitrary",), 32), name=name,
    )(slots, *_in_hbm(g, sib))


def _exchange_scratch(n_arr, n_copies):
    return [pltpu.SemaphoreType.DMA((n_arr, n_copies)), pltpu.SemaphoreType.DMA((n_arr, n_copies))]


def _final_small(vrecv_m, vrecv_b, wab, wrecv, vec_x):
    wrows = wab.shape[0] // N_DEV

    def body(vm_ref, vb_ref, w_ref, wr_ref, vx_ref, o_vec, o_w, xrecv, wred, x_send, x_recv, b_send, b_recv):
        x, y, c = _place()
        my_id = _block_id((x, y), c)
        my_rows = pl.ds(pl.multiple_of(my_id * wrows, SUB), wrows)

        def xcopy(k):
            return _remote_copy(vx_ref, xrecv.at[my_id], x_send.at[k], x_recv.at[k], _peer(x, y, c, k))

        def bcopy(k):
            return _remote_copy(wred, o_w.at[my_rows, :], b_send.at[k], b_recv.at[k], _peer(x, y, c, k))

        xrecv[my_id] = vx_ref[...]
        for k in range(1, N_DEV):
            xcopy(k).start()
        red = w_ref[my_rows, :]
        for k in range(1, N_DEV):
            red = red + wr_ref[k]
        wred[...] = red
        o_w[my_rows, :] = red
        for k in range(1, N_DEV):
            bcopy(k).start()
        for k in range(1, N_DEV):
            xcopy(k).wait_recv()
        for rows, ref in ((slice(0, 8), vm_ref), (slice(8, 24), vb_ref), (slice(24, 32), xrecv)):
            tot = ref[0]
            for s in range(1, N_DEV):
                tot = tot + ref[s]
            o_vec[rows, :] = tot
        for k in range(1, N_DEV):
            bcopy(k).wait_recv()
        for k in range(1, N_DEV):
            xcopy(k).wait_send()
            bcopy(k).wait_send()

    vm = pl.BlockSpec(memory_space=pltpu.VMEM)
    dma8 = pltpu.SemaphoreType.DMA((N_DEV,))
    return pl.pallas_call(
        body, out_shape=(jax.ShapeDtypeStruct((VEC_ROWS, D_MODEL), F32), jax.ShapeDtypeStruct(wab.shape, F32)),
        in_specs=[vm] * 5, out_specs=[vm] * 2,
        scratch_shapes=[pltpu.VMEM((N_DEV, SUB, D_MODEL), F32), pltpu.VMEM((wrows, HEAD_DIM), F32),
                        dma8, dma8, dma8, dma8],
        compiler_params=_params(vmem_mib=32), name="final_small",
    )(vrecv_m, vrecv_b, wab, wrecv, vec_x)


def _in_proj(x, g_mix, shards, tm):
    t_len = x.shape[0]
    n_t = t_len // tm
    n_arr = len(shards)
    rows = [s.shape[0] for s in shards]
    width = 2 * rows[0]
    ax, ay = lax.axis_index("x"), lax.axis_index("y")
    order = jnp.stack([2 * cx + cy for cx, cy in [(ax, ay)] + _other_chips(ax, ay)]).astype(jnp.int32)

    def body(order_ref, x_ref, g_ref, *rest):
        shard_refs = rest[0:n_arr]
        u_ref, h_ref = rest[n_arr:n_arr + 2]
        fulls = rest[n_arr + 2:2 * n_arr + 2]
        h_s, wbuf, send_sems, recv_sems, local_sems, load_sem = rest[2 * n_arr + 2:]
        p = pl.program_id(0)
        i = pl.program_id(1)
        x_, y_, c = _place()
        me = (x_, y_, c)
        my_id = _block_id((x_, y_), c)
        sibling = (x_, y_, 1 - c)
        chips = _other_chips(x_, y_)

        def block(arr, blk):
            return fulls[arr].at[pl.ds(pl.multiple_of(blk * rows[arr], rows[arr]), rows[arr]), :]

        def copy(arr, k, blk, to, src=None):
            dst = block(arr, blk)
            return _remote_copy(dst if src is None else src, dst, send_sems.at[arr, k], recv_sems.at[arr, k], to)

        def local(arr):
            return pltpu.make_async_copy(shard_refs[arr], block(arr, my_id), local_sems.at[arr])

        def load_chip(chip):
            start = pl.multiple_of((2 * chip[0] + chip[1]) * width, width)
            cp = pltpu.make_async_copy(fulls[0].at[pl.ds(start, width), :], wbuf, load_sem.at[0])
            cp.start()
            cp.wait()

        @pl.when((p == 0) & (i == 0))
        def _():
            for arr in range(n_arr):
                local(arr).start()
                copy(arr, 0, my_id, sibling, shard_refs[arr]).start()
                for j in (0, 1):
                    copy(arr, 1 + j, my_id, (*chips[j], c), shard_refs[arr]).start()
            for arr in range(n_arr):
                local(arr).wait()
                copy(arr, 0, _block_id((x_, y_), 1 - c), me).wait_recv()
            load_chip((x_, y_))

        for j, chip in enumerate(chips):
            @pl.when((p == j + 1) & (i == 0))
            def _(j=j, chip=chip):
                for arr in range(n_arr):
                    copy(arr, 1 + j, _block_id(chip, c), me).wait_recv()
                    copy(arr, 4 + j, _block_id(chip, c), sibling).start()
                    if j == 0:
                        copy(arr, 3, my_id, (*chips[2], c), shard_refs[arr]).start()
                for arr in range(n_arr):
                    copy(arr, 4 + j, _block_id(chip, 1 - c), me).wait_recv()
                load_chip(chip)

        @pl.when((p == 3) & (i == n_t - 1))
        def _():
            for arr in range(n_arr):
                for k in range(4):
                    copy(arr, k, my_id, me, shard_refs[arr]).wait_send()
                for j, chip in enumerate(chips):
                    copy(arr, 4 + j, _block_id(chip, c), me).wait_send()

        tile = pl.ds(pl.multiple_of(i * tm, tm), tm)

        @pl.when(p == 0)
        def _():
            xv = x_ref[...]
            h = (xv * _rms(xv) * g_ref[...]).astype(BF16)
            h_ref[...] = h
            h_s[tile, :] = h

        u_ref[...] = _dot_nt(h_s[tile, :], wbuf[...])

    first_pass = lambda p, i, o: (jnp.where(p == 0, i, n_t - 1), 0)
    grid_spec = pltpu.PrefetchScalarGridSpec(
        num_scalar_prefetch=1, grid=(4, n_t),
        in_specs=[pl.BlockSpec((tm, D_MODEL), first_pass), pl.BlockSpec((1, D_MODEL), lambda p, i, o: (0, 0))]
        + [HBM_SPEC] * n_arr,
        out_specs=[pl.BlockSpec((tm, width), lambda p, i, o: (i, o[p])), pl.BlockSpec((tm, D_MODEL), first_pass)]
        + [HBM_SPEC] * n_arr,
        scratch_shapes=[pltpu.VMEM((t_len, D_MODEL), BF16), pltpu.VMEM((width, D_MODEL), BF16)]
        + _exchange_scratch(n_arr, 7) + [pltpu.SemaphoreType.DMA((n_arr,)), pltpu.SemaphoreType.DMA((1,))])
    return pl.pallas_call(
        body, grid_spec=grid_spec,
        out_shape=[jax.ShapeDtypeStruct((t_len, IN_COLS), F32), jax.ShapeDtypeStruct((t_len, D_MODEL), BF16)]
        + [jax.ShapeDtypeStruct((N_DEV * s.shape[0], s.shape[1]), s.dtype) for s in shards],
        compiler_params=_params(("arbitrary", "arbitrary"), 48), name="in_proj",
    )(order, x, g_mix, *shards)


def _conv3_chunk(u_ref, r, cv_prev, cw, row):
    gb = u_ref[pl.ds(r, SUB), OFF_GB:OFF_GB + CONV_WIDTH]
    gc = u_ref[pl.ds(r, SUB), OFF_GC:OFF_GC + CONV_WIDTH]
    v = u_ref[pl.ds(r, SUB), OFF_V:OFF_V + CONV_WIDTH]
    cv = gc * v
    cv_m1 = _down(cv, cv_prev, 1, row)
    cv_m2 = _down(cv, cv_prev, 2, row)
    cq = cw[2:3, :] * cv + cw[1:2, :] * cv_m1 + cw[0:1, :] * cv_m2
    return gb, gc, v, cv, cv_m1, cv_m2, cq


def _conv4_chunk(u_ref, r, xin_prev, rw, rb, row):
    xin = u_ref[pl.ds(r, SUB), OFF_XR:OFF_XR + LRU_WIDTH]
    m1 = _down(xin, xin_prev, 1, row)
    m2 = _down(xin, xin_prev, 2, row)
    m3 = _down(xin, xin_prev, 3, row)
    xr = rw[3:4, :] * xin + rw[2:3, :] * m1 + rw[1:2, :] * m2 + rw[0:1, :] * m3 + rb
    return xin, m1, m2, m3, xr


def _mixer_fwd(u, conv_w, rnn_conv_w, rnn_conv_b, wa, b_a, wx, b_x, lam, gnc, gnr, shards, tm):
    t_len = u.shape[0]
    n_steps = t_len // tm
    n_chunks = tm // SUB
    n_arr = len(shards)

    def body(u_ref, cw_ref, rw_ref, rb_ref, wa_ref, ba_ref, wx_ref, bx_ref, lam_ref, gnc_ref, gnr_ref, *rest):
        shard_refs = rest[0:n_arr]
        hs_ref, y_ref, xr_s, ra_ref, ii_ref, mult_ref = rest[n_arr:n_arr + 6]
        fulls = rest[n_arr + 6:2 * n_arr + 6]
        (y_s, pa_s, px_s, wabd, wxbd, cv_car, xin_car, h_car,
         send_sems, recv_sems, local_sems) = rest[2 * n_arr + 6:]
        _host_all_gather(pl.program_id(0), n_steps, shard_refs, fulls, send_sems, recv_sems, local_sems)

        @pl.when(pl.program_id(0) == 0)
        def _():
            cv_car[...] = jnp.zeros(cv_car.shape, F32)
            xin_car[...] = jnp.zeros(xin_car.shape, F32)
            h_car[...] = jnp.zeros(h_car.shape, F32)
            wabd[...] = _expand_heads(wa_ref[...])
            wxbd[...] = _expand_heads(wx_ref[...])

        row_c = lax.broadcasted_iota(jnp.int32, (SUB, CONV_WIDTH), 0)
        row_r = lax.broadcasted_iota(jnp.int32, (SUB, LRU_WIDTH), 0)
        cw = cw_ref[...]
        rw = rw_ref[...]
        rb = rb_ref[...]
        g_c = gnc_ref[...]
        g_r = gnr_ref[...]
        sp_c = LRU_C * _softplus_neg(lam_ref[...])

        def convs(i, carry):
            cv_prev, xin_prev = carry
            r = pl.multiple_of(i * SUB, SUB)
            gb, _, _, cv, _, _, cq = _conv3_chunk(u_ref, r, cv_prev, cw, row_c)
            y_c = gb * cq
            y_s[pl.ds(r, SUB), 0:CONV_WIDTH] = y_c * _rms(y_c) * g_c
            xin, _, _, _, xr = _conv4_chunk(u_ref, r, xin_prev, rw, rb, row_r)
            xr_s[pl.ds(r, SUB), :] = xr
            return cv, xin

        cv_last, xin_last = _chunk_loop(n_chunks, convs, (cv_car[...], xin_car[...]), in_flight=8)
        cv_car[...] = cv_last
        xin_car[...] = xin_last

        xrb = xr_s[...].astype(BF16)
        pa_s[...] = _block_diag_apply(xrb, wabd) + ba_ref[...]
        px_s[...] = _block_diag_apply(xrb, wxbd) + bx_ref[...]

        def recur(i, h_prev):
            r = pl.multiple_of(i * SUB, SUB)
            xr = xr_s[pl.ds(r, SUB), :]
            ra, ii, a, mult, _ = _lru_gates(pa_s[pl.ds(r, SUB), :], px_s[pl.ds(r, SUB), :], sp_c)
            ra_ref[pl.ds(r, SUB), :] = ra
            ii_ref[pl.ds(r, SUB), :] = ii
            mult_ref[pl.ds(r, SUB), :] = mult
            a_cum, b_cum = _scan8_fwd(a, mult * ii * xr, row_r)
            h = a_cum * h_prev + b_cum
            hs_ref[pl.ds(r, SUB), :] = h
            ge, _ = _gelu(u_ref[pl.ds(r, SUB), OFF_G:OFF_G + LRU_WIDTH])
            y_r = h * ge
            y_s[pl.ds(r, SUB), CONV_WIDTH:MIX_WIDTH] = y_r * _rms(y_r) * g_r
            return h[SUB - 1:SUB, :]

        h_car[...] = _chunk_loop(n_chunks, recur, h_car[...], in_flight=8)

        y_ref[...] = y_s[...].astype(BF16)

    row_tile = lambda w: pl.BlockSpec((tm, w), lambda i: (i, 0))
    whole = lambda a: pl.BlockSpec(a.shape, lambda i: (0,) * a.ndim)
    smalls = (conv_w, rnn_conv_w, rnn_conv_b, wa, b_a, wx, b_x, lam, gnc, gnr)
    return pl.pallas_call(
        body, grid=(n_steps,),
        in_specs=[row_tile(IN_COLS)] + [whole(a) for a in smalls] + [HBM_SPEC] * n_arr,
        out_specs=[row_tile(LRU_WIDTH), row_tile(MIX_WIDTH)] + [row_tile(LRU_WIDTH)] * 4 + [HBM_SPEC] * n_arr,
        out_shape=[jax.ShapeDtypeStruct((t_len, LRU_WIDTH), F32), jax.ShapeDtypeStruct((t_len, MIX_WIDTH), BF16)]
        + [jax.ShapeDtypeStruct((t_len, LRU_WIDTH), F32)] * 4
        + [jax.ShapeDtypeStruct((N_DEV,) + s.shape, BF16) for s in shards],
        scratch_shapes=[pltpu.VMEM((tm, MIX_WIDTH), F32),
                        pltpu.VMEM((tm, LRU_WIDTH), F32), pltpu.VMEM((tm, LRU_WIDTH), F32),
                        pltpu.VMEM((LRU_WIDTH, GROUP), BF16), pltpu.VMEM((LRU_WIDTH, GROUP), BF16),
                        pltpu.VMEM((SUB, CONV_WIDTH), F32), pltpu.VMEM((SUB, LRU_WIDTH), F32),
                        pltpu.VMEM((1, LRU_WIDTH), F32)]
        + _exchange_scratch(n_arr, 7) + [pltpu.SemaphoreType.DMA((n_arr,))],
        compiler_params=_params(("arbitrary",), 56), name="mixer_fwd",
    )(u, *smalls, *shards)


def _mlp_up(x, y, g_mlp, w_out, w1, w2_shard, tm):
    t_len = x.shape[0]
    n_steps = t_len // tm
    n_blk, _, blk = w1.shape

    def body(x_ref, y_ref, gm_ref, wout_hbm, w1_hbm, w2_ref, x1_ref, h2_ref, z_ref, w2_full,
             wout_s, w1_s, sem, send_sems, recv_sems, local_sems):
        step = pl.program_id(0)
        _host_all_gather(step, n_steps, [w2_ref], [w2_full], send_sems, recv_sems, local_sems)

        load_wout = pltpu.make_async_copy(wout_hbm, wout_s, sem.at[0])
        load_w1 = pltpu.make_async_copy(w1_hbm, w1_s, sem.at[1])

        @pl.when(step == 0)
        def _():
            load_wout.start()
            load_w1.start()
            load_wout.wait()

        x1v = x_ref[...] + jnp.dot(y_ref[...], wout_s[...], preferred_element_type=F32)
        x1_ref[...] = x1v
        h2 = (x1v * _rms(x1v) * gm_ref[...]).astype(BF16)
        h2_ref[...] = h2

        @pl.when(step == 0)
        def _():
            load_w1.wait()

        for k in range(n_blk):
            rp = jnp.maximum(jnp.dot(h2, w1_s[k], preferred_element_type=F32), 0.0)
            z_ref[:, k * blk:(k + 1) * blk] = (rp * rp).astype(BF16)

    row_tile = lambda w: pl.BlockSpec((tm, w), lambda i: (i, 0))
    return pl.pallas_call(
        body, grid=(n_steps,),
        in_specs=[row_tile(D_MODEL), row_tile(MIX_WIDTH), pl.BlockSpec((1, D_MODEL), lambda i: (0, 0)),
                  HBM_SPEC, HBM_SPEC, HBM_SPEC],
        out_specs=[row_tile(D_MODEL), row_tile(D_MODEL), row_tile(D_FF), HBM_SPEC],
        out_shape=[jax.ShapeDtypeStruct((t_len, D_MODEL), F32), jax.ShapeDtypeStruct((t_len, D_MODEL), BF16),
                   jax.ShapeDtypeStruct((t_len, D_FF), BF16), jax.ShapeDtypeStruct((N_DEV,) + w2_shard.shape, BF16)],
        scratch_shapes=[pltpu.VMEM(w_out.shape, BF16), pltpu.VMEM(w1.shape, BF16), pltpu.SemaphoreType.DMA((2,))]
        + _exchange_scratch(1, 7) + [pltpu.SemaphoreType.DMA((1,))],
        compiler_params=_params(("arbitrary",), 48), name="mlp_up",
    )(x, y, g_mlp, w_out, w1, w2_shard)


def _mlp_down_bwd(x1, z, target, g_mlp, g_f, w1, w2, tm):
    t_len = x1.shape[0]
    n_steps = t_len // tm
    n_blk, _, blk = w1.shape

    def body(x1_ref, z_ref, tg_ref, gm_ref, gf_ref, w1_hbm, w2_hbm, dx1_ref, dx2_ref, vec_ref, dpre_hbm,
             w1_s, w2_s, dp_s, sem, out_sem):
        step = pl.program_id(0)
        rows = pl.ds(pl.multiple_of(step * tm, tm), tm)
        dp_out = pltpu.make_async_copy(dp_s, dpre_hbm.at[rows, :], out_sem.at[0])

        load_w1 = pltpu.make_async_copy(w1_hbm, w1_s, sem.at[0])
        load_w2 = pltpu.make_async_copy(w2_hbm, w2_s, sem.at[1])

        @pl.when(step == 0)
        def _():
            load_w2.start()
            load_w1.start()
            vec_ref[...] = jnp.zeros(vec_ref.shape, F32)
            load_w2.wait()

        x1v = x1_ref[...]
        g_m = gm_ref[...]
        g_o = gf_ref[...]
        r2 = _rms(x1v)
        x1h = x1v * r2
        x2 = x1v + jnp.dot(z_ref[...], w2_s[...], preferred_element_type=F32)
        r3 = _rms(x2)
        x2h = x2 * r3
        err = x2h * g_o - tg_ref[...]
        dout = err * (1.0 / D_MODEL)
        vec_ref[ROW_LOSS:ROW_LOSS + 1, :] += (0.5 / D_MODEL) * jnp.sum(err * err, axis=0, keepdims=True)
        vec_ref[ROW_GF:ROW_GF + 1, :] += jnp.sum(dout * x2h, axis=0, keepdims=True)
        dx2 = _rms_bwd(dout, x2h, r3, g_o)
        dx2b = dx2.astype(BF16)
        dx2_ref[...] = dx2b
        dh2 = jnp.zeros((tm, D_MODEL), F32)

        @pl.when(step > 0)
        def _():
            dp_out.wait()

        @pl.when(step == 0)
        def _():
            load_w1.wait()

        for k in range(n_blk):
            cols = slice(k * blk, (k + 1) * blk)
            dz = _dot_nt(dx2b, w2_s[cols, :])
            dpb = (dz * 2.0 * jnp.sqrt(z_ref[:, cols].astype(F32))).astype(BF16)
            dp_s[:, cols] = dpb
            dh2 = dh2 + _dot_nt(dpb, w1_s[k])
        dp_out.start()
        vec_ref[ROW_GMLP:ROW_GMLP + 1, :] += jnp.sum(dh2 * x1h, axis=0, keepdims=True)
        dx1_ref[...] = dx2 + _rms_bwd(dh2, x1h, r2, g_m)

        @pl.when(step == n_steps - 1)
        def _():
            dp_out.wait()

    row_tile = lambda w: pl.BlockSpec((tm, w), lambda i: (i, 0))
    vec_spec = pl.BlockSpec((1, D_MODEL), lambda i: (0, 0))
    return pl.pallas_call(
        body, grid=(n_steps,),
        in_specs=[row_tile(D_MODEL), row_tile(D_FF), row_tile(D_MODEL), vec_spec, vec_spec, HBM_SPEC, HBM_SPEC],
        out_specs=[row_tile(D_MODEL), row_tile(D_MODEL), pl.BlockSpec((SUB, D_MODEL), lambda i: (0, 0)), HBM_SPEC],
        out_shape=[jax.ShapeDtypeStruct((t_len, D_MODEL), F32), jax.ShapeDtypeStruct((t_len, D_MODEL), BF16),
                   jax.ShapeDtypeStruct((SUB, D_MODEL), F32), jax.ShapeDtypeStruct((t_len, D_FF), BF16)],
        scratch_shapes=[pltpu.VMEM(w1.shape, BF16), pltpu.VMEM(w2.shape, BF16), pltpu.VMEM((tm, D_FF), BF16),
                        pltpu.SemaphoreType.DMA((2,)), pltpu.SemaphoreType.DMA((1,))],
        compiler_params=_params(("arbitrary",), 56), name="mlp_down_bwd",
    )(x1, z, target, g_mlp, g_f, w1, w2)


def _mixer_bwd(u, hs, dx1, saved, conv_w, rnn_conv_w, rnn_conv_b, wa, b_a, wx, b_x, lam, gnc, gnr, w_out,
               chip_sums, g_wout, tm):
    t_len = u.shape[0]
    n_tiles = t_len // tm
    n_chunks = tm // SUB
    per_tile = tm // SUB
    n_sums = len(chip_sums)

    def body(u_ref, up_ref, hs_ref, hp_ref, dx1_ref, xr_ref, ra_ref, ii_ref, mult_ref,
             cw_ref, rw_ref, rb_ref, wa_ref, ba_ref, wx_ref, bx_ref, lam_ref, gnc_ref, gnr_ref, wout_ref, *rest):
        hsends = rest[0:n_sums]
        gwout_ref = rest[n_sums]
        du_ref, vec_ref, wab_ref = rest[n_sums + 1:n_sums + 4]
        hrecvs = rest[n_sums + 4:2 * n_sums + 4]
        sib_wout = rest[2 * n_sums + 4]
        (du_s, dy_s, dpa_s, dpx_s, dxr_s, wabd, wxbd, acc, dwa_acc, dwx_acc,
         a_car, dh_car, dcq_car, dxr_car, i_send, i_recv, d_send, d_recv) = rest[2 * n_sums + 5:]
        step = pl.program_id(0)
        _host_chip_exchange(step, n_tiles, hsends, hrecvs, i_send, i_recv)
        _host_pair_exchange(step, n_tiles, [gwout_ref], [sib_wout], d_send, d_recv)
        has_prev = (step < n_tiles - 1).astype(F32)

        @pl.when(step == 0)
        def _():
            acc[...] = jnp.zeros(acc.shape, F32)
            dwa_acc[...] = jnp.zeros(dwa_acc.shape, F32)
            dwx_acc[...] = jnp.zeros(dwx_acc.shape, F32)
            a_car[...] = jnp.ones(a_car.shape, F32)
            dh_car[...] = jnp.zeros(dh_car.shape, F32)
            dcq_car[...] = jnp.zeros(dcq_car.shape, F32)
            dxr_car[...] = jnp.zeros(dxr_car.shape, F32)
            wabd[...] = _expand_heads(wa_ref[...])
            wxbd[...] = _expand_heads(wx_ref[...])

        row_c = lax.broadcasted_iota(jnp.int32, (SUB, CONV_WIDTH), 0)
        row_r = lax.broadcasted_iota(jnp.int32, (SUB, LRU_WIDTH), 0)
        cw = cw_ref[...]
        rw = rw_ref[...]
        rb = rb_ref[...]
        g_c = gnc_ref[...]
        g_r = gnr_ref[...]
        sp_c = LRU_C * _softplus_neg(lam_ref[...])

        up = up_ref[...] * has_prev
        cv_before = up[:, OFF_GC:OFF_GC + CONV_WIDTH] * up[:, OFF_V:OFF_V + CONV_WIDTH]
        xin_before = up[:, OFF_XR:OFF_XR + LRU_WIDTH]
        hs_before = hp_ref[...] * has_prev

        dy_s[...] = _dot_nt(dx1_ref[...].astype(BF16), wout_ref[...])

        xrb = xr_ref[...].astype(BF16)

        def recur_bwd(j, carry):
            a_later, dh_later = carry
            i = n_chunks - 1 - j
            r = pl.multiple_of(i * SUB, SUB)
            rp = pl.multiple_of(jnp.maximum(i - 1, 0) * SUB, SUB)
            xr = xr_ref[pl.ds(r, SUB), :]
            hs_c = hs_ref[pl.ds(r, SUB), :]
            hs_prev = jnp.where(i == 0, hs_before, hs_ref[pl.ds(rp, SUB), :])
            h_m1 = _down(hs_c, hs_prev, 1, row_r)
            ra = ra_ref[pl.ds(r, SUB), :]
            ii = ii_ref[pl.ds(r, SUB), :]
            mult = mult_ref[pl.ds(r, SUB), :]
            a = jnp.exp(-ra * sp_c)
            inv_mult = lax.rsqrt(mult * mult)
            ge, dge = _gelu(u_ref[pl.ds(r, SUB), OFF_G:OFF_G + LRU_WIDTH])
            y_r = hs_c * ge
            rr = _rms(y_r)
            yhat = y_r * rr
            dyn = dy_s[pl.ds(r, SUB), CONV_WIDTH:MIX_WIDTH]
            acc[ACC_GNR] += dyn * yhat
            dy_r = _rms_bwd(dyn, yhat, rr, g_r)
            du_s[pl.ds(r, SUB), OFF_G:OFF_G + LRU_WIDTH] = dy_r * hs_c * dge
            a_cum, d_cum = _scan8_rev(_up(a, a_later, 1, row_r), dy_r * ge, row_r)
            dh = a_cum * dh_later + d_cum
            dmult = dh * ii * xr
            dii = dh * mult * xr
            dxr_s[pl.ds(r, SUB), :] = dh * mult * ii
            dla = dh * h_m1 * a - dmult * a * a * inv_mult
            acc[ACC_SP] += -dla * ra
            dpa = -dla * sp_c * ra * (1.0 - ra)
            dpx = dii * ii * (1.0 - ii)
            acc[ACC_BA] += dpa
            acc[ACC_BX] += dpx
            dpa_s[pl.ds(r, SUB), :] = dpa
            dpx_s[pl.ds(r, SUB), :] = dpx
            return a, dh[0:1, :]

        a_first, dh_first = _chunk_loop(n_chunks, recur_bwd, (a_car[...], dh_car[...]), in_flight=8)
        a_car[...] = a_first
        dh_car[...] = dh_first

        dpab = dpa_s[...].astype(BF16)
        dpxb = dpx_s[...].astype(BF16)
        dxr_s[...] += _block_diag_apply_t(dpab, wabd) + _block_diag_apply_t(dpxb, wxbd)
        for g in range(LRU_WIDTH // GROUP):
            cols = slice(g * GROUP, (g + 1) * GROUP)
            dwa_acc[cols, :] += _dot_tn(xrb[:, cols], dpab[:, cols])
            dwx_acc[cols, :] += _dot_tn(xrb[:, cols], dpxb[:, cols])

        def convs_bwd(j, carry):
            dcq_later, dxr_later = carry
            i = n_chunks - 1 - j
            r = pl.multiple_of(i * SUB, SUB)
            rp = pl.multiple_of(jnp.maximum(i - 1, 0) * SUB, SUB)
            cv_prev = jnp.where(i == 0, cv_before,
                                u_ref[pl.ds(rp, SUB), OFF_GC:OFF_GC + CONV_WIDTH]
                                * u_ref[pl.ds(rp, SUB), OFF_V:OFF_V + CONV_WIDTH])
            gb, gc, v, cv, cv_m1, cv_m2, cq = _conv3_chunk(u_ref, r, cv_prev, cw, row_c)
            y_c = gb * cq
            rc = _rms(y_c)
            yhat = y_c * rc
            dyn = dy_s[pl.ds(r, SUB), 0:CONV_WIDTH]
            acc[ACC_GNC, :, 0:CONV_WIDTH] += dyn * yhat
            dy_c = _rms_bwd(dyn, yhat, rc, g_c)
            dcq = dy_c * gb
            dcv = (cw[2:3, :] * dcq + cw[1:2, :] * _up(dcq, dcq_later, 1, row_c)
                   + cw[0:1, :] * _up(dcq, dcq_later, 2, row_c))
            acc[ACC_CW + 2, :, 0:CONV_WIDTH] += dcq * cv
            acc[ACC_CW + 1, :, 0:CONV_WIDTH] += dcq * cv_m1
            acc[ACC_CW + 0, :, 0:CONV_WIDTH] += dcq * cv_m2
            du_s[pl.ds(r, SUB), OFF_GB:OFF_GB + CONV_WIDTH] = dy_c * cq
            du_s[pl.ds(r, SUB), OFF_GC:OFF_GC + CONV_WIDTH] = dcv * v
            du_s[pl.ds(r, SUB), OFF_V:OFF_V + CONV_WIDTH] = dcv * gc

            xin_prev = jnp.where(i == 0, xin_before, u_ref[pl.ds(rp, SUB), OFF_XR:OFF_XR + LRU_WIDTH])
            xin, m1, m2, m3, _ = _conv4_chunk(u_ref, r, xin_prev, rw, rb, row_r)
            dxr = dxr_s[pl.ds(r, SUB), :]
            du_s[pl.ds(r, SUB), OFF_XR:OFF_XR + LRU_WIDTH] = (
                rw[3:4, :] * dxr + rw[2:3, :] * _up(dxr, dxr_later, 1, row_r)
                + rw[1:2, :] * _up(dxr, dxr_later, 2, row_r) + rw[0:1, :] * _up(dxr, dxr_later, 3, row_r))
            acc[ACC_RW + 3] += dxr * xin
            acc[ACC_RW + 2] += dxr * m1
            acc[ACC_RW + 1] += dxr * m2
            acc[ACC_RW + 0] += dxr * m3
            acc[ACC_BR] += dxr
            return dcq, dxr

        dcq_first, dxr_first = _chunk_loop(n_chunks, convs_bwd, (dcq_car[...], dxr_car[...]), in_flight=8)
        dcq_car[...] = dcq_first
        dxr_car[...] = dxr_first

        du_ref[...] = du_s[...].astype(BF16)

        @pl.when(step == n_tiles - 1)
        def _():
            vec_ref[...] = jnp.zeros(vec_ref.shape, F32)
            rows = {ACC_GNC: ROW_GNC, ACC_GNR: ROW_GNR, ACC_BR: ROW_BR, ACC_BA: ROW_BA, ACC_BX: ROW_BX}
            for k in range(3):
                rows[ACC_CW + k] = ROW_CW + k
            for k in range(4):
                rows[ACC_RW + k] = ROW_RW + k
            for slot, out_row in rows.items():
                o = out_row - ROW_GNC
                vec_ref[o:o + 1, :] = jnp.sum(acc[slot], axis=0, keepdims=True)
            lam_v = lam_ref[...]
            dsp = jnp.sum(acc[ACC_SP], axis=0, keepdims=True)
            o = ROW_LAM - ROW_GNC
            vec_ref[o:o + 1, :] = -dsp * LRU_C / (1.0 + jnp.exp(lam_v))
            wab_ref[0:LRU_WIDTH, :] = _fold_heads(dwa_acc[...])
            wab_ref[LRU_WIDTH:2 * LRU_WIDTH, :] = _fold_heads(dwx_acc[...])

    rev = lambda w: pl.BlockSpec((tm, w), lambda s: (n_tiles - 1 - s, 0))
    before = lambda w: pl.BlockSpec((SUB, w), lambda s: (jnp.maximum((n_tiles - 1 - s) * per_tile - 1, 0), 0))
    whole = lambda a: pl.BlockSpec(a.shape, lambda s: (0,) * a.ndim)
    smalls = (conv_w, rnn_conv_w, rnn_conv_b, wa, b_a, wx, b_x, lam, gnc, gnr, w_out)
    full = lambda w: pltpu.VMEM((tm, w), F32)
    return pl.pallas_call(
        body, grid=(n_tiles,),
        in_specs=[rev(IN_COLS), before(IN_COLS), rev(LRU_WIDTH), before(LRU_WIDTH), rev(D_MODEL)]
        + [rev(LRU_WIDTH)] * len(saved) + [whole(a) for a in smalls] + [HBM_SPEC] * (n_sums + 1),
        out_specs=[rev(IN_COLS), pl.BlockSpec((16, D_MODEL), lambda s: (0, 0)),
                   pl.BlockSpec((2 * LRU_WIDTH, HEAD_DIM), lambda s: (0, 0))] + [HBM_SPEC] * (n_sums + 1),
        out_shape=[jax.ShapeDtypeStruct((t_len, IN_COLS), BF16), jax.ShapeDtypeStruct((16, D_MODEL), F32),
                   jax.ShapeDtypeStruct((2 * LRU_WIDTH, HEAD_DIM), F32)]
        + [jax.ShapeDtypeStruct(s.shape, BF16) for s in chip_sums]
        + [jax.ShapeDtypeStruct((4,) + g_wout.shape[1:], BF16)],
        scratch_shapes=[full(IN_COLS), full(MIX_WIDTH), full(LRU_WIDTH), full(LRU_WIDTH), full(LRU_WIDTH),
                        pltpu.VMEM((LRU_WIDTH, GROUP), BF16), pltpu.VMEM((LRU_WIDTH, GROUP), BF16),
                        pltpu.VMEM((N_ACC, SUB, LRU_WIDTH), F32),
                        pltpu.VMEM((LRU_WIDTH, GROUP), F32), pltpu.VMEM((LRU_WIDTH, GROUP), F32),
                        pltpu.VMEM((SUB, LRU_WIDTH), F32), pltpu.VMEM((1, LRU_WIDTH), F32),
                        pltpu.VMEM((SUB, CONV_WIDTH), F32), pltpu.VMEM((SUB, LRU_WIDTH), F32)]
        + _exchange_scratch(n_sums, 3) + _exchange_scratch(1, 4),
        compiler_params=_params(("arbitrary",), 56), name="mixer_bwd",
    )(u, u, hs, hs, dx1, *saved, *smalls, *chip_sums, g_wout)


def _in_proj_bwd(du, dx1, x, g_mix, win_t, tm, chip_sums, g_own):
    t_len = x.shape[0]
    n_steps = t_len // tm

    def body(du_ref, dx1_ref, x_ref, g_ref, w_ref, hs_ref, gown_ref,
             dx_ref, vec_ref, landed_ref, sib_ref, i_send, i_recv, d_send, d_recv):
        step = pl.program_id(0)
        _host_chip_exchange(step, n_steps, [hs_ref], [landed_ref], i_send, i_recv)
        _host_half_exchange(step, n_steps, gown_ref, sib_ref, d_send, d_recv)

        @pl.when(step == 0)
        def _():
            vec_ref[...] = jnp.zeros(vec_ref.shape, F32)

        dh = jnp.dot(du_ref[...], w_ref[...], preferred_element_type=F32)
        xv = x_ref[...]
        r1 = _rms(xv)
        xh = xv * r1
        vec_ref[0:1, :] += jnp.sum(dh * xh, axis=0, keepdims=True)
        dx_ref[...] = dx1_ref[...] + _rms_bwd(dh, xh, r1, g_ref[...])

    row_tile = lambda w: pl.BlockSpec((tm, w), lambda i: (i, 0))
    half_shape = (g_own.shape[0], g_own.shape[1] // 2, g_own.shape[2])
    return pl.pallas_call(
        body, grid=(n_steps,),
        in_specs=[row_tile(IN_COLS), row_tile(D_MODEL), row_tile(D_MODEL), pl.BlockSpec((1, D_MODEL), lambda i: (0, 0)),
                  pl.BlockSpec((IN_COLS, D_MODEL), lambda i: (0, 0))] + [HBM_SPEC] * 2,
        out_specs=[row_tile(D_MODEL), pl.BlockSpec((SUB, D_MODEL), lambda i: (0, 0))] + [HBM_SPEC] * 2,
        out_shape=[jax.ShapeDtypeStruct((t_len, D_MODEL), F32), jax.ShapeDtypeStruct((SUB, D_MODEL), F32),
                   jax.ShapeDtypeStruct(chip_sums.shape, BF16), jax.ShapeDtypeStruct(half_shape, BF16)],
        scratch_shapes=_exchange_scratch(1, 3) + [pltpu.SemaphoreType.DMA((1,)), pltpu.SemaphoreType.DMA((1,))],
        compiler_params=_params(("arbitrary",), 56), name="in_proj_bwd",
    )(du, dx1, x, g_mix, win_t, chip_sums, g_own)


def _tn_weight_grad(a, b, tk, name, pair=(), col_blocks=1):
    t_len, m = a.shape
    n = b.shape[1]
    n_steps = t_len // tk
    sent = tuple(pair)
    n_sent = len(sent)

    def body(a_ref, b_ref, *rest):
        srcs = rest[0:n_sent]
        o_ref = rest[n_sent]
        dsts = rest[n_sent + 1:2 * n_sent + 1]
        acc = rest[2 * n_sent + 1]
        sems = rest[2 * n_sent + 2:]
        j = pl.program_id(0)
        if pair:
            _host_pair_exchange(j, n_steps, srcs, dsts, *sems)

        @pl.when(j == 0)
        def _():
            acc[...] = jnp.zeros(acc.shape, F32)

        acc[...] += _dot_tn(a_ref[...].astype(BF16), b_ref[...].astype(BF16))

        @pl.when(j == n_steps - 1)
        def _():
            if col_blocks == 1:
                o_ref[...] = acc[...].astype(BF16)
            else:
                for k in range(col_blocks):
                    o_ref[k] = acc[:, k * nb:(k + 1) * nb].astype(BF16)

    nb = n // col_blocks
    out_dims = (m, n) if col_blocks == 1 else (col_blocks, m, nb)
    landed = [jax.ShapeDtypeStruct((4,) + g.shape[1:], BF16) for g in pair]
    scratch = [pltpu.VMEM((m, n), F32)]
    if n_sent:
        scratch += _exchange_scratch(n_sent, 4)
    return pl.pallas_call(
        body, grid=(n_steps,),
        in_specs=[pl.BlockSpec((tk, m), lambda j: (j, 0)), pl.BlockSpec((tk, n), lambda j: (j, 0))]
        + [HBM_SPEC] * n_sent,
        out_specs=[pl.BlockSpec(out_dims, lambda j: (0,) * len(out_dims))] + [HBM_SPEC] * n_sent,
        out_shape=[jax.ShapeDtypeStruct(out_dims, BF16)] + landed,
        scratch_shapes=scratch,
        compiler_params=_params(("arbitrary",), 56), name=name,
    )(a, b, *sent)


def _w_in_grad_part(du, h, tk, name, chip_ids, chip=(), halves=None, small=None):
    t_len = du.shape[0]
    n_t = t_len // tk
    n_q = chip_ids.shape[0]
    width = 2 * (IN_COLS // N_DEV)
    n_steps = n_q * n_t
    n_chip = len(chip)
    sent = tuple(chip) + (() if halves is None else (halves,)) + (() if small is None else tuple(small))
    n_sent = len(sent)

    def body(ids_ref, a_ref, b_ref, *rest):
        srcs = rest[0:n_sent]
        o_ref = rest[n_sent]
        dsts = rest[n_sent + 1:2 * n_sent + 1]
        acc = rest[2 * n_sent + 1]
        sems = list(rest[2 * n_sent + 2:])
        j = pl.program_id(1)
        step = pl.program_id(0) * n_t + j
        if chip:
            _host_chip_exchange(step, n_steps, srcs[0:n_chip], dsts[0:n_chip], sems.pop(0), sems.pop(0))
        if halves is not None:
            _host_half_exchange(step, n_steps, srcs[n_chip], dsts[n_chip], sems.pop(0), sems.pop(0))
        if small is not None:
            _host_small_exchange(step, n_steps, *srcs[n_sent - 3:], *dsts[n_sent - 3:], *sems)

        @pl.when(j == 0)
        def _():
            acc[...] = jnp.zeros(acc.shape, F32)

        acc[...] += _dot_tn(a_ref[...], b_ref[...])

        @pl.when(j == n_t - 1)
        def _():
            o_ref[0] = acc[...].astype(BF16)

    landed = [jax.ShapeDtypeStruct(s.shape, BF16) for s in chip]
    scratch = [pltpu.VMEM((width, D_MODEL), F32)]
    if chip:
        scratch += _exchange_scratch(len(chip), 3)
    if halves is not None:
        landed.append(jax.ShapeDtypeStruct((halves.shape[0], halves.shape[1] // 2, halves.shape[2]), BF16))
        scratch += [pltpu.SemaphoreType.DMA((halves.shape[0],)), pltpu.SemaphoreType.DMA((halves.shape[0],))]
    if small is not None:
        vec_m, vec_b, wab = small
        landed += [jax.ShapeDtypeStruct((N_DEV,) + vec_m.shape, F32), jax.ShapeDtypeStruct((N_DEV,) + vec_b.shape, F32),
                   jax.ShapeDtypeStruct((N_DEV, wab.shape[0] // N_DEV, wab.shape[1]), F32)]
        scratch += _exchange_scratch(3, N_DEV) + [pltpu.SemaphoreType.DMA((2,))]
    grid_spec = pltpu.PrefetchScalarGridSpec(
        num_scalar_prefetch=1, grid=(n_q, n_t),
        in_specs=[pl.BlockSpec((tk, width), lambda q, j, ids: (j, ids[q])),
                  pl.BlockSpec((tk, D_MODEL), lambda q, j, ids: (j, 0))] + [HBM_SPEC] * n_sent,
        out_specs=[pl.BlockSpec((1, width, D_MODEL), lambda q, j, ids: (q, 0, 0))] + [HBM_SPEC] * n_sent,
        scratch_shapes=scratch)
    return pl.pallas_call(
        body, grid_spec=grid_spec, out_shape=[jax.ShapeDtypeStruct((n_q, width, D_MODEL), BF16)] + landed,
        compiler_params=_params(("arbitrary", "arbitrary"), 40), name=name,
    )(chip_ids, du, h, *sent)


def _adamw(w, g, m, v):
    m = ADAM_B1 * m + (1.0 - ADAM_B1) * g
    v = ADAM_B2 * v + (1.0 - ADAM_B2) * (g * g)
    delta = -ADAM_LR * ((m / BC1) / (jnp.sqrt(v / BC2) + ADAM_EPS) + ADAM_WD * w)
    return delta, m, v


def _update_sharded(g, landed, w, m, v, rows_blk, name):
    rows, cols = w.shape

    def body(g_ref, l_ref, w_ref, m_ref, v_ref, og, od, om, ov):
        gv = g_ref[...]
        for j in range(3):
            gv = gv + l_ref[j].astype(F32)
        delta, mn, vn = _adamw(w_ref[...], gv, m_ref[...], v_ref[...])
        og[...] = gv
        od[...] = delta
        om[...] = mn
        ov[...] = vn

    blk = pl.BlockSpec((rows_blk, cols), lambda i: (i, 0))
    shape = pltpu.HBM((rows, cols), F32)
    return pl.pallas_call(
        body, grid=(rows // rows_blk,),
        in_specs=[blk, pl.BlockSpec((3, rows_blk, cols), lambda i: (0, i, 0)), blk, blk, blk],
        out_specs=[blk] * 4, out_shape=[shape] * 4,
        compiler_params=_params(("arbitrary",), 32), name=name,
    )(*_in_hbm(g, landed, w, m, v))


def _update_w_in(g_own, sib_own, landed, w_t, m_t, v_t, core, cols_blk):
    rows, cols = w_t.shape

    def body(core_ref, g_ref, s_ref, l_ref, w_ref, m_ref, v_ref, og, od, om, ov):
        gv = g_ref[0, 0].astype(F32) + s_ref[0].astype(F32)
        for j in range(3):
            gv = gv + l_ref[j].astype(F32)
        delta, mn, vn = _adamw(w_ref[...], gv, m_ref[...], v_ref[...])
        og[...] = gv
        od[...] = delta
        om[...] = mn
        ov[...] = vn

    blk = pl.BlockSpec((rows, cols_blk), lambda i, cr: (0, i))
    grid_spec = pltpu.PrefetchScalarGridSpec(
        num_scalar_prefetch=1, grid=(cols // cols_blk,),
        in_specs=[pl.BlockSpec((1, 1, rows, cols_blk), lambda i, cr: (0, cr[0], 0, i)),
                  pl.BlockSpec((1, rows, cols_blk), lambda i, cr: (0, 0, i)),
                  pl.BlockSpec((3, rows, cols_blk), lambda i, cr: (0, 0, i)), blk, blk, blk],
        out_specs=[blk] * 4)
    return pl.pallas_call(
        body, grid_spec=grid_spec, out_shape=[pltpu.HBM((rows, cols), F32)] * 4,
        compiler_params=_params(("arbitrary",), 32), name="update_w_in",
    )(core, *_in_hbm(g_own.reshape(1, 2, rows, cols), sib_own, landed, w_t, m_t, v_t))


def _update_small(vsum, wsum, g_cw, g_rw, weights, moments_m, moments_v):
    n = len(weights)

    def body(*refs):
        vs, ws, gcw, grw = refs[0:4]
        w_refs = refs[4:4 + n]
        m_refs = refs[4 + n:4 + 2 * n]
        v_refs = refs[4 + 2 * n:4 + 3 * n]
        outs = refs[4 + 3 * n:]
        loss_ref = outs[0]
        loss_ref[...] = jnp.sum(vs[ROW_LOSS:ROW_LOSS + 1, :], axis=1, keepdims=True)
        grads = [
            vs[ROW_GMIX:ROW_GMIX + 1, :], gcw[...], grw[...], vs[ROW_BR:ROW_BR + 1, :],
            ws[0:LRU_WIDTH, :], vs[ROW_BA:ROW_BA + 1, :], ws[LRU_WIDTH:2 * LRU_WIDTH, :], vs[ROW_BX:ROW_BX + 1, :],
            vs[ROW_LAM:ROW_LAM + 1, :], vs[ROW_GNC:ROW_GNC + 1, 0:CONV_WIDTH], vs[ROW_GNR:ROW_GNR + 1, :],
            vs[ROW_GMLP:ROW_GMLP + 1, :], vs[ROW_GF:ROW_GF + 1, :],
        ]
        for k in range(n):
            gk = grads[k]
            delta, mn, vn = _adamw(w_refs[k][...], gk, m_refs[k][...], v_refs[k][...])
            outs[1 + 4 * k][...] = gk
            outs[2 + 4 * k][...] = delta
            outs[3 + 4 * k][...] = mn
            outs[4 + 4 * k][...] = vn

    whole = lambda a: pl.BlockSpec(a.shape, lambda i: (0,) * len(a.shape))
    out_shape = [jax.ShapeDtypeStruct((1, 1), F32)]
    for w in weights:
        out_shape += [jax.ShapeDtypeStruct(w.shape, F32)] * 4
    args = (vsum, wsum, g_cw, g_rw, *weights, *moments_m, *moments_v)
    return pl.pallas_call(
        body, grid=(1,), out_shape=out_shape, in_specs=[whole(a) for a in args], out_specs=[whole(s) for s in out_shape],
        compiler_params=_params(("arbitrary",), 32), name="update_small",
    )(*args)


def kernel(x, norm_mix_g, w_in, conv_w, rnn_conv_w, rnn_conv_b, w_a, b_a, w_x, b_x, lru_lambda, g_norm_conv, g_norm_rnn, w_out, norm_mlp_g, w_mlp_in, w_mlp_out, final_norm_g, loss_target, m_norm_mix_g, m_w_in, m_conv_w, m_rnn_conv_w, m_rnn_conv_b, m_w_a, m_b_a, m_w_x, m_b_x, m_lru_lambda, m_g_norm_conv, m_g_norm_rnn, m_w_out, m_norm_mlp_g, m_w_mlp_in, m_w_mlp_out, m_final_norm_g, v_norm_mix_g, v_w_in, v_conv_w, v_rnn_conv_w, v_rnn_conv_b, v_w_a, v_b_a, v_w_x, v_b_x, v_lru_lambda, v_g_norm_conv, v_g_norm_rnn, v_w_out, v_norm_mlp_g, v_w_mlp_in, v_w_mlp_out, v_final_norm_g):
    t_len = x.shape[1]
    my_id = 4 * lax.axis_index("x") + 2 * lax.axis_index("y") + lax.axis_index("c")
    tm = min(256, t_len)
    tb = min(512, t_len)
    tk = min(512, t_len)

    xs = x.reshape(t_len, D_MODEL)
    tgt = loss_target.reshape(t_len, D_MODEL)
    flat = lambda a: a.reshape(a.shape[-2:]) if a.ndim == 3 else a.reshape(1, -1)
    heads = lambda a: a.reshape(LRU_WIDTH, HEAD_DIM)

    turned = lambda a: jnp.transpose(flat(a))
    win_shard, wout_shard, w1_shard, w2_shard, cp_shard = _prep_shards(
        turned(w_in), flat(w_out), flat(w_mlp_in), flat(w_mlp_out), flat(conv_w), flat(rnn_conv_w))

    u, h, win_t, cp_full = _in_proj(xs, flat(norm_mix_g), (win_shard, cp_shard), min(1024, t_len))
    cpack = cp_full.reshape(N_DEV, 8, 128)
    conv_full = jnp.transpose(cpack[:, 0:3, 0:64], (1, 0, 2)).reshape(3, CONV_WIDTH)
    rnn_full = jnp.transpose(cpack[:, 3:7, :], (1, 0, 2)).reshape(4, LRU_WIDTH)
    mixer_small = (conv_full, rnn_full, flat(rnn_conv_b), heads(w_a), flat(b_a), heads(w_x), flat(b_x),
                   flat(lru_lambda), flat(g_norm_conv), flat(g_norm_rnn))
    hs, y, xr, gate_r, gate_i, mult, w1_blk, wout_blk = _mixer_fwd(u, *mixer_small, (w1_shard, wout_shard), tm)
    wout_f = wout_blk.reshape(MIX_WIDTH, D_MODEL)
    x1, h2, z, w2_blk = _mlp_up(xs, y, flat(norm_mlp_g), wout_f, w1_blk, w2_shard, tb)
    dx1, dx2, vec_m, dpre = _mlp_down_bwd(x1, z, tgt, flat(norm_mlp_g), flat(final_norm_g), w1_blk,
                                          w2_blk.reshape(D_FF, D_MODEL), tb)
    (g_w1,) = _tn_weight_grad(h2, dpre, tk, "w_mlp_in_grad", col_blocks=N_DEV)
    (g_w2,) = _tn_weight_grad(z, dx2, tk, "w_mlp_out_grad")
    g_w2 = g_w2.reshape(N_DEV, D_FF // N_DEV, D_MODEL)
    g_wout, sib_w1, sib_w2 = _tn_weight_grad(y, dx1, tk, "w_out_grad", pair=(g_w1, g_w2))
    g_wout = g_wout.reshape(N_DEV, MIX_WIDTH // N_DEV, D_MODEL)
    hsend_w1, own_w1 = _pair_sum(g_w1, sib_w1, "pair_sum_w_mlp_in")
    hsend_w2, own_w2 = _pair_sum(g_w2, sib_w2, "pair_sum_w_mlp_out")
    du, vec_b, wab, landed_w1, landed_w2, sib_wout = _mixer_bwd(
        u, hs, dx1, (xr, gate_r, gate_i, mult), *mixer_small, wout_f, (hsend_w1, hsend_w2), g_wout, tm)
    hsend_wout, own_wout = _pair_sum(g_wout, sib_wout, "pair_sum_w_out")
    ax, ay, ac = lax.axis_index("x"), lax.axis_index("y"), lax.axis_index("c")
    chip_ids = jnp.stack([2 * cx + cy for cx, cy in [(ax, ay)] + _other_chips(ax, ay)]).astype(jnp.int32)
    core = jnp.reshape(ac, (1,)).astype(jnp.int32)
    tw = min(1024, t_len)
    g_others, landed_wout, vrecv_m, vrecv_b, wrecv = _w_in_grad_part(
        du, h, tw, "w_in_grad_others", chip_ids[1:4], chip=(hsend_wout,), small=(vec_m, vec_b, wab))
    g_own, sib_others = _w_in_grad_part(du, h, tw, "w_in_grad_own", chip_ids[0:1], halves=g_others)
    hsend_win = _pair_sum_parts(g_others, sib_others, core)
    grad_x, vec_x, landed_win, sib_own = _in_proj_bwd(du, dx1, xs, flat(norm_mix_g), win_t, tm, hsend_win, g_own)

    vsum, wsum = _final_small(vrecv_m, vrecv_b, wab, wrecv, vec_x)

    up_win = _update_w_in(g_own, sib_own, landed_win, turned(w_in), turned(m_w_in), turned(v_w_in), core, 256)
    up_win = [jnp.transpose(a) for a in up_win]
    up_wout = _update_sharded(own_wout, landed_wout, flat(w_out), flat(m_w_out), flat(v_w_out), 96, "update_w_out")
    up_w1 = _update_sharded(own_w1, landed_w1, flat(w_mlp_in), flat(m_w_mlp_in), flat(v_w_mlp_in), 256,
                            "update_w_mlp_in")
    up_w2 = _update_sharded(own_w2, landed_w2, flat(w_mlp_out), flat(m_w_mlp_out), flat(v_w_mlp_out), 256,
                            "update_w_mlp_out")

    g_cw = lax.dynamic_slice(vsum, (ROW_CW, 64 * my_id), (3, 64))
    g_rw = lax.dynamic_slice(vsum, (ROW_RW, 128 * my_id), (4, 128))
    small_w = (norm_mix_g, conv_w, rnn_conv_w, rnn_conv_b, w_a, b_a, w_x, b_x, lru_lambda, g_norm_conv, g_norm_rnn,
               norm_mlp_g, final_norm_g)
    small_m = (m_norm_mix_g, m_conv_w, m_rnn_conv_w, m_rnn_conv_b, m_w_a, m_b_a, m_w_x, m_b_x, m_lru_lambda,
               m_g_norm_conv, m_g_norm_rnn, m_norm_mlp_g, m_final_norm_g)
    small_v = (v_norm_mix_g, v_conv_w, v_rnn_conv_w, v_rnn_conv_b, v_w_a, v_b_a, v_w_x, v_b_x, v_lru_lambda,
               v_g_norm_conv, v_g_norm_rnn, v_norm_mlp_g, v_final_norm_g)
    is_heads = (False, False, False, False, True, False, True, False, False, False, False, False, False)
    as2d = lambda arrs: [heads(a) if hd else flat(a) for a, hd in zip(arrs, is_heads)]
    small_out = _update_small(vsum, wsum, g_cw, g_rw, as2d(small_w), as2d(small_m), as2d(small_v))
    loss = small_out[0].reshape(())

    names = ["norm_mix_g", "w_in", "conv_w", "rnn_conv_w", "rnn_conv_b", "w_a", "b_a", "w_x", "b_x", "lru_lambda",
             "g_norm_conv", "g_norm_rnn", "w_out", "norm_mlp_g", "w_mlp_in", "w_mlp_out", "final_norm_g"]
    originals = dict(zip(names, (norm_mix_g, w_in, conv_w, rnn_conv_w, rnn_conv_b, w_a, b_a, w_x, b_x, lru_lambda,
                                 g_norm_conv, g_norm_rnn, w_out, norm_mlp_g, w_mlp_in, w_mlp_out, final_norm_g)))
    results = {"w_in": up_win, "w_out": up_wout, "w_mlp_in": up_w1, "w_mlp_out": up_w2}
    small_names = ["norm_mix_g", "conv_w", "rnn_conv_w", "rnn_conv_b", "w_a", "b_a", "w_x", "b_x", "lru_lambda",
                   "g_norm_conv", "g_norm_rnn", "norm_mlp_g", "final_norm_g"]
    for k, nm in enumerate(small_names):
        results[nm] = small_out[1 + 4 * k:5 + 4 * k]
    out = [loss, grad_x.reshape(x.shape)]
    for kind in range(4):
        out += [results[nm][kind].reshape(originals[nm].shape) for nm in names]
    return tuple(out)
```

```python
import functools

import jax
import jax.numpy as jnp
from jax import lax
from jax.experimental import pallas as pl
from jax.experimental.pallas import tpu as pltpu

F32 = jnp.float32
BF16 = jnp.bfloat16

D_MODEL = 1024
HEAD_DIM = 64
CONV_WIDTH = 512
LRU_WIDTH = 1024
MIX_WIDTH = CONV_WIDTH + LRU_WIDTH
IN_COLS = 3 * CONV_WIDTH + 2 * LRU_WIDTH
D_FF = 4 * D_MODEL
GROUP = 256
EPS = 1e-6
LRU_C = 8.0
N_DEV = 8
SUB = 8

OFF_GB, OFF_GC, OFF_V, OFF_XR, OFF_G = 0, 512, 1024, 1536, 2560

ADAM_LR, ADAM_B1, ADAM_B2, ADAM_EPS, ADAM_WD, ADAM_STEP = 0.001, 0.9, 0.999, 1e-08, 0.01, 10
BC1 = 1.0 - ADAM_B1 ** ADAM_STEP
BC2 = 1.0 - ADAM_B2 ** ADAM_STEP

MIB = 1024 * 1024
MESH = pl.DeviceIdType.MESH

VEC_ROWS = 32
ROW_GF, ROW_GMLP, ROW_LOSS = 0, 1, 2
ROW_GNC, ROW_GNR, ROW_BR, ROW_BA, ROW_BX, ROW_LAM, ROW_CW, ROW_RW = 8, 9, 10, 11, 12, 13, 14, 17
ROW_GMIX = 24
ACC_GNC, ACC_GNR, ACC_BR, ACC_BA, ACC_BX, ACC_SP, ACC_CW, ACC_RW, N_ACC = 0, 1, 2, 3, 4, 5, 6, 9, 13


def _params(semantics=None, vmem_mib=48):
    return pltpu.CompilerParams(dimension_semantics=semantics, vmem_limit_bytes=vmem_mib * MIB)


def _rms(x):
    return lax.rsqrt(jnp.mean(x * x, axis=-1, keepdims=True) + EPS)


def _rms_bwd(dy, xhat, r, g):
    dyh = dy * g
    return r * (dyh - xhat * jnp.mean(dyh * xhat, axis=-1, keepdims=True))


def _sigmoid(x):
    return 0.5 + 0.5 * jnp.tanh(0.5 * x)


def _gelu(x):
    c0, c1 = 0.7978845608028654, 0.044715
    x2 = x * x
    t = jnp.tanh(x * (c0 + (c0 * c1) * x2))
    half = 0.5 + 0.5 * t
    ge = x * half
    dge = half + (0.5 * x) * (1.0 - t * t) * (c0 + (3.0 * c0 * c1) * x2)
    return ge, dge


def _softplus_neg(lam):
    z = -lam
    e = jnp.exp(-jnp.abs(z))
    return jnp.maximum(z, 0.0) + jnp.where(e < 1e-4, e * (1.0 - 0.5 * e), jnp.log(1.0 + e))


def _lru_gates(pa, px, sp_c):
    ra = _sigmoid(pa)
    ii = _sigmoid(px)
    la = -ra * sp_c
    a = jnp.exp(la)
    x2 = 2.0 * la
    series = -x2 * (1.0 + x2 * (0.5 + x2 * (1.0 / 6.0 + x2 * (1.0 / 24.0))))
    m2 = jnp.where(x2 > -0.01, series, 1.0 - a * a)
    inv_mult = lax.rsqrt(m2)
    mult = jnp.where(m2 > 0.0, m2 * inv_mult, 0.0)
    return ra, ii, a, mult, inv_mult


def _down(cur, prev, s, row):
    return jnp.where(row >= s, pltpu.roll(cur, s, 0), pltpu.roll(prev, s, 0))


def _up(cur, nxt, s, row):
    return jnp.where(row < SUB - s, pltpu.roll(cur, SUB - s, 0), pltpu.roll(nxt, SUB - s, 0))


def _scan8_fwd(a, b, row):
    for s in (1, 2, 4):
        m = row >= s
        a_sh = pltpu.roll(a, s, 0)
        b_sh = pltpu.roll(b, s, 0)
        b = jnp.where(m, a * b_sh + b, b)
        a = jnp.where(m, a * a_sh, a)
    return a, b


def _scan8_rev(a, b, row):
    for s in (1, 2, 4):
        m = row < SUB - s
        a_sh = pltpu.roll(a, SUB - s, 0)
        b_sh = pltpu.roll(b, SUB - s, 0)
        b = jnp.where(m, a * b_sh + b, b)
        a = jnp.where(m, a * a_sh, a)
    return a, b


def _group_mask(shape):
    r = lax.broadcasted_iota(jnp.int32, shape, 0)
    c = lax.broadcasted_iota(jnp.int32, shape, 1)
    return ((r % GROUP) // HEAD_DIM) == (c // HEAD_DIM)


def _expand_heads(w):
    j = lax.broadcasted_iota(jnp.int32, (HEAD_DIM, GROUP), 0)
    c = lax.broadcasted_iota(jnp.int32, (HEAD_DIM, GROUP), 1)
    spread = (c % HEAD_DIM == j).astype(BF16)
    e = jnp.dot(w.astype(BF16), spread, preferred_element_type=F32)
    return jnp.where(_group_mask(e.shape), e, 0.0).astype(BF16)


def _fold_heads(p):
    p = jnp.where(_group_mask(p.shape), p, 0.0)
    c = lax.broadcasted_iota(jnp.int32, (GROUP, HEAD_DIM), 0)
    j = lax.broadcasted_iota(jnp.int32, (GROUP, HEAD_DIM), 1)
    fold = (c % HEAD_DIM == j).astype(BF16)
    hi = p.astype(BF16)
    rest = p - hi.astype(F32)
    mid = rest.astype(BF16)
    lo = (rest - mid.astype(F32)).astype(BF16)
    dot = functools.partial(jnp.dot, preferred_element_type=F32)
    return dot(hi, fold) + dot(mid, fold) + dot(lo, fold)


def _block_diag_apply(xb, wbd_ref):
    parts = [jnp.dot(xb[:, g * GROUP:(g + 1) * GROUP], wbd_ref[g * GROUP:(g + 1) * GROUP, :],
                     preferred_element_type=F32) for g in range(LRU_WIDTH // GROUP)]
    return jnp.concatenate(parts, axis=1)


def _block_diag_apply_t(db, wbd_ref):
    parts = [lax.dot_general(db[:, g * GROUP:(g + 1) * GROUP], wbd_ref[g * GROUP:(g + 1) * GROUP, :],
                             (((1,), (1,)), ((), ())), preferred_element_type=F32)
             for g in range(LRU_WIDTH // GROUP)]
    return jnp.concatenate(parts, axis=1)


def _dot_nt(a, b):
    return lax.dot_general(a, b, (((1,), (1,)), ((), ())), preferred_element_type=F32)


def _dot_tn(a, b):
    return lax.dot_general(a, b, (((0,), (0,)), ((), ())), preferred_element_type=F32)


def _chunk_loop(n_chunks, chunk, init, in_flight=4):
    def body(k, carry):
        for j in range(in_flight):
            carry = chunk(k * in_flight + j, carry)
        return carry

    return lax.fori_loop(0, n_chunks // in_flight, body, init)


def _place():
    x, y, c = lax.axis_index("x"), lax.axis_index("y"), lax.axis_index("c")
    return x, y, c


def _block_id(chip, core):
    return 4 * chip[0] + 2 * chip[1] + core


def _other_chips(x, y):
    return [(1 - x, y), (x, 1 - y), (1 - x, 1 - y)]


def _remote_copy(src, dst, send_sem, recv_sem, to):
    return pltpu.make_async_remote_copy(src_ref=src, dst_ref=dst, send_sem=send_sem, recv_sem=recv_sem,
                                        device_id=to, device_id_type=MESH)


HBM_SPEC = pl.BlockSpec(memory_space=pl.ANY)


def _in_hbm(*arrays):
    return [pltpu.with_memory_space_constraint(a, pltpu.HBM) for a in arrays]


def _prep_shards(w_in_t, w_out, w_mlp_in, w_mlp_out, conv_w, rnn_conv_w):
    def body(win_ref, wout_ref, w1_ref, w2_ref, cw_ref, rw_ref, o_win, o_wout, o_w1, o_w2, o_cp):
        o_win[...] = win_ref[...].astype(BF16)
        o_wout[...] = wout_ref[...].astype(BF16)
        o_w1[...] = w1_ref[...].astype(BF16)
        o_w2[...] = w2_ref[...].astype(BF16)
        o_cp[...] = jnp.zeros(o_cp.shape, F32)
        o_cp[0:3, 0:64] = cw_ref[...]
        o_cp[3:7, :] = rw_ref[...]

    whole = lambda shape: pl.BlockSpec(shape, lambda i: (0,) * len(shape))
    args = (w_in_t, w_out, w_mlp_in, w_mlp_out, conv_w, rnn_conv_w)
    shapes = [(w_in_t.shape, BF16), (w_out.shape, BF16), (w_mlp_in.shape, BF16), (w_mlp_out.shape, BF16),
              ((8, 128), F32)]
    return pl.pallas_call(
        body, grid=(1,), out_shape=[jax.ShapeDtypeStruct(s, d) for s, d in shapes],
        in_specs=[whole(a.shape) for a in args], out_specs=[whole(s) for s, _ in shapes],
        compiler_params=_params(("arbitrary",), 40), name="prep_shards",
    )(*args)


def _host_all_gather(step, n_steps, shards, fulls, send_sems, recv_sems, local_sems):
    x, y, c = _place()
    me = (x, y, c)
    my_id = _block_id((x, y), c)
    sibling = (x, y, 1 - c)
    chips = _other_chips(x, y)
    n_arr = len(shards)

    def copy(arr, k, block, to, src=None):
        dst = fulls[arr].at[block]
        return _remote_copy(dst if src is None else src, dst, send_sems.at[arr, k], recv_sems.at[arr, k], to)

    def local(arr):
        return pltpu.make_async_copy(shards[arr], fulls[arr].at[my_id], local_sems.at[arr])

    @pl.when(step == 0)
    def _():
        for arr in range(n_arr):
            local(arr).start()
            copy(arr, 0, my_id, sibling, shards[arr]).start()
            for j, chip in enumerate(chips):
                copy(arr, 1 + j, my_id, (*chip, c), shards[arr]).start()

    @pl.when(step == max(n_steps - 2, 0))
    def _():
        for j, chip in enumerate(chips):
            for arr in range(n_arr):
                copy(arr, 1 + j, _block_id(chip, c), me).wait_recv()
                copy(arr, 4 + j, _block_id(chip, c), sibling).start()

    @pl.when(step == n_steps - 1)
    def _():
        for arr in range(n_arr):
            copy(arr, 0, _block_id((x, y), 1 - c), me).wait_recv()
            for j, chip in enumerate(chips):
                copy(arr, 4 + j, _block_id(chip, 1 - c), me).wait_recv()
            for k in range(4):
                copy(arr, k, my_id, me, shards[arr]).wait_send()
            for j, chip in enumerate(chips):
                copy(arr, 4 + j, _block_id(chip, c), me).wait_send()
            local(arr).wait()


def _host_pair_exchange(step, n_steps, gs, sibs, send_sems, recv_sems):
    x, y, c = _place()
    sibling = (x, y, 1 - c)
    chips = [(x, y)] + _other_chips(x, y)

    def d2d(arr, q):
        return _remote_copy(gs[arr].at[_block_id(chips[q], 1 - c)], sibs[arr].at[q],
                            send_sems.at[arr, q], recv_sems.at[arr, q], sibling)

    @pl.when(step == 0)
    def _():
        for arr in range(len(gs)):
            for q in (1, 2, 3, 0):
                d2d(arr, q).start()

    @pl.when(step == n_steps - 1)
    def _():
        for arr in range(len(gs)):
            for q in range(4):
                d2d(arr, q).wait()


def _host_chip_exchange(step, n_steps, hsends, hrecvs, send_sems, recv_sems):
    x, y, c = _place()
    chips = _other_chips(x, y)

    def ici(arr, j):
        return _remote_copy(hsends[arr].at[j], hrecvs[arr].at[j], send_sems.at[arr, j], recv_sems.at[arr, j],
                            (*chips[j], c))

    @pl.when(step == 0)
    def _():
        for arr in range(len(hsends)):
            for j in range(3):
                ici(arr, j).start()

    @pl.when(step == n_steps - 1)
    def _():
        for arr in range(len(hsends)):
            for j in range(3):
                ici(arr, j).wait()


def _host_half_exchange(step, n_steps, parts, sibs, send_sems, recv_sems):
    x, y, c = _place()
    n_q, rows2, _ = parts.shape
    half = rows2 // 2

    def d2d(q):
        src = parts.at[q, pl.ds(pl.multiple_of((1 - c) * half, 16), half), :]
        return _remote_copy(src, sibs.at[q], send_sems.at[q], recv_sems.at[q], (x, y, 1 - c))

    @pl.when(step == 0)
    def _():
        for q in range(n_q):
            d2d(q).start()

    @pl.when(step == n_steps - 1)
    def _():
        for q in range(n_q):
            d2d(q).wait()


def _peer(x, y, c, k):
    return (x ^ ((k >> 2) & 1), y ^ ((k >> 1) & 1), c ^ (k & 1))


def _host_small_exchange(step, n_steps, vec_m, vec_b, wab, vrecv_m, vrecv_b, wrecv, send_sems, recv_sems, local_sems):
    x, y, c = _place()
    my_id = _block_id((x, y), c)
    wrows = wab.shape[0] // N_DEV

    def copies(k):
        to = _peer(x, y, c, k)
        block = wab.at[pl.ds(pl.multiple_of(_block_id(to[0:2], to[2]) * wrows, SUB), wrows), :]
        return [_remote_copy(vec_m, vrecv_m.at[my_id], send_sems.at[0, k], recv_sems.at[0, k], to),
                _remote_copy(vec_b, vrecv_b.at[my_id], send_sems.at[1, k], recv_sems.at[1, k], to),
                _remote_copy(block, wrecv.at[k], send_sems.at[2, k], recv_sems.at[2, k], to)]

    mine = [pltpu.make_async_copy(vec_m, vrecv_m.at[my_id], local_sems.at[0]),
            pltpu.make_async_copy(vec_b, vrecv_b.at[my_id], local_sems.at[1])]

    @pl.when(step == 0)
    def _():
        for cp in mine:
            cp.start()
        for k in range(1, N_DEV):
            for cp in copies(k):
                cp.start()

    @pl.when(step == n_steps - 1)
    def _():
        for k in range(1, N_DEV):
            for cp in copies(k):
                cp.wait()
        for cp in mine:
            cp.wait()


def _pair_sum_parts(parts, sibs, core):
    n_q, rows2, cols = parts.shape
    half = rows2 // 2

    def body(core_ref, g_ref, s_ref, o_ref):
        o_ref[0] = (g_ref[0, 0].astype(F32) + s_ref[0].astype(F32)).astype(BF16)

    block = (1, half, cols)
    grid_spec = pltpu.PrefetchScalarGridSpec(
        num_scalar_prefetch=1, grid=(n_q,),
        in_specs=[pl.BlockSpec((1, 1, half, cols), lambda q, cr: (q, cr[0], 0, 0)),
                  pl.BlockSpec(block, lambda q, cr: (q, 0, 0))],
        out_specs=pl.BlockSpec(block, lambda q, cr: (q, 0, 0)))
    return pl.pallas_call(
        body, grid_spec=grid_spec, out_shape=pltpu.HBM((n_q, half, cols), BF16),
        compiler_params=_params(("arbitrary",), 32), name="pair_sum_w_in",
    )(core, *_in_hbm(parts.reshape(n_q, 2, half, cols), sibs))


def _pair_sum(g, sib, name):
    _, rows, cols = g.shape
    x, y, c = _place()
    slots = jnp.stack([_block_id(chip, c) for chip in [(x, y)] + _other_chips(x, y)]).astype(jnp.int32)

    def body(slots_ref, g_ref, sib_ref, hs_ref, own_ref):
        q = pl.program_id(0)
        both = g_ref[0].astype(F32) + sib_ref[0].astype(F32)

        @pl.when(q == 0)
        def _():
            own_ref[...] = both

        @pl.when(q > 0)
        def _():
            hs_ref[0] = both.astype(BF16)

    block = (1, rows, cols)
    grid_spec = pltpu.PrefetchScalarGridSpec(
        num_scalar_prefetch=1, grid=(4,),
        in_specs=[pl.BlockSpec(block, lambda q, s: (s[q], 0, 0)), pl.BlockSpec(block, lambda q, s: (q, 0, 0))],
        out_specs=[pl.BlockSpec(block, lambda q, s: (jnp.maximum(q - 1, 0), 0, 0)),
                   pl.BlockSpec((rows, cols), lambda q, s: (0, 0))])
    return pl.pallas_call(
        body, grid_spec=grid_spec,
        out_shape=(pltpu.HBM((3, rows, cols), BF16), pltpu.HBM((rows, cols), F32)),
        compiler_params=_params(("arbitrary",), 32), name=name,
    )(slots, *_in_hbm(g, sib))


def _exchange_scratch(n_arr, n_copies):
    return [pltpu.SemaphoreType.DMA((n_arr, n_copies)), pltpu.SemaphoreType.DMA((n_arr, n_copies))]


def _host_small_finish(step, n_steps, vm_ref, vb_ref, wab_ref, wr_ref, vx_ref, o_vec, o_w, xrecv, wred,
                       x_send, x_recv, b_send, b_recv, local_sem):
    x, y, c = _place()
    my_id = _block_id((x, y), c)
    wrows = wr_ref.shape[1]
    my_rows = pl.ds(pl.multiple_of(my_id * wrows, SUB), wrows)

    def xcopy(k):
        return _remote_copy(xrecv.at[my_id], xrecv.at[my_id], x_send.at[k], x_recv.at[k], _peer(x, y, c, k))

    def bcopy(k):
        return _remote_copy(wred, o_w.at[my_rows, :], b_send.at[k], b_recv.at[k], _peer(x, y, c, k))

    mine = pltpu.make_async_copy(wred, o_w.at[my_rows, :], local_sem.at[0])

    @pl.when(step == 0)
    def _():
        xrecv[my_id] = vx_ref[...]
        for k in range(1, N_DEV):
            xcopy(k).start()
        red = wab_ref[my_rows, :]
        for k in range(1, N_DEV):
            red = red + wr_ref[k]
        wred[...] = red
        mine.start()
        for k in range(1, N_DEV):
            bcopy(k).start()

    @pl.when(step == n_steps - 1)
    def _():
        for k in range(1, N_DEV):
            xcopy(k).wait_recv()
        for rows, ref in ((slice(0, 8), vm_ref), (slice(8, 24), vb_ref), (slice(24, 32), xrecv)):
            tot = ref[0]
            for s in range(1, N_DEV):
                tot = tot + ref[s]
            o_vec[rows, :] = tot
        for k in range(1, N_DEV):
            bcopy(k).wait_recv()
        for k in range(1, N_DEV):
            xcopy(k).wait_send()
            bcopy(k).wait_send()
        mine.wait()


def _in_proj(x, g_mix, shards, tm):
    t_len = x.shape[0]
    n_t = t_len // tm
    n_arr = len(shards)
    rows = [s.shape[0] for s in shards]
    width = 2 * rows[0]
    ax, ay = lax.axis_index("x"), lax.axis_index("y")
    order = jnp.stack([2 * cx + cy for cx, cy in [(ax, ay)] + _other_chips(ax, ay)]).astype(jnp.int32)

    def body(order_ref, x_ref, g_ref, *rest):
        shard_refs = rest[0:n_arr]
        u_ref, h_ref = rest[n_arr:n_arr + 2]
        fulls = rest[n_arr + 2:2 * n_arr + 2]
        h_s, wbuf, send_sems, recv_sems, local_sems, load_sem = rest[2 * n_arr + 2:]
        p = pl.program_id(0)
        i = pl.program_id(1)
        x_, y_, c = _place()
        me = (x_, y_, c)
        my_id = _block_id((x_, y_), c)
        sibling = (x_, y_, 1 - c)
        chips = _other_chips(x_, y_)

        def block(arr, blk):
            return fulls[arr].at[pl.ds(pl.multiple_of(blk * rows[arr], rows[arr]), rows[arr]), :]

        def copy(arr, k, blk, to, src=None):
            dst = block(arr, blk)
            return _remote_copy(dst if src is None else src, dst, send_sems.at[arr, k], recv_sems.at[arr, k], to)

        def local(arr):
            return pltpu.make_async_copy(shard_refs[arr], block(arr, my_id), local_sems.at[arr])

        def load_chip(chip):
            start = pl.multiple_of((2 * chip[0] + chip[1]) * width, width)
            cp = pltpu.make_async_copy(fulls[0].at[pl.ds(start, width), :], wbuf, load_sem.at[0])
            cp.start()
            cp.wait()

        @pl.when((p == 0) & (i == 0))
        def _():
            for arr in range(n_arr):
                local(arr).start()
                copy(arr, 0, my_id, sibling, shard_refs[arr]).start()
                for j in (0, 1):
                    copy(arr, 1 + j, my_id, (*chips[j], c), shard_refs[arr]).start()
            for arr in range(n_arr):
                local(arr).wait()
                copy(arr, 0, _block_id((x_, y_), 1 - c), me).wait_recv()
            load_chip((x_, y_))

        for j, chip in enumerate(chips):
            @pl.when((p == j + 1) & (i == 0))
            def _(j=j, chip=chip):
                for arr in range(n_arr):
                    copy(arr, 1 + j, _block_id(chip, c), me).wait_recv()
                    copy(arr, 4 + j, _block_id(chip, c), sibling).start()
                    if j == 0:
                        copy(arr, 3, my_id, (*chips[2], c), shard_refs[arr]).start()
                for arr in range(n_arr):
                    copy(arr, 4 + j, _block_id(chip, 1 - c), me).wait_recv()
                load_chip(chip)

        @pl.when((p == 3) & (i == n_t - 1))
        def _():
            for arr in range(n_arr):
                for k in range(4):
                    copy(arr, k, my_id, me, shard_refs[arr]).wait_send()
                for j, chip in enumerate(chips):
                    copy(arr, 4 + j, _block_id(chip, c), me).wait_send()

        tile = pl.ds(pl.multiple_of(i * tm, tm), tm)

        @pl.when(p == 0)
        def _():
            xv = x_ref[...]
            h = (xv * _rms(xv) * g_ref[...]).astype(BF16)
            h_ref[...] = h
            h_s[tile, :] = h

        u_ref[...] = _dot_nt(h_s[tile, :], wbuf[...])

    first_pass = lambda p, i, o: (jnp.where(p == 0, i, n_t - 1), 0)
    grid_spec = pltpu.PrefetchScalarGridSpec(
        num_scalar_prefetch=1, grid=(4, n_t),
        in_specs=[pl.BlockSpec((tm, D_MODEL), first_pass), pl.BlockSpec((1, D_MODEL), lambda p, i, o: (0, 0))]
        + [HBM_SPEC] * n_arr,
        out_specs=[pl.BlockSpec((tm, width), lambda p, i, o: (i, o[p])), pl.BlockSpec((tm, D_MODEL), first_pass)]
        + [HBM_SPEC] * n_arr,
        scratch_shapes=[pltpu.VMEM((t_len, D_MODEL), BF16), pltpu.VMEM((width, D_MODEL), BF16)]
        + _exchange_scratch(n_arr, 7) + [pltpu.SemaphoreType.DMA((n_arr,)), pltpu.SemaphoreType.DMA((1,))])
    return pl.pallas_call(
        body, grid_spec=grid_spec,
        out_shape=[jax.ShapeDtypeStruct((t_len, IN_COLS), F32), jax.ShapeDtypeStruct((t_len, D_MODEL), BF16)]
        + [jax.ShapeDtypeStruct((N_DEV * s.shape[0], s.shape[1]), s.dtype) for s in shards],
        compiler_params=_params(("arbitrary", "arbitrary"), 48), name="in_proj",
    )(order, x, g_mix, *shards)


def _conv3_chunk(u_ref, r, cv_prev, cw, row):
    gb = u_ref[pl.ds(r, SUB), OFF_GB:OFF_GB + CONV_WIDTH]
    gc = u_ref[pl.ds(r, SUB), OFF_GC:OFF_GC + CONV_WIDTH]
    v = u_ref[pl.ds(r, SUB), OFF_V:OFF_V + CONV_WIDTH]
    cv = gc * v
    cv_m1 = _down(cv, cv_prev, 1, row)
    cv_m2 = _down(cv, cv_prev, 2, row)
    cq = cw[2:3, :] * cv + cw[1:2, :] * cv_m1 + cw[0:1, :] * cv_m2
    return gb, gc, v, cv, cv_m1, cv_m2, cq


def _conv4_chunk(u_ref, r, xin_prev, rw, rb, row):
    xin = u_ref[pl.ds(r, SUB), OFF_XR:OFF_XR + LRU_WIDTH]
    m1 = _down(xin, xin_prev, 1, row)
    m2 = _down(xin, xin_prev, 2, row)
    m3 = _down(xin, xin_prev, 3, row)
    xr = rw[3:4, :] * xin + rw[2:3, :] * m1 + rw[1:2, :] * m2 + rw[0:1, :] * m3 + rb
    return xin, m1, m2, m3, xr


def _mixer_fwd(u, conv_w, rnn_conv_w, rnn_conv_b, wa, b_a, wx, b_x, lam, gnc, gnr, shards, tm):
    t_len = u.shape[0]
    n_steps = t_len // tm
    n_chunks = tm // SUB
    n_arr = len(shards)

    def body(u_ref, cw_ref, rw_ref, rb_ref, wa_ref, ba_ref, wx_ref, bx_ref, lam_ref, gnc_ref, gnr_ref, *rest):
        shard_refs = rest[0:n_arr]
        hs_ref, y_ref, xr_s, ra_ref, ii_ref, mult_ref = rest[n_arr:n_arr + 6]
        fulls = rest[n_arr + 6:2 * n_arr + 6]
        (y_s, pa_s, px_s, wabd, wxbd, cv_car, xin_car, h_car,
         send_sems, recv_sems, local_sems) = rest[2 * n_arr + 6:]
        _host_all_gather(pl.program_id(0), n_steps, shard_refs, fulls, send_sems, recv_sems, local_sems)

        @pl.when(pl.program_id(0) == 0)
        def _():
            cv_car[...] = jnp.zeros(cv_car.shape, F32)
            xin_car[...] = jnp.zeros(xin_car.shape, F32)
            h_car[...] = jnp.zeros(h_car.shape, F32)
            wabd[...] = _expand_heads(wa_ref[...])
            wxbd[...] = _expand_heads(wx_ref[...])

        row_c = lax.broadcasted_iota(jnp.int32, (SUB, CONV_WIDTH), 0)
        row_r = lax.broadcasted_iota(jnp.int32, (SUB, LRU_WIDTH), 0)
        cw = cw_ref[...]
        rw = rw_ref[...]
        rb = rb_ref[...]
        g_c = gnc_ref[...]
        g_r = gnr_ref[...]
        sp_c = LRU_C * _softplus_neg(lam_ref[...])

        def convs(i, carry):
            cv_prev, xin_prev = carry
            r = pl.multiple_of(i * SUB, SUB)
            gb, _, _, cv, _, _, cq = _conv3_chunk(u_ref, r, cv_prev, cw, row_c)
            y_c = gb * cq
            y_s[pl.ds(r, SUB), 0:CONV_WIDTH] = y_c * _rms(y_c) * g_c
            xin, _, _, _, xr = _conv4_chunk(u_ref, r, xin_prev, rw, rb, row_r)
            xr_s[pl.ds(r, SUB), :] = xr
            return cv, xin

        cv_last, xin_last = _chunk_loop(n_chunks, convs, (cv_car[...], xin_car[...]), in_flight=8)
        cv_car[...] = cv_last
        xin_car[...] = xin_last

        xrb = xr_s[...].astype(BF16)
        pa_s[...] = _block_diag_apply(xrb, wabd) + ba_ref[...]
        px_s[...] = _block_diag_apply(xrb, wxbd) + bx_ref[...]

        def recur(i, h_prev):
            r = pl.multiple_of(i * SUB, SUB)
            xr = xr_s[pl.ds(r, SUB), :]
            ra, ii, a, mult, _ = _lru_gates(pa_s[pl.ds(r, SUB), :], px_s[pl.ds(r, SUB), :], sp_c)
            ra_ref[pl.ds(r, SUB), :] = ra
            ii_ref[pl.ds(r, SUB), :] = ii
            mult_ref[pl.ds(r, SUB), :] = mult
            a_cum, b_cum = _scan8_fwd(a, mult * ii * xr, row_r)
            h = a_cum * h_prev + b_cum
            hs_ref[pl.ds(r, SUB), :] = h
            ge, _ = _gelu(u_ref[pl.ds(r, SUB), OFF_G:OFF_G + LRU_WIDTH])
            y_r = h * ge
            y_s[pl.ds(r, SUB), CONV_WIDTH:MIX_WIDTH] = y_r * _rms(y_r) * g_r
            return h[SUB - 1:SUB, :]

        h_car[...] = _chunk_loop(n_chunks, recur, h_car[...], in_flight=8)

        y_ref[...] = y_s[...].astype(BF16)

    row_tile = lambda w: pl.BlockSpec((tm, w), lambda i: (i, 0))
    whole = lambda a: pl.BlockSpec(a.shape, lambda i: (0,) * a.ndim)
    smalls = (conv_w, rnn_conv_w, rnn_conv_b, wa, b_a, wx, b_x, lam, gnc, gnr)
    return pl.pallas_call(
        body, grid=(n_steps,),
        in_specs=[row_tile(IN_COLS)] + [whole(a) for a in smalls] + [HBM_SPEC] * n_arr,
        out_specs=[row_tile(LRU_WIDTH), row_tile(MIX_WIDTH)] + [row_tile(LRU_WIDTH)] * 4 + [HBM_SPEC] * n_arr,
        out_shape=[jax.ShapeDtypeStruct((t_len, LRU_WIDTH), F32), jax.ShapeDtypeStruct((t_len, MIX_WIDTH), BF16)]
        + [jax.ShapeDtypeStruct((t_len, LRU_WIDTH), F32)] * 4
        + [jax.ShapeDtypeStruct((N_DEV,) + s.shape, BF16) for s in shards],
        scratch_shapes=[pltpu.VMEM((tm, MIX_WIDTH), F32),
                        pltpu.VMEM((tm, LRU_WIDTH), F32), pltpu.VMEM((tm, LRU_WIDTH), F32),
                        pltpu.VMEM((LRU_WIDTH, GROUP), BF16), pltpu.VMEM((LRU_WIDTH, GROUP), BF16),
                        pltpu.VMEM((SUB, CONV_WIDTH), F32), pltpu.VMEM((SUB, LRU_WIDTH), F32),
                        pltpu.VMEM((1, LRU_WIDTH), F32)]
        + _exchange_scratch(n_arr, 7) + [pltpu.SemaphoreType.DMA((n_arr,))],
        compiler_params=_params(("arbitrary",), 56), name="mixer_fwd",
    )(u, *smalls, *shards)


def _mlp_up(x, y, g_mlp, w_out, w1, w2_shard, tm):
    t_len = x.shape[0]
    n_steps = t_len // tm
    n_blk, _, blk = w1.shape

    def body(x_ref, y_ref, gm_ref, wout_hbm, w1_hbm, w2_ref, x1_ref, h2_ref, z_ref, w2_full,
             wout_s, w1_s, sem, send_sems, recv_sems, local_sems):
        step = pl.program_id(0)
        _host_all_gather(step, n_steps, [w2_ref], [w2_full], send_sems, recv_sems, local_sems)

        load_wout = pltpu.make_async_copy(wout_hbm, wout_s, sem.at[0])
        load_w1 = pltpu.make_async_copy(w1_hbm, w1_s, sem.at[1])

        @pl.when(step == 0)
        def _():
            load_wout.start()
            load_w1.start()
            load_wout.wait()

        x1v = x_ref[...] + jnp.dot(y_ref[...], wout_s[...], preferred_element_type=F32)
        x1_ref[...] = x1v
        h2 = (x1v * _rms(x1v) * gm_ref[...]).astype(BF16)
        h2_ref[...] = h2

        @pl.when(step == 0)
        def _():
            load_w1.wait()

        for k in range(n_blk):
            rp = jnp.maximum(jnp.dot(h2, w1_s[k], preferred_element_type=F32), 0.0)
            z_ref[:, k * blk:(k + 1) * blk] = (rp * rp).astype(BF16)

    row_tile = lambda w: pl.BlockSpec((tm, w), lambda i: (i, 0))
    return pl.pallas_call(
        body, grid=(n_steps,),
        in_specs=[row_tile(D_MODEL), row_tile(MIX_WIDTH), pl.BlockSpec((1, D_MODEL), lambda i: (0, 0)),
                  HBM_SPEC, HBM_SPEC, HBM_SPEC],
        out_specs=[row_tile(D_MODEL), row_tile(D_MODEL), row_tile(D_FF), HBM_SPEC],
        out_shape=[jax.ShapeDtypeStruct((t_len, D_MODEL), F32), jax.ShapeDtypeStruct((t_len, D_MODEL), BF16),
                   jax.ShapeDtypeStruct((t_len, D_FF), BF16), jax.ShapeDtypeStruct((N_DEV,) + w2_shard.shape, BF16)],
        scratch_shapes=[pltpu.VMEM(w_out.shape, BF16), pltpu.VMEM(w1.shape, BF16), pltpu.SemaphoreType.DMA((2,))]
        + _exchange_scratch(1, 7) + [pltpu.SemaphoreType.DMA((1,))],
        compiler_params=_params(("arbitrary",), 48), name="mlp_up",
    )(x, y, g_mlp, w_out, w1, w2_shard)


def _mlp_down_bwd(x1, z, target, g_mlp, g_f, w1, w2, tm):
    t_len = x1.shape[0]
    n_steps = t_len // tm
    n_blk, _, blk = w1.shape

    def body(x1_ref, z_ref, tg_ref, gm_ref, gf_ref, w1_hbm, w2_hbm, dx1_ref, dx2_ref, vec_ref, dpre_hbm,
             w1_s, w2_s, dp_s, sem, out_sem):
        step = pl.program_id(0)
        rows = pl.ds(pl.multiple_of(step * tm, tm), tm)
        dp_out = pltpu.make_async_copy(dp_s, dpre_hbm.at[rows, :], out_sem.at[0])

        load_w1 = pltpu.make_async_copy(w1_hbm, w1_s, sem.at[0])
        load_w2 = pltpu.make_async_copy(w2_hbm, w2_s, sem.at[1])

        @pl.when(step == 0)
        def _():
            load_w2.start()
            load_w1.start()
            vec_ref[...] = jnp.zeros(vec_ref.shape, F32)
            load_w2.wait()

        x1v = x1_ref[...]
        g_m = gm_ref[...]
        g_o = gf_ref[...]
        r2 = _rms(x1v)
        x1h = x1v * r2
        x2 = x1v + jnp.dot(z_ref[...], w2_s[...], preferred_element_type=F32)
        r3 = _rms(x2)
        x2h = x2 * r3
        err = x2h * g_o - tg_ref[...]
        dout = err * (1.0 / D_MODEL)
        vec_ref[ROW_LOSS:ROW_LOSS + 1, :] += (0.5 / D_MODEL) * jnp.sum(err * err, axis=0, keepdims=True)
        vec_ref[ROW_GF:ROW_GF + 1, :] += jnp.sum(dout * x2h, axis=0, keepdims=True)
        dx2 = _rms_bwd(dout, x2h, r3, g_o)
        dx2b = dx2.astype(BF16)
        dx2_ref[...] = dx2b
        dh2 = jnp.zeros((tm, D_MODEL), F32)

        @pl.when(step > 0)
        def _():
            dp_out.wait()

        @pl.when(step == 0)
        def _():
            load_w1.wait()

        for k in range(n_blk):
            cols = slice(k * blk, (k + 1) * blk)
            dz = _dot_nt(dx2b, w2_s[cols, :])
            dpb = (dz * 2.0 * jnp.sqrt(z_ref[:, cols].astype(F32))).astype(BF16)
            dp_s[:, cols] = dpb
            dh2 = dh2 + _dot_nt(dpb, w1_s[k])
        dp_out.start()
        vec_ref[ROW_GMLP:ROW_GMLP + 1, :] += jnp.sum(dh2 * x1h, axis=0, keepdims=True)
        dx1_ref[...] = dx2 + _rms_bwd(dh2, x1h, r2, g_m)

        @pl.when(step == n_steps - 1)
        def _():
            dp_out.wait()

    row_tile = lambda w: pl.BlockSpec((tm, w), lambda i: (i, 0))
    vec_spec = pl.BlockSpec((1, D_MODEL), lambda i: (0, 0))
    return pl.pallas_call(
        body, grid=(n_steps,),
        in_specs=[row_tile(D_MODEL), row_tile(D_FF), row_tile(D_MODEL), vec_spec, vec_spec, HBM_SPEC, HBM_SPEC],
        out_specs=[row_tile(D_MODEL), row_tile(D_MODEL), pl.BlockSpec((SUB, D_MODEL), lambda i: (0, 0)), HBM_SPEC],
        out_shape=[jax.ShapeDtypeStruct((t_len, D_MODEL), F32), jax.ShapeDtypeStruct((t_len, D_MODEL), BF16),
                   jax.ShapeDtypeStruct((SUB, D_MODEL), F32), jax.ShapeDtypeStruct((t_len, D_FF), BF16)],
        scratch_shapes=[pltpu.VMEM(w1.shape, BF16), pltpu.VMEM(w2.shape, BF16), pltpu.VMEM((tm, D_FF), BF16),
                        pltpu.SemaphoreType.DMA((2,)), pltpu.SemaphoreType.DMA((1,))],
        compiler_params=_params(("arbitrary",), 56), name="mlp_down_bwd",
    )(x1, z, target, g_mlp, g_f, w1, w2)


def _mixer_bwd(u, hs, dx1, saved, conv_w, rnn_conv_w, rnn_conv_b, wa, b_a, wx, b_x, lam, gnc, gnr, w_out,
               chip_sums, g_wout, tm):
    t_len = u.shape[0]
    n_tiles = t_len // tm
    n_chunks = tm // SUB
    per_tile = tm // SUB
    n_sums = len(chip_sums)

    def body(u_ref, up_ref, hs_ref, hp_ref, dx1_ref, xr_ref, ra_ref, ii_ref, mult_ref,
             cw_ref, rw_ref, rb_ref, wa_ref, ba_ref, wx_ref, bx_ref, lam_ref, gnc_ref, gnr_ref, wout_ref, *rest):
        hsends = rest[0:n_sums]
        gwout_ref = rest[n_sums]
        du_ref, vec_ref, wab_ref = rest[n_sums + 1:n_sums + 4]
        hrecvs = rest[n_sums + 4:2 * n_sums + 4]
        sib_wout = rest[2 * n_sums + 4]
        (du_s, dy_s, dpa_s, dpx_s, dxr_s, wabd, wxbd, acc, dwa_acc, dwx_acc,
         a_car, dh_car, dcq_car, dxr_car, i_send, i_recv, d_send, d_recv) = rest[2 * n_sums + 5:]
        step = pl.program_id(0)
        _host_chip_exchange(step, n_tiles, hsends, hrecvs, i_send, i_recv)
        _host_pair_exchange(step, n_tiles, [gwout_ref], [sib_wout], d_send, d_recv)
        has_prev = (step < n_tiles - 1).astype(F32)

        @pl.when(step == 0)
        def _():
            acc[...] = jnp.zeros(acc.shape, F32)
            dwa_acc[...] = jnp.zeros(dwa_acc.shape, F32)
            dwx_acc[...] = jnp.zeros(dwx_acc.shape, F32)
            a_car[...] = jnp.ones(a_car.shape, F32)
            dh_car[...] = jnp.zeros(dh_car.shape, F32)
            dcq_car[...] = jnp.zeros(dcq_car.shape, F32)
            dxr_car[...] = jnp.zeros(dxr_car.shape, F32)
            wabd[...] = _expand_heads(wa_ref[...])
            wxbd[...] = _expand_heads(wx_ref[...])

        row_c = lax.broadcasted_iota(jnp.int32, (SUB, CONV_WIDTH), 0)
        row_r = lax.broadcasted_iota(jnp.int32, (SUB, LRU_WIDTH), 0)
        cw = cw_ref[...]
        rw = rw_ref[...]
        rb = rb_ref[...]
        g_c = gnc_ref[...]
        g_r = gnr_ref[...]
        sp_c = LRU_C * _softplus_neg(lam_ref[...])

        up = up_ref[...] * has_prev
        cv_before = up[:, OFF_GC:OFF_GC + CONV_WIDTH] * up[:, OFF_V:OFF_V + CONV_WIDTH]
        xin_before = up[:, OFF_XR:OFF_XR + LRU_WIDTH]
        hs_before = hp_ref[...] * has_prev

        dy_s[...] = _dot_nt(dx1_ref[...].astype(BF16), wout_ref[...])

        xrb = xr_ref[...].astype(BF16)

        def recur_bwd(j, carry):
            a_later, dh_later = carry
            i = n_chunks - 1 - j
            r = pl.multiple_of(i * SUB, SUB)
            rp = pl.multiple_of(jnp.maximum(i - 1, 0) * SUB, SUB)
            xr = xr_ref[pl.ds(r, SUB), :]
            hs_c = hs_ref[pl.ds(r, SUB), :]
            hs_prev = jnp.where(i == 0, hs_before, hs_ref[pl.ds(rp, SUB), :])
            h_m1 = _down(hs_c, hs_prev, 1, row_r)
            ra = ra_ref[pl.ds(r, SUB), :]
            ii = ii_ref[pl.ds(r, SUB), :]
            mult = mult_ref[pl.ds(r, SUB), :]
            a = jnp.exp(-ra * sp_c)
            inv_mult = lax.rsqrt(mult * mult)
            ge, dge = _gelu(u_ref[pl.ds(r, SUB), OFF_G:OFF_G + LRU_WIDTH])
            y_r = hs_c * ge
            rr = _rms(y_r)
            yhat = y_r * rr
            dyn = dy_s[pl.ds(r, SUB), CONV_WIDTH:MIX_WIDTH]
            acc[ACC_GNR] += dyn * yhat
            dy_r = _rms_bwd(dyn, yhat, rr, g_r)
            du_s[pl.ds(r, SUB), OFF_G:OFF_G + LRU_WIDTH] = dy_r * hs_c * dge
            a_cum, d_cum = _scan8_rev(_up(a, a_later, 1, row_r), dy_r * ge, row_r)
            dh = a_cum * dh_later + d_cum
            dmult = dh * ii * xr
            dii = dh * mult * xr
            dxr_s[pl.ds(r, SUB), :] = dh * mult * ii
            dla = dh * h_m1 * a - dmult * a * a * inv_mult
            acc[ACC_SP] += -dla * ra
            dpa = -dla * sp_c * ra * (1.0 - ra)
            dpx = dii * ii * (1.0 - ii)
            acc[ACC_BA] += dpa
            acc[ACC_BX] += dpx
            dpa_s[pl.ds(r, SUB), :] = dpa
            dpx_s[pl.ds(r, SUB), :] = dpx
            return a, dh[0:1, :]

        a_first, dh_first = _chunk_loop(n_chunks, recur_bwd, (a_car[...], dh_car[...]), in_flight=8)
        a_car[...] = a_first
        dh_car[...] = dh_first

        dpab = dpa_s[...].astype(BF16)
        dpxb = dpx_s[...].astype(BF16)
        dxr_s[...] += _block_diag_apply_t(dpab, wabd) + _block_diag_apply_t(dpxb, wxbd)
        for g in range(LRU_WIDTH // GROUP):
            cols = slice(g * GROUP, (g + 1) * GROUP)
            dwa_acc[cols, :] += _dot_tn(xrb[:, cols], dpab[:, cols])
            dwx_acc[cols, :] += _dot_tn(xrb[:, cols], dpxb[:, cols])

        def convs_bwd(j, carry):
            dcq_later, dxr_later = carry
            i = n_chunks - 1 - j
            r = pl.multiple_of(i * SUB, SUB)
            rp = pl.multiple_of(jnp.maximum(i - 1, 0) * SUB, SUB)
            cv_prev = jnp.where(i == 0, cv_before,
                                u_ref[pl.ds(rp, SUB), OFF_GC:OFF_GC + CONV_WIDTH]
                                * u_ref[pl.ds(rp, SUB), OFF_V:OFF_V + CONV_WIDTH])
            gb, gc, v, cv, cv_m1, cv_m2, cq = _conv3_chunk(u_ref, r, cv_prev, cw, row_c)
            y_c = gb * cq
            rc = _rms(y_c)
            yhat = y_c * rc
            dyn = dy_s[pl.ds(r, SUB), 0:CONV_WIDTH]
            acc[ACC_GNC, :, 0:CONV_WIDTH] += dyn * yhat
            dy_c = _rms_bwd(dyn, yhat, rc, g_c)
            dcq = dy_c * gb
            dcv = (cw[2:3, :] * dcq + cw[1:2, :] * _up(dcq, dcq_later, 1, row_c)
                   + cw[0:1, :] * _up(dcq, dcq_later, 2, row_c))
            acc[ACC_CW + 2, :, 0:CONV_WIDTH] += dcq * cv
            acc[ACC_CW + 1, :, 0:CONV_WIDTH] += dcq * cv_m1
            acc[ACC_CW + 0, :, 0:CONV_WIDTH] += dcq * cv_m2
            du_s[pl.ds(r, SUB), OFF_GB:OFF_GB + CONV_WIDTH] = dy_c * cq
            du_s[pl.ds(r, SUB), OFF_GC:OFF_GC + CONV_WIDTH] = dcv * v
            du_s[pl.ds(r, SUB), OFF_V:OFF_V + CONV_WIDTH] = dcv * gc

            xin_prev = jnp.where(i == 0, xin_before, u_ref[pl.ds(rp, SUB), OFF_XR:OFF_XR + LRU_WIDTH])
            xin, m1, m2, m3, _ = _conv4_chunk(u_ref, r, xin_prev, rw, rb, row_r)
            dxr = dxr_s[pl.ds(r, SUB), :]
            du_s[pl.ds(r, SUB), OFF_XR:OFF_XR + LRU_WIDTH] = (
                rw[3:4, :] * dxr + rw[2:3, :] * _up(dxr, dxr_later, 1, row_r)
                + rw[1:2, :] * _up(dxr, dxr_later, 2, row_r) + rw[0:1, :] * _up(dxr, dxr_later, 3, row_r))
            acc[ACC_RW + 3] += dxr * xin
            acc[ACC_RW + 2] += dxr * m1
            acc[ACC_RW + 1] += dxr * m2
            acc[ACC_RW + 0] += dxr * m3
            acc[ACC_BR] += dxr
            return dcq, dxr

        dcq_first, dxr_first = _chunk_loop(n_chunks, convs_bwd, (dcq_car[...], dxr_car[...]), in_flight=8)
        dcq_car[...] = dcq_first
        dxr_car[...] = dxr_first

        du_ref[...] = du_s[...].astype(BF16)

        @pl.when(step == n_tiles - 1)
        def _():
            vec_ref[...] = jnp.zeros(vec_ref.shape, F32)
            rows = {ACC_GNC: ROW_GNC, ACC_GNR: ROW_GNR, ACC_BR: ROW_BR, ACC_BA: ROW_BA, ACC_BX: ROW_BX}
            for k in range(3):
                rows[ACC_CW + k] = ROW_CW + k
            for k in range(4):
                rows[ACC_RW + k] = ROW_RW + k
            for slot, out_row in rows.items():
                o = out_row - ROW_GNC
                vec_ref[o:o + 1, :] = jnp.sum(acc[slot], axis=0, keepdims=True)
            lam_v = lam_ref[...]
            dsp = jnp.sum(acc[ACC_SP], axis=0, keepdims=True)
            o = ROW_LAM - ROW_GNC
            vec_ref[o:o + 1, :] = -dsp * LRU_C / (1.0 + jnp.exp(lam_v))
            wab_ref[0:LRU_WIDTH, :] = _fold_heads(dwa_acc[...])
            wab_ref[LRU_WIDTH:2 * LRU_WIDTH, :] = _fold_heads(dwx_acc[...])

    rev = lambda w: pl.BlockSpec((tm, w), lambda s: (n_tiles - 1 - s, 0))
    before = lambda w: pl.BlockSpec((SUB, w), lambda s: (jnp.maximum((n_tiles - 1 - s) * per_tile - 1, 0), 0))
    whole = lambda a: pl.BlockSpec(a.shape, lambda s: (0,) * a.ndim)
    smalls = (conv_w, rnn_conv_w, rnn_conv_b, wa, b_a, wx, b_x, lam, gnc, gnr, w_out)
    full = lambda w: pltpu.VMEM((tm, w), F32)
    return pl.pallas_call(
        body, grid=(n_tiles,),
        in_specs=[rev(IN_COLS), before(IN_COLS), rev(LRU_WIDTH), before(LRU_WIDTH), rev(D_MODEL)]
        + [rev(LRU_WIDTH)] * len(saved) + [whole(a) for a in smalls] + [HBM_SPEC] * (n_sums + 1),
        out_specs=[rev(IN_COLS), pl.BlockSpec((16, D_MODEL), lambda s: (0, 0)),
                   pl.BlockSpec((2 * LRU_WIDTH, HEAD_DIM), lambda s: (0, 0))] + [HBM_SPEC] * (n_sums + 1),
        out_shape=[jax.ShapeDtypeStruct((t_len, IN_COLS), BF16), jax.ShapeDtypeStruct((16, D_MODEL), F32),
                   jax.ShapeDtypeStruct((2 * LRU_WIDTH, HEAD_DIM), F32)]
        + [jax.ShapeDtypeStruct(s.shape, BF16) for s in chip_sums]
        + [jax.ShapeDtypeStruct((4,) + g_wout.shape[1:], BF16)],
        scratch_shapes=[full(IN_COLS), full(MIX_WIDTH), full(LRU_WIDTH), full(LRU_WIDTH), full(LRU_WIDTH),
                        pltpu.VMEM((LRU_WIDTH, GROUP), BF16), pltpu.VMEM((LRU_WIDTH, GROUP), BF16),
                        pltpu.VMEM((N_ACC, SUB, LRU_WIDTH), F32),
                        pltpu.VMEM((LRU_WIDTH, GROUP), F32), pltpu.VMEM((LRU_WIDTH, GROUP), F32),
                        pltpu.VMEM((SUB, LRU_WIDTH), F32), pltpu.VMEM((1, LRU_WIDTH), F32),
                        pltpu.VMEM((SUB, CONV_WIDTH), F32), pltpu.VMEM((SUB, LRU_WIDTH), F32)]
        + _exchange_scratch(n_sums, 3) + _exchange_scratch(1, 4),
        compiler_params=_params(("arbitrary",), 56), name="mixer_bwd",
    )(u, u, hs, hs, dx1, *saved, *smalls, *chip_sums, g_wout)


def _in_proj_bwd(du, dx1, x, g_mix, win_t, tm, chip_sums, g_own):
    t_len = x.shape[0]
    n_steps = t_len // tm

    def body(du_ref, dx1_ref, x_ref, g_ref, w_ref, hs_ref, gown_ref,
             dx_ref, vec_ref, landed_ref, sib_ref, i_send, i_recv, d_send, d_recv):
        step = pl.program_id(0)
        _host_chip_exchange(step, n_steps, [hs_ref], [landed_ref], i_send, i_recv)
        _host_half_exchange(step, n_steps, gown_ref, sib_ref, d_send, d_recv)

        @pl.when(step == 0)
        def _():
            vec_ref[...] = jnp.zeros(vec_ref.shape, F32)

        dh = jnp.dot(du_ref[...], w_ref[...], preferred_element_type=F32)
        xv = x_ref[...]
        r1 = _rms(xv)
        xh = xv * r1
        vec_ref[0:1, :] += jnp.sum(dh * xh, axis=0, keepdims=True)
        dx_ref[...] = dx1_ref[...] + _rms_bwd(dh, xh, r1, g_ref[...])

    row_tile = lambda w: pl.BlockSpec((tm, w), lambda i: (i, 0))
    half_shape = (g_own.shape[0], g_own.shape[1] // 2, g_own.shape[2])
    return pl.pallas_call(
        body, grid=(n_steps,),
        in_specs=[row_tile(IN_COLS), row_tile(D_MODEL), row_tile(D_MODEL), pl.BlockSpec((1, D_MODEL), lambda i: (0, 0)),
                  pl.BlockSpec((IN_COLS, D_MODEL), lambda i: (0, 0))] + [HBM_SPEC] * 2,
        out_specs=[row_tile(D_MODEL), pl.BlockSpec((SUB, D_MODEL), lambda i: (0, 0))] + [HBM_SPEC] * 2,
        out_shape=[jax.ShapeDtypeStruct((t_len, D_MODEL), F32), jax.ShapeDtypeStruct((SUB, D_MODEL), F32),
                   jax.ShapeDtypeStruct(chip_sums.shape, BF16), jax.ShapeDtypeStruct(half_shape, BF16)],
        scratch_shapes=_exchange_scratch(1, 3) + [pltpu.SemaphoreType.DMA((1,)), pltpu.SemaphoreType.DMA((1,))],
        compiler_params=_params(("arbitrary",), 56), name="in_proj_bwd",
    )(du, dx1, x, g_mix, win_t, chip_sums, g_own)


def _tn_weight_grad(a, b, tk, name, pair=(), col_blocks=1):
    t_len, m = a.shape
    n = b.shape[1]
    n_steps = t_len // tk
    sent = tuple(pair)
    n_sent = len(sent)

    def body(a_ref, b_ref, *rest):
        srcs = rest[0:n_sent]
        o_ref = rest[n_sent]
        dsts = rest[n_sent + 1:2 * n_sent + 1]
        acc = rest[2 * n_sent + 1]
        sems = rest[2 * n_sent + 2:]
        j = pl.program_id(0)
        if pair:
            _host_pair_exchange(j, n_steps, srcs, dsts, *sems)

        @pl.when(j == 0)
        def _():
            acc[...] = jnp.zeros(acc.shape, F32)

        acc[...] += _dot_tn(a_ref[...].astype(BF16), b_ref[...].astype(BF16))

        @pl.when(j == n_steps - 1)
        def _():
            if col_blocks == 1:
                o_ref[...] = acc[...].astype(BF16)
            else:
                for k in range(col_blocks):
                    o_ref[k] = acc[:, k * nb:(k + 1) * nb].astype(BF16)

    nb = n // col_blocks
    out_dims = (m, n) if col_blocks == 1 else (col_blocks, m, nb)
    landed = [jax.ShapeDtypeStruct((4,) + g.shape[1:], BF16) for g in pair]
    scratch = [pltpu.VMEM((m, n), F32)]
    if n_sent:
        scratch += _exchange_scratch(n_sent, 4)
    return pl.pallas_call(
        body, grid=(n_steps,),
        in_specs=[pl.BlockSpec((tk, m), lambda j: (j, 0)), pl.BlockSpec((tk, n), lambda j: (j, 0))]
        + [HBM_SPEC] * n_sent,
        out_specs=[pl.BlockSpec(out_dims, lambda j: (0,) * len(out_dims))] + [HBM_SPEC] * n_sent,
        out_shape=[jax.ShapeDtypeStruct(out_dims, BF16)] + landed,
        scratch_shapes=scratch,
        compiler_params=_params(("arbitrary",), 56), name=name,
    )(a, b, *sent)


def _w_in_grad_part(du, h, tk, name, chip_ids, chip=(), halves=None, small=None):
    t_len = du.shape[0]
    n_t = t_len // tk
    n_q = chip_ids.shape[0]
    width = 2 * (IN_COLS // N_DEV)
    n_steps = n_q * n_t
    n_chip = len(chip)
    sent = tuple(chip) + (() if halves is None else (halves,)) + (() if small is None else tuple(small))
    n_sent = len(sent)

    def body(ids_ref, a_ref, b_ref, *rest):
        srcs = rest[0:n_sent]
        o_ref = rest[n_sent]
        dsts = rest[n_sent + 1:2 * n_sent + 1]
        acc = rest[2 * n_sent + 1]
        sems = list(rest[2 * n_sent + 2:])
        j = pl.program_id(1)
        step = pl.program_id(0) * n_t + j
        if chip:
            _host_chip_exchange(step, n_steps, srcs[0:n_chip], dsts[0:n_chip], sems.pop(0), sems.pop(0))
        if halves is not None:
            _host_half_exchange(step, n_steps, srcs[n_chip], dsts[n_chip], sems.pop(0), sems.pop(0))
        if small is not None:
            _host_small_exchange(step, n_steps, *srcs[n_sent - 3:], *dsts[n_sent - 3:], *sems)

        @pl.when(j == 0)
        def _():
            acc[...] = jnp.zeros(acc.shape, F32)

        acc[...] += _dot_tn(a_ref[...], b_ref[...])

        @pl.when(j == n_t - 1)
        def _():
            o_ref[0] = acc[...].astype(BF16)

    landed = [jax.ShapeDtypeStruct(s.shape, BF16) for s in chip]
    scratch = [pltpu.VMEM((width, D_MODEL), F32)]
    if chip:
        scratch += _exchange_scratch(len(chip), 3)
    if halves is not None:
        landed.append(jax.ShapeDtypeStruct((halves.shape[0], halves.shape[1] // 2, halves.shape[2]), BF16))
        scratch += [pltpu.SemaphoreType.DMA((halves.shape[0],)), pltpu.SemaphoreType.DMA((halves.shape[0],))]
    if small is not None:
        vec_m, vec_b, wab = small
        landed += [jax.ShapeDtypeStruct((N_DEV,) + vec_m.shape, F32), jax.ShapeDtypeStruct((N_DEV,) + vec_b.shape, F32),
                   jax.ShapeDtypeStruct((N_DEV, wab.shape[0] // N_DEV, wab.shape[1]), F32)]
        scratch += _exchange_scratch(3, N_DEV) + [pltpu.SemaphoreType.DMA((2,))]
    grid_spec = pltpu.PrefetchScalarGridSpec(
        num_scalar_prefetch=1, grid=(n_q, n_t),
        in_specs=[pl.BlockSpec((tk, width), lambda q, j, ids: (j, ids[q])),
                  pl.BlockSpec((tk, D_MODEL), lambda q, j, ids: (j, 0))] + [HBM_SPEC] * n_sent,
        out_specs=[pl.BlockSpec((1, width, D_MODEL), lambda q, j, ids: (q, 0, 0))] + [HBM_SPEC] * n_sent,
        scratch_shapes=scratch)
    return pl.pallas_call(
        body, grid_spec=grid_spec, out_shape=[jax.ShapeDtypeStruct((n_q, width, D_MODEL), BF16)] + landed,
        compiler_params=_params(("arbitrary", "arbitrary"), 40), name=name,
    )(chip_ids, du, h, *sent)


def _adamw(w, g, m, v):
    m = ADAM_B1 * m + (1.0 - ADAM_B1) * g
    v = ADAM_B2 * v + (1.0 - ADAM_B2) * (g * g)
    delta = -ADAM_LR * ((m / BC1) / (jnp.sqrt(v / BC2) + ADAM_EPS) + ADAM_WD * w)
    return delta, m, v


UPDATE_STEPS = 4


def _update_sharded(plain, w_in_parts, small, core):
    g_own, sib_own, landed_in, w_t, m_t, v_t = w_in_parts
    rows_t, cols_t = w_t.shape
    cb = cols_t // UPDATE_STEPS
    n_plain = len(plain)
    vrecv_m, vrecv_b, wab, wrecv, vec_x = small

    def body(core_ref, *refs):
        refs = list(refs)
        take = lambda n: [refs.pop(0) for _ in range(n)]
        plain_in = [take(5) for _ in range(n_plain)]
        g_ref, s_ref, lt_ref, wt_ref, mt_ref, vt_ref = take(6)
        small_in = take(5)
        plain_out = [take(4) for _ in range(n_plain)]
        win_out = take(4)
        o_vec, o_w = take(2)
        step = pl.program_id(0)
        _host_small_finish(step, UPDATE_STEPS, *small_in, o_vec, o_w, *refs)

        def write(outs, w_ref, gv, m_ref, v_ref):
            delta, mn, vn = _adamw(w_ref[...], gv, m_ref[...], v_ref[...])
            for ref, val in zip(outs, (gv, delta, mn, vn)):
                ref[...] = val

        for (gp_ref, l_ref, w_ref, m_ref, v_ref), outs in zip(plain_in, plain_out):
            gv = gp_ref[...]
            for j in range(3):
                gv = gv + l_ref[j].astype(F32)
            write(outs, w_ref, gv, m_ref, v_ref)
        gv = g_ref[0, 0].astype(F32) + s_ref[0].astype(F32)
        for j in range(3):
            gv = gv + lt_ref[j].astype(F32)
        write(win_out, wt_ref, gv, mt_ref, vt_ref)

    in_specs, out_specs, out_shape, args = [], [], [], []
    for g, landed, w, m, v in plain:
        rows, cols = w.shape
        rb = rows // UPDATE_STEPS
        blk = pl.BlockSpec((rb, cols), lambda i, cr: (i, 0))
        in_specs += [blk, pl.BlockSpec((3, rb, cols), lambda i, cr: (0, i, 0)), blk, blk, blk]
        out_specs += [blk] * 4
        out_shape += [pltpu.HBM((rows, cols), F32)] * 4
        args += _in_hbm(g, landed, w, m, v)
    blk_t = pl.BlockSpec((rows_t, cb), lambda i, cr: (0, i))
    in_specs += [pl.BlockSpec((1, 1, rows_t, cb), lambda i, cr: (0, cr[0], 0, i)),
                 pl.BlockSpec((1, rows_t, cb), lambda i, cr: (0, 0, i)),
                 pl.BlockSpec((3, rows_t, cb), lambda i, cr: (0, 0, i)), blk_t, blk_t, blk_t]
    out_specs += [blk_t] * 4
    out_shape += [pltpu.HBM((rows_t, cols_t), F32)] * 4
    args += _in_hbm(g_own.reshape(1, 2, rows_t, cols_t), sib_own, landed_in, w_t, m_t, v_t)
    whole = lambda a: pl.BlockSpec(a.shape, lambda i, cr: (0,) * a.ndim)
    in_specs += [whole(a) for a in small]
    args += list(small)
    out_specs += [pl.BlockSpec((VEC_ROWS, D_MODEL), lambda i, cr: (0, 0)), HBM_SPEC]
    out_shape += [jax.ShapeDtypeStruct((VEC_ROWS, D_MODEL), F32), jax.ShapeDtypeStruct(wab.shape, F32)]
    dma8 = pltpu.SemaphoreType.DMA((N_DEV,))
    grid_spec = pltpu.PrefetchScalarGridSpec(
        num_scalar_prefetch=1, grid=(UPDATE_STEPS,), in_specs=in_specs, out_specs=out_specs,
        scratch_shapes=[pltpu.VMEM((N_DEV,) + vec_x.shape, F32), pltpu.VMEM(wrecv.shape[1:], F32),
                        dma8, dma8, dma8, dma8, pltpu.SemaphoreType.DMA((1,))])
    return pl.pallas_call(
        body, grid_spec=grid_spec, out_shape=out_shape,
        compiler_params=_params(("arbitrary",), 48), name="update_sharded",
    )(core, *args)


def _update_small(vsum, wsum, g_cw, g_rw, weights, moments_m, moments_v):
    n = len(weights)

    def body(*refs):
        vs, ws, gcw, grw = refs[0:4]
        w_refs = refs[4:4 + n]
        m_refs = refs[4 + n:4 + 2 * n]
        v_refs = refs[4 + 2 * n:4 + 3 * n]
        outs = refs[4 + 3 * n:]
        loss_ref = outs[0]
        loss_ref[...] = jnp.sum(vs[ROW_LOSS:ROW_LOSS + 1, :], axis=1, keepdims=True)
        grads = [
            vs[ROW_GMIX:ROW_GMIX + 1, :], gcw[...], grw[...], vs[ROW_BR:ROW_BR + 1, :],
            ws[0:LRU_WIDTH, :], vs[ROW_BA:ROW_BA + 1, :], ws[LRU_WIDTH:2 * LRU_WIDTH, :], vs[ROW_BX:ROW_BX + 1, :],
            vs[ROW_LAM:ROW_LAM + 1, :], vs[ROW_GNC:ROW_GNC + 1, 0:CONV_WIDTH], vs[ROW_GNR:ROW_GNR + 1, :],
            vs[ROW_GMLP:ROW_GMLP + 1, :], vs[ROW_GF:ROW_GF + 1, :],
        ]
        for k in range(n):
            gk = grads[k]
            delta, mn, vn = _adamw(w_refs[k][...], gk, m_refs[k][...], v_refs[k][...])
            outs[1 + 4 * k][...] = gk
            outs[2 + 4 * k][...] = delta
            outs[3 + 4 * k][...] = mn
            outs[4 + 4 * k][...] = vn

    whole = lambda a: pl.BlockSpec(a.shape, lambda i: (0,) * len(a.shape))
    out_shape = [jax.ShapeDtypeStruct((1, 1), F32)]
    for w in weights:
        out_shape += [jax.ShapeDtypeStruct(w.shape, F32)] * 4
    args = (vsum, wsum, g_cw, g_rw, *weights, *moments_m, *moments_v)
    return pl.pallas_call(
        body, grid=(1,), out_shape=out_shape, in_specs=[whole(a) for a in args], out_specs=[whole(s) for s in out_shape],
        compiler_params=_params(("arbitrary",), 32), name="update_small",
    )(*args)


def kernel(x, norm_mix_g, w_in, conv_w, rnn_conv_w, rnn_conv_b, w_a, b_a, w_x, b_x, lru_lambda, g_norm_conv, g_norm_rnn, w_out, norm_mlp_g, w_mlp_in, w_mlp_out, final_norm_g, loss_target, m_norm_mix_g, m_w_in, m_conv_w, m_rnn_conv_w, m_rnn_conv_b, m_w_a, m_b_a, m_w_x, m_b_x, m_lru_lambda, m_g_norm_conv, m_g_norm_rnn, m_w_out, m_norm_mlp_g, m_w_mlp_in, m_w_mlp_out, m_final_norm_g, v_norm_mix_g, v_w_in, v_conv_w, v_rnn_conv_w, v_rnn_conv_b, v_w_a, v_b_a, v_w_x, v_b_x, v_lru_lambda, v_g_norm_conv, v_g_norm_rnn, v_w_out, v_norm_mlp_g, v_w_mlp_in, v_w_mlp_out, v_final_norm_g):
    t_len = x.shape[1]
    my_id = 4 * lax.axis_index("x") + 2 * lax.axis_index("y") + lax.axis_index("c")
    tm = min(256, t_len)
    tb = min(512, t_len)
    tk = min(512, t_len)

    xs = x.reshape(t_len, D_MODEL)
    tgt = loss_target.reshape(t_len, D_MODEL)
    flat = lambda a: a.reshape(a.shape[-2:]) if a.ndim == 3 else a.reshape(1, -1)
    heads = lambda a: a.reshape(LRU_WIDTH, HEAD_DIM)

    turned = lambda a: jnp.transpose(flat(a))
    win_shard, wout_shard, w1_shard, w2_shard, cp_shard = _prep_shards(
        turned(w_in), flat(w_out), flat(w_mlp_in), flat(w_mlp_out), flat(conv_w), flat(rnn_conv_w))

    u, h, win_t, cp_full = _in_proj(xs, flat(norm_mix_g), (win_shard, cp_shard), min(1024, t_len))
    cpack = cp_full.reshape(N_DEV, 8, 128)
    conv_full = jnp.transpose(cpack[:, 0:3, 0:64], (1, 0, 2)).reshape(3, CONV_WIDTH)
    rnn_full = jnp.transpose(cpack[:, 3:7, :], (1, 0, 2)).reshape(4, LRU_WIDTH)
    mixer_small = (conv_full, rnn_full, flat(rnn_conv_b), heads(w_a), flat(b_a), heads(w_x), flat(b_x),
                   flat(lru_lambda), flat(g_norm_conv), flat(g_norm_rnn))
    hs, y, xr, gate_r, gate_i, mult, w1_blk, wout_blk = _mixer_fwd(u, *mixer_small, (w1_shard, wout_shard), tm)
    wout_f = wout_blk.reshape(MIX_WIDTH, D_MODEL)
    x1, h2, z, w2_blk = _mlp_up(xs, y, flat(norm_mlp_g), wout_f, w1_blk, w2_shard, tb)
    dx1, dx2, vec_m, dpre = _mlp_down_bwd(x1, z, tgt, flat(norm_mlp_g), flat(final_norm_g), w1_blk,
                                          w2_blk.reshape(D_FF, D_MODEL), tb)
    (g_w1,) = _tn_weight_grad(h2, dpre, tk, "w_mlp_in_grad", col_blocks=N_DEV)
    (g_w2,) = _tn_weight_grad(z, dx2, tk, "w_mlp_out_grad")
    g_w2 = g_w2.reshape(N_DEV, D_FF // N_DEV, D_MODEL)
    g_wout, sib_w1, sib_w2 = _tn_weight_grad(y, dx1, tk, "w_out_grad", pair=(g_w1, g_w2))
    g_wout = g_wout.reshape(N_DEV, MIX_WIDTH // N_DEV, D_MODEL)
    hsend_w1, own_w1 = _pair_sum(g_w1, sib_w1, "pair_sum_w_mlp_in")
    hsend_w2, own_w2 = _pair_sum(g_w2, sib_w2, "pair_sum_w_mlp_out")
    du, vec_b, wab, landed_w1, landed_w2, sib_wout = _mixer_bwd(
        u, hs, dx1, (xr, gate_r, gate_i, mult), *mixer_small, wout_f, (hsend_w1, hsend_w2), g_wout, tm)
    hsend_wout, own_wout = _pair_sum(g_wout, sib_wout, "pair_sum_w_out")
    ax, ay, ac = lax.axis_index("x"), lax.axis_index("y"), lax.axis_index("c")
    chip_ids = jnp.stack([2 * cx + cy for cx, cy in [(ax, ay)] + _other_chips(ax, ay)]).astype(jnp.int32)
    core = jnp.reshape(ac, (1,)).astype(jnp.int32)
    tw = min(1024, t_len)
    g_others, landed_wout, vrecv_m, vrecv_b, wrecv = _w_in_grad_part(
        du, h, tw, "w_in_grad_others", chip_ids[1:4], chip=(hsend_wout,), small=(vec_m, vec_b, wab))
    g_own, sib_others = _w_in_grad_part(du, h, tw, "w_in_grad_own", chip_ids[0:1], halves=g_others)
    hsend_win = _pair_sum_parts(g_others, sib_others, core)
    grad_x, vec_x, landed_win, sib_own = _in_proj_bwd(du, dx1, xs, flat(norm_mix_g), win_t, tm, hsend_win, g_own)

    updated = _update_sharded(
        [(own_w1, landed_w1, flat(w_mlp_in), flat(m_w_mlp_in), flat(v_w_mlp_in)),
         (own_w2, landed_w2, flat(w_mlp_out), flat(m_w_mlp_out), flat(v_w_mlp_out)),
         (own_wout, landed_wout, flat(w_out), flat(m_w_out), flat(v_w_out))],
        (g_own, sib_own, landed_win, turned(w_in), turned(m_w_in), turned(v_w_in)),
        (vrecv_m, vrecv_b, wab, wrecv, vec_x), core)
    up_w1, up_w2, up_wout = updated[0:4], updated[4:8], updated[8:12]
    up_win = [jnp.transpose(a) for a in updated[12:16]]
    vsum, wsum = updated[16:18]


    g_cw = lax.dynamic_slice(vsum, (ROW_CW, 64 * my_id), (3, 64))
    g_rw = lax.dynamic_slice(vsum, (ROW_RW, 128 * my_id), (4, 128))
    small_w = (norm_mix_g, conv_w, rnn_conv_w, rnn_conv_b, w_a, b_a, w_x, b_x, lru_lambda, g_norm_conv, g_norm_rnn,
               norm_mlp_g, final_norm_g)
    small_m = (m_norm_mix_g, m_conv_w, m_rnn_conv_w, m_rnn_conv_b, m_w_a, m_b_a, m_w_x, m_b_x, m_lru_lambda,
               m_g_norm_conv, m_g_norm_rnn, m_norm_mlp_g, m_final_norm_g)
    small_v = (v_norm_mix_g, v_conv_w, v_rnn_conv_w, v_rnn_conv_b, v_w_a, v_b_a, v_w_x, v_b_x, v_lru_lambda,
               v_g_norm_conv, v_g_norm_rnn, v_norm_mlp_g, v_final_norm_g)
    is_heads = (False, False, False, False, True, False, True, False, False, False, False, False, False)
    as2d = lambda arrs: [heads(a) if hd else flat(a) for a, hd in zip(arrs, is_heads)]
    small_out = _update_small(vsum, wsum, g_cw, g_rw, as2d(small_w), as2d(small_m), as2d(small_v))
    loss = small_out[0].reshape(())

    names = ["norm_mix_g", "w_in", "conv_w", "rnn_conv_w", "rnn_conv_b", "w_a", "b_a", "w_x", "b_x", "lru_lambda",
             "g_norm_conv", "g_norm_rnn", "w_out", "norm_mlp_g", "w_mlp_in", "w_mlp_out", "final_norm_g"]
    originals = dict(zip(names, (norm_mix_g, w_in, conv_w, rnn_conv_w, rnn_conv_b, w_a, b_a, w_x, b_x, lru_lambda,
                                 g_norm_conv, g_norm_rnn, w_out, norm_mlp_g, w_mlp_in, w_mlp_out, final_norm_g)))
    results = {"w_in": up_win, "w_out": up_wout, "w_mlp_in": up_w1, "w_mlp_out": up_w2}
    small_names = ["norm_mix_g", "conv_w", "rnn_conv_w", "rnn_conv_b", "w_a", "b_a", "w_x", "b_x", "lru_lambda",
                   "g_norm_conv", "g_norm_rnn", "norm_mlp_g", "final_norm_g"]
    for k, nm in enumerate(small_names):
        results[nm] = small_out[1 + 4 * k:5 + 4 * k]
    out = [loss, grad_x.reshape(x.shape)]
    for kind in range(4):
        out += [results[nm][kind].reshape(originals[nm].shape) for nm in names]
    return tuple(out)
```

```python
import functools

import jax
import jax.numpy as jnp
from jax import lax
from jax.experimental import pallas as pl
from jax.experimental.pallas import tpu as pltpu

F32 = jnp.float32
BF16 = jnp.bfloat16

D_MODEL = 1024
HEAD_DIM = 64
CONV_WIDTH = 512
LRU_WIDTH = 1024
MIX_WIDTH = CONV_WIDTH + LRU_WIDTH
IN_COLS = 3 * CONV_WIDTH + 2 * LRU_WIDTH
D_FF = 4 * D_MODEL
GROUP = 256
EPS = 1e-6
LRU_C = 8.0
N_DEV = 8
SUB = 8

OFF_GB, OFF_GC, OFF_V, OFF_XR, OFF_G = 0, 512, 1024, 1536, 2560

ADAM_LR, ADAM_B1, ADAM_B2, ADAM_EPS, ADAM_WD, ADAM_STEP = 0.001, 0.9, 0.999, 1e-08, 0.01, 10
BC1 = 1.0 - ADAM_B1 ** ADAM_STEP
BC2 = 1.0 - ADAM_B2 ** ADAM_STEP

MIB = 1024 * 1024
MESH = pl.DeviceIdType.MESH

VEC_ROWS = 32
ROW_GF, ROW_GMLP, ROW_LOSS = 0, 1, 2
ROW_GNC, ROW_GNR, ROW_BR, ROW_BA, ROW_BX, ROW_LAM, ROW_CW, ROW_RW = 8, 9, 10, 11, 12, 13, 14, 17
ROW_GMIX = 24
ACC_GNC, ACC_GNR, ACC_BR, ACC_BA, ACC_BX, ACC_SP, ACC_CW, ACC_RW, N_ACC = 0, 1, 2, 3, 4, 5, 6, 9, 13


def _params(semantics=None, vmem_mib=48):
    return pltpu.CompilerParams(dimension_semantics=semantics, vmem_limit_bytes=vmem_mib * MIB)


def _rms(x):
    return lax.rsqrt(jnp.mean(x * x, axis=-1, keepdims=True) + EPS)


def _rms_bwd(dy, xhat, r, g):
    dyh = dy * g
    return r * (dyh - xhat * jnp.mean(dyh * xhat, axis=-1, keepdims=True))


def _sigmoid(x):
    return 0.5 + 0.5 * jnp.tanh(0.5 * x)


def _gelu(x):
    c0, c1 = 0.7978845608028654, 0.044715
    x2 = x * x
    t = jnp.tanh(x * (c0 + (c0 * c1) * x2))
    half = 0.5 + 0.5 * t
    ge = x * half
    dge = half + (0.5 * x) * (1.0 - t * t) * (c0 + (3.0 * c0 * c1) * x2)
    return ge, dge


def _softplus_neg(lam):
    z = -lam
    e = jnp.exp(-jnp.abs(z))
    return jnp.maximum(z, 0.0) + jnp.where(e < 1e-4, e * (1.0 - 0.5 * e), jnp.log(1.0 + e))


def _lru_gates(pa, px, sp_c):
    ra = _sigmoid(pa)
    ii = _sigmoid(px)
    la = -ra * sp_c
    a = jnp.exp(la)
    x2 = 2.0 * la
    series = -x2 * (1.0 + x2 * (0.5 + x2 * (1.0 / 6.0 + x2 * (1.0 / 24.0))))
    m2 = jnp.where(x2 > -0.01, series, 1.0 - a * a)
    inv_mult = lax.rsqrt(m2)
    mult = jnp.where(m2 > 0.0, m2 * inv_mult, 0.0)
    return ra, ii, a, mult, inv_mult


def _down(cur, prev, s, row):
    return jnp.where(row >= s, pltpu.roll(cur, s, 0), pltpu.roll(prev, s, 0))


def _up(cur, nxt, s, row):
    return jnp.where(row < SUB - s, pltpu.roll(cur, SUB - s, 0), pltpu.roll(nxt, SUB - s, 0))


def _scan8_fwd(a, b, row):
    for s in (1, 2, 4):
        m = row >= s
        a_sh = pltpu.roll(a, s, 0)
        b_sh = pltpu.roll(b, s, 0)
        b = jnp.where(m, a * b_sh + b, b)
        a = jnp.where(m, a * a_sh, a)
    return a, b


def _scan8_rev(a, b, row):
    for s in (1, 2, 4):
        m = row < SUB - s
        a_sh = pltpu.roll(a, SUB - s, 0)
        b_sh = pltpu.roll(b, SUB - s, 0)
        b = jnp.where(m, a * b_sh + b, b)
        a = jnp.where(m, a * a_sh, a)
    return a, b


def _group_mask(shape):
    r = lax.broadcasted_iota(jnp.int32, shape, 0)
    c = lax.broadcasted_iota(jnp.int32, shape, 1)
    return ((r % GROUP) // HEAD_DIM) == (c // HEAD_DIM)


def _expand_heads(w):
    j = lax.broadcasted_iota(jnp.int32, (HEAD_DIM, GROUP), 0)
    c = lax.broadcasted_iota(jnp.int32, (HEAD_DIM, GROUP), 1)
    spread = (c % HEAD_DIM == j).astype(BF16)
    e = jnp.dot(w.astype(BF16), spread, preferred_element_type=F32)
    return jnp.where(_group_mask(e.shape), e, 0.0).astype(BF16)


def _fold_heads(p):
    p = jnp.where(_group_mask(p.shape), p, 0.0)
    c = lax.broadcasted_iota(jnp.int32, (GROUP, HEAD_DIM), 0)
    j = lax.broadcasted_iota(jnp.int32, (GROUP, HEAD_DIM), 1)
    fold = (c % HEAD_DIM == j).astype(BF16)
    hi = p.astype(BF16)
    rest = p - hi.astype(F32)
    mid = rest.astype(BF16)
    lo = (rest - mid.astype(F32)).astype(BF16)
    dot = functools.partial(jnp.dot, preferred_element_type=F32)
    return dot(hi, fold) + dot(mid, fold) + dot(lo, fold)


def _block_diag_apply(xb, wbd_ref):
    parts = [jnp.dot(xb[:, g * GROUP:(g + 1) * GROUP], wbd_ref[g * GROUP:(g + 1) * GROUP, :],
                     preferred_element_type=F32) for g in range(LRU_WIDTH // GROUP)]
    return jnp.concatenate(parts, axis=1)


def _block_diag_apply_t(db, wbd_ref):
    parts = [lax.dot_general(db[:, g * GROUP:(g + 1) * GROUP], wbd_ref[g * GROUP:(g + 1) * GROUP, :],
                             (((1,), (1,)), ((), ())), preferred_element_type=F32)
             for g in range(LRU_WIDTH // GROUP)]
    return jnp.concatenate(parts, axis=1)


def _dot_nt(a, b):
    return lax.dot_general(a, b, (((1,), (1,)), ((), ())), preferred_element_type=F32)


def _dot_tn(a, b):
    return lax.dot_general(a, b, (((0,), (0,)), ((), ())), preferred_element_type=F32)


def _chunk_loop(n_chunks, chunk, init, in_flight=4):
    def body(k, carry):
        for j in range(in_flight):
            carry = chunk(k * in_flight + j, carry)
        return carry

    return lax.fori_loop(0, n_chunks // in_flight, body, init)


def _place():
    x, y, c = lax.axis_index("x"), lax.axis_index("y"), lax.axis_index("c")
    return x, y, c


def _block_id(chip, core):
    return 4 * chip[0] + 2 * chip[1] + core


def _other_chips(x, y):
    return [(1 - x, y), (x, 1 - y), (1 - x, 1 - y)]


def _remote_copy(src, dst, send_sem, recv_sem, to):
    return pltpu.make_async_remote_copy(src_ref=src, dst_ref=dst, send_sem=send_sem, recv_sem=recv_sem,
                                        device_id=to, device_id_type=MESH)


HBM_SPEC = pl.BlockSpec(memory_space=pl.ANY)


def _in_hbm(*arrays):
    return [pltpu.with_memory_space_constraint(a, pltpu.HBM) for a in arrays]


def _prep_shards(w_in_t, w_out, w_mlp_in, w_mlp_out, conv_w, rnn_conv_w):
    def body(win_ref, wout_ref, w1_ref, w2_ref, cw_ref, rw_ref, o_win, o_wout, o_w1, o_w2, o_cp):
        o_win[...] = win_ref[...].astype(BF16)
        o_wout[...] = wout_ref[...].astype(BF16)
        o_w1[...] = w1_ref[...].astype(BF16)
        o_w2[...] = w2_ref[...].astype(BF16)
        o_cp[...] = jnp.zeros(o_cp.shape, F32)
        o_cp[0:3, 0:64] = cw_ref[...]
        o_cp[3:7, :] = rw_ref[...]

    whole = lambda shape: pl.BlockSpec(shape, lambda i: (0,) * len(shape))
    args = (w_in_t, w_out, w_mlp_in, w_mlp_out, conv_w, rnn_conv_w)
    shapes = [(w_in_t.shape, BF16), (w_out.shape, BF16), (w_mlp_in.shape, BF16), (w_mlp_out.shape, BF16),
              ((8, 128), F32)]
    return pl.pallas_call(
        body, grid=(1,), out_shape=[jax.ShapeDtypeStruct(s, d) for s, d in shapes],
        in_specs=[whole(a.shape) for a in args], out_specs=[whole(s) for s, _ in shapes],
        compiler_params=_params(("arbitrary",), 40), name="prep_shards",
    )(*args)


def _host_all_gather(step, n_steps, shards, fulls, send_sems, recv_sems, local_sems):
    x, y, c = _place()
    me = (x, y, c)
    my_id = _block_id((x, y), c)
    sibling = (x, y, 1 - c)
    chips = _other_chips(x, y)
    n_arr = len(shards)

    def copy(arr, k, block, to, src=None):
        dst = fulls[arr].at[block]
        return _remote_copy(dst if src is None else src, dst, send_sems.at[arr, k], recv_sems.at[arr, k], to)

    def local(arr):
        return pltpu.make_async_copy(shards[arr], fulls[arr].at[my_id], local_sems.at[arr])

    @pl.when(step == 0)
    def _():
        for arr in range(n_arr):
            local(arr).start()
            copy(arr, 0, my_id, sibling, shards[arr]).start()
            for j, chip in enumerate(chips):
                copy(arr, 1 + j, my_id, (*chip, c), shards[arr]).start()

    @pl.when(step == max(n_steps - 2, 0))
    def _():
        for j, chip in enumerate(chips):
            for arr in range(n_arr):
                copy(arr, 1 + j, _block_id(chip, c), me).wait_recv()
                copy(arr, 4 + j, _block_id(chip, c), sibling).start()

    @pl.when(step == n_steps - 1)
    def _():
        for arr in range(n_arr):
            copy(arr, 0, _block_id((x, y), 1 - c), me).wait_recv()
            for j, chip in enumerate(chips):
                copy(arr, 4 + j, _block_id(chip, 1 - c), me).wait_recv()
            for k in range(4):
                copy(arr, k, my_id, me, shards[arr]).wait_send()
            for j, chip in enumerate(chips):
                copy(arr, 4 + j, _block_id(chip, c), me).wait_send()
            local(arr).wait()


def _host_pair_exchange(step, n_steps, gs, sibs, send_sems, recv_sems):
    x, y, c = _place()
    sibling = (x, y, 1 - c)
    chips = [(x, y)] + _other_chips(x, y)

    def d2d(arr, q):
        return _remote_copy(gs[arr].at[_block_id(chips[q], 1 - c)], sibs[arr].at[q],
                            send_sems.at[arr, q], recv_sems.at[arr, q], sibling)

    @pl.when(step == 0)
    def _():
        for arr in range(len(gs)):
            for q in (1, 2, 3, 0):
                d2d(arr, q).start()

    @pl.when(step == n_steps - 1)
    def _():
        for arr in range(len(gs)):
            for q in range(4):
                d2d(arr, q).wait()


def _host_chip_exchange(step, n_steps, hsends, hrecvs, send_sems, recv_sems):
    x, y, c = _place()
    chips = _other_chips(x, y)

    def ici(arr, j):
        return _remote_copy(hsends[arr].at[j], hrecvs[arr].at[j], send_sems.at[arr, j], recv_sems.at[arr, j],
                            (*chips[j], c))

    @pl.when(step == 0)
    def _():
        for arr in range(len(hsends)):
            for j in range(3):
                ici(arr, j).start()

    @pl.when(step == n_steps - 1)
    def _():
        for arr in range(len(hsends)):
            for j in range(3):
                ici(arr, j).wait()


def _host_half_exchange(step, n_steps, parts, sibs, send_sems, recv_sems):
    x, y, c = _place()
    n_q, rows2, _ = parts.shape
    half = rows2 // 2

    def d2d(q):
        src = parts.at[q, pl.ds(pl.multiple_of((1 - c) * half, 16), half), :]
        return _remote_copy(src, sibs.at[q], send_sems.at[q], recv_sems.at[q], (x, y, 1 - c))

    @pl.when(step == 0)
    def _():
        for q in range(n_q):
            d2d(q).start()

    @pl.when(step == n_steps - 1)
    def _():
        for q in range(n_q):
            d2d(q).wait()


def _peer(x, y, c, k):
    return (x ^ ((k >> 2) & 1), y ^ ((k >> 1) & 1), c ^ (k & 1))


def _host_small_exchange(step, n_steps, vec_m, vec_b, wab, vrecv_m, vrecv_b, wrecv, send_sems, recv_sems, local_sems):
    x, y, c = _place()
    my_id = _block_id((x, y), c)
    wrows = wab.shape[0] // N_DEV

    def copies(k):
        to = _peer(x, y, c, k)
        block = wab.at[pl.ds(pl.multiple_of(_block_id(to[0:2], to[2]) * wrows, SUB), wrows), :]
        return [_remote_copy(vec_m, vrecv_m.at[my_id], send_sems.at[0, k], recv_sems.at[0, k], to),
                _remote_copy(vec_b, vrecv_b.at[my_id], send_sems.at[1, k], recv_sems.at[1, k], to),
                _remote_copy(block, wrecv.at[k], send_sems.at[2, k], recv_sems.at[2, k], to)]

    mine = [pltpu.make_async_copy(vec_m, vrecv_m.at[my_id], local_sems.at[0]),
            pltpu.make_async_copy(vec_b, vrecv_b.at[my_id], local_sems.at[1])]

    @pl.when(step == 0)
    def _():
        for cp in mine:
            cp.start()
        for k in range(1, N_DEV):
            for cp in copies(k):
                cp.start()

    @pl.when(step == n_steps - 1)
    def _():
        for k in range(1, N_DEV):
            for cp in copies(k):
                cp.wait()
        for cp in mine:
            cp.wait()


def _pair_sum_parts(parts, sibs, core):
    n_q, rows2, cols = parts.shape
    half = rows2 // 2

    def body(core_ref, g_ref, s_ref, o_ref):
        o_ref[0] = (g_ref[0, 0].astype(F32) + s_ref[0].astype(F32)).astype(BF16)

    block = (1, half, cols)
    grid_spec = pltpu.PrefetchScalarGridSpec(
        num_scalar_prefetch=1, grid=(n_q,),
        in_specs=[pl.BlockSpec((1, 1, half, cols), lambda q, cr: (q, cr[0], 0, 0)),
                  pl.BlockSpec(block, lambda q, cr: (q, 0, 0))],
        out_specs=pl.BlockSpec(block, lambda q, cr: (q, 0, 0)))
    return pl.pallas_call(
        body, grid_spec=grid_spec, out_shape=pltpu.HBM((n_q, half, cols), BF16),
        compiler_params=_params(("arbitrary",), 32), name="pair_sum_w_in",
    )(core, *_in_hbm(parts.reshape(n_q, 2, half, cols), sibs))


def _pair_sum(g, sib, name):
    _, rows, cols = g.shape
    x, y, c = _place()
    slots = jnp.stack([_block_id(chip, c) for chip in [(x, y)] + _other_chips(x, y)]).astype(jnp.int32)

    def body(slots_ref, g_ref, sib_ref, hs_ref, own_ref):
        q = pl.program_id(0)
        both = g_ref[0].astype(F32) + sib_ref[0].astype(F32)

        @pl.when(q == 0)
        def _():
            own_ref[...] = both

        @pl.when(q > 0)
        def _():
            hs_ref[0] = both.astype(BF16)

    block = (1, rows, cols)
    grid_spec = pltpu.PrefetchScalarGridSpec(
        num_scalar_prefetch=1, grid=(4,),
        in_specs=[pl.BlockSpec(block, lambda q, s: (s[q], 0, 0)), pl.BlockSpec(block, lambda q, s: (q, 0, 0))],
        out_specs=[pl.BlockSpec(block, lambda q, s: (jnp.maximum(q - 1, 0), 0, 0)),
                   pl.BlockSpec((rows, cols), lambda q, s: (0, 0))])
    return pl.pallas_call(
        body, grid_spec=grid_spec,
        out_shape=(pltpu.HBM((3, rows, cols), BF16), pltpu.HBM((rows, cols), F32)),
        compiler_params=_params(("arbitrary",), 32), name=name,
    )(slots, *_in_hbm(g, sib))


def _exchange_scratch(n_arr, n_copies):
    return [pltpu.SemaphoreType.DMA((n_arr, n_copies)), pltpu.SemaphoreType.DMA((n_arr, n_copies))]


def _final_small(vrecv_m, vrecv_b, wab, wrecv, vec_x):
    wrows = wab.shape[0] // N_DEV

    def body(vm_ref, vb_ref, w_ref, wr_ref, vx_ref, o_vec, o_w, xrecv, wred, x_send, x_recv, b_send, b_recv):
        x, y, c = _place()
        my_id = _block_id((x, y), c)
        my_rows = pl.ds(pl.multiple_of(my_id * wrows, SUB), wrows)

        def xcopy(k):
            return _remote_copy(vx_ref, xrecv.at[my_id], x_send.at[k], x_recv.at[k], _peer(x, y, c, k))

        def bcopy(k):
            return _remote_copy(wred, o_w.at[my_rows, :], b_send.at[k], b_recv.at[k], _peer(x, y, c, k))

        xrecv[my_id] = vx_ref[...]
        for k in range(1, N_DEV):
            xcopy(k).start()
        red = w_ref[my_rows, :]
        for k in range(1, N_DEV):
            red = red + wr_ref[k]
        wred[...] = red
        o_w[my_rows, :] = red
        for k in range(1, N_DEV):
            bcopy(k).start()
        for k in range(1, N_DEV):
            xcopy(k).wait_recv()
        for rows, ref in ((slice(0, 8), vm_ref), (slice(8, 24), vb_ref), (slice(24, 32), xrecv)):
            tot = ref[0]
            for s in range(1, N_DEV):
                tot = tot + ref[s]
            o_vec[rows, :] = tot
        for k in range(1, N_DEV):
            bcopy(k).wait_recv()
        for k in range(1, N_DEV):
            xcopy(k).wait_send()
            bcopy(k).wait_send()

    vm = pl.BlockSpec(memory_space=pltpu.VMEM)
    dma8 = pltpu.SemaphoreType.DMA((N_DEV,))
    return pl.pallas_call(
        body, out_shape=(jax.ShapeDtypeStruct((VEC_ROWS, D_MODEL), F32), jax.ShapeDtypeStruct(wab.shape, F32)),
        in_specs=[vm] * 5, out_specs=[vm] * 2,
        scratch_shapes=[pltpu.VMEM((N_DEV, SUB, D_MODEL), F32), pltpu.VMEM((wrows, HEAD_DIM), F32),
                        dma8, dma8, dma8, dma8],
        compiler_params=_params(vmem_mib=32), name="final_small",
    )(vrecv_m, vrecv_b, wab, wrecv, vec_x)


def _in_proj(x, g_mix, shards, tm):
    t_len = x.shape[0]
    n_t = t_len // tm
    n_arr = len(shards)
    rows = [s.shape[0] for s in shards]
    width = 2 * rows[0]
    ax, ay = lax.axis_index("x"), lax.axis_index("y")
    order = jnp.stack([2 * cx + cy for cx, cy in [(ax, ay)] + _other_chips(ax, ay)]).astype(jnp.int32)

    def body(order_ref, x_ref, g_ref, *rest):
        shard_refs = rest[0:n_arr]
        u_ref, h_ref = rest[n_arr:n_arr + 2]
        fulls = rest[n_arr + 2:2 * n_arr + 2]
        h_s, wbuf, send_sems, recv_sems, local_sems, load_sem = rest[2 * n_arr + 2:]
        p = pl.program_id(0)
        i = pl.program_id(1)
        x_, y_, c = _place()
        me = (x_, y_, c)
        my_id = _block_id((x_, y_), c)
        sibling = (x_, y_, 1 - c)
        chips = _other_chips(x_, y_)

        def block(arr, blk):
            return fulls[arr].at[pl.ds(pl.multiple_of(blk * rows[arr], rows[arr]), rows[arr]), :]

        def copy(arr, k, blk, to, src=None):
            dst = block(arr, blk)
            return _remote_copy(dst if src is None else src, dst, send_sems.at[arr, k], recv_sems.at[arr, k], to)

        def local(arr):
            return pltpu.make_async_copy(shard_refs[arr], block(arr, my_id), local_sems.at[arr])

        def load_chip(chip):
            start = pl.multiple_of((2 * chip[0] + chip[1]) * width, width)
            cp = pltpu.make_async_copy(fulls[0].at[pl.ds(start, width), :], wbuf, load_sem.at[0])
            cp.start()
            cp.wait()

        @pl.when((p == 0) & (i == 0))
        def _():
            for arr in range(n_arr):
                local(arr).start()
                copy(arr, 0, my_id, sibling, shard_refs[arr]).start()
                for j in (0, 1):
                    copy(arr, 1 + j, my_id, (*chips[j], c), shard_refs[arr]).start()
            for arr in range(n_arr):
                local(arr).wait()
                copy(arr, 0, _block_id((x_, y_), 1 - c), me).wait_recv()
            load_chip((x_, y_))

        for j, chip in enumerate(chips):
            @pl.when((p == j + 1) & (i == 0))
            def _(j=j, chip=chip):
                for arr in range(n_arr):
                    copy(arr, 1 + j, _block_id(chip, c), me).wait_recv()
                    copy(arr, 4 + j, _block_id(chip, c), sibling).start()
                    if j == 0:
                        copy(arr, 3, my_id, (*chips[2], c), shard_refs[arr]).start()
                for arr in range(n_arr):
                    copy(arr, 4 + j, _block_id(chip, 1 - c), me).wait_recv()
                load_chip(chip)

        @pl.when((p == 3) & (i == n_t - 1))
        def _():
            for arr in range(n_arr):
                for k in range(4):
                    copy(arr, k, my_id, me, shard_refs[arr]).wait_send()
                for j, chip in enumerate(chips):
                    copy(arr, 4 + j, _block_id(chip, c), me).wait_send()

        tile = pl.ds(pl.multiple_of(i * tm, tm), tm)

        @pl.when(p == 0)
        def _():
            xv = x_ref[...]
            h = (xv * _rms(xv) * g_ref[...]).astype(BF16)
            h_ref[...] = h
            h_s[tile, :] = h

        u_ref[...] = _dot_nt(h_s[tile, :], wbuf[...])

    first_pass = lambda p, i, o: (jnp.where(p == 0, i, n_t - 1), 0)
    grid_spec = pltpu.PrefetchScalarGridSpec(
        num_scalar_prefetch=1, grid=(4, n_t),
        in_specs=[pl.BlockSpec((tm, D_MODEL), first_pass), pl.BlockSpec((1, D_MODEL), lambda p, i, o: (0, 0))]
        + [HBM_SPEC] * n_arr,
        out_specs=[pl.BlockSpec((tm, width), lambda p, i, o: (i, o[p])), pl.BlockSpec((tm, D_MODEL), first_pass)]
        + [HBM_SPEC] * n_arr,
        scratch_shapes=[pltpu.VMEM((t_len, D_MODEL), BF16), pltpu.VMEM((width, D_MODEL), BF16)]
        + _exchange_scratch(n_arr, 7) + [pltpu.SemaphoreType.DMA((n_arr,)), pltpu.SemaphoreType.DMA((1,))])
    return pl.pallas_call(
        body, grid_spec=grid_spec,
        out_shape=[jax.ShapeDtypeStruct((t_len, IN_COLS), F32), jax.ShapeDtypeStruct((t_len, D_MODEL), BF16)]
        + [jax.ShapeDtypeStruct((N_DEV * s.shape[0], s.shape[1]), s.dtype) for s in shards],
        compiler_params=_params(("arbitrary", "arbitrary"), 48), name="in_proj",
    )(order, x, g_mix, *shards)


def _conv3_chunk(u_ref, r, cv_prev, cw, row):
    gb = u_ref[pl.ds(r, SUB), OFF_GB:OFF_GB + CONV_WIDTH]
    gc = u_ref[pl.ds(r, SUB), OFF_GC:OFF_GC + CONV_WIDTH]
    v = u_ref[pl.ds(r, SUB), OFF_V:OFF_V + CONV_WIDTH]
    cv = gc * v
    cv_m1 = _down(cv, cv_prev, 1, row)
    cv_m2 = _down(cv, cv_prev, 2, row)
    cq = cw[2:3, :] * cv + cw[1:2, :] * cv_m1 + cw[0:1, :] * cv_m2
    return gb, gc, v, cv, cv_m1, cv_m2, cq


def _conv4_chunk(u_ref, r, xin_prev, rw, rb, row):
    xin = u_ref[pl.ds(r, SUB), OFF_XR:OFF_XR + LRU_WIDTH]
    m1 = _down(xin, xin_prev, 1, row)
    m2 = _down(xin, xin_prev, 2, row)
    m3 = _down(xin, xin_prev, 3, row)
    xr = rw[3:4, :] * xin + rw[2:3, :] * m1 + rw[1:2, :] * m2 + rw[0:1, :] * m3 + rb
    return xin, m1, m2, m3, xr


def _mixer_fwd(u, conv_w, rnn_conv_w, rnn_conv_b, wa, b_a, wx, b_x, lam, gnc, gnr, shards, tm):
    t_len = u.shape[0]
    n_steps = t_len // tm
    n_chunks = tm // SUB
    n_arr = len(shards)

    def body(u_ref, cw_ref, rw_ref, rb_ref, wa_ref, ba_ref, wx_ref, bx_ref, lam_ref, gnc_ref, gnr_ref, *rest):
        shard_refs = rest[0:n_arr]
        hs_ref, y_ref, xr_s, ra_ref, ii_ref, mult_ref = rest[n_arr:n_arr + 6]
        fulls = rest[n_arr + 6:2 * n_arr + 6]
        (y_s, pa_s, px_s, wabd, wxbd, cv_car, xin_car, h_car,
         send_sems, recv_sems, local_sems) = rest[2 * n_arr + 6:]
        _host_all_gather(pl.program_id(0), n_steps, shard_refs, fulls, send_sems, recv_sems, local_sems)

        @pl.when(pl.program_id(0) == 0)
        def _():
            cv_car[...] = jnp.zeros(cv_car.shape, F32)
            xin_car[...] = jnp.zeros(xin_car.shape, F32)
            h_car[...] = jnp.zeros(h_car.shape, F32)
            wabd[...] = _expand_heads(wa_ref[...])
            wxbd[...] = _expand_heads(wx_ref[...])

        row_c = lax.broadcasted_iota(jnp.int32, (SUB, CONV_WIDTH), 0)
        row_r = lax.broadcasted_iota(jnp.int32, (SUB, LRU_WIDTH), 0)
        cw = cw_ref[...]
        rw = rw_ref[...]
        rb = rb_ref[...]
        g_c = gnc_ref[...]
        g_r = gnr_ref[...]
        sp_c = LRU_C * _softplus_neg(lam_ref[...])

        def convs(i, carry):
            cv_prev, xin_prev = carry
            r = pl.multiple_of(i * SUB, SUB)
            gb, _, _, cv, _, _, cq = _conv3_chunk(u_ref, r, cv_prev, cw, row_c)
            y_c = gb * cq
            y_s[pl.ds(r, SUB), 0:CONV_WIDTH] = y_c * _rms(y_c) * g_c
            xin, _, _, _, xr = _conv4_chunk(u_ref, r, xin_prev, rw, rb, row_r)
            xr_s[pl.ds(r, SUB), :] = xr
            return cv, xin

        cv_last, xin_last = _chunk_loop(n_chunks, convs, (cv_car[...], xin_car[...]), in_flight=8)
        cv_car[...] = cv_last
        xin_car[...] = xin_last

        xrb = xr_s[...].astype(BF16)
        pa_s[...] = _block_diag_apply(xrb, wabd) + ba_ref[...]
        px_s[...] = _block_diag_apply(xrb, wxbd) + bx_ref[...]

        def recur(i, h_prev):
            r = pl.multiple_of(i * SUB, SUB)
            xr = xr_s[pl.ds(r, SUB), :]
            ra, ii, a, mult, _ = _lru_gates(pa_s[pl.ds(r, SUB), :], px_s[pl.ds(r, SUB), :], sp_c)
            ra_ref[pl.ds(r, SUB), :] = ra
            ii_ref[pl.ds(r, SUB), :] = ii
            mult_ref[pl.ds(r, SUB), :] = mult
            a_cum, b_cum = _scan8_fwd(a, mult * ii * xr, row_r)
            h = a_cum * h_prev + b_cum
            hs_ref[pl.ds(r, SUB), :] = h
            ge, _ = _gelu(u_ref[pl.ds(r, SUB), OFF_G:OFF_G + LRU_WIDTH])
            y_r = h * ge
            y_s[pl.ds(r, SUB), CONV_WIDTH:MIX_WIDTH] = y_r * _rms(y_r) * g_r
            return h[SUB - 1:SUB, :]

        h_car[...] = _chunk_loop(n_chunks, recur, h_car[...], in_flight=8)

        y_ref[...] = y_s[...].astype(BF16)

    row_tile = lambda w: pl.BlockSpec((tm, w), lambda i: (i, 0))
    whole = lambda a: pl.BlockSpec(a.shape, lambda i: (0,) * a.ndim)
    smalls = (conv_w, rnn_conv_w, rnn_conv_b, wa, b_a, wx, b_x, lam, gnc, gnr)
    return pl.pallas_call(
        body, grid=(n_steps,),
        in_specs=[row_tile(IN_COLS)] + [whole(a) for a in smalls] + [HBM_SPEC] * n_arr,
        out_specs=[row_tile(LRU_WIDTH), row_tile(MIX_WIDTH)] + [row_tile(LRU_WIDTH)] * 4 + [HBM_SPEC] * n_arr,
        out_shape=[jax.ShapeDtypeStruct((t_len, LRU_WIDTH), F32), jax.ShapeDtypeStruct((t_len, MIX_WIDTH), BF16)]
        + [jax.ShapeDtypeStruct((t_len, LRU_WIDTH), F32)] * 4
        + [jax.ShapeDtypeStruct((N_DEV,) + s.shape, BF16) for s in shards],
        scratch_shapes=[pltpu.VMEM((tm, MIX_WIDTH), F32),
                        pltpu.VMEM((tm, LRU_WIDTH), F32), pltpu.VMEM((tm, LRU_WIDTH), F32),
                        pltpu.VMEM((LRU_WIDTH, GROUP), BF16), pltpu.VMEM((LRU_WIDTH, GROUP), BF16),
                        pltpu.VMEM((SUB, CONV_WIDTH), F32), pltpu.VMEM((SUB, LRU_WIDTH), F32),
                        pltpu.VMEM((1, LRU_WIDTH), F32)]
        + _exchange_scratch(n_arr, 7) + [pltpu.SemaphoreType.DMA((n_arr,))],
        compiler_params=_params(("arbitrary",), 56), name="mixer_fwd",
    )(u, *smalls, *shards)


def _mlp_up(x, y, g_mlp, w_out, w1, w2_shard, tm):
    t_len = x.shape[0]
    n_steps = t_len // tm
    n_blk, _, blk = w1.shape

    def body(x_ref, y_ref, gm_ref, wout_hbm, w1_hbm, w2_ref, x1_ref, h2_ref, z_ref, w2_full,
             wout_s, w1_s, sem, send_sems, recv_sems, local_sems):
        step = pl.program_id(0)
        _host_all_gather(step, n_steps, [w2_ref], [w2_full], send_sems, recv_sems, local_sems)

        load_wout = pltpu.make_async_copy(wout_hbm, wout_s, sem.at[n_blk])
        load_w1 = [pltpu.make_async_copy(w1_hbm.at[k], w1_s.at[k], sem.at[k]) for k in range(n_blk)]

        @pl.when(step == 0)
        def _():
            load_wout.start()
            for cp in load_w1:
                cp.start()
            load_wout.wait()

        x1v = x_ref[...] + jnp.dot(y_ref[...], wout_s[...], preferred_element_type=F32)
        x1_ref[...] = x1v
        h2 = (x1v * _rms(x1v) * gm_ref[...]).astype(BF16)
        h2_ref[...] = h2
        for k in range(n_blk):
            @pl.when(step == 0)
            def _(k=k):
                load_w1[k].wait()

            rp = jnp.maximum(jnp.dot(h2, w1_s[k], preferred_element_type=F32), 0.0)
            z_ref[:, k * blk:(k + 1) * blk] = (rp * rp).astype(BF16)

    row_tile = lambda w: pl.BlockSpec((tm, w), lambda i: (i, 0))
    return pl.pallas_call(
        body, grid=(n_steps,),
        in_specs=[row_tile(D_MODEL), row_tile(MIX_WIDTH), pl.BlockSpec((1, D_MODEL), lambda i: (0, 0)),
                  HBM_SPEC, HBM_SPEC, HBM_SPEC],
        out_specs=[row_tile(D_MODEL), row_tile(D_MODEL), row_tile(D_FF), HBM_SPEC],
        out_shape=[jax.ShapeDtypeStruct((t_len, D_MODEL), F32), jax.ShapeDtypeStruct((t_len, D_MODEL), BF16),
                   jax.ShapeDtypeStruct((t_len, D_FF), BF16), jax.ShapeDtypeStruct((N_DEV,) + w2_shard.shape, BF16)],
        scratch_shapes=[pltpu.VMEM(w_out.shape, BF16), pltpu.VMEM(w1.shape, BF16),
                        pltpu.SemaphoreType.DMA((n_blk + 1,))]
        + _exchange_scratch(1, 7) + [pltpu.SemaphoreType.DMA((1,))],
        compiler_params=_params(("arbitrary",), 48), name="mlp_up",
    )(x, y, g_mlp, w_out, w1, w2_shard)


def _mlp_down_bwd(x1, z, target, g_mlp, g_f, w1, w2, tm):
    t_len = x1.shape[0]
    n_steps = t_len // tm
    n_blk, _, blk = w1.shape

    def body(x1_ref, z_ref, tg_ref, gm_ref, gf_ref, w1_hbm, w2_hbm, dx1_ref, dx2_ref, vec_ref, dpre_hbm,
             w1_s, w2_s, dp_s, sem, out_sem):
        step = pl.program_id(0)
        rows = pl.ds(pl.multiple_of(step * tm, tm), tm)
        dp_out = pltpu.make_async_copy(dp_s, dpre_hbm.at[rows, :], out_sem.at[0])

        load_w1 = pltpu.make_async_copy(w1_hbm, w1_s, sem.at[0])
        load_w2 = pltpu.make_async_copy(w2_hbm, w2_s, sem.at[1])

        @pl.when(step == 0)
        def _():
            load_w2.start()
            load_w1.start()
            vec_ref[...] = jnp.zeros(vec_ref.shape, F32)
            load_w2.wait()

        x1v = x1_ref[...]
        g_m = gm_ref[...]
        g_o = gf_ref[...]
        r2 = _rms(x1v)
        x1h = x1v * r2
        x2 = x1v + jnp.dot(z_ref[...], w2_s[...], preferred_element_type=F32)
        r3 = _rms(x2)
        x2h = x2 * r3
        err = x2h * g_o - tg_ref[...]
        dout = err * (1.0 / D_MODEL)
        vec_ref[ROW_LOSS:ROW_LOSS + 1, :] += (0.5 / D_MODEL) * jnp.sum(err * err, axis=0, keepdims=True)
        vec_ref[ROW_GF:ROW_GF + 1, :] += jnp.sum(dout * x2h, axis=0, keepdims=True)
        dx2 = _rms_bwd(dout, x2h, r3, g_o)
        dx2b = dx2.astype(BF16)
        dx2_ref[...] = dx2b
        dh2 = jnp.zeros((tm, D_MODEL), F32)

        @pl.when(step > 0)
        def _():
            dp_out.wait()

        @pl.when(step == 0)
        def _():
            load_w1.wait()

        for k in range(n_blk):
            cols = slice(k * blk, (k + 1) * blk)
            dz = _dot_nt(dx2b, w2_s[cols, :])
            dpb = (dz * 2.0 * jnp.sqrt(z_ref[:, cols].astype(F32))).astype(BF16)
            dp_s[:, cols] = dpb
            dh2 = dh2 + _dot_nt(dpb, w1_s[k])
        dp_out.start()
        vec_ref[ROW_GMLP:ROW_GMLP + 1, :] += jnp.sum(dh2 * x1h, axis=0, keepdims=True)
        dx1_ref[...] = dx2 + _rms_bwd(dh2, x1h, r2, g_m)

        @pl.when(step == n_steps - 1)
        def _():
            dp_out.wait()

    row_tile = lambda w: pl.BlockSpec((tm, w), lambda i: (i, 0))
    vec_spec = pl.BlockSpec((1, D_MODEL), lambda i: (0, 0))
    return pl.pallas_call(
        body, grid=(n_steps,),
        in_specs=[row_tile(D_MODEL), row_tile(D_FF), row_tile(D_MODEL), vec_spec, vec_spec, HBM_SPEC, HBM_SPEC],
        out_specs=[row_tile(D_MODEL), row_tile(D_MODEL), pl.BlockSpec((SUB, D_MODEL), lambda i: (0, 0)), HBM_SPEC],
        out_shape=[jax.ShapeDtypeStruct((t_len, D_MODEL), F32), jax.ShapeDtypeStruct((t_len, D_MODEL), BF16),
                   jax.ShapeDtypeStruct((SUB, D_MODEL), F32), jax.ShapeDtypeStruct((t_len, D_FF), BF16)],
        scratch_shapes=[pltpu.VMEM(w1.shape, BF16), pltpu.VMEM(w2.shape, BF16), pltpu.VMEM((tm, D_FF), BF16),
                        pltpu.SemaphoreType.DMA((2,)), pltpu.SemaphoreType.DMA((1,))],
        compiler_params=_params(("arbitrary",), 56), name="mlp_down_bwd",
    )(x1, z, target, g_mlp, g_f, w1, w2)


def _mixer_bwd(u, hs, dx1, saved, conv_w, rnn_conv_w, rnn_conv_b, wa, b_a, wx, b_x, lam, gnc, gnr, w_out,
               chip_sums, g_wout, tm):
    t_len = u.shape[0]
    n_tiles = t_len // tm
    n_chunks = tm // SUB
    per_tile = tm // SUB
    n_sums = len(chip_sums)

    def body(u_ref, up_ref, hs_ref, hp_ref, dx1_ref, xr_ref, ra_ref, ii_ref, mult_ref,
             cw_ref, rw_ref, rb_ref, wa_ref, ba_ref, wx_ref, bx_ref, lam_ref, gnc_ref, gnr_ref, wout_ref, *rest):
        hsends = rest[0:n_sums]
        gwout_ref = rest[n_sums]
        du_ref, vec_ref, wab_ref = rest[n_sums + 1:n_sums + 4]
        hrecvs = rest[n_sums + 4:2 * n_sums + 4]
        sib_wout = rest[2 * n_sums + 4]
        (du_s, dy_s, dpa_s, dpx_s, dxr_s, wabd, wxbd, acc, dwa_acc, dwx_acc,
         a_car, dh_car, dcq_car, dxr_car, i_send, i_recv, d_send, d_recv) = rest[2 * n_sums + 5:]
        step = pl.program_id(0)
        _host_chip_exchange(step, n_tiles, hsends, hrecvs, i_send, i_recv)
        _host_pair_exchange(step, n_tiles, [gwout_ref], [sib_wout], d_send, d_recv)
        has_prev = (step < n_tiles - 1).astype(F32)

        @pl.when(step == 0)
        def _():
            acc[...] = jnp.zeros(acc.shape, F32)
            dwa_acc[...] = jnp.zeros(dwa_acc.shape, F32)
            dwx_acc[...] = jnp.zeros(dwx_acc.shape, F32)
            a_car[...] = jnp.ones(a_car.shape, F32)
            dh_car[...] = jnp.zeros(dh_car.shape, F32)
            dcq_car[...] = jnp.zeros(dcq_car.shape, F32)
            dxr_car[...] = jnp.zeros(dxr_car.shape, F32)
            wabd[...] = _expand_heads(wa_ref[...])
            wxbd[...] = _expand_heads(wx_ref[...])

        row_c = lax.broadcasted_iota(jnp.int32, (SUB, CONV_WIDTH), 0)
        row_r = lax.broadcasted_iota(jnp.int32, (SUB, LRU_WIDTH), 0)
        cw = cw_ref[...]
        rw = rw_ref[...]
        rb = rb_ref[...]
        g_c = gnc_ref[...]
        g_r = gnr_ref[...]
        sp_c = LRU_C * _softplus_neg(lam_ref[...])

        up = up_ref[...] * has_prev
        cv_before = up[:, OFF_GC:OFF_GC + CONV_WIDTH] * up[:, OFF_V:OFF_V + CONV_WIDTH]
        xin_before = up[:, OFF_XR:OFF_XR + LRU_WIDTH]
        hs_before = hp_ref[...] * has_prev

        dy_s[...] = _dot_nt(dx1_ref[...].astype(BF16), wout_ref[...])

        xrb = xr_ref[...].astype(BF16)

        def recur_bwd(j, carry):
            a_later, dh_later = carry
            i = n_chunks - 1 - j
            r = pl.multiple_of(i * SUB, SUB)
            rp = pl.multiple_of(jnp.maximum(i - 1, 0) * SUB, SUB)
            xr = xr_ref[pl.ds(r, SUB), :]
            hs_c = hs_ref[pl.ds(r, SUB), :]
            hs_prev = jnp.where(i == 0, hs_before, hs_ref[pl.ds(rp, SUB), :])
            h_m1 = _down(hs_c, hs_prev, 1, row_r)
            ra = ra_ref[pl.ds(r, SUB), :]
            ii = ii_ref[pl.ds(r, SUB), :]
            mult = mult_ref[pl.ds(r, SUB), :]
            a = jnp.exp(-ra * sp_c)
            inv_mult = lax.rsqrt(mult * mult)
            ge, dge = _gelu(u_ref[pl.ds(r, SUB), OFF_G:OFF_G + LRU_WIDTH])
            y_r = hs_c * ge
            rr = _rms(y_r)
            yhat = y_r * rr
            dyn = dy_s[pl.ds(r, SUB), CONV_WIDTH:MIX_WIDTH]
            acc[ACC_GNR] += dyn * yhat
            dy_r = _rms_bwd(dyn, yhat, rr, g_r)
            du_s[pl.ds(r, SUB), OFF_G:OFF_G + LRU_WIDTH] = dy_r * hs_c * dge
            a_cum, d_cum = _scan8_rev(_up(a, a_later, 1, row_r), dy_r * ge, row_r)
            dh = a_cum * dh_later + d_cum
            dmult = dh * ii * xr
            dii = dh * mult * xr
            dxr_s[pl.ds(r, SUB), :] = dh * mult * ii
            dla = dh * h_m1 * a - dmult * a * a * inv_mult
            acc[ACC_SP] += -dla * ra
            dpa = -dla * sp_c * ra * (1.0 - ra)
            dpx = dii * ii * (1.0 - ii)
            acc[ACC_BA] += dpa
            acc[ACC_BX] += dpx
            dpa_s[pl.ds(r, SUB), :] = dpa
            dpx_s[pl.ds(r, SUB), :] = dpx
            return a, dh[0:1, :]

        a_first, dh_first = _chunk_loop(n_chunks, recur_bwd, (a_car[...], dh_car[...]), in_flight=8)
        a_car[...] = a_first
        dh_car[...] = dh_first

        dpab = dpa_s[...].astype(BF16)
        dpxb = dpx_s[...].astype(BF16)
        dxr_s[...] += _block_diag_apply_t(dpab, wabd) + _block_diag_apply_t(dpxb, wxbd)
        for g in range(LRU_WIDTH // GROUP):
            cols = slice(g * GROUP, (g + 1) * GROUP)
            dwa_acc[cols, :] += _dot_tn(xrb[:, cols], dpab[:, cols])
            dwx_acc[cols, :] += _dot_tn(xrb[:, cols], dpxb[:, cols])

        def convs_bwd(j, carry):
            dcq_later, dxr_later = carry
            i = n_chunks - 1 - j
            r = pl.multiple_of(i * SUB, SUB)
            rp = pl.multiple_of(jnp.maximum(i - 1, 0) * SUB, SUB)
            cv_prev = jnp.where(i == 0, cv_before,
                                u_ref[pl.ds(rp, SUB), OFF_GC:OFF_GC + CONV_WIDTH]
                                * u_ref[pl.ds(rp, SUB), OFF_V:OFF_V + CONV_WIDTH])
            gb, gc, v, cv, cv_m1, cv_m2, cq = _conv3_chunk(u_ref, r, cv_prev, cw, row_c)
            y_c = gb * cq
            rc = _rms(y_c)
            yhat = y_c * rc
            dyn = dy_s[pl.ds(r, SUB), 0:CONV_WIDTH]
            acc[ACC_GNC, :, 0:CONV_WIDTH] += dyn * yhat
            dy_c = _rms_bwd(dyn, yhat, rc, g_c)
            dcq = dy_c * gb
            dcv = (cw[2:3, :] * dcq + cw[1:2, :] * _up(dcq, dcq_later, 1, row_c)
                   + cw[0:1, :] * _up(dcq, dcq_later, 2, row_c))
            acc[ACC_CW + 2, :, 0:CONV_WIDTH] += dcq * cv
            acc[ACC_CW + 1, :, 0:CONV_WIDTH] += dcq * cv_m1
            acc[ACC_CW + 0, :, 0:CONV_WIDTH] += dcq * cv_m2
            du_s[pl.ds(r, SUB), OFF_GB:OFF_GB + CONV_WIDTH] = dy_c * cq
            du_s[pl.ds(r, SUB), OFF_GC:OFF_GC + CONV_WIDTH] = dcv * v
            du_s[pl.ds(r, SUB), OFF_V:OFF_V + CONV_WIDTH] = dcv * gc

            xin_prev = jnp.where(i == 0, xin_before, u_ref[pl.ds(rp, SUB), OFF_XR:OFF_XR + LRU_WIDTH])
            xin, m1, m2, m3, _ = _conv4_chunk(u_ref, r, xin_prev, rw, rb, row_r)
            dxr = dxr_s[pl.ds(r, SUB), :]
            du_s[pl.ds(r, SUB), OFF_XR:OFF_XR + LRU_WIDTH] = (
                rw[3:4, :] * dxr + rw[2:3, :] * _up(dxr, dxr_later, 1, row_r)
                + rw[1:2, :] * _up(dxr, dxr_later, 2, row_r) + rw[0:1, :] * _up(dxr, dxr_later, 3, row_r))
            acc[ACC_RW + 3] += dxr * xin
            acc[ACC_RW + 2] += dxr * m1
            acc[ACC_RW + 1] += dxr * m2
            acc[ACC_RW + 0] += dxr * m3
            acc[ACC_BR] += dxr
            return dcq, dxr

        dcq_first, dxr_first = _chunk_loop(n_chunks, convs_bwd, (dcq_car[...], dxr_car[...]), in_flight=8)
        dcq_car[...] = dcq_first
        dxr_car[...] = dxr_first

        du_ref[...] = du_s[...].astype(BF16)

        @pl.when(step == n_tiles - 1)
        def _():
            vec_ref[...] = jnp.zeros(vec_ref.shape, F32)
            rows = {ACC_GNC: ROW_GNC, ACC_GNR: ROW_GNR, ACC_BR: ROW_BR, ACC_BA: ROW_BA, ACC_BX: ROW_BX}
            for k in range(3):
                rows[ACC_CW + k] = ROW_CW + k
            for k in range(4):
                rows[ACC_RW + k] = ROW_RW + k
            for slot, out_row in rows.items():
                o = out_row - ROW_GNC
                vec_ref[o:o + 1, :] = jnp.sum(acc[slot], axis=0, keepdims=True)
            lam_v = lam_ref[...]
            dsp = jnp.sum(acc[ACC_SP], axis=0, keepdims=True)
            o = ROW_LAM - ROW_GNC
            vec_ref[o:o + 1, :] = -dsp * LRU_C / (1.0 + jnp.exp(lam_v))
            wab_ref[0:LRU_WIDTH, :] = _fold_heads(dwa_acc[...])
            wab_ref[LRU_WIDTH:2 * LRU_WIDTH, :] = _fold_heads(dwx_acc[...])

    rev = lambda w: pl.BlockSpec((tm, w), lambda s: (n_tiles - 1 - s, 0))
    before = lambda w: pl.BlockSpec((SUB, w), lambda s: (jnp.maximum((n_tiles - 1 - s) * per_tile - 1, 0), 0))
    whole = lambda a: pl.BlockSpec(a.shape, lambda s: (0,) * a.ndim)
    smalls = (conv_w, rnn_conv_w, rnn_conv_b, wa, b_a, wx, b_x, lam, gnc, gnr, w_out)
    full = lambda w: pltpu.VMEM((tm, w), F32)
    return pl.pallas_call(
        body, grid=(n_tiles,),
        in_specs=[rev(IN_COLS), before(IN_COLS), rev(LRU_WIDTH), before(LRU_WIDTH), rev(D_MODEL)]
        + [rev(LRU_WIDTH)] * len(saved) + [whole(a) for a in smalls] + [HBM_SPEC] * (n_sums + 1),
        out_specs=[rev(IN_COLS), pl.BlockSpec((16, D_MODEL), lambda s: (0, 0)),
                   pl.BlockSpec((2 * LRU_WIDTH, HEAD_DIM), lambda s: (0, 0))] + [HBM_SPEC] * (n_sums + 1),
        out_shape=[jax.ShapeDtypeStruct((t_len, IN_COLS), BF16), jax.ShapeDtypeStruct((16, D_MODEL), F32),
                   jax.ShapeDtypeStruct((2 * LRU_WIDTH, HEAD_DIM), F32)]
        + [jax.ShapeDtypeStruct(s.shape, BF16) for s in chip_sums]
        + [jax.ShapeDtypeStruct((4,) + g_wout.shape[1:], BF16)],
        scratch_shapes=[full(IN_COLS), full(MIX_WIDTH), full(LRU_WIDTH), full(LRU_WIDTH), full(LRU_WIDTH),
                        pltpu.VMEM((LRU_WIDTH, GROUP), BF16), pltpu.VMEM((LRU_WIDTH, GROUP), BF16),
                        pltpu.VMEM((N_ACC, SUB, LRU_WIDTH), F32),
                        pltpu.VMEM((LRU_WIDTH, GROUP), F32), pltpu.VMEM((LRU_WIDTH, GROUP), F32),
                        pltpu.VMEM((SUB, LRU_WIDTH), F32), pltpu.VMEM((1, LRU_WIDTH), F32),
                        pltpu.VMEM((SUB, CONV_WIDTH), F32), pltpu.VMEM((SUB, LRU_WIDTH), F32)]
        + _exchange_scratch(n_sums, 3) + _exchange_scratch(1, 4),
        compiler_params=_params(("arbitrary",), 56), name="mixer_bwd",
    )(u, u, hs, hs, dx1, *saved, *smalls, *chip_sums, g_wout)


def _in_proj_bwd(du, dx1, x, g_mix, win_t, tm, chip_sums, g_own):
    t_len = x.shape[0]
    n_steps = t_len // tm

    def body(du_ref, dx1_ref, x_ref, g_ref, w_ref, hs_ref, gown_ref,
             dx_ref, vec_ref, landed_ref, sib_ref, i_send, i_recv, d_send, d_recv):
        step = pl.program_id(0)
        _host_chip_exchange(step, n_steps, [hs_ref], [landed_ref], i_send, i_recv)
        _host_half_exchange(step, n_steps, gown_ref, sib_ref, d_send, d_recv)

        @pl.when(step == 0)
        def _():
            vec_ref[...] = jnp.zeros(vec_ref.shape, F32)

        dh = jnp.dot(du_ref[...], w_ref[...], preferred_element_type=F32)
        xv = x_ref[...]
        r1 = _rms(xv)
        xh = xv * r1
        vec_ref[0:1, :] += jnp.sum(dh * xh, axis=0, keepdims=True)
        dx_ref[...] = dx1_ref[...] + _rms_bwd(dh, xh, r1, g_ref[...])

    row_tile = lambda w: pl.BlockSpec((tm, w), lambda i: (i, 0))
    half_shape = (g_own.shape[0], g_own.shape[1] // 2, g_own.shape[2])
    return pl.pallas_call(
        body, grid=(n_steps,),
        in_specs=[row_tile(IN_COLS), row_tile(D_MODEL), row_tile(D_MODEL), pl.BlockSpec((1, D_MODEL), lambda i: (0, 0)),
                  pl.BlockSpec((IN_COLS, D_MODEL), lambda i: (0, 0))] + [HBM_SPEC] * 2,
        out_specs=[row_tile(D_MODEL), pl.BlockSpec((SUB, D_MODEL), lambda i: (0, 0))] + [HBM_SPEC] * 2,
        out_shape=[jax.ShapeDtypeStruct((t_len, D_MODEL), F32), jax.ShapeDtypeStruct((SUB, D_MODEL), F32),
                   jax.ShapeDtypeStruct(chip_sums.shape, BF16), jax.ShapeDtypeStruct(half_shape, BF16)],
        scratch_shapes=_exchange_scratch(1, 3) + [pltpu.SemaphoreType.DMA((1,)), pltpu.SemaphoreType.DMA((1,))],
        compiler_params=_params(("arbitrary",), 56), name="in_proj_bwd",
    )(du, dx1, x, g_mix, win_t, chip_sums, g_own)


def _tn_weight_grad(a, b, tk, name, pair=(), col_blocks=1):
    t_len, m = a.shape
    n = b.shape[1]
    n_steps = t_len // tk
    sent = tuple(pair)
    n_sent = len(sent)

    def body(a_ref, b_ref, *rest):
        srcs = rest[0:n_sent]
        o_ref = rest[n_sent]
        dsts = rest[n_sent + 1:2 * n_sent + 1]
        acc = rest[2 * n_sent + 1]
        sems = rest[2 * n_sent + 2:]
        j = pl.program_id(0)
        if pair:
            _host_pair_exchange(j, n_steps, srcs, dsts, *sems)

        @pl.when(j == 0)
        def _():
            acc[...] = jnp.zeros(acc.shape, F32)

        acc[...] += _dot_tn(a_ref[...].astype(BF16), b_ref[...].astype(BF16))

        @pl.when(j == n_steps - 1)
        def _():
            if col_blocks == 1:
                o_ref[...] = acc[...].astype(BF16)
            else:
                for k in range(col_blocks):
                    o_ref[k] = acc[:, k * nb:(k + 1) * nb].astype(BF16)

    nb = n // col_blocks
    out_dims = (m, n) if col_blocks == 1 else (col_blocks, m, nb)
    landed = [jax.ShapeDtypeStruct((4,) + g.shape[1:], BF16) for g in pair]
    scratch = [pltpu.VMEM((m, n), F32)]
    if n_sent:
        scratch += _exchange_scratch(n_sent, 4)
    return pl.pallas_call(
        body, grid=(n_steps,),
        in_specs=[pl.BlockSpec((tk, m), lambda j: (j, 0)), pl.BlockSpec((tk, n), lambda j: (j, 0))]
        + [HBM_SPEC] * n_sent,
        out_specs=[pl.BlockSpec(out_dims, lambda j: (0,) * len(out_dims))] + [HBM_SPEC] * n_sent,
        out_shape=[jax.ShapeDtypeStruct(out_dims, BF16)] + landed,
        scratch_shapes=scratch,
        compiler_params=_params(("arbitrary",), 56), name=name,
    )(a, b, *sent)


def _w_in_grad_part(du, h, tk, name, chip_ids, chip=(), halves=None, small=None):
    t_len = du.shape[0]
    n_t = t_len // tk
    n_q = chip_ids.shape[0]
    width = 2 * (IN_COLS // N_DEV)
    n_steps = n_q * n_t
    n_chip = len(chip)
    sent = tuple(chip) + (() if halves is None else (halves,)) + (() if small is None else tuple(small))
    n_sent = len(sent)

    def body(ids_ref, a_ref, b_ref, *rest):
        srcs = rest[0:n_sent]
        o_ref = rest[n_sent]
        dsts = rest[n_sent + 1:2 * n_sent + 1]
        acc = rest[2 * n_sent + 1]
        sems = list(rest[2 * n_sent + 2:])
        j = pl.program_id(1)
        step = pl.program_id(0) * n_t + j
        if chip:
            _host_chip_exchange(step, n_steps, srcs[0:n_chip], dsts[0:n_chip], sems.pop(0), sems.pop(0))
        if halves is not None:
            _host_half_exchange(step, n_steps, srcs[n_chip], dsts[n_chip], sems.pop(0), sems.pop(0))
        if small is not None:
            _host_small_exchange(step, n_steps, *srcs[n_sent - 3:], *dsts[n_sent - 3:], *sems)

        @pl.when(j == 0)
        def _():
            acc[...] = jnp.zeros(acc.shape, F32)

        acc[...] += _dot_tn(a_ref[...], b_ref[...])

        @pl.when(j == n_t - 1)
        def _():
            o_ref[0] = acc[...].astype(BF16)

    landed = [jax.ShapeDtypeStruct(s.shape, BF16) for s in chip]
    scratch = [pltpu.VMEM((width, D_MODEL), F32)]
    if chip:
        scratch += _exchange_scratch(len(chip), 3)
    if halves is not None:
        landed.append(jax.ShapeDtypeStruct((halves.shape[0], halves.shape[1] // 2, halves.shape[2]), BF16))
        scratch += [pltpu.SemaphoreType.DMA((halves.shape[0],)), pltpu.SemaphoreType.DMA((halves.shape[0],))]
    if small is not None:
        vec_m, vec_b, wab = small
        landed += [jax.ShapeDtypeStruct((N_DEV,) + vec_m.shape, F32), jax.ShapeDtypeStruct((N_DEV,) + vec_b.shape, F32),
                   jax.ShapeDtypeStruct((N_DEV, wab.shape[0] // N_DEV, wab.shape[1]), F32)]
        scratch += _exchange_scratch(3, N_DEV) + [pltpu.SemaphoreType.DMA((2,))]
    grid_spec = pltpu.PrefetchScalarGridSpec(
        num_scalar_prefetch=1, grid=(n_q, n_t),
        in_specs=[pl.BlockSpec((tk, width), lambda q, j, ids: (j, ids[q])),
                  pl.BlockSpec((tk, D_MODEL), lambda q, j, ids: (j, 0))] + [HBM_SPEC] * n_sent,
        out_specs=[pl.BlockSpec((1, width, D_MODEL), lambda q, j, ids: (q, 0, 0))] + [HBM_SPEC] * n_sent,
        scratch_shapes=scratch)
    return pl.pallas_call(
        body, grid_spec=grid_spec, out_shape=[jax.ShapeDtypeStruct((n_q, width, D_MODEL), BF16)] + landed,
        compiler_params=_params(("arbitrary", "arbitrary"), 40), name=name,
    )(chip_ids, du, h, *sent)


def _adamw(w, g, m, v):
    m = ADAM_B1 * m + (1.0 - ADAM_B1) * g
    v = ADAM_B2 * v + (1.0 - ADAM_B2) * (g * g)
    delta = -ADAM_LR * ((m / BC1) / (jnp.sqrt(v / BC2) + ADAM_EPS) + ADAM_WD * w)
    return delta, m, v


def _update_sharded(g, landed, w, m, v, rows_blk, name):
    rows, cols = w.shape

    def body(g_ref, l_ref, w_ref, m_ref, v_ref, og, od, om, ov):
        gv = g_ref[...]
        for j in range(3):
            gv = gv + l_ref[j].astype(F32)
        delta, mn, vn = _adamw(w_ref[...], gv, m_ref[...], v_ref[...])
        og[...] = gv
        od[...] = delta
        om[...] = mn
        ov[...] = vn

    blk = pl.BlockSpec((rows_blk, cols), lambda i: (i, 0))
    shape = pltpu.HBM((rows, cols), F32)
    return pl.pallas_call(
        body, grid=(rows // rows_blk,),
        in_specs=[blk, pl.BlockSpec((3, rows_blk, cols), lambda i: (0, i, 0)), blk, blk, blk],
        out_specs=[blk] * 4, out_shape=[shape] * 4,
        compiler_params=_params(("arbitrary",), 32), name=name,
    )(*_in_hbm(g, landed, w, m, v))


def _update_w_in(g_own, sib_own, landed, w_t, m_t, v_t, core, cols_blk):
    rows, cols = w_t.shape

    def body(core_ref, g_ref, s_ref, l_ref, w_ref, m_ref, v_ref, og, od, om, ov):
        gv = g_ref[0, 0].astype(F32) + s_ref[0].astype(F32)
        for j in range(3):
            gv = gv + l_ref[j].astype(F32)
        delta, mn, vn = _adamw(w_ref[...], gv, m_ref[...], v_ref[...])
        og[...] = gv
        od[...] = delta
        om[...] = mn
        ov[...] = vn

    blk = pl.BlockSpec((rows, cols_blk), lambda i, cr: (0, i))
    grid_spec = pltpu.PrefetchScalarGridSpec(
        num_scalar_prefetch=1, grid=(cols // cols_blk,),
        in_specs=[pl.BlockSpec((1, 1, rows, cols_blk), lambda i, cr: (0, cr[0], 0, i)),
                  pl.BlockSpec((1, rows, cols_blk), lambda i, cr: (0, 0, i)),
                  pl.BlockSpec((3, rows, cols_blk), lambda i, cr: (0, 0, i)), blk, blk, blk],
        out_specs=[blk] * 4)
    return pl.pallas_call(
        body, grid_spec=grid_spec, out_shape=[pltpu.HBM((rows, cols), F32)] * 4,
        compiler_params=_params(("arbitrary",), 32), name="update_w_in",
    )(core, *_in_hbm(g_own.reshape(1, 2, rows, cols), sib_own, landed, w_t, m_t, v_t))


def _update_small(vsum, wsum, g_cw, g_rw, weights, moments_m, moments_v):
    n = len(weights)

    def body(*refs):
        vs, ws, gcw, grw = refs[0:4]
        w_refs = refs[4:4 + n]
        m_refs = refs[4 + n:4 + 2 * n]
        v_refs = refs[4 + 2 * n:4 + 3 * n]
        outs = refs[4 + 3 * n:]
        loss_ref = outs[0]
        loss_ref[...] = jnp.sum(vs[ROW_LOSS:ROW_LOSS + 1, :], axis=1, keepdims=True)
        grads = [
            vs[ROW_GMIX:ROW_GMIX + 1, :], gcw[...], grw[...], vs[ROW_BR:ROW_BR + 1, :],
            ws[0:LRU_WIDTH, :], vs[ROW_BA:ROW_BA + 1, :], ws[LRU_WIDTH:2 * LRU_WIDTH, :], vs[ROW_BX:ROW_BX + 1, :],
            vs[ROW_LAM:ROW_LAM + 1, :], vs[ROW_GNC:ROW_GNC + 1, 0:CONV_WIDTH], vs[ROW_GNR:ROW_GNR + 1, :],
            vs[ROW_GMLP:ROW_GMLP + 1, :], vs[ROW_GF:ROW_GF + 1, :],
        ]
        for k in range(n):
            gk = grads[k]
            delta, mn, vn = _adamw(w_refs[k][...], gk, m_refs[k][...], v_refs[k][...])
            outs[1 + 4 * k][...] = gk
            outs[2 + 4 * k][...] = delta
            outs[3 + 4 * k][...] = mn
            outs[4 + 4 * k][...] = vn

    whole = lambda a: pl.BlockSpec(a.shape, lambda i: (0,) * len(a.shape))
    out_shape = [jax.ShapeDtypeStruct((1, 1), F32)]
    for w in weights:
        out_shape += [jax.ShapeDtypeStruct(w.shape, F32)] * 4
    args = (vsum, wsum, g_cw, g_rw, *weights, *moments_m, *moments_v)
    return pl.pallas_call(
        body, grid=(1,), out_shape=out_shape, in_specs=[whole(a) for a in args], out_specs=[whole(s) for s in out_shape],
        compiler_params=_params(("arbitrary",), 32), name="update_small",
    )(*args)


def kernel(x, norm_mix_g, w_in, conv_w, rnn_conv_w, rnn_conv_b, w_a, b_a, w_x, b_x, lru_lambda, g_norm_conv, g_norm_rnn, w_out, norm_mlp_g, w_mlp_in, w_mlp_out, final_norm_g, loss_target, m_norm_mix_g, m_w_in, m_conv_w, m_rnn_conv_w, m_rnn_conv_b, m_w_a, m_b_a, m_w_x, m_b_x, m_lru_lambda, m_g_norm_conv, m_g_norm_rnn, m_w_out, m_norm_mlp_g, m_w_mlp_in, m_w_mlp_out, m_final_norm_g, v_norm_mix_g, v_w_in, v_conv_w, v_rnn_conv_w, v_rnn_conv_b, v_w_a, v_b_a, v_w_x, v_b_x, v_lru_lambda, v_g_norm_conv, v_g_norm_rnn, v_w_out, v_norm_mlp_g, v_w_mlp_in, v_w_mlp_out, v_final_norm_g):
    t_len = x.shape[1]
    my_id = 4 * lax.axis_index("x") + 2 * lax.axis_index("y") + lax.axis_index("c")
    tm = min(256, t_len)
    tb = min(512, t_len)
    tk = min(512, t_len)

    xs = x.reshape(t_len, D_MODEL)
    tgt = loss_target.reshape(t_len, D_MODEL)
    flat = lambda a: a.reshape(a.shape[-2:]) if a.ndim == 3 else a.reshape(1, -1)
    heads = lambda a: a.reshape(LRU_WIDTH, HEAD_DIM)

    turned = lambda a: jnp.transpose(flat(a))
    win_shard, wout_shard, w1_shard, w2_shard, cp_shard = _prep_shards(
        turned(w_in), flat(w_out), flat(w_mlp_in), flat(w_mlp_out), flat(conv_w), flat(rnn_conv_w))

    u, h, win_t, cp_full = _in_proj(xs, flat(norm_mix_g), (win_shard, cp_shard), min(1024, t_len))
    cpack = cp_full.reshape(N_DEV, 8, 128)
    conv_full = jnp.transpose(cpack[:, 0:3, 0:64], (1, 0, 2)).reshape(3, CONV_WIDTH)
    rnn_full = jnp.transpose(cpack[:, 3:7, :], (1, 0, 2)).reshape(4, LRU_WIDTH)
    mixer_small = (conv_full, rnn_full, flat(rnn_conv_b), heads(w_a), flat(b_a), heads(w_x), flat(b_x),
                   flat(lru_lambda), flat(g_norm_conv), flat(g_norm_rnn))
    hs, y, xr, gate_r, gate_i, mult, w1_blk, wout_blk = _mixer_fwd(u, *mixer_small, (w1_shard, wout_shard), tm)
    wout_f = wout_blk.reshape(MIX_WIDTH, D_MODEL)
    x1, h2, z, w2_blk = _mlp_up(xs, y, flat(norm_mlp_g), wout_f, w1_blk, w2_shard, tb)
    dx1, dx2, vec_m, dpre = _mlp_down_bwd(x1, z, tgt, flat(norm_mlp_g), flat(final_norm_g), w1_blk,
                                          w2_blk.reshape(D_FF, D_MODEL), tb)
    (g_w1,) = _tn_weight_grad(h2, dpre, tk, "w_mlp_in_grad", col_blocks=N_DEV)
    (g_w2,) = _tn_weight_grad(z, dx2, tk, "w_mlp_out_grad")
    g_w2 = g_w2.reshape(N_DEV, D_FF // N_DEV, D_MODEL)
    g_wout, sib_w1, sib_w2 = _tn_weight_grad(y, dx1, min(1024, t_len), "w_out_grad", pair=(g_w1, g_w2))
    g_wout = g_wout.reshape(N_DEV, MIX_WIDTH // N_DEV, D_MODEL)
    hsend_w1, own_w1 = _pair_sum(g_w1, sib_w1, "pair_sum_w_mlp_in")
    hsend_w2, own_w2 = _pair_sum(g_w2, sib_w2, "pair_sum_w_mlp_out")
    du, vec_b, wab, landed_w1, landed_w2, sib_wout = _mixer_bwd(
        u, hs, dx1, (xr, gate_r, gate_i, mult), *mixer_small, wout_f, (hsend_w1, hsend_w2), g_wout, tm)
    hsend_wout, own_wout = _pair_sum(g_wout, sib_wout, "pair_sum_w_out")
    ax, ay, ac = lax.axis_index("x"), lax.axis_index("y"), lax.axis_index("c")
    chip_ids = jnp.stack([2 * cx + cy for cx, cy in [(ax, ay)] + _other_chips(ax, ay)]).astype(jnp.int32)
    core = jnp.reshape(ac, (1,)).astype(jnp.int32)
    tw = min(1024, t_len)
    g_others, landed_wout, vrecv_m, vrecv_b, wrecv = _w_in_grad_part(
        du, h, tw, "w_in_grad_others", chip_ids[1:4], chip=(hsend_wout,), small=(vec_m, vec_b, wab))
    g_own, sib_others = _w_in_grad_part(du, h, tw, "w_in_grad_own", chip_ids[0:1], halves=g_others)
    hsend_win = _pair_sum_parts(g_others, sib_others, core)
    grad_x, vec_x, landed_win, sib_own = _in_proj_bwd(du, dx1, xs, flat(norm_mix_g), win_t, tm, hsend_win, g_own)

    vsum, wsum = _final_small(vrecv_m, vrecv_b, wab, wrecv, vec_x)

    up_win = _update_w_in(g_own, sib_own, landed_win, turned(w_in), turned(m_w_in), turned(v_w_in), core, 256)
    up_win = [jnp.transpose(a) for a in up_win]
    up_wout = _update_sharded(own_wout, landed_wout, flat(w_out), flat(m_w_out), flat(v_w_out), 96, "update_w_out")
    up_w1 = _update_sharded(own_w1, landed_w1, flat(w_mlp_in), flat(m_w_mlp_in), flat(v_w_mlp_in), 256,
                            "update_w_mlp_in")
    up_w2 = _update_sharded(own_w2, landed_w2, flat(w_mlp_out), flat(m_w_mlp_out), flat(v_w_mlp_out), 256,
                            "update_w_mlp_out")

    g_cw = lax.dynamic_slice(vsum, (ROW_CW, 64 * my_id), (3, 64))
    g_rw = lax.dynamic_slice(vsum, (ROW_RW, 128 * my_id), (4, 128))
    small_w = (norm_mix_g, conv_w, rnn_conv_w, rnn_conv_b, w_a, b_a, w_x, b_x, lru_lambda, g_norm_conv, g_norm_rnn,
               norm_mlp_g, final_norm_g)
    small_m = (m_norm_mix_g, m_conv_w, m_rnn_conv_w, m_rnn_conv_b, m_w_a, m_b_a, m_w_x, m_b_x, m_lru_lambda,
               m_g_norm_conv, m_g_norm_rnn, m_norm_mlp_g, m_final_norm_g)
    small_v = (v_norm_mix_g, v_conv_w, v_rnn_conv_w, v_rnn_conv_b, v_w_a, v_b_a, v_w_x, v_b_x, v_lru_lambda,
               v_g_norm_conv, v_g_norm_rnn, v_norm_mlp_g, v_final_norm_g)
    is_heads = (False, False, False, False, True, False, True, False, False, False, False, False, False)
    as2d = lambda arrs: [heads(a) if hd else flat(a) for a, hd in zip(arrs, is_heads)]
    small_out = _update_small(vsum, wsum, g_cw, g_rw, as2d(small_w), as2d(small_m), as2d(small_v))
    loss = small_out[0].reshape(())

    names = ["norm_mix_g", "w_in", "conv_w", "rnn_conv_w", "rnn_conv_b", "w_a", "b_a", "w_x", "b_x", "lru_lambda",
             "g_norm_conv", "g_norm_rnn", "w_out", "norm_mlp_g", "w_mlp_in", "w_mlp_out", "final_norm_g"]
    originals = dict(zip(names, (norm_mix_g, w_in, conv_w, rnn_conv_w, rnn_conv_b, w_a, b_a, w_x, b_x, lru_lambda,
                                 g_norm_conv, g_norm_rnn, w_out, norm_mlp_g, w_mlp_in, w_mlp_out, final_norm_g)))
    results = {"w_in": up_win, "w_out": up_wout, "w_mlp_in": up_w1, "w_mlp_out": up_w2}
    small_names = ["norm_mix_g", "conv_w", "rnn_conv_w", "rnn_conv_b", "w_a", "b_a", "w_x", "b_x", "lru_lambda",
                   "g_norm_conv", "g_norm_rnn", "norm_mlp_g", "final_norm_g"]
    for k, nm in enumerate(small_names):
        results[nm] = small_out[1 + 4 * k:5 + 4 * k]
    out = [loss, grad_x.reshape(x.shape)]
    for kind in range(4):
        out += [results[nm][kind].reshape(originals[nm].shape) for nm in names]
    return tuple(out)
```

```python
import functools

import jax
import jax.numpy as jnp
from jax import lax
from jax.experimental import pallas as pl
from jax.experimental.pallas import tpu as pltpu

F32 = jnp.float32
BF16 = jnp.bfloat16

D_MODEL = 1024
HEAD_DIM = 64
CONV_WIDTH = 512
LRU_WIDTH = 1024
MIX_WIDTH = CONV_WIDTH + LRU_WIDTH
IN_COLS = 3 * CONV_WIDTH + 2 * LRU_WIDTH
D_FF = 4 * D_MODEL
GROUP = 256
EPS = 1e-6
LRU_C = 8.0
N_DEV = 8
SUB = 8

OFF_GB, OFF_GC, OFF_V, OFF_XR, OFF_G = 0, 512, 1024, 1536, 2560

ADAM_LR, ADAM_B1, ADAM_B2, ADAM_EPS, ADAM_WD, ADAM_STEP = 0.001, 0.9, 0.999, 1e-08, 0.01, 10
BC1 = 1.0 - ADAM_B1 ** ADAM_STEP
BC2 = 1.0 - ADAM_B2 ** ADAM_STEP

MIB = 1024 * 1024
MESH = pl.DeviceIdType.MESH

VEC_ROWS = 32
ROW_GF, ROW_GMLP, ROW_LOSS = 0, 1, 2
ROW_GNC, ROW_GNR, ROW_BR, ROW_BA, ROW_BX, ROW_LAM, ROW_CW, ROW_RW = 8, 9, 10, 11, 12, 13, 14, 17
ROW_GMIX = 24
ACC_GNC, ACC_GNR, ACC_BR, ACC_BA, ACC_BX, ACC_SP, ACC_CW, ACC_RW, N_ACC = 0, 1, 2, 3, 4, 5, 6, 9, 13


def _params(semantics=None, vmem_mib=48):
    return pltpu.CompilerParams(dimension_semantics=semantics, vmem_limit_bytes=vmem_mib * MIB)


def _rms(x):
    return lax.rsqrt(jnp.mean(x * x, axis=-1, keepdims=True) + EPS)


def _rms_bwd(dy, xhat, r, g):
    dyh = dy * g
    return r * (dyh - xhat * jnp.mean(dyh * xhat, axis=-1, keepdims=True))


def _sigmoid(x):
    return 0.5 + 0.5 * jnp.tanh(0.5 * x)


def _gelu(x):
    c0, c1 = 0.7978845608028654, 0.044715
    x2 = x * x
    t = jnp.tanh(x * (c0 + (c0 * c1) * x2))
    half = 0.5 + 0.5 * t
    ge = x * half
    dge = half + (ge - ge * half) * (2.0 * c0 + (6.0 * c0 * c1) * x2)
    return ge, dge


def _softplus_neg(lam):
    z = -lam
    e = jnp.exp(-jnp.abs(z))
    return jnp.maximum(z, 0.0) + jnp.where(e < 1e-4, e * (1.0 - 0.5 * e), jnp.log(1.0 + e))


def _lru_gates(pa, px, sp_c):
    ra = _sigmoid(pa)
    ii = _sigmoid(px)
    la = -ra * sp_c
    a = jnp.exp(la)
    x2 = 2.0 * la
    series = -x2 * (1.0 + x2 * (0.5 + x2 * (1.0 / 6.0 + x2 * (1.0 / 24.0))))
    m2 = jnp.where(x2 > -0.01, series, 1.0 - a * a)
    inv_mult = lax.rsqrt(m2)
    mult = jnp.where(m2 > 0.0, m2 * inv_mult, 0.0)
    return ra, ii, a, mult, inv_mult


def _down(cur, prev, s, row):
    return pltpu.roll(jnp.where(row < SUB - s, cur, prev), s, 0)


def _up(cur, nxt, s, row):
    return pltpu.roll(jnp.where(row >= s, cur, nxt), SUB - s, 0)


def _scan8_fwd(a, b, row):
    for s in (1, 2, 4):
        m = row >= s
        a_sh = pltpu.roll(a, s, 0)
        b_sh = pltpu.roll(b, s, 0)
        b = jnp.where(m, a * b_sh + b, b)
        a = jnp.where(m, a * a_sh, a)
    return a, b


def _scan8_rev(a, b, row):
    for s in (1, 2, 4):
        m = row < SUB - s
        a_sh = pltpu.roll(a, SUB - s, 0)
        b_sh = pltpu.roll(b, SUB - s, 0)
        b = jnp.where(m, a * b_sh + b, b)
        a = jnp.where(m, a * a_sh, a)
    return a, b


def _group_mask(shape):
    r = lax.broadcasted_iota(jnp.int32, shape, 0)
    c = lax.broadcasted_iota(jnp.int32, shape, 1)
    return ((r % GROUP) // HEAD_DIM) == (c // HEAD_DIM)


def _expand_heads(w):
    j = lax.broadcasted_iota(jnp.int32, (HEAD_DIM, GROUP), 0)
    c = lax.broadcasted_iota(jnp.int32, (HEAD_DIM, GROUP), 1)
    spread = (c % HEAD_DIM == j).astype(BF16)
    e = jnp.dot(w.astype(BF16), spread, preferred_element_type=F32)
    return jnp.where(_group_mask(e.shape), e, 0.0).astype(BF16)


def _fold_heads(p):
    p = jnp.where(_group_mask(p.shape), p, 0.0)
    c = lax.broadcasted_iota(jnp.int32, (GROUP, HEAD_DIM), 0)
    j = lax.broadcasted_iota(jnp.int32, (GROUP, HEAD_DIM), 1)
    fold = (c % HEAD_DIM == j).astype(BF16)
    hi = p.astype(BF16)
    rest = p - hi.astype(F32)
    mid = rest.astype(BF16)
    lo = (rest - mid.astype(F32)).astype(BF16)
    dot = functools.partial(jnp.dot, preferred_element_type=F32)
    return dot(hi, fold) + dot(mid, fold) + dot(lo, fold)


def _block_diag_apply(xb, wbd_ref):
    parts = [jnp.dot(xb[:, g * GROUP:(g + 1) * GROUP], wbd_ref[g * GROUP:(g + 1) * GROUP, :],
                     preferred_element_type=F32) for g in range(LRU_WIDTH // GROUP)]
    return jnp.concatenate(parts, axis=1)


def _block_diag_apply_t(db, wbd_ref):
    parts = [lax.dot_general(db[:, g * GROUP:(g + 1) * GROUP], wbd_ref[g * GROUP:(g + 1) * GROUP, :],
                             (((1,), (1,)), ((), ())), preferred_element_type=F32)
             for g in range(LRU_WIDTH // GROUP)]
    return jnp.concatenate(parts, axis=1)


def _dot_nt(a, b):
    return lax.dot_general(a, b, (((1,), (1,)), ((), ())), preferred_element_type=F32)


def _dot_tn(a, b):
    return lax.dot_general(a, b, (((0,), (0,)), ((), ())), preferred_element_type=F32)


def _chunk_loop(n_chunks, chunk, init, in_flight=4):
    def body(k, carry):
        for j in range(in_flight):
            carry = chunk(k * in_flight + j, carry)
        return carry

    return lax.fori_loop(0, n_chunks // in_flight, body, init)


def _place():
    x, y, c = lax.axis_index("x"), lax.axis_index("y"), lax.axis_index("c")
    return x, y, c


def _block_id(chip, core):
    return 4 * chip[0] + 2 * chip[1] + core


def _other_chips(x, y):
    return [(1 - x, y), (x, 1 - y), (1 - x, 1 - y)]


def _remote_copy(src, dst, send_sem, recv_sem, to):
    return pltpu.make_async_remote_copy(src_ref=src, dst_ref=dst, send_sem=send_sem, recv_sem=recv_sem,
                                        device_id=to, device_id_type=MESH)


HBM_SPEC = pl.BlockSpec(memory_space=pl.ANY)


def _in_hbm(*arrays):
    return [pltpu.with_memory_space_constraint(a, pltpu.HBM) for a in arrays]


def _prep_shards(w_in_t, w_out, w_mlp_in, w_mlp_out, conv_w, rnn_conv_w):
    def body(win_ref, wout_ref, w1_ref, w2_ref, cw_ref, rw_ref, o_win, o_wout, o_w1, o_w2, o_cp):
        o_win[...] = win_ref[...].astype(BF16)
        o_wout[...] = wout_ref[...].astype(BF16)
        o_w1[...] = w1_ref[...].astype(BF16)
        o_w2[...] = w2_ref[...].astype(BF16)
        o_cp[...] = jnp.zeros(o_cp.shape, F32)
        o_cp[0:3, 0:64] = cw_ref[...]
        o_cp[3:7, :] = rw_ref[...]

    whole = lambda shape: pl.BlockSpec(shape, lambda i: (0,) * len(shape))
    args = (w_in_t, w_out, w_mlp_in, w_mlp_out, conv_w, rnn_conv_w)
    shapes = [(w_in_t.shape, BF16), (w_out.shape, BF16), (w_mlp_in.shape, BF16), (w_mlp_out.shape, BF16),
              ((8, 128), F32)]
    return pl.pallas_call(
        body, grid=(1,), out_shape=[jax.ShapeDtypeStruct(s, d) for s, d in shapes],
        in_specs=[whole(a.shape) for a in args], out_specs=[whole(s) for s, _ in shapes],
        compiler_params=_params(("arbitrary",), 40), name="prep_shards",
    )(*args)


def _host_all_gather(step, n_steps, shards, fulls, send_sems, recv_sems, local_sems):
    x, y, c = _place()
    me = (x, y, c)
    my_id = _block_id((x, y), c)
    sibling = (x, y, 1 - c)
    chips = _other_chips(x, y)
    n_arr = len(shards)

    def copy(arr, k, block, to, src=None):
        dst = fulls[arr].at[block]
        return _remote_copy(dst if src is None else src, dst, send_sems.at[arr, k], recv_sems.at[arr, k], to)

    def local(arr):
        return pltpu.make_async_copy(shards[arr], fulls[arr].at[my_id], local_sems.at[arr])

    @pl.when(step == 0)
    def _():
        for arr in range(n_arr):
            local(arr).start()
            copy(arr, 0, my_id, sibling, shards[arr]).start()
            for j, chip in enumerate(chips):
                copy(arr, 1 + j, my_id, (*chip, c), shards[arr]).start()

    @pl.when(step == max(n_steps - 2, 0))
    def _():
        for j, chip in enumerate(chips):
            for arr in range(n_arr):
                copy(arr, 1 + j, _block_id(chip, c), me).wait_recv()
                copy(arr, 4 + j, _block_id(chip, c), sibling).start()

    @pl.when(step == n_steps - 1)
    def _():
        for arr in range(n_arr):
            copy(arr, 0, _block_id((x, y), 1 - c), me).wait_recv()
            for j, chip in enumerate(chips):
                copy(arr, 4 + j, _block_id(chip, 1 - c), me).wait_recv()
            for k in range(4):
                copy(arr, k, my_id, me, shards[arr]).wait_send()
            for j, chip in enumerate(chips):
                copy(arr, 4 + j, _block_id(chip, c), me).wait_send()
            local(arr).wait()


def _host_pair_exchange(step, n_steps, gs, sibs, send_sems, recv_sems):
    x, y, c = _place()
    sibling = (x, y, 1 - c)
    chips = [(x, y)] + _other_chips(x, y)

    def d2d(arr, q):
        return _remote_copy(gs[arr].at[_block_id(chips[q], 1 - c)], sibs[arr].at[q],
                            send_sems.at[arr, q], recv_sems.at[arr, q], sibling)

    @pl.when(step == 0)
    def _():
        for arr in range(len(gs)):
            for q in (1, 2, 3, 0):
                d2d(arr, q).start()

    @pl.when(step == n_steps - 1)
    def _():
        for arr in range(len(gs)):
            for q in range(4):
                d2d(arr, q).wait()


def _host_chip_exchange(step, n_steps, hsends, hrecvs, send_sems, recv_sems):
    x, y, c = _place()
    chips = _other_chips(x, y)

    def ici(arr, j):
        return _remote_copy(hsends[arr].at[j], hrecvs[arr].at[j], send_sems.at[arr, j], recv_sems.at[arr, j],
                            (*chips[j], c))

    @pl.when(step == 0)
    def _():
        for arr in range(len(hsends)):
            for j in range(3):
                ici(arr, j).start()

    @pl.when(step == n_steps - 1)
    def _():
        for arr in range(len(hsends)):
            for j in range(3):
                ici(arr, j).wait()


def _host_half_exchange(step, n_steps, parts, sibs, send_sems, recv_sems):
    x, y, c = _place()
    n_q, rows2, _ = parts.shape
    half = rows2 // 2

    def d2d(q):
        src = parts.at[q, pl.ds(pl.multiple_of((1 - c) * half, 16), half), :]
        return _remote_copy(src, sibs.at[q], send_sems.at[q], recv_sems.at[q], (x, y, 1 - c))

    @pl.when(step == 0)
    def _():
        for q in range(n_q):
            d2d(q).start()

    @pl.when(step == n_steps - 1)
    def _():
        for q in range(n_q):
            d2d(q).wait()


def _peer(x, y, c, k):
    return (x ^ ((k >> 2) & 1), y ^ ((k >> 1) & 1), c ^ (k & 1))


def _host_small_exchange(step, n_steps, vec_m, vec_b, wab, vrecv_m, vrecv_b, wrecv, send_sems, recv_sems, local_sems):
    x, y, c = _place()
    my_id = _block_id((x, y), c)
    wrows = wab.shape[0] // N_DEV

    def copies(k):
        to = _peer(x, y, c, k)
        block = wab.at[pl.ds(pl.multiple_of(_block_id(to[0:2], to[2]) * wrows, SUB), wrows), :]
        return [_remote_copy(vec_m, vrecv_m.at[my_id], send_sems.at[0, k], recv_sems.at[0, k], to),
                _remote_copy(vec_b, vrecv_b.at[my_id], send_sems.at[1, k], recv_sems.at[1, k], to),
                _remote_copy(block, wrecv.at[k], send_sems.at[2, k], recv_sems.at[2, k], to)]

    mine = [pltpu.make_async_copy(vec_m, vrecv_m.at[my_id], local_sems.at[0]),
            pltpu.make_async_copy(vec_b, vrecv_b.at[my_id], local_sems.at[1])]

    @pl.when(step == 0)
    def _():
        for cp in mine:
            cp.start()
        for k in range(1, N_DEV):
            for cp in copies(k):
                cp.start()

    @pl.when(step == n_steps - 1)
    def _():
        for k in range(1, N_DEV):
            for cp in copies(k):
                cp.wait()
        for cp in mine:
            cp.wait()


def _pair_sum_parts(parts, sibs, core):
    n_q, rows2, cols = parts.shape
    half = rows2 // 2

    def body(core_ref, g_ref, s_ref, o_ref):
        o_ref[0] = (g_ref[0, 0].astype(F32) + s_ref[0].astype(F32)).astype(BF16)

    block = (1, half, cols)
    grid_spec = pltpu.PrefetchScalarGridSpec(
        num_scalar_prefetch=1, grid=(n_q,),
        in_specs=[pl.BlockSpec((1, 1, half, cols), lambda q, cr: (q, cr[0], 0, 0)),
                  pl.BlockSpec(block, lambda q, cr: (q, 0, 0))],
        out_specs=pl.BlockSpec(block, lambda q, cr: (q, 0, 0)))
    return pl.pallas_call(
        body, grid_spec=grid_spec, out_shape=pltpu.HBM((n_q, half, cols), BF16),
        compiler_params=_params(("arbitrary",), 32), name="pair_sum_w_in",
    )(core, *_in_hbm(parts.reshape(n_q, 2, half, cols), sibs))


def _pair_sum(g, sib, name):
    _, rows, cols = g.shape
    x, y, c = _place()
    slots = jnp.stack([_block_id(chip, c) for chip in [(x, y)] + _other_chips(x, y)]).astype(jnp.int32)

    def body(slots_ref, g_ref, sib_ref, hs_ref, own_ref):
        q = pl.program_id(0)
        both = g_ref[0].astype(F32) + sib_ref[0].astype(F32)

        @pl.when(q == 0)
        def _():
            own_ref[...] = both

        @pl.when(q > 0)
        def _():
            hs_ref[0] = both.astype(BF16)

    block = (1, rows, cols)
    grid_spec = pltpu.PrefetchScalarGridSpec(
        num_scalar_prefetch=1, grid=(4,),
        in_specs=[pl.BlockSpec(block, lambda q, s: (s[q], 0, 0)), pl.BlockSpec(block, lambda q, s: (q, 0, 0))],
        out_specs=[pl.BlockSpec(block, lambda q, s: (jnp.maximum(q - 1, 0), 0, 0)),
                   pl.BlockSpec((rows, cols), lambda q, s: (0, 0))])
    return pl.pallas_call(
        body, grid_spec=grid_spec,
        out_shape=(pltpu.HBM((3, rows, cols), BF16), pltpu.HBM((rows, cols), F32)),
        compiler_params=_params(("arbitrary",), 32), name=name,
    )(slots, *_in_hbm(g, sib))


def _exchange_scratch(n_arr, n_copies):
    return [pltpu.SemaphoreType.DMA((n_arr, n_copies)), pltpu.SemaphoreType.DMA((n_arr, n_copies))]


def _final_small(vrecv_m, vrecv_b, wab, wrecv, vec_x):
    wrows = wab.shape[0] // N_DEV

    def body(vm_ref, vb_ref, w_ref, wr_ref, vx_ref, o_vec, o_w, xrecv, wred, x_send, x_recv, b_send, b_recv):
        x, y, c = _place()
        my_id = _block_id((x, y), c)
        my_rows = pl.ds(pl.multiple_of(my_id * wrows, SUB), wrows)

        def xcopy(k):
            return _remote_copy(vx_ref, xrecv.at[my_id], x_send.at[k], x_recv.at[k], _peer(x, y, c, k))

        def bcopy(k):
            return _remote_copy(wred, o_w.at[my_rows, :], b_send.at[k], b_recv.at[k], _peer(x, y, c, k))

        xrecv[my_id] = vx_ref[...]
        for k in range(1, N_DEV):
            xcopy(k).start()
        red = w_ref[my_rows, :]
        for k in range(1, N_DEV):
            red = red + wr_ref[k]
        wred[...] = red
        o_w[my_rows, :] = red
        for k in range(1, N_DEV):
            bcopy(k).start()
        for k in range(1, N_DEV):
            xcopy(k).wait_recv()
        for rows, ref in ((slice(0, 8), vm_ref), (slice(8, 24), vb_ref), (slice(24, 32), xrecv)):
            tot = ref[0]
            for s in range(1, N_DEV):
                tot = tot + ref[s]
            o_vec[rows, :] = tot
        for k in range(1, N_DEV):
            bcopy(k).wait_recv()
        for k in range(1, N_DEV):
            xcopy(k).wait_send()
            bcopy(k).wait_send()

    vm = pl.BlockSpec(memory_space=pltpu.VMEM)
    dma8 = pltpu.SemaphoreType.DMA((N_DEV,))
    return pl.pallas_call(
        body, out_shape=(jax.ShapeDtypeStruct((VEC_ROWS, D_MODEL), F32), jax.ShapeDtypeStruct(wab.shape, F32)),
        in_specs=[vm] * 5, out_specs=[vm] * 2,
        scratch_shapes=[pltpu.VMEM((N_DEV, SUB, D_MODEL), F32), pltpu.VMEM((wrows, HEAD_DIM), F32),
                        dma8, dma8, dma8, dma8],
        compiler_params=_params(vmem_mib=32), name="final_small",
    )(vrecv_m, vrecv_b, wab, wrecv, vec_x)


def _in_proj(x, g_mix, shards, tm):
    t_len = x.shape[0]
    n_t = t_len // tm
    n_arr = len(shards)
    rows = [s.shape[0] for s in shards]
    width = 2 * rows[0]
    ax, ay = lax.axis_index("x"), lax.axis_index("y")
    order = jnp.stack([2 * cx + cy for cx, cy in [(ax, ay)] + _other_chips(ax, ay)]).astype(jnp.int32)

    def body(order_ref, x_ref, g_ref, *rest):
        shard_refs = rest[0:n_arr]
        u_ref, h_ref = rest[n_arr:n_arr + 2]
        fulls = rest[n_arr + 2:2 * n_arr + 2]
        h_s, wbuf, send_sems, recv_sems, local_sems, load_sem = rest[2 * n_arr + 2:]
        p = pl.program_id(0)
        i = pl.program_id(1)
        x_, y_, c = _place()
        me = (x_, y_, c)
        my_id = _block_id((x_, y_), c)
        sibling = (x_, y_, 1 - c)
        chips = _other_chips(x_, y_)

        def block(arr, blk):
            return fulls[arr].at[pl.ds(pl.multiple_of(blk * rows[arr], rows[arr]), rows[arr]), :]

        def copy(arr, k, blk, to, src=None):
            dst = block(arr, blk)
            return _remote_copy(dst if src is None else src, dst, send_sems.at[arr, k], recv_sems.at[arr, k], to)

        def local(arr):
            return pltpu.make_async_copy(shard_refs[arr], block(arr, my_id), local_sems.at[arr])

        def load_chip(chip):
            start = pl.multiple_of((2 * chip[0] + chip[1]) * width, width)
            cp = pltpu.make_async_copy(fulls[0].at[pl.ds(start, width), :], wbuf, load_sem.at[0])
            cp.start()
            cp.wait()

        @pl.when((p == 0) & (i == 0))
        def _():
            for arr in range(n_arr):
                local(arr).start()
                copy(arr, 0, my_id, sibling, shard_refs[arr]).start()
                for j in (0, 1):
                    copy(arr, 1 + j, my_id, (*chips[j], c), shard_refs[arr]).start()
            for arr in range(n_arr):
                local(arr).wait()
                copy(arr, 0, _block_id((x_, y_), 1 - c), me).wait_recv()
            load_chip((x_, y_))

        for j, chip in enumerate(chips):
            @pl.when((p == j + 1) & (i == 0))
            def _(j=j, chip=chip):
                for arr in range(n_arr):
                    copy(arr, 1 + j, _block_id(chip, c), me).wait_recv()
                    copy(arr, 4 + j, _block_id(chip, c), sibling).start()
                    if j == 0:
                        copy(arr, 3, my_id, (*chips[2], c), shard_refs[arr]).start()
                for arr in range(n_arr):
                    copy(arr, 4 + j, _block_id(chip, 1 - c), me).wait_recv()
                load_chip(chip)

        @pl.when((p == 3) & (i == n_t - 1))
        def _():
            for arr in range(n_arr):
                for k in range(4):
                    copy(arr, k, my_id, me, shard_refs[arr]).wait_send()
                for j, chip in enumerate(chips):
                    copy(arr, 4 + j, _block_id(chip, c), me).wait_send()

        tile = pl.ds(pl.multiple_of(i * tm, tm), tm)

        @pl.when(p == 0)
        def _():
            xv = x_ref[...]
            h = (xv * _rms(xv) * g_ref[...]).astype(BF16)
            h_ref[...] = h
            h_s[tile, :] = h

        u_ref[...] = _dot_nt(h_s[tile, :], wbuf[...])

    first_pass = lambda p, i, o: (jnp.where(p == 0, i, n_t - 1), 0)
    grid_spec = pltpu.PrefetchScalarGridSpec(
        num_scalar_prefetch=1, grid=(4, n_t),
        in_specs=[pl.BlockSpec((tm, D_MODEL), first_pass), pl.BlockSpec((1, D_MODEL), lambda p, i, o: (0, 0))]
        + [HBM_SPEC] * n_arr,
        out_specs=[pl.BlockSpec((tm, width), lambda p, i, o: (i, o[p])), pl.BlockSpec((tm, D_MODEL), first_pass)]
        + [HBM_SPEC] * n_arr,
        scratch_shapes=[pltpu.VMEM((t_len, D_MODEL), BF16), pltpu.VMEM((width, D_MODEL), BF16)]
        + _exchange_scratch(n_arr, 7) + [pltpu.SemaphoreType.DMA((n_arr,)), pltpu.SemaphoreType.DMA((1,))])
    return pl.pallas_call(
        body, grid_spec=grid_spec,
        out_shape=[jax.ShapeDtypeStruct((t_len, IN_COLS), F32), jax.ShapeDtypeStruct((t_len, D_MODEL), BF16)]
        + [jax.ShapeDtypeStruct((N_DEV * s.shape[0], s.shape[1]), s.dtype) for s in shards],
        compiler_params=_params(("arbitrary", "arbitrary"), 48), name="in_proj",
    )(order, x, g_mix, *shards)


def _conv3_chunk(u_ref, r, cv_prev, cw, row):
    gb = u_ref[pl.ds(r, SUB), OFF_GB:OFF_GB + CONV_WIDTH]
    gc = u_ref[pl.ds(r, SUB), OFF_GC:OFF_GC + CONV_WIDTH]
    v = u_ref[pl.ds(r, SUB), OFF_V:OFF_V + CONV_WIDTH]
    cv = gc * v
    cv_m1 = _down(cv, cv_prev, 1, row)
    cv_m2 = _down(cv, cv_prev, 2, row)
    cq = cw[2:3, :] * cv + cw[1:2, :] * cv_m1 + cw[0:1, :] * cv_m2
    return gb, gc, v, cv, cv_m1, cv_m2, cq


def _conv4_chunk(u_ref, r, xin_prev, rw, rb, row):
    xin = u_ref[pl.ds(r, SUB), OFF_XR:OFF_XR + LRU_WIDTH]
    m1 = _down(xin, xin_prev, 1, row)
    m2 = _down(xin, xin_prev, 2, row)
    m3 = _down(xin, xin_prev, 3, row)
    xr = rw[3:4, :] * xin + rw[2:3, :] * m1 + rw[1:2, :] * m2 + rw[0:1, :] * m3 + rb
    return xin, m1, m2, m3, xr


def _mixer_fwd(u, conv_w, rnn_conv_w, rnn_conv_b, wa, b_a, wx, b_x, lam, gnc, gnr, shards, tm):
    t_len = u.shape[0]
    n_steps = t_len // tm
    n_chunks = tm // SUB
    n_arr = len(shards)

    def body(u_ref, cw_ref, rw_ref, rb_ref, wa_ref, ba_ref, wx_ref, bx_ref, lam_ref, gnc_ref, gnr_ref, *rest):
        shard_refs = rest[0:n_arr]
        hs_ref, y_ref, xr_s, ra_ref, ii_ref, mult_ref = rest[n_arr:n_arr + 6]
        fulls = rest[n_arr + 6:2 * n_arr + 6]
        (y_s, pa_s, px_s, wabd, wxbd, cv_car, xin_car, h_car,
         send_sems, recv_sems, local_sems) = rest[2 * n_arr + 6:]
        _host_all_gather(pl.program_id(0), n_steps, shard_refs, fulls, send_sems, recv_sems, local_sems)

        @pl.when(pl.program_id(0) == 0)
        def _():
            cv_car[...] = jnp.zeros(cv_car.shape, F32)
            xin_car[...] = jnp.zeros(xin_car.shape, F32)
            h_car[...] = jnp.zeros(h_car.shape, F32)
            wabd[...] = _expand_heads(wa_ref[...])
            wxbd[...] = _expand_heads(wx_ref[...])

        row_c = lax.broadcasted_iota(jnp.int32, (SUB, CONV_WIDTH), 0)
        row_r = lax.broadcasted_iota(jnp.int32, (SUB, LRU_WIDTH), 0)
        cw = cw_ref[...]
        rw = rw_ref[...]
        rb = rb_ref[...]
        g_c = gnc_ref[...]
        g_r = gnr_ref[...]
        sp_c = LRU_C * _softplus_neg(lam_ref[...])

        def convs(i, carry):
            cv_prev, xin_prev = carry
            r = pl.multiple_of(i * SUB, SUB)
            gb, _, _, cv, _, _, cq = _conv3_chunk(u_ref, r, cv_prev, cw, row_c)
            y_c = gb * cq
            y_s[pl.ds(r, SUB), 0:CONV_WIDTH] = y_c * _rms(y_c) * g_c
            xin, _, _, _, xr = _conv4_chunk(u_ref, r, xin_prev, rw, rb, row_r)
            xr_s[pl.ds(r, SUB), :] = xr
            return cv, xin

        cv_last, xin_last = _chunk_loop(n_chunks, convs, (cv_car[...], xin_car[...]), in_flight=8)
        cv_car[...] = cv_last
        xin_car[...] = xin_last

        xrb = xr_s[...].astype(BF16)
        pa_s[...] = _block_diag_apply(xrb, wabd) + ba_ref[...]
        px_s[...] = _block_diag_apply(xrb, wxbd) + bx_ref[...]

        def recur(i, h_prev):
            r = pl.multiple_of(i * SUB, SUB)
            xr = xr_s[pl.ds(r, SUB), :]
            ra, ii, a, mult, _ = _lru_gates(pa_s[pl.ds(r, SUB), :], px_s[pl.ds(r, SUB), :], sp_c)
            ra_ref[pl.ds(r, SUB), :] = ra
            ii_ref[pl.ds(r, SUB), :] = ii
            mult_ref[pl.ds(r, SUB), :] = mult
            a_cum, b_cum = _scan8_fwd(a, mult * ii * xr, row_r)
            h = a_cum * h_prev + b_cum
            hs_ref[pl.ds(r, SUB), :] = h
            ge, _ = _gelu(u_ref[pl.ds(r, SUB), OFF_G:OFF_G + LRU_WIDTH])
            y_r = h * ge
            y_s[pl.ds(r, SUB), CONV_WIDTH:MIX_WIDTH] = y_r * _rms(y_r) * g_r
            return h[SUB - 1:SUB, :]

        h_car[...] = _chunk_loop(n_chunks, recur, h_car[...], in_flight=8)

        y_ref[...] = y_s[...].astype(BF16)

    row_tile = lambda w: pl.BlockSpec((tm, w), lambda i: (i, 0))
    whole = lambda a: pl.BlockSpec(a.shape, lambda i: (0,) * a.ndim)
    smalls = (conv_w, rnn_conv_w, rnn_conv_b, wa, b_a, wx, b_x, lam, gnc, gnr)
    return pl.pallas_call(
        body, grid=(n_steps,),
        in_specs=[row_tile(IN_COLS)] + [whole(a) for a in smalls] + [HBM_SPEC] * n_arr,
        out_specs=[row_tile(LRU_WIDTH), row_tile(MIX_WIDTH)] + [row_tile(LRU_WIDTH)] * 4 + [HBM_SPEC] * n_arr,
        out_shape=[jax.ShapeDtypeStruct((t_len, LRU_WIDTH), F32), jax.ShapeDtypeStruct((t_len, MIX_WIDTH), BF16)]
        + [jax.ShapeDtypeStruct((t_len, LRU_WIDTH), F32)] * 4
        + [jax.ShapeDtypeStruct((N_DEV,) + s.shape, BF16) for s in shards],
        scratch_shapes=[pltpu.VMEM((tm, MIX_WIDTH), F32),
                        pltpu.VMEM((tm, LRU_WIDTH), F32), pltpu.VMEM((tm, LRU_WIDTH), F32),
                        pltpu.VMEM((LRU_WIDTH, GROUP), BF16), pltpu.VMEM((LRU_WIDTH, GROUP), BF16),
                        pltpu.VMEM((SUB, CONV_WIDTH), F32), pltpu.VMEM((SUB, LRU_WIDTH), F32),
                        pltpu.VMEM((1, LRU_WIDTH), F32)]
        + _exchange_scratch(n_arr, 7) + [pltpu.SemaphoreType.DMA((n_arr,))],
        compiler_params=_params(("arbitrary",), 56), name="mixer_fwd",
    )(u, *smalls, *shards)


def _mlp_up(x, y, g_mlp, w_out, w1, w2_shard, tm):
    t_len = x.shape[0]
    n_steps = t_len // tm
    n_blk, _, blk = w1.shape

    def body(x_ref, y_ref, gm_ref, wout_hbm, w1_hbm, w2_ref, x1_ref, h2_ref, z_ref, w2_full,
             wout_s, w1_s, sem, send_sems, recv_sems, local_sems):
        step = pl.program_id(0)
        _host_all_gather(step, n_steps, [w2_ref], [w2_full], send_sems, recv_sems, local_sems)

        load_wout = pltpu.make_async_copy(wout_hbm, wout_s, sem.at[0])
        load_w1 = pltpu.make_async_copy(w1_hbm, w1_s, sem.at[1])

        @pl.when(step == 0)
        def _():
            load_wout.start()
            load_w1.start()
            load_wout.wait()

        x1v = x_ref[...] + jnp.dot(y_ref[...], wout_s[...], preferred_element_type=F32)
        x1_ref[...] = x1v
        h2 = (x1v * _rms(x1v) * gm_ref[...]).astype(BF16)
        h2_ref[...] = h2

        @pl.when(step == 0)
        def _():
            load_w1.wait()

        for k in range(n_blk):
            rp = jnp.maximum(jnp.dot(h2, w1_s[k], preferred_element_type=F32), 0.0)
            z_ref[:, k * blk:(k + 1) * blk] = (rp * rp).astype(BF16)

    row_tile = lambda w: pl.BlockSpec((tm, w), lambda i: (i, 0))
    return pl.pallas_call(
        body, grid=(n_steps,),
        in_specs=[row_tile(D_MODEL), row_tile(MIX_WIDTH), pl.BlockSpec((1, D_MODEL), lambda i: (0, 0)),
                  HBM_SPEC, HBM_SPEC, HBM_SPEC],
        out_specs=[row_tile(D_MODEL), row_tile(D_MODEL), row_tile(D_FF), HBM_SPEC],
        out_shape=[jax.ShapeDtypeStruct((t_len, D_MODEL), F32), jax.ShapeDtypeStruct((t_len, D_MODEL), BF16),
                   jax.ShapeDtypeStruct((t_len, D_FF), BF16), jax.ShapeDtypeStruct((N_DEV,) + w2_shard.shape, BF16)],
        scratch_shapes=[pltpu.VMEM(w_out.shape, BF16), pltpu.VMEM(w1.shape, BF16), pltpu.SemaphoreType.DMA((2,))]
        + _exchange_scratch(1, 7) + [pltpu.SemaphoreType.DMA((1,))],
        compiler_params=_params(("arbitrary",), 48), name="mlp_up",
    )(x, y, g_mlp, w_out, w1, w2_shard)


def _mlp_down_bwd(x1, z, target, g_mlp, g_f, w1, w2, tm):
    t_len = x1.shape[0]
    n_steps = t_len // tm
    n_blk, _, blk = w1.shape

    def body(x1_ref, z_ref, tg_ref, gm_ref, gf_ref, w1_hbm, w2_hbm, dx1_ref, dx2_ref, vec_ref, dpre_hbm,
             w1_s, w2_s, dp_s, sem, out_sem):
        step = pl.program_id(0)
        rows = pl.ds(pl.multiple_of(step * tm, tm), tm)
        dp_out = pltpu.make_async_copy(dp_s, dpre_hbm.at[rows, :], out_sem.at[0])

        load_w1 = pltpu.make_async_copy(w1_hbm, w1_s, sem.at[0])
        load_w2 = pltpu.make_async_copy(w2_hbm, w2_s, sem.at[1])

        @pl.when(step == 0)
        def _():
            load_w2.start()
            load_w1.start()
            vec_ref[...] = jnp.zeros(vec_ref.shape, F32)
            load_w2.wait()

        x1v = x1_ref[...]
        g_m = gm_ref[...]
        g_o = gf_ref[...]
        r2 = _rms(x1v)
        x1h = x1v * r2
        x2 = x1v + jnp.dot(z_ref[...], w2_s[...], preferred_element_type=F32)
        r3 = _rms(x2)
        x2h = x2 * r3
        err = x2h * g_o - tg_ref[...]
        dout = err * (1.0 / D_MODEL)
        vec_ref[ROW_LOSS:ROW_LOSS + 1, :] += (0.5 / D_MODEL) * jnp.sum(err * err, axis=0, keepdims=True)
        vec_ref[ROW_GF:ROW_GF + 1, :] += jnp.sum(dout * x2h, axis=0, keepdims=True)
        dx2 = _rms_bwd(dout, x2h, r3, g_o)
        dx2b = dx2.astype(BF16)
        dx2_ref[...] = dx2b
        dh2 = jnp.zeros((tm, D_MODEL), F32)

        @pl.when(step > 0)
        def _():
            dp_out.wait()

        @pl.when(step == 0)
        def _():
            load_w1.wait()

        for k in range(n_blk):
            cols = slice(k * blk, (k + 1) * blk)
            dz = _dot_nt(dx2b, w2_s[cols, :])
            dpb = (dz * 2.0 * jnp.sqrt(z_ref[:, cols].astype(F32))).astype(BF16)
            dp_s[:, cols] = dpb
            dh2 = dh2 + _dot_nt(dpb, w1_s[k])
        dp_out.start()
        vec_ref[ROW_GMLP:ROW_GMLP + 1, :] += jnp.sum(dh2 * x1h, axis=0, keepdims=True)
        dx1_ref[...] = dx2 + _rms_bwd(dh2, x1h, r2, g_m)

        @pl.when(step == n_steps - 1)
        def _():
            dp_out.wait()

    row_tile = lambda w: pl.BlockSpec((tm, w), lambda i: (i, 0))
    vec_spec = pl.BlockSpec((1, D_MODEL), lambda i: (0, 0))
    return pl.pallas_call(
        body, grid=(n_steps,),
        in_specs=[row_tile(D_MODEL), row_tile(D_FF), row_tile(D_MODEL), vec_spec, vec_spec, HBM_SPEC, HBM_SPEC],
        out_specs=[row_tile(D_MODEL), row_tile(D_MODEL), pl.BlockSpec((SUB, D_MODEL), lambda i: (0, 0)), HBM_SPEC],
        out_shape=[jax.ShapeDtypeStruct((t_len, D_MODEL), F32), jax.ShapeDtypeStruct((t_len, D_MODEL), BF16),
                   jax.ShapeDtypeStruct((SUB, D_MODEL), F32), jax.ShapeDtypeStruct((t_len, D_FF), BF16)],
        scratch_shapes=[pltpu.VMEM(w1.shape, BF16), pltpu.VMEM(w2.shape, BF16), pltpu.VMEM((tm, D_FF), BF16),
                        pltpu.SemaphoreType.DMA((2,)), pltpu.SemaphoreType.DMA((1,))],
        compiler_params=_params(("arbitrary",), 56), name="mlp_down_bwd",
    )(x1, z, target, g_mlp, g_f, w1, w2)


def _mixer_bwd(u, hs, dx1, saved, conv_w, rnn_conv_w, rnn_conv_b, wa, b_a, wx, b_x, lam, gnc, gnr, w_out,
               chip_sums, g_wout, tm):
    t_len = u.shape[0]
    n_tiles = t_len // tm
    n_chunks = tm // SUB
    per_tile = tm // SUB
    n_sums = len(chip_sums)

    def body(u_ref, up_ref, hs_ref, hp_ref, dx1_ref, xr_ref, ra_ref, ii_ref, mult_ref,
             cw_ref, rw_ref, rb_ref, wa_ref, ba_ref, wx_ref, bx_ref, lam_ref, gnc_ref, gnr_ref, wout_ref, *rest):
        hsends = rest[0:n_sums]
        gwout_ref = rest[n_sums]
        du_ref, vec_ref, wab_ref = rest[n_sums + 1:n_sums + 4]
        hrecvs = rest[n_sums + 4:2 * n_sums + 4]
        sib_wout = rest[2 * n_sums + 4]
        (du_s, dy_s, dpa_s, dpx_s, dxr_s, wabd, wxbd, acc, dwa_acc, dwx_acc,
         a_car, dh_car, dcq_car, dxr_car, i_send, i_recv, d_send, d_recv) = rest[2 * n_sums + 5:]
        step = pl.program_id(0)
        _host_chip_exchange(step, n_tiles, hsends, hrecvs, i_send, i_recv)
        _host_pair_exchange(step, n_tiles, [gwout_ref], [sib_wout], d_send, d_recv)
        has_prev = (step < n_tiles - 1).astype(F32)

        @pl.when(step == 0)
        def _():
            acc[...] = jnp.zeros(acc.shape, F32)
            dwa_acc[...] = jnp.zeros(dwa_acc.shape, F32)
            dwx_acc[...] = jnp.zeros(dwx_acc.shape, F32)
            a_car[...] = jnp.ones(a_car.shape, F32)
            dh_car[...] = jnp.zeros(dh_car.shape, F32)
            dcq_car[...] = jnp.zeros(dcq_car.shape, F32)
            dxr_car[...] = jnp.zeros(dxr_car.shape, F32)
            wabd[...] = _expand_heads(wa_ref[...])
            wxbd[...] = _expand_heads(wx_ref[...])

        row_c = lax.broadcasted_iota(jnp.int32, (SUB, CONV_WIDTH), 0)
        row_r = lax.broadcasted_iota(jnp.int32, (SUB, LRU_WIDTH), 0)
        cw = cw_ref[...]
        rw = rw_ref[...]
        rb = rb_ref[...]
        g_c = gnc_ref[...]
        g_r = gnr_ref[...]
        sp_c = LRU_C * _softplus_neg(lam_ref[...])

        up = up_ref[...] * has_prev
        cv_before = up[:, OFF_GC:OFF_GC + CONV_WIDTH] * up[:, OFF_V:OFF_V + CONV_WIDTH]
        xin_before = up[:, OFF_XR:OFF_XR + LRU_WIDTH]
        hs_before = hp_ref[...] * has_prev

        dy_s[...] = _dot_nt(dx1_ref[...].astype(BF16), wout_ref[...])

        xrb = xr_ref[...].astype(BF16)

        def recur_bwd(j, carry):
            a_later, dh_later = carry
            i = n_chunks - 1 - j
            r = pl.multiple_of(i * SUB, SUB)
            rp = pl.multiple_of(jnp.maximum(i - 1, 0) * SUB, SUB)
            xr = xr_ref[pl.ds(r, SUB), :]
            hs_c = hs_ref[pl.ds(r, SUB), :]
            hs_prev = jnp.where(i == 0, hs_before, hs_ref[pl.ds(rp, SUB), :])
            h_m1 = _down(hs_c, hs_prev, 1, row_r)
            ra = ra_ref[pl.ds(r, SUB), :]
            ii = ii_ref[pl.ds(r, SUB), :]
            mult = mult_ref[pl.ds(r, SUB), :]
            a = jnp.exp(-ra * sp_c)
            inv_mult = lax.rsqrt(mult * mult)
            ge, dge = _gelu(u_ref[pl.ds(r, SUB), OFF_G:OFF_G + LRU_WIDTH])
            y_r = hs_c * ge
            rr = _rms(y_r)
            yhat = y_r * rr
            dyn = dy_s[pl.ds(r, SUB), CONV_WIDTH:MIX_WIDTH]
            acc[ACC_GNR] += dyn * yhat
            dy_r = _rms_bwd(dyn, yhat, rr, g_r)
            du_s[pl.ds(r, SUB), OFF_G:OFF_G + LRU_WIDTH] = dy_r * hs_c * dge
            a_cum, d_cum = _scan8_rev(_up(a, a_later, 1, row_r), dy_r * ge, row_r)
            dh = a_cum * dh_later + d_cum
            dm = dh * mult
            dii = dm * xr
            dxr_s[pl.ds(r, SUB), :] = dm * ii
            dla = a * dh * (h_m1 - (ii * xr) * a * inv_mult)
            dla_r = dla * ra
            acc[ACC_SP] -= dla_r
            dpa = dla_r * (sp_c * (ra - 1.0))
            dpx = dii * ii * (1.0 - ii)
            acc[ACC_BA] += dpa
            acc[ACC_BX] += dpx
            dpa_s[pl.ds(r, SUB), :] = dpa
            dpx_s[pl.ds(r, SUB), :] = dpx
            return a, dh[0:1, :]

        a_first, dh_first = _chunk_loop(n_chunks, recur_bwd, (a_car[...], dh_car[...]), in_flight=8)
        a_car[...] = a_first
        dh_car[...] = dh_first

        dpab = dpa_s[...].astype(BF16)
        dpxb = dpx_s[...].astype(BF16)
        dxr_s[...] += _block_diag_apply_t(dpab, wabd) + _block_diag_apply_t(dpxb, wxbd)
        for g in range(LRU_WIDTH // GROUP):
            cols = slice(g * GROUP, (g + 1) * GROUP)
            dwa_acc[cols, :] += _dot_tn(xrb[:, cols], dpab[:, cols])
            dwx_acc[cols, :] += _dot_tn(xrb[:, cols], dpxb[:, cols])

        def convs_bwd(j, carry):
            dcq_later, dxr_later = carry
            i = n_chunks - 1 - j
            r = pl.multiple_of(i * SUB, SUB)
            rp = pl.multiple_of(jnp.maximum(i - 1, 0) * SUB, SUB)
            cv_prev = jnp.where(i == 0, cv_before,
                                u_ref[pl.ds(rp, SUB), OFF_GC:OFF_GC + CONV_WIDTH]
                                * u_ref[pl.ds(rp, SUB), OFF_V:OFF_V + CONV_WIDTH])
            gb, gc, v, cv, cv_m1, cv_m2, cq = _conv3_chunk(u_ref, r, cv_prev, cw, row_c)
            y_c = gb * cq
            rc = _rms(y_c)
            yhat = y_c * rc
            dyn = dy_s[pl.ds(r, SUB), 0:CONV_WIDTH]
            acc[ACC_GNC, :, 0:CONV_WIDTH] += dyn * yhat
            dy_c = _rms_bwd(dyn, yhat, rc, g_c)
            dcq = dy_c * gb
            dcv = (cw[2:3, :] * dcq + cw[1:2, :] * _up(dcq, dcq_later, 1, row_c)
                   + cw[0:1, :] * _up(dcq, dcq_later, 2, row_c))
            acc[ACC_CW + 2, :, 0:CONV_WIDTH] += dcq * cv
            acc[ACC_CW + 1, :, 0:CONV_WIDTH] += dcq * cv_m1
            acc[ACC_CW + 0, :, 0:CONV_WIDTH] += dcq * cv_m2
            du_s[pl.ds(r, SUB), OFF_GB:OFF_GB + CONV_WIDTH] = dy_c * cq
            du_s[pl.ds(r, SUB), OFF_GC:OFF_GC + CONV_WIDTH] = dcv * v
            du_s[pl.ds(r, SUB), OFF_V:OFF_V + CONV_WIDTH] = dcv * gc

            xin_prev = jnp.where(i == 0, xin_before, u_ref[pl.ds(rp, SUB), OFF_XR:OFF_XR + LRU_WIDTH])
            xin, m1, m2, m3, _ = _conv4_chunk(u_ref, r, xin_prev, rw, rb, row_r)
            dxr = dxr_s[pl.ds(r, SUB), :]
            du_s[pl.ds(r, SUB), OFF_XR:OFF_XR + LRU_WIDTH] = (
                rw[3:4, :] * dxr + rw[2:3, :] * _up(dxr, dxr_later, 1, row_r)
                + rw[1:2, :] * _up(dxr, dxr_later, 2, row_r) + rw[0:1, :] * _up(dxr, dxr_later, 3, row_r))
            acc[ACC_RW + 3] += dxr * xin
            acc[ACC_RW + 2] += dxr * m1
            acc[ACC_RW + 1] += dxr * m2
            acc[ACC_RW + 0] += dxr * m3
            acc[ACC_BR] += dxr
            return dcq, dxr

        dcq_first, dxr_first = _chunk_loop(n_chunks, convs_bwd, (dcq_car[...], dxr_car[...]), in_flight=8)
        dcq_car[...] = dcq_first
        dxr_car[...] = dxr_first

        du_ref[...] = du_s[...].astype(BF16)

        @pl.when(step == n_tiles - 1)
        def _():
            vec_ref[...] = jnp.zeros(vec_ref.shape, F32)
            rows = {ACC_GNC: ROW_GNC, ACC_GNR: ROW_GNR, ACC_BR: ROW_BR, ACC_BA: ROW_BA, ACC_BX: ROW_BX}
            for k in range(3):
                rows[ACC_CW + k] = ROW_CW + k
            for k in range(4):
                rows[ACC_RW + k] = ROW_RW + k
            for slot, out_row in rows.items():
                o = out_row - ROW_GNC
                vec_ref[o:o + 1, :] = jnp.sum(acc[slot], axis=0, keepdims=True)
            lam_v = lam_ref[...]
            dsp = jnp.sum(acc[ACC_SP], axis=0, keepdims=True)
            o = ROW_LAM - ROW_GNC
            vec_ref[o:o + 1, :] = -dsp * LRU_C / (1.0 + jnp.exp(lam_v))
            wab_ref[0:LRU_WIDTH, :] = _fold_heads(dwa_acc[...])
            wab_ref[LRU_WIDTH:2 * LRU_WIDTH, :] = _fold_heads(dwx_acc[...])

    rev = lambda w: pl.BlockSpec((tm, w), lambda s: (n_tiles - 1 - s, 0))
    before = lambda w: pl.BlockSpec((SUB, w), lambda s: (jnp.maximum((n_tiles - 1 - s) * per_tile - 1, 0), 0))
    whole = lambda a: pl.BlockSpec(a.shape, lambda s: (0,) * a.ndim)
    smalls = (conv_w, rnn_conv_w, rnn_conv_b, wa, b_a, wx, b_x, lam, gnc, gnr, w_out)
    full = lambda w: pltpu.VMEM((tm, w), F32)
    return pl.pallas_call(
        body, grid=(n_tiles,),
        in_specs=[rev(IN_COLS), before(IN_COLS), rev(LRU_WIDTH), before(LRU_WIDTH), rev(D_MODEL)]
        + [rev(LRU_WIDTH)] * len(saved) + [whole(a) for a in smalls] + [HBM_SPEC] * (n_sums + 1),
        out_specs=[rev(IN_COLS), pl.BlockSpec((16, D_MODEL), lambda s: (0, 0)),
                   pl.BlockSpec((2 * LRU_WIDTH, HEAD_DIM), lambda s: (0, 0))] + [HBM_SPEC] * (n_sums + 1),
        out_shape=[jax.ShapeDtypeStruct((t_len, IN_COLS), BF16), jax.ShapeDtypeStruct((16, D_MODEL), F32),
                   jax.ShapeDtypeStruct((2 * LRU_WIDTH, HEAD_DIM), F32)]
        + [jax.ShapeDtypeStruct(s.shape, BF16) for s in chip_sums]
        + [jax.ShapeDtypeStruct((4,) + g_wout.shape[1:], BF16)],
        scratch_shapes=[full(IN_COLS), full(MIX_WIDTH), full(LRU_WIDTH), full(LRU_WIDTH), full(LRU_WIDTH),
                        pltpu.VMEM((LRU_WIDTH, GROUP), BF16), pltpu.VMEM((LRU_WIDTH, GROUP), BF16),
                        pltpu.VMEM((N_ACC, SUB, LRU_WIDTH), F32),
                        pltpu.VMEM((LRU_WIDTH, GROUP), F32), pltpu.VMEM((LRU_WIDTH, GROUP), F32),
                        pltpu.VMEM((SUB, LRU_WIDTH), F32), pltpu.VMEM((1, LRU_WIDTH), F32),
                        pltpu.VMEM((SUB, CONV_WIDTH), F32), pltpu.VMEM((SUB, LRU_WIDTH), F32)]
        + _exchange_scratch(n_sums, 3) + _exchange_scratch(1, 4),
        compiler_params=_params(("arbitrary",), 56), name="mixer_bwd",
    )(u, u, hs, hs, dx1, *saved, *smalls, *chip_sums, g_wout)


def _in_proj_bwd(du, dx1, x, g_mix, win_t, tm, chip_sums, g_own):
    t_len = x.shape[0]
    n_steps = t_len // tm

    def body(du_ref, dx1_ref, x_ref, g_ref, w_ref, hs_ref, gown_ref,
             dx_ref, vec_ref, landed_ref, sib_ref, i_send, i_recv, d_send, d_recv):
        step = pl.program_id(0)
        _host_chip_exchange(step, n_steps, [hs_ref], [landed_ref], i_send, i_recv)
        _host_half_exchange(step, n_steps, gown_ref, sib_ref, d_send, d_recv)

        @pl.when(step == 0)
        def _():
            vec_ref[...] = jnp.zeros(vec_ref.shape, F32)

        dh = jnp.dot(du_ref[...], w_ref[...], preferred_element_type=F32)
        xv = x_ref[...]
        r1 = _rms(xv)
        xh = xv * r1
        vec_ref[0:1, :] += jnp.sum(dh * xh, axis=0, keepdims=True)
        dx_ref[...] = dx1_ref[...] + _rms_bwd(dh, xh, r1, g_ref[...])

    row_tile = lambda w: pl.BlockSpec((tm, w), lambda i: (i, 0))
    half_shape = (g_own.shape[0], g_own.shape[1] // 2, g_own.shape[2])
    return pl.pallas_call(
        body, grid=(n_steps,),
        in_specs=[row_tile(IN_COLS), row_tile(D_MODEL), row_tile(D_MODEL), pl.BlockSpec((1, D_MODEL), lambda i: (0, 0)),
                  pl.BlockSpec((IN_COLS, D_MODEL), lambda i: (0, 0))] + [HBM_SPEC] * 2,
        out_specs=[row_tile(D_MODEL), pl.BlockSpec((SUB, D_MODEL), lambda i: (0, 0))] + [HBM_SPEC] * 2,
        out_shape=[jax.ShapeDtypeStruct((t_len, D_MODEL), F32), jax.ShapeDtypeStruct((SUB, D_MODEL), F32),
                   jax.ShapeDtypeStruct(chip_sums.shape, BF16), jax.ShapeDtypeStruct(half_shape, BF16)],
        scratch_shapes=_exchange_scratch(1, 3) + [pltpu.SemaphoreType.DMA((1,)), pltpu.SemaphoreType.DMA((1,))],
        compiler_params=_params(("arbitrary",), 56), name="in_proj_bwd",
    )(du, dx1, x, g_mix, win_t, chip_sums, g_own)


def _tn_weight_grad(a, b, tk, name, pair=(), col_blocks=1):
    t_len, m = a.shape
    n = b.shape[1]
    n_steps = t_len // tk
    sent = tuple(pair)
    n_sent = len(sent)

    def body(a_ref, b_ref, *rest):
        srcs = rest[0:n_sent]
        o_ref = rest[n_sent]
        dsts = rest[n_sent + 1:2 * n_sent + 1]
        acc = rest[2 * n_sent + 1]
        sems = rest[2 * n_sent + 2:]
        j = pl.program_id(0)
        if pair:
            _host_pair_exchange(j, n_steps, srcs, dsts, *sems)

        @pl.when(j == 0)
        def _():
            acc[...] = jnp.zeros(acc.shape, F32)

        acc[...] += _dot_tn(a_ref[...].astype(BF16), b_ref[...].astype(BF16))

        @pl.when(j == n_steps - 1)
        def _():
            if col_blocks == 1:
                o_ref[...] = acc[...].astype(BF16)
            else:
                for k in range(col_blocks):
                    o_ref[k] = acc[:, k * nb:(k + 1) * nb].astype(BF16)

    nb = n // col_blocks
    out_dims = (m, n) if col_blocks == 1 else (col_blocks, m, nb)
    landed = [jax.ShapeDtypeStruct((4,) + g.shape[1:], BF16) for g in pair]
    scratch = [pltpu.VMEM((m, n), F32)]
    if n_sent:
        scratch += _exchange_scratch(n_sent, 4)
    return pl.pallas_call(
        body, grid=(n_steps,),
        in_specs=[pl.BlockSpec((tk, m), lambda j: (j, 0)), pl.BlockSpec((tk, n), lambda j: (j, 0))]
        + [HBM_SPEC] * n_sent,
        out_specs=[pl.BlockSpec(out_dims, lambda j: (0,) * len(out_dims))] + [HBM_SPEC] * n_sent,
        out_shape=[jax.ShapeDtypeStruct(out_dims, BF16)] + landed,
        scratch_shapes=scratch,
        compiler_params=_params(("arbitrary",), 56), name=name,
    )(a, b, *sent)


def _w_in_grad_part(du, h, tk, name, chip_ids, chip=(), halves=None, small=None):
    t_len = du.shape[0]
    n_t = t_len // tk
    n_q = chip_ids.shape[0]
    width = 2 * (IN_COLS // N_DEV)
    n_steps = n_q * n_t
    n_chip = len(chip)
    sent = tuple(chip) + (() if halves is None else (halves,)) + (() if small is None else tuple(small))
    n_sent = len(sent)

    def body(ids_ref, a_ref, b_ref, *rest):
        srcs = rest[0:n_sent]
        o_ref = rest[n_sent]
        dsts = rest[n_sent + 1:2 * n_sent + 1]
        acc = rest[2 * n_sent + 1]
        sems = list(rest[2 * n_sent + 2:])
        j = pl.program_id(1)
        step = pl.program_id(0) * n_t + j
        if chip:
            _host_chip_exchange(step, n_steps, srcs[0:n_chip], dsts[0:n_chip], sems.pop(0), sems.pop(0))
        if halves is not None:
            _host_half_exchange(step, n_steps, srcs[n_chip], dsts[n_chip], sems.pop(0), sems.pop(0))
        if small is not None:
            _host_small_exchange(step, n_steps, *srcs[n_sent - 3:], *dsts[n_sent - 3:], *sems)

        @pl.when(j == 0)
        def _():
            acc[...] = jnp.zeros(acc.shape, F32)

        acc[...] += _dot_tn(a_ref[...], b_ref[...])

        @pl.when(j == n_t - 1)
        def _():
            o_ref[0] = acc[...].astype(BF16)

    landed = [jax.ShapeDtypeStruct(s.shape, BF16) for s in chip]
    scratch = [pltpu.VMEM((width, D_MODEL), F32)]
    if chip:
        scratch += _exchange_scratch(len(chip), 3)
    if halves is not None:
        landed.append(jax.ShapeDtypeStruct((halves.shape[0], halves.shape[1] // 2, halves.shape[2]), BF16))
        scratch += [pltpu.SemaphoreType.DMA((halves.shape[0],)), pltpu.SemaphoreType.DMA((halves.shape[0],))]
    if small is not None:
        vec_m, vec_b, wab = small
        landed += [jax.ShapeDtypeStruct((N_DEV,) + vec_m.shape, F32), jax.ShapeDtypeStruct((N_DEV,) + vec_b.shape, F32),
                   jax.ShapeDtypeStruct((N_DEV, wab.shape[0] // N_DEV, wab.shape[1]), F32)]
        scratch += _exchange_scratch(3, N_DEV) + [pltpu.SemaphoreType.DMA((2,))]
    grid_spec = pltpu.PrefetchScalarGridSpec(
        num_scalar_prefetch=1, grid=(n_q, n_t),
        in_specs=[pl.BlockSpec((tk, width), lambda q, j, ids: (j, ids[q])),
                  pl.BlockSpec((tk, D_MODEL), lambda q, j, ids: (j, 0))] + [HBM_SPEC] * n_sent,
        out_specs=[pl.BlockSpec((1, width, D_MODEL), lambda q, j, ids: (q, 0, 0))] + [HBM_SPEC] * n_sent,
        scratch_shapes=scratch)
    return pl.pallas_call(
        body, grid_spec=grid_spec, out_shape=[jax.ShapeDtypeStruct((n_q, width, D_MODEL), BF16)] + landed,
        compiler_params=_params(("arbitrary", "arbitrary"), 40), name=name,
    )(chip_ids, du, h, *sent)


def _adamw(w, g, m, v):
    m = ADAM_B1 * m + (1.0 - ADAM_B1) * g
    v = ADAM_B2 * v + (1.0 - ADAM_B2) * (g * g)
    delta = -ADAM_LR * ((m / BC1) / (jnp.sqrt(v / BC2) + ADAM_EPS) + ADAM_WD * w)
    return delta, m, v


def _update_sharded(g, landed, w, m, v, rows_blk, name):
    rows, cols = w.shape

    def body(g_ref, l_ref, w_ref, m_ref, v_ref, og, od, om, ov):
        gv = g_ref[...]
        for j in range(3):
            gv = gv + l_ref[j].astype(F32)
        delta, mn, vn = _adamw(w_ref[...], gv, m_ref[...], v_ref[...])
        og[...] = gv
        od[...] = delta
        om[...] = mn
        ov[...] = vn

    blk = pl.BlockSpec((rows_blk, cols), lambda i: (i, 0))
    shape = pltpu.HBM((rows, cols), F32)
    return pl.pallas_call(
        body, grid=(rows // rows_blk,),
        in_specs=[blk, pl.BlockSpec((3, rows_blk, cols), lambda i: (0, i, 0)), blk, blk, blk],
        out_specs=[blk] * 4, out_shape=[shape] * 4,
        compiler_params=_params(("arbitrary",), 32), name=name,
    )(*_in_hbm(g, landed, w, m, v))


def _update_w_in(g_own, sib_own, landed, w_t, m_t, v_t, core, cols_blk):
    rows, cols = w_t.shape

    def body(core_ref, g_ref, s_ref, l_ref, w_ref, m_ref, v_ref, og, od, om, ov):
        gv = g_ref[0, 0].astype(F32) + s_ref[0].astype(F32)
        for j in range(3):
            gv = gv + l_ref[j].astype(F32)
        delta, mn, vn = _adamw(w_ref[...], gv, m_ref[...], v_ref[...])
        og[...] = gv
        od[...] = delta
        om[...] = mn
        ov[...] = vn

    blk = pl.BlockSpec((rows, cols_blk), lambda i, cr: (0, i))
    grid_spec = pltpu.PrefetchScalarGridSpec(
        num_scalar_prefetch=1, grid=(cols // cols_blk,),
        in_specs=[pl.BlockSpec((1, 1, rows, cols_blk), lambda i, cr: (0, cr[0], 0, i)),
                  pl.BlockSpec((1, rows, cols_blk), lambda i, cr: (0, 0, i)),
                  pl.BlockSpec((3, rows, cols_blk), lambda i, cr: (0, 0, i)), blk, blk, blk],
        out_specs=[blk] * 4)
    return pl.pallas_call(
        body, grid_spec=grid_spec, out_shape=[pltpu.HBM((rows, cols), F32)] * 4,
        compiler_params=_params(("arbitrary",), 32), name="update_w_in",
    )(core, *_in_hbm(g_own.reshape(1, 2, rows, cols), sib_own, landed, w_t, m_t, v_t))


def _update_small(vsum, wsum, g_cw, g_rw, weights, moments_m, moments_v):
    n = len(weights)

    def body(*refs):
        vs, ws, gcw, grw = refs[0:4]
        w_refs = refs[4:4 + n]
        m_refs = refs[4 + n:4 + 2 * n]
        v_refs = refs[4 + 2 * n:4 + 3 * n]
        outs = refs[4 + 3 * n:]
        loss_ref = outs[0]
        loss_ref[...] = jnp.sum(vs[ROW_LOSS:ROW_LOSS + 1, :], axis=1, keepdims=True)
        grads = [
            vs[ROW_GMIX:ROW_GMIX + 1, :], gcw[...], grw[...], vs[ROW_BR:ROW_BR + 1, :],
            ws[0:LRU_WIDTH, :], vs[ROW_BA:ROW_BA + 1, :], ws[LRU_WIDTH:2 * LRU_WIDTH, :], vs[ROW_BX:ROW_BX + 1, :],
            vs[ROW_LAM:ROW_LAM + 1, :], vs[ROW_GNC:ROW_GNC + 1, 0:CONV_WIDTH], vs[ROW_GNR:ROW_GNR + 1, :],
            vs[ROW_GMLP:ROW_GMLP + 1, :], vs[ROW_GF:ROW_GF + 1, :],
        ]
        for k in range(n):
            gk = grads[k]
            delta, mn, vn = _adamw(w_refs[k][...], gk, m_refs[k][...], v_refs[k][...])
            outs[1 + 4 * k][...] = gk
            outs[2 + 4 * k][...] = delta
            outs[3 + 4 * k][...] = mn
            outs[4 + 4 * k][...] = vn

    whole = lambda a: pl.BlockSpec(a.shape, lambda i: (0,) * len(a.shape))
    out_shape = [jax.ShapeDtypeStruct((1, 1), F32)]
    for w in weights:
        out_shape += [jax.ShapeDtypeStruct(w.shape, F32)] * 4
    args = (vsum, wsum, g_cw, g_rw, *weights, *moments_m, *moments_v)
    return pl.pallas_call(
        body, grid=(1,), out_shape=out_shape, in_specs=[whole(a) for a in args], out_specs=[whole(s) for s in out_shape],
        compiler_params=_params(("arbitrary",), 32), name="update_small",
    )(*args)


def kernel(x, norm_mix_g, w_in, conv_w, rnn_conv_w, rnn_conv_b, w_a, b_a, w_x, b_x, lru_lambda, g_norm_conv, g_norm_rnn, w_out, norm_mlp_g, w_mlp_in, w_mlp_out, final_norm_g, loss_target, m_norm_mix_g, m_w_in, m_conv_w, m_rnn_conv_w, m_rnn_conv_b, m_w_a, m_b_a, m_w_x, m_b_x, m_lru_lambda, m_g_norm_conv, m_g_norm_rnn, m_w_out, m_norm_mlp_g, m_w_mlp_in, m_w_mlp_out, m_final_norm_g, v_norm_mix_g, v_w_in, v_conv_w, v_rnn_conv_w, v_rnn_conv_b, v_w_a, v_b_a, v_w_x, v_b_x, v_lru_lambda, v_g_norm_conv, v_g_norm_rnn, v_w_out, v_norm_mlp_g, v_w_mlp_in, v_w_mlp_out, v_final_norm_g):
    t_len = x.shape[1]
    my_id = 4 * lax.axis_index("x") + 2 * lax.axis_index("y") + lax.axis_index("c")
    tm = min(256, t_len)
    tb = min(512, t_len)
    tk = min(512, t_len)

    xs = x.reshape(t_len, D_MODEL)
    tgt = loss_target.reshape(t_len, D_MODEL)
    flat = lambda a: a.reshape(a.shape[-2:]) if a.ndim == 3 else a.reshape(1, -1)
    heads = lambda a: a.reshape(LRU_WIDTH, HEAD_DIM)

    turned = lambda a: jnp.transpose(flat(a))
    win_shard, wout_shard, w1_shard, w2_shard, cp_shard = _prep_shards(
        turned(w_in), flat(w_out), flat(w_mlp_in), flat(w_mlp_out), flat(conv_w), flat(rnn_conv_w))

    u, h, win_t, cp_full = _in_proj(xs, flat(norm_mix_g), (win_shard, cp_shard), min(1024, t_len))
    cpack = cp_full.reshape(N_DEV, 8, 128)
    conv_full = jnp.transpose(cpack[:, 0:3, 0:64], (1, 0, 2)).reshape(3, CONV_WIDTH)
    rnn_full = jnp.transpose(cpack[:, 3:7, :], (1, 0, 2)).reshape(4, LRU_WIDTH)
    mixer_small = (conv_full, rnn_full, flat(rnn_conv_b), heads(w_a), flat(b_a), heads(w_x), flat(b_x),
                   flat(lru_lambda), flat(g_norm_conv), flat(g_norm_rnn))
    hs, y, xr, gate_r, gate_i, mult, w1_blk, wout_blk = _mixer_fwd(u, *mixer_small, (w1_shard, wout_shard), tm)
    wout_f = wout_blk.reshape(MIX_WIDTH, D_MODEL)
    x1, h2, z, w2_blk = _mlp_up(xs, y, flat(norm_mlp_g), wout_f, w1_blk, w2_shard, tb)
    dx1, dx2, vec_m, dpre = _mlp_down_bwd(x1, z, tgt, flat(norm_mlp_g), flat(final_norm_g), w1_blk,
                                          w2_blk.reshape(D_FF, D_MODEL), tb)
    (g_w1,) = _tn_weight_grad(h2, dpre, tk, "w_mlp_in_grad", col_blocks=N_DEV)
    (g_w2,) = _tn_weight_grad(z, dx2, tk, "w_mlp_out_grad")
    g_w2 = g_w2.reshape(N_DEV, D_FF // N_DEV, D_MODEL)
    g_wout, sib_w1, sib_w2 = _tn_weight_grad(y, dx1, tk, "w_out_grad", pair=(g_w1, g_w2))
    g_wout = g_wout.reshape(N_DEV, MIX_WIDTH // N_DEV, D_MODEL)
    hsend_w1, own_w1 = _pair_sum(g_w1, sib_w1, "pair_sum_w_mlp_in")
    hsend_w2, own_w2 = _pair_sum(g_w2, sib_w2, "pair_sum_w_mlp_out")
    du, vec_b, wab, landed_w1, landed_w2, sib_wout = _mixer_bwd(
        u, hs, dx1, (xr, gate_r, gate_i, mult), *mixer_small, wout_f, (hsend_w1, hsend_w2), g_wout, tm)
    hsend_wout, own_wout = _pair_sum(g_wout, sib_wout, "pair_sum_w_out")
    ax, ay, ac = lax.axis_index("x"), lax.axis_index("y"), lax.axis_index("c")
    chip_ids = jnp.stack([2 * cx + cy for cx, cy in [(ax, ay)] + _other_chips(ax, ay)]).astype(jnp.int32)
    core = jnp.reshape(ac, (1,)).astype(jnp.int32)
    tw = min(1024, t_len)
    g_others, landed_wout, vrecv_m, vrecv_b, wrecv = _w_in_grad_part(
        du, h, tw, "w_in_grad_others", chip_ids[1:4], chip=(hsend_wout,), small=(vec_m, vec_b, wab))
    g_own, sib_others = _w_in_grad_part(du, h, tw, "w_in_grad_own", chip_ids[0:1], halves=g_others)
    hsend_win = _pair_sum_parts(g_others, sib_others, core)
    grad_x, vec_x, landed_win, sib_own = _in_proj_bwd(du, dx1, xs, flat(norm_mix_g), win_t, tm, hsend_win, g_own)

    vsum, wsum = _final_small(vrecv_m, vrecv_b, wab, wrecv, vec_x)

    up_win = _update_w_in(g_own, sib_own, landed_win, turned(w_in), turned(m_w_in), turned(v_w_in), core, 256)
    up_win = [jnp.transpose(a) for a in up_win]
    up_wout = _update_sharded(own_wout, landed_wout, flat(w_out), flat(m_w_out), flat(v_w_out), 96, "update_w_out")
    up_w1 = _update_sharded(own_w1, landed_w1, flat(w_mlp_in), flat(m_w_mlp_in), flat(v_w_mlp_in), 256,
                            "update_w_mlp_in")
    up_w2 = _update_sharded(own_w2, landed_w2, flat(w_mlp_out), flat(m_w_mlp_out), flat(v_w_mlp_out), 256,
                            "update_w_mlp_out")

    g_cw = lax.dynamic_slice(vsum, (ROW_CW, 64 * my_id), (3, 64))
    g_rw = lax.dynamic_slice(vsum, (ROW_RW, 128 * my_id), (4, 128))
    small_w = (norm_mix_g, conv_w, rnn_conv_w, rnn_conv_b, w_a, b_a, w_x, b_x, lru_lambda, g_norm_conv, g_norm_rnn,
               norm_mlp_g, final_norm_g)
    small_m = (m_norm_mix_g, m_conv_w, m_rnn_conv_w, m_rnn_conv_b, m_w_a, m_b_a, m_w_x, m_b_x, m_lru_lambda,
               m_g_norm_conv, m_g_norm_rnn, m_norm_mlp_g, m_final_norm_g)
    small_v = (v_norm_mix_g, v_conv_w, v_rnn_conv_w, v_rnn_conv_b, v_w_a, v_b_a, v_w_x, v_b_x, v_lru_lambda,
               v_g_norm_conv, v_g_norm_rnn, v_norm_mlp_g, v_final_norm_g)
    is_heads = (False, False, False, False, True, False, True, False, False, False, False, False, False)
    as2d = lambda arrs: [heads(a) if hd else flat(a) for a, hd in zip(arrs, is_heads)]
    small_out = _update_small(vsum, wsum, g_cw, g_rw, as2d(small_w), as2d(small_m), as2d(small_v))
    loss = small_out[0].reshape(())

    names = ["norm_mix_g", "w_in", "conv_w", "rnn_conv_w", "rnn_conv_b", "w_a", "b_a", "w_x", "b_x", "lru_lambda",
             "g_norm_conv", "g_norm_rnn", "w_out", "norm_mlp_g", "w_mlp_in", "w_mlp_out", "final_norm_g"]
    originals = dict(zip(names, (norm_mix_g, w_in, conv_w, rnn_conv_w, rnn_conv_b, w_a, b_a, w_x, b_x, lru_lambda,
                                 g_norm_conv, g_norm_rnn, w_out, norm_mlp_g, w_mlp_in, w_mlp_out, final_norm_g)))
    results = {"w_in": up_win, "w_out": up_wout, "w_mlp_in": up_w1, "w_mlp_out": up_w2}
    small_names = ["norm_mix_g", "conv_w", "rnn_conv_w", "rnn_conv_b", "w_a", "b_a", "w_x", "b_x", "lru_lambda",
                   "g_norm_conv", "g_norm_rnn", "norm_mlp_g", "final_norm_g"]
    for k, nm in enumerate(small_names):
        results[nm] = small_out[1 + 4 * k:5 + 4 * k]
    out = [loss, grad_x.reshape(x.shape)]
    for kind in range(4):
        out += [results[nm][kind].reshape(originals[nm].shape) for nm in names]
    return tuple(out)
```

```python
import functools

import jax
import jax.numpy as jnp
from jax import lax
from jax.experimental import pallas as pl
from jax.experimental.pallas import tpu as pltpu

F32 = jnp.float32
BF16 = jnp.bfloat16

D_MODEL = 1024
HEAD_DIM = 64
CONV_WIDTH = 512
LRU_WIDTH = 1024
MIX_WIDTH = CONV_WIDTH + LRU_WIDTH
IN_COLS = 3 * CONV_WIDTH + 2 * LRU_WIDTH
D_FF = 4 * D_MODEL
GROUP = 256
EPS = 1e-6
LRU_C = 8.0
N_DEV = 8
SUB = 8

OFF_GB, OFF_GC, OFF_V, OFF_XR, OFF_G = 0, 512, 1024, 1536, 2560

ADAM_LR, ADAM_B1, ADAM_B2, ADAM_EPS, ADAM_WD, ADAM_STEP = 0.001, 0.9, 0.999, 1e-08, 0.01, 10
BC1 = 1.0 - ADAM_B1 ** ADAM_STEP
BC2 = 1.0 - ADAM_B2 ** ADAM_STEP

MIB = 1024 * 1024
MESH = pl.DeviceIdType.MESH

VEC_ROWS = 32
ROW_GF, ROW_GMLP, ROW_LOSS = 0, 1, 2
ROW_GNC, ROW_GNR, ROW_BR, ROW_BA, ROW_BX, ROW_LAM, ROW_CW, ROW_RW = 8, 9, 10, 11, 12, 13, 14, 17
ROW_GMIX = 24
ACC_GNC, ACC_GNR, ACC_BR, ACC_BA, ACC_BX, ACC_SP, ACC_CW, ACC_RW, N_ACC = 0, 1, 2, 3, 4, 5, 6, 9, 13


def _params(semantics=None, vmem_mib=48):
    return pltpu.CompilerParams(dimension_semantics=semantics, vmem_limit_bytes=vmem_mib * MIB)


def _rms(x):
    return lax.rsqrt(jnp.mean(x * x, axis=-1, keepdims=True) + EPS)


def _rms_bwd(dy, xhat, r, g):
    dyh = dy * g
    return r * (dyh - xhat * jnp.mean(dyh * xhat, axis=-1, keepdims=True))


def _sigmoid(x):
    return 0.5 + 0.5 * jnp.tanh(0.5 * x)


def _gelu(x):
    c0, c1 = 0.7978845608028654, 0.044715
    x2 = x * x
    t = jnp.tanh(x * (c0 + (c0 * c1) * x2))
    half = 0.5 + 0.5 * t
    ge = x * half
    dge = half + (ge - ge * half) * (2.0 * c0 + (6.0 * c0 * c1) * x2)
    return ge, dge


def _softplus_neg(lam):
    z = -lam
    e = jnp.exp(-jnp.abs(z))
    return jnp.maximum(z, 0.0) + jnp.where(e < 1e-4, e * (1.0 - 0.5 * e), jnp.log(1.0 + e))


def _lru_gates(pa, px, sp_c):
    ra = _sigmoid(pa)
    ii = _sigmoid(px)
    la = -ra * sp_c
    a = jnp.exp(la)
    x2 = 2.0 * la
    series = -x2 * (1.0 + x2 * (0.5 + x2 * (1.0 / 6.0 + x2 * (1.0 / 24.0))))
    m2 = jnp.where(x2 > -0.01, series, 1.0 - a * a)
    mult = jnp.where(m2 > 0.0, m2 * lax.rsqrt(m2), 0.0)
    return ra, ii, a, mult


def _down(cur, prev, s, row):
    return pltpu.roll(jnp.where(row < SUB - s, cur, prev), s, 0)


def _up(cur, nxt, s, row):
    return pltpu.roll(jnp.where(row >= s, cur, nxt), SUB - s, 0)


def _scan8_fwd(a, b, row):
    for s in (1, 2, 4):
        m = row >= s
        a_sh = pltpu.roll(a, s, 0)
        b_sh = pltpu.roll(b, s, 0)
        b = jnp.where(m, a * b_sh + b, b)
        a = jnp.where(m, a * a_sh, a)
    return a, b


def _scan8_rev(a, b, row):
    for s in (1, 2, 4):
        m = row < SUB - s
        a_sh = pltpu.roll(a, SUB - s, 0)
        b_sh = pltpu.roll(b, SUB - s, 0)
        b = jnp.where(m, a * b_sh + b, b)
        a = jnp.where(m, a * a_sh, a)
    return a, b


def _group_mask(shape):
    r = lax.broadcasted_iota(jnp.int32, shape, 0)
    c = lax.broadcasted_iota(jnp.int32, shape, 1)
    return ((r % GROUP) // HEAD_DIM) == (c // HEAD_DIM)


def _expand_heads(w):
    j = lax.broadcasted_iota(jnp.int32, (HEAD_DIM, GROUP), 0)
    c = lax.broadcasted_iota(jnp.int32, (HEAD_DIM, GROUP), 1)
    spread = (c % HEAD_DIM == j).astype(BF16)
    e = jnp.dot(w.astype(BF16), spread, preferred_element_type=F32)
    return jnp.where(_group_mask(e.shape), e, 0.0).astype(BF16)


def _fold_heads(p):
    p = jnp.where(_group_mask(p.shape), p, 0.0)
    c = lax.broadcasted_iota(jnp.int32, (GROUP, HEAD_DIM), 0)
    j = lax.broadcasted_iota(jnp.int32, (GROUP, HEAD_DIM), 1)
    fold = (c % HEAD_DIM == j).astype(BF16)
    hi = p.astype(BF16)
    rest = p - hi.astype(F32)
    mid = rest.astype(BF16)
    lo = (rest - mid.astype(F32)).astype(BF16)
    dot = functools.partial(jnp.dot, preferred_element_type=F32)
    return dot(hi, fold) + dot(mid, fold) + dot(lo, fold)


def _block_diag_apply(xb, wbd_ref):
    parts = [jnp.dot(xb[:, g * GROUP:(g + 1) * GROUP], wbd_ref[g * GROUP:(g + 1) * GROUP, :],
                     preferred_element_type=F32) for g in range(LRU_WIDTH // GROUP)]
    return jnp.concatenate(parts, axis=1)


def _block_diag_apply_t(db, wbd_ref):
    parts = [lax.dot_general(db[:, g * GROUP:(g + 1) * GROUP], wbd_ref[g * GROUP:(g + 1) * GROUP, :],
                             (((1,), (1,)), ((), ())), preferred_element_type=F32)
             for g in range(LRU_WIDTH // GROUP)]
    return jnp.concatenate(parts, axis=1)


def _dot_nt(a, b):
    return lax.dot_general(a, b, (((1,), (1,)), ((), ())), preferred_element_type=F32)


def _dot_tn(a, b):
    return lax.dot_general(a, b, (((0,), (0,)), ((), ())), preferred_element_type=F32)


def _chunk_loop(n_chunks, chunk, init, in_flight=4):
    def body(k, carry):
        for j in range(in_flight):
            carry = chunk(k * in_flight + j, carry)
        return carry

    return lax.fori_loop(0, n_chunks // in_flight, body, init)


def _place():
    x, y, c = lax.axis_index("x"), lax.axis_index("y"), lax.axis_index("c")
    return x, y, c


def _block_id(chip, core):
    return 4 * chip[0] + 2 * chip[1] + core


def _other_chips(x, y):
    return [(1 - x, y), (x, 1 - y), (1 - x, 1 - y)]


def _remote_copy(src, dst, send_sem, recv_sem, to):
    return pltpu.make_async_remote_copy(src_ref=src, dst_ref=dst, send_sem=send_sem, recv_sem=recv_sem,
                                        device_id=to, device_id_type=MESH)


HBM_SPEC = pl.BlockSpec(memory_space=pl.ANY)


def _in_hbm(*arrays):
    return [pltpu.with_memory_space_constraint(a, pltpu.HBM) for a in arrays]


def _prep_shards(w_in_t, w_out, w_mlp_in, w_mlp_out, conv_w, rnn_conv_w):
    def body(win_ref, wout_ref, w1_ref, w2_ref, cw_ref, rw_ref, o_win, o_wout, o_w1, o_w2, o_cp):
        o_win[...] = win_ref[...].astype(BF16)
        o_wout[...] = wout_ref[...].astype(BF16)
        o_w1[...] = w1_ref[...].astype(BF16)
        o_w2[...] = w2_ref[...].astype(BF16)
        o_cp[...] = jnp.zeros(o_cp.shape, F32)
        o_cp[0:3, 0:64] = cw_ref[...]
        o_cp[3:7, :] = rw_ref[...]

    whole = lambda shape: pl.BlockSpec(shape, lambda i: (0,) * len(shape))
    args = (w_in_t, w_out, w_mlp_in, w_mlp_out, conv_w, rnn_conv_w)
    shapes = [(w_in_t.shape, BF16), (w_out.shape, BF16), (w_mlp_in.shape, BF16), (w_mlp_out.shape, BF16),
              ((8, 128), F32)]
    return pl.pallas_call(
        body, grid=(1,), out_shape=[jax.ShapeDtypeStruct(s, d) for s, d in shapes],
        in_specs=[whole(a.shape) for a in args], out_specs=[whole(s) for s, _ in shapes],
        compiler_params=_params(("arbitrary",), 40), name="prep_shards",
    )(*args)


def _host_all_gather(step, n_steps, shards, fulls, send_sems, recv_sems, local_sems):
    x, y, c = _place()
    me = (x, y, c)
    my_id = _block_id((x, y), c)
    sibling = (x, y, 1 - c)
    chips = _other_chips(x, y)
    n_arr = len(shards)

    def copy(arr, k, block, to, src=None):
        dst = fulls[arr].at[block]
        return _remote_copy(dst if src is None else src, dst, send_sems.at[arr, k], recv_sems.at[arr, k], to)

    def local(arr):
        return pltpu.make_async_copy(shards[arr], fulls[arr].at[my_id], local_sems.at[arr])

    @pl.when(step == 0)
    def _():
        for arr in range(n_arr):
            local(arr).start()
            copy(arr, 0, my_id, sibling, shards[arr]).start()
            for j, chip in enumerate(chips):
                copy(arr, 1 + j, my_id, (*chip, c), shards[arr]).start()

    @pl.when(step == max(n_steps - 2, 0))
    def _():
        for j, chip in enumerate(chips):
            for arr in range(n_arr):
                copy(arr, 1 + j, _block_id(chip, c), me).wait_recv()
                copy(arr, 4 + j, _block_id(chip, c), sibling).start()

    @pl.when(step == n_steps - 1)
    def _():
        for arr in range(n_arr):
            copy(arr, 0, _block_id((x, y), 1 - c), me).wait_recv()
            for j, chip in enumerate(chips):
                copy(arr, 4 + j, _block_id(chip, 1 - c), me).wait_recv()
            for k in range(4):
                copy(arr, k, my_id, me, shards[arr]).wait_send()
            for j, chip in enumerate(chips):
                copy(arr, 4 + j, _block_id(chip, c), me).wait_send()
            local(arr).wait()


def _host_pair_exchange(step, n_steps, gs, sibs, send_sems, recv_sems):
    x, y, c = _place()
    sibling = (x, y, 1 - c)
    chips = [(x, y)] + _other_chips(x, y)

    def d2d(arr, q):
        return _remote_copy(gs[arr].at[_block_id(chips[q], 1 - c)], sibs[arr].at[q],
                            send_sems.at[arr, q], recv_sems.at[arr, q], sibling)

    @pl.when(step == 0)
    def _():
        for arr in range(len(gs)):
            for q in (1, 2, 3, 0):
                d2d(arr, q).start()

    @pl.when(step == n_steps - 1)
    def _():
        for arr in range(len(gs)):
            for q in range(4):
                d2d(arr, q).wait()


def _host_chip_exchange(step, n_steps, hsends, hrecvs, send_sems, recv_sems):
    x, y, c = _place()
    chips = _other_chips(x, y)

    def ici(arr, j):
        return _remote_copy(hsends[arr].at[j], hrecvs[arr].at[j], send_sems.at[arr, j], recv_sems.at[arr, j],
                            (*chips[j], c))

    @pl.when(step == 0)
    def _():
        for arr in range(len(hsends)):
            for j in range(3):
                ici(arr, j).start()

    @pl.when(step == n_steps - 1)
    def _():
        for arr in range(len(hsends)):
            for j in range(3):
                ici(arr, j).wait()


def _host_half_exchange(step, n_steps, parts, sibs, send_sems, recv_sems):
    x, y, c = _place()
    n_q, rows2, _ = parts.shape
    half = rows2 // 2

    def d2d(q):
        src = parts.at[q, pl.ds(pl.multiple_of((1 - c) * half, 16), half), :]
        return _remote_copy(src, sibs.at[q], send_sems.at[q], recv_sems.at[q], (x, y, 1 - c))

    @pl.when(step == 0)
    def _():
        for q in range(n_q):
            d2d(q).start()

    @pl.when(step == n_steps - 1)
    def _():
        for q in range(n_q):
            d2d(q).wait()


def _peer(x, y, c, k):
    return (x ^ ((k >> 2) & 1), y ^ ((k >> 1) & 1), c ^ (k & 1))


def _host_small_exchange(step, n_steps, vec_m, vec_b, wab, vrecv_m, vrecv_b, wrecv, send_sems, recv_sems, local_sems):
    x, y, c = _place()
    my_id = _block_id((x, y), c)
    wrows = wab.shape[0] // N_DEV

    def copies(k):
        to = _peer(x, y, c, k)
        block = wab.at[pl.ds(pl.multiple_of(_block_id(to[0:2], to[2]) * wrows, SUB), wrows), :]
        return [_remote_copy(vec_m, vrecv_m.at[my_id], send_sems.at[0, k], recv_sems.at[0, k], to),
                _remote_copy(vec_b, vrecv_b.at[my_id], send_sems.at[1, k], recv_sems.at[1, k], to),
                _remote_copy(block, wrecv.at[k], send_sems.at[2, k], recv_sems.at[2, k], to)]

    mine = [pltpu.make_async_copy(vec_m, vrecv_m.at[my_id], local_sems.at[0]),
            pltpu.make_async_copy(vec_b, vrecv_b.at[my_id], local_sems.at[1])]

    @pl.when(step == 0)
    def _():
        for cp in mine:
            cp.start()
        for k in range(1, N_DEV):
            for cp in copies(k):
                cp.start()

    @pl.when(step == n_steps - 1)
    def _():
        for k in range(1, N_DEV):
            for cp in copies(k):
                cp.wait()
        for cp in mine:
            cp.wait()


def _pair_sum_parts(parts, sibs, core):
    n_q, rows2, cols = parts.shape
    half = rows2 // 2

    def body(core_ref, g_ref, s_ref, o_ref):
        o_ref[0] = (g_ref[0, 0].astype(F32) + s_ref[0].astype(F32)).astype(BF16)

    block = (1, half, cols)
    grid_spec = pltpu.PrefetchScalarGridSpec(
        num_scalar_prefetch=1, grid=(n_q,),
        in_specs=[pl.BlockSpec((1, 1, half, cols), lambda q, cr: (q, cr[0], 0, 0)),
                  pl.BlockSpec(block, lambda q, cr: (q, 0, 0))],
        out_specs=pl.BlockSpec(block, lambda q, cr: (q, 0, 0)))
    return pl.pallas_call(
        body, grid_spec=grid_spec, out_shape=pltpu.HBM((n_q, half, cols), BF16),
        compiler_params=_params(("arbitrary",), 32), name="pair_sum_w_in",
    )(core, *_in_hbm(parts.reshape(n_q, 2, half, cols), sibs))


def _pair_sum(gs, sibs, name):
    n_arr = len(gs)
    x, y, c = _place()
    slots = jnp.stack([_block_id(chip, c) for chip in [(x, y)] + _other_chips(x, y)]).astype(jnp.int32)

    def body(slots_ref, *refs):
        q = pl.program_id(0)
        for k in range(n_arr):
            g_ref, sib_ref = refs[2 * k:2 * k + 2]
            hs_ref, own_ref = refs[2 * n_arr + 2 * k:2 * n_arr + 2 * k + 2]
            both = g_ref[0].astype(F32) + sib_ref[0].astype(F32)

            @pl.when(q == 0)
            def _(own_ref=own_ref, both=both):
                own_ref[...] = both

            @pl.when(q > 0)
            def _(hs_ref=hs_ref, both=both):
                hs_ref[0] = both.astype(BF16)

    in_specs, out_specs, out_shape, args = [], [], [], []
    for g, sib in zip(gs, sibs):
        _, rows, cols = g.shape
        block = (1, rows, cols)
        in_specs += [pl.BlockSpec(block, lambda q, s: (s[q], 0, 0)), pl.BlockSpec(block, lambda q, s: (q, 0, 0))]
        out_specs += [pl.BlockSpec(block, lambda q, s: (jnp.maximum(q - 1, 0), 0, 0)),
                      pl.BlockSpec((rows, cols), lambda q, s: (0, 0))]
        out_shape += [pltpu.HBM((3, rows, cols), BF16), pltpu.HBM((rows, cols), F32)]
        args += _in_hbm(g, sib)
    grid_spec = pltpu.PrefetchScalarGridSpec(num_scalar_prefetch=1, grid=(4,), in_specs=in_specs, out_specs=out_specs)
    return pl.pallas_call(
        body, grid_spec=grid_spec, out_shape=out_shape,
        compiler_params=_params(("arbitrary",), 40), name=name,
    )(slots, *args)


def _exchange_scratch(n_arr, n_copies):
    return [pltpu.SemaphoreType.DMA((n_arr, n_copies)), pltpu.SemaphoreType.DMA((n_arr, n_copies))]


def _final_small(vrecv_m, vrecv_b, wab, wrecv, vec_x):
    wrows = wab.shape[0] // N_DEV

    def body(vm_ref, vb_ref, w_ref, wr_ref, vx_ref, o_vec, o_w, xrecv, wred, x_send, x_recv, b_send, b_recv):
        x, y, c = _place()
        my_id = _block_id((x, y), c)
        my_rows = pl.ds(pl.multiple_of(my_id * wrows, SUB), wrows)

        def xcopy(k):
            return _remote_copy(vx_ref, xrecv.at[my_id], x_send.at[k], x_recv.at[k], _peer(x, y, c, k))

        def bcopy(k):
            return _remote_copy(wred, o_w.at[my_rows, :], b_send.at[k], b_recv.at[k], _peer(x, y, c, k))

        xrecv[my_id] = vx_ref[...]
        for k in range(1, N_DEV):
            xcopy(k).start()
        red = w_ref[my_rows, :]
        for k in range(1, N_DEV):
            red = red + wr_ref[k]
        wred[...] = red
        o_w[my_rows, :] = red
        for k in range(1, N_DEV):
            bcopy(k).start()
        for k in range(1, N_DEV):
            xcopy(k).wait_recv()
        for rows, ref in ((slice(0, 8), vm_ref), (slice(8, 24), vb_ref), (slice(24, 32), xrecv)):
            tot = ref[0]
            for s in range(1, N_DEV):
                tot = tot + ref[s]
            o_vec[rows, :] = tot
        for k in range(1, N_DEV):
            bcopy(k).wait_recv()
        for k in range(1, N_DEV):
            xcopy(k).wait_send()
            bcopy(k).wait_send()

    vm = pl.BlockSpec(memory_space=pltpu.VMEM)
    dma8 = pltpu.SemaphoreType.DMA((N_DEV,))
    return pl.pallas_call(
        body, out_shape=(jax.ShapeDtypeStruct((VEC_ROWS, D_MODEL), F32), jax.ShapeDtypeStruct(wab.shape, F32)),
        in_specs=[vm] * 5, out_specs=[vm] * 2,
        scratch_shapes=[pltpu.VMEM((N_DEV, SUB, D_MODEL), F32), pltpu.VMEM((wrows, HEAD_DIM), F32),
                        dma8, dma8, dma8, dma8],
        compiler_params=_params(vmem_mib=32), name="final_small",
    )(vrecv_m, vrecv_b, wab, wrecv, vec_x)


def _in_proj(x, g_mix, shards, tm):
    t_len = x.shape[0]
    n_t = t_len // tm
    n_arr = len(shards)
    rows = [s.shape[0] for s in shards]
    width = 2 * rows[0]
    ax, ay = lax.axis_index("x"), lax.axis_index("y")
    order = jnp.stack([2 * cx + cy for cx, cy in [(ax, ay)] + _other_chips(ax, ay)]).astype(jnp.int32)

    def body(order_ref, x_ref, g_ref, *rest):
        shard_refs = rest[0:n_arr]
        u_ref, h_ref = rest[n_arr:n_arr + 2]
        fulls = rest[n_arr + 2:2 * n_arr + 2]
        h_s, wbuf, send_sems, recv_sems, local_sems, load_sem = rest[2 * n_arr + 2:]
        p = pl.program_id(0)
        i = pl.program_id(1)
        x_, y_, c = _place()
        me = (x_, y_, c)
        my_id = _block_id((x_, y_), c)
        sibling = (x_, y_, 1 - c)
        chips = _other_chips(x_, y_)

        def block(arr, blk):
            return fulls[arr].at[pl.ds(pl.multiple_of(blk * rows[arr], rows[arr]), rows[arr]), :]

        def copy(arr, k, blk, to, src=None):
            dst = block(arr, blk)
            return _remote_copy(dst if src is None else src, dst, send_sems.at[arr, k], recv_sems.at[arr, k], to)

        def local(arr):
            return pltpu.make_async_copy(shard_refs[arr], block(arr, my_id), local_sems.at[arr])

        def load_chip(chip):
            start = pl.multiple_of((2 * chip[0] + chip[1]) * width, width)
            cp = pltpu.make_async_copy(fulls[0].at[pl.ds(start, width), :], wbuf, load_sem.at[0])
            cp.start()
            cp.wait()

        @pl.when((p == 0) & (i == 0))
        def _():
            for arr in range(n_arr):
                local(arr).start()
                copy(arr, 0, my_id, sibling, shard_refs[arr]).start()
                for j in (0, 1):
                    copy(arr, 1 + j, my_id, (*chips[j], c), shard_refs[arr]).start()
            for arr in range(n_arr):
                local(arr).wait()
                copy(arr, 0, _block_id((x_, y_), 1 - c), me).wait_recv()
            load_chip((x_, y_))

        for j, chip in enumerate(chips):
            @pl.when((p == j + 1) & (i == 0))
            def _(j=j, chip=chip):
                for arr in range(n_arr):
                    copy(arr, 1 + j, _block_id(chip, c), me).wait_recv()
                    copy(arr, 4 + j, _block_id(chip, c), sibling).start()
                    if j == 0:
                        copy(arr, 3, my_id, (*chips[2], c), shard_refs[arr]).start()
                for arr in range(n_arr):
                    copy(arr, 4 + j, _block_id(chip, 1 - c), me).wait_recv()
                load_chip(chip)

        @pl.when((p == 3) & (i == n_t - 1))
        def _():
            for arr in range(n_arr):
                for k in range(4):
                    copy(arr, k, my_id, me, shard_refs[arr]).wait_send()
                for j, chip in enumerate(chips):
                    copy(arr, 4 + j, _block_id(chip, c), me).wait_send()

        tile = pl.ds(pl.multiple_of(i * tm, tm), tm)

        @pl.when(p == 0)
        def _():
            xv = x_ref[...]
            h = (xv * _rms(xv) * g_ref[...]).astype(BF16)
            h_ref[...] = h
            h_s[tile, :] = h

        u_ref[...] = _dot_nt(h_s[tile, :], wbuf[...])

    first_pass = lambda p, i, o: (jnp.where(p == 0, i, n_t - 1), 0)
    grid_spec = pltpu.PrefetchScalarGridSpec(
        num_scalar_prefetch=1, grid=(4, n_t),
        in_specs=[pl.BlockSpec((tm, D_MODEL), first_pass), pl.BlockSpec((1, D_MODEL), lambda p, i, o: (0, 0))]
        + [HBM_SPEC] * n_arr,
        out_specs=[pl.BlockSpec((tm, width), lambda p, i, o: (i, o[p])), pl.BlockSpec((tm, D_MODEL), first_pass)]
        + [HBM_SPEC] * n_arr,
        scratch_shapes=[pltpu.VMEM((t_len, D_MODEL), BF16), pltpu.VMEM((width, D_MODEL), BF16)]
        + _exchange_scratch(n_arr, 7) + [pltpu.SemaphoreType.DMA((n_arr,)), pltpu.SemaphoreType.DMA((1,))])
    return pl.pallas_call(
        body, grid_spec=grid_spec,
        out_shape=[jax.ShapeDtypeStruct((t_len, IN_COLS), F32), jax.ShapeDtypeStruct((t_len, D_MODEL), BF16)]
        + [jax.ShapeDtypeStruct((N_DEV * s.shape[0], s.shape[1]), s.dtype) for s in shards],
        compiler_params=_params(("arbitrary", "arbitrary"), 48), name="in_proj",
    )(order, x, g_mix, *shards)


def _conv3_chunk(u_ref, r, cv_prev, cw, row):
    gb = u_ref[pl.ds(r, SUB), OFF_GB:OFF_GB + CONV_WIDTH]
    gc = u_ref[pl.ds(r, SUB), OFF_GC:OFF_GC + CONV_WIDTH]
    v = u_ref[pl.ds(r, SUB), OFF_V:OFF_V + CONV_WIDTH]
    cv = gc * v
    cv_m1 = _down(cv, cv_prev, 1, row)
    cv_m2 = _down(cv, cv_prev, 2, row)
    cq = cw[2:3, :] * cv + cw[1:2, :] * cv_m1 + cw[0:1, :] * cv_m2
    return gb, gc, v, cv, cv_m1, cv_m2, cq


def _conv4_chunk(u_ref, r, xin_prev, rw, rb, row):
    xin = u_ref[pl.ds(r, SUB), OFF_XR:OFF_XR + LRU_WIDTH]
    m1 = _down(xin, xin_prev, 1, row)
    m2 = _down(xin, xin_prev, 2, row)
    m3 = _down(xin, xin_prev, 3, row)
    xr = rw[3:4, :] * xin + rw[2:3, :] * m1 + rw[1:2, :] * m2 + rw[0:1, :] * m3 + rb
    return xin, m1, m2, m3, xr


def _mixer_fwd(u, conv_w, rnn_conv_w, rnn_conv_b, wa, b_a, wx, b_x, lam, gnc, gnr, shards, tm):
    t_len = u.shape[0]
    n_steps = t_len // tm
    n_chunks = tm // SUB
    n_arr = len(shards)

    def body(u_ref, cw_ref, rw_ref, rb_ref, wa_ref, ba_ref, wx_ref, bx_ref, lam_ref, gnc_ref, gnr_ref, *rest):
        shard_refs = rest[0:n_arr]
        hs_ref, y_ref, xr_s, ra_ref, ii_ref, mult_ref = rest[n_arr:n_arr + 6]
        fulls = rest[n_arr + 6:2 * n_arr + 6]
        (y_s, pa_s, px_s, wabd, wxbd, cv_car, xin_car, h_car,
         send_sems, recv_sems, local_sems) = rest[2 * n_arr + 6:]
        _host_all_gather(pl.program_id(0), n_steps, shard_refs, fulls, send_sems, recv_sems, local_sems)

        @pl.when(pl.program_id(0) == 0)
        def _():
            cv_car[...] = jnp.zeros(cv_car.shape, F32)
            xin_car[...] = jnp.zeros(xin_car.shape, F32)
            h_car[...] = jnp.zeros(h_car.shape, F32)
            wabd[...] = _expand_heads(wa_ref[...])
            wxbd[...] = _expand_heads(wx_ref[...])

        row_c = lax.broadcasted_iota(jnp.int32, (SUB, CONV_WIDTH), 0)
        row_r = lax.broadcasted_iota(jnp.int32, (SUB, LRU_WIDTH), 0)
        cw = cw_ref[...]
        rw = rw_ref[...]
        rb = rb_ref[...]
        g_c = gnc_ref[...]
        g_r = gnr_ref[...]
        sp_c = LRU_C * _softplus_neg(lam_ref[...])

        def convs(i, carry):
            cv_prev, xin_prev = carry
            r = pl.multiple_of(i * SUB, SUB)
            gb, _, _, cv, _, _, cq = _conv3_chunk(u_ref, r, cv_prev, cw, row_c)
            y_c = gb * cq
            y_s[pl.ds(r, SUB), 0:CONV_WIDTH] = y_c * _rms(y_c) * g_c
            xin, _, _, _, xr = _conv4_chunk(u_ref, r, xin_prev, rw, rb, row_r)
            xr_s[pl.ds(r, SUB), :] = xr
            return cv, xin

        cv_last, xin_last = _chunk_loop(n_chunks, convs, (cv_car[...], xin_car[...]), in_flight=8)
        cv_car[...] = cv_last
        xin_car[...] = xin_last

        xrb = xr_s[...].astype(BF16)
        pa_s[...] = _block_diag_apply(xrb, wabd) + ba_ref[...]
        px_s[...] = _block_diag_apply(xrb, wxbd) + bx_ref[...]

        def recur(i, h_prev):
            r = pl.multiple_of(i * SUB, SUB)
            xr = xr_s[pl.ds(r, SUB), :]
            ra, ii, a, mult = _lru_gates(pa_s[pl.ds(r, SUB), :], px_s[pl.ds(r, SUB), :], sp_c)
            ra_ref[pl.ds(r, SUB), :] = ra
            ii_ref[pl.ds(r, SUB), :] = ii
            mult_ref[pl.ds(r, SUB), :] = mult
            a_cum, b_cum = _scan8_fwd(a, mult * ii * xr, row_r)
            h = a_cum * h_prev + b_cum
            hs_ref[pl.ds(r, SUB), :] = h
            ge, _ = _gelu(u_ref[pl.ds(r, SUB), OFF_G:OFF_G + LRU_WIDTH])
            y_r = h * ge
            y_s[pl.ds(r, SUB), CONV_WIDTH:MIX_WIDTH] = y_r * _rms(y_r) * g_r
            return h[SUB - 1:SUB, :]

        h_car[...] = _chunk_loop(n_chunks, recur, h_car[...], in_flight=8)

        y_ref[...] = y_s[...].astype(BF16)

    row_tile = lambda w: pl.BlockSpec((tm, w), lambda i: (i, 0))
    whole = lambda a: pl.BlockSpec(a.shape, lambda i: (0,) * a.ndim)
    smalls = (conv_w, rnn_conv_w, rnn_conv_b, wa, b_a, wx, b_x, lam, gnc, gnr)
    return pl.pallas_call(
        body, grid=(n_steps,),
        in_specs=[row_tile(IN_COLS)] + [whole(a) for a in smalls] + [HBM_SPEC] * n_arr,
        out_specs=[row_tile(LRU_WIDTH), row_tile(MIX_WIDTH)] + [row_tile(LRU_WIDTH)] * 4 + [HBM_SPEC] * n_arr,
        out_shape=[jax.ShapeDtypeStruct((t_len, LRU_WIDTH), F32), jax.ShapeDtypeStruct((t_len, MIX_WIDTH), BF16)]
        + [jax.ShapeDtypeStruct((t_len, LRU_WIDTH), F32)] * 4
        + [jax.ShapeDtypeStruct((N_DEV,) + s.shape, BF16) for s in shards],
        scratch_shapes=[pltpu.VMEM((tm, MIX_WIDTH), F32),
                        pltpu.VMEM((tm, LRU_WIDTH), F32), pltpu.VMEM((tm, LRU_WIDTH), F32),
                        pltpu.VMEM((LRU_WIDTH, GROUP), BF16), pltpu.VMEM((LRU_WIDTH, GROUP), BF16),
                        pltpu.VMEM((SUB, CONV_WIDTH), F32), pltpu.VMEM((SUB, LRU_WIDTH), F32),
                        pltpu.VMEM((1, LRU_WIDTH), F32)]
        + _exchange_scratch(n_arr, 7) + [pltpu.SemaphoreType.DMA((n_arr,))],
        compiler_params=_params(("arbitrary",), 56), name="mixer_fwd",
    )(u, *smalls, *shards)


def _mlp_up(x, y, g_mlp, w_out, w1, w2_shard, tm):
    t_len = x.shape[0]
    n_steps = t_len // tm
    n_blk, _, blk = w1.shape

    def body(x_ref, y_ref, gm_ref, wout_hbm, w1_hbm, w2_ref, x1_ref, h2_ref, z_ref, w2_full,
             wout_s, w1_s, sem, send_sems, recv_sems, local_sems):
        step = pl.program_id(0)
        _host_all_gather(step, n_steps, [w2_ref], [w2_full], send_sems, recv_sems, local_sems)

        load_wout = pltpu.make_async_copy(wout_hbm, wout_s, sem.at[0])
        load_w1 = pltpu.make_async_copy(w1_hbm, w1_s, sem.at[1])

        @pl.when(step == 0)
        def _():
            load_wout.start()
            load_w1.start()
            load_wout.wait()

        x1v = x_ref[...] + jnp.dot(y_ref[...], wout_s[...], preferred_element_type=F32)
        x1_ref[...] = x1v
        h2 = (x1v * _rms(x1v) * gm_ref[...]).astype(BF16)
        h2_ref[...] = h2

        @pl.when(step == 0)
        def _():
            load_w1.wait()

        for k in range(n_blk):
            rp = jnp.maximum(jnp.dot(h2, w1_s[k], preferred_element_type=F32), 0.0)
            z_ref[:, k * blk:(k + 1) * blk] = (rp * rp).astype(BF16)

    row_tile = lambda w: pl.BlockSpec((tm, w), lambda i: (i, 0))
    return pl.pallas_call(
        body, grid=(n_steps,),
        in_specs=[row_tile(D_MODEL), row_tile(MIX_WIDTH), pl.BlockSpec((1, D_MODEL), lambda i: (0, 0)),
                  HBM_SPEC, HBM_SPEC, HBM_SPEC],
        out_specs=[row_tile(D_MODEL), row_tile(D_MODEL), row_tile(D_FF), HBM_SPEC],
        out_shape=[jax.ShapeDtypeStruct((t_len, D_MODEL), F32), jax.ShapeDtypeStruct((t_len, D_MODEL), BF16),
                   jax.ShapeDtypeStruct((t_len, D_FF), BF16), jax.ShapeDtypeStruct((N_DEV,) + w2_shard.shape, BF16)],
        scratch_shapes=[pltpu.VMEM(w_out.shape, BF16), pltpu.VMEM(w1.shape, BF16), pltpu.SemaphoreType.DMA((2,))]
        + _exchange_scratch(1, 7) + [pltpu.SemaphoreType.DMA((1,))],
        compiler_params=_params(("arbitrary",), 48), name="mlp_up",
    )(x, y, g_mlp, w_out, w1, w2_shard)


def _mlp_down_bwd(x1, z, target, g_mlp, g_f, w1, w2, tm):
    t_len = x1.shape[0]
    n_steps = t_len // tm
    n_blk, _, blk = w1.shape

    def body(x1_ref, z_ref, tg_ref, gm_ref, gf_ref, w1_hbm, w2_hbm, dx1_ref, dx2_ref, vec_ref, dpre_hbm,
             w1_s, w2_s, dp_s, sem, out_sem):
        step = pl.program_id(0)
        rows = pl.ds(pl.multiple_of(step * tm, tm), tm)
        dp_out = pltpu.make_async_copy(dp_s, dpre_hbm.at[rows, :], out_sem.at[0])

        load_w1 = pltpu.make_async_copy(w1_hbm, w1_s, sem.at[0])
        load_w2 = pltpu.make_async_copy(w2_hbm, w2_s, sem.at[1])

        @pl.when(step == 0)
        def _():
            load_w2.start()
            load_w1.start()
            vec_ref[...] = jnp.zeros(vec_ref.shape, F32)
            load_w2.wait()

        x1v = x1_ref[...]
        g_m = gm_ref[...]
        g_o = gf_ref[...]
        r2 = _rms(x1v)
        x1h = x1v * r2
        x2 = x1v + jnp.dot(z_ref[...], w2_s[...], preferred_element_type=F32)
        r3 = _rms(x2)
        x2h = x2 * r3
        err = x2h * g_o - tg_ref[...]
        dout = err * (1.0 / D_MODEL)
        vec_ref[ROW_LOSS:ROW_LOSS + 1, :] += (0.5 / D_MODEL) * jnp.sum(err * err, axis=0, keepdims=True)
        vec_ref[ROW_GF:ROW_GF + 1, :] += jnp.sum(dout * x2h, axis=0, keepdims=True)
        dx2 = _rms_bwd(dout, x2h, r3, g_o)
        dx2b = dx2.astype(BF16)
        dx2_ref[...] = dx2b
        dh2 = jnp.zeros((tm, D_MODEL), F32)

        @pl.when(step > 0)
        def _():
            dp_out.wait()

        @pl.when(step == 0)
        def _():
            load_w1.wait()

        for k in range(n_blk):
            cols = slice(k * blk, (k + 1) * blk)
            dz = _dot_nt(dx2b, w2_s[cols, :])
            dpb = (dz * 2.0 * jnp.sqrt(z_ref[:, cols].astype(F32))).astype(BF16)
            dp_s[:, cols] = dpb
            dh2 = dh2 + _dot_nt(dpb, w1_s[k])
        dp_out.start()
        vec_ref[ROW_GMLP:ROW_GMLP + 1, :] += jnp.sum(dh2 * x1h, axis=0, keepdims=True)
        dx1_ref[...] = dx2 + _rms_bwd(dh2, x1h, r2, g_m)

        @pl.when(step == n_steps - 1)
        def _():
            dp_out.wait()

    row_tile = lambda w: pl.BlockSpec((tm, w), lambda i: (i, 0))
    vec_spec = pl.BlockSpec((1, D_MODEL), lambda i: (0, 0))
    return pl.pallas_call(
        body, grid=(n_steps,),
        in_specs=[row_tile(D_MODEL), row_tile(D_FF), row_tile(D_MODEL), vec_spec, vec_spec, HBM_SPEC, HBM_SPEC],
        out_specs=[row_tile(D_MODEL), row_tile(D_MODEL), pl.BlockSpec((SUB, D_MODEL), lambda i: (0, 0)), HBM_SPEC],
        out_shape=[jax.ShapeDtypeStruct((t_len, D_MODEL), F32), jax.ShapeDtypeStruct((t_len, D_MODEL), BF16),
                   jax.ShapeDtypeStruct((SUB, D_MODEL), F32), jax.ShapeDtypeStruct((t_len, D_FF), BF16)],
        scratch_shapes=[pltpu.VMEM(w1.shape, BF16), pltpu.VMEM(w2.shape, BF16), pltpu.VMEM((tm, D_FF), BF16),
                        pltpu.SemaphoreType.DMA((2,)), pltpu.SemaphoreType.DMA((1,))],
        compiler_params=_params(("arbitrary",), 56), name="mlp_down_bwd",
    )(x1, z, target, g_mlp, g_f, w1, w2)


def _mixer_bwd(u, hs, dx1, saved, conv_w, rnn_conv_w, rnn_conv_b, wa, wx, lam, gnc, gnr, w_out,
               chip_sums, g_wout, tm):
    t_len = u.shape[0]
    n_tiles = t_len // tm
    n_chunks = tm // SUB
    per_tile = tm // SUB
    n_sums = len(chip_sums)

    def body(u_ref, up_ref, hs_ref, hp_ref, dx1_ref, xr_ref, ra_ref, ii_ref, mult_ref,
             cw_ref, rw_ref, rb_ref, wa_ref, wx_ref, lam_ref, gnc_ref, gnr_ref, wout_ref, *rest):
        hsends = rest[0:n_sums]
        gwout_ref = rest[n_sums]
        du_ref, vec_ref, wab_ref = rest[n_sums + 1:n_sums + 4]
        hrecvs = rest[n_sums + 4:2 * n_sums + 4]
        sib_wout = rest[2 * n_sums + 4]
        (du_s, dy_s, dpa_s, dpx_s, dxr_s, wabd, wxbd, acc, dwa_acc, dwx_acc,
         a_car, dh_car, dcq_car, dxr_car, i_send, i_recv, d_send, d_recv) = rest[2 * n_sums + 5:]
        step = pl.program_id(0)
        _host_chip_exchange(step, n_tiles, hsends, hrecvs, i_send, i_recv)
        _host_pair_exchange(step, n_tiles, [gwout_ref], [sib_wout], d_send, d_recv)
        has_prev = (step < n_tiles - 1).astype(F32)

        @pl.when(step == 0)
        def _():
            acc[...] = jnp.zeros(acc.shape, F32)
            dwa_acc[...] = jnp.zeros(dwa_acc.shape, F32)
            dwx_acc[...] = jnp.zeros(dwx_acc.shape, F32)
            a_car[...] = jnp.ones(a_car.shape, F32)
            dh_car[...] = jnp.zeros(dh_car.shape, F32)
            dcq_car[...] = jnp.zeros(dcq_car.shape, F32)
            dxr_car[...] = jnp.zeros(dxr_car.shape, F32)
            wabd[...] = _expand_heads(wa_ref[...])
            wxbd[...] = _expand_heads(wx_ref[...])

        row_c = lax.broadcasted_iota(jnp.int32, (SUB, CONV_WIDTH), 0)
        row_r = lax.broadcasted_iota(jnp.int32, (SUB, LRU_WIDTH), 0)
        cw = cw_ref[...]
        rw = rw_ref[...]
        rb = rb_ref[...]
        g_c = gnc_ref[...]
        g_r = gnr_ref[...]
        sp_c = LRU_C * _softplus_neg(lam_ref[...])

        up = up_ref[...] * has_prev
        cv_before = up[:, OFF_GC:OFF_GC + CONV_WIDTH] * up[:, OFF_V:OFF_V + CONV_WIDTH]
        xin_before = up[:, OFF_XR:OFF_XR + LRU_WIDTH]
        hs_before = hp_ref[...] * has_prev

        dy_s[...] = _dot_nt(dx1_ref[...].astype(BF16), wout_ref[...])

        xrb = xr_ref[...].astype(BF16)

        def recur_bwd(j, carry):
            a_later, dh_later = carry
            i = n_chunks - 1 - j
            r = pl.multiple_of(i * SUB, SUB)
            rp = pl.multiple_of(jnp.maximum(i - 1, 0) * SUB, SUB)
            xr = xr_ref[pl.ds(r, SUB), :]
            hs_c = hs_ref[pl.ds(r, SUB), :]
            hs_prev = jnp.where(i == 0, hs_before, hs_ref[pl.ds(rp, SUB), :])
            h_m1 = _down(hs_c, hs_prev, 1, row_r)
            ra = ra_ref[pl.ds(r, SUB), :]
            ii = ii_ref[pl.ds(r, SUB), :]
            mult = mult_ref[pl.ds(r, SUB), :]
            a = jnp.exp(-ra * sp_c)
            inv_mult = lax.rsqrt(mult * mult)
            ge, dge = _gelu(u_ref[pl.ds(r, SUB), OFF_G:OFF_G + LRU_WIDTH])
            y_r = hs_c * ge
            rr = _rms(y_r)
            yhat = y_r * rr
            dyn = dy_s[pl.ds(r, SUB), CONV_WIDTH:MIX_WIDTH]
            acc[ACC_GNR] += dyn * yhat
            dy_r = _rms_bwd(dyn, yhat, rr, g_r)
            du_s[pl.ds(r, SUB), OFF_G:OFF_G + LRU_WIDTH] = dy_r * hs_c * dge
            a_cum, d_cum = _scan8_rev(_up(a, a_later, 1, row_r), dy_r * ge, row_r)
            dh = a_cum * dh_later + d_cum
            dm = dh * mult
            dii = dm * xr
            dxr_s[pl.ds(r, SUB), :] = dm * ii
            dla = a * dh * (h_m1 - (ii * xr) * a * inv_mult)
            dla_r = dla * ra
            acc[ACC_SP] -= dla_r
            dpa = dla_r * (sp_c * (ra - 1.0))
            dpx = dii * ii * (1.0 - ii)
            acc[ACC_BA] += dpa
            acc[ACC_BX] += dpx
            dpa_s[pl.ds(r, SUB), :] = dpa
            dpx_s[pl.ds(r, SUB), :] = dpx
            return a, dh[0:1, :]

        a_first, dh_first = _chunk_loop(n_chunks, recur_bwd, (a_car[...], dh_car[...]), in_flight=8)
        a_car[...] = a_first
        dh_car[...] = dh_first

        dpab = dpa_s[...].astype(BF16)
        dpxb = dpx_s[...].astype(BF16)
        dxr_s[...] += _block_diag_apply_t(dpab, wabd) + _block_diag_apply_t(dpxb, wxbd)
        for g in range(LRU_WIDTH // GROUP):
            cols = slice(g * GROUP, (g + 1) * GROUP)
            dwa_acc[cols, :] += _dot_tn(xrb[:, cols], dpab[:, cols])
            dwx_acc[cols, :] += _dot_tn(xrb[:, cols], dpxb[:, cols])

        def convs_bwd(j, carry):
            dcq_later, dxr_later = carry
            i = n_chunks - 1 - j
            r = pl.multiple_of(i * SUB, SUB)
            rp = pl.multiple_of(jnp.maximum(i - 1, 0) * SUB, SUB)
            cv_prev = jnp.where(i == 0, cv_before,
                                u_ref[pl.ds(rp, SUB), OFF_GC:OFF_GC + CONV_WIDTH]
                                * u_ref[pl.ds(rp, SUB), OFF_V:OFF_V + CONV_WIDTH])
            gb, gc, v, cv, cv_m1, cv_m2, cq = _conv3_chunk(u_ref, r, cv_prev, cw, row_c)
            y_c = gb * cq
            rc = _rms(y_c)
            yhat = y_c * rc
            dyn = dy_s[pl.ds(r, SUB), 0:CONV_WIDTH]
            acc[ACC_GNC, :, 0:CONV_WIDTH] += dyn * yhat
            dy_c = _rms_bwd(dyn, yhat, rc, g_c)
            dcq = dy_c * gb
            dcv = (cw[2:3, :] * dcq + cw[1:2, :] * _up(dcq, dcq_later, 1, row_c)
                   + cw[0:1, :] * _up(dcq, dcq_later, 2, row_c))
            acc[ACC_CW + 2, :, 0:CONV_WIDTH] += dcq * cv
            acc[ACC_CW + 1, :, 0:CONV_WIDTH] += dcq * cv_m1
            acc[ACC_CW + 0, :, 0:CONV_WIDTH] += dcq * cv_m2
            du_s[pl.ds(r, SUB), OFF_GB:OFF_GB + CONV_WIDTH] = dy_c * cq
            du_s[pl.ds(r, SUB), OFF_GC:OFF_GC + CONV_WIDTH] = dcv * v
            du_s[pl.ds(r, SUB), OFF_V:OFF_V + CONV_WIDTH] = dcv * gc

            xin_prev = jnp.where(i == 0, xin_before, u_ref[pl.ds(rp, SUB), OFF_XR:OFF_XR + LRU_WIDTH])
            xin, m1, m2, m3, _ = _conv4_chunk(u_ref, r, xin_prev, rw, rb, row_r)
            dxr = dxr_s[pl.ds(r, SUB), :]
            du_s[pl.ds(r, SUB), OFF_XR:OFF_XR + LRU_WIDTH] = (
                rw[3:4, :] * dxr + rw[2:3, :] * _up(dxr, dxr_later, 1, row_r)
                + rw[1:2, :] * _up(dxr, dxr_later, 2, row_r) + rw[0:1, :] * _up(dxr, dxr_later, 3, row_r))
            acc[ACC_RW + 3] += dxr * xin
            acc[ACC_RW + 2] += dxr * m1
            acc[ACC_RW + 1] += dxr * m2
            acc[ACC_RW + 0] += dxr * m3
            acc[ACC_BR] += dxr
            return dcq, dxr

        dcq_first, dxr_first = _chunk_loop(n_chunks, convs_bwd, (dcq_car[...], dxr_car[...]), in_flight=8)
        dcq_car[...] = dcq_first
        dxr_car[...] = dxr_first

        du_ref[...] = du_s[...].astype(BF16)

        @pl.when(step == n_tiles - 1)
        def _():
            vec_ref[...] = jnp.zeros(vec_ref.shape, F32)
            rows = {ACC_GNC: ROW_GNC, ACC_GNR: ROW_GNR, ACC_BR: ROW_BR, ACC_BA: ROW_BA, ACC_BX: ROW_BX}
            for k in range(3):
                rows[ACC_CW + k] = ROW_CW + k
            for k in range(4):
                rows[ACC_RW + k] = ROW_RW + k
            for slot, out_row in rows.items():
                o = out_row - ROW_GNC
                vec_ref[o:o + 1, :] = jnp.sum(acc[slot], axis=0, keepdims=True)
            lam_v = lam_ref[...]
            dsp = jnp.sum(acc[ACC_SP], axis=0, keepdims=True)
            o = ROW_LAM - ROW_GNC
            vec_ref[o:o + 1, :] = -dsp * LRU_C / (1.0 + jnp.exp(lam_v))
            wab_ref[0:LRU_WIDTH, :] = _fold_heads(dwa_acc[...])
            wab_ref[LRU_WIDTH:2 * LRU_WIDTH, :] = _fold_heads(dwx_acc[...])

    rev = lambda w: pl.BlockSpec((tm, w), lambda s: (n_tiles - 1 - s, 0))
    before = lambda w: pl.BlockSpec((SUB, w), lambda s: (jnp.maximum((n_tiles - 1 - s) * per_tile - 1, 0), 0))
    whole = lambda a: pl.BlockSpec(a.shape, lambda s: (0,) * a.ndim)
    smalls = (conv_w, rnn_conv_w, rnn_conv_b, wa, wx, lam, gnc, gnr, w_out)
    full = lambda w: pltpu.VMEM((tm, w), F32)
    return pl.pallas_call(
        body, grid=(n_tiles,),
        in_specs=[rev(IN_COLS), before(IN_COLS), rev(LRU_WIDTH), before(LRU_WIDTH), rev(D_MODEL)]
        + [rev(LRU_WIDTH)] * len(saved) + [whole(a) for a in smalls] + [HBM_SPEC] * (n_sums + 1),
        out_specs=[rev(IN_COLS), pl.BlockSpec((16, D_MODEL), lambda s: (0, 0)),
                   pl.BlockSpec((2 * LRU_WIDTH, HEAD_DIM), lambda s: (0, 0))] + [HBM_SPEC] * (n_sums + 1),
        out_shape=[jax.ShapeDtypeStruct((t_len, IN_COLS), BF16), jax.ShapeDtypeStruct((16, D_MODEL), F32),
                   jax.ShapeDtypeStruct((2 * LRU_WIDTH, HEAD_DIM), F32)]
        + [jax.ShapeDtypeStruct(s.shape, BF16) for s in chip_sums]
        + [jax.ShapeDtypeStruct((4,) + g_wout.shape[1:], BF16)],
        scratch_shapes=[full(IN_COLS), full(MIX_WIDTH), full(LRU_WIDTH), full(LRU_WIDTH), full(LRU_WIDTH),
                        pltpu.VMEM((LRU_WIDTH, GROUP), BF16), pltpu.VMEM((LRU_WIDTH, GROUP), BF16),
                        pltpu.VMEM((N_ACC, SUB, LRU_WIDTH), F32),
                        pltpu.VMEM((LRU_WIDTH, GROUP), F32), pltpu.VMEM((LRU_WIDTH, GROUP), F32),
                        pltpu.VMEM((SUB, LRU_WIDTH), F32), pltpu.VMEM((1, LRU_WIDTH), F32),
                        pltpu.VMEM((SUB, CONV_WIDTH), F32), pltpu.VMEM((SUB, LRU_WIDTH), F32)]
        + _exchange_scratch(n_sums, 3) + _exchange_scratch(1, 4),
        compiler_params=_params(("arbitrary",), 56), name="mixer_bwd",
    )(u, u, hs, hs, dx1, *saved, *smalls, *chip_sums, g_wout)


def _in_proj_bwd(du, dx1, x, g_mix, win_t, tm, chip_sums, g_own):
    t_len = x.shape[0]
    n_steps = t_len // tm

    def body(du_ref, dx1_ref, x_ref, g_ref, w_ref, hs_ref, gown_ref,
             dx_ref, vec_ref, landed_ref, sib_ref, i_send, i_recv, d_send, d_recv):
        step = pl.program_id(0)
        _host_chip_exchange(step, n_steps, [hs_ref], [landed_ref], i_send, i_recv)
        _host_half_exchange(step, n_steps, gown_ref, sib_ref, d_send, d_recv)

        @pl.when(step == 0)
        def _():
            vec_ref[...] = jnp.zeros(vec_ref.shape, F32)

        dh = jnp.dot(du_ref[...], w_ref[...], preferred_element_type=F32)
        xv = x_ref[...]
        r1 = _rms(xv)
        xh = xv * r1
        vec_ref[0:1, :] += jnp.sum(dh * xh, axis=0, keepdims=True)
        dx_ref[...] = dx1_ref[...] + _rms_bwd(dh, xh, r1, g_ref[...])

    row_tile = lambda w: pl.BlockSpec((tm, w), lambda i: (i, 0))
    half_shape = (g_own.shape[0], g_own.shape[1] // 2, g_own.shape[2])
    return pl.pallas_call(
        body, grid=(n_steps,),
        in_specs=[row_tile(IN_COLS), row_tile(D_MODEL), row_tile(D_MODEL), pl.BlockSpec((1, D_MODEL), lambda i: (0, 0)),
                  pl.BlockSpec((IN_COLS, D_MODEL), lambda i: (0, 0))] + [HBM_SPEC] * 2,
        out_specs=[row_tile(D_MODEL), pl.BlockSpec((SUB, D_MODEL), lambda i: (0, 0))] + [HBM_SPEC] * 2,
        out_shape=[jax.ShapeDtypeStruct((t_len, D_MODEL), F32), jax.ShapeDtypeStruct((SUB, D_MODEL), F32),
                   jax.ShapeDtypeStruct(chip_sums.shape, BF16), jax.ShapeDtypeStruct(half_shape, BF16)],
        scratch_shapes=_exchange_scratch(1, 3) + [pltpu.SemaphoreType.DMA((1,)), pltpu.SemaphoreType.DMA((1,))],
        compiler_params=_params(("arbitrary",), 56), name="in_proj_bwd",
    )(du, dx1, x, g_mix, win_t, chip_sums, g_own)


def _tn_weight_grad(a, b, tk, name, pair=(), col_blocks=1):
    t_len, m = a.shape
    n = b.shape[1]
    n_steps = t_len // tk
    sent = tuple(pair)
    n_sent = len(sent)

    def body(a_ref, b_ref, *rest):
        srcs = rest[0:n_sent]
        o_ref = rest[n_sent]
        dsts = rest[n_sent + 1:2 * n_sent + 1]
        acc = rest[2 * n_sent + 1]
        sems = rest[2 * n_sent + 2:]
        j = pl.program_id(0)
        if pair:
            _host_pair_exchange(j, n_steps, srcs, dsts, *sems)

        @pl.when(j == 0)
        def _():
            acc[...] = jnp.zeros(acc.shape, F32)

        acc[...] += _dot_tn(a_ref[...].astype(BF16), b_ref[...].astype(BF16))

        @pl.when(j == n_steps - 1)
        def _():
            if col_blocks == 1:
                o_ref[...] = acc[...].astype(BF16)
            else:
                for k in range(col_blocks):
                    o_ref[k] = acc[:, k * nb:(k + 1) * nb].astype(BF16)

    nb = n // col_blocks
    out_dims = (m, n) if col_blocks == 1 else (col_blocks, m, nb)
    landed = [jax.ShapeDtypeStruct((4,) + g.shape[1:], BF16) for g in pair]
    scratch = [pltpu.VMEM((m, n), F32)]
    if n_sent:
        scratch += _exchange_scratch(n_sent, 4)
    return pl.pallas_call(
        body, grid=(n_steps,),
        in_specs=[pl.BlockSpec((tk, m), lambda j: (j, 0)), pl.BlockSpec((tk, n), lambda j: (j, 0))]
        + [HBM_SPEC] * n_sent,
        out_specs=[pl.BlockSpec(out_dims, lambda j: (0,) * len(out_dims))] + [HBM_SPEC] * n_sent,
        out_shape=[jax.ShapeDtypeStruct(out_dims, BF16)] + landed,
        scratch_shapes=scratch,
        compiler_params=_params(("arbitrary",), 56), name=name,
    )(a, b, *sent)


def _w_in_grad_part(du, h, tk, name, chip_ids, chip=(), halves=None, small=None):
    t_len = du.shape[0]
    n_t = t_len // tk
    n_q = chip_ids.shape[0]
    width = 2 * (IN_COLS // N_DEV)
    n_steps = n_q * n_t
    n_chip = len(chip)
    sent = tuple(chip) + (() if halves is None else (halves,)) + (() if small is None else tuple(small))
    n_sent = len(sent)

    def body(ids_ref, a_ref, b_ref, *rest):
        srcs = rest[0:n_sent]
        o_ref = rest[n_sent]
        dsts = rest[n_sent + 1:2 * n_sent + 1]
        acc = rest[2 * n_sent + 1]
        sems = list(rest[2 * n_sent + 2:])
        j = pl.program_id(1)
        step = pl.program_id(0) * n_t + j
        if chip:
            _host_chip_exchange(step, n_steps, srcs[0:n_chip], dsts[0:n_chip], sems.pop(0), sems.pop(0))
        if halves is not None:
            _host_half_exchange(step, n_steps, srcs[n_chip], dsts[n_chip], sems.pop(0), sems.pop(0))
        if small is not None:
            _host_small_exchange(step, n_steps, *srcs[n_sent - 3:], *dsts[n_sent - 3:], *sems)

        @pl.when(j == 0)
        def _():
            acc[...] = jnp.zeros(acc.shape, F32)

        acc[...] += _dot_tn(a_ref[...], b_ref[...])

        @pl.when(j == n_t - 1)
        def _():
            o_ref[0] = acc[...].astype(BF16)

    landed = [jax.ShapeDtypeStruct(s.shape, BF16) for s in chip]
    scratch = [pltpu.VMEM((width, D_MODEL), F32)]
    if chip:
        scratch += _exchange_scratch(len(chip), 3)
    if halves is not None:
        landed.append(jax.ShapeDtypeStruct((halves.shape[0], halves.shape[1] // 2, halves.shape[2]), BF16))
        scratch += [pltpu.SemaphoreType.DMA((halves.shape[0],)), pltpu.SemaphoreType.DMA((halves.shape[0],))]
    if small is not None:
        vec_m, vec_b, wab = small
        landed += [jax.ShapeDtypeStruct((N_DEV,) + vec_m.shape, F32), jax.ShapeDtypeStruct((N_DEV,) + vec_b.shape, F32),
                   jax.ShapeDtypeStruct((N_DEV, wab.shape[0] // N_DEV, wab.shape[1]), F32)]
        scratch += _exchange_scratch(3, N_DEV) + [pltpu.SemaphoreType.DMA((2,))]
    grid_spec = pltpu.PrefetchScalarGridSpec(
        num_scalar_prefetch=1, grid=(n_q, n_t),
        in_specs=[pl.BlockSpec((tk, width), lambda q, j, ids: (j, ids[q])),
                  pl.BlockSpec((tk, D_MODEL), lambda q, j, ids: (j, 0))] + [HBM_SPEC] * n_sent,
        out_specs=[pl.BlockSpec((1, width, D_MODEL), lambda q, j, ids: (q, 0, 0))] + [HBM_SPEC] * n_sent,
        scratch_shapes=scratch)
    return pl.pallas_call(
        body, grid_spec=grid_spec, out_shape=[jax.ShapeDtypeStruct((n_q, width, D_MODEL), BF16)] + landed,
        compiler_params=_params(("arbitrary", "arbitrary"), 40), name=name,
    )(chip_ids, du, h, *sent)


def _adamw(w, g, m, v):
    m = ADAM_B1 * m + (1.0 - ADAM_B1) * g
    v = ADAM_B2 * v + (1.0 - ADAM_B2) * (g * g)
    delta = -ADAM_LR * ((m / BC1) / (jnp.sqrt(v / BC2) + ADAM_EPS) + ADAM_WD * w)
    return delta, m, v


def _update_sharded(g, landed, w, m, v, rows_blk, name):
    rows, cols = w.shape

    def body(g_ref, l_ref, w_ref, m_ref, v_ref, og, od, om, ov):
        gv = g_ref[...]
        for j in range(3):
            gv = gv + l_ref[j].astype(F32)
        delta, mn, vn = _adamw(w_ref[...], gv, m_ref[...], v_ref[...])
        og[...] = gv
        od[...] = delta
        om[...] = mn
        ov[...] = vn

    blk = pl.BlockSpec((rows_blk, cols), lambda i: (i, 0))
    shape = pltpu.HBM((rows, cols), F32)
    return pl.pallas_call(
        body, grid=(rows // rows_blk,),
        in_specs=[blk, pl.BlockSpec((3, rows_blk, cols), lambda i: (0, i, 0)), blk, blk, blk],
        out_specs=[blk] * 4, out_shape=[shape] * 4,
        compiler_params=_params(("arbitrary",), 32), name=name,
    )(*_in_hbm(g, landed, w, m, v))


def _update_w_in(g_own, sib_own, landed, w_t, m_t, v_t, core, cols_blk):
    rows, cols = w_t.shape

    def body(core_ref, g_ref, s_ref, l_ref, w_ref, m_ref, v_ref, og, od, om, ov):
        gv = g_ref[0, 0].astype(F32) + s_ref[0].astype(F32)
        for j in range(3):
            gv = gv + l_ref[j].astype(F32)
        delta, mn, vn = _adamw(w_ref[...], gv, m_ref[...], v_ref[...])
        og[...] = gv
        od[...] = delta
        om[...] = mn
        ov[...] = vn

    blk = pl.BlockSpec((rows, cols_blk), lambda i, cr: (0, i))
    grid_spec = pltpu.PrefetchScalarGridSpec(
        num_scalar_prefetch=1, grid=(cols // cols_blk,),
        in_specs=[pl.BlockSpec((1, 1, rows, cols_blk), lambda i, cr: (0, cr[0], 0, i)),
                  pl.BlockSpec((1, rows, cols_blk), lambda i, cr: (0, 0, i)),
                  pl.BlockSpec((3, rows, cols_blk), lambda i, cr: (0, 0, i)), blk, blk, blk],
        out_specs=[blk] * 4)
    return pl.pallas_call(
        body, grid_spec=grid_spec, out_shape=[pltpu.HBM((rows, cols), F32)] * 4,
        compiler_params=_params(("arbitrary",), 32), name="update_w_in",
    )(core, *_in_hbm(g_own.reshape(1, 2, rows, cols), sib_own, landed, w_t, m_t, v_t))


def _update_small(vsum, wsum, g_cw, g_rw, weights, moments_m, moments_v):
    n = len(weights)

    def body(*refs):
        vs, ws, gcw, grw = refs[0:4]
        w_refs = refs[4:4 + n]
        m_refs = refs[4 + n:4 + 2 * n]
        v_refs = refs[4 + 2 * n:4 + 3 * n]
        outs = refs[4 + 3 * n:]
        loss_ref = outs[0]
        loss_ref[...] = jnp.sum(vs[ROW_LOSS:ROW_LOSS + 1, :], axis=1, keepdims=True)
        grads = [
            vs[ROW_GMIX:ROW_GMIX + 1, :], gcw[...], grw[...], vs[ROW_BR:ROW_BR + 1, :],
            ws[0:LRU_WIDTH, :], vs[ROW_BA:ROW_BA + 1, :], ws[LRU_WIDTH:2 * LRU_WIDTH, :], vs[ROW_BX:ROW_BX + 1, :],
            vs[ROW_LAM:ROW_LAM + 1, :], vs[ROW_GNC:ROW_GNC + 1, 0:CONV_WIDTH], vs[ROW_GNR:ROW_GNR + 1, :],
            vs[ROW_GMLP:ROW_GMLP + 1, :], vs[ROW_GF:ROW_GF + 1, :],
        ]
        for k in range(n):
            gk = grads[k]
            delta, mn, vn = _adamw(w_refs[k][...], gk, m_refs[k][...], v_refs[k][...])
            outs[1 + 4 * k][...] = gk
            outs[2 + 4 * k][...] = delta
            outs[3 + 4 * k][...] = mn
            outs[4 + 4 * k][...] = vn

    whole = lambda a: pl.BlockSpec(a.shape, lambda i: (0,) * len(a.shape))
    out_shape = [jax.ShapeDtypeStruct((1, 1), F32)]
    for w in weights:
        out_shape += [jax.ShapeDtypeStruct(w.shape, F32)] * 4
    args = (vsum, wsum, g_cw, g_rw, *weights, *moments_m, *moments_v)
    return pl.pallas_call(
        body, grid=(1,), out_shape=out_shape, in_specs=[whole(a) for a in args], out_specs=[whole(s) for s in out_shape],
        compiler_params=_params(("arbitrary",), 32), name="update_small",
    )(*args)


def kernel(x, norm_mix_g, w_in, conv_w, rnn_conv_w, rnn_conv_b, w_a, b_a, w_x, b_x, lru_lambda, g_norm_conv, g_norm_rnn, w_out, norm_mlp_g, w_mlp_in, w_mlp_out, final_norm_g, loss_target, m_norm_mix_g, m_w_in, m_conv_w, m_rnn_conv_w, m_rnn_conv_b, m_w_a, m_b_a, m_w_x, m_b_x, m_lru_lambda, m_g_norm_conv, m_g_norm_rnn, m_w_out, m_norm_mlp_g, m_w_mlp_in, m_w_mlp_out, m_final_norm_g, v_norm_mix_g, v_w_in, v_conv_w, v_rnn_conv_w, v_rnn_conv_b, v_w_a, v_b_a, v_w_x, v_b_x, v_lru_lambda, v_g_norm_conv, v_g_norm_rnn, v_w_out, v_norm_mlp_g, v_w_mlp_in, v_w_mlp_out, v_final_norm_g):
    t_len = x.shape[1]
    my_id = 4 * lax.axis_index("x") + 2 * lax.axis_index("y") + lax.axis_index("c")
    tm = min(256, t_len)
    tb = min(512, t_len)
    tk = min(512, t_len)

    xs = x.reshape(t_len, D_MODEL)
    tgt = loss_target.reshape(t_len, D_MODEL)
    flat = lambda a: a.reshape(a.shape[-2:]) if a.ndim == 3 else a.reshape(1, -1)
    heads = lambda a: a.reshape(LRU_WIDTH, HEAD_DIM)

    turned = lambda a: jnp.transpose(flat(a))
    win_shard, wout_shard, w1_shard, w2_shard, cp_shard = _prep_shards(
        turned(w_in), flat(w_out), flat(w_mlp_in), flat(w_mlp_out), flat(conv_w), flat(rnn_conv_w))

    u, h, win_t, cp_full = _in_proj(xs, flat(norm_mix_g), (win_shard, cp_shard), min(1024, t_len))
    cpack = cp_full.reshape(N_DEV, 8, 128)
    conv_full = jnp.transpose(cpack[:, 0:3, 0:64], (1, 0, 2)).reshape(3, CONV_WIDTH)
    rnn_full = jnp.transpose(cpack[:, 3:7, :], (1, 0, 2)).reshape(4, LRU_WIDTH)
    mixer_small = (conv_full, rnn_full, flat(rnn_conv_b), heads(w_a), flat(b_a), heads(w_x), flat(b_x),
                   flat(lru_lambda), flat(g_norm_conv), flat(g_norm_rnn))
    hs, y, xr, gate_r, gate_i, mult, w1_blk, wout_blk = _mixer_fwd(u, *mixer_small, (w1_shard, wout_shard), tm)
    wout_f = wout_blk.reshape(MIX_WIDTH, D_MODEL)
    x1, h2, z, w2_blk = _mlp_up(xs, y, flat(norm_mlp_g), wout_f, w1_blk, w2_shard, tb)
    dx1, dx2, vec_m, dpre = _mlp_down_bwd(x1, z, tgt, flat(norm_mlp_g), flat(final_norm_g), w1_blk,
                                          w2_blk.reshape(D_FF, D_MODEL), tb)
    (g_w1,) = _tn_weight_grad(h2, dpre, tk, "w_mlp_in_grad", col_blocks=N_DEV)
    (g_w2,) = _tn_weight_grad(z, dx2, tk, "w_mlp_out_grad")
    g_w2 = g_w2.reshape(N_DEV, D_FF // N_DEV, D_MODEL)
    g_wout, sib_w1, sib_w2 = _tn_weight_grad(y, dx1, tk, "w_out_grad", pair=(g_w1, g_w2))
    g_wout = g_wout.reshape(N_DEV, MIX_WIDTH // N_DEV, D_MODEL)
    hsend_w1, own_w1, hsend_w2, own_w2 = _pair_sum((g_w1, g_w2), (sib_w1, sib_w2), "pair_sum_w_mlp")
    du, vec_b, wab, landed_w1, landed_w2, sib_wout = _mixer_bwd(
        u, hs, dx1, (xr, gate_r, gate_i, mult), conv_full, rnn_full, flat(rnn_conv_b), heads(w_a), heads(w_x),
        flat(lru_lambda), flat(g_norm_conv), flat(g_norm_rnn), wout_f, (hsend_w1, hsend_w2), g_wout, tm)
    hsend_wout, own_wout = _pair_sum((g_wout,), (sib_wout,), "pair_sum_w_out")
    ax, ay, ac = lax.axis_index("x"), lax.axis_index("y"), lax.axis_index("c")
    chip_ids = jnp.stack([2 * cx + cy for cx, cy in [(ax, ay)] + _other_chips(ax, ay)]).astype(jnp.int32)
    core = jnp.reshape(ac, (1,)).astype(jnp.int32)
    tw = min(1024, t_len)
    g_others, landed_wout, vrecv_m, vrecv_b, wrecv = _w_in_grad_part(
        du, h, tw, "w_in_grad_others", chip_ids[1:4], chip=(hsend_wout,), small=(vec_m, vec_b, wab))
    g_own, sib_others = _w_in_grad_part(du, h, tw, "w_in_grad_own", chip_ids[0:1], halves=g_others)
    hsend_win = _pair_sum_parts(g_others, sib_others, core)
    grad_x, vec_x, landed_win, sib_own = _in_proj_bwd(du, dx1, xs, flat(norm_mix_g), win_t, tm, hsend_win, g_own)

    vsum, wsum = _final_small(vrecv_m, vrecv_b, wab, wrecv, vec_x)

    up_win = _update_w_in(g_own, sib_own, landed_win, turned(w_in), turned(m_w_in), turned(v_w_in), core, 256)
    up_win = [jnp.transpose(a) for a in up_win]
    up_wout = _update_sharded(own_wout, landed_wout, flat(w_out), flat(m_w_out), flat(v_w_out), 96, "update_w_out")
    up_w1 = _update_sharded(own_w1, landed_w1, flat(w_mlp_in), flat(m_w_mlp_in), flat(v_w_mlp_in), 256,
                            "update_w_mlp_in")
    up_w2 = _update_sharded(own_w2, landed_w2, flat(w_mlp_out), flat(m_w_mlp_out), flat(v_w_mlp_out), 256,
                            "update_w_mlp_out")

    g_cw = lax.dynamic_slice(vsum, (ROW_CW, 64 * my_id), (3, 64))
    g_rw = lax.dynamic_slice(vsum, (ROW_RW, 128 * my_id), (4, 128))
    small_w = (norm_mix_g, conv_w, rnn_conv_w, rnn_conv_b, w_a, b_a, w_x, b_x, lru_lambda, g_norm_conv, g_norm_rnn,
               norm_mlp_g, final_norm_g)
    small_m = (m_norm_mix_g, m_conv_w, m_rnn_conv_w, m_rnn_conv_b, m_w_a, m_b_a, m_w_x, m_b_x, m_lru_lambda,
               m_g_norm_conv, m_g_norm_rnn, m_norm_mlp_g, m_final_norm_g)
    small_v = (v_norm_mix_g, v_conv_w, v_rnn_conv_w, v_rnn_conv_b, v_w_a, v_b_a, v_w_x, v_b_x, v_lru_lambda,
               v_g_norm_conv, v_g_norm_rnn, v_norm_mlp_g, v_final_norm_g)
    is_heads = (False, False, False, False, True, False, True, False, False, False, False, False, False)
    as2d = lambda arrs: [heads(a) if hd else flat(a) for a, hd in zip(arrs, is_heads)]
    small_out = _update_small(vsum, wsum, g_cw, g_rw, as2d(small_w), as2d(small_m), as2d(small_v))
    loss = small_out[0].reshape(())

    names = ["norm_mix_g", "w_in", "conv_w", "rnn_conv_w", "rnn_conv_b", "w_a", "b_a", "w_x", "b_x", "lru_lambda",
             "g_norm_conv", "g_norm_rnn", "w_out", "norm_mlp_g", "w_mlp_in", "w_mlp_out", "final_norm_g"]
    originals = dict(zip(names, (norm_mix_g, w_in, conv_w, rnn_conv_w, rnn_conv_b, w_a, b_a, w_x, b_x, lru_lambda,
                                 g_norm_conv, g_norm_rnn, w_out, norm_mlp_g, w_mlp_in, w_mlp_out, final_norm_g)))
    results = {"w_in": up_win, "w_out": up_wout, "w_mlp_in": up_w1, "w_mlp_out": up_w2}
    small_names = ["norm_mix_g", "conv_w", "rnn_conv_w", "rnn_conv_b", "w_a", "b_a", "w_x", "b_x", "lru_lambda",
                   "g_norm_conv", "g_norm_rnn", "norm_mlp_g", "final_norm_g"]
    for k, nm in enumerate(small_names):
        results[nm] = small_out[1 + 4 * k:5 + 4 * k]
    out = [loss, grad_x.reshape(x.shape)]
    for kind in range(4):
        out += [results[nm][kind].reshape(originals[nm].shape) for nm in names]
    return tuple(out)
```

```python
import functools

import jax
import jax.numpy as jnp
from jax import lax
from jax.experimental import pallas as pl
from jax.experimental.pallas import tpu as pltpu

F32 = jnp.float32
BF16 = jnp.bfloat16

D_MODEL = 1024
HEAD_DIM = 64
CONV_WIDTH = 512
LRU_WIDTH = 1024
MIX_WIDTH = CONV_WIDTH + LRU_WIDTH
IN_COLS = 3 * CONV_WIDTH + 2 * LRU_WIDTH
D_FF = 4 * D_MODEL
GROUP = 256
EPS = 1e-6
LRU_C = 8.0
N_DEV = 8
SUB = 8

OFF_GB, OFF_GC, OFF_V, OFF_XR, OFF_G = 0, 512, 1024, 1536, 2560

ADAM_LR, ADAM_B1, ADAM_B2, ADAM_EPS, ADAM_WD, ADAM_STEP = 0.001, 0.9, 0.999, 1e-08, 0.01, 10
BC1 = 1.0 - ADAM_B1 ** ADAM_STEP
BC2 = 1.0 - ADAM_B2 ** ADAM_STEP

MIB = 1024 * 1024
MESH = pl.DeviceIdType.MESH

VEC_ROWS = 32
ROW_GF, ROW_GMLP, ROW_LOSS = 0, 1, 2
ROW_GNC, ROW_GNR, ROW_BR, ROW_BA, ROW_BX, ROW_LAM, ROW_CW, ROW_RW = 8, 9, 10, 11, 12, 13, 14, 17
ROW_GMIX = 24
ACC_GNC, ACC_GNR, ACC_BR, ACC_BA, ACC_BX, ACC_SP, ACC_CW, ACC_RW, N_ACC = 0, 1, 2, 3, 4, 5, 6, 9, 13


def _params(semantics=None, vmem_mib=48):
    return pltpu.CompilerParams(dimension_semantics=semantics, vmem_limit_bytes=vmem_mib * MIB)


def _rms(x):
    return lax.rsqrt(jnp.mean(x * x, axis=-1, keepdims=True) + EPS)


def _rms_bwd(dy, xhat, r, g):
    dyh = dy * g
    return r * (dyh - xhat * jnp.mean(dyh * xhat, axis=-1, keepdims=True))


def _sigmoid(x):
    return 0.5 + 0.5 * jnp.tanh(0.5 * x)


def _gelu(x):
    c0, c1 = 0.7978845608028654, 0.044715
    x2 = x * x
    t = jnp.tanh(x * (c0 + (c0 * c1) * x2))
    half = 0.5 + 0.5 * t
    ge = x * half
    dge = half + (ge - ge * half) * (2.0 * c0 + (6.0 * c0 * c1) * x2)
    return ge, dge


def _softplus_neg(lam):
    z = -lam
    e = jnp.exp(-jnp.abs(z))
    return jnp.maximum(z, 0.0) + jnp.where(e < 1e-4, e * (1.0 - 0.5 * e), jnp.log(1.0 + e))


def _lru_gates(pa, px, sp_c):
    ra = _sigmoid(pa)
    ii = _sigmoid(px)
    la = -ra * sp_c
    a = jnp.exp(la)
    x2 = 2.0 * la
    series = -x2 * (1.0 + x2 * (0.5 + x2 * (1.0 / 6.0 + x2 * (1.0 / 24.0))))
    m2 = jnp.where(x2 > -0.01, series, 1.0 - a * a)
    mult = jnp.where(m2 > 0.0, m2 * lax.rsqrt(m2), 0.0)
    return ra, ii, a, mult


def _down(cur, prev, s, row):
    return pltpu.roll(jnp.where(row < SUB - s, cur, prev), s, 0)


def _up(cur, nxt, s, row):
    return pltpu.roll(jnp.where(row >= s, cur, nxt), SUB - s, 0)


def _scan8_fwd(a, b, row):
    for s in (1, 2, 4):
        m = row >= s
        a_sh = pltpu.roll(a, s, 0)
        b_sh = pltpu.roll(b, s, 0)
        b = jnp.where(m, a * b_sh + b, b)
        a = jnp.where(m, a * a_sh, a)
    return a, b


def _scan8_rev(a, b, row):
    for s in (1, 2, 4):
        m = row < SUB - s
        a_sh = pltpu.roll(a, SUB - s, 0)
        b_sh = pltpu.roll(b, SUB - s, 0)
        b = jnp.where(m, a * b_sh + b, b)
        a = jnp.where(m, a * a_sh, a)
    return a, b


def _group_mask(shape):
    r = lax.broadcasted_iota(jnp.int32, shape, 0)
    c = lax.broadcasted_iota(jnp.int32, shape, 1)
    return ((r % GROUP) // HEAD_DIM) == (c // HEAD_DIM)


def _expand_heads(w):
    j = lax.broadcasted_iota(jnp.int32, (HEAD_DIM, GROUP), 0)
    c = lax.broadcasted_iota(jnp.int32, (HEAD_DIM, GROUP), 1)
    spread = (c % HEAD_DIM == j).astype(BF16)
    e = jnp.dot(w.astype(BF16), spread, preferred_element_type=F32)
    return jnp.where(_group_mask(e.shape), e, 0.0).astype(BF16)


def _fold_heads(p):
    p = jnp.where(_group_mask(p.shape), p, 0.0)
    c = lax.broadcasted_iota(jnp.int32, (GROUP, HEAD_DIM), 0)
    j = lax.broadcasted_iota(jnp.int32, (GROUP, HEAD_DIM), 1)
    fold = (c % HEAD_DIM == j).astype(BF16)
    hi = p.astype(BF16)
    rest = p - hi.astype(F32)
    mid = rest.astype(BF16)
    lo = (rest - mid.astype(F32)).astype(BF16)
    dot = functools.partial(jnp.dot, preferred_element_type=F32)
    return dot(hi, fold) + dot(mid, fold) + dot(lo, fold)


def _block_diag_apply(xb, wbd_ref):
    parts = [jnp.dot(xb[:, g * GROUP:(g + 1) * GROUP], wbd_ref[g * GROUP:(g + 1) * GROUP, :],
                     preferred_element_type=F32) for g in range(LRU_WIDTH // GROUP)]
    return jnp.concatenate(parts, axis=1)


def _block_diag_apply_t(db, wbd_ref):
    parts = [lax.dot_general(db[:, g * GROUP:(g + 1) * GROUP], wbd_ref[g * GROUP:(g + 1) * GROUP, :],
                             (((1,), (1,)), ((), ())), preferred_element_type=F32)
             for g in range(LRU_WIDTH // GROUP)]
    return jnp.concatenate(parts, axis=1)


def _dot_nt(a, b):
    return lax.dot_general(a, b, (((1,), (1,)), ((), ())), preferred_element_type=F32)


def _dot_tn(a, b):
    return lax.dot_general(a, b, (((0,), (0,)), ((), ())), preferred_element_type=F32)


CHUNKS_IN_FLIGHT = 8


def _chunk_loop(n_chunks, chunk, init):
    def body(k, carry):
        for j in range(CHUNKS_IN_FLIGHT):
            carry = chunk(k * CHUNKS_IN_FLIGHT + j, carry)
        return carry

    return lax.fori_loop(0, n_chunks // CHUNKS_IN_FLIGHT, body, init)


def _place():
    x, y, c = lax.axis_index("x"), lax.axis_index("y"), lax.axis_index("c")
    return x, y, c


def _block_id(chip, core):
    return 4 * chip[0] + 2 * chip[1] + core


def _other_chips(x, y):
    return [(1 - x, y), (x, 1 - y), (1 - x, 1 - y)]


def _remote_copy(src, dst, send_sem, recv_sem, to):
    return pltpu.make_async_remote_copy(src_ref=src, dst_ref=dst, send_sem=send_sem, recv_sem=recv_sem,
                                        device_id=to, device_id_type=MESH)


HBM_SPEC = pl.BlockSpec(memory_space=pl.ANY)


def _in_hbm(*arrays):
    return [pltpu.with_memory_space_constraint(a, pltpu.HBM) for a in arrays]


def _prep_shards(w_in_t, w_out, w_mlp_in, w_mlp_out, conv_w, rnn_conv_w):
    def body(win_ref, wout_ref, w1_ref, w2_ref, cw_ref, rw_ref, o_win, o_wout, o_w1, o_w2, o_cp):
        o_win[...] = win_ref[...].astype(BF16)
        o_wout[...] = wout_ref[...].astype(BF16)
        o_w1[...] = w1_ref[...].astype(BF16)
        o_w2[...] = w2_ref[...].astype(BF16)
        o_cp[...] = jnp.zeros(o_cp.shape, F32)
        o_cp[0:3, 0:64] = cw_ref[...]
        o_cp[3:7, :] = rw_ref[...]

    whole = lambda shape: pl.BlockSpec(shape, lambda i: (0,) * len(shape))
    args = (w_in_t, w_out, w_mlp_in, w_mlp_out, conv_w, rnn_conv_w)
    shapes = [(w_in_t.shape, BF16), (w_out.shape, BF16), (w_mlp_in.shape, BF16), (w_mlp_out.shape, BF16),
              ((8, 128), F32)]
    return pl.pallas_call(
        body, grid=(1,), out_shape=[jax.ShapeDtypeStruct(s, d) for s, d in shapes],
        in_specs=[whole(a.shape) for a in args], out_specs=[whole(s) for s, _ in shapes],
        compiler_params=_params(("arbitrary",), 40), name="prep_shards",
    )(*args)


def _host_all_gather(step, n_steps, shards, fulls, send_sems, recv_sems, local_sems):
    x, y, c = _place()
    me = (x, y, c)
    my_id = _block_id((x, y), c)
    sibling = (x, y, 1 - c)
    chips = _other_chips(x, y)
    n_arr = len(shards)

    def copy(arr, k, block, to, src=None):
        dst = fulls[arr].at[block]
        return _remote_copy(dst if src is None else src, dst, send_sems.at[arr, k], recv_sems.at[arr, k], to)

    def local(arr):
        return pltpu.make_async_copy(shards[arr], fulls[arr].at[my_id], local_sems.at[arr])

    @pl.when(step == 0)
    def _():
        for arr in range(n_arr):
            local(arr).start()
            copy(arr, 0, my_id, sibling, shards[arr]).start()
            for j, chip in enumerate(chips):
                copy(arr, 1 + j, my_id, (*chip, c), shards[arr]).start()

    @pl.when(step == max(n_steps - 2, 0))
    def _():
        for j, chip in enumerate(chips):
            for arr in range(n_arr):
                copy(arr, 1 + j, _block_id(chip, c), me).wait_recv()
                copy(arr, 4 + j, _block_id(chip, c), sibling).start()

    @pl.when(step == n_steps - 1)
    def _():
        for arr in range(n_arr):
            copy(arr, 0, _block_id((x, y), 1 - c), me).wait_recv()
            for j, chip in enumerate(chips):
                copy(arr, 4 + j, _block_id(chip, 1 - c), me).wait_recv()
            for k in range(4):
                copy(arr, k, my_id, me, shards[arr]).wait_send()
            for j, chip in enumerate(chips):
                copy(arr, 4 + j, _block_id(chip, c), me).wait_send()
            local(arr).wait()


def _host_pair_exchange(step, n_steps, gs, sibs, send_sems, recv_sems):
    x, y, c = _place()
    sibling = (x, y, 1 - c)
    chips = [(x, y)] + _other_chips(x, y)

    def d2d(arr, q):
        return _remote_copy(gs[arr].at[_block_id(chips[q], 1 - c)], sibs[arr].at[q],
                            send_sems.at[arr, q], recv_sems.at[arr, q], sibling)

    @pl.when(step == 0)
    def _():
        for arr in range(len(gs)):
            for q in (1, 2, 3, 0):
                d2d(arr, q).start()

    @pl.when(step == n_steps - 1)
    def _():
        for arr in range(len(gs)):
            for q in range(4):
                d2d(arr, q).wait()


def _host_chip_exchange(step, n_steps, hsends, hrecvs, send_sems, recv_sems):
    x, y, c = _place()
    chips = _other_chips(x, y)

    def ici(arr, j):
        return _remote_copy(hsends[arr].at[j], hrecvs[arr].at[j], send_sems.at[arr, j], recv_sems.at[arr, j],
                            (*chips[j], c))

    @pl.when(step == 0)
    def _():
        for arr in range(len(hsends)):
            for j in range(3):
                ici(arr, j).start()

    @pl.when(step == n_steps - 1)
    def _():
        for arr in range(len(hsends)):
            for j in range(3):
                ici(arr, j).wait()


def _host_half_exchange(step, n_steps, parts, sibs, send_sems, recv_sems):
    x, y, c = _place()
    n_q, rows2, _ = parts.shape
    half = rows2 // 2

    def d2d(q):
        src = parts.at[q, pl.ds(pl.multiple_of((1 - c) * half, 16), half), :]
        return _remote_copy(src, sibs.at[q], send_sems.at[q], recv_sems.at[q], (x, y, 1 - c))

    @pl.when(step == 0)
    def _():
        for q in range(n_q):
            d2d(q).start()

    @pl.when(step == n_steps - 1)
    def _():
        for q in range(n_q):
            d2d(q).wait()


def _peer(x, y, c, k):
    return (x ^ ((k >> 2) & 1), y ^ ((k >> 1) & 1), c ^ (k & 1))


def _host_small_exchange(step, n_steps, vec_m, vec_b, wab, vrecv_m, vrecv_b, wrecv, send_sems, recv_sems, local_sems):
    x, y, c = _place()
    my_id = _block_id((x, y), c)
    wrows = wab.shape[0] // N_DEV

    def copies(k):
        to = _peer(x, y, c, k)
        block = wab.at[pl.ds(pl.multiple_of(_block_id(to[0:2], to[2]) * wrows, SUB), wrows), :]
        return [_remote_copy(vec_m, vrecv_m.at[my_id], send_sems.at[0, k], recv_sems.at[0, k], to),
                _remote_copy(vec_b, vrecv_b.at[my_id], send_sems.at[1, k], recv_sems.at[1, k], to),
                _remote_copy(block, wrecv.at[k], send_sems.at[2, k], recv_sems.at[2, k], to)]

    mine = [pltpu.make_async_copy(vec_m, vrecv_m.at[my_id], local_sems.at[0]),
            pltpu.make_async_copy(vec_b, vrecv_b.at[my_id], local_sems.at[1])]

    @pl.when(step == 0)
    def _():
        for cp in mine:
            cp.start()
        for k in range(1, N_DEV):
            for cp in copies(k):
                cp.start()

    @pl.when(step == n_steps - 1)
    def _():
        for k in range(1, N_DEV):
            for cp in copies(k):
                cp.wait()
        for cp in mine:
            cp.wait()


def _pair_sum_parts(parts, sibs, core):
    n_q, rows2, cols = parts.shape
    half = rows2 // 2

    def body(core_ref, g_ref, s_ref, o_ref):
        o_ref[0] = (g_ref[0, 0].astype(F32) + s_ref[0].astype(F32)).astype(BF16)

    block = (1, half, cols)
    grid_spec = pltpu.PrefetchScalarGridSpec(
        num_scalar_prefetch=1, grid=(n_q,),
        in_specs=[pl.BlockSpec((1, 1, half, cols), lambda q, cr: (q, cr[0], 0, 0)),
                  pl.BlockSpec(block, lambda q, cr: (q, 0, 0))],
        out_specs=pl.BlockSpec(block, lambda q, cr: (q, 0, 0)))
    return pl.pallas_call(
        body, grid_spec=grid_spec, out_shape=pltpu.HBM((n_q, half, cols), BF16),
        compiler_params=_params(("arbitrary",), 32), name="pair_sum_w_in",
    )(core, *_in_hbm(parts.reshape(n_q, 2, half, cols), sibs))


def _pair_sum(gs, sibs, name):
    n_arr = len(gs)
    x, y, c = _place()
    slots = jnp.stack([_block_id(chip, c) for chip in [(x, y)] + _other_chips(x, y)]).astype(jnp.int32)

    def body(slots_ref, *refs):
        q = pl.program_id(0)
        for k in range(n_arr):
            g_ref, sib_ref = refs[2 * k:2 * k + 2]
            hs_ref, own_ref = refs[2 * n_arr + 2 * k:2 * n_arr + 2 * k + 2]
            both = g_ref[0].astype(F32) + sib_ref[0].astype(F32)

            @pl.when(q == 0)
            def _(own_ref=own_ref, both=both):
                own_ref[...] = both

            @pl.when(q > 0)
            def _(hs_ref=hs_ref, both=both):
                hs_ref[0] = both.astype(BF16)

    in_specs, out_specs, out_shape, args = [], [], [], []
    for g, sib in zip(gs, sibs):
        _, rows, cols = g.shape
        block = (1, rows, cols)
        in_specs += [pl.BlockSpec(block, lambda q, s: (s[q], 0, 0)), pl.BlockSpec(block, lambda q, s: (q, 0, 0))]
        out_specs += [pl.BlockSpec(block, lambda q, s: (jnp.maximum(q - 1, 0), 0, 0)),
                      pl.BlockSpec((rows, cols), lambda q, s: (0, 0))]
        out_shape += [pltpu.HBM((3, rows, cols), BF16), pltpu.HBM((rows, cols), F32)]
        args += _in_hbm(g, sib)
    grid_spec = pltpu.PrefetchScalarGridSpec(num_scalar_prefetch=1, grid=(4,), in_specs=in_specs, out_specs=out_specs)
    return pl.pallas_call(
        body, grid_spec=grid_spec, out_shape=out_shape,
        compiler_params=_params(("arbitrary",), 40), name=name,
    )(slots, *args)


def _exchange_scratch(n_arr, n_copies):
    return [pltpu.SemaphoreType.DMA((n_arr, n_copies)), pltpu.SemaphoreType.DMA((n_arr, n_copies))]


def _final_small(vrecv_m, vrecv_b, wab, wrecv, vec_x):
    wrows = wab.shape[0] // N_DEV

    def body(vm_ref, vb_ref, w_ref, wr_ref, vx_ref, o_vec, o_w, xrecv, wred, x_send, x_recv, b_send, b_recv):
        x, y, c = _place()
        my_id = _block_id((x, y), c)
        my_rows = pl.ds(pl.multiple_of(my_id * wrows, SUB), wrows)

        def xcopy(k):
            return _remote_copy(vx_ref, xrecv.at[my_id], x_send.at[k], x_recv.at[k], _peer(x, y, c, k))

        def bcopy(k):
            return _remote_copy(wred, o_w.at[my_rows, :], b_send.at[k], b_recv.at[k], _peer(x, y, c, k))

        xrecv[my_id] = vx_ref[...]
        for k in range(1, N_DEV):
            xcopy(k).start()
        red = w_ref[my_rows, :]
        for k in range(1, N_DEV):
            red = red + wr_ref[k]
        wred[...] = red
        o_w[my_rows, :] = red
        for k in range(1, N_DEV):
            bcopy(k).start()
        for k in range(1, N_DEV):
            xcopy(k).wait_recv()
        for rows, ref in ((slice(0, 8), vm_ref), (slice(8, 24), vb_ref), (slice(24, 32), xrecv)):
            tot = ref[0]
            for s in range(1, N_DEV):
                tot = tot + ref[s]
            o_vec[rows, :] = tot
        for k in range(1, N_DEV):
            bcopy(k).wait_recv()
        for k in range(1, N_DEV):
            xcopy(k).wait_send()
            bcopy(k).wait_send()

    vm = pl.BlockSpec(memory_space=pltpu.VMEM)
    dma8 = pltpu.SemaphoreType.DMA((N_DEV,))
    return pl.pallas_call(
        body, out_shape=(jax.ShapeDtypeStruct((VEC_ROWS, D_MODEL), F32), jax.ShapeDtypeStruct(wab.shape, F32)),
        in_specs=[vm] * 5, out_specs=[vm] * 2,
        scratch_shapes=[pltpu.VMEM((N_DEV, SUB, D_MODEL), F32), pltpu.VMEM((wrows, HEAD_DIM), F32),
                        dma8, dma8, dma8, dma8],
        compiler_params=_params(vmem_mib=32), name="final_small",
    )(vrecv_m, vrecv_b, wab, wrecv, vec_x)


def _in_proj(x, g_mix, shards, tm):
    t_len = x.shape[0]
    n_t = t_len // tm
    n_arr = len(shards)
    rows = [s.shape[0] for s in shards]
    width = 2 * rows[0]
    ax, ay = lax.axis_index("x"), lax.axis_index("y")
    order = jnp.stack([2 * cx + cy for cx, cy in [(ax, ay)] + _other_chips(ax, ay)]).astype(jnp.int32)

    def body(order_ref, x_ref, g_ref, *rest):
        shard_refs = rest[0:n_arr]
        u_ref, h_ref = rest[n_arr:n_arr + 2]
        fulls = rest[n_arr + 2:2 * n_arr + 2]
        h_s, wbuf, send_sems, recv_sems, local_sems, load_sem = rest[2 * n_arr + 2:]
        p = pl.program_id(0)
        i = pl.program_id(1)
        x_, y_, c = _place()
        me = (x_, y_, c)
        my_id = _block_id((x_, y_), c)
        sibling = (x_, y_, 1 - c)
        chips = _other_chips(x_, y_)

        def block(arr, blk):
            return fulls[arr].at[pl.ds(pl.multiple_of(blk * rows[arr], rows[arr]), rows[arr]), :]

        def copy(arr, k, blk, to, src=None):
            dst = block(arr, blk)
            return _remote_copy(dst if src is None else src, dst, send_sems.at[arr, k], recv_sems.at[arr, k], to)

        def local(arr):
            return pltpu.make_async_copy(shard_refs[arr], block(arr, my_id), local_sems.at[arr])

        def load_chip(chip, slot):
            start = pl.multiple_of((2 * chip[0] + chip[1]) * width, width)
            return pltpu.make_async_copy(fulls[0].at[pl.ds(start, width), :], wbuf.at[slot], load_sem.at[slot])

        def pass_on(j):
            for arr in range(n_arr):
                copy(arr, 1 + j, _block_id(chips[j], c), me).wait_recv()
                copy(arr, 4 + j, _block_id(chips[j], c), sibling).start()

        def complete(j):
            for arr in range(n_arr):
                copy(arr, 4 + j, _block_id(chips[j], 1 - c), me).wait_recv()

        @pl.when((p == 0) & (i == 0))
        def _():
            for arr in range(n_arr):
                local(arr).start()
                copy(arr, 0, my_id, sibling, shard_refs[arr]).start()
                for j in (0, 1):
                    copy(arr, 1 + j, my_id, (*chips[j], c), shard_refs[arr]).start()
            for arr in range(n_arr):
                local(arr).wait()
                copy(arr, 0, _block_id((x_, y_), 1 - c), me).wait_recv()
            load_chip((x_, y_), 0).start()
            load_chip((x_, y_), 0).wait()

        @pl.when((p == 1) & (i == 0))
        def _():
            pass_on(0)
            for arr in range(n_arr):
                copy(arr, 3, my_id, (*chips[2], c), shard_refs[arr]).start()
            pass_on(1)
            complete(0)
            load_chip(chips[0], 1).start()
            load_chip(chips[0], 1).wait()
            complete(1)
            load_chip(chips[1], 0).start()

        @pl.when((p == 2) & (i == 0))
        def _():
            load_chip(chips[1], 0).wait()

        @pl.when((p == 3) & (i == 0))
        def _():
            pass_on(2)
            complete(2)
            load_chip(chips[2], 1).start()
            load_chip(chips[2], 1).wait()

        @pl.when((p == 3) & (i == n_t - 1))
        def _():
            for arr in range(n_arr):
                for k in range(4):
                    copy(arr, k, my_id, me, shard_refs[arr]).wait_send()
                for j, chip in enumerate(chips):
                    copy(arr, 4 + j, _block_id(chip, c), me).wait_send()

        tile = pl.ds(pl.multiple_of(i * tm, tm), tm)

        @pl.when(p == 0)
        def _():
            xv = x_ref[...]
            h = (xv * _rms(xv) * g_ref[...]).astype(BF16)
            h_ref[...] = h
            h_s[tile, :] = h

        for slot in (0, 1):
            @pl.when(p % 2 == slot)
            def _(slot=slot):
                u_ref[...] = _dot_nt(h_s[tile, :], wbuf[slot])

    first_pass = lambda p, i, o: (jnp.where(p == 0, i, n_t - 1), 0)
    grid_spec = pltpu.PrefetchScalarGridSpec(
        num_scalar_prefetch=1, grid=(4, n_t),
        in_specs=[pl.BlockSpec((tm, D_MODEL), first_pass), pl.BlockSpec((1, D_MODEL), lambda p, i, o: (0, 0))]
        + [HBM_SPEC] * n_arr,
        out_specs=[pl.BlockSpec((tm, width), lambda p, i, o: (i, o[p])), pl.BlockSpec((tm, D_MODEL), first_pass)]
        + [HBM_SPEC] * n_arr,
        scratch_shapes=[pltpu.VMEM((t_len, D_MODEL), BF16), pltpu.VMEM((2, width, D_MODEL), BF16)]
        + _exchange_scratch(n_arr, 7) + [pltpu.SemaphoreType.DMA((n_arr,)), pltpu.SemaphoreType.DMA((2,))])
    return pl.pallas_call(
        body, grid_spec=grid_spec,
        out_shape=[jax.ShapeDtypeStruct((t_len, IN_COLS), F32), jax.ShapeDtypeStruct((t_len, D_MODEL), BF16)]
        + [jax.ShapeDtypeStruct((N_DEV * s.shape[0], s.shape[1]), s.dtype) for s in shards],
        compiler_params=_params(("arbitrary", "arbitrary"), 48), name="in_proj",
    )(order, x, g_mix, *shards)


def _conv3_chunk(u_ref, r, cv_prev, cw, row):
    gb = u_ref[pl.ds(r, SUB), OFF_GB:OFF_GB + CONV_WIDTH]
    gc = u_ref[pl.ds(r, SUB), OFF_GC:OFF_GC + CONV_WIDTH]
    v = u_ref[pl.ds(r, SUB), OFF_V:OFF_V + CONV_WIDTH]
    cv = gc * v
    cv_m1 = _down(cv, cv_prev, 1, row)
    cv_m2 = _down(cv, cv_prev, 2, row)
    cq = cw[2:3, :] * cv + cw[1:2, :] * cv_m1 + cw[0:1, :] * cv_m2
    return gb, gc, v, cv, cv_m1, cv_m2, cq


def _conv4_chunk(u_ref, r, xin_prev, rw, rb, row):
    xin = u_ref[pl.ds(r, SUB), OFF_XR:OFF_XR + LRU_WIDTH]
    m1 = _down(xin, xin_prev, 1, row)
    m2 = _down(xin, xin_prev, 2, row)
    m3 = _down(xin, xin_prev, 3, row)
    xr = rw[3:4, :] * xin + rw[2:3, :] * m1 + rw[1:2, :] * m2 + rw[0:1, :] * m3 + rb
    return xin, m1, m2, m3, xr


def _mixer_fwd(u, conv_w, rnn_conv_w, rnn_conv_b, wa, b_a, wx, b_x, lam, gnc, gnr, shards, tm):
    t_len = u.shape[0]
    n_steps = t_len // tm
    n_chunks = tm // SUB
    n_arr = len(shards)

    def body(u_ref, cw_ref, rw_ref, rb_ref, wa_ref, ba_ref, wx_ref, bx_ref, lam_ref, gnc_ref, gnr_ref, *rest):
        shard_refs = rest[0:n_arr]
        hs_ref, y_ref, xr_s, ra_ref, ii_ref, mult_ref = rest[n_arr:n_arr + 6]
        fulls = rest[n_arr + 6:2 * n_arr + 6]
        (y_s, pa_s, px_s, wabd, wxbd, cv_car, xin_car, h_car,
         send_sems, recv_sems, local_sems) = rest[2 * n_arr + 6:]
        _host_all_gather(pl.program_id(0), n_steps, shard_refs, fulls, send_sems, recv_sems, local_sems)

        @pl.when(pl.program_id(0) == 0)
        def _():
            cv_car[...] = jnp.zeros(cv_car.shape, F32)
            xin_car[...] = jnp.zeros(xin_car.shape, F32)
            h_car[...] = jnp.zeros(h_car.shape, F32)
            wabd[...] = _expand_heads(wa_ref[...])
            wxbd[...] = _expand_heads(wx_ref[...])

        row_c = lax.broadcasted_iota(jnp.int32, (SUB, CONV_WIDTH), 0)
        row_r = lax.broadcasted_iota(jnp.int32, (SUB, LRU_WIDTH), 0)
        cw = cw_ref[...]
        rw = rw_ref[...]
        rb = rb_ref[...]
        g_c = gnc_ref[...]
        g_r = gnr_ref[...]
        sp_c = LRU_C * _softplus_neg(lam_ref[...])

        def convs(i, carry):
            cv_prev, xin_prev = carry
            r = pl.multiple_of(i * SUB, SUB)
            gb, _, _, cv, _, _, cq = _conv3_chunk(u_ref, r, cv_prev, cw, row_c)
            y_c = gb * cq
            y_s[pl.ds(r, SUB), 0:CONV_WIDTH] = y_c * _rms(y_c) * g_c
            xin, _, _, _, xr = _conv4_chunk(u_ref, r, xin_prev, rw, rb, row_r)
            xr_s[pl.ds(r, SUB), :] = xr
            return cv, xin

        cv_last, xin_last = _chunk_loop(n_chunks, convs, (cv_car[...], xin_car[...]))
        cv_car[...] = cv_last
        xin_car[...] = xin_last

        xrb = xr_s[...].astype(BF16)
        pa_s[...] = _block_diag_apply(xrb, wabd) + ba_ref[...]
        px_s[...] = _block_diag_apply(xrb, wxbd) + bx_ref[...]

        def recur(i, h_prev):
            r = pl.multiple_of(i * SUB, SUB)
            xr = xr_s[pl.ds(r, SUB), :]
            ra, ii, a, mult = _lru_gates(pa_s[pl.ds(r, SUB), :], px_s[pl.ds(r, SUB), :], sp_c)
            ra_ref[pl.ds(r, SUB), :] = ra
            ii_ref[pl.ds(r, SUB), :] = ii
            mult_ref[pl.ds(r, SUB), :] = mult
            a_cum, b_cum = _scan8_fwd(a, mult * ii * xr, row_r)
            h = a_cum * h_prev + b_cum
            hs_ref[pl.ds(r, SUB), :] = h
            ge, _ = _gelu(u_ref[pl.ds(r, SUB), OFF_G:OFF_G + LRU_WIDTH])
            y_r = h * ge
            y_s[pl.ds(r, SUB), CONV_WIDTH:MIX_WIDTH] = y_r * _rms(y_r) * g_r
            return h[SUB - 1:SUB, :]

        h_car[...] = _chunk_loop(n_chunks, recur, h_car[...])

        y_ref[...] = y_s[...].astype(BF16)

    row_tile = lambda w: pl.BlockSpec((tm, w), lambda i: (i, 0))
    whole = lambda a: pl.BlockSpec(a.shape, lambda i: (0,) * a.ndim)
    smalls = (conv_w, rnn_conv_w, rnn_conv_b, wa, b_a, wx, b_x, lam, gnc, gnr)
    return pl.pallas_call(
        body, grid=(n_steps,),
        in_specs=[row_tile(IN_COLS)] + [whole(a) for a in smalls] + [HBM_SPEC] * n_arr,
        out_specs=[row_tile(LRU_WIDTH), row_tile(MIX_WIDTH)] + [row_tile(LRU_WIDTH)] * 4 + [HBM_SPEC] * n_arr,
        out_shape=[jax.ShapeDtypeStruct((t_len, LRU_WIDTH), F32), jax.ShapeDtypeStruct((t_len, MIX_WIDTH), BF16)]
        + [jax.ShapeDtypeStruct((t_len, LRU_WIDTH), F32)] * 4
        + [jax.ShapeDtypeStruct((N_DEV,) + s.shape, BF16) for s in shards],
        scratch_shapes=[pltpu.VMEM((tm, MIX_WIDTH), F32),
                        pltpu.VMEM((tm, LRU_WIDTH), F32), pltpu.VMEM((tm, LRU_WIDTH), F32),
                        pltpu.VMEM((LRU_WIDTH, GROUP), BF16), pltpu.VMEM((LRU_WIDTH, GROUP), BF16),
                        pltpu.VMEM((SUB, CONV_WIDTH), F32), pltpu.VMEM((SUB, LRU_WIDTH), F32),
                        pltpu.VMEM((1, LRU_WIDTH), F32)]
        + _exchange_scratch(n_arr, 7) + [pltpu.SemaphoreType.DMA((n_arr,))],
        compiler_params=_params(("arbitrary",), 56), name="mixer_fwd",
    )(u, *smalls, *shards)


def _mlp_up(x, y, g_mlp, w_out, w1, w2_shard, tm):
    t_len = x.shape[0]
    n_steps = t_len // tm
    n_blk, _, blk = w1.shape

    def body(x_ref, y_ref, gm_ref, wout_hbm, w1_hbm, w2_ref, x1_ref, h2_ref, z_ref, w2_full,
             wout_s, w1_s, sem, send_sems, recv_sems, local_sems):
        step = pl.program_id(0)
        _host_all_gather(step, n_steps, [w2_ref], [w2_full], send_sems, recv_sems, local_sems)

        load_wout = pltpu.make_async_copy(wout_hbm, wout_s, sem.at[0])
        load_w1 = pltpu.make_async_copy(w1_hbm, w1_s, sem.at[1])

        @pl.when(step == 0)
        def _():
            load_wout.start()
            load_w1.start()
            load_wout.wait()

        x1v = x_ref[...] + jnp.dot(y_ref[...], wout_s[...], preferred_element_type=F32)
        x1_ref[...] = x1v
        h2 = (x1v * _rms(x1v) * gm_ref[...]).astype(BF16)
        h2_ref[...] = h2

        @pl.when(step == 0)
        def _():
            load_w1.wait()

        for k in range(n_blk):
            rp = jnp.maximum(jnp.dot(h2, w1_s[k], preferred_element_type=F32), 0.0)
            z_ref[:, k * blk:(k + 1) * blk] = (rp * rp).astype(BF16)

    row_tile = lambda w: pl.BlockSpec((tm, w), lambda i: (i, 0))
    return pl.pallas_call(
        body, grid=(n_steps,),
        in_specs=[row_tile(D_MODEL), row_tile(MIX_WIDTH), pl.BlockSpec((1, D_MODEL), lambda i: (0, 0)),
                  HBM_SPEC, HBM_SPEC, HBM_SPEC],
        out_specs=[row_tile(D_MODEL), row_tile(D_MODEL), row_tile(D_FF), HBM_SPEC],
        out_shape=[jax.ShapeDtypeStruct((t_len, D_MODEL), F32), jax.ShapeDtypeStruct((t_len, D_MODEL), BF16),
                   jax.ShapeDtypeStruct((t_len, D_FF), BF16), jax.ShapeDtypeStruct((N_DEV,) + w2_shard.shape, BF16)],
        scratch_shapes=[pltpu.VMEM(w_out.shape, BF16), pltpu.VMEM(w1.shape, BF16), pltpu.SemaphoreType.DMA((2,))]
        + _exchange_scratch(1, 7) + [pltpu.SemaphoreType.DMA((1,))],
        compiler_params=_params(("arbitrary",), 48), name="mlp_up",
    )(x, y, g_mlp, w_out, w1, w2_shard)


def _mlp_down_bwd(x1, z, target, g_mlp, g_f, w1, w2, tm):
    t_len = x1.shape[0]
    n_steps = t_len // tm
    n_blk, _, blk = w1.shape

    def body(x1_ref, z_ref, tg_ref, gm_ref, gf_ref, w1_hbm, w2_hbm, dx1_ref, dx2_ref, vec_ref, dpre_hbm,
             w1_s, w2_s, dp_s, sem, out_sem):
        step = pl.program_id(0)
        rows = pl.ds(pl.multiple_of(step * tm, tm), tm)
        dp_out = pltpu.make_async_copy(dp_s, dpre_hbm.at[rows, :], out_sem.at[0])

        load_w1 = pltpu.make_async_copy(w1_hbm, w1_s, sem.at[0])
        load_w2 = pltpu.make_async_copy(w2_hbm, w2_s, sem.at[1])

        @pl.when(step == 0)
        def _():
            load_w2.start()
            load_w1.start()
            vec_ref[...] = jnp.zeros(vec_ref.shape, F32)
            load_w2.wait()

        x1v = x1_ref[...]
        g_m = gm_ref[...]
        g_o = gf_ref[...]
        r2 = _rms(x1v)
        x1h = x1v * r2
        x2 = x1v + jnp.dot(z_ref[...], w2_s[...], preferred_element_type=F32)
        r3 = _rms(x2)
        x2h = x2 * r3
        err = x2h * g_o - tg_ref[...]
        dout = err * (1.0 / D_MODEL)
        vec_ref[ROW_LOSS:ROW_LOSS + 1, :] += (0.5 / D_MODEL) * jnp.sum(err * err, axis=0, keepdims=True)
        vec_ref[ROW_GF:ROW_GF + 1, :] += jnp.sum(dout * x2h, axis=0, keepdims=True)
        dx2 = _rms_bwd(dout, x2h, r3, g_o)
        dx2b = dx2.astype(BF16)
        dx2_ref[...] = dx2b
        dh2 = jnp.zeros((tm, D_MODEL), F32)

        @pl.when(step > 0)
        def _():
            dp_out.wait()

        @pl.when(step == 0)
        def _():
            load_w1.wait()

        for k in range(n_blk):
            cols = slice(k * blk, (k + 1) * blk)
            dz = _dot_nt(dx2b, w2_s[cols, :])
            dpb = (dz * 2.0 * jnp.sqrt(z_ref[:, cols].astype(F32))).astype(BF16)
            dp_s[:, cols] = dpb
            dh2 = dh2 + _dot_nt(dpb, w1_s[k])
        dp_out.start()
        vec_ref[ROW_GMLP:ROW_GMLP + 1, :] += jnp.sum(dh2 * x1h, axis=0, keepdims=True)
        dx1_ref[...] = dx2 + _rms_bwd(dh2, x1h, r2, g_m)

        @pl.when(step == n_steps - 1)
        def _():
            dp_out.wait()

    row_tile = lambda w: pl.BlockSpec((tm, w), lambda i: (i, 0))
    vec_spec = pl.BlockSpec((1, D_MODEL), lambda i: (0, 0))
    return pl.pallas_call(
        body, grid=(n_steps,),
        in_specs=[row_tile(D_MODEL), row_tile(D_FF), row_tile(D_MODEL), vec_spec, vec_spec, HBM_SPEC, HBM_SPEC],
        out_specs=[row_tile(D_MODEL), row_tile(D_MODEL), pl.BlockSpec((SUB, D_MODEL), lambda i: (0, 0)), HBM_SPEC],
        out_shape=[jax.ShapeDtypeStruct((t_len, D_MODEL), F32), jax.ShapeDtypeStruct((t_len, D_MODEL), BF16),
                   jax.ShapeDtypeStruct((SUB, D_MODEL), F32), jax.ShapeDtypeStruct((t_len, D_FF), BF16)],
        scratch_shapes=[pltpu.VMEM(w1.shape, BF16), pltpu.VMEM(w2.shape, BF16), pltpu.VMEM((tm, D_FF), BF16),
                        pltpu.SemaphoreType.DMA((2,)), pltpu.SemaphoreType.DMA((1,))],
        compiler_params=_params(("arbitrary",), 56), name="mlp_down_bwd",
    )(x1, z, target, g_mlp, g_f, w1, w2)


def _mixer_bwd(u, hs, dx1, saved, conv_w, rnn_conv_w, rnn_conv_b, wa, wx, lam, gnc, gnr, w_out,
               chip_sums, g_wout, tm):
    t_len = u.shape[0]
    n_tiles = t_len // tm
    n_chunks = tm // SUB
    per_tile = tm // SUB
    n_sums = len(chip_sums)

    def body(u_ref, up_ref, hs_ref, hp_ref, dx1_ref, xr_ref, ra_ref, ii_ref, mult_ref,
             cw_ref, rw_ref, rb_ref, wa_ref, wx_ref, lam_ref, gnc_ref, gnr_ref, wout_ref, *rest):
        hsends = rest[0:n_sums]
        gwout_ref = rest[n_sums]
        du_ref, vec_ref, wab_ref = rest[n_sums + 1:n_sums + 4]
        hrecvs = rest[n_sums + 4:2 * n_sums + 4]
        sib_wout = rest[2 * n_sums + 4]
        (du_s, dy_s, dpa_s, dpx_s, dxr_s, wabd, wxbd, acc, dwa_acc, dwx_acc,
         a_car, dh_car, dcq_car, dxr_car, i_send, i_recv, d_send, d_recv) = rest[2 * n_sums + 5:]
        step = pl.program_id(0)
        _host_chip_exchange(step, n_tiles, hsends, hrecvs, i_send, i_recv)
        _host_pair_exchange(step, n_tiles, [gwout_ref], [sib_wout], d_send, d_recv)
        has_prev = (step < n_tiles - 1).astype(F32)

        @pl.when(step == 0)
        def _():
            acc[...] = jnp.zeros(acc.shape, F32)
            dwa_acc[...] = jnp.zeros(dwa_acc.shape, F32)
            dwx_acc[...] = jnp.zeros(dwx_acc.shape, F32)
            a_car[...] = jnp.ones(a_car.shape, F32)
            dh_car[...] = jnp.zeros(dh_car.shape, F32)
            dcq_car[...] = jnp.zeros(dcq_car.shape, F32)
            dxr_car[...] = jnp.zeros(dxr_car.shape, F32)
            wabd[...] = _expand_heads(wa_ref[...])
            wxbd[...] = _expand_heads(wx_ref[...])

        row_c = lax.broadcasted_iota(jnp.int32, (SUB, CONV_WIDTH), 0)
        row_r = lax.broadcasted_iota(jnp.int32, (SUB, LRU_WIDTH), 0)
        cw = cw_ref[...]
        rw = rw_ref[...]
        rb = rb_ref[...]
        g_c = gnc_ref[...]
        g_r = gnr_ref[...]
        sp_c = LRU_C * _softplus_neg(lam_ref[...])

        up = up_ref[...] * has_prev
        cv_before = up[:, OFF_GC:OFF_GC + CONV_WIDTH] * up[:, OFF_V:OFF_V + CONV_WIDTH]
        xin_before = up[:, OFF_XR:OFF_XR + LRU_WIDTH]
        hs_before = hp_ref[...] * has_prev

        dy_s[...] = _dot_nt(dx1_ref[...].astype(BF16), wout_ref[...])

        xrb = xr_ref[...].astype(BF16)

        def recur_bwd(j, carry):
            a_later, dh_later = carry
            i = n_chunks - 1 - j
            r = pl.multiple_of(i * SUB, SUB)
            rp = pl.multiple_of(jnp.maximum(i - 1, 0) * SUB, SUB)
            xr = xr_ref[pl.ds(r, SUB), :]
            hs_c = hs_ref[pl.ds(r, SUB), :]
            hs_prev = jnp.where(i == 0, hs_before, hs_ref[pl.ds(rp, SUB), :])
            h_m1 = _down(hs_c, hs_prev, 1, row_r)
            ra = ra_ref[pl.ds(r, SUB), :]
            ii = ii_ref[pl.ds(r, SUB), :]
            mult = mult_ref[pl.ds(r, SUB), :]
            a = jnp.exp(-ra * sp_c)
            inv_mult = lax.rsqrt(mult * mult)
            ge, dge = _gelu(u_ref[pl.ds(r, SUB), OFF_G:OFF_G + LRU_WIDTH])
            y_r = hs_c * ge
            rr = _rms(y_r)
            yhat = y_r * rr
            dyn = dy_s[pl.ds(r, SUB), CONV_WIDTH:MIX_WIDTH]
            acc[ACC_GNR] += dyn * yhat
            dy_r = _rms_bwd(dyn, yhat, rr, g_r)
            du_s[pl.ds(r, SUB), OFF_G:OFF_G + LRU_WIDTH] = dy_r * hs_c * dge
            a_cum, d_cum = _scan8_rev(_up(a, a_later, 1, row_r), dy_r * ge, row_r)
            dh = a_cum * dh_later + d_cum
            dm = dh * mult
            dii = dm * xr
            dxr_s[pl.ds(r, SUB), :] = dm * ii
            dla = a * dh * (h_m1 - (ii * xr) * a * inv_mult)
            dla_r = dla * ra
            acc[ACC_SP] -= dla_r
            dpa = dla_r * (sp_c * (ra - 1.0))
            dpx = dii * ii * (1.0 - ii)
            acc[ACC_BA] += dpa
            acc[ACC_BX] += dpx
            dpa_s[pl.ds(r, SUB), :] = dpa
            dpx_s[pl.ds(r, SUB), :] = dpx
            return a, dh[0:1, :]

        a_first, dh_first = _chunk_loop(n_chunks, recur_bwd, (a_car[...], dh_car[...]))
        a_car[...] = a_first
        dh_car[...] = dh_first

        dpab = dpa_s[...].astype(BF16)
        dpxb = dpx_s[...].astype(BF16)
        dxr_s[...] += _block_diag_apply_t(dpab, wabd) + _block_diag_apply_t(dpxb, wxbd)
        for g in range(LRU_WIDTH // GROUP):
            cols = slice(g * GROUP, (g + 1) * GROUP)
            dwa_acc[cols, :] += _dot_tn(xrb[:, cols], dpab[:, cols])
            dwx_acc[cols, :] += _dot_tn(xrb[:, cols], dpxb[:, cols])

        def convs_bwd(j, carry):
            dcq_later, dxr_later = carry
            i = n_chunks - 1 - j
            r = pl.multiple_of(i * SUB, SUB)
            rp = pl.multiple_of(jnp.maximum(i - 1, 0) * SUB, SUB)
            cv_prev = jnp.where(i == 0, cv_before,
                                u_ref[pl.ds(rp, SUB), OFF_GC:OFF_GC + CONV_WIDTH]
                                * u_ref[pl.ds(rp, SUB), OFF_V:OFF_V + CONV_WIDTH])
            gb, gc, v, cv, cv_m1, cv_m2, cq = _conv3_chunk(u_ref, r, cv_prev, cw, row_c)
            y_c = gb * cq
            rc = _rms(y_c)
            yhat = y_c * rc
            dyn = dy_s[pl.ds(r, SUB), 0:CONV_WIDTH]
            acc[ACC_GNC, :, 0:CONV_WIDTH] += dyn * yhat
            dy_c = _rms_bwd(dyn, yhat, rc, g_c)
            dcq = dy_c * gb
            dcv = (cw[2:3, :] * dcq + cw[1:2, :] * _up(dcq, dcq_later, 1, row_c)
                   + cw[0:1, :] * _up(dcq, dcq_later, 2, row_c))
            acc[ACC_CW + 2, :, 0:CONV_WIDTH] += dcq * cv
            acc[ACC_CW + 1, :, 0:CONV_WIDTH] += dcq * cv_m1
            acc[ACC_CW + 0, :, 0:CONV_WIDTH] += dcq * cv_m2
            du_s[pl.ds(r, SUB), OFF_GB:OFF_GB + CONV_WIDTH] = dy_c * cq
            du_s[pl.ds(r, SUB), OFF_GC:OFF_GC + CONV_WIDTH] = dcv * v
            du_s[pl.ds(r, SUB), OFF_V:OFF_V + CONV_WIDTH] = dcv * gc

            xin_prev = jnp.where(i == 0, xin_before, u_ref[pl.ds(rp, SUB), OFF_XR:OFF_XR + LRU_WIDTH])
            xin, m1, m2, m3, _ = _conv4_chunk(u_ref, r, xin_prev, rw, rb, row_r)
            dxr = dxr_s[pl.ds(r, SUB), :]
            du_s[pl.ds(r, SUB), OFF_XR:OFF_XR + LRU_WIDTH] = (
                rw[3:4, :] * dxr + rw[2:3, :] * _up(dxr, dxr_later, 1, row_r)
                + rw[1:2, :] * _up(dxr, dxr_later, 2, row_r) + rw[0:1, :] * _up(dxr, dxr_later, 3, row_r))
            acc[ACC_RW + 3] += dxr * xin
            acc[ACC_RW + 2] += dxr * m1
            acc[ACC_RW + 1] += dxr * m2
            acc[ACC_RW + 0] += dxr * m3
            acc[ACC_BR] += dxr
            return dcq, dxr

        dcq_first, dxr_first = _chunk_loop(n_chunks, convs_bwd, (dcq_car[...], dxr_car[...]))
        dcq_car[...] = dcq_first
        dxr_car[...] = dxr_first

        du_ref[...] = du_s[...].astype(BF16)

        @pl.when(step == n_tiles - 1)
        def _():
            vec_ref[...] = jnp.zeros(vec_ref.shape, F32)
            rows = {ACC_GNC: ROW_GNC, ACC_GNR: ROW_GNR, ACC_BR: ROW_BR, ACC_BA: ROW_BA, ACC_BX: ROW_BX}
            for k in range(3):
                rows[ACC_CW + k] = ROW_CW + k
            for k in range(4):
                rows[ACC_RW + k] = ROW_RW + k
            for slot, out_row in rows.items():
                o = out_row - ROW_GNC
                vec_ref[o:o + 1, :] = jnp.sum(acc[slot], axis=0, keepdims=True)
            lam_v = lam_ref[...]
            dsp = jnp.sum(acc[ACC_SP], axis=0, keepdims=True)
            o = ROW_LAM - ROW_GNC
            vec_ref[o:o + 1, :] = -dsp * LRU_C / (1.0 + jnp.exp(lam_v))
            wab_ref[0:LRU_WIDTH, :] = _fold_heads(dwa_acc[...])
            wab_ref[LRU_WIDTH:2 * LRU_WIDTH, :] = _fold_heads(dwx_acc[...])

    rev = lambda w: pl.BlockSpec((tm, w), lambda s: (n_tiles - 1 - s, 0))
    before = lambda w: pl.BlockSpec((SUB, w), lambda s: (jnp.maximum((n_tiles - 1 - s) * per_tile - 1, 0), 0))
    whole = lambda a: pl.BlockSpec(a.shape, lambda s: (0,) * a.ndim)
    smalls = (conv_w, rnn_conv_w, rnn_conv_b, wa, wx, lam, gnc, gnr, w_out)
    full = lambda w: pltpu.VMEM((tm, w), F32)
    return pl.pallas_call(
        body, grid=(n_tiles,),
        in_specs=[rev(IN_COLS), before(IN_COLS), rev(LRU_WIDTH), before(LRU_WIDTH), rev(D_MODEL)]
        + [rev(LRU_WIDTH)] * len(saved) + [whole(a) for a in smalls] + [HBM_SPEC] * (n_sums + 1),
        out_specs=[rev(IN_COLS), pl.BlockSpec((16, D_MODEL), lambda s: (0, 0)),
                   pl.BlockSpec((2 * LRU_WIDTH, HEAD_DIM), lambda s: (0, 0))] + [HBM_SPEC] * (n_sums + 1),
        out_shape=[jax.ShapeDtypeStruct((t_len, IN_COLS), BF16), jax.ShapeDtypeStruct((16, D_MODEL), F32),
                   jax.ShapeDtypeStruct((2 * LRU_WIDTH, HEAD_DIM), F32)]
        + [jax.ShapeDtypeStruct(s.shape, BF16) for s in chip_sums]
        + [jax.ShapeDtypeStruct((4,) + g_wout.shape[1:], BF16)],
        scratch_shapes=[full(IN_COLS), full(MIX_WIDTH), full(LRU_WIDTH), full(LRU_WIDTH), full(LRU_WIDTH),
                        pltpu.VMEM((LRU_WIDTH, GROUP), BF16), pltpu.VMEM((LRU_WIDTH, GROUP), BF16),
                        pltpu.VMEM((N_ACC, SUB, LRU_WIDTH), F32),
                        pltpu.VMEM((LRU_WIDTH, GROUP), F32), pltpu.VMEM((LRU_WIDTH, GROUP), F32),
                        pltpu.VMEM((SUB, LRU_WIDTH), F32), pltpu.VMEM((1, LRU_WIDTH), F32),
                        pltpu.VMEM((SUB, CONV_WIDTH), F32), pltpu.VMEM((SUB, LRU_WIDTH), F32)]
        + _exchange_scratch(n_sums, 3) + _exchange_scratch(1, 4),
        compiler_params=_params(("arbitrary",), 56), name="mixer_bwd",
    )(u, u, hs, hs, dx1, *saved, *smalls, *chip_sums, g_wout)


def _in_proj_bwd(du, dx1, x, g_mix, win_t, tm, chip_sums, g_own):
    t_len = x.shape[0]
    n_steps = t_len // tm

    def body(du_ref, dx1_ref, x_ref, g_ref, w_ref, hs_ref, gown_ref,
             dx_ref, vec_ref, landed_ref, sib_ref, i_send, i_recv, d_send, d_recv):
        step = pl.program_id(0)
        _host_chip_exchange(step, n_steps, [hs_ref], [landed_ref], i_send, i_recv)
        _host_half_exchange(step, n_steps, gown_ref, sib_ref, d_send, d_recv)

        @pl.when(step == 0)
        def _():
            vec_ref[...] = jnp.zeros(vec_ref.shape, F32)

        dh = jnp.dot(du_ref[...], w_ref[...], preferred_element_type=F32)
        xv = x_ref[...]
        r1 = _rms(xv)
        xh = xv * r1
        vec_ref[0:1, :] += jnp.sum(dh * xh, axis=0, keepdims=True)
        dx_ref[...] = dx1_ref[...] + _rms_bwd(dh, xh, r1, g_ref[...])

    row_tile = lambda w: pl.BlockSpec((tm, w), lambda i: (i, 0))
    half_shape = (g_own.shape[0], g_own.shape[1] // 2, g_own.shape[2])
    return pl.pallas_call(
        body, grid=(n_steps,),
        in_specs=[row_tile(IN_COLS), row_tile(D_MODEL), row_tile(D_MODEL), pl.BlockSpec((1, D_MODEL), lambda i: (0, 0)),
                  pl.BlockSpec((IN_COLS, D_MODEL), lambda i: (0, 0))] + [HBM_SPEC] * 2,
        out_specs=[row_tile(D_MODEL), pl.BlockSpec((SUB, D_MODEL), lambda i: (0, 0))] + [HBM_SPEC] * 2,
        out_shape=[jax.ShapeDtypeStruct((t_len, D_MODEL), F32), jax.ShapeDtypeStruct((SUB, D_MODEL), F32),
                   jax.ShapeDtypeStruct(chip_sums.shape, BF16), jax.ShapeDtypeStruct(half_shape, BF16)],
        scratch_shapes=_exchange_scratch(1, 3) + [pltpu.SemaphoreType.DMA((1,)), pltpu.SemaphoreType.DMA((1,))],
        compiler_params=_params(("arbitrary",), 56), name="in_proj_bwd",
    )(du, dx1, x, g_mix, win_t, chip_sums, g_own)


def _tn_weight_grad(a, b, tk, name, pair=(), col_blocks=1):
    t_len, m = a.shape
    n = b.shape[1]
    n_steps = t_len // tk
    sent = tuple(pair)
    n_sent = len(sent)

    def body(a_ref, b_ref, *rest):
        srcs = rest[0:n_sent]
        o_ref = rest[n_sent]
        dsts = rest[n_sent + 1:2 * n_sent + 1]
        acc = rest[2 * n_sent + 1]
        sems = rest[2 * n_sent + 2:]
        j = pl.program_id(0)
        if pair:
            _host_pair_exchange(j, n_steps, srcs, dsts, *sems)

        @pl.when(j == 0)
        def _():
            acc[...] = jnp.zeros(acc.shape, F32)

        acc[...] += _dot_tn(a_ref[...].astype(BF16), b_ref[...].astype(BF16))

        @pl.when(j == n_steps - 1)
        def _():
            if col_blocks == 1:
                o_ref[...] = acc[...].astype(BF16)
            else:
                for k in range(col_blocks):
                    o_ref[k] = acc[:, k * nb:(k + 1) * nb].astype(BF16)

    nb = n // col_blocks
    out_dims = (m, n) if col_blocks == 1 else (col_blocks, m, nb)
    landed = [jax.ShapeDtypeStruct((4,) + g.shape[1:], BF16) for g in pair]
    scratch = [pltpu.VMEM((m, n), F32)]
    if n_sent:
        scratch += _exchange_scratch(n_sent, 4)
    return pl.pallas_call(
        body, grid=(n_steps,),
        in_specs=[pl.BlockSpec((tk, m), lambda j: (j, 0)), pl.BlockSpec((tk, n), lambda j: (j, 0))]
        + [HBM_SPEC] * n_sent,
        out_specs=[pl.BlockSpec(out_dims, lambda j: (0,) * len(out_dims))] + [HBM_SPEC] * n_sent,
        out_shape=[jax.ShapeDtypeStruct(out_dims, BF16)] + landed,
        scratch_shapes=scratch,
        compiler_params=_params(("arbitrary",), 56), name=name,
    )(a, b, *sent)


def _w_in_grad_part(du, h, tk, name, chip_ids, chip=(), halves=None, small=None):
    t_len = du.shape[0]
    n_t = t_len // tk
    n_q = chip_ids.shape[0]
    width = 2 * (IN_COLS // N_DEV)
    n_steps = n_q * n_t
    n_chip = len(chip)
    sent = tuple(chip) + (() if halves is None else (halves,)) + (() if small is None else tuple(small))
    n_sent = len(sent)

    def body(ids_ref, a_ref, b_ref, *rest):
        srcs = rest[0:n_sent]
        o_ref = rest[n_sent]
        dsts = rest[n_sent + 1:2 * n_sent + 1]
        acc = rest[2 * n_sent + 1]
        sems = list(rest[2 * n_sent + 2:])
        j = pl.program_id(1)
        step = pl.program_id(0) * n_t + j
        if chip:
            _host_chip_exchange(step, n_steps, srcs[0:n_chip], dsts[0:n_chip], sems.pop(0), sems.pop(0))
        if halves is not None:
            _host_half_exchange(step, n_steps, srcs[n_chip], dsts[n_chip], sems.pop(0), sems.pop(0))
        if small is not None:
            _host_small_exchange(step, n_steps, *srcs[n_sent - 3:], *dsts[n_sent - 3:], *sems)

        @pl.when(j == 0)
        def _():
            acc[...] = jnp.zeros(acc.shape, F32)

        acc[...] += _dot_tn(a_ref[...], b_ref[...])

        @pl.when(j == n_t - 1)
        def _():
            o_ref[0] = acc[...].astype(BF16)

    landed = [jax.ShapeDtypeStruct(s.shape, BF16) for s in chip]
    scratch = [pltpu.VMEM((width, D_MODEL), F32)]
    if chip:
        scratch += _exchange_scratch(len(chip), 3)
    if halves is not None:
        landed.append(jax.ShapeDtypeStruct((halves.shape[0], halves.shape[1] // 2, halves.shape[2]), BF16))
        scratch += [pltpu.SemaphoreType.DMA((halves.shape[0],)), pltpu.SemaphoreType.DMA((halves.shape[0],))]
    if small is not None:
        vec_m, vec_b, wab = small
        landed += [jax.ShapeDtypeStruct((N_DEV,) + vec_m.shape, F32), jax.ShapeDtypeStruct((N_DEV,) + vec_b.shape, F32),
                   jax.ShapeDtypeStruct((N_DEV, wab.shape[0] // N_DEV, wab.shape[1]), F32)]
        scratch += _exchange_scratch(3, N_DEV) + [pltpu.SemaphoreType.DMA((2,))]
    grid_spec = pltpu.PrefetchScalarGridSpec(
        num_scalar_prefetch=1, grid=(n_q, n_t),
        in_specs=[pl.BlockSpec((tk, width), lambda q, j, ids: (j, ids[q])),
                  pl.BlockSpec((tk, D_MODEL), lambda q, j, ids: (j, 0))] + [HBM_SPEC] * n_sent,
        out_specs=[pl.BlockSpec((1, width, D_MODEL), lambda q, j, ids: (q, 0, 0))] + [HBM_SPEC] * n_sent,
        scratch_shapes=scratch)
    return pl.pallas_call(
        body, grid_spec=grid_spec, out_shape=[jax.ShapeDtypeStruct((n_q, width, D_MODEL), BF16)] + landed,
        compiler_params=_params(("arbitrary", "arbitrary"), 40), name=name,
    )(chip_ids, du, h, *sent)


def _adamw(w, g, m, v):
    m = ADAM_B1 * m + (1.0 - ADAM_B1) * g
    v = ADAM_B2 * v + (1.0 - ADAM_B2) * (g * g)
    delta = -ADAM_LR * ((m / BC1) / (jnp.sqrt(v / BC2) + ADAM_EPS) + ADAM_WD * w)
    return delta, m, v


def _update_sharded(g, landed, w, m, v, rows_blk, name):
    rows, cols = w.shape

    def body(g_ref, l_ref, w_ref, m_ref, v_ref, og, od, om, ov):
        gv = g_ref[...]
        for j in range(3):
            gv = gv + l_ref[j].astype(F32)
        delta, mn, vn = _adamw(w_ref[...], gv, m_ref[...], v_ref[...])
        og[...] = gv
        od[...] = delta
        om[...] = mn
        ov[...] = vn

    blk = pl.BlockSpec((rows_blk, cols), lambda i: (i, 0))
    shape = pltpu.HBM((rows, cols), F32)
    return pl.pallas_call(
        body, grid=(rows // rows_blk,),
        in_specs=[blk, pl.BlockSpec((3, rows_blk, cols), lambda i: (0, i, 0)), blk, blk, blk],
        out_specs=[blk] * 4, out_shape=[shape] * 4,
        compiler_params=_params(("arbitrary",), 32), name=name,
    )(*_in_hbm(g, landed, w, m, v))


def _update_w_in(g_own, sib_own, landed, w_t, m_t, v_t, core, cols_blk):
    rows, cols = w_t.shape

    def body(core_ref, g_ref, s_ref, l_ref, w_ref, m_ref, v_ref, og, od, om, ov):
        gv = g_ref[0, 0].astype(F32) + s_ref[0].astype(F32)
        for j in range(3):
            gv = gv + l_ref[j].astype(F32)
        delta, mn, vn = _adamw(w_ref[...], gv, m_ref[...], v_ref[...])
        og[...] = gv
        od[...] = delta
        om[...] = mn
        ov[...] = vn

    blk = pl.BlockSpec((rows, cols_blk), lambda i, cr: (0, i))
    grid_spec = pltpu.PrefetchScalarGridSpec(
        num_scalar_prefetch=1, grid=(cols // cols_blk,),
        in_specs=[pl.BlockSpec((1, 1, rows, cols_blk), lambda i, cr: (0, cr[0], 0, i)),
                  pl.BlockSpec((1, rows, cols_blk), lambda i, cr: (0, 0, i)),
                  pl.BlockSpec((3, rows, cols_blk), lambda i, cr: (0, 0, i)), blk, blk, blk],
        out_specs=[blk] * 4)
    return pl.pallas_call(
        body, grid_spec=grid_spec, out_shape=[pltpu.HBM((rows, cols), F32)] * 4,
        compiler_params=_params(("arbitrary",), 32), name="update_w_in",
    )(core, *_in_hbm(g_own.reshape(1, 2, rows, cols), sib_own, landed, w_t, m_t, v_t))


def _update_small(vsum, wsum, g_cw, g_rw, weights, moments_m, moments_v):
    n = len(weights)

    def body(*refs):
        vs, ws, gcw, grw = refs[0:4]
        w_refs = refs[4:4 + n]
        m_refs = refs[4 + n:4 + 2 * n]
        v_refs = refs[4 + 2 * n:4 + 3 * n]
        outs = refs[4 + 3 * n:]
        loss_ref = outs[0]
        loss_ref[...] = jnp.sum(vs[ROW_LOSS:ROW_LOSS + 1, :], axis=1, keepdims=True)
        grads = [
            vs[ROW_GMIX:ROW_GMIX + 1, :], gcw[...], grw[...], vs[ROW_BR:ROW_BR + 1, :],
            ws[0:LRU_WIDTH, :], vs[ROW_BA:ROW_BA + 1, :], ws[LRU_WIDTH:2 * LRU_WIDTH, :], vs[ROW_BX:ROW_BX + 1, :],
            vs[ROW_LAM:ROW_LAM + 1, :], vs[ROW_GNC:ROW_GNC + 1, 0:CONV_WIDTH], vs[ROW_GNR:ROW_GNR + 1, :],
            vs[ROW_GMLP:ROW_GMLP + 1, :], vs[ROW_GF:ROW_GF + 1, :],
        ]
        for k in range(n):
            gk = grads[k]
            delta, mn, vn = _adamw(w_refs[k][...], gk, m_refs[k][...], v_refs[k][...])
            outs[1 + 4 * k][...] = gk
            outs[2 + 4 * k][...] = delta
            outs[3 + 4 * k][...] = mn
            outs[4 + 4 * k][...] = vn

    whole = lambda a: pl.BlockSpec(a.shape, lambda i: (0,) * len(a.shape))
    out_shape = [jax.ShapeDtypeStruct((1, 1), F32)]
    for w in weights:
        out_shape += [jax.ShapeDtypeStruct(w.shape, F32)] * 4
    args = (vsum, wsum, g_cw, g_rw, *weights, *moments_m, *moments_v)
    return pl.pallas_call(
        body, grid=(1,), out_shape=out_shape, in_specs=[whole(a) for a in args], out_specs=[whole(s) for s in out_shape],
        compiler_params=_params(("arbitrary",), 32), name="update_small",
    )(*args)


def kernel(x, norm_mix_g, w_in, conv_w, rnn_conv_w, rnn_conv_b, w_a, b_a, w_x, b_x, lru_lambda, g_norm_conv, g_norm_rnn, w_out, norm_mlp_g, w_mlp_in, w_mlp_out, final_norm_g, loss_target, m_norm_mix_g, m_w_in, m_conv_w, m_rnn_conv_w, m_rnn_conv_b, m_w_a, m_b_a, m_w_x, m_b_x, m_lru_lambda, m_g_norm_conv, m_g_norm_rnn, m_w_out, m_norm_mlp_g, m_w_mlp_in, m_w_mlp_out, m_final_norm_g, v_norm_mix_g, v_w_in, v_conv_w, v_rnn_conv_w, v_rnn_conv_b, v_w_a, v_b_a, v_w_x, v_b_x, v_lru_lambda, v_g_norm_conv, v_g_norm_rnn, v_w_out, v_norm_mlp_g, v_w_mlp_in, v_w_mlp_out, v_final_norm_g):
    t_len = x.shape[1]
    my_id = 4 * lax.axis_index("x") + 2 * lax.axis_index("y") + lax.axis_index("c")
    tm = min(256, t_len)
    tb = min(512, t_len)
    tk = min(512, t_len)

    xs = x.reshape(t_len, D_MODEL)
    tgt = loss_target.reshape(t_len, D_MODEL)
    flat = lambda a: a.reshape(a.shape[-2:]) if a.ndim == 3 else a.reshape(1, -1)
    heads = lambda a: a.reshape(LRU_WIDTH, HEAD_DIM)

    turned = lambda a: jnp.transpose(flat(a))
    win_shard, wout_shard, w1_shard, w2_shard, cp_shard = _prep_shards(
        turned(w_in), flat(w_out), flat(w_mlp_in), flat(w_mlp_out), flat(conv_w), flat(rnn_conv_w))

    u, h, win_t, cp_full = _in_proj(xs, flat(norm_mix_g), (win_shard, cp_shard), min(1024, t_len))
    cpack = cp_full.reshape(N_DEV, 8, 128)
    conv_full = jnp.transpose(cpack[:, 0:3, 0:64], (1, 0, 2)).reshape(3, CONV_WIDTH)
    rnn_full = jnp.transpose(cpack[:, 3:7, :], (1, 0, 2)).reshape(4, LRU_WIDTH)
    mixer_small = (conv_full, rnn_full, flat(rnn_conv_b), heads(w_a), flat(b_a), heads(w_x), flat(b_x),
                   flat(lru_lambda), flat(g_norm_conv), flat(g_norm_rnn))
    hs, y, xr, gate_r, gate_i, mult, w1_blk, wout_blk = _mixer_fwd(u, *mixer_small, (w1_shard, wout_shard), tm)
    wout_f = wout_blk.reshape(MIX_WIDTH, D_MODEL)
    x1, h2, z, w2_blk = _mlp_up(xs, y, flat(norm_mlp_g), wout_f, w1_blk, w2_shard, tb)
    dx1, dx2, vec_m, dpre = _mlp_down_bwd(x1, z, tgt, flat(norm_mlp_g), flat(final_norm_g), w1_blk,
                                          w2_blk.reshape(D_FF, D_MODEL), tb)
    (g_w1,) = _tn_weight_grad(h2, dpre, tk, "w_mlp_in_grad", col_blocks=N_DEV)
    (g_w2,) = _tn_weight_grad(z, dx2, tk, "w_mlp_out_grad")
    g_w2 = g_w2.reshape(N_DEV, D_FF // N_DEV, D_MODEL)
    g_wout, sib_w1, sib_w2 = _tn_weight_grad(y, dx1, tk, "w_out_grad", pair=(g_w1, g_w2))
    g_wout = g_wout.reshape(N_DEV, MIX_WIDTH // N_DEV, D_MODEL)
    hsend_w1, own_w1, hsend_w2, own_w2 = _pair_sum((g_w1, g_w2), (sib_w1, sib_w2), "pair_sum_w_mlp")
    du, vec_b, wab, landed_w1, landed_w2, sib_wout = _mixer_bwd(
        u, hs, dx1, (xr, gate_r, gate_i, mult), conv_full, rnn_full, flat(rnn_conv_b), heads(w_a), heads(w_x),
        flat(lru_lambda), flat(g_norm_conv), flat(g_norm_rnn), wout_f, (hsend_w1, hsend_w2), g_wout, tm)
    hsend_wout, own_wout = _pair_sum((g_wout,), (sib_wout,), "pair_sum_w_out")
    ax, ay, ac = lax.axis_index("x"), lax.axis_index("y"), lax.axis_index("c")
    chip_ids = jnp.stack([2 * cx + cy for cx, cy in [(ax, ay)] + _other_chips(ax, ay)]).astype(jnp.int32)
    core = jnp.reshape(ac, (1,)).astype(jnp.int32)
    tw = min(1024, t_len)
    g_others, landed_wout, vrecv_m, vrecv_b, wrecv = _w_in_grad_part(
        du, h, tw, "w_in_grad_others", chip_ids[1:4], chip=(hsend_wout,), small=(vec_m, vec_b, wab))
    g_own, sib_others = _w_in_grad_part(du, h, tw, "w_in_grad_own", chip_ids[0:1], halves=g_others)
    hsend_win = _pair_sum_parts(g_others, sib_others, core)
    grad_x, vec_x, landed_win, sib_own = _in_proj_bwd(du, dx1, xs, flat(norm_mix_g), win_t, tm, hsend_win, g_own)

    vsum, wsum = _final_small(vrecv_m, vrecv_b, wab, wrecv, vec_x)

    up_win = _update_w_in(g_own, sib_own, landed_win, turned(w_in), turned(m_w_in), turned(v_w_in), core, 256)
    up_win = [jnp.transpose(a) for a in up_win]
    up_wout = _update_sharded(own_wout, landed_wout, flat(w_out), flat(m_w_out), flat(v_w_out), 96, "update_w_out")
    up_w1 = _update_sharded(own_w1, landed_w1, flat(w_mlp_in), flat(m_w_mlp_in), flat(v_w_mlp_in), 256,
                            "update_w_mlp_in")
    up_w2 = _update_sharded(own_w2, landed_w2, flat(w_mlp_out), flat(m_w_mlp_out), flat(v_w_mlp_out), 256,
                            "update_w_mlp_out")

    g_cw = lax.dynamic_slice(vsum, (ROW_CW, 64 * my_id), (3, 64))
    g_rw = lax.dynamic_slice(vsum, (ROW_RW, 128 * my_id), (4, 128))
    small_w = (norm_mix_g, conv_w, rnn_conv_w, rnn_conv_b, w_a, b_a, w_x, b_x, lru_lambda, g_norm_conv, g_norm_rnn,
               norm_mlp_g, final_norm_g)
    small_m = (m_norm_mix_g, m_conv_w, m_rnn_conv_w, m_rnn_conv_b, m_w_a, m_b_a, m_w_x, m_b_x, m_lru_lambda,
               m_g_norm_conv, m_g_norm_rnn, m_norm_mlp_g, m_final_norm_g)
    small_v = (v_norm_mix_g, v_conv_w, v_rnn_conv_w, v_rnn_conv_b, v_w_a, v_b_a, v_w_x, v_b_x, v_lru_lambda,
               v_g_norm_conv, v_g_norm_rnn, v_norm_mlp_g, v_final_norm_g)
    is_heads = (False, False, False, False, True, False, True, False, False, False, False, False, False)
    as2d = lambda arrs: [heads(a) if hd else flat(a) for a, hd in zip(arrs, is_heads)]
    small_out = _update_small(vsum, wsum, g_cw, g_rw, as2d(small_w), as2d(small_m), as2d(small_v))
    loss = small_out[0].reshape(())

    names = ["norm_mix_g", "w_in", "conv_w", "rnn_conv_w", "rnn_conv_b", "w_a", "b_a", "w_x", "b_x", "lru_lambda",
             "g_norm_conv", "g_norm_rnn", "w_out", "norm_mlp_g", "w_mlp_in", "w_mlp_out", "final_norm_g"]
    originals = dict(zip(names, (norm_mix_g, w_in, conv_w, rnn_conv_w, rnn_conv_b, w_a, b_a, w_x, b_x, lru_lambda,
                                 g_norm_conv, g_norm_rnn, w_out, norm_mlp_g, w_mlp_in, w_mlp_out, final_norm_g)))
    results = {"w_in": up_win, "w_out": up_wout, "w_mlp_in": up_w1, "w_mlp_out": up_w2}
    small_names = ["norm_mix_g", "conv_w", "rnn_conv_w", "rnn_conv_b", "w_a", "b_a", "w_x", "b_x", "lru_lambda",
                   "g_norm_conv", "g_norm_rnn", "norm_mlp_g", "final_norm_g"]
    for k, nm in enumerate(small_names):
        results[nm] = small_out[1 + 4 * k:5 + 4 * k]
    out = [loss, grad_x.reshape(x.shape)]
    for kind in range(4):
        out += [results[nm][kind].reshape(originals[nm].shape) for nm in names]
    return tuple(out)
```

```python
import functools

import jax
import jax.numpy as jnp
from jax import lax
from jax.experimental import pallas as pl
from jax.experimental.pallas import tpu as pltpu

F32 = jnp.float32
BF16 = jnp.bfloat16

D_MODEL = 1024
HEAD_DIM = 64
CONV_WIDTH = 512
LRU_WIDTH = 1024
MIX_WIDTH = CONV_WIDTH + LRU_WIDTH
IN_COLS = 3 * CONV_WIDTH + 2 * LRU_WIDTH
D_FF = 4 * D_MODEL
GROUP = 256
EPS = 1e-6
LRU_C = 8.0
N_DEV = 8
SUB = 8

OFF_GB, OFF_GC, OFF_V, OFF_XR, OFF_G = 0, 512, 1024, 1536, 2560

ADAM_LR, ADAM_B1, ADAM_B2, ADAM_EPS, ADAM_WD, ADAM_STEP = 0.001, 0.9, 0.999, 1e-08, 0.01, 10
BC1 = 1.0 - ADAM_B1 ** ADAM_STEP
BC2 = 1.0 - ADAM_B2 ** ADAM_STEP

MIB = 1024 * 1024
MESH = pl.DeviceIdType.MESH

VEC_ROWS = 32
ROW_GF, ROW_GMLP, ROW_LOSS = 0, 1, 2
ROW_GNC, ROW_GNR, ROW_BR, ROW_BA, ROW_BX, ROW_LAM, ROW_CW, ROW_RW = 8, 9, 10, 11, 12, 13, 14, 17
ROW_GMIX = 24
ACC_GNC, ACC_GNR, ACC_BR, ACC_BA, ACC_BX, ACC_SP, ACC_CW, ACC_RW, N_ACC = 0, 1, 2, 3, 4, 5, 6, 9, 13


def _params(semantics=None, vmem_mib=48):
    return pltpu.CompilerParams(dimension_semantics=semantics, vmem_limit_bytes=vmem_mib * MIB)


def _rms(x):
    return lax.rsqrt(jnp.mean(x * x, axis=-1, keepdims=True) + EPS)


def _rms_bwd(dy, xhat, r, g):
    dyh = dy * g
    return r * (dyh - xhat * jnp.mean(dyh * xhat, axis=-1, keepdims=True))


def _sigmoid(x):
    return 0.5 + 0.5 * jnp.tanh(0.5 * x)


def _gelu(x):
    c0, c1 = 0.7978845608028654, 0.044715
    x2 = x * x
    t = jnp.tanh(x * (c0 + (c0 * c1) * x2))
    half = 0.5 + 0.5 * t
    ge = x * half
    dge = half + (ge - ge * half) * (2.0 * c0 + (6.0 * c0 * c1) * x2)
    return ge, dge


def _softplus_neg(lam):
    z = -lam
    e = jnp.exp(-jnp.abs(z))
    return jnp.maximum(z, 0.0) + jnp.where(e < 1e-4, e * (1.0 - 0.5 * e), jnp.log(1.0 + e))


def _lru_gates(pa, px, sp_c):
    ra = _sigmoid(pa)
    ii = _sigmoid(px)
    la = -ra * sp_c
    a = jnp.exp(la)
    x2 = 2.0 * la
    series = -x2 * (1.0 + x2 * (0.5 + x2 * (1.0 / 6.0 + x2 * (1.0 / 24.0))))
    m2 = jnp.where(x2 > -0.01, series, 1.0 - a * a)
    mult = jnp.where(m2 > 0.0, m2 * lax.rsqrt(m2), 0.0)
    return ra, ii, a, mult


def _down(cur, prev, s, row):
    return pltpu.roll(jnp.where(row < SUB - s, cur, prev), s, 0)


def _up(cur, nxt, s, row):
    return pltpu.roll(jnp.where(row >= s, cur, nxt), SUB - s, 0)


def _scan8_fwd(a, b, row):
    for s in (1, 2, 4):
        m = row >= s
        a_sh = pltpu.roll(a, s, 0)
        b_sh = pltpu.roll(b, s, 0)
        b = jnp.where(m, a * b_sh + b, b)
        a = jnp.where(m, a * a_sh, a)
    return a, b


def _scan8_rev(a, b, row):
    for s in (1, 2, 4):
        m = row < SUB - s
        a_sh = pltpu.roll(a, SUB - s, 0)
        b_sh = pltpu.roll(b, SUB - s, 0)
        b = jnp.where(m, a * b_sh + b, b)
        a = jnp.where(m, a * a_sh, a)
    return a, b


def _group_mask(shape):
    r = lax.broadcasted_iota(jnp.int32, shape, 0)
    c = lax.broadcasted_iota(jnp.int32, shape, 1)
    return ((r % GROUP) // HEAD_DIM) == (c // HEAD_DIM)


def _expand_heads(w):
    j = lax.broadcasted_iota(jnp.int32, (HEAD_DIM, GROUP), 0)
    c = lax.broadcasted_iota(jnp.int32, (HEAD_DIM, GROUP), 1)
    spread = (c % HEAD_DIM == j).astype(BF16)
    e = jnp.dot(w.astype(BF16), spread, preferred_element_type=F32)
    return jnp.where(_group_mask(e.shape), e, 0.0).astype(BF16)


def _fold_heads(p):
    p = jnp.where(_group_mask(p.shape), p, 0.0)
    c = lax.broadcasted_iota(jnp.int32, (GROUP, HEAD_DIM), 0)
    j = lax.broadcasted_iota(jnp.int32, (GROUP, HEAD_DIM), 1)
    fold = (c % HEAD_DIM == j).astype(BF16)
    hi = p.astype(BF16)
    rest = p - hi.astype(F32)
    mid = rest.astype(BF16)
    lo = (rest - mid.astype(F32)).astype(BF16)
    dot = functools.partial(jnp.dot, preferred_element_type=F32)
    return dot(hi, fold) + dot(mid, fold) + dot(lo, fold)


def _block_diag_apply(xb, wbd_ref):
    parts = [jnp.dot(xb[:, g * GROUP:(g + 1) * GROUP], wbd_ref[g * GROUP:(g + 1) * GROUP, :],
                     preferred_element_type=F32) for g in range(LRU_WIDTH // GROUP)]
    return jnp.concatenate(parts, axis=1)


def _block_diag_apply_t(db, wbd_ref):
    parts = [lax.dot_general(db[:, g * GROUP:(g + 1) * GROUP], wbd_ref[g * GROUP:(g + 1) * GROUP, :],
                             (((1,), (1,)), ((), ())), preferred_element_type=F32)
             for g in range(LRU_WIDTH // GROUP)]
    return jnp.concatenate(parts, axis=1)


def _dot_nt(a, b):
    return lax.dot_general(a, b, (((1,), (1,)), ((), ())), preferred_element_type=F32)


def _dot_tn(a, b):
    return lax.dot_general(a, b, (((0,), (0,)), ((), ())), preferred_element_type=F32)


CHUNKS_IN_FLIGHT = 8


def _chunk_loop(n_chunks, chunk, init):
    def body(k, carry):
        for j in range(CHUNKS_IN_FLIGHT):
            carry = chunk(k * CHUNKS_IN_FLIGHT + j, carry)
        return carry

    return lax.fori_loop(0, n_chunks // CHUNKS_IN_FLIGHT, body, init)


def _place():
    x, y, c = lax.axis_index("x"), lax.axis_index("y"), lax.axis_index("c")
    return x, y, c


def _block_id(chip, core):
    return 4 * chip[0] + 2 * chip[1] + core


def _other_chips(x, y):
    return [(1 - x, y), (x, 1 - y), (1 - x, 1 - y)]


def _remote_copy(src, dst, send_sem, recv_sem, to):
    return pltpu.make_async_remote_copy(src_ref=src, dst_ref=dst, send_sem=send_sem, recv_sem=recv_sem,
                                        device_id=to, device_id_type=MESH)


HBM_SPEC = pl.BlockSpec(memory_space=pl.ANY)


def _in_hbm(*arrays):
    return [pltpu.with_memory_space_constraint(a, pltpu.HBM) for a in arrays]


def _prep_shards(w_in_t, w_out, w_mlp_in, w_mlp_out, conv_w, rnn_conv_w):
    def body(win_ref, wout_ref, w1_ref, w2_ref, cw_ref, rw_ref, o_win, o_wout, o_w1, o_w2, o_cp):
        o_win[...] = win_ref[...].astype(BF16)
        o_wout[...] = wout_ref[...].astype(BF16)
        o_w1[...] = w1_ref[...].astype(BF16)
        o_w2[...] = w2_ref[...].astype(BF16)
        o_cp[...] = jnp.zeros(o_cp.shape, F32)
        o_cp[0:3, 0:64] = cw_ref[...]
        o_cp[3:7, :] = rw_ref[...]

    whole = lambda shape: pl.BlockSpec(shape, lambda i: (0,) * len(shape))
    args = (w_in_t, w_out, w_mlp_in, w_mlp_out, conv_w, rnn_conv_w)
    shapes = [(w_in_t.shape, BF16), (w_out.shape, BF16), (w_mlp_in.shape, BF16), (w_mlp_out.shape, BF16),
              ((8, 128), F32)]
    return pl.pallas_call(
        body, grid=(1,), out_shape=[jax.ShapeDtypeStruct(s, d) for s, d in shapes],
        in_specs=[whole(a.shape) for a in args], out_specs=[whole(s) for s, _ in shapes],
        compiler_params=_params(("arbitrary",), 40), name="prep_shards",
    )(*args)


def _host_all_gather(step, n_steps, shards, fulls, send_sems, recv_sems, local_sems):
    x, y, c = _place()
    me = (x, y, c)
    my_id = _block_id((x, y), c)
    sibling = (x, y, 1 - c)
    chips = _other_chips(x, y)
    n_arr = len(shards)

    def copy(arr, k, block, to, src=None):
        dst = fulls[arr].at[block]
        return _remote_copy(dst if src is None else src, dst, send_sems.at[arr, k], recv_sems.at[arr, k], to)

    def local(arr):
        return pltpu.make_async_copy(shards[arr], fulls[arr].at[my_id], local_sems.at[arr])

    @pl.when(step == 0)
    def _():
        for arr in range(n_arr):
            local(arr).start()
            copy(arr, 0, my_id, sibling, shards[arr]).start()
            for j, chip in enumerate(chips):
                copy(arr, 1 + j, my_id, (*chip, c), shards[arr]).start()

    @pl.when(step == max(n_steps - 2, 0))
    def _():
        for j, chip in enumerate(chips):
            for arr in range(n_arr):
                copy(arr, 1 + j, _block_id(chip, c), me).wait_recv()
                copy(arr, 4 + j, _block_id(chip, c), sibling).start()

    @pl.when(step == n_steps - 1)
    def _():
        for arr in range(n_arr):
            copy(arr, 0, _block_id((x, y), 1 - c), me).wait_recv()
            for j, chip in enumerate(chips):
                copy(arr, 4 + j, _block_id(chip, 1 - c), me).wait_recv()
            for k in range(4):
                copy(arr, k, my_id, me, shards[arr]).wait_send()
            for j, chip in enumerate(chips):
                copy(arr, 4 + j, _block_id(chip, c), me).wait_send()
            local(arr).wait()


def _host_pair_exchange(step, n_steps, gs, sibs, send_sems, recv_sems):
    x, y, c = _place()
    sibling = (x, y, 1 - c)
    chips = [(x, y)] + _other_chips(x, y)

    def d2d(arr, q):
        return _remote_copy(gs[arr].at[_block_id(chips[q], 1 - c)], sibs[arr].at[q],
                            send_sems.at[arr, q], recv_sems.at[arr, q], sibling)

    @pl.when(step == 0)
    def _():
        for arr in range(len(gs)):
            for q in (1, 2, 3, 0):
                d2d(arr, q).start()

    @pl.when(step == n_steps - 1)
    def _():
        for arr in range(len(gs)):
            for q in range(4):
                d2d(arr, q).wait()


def _host_chip_exchange(step, n_steps, hsends, hrecvs, send_sems, recv_sems):
    x, y, c = _place()
    chips = _other_chips(x, y)

    def ici(arr, j):
        return _remote_copy(hsends[arr].at[j], hrecvs[arr].at[j], send_sems.at[arr, j], recv_sems.at[arr, j],
                            (*chips[j], c))

    @pl.when(step == 0)
    def _():
        for arr in range(len(hsends)):
            for j in range(3):
                ici(arr, j).start()

    @pl.when(step == n_steps - 1)
    def _():
        for arr in range(len(hsends)):
            for j in range(3):
                ici(arr, j).wait()


def _host_half_exchange(step, n_steps, parts, sibs, send_sems, recv_sems):
    x, y, c = _place()
    n_q, rows2, _ = parts.shape
    half = rows2 // 2

    def d2d(q):
        src = parts.at[q, pl.ds(pl.multiple_of((1 - c) * half, 16), half), :]
        return _remote_copy(src, sibs.at[q], send_sems.at[q], recv_sems.at[q], (x, y, 1 - c))

    @pl.when(step == 0)
    def _():
        for q in range(n_q):
            d2d(q).start()

    @pl.when(step == n_steps - 1)
    def _():
        for q in range(n_q):
            d2d(q).wait()


def _peer(x, y, c, k):
    return (x ^ ((k >> 2) & 1), y ^ ((k >> 1) & 1), c ^ (k & 1))


def _host_small_exchange(step, n_steps, vec_m, vec_b, wab, vrecv_m, vrecv_b, wrecv, send_sems, recv_sems, local_sems):
    x, y, c = _place()
    my_id = _block_id((x, y), c)
    wrows = wab.shape[0] // N_DEV

    def copies(k):
        to = _peer(x, y, c, k)
        block = wab.at[pl.ds(pl.multiple_of(_block_id(to[0:2], to[2]) * wrows, SUB), wrows), :]
        return [_remote_copy(vec_m, vrecv_m.at[my_id], send_sems.at[0, k], recv_sems.at[0, k], to),
                _remote_copy(vec_b, vrecv_b.at[my_id], send_sems.at[1, k], recv_sems.at[1, k], to),
                _remote_copy(block, wrecv.at[k], send_sems.at[2, k], recv_sems.at[2, k], to)]

    mine = [pltpu.make_async_copy(vec_m, vrecv_m.at[my_id], local_sems.at[0]),
            pltpu.make_async_copy(vec_b, vrecv_b.at[my_id], local_sems.at[1])]

    @pl.when(step == 0)
    def _():
        for cp in mine:
            cp.start()
        for k in range(1, N_DEV):
            for cp in copies(k):
                cp.start()

    @pl.when(step == n_steps - 1)
    def _():
        for k in range(1, N_DEV):
            for cp in copies(k):
                cp.wait()
        for cp in mine:
            cp.wait()


def _pair_sum_parts(parts, sibs, core):
    n_q, rows2, cols = parts.shape
    half = rows2 // 2

    def body(core_ref, g_ref, s_ref, o_ref):
        o_ref[0] = (g_ref[0, 0].astype(F32) + s_ref[0].astype(F32)).astype(BF16)

    block = (1, half, cols)
    grid_spec = pltpu.PrefetchScalarGridSpec(
        num_scalar_prefetch=1, grid=(n_q,),
        in_specs=[pl.BlockSpec((1, 1, half, cols), lambda q, cr: (q, cr[0], 0, 0)),
                  pl.BlockSpec(block, lambda q, cr: (q, 0, 0))],
        out_specs=pl.BlockSpec(block, lambda q, cr: (q, 0, 0)))
    return pl.pallas_call(
        body, grid_spec=grid_spec, out_shape=pltpu.HBM((n_q, half, cols), BF16),
        compiler_params=_params(("arbitrary",), 32), name="pair_sum_w_in",
    )(core, *_in_hbm(parts.reshape(n_q, 2, half, cols), sibs))


def _pair_sum(gs, sibs, name):
    n_arr = len(gs)
    x, y, c = _place()
    slots = jnp.stack([_block_id(chip, c) for chip in [(x, y)] + _other_chips(x, y)]).astype(jnp.int32)

    def body(slots_ref, *refs):
        q = pl.program_id(0)
        for k in range(n_arr):
            g_ref, sib_ref = refs[2 * k:2 * k + 2]
            hs_ref, own_ref = refs[2 * n_arr + 2 * k:2 * n_arr + 2 * k + 2]
            both = g_ref[0].astype(F32) + sib_ref[0].astype(F32)

            @pl.when(q == 0)
            def _(own_ref=own_ref, both=both):
                own_ref[...] = both

            @pl.when(q > 0)
            def _(hs_ref=hs_ref, both=both):
                hs_ref[0] = both.astype(BF16)

    in_specs, out_specs, out_shape, args = [], [], [], []
    for g, sib in zip(gs, sibs):
        _, rows, cols = g.shape
        block = (1, rows, cols)
        in_specs += [pl.BlockSpec(block, lambda q, s: (s[q], 0, 0)), pl.BlockSpec(block, lambda q, s: (q, 0, 0))]
        out_specs += [pl.BlockSpec(block, lambda q, s: (jnp.maximum(q - 1, 0), 0, 0)),
                      pl.BlockSpec((rows, cols), lambda q, s: (0, 0))]
        out_shape += [pltpu.HBM((3, rows, cols), BF16), pltpu.HBM((rows, cols), F32)]
        args += _in_hbm(g, sib)
    grid_spec = pltpu.PrefetchScalarGridSpec(num_scalar_prefetch=1, grid=(4,), in_specs=in_specs, out_specs=out_specs)
    return pl.pallas_call(
        body, grid_spec=grid_spec, out_shape=out_shape,
        compiler_params=_params(("arbitrary",), 40), name=name,
    )(slots, *args)


def _exchange_scratch(n_arr, n_copies):
    return [pltpu.SemaphoreType.DMA((n_arr, n_copies)), pltpu.SemaphoreType.DMA((n_arr, n_copies))]


def _final_small(vrecv_m, vrecv_b, wab, wrecv, vec_x):
    wrows = wab.shape[0] // N_DEV

    def body(vm_ref, vb_ref, w_ref, wr_ref, vx_ref, o_vec, o_w, xrecv, wred, x_send, x_recv, b_send, b_recv):
        x, y, c = _place()
        my_id = _block_id((x, y), c)
        my_rows = pl.ds(pl.multiple_of(my_id * wrows, SUB), wrows)

        def xcopy(k):
            return _remote_copy(vx_ref, xrecv.at[my_id], x_send.at[k], x_recv.at[k], _peer(x, y, c, k))

        def bcopy(k):
            return _remote_copy(wred, o_w.at[my_rows, :], b_send.at[k], b_recv.at[k], _peer(x, y, c, k))

        xrecv[my_id] = vx_ref[...]
        for k in range(1, N_DEV):
            xcopy(k).start()
        red = w_ref[my_rows, :]
        for k in range(1, N_DEV):
            red = red + wr_ref[k]
        wred[...] = red
        o_w[my_rows, :] = red
        for k in range(1, N_DEV):
            bcopy(k).start()
        for k in range(1, N_DEV):
            xcopy(k).wait_recv()
        for rows, ref in ((slice(0, 8), vm_ref), (slice(8, 24), vb_ref), (slice(24, 32), xrecv)):
            tot = ref[0]
            for s in range(1, N_DEV):
                tot = tot + ref[s]
            o_vec[rows, :] = tot
        for k in range(1, N_DEV):
            bcopy(k).wait_recv()
        for k in range(1, N_DEV):
            xcopy(k).wait_send()
            bcopy(k).wait_send()

    vm = pl.BlockSpec(memory_space=pltpu.VMEM)
    dma8 = pltpu.SemaphoreType.DMA((N_DEV,))
    return pl.pallas_call(
        body, out_shape=(jax.ShapeDtypeStruct((VEC_ROWS, D_MODEL), F32), jax.ShapeDtypeStruct(wab.shape, F32)),
        in_specs=[vm] * 5, out_specs=[vm] * 2,
        scratch_shapes=[pltpu.VMEM((N_DEV, SUB, D_MODEL), F32), pltpu.VMEM((wrows, HEAD_DIM), F32),
                        dma8, dma8, dma8, dma8],
        compiler_params=_params(vmem_mib=32), name="final_small",
    )(vrecv_m, vrecv_b, wab, wrecv, vec_x)


def _in_proj(x, g_mix, shards, tm):
    t_len = x.shape[0]
    n_t = t_len // tm
    n_arr = len(shards)
    rows = [s.shape[0] for s in shards]
    width = 2 * rows[0]
    ax, ay = lax.axis_index("x"), lax.axis_index("y")
    order = jnp.stack([2 * cx + cy for cx, cy in [(ax, ay)] + _other_chips(ax, ay)]).astype(jnp.int32)

    def body(order_ref, x_ref, g_ref, *rest):
        shard_refs = rest[0:n_arr]
        u_ref, h_ref = rest[n_arr:n_arr + 2]
        fulls = rest[n_arr + 2:2 * n_arr + 2]
        h_s, wbuf, send_sems, recv_sems, local_sems, load_sem = rest[2 * n_arr + 2:]
        p = pl.program_id(0)
        i = pl.program_id(1)
        x_, y_, c = _place()
        me = (x_, y_, c)
        my_id = _block_id((x_, y_), c)
        sibling = (x_, y_, 1 - c)
        chips = _other_chips(x_, y_)

        def block(arr, blk):
            return fulls[arr].at[pl.ds(pl.multiple_of(blk * rows[arr], rows[arr]), rows[arr]), :]

        def copy(arr, k, blk, to, src=None):
            dst = block(arr, blk)
            return _remote_copy(dst if src is None else src, dst, send_sems.at[arr, k], recv_sems.at[arr, k], to)

        def local(arr):
            return pltpu.make_async_copy(shard_refs[arr], block(arr, my_id), local_sems.at[arr])

        def load_chip(chip, slot):
            start = pl.multiple_of((2 * chip[0] + chip[1]) * width, width)
            return pltpu.make_async_copy(fulls[0].at[pl.ds(start, width), :], wbuf.at[slot], load_sem.at[slot])

        def pass_on(j):
            for arr in range(n_arr):
                copy(arr, 1 + j, _block_id(chips[j], c), me).wait_recv()
                copy(arr, 4 + j, _block_id(chips[j], c), sibling).start()

        def complete(j):
            for arr in range(n_arr):
                copy(arr, 4 + j, _block_id(chips[j], 1 - c), me).wait_recv()

        @pl.when((p == 0) & (i == 0))
        def _():
            for arr in range(n_arr):
                local(arr).start()
                copy(arr, 0, my_id, sibling, shard_refs[arr]).start()
                for j in (0, 1):
                    copy(arr, 1 + j, my_id, (*chips[j], c), shard_refs[arr]).start()
            for arr in range(n_arr):
                local(arr).wait()
                copy(arr, 0, _block_id((x_, y_), 1 - c), me).wait_recv()
            load_chip((x_, y_), 0).start()
            load_chip((x_, y_), 0).wait()

        @pl.when((p == 1) & (i == 0))
        def _():
            pass_on(0)
            for arr in range(n_arr):
                copy(arr, 3, my_id, (*chips[2], c), shard_refs[arr]).start()
            pass_on(1)
            complete(0)
            load_chip(chips[0], 1).start()
            load_chip(chips[0], 1).wait()
            complete(1)
            load_chip(chips[1], 0).start()

        @pl.when((p == 2) & (i == 0))
        def _():
            load_chip(chips[1], 0).wait()

        @pl.when((p == 2) & (i == n_t - 1))
        def _():
            pass_on(2)
            complete(2)
            load_chip(chips[2], 1).start()

        @pl.when((p == 3) & (i == 0))
        def _():
            load_chip(chips[2], 1).wait()

        @pl.when((p == 3) & (i == n_t - 1))
        def _():
            for arr in range(n_arr):
                for k in range(4):
                    copy(arr, k, my_id, me, shard_refs[arr]).wait_send()
                for j, chip in enumerate(chips):
                    copy(arr, 4 + j, _block_id(chip, c), me).wait_send()

        tile = pl.ds(pl.multiple_of(i * tm, tm), tm)

        @pl.when(p == 0)
        def _():
            xv = x_ref[...]
            h = (xv * _rms(xv) * g_ref[...]).astype(BF16)
            h_ref[...] = h
            h_s[tile, :] = h

        for slot in (0, 1):
            @pl.when(p % 2 == slot)
            def _(slot=slot):
                u_ref[...] = _dot_nt(h_s[tile, :], wbuf[slot])

    first_pass = lambda p, i, o: (jnp.where(p == 0, i, n_t - 1), 0)
    grid_spec = pltpu.PrefetchScalarGridSpec(
        num_scalar_prefetch=1, grid=(4, n_t),
        in_specs=[pl.BlockSpec((tm, D_MODEL), first_pass), pl.BlockSpec((1, D_MODEL), lambda p, i, o: (0, 0))]
        + [HBM_SPEC] * n_arr,
        out_specs=[pl.BlockSpec((tm, width), lambda p, i, o: (i, o[p])), pl.BlockSpec((tm, D_MODEL), first_pass)]
        + [HBM_SPEC] * n_arr,
        scratch_shapes=[pltpu.VMEM((t_len, D_MODEL), BF16), pltpu.VMEM((2, width, D_MODEL), BF16)]
        + _exchange_scratch(n_arr, 7) + [pltpu.SemaphoreType.DMA((n_arr,)), pltpu.SemaphoreType.DMA((2,))])
    return pl.pallas_call(
        body, grid_spec=grid_spec,
        out_shape=[jax.ShapeDtypeStruct((t_len, IN_COLS), F32), jax.ShapeDtypeStruct((t_len, D_MODEL), BF16)]
        + [jax.ShapeDtypeStruct((N_DEV * s.shape[0], s.shape[1]), s.dtype) for s in shards],
        compiler_params=_params(("arbitrary", "arbitrary"), 48), name="in_proj",
    )(order, x, g_mix, *shards)


def _conv3_chunk(u_ref, r, cv_prev, cw, row):
    gb = u_ref[pl.ds(r, SUB), OFF_GB:OFF_GB + CONV_WIDTH]
    gc = u_ref[pl.ds(r, SUB), OFF_GC:OFF_GC + CONV_WIDTH]
    v = u_ref[pl.ds(r, SUB), OFF_V:OFF_V + CONV_WIDTH]
    cv = gc * v
    cv_m1 = _down(cv, cv_prev, 1, row)
    cv_m2 = _down(cv, cv_prev, 2, row)
    cq = cw[2:3, :] * cv + cw[1:2, :] * cv_m1 + cw[0:1, :] * cv_m2
    return gb, gc, v, cv, cv_m1, cv_m2, cq


def _conv4_chunk(u_ref, r, xin_prev, rw, rb, row):
    xin = u_ref[pl.ds(r, SUB), OFF_XR:OFF_XR + LRU_WIDTH]
    m1 = _down(xin, xin_prev, 1, row)
    m2 = _down(xin, xin_prev, 2, row)
    m3 = _down(xin, xin_prev, 3, row)
    xr = rw[3:4, :] * xin + rw[2:3, :] * m1 + rw[1:2, :] * m2 + rw[0:1, :] * m3 + rb
    return xin, m1, m2, m3, xr


def _mixer_fwd(u, conv_w, rnn_conv_w, rnn_conv_b, wa, b_a, wx, b_x, lam, gnc, gnr, shards, tm):
    t_len = u.shape[0]
    n_steps = t_len // tm
    n_chunks = tm // SUB
    n_arr = len(shards)

    def body(u_ref, cw_ref, rw_ref, rb_ref, wa_ref, ba_ref, wx_ref, bx_ref, lam_ref, gnc_ref, gnr_ref, *rest):
        shard_refs = rest[0:n_arr]
        hs_ref, y_ref, xr_s, ra_ref, ii_ref, mult_ref = rest[n_arr:n_arr + 6]
        fulls = rest[n_arr + 6:2 * n_arr + 6]
        (y_s, pa_s, px_s, wabd, wxbd, cv_car, xin_car, h_car,
         send_sems, recv_sems, local_sems) = rest[2 * n_arr + 6:]
        _host_all_gather(pl.program_id(0), n_steps, shard_refs, fulls, send_sems, recv_sems, local_sems)

        @pl.when(pl.program_id(0) == 0)
        def _():
            cv_car[...] = jnp.zeros(cv_car.shape, F32)
            xin_car[...] = jnp.zeros(xin_car.shape, F32)
            h_car[...] = jnp.zeros(h_car.shape, F32)
            wabd[...] = _expand_heads(wa_ref[...])
            wxbd[...] = _expand_heads(wx_ref[...])

        row_c = lax.broadcasted_iota(jnp.int32, (SUB, CONV_WIDTH), 0)
        row_r = lax.broadcasted_iota(jnp.int32, (SUB, LRU_WIDTH), 0)
        cw = cw_ref[...]
        rw = rw_ref[...]
        rb = rb_ref[...]
        g_c = gnc_ref[...]
        g_r = gnr_ref[...]
        sp_c = LRU_C * _softplus_neg(lam_ref[...])

        def convs(i, carry):
            cv_prev, xin_prev = carry
            r = pl.multiple_of(i * SUB, SUB)
            gb, _, _, cv, _, _, cq = _conv3_chunk(u_ref, r, cv_prev, cw, row_c)
            y_c = gb * cq
            y_s[pl.ds(r, SUB), 0:CONV_WIDTH] = y_c * _rms(y_c) * g_c
            xin, _, _, _, xr = _conv4_chunk(u_ref, r, xin_prev, rw, rb, row_r)
            xr_s[pl.ds(r, SUB), :] = xr
            return cv, xin

        cv_last, xin_last = _chunk_loop(n_chunks, convs, (cv_car[...], xin_car[...]))
        cv_car[...] = cv_last
        xin_car[...] = xin_last

        xrb = xr_s[...].astype(BF16)
        pa_s[...] = _block_diag_apply(xrb, wabd) + ba_ref[...]
        px_s[...] = _block_diag_apply(xrb, wxbd) + bx_ref[...]

        def recur(i, h_prev):
            r = pl.multiple_of(i * SUB, SUB)
            xr = xr_s[pl.ds(r, SUB), :]
            ra, ii, a, mult = _lru_gates(pa_s[pl.ds(r, SUB), :], px_s[pl.ds(r, SUB), :], sp_c)
            ra_ref[pl.ds(r, SUB), :] = ra
            ii_ref[pl.ds(r, SUB), :] = ii
            mult_ref[pl.ds(r, SUB), :] = mult
            a_cum, b_cum = _scan8_fwd(a, mult * ii * xr, row_r)
            h = a_cum * h_prev + b_cum
            hs_ref[pl.ds(r, SUB), :] = h
            ge, _ = _gelu(u_ref[pl.ds(r, SUB), OFF_G:OFF_G + LRU_WIDTH])
            y_r = h * ge
            y_s[pl.ds(r, SUB), CONV_WIDTH:MIX_WIDTH] = y_r * _rms(y_r) * g_r
            return h[SUB - 1:SUB, :]

        h_car[...] = _chunk_loop(n_chunks, recur, h_car[...])

        y_ref[...] = y_s[...].astype(BF16)

    row_tile = lambda w: pl.BlockSpec((tm, w), lambda i: (i, 0))
    whole = lambda a: pl.BlockSpec(a.shape, lambda i: (0,) * a.ndim)
    smalls = (conv_w, rnn_conv_w, rnn_conv_b, wa, b_a, wx, b_x, lam, gnc, gnr)
    return pl.pallas_call(
        body, grid=(n_steps,),
        in_specs=[row_tile(IN_COLS)] + [whole(a) for a in smalls] + [HBM_SPEC] * n_arr,
        out_specs=[row_tile(LRU_WIDTH), row_tile(MIX_WIDTH)] + [row_tile(LRU_WIDTH)] * 4 + [HBM_SPEC] * n_arr,
        out_shape=[jax.ShapeDtypeStruct((t_len, LRU_WIDTH), F32), jax.ShapeDtypeStruct((t_len, MIX_WIDTH), BF16)]
        + [jax.ShapeDtypeStruct((t_len, LRU_WIDTH), F32)] * 4
        + [jax.ShapeDtypeStruct((N_DEV,) + s.shape, BF16) for s in shards],
        scratch_shapes=[pltpu.VMEM((tm, MIX_WIDTH), F32),
                        pltpu.VMEM((tm, LRU_WIDTH), F32), pltpu.VMEM((tm, LRU_WIDTH), F32),
                        pltpu.VMEM((LRU_WIDTH, GROUP), BF16), pltpu.VMEM((LRU_WIDTH, GROUP), BF16),
                        pltpu.VMEM((SUB, CONV_WIDTH), F32), pltpu.VMEM((SUB, LRU_WIDTH), F32),
                        pltpu.VMEM((1, LRU_WIDTH), F32)]
        + _exchange_scratch(n_arr, 7) + [pltpu.SemaphoreType.DMA((n_arr,))],
        compiler_params=_params(("arbitrary",), 56), name="mixer_fwd",
    )(u, *smalls, *shards)


def _mlp_up(x, y, g_mlp, w_out, w1, w2_shard, tm):
    t_len = x.shape[0]
    n_steps = t_len // tm
    n_blk, _, blk = w1.shape

    def body(x_ref, y_ref, gm_ref, wout_hbm, w1_hbm, w2_ref, x1_ref, h2_ref, z_ref, w2_full,
             wout_s, w1_s, sem, send_sems, recv_sems, local_sems):
        step = pl.program_id(0)
        _host_all_gather(step, n_steps, [w2_ref], [w2_full], send_sems, recv_sems, local_sems)

        load_wout = pltpu.make_async_copy(wout_hbm, wout_s, sem.at[0])
        load_w1 = pltpu.make_async_copy(w1_hbm, w1_s, sem.at[1])

        @pl.when(step == 0)
        def _():
            load_wout.start()
            load_w1.start()
            load_wout.wait()

        x1v = x_ref[...] + jnp.dot(y_ref[...], wout_s[...], preferred_element_type=F32)
        x1_ref[...] = x1v
        h2 = (x1v * _rms(x1v) * gm_ref[...]).astype(BF16)
        h2_ref[...] = h2

        @pl.when(step == 0)
        def _():
            load_w1.wait()

        for k in range(n_blk):
            rp = jnp.maximum(jnp.dot(h2, w1_s[k], preferred_element_type=F32), 0.0)
            z_ref[:, k * blk:(k + 1) * blk] = (rp * rp).astype(BF16)

    row_tile = lambda w: pl.BlockSpec((tm, w), lambda i: (i, 0))
    return pl.pallas_call(
        body, grid=(n_steps,),
        in_specs=[row_tile(D_MODEL), row_tile(MIX_WIDTH), pl.BlockSpec((1, D_MODEL), lambda i: (0, 0)),
                  HBM_SPEC, HBM_SPEC, HBM_SPEC],
        out_specs=[row_tile(D_MODEL), row_tile(D_MODEL), row_tile(D_FF), HBM_SPEC],
        out_shape=[jax.ShapeDtypeStruct((t_len, D_MODEL), F32), jax.ShapeDtypeStruct((t_len, D_MODEL), BF16),
                   jax.ShapeDtypeStruct((t_len, D_FF), BF16), jax.ShapeDtypeStruct((N_DEV,) + w2_shard.shape, BF16)],
        scratch_shapes=[pltpu.VMEM(w_out.shape, BF16), pltpu.VMEM(w1.shape, BF16), pltpu.SemaphoreType.DMA((2,))]
        + _exchange_scratch(1, 7) + [pltpu.SemaphoreType.DMA((1,))],
        compiler_params=_params(("arbitrary",), 48), name="mlp_up",
    )(x, y, g_mlp, w_out, w1, w2_shard)


def _mlp_down_bwd(x1, z, target, g_mlp, g_f, w1, w2, tm):
    t_len = x1.shape[0]
    n_steps = t_len // tm
    n_blk, _, blk = w1.shape

    def body(x1_ref, z_ref, tg_ref, gm_ref, gf_ref, w1_hbm, w2_hbm, dx1_ref, dx2_ref, vec_ref, dpre_hbm,
             w1_s, w2_s, dp_s, sem, out_sem):
        step = pl.program_id(0)
        rows = pl.ds(pl.multiple_of(step * tm, tm), tm)
        dp_out = pltpu.make_async_copy(dp_s, dpre_hbm.at[rows, :], out_sem.at[0])

        load_w1 = pltpu.make_async_copy(w1_hbm, w1_s, sem.at[0])
        load_w2 = pltpu.make_async_copy(w2_hbm, w2_s, sem.at[1])

        @pl.when(step == 0)
        def _():
            load_w2.start()
            load_w1.start()
            vec_ref[...] = jnp.zeros(vec_ref.shape, F32)
            load_w2.wait()

        x1v = x1_ref[...]
        g_m = gm_ref[...]
        g_o = gf_ref[...]
        r2 = _rms(x1v)
        x1h = x1v * r2
        x2 = x1v + jnp.dot(z_ref[...], w2_s[...], preferred_element_type=F32)
        r3 = _rms(x2)
        x2h = x2 * r3
        err = x2h * g_o - tg_ref[...]
        dout = err * (1.0 / D_MODEL)
        vec_ref[ROW_LOSS:ROW_LOSS + 1, :] += (0.5 / D_MODEL) * jnp.sum(err * err, axis=0, keepdims=True)
        vec_ref[ROW_GF:ROW_GF + 1, :] += jnp.sum(dout * x2h, axis=0, keepdims=True)
        dx2 = _rms_bwd(dout, x2h, r3, g_o)
        dx2b = dx2.astype(BF16)
        dx2_ref[...] = dx2b
        dh2 = jnp.zeros((tm, D_MODEL), F32)

        @pl.when(step > 0)
        def _():
            dp_out.wait()

        @pl.when(step == 0)
        def _():
            load_w1.wait()

        for k in range(n_blk):
            cols = slice(k * blk, (k + 1) * blk)
            dz = _dot_nt(dx2b, w2_s[cols, :])
            dpb = (dz * 2.0 * jnp.sqrt(z_ref[:, cols].astype(F32))).astype(BF16)
            dp_s[:, cols] = dpb
            dh2 = dh2 + _dot_nt(dpb, w1_s[k])
        dp_out.start()
        vec_ref[ROW_GMLP:ROW_GMLP + 1, :] += jnp.sum(dh2 * x1h, axis=0, keepdims=True)
        dx1_ref[...] = dx2 + _rms_bwd(dh2, x1h, r2, g_m)

        @pl.when(step == n_steps - 1)
        def _():
            dp_out.wait()

    row_tile = lambda w: pl.BlockSpec((tm, w), lambda i: (i, 0))
    vec_spec = pl.BlockSpec((1, D_MODEL), lambda i: (0, 0))
    return pl.pallas_call(
        body, grid=(n_steps,),
        in_specs=[row_tile(D_MODEL), row_tile(D_FF), row_tile(D_MODEL), vec_spec, vec_spec, HBM_SPEC, HBM_SPEC],
        out_specs=[row_tile(D_MODEL), row_tile(D_MODEL), pl.BlockSpec((SUB, D_MODEL), lambda i: (0, 0)), HBM_SPEC],
        out_shape=[jax.ShapeDtypeStruct((t_len, D_MODEL), F32), jax.ShapeDtypeStruct((t_len, D_MODEL), BF16),
                   jax.ShapeDtypeStruct((SUB, D_MODEL), F32), jax.ShapeDtypeStruct((t_len, D_FF), BF16)],
        scratch_shapes=[pltpu.VMEM(w1.shape, BF16), pltpu.VMEM(w2.shape, BF16), pltpu.VMEM((tm, D_FF), BF16),
                        pltpu.SemaphoreType.DMA((2,)), pltpu.SemaphoreType.DMA((1,))],
        compiler_params=_params(("arbitrary",), 56), name="mlp_down_bwd",
    )(x1, z, target, g_mlp, g_f, w1, w2)


def _mixer_bwd(u, hs, dx1, saved, conv_w, rnn_conv_w, rnn_conv_b, wa, wx, lam, gnc, gnr, w_out,
               chip_sums, g_wout, tm):
    t_len = u.shape[0]
    n_tiles = t_len // tm
    n_chunks = tm // SUB
    per_tile = tm // SUB
    n_sums = len(chip_sums)

    def body(u_ref, up_ref, hs_ref, hp_ref, dx1_ref, xr_ref, ra_ref, ii_ref, mult_ref,
             cw_ref, rw_ref, rb_ref, wa_ref, wx_ref, lam_ref, gnc_ref, gnr_ref, wout_ref, *rest):
        hsends = rest[0:n_sums]
        gwout_ref = rest[n_sums]
        du_ref, vec_ref, wab_ref = rest[n_sums + 1:n_sums + 4]
        hrecvs = rest[n_sums + 4:2 * n_sums + 4]
        sib_wout = rest[2 * n_sums + 4]
        (du_s, dy_s, dpa_s, dpx_s, dxr_s, wabd, wxbd, acc, dwa_acc, dwx_acc,
         a_car, dh_car, dcq_car, dxr_car, i_send, i_recv, d_send, d_recv) = rest[2 * n_sums + 5:]
        step = pl.program_id(0)
        _host_chip_exchange(step, n_tiles, hsends, hrecvs, i_send, i_recv)
        _host_pair_exchange(step, n_tiles, [gwout_ref], [sib_wout], d_send, d_recv)
        has_prev = (step < n_tiles - 1).astype(F32)

        @pl.when(step == 0)
        def _():
            acc[...] = jnp.zeros(acc.shape, F32)
            dwa_acc[...] = jnp.zeros(dwa_acc.shape, F32)
            dwx_acc[...] = jnp.zeros(dwx_acc.shape, F32)
            a_car[...] = jnp.ones(a_car.shape, F32)
            dh_car[...] = jnp.zeros(dh_car.shape, F32)
            dcq_car[...] = jnp.zeros(dcq_car.shape, F32)
            dxr_car[...] = jnp.zeros(dxr_car.shape, F32)
            wabd[...] = _expand_heads(wa_ref[...])
            wxbd[...] = _expand_heads(wx_ref[...])

        row_c = lax.broadcasted_iota(jnp.int32, (SUB, CONV_WIDTH), 0)
        row_r = lax.broadcasted_iota(jnp.int32, (SUB, LRU_WIDTH), 0)
        cw = cw_ref[...]
        rw = rw_ref[...]
        rb = rb_ref[...]
        g_c = gnc_ref[...]
        g_r = gnr_ref[...]
        sp_c = LRU_C * _softplus_neg(lam_ref[...])

        up = up_ref[...] * has_prev
        cv_before = up[:, OFF_GC:OFF_GC + CONV_WIDTH] * up[:, OFF_V:OFF_V + CONV_WIDTH]
        xin_before = up[:, OFF_XR:OFF_XR + LRU_WIDTH]
        hs_before = hp_ref[...] * has_prev

        dy_s[...] = _dot_nt(dx1_ref[...].astype(BF16), wout_ref[...])

        xrb = xr_ref[...].astype(BF16)

        def recur_bwd(j, carry):
            a_later, dh_later = carry
            i = n_chunks - 1 - j
            r = pl.multiple_of(i * SUB, SUB)
            rp = pl.multiple_of(jnp.maximum(i - 1, 0) * SUB, SUB)
            xr = xr_ref[pl.ds(r, SUB), :]
            hs_c = hs_ref[pl.ds(r, SUB), :]
            hs_prev = jnp.where(i == 0, hs_before, hs_ref[pl.ds(rp, SUB), :])
            h_m1 = _down(hs_c, hs_prev, 1, row_r)
            ra = ra_ref[pl.ds(r, SUB), :]
            ii = ii_ref[pl.ds(r, SUB), :]
            mult = mult_ref[pl.ds(r, SUB), :]
            a = jnp.exp(-ra * sp_c)
            inv_mult = lax.rsqrt(mult * mult)
            ge, dge = _gelu(u_ref[pl.ds(r, SUB), OFF_G:OFF_G + LRU_WIDTH])
            y_r = hs_c * ge
            rr = _rms(y_r)
            yhat = y_r * rr
            dyn = dy_s[pl.ds(r, SUB), CONV_WIDTH:MIX_WIDTH]
            acc[ACC_GNR] += dyn * yhat
            dy_r = _rms_bwd(dyn, yhat, rr, g_r)
            du_s[pl.ds(r, SUB), OFF_G:OFF_G + LRU_WIDTH] = dy_r * hs_c * dge
            a_cum, d_cum = _scan8_rev(_up(a, a_later, 1, row_r), dy_r * ge, row_r)
            dh = a_cum * dh_later + d_cum
            dm = dh * mult
            dii = dm * xr
            dxr_s[pl.ds(r, SUB), :] = dm * ii
            dla = a * dh * (h_m1 - (ii * xr) * a * inv_mult)
            dla_r = dla * ra
            acc[ACC_SP] -= dla_r
            dpa = dla_r * (sp_c * (ra - 1.0))
            dpx = dii * ii * (1.0 - ii)
            acc[ACC_BA] += dpa
            acc[ACC_BX] += dpx
            dpa_s[pl.ds(r, SUB), :] = dpa
            dpx_s[pl.ds(r, SUB), :] = dpx
            return a, dh[0:1, :]

        a_first, dh_first = _chunk_loop(n_chunks, recur_bwd, (a_car[...], dh_car[...]))
        a_car[...] = a_first
        dh_car[...] = dh_first

        dpab = dpa_s[...].astype(BF16)
        dpxb = dpx_s[...].astype(BF16)
        dxr_s[...] += _block_diag_apply_t(dpab, wabd) + _block_diag_apply_t(dpxb, wxbd)
        for g in range(LRU_WIDTH // GROUP):
            cols = slice(g * GROUP, (g + 1) * GROUP)
            dwa_acc[cols, :] += _dot_tn(xrb[:, cols], dpab[:, cols])
            dwx_acc[cols, :] += _dot_tn(xrb[:, cols], dpxb[:, cols])

        def convs_bwd(j, carry):
            dcq_later, dxr_later = carry
            i = n_chunks - 1 - j
            r = pl.multiple_of(i * SUB, SUB)
            rp = pl.multiple_of(jnp.maximum(i - 1, 0) * SUB, SUB)
            cv_prev = jnp.where(i == 0, cv_before,
                                u_ref[pl.ds(rp, SUB), OFF_GC:OFF_GC + CONV_WIDTH]
                                * u_ref[pl.ds(rp, SUB), OFF_V:OFF_V + CONV_WIDTH])
            gb, gc, v, cv, cv_m1, cv_m2, cq = _conv3_chunk(u_ref, r, cv_prev, cw, row_c)
            y_c = gb * cq
            rc = _rms(y_c)
            yhat = y_c * rc
            dyn = dy_s[pl.ds(r, SUB), 0:CONV_WIDTH]
            acc[ACC_GNC, :, 0:CONV_WIDTH] += dyn * yhat
            dy_c = _rms_bwd(dyn, yhat, rc, g_c)
            dcq = dy_c * gb
            dcv = (cw[2:3, :] * dcq + cw[1:2, :] * _up(dcq, dcq_later, 1, row_c)
                   + cw[0:1, :] * _up(dcq, dcq_later, 2, row_c))
            acc[ACC_CW + 2, :, 0:CONV_WIDTH] += dcq * cv
            acc[ACC_CW + 1, :, 0:CONV_WIDTH] += dcq * cv_m1
            acc[ACC_CW + 0, :, 0:CONV_WIDTH] += dcq * cv_m2
            du_s[pl.ds(r, SUB), OFF_GB:OFF_GB + CONV_WIDTH] = dy_c * cq
            du_s[pl.ds(r, SUB), OFF_GC:OFF_GC + CONV_WIDTH] = dcv * v
            du_s[pl.ds(r, SUB), OFF_V:OFF_V + CONV_WIDTH] = dcv * gc

            xin_prev = jnp.where(i == 0, xin_before, u_ref[pl.ds(rp, SUB), OFF_XR:OFF_XR + LRU_WIDTH])
            xin, m1, m2, m3, _ = _conv4_chunk(u_ref, r, xin_prev, rw, rb, row_r)
            dxr = dxr_s[pl.ds(r, SUB), :]
            du_s[pl.ds(r, SUB), OFF_XR:OFF_XR + LRU_WIDTH] = (
                rw[3:4, :] * dxr + rw[2:3, :] * _up(dxr, dxr_later, 1, row_r)
                + rw[1:2, :] * _up(dxr, dxr_later, 2, row_r) + rw[0:1, :] * _up(dxr, dxr_later, 3, row_r))
            acc[ACC_RW + 3] += dxr * xin
            acc[ACC_RW + 2] += dxr * m1
            acc[ACC_RW + 1] += dxr * m2
            acc[ACC_RW + 0] += dxr * m3
            acc[ACC_BR] += dxr
            return dcq, dxr

        dcq_first, dxr_first = _chunk_loop(n_chunks, convs_bwd, (dcq_car[...], dxr_car[...]))
        dcq_car[...] = dcq_first
        dxr_car[...] = dxr_first

        du_ref[...] = du_s[...].astype(BF16)

        @pl.when(step == n_tiles - 1)
        def _():
            vec_ref[...] = jnp.zeros(vec_ref.shape, F32)
            rows = {ACC_GNC: ROW_GNC, ACC_GNR: ROW_GNR, ACC_BR: ROW_BR, ACC_BA: ROW_BA, ACC_BX: ROW_BX}
            for k in range(3):
                rows[ACC_CW + k] = ROW_CW + k
            for k in range(4):
                rows[ACC_RW + k] = ROW_RW + k
            for slot, out_row in rows.items():
                o = out_row - ROW_GNC
                vec_ref[o:o + 1, :] = jnp.sum(acc[slot], axis=0, keepdims=True)
            lam_v = lam_ref[...]
            dsp = jnp.sum(acc[ACC_SP], axis=0, keepdims=True)
            o = ROW_LAM - ROW_GNC
            vec_ref[o:o + 1, :] = -dsp * LRU_C / (1.0 + jnp.exp(lam_v))
            wab_ref[0:LRU_WIDTH, :] = _fold_heads(dwa_acc[...])
            wab_ref[LRU_WIDTH:2 * LRU_WIDTH, :] = _fold_heads(dwx_acc[...])

    rev = lambda w: pl.BlockSpec((tm, w), lambda s: (n_tiles - 1 - s, 0))
    before = lambda w: pl.BlockSpec((SUB, w), lambda s: (jnp.maximum((n_tiles - 1 - s) * per_tile - 1, 0), 0))
    whole = lambda a: pl.BlockSpec(a.shape, lambda s: (0,) * a.ndim)
    smalls = (conv_w, rnn_conv_w, rnn_conv_b, wa, wx, lam, gnc, gnr, w_out)
    full = lambda w: pltpu.VMEM((tm, w), F32)
    return pl.pallas_call(
        body, grid=(n_tiles,),
        in_specs=[rev(IN_COLS), before(IN_COLS), rev(LRU_WIDTH), before(LRU_WIDTH), rev(D_MODEL)]
        + [rev(LRU_WIDTH)] * len(saved) + [whole(a) for a in smalls] + [HBM_SPEC] * (n_sums + 1),
        out_specs=[rev(IN_COLS), pl.BlockSpec((16, D_MODEL), lambda s: (0, 0)),
                   pl.BlockSpec((2 * LRU_WIDTH, HEAD_DIM), lambda s: (0, 0))] + [HBM_SPEC] * (n_sums + 1),
        out_shape=[jax.ShapeDtypeStruct((t_len, IN_COLS), BF16), jax.ShapeDtypeStruct((16, D_MODEL), F32),
                   jax.ShapeDtypeStruct((2 * LRU_WIDTH, HEAD_DIM), F32)]
        + [jax.ShapeDtypeStruct(s.shape, BF16) for s in chip_sums]
        + [jax.ShapeDtypeStruct((4,) + g_wout.shape[1:], BF16)],
        scratch_shapes=[full(IN_COLS), full(MIX_WIDTH), full(LRU_WIDTH), full(LRU_WIDTH), full(LRU_WIDTH),
                        pltpu.VMEM((LRU_WIDTH, GROUP), BF16), pltpu.VMEM((LRU_WIDTH, GROUP), BF16),
                        pltpu.VMEM((N_ACC, SUB, LRU_WIDTH), F32),
                        pltpu.VMEM((LRU_WIDTH, GROUP), F32), pltpu.VMEM((LRU_WIDTH, GROUP), F32),
                        pltpu.VMEM((SUB, LRU_WIDTH), F32), pltpu.VMEM((1, LRU_WIDTH), F32),
                        pltpu.VMEM((SUB, CONV_WIDTH), F32), pltpu.VMEM((SUB, LRU_WIDTH), F32)]
        + _exchange_scratch(n_sums, 3) + _exchange_scratch(1, 4),
        compiler_params=_params(("arbitrary",), 56), name="mixer_bwd",
    )(u, u, hs, hs, dx1, *saved, *smalls, *chip_sums, g_wout)


def _in_proj_bwd(du, dx1, x, g_mix, win_t, tm, chip_sums, g_own):
    t_len = x.shape[0]
    n_steps = t_len // tm

    def body(du_ref, dx1_ref, x_ref, g_ref, w_ref, hs_ref, gown_ref,
             dx_ref, vec_ref, landed_ref, sib_ref, i_send, i_recv, d_send, d_recv):
        step = pl.program_id(0)
        _host_chip_exchange(step, n_steps, [hs_ref], [landed_ref], i_send, i_recv)
        _host_half_exchange(step, n_steps, gown_ref, sib_ref, d_send, d_recv)

        @pl.when(step == 0)
        def _():
            vec_ref[...] = jnp.zeros(vec_ref.shape, F32)

        dh = jnp.dot(du_ref[...], w_ref[...], preferred_element_type=F32)
        xv = x_ref[...]
        r1 = _rms(xv)
        xh = xv * r1
        vec_ref[0:1, :] += jnp.sum(dh * xh, axis=0, keepdims=True)
        dx_ref[...] = dx1_ref[...] + _rms_bwd(dh, xh, r1, g_ref[...])

    row_tile = lambda w: pl.BlockSpec((tm, w), lambda i: (i, 0))
    half_shape = (g_own.shape[0], g_own.shape[1] // 2, g_own.shape[2])
    return pl.pallas_call(
        body, grid=(n_steps,),
        in_specs=[row_tile(IN_COLS), row_tile(D_MODEL), row_tile(D_MODEL), pl.BlockSpec((1, D_MODEL), lambda i: (0, 0)),
                  pl.BlockSpec((IN_COLS, D_MODEL), lambda i: (0, 0))] + [HBM_SPEC] * 2,
        out_specs=[row_tile(D_MODEL), pl.BlockSpec((SUB, D_MODEL), lambda i: (0, 0))] + [HBM_SPEC] * 2,
        out_shape=[jax.ShapeDtypeStruct((t_len, D_MODEL), F32), jax.ShapeDtypeStruct((SUB, D_MODEL), F32),
                   jax.ShapeDtypeStruct(chip_sums.shape, BF16), jax.ShapeDtypeStruct(half_shape, BF16)],
        scratch_shapes=_exchange_scratch(1, 3) + [pltpu.SemaphoreType.DMA((1,)), pltpu.SemaphoreType.DMA((1,))],
        compiler_params=_params(("arbitrary",), 56), name="in_proj_bwd",
    )(du, dx1, x, g_mix, win_t, chip_sums, g_own)


def _tn_weight_grad(a, b, tk, name, pair=(), col_blocks=1):
    t_len, m = a.shape
    n = b.shape[1]
    n_steps = t_len // tk
    sent = tuple(pair)
    n_sent = len(sent)

    def body(a_ref, b_ref, *rest):
        srcs = rest[0:n_sent]
        o_ref = rest[n_sent]
        dsts = rest[n_sent + 1:2 * n_sent + 1]
        acc = rest[2 * n_sent + 1]
        sems = rest[2 * n_sent + 2:]
        j = pl.program_id(0)
        if pair:
            _host_pair_exchange(j, n_steps, srcs, dsts, *sems)

        @pl.when(j == 0)
        def _():
            acc[...] = jnp.zeros(acc.shape, F32)

        acc[...] += _dot_tn(a_ref[...].astype(BF16), b_ref[...].astype(BF16))

        @pl.when(j == n_steps - 1)
        def _():
            if col_blocks == 1:
                o_ref[...] = acc[...].astype(BF16)
            else:
                for k in range(col_blocks):
                    o_ref[k] = acc[:, k * nb:(k + 1) * nb].astype(BF16)

    nb = n // col_blocks
    out_dims = (m, n) if col_blocks == 1 else (col_blocks, m, nb)
    landed = [jax.ShapeDtypeStruct((4,) + g.shape[1:], BF16) for g in pair]
    scratch = [pltpu.VMEM((m, n), F32)]
    if n_sent:
        scratch += _exchange_scratch(n_sent, 4)
    return pl.pallas_call(
        body, grid=(n_steps,),
        in_specs=[pl.BlockSpec((tk, m), lambda j: (j, 0)), pl.BlockSpec((tk, n), lambda j: (j, 0))]
        + [HBM_SPEC] * n_sent,
        out_specs=[pl.BlockSpec(out_dims, lambda j: (0,) * len(out_dims))] + [HBM_SPEC] * n_sent,
        out_shape=[jax.ShapeDtypeStruct(out_dims, BF16)] + landed,
        scratch_shapes=scratch,
        compiler_params=_params(("arbitrary",), 56), name=name,
    )(a, b, *sent)


def _w_in_grad_part(du, h, tk, name, chip_ids, chip=(), halves=None, small=None):
    t_len = du.shape[0]
    n_t = t_len // tk
    n_q = chip_ids.shape[0]
    width = 2 * (IN_COLS // N_DEV)
    n_steps = n_q * n_t
    n_chip = len(chip)
    sent = tuple(chip) + (() if halves is None else (halves,)) + (() if small is None else tuple(small))
    n_sent = len(sent)

    def body(ids_ref, a_ref, b_ref, *rest):
        srcs = rest[0:n_sent]
        o_ref = rest[n_sent]
        dsts = rest[n_sent + 1:2 * n_sent + 1]
        acc = rest[2 * n_sent + 1]
        sems = list(rest[2 * n_sent + 2:])
        j = pl.program_id(1)
        step = pl.program_id(0) * n_t + j
        if chip:
            _host_chip_exchange(step, n_steps, srcs[0:n_chip], dsts[0:n_chip], sems.pop(0), sems.pop(0))
        if halves is not None:
            _host_half_exchange(step, n_steps, srcs[n_chip], dsts[n_chip], sems.pop(0), sems.pop(0))
        if small is not None:
            _host_small_exchange(step, n_steps, *srcs[n_sent - 3:], *dsts[n_sent - 3:], *sems)

        @pl.when(j == 0)
        def _():
            acc[...] = jnp.zeros(acc.shape, F32)

        acc[...] += _dot_tn(a_ref[...], b_ref[...])

        @pl.when(j == n_t - 1)
        def _():
            o_ref[0] = acc[...].astype(BF16)

    landed = [jax.ShapeDtypeStruct(s.shape, BF16) for s in chip]
    scratch = [pltpu.VMEM((width, D_MODEL), F32)]
    if chip:
        scratch += _exchange_scratch(len(chip), 3)
    if halves is not None:
        landed.append(jax.ShapeDtypeStruct((halves.shape[0], halves.shape[1] // 2, halves.shape[2]), BF16))
        scratch += [pltpu.SemaphoreType.DMA((halves.shape[0],)), pltpu.SemaphoreType.DMA((halves.shape[0],))]
    if small is not None:
        vec_m, vec_b, wab = small
        landed += [jax.ShapeDtypeStruct((N_DEV,) + vec_m.shape, F32), jax.ShapeDtypeStruct((N_DEV,) + vec_b.shape, F32),
                   jax.ShapeDtypeStruct((N_DEV, wab.shape[0] // N_DEV, wab.shape[1]), F32)]
        scratch += _exchange_scratch(3, N_DEV) + [pltpu.SemaphoreType.DMA((2,))]
    grid_spec = pltpu.PrefetchScalarGridSpec(
        num_scalar_prefetch=1, grid=(n_q, n_t),
        in_specs=[pl.BlockSpec((tk, width), lambda q, j, ids: (j, ids[q])),
                  pl.BlockSpec((tk, D_MODEL), lambda q, j, ids: (j, 0))] + [HBM_SPEC] * n_sent,
        out_specs=[pl.BlockSpec((1, width, D_MODEL), lambda q, j, ids: (q, 0, 0))] + [HBM_SPEC] * n_sent,
        scratch_shapes=scratch)
    return pl.pallas_call(
        body, grid_spec=grid_spec, out_shape=[jax.ShapeDtypeStruct((n_q, width, D_MODEL), BF16)] + landed,
        compiler_params=_params(("arbitrary", "arbitrary"), 40), name=name,
    )(chip_ids, du, h, *sent)


def _adamw(w, g, m, v):
    m = ADAM_B1 * m + (1.0 - ADAM_B1) * g
    v = ADAM_B2 * v + (1.0 - ADAM_B2) * (g * g)
    delta = -ADAM_LR * ((m / BC1) / (jnp.sqrt(v / BC2) + ADAM_EPS) + ADAM_WD * w)
    return delta, m, v


def _update_sharded(g, landed, w, m, v, rows_blk, name):
    rows, cols = w.shape

    def body(g_ref, l_ref, w_ref, m_ref, v_ref, og, od, om, ov):
        gv = g_ref[...]
        for j in range(3):
            gv = gv + l_ref[j].astype(F32)
        delta, mn, vn = _adamw(w_ref[...], gv, m_ref[...], v_ref[...])
        og[...] = gv
        od[...] = delta
        om[...] = mn
        ov[...] = vn

    blk = pl.BlockSpec((rows_blk, cols), lambda i: (i, 0))
    shape = pltpu.HBM((rows, cols), F32)
    return pl.pallas_call(
        body, grid=(rows // rows_blk,),
        in_specs=[blk, pl.BlockSpec((3, rows_blk, cols), lambda i: (0, i, 0)), blk, blk, blk],
        out_specs=[blk] * 4, out_shape=[shape] * 4,
        compiler_params=_params(("arbitrary",), 32), name=name,
    )(*_in_hbm(g, landed, w, m, v))


def _update_w_in(g_own, sib_own, landed, w_t, m_t, v_t, core, cols_blk):
    rows, cols = w_t.shape

    def body(core_ref, g_ref, s_ref, l_ref, w_ref, m_ref, v_ref, og, od, om, ov):
        gv = g_ref[0, 0].astype(F32) + s_ref[0].astype(F32)
        for j in range(3):
            gv = gv + l_ref[j].astype(F32)
        delta, mn, vn = _adamw(w_ref[...], gv, m_ref[...], v_ref[...])
        og[...] = gv
        od[...] = delta
        om[...] = mn
        ov[...] = vn

    blk = pl.BlockSpec((rows, cols_blk), lambda i, cr: (0, i))
    grid_spec = pltpu.PrefetchScalarGridSpec(
        num_scalar_prefetch=1, grid=(cols // cols_blk,),
        in_specs=[pl.BlockSpec((1, 1, rows, cols_blk), lambda i, cr: (0, cr[0], 0, i)),
                  pl.BlockSpec((1, rows, cols_blk), lambda i, cr: (0, 0, i)),
                  pl.BlockSpec((3, rows, cols_blk), lambda i, cr: (0, 0, i)), blk, blk, blk],
        out_specs=[blk] * 4)
    return pl.pallas_call(
        body, grid_spec=grid_spec, out_shape=[pltpu.HBM((rows, cols), F32)] * 4,
        compiler_params=_params(("arbitrary",), 32), name="update_w_in",
    )(core, *_in_hbm(g_own.reshape(1, 2, rows, cols), sib_own, landed, w_t, m_t, v_t))


def _update_small(vsum, wsum, g_cw, g_rw, weights, moments_m, moments_v):
    n = len(weights)

    def body(*refs):
        vs, ws, gcw, grw = refs[0:4]
        w_refs = refs[4:4 + n]
        m_refs = refs[4 + n:4 + 2 * n]
        v_refs = refs[4 + 2 * n:4 + 3 * n]
        outs = refs[4 + 3 * n:]
        loss_ref = outs[0]
        loss_ref[...] = jnp.sum(vs[ROW_LOSS:ROW_LOSS + 1, :], axis=1, keepdims=True)
        grads = [
            vs[ROW_GMIX:ROW_GMIX + 1, :], gcw[...], grw[...], vs[ROW_BR:ROW_BR + 1, :],
            ws[0:LRU_WIDTH, :], vs[ROW_BA:ROW_BA + 1, :], ws[LRU_WIDTH:2 * LRU_WIDTH, :], vs[ROW_BX:ROW_BX + 1, :],
            vs[ROW_LAM:ROW_LAM + 1, :], vs[ROW_GNC:ROW_GNC + 1, 0:CONV_WIDTH], vs[ROW_GNR:ROW_GNR + 1, :],
            vs[ROW_GMLP:ROW_GMLP + 1, :], vs[ROW_GF:ROW_GF + 1, :],
        ]
        for k in range(n):
            gk = grads[k]
            delta, mn, vn = _adamw(w_refs[k][...], gk, m_refs[k][...], v_refs[k][...])
            outs[1 + 4 * k][...] = gk
            outs[2 + 4 * k][...] = delta
            outs[3 + 4 * k][...] = mn
            outs[4 + 4 * k][...] = vn

    whole = lambda a: pl.BlockSpec(a.shape, lambda i: (0,) * len(a.shape))
    out_shape = [jax.ShapeDtypeStruct((1, 1), F32)]
    for w in weights:
        out_shape += [jax.ShapeDtypeStruct(w.shape, F32)] * 4
    args = (vsum, wsum, g_cw, g_rw, *weights, *moments_m, *moments_v)
    return pl.pallas_call(
        body, grid=(1,), out_shape=out_shape, in_specs=[whole(a) for a in args], out_specs=[whole(s) for s in out_shape],
        compiler_params=_params(("arbitrary",), 32), name="update_small",
    )(*args)


def kernel(x, norm_mix_g, w_in, conv_w, rnn_conv_w, rnn_conv_b, w_a, b_a, w_x, b_x, lru_lambda, g_norm_conv, g_norm_rnn, w_out, norm_mlp_g, w_mlp_in, w_mlp_out, final_norm_g, loss_target, m_norm_mix_g, m_w_in, m_conv_w, m_rnn_conv_w, m_rnn_conv_b, m_w_a, m_b_a, m_w_x, m_b_x, m_lru_lambda, m_g_norm_conv, m_g_norm_rnn, m_w_out, m_norm_mlp_g, m_w_mlp_in, m_w_mlp_out, m_final_norm_g, v_norm_mix_g, v_w_in, v_conv_w, v_rnn_conv_w, v_rnn_conv_b, v_w_a, v_b_a, v_w_x, v_b_x, v_lru_lambda, v_g_norm_conv, v_g_norm_rnn, v_w_out, v_norm_mlp_g, v_w_mlp_in, v_w_mlp_out, v_final_norm_g):
    t_len = x.shape[1]
    my_id = 4 * lax.axis_index("x") + 2 * lax.axis_index("y") + lax.axis_index("c")
    tm = min(256, t_len)
    tb = min(512, t_len)
    tk = min(512, t_len)

    xs = x.reshape(t_len, D_MODEL)
    tgt = loss_target.reshape(t_len, D_MODEL)
    flat = lambda a: a.reshape(a.shape[-2:]) if a.ndim == 3 else a.reshape(1, -1)
    heads = lambda a: a.reshape(LRU_WIDTH, HEAD_DIM)

    turned = lambda a: jnp.transpose(flat(a))
    win_shard, wout_shard, w1_shard, w2_shard, cp_shard = _prep_shards(
        turned(w_in), flat(w_out), flat(w_mlp_in), flat(w_mlp_out), flat(conv_w), flat(rnn_conv_w))

    u, h, win_t, cp_full = _in_proj(xs, flat(norm_mix_g), (win_shard, cp_shard), min(1024, t_len))
    cpack = cp_full.reshape(N_DEV, 8, 128)
    conv_full = jnp.transpose(cpack[:, 0:3, 0:64], (1, 0, 2)).reshape(3, CONV_WIDTH)
    rnn_full = jnp.transpose(cpack[:, 3:7, :], (1, 0, 2)).reshape(4, LRU_WIDTH)
    mixer_small = (conv_full, rnn_full, flat(rnn_conv_b), heads(w_a), flat(b_a), heads(w_x), flat(b_x),
                   flat(lru_lambda), flat(g_norm_conv), flat(g_norm_rnn))
    hs, y, xr, gate_r, gate_i, mult, w1_blk, wout_blk = _mixer_fwd(u, *mixer_small, (w1_shard, wout_shard), tm)
    wout_f = wout_blk.reshape(MIX_WIDTH, D_MODEL)
    x1, h2, z, w2_blk = _mlp_up(xs, y, flat(norm_mlp_g), wout_f, w1_blk, w2_shard, tb)
    dx1, dx2, vec_m, dpre = _mlp_down_bwd(x1, z, tgt, flat(norm_mlp_g), flat(final_norm_g), w1_blk,
                                          w2_blk.reshape(D_FF, D_MODEL), tb)
    (g_w1,) = _tn_weight_grad(h2, dpre, tk, "w_mlp_in_grad", col_blocks=N_DEV)
    (g_w2,) = _tn_weight_grad(z, dx2, tk, "w_mlp_out_grad")
    g_w2 = g_w2.reshape(N_DEV, D_FF // N_DEV, D_MODEL)
    g_wout, sib_w1, sib_w2 = _tn_weight_grad(y, dx1, tk, "w_out_grad", pair=(g_w1, g_w2))
    g_wout = g_wout.reshape(N_DEV, MIX_WIDTH // N_DEV, D_MODEL)
    hsend_w1, own_w1, hsend_w2, own_w2 = _pair_sum((g_w1, g_w2), (sib_w1, sib_w2), "pair_sum_w_mlp")
    du, vec_b, wab, landed_w1, landed_w2, sib_wout = _mixer_bwd(
        u, hs, dx1, (xr, gate_r, gate_i, mult), conv_full, rnn_full, flat(rnn_conv_b), heads(w_a), heads(w_x),
        flat(lru_lambda), flat(g_norm_conv), flat(g_norm_rnn), wout_f, (hsend_w1, hsend_w2), g_wout, tm)
    hsend_wout, own_wout = _pair_sum((g_wout,), (sib_wout,), "pair_sum_w_out")
    ax, ay, ac = lax.axis_index("x"), lax.axis_index("y"), lax.axis_index("c")
    chip_ids = jnp.stack([2 * cx + cy for cx, cy in [(ax, ay)] + _other_chips(ax, ay)]).astype(jnp.int32)
    core = jnp.reshape(ac, (1,)).astype(jnp.int32)
    tw = min(1024, t_len)
    g_others, landed_wout, vrecv_m, vrecv_b, wrecv = _w_in_grad_part(
        du, h, tw, "w_in_grad_others", chip_ids[1:4], chip=(hsend_wout,), small=(vec_m, vec_b, wab))
    g_own, sib_others = _w_in_grad_part(du, h, tw, "w_in_grad_own", chip_ids[0:1], halves=g_others)
    hsend_win = _pair_sum_parts(g_others, sib_others, core)
    grad_x, vec_x, landed_win, sib_own = _in_proj_bwd(du, dx1, xs, flat(norm_mix_g), win_t, tm, hsend_win, g_own)

    vsum, wsum = _final_small(vrecv_m, vrecv_b, wab, wrecv, vec_x)

    up_win = _update_w_in(g_own, sib_own, landed_win, turned(w_in), turned(m_w_in), turned(v_w_in), core, 256)
    up_win = [jnp.transpose(a) for a in up_win]
    up_wout = _update_sharded(own_wout, landed_wout, flat(w_out), flat(m_w_out), flat(v_w_out), 96, "update_w_out")
    up_w1 = _update_sharded(own_w1, landed_w1, flat(w_mlp_in), flat(m_w_mlp_in), flat(v_w_mlp_in), 256,
                            "update_w_mlp_in")
    up_w2 = _update_sharded(own_w2, landed_w2, flat(w_mlp_out), flat(m_w_mlp_out), flat(v_w_mlp_out), 256,
                            "update_w_mlp_out")

    g_cw = lax.dynamic_slice(vsum, (ROW_CW, 64 * my_id), (3, 64))
    g_rw = lax.dynamic_slice(vsum, (ROW_RW, 128 * my_id), (4, 128))
    small_w = (norm_mix_g, conv_w, rnn_conv_w, rnn_conv_b, w_a, b_a, w_x, b_x, lru_lambda, g_norm_conv, g_norm_rnn,
               norm_mlp_g, final_norm_g)
    small_m = (m_norm_mix_g, m_conv_w, m_rnn_conv_w, m_rnn_conv_b, m_w_a, m_b_a, m_w_x, m_b_x, m_lru_lambda,
               m_g_norm_conv, m_g_norm_rnn, m_norm_mlp_g, m_final_norm_g)
    small_v = (v_norm_mix_g, v_conv_w, v_rnn_conv_w, v_rnn_conv_b, v_w_a, v_b_a, v_w_x, v_b_x, v_lru_lambda,
               v_g_norm_conv, v_g_norm_rnn, v_norm_mlp_g, v_final_norm_g)
    is_heads = (False, False, False, False, True, False, True, False, False, False, False, False, False)
    as2d = lambda arrs: [heads(a) if hd else flat(a) for a, hd in zip(arrs, is_heads)]
    small_out = _update_small(vsum, wsum, g_cw, g_rw, as2d(small_w), as2d(small_m), as2d(small_v))
    loss = small_out[0].reshape(())

    names = ["norm_mix_g", "w_in", "conv_w", "rnn_conv_w", "rnn_conv_b", "w_a", "b_a", "w_x", "b_x", "lru_lambda",
             "g_norm_conv", "g_norm_rnn", "w_out", "norm_mlp_g", "w_mlp_in", "w_mlp_out", "final_norm_g"]
    originals = dict(zip(names, (norm_mix_g, w_in, conv_w, rnn_conv_w, rnn_conv_b, w_a, b_a, w_x, b_x, lru_lambda,
                                 g_norm_conv, g_norm_rnn, w_out, norm_mlp_g, w_mlp_in, w_mlp_out, final_norm_g)))
    results = {"w_in": up_win, "w_out": up_wout, "w_mlp_in": up_w1, "w_mlp_out": up_w2}
    small_names = ["norm_mix_g", "conv_w", "rnn_conv_w", "rnn_conv_b", "w_a", "b_a", "w_x", "b_x", "lru_lambda",
                   "g_norm_conv", "g_norm_rnn", "norm_mlp_g", "final_norm_g"]
    for k, nm in enumerate(small_names):
        results[nm] = small_out[1 + 4 * k:5 + 4 * k]
    out = [loss, grad_x.reshape(x.shape)]
    for kind in range(4):
        out += [results[nm][kind].reshape(originals[nm].shape) for nm in names]
    return tuple(out)
```

```python
import functools

import jax
import jax.numpy as jnp
from jax import lax
from jax.experimental import pallas as pl
from jax.experimental.pallas import tpu as pltpu

F32 = jnp.float32
BF16 = jnp.bfloat16

D_MODEL = 1024
HEAD_DIM = 64
CONV_WIDTH = 512
LRU_WIDTH = 1024
MIX_WIDTH = CONV_WIDTH + LRU_WIDTH
IN_COLS = 3 * CONV_WIDTH + 2 * LRU_WIDTH
D_FF = 4 * D_MODEL
GROUP = 256
EPS = 1e-6
LRU_C = 8.0
N_DEV = 8
SUB = 8

OFF_GB, OFF_GC, OFF_V, OFF_XR, OFF_G = 0, 512, 1024, 1536, 2560

ADAM_LR, ADAM_B1, ADAM_B2, ADAM_EPS, ADAM_WD, ADAM_STEP = 0.001, 0.9, 0.999, 1e-08, 0.01, 10
BC1 = 1.0 - ADAM_B1 ** ADAM_STEP
BC2 = 1.0 - ADAM_B2 ** ADAM_STEP

MIB = 1024 * 1024
MESH = pl.DeviceIdType.MESH

VEC_ROWS = 32
ROW_GF, ROW_GMLP, ROW_LOSS = 0, 1, 2
ROW_GNC, ROW_GNR, ROW_BR, ROW_BA, ROW_BX, ROW_LAM, ROW_CW, ROW_RW = 8, 9, 10, 11, 12, 13, 14, 17
ROW_GMIX = 24
ACC_GNC, ACC_GNR, ACC_BR, ACC_BA, ACC_BX, ACC_SP, ACC_CW, ACC_RW, N_ACC = 0, 1, 2, 3, 4, 5, 6, 9, 13


def _params(semantics=None, vmem_mib=48):
    return pltpu.CompilerParams(dimension_semantics=semantics, vmem_limit_bytes=vmem_mib * MIB)


def _rms(x):
    return lax.rsqrt(jnp.mean(x * x, axis=-1, keepdims=True) + EPS)


def _rms_bwd(dy, xhat, r, g):
    dyh = dy * g
    return r * (dyh - xhat * jnp.mean(dyh * xhat, axis=-1, keepdims=True))


def _sigmoid(x):
    return 0.5 + 0.5 * jnp.tanh(0.5 * x)


def _gelu(x):
    c0, c1 = 0.7978845608028654, 0.044715
    x2 = x * x
    t = jnp.tanh(x * (c0 + (c0 * c1) * x2))
    half = 0.5 + 0.5 * t
    ge = x * half
    dge = half + (ge - ge * half) * (2.0 * c0 + (6.0 * c0 * c1) * x2)
    return ge, dge


def _softplus_neg(lam):
    z = -lam
    e = jnp.exp(-jnp.abs(z))
    return jnp.maximum(z, 0.0) + jnp.where(e < 1e-4, e * (1.0 - 0.5 * e), jnp.log(1.0 + e))


def _lru_gates(pa, px, sp_c):
    ra = _sigmoid(pa)
    ii = _sigmoid(px)
    la = -ra * sp_c
    a = jnp.exp(la)
    x2 = 2.0 * la
    series = -x2 * (1.0 + x2 * (0.5 + x2 * (1.0 / 6.0 + x2 * (1.0 / 24.0))))
    m2 = jnp.where(x2 > -0.01, series, 1.0 - a * a)
    mult = jnp.where(m2 > 0.0, m2 * lax.rsqrt(m2), 0.0)
    return ra, ii, a, mult


def _down(cur, prev, s, row):
    return pltpu.roll(jnp.where(row < SUB - s, cur, prev), s, 0)


def _up(cur, nxt, s, row):
    return pltpu.roll(jnp.where(row >= s, cur, nxt), SUB - s, 0)


def _scan8_fwd(a, b, row):
    for s in (1, 2, 4):
        m = row >= s
        a_sh = pltpu.roll(a, s, 0)
        b_sh = pltpu.roll(b, s, 0)
        b = jnp.where(m, a * b_sh + b, b)
        a = jnp.where(m, a * a_sh, a)
    return a, b


def _scan8_rev(a, b, row):
    for s in (1, 2, 4):
        m = row < SUB - s
        a_sh = pltpu.roll(a, SUB - s, 0)
        b_sh = pltpu.roll(b, SUB - s, 0)
        b = jnp.where(m, a * b_sh + b, b)
        a = jnp.where(m, a * a_sh, a)
    return a, b


def _group_mask(shape):
    r = lax.broadcasted_iota(jnp.int32, shape, 0)
    c = lax.broadcasted_iota(jnp.int32, shape, 1)
    return ((r % GROUP) // HEAD_DIM) == (c // HEAD_DIM)


def _expand_heads(w):
    j = lax.broadcasted_iota(jnp.int32, (HEAD_DIM, GROUP), 0)
    c = lax.broadcasted_iota(jnp.int32, (HEAD_DIM, GROUP), 1)
    spread = (c % HEAD_DIM == j).astype(BF16)
    e = jnp.dot(w.astype(BF16), spread, preferred_element_type=F32)
    return jnp.where(_group_mask(e.shape), e, 0.0).astype(BF16)


def _fold_heads(p):
    p = jnp.where(_group_mask(p.shape), p, 0.0)
    c = lax.broadcasted_iota(jnp.int32, (GROUP, HEAD_DIM), 0)
    j = lax.broadcasted_iota(jnp.int32, (GROUP, HEAD_DIM), 1)
    fold = (c % HEAD_DIM == j).astype(BF16)
    hi = p.astype(BF16)
    rest = p - hi.astype(F32)
    mid = rest.astype(BF16)
    lo = (rest - mid.astype(F32)).astype(BF16)
    dot = functools.partial(jnp.dot, preferred_element_type=F32)
    return dot(hi, fold) + dot(mid, fold) + dot(lo, fold)


def _block_diag_apply(xb, wbd_ref):
    parts = [jnp.dot(xb[:, g * GROUP:(g + 1) * GROUP], wbd_ref[g * GROUP:(g + 1) * GROUP, :],
                     preferred_element_type=F32) for g in range(LRU_WIDTH // GROUP)]
    return jnp.concatenate(parts, axis=1)


def _block_diag_apply_t(db, wbd_ref):
    parts = [lax.dot_general(db[:, g * GROUP:(g + 1) * GROUP], wbd_ref[g * GROUP:(g + 1) * GROUP, :],
                             (((1,), (1,)), ((), ())), preferred_element_type=F32)
             for g in range(LRU_WIDTH // GROUP)]
    return jnp.concatenate(parts, axis=1)


def _dot_nt(a, b):
    return lax.dot_general(a, b, (((1,), (1,)), ((), ())), preferred_element_type=F32)


def _dot_tn(a, b):
    return lax.dot_general(a, b, (((0,), (0,)), ((), ())), preferred_element_type=F32)


CHUNKS_IN_FLIGHT = 8


def _chunk_loop(n_chunks, chunk, init):
    def body(k, carry):
        for j in range(CHUNKS_IN_FLIGHT):
            carry = chunk(k * CHUNKS_IN_FLIGHT + j, carry)
        return carry

    return lax.fori_loop(0, n_chunks // CHUNKS_IN_FLIGHT, body, init)


def _place():
    x, y, c = lax.axis_index("x"), lax.axis_index("y"), lax.axis_index("c")
    return x, y, c


def _block_id(chip, core):
    return 4 * chip[0] + 2 * chip[1] + core


def _other_chips(x, y):
    return [(1 - x, y), (x, 1 - y), (1 - x, 1 - y)]


def _remote_copy(src, dst, send_sem, recv_sem, to):
    return pltpu.make_async_remote_copy(src_ref=src, dst_ref=dst, send_sem=send_sem, recv_sem=recv_sem,
                                        device_id=to, device_id_type=MESH)


HBM_SPEC = pl.BlockSpec(memory_space=pl.ANY)


def _in_hbm(*arrays):
    return [pltpu.with_memory_space_constraint(a, pltpu.HBM) for a in arrays]


def _prep_shards(w_in_t, w_out, w_mlp_in, w_mlp_out, conv_w, rnn_conv_w):
    def body(win_ref, wout_ref, w1_ref, w2_ref, cw_ref, rw_ref, o_win, o_wout, o_w1, o_w2, o_cp):
        o_win[...] = win_ref[...].astype(BF16)
        o_wout[...] = wout_ref[...].astype(BF16)
        o_w1[...] = w1_ref[...].astype(BF16)
        o_w2[...] = w2_ref[...].astype(BF16)
        o_cp[...] = jnp.zeros(o_cp.shape, F32)
        o_cp[0:3, 0:64] = cw_ref[...]
        o_cp[3:7, :] = rw_ref[...]

    whole = lambda shape: pl.BlockSpec(shape, lambda i: (0,) * len(shape))
    args = (w_in_t, w_out, w_mlp_in, w_mlp_out, conv_w, rnn_conv_w)
    shapes = [(w_in_t.shape, BF16), (w_out.shape, BF16), (w_mlp_in.shape, BF16), (w_mlp_out.shape, BF16),
              ((8, 128), F32)]
    return pl.pallas_call(
        body, grid=(1,), out_shape=[jax.ShapeDtypeStruct(s, d) for s, d in shapes],
        in_specs=[whole(a.shape) for a in args], out_specs=[whole(s) for s, _ in shapes],
        compiler_params=_params(("arbitrary",), 40), name="prep_shards",
    )(*args)


def _host_all_gather(step, n_steps, shards, fulls, send_sems, recv_sems, local_sems):
    x, y, c = _place()
    me = (x, y, c)
    my_id = _block_id((x, y), c)
    sibling = (x, y, 1 - c)
    chips = _other_chips(x, y)
    n_arr = len(shards)

    def copy(arr, k, block, to, src=None):
        dst = fulls[arr].at[block]
        return _remote_copy(dst if src is None else src, dst, send_sems.at[arr, k], recv_sems.at[arr, k], to)

    def local(arr):
        return pltpu.make_async_copy(shards[arr], fulls[arr].at[my_id], local_sems.at[arr])

    @pl.when(step == 0)
    def _():
        for arr in range(n_arr):
            local(arr).start()
            copy(arr, 0, my_id, sibling, shards[arr]).start()
            for j, chip in enumerate(chips):
                copy(arr, 1 + j, my_id, (*chip, c), shards[arr]).start()

    @pl.when(step == max(n_steps - 2, 0))
    def _():
        for j, chip in enumerate(chips):
            for arr in range(n_arr):
                copy(arr, 1 + j, _block_id(chip, c), me).wait_recv()
                copy(arr, 4 + j, _block_id(chip, c), sibling).start()

    @pl.when(step == n_steps - 1)
    def _():
        for arr in range(n_arr):
            copy(arr, 0, _block_id((x, y), 1 - c), me).wait_recv()
            for j, chip in enumerate(chips):
                copy(arr, 4 + j, _block_id(chip, 1 - c), me).wait_recv()
            for k in range(4):
                copy(arr, k, my_id, me, shards[arr]).wait_send()
            for j, chip in enumerate(chips):
                copy(arr, 4 + j, _block_id(chip, c), me).wait_send()
            local(arr).wait()


def _host_pair_exchange(step, n_steps, gs, sibs, send_sems, recv_sems):
    x, y, c = _place()
    sibling = (x, y, 1 - c)
    chips = [(x, y)] + _other_chips(x, y)

    def d2d(arr, q):
        return _remote_copy(gs[arr].at[_block_id(chips[q], 1 - c)], sibs[arr].at[q],
                            send_sems.at[arr, q], recv_sems.at[arr, q], sibling)

    @pl.when(step == 0)
    def _():
        for arr in range(len(gs)):
            for q in (1, 2, 3, 0):
                d2d(arr, q).start()

    @pl.when(step == n_steps - 1)
    def _():
        for arr in range(len(gs)):
            for q in range(4):
                d2d(arr, q).wait()


def _host_chip_exchange(step, n_steps, hsends, hrecvs, send_sems, recv_sems):
    x, y, c = _place()
    chips = _other_chips(x, y)

    def ici(arr, j):
        return _remote_copy(hsends[arr].at[j], hrecvs[arr].at[j], send_sems.at[arr, j], recv_sems.at[arr, j],
                            (*chips[j], c))

    @pl.when(step == 0)
    def _():
        for arr in range(len(hsends)):
            for j in range(3):
                ici(arr, j).start()

    @pl.when(step == n_steps - 1)
    def _():
        for arr in range(len(hsends)):
            for j in range(3):
                ici(arr, j).wait()


def _host_half_exchange(step, n_steps, parts, sibs, send_sems, recv_sems):
    x, y, c = _place()
    n_q, rows2, _ = parts.shape
    half = rows2 // 2

    def d2d(q):
        src = parts.at[q, pl.ds(pl.multiple_of((1 - c) * half, 16), half), :]
        return _remote_copy(src, sibs.at[q], send_sems.at[q], recv_sems.at[q], (x, y, 1 - c))

    @pl.when(step == 0)
    def _():
        for q in range(n_q):
            d2d(q).start()

    @pl.when(step == n_steps - 1)
    def _():
        for q in range(n_q):
            d2d(q).wait()


def _peer(x, y, c, k):
    return (x ^ ((k >> 2) & 1), y ^ ((k >> 1) & 1), c ^ (k & 1))


def _host_small_exchange(step, n_steps, vec_m, vec_b, wab, vrecv_m, vrecv_b, wrecv, send_sems, recv_sems, local_sems):
    x, y, c = _place()
    my_id = _block_id((x, y), c)
    wrows = wab.shape[0] // N_DEV

    def copies(k):
        to = _peer(x, y, c, k)
        block = wab.at[pl.ds(pl.multiple_of(_block_id(to[0:2], to[2]) * wrows, SUB), wrows), :]
        return [_remote_copy(vec_m, vrecv_m.at[my_id], send_sems.at[0, k], recv_sems.at[0, k], to),
                _remote_copy(vec_b, vrecv_b.at[my_id], send_sems.at[1, k], recv_sems.at[1, k], to),
                _remote_copy(block, wrecv.at[k], send_sems.at[2, k], recv_sems.at[2, k], to)]

    mine = [pltpu.make_async_copy(vec_m, vrecv_m.at[my_id], local_sems.at[0]),
            pltpu.make_async_copy(vec_b, vrecv_b.at[my_id], local_sems.at[1])]

    @pl.when(step == 0)
    def _():
        for cp in mine:
            cp.start()
        for k in range(1, N_DEV):
            for cp in copies(k):
                cp.start()

    @pl.when(step == n_steps - 1)
    def _():
        for k in range(1, N_DEV):
            for cp in copies(k):
                cp.wait()
        for cp in mine:
            cp.wait()


def _pair_sum_parts(parts, sibs, core):
    n_q, rows2, cols = parts.shape
    half = rows2 // 2

    def body(core_ref, g_ref, s_ref, o_ref):
        o_ref[0] = (g_ref[0, 0].astype(F32) + s_ref[0].astype(F32)).astype(BF16)

    block = (1, half, cols)
    grid_spec = pltpu.PrefetchScalarGridSpec(
        num_scalar_prefetch=1, grid=(n_q,),
        in_specs=[pl.BlockSpec((1, 1, half, cols), lambda q, cr: (q, cr[0], 0, 0)),
                  pl.BlockSpec(block, lambda q, cr: (q, 0, 0))],
        out_specs=pl.BlockSpec(block, lambda q, cr: (q, 0, 0)))
    return pl.pallas_call(
        body, grid_spec=grid_spec, out_shape=pltpu.HBM((n_q, half, cols), BF16),
        compiler_params=_params(("arbitrary",), 32), name="pair_sum_w_in",
    )(core, *_in_hbm(parts.reshape(n_q, 2, half, cols), sibs))


def _pair_sum(gs, sibs, name):
    n_arr = len(gs)
    x, y, c = _place()
    slots = jnp.stack([_block_id(chip, c) for chip in [(x, y)] + _other_chips(x, y)]).astype(jnp.int32)

    def body(slots_ref, *refs):
        q = pl.program_id(0)
        for k in range(n_arr):
            g_ref, sib_ref = refs[2 * k:2 * k + 2]
            hs_ref, own_ref = refs[2 * n_arr + 2 * k:2 * n_arr + 2 * k + 2]
            both = g_ref[0].astype(F32) + sib_ref[0].astype(F32)

            @pl.when(q == 0)
            def _(own_ref=own_ref, both=both):
                own_ref[...] = both

            @pl.when(q > 0)
            def _(hs_ref=hs_ref, both=both):
                hs_ref[0] = both.astype(BF16)

    in_specs, out_specs, out_shape, args = [], [], [], []
    for g, sib in zip(gs, sibs):
        _, rows, cols = g.shape
        block = (1, rows, cols)
        in_specs += [pl.BlockSpec(block, lambda q, s: (s[q], 0, 0)), pl.BlockSpec(block, lambda q, s: (q, 0, 0))]
        out_specs += [pl.BlockSpec(block, lambda q, s: (jnp.maximum(q - 1, 0), 0, 0)),
                      pl.BlockSpec((rows, cols), lambda q, s: (0, 0))]
        out_shape += [pltpu.HBM((3, rows, cols), BF16), pltpu.HBM((rows, cols), F32)]
        args += _in_hbm(g, sib)
    grid_spec = pltpu.PrefetchScalarGridSpec(num_scalar_prefetch=1, grid=(4,), in_specs=in_specs, out_specs=out_specs)
    return pl.pallas_call(
        body, grid_spec=grid_spec, out_shape=out_shape,
        compiler_params=_params(("arbitrary",), 40), name=name,
    )(slots, *args)


def _exchange_scratch(n_arr, n_copies):
    return [pltpu.SemaphoreType.DMA((n_arr, n_copies)), pltpu.SemaphoreType.DMA((n_arr, n_copies))]


def _final_small(vrecv_m, vrecv_b, wab, wrecv, vec_x):
    wrows = wab.shape[0] // N_DEV

    def body(vm_ref, vb_ref, w_ref, wr_ref, vx_ref, o_vec, o_w, xrecv, wred, x_send, x_recv, b_send, b_recv):
        x, y, c = _place()
        my_id = _block_id((x, y), c)
        my_rows = pl.ds(pl.multiple_of(my_id * wrows, SUB), wrows)

        def xcopy(k):
            return _remote_copy(vx_ref, xrecv.at[my_id], x_send.at[k], x_recv.at[k], _peer(x, y, c, k))

        def bcopy(k):
            return _remote_copy(wred, o_w.at[my_rows, :], b_send.at[k], b_recv.at[k], _peer(x, y, c, k))

        xrecv[my_id] = vx_ref[...]
        for k in range(1, N_DEV):
            xcopy(k).start()
        red = w_ref[my_rows, :]
        for k in range(1, N_DEV):
            red = red + wr_ref[k]
        wred[...] = red
        o_w[my_rows, :] = red
        for k in range(1, N_DEV):
            bcopy(k).start()
        for k in range(1, N_DEV):
            xcopy(k).wait_recv()
        for rows, ref in ((slice(0, 8), vm_ref), (slice(8, 24), vb_ref), (slice(24, 32), xrecv)):
            tot = ref[0]
            for s in range(1, N_DEV):
                tot = tot + ref[s]
            o_vec[rows, :] = tot
        for k in range(1, N_DEV):
            bcopy(k).wait_recv()
        for k in range(1, N_DEV):
            xcopy(k).wait_send()
            bcopy(k).wait_send()

    vm = pl.BlockSpec(memory_space=pltpu.VMEM)
    dma8 = pltpu.SemaphoreType.DMA((N_DEV,))
    return pl.pallas_call(
        body, out_shape=(jax.ShapeDtypeStruct((VEC_ROWS, D_MODEL), F32), jax.ShapeDtypeStruct(wab.shape, F32)),
        in_specs=[vm] * 5, out_specs=[vm] * 2,
        scratch_shapes=[pltpu.VMEM((N_DEV, SUB, D_MODEL), F32), pltpu.VMEM((wrows, HEAD_DIM), F32),
                        dma8, dma8, dma8, dma8],
        compiler_params=_params(vmem_mib=32), name="final_small",
    )(vrecv_m, vrecv_b, wab, wrecv, vec_x)


def _in_proj(x, g_mix, shards, tm):
    t_len = x.shape[0]
    n_t = t_len // tm
    n_arr = len(shards)
    rows = [s.shape[0] for s in shards]
    width = 2 * rows[0]
    ax, ay = lax.axis_index("x"), lax.axis_index("y")
    order = jnp.stack([2 * cx + cy for cx, cy in [(ax, ay)] + _other_chips(ax, ay)]).astype(jnp.int32)

    def body(order_ref, x_ref, g_ref, *rest):
        shard_refs = rest[0:n_arr]
        u_ref, h_ref = rest[n_arr:n_arr + 2]
        fulls = rest[n_arr + 2:2 * n_arr + 2]
        h_s, wbuf, send_sems, recv_sems, local_sems, load_sem = rest[2 * n_arr + 2:]
        p = pl.program_id(0)
        i = pl.program_id(1)
        x_, y_, c = _place()
        me = (x_, y_, c)
        my_id = _block_id((x_, y_), c)
        sibling = (x_, y_, 1 - c)
        chips = _other_chips(x_, y_)

        def block(arr, blk):
            return fulls[arr].at[pl.ds(pl.multiple_of(blk * rows[arr], rows[arr]), rows[arr]), :]

        def copy(arr, k, blk, to, src=None):
            dst = block(arr, blk)
            return _remote_copy(dst if src is None else src, dst, send_sems.at[arr, k], recv_sems.at[arr, k], to)

        def local(arr):
            return pltpu.make_async_copy(shard_refs[arr], block(arr, my_id), local_sems.at[arr])

        def load_chip(chip, slot):
            start = pl.multiple_of((2 * chip[0] + chip[1]) * width, width)
            return pltpu.make_async_copy(fulls[0].at[pl.ds(start, width), :], wbuf.at[slot], load_sem.at[slot])

        def pass_on(j):
            for arr in range(n_arr):
                copy(arr, 1 + j, _block_id(chips[j], c), me).wait_recv()
                copy(arr, 4 + j, _block_id(chips[j], c), sibling).start()

        def complete(j):
            for arr in range(n_arr):
                copy(arr, 4 + j, _block_id(chips[j], 1 - c), me).wait_recv()

        @pl.when((p == 0) & (i == 0))
        def _():
            for arr in range(n_arr):
                local(arr).start()
                copy(arr, 0, my_id, sibling, shard_refs[arr]).start()
                for j in (0, 1):
                    copy(arr, 1 + j, my_id, (*chips[j], c), shard_refs[arr]).start()
            for arr in range(n_arr):
                local(arr).wait()
                copy(arr, 0, _block_id((x_, y_), 1 - c), me).wait_recv()
            load_chip((x_, y_), 0).start()
            load_chip((x_, y_), 0).wait()

        @pl.when((p == 1) & (i == 0))
        def _():
            pass_on(0)
            for arr in range(n_arr):
                copy(arr, 3, my_id, (*chips[2], c), shard_refs[arr]).start()
            pass_on(1)
            complete(0)
            load_chip(chips[0], 1).start()
            load_chip(chips[0], 1).wait()
            complete(1)
            load_chip(chips[1], 0).start()

        @pl.when((p == 2) & (i == 0))
        def _():
            load_chip(chips[1], 0).wait()

        @pl.when((p == 3) & (i == 0))
        def _():
            pass_on(2)
            complete(2)
            load_chip(chips[2], 1).start()
            load_chip(chips[2], 1).wait()

        @pl.when((p == 3) & (i == n_t - 1))
        def _():
            for arr in range(n_arr):
                for k in range(4):
                    copy(arr, k, my_id, me, shard_refs[arr]).wait_send()
                for j, chip in enumerate(chips):
                    copy(arr, 4 + j, _block_id(chip, c), me).wait_send()

        tile = pl.ds(pl.multiple_of(i * tm, tm), tm)

        @pl.when(p == 0)
        def _():
            xv = x_ref[...]
            h = (xv * _rms(xv) * g_ref[...]).astype(BF16)
            h_ref[...] = h
            h_s[tile, :] = h

        for slot in (0, 1):
            @pl.when(p % 2 == slot)
            def _(slot=slot):
                u_ref[...] = _dot_nt(h_s[tile, :], wbuf[slot])

    first_pass = lambda p, i, o: (jnp.where(p == 0, i, n_t - 1), 0)
    grid_spec = pltpu.PrefetchScalarGridSpec(
        num_scalar_prefetch=1, grid=(4, n_t),
        in_specs=[pl.BlockSpec((tm, D_MODEL), first_pass), pl.BlockSpec((1, D_MODEL), lambda p, i, o: (0, 0))]
        + [HBM_SPEC] * n_arr,
        out_specs=[pl.BlockSpec((tm, width), lambda p, i, o: (i, o[p])), pl.BlockSpec((tm, D_MODEL), first_pass)]
        + [HBM_SPEC] * n_arr,
        scratch_shapes=[pltpu.VMEM((t_len, D_MODEL), BF16), pltpu.VMEM((2, width, D_MODEL), BF16)]
        + _exchange_scratch(n_arr, 7) + [pltpu.SemaphoreType.DMA((n_arr,)), pltpu.SemaphoreType.DMA((2,))])
    return pl.pallas_call(
        body, grid_spec=grid_spec,
        out_shape=[jax.ShapeDtypeStruct((t_len, IN_COLS), F32), jax.ShapeDtypeStruct((t_len, D_MODEL), BF16)]
        + [jax.ShapeDtypeStruct((N_DEV * s.shape[0], s.shape[1]), s.dtype) for s in shards],
        compiler_params=_params(("arbitrary", "arbitrary"), 48), name="in_proj",
    )(order, x, g_mix, *shards)


def _conv3_chunk(u_ref, r, cv_prev, cw, row):
    gb = u_ref[pl.ds(r, SUB), OFF_GB:OFF_GB + CONV_WIDTH]
    gc = u_ref[pl.ds(r, SUB), OFF_GC:OFF_GC + CONV_WIDTH]
    v = u_ref[pl.ds(r, SUB), OFF_V:OFF_V + CONV_WIDTH]
    cv = gc * v
    cv_m1 = _down(cv, cv_prev, 1, row)
    cv_m2 = _down(cv, cv_prev, 2, row)
    cq = cw[2:3, :] * cv + cw[1:2, :] * cv_m1 + cw[0:1, :] * cv_m2
    return gb, gc, v, cv, cv_m1, cv_m2, cq


def _conv4_chunk(u_ref, r, xin_prev, rw, rb, row):
    xin = u_ref[pl.ds(r, SUB), OFF_XR:OFF_XR + LRU_WIDTH]
    m1 = _down(xin, xin_prev, 1, row)
    m2 = _down(xin, xin_prev, 2, row)
    m3 = _down(xin, xin_prev, 3, row)
    xr = rw[3:4, :] * xin + rw[2:3, :] * m1 + rw[1:2, :] * m2 + rw[0:1, :] * m3 + rb
    return xin, m1, m2, m3, xr


def _mixer_fwd(u, conv_w, rnn_conv_w, rnn_conv_b, wa, b_a, wx, b_x, lam, gnc, gnr, shards, tm):
    t_len = u.shape[0]
    n_steps = t_len // tm
    n_chunks = tm // SUB
    n_arr = len(shards)

    def body(u_ref, cw_ref, rw_ref, rb_ref, wa_ref, ba_ref, wx_ref, bx_ref, lam_ref, gnc_ref, gnr_ref, *rest):
        shard_refs = rest[0:n_arr]
        hs_ref, y_ref, xr_s, ra_ref, ii_ref, mult_ref = rest[n_arr:n_arr + 6]
        fulls = rest[n_arr + 6:2 * n_arr + 6]
        (y_s, pa_s, px_s, wabd, wxbd, cv_car, xin_car, h_car,
         send_sems, recv_sems, local_sems) = rest[2 * n_arr + 6:]
        _host_all_gather(pl.program_id(0), n_steps, shard_refs, fulls, send_sems, recv_sems, local_sems)

        @pl.when(pl.program_id(0) == 0)
        def _():
            cv_car[...] = jnp.zeros(cv_car.shape, F32)
            xin_car[...] = jnp.zeros(xin_car.shape, F32)
            h_car[...] = jnp.zeros(h_car.shape, F32)
            wabd[...] = _expand_heads(wa_ref[...])
            wxbd[...] = _expand_heads(wx_ref[...])

        row_c = lax.broadcasted_iota(jnp.int32, (SUB, CONV_WIDTH), 0)
        row_r = lax.broadcasted_iota(jnp.int32, (SUB, LRU_WIDTH), 0)
        cw = cw_ref[...]
        rw = rw_ref[...]
        rb = rb_ref[...]
        g_c = gnc_ref[...]
        g_r = gnr_ref[...]
        sp_c = LRU_C * _softplus_neg(lam_ref[...])

        def convs(i, carry):
            cv_prev, xin_prev = carry
            r = pl.multiple_of(i * SUB, SUB)
            gb, _, _, cv, _, _, cq = _conv3_chunk(u_ref, r, cv_prev, cw, row_c)
            y_c = gb * cq
            y_s[pl.ds(r, SUB), 0:CONV_WIDTH] = y_c * _rms(y_c) * g_c
            xin, _, _, _, xr = _conv4_chunk(u_ref, r, xin_prev, rw, rb, row_r)
            xr_s[pl.ds(r, SUB), :] = xr
            return cv, xin

        cv_last, xin_last = _chunk_loop(n_chunks, convs, (cv_car[...], xin_car[...]))
        cv_car[...] = cv_last
        xin_car[...] = xin_last

        xrb = xr_s[...].astype(BF16)
        pa_s[...] = _block_diag_apply(xrb, wabd) + ba_ref[...]
        px_s[...] = _block_diag_apply(xrb, wxbd) + bx_ref[...]

        def recur(i, h_prev):
            r = pl.multiple_of(i * SUB, SUB)
            xr = xr_s[pl.ds(r, SUB), :]
            ra, ii, a, mult = _lru_gates(pa_s[pl.ds(r, SUB), :], px_s[pl.ds(r, SUB), :], sp_c)
            ra_ref[pl.ds(r, SUB), :] = ra
            ii_ref[pl.ds(r, SUB), :] = ii
            mult_ref[pl.ds(r, SUB), :] = mult
            a_cum, b_cum = _scan8_fwd(a, mult * ii * xr, row_r)
            h = a_cum * h_prev + b_cum
            hs_ref[pl.ds(r, SUB), :] = h
            ge, _ = _gelu(u_ref[pl.ds(r, SUB), OFF_G:OFF_G + LRU_WIDTH])
            y_r = h * ge
            y_s[pl.ds(r, SUB), CONV_WIDTH:MIX_WIDTH] = y_r * _rms(y_r) * g_r
            return h[SUB - 1:SUB, :]

        h_car[...] = _chunk_loop(n_chunks, recur, h_car[...])

        y_ref[...] = y_s[...].astype(BF16)

    row_tile = lambda w: pl.BlockSpec((tm, w), lambda i: (i, 0))
    whole = lambda a: pl.BlockSpec(a.shape, lambda i: (0,) * a.ndim)
    smalls = (conv_w, rnn_conv_w, rnn_conv_b, wa, b_a, wx, b_x, lam, gnc, gnr)
    return pl.pallas_call(
        body, grid=(n_steps,),
        in_specs=[row_tile(IN_COLS)] + [whole(a) for a in smalls] + [HBM_SPEC] * n_arr,
        out_specs=[row_tile(LRU_WIDTH), row_tile(MIX_WIDTH)] + [row_tile(LRU_WIDTH)] * 4 + [HBM_SPEC] * n_arr,
        out_shape=[jax.ShapeDtypeStruct((t_len, LRU_WIDTH), F32), jax.ShapeDtypeStruct((t_len, MIX_WIDTH), BF16)]
        + [jax.ShapeDtypeStruct((t_len, LRU_WIDTH), F32)] * 4
        + [jax.ShapeDtypeStruct((N_DEV,) + s.shape, BF16) for s in shards],
        scratch_shapes=[pltpu.VMEM((tm, MIX_WIDTH), F32),
                        pltpu.VMEM((tm, LRU_WIDTH), F32), pltpu.VMEM((tm, LRU_WIDTH), F32),
                        pltpu.VMEM((LRU_WIDTH, GROUP), BF16), pltpu.VMEM((LRU_WIDTH, GROUP), BF16),
                        pltpu.VMEM((SUB, CONV_WIDTH), F32), pltpu.VMEM((SUB, LRU_WIDTH), F32),
                        pltpu.VMEM((1, LRU_WIDTH), F32)]
        + _exchange_scratch(n_arr, 7) + [pltpu.SemaphoreType.DMA((n_arr,))],
        compiler_params=_params(("arbitrary",), 56), name="mixer_fwd",
    )(u, *smalls, *shards)


def _mlp_up(x, y, g_mlp, w_out, w1, w2_shard, tm):
    t_len = x.shape[0]
    n_steps = t_len // tm
    n_blk, _, blk = w1.shape

    def body(x_ref, y_ref, gm_ref, wout_hbm, w1_hbm, w2_ref, x1_ref, h2_ref, z_ref, w2_full,
             wout_s, w1_s, sem, send_sems, recv_sems, local_sems):
        step = pl.program_id(0)
        _host_all_gather(step, n_steps, [w2_ref], [w2_full], send_sems, recv_sems, local_sems)

        load_wout = pltpu.make_async_copy(wout_hbm, wout_s, sem.at[0])
        load_w1 = pltpu.make_async_copy(w1_hbm, w1_s, sem.at[1])

        @pl.when(step == 0)
        def _():
            load_wout.start()
            load_w1.start()
            load_wout.wait()

        x1v = x_ref[...] + jnp.dot(y_ref[...], wout_s[...], preferred_element_type=F32)
        x1_ref[...] = x1v
        h2 = (x1v * _rms(x1v) * gm_ref[...]).astype(BF16)
        h2_ref[...] = h2

        @pl.when(step == 0)
        def _():
            load_w1.wait()

        for k in range(n_blk):
            rp = jnp.maximum(jnp.dot(h2, w1_s[k], preferred_element_type=F32), 0.0)
            z_ref[:, k * blk:(k + 1) * blk] = (rp * rp).astype(BF16)

    row_tile = lambda w: pl.BlockSpec((tm, w), lambda i: (i, 0))
    return pl.pallas_call(
        body, grid=(n_steps,),
        in_specs=[row_tile(D_MODEL), row_tile(MIX_WIDTH), pl.BlockSpec((1, D_MODEL), lambda i: (0, 0)),
                  HBM_SPEC, HBM_SPEC, HBM_SPEC],
        out_specs=[row_tile(D_MODEL), row_tile(D_MODEL), row_tile(D_FF), HBM_SPEC],
        out_shape=[jax.ShapeDtypeStruct((t_len, D_MODEL), F32), jax.ShapeDtypeStruct((t_len, D_MODEL), BF16),
                   jax.ShapeDtypeStruct((t_len, D_FF), BF16), jax.ShapeDtypeStruct((N_DEV,) + w2_shard.shape, BF16)],
        scratch_shapes=[pltpu.VMEM(w_out.shape, BF16), pltpu.VMEM(w1.shape, BF16), pltpu.SemaphoreType.DMA((2,))]
        + _exchange_scratch(1, 7) + [pltpu.SemaphoreType.DMA((1,))],
        compiler_params=_params(("arbitrary",), 48), name="mlp_up",
    )(x, y, g_mlp, w_out, w1, w2_shard)


def _mlp_down_bwd(x1, z, target, g_mlp, g_f, w1, w2, tm):
    t_len = x1.shape[0]
    n_steps = t_len // tm
    n_blk, _, blk = w1.shape

    def body(x1_ref, z_ref, tg_ref, gm_ref, gf_ref, w1_hbm, w2_hbm, dx1_ref, dx2_ref, vec_ref, dpre_hbm,
             w1_s, w2_s, dp_s, sem, out_sem):
        step = pl.program_id(0)
        rows = pl.ds(pl.multiple_of(step * tm, tm), tm)
        dp_out = pltpu.make_async_copy(dp_s, dpre_hbm.at[rows, :], out_sem.at[0])

        load_w1 = pltpu.make_async_copy(w1_hbm, w1_s, sem.at[0])
        load_w2 = pltpu.make_async_copy(w2_hbm, w2_s, sem.at[1])

        @pl.when(step == 0)
        def _():
            load_w2.start()
            load_w1.start()
            vec_ref[...] = jnp.zeros(vec_ref.shape, F32)
            load_w2.wait()

        x1v = x1_ref[...]
        g_m = gm_ref[...]
        g_o = gf_ref[...]
        r2 = _rms(x1v)
        x1h = x1v * r2
        x2 = x1v + jnp.dot(z_ref[...], w2_s[...], preferred_element_type=F32)
        r3 = _rms(x2)
        x2h = x2 * r3
        err = x2h * g_o - tg_ref[...]
        dout = err * (1.0 / D_MODEL)
        vec_ref[ROW_LOSS:ROW_LOSS + 1, :] += (0.5 / D_MODEL) * jnp.sum(err * err, axis=0, keepdims=True)
        vec_ref[ROW_GF:ROW_GF + 1, :] += jnp.sum(dout * x2h, axis=0, keepdims=True)
        dx2 = _rms_bwd(dout, x2h, r3, g_o)
        dx2b = dx2.astype(BF16)
        dx2_ref[...] = dx2b
        dh2 = jnp.zeros((tm, D_MODEL), F32)

        @pl.when(step > 0)
        def _():
            dp_out.wait()

        @pl.when(step == 0)
        def _():
            load_w1.wait()

        for k in range(n_blk):
            cols = slice(k * blk, (k + 1) * blk)
            dz = _dot_nt(dx2b, w2_s[cols, :])
            dpb = (dz * 2.0 * jnp.sqrt(z_ref[:, cols].astype(F32))).astype(BF16)
            dp_s[:, cols] = dpb
            dh2 = dh2 + _dot_nt(dpb, w1_s[k])
        dp_out.start()
        vec_ref[ROW_GMLP:ROW_GMLP + 1, :] += jnp.sum(dh2 * x1h, axis=0, keepdims=True)
        dx1_ref[...] = dx2 + _rms_bwd(dh2, x1h, r2, g_m)

        @pl.when(step == n_steps - 1)
        def _():
            dp_out.wait()

    row_tile = lambda w: pl.BlockSpec((tm, w), lambda i: (i, 0))
    vec_spec = pl.BlockSpec((1, D_MODEL), lambda i: (0, 0))
    return pl.pallas_call(
        body, grid=(n_steps,),
        in_specs=[row_tile(D_MODEL), row_tile(D_FF), row_tile(D_MODEL), vec_spec, vec_spec, HBM_SPEC, HBM_SPEC],
        out_specs=[row_tile(D_MODEL), row_tile(D_MODEL), pl.BlockSpec((SUB, D_MODEL), lambda i: (0, 0)), HBM_SPEC],
        out_shape=[jax.ShapeDtypeStruct((t_len, D_MODEL), F32), jax.ShapeDtypeStruct((t_len, D_MODEL), BF16),
                   jax.ShapeDtypeStruct((SUB, D_MODEL), F32), jax.ShapeDtypeStruct((t_len, D_FF), BF16)],
        scratch_shapes=[pltpu.VMEM(w1.shape, BF16), pltpu.VMEM(w2.shape, BF16), pltpu.VMEM((tm, D_FF), BF16),
                        pltpu.SemaphoreType.DMA((2,)), pltpu.SemaphoreType.DMA((1,))],
        compiler_params=_params(("arbitrary",), 56), name="mlp_down_bwd",
    )(x1, z, target, g_mlp, g_f, w1, w2)


def _mixer_bwd(u, hs, dx1, saved, conv_w, rnn_conv_w, rnn_conv_b, wa, wx, lam, gnc, gnr, w_out,
               chip_sums, g_wout, tm):
    t_len = u.shape[0]
    n_tiles = t_len // tm
    n_chunks = tm // SUB
    per_tile = tm // SUB
    n_sums = len(chip_sums)

    def body(u_ref, up_ref, hs_ref, hp_ref, dx1_ref, xr_ref, ra_ref, ii_ref, mult_ref,
             cw_ref, rw_ref, rb_ref, wa_ref, wx_ref, lam_ref, gnc_ref, gnr_ref, wout_ref, *rest):
        hsends = rest[0:n_sums]
        gwout_ref = rest[n_sums]
        du_ref, vec_ref, wab_ref = rest[n_sums + 1:n_sums + 4]
        hrecvs = rest[n_sums + 4:2 * n_sums + 4]
        sib_wout = rest[2 * n_sums + 4]
        (du_s, dy_s, dpa_s, dpx_s, dxr_s, wabd, wxbd, acc, dwa_acc, dwx_acc,
         a_car, dh_car, dcq_car, dxr_car, i_send, i_recv, d_send, d_recv) = rest[2 * n_sums + 5:]
        step = pl.program_id(0)
        _host_chip_exchange(step, n_tiles, hsends, hrecvs, i_send, i_recv)
        _host_pair_exchange(step, n_tiles, [gwout_ref], [sib_wout], d_send, d_recv)
        has_prev = (step < n_tiles - 1).astype(F32)

        @pl.when(step == 0)
        def _():
            acc[...] = jnp.zeros(acc.shape, F32)
            dwa_acc[...] = jnp.zeros(dwa_acc.shape, F32)
            dwx_acc[...] = jnp.zeros(dwx_acc.shape, F32)
            a_car[...] = jnp.ones(a_car.shape, F32)
            dh_car[...] = jnp.zeros(dh_car.shape, F32)
            dcq_car[...] = jnp.zeros(dcq_car.shape, F32)
            dxr_car[...] = jnp.zeros(dxr_car.shape, F32)
            wabd[...] = _expand_heads(wa_ref[...])
            wxbd[...] = _expand_heads(wx_ref[...])

        row_c = lax.broadcasted_iota(jnp.int32, (SUB, CONV_WIDTH), 0)
        row_r = lax.broadcasted_iota(jnp.int32, (SUB, LRU_WIDTH), 0)
        cw = cw_ref[...]
        rw = rw_ref[...]
        rb = rb_ref[...]
        g_c = gnc_ref[...]
        g_r = gnr_ref[...]
        sp_c = LRU_C * _softplus_neg(lam_ref[...])

        up = up_ref[...] * has_prev
        cv_before = up[:, OFF_GC:OFF_GC + CONV_WIDTH] * up[:, OFF_V:OFF_V + CONV_WIDTH]
        xin_before = up[:, OFF_XR:OFF_XR + LRU_WIDTH]
        hs_before = hp_ref[...] * has_prev

        dy_s[...] = _dot_nt(dx1_ref[...].astype(BF16), wout_ref[...])

        xrb = xr_ref[...].astype(BF16)

        def recur_bwd(j, carry):
            a_later, dh_later = carry
            i = n_chunks - 1 - j
            r = pl.multiple_of(i * SUB, SUB)
            rp = pl.multiple_of(jnp.maximum(i - 1, 0) * SUB, SUB)
            xr = xr_ref[pl.ds(r, SUB), :]
            hs_c = hs_ref[pl.ds(r, SUB), :]
            hs_prev = jnp.where(i == 0, hs_before, hs_ref[pl.ds(rp, SUB), :])
            h_m1 = _down(hs_c, hs_prev, 1, row_r)
            ra = ra_ref[pl.ds(r, SUB), :]
            ii = ii_ref[pl.ds(r, SUB), :]
            mult = mult_ref[pl.ds(r, SUB), :]
            a = jnp.exp(-ra * sp_c)
            inv_mult = lax.rsqrt(mult * mult)
            ge, dge = _gelu(u_ref[pl.ds(r, SUB), OFF_G:OFF_G + LRU_WIDTH])
            y_r = hs_c * ge
            rr = _rms(y_r)
            yhat = y_r * rr
            dyn = dy_s[pl.ds(r, SUB), CONV_WIDTH:MIX_WIDTH]
            acc[ACC_GNR] += dyn * yhat
            dy_r = _rms_bwd(dyn, yhat, rr, g_r)
            du_s[pl.ds(r, SUB), OFF_G:OFF_G + LRU_WIDTH] = dy_r * hs_c * dge
            a_cum, d_cum = _scan8_rev(_up(a, a_later, 1, row_r), dy_r * ge, row_r)
            dh = a_cum * dh_later + d_cum
            dm = dh * mult
            dii = dm * xr
            dxr_s[pl.ds(r, SUB), :] = dm * ii
            dla = a * dh * (h_m1 - (ii * xr) * a * inv_mult)
            dla_r = dla * ra
            acc[ACC_SP] -= dla_r
            dpa = dla_r * (sp_c * (ra - 1.0))
            dpx = dii * ii * (1.0 - ii)
            acc[ACC_BA] += dpa
            acc[ACC_BX] += dpx
            dpa_s[pl.ds(r, SUB), :] = dpa
            dpx_s[pl.ds(r, SUB), :] = dpx
            return a, dh[0:1, :]

        a_first, dh_first = _chunk_loop(n_chunks, recur_bwd, (a_car[...], dh_car[...]))
        a_car[...] = a_first
        dh_car[...] = dh_first

        dpab = dpa_s[...].astype(BF16)
        dpxb = dpx_s[...].astype(BF16)
        dxr_s[...] += _block_diag_apply_t(dpab, wabd) + _block_diag_apply_t(dpxb, wxbd)
        for g in range(LRU_WIDTH // GROUP):
            cols = slice(g * GROUP, (g + 1) * GROUP)
            dwa_acc[cols, :] += _dot_tn(xrb[:, cols], dpab[:, cols])
            dwx_acc[cols, :] += _dot_tn(xrb[:, cols], dpxb[:, cols])

        def convs_bwd(j, carry):
            dcq_later, dxr_later = carry
            i = n_chunks - 1 - j
            r = pl.multiple_of(i * SUB, SUB)
            rp = pl.multiple_of(jnp.maximum(i - 1, 0) * SUB, SUB)
            cv_prev = jnp.where(i == 0, cv_before,
                                u_ref[pl.ds(rp, SUB), OFF_GC:OFF_GC + CONV_WIDTH]
                                * u_ref[pl.ds(rp, SUB), OFF_V:OFF_V + CONV_WIDTH])
            gb, gc, v, cv, cv_m1, cv_m2, cq = _conv3_chunk(u_ref, r, cv_prev, cw, row_c)
            y_c = gb * cq
            rc = _rms(y_c)
            yhat = y_c * rc
            dyn = dy_s[pl.ds(r, SUB), 0:CONV_WIDTH]
            acc[ACC_GNC, :, 0:CONV_WIDTH] += dyn * yhat
            dy_c = _rms_bwd(dyn, yhat, rc, g_c)
            dcq = dy_c * gb
            dcv = (cw[2:3, :] * dcq + cw[1:2, :] * _up(dcq, dcq_later, 1, row_c)
                   + cw[0:1, :] * _up(dcq, dcq_later, 2, row_c))
            acc[ACC_CW + 2, :, 0:CONV_WIDTH] += dcq * cv
            acc[ACC_CW + 1, :, 0:CONV_WIDTH] += dcq * cv_m1
            acc[ACC_CW + 0, :, 0:CONV_WIDTH] += dcq * cv_m2
            du_s[pl.ds(r, SUB), OFF_GB:OFF_GB + CONV_WIDTH] = dy_c * cq
            du_s[pl.ds(r, SUB), OFF_GC:OFF_GC + CONV_WIDTH] = dcv * v
            du_s[pl.ds(r, SUB), OFF_V:OFF_V + CONV_WIDTH] = dcv * gc

            xin_prev = jnp.where(i == 0, xin_before, u_ref[pl.ds(rp, SUB), OFF_XR:OFF_XR + LRU_WIDTH])
            xin, m1, m2, m3, _ = _conv4_chunk(u_ref, r, xin_prev, rw, rb, row_r)
            dxr = dxr_s[pl.ds(r, SUB), :]
            du_s[pl.ds(r, SUB), OFF_XR:OFF_XR + LRU_WIDTH] = (
                rw[3:4, :] * dxr + rw[2:3, :] * _up(dxr, dxr_later, 1, row_r)
                + rw[1:2, :] * _up(dxr, dxr_later, 2, row_r) + rw[0:1, :] * _up(dxr, dxr_later, 3, row_r))
            acc[ACC_RW + 3] += dxr * xin
            acc[ACC_RW + 2] += dxr * m1
            acc[ACC_RW + 1] += dxr * m2
            acc[ACC_RW + 0] += dxr * m3
            acc[ACC_BR] += dxr
            return dcq, dxr

        dcq_first, dxr_first = _chunk_loop(n_chunks, convs_bwd, (dcq_car[...], dxr_car[...]))
        dcq_car[...] = dcq_first
        dxr_car[...] = dxr_first

        du_ref[...] = du_s[...].astype(BF16)

        @pl.when(step == n_tiles - 1)
        def _():
            vec_ref[...] = jnp.zeros(vec_ref.shape, F32)
            rows = {ACC_GNC: ROW_GNC, ACC_GNR: ROW_GNR, ACC_BR: ROW_BR, ACC_BA: ROW_BA, ACC_BX: ROW_BX}
            for k in range(3):
                rows[ACC_CW + k] = ROW_CW + k
            for k in range(4):
                rows[ACC_RW + k] = ROW_RW + k
            for slot, out_row in rows.items():
                o = out_row - ROW_GNC
                vec_ref[o:o + 1, :] = jnp.sum(acc[slot], axis=0, keepdims=True)
            lam_v = lam_ref[...]
            dsp = jnp.sum(acc[ACC_SP], axis=0, keepdims=True)
            o = ROW_LAM - ROW_GNC
            vec_ref[o:o + 1, :] = -dsp * LRU_C / (1.0 + jnp.exp(lam_v))
            wab_ref[0:LRU_WIDTH, :] = _fold_heads(dwa_acc[...])
            wab_ref[LRU_WIDTH:2 * LRU_WIDTH, :] = _fold_heads(dwx_acc[...])

    rev = lambda w: pl.BlockSpec((tm, w), lambda s: (n_tiles - 1 - s, 0))
    before = lambda w: pl.BlockSpec((SUB, w), lambda s: (jnp.maximum((n_tiles - 1 - s) * per_tile - 1, 0), 0))
    whole = lambda a: pl.BlockSpec(a.shape, lambda s: (0,) * a.ndim)
    smalls = (conv_w, rnn_conv_w, rnn_conv_b, wa, wx, lam, gnc, gnr, w_out)
    full = lambda w: pltpu.VMEM((tm, w), F32)
    return pl.pallas_call(
        body, grid=(n_tiles,),
        in_specs=[rev(IN_COLS), before(IN_COLS), rev(LRU_WIDTH), before(LRU_WIDTH), rev(D_MODEL)]
        + [rev(LRU_WIDTH)] * len(saved) + [whole(a) for a in smalls] + [HBM_SPEC] * (n_sums + 1),
        out_specs=[rev(IN_COLS), pl.BlockSpec((16, D_MODEL), lambda s: (0, 0)),
                   pl.BlockSpec((2 * LRU_WIDTH, HEAD_DIM), lambda s: (0, 0))] + [HBM_SPEC] * (n_sums + 1),
        out_shape=[jax.ShapeDtypeStruct((t_len, IN_COLS), BF16), jax.ShapeDtypeStruct((16, D_MODEL), F32),
                   jax.ShapeDtypeStruct((2 * LRU_WIDTH, HEAD_DIM), F32)]
        + [jax.ShapeDtypeStruct(s.shape, BF16) for s in chip_sums]
        + [jax.ShapeDtypeStruct((4,) + g_wout.shape[1:], BF16)],
        scratch_shapes=[full(IN_COLS), full(MIX_WIDTH), full(LRU_WIDTH), full(LRU_WIDTH), full(LRU_WIDTH),
                        pltpu.VMEM((LRU_WIDTH, GROUP), BF16), pltpu.VMEM((LRU_WIDTH, GROUP), BF16),
                        pltpu.VMEM((N_ACC, SUB, LRU_WIDTH), F32),
                        pltpu.VMEM((LRU_WIDTH, GROUP), F32), pltpu.VMEM((LRU_WIDTH, GROUP), F32),
                        pltpu.VMEM((SUB, LRU_WIDTH), F32), pltpu.VMEM((1, LRU_WIDTH), F32),
                        pltpu.VMEM((SUB, CONV_WIDTH), F32), pltpu.VMEM((SUB, LRU_WIDTH), F32)]
        + _exchange_scratch(n_sums, 3) + _exchange_scratch(1, 4),
        compiler_params=_params(("arbitrary",), 56), name="mixer_bwd",
    )(u, u, hs, hs, dx1, *saved, *smalls, *chip_sums, g_wout)


def _in_proj_bwd(du, dx1, x, g_mix, win_t, tm, chip_sums, g_own):
    t_len = x.shape[0]
    n_steps = t_len // tm

    def body(du_ref, dx1_ref, x_ref, g_ref, w_ref, hs_ref, gown_ref,
             dx_ref, vec_ref, landed_ref, sib_ref, i_send, i_recv, d_send, d_recv):
        step = pl.program_id(0)
        _host_chip_exchange(step, n_steps, [hs_ref], [landed_ref], i_send, i_recv)
        _host_half_exchange(step, n_steps, gown_ref, sib_ref, d_send, d_recv)

        @pl.when(step == 0)
        def _():
            vec_ref[...] = jnp.zeros(vec_ref.shape, F32)

        dh = jnp.dot(du_ref[...], w_ref[...], preferred_element_type=F32)
        xv = x_ref[...]
        r1 = _rms(xv)
        xh = xv * r1
        vec_ref[0:1, :] += jnp.sum(dh * xh, axis=0, keepdims=True)
        dx_ref[...] = dx1_ref[...] + _rms_bwd(dh, xh, r1, g_ref[...])

    row_tile = lambda w: pl.BlockSpec((tm, w), lambda i: (i, 0))
    half_shape = (g_own.shape[0], g_own.shape[1] // 2, g_own.shape[2])
    return pl.pallas_call(
        body, grid=(n_steps,),
        in_specs=[row_tile(IN_COLS), row_tile(D_MODEL), row_tile(D_MODEL), pl.BlockSpec((1, D_MODEL), lambda i: (0, 0)),
                  pl.BlockSpec((IN_COLS, D_MODEL), lambda i: (0, 0))] + [HBM_SPEC] * 2,
        out_specs=[row_tile(D_MODEL), pl.BlockSpec((SUB, D_MODEL), lambda i: (0, 0))] + [HBM_SPEC] * 2,
        out_shape=[jax.ShapeDtypeStruct((t_len, D_MODEL), F32), jax.ShapeDtypeStruct((SUB, D_MODEL), F32),
                   jax.ShapeDtypeStruct(chip_sums.shape, BF16), jax.ShapeDtypeStruct(half_shape, BF16)],
        scratch_shapes=_exchange_scratch(1, 3) + [pltpu.SemaphoreType.DMA((1,)), pltpu.SemaphoreType.DMA((1,))],
        compiler_params=_params(("arbitrary",), 56), name="in_proj_bwd",
    )(du, dx1, x, g_mix, win_t, chip_sums, g_own)


def _tn_weight_grad(a, b, tk, name, pair=(), col_blocks=1):
    t_len, m = a.shape
    n = b.shape[1]
    n_steps = t_len // tk
    sent = tuple(pair)
    n_sent = len(sent)

    def body(a_ref, b_ref, *rest):
        srcs = rest[0:n_sent]
        o_ref = rest[n_sent]
        dsts = rest[n_sent + 1:2 * n_sent + 1]
        acc = rest[2 * n_sent + 1]
        sems = rest[2 * n_sent + 2:]
        j = pl.program_id(0)
        if pair:
            _host_pair_exchange(j, n_steps, srcs, dsts, *sems)

        @pl.when(j == 0)
        def _():
            acc[...] = jnp.zeros(acc.shape, F32)

        acc[...] += _dot_tn(a_ref[...].astype(BF16), b_ref[...].astype(BF16))

        @pl.when(j == n_steps - 1)
        def _():
            if col_blocks == 1:
                o_ref[...] = acc[...].astype(BF16)
            else:
                for k in range(col_blocks):
                    o_ref[k] = acc[:, k * nb:(k + 1) * nb].astype(BF16)

    nb = n // col_blocks
    out_dims = (m, n) if col_blocks == 1 else (col_blocks, m, nb)
    landed = [jax.ShapeDtypeStruct((4,) + g.shape[1:], BF16) for g in pair]
    scratch = [pltpu.VMEM((m, n), F32)]
    if n_sent:
        scratch += _exchange_scratch(n_sent, 4)
    return pl.pallas_call(
        body, grid=(n_steps,),
        in_specs=[pl.BlockSpec((tk, m), lambda j: (j, 0)), pl.BlockSpec((tk, n), lambda j: (j, 0))]
        + [HBM_SPEC] * n_sent,
        out_specs=[pl.BlockSpec(out_dims, lambda j: (0,) * len(out_dims))] + [HBM_SPEC] * n_sent,
        out_shape=[jax.ShapeDtypeStruct(out_dims, BF16)] + landed,
        scratch_shapes=scratch,
        compiler_params=_params(("arbitrary",), 56), name=name,
    )(a, b, *sent)


def _w_in_grad_part(du, h, tk, name, chip_ids, chip=(), halves=None, small=None):
    t_len = du.shape[0]
    n_t = t_len // tk
    n_q = chip_ids.shape[0]
    width = 2 * (IN_COLS // N_DEV)
    n_steps = n_q * n_t
    n_chip = len(chip)
    sent = tuple(chip) + (() if halves is None else (halves,)) + (() if small is None else tuple(small))
    n_sent = len(sent)

    def body(ids_ref, a_ref, b_ref, *rest):
        srcs = rest[0:n_sent]
        o_ref = rest[n_sent]
        dsts = rest[n_sent + 1:2 * n_sent + 1]
        acc = rest[2 * n_sent + 1]
        sems = list(rest[2 * n_sent + 2:])
        j = pl.program_id(1)
        step = pl.program_id(0) * n_t + j
        if chip:
            _host_chip_exchange(step, n_steps, srcs[0:n_chip], dsts[0:n_chip], sems.pop(0), sems.pop(0))
        if halves is not None:
            _host_half_exchange(step, n_steps, srcs[n_chip], dsts[n_chip], sems.pop(0), sems.pop(0))
        if small is not None:
            _host_small_exchange(step, n_steps, *srcs[n_sent - 3:], *dsts[n_sent - 3:], *sems)

        @pl.when(j == 0)
        def _():
            acc[...] = jnp.zeros(acc.shape, F32)

        acc[...] += _dot_tn(a_ref[...], b_ref[...])

        @pl.when(j == n_t - 1)
        def _():
            o_ref[0] = acc[...].astype(BF16)

    landed = [jax.ShapeDtypeStruct(s.shape, BF16) for s in chip]
    scratch = [pltpu.VMEM((width, D_MODEL), F32)]
    if chip:
        scratch += _exchange_scratch(len(chip), 3)
    if halves is not None:
        landed.append(jax.ShapeDtypeStruct((halves.shape[0], halves.shape[1] // 2, halves.shape[2]), BF16))
        scratch += [pltpu.SemaphoreType.DMA((halves.shape[0],)), pltpu.SemaphoreType.DMA((halves.shape[0],))]
    if small is not None:
        vec_m, vec_b, wab = small
        landed += [jax.ShapeDtypeStruct((N_DEV,) + vec_m.shape, F32), jax.ShapeDtypeStruct((N_DEV,) + vec_b.shape, F32),
                   jax.ShapeDtypeStruct((N_DEV, wab.shape[0] // N_DEV, wab.shape[1]), F32)]
        scratch += _exchange_scratch(3, N_DEV) + [pltpu.SemaphoreType.DMA((2,))]
    grid_spec = pltpu.PrefetchScalarGridSpec(
        num_scalar_prefetch=1, grid=(n_q, n_t),
        in_specs=[pl.BlockSpec((tk, width), lambda q, j, ids: (j, ids[q])),
                  pl.BlockSpec((tk, D_MODEL), lambda q, j, ids: (j, 0))] + [HBM_SPEC] * n_sent,
        out_specs=[pl.BlockSpec((1, width, D_MODEL), lambda q, j, ids: (q, 0, 0))] + [HBM_SPEC] * n_sent,
        scratch_shapes=scratch)
    return pl.pallas_call(
        body, grid_spec=grid_spec, out_shape=[jax.ShapeDtypeStruct((n_q, width, D_MODEL), BF16)] + landed,
        compiler_params=_params(("arbitrary", "arbitrary"), 40), name=name,
    )(chip_ids, du, h, *sent)


def _adamw(w, g, m, v):
    m = ADAM_B1 * m + (1.0 - ADAM_B1) * g
    v = ADAM_B2 * v + (1.0 - ADAM_B2) * (g * g)
    delta = -ADAM_LR * ((m / BC1) / (jnp.sqrt(v / BC2) + ADAM_EPS) + ADAM_WD * w)
    return delta, m, v


def _update_sharded(g, landed, w, m, v, rows_blk, name):
    rows, cols = w.shape

    def body(g_ref, l_ref, w_ref, m_ref, v_ref, og, od, om, ov):
        gv = g_ref[...]
        for j in range(3):
            gv = gv + l_ref[j].astype(F32)
        delta, mn, vn = _adamw(w_ref[...], gv, m_ref[...], v_ref[...])
        og[...] = gv
        od[...] = delta
        om[...] = mn
        ov[...] = vn

    blk = pl.BlockSpec((rows_blk, cols), lambda i: (i, 0))
    shape = pltpu.HBM((rows, cols), F32)
    return pl.pallas_call(
        body, grid=(rows // rows_blk,),
        in_specs=[blk, pl.BlockSpec((3, rows_blk, cols), lambda i: (0, i, 0)), blk, blk, blk],
        out_specs=[blk] * 4, out_shape=[shape] * 4,
        compiler_params=_params(("arbitrary",), 32), name=name,
    )(*_in_hbm(g, landed, w, m, v))


def _update_w_in(g_own, sib_own, landed, w_t, m_t, v_t, core, cols_blk):
    rows, cols = w_t.shape

    def body(core_ref, g_ref, s_ref, l_ref, w_ref, m_ref, v_ref, og, od, om, ov):
        gv = g_ref[0, 0].astype(F32) + s_ref[0].astype(F32)
        for j in range(3):
            gv = gv + l_ref[j].astype(F32)
        delta, mn, vn = _adamw(w_ref[...], gv, m_ref[...], v_ref[...])
        og[...] = gv
        od[...] = delta
        om[...] = mn
        ov[...] = vn

    blk = pl.BlockSpec((rows, cols_blk), lambda i, cr: (0, i))
    grid_spec = pltpu.PrefetchScalarGridSpec(
        num_scalar_prefetch=1, grid=(cols // cols_blk,),
        in_specs=[pl.BlockSpec((1, 1, rows, cols_blk), lambda i, cr: (0, cr[0], 0, i)),
                  pl.BlockSpec((1, rows, cols_blk), lambda i, cr: (0, 0, i)),
                  pl.BlockSpec((3, rows, cols_blk), lambda i, cr: (0, 0, i)), blk, blk, blk],
        out_specs=[blk] * 4)
    return pl.pallas_call(
        body, grid_spec=grid_spec, out_shape=[pltpu.HBM((rows, cols), F32)] * 4,
        compiler_params=_params(("arbitrary",), 32), name="update_w_in",
    )(core, *_in_hbm(g_own.reshape(1, 2, rows, cols), sib_own, landed, w_t, m_t, v_t))


def _update_small(vsum, wsum, g_cw, g_rw, weights, moments_m, moments_v):
    n = len(weights)

    def body(*refs):
        vs, ws, gcw, grw = refs[0:4]
        w_refs = refs[4:4 + n]
        m_refs = refs[4 + n:4 + 2 * n]
        v_refs = refs[4 + 2 * n:4 + 3 * n]
        outs = refs[4 + 3 * n:]
        loss_ref = outs[0]
        loss_ref[...] = jnp.sum(vs[ROW_LOSS:ROW_LOSS + 1, :], axis=1, keepdims=True)
        grads = [
            vs[ROW_GMIX:ROW_GMIX + 1, :], gcw[...], grw[...], vs[ROW_BR:ROW_BR + 1, :],
            ws[0:LRU_WIDTH, :], vs[ROW_BA:ROW_BA + 1, :], ws[LRU_WIDTH:2 * LRU_WIDTH, :], vs[ROW_BX:ROW_BX + 1, :],
            vs[ROW_LAM:ROW_LAM + 1, :], vs[ROW_GNC:ROW_GNC + 1, 0:CONV_WIDTH], vs[ROW_GNR:ROW_GNR + 1, :],
            vs[ROW_GMLP:ROW_GMLP + 1, :], vs[ROW_GF:ROW_GF + 1, :],
        ]
        for k in range(n):
            gk = grads[k]
            delta, mn, vn = _adamw(w_refs[k][...], gk, m_refs[k][...], v_refs[k][...])
            outs[1 + 4 * k][...] = gk
            outs[2 + 4 * k][...] = delta
            outs[3 + 4 * k][...] = mn
            outs[4 + 4 * k][...] = vn

    whole = lambda a: pl.BlockSpec(a.shape, lambda i: (0,) * len(a.shape))
    out_shape = [jax.ShapeDtypeStruct((1, 1), F32)]
    for w in weights:
        out_shape += [jax.ShapeDtypeStruct(w.shape, F32)] * 4
    args = (vsum, wsum, g_cw, g_rw, *weights, *moments_m, *moments_v)
    return pl.pallas_call(
        body, grid=(1,), out_shape=out_shape, in_specs=[whole(a) for a in args], out_specs=[whole(s) for s in out_shape],
        compiler_params=_params(("arbitrary",), 32), name="update_small",
    )(*args)


def kernel(x, norm_mix_g, w_in, conv_w, rnn_conv_w, rnn_conv_b, w_a, b_a, w_x, b_x, lru_lambda, g_norm_conv, g_norm_rnn, w_out, norm_mlp_g, w_mlp_in, w_mlp_out, final_norm_g, loss_target, m_norm_mix_g, m_w_in, m_conv_w, m_rnn_conv_w, m_rnn_conv_b, m_w_a, m_b_a, m_w_x, m_b_x, m_lru_lambda, m_g_norm_conv, m_g_norm_rnn, m_w_out, m_norm_mlp_g, m_w_mlp_in, m_w_mlp_out, m_final_norm_g, v_norm_mix_g, v_w_in, v_conv_w, v_rnn_conv_w, v_rnn_conv_b, v_w_a, v_b_a, v_w_x, v_b_x, v_lru_lambda, v_g_norm_conv, v_g_norm_rnn, v_w_out, v_norm_mlp_g, v_w_mlp_in, v_w_mlp_out, v_final_norm_g):
    t_len = x.shape[1]
    my_id = 4 * lax.axis_index("x") + 2 * lax.axis_index("y") + lax.axis_index("c")
    tm = min(256, t_len)
    tb = min(512, t_len)
    tk = min(512, t_len)

    xs = x.reshape(t_len, D_MODEL)
    tgt = loss_target.reshape(t_len, D_MODEL)
    flat = lambda a: a.reshape(a.shape[-2:]) if a.ndim == 3 else a.reshape(1, -1)
    heads = lambda a: a.reshape(LRU_WIDTH, HEAD_DIM)

    turned = lambda a: jnp.transpose(flat(a))
    win_shard, wout_shard, w1_shard, w2_shard, cp_shard = _prep_shards(
        turned(w_in), flat(w_out), flat(w_mlp_in), flat(w_mlp_out), flat(conv_w), flat(rnn_conv_w))

    u, h, win_t, cp_full = _in_proj(xs, flat(norm_mix_g), (win_shard, cp_shard), min(1024, t_len))
    cpack = cp_full.reshape(N_DEV, 8, 128)
    conv_full = jnp.transpose(cpack[:, 0:3, 0:64], (1, 0, 2)).reshape(3, CONV_WIDTH)
    rnn_full = jnp.transpose(cpack[:, 3:7, :], (1, 0, 2)).reshape(4, LRU_WIDTH)
    mixer_small = (conv_full, rnn_full, flat(rnn_conv_b), heads(w_a), flat(b_a), heads(w_x), flat(b_x),
                   flat(lru_lambda), flat(g_norm_conv), flat(g_norm_rnn))
    hs, y, xr, gate_r, gate_i, mult, w1_blk, wout_blk = _mixer_fwd(u, *mixer_small, (w1_shard, wout_shard), tm)
    wout_f = wout_blk.reshape(MIX_WIDTH, D_MODEL)
    x1, h2, z, w2_blk = _mlp_up(xs, y, flat(norm_mlp_g), wout_f, w1_blk, w2_shard, tb)
    dx1, dx2, vec_m, dpre = _mlp_down_bwd(x1, z, tgt, flat(norm_mlp_g), flat(final_norm_g), w1_blk,
                                          w2_blk.reshape(D_FF, D_MODEL), tb)
    (g_w1,) = _tn_weight_grad(h2, dpre, tk, "w_mlp_in_grad", col_blocks=N_DEV)
    (g_w2,) = _tn_weight_grad(z, dx2, tk, "w_mlp_out_grad")
    g_w2 = g_w2.reshape(N_DEV, D_FF // N_DEV, D_MODEL)
    g_wout, sib_w1, sib_w2 = _tn_weight_grad(y, dx1, tk, "w_out_grad", pair=(g_w1, g_w2))
    g_wout = g_wout.reshape(N_DEV, MIX_WIDTH // N_DEV, D_MODEL)
    hsend_w1, own_w1, hsend_w2, own_w2 = _pair_sum((g_w1, g_w2), (sib_w1, sib_w2), "pair_sum_w_mlp")
    du, vec_b, wab, landed_w1, landed_w2, sib_wout = _mixer_bwd(
        u, hs, dx1, (xr, gate_r, gate_i, mult), conv_full, rnn_full, flat(rnn_conv_b), heads(w_a), heads(w_x),
        flat(lru_lambda), flat(g_norm_conv), flat(g_norm_rnn), wout_f, (hsend_w1, hsend_w2), g_wout, tm)
    hsend_wout, own_wout = _pair_sum((g_wout,), (sib_wout,), "pair_sum_w_out")
    ax, ay, ac = lax.axis_index("x"), lax.axis_index("y"), lax.axis_index("c")
    chip_ids = jnp.stack([2 * cx + cy for cx, cy in [(ax, ay)] + _other_chips(ax, ay)]).astype(jnp.int32)
    core = jnp.reshape(ac, (1,)).astype(jnp.int32)
    tw = min(1024, t_len)
    g_others, landed_wout = _w_in_grad_part(du, h, tw, "w_in_grad_others", chip_ids[1:4], chip=(hsend_wout,))
    g_own, sib_others, vrecv_m, vrecv_b, wrecv = _w_in_grad_part(
        du, h, tw, "w_in_grad_own", chip_ids[0:1], halves=g_others, small=(vec_m, vec_b, wab))
    hsend_win = _pair_sum_parts(g_others, sib_others, core)
    grad_x, vec_x, landed_win, sib_own = _in_proj_bwd(du, dx1, xs, flat(norm_mix_g), win_t, tm, hsend_win, g_own)

    vsum, wsum = _final_small(vrecv_m, vrecv_b, wab, wrecv, vec_x)

    up_win = _update_w_in(g_own, sib_own, landed_win, turned(w_in), turned(m_w_in), turned(v_w_in), core, 256)
    up_win = [jnp.transpose(a) for a in up_win]
    up_wout = _update_sharded(own_wout, landed_wout, flat(w_out), flat(m_w_out), flat(v_w_out), 96, "update_w_out")
    up_w1 = _update_sharded(own_w1, landed_w1, flat(w_mlp_in), flat(m_w_mlp_in), flat(v_w_mlp_in), 256,
                            "update_w_mlp_in")
    up_w2 = _update_sharded(own_w2, landed_w2, flat(w_mlp_out), flat(m_w_mlp_out), flat(v_w_mlp_out), 256,
                            "update_w_mlp_out")

    g_cw = lax.dynamic_slice(vsum, (ROW_CW, 64 * my_id), (3, 64))
    g_rw = lax.dynamic_slice(vsum, (ROW_RW, 128 * my_id), (4, 128))
    small_w = (norm_mix_g, conv_w, rnn_conv_w, rnn_conv_b, w_a, b_a, w_x, b_x, lru_lambda, g_norm_conv, g_norm_rnn,
               norm_mlp_g, final_norm_g)
    small_m = (m_norm_mix_g, m_conv_w, m_rnn_conv_w, m_rnn_conv_b, m_w_a, m_b_a, m_w_x, m_b_x, m_lru_lambda,
               m_g_norm_conv, m_g_norm_rnn, m_norm_mlp_g, m_final_norm_g)
    small_v = (v_norm_mix_g, v_conv_w, v_rnn_conv_w, v_rnn_conv_b, v_w_a, v_b_a, v_w_x, v_b_x, v_lru_lambda,
               v_g_norm_conv, v_g_norm_rnn, v_norm_mlp_g, v_final_norm_g)
    is_heads = (False, False, False, False, True, False, True, False, False, False, False, False, False)
    as2d = lambda arrs: [heads(a) if hd else flat(a) for a, hd in zip(arrs, is_heads)]
    small_out = _update_small(vsum, wsum, g_cw, g_rw, as2d(small_w), as2d(small_m), as2d(small_v))
    loss = small_out[0].reshape(())

    names = ["norm_mix_g", "w_in", "conv_w", "rnn_conv_w", "rnn_conv_b", "w_a", "b_a", "w_x", "b_x", "lru_lambda",
             "g_norm_conv", "g_norm_rnn", "w_out", "norm_mlp_g", "w_mlp_in", "w_mlp_out", "final_norm_g"]
    originals = dict(zip(names, (norm_mix_g, w_in, conv_w, rnn_conv_w, rnn_conv_b, w_a, b_a, w_x, b_x, lru_lambda,
                                 g_norm_conv, g_norm_rnn, w_out, norm_mlp_g, w_mlp_in, w_mlp_out, final_norm_g)))
    results = {"w_in": up_win, "w_out": up_wout, "w_mlp_in": up_w1, "w_mlp_out": up_w2}
    small_names = ["norm_mix_g", "conv_w", "rnn_conv_w", "rnn_conv_b", "w_a", "b_a", "w_x", "b_x", "lru_lambda",
                   "g_norm_conv", "g_norm_rnn", "norm_mlp_g", "final_norm_g"]
    for k, nm in enumerate(small_names):
        results[nm] = small_out[1 + 4 * k:5 + 4 * k]
    out = [loss, grad_x.reshape(x.shape)]
    for kind in range(4):
        out += [results[nm][kind].reshape(originals[nm].shape) for nm in names]
    return tuple(out)
```

```python
import functools

import jax
import jax.numpy as jnp
from jax import lax
from jax.experimental import pallas as pl
from jax.experimental.pallas import tpu as pltpu

F32 = jnp.float32
BF16 = jnp.bfloat16

D_MODEL = 1024
HEAD_DIM = 64
CONV_WIDTH = 512
LRU_WIDTH = 1024
MIX_WIDTH = CONV_WIDTH + LRU_WIDTH
IN_COLS = 3 * CONV_WIDTH + 2 * LRU_WIDTH
D_FF = 4 * D_MODEL
GROUP = 256
EPS = 1e-6
LRU_C = 8.0
N_DEV = 8
SUB = 8

OFF_GB, OFF_GC, OFF_V, OFF_XR, OFF_G = 0, 512, 1024, 1536, 2560

ADAM_LR, ADAM_B1, ADAM_B2, ADAM_EPS, ADAM_WD, ADAM_STEP = 0.001, 0.9, 0.999, 1e-08, 0.01, 10
BC1 = 1.0 - ADAM_B1 ** ADAM_STEP
BC2 = 1.0 - ADAM_B2 ** ADAM_STEP

MIB = 1024 * 1024
MESH = pl.DeviceIdType.MESH

VEC_ROWS = 32
ROW_GF, ROW_GMLP, ROW_LOSS = 0, 1, 2
ROW_GNC, ROW_GNR, ROW_BR, ROW_BA, ROW_BX, ROW_LAM, ROW_CW, ROW_RW = 8, 9, 10, 11, 12, 13, 14, 17
ROW_GMIX = 24
ACC_GNC, ACC_GNR, ACC_BR, ACC_BA, ACC_BX, ACC_SP, ACC_CW, ACC_RW, N_ACC = 0, 1, 2, 3, 4, 5, 6, 9, 13


def _params(semantics=None, vmem_mib=48):
    return pltpu.CompilerParams(dimension_semantics=semantics, vmem_limit_bytes=vmem_mib * MIB)


def _rms(x):
    return lax.rsqrt(jnp.mean(x * x, axis=-1, keepdims=True) + EPS)


def _rms_bwd(dy, xhat, r, g):
    dyh = dy * g
    return r * (dyh - xhat * jnp.mean(dyh * xhat, axis=-1, keepdims=True))


def _sigmoid(x):
    return 0.5 + 0.5 * jnp.tanh(0.5 * x)


def _gelu(x):
    c0, c1 = 0.7978845608028654, 0.044715
    x2 = x * x
    t = jnp.tanh(x * (c0 + (c0 * c1) * x2))
    half = 0.5 + 0.5 * t
    ge = x * half
    dge = half + (ge - ge * half) * (2.0 * c0 + (6.0 * c0 * c1) * x2)
    return ge, dge


def _softplus_neg(lam):
    z = -lam
    e = jnp.exp(-jnp.abs(z))
    return jnp.maximum(z, 0.0) + jnp.where(e < 1e-4, e * (1.0 - 0.5 * e), jnp.log(1.0 + e))


def _lru_gates(pa, px, sp_c):
    ra = _sigmoid(pa)
    ii = _sigmoid(px)
    la = -ra * sp_c
    a = jnp.exp(la)
    x2 = 2.0 * la
    series = -x2 * (1.0 + x2 * (0.5 + x2 * (1.0 / 6.0 + x2 * (1.0 / 24.0))))
    m2 = jnp.where(x2 > -0.01, series, 1.0 - a * a)
    mult = jnp.where(m2 > 0.0, m2 * lax.rsqrt(m2), 0.0)
    return ra, ii, a, mult


def _down(cur, prev, s, row):
    return pltpu.roll(jnp.where(row < SUB - s, cur, prev), s, 0)


def _up(cur, nxt, s, row):
    return pltpu.roll(jnp.where(row >= s, cur, nxt), SUB - s, 0)


def _scan8_fwd(a, b, row):
    for s in (1, 2, 4):
        m = row >= s
        a_sh = pltpu.roll(a, s, 0)
        b_sh = pltpu.roll(b, s, 0)
        b = jnp.where(m, a * b_sh + b, b)
        a = jnp.where(m, a * a_sh, a)
    return a, b


def _scan8_rev(a, b, row):
    for s in (1, 2, 4):
        m = row < SUB - s
        a_sh = pltpu.roll(a, SUB - s, 0)
        b_sh = pltpu.roll(b, SUB - s, 0)
        b = jnp.where(m, a * b_sh + b, b)
        a = jnp.where(m, a * a_sh, a)
    return a, b


def _group_mask(shape):
    r = lax.broadcasted_iota(jnp.int32, shape, 0)
    c = lax.broadcasted_iota(jnp.int32, shape, 1)
    return ((r % GROUP) // HEAD_DIM) == (c // HEAD_DIM)


def _expand_heads(w):
    j = lax.broadcasted_iota(jnp.int32, (HEAD_DIM, GROUP), 0)
    c = lax.broadcasted_iota(jnp.int32, (HEAD_DIM, GROUP), 1)
    spread = (c % HEAD_DIM == j).astype(BF16)
    e = jnp.dot(w.astype(BF16), spread, preferred_element_type=F32)
    return jnp.where(_group_mask(e.shape), e, 0.0).astype(BF16)


def _fold_heads(p):
    p = jnp.where(_group_mask(p.shape), p, 0.0)
    c = lax.broadcasted_iota(jnp.int32, (GROUP, HEAD_DIM), 0)
    j = lax.broadcasted_iota(jnp.int32, (GROUP, HEAD_DIM), 1)
    fold = (c % HEAD_DIM == j).astype(BF16)
    hi = p.astype(BF16)
    rest = p - hi.astype(F32)
    mid = rest.astype(BF16)
    lo = (rest - mid.astype(F32)).astype(BF16)
    dot = functools.partial(jnp.dot, preferred_element_type=F32)
    return dot(hi, fold) + dot(mid, fold) + dot(lo, fold)


def _block_diag_apply(xb, wbd_ref):
    parts = [jnp.dot(xb[:, g * GROUP:(g + 1) * GROUP], wbd_ref[g * GROUP:(g + 1) * GROUP, :],
                     preferred_element_type=F32) for g in range(LRU_WIDTH // GROUP)]
    return jnp.concatenate(parts, axis=1)


def _block_diag_apply_t(db, wbd_ref):
    parts = [lax.dot_general(db[:, g * GROUP:(g + 1) * GROUP], wbd_ref[g * GROUP:(g + 1) * GROUP, :],
                             (((1,), (1,)), ((), ())), preferred_element_type=F32)
             for g in range(LRU_WIDTH // GROUP)]
    return jnp.concatenate(parts, axis=1)


def _dot_nt(a, b):
    return lax.dot_general(a, b, (((1,), (1,)), ((), ())), preferred_element_type=F32)


def _dot_tn(a, b):
    return lax.dot_general(a, b, (((0,), (0,)), ((), ())), preferred_element_type=F32)


CHUNKS_IN_FLIGHT = 8


def _chunk_loop(n_chunks, chunk, init):
    def body(k, carry):
        for j in range(CHUNKS_IN_FLIGHT):
            carry = chunk(k * CHUNKS_IN_FLIGHT + j, carry)
        return carry

    return lax.fori_loop(0, n_chunks // CHUNKS_IN_FLIGHT, body, init)


def _place():
    x, y, c = lax.axis_index("x"), lax.axis_index("y"), lax.axis_index("c")
    return x, y, c


def _block_id(chip, core):
    return 4 * chip[0] + 2 * chip[1] + core


def _other_chips(x, y):
    return [(1 - x, y), (x, 1 - y), (1 - x, 1 - y)]


def _remote_copy(src, dst, send_sem, recv_sem, to):
    return pltpu.make_async_remote_copy(src_ref=src, dst_ref=dst, send_sem=send_sem, recv_sem=recv_sem,
                                        device_id=to, device_id_type=MESH)


HBM_SPEC = pl.BlockSpec(memory_space=pl.ANY)


def _in_hbm(*arrays):
    return [pltpu.with_memory_space_constraint(a, pltpu.HBM) for a in arrays]


def _prep_shards(w_in_t, w_out, w_mlp_in, w_mlp_out, conv_w, rnn_conv_w):
    def body(win_ref, wout_ref, w1_ref, w2_ref, cw_ref, rw_ref, o_win, o_wout, o_w1, o_w2, o_cp):
        o_win[...] = win_ref[...].astype(BF16)
        o_wout[...] = wout_ref[...].astype(BF16)
        o_w1[...] = w1_ref[...].astype(BF16)
        o_w2[...] = w2_ref[...].astype(BF16)
        o_cp[...] = jnp.zeros(o_cp.shape, F32)
        o_cp[0:3, 0:64] = cw_ref[...]
        o_cp[3:7, :] = rw_ref[...]

    whole = lambda shape: pl.BlockSpec(shape, lambda i: (0,) * len(shape))
    args = (w_in_t, w_out, w_mlp_in, w_mlp_out, conv_w, rnn_conv_w)
    shapes = [(w_in_t.shape, BF16), (w_out.shape, BF16), (w_mlp_in.shape, BF16), (w_mlp_out.shape, BF16),
              ((8, 128), F32)]
    return pl.pallas_call(
        body, grid=(1,), out_shape=[jax.ShapeDtypeStruct(s, d) for s, d in shapes],
        in_specs=[whole(a.shape) for a in args], out_specs=[whole(s) for s, _ in shapes],
        compiler_params=_params(("arbitrary",), 40), name="prep_shards",
    )(*args)


def _host_all_gather(step, n_steps, shards, fulls, send_sems, recv_sems, local_sems):
    x, y, c = _place()
    me = (x, y, c)
    my_id = _block_id((x, y), c)
    sibling = (x, y, 1 - c)
    chips = _other_chips(x, y)
    n_arr = len(shards)

    def copy(arr, k, block, to, src=None):
        dst = fulls[arr].at[block]
        return _remote_copy(dst if src is None else src, dst, send_sems.at[arr, k], recv_sems.at[arr, k], to)

    def local(arr):
        return pltpu.make_async_copy(shards[arr], fulls[arr].at[my_id], local_sems.at[arr])

    @pl.when(step == 0)
    def _():
        for arr in range(n_arr):
            local(arr).start()
            copy(arr, 0, my_id, sibling, shards[arr]).start()
            for j, chip in enumerate(chips):
                copy(arr, 1 + j, my_id, (*chip, c), shards[arr]).start()

    @pl.when(step == max(n_steps - 2, 0))
    def _():
        for j, chip in enumerate(chips):
            for arr in range(n_arr):
                copy(arr, 1 + j, _block_id(chip, c), me).wait_recv()
                copy(arr, 4 + j, _block_id(chip, c), sibling).start()

    @pl.when(step == n_steps - 1)
    def _():
        for arr in range(n_arr):
            copy(arr, 0, _block_id((x, y), 1 - c), me).wait_recv()
            for j, chip in enumerate(chips):
                copy(arr, 4 + j, _block_id(chip, 1 - c), me).wait_recv()
            for k in range(4):
                copy(arr, k, my_id, me, shards[arr]).wait_send()
            for j, chip in enumerate(chips):
                copy(arr, 4 + j, _block_id(chip, c), me).wait_send()
            local(arr).wait()


def _host_pair_exchange(step, n_steps, gs, sibs, send_sems, recv_sems):
    x, y, c = _place()
    sibling = (x, y, 1 - c)
    chips = [(x, y)] + _other_chips(x, y)

    def d2d(arr, q):
        return _remote_copy(gs[arr].at[_block_id(chips[q], 1 - c)], sibs[arr].at[q],
                            send_sems.at[arr, q], recv_sems.at[arr, q], sibling)

    @pl.when(step == 0)
    def _():
        for arr in range(len(gs)):
            for q in (1, 2, 3, 0):
                d2d(arr, q).start()

    @pl.when(step == n_steps - 1)
    def _():
        for arr in range(len(gs)):
            for q in range(4):
                d2d(arr, q).wait()


def _host_chip_exchange(step, n_steps, hsends, hrecvs, send_sems, recv_sems):
    x, y, c = _place()
    chips = _other_chips(x, y)

    def ici(arr, j):
        return _remote_copy(hsends[arr].at[j], hrecvs[arr].at[j], send_sems.at[arr, j], recv_sems.at[arr, j],
                            (*chips[j], c))

    @pl.when(step == 0)
    def _():
        for arr in range(len(hsends)):
            for j in range(3):
                ici(arr, j).start()

    @pl.when(step == n_steps - 1)
    def _():
        for arr in range(len(hsends)):
            for j in range(3):
                ici(arr, j).wait()


def _host_half_exchange(step, n_steps, parts, sibs, send_sems, recv_sems):
    x, y, c = _place()
    n_q, rows2, _ = parts.shape
    half = rows2 // 2

    def d2d(q):
        src = parts.at[q, pl.ds(pl.multiple_of((1 - c) * half, 16), half), :]
        return _remote_copy(src, sibs.at[q], send_sems.at[q], recv_sems.at[q], (x, y, 1 - c))

    @pl.when(step == 0)
    def _():
        for q in range(n_q):
            d2d(q).start()

    @pl.when(step == n_steps - 1)
    def _():
        for q in range(n_q):
            d2d(q).wait()


def _peer(x, y, c, k):
    return (x ^ ((k >> 2) & 1), y ^ ((k >> 1) & 1), c ^ (k & 1))


def _host_small_exchange(step, n_steps, vec_m, vec_b, wab, vrecv_m, vrecv_b, wrecv, send_sems, recv_sems, local_sems):
    x, y, c = _place()
    my_id = _block_id((x, y), c)
    wrows = wab.shape[0] // N_DEV

    def copies(k):
        to = _peer(x, y, c, k)
        block = wab.at[pl.ds(pl.multiple_of(_block_id(to[0:2], to[2]) * wrows, SUB), wrows), :]
        return [_remote_copy(vec_m, vrecv_m.at[my_id], send_sems.at[0, k], recv_sems.at[0, k], to),
                _remote_copy(vec_b, vrecv_b.at[my_id], send_sems.at[1, k], recv_sems.at[1, k], to),
                _remote_copy(block, wrecv.at[k], send_sems.at[2, k], recv_sems.at[2, k], to)]

    mine = [pltpu.make_async_copy(vec_m, vrecv_m.at[my_id], local_sems.at[0]),
            pltpu.make_async_copy(vec_b, vrecv_b.at[my_id], local_sems.at[1])]

    @pl.when(step == 0)
    def _():
        for cp in mine:
            cp.start()
        for k in range(1, N_DEV):
            for cp in copies(k):
                cp.start()

    @pl.when(step == n_steps - 1)
    def _():
        for k in range(1, N_DEV):
            for cp in copies(k):
                cp.wait()
        for cp in mine:
            cp.wait()


def _pair_sum_parts(parts, sibs, core):
    n_q, rows2, cols = parts.shape
    half = rows2 // 2

    def body(core_ref, g_ref, s_ref, o_ref):
        o_ref[0] = (g_ref[0, 0].astype(F32) + s_ref[0].astype(F32)).astype(BF16)

    block = (1, half, cols)
    grid_spec = pltpu.PrefetchScalarGridSpec(
        num_scalar_prefetch=1, grid=(n_q,),
        in_specs=[pl.BlockSpec((1, 1, half, cols), lambda q, cr: (q, cr[0], 0, 0)),
                  pl.BlockSpec(block, lambda q, cr: (q, 0, 0))],
        out_specs=pl.BlockSpec(block, lambda q, cr: (q, 0, 0)))
    return pl.pallas_call(
        body, grid_spec=grid_spec, out_shape=pltpu.HBM((n_q, half, cols), BF16),
        compiler_params=_params(("arbitrary",), 32), name="pair_sum_w_in",
    )(core, *_in_hbm(parts.reshape(n_q, 2, half, cols), sibs))


def _pair_sum(gs, sibs, name):
    n_arr = len(gs)
    x, y, c = _place()
    slots = jnp.stack([_block_id(chip, c) for chip in [(x, y)] + _other_chips(x, y)]).astype(jnp.int32)

    def body(slots_ref, *refs):
        q = pl.program_id(0)
        for k in range(n_arr):
            g_ref, sib_ref = refs[2 * k:2 * k + 2]
            hs_ref, own_ref = refs[2 * n_arr + 2 * k:2 * n_arr + 2 * k + 2]
            both = g_ref[0].astype(F32) + sib_ref[0].astype(F32)

            @pl.when(q == 0)
            def _(own_ref=own_ref, both=both):
                own_ref[...] = both

            @pl.when(q > 0)
            def _(hs_ref=hs_ref, both=both):
                hs_ref[0] = both.astype(BF16)

    in_specs, out_specs, out_shape, args = [], [], [], []
    for g, sib in zip(gs, sibs):
        _, rows, cols = g.shape
        block = (1, rows, cols)
        in_specs += [pl.BlockSpec(block, lambda q, s: (s[q], 0, 0)), pl.BlockSpec(block, lambda q, s: (q, 0, 0))]
        out_specs += [pl.BlockSpec(block, lambda q, s: (jnp.maximum(q - 1, 0), 0, 0)),
                      pl.BlockSpec((rows, cols), lambda q, s: (0, 0))]
        out_shape += [pltpu.HBM((3, rows, cols), BF16), pltpu.HBM((rows, cols), F32)]
        args += _in_hbm(g, sib)
    grid_spec = pltpu.PrefetchScalarGridSpec(num_scalar_prefetch=1, grid=(4,), in_specs=in_specs, out_specs=out_specs)
    return pl.pallas_call(
        body, grid_spec=grid_spec, out_shape=out_shape,
        compiler_params=_params(("arbitrary",), 40), name=name,
    )(slots, *args)


def _exchange_scratch(n_arr, n_copies):
    return [pltpu.SemaphoreType.DMA((n_arr, n_copies)), pltpu.SemaphoreType.DMA((n_arr, n_copies))]


def _final_small(vrecv_m, vrecv_b, wab, wrecv, vec_x):
    wrows = wab.shape[0] // N_DEV

    def body(vm_ref, vb_ref, w_ref, wr_ref, vx_ref, o_vec, o_w, xrecv, wred, x_send, x_recv, b_send, b_recv):
        x, y, c = _place()
        my_id = _block_id((x, y), c)
        my_rows = pl.ds(pl.multiple_of(my_id * wrows, SUB), wrows)

        def xcopy(k):
            return _remote_copy(vx_ref, xrecv.at[my_id], x_send.at[k], x_recv.at[k], _peer(x, y, c, k))

        def bcopy(k):
            return _remote_copy(wred, o_w.at[my_rows, :], b_send.at[k], b_recv.at[k], _peer(x, y, c, k))

        xrecv[my_id] = vx_ref[...]
        for k in range(1, N_DEV):
            xcopy(k).start()
        red = w_ref[my_rows, :]
        for k in range(1, N_DEV):
            red = red + wr_ref[k]
        wred[...] = red
        o_w[my_rows, :] = red
        for k in range(1, N_DEV):
            bcopy(k).start()
        for k in range(1, N_DEV):
            xcopy(k).wait_recv()
        for rows, ref in ((slice(0, 8), vm_ref), (slice(8, 24), vb_ref), (slice(24, 32), xrecv)):
            tot = ref[0]
            for s in range(1, N_DEV):
                tot = tot + ref[s]
            o_vec[rows, :] = tot
        for k in range(1, N_DEV):
            bcopy(k).wait_recv()
        for k in range(1, N_DEV):
            xcopy(k).wait_send()
            bcopy(k).wait_send()

    vm = pl.BlockSpec(memory_space=pltpu.VMEM)
    dma8 = pltpu.SemaphoreType.DMA((N_DEV,))
    return pl.pallas_call(
        body, out_shape=(jax.ShapeDtypeStruct((VEC_ROWS, D_MODEL), F32), jax.ShapeDtypeStruct(wab.shape, F32)),
        in_specs=[vm] * 5, out_specs=[vm] * 2,
        scratch_shapes=[pltpu.VMEM((N_DEV, SUB, D_MODEL), F32), pltpu.VMEM((wrows, HEAD_DIM), F32),
                        dma8, dma8, dma8, dma8],
        compiler_params=_params(vmem_mib=32), name="final_small",
    )(vrecv_m, vrecv_b, wab, wrecv, vec_x)


def _in_proj(x, g_mix, shards, tm):
    t_len = x.shape[0]
    n_t = t_len // tm
    n_arr = len(shards)
    rows = [s.shape[0] for s in shards]
    width = 2 * rows[0]
    ax, ay = lax.axis_index("x"), lax.axis_index("y")
    order = jnp.stack([2 * cx + cy for cx, cy in [(ax, ay)] + _other_chips(ax, ay)]).astype(jnp.int32)

    def body(order_ref, x_ref, g_ref, *rest):
        shard_refs = rest[0:n_arr]
        u_ref, h_ref = rest[n_arr:n_arr + 2]
        fulls = rest[n_arr + 2:2 * n_arr + 2]
        h_s, wbuf, send_sems, recv_sems, local_sems, load_sem = rest[2 * n_arr + 2:]
        p = pl.program_id(0)
        i = pl.program_id(1)
        x_, y_, c = _place()
        me = (x_, y_, c)
        my_id = _block_id((x_, y_), c)
        sibling = (x_, y_, 1 - c)
        chips = _other_chips(x_, y_)

        def block(arr, blk):
            return fulls[arr].at[pl.ds(pl.multiple_of(blk * rows[arr], rows[arr]), rows[arr]), :]

        def copy(arr, k, blk, to, src=None):
            dst = block(arr, blk)
            return _remote_copy(dst if src is None else src, dst, send_sems.at[arr, k], recv_sems.at[arr, k], to)

        def local(arr):
            return pltpu.make_async_copy(shard_refs[arr], block(arr, my_id), local_sems.at[arr])

        def load_chip(chip, slot):
            start = pl.multiple_of((2 * chip[0] + chip[1]) * width, width)
            return pltpu.make_async_copy(fulls[0].at[pl.ds(start, width), :], wbuf.at[slot], load_sem.at[slot])

        def pass_on(j):
            for arr in range(n_arr):
                copy(arr, 1 + j, _block_id(chips[j], c), me).wait_recv()
                copy(arr, 4 + j, _block_id(chips[j], c), sibling).start()

        def complete(j):
            for arr in range(n_arr):
                copy(arr, 4 + j, _block_id(chips[j], 1 - c), me).wait_recv()

        @pl.when((p == 0) & (i == 0))
        def _():
            for arr in range(n_arr):
                local(arr).start()
                copy(arr, 0, my_id, sibling, shard_refs[arr]).start()
                for j in (0, 1):
                    copy(arr, 1 + j, my_id, (*chips[j], c), shard_refs[arr]).start()
            for arr in range(n_arr):
                local(arr).wait()
                copy(arr, 0, _block_id((x_, y_), 1 - c), me).wait_recv()
            load_chip((x_, y_), 0).start()
            load_chip((x_, y_), 0).wait()

        @pl.when((p == 1) & (i == 0))
        def _():
            pass_on(0)
            for arr in range(n_arr):
                copy(arr, 3, my_id, (*chips[2], c), shard_refs[arr]).start()
            pass_on(1)
            complete(0)
            load_chip(chips[0], 1).start()
            load_chip(chips[0], 1).wait()
            complete(1)
            load_chip(chips[1], 0).start()

        @pl.when((p == 2) & (i == 0))
        def _():
            load_chip(chips[1], 0).wait()

        @pl.when((p == 3) & (i == 0))
        def _():
            pass_on(2)
            complete(2)
            load_chip(chips[2], 1).start()
            load_chip(chips[2], 1).wait()

        @pl.when((p == 3) & (i == n_t - 1))
        def _():
            for arr in range(n_arr):
                for k in range(4):
                    copy(arr, k, my_id, me, shard_refs[arr]).wait_send()
                for j, chip in enumerate(chips):
                    copy(arr, 4 + j, _block_id(chip, c), me).wait_send()

        tile = pl.ds(pl.multiple_of(i * tm, tm), tm)

        @pl.when(p == 0)
        def _():
            xv = x_ref[...]
            h = (xv * _rms(xv) * g_ref[...]).astype(BF16)
            h_ref[...] = h
            h_s[tile, :] = h

        for slot in (0, 1):
            @pl.when(p % 2 == slot)
            def _(slot=slot):
                u_ref[...] = _dot_nt(h_s[tile, :], wbuf[slot])

    first_pass = lambda p, i, o: (jnp.where(p == 0, i, n_t - 1), 0)
    grid_spec = pltpu.PrefetchScalarGridSpec(
        num_scalar_prefetch=1, grid=(4, n_t),
        in_specs=[pl.BlockSpec((tm, D_MODEL), first_pass), pl.BlockSpec((1, D_MODEL), lambda p, i, o: (0, 0))]
        + [HBM_SPEC] * n_arr,
        out_specs=[pl.BlockSpec((tm, width), lambda p, i, o: (i, o[p])), pl.BlockSpec((tm, D_MODEL), first_pass)]
        + [HBM_SPEC] * n_arr,
        scratch_shapes=[pltpu.VMEM((t_len, D_MODEL), BF16), pltpu.VMEM((2, width, D_MODEL), BF16)]
        + _exchange_scratch(n_arr, 7) + [pltpu.SemaphoreType.DMA((n_arr,)), pltpu.SemaphoreType.DMA((2,))])
    return pl.pallas_call(
        body, grid_spec=grid_spec,
        out_shape=[jax.ShapeDtypeStruct((t_len, IN_COLS), F32), jax.ShapeDtypeStruct((t_len, D_MODEL), BF16)]
        + [jax.ShapeDtypeStruct((N_DEV * s.shape[0], s.shape[1]), s.dtype) for s in shards],
        compiler_params=_params(("arbitrary", "arbitrary"), 48), name="in_proj",
    )(order, x, g_mix, *shards)


def _conv3_chunk(u_ref, r, cv_prev, cw, row):
    gb = u_ref[pl.ds(r, SUB), OFF_GB:OFF_GB + CONV_WIDTH]
    gc = u_ref[pl.ds(r, SUB), OFF_GC:OFF_GC + CONV_WIDTH]
    v = u_ref[pl.ds(r, SUB), OFF_V:OFF_V + CONV_WIDTH]
    cv = gc * v
    cv_m1 = _down(cv, cv_prev, 1, row)
    cv_m2 = _down(cv, cv_prev, 2, row)
    cq = cw[2:3, :] * cv + cw[1:2, :] * cv_m1 + cw[0:1, :] * cv_m2
    return gb, gc, v, cv, cv_m1, cv_m2, cq


def _conv4_chunk(u_ref, r, xin_prev, rw, rb, row):
    xin = u_ref[pl.ds(r, SUB), OFF_XR:OFF_XR + LRU_WIDTH]
    m1 = _down(xin, xin_prev, 1, row)
    m2 = _down(xin, xin_prev, 2, row)
    m3 = _down(xin, xin_prev, 3, row)
    xr = rw[3:4, :] * xin + rw[2:3, :] * m1 + rw[1:2, :] * m2 + rw[0:1, :] * m3 + rb
    return xin, m1, m2, m3, xr


def _mixer_fwd(u, conv_w, rnn_conv_w, rnn_conv_b, wa, b_a, wx, b_x, lam, gnc, gnr, shards, tm):
    t_len = u.shape[0]
    n_steps = t_len // tm
    n_chunks = tm // SUB
    n_arr = len(shards)

    def body(u_ref, cw_ref, rw_ref, rb_ref, wa_ref, ba_ref, wx_ref, bx_ref, lam_ref, gnc_ref, gnr_ref, *rest):
        shard_refs = rest[0:n_arr]
        hs_ref, y_ref, xr_s, ra_ref, ii_ref, mult_ref = rest[n_arr:n_arr + 6]
        fulls = rest[n_arr + 6:2 * n_arr + 6]
        (y_s, pa_s, px_s, wabd, wxbd, cv_car, xin_car, h_car,
         send_sems, recv_sems, local_sems) = rest[2 * n_arr + 6:]
        _host_all_gather(pl.program_id(0), n_steps, shard_refs, fulls, send_sems, recv_sems, local_sems)

        @pl.when(pl.program_id(0) == 0)
        def _():
            cv_car[...] = jnp.zeros(cv_car.shape, F32)
            xin_car[...] = jnp.zeros(xin_car.shape, F32)
            h_car[...] = jnp.zeros(h_car.shape, F32)
            wabd[...] = _expand_heads(wa_ref[...])
            wxbd[...] = _expand_heads(wx_ref[...])

        row_c = lax.broadcasted_iota(jnp.int32, (SUB, CONV_WIDTH), 0)
        row_r = lax.broadcasted_iota(jnp.int32, (SUB, LRU_WIDTH), 0)
        cw = cw_ref[...]
        rw = rw_ref[...]
        rb = rb_ref[...]
        g_c = gnc_ref[...]
        g_r = gnr_ref[...]
        sp_c = LRU_C * _softplus_neg(lam_ref[...])

        def convs(i, carry):
            cv_prev, xin_prev = carry
            r = pl.multiple_of(i * SUB, SUB)
            gb, _, _, cv, _, _, cq = _conv3_chunk(u_ref, r, cv_prev, cw, row_c)
            y_c = gb * cq
            y_s[pl.ds(r, SUB), 0:CONV_WIDTH] = y_c * _rms(y_c) * g_c
            xin, _, _, _, xr = _conv4_chunk(u_ref, r, xin_prev, rw, rb, row_r)
            xr_s[pl.ds(r, SUB), :] = xr
            return cv, xin

        cv_last, xin_last = _chunk_loop(n_chunks, convs, (cv_car[...], xin_car[...]))
        cv_car[...] = cv_last
        xin_car[...] = xin_last

        xrb = xr_s[...].astype(BF16)
        pa_s[...] = _block_diag_apply(xrb, wabd) + ba_ref[...]
        px_s[...] = _block_diag_apply(xrb, wxbd) + bx_ref[...]

        def recur(i, h_prev):
            r = pl.multiple_of(i * SUB, SUB)
            xr = xr_s[pl.ds(r, SUB), :]
            ra, ii, a, mult = _lru_gates(pa_s[pl.ds(r, SUB), :], px_s[pl.ds(r, SUB), :], sp_c)
            ra_ref[pl.ds(r, SUB), :] = ra
            ii_ref[pl.ds(r, SUB), :] = ii
            mult_ref[pl.ds(r, SUB), :] = mult
            a_cum, b_cum = _scan8_fwd(a, mult * ii * xr, row_r)
            h = a_cum * h_prev + b_cum
            hs_ref[pl.ds(r, SUB), :] = h
            ge, _ = _gelu(u_ref[pl.ds(r, SUB), OFF_G:OFF_G + LRU_WIDTH])
            y_r = h * ge
            y_s[pl.ds(r, SUB), CONV_WIDTH:MIX_WIDTH] = y_r * _rms(y_r) * g_r
            return h[SUB - 1:SUB, :]

        h_car[...] = _chunk_loop(n_chunks, recur, h_car[...])

        y_ref[...] = y_s[...].astype(BF16)

    row_tile = lambda w: pl.BlockSpec((tm, w), lambda i: (i, 0))
    whole = lambda a: pl.BlockSpec(a.shape, lambda i: (0,) * a.ndim)
    smalls = (conv_w, rnn_conv_w, rnn_conv_b, wa, b_a, wx, b_x, lam, gnc, gnr)
    return pl.pallas_call(
        body, grid=(n_steps,),
        in_specs=[row_tile(IN_COLS)] + [whole(a) for a in smalls] + [HBM_SPEC] * n_arr,
        out_specs=[row_tile(LRU_WIDTH), row_tile(MIX_WIDTH)] + [row_tile(LRU_WIDTH)] * 4 + [HBM_SPEC] * n_arr,
        out_shape=[jax.ShapeDtypeStruct((t_len, LRU_WIDTH), F32), jax.ShapeDtypeStruct((t_len, MIX_WIDTH), BF16)]
        + [jax.ShapeDtypeStruct((t_len, LRU_WIDTH), F32)] * 4
        + [jax.ShapeDtypeStruct((N_DEV,) + s.shape, BF16) for s in shards],
        scratch_shapes=[pltpu.VMEM((tm, MIX_WIDTH), F32),
                        pltpu.VMEM((tm, LRU_WIDTH), F32), pltpu.VMEM((tm, LRU_WIDTH), F32),
                        pltpu.VMEM((LRU_WIDTH, GROUP), BF16), pltpu.VMEM((LRU_WIDTH, GROUP), BF16),
                        pltpu.VMEM((SUB, CONV_WIDTH), F32), pltpu.VMEM((SUB, LRU_WIDTH), F32),
                        pltpu.VMEM((1, LRU_WIDTH), F32)]
        + _exchange_scratch(n_arr, 7) + [pltpu.SemaphoreType.DMA((n_arr,))],
        compiler_params=_params(("arbitrary",), 56), name="mixer_fwd",
    )(u, *smalls, *shards)


def _mlp_up(x, y, g_mlp, w_out, w1, w2_shard, tm):
    t_len = x.shape[0]
    n_steps = t_len // tm
    n_blk, _, blk = w1.shape

    def body(x_ref, y_ref, gm_ref, wout_hbm, w1_hbm, w2_ref, x1_ref, h2_ref, z_ref, w2_full,
             wout_s, w1_s, sem, send_sems, recv_sems, local_sems):
        step = pl.program_id(0)
        _host_all_gather(step, n_steps, [w2_ref], [w2_full], send_sems, recv_sems, local_sems)

        load_wout = pltpu.make_async_copy(wout_hbm, wout_s, sem.at[0])
        load_w1 = pltpu.make_async_copy(w1_hbm, w1_s, sem.at[1])

        @pl.when(step == 0)
        def _():
            load_wout.start()
            load_w1.start()
            load_wout.wait()

        x1v = x_ref[...] + jnp.dot(y_ref[...], wout_s[...], preferred_element_type=F32)
        x1_ref[...] = x1v
        h2 = (x1v * _rms(x1v) * gm_ref[...]).astype(BF16)
        h2_ref[...] = h2

        @pl.when(step == 0)
        def _():
            load_w1.wait()

        for k in range(n_blk):
            rp = jnp.maximum(jnp.dot(h2, w1_s[k], preferred_element_type=F32), 0.0)
            z_ref[:, k * blk:(k + 1) * blk] = (rp * rp).astype(BF16)

    row_tile = lambda w: pl.BlockSpec((tm, w), lambda i: (i, 0))
    return pl.pallas_call(
        body, grid=(n_steps,),
        in_specs=[row_tile(D_MODEL), row_tile(MIX_WIDTH), pl.BlockSpec((1, D_MODEL), lambda i: (0, 0)),
                  HBM_SPEC, HBM_SPEC, HBM_SPEC],
        out_specs=[row_tile(D_MODEL), row_tile(D_MODEL), row_tile(D_FF), HBM_SPEC],
        out_shape=[jax.ShapeDtypeStruct((t_len, D_MODEL), F32), jax.ShapeDtypeStruct((t_len, D_MODEL), BF16),
                   jax.ShapeDtypeStruct((t_len, D_FF), BF16), jax.ShapeDtypeStruct((N_DEV,) + w2_shard.shape, BF16)],
        scratch_shapes=[pltpu.VMEM(w_out.shape, BF16), pltpu.VMEM(w1.shape, BF16), pltpu.SemaphoreType.DMA((2,))]
        + _exchange_scratch(1, 7) + [pltpu.SemaphoreType.DMA((1,))],
        compiler_params=_params(("arbitrary",), 48), name="mlp_up",
    )(x, y, g_mlp, w_out, w1, w2_shard)


def _mlp_down_bwd(x1, z, target, g_mlp, g_f, w1, w2, tm):
    t_len = x1.shape[0]
    n_steps = t_len // tm
    n_blk, _, blk = w1.shape

    def body(x1_ref, z_ref, tg_ref, gm_ref, gf_ref, w1_hbm, w2_hbm, dx1_ref, dx2_ref, vec_ref, dpre_hbm,
             w1_s, w2_s, dp_s, sem, out_sem):
        step = pl.program_id(0)
        rows = pl.ds(pl.multiple_of(step * tm, tm), tm)
        dp_out = pltpu.make_async_copy(dp_s, dpre_hbm.at[rows, :], out_sem.at[0])

        load_w1 = pltpu.make_async_copy(w1_hbm, w1_s, sem.at[0])
        load_w2 = pltpu.make_async_copy(w2_hbm, w2_s, sem.at[1])

        @pl.when(step == 0)
        def _():
            load_w2.start()
            load_w1.start()
            vec_ref[...] = jnp.zeros(vec_ref.shape, F32)
            load_w2.wait()

        x1v = x1_ref[...]
        g_m = gm_ref[...]
        g_o = gf_ref[...]
        r2 = _rms(x1v)
        x1h = x1v * r2
        x2 = x1v + jnp.dot(z_ref[...], w2_s[...], preferred_element_type=F32)
        r3 = _rms(x2)
        x2h = x2 * r3
        err = x2h * g_o - tg_ref[...]
        dout = err * (1.0 / D_MODEL)
        vec_ref[ROW_LOSS:ROW_LOSS + 1, :] += (0.5 / D_MODEL) * jnp.sum(err * err, axis=0, keepdims=True)
        vec_ref[ROW_GF:ROW_GF + 1, :] += jnp.sum(dout * x2h, axis=0, keepdims=True)
        dx2 = _rms_bwd(dout, x2h, r3, g_o)
        dx2b = dx2.astype(BF16)
        dx2_ref[...] = dx2b
        dh2 = jnp.zeros((tm, D_MODEL), F32)

        @pl.when(step > 0)
        def _():
            dp_out.wait()

        @pl.when(step == 0)
        def _():
            load_w1.wait()

        for k in range(n_blk):
            cols = slice(k * blk, (k + 1) * blk)
            dz = _dot_nt(dx2b, w2_s[cols, :])
            dpb = (dz * 2.0 * jnp.sqrt(z_ref[:, cols].astype(F32))).astype(BF16)
            dp_s[:, cols] = dpb
            dh2 = dh2 + _dot_nt(dpb, w1_s[k])
        dp_out.start()
        vec_ref[ROW_GMLP:ROW_GMLP + 1, :] += jnp.sum(dh2 * x1h, axis=0, keepdims=True)
        dx1_ref[...] = dx2 + _rms_bwd(dh2, x1h, r2, g_m)

        @pl.when(step == n_steps - 1)
        def _():
            dp_out.wait()

    row_tile = lambda w: pl.BlockSpec((tm, w), lambda i: (i, 0))
    vec_spec = pl.BlockSpec((1, D_MODEL), lambda i: (0, 0))
    return pl.pallas_call(
        body, grid=(n_steps,),
        in_specs=[row_tile(D_MODEL), row_tile(D_FF), row_tile(D_MODEL), vec_spec, vec_spec, HBM_SPEC, HBM_SPEC],
        out_specs=[row_tile(D_MODEL), row_tile(D_MODEL), pl.BlockSpec((SUB, D_MODEL), lambda i: (0, 0)), HBM_SPEC],
        out_shape=[jax.ShapeDtypeStruct((t_len, D_MODEL), F32), jax.ShapeDtypeStruct((t_len, D_MODEL), BF16),
                   jax.ShapeDtypeStruct((SUB, D_MODEL), F32), jax.ShapeDtypeStruct((t_len, D_FF), BF16)],
        scratch_shapes=[pltpu.VMEM(w1.shape, BF16), pltpu.VMEM(w2.shape, BF16), pltpu.VMEM((tm, D_FF), BF16),
                        pltpu.SemaphoreType.DMA((2,)), pltpu.SemaphoreType.DMA((1,))],
        compiler_params=_params(("arbitrary",), 56), name="mlp_down_bwd",
    )(x1, z, target, g_mlp, g_f, w1, w2)


def _mixer_bwd(u, hs, dx1, saved, conv_w, rnn_conv_w, rnn_conv_b, wa, wx, lam, gnc, gnr, w_out,
               chip_sums, g_wout, tm):
    t_len = u.shape[0]
    n_tiles = t_len // tm
    n_chunks = tm // SUB
    per_tile = tm // SUB
    n_sums = len(chip_sums)

    def body(u_ref, up_ref, hs_ref, hp_ref, dx1_ref, xr_ref, ra_ref, ii_ref, mult_ref,
             cw_ref, rw_ref, rb_ref, wa_ref, wx_ref, lam_ref, gnc_ref, gnr_ref, wout_ref, *rest):
        hsends = rest[0:n_sums]
        gwout_ref = rest[n_sums]
        du_ref, vec_ref, wab_ref = rest[n_sums + 1:n_sums + 4]
        hrecvs = rest[n_sums + 4:2 * n_sums + 4]
        sib_wout = rest[2 * n_sums + 4]
        (du_s, dy_s, dpa_s, dpx_s, dxr_s, wabd, wxbd, acc, dwa_acc, dwx_acc,
         a_car, dh_car, dcq_car, dxr_car, i_send, i_recv, d_send, d_recv) = rest[2 * n_sums + 5:]
        step = pl.program_id(0)
        _host_chip_exchange(step, n_tiles, hsends, hrecvs, i_send, i_recv)
        _host_pair_exchange(step, n_tiles, [gwout_ref], [sib_wout], d_send, d_recv)
        has_prev = (step < n_tiles - 1).astype(F32)

        @pl.when(step == 0)
        def _():
            acc[...] = jnp.zeros(acc.shape, F32)
            dwa_acc[...] = jnp.zeros(dwa_acc.shape, F32)
            dwx_acc[...] = jnp.zeros(dwx_acc.shape, F32)
            a_car[...] = jnp.ones(a_car.shape, F32)
            dh_car[...] = jnp.zeros(dh_car.shape, F32)
            dcq_car[...] = jnp.zeros(dcq_car.shape, F32)
            dxr_car[...] = jnp.zeros(dxr_car.shape, F32)
            wabd[...] = _expand_heads(wa_ref[...])
            wxbd[...] = _expand_heads(wx_ref[...])

        row_c = lax.broadcasted_iota(jnp.int32, (SUB, CONV_WIDTH), 0)
        row_r = lax.broadcasted_iota(jnp.int32, (SUB, LRU_WIDTH), 0)
        cw = cw_ref[...]
        rw = rw_ref[...]
        rb = rb_ref[...]
        g_c = gnc_ref[...]
        g_r = gnr_ref[...]
        sp_c = LRU_C * _softplus_neg(lam_ref[...])

        up = up_ref[...] * has_prev
        cv_before = up[:, OFF_GC:OFF_GC + CONV_WIDTH] * up[:, OFF_V:OFF_V + CONV_WIDTH]
        xin_before = up[:, OFF_XR:OFF_XR + LRU_WIDTH]
        hs_before = hp_ref[...] * has_prev

        dy_s[...] = _dot_nt(dx1_ref[...].astype(BF16), wout_ref[...])

        xrb = xr_ref[...].astype(BF16)

        def recur_bwd(j, carry):
            a_later, dh_later = carry
            i = n_chunks - 1 - j
            r = pl.multiple_of(i * SUB, SUB)
            rp = pl.multiple_of(jnp.maximum(i - 1, 0) * SUB, SUB)
            xr = xr_ref[pl.ds(r, SUB), :]
            hs_c = hs_ref[pl.ds(r, SUB), :]
            hs_prev = jnp.where(i == 0, hs_before, hs_ref[pl.ds(rp, SUB), :])
            h_m1 = _down(hs_c, hs_prev, 1, row_r)
            ra = ra_ref[pl.ds(r, SUB), :]
            ii = ii_ref[pl.ds(r, SUB), :]
            mult = mult_ref[pl.ds(r, SUB), :]
            a = jnp.exp(-ra * sp_c)
            inv_mult = lax.rsqrt(mult * mult)
            ge, dge = _gelu(u_ref[pl.ds(r, SUB), OFF_G:OFF_G + LRU_WIDTH])
            y_r = hs_c * ge
            rr = _rms(y_r)
            yhat = y_r * rr
            dyn = dy_s[pl.ds(r, SUB), CONV_WIDTH:MIX_WIDTH]
            acc[ACC_GNR] += dyn * yhat
            dy_r = _rms_bwd(dyn, yhat, rr, g_r)
            du_s[pl.ds(r, SUB), OFF_G:OFF_G + LRU_WIDTH] = dy_r * hs_c * dge
            a_cum, d_cum = _scan8_rev(_up(a, a_later, 1, row_r), dy_r * ge, row_r)
            dh = a_cum * dh_later + d_cum
            dm = dh * mult
            dii = dm * xr
            dxr_s[pl.ds(r, SUB), :] = dm * ii
            dla = a * dh * (h_m1 - (ii * xr) * a * inv_mult)
            dla_r = dla * ra
            acc[ACC_SP] -= dla_r
            dpa = dla_r * (sp_c * (ra - 1.0))
            dpx = dii * ii * (1.0 - ii)
            acc[ACC_BA] += dpa
            acc[ACC_BX] += dpx
            dpa_s[pl.ds(r, SUB), :] = dpa
            dpx_s[pl.ds(r, SUB), :] = dpx
            return a, dh[0:1, :]

        a_first, dh_first = _chunk_loop(n_chunks, recur_bwd, (a_car[...], dh_car[...]))
        a_car[...] = a_first
        dh_car[...] = dh_first

        dpab = dpa_s[...].astype(BF16)
        dpxb = dpx_s[...].astype(BF16)
        dxr_s[...] += _block_diag_apply_t(dpab, wabd) + _block_diag_apply_t(dpxb, wxbd)
        for g in range(LRU_WIDTH // GROUP):
            cols = slice(g * GROUP, (g + 1) * GROUP)
            dwa_acc[cols, :] += _dot_tn(xrb[:, cols], dpab[:, cols])
            dwx_acc[cols, :] += _dot_tn(xrb[:, cols], dpxb[:, cols])

        def convs_bwd(j, carry):
            dcq_later, dxr_later = carry
            i = n_chunks - 1 - j
            r = pl.multiple_of(i * SUB, SUB)
            rp = pl.multiple_of(jnp.maximum(i - 1, 0) * SUB, SUB)
            cv_prev = jnp.where(i == 0, cv_before,
                                u_ref[pl.ds(rp, SUB), OFF_GC:OFF_GC + CONV_WIDTH]
                                * u_ref[pl.ds(rp, SUB), OFF_V:OFF_V + CONV_WIDTH])
            gb, gc, v, cv, cv_m1, cv_m2, cq = _conv3_chunk(u_ref, r, cv_prev, cw, row_c)
            y_c = gb * cq
            rc = _rms(y_c)
            yhat = y_c * rc
            dyn = dy_s[pl.ds(r, SUB), 0:CONV_WIDTH]
            acc[ACC_GNC, :, 0:CONV_WIDTH] += dyn * yhat
            dy_c = _rms_bwd(dyn, yhat, rc, g_c)
            dcq = dy_c * gb
            dcv = (cw[2:3, :] * dcq + cw[1:2, :] * _up(dcq, dcq_later, 1, row_c)
                   + cw[0:1, :] * _up(dcq, dcq_later, 2, row_c))
            acc[ACC_CW + 2, :, 0:CONV_WIDTH] += dcq * cv
            acc[ACC_CW + 1, :, 0:CONV_WIDTH] += dcq * cv_m1
            acc[ACC_CW + 0, :, 0:CONV_WIDTH] += dcq * cv_m2
            du_s[pl.ds(r, SUB), OFF_GB:OFF_GB + CONV_WIDTH] = dy_c * cq
            du_s[pl.ds(r, SUB), OFF_GC:OFF_GC + CONV_WIDTH] = dcv * v
            du_s[pl.ds(r, SUB), OFF_V:OFF_V + CONV_WIDTH] = dcv * gc

            xin_prev = jnp.where(i == 0, xin_before, u_ref[pl.ds(rp, SUB), OFF_XR:OFF_XR + LRU_WIDTH])
            xin, m1, m2, m3, _ = _conv4_chunk(u_ref, r, xin_prev, rw, rb, row_r)
            dxr = dxr_s[pl.ds(r, SUB), :]
            du_s[pl.ds(r, SUB), OFF_XR:OFF_XR + LRU_WIDTH] = (
                rw[3:4, :] * dxr + rw[2:3, :] * _up(dxr, dxr_later, 1, row_r)
                + rw[1:2, :] * _up(dxr, dxr_later, 2, row_r) + rw[0:1, :] * _up(dxr, dxr_later, 3, row_r))
            acc[ACC_RW + 3] += dxr * xin
            acc[ACC_RW + 2] += dxr * m1
            acc[ACC_RW + 1] += dxr * m2
            acc[ACC_RW + 0] += dxr * m3
            acc[ACC_BR] += dxr
            return dcq, dxr

        dcq_first, dxr_first = _chunk_loop(n_chunks, convs_bwd, (dcq_car[...], dxr_car[...]))
        dcq_car[...] = dcq_first
        dxr_car[...] = dxr_first

        du_ref[...] = du_s[...].astype(BF16)

        @pl.when(step == n_tiles - 1)
        def _():
            vec_ref[...] = jnp.zeros(vec_ref.shape, F32)
            rows = {ACC_GNC: ROW_GNC, ACC_GNR: ROW_GNR, ACC_BR: ROW_BR, ACC_BA: ROW_BA, ACC_BX: ROW_BX}
            for k in range(3):
                rows[ACC_CW + k] = ROW_CW + k
            for k in range(4):
                rows[ACC_RW + k] = ROW_RW + k
            for slot, out_row in rows.items():
                o = out_row - ROW_GNC
                vec_ref[o:o + 1, :] = jnp.sum(acc[slot], axis=0, keepdims=True)
            lam_v = lam_ref[...]
            dsp = jnp.sum(acc[ACC_SP], axis=0, keepdims=True)
            o = ROW_LAM - ROW_GNC
            vec_ref[o:o + 1, :] = -dsp * LRU_C / (1.0 + jnp.exp(lam_v))
            wab_ref[0:LRU_WIDTH, :] = _fold_heads(dwa_acc[...])
            wab_ref[LRU_WIDTH:2 * LRU_WIDTH, :] = _fold_heads(dwx_acc[...])

    rev = lambda w: pl.BlockSpec((tm, w), lambda s: (n_tiles - 1 - s, 0))
    before = lambda w: pl.BlockSpec((SUB, w), lambda s: (jnp.maximum((n_tiles - 1 - s) * per_tile - 1, 0), 0))
    whole = lambda a: pl.BlockSpec(a.shape, lambda s: (0,) * a.ndim)
    smalls = (conv_w, rnn_conv_w, rnn_conv_b, wa, wx, lam, gnc, gnr, w_out)
    full = lambda w: pltpu.VMEM((tm, w), F32)
    return pl.pallas_call(
        body, grid=(n_tiles,),
        in_specs=[rev(IN_COLS), before(IN_COLS), rev(LRU_WIDTH), before(LRU_WIDTH), rev(D_MODEL)]
        + [rev(LRU_WIDTH)] * len(saved) + [whole(a) for a in smalls] + [HBM_SPEC] * (n_sums + 1),
        out_specs=[rev(IN_COLS), pl.BlockSpec((16, D_MODEL), lambda s: (0, 0)),
                   pl.BlockSpec((2 * LRU_WIDTH, HEAD_DIM), lambda s: (0, 0))] + [HBM_SPEC] * (n_sums + 1),
        out_shape=[jax.ShapeDtypeStruct((t_len, IN_COLS), BF16), jax.ShapeDtypeStruct((16, D_MODEL), F32),
                   jax.ShapeDtypeStruct((2 * LRU_WIDTH, HEAD_DIM), F32)]
        + [jax.ShapeDtypeStruct(s.shape, BF16) for s in chip_sums]
        + [jax.ShapeDtypeStruct((4,) + g_wout.shape[1:], BF16)],
        scratch_shapes=[full(IN_COLS), full(MIX_WIDTH), full(LRU_WIDTH), full(LRU_WIDTH), full(LRU_WIDTH),
                        pltpu.VMEM((LRU_WIDTH, GROUP), BF16), pltpu.VMEM((LRU_WIDTH, GROUP), BF16),
                        pltpu.VMEM((N_ACC, SUB, LRU_WIDTH), F32),
                        pltpu.VMEM((LRU_WIDTH, GROUP), F32), pltpu.VMEM((LRU_WIDTH, GROUP), F32),
                        pltpu.VMEM((SUB, LRU_WIDTH), F32), pltpu.VMEM((1, LRU_WIDTH), F32),
                        pltpu.VMEM((SUB, CONV_WIDTH), F32), pltpu.VMEM((SUB, LRU_WIDTH), F32)]
        + _exchange_scratch(n_sums, 3) + _exchange_scratch(1, 4),
        compiler_params=_params(("arbitrary",), 56), name="mixer_bwd",
    )(u, u, hs, hs, dx1, *saved, *smalls, *chip_sums, g_wout)


def _in_proj_bwd(du, dx1, x, g_mix, win_t, tm, chip_sums, g_own):
    t_len = x.shape[0]
    n_steps = t_len // tm

    def body(du_ref, dx1_ref, x_ref, g_ref, w_ref, hs_ref, gown_ref,
             dx_ref, vec_ref, landed_ref, sib_ref, i_send, i_recv, d_send, d_recv):
        step = pl.program_id(0)
        _host_chip_exchange(step, n_steps, [hs_ref], [landed_ref], i_send, i_recv)
        _host_half_exchange(step, n_steps, gown_ref, sib_ref, d_send, d_recv)

        @pl.when(step == 0)
        def _():
            vec_ref[...] = jnp.zeros(vec_ref.shape, F32)

        dh = jnp.dot(du_ref[...], w_ref[...], preferred_element_type=F32)
        xv = x_ref[...]
        r1 = _rms(xv)
        xh = xv * r1
        vec_ref[0:1, :] += jnp.sum(dh * xh, axis=0, keepdims=True)
        dx_ref[...] = dx1_ref[...] + _rms_bwd(dh, xh, r1, g_ref[...])

    row_tile = lambda w: pl.BlockSpec((tm, w), lambda i: (i, 0))
    half_shape = (g_own.shape[0], g_own.shape[1] // 2, g_own.shape[2])
    return pl.pallas_call(
        body, grid=(n_steps,),
        in_specs=[row_tile(IN_COLS), row_tile(D_MODEL), row_tile(D_MODEL), pl.BlockSpec((1, D_MODEL), lambda i: (0, 0)),
                  pl.BlockSpec((IN_COLS, D_MODEL), lambda i: (0, 0))] + [HBM_SPEC] * 2,
        out_specs=[row_tile(D_MODEL), pl.BlockSpec((SUB, D_MODEL), lambda i: (0, 0))] + [HBM_SPEC] * 2,
        out_shape=[jax.ShapeDtypeStruct((t_len, D_MODEL), F32), jax.ShapeDtypeStruct((SUB, D_MODEL), F32),
                   jax.ShapeDtypeStruct(chip_sums.shape, BF16), jax.ShapeDtypeStruct(half_shape, BF16)],
        scratch_shapes=_exchange_scratch(1, 3) + [pltpu.SemaphoreType.DMA((1,)), pltpu.SemaphoreType.DMA((1,))],
        compiler_params=_params(("arbitrary",), 56), name="in_proj_bwd",
    )(du, dx1, x, g_mix, win_t, chip_sums, g_own)


def _tn_weight_grad(a, b, tk, name, pair=(), col_blocks=1):
    t_len, m = a.shape
    n = b.shape[1]
    n_steps = t_len // tk
    sent = tuple(pair)
    n_sent = len(sent)

    def body(a_ref, b_ref, *rest):
        srcs = rest[0:n_sent]
        o_ref = rest[n_sent]
        dsts = rest[n_sent + 1:2 * n_sent + 1]
        acc = rest[2 * n_sent + 1]
        sems = rest[2 * n_sent + 2:]
        j = pl.program_id(0)
        if pair:
            _host_pair_exchange(j, n_steps, srcs, dsts, *sems)

        @pl.when(j == 0)
        def _():
            acc[...] = jnp.zeros(acc.shape, F32)

        acc[...] += _dot_tn(a_ref[...].astype(BF16), b_ref[...].astype(BF16))

        @pl.when(j == n_steps - 1)
        def _():
            if col_blocks == 1:
                o_ref[...] = acc[...].astype(BF16)
            else:
                for k in range(col_blocks):
                    o_ref[k] = acc[:, k * nb:(k + 1) * nb].astype(BF16)

    nb = n // col_blocks
    out_dims = (m, n) if col_blocks == 1 else (col_blocks, m, nb)
    landed = [jax.ShapeDtypeStruct((4,) + g.shape[1:], BF16) for g in pair]
    scratch = [pltpu.VMEM((m, n), F32)]
    if n_sent:
        scratch += _exchange_scratch(n_sent, 4)
    return pl.pallas_call(
        body, grid=(n_steps,),
        in_specs=[pl.BlockSpec((tk, m), lambda j: (j, 0)), pl.BlockSpec((tk, n), lambda j: (j, 0))]
        + [HBM_SPEC] * n_sent,
        out_specs=[pl.BlockSpec(out_dims, lambda j: (0,) * len(out_dims))] + [HBM_SPEC] * n_sent,
        out_shape=[jax.ShapeDtypeStruct(out_dims, BF16)] + landed,
        scratch_shapes=scratch,
        compiler_params=_params(("arbitrary",), 56), name=name,
    )(a, b, *sent)


def _w_in_grad_part(du, h, tk, name, chip_ids, chip=(), halves=None, small=None):
    t_len = du.shape[0]
    n_t = t_len // tk
    n_q = chip_ids.shape[0]
    width = 2 * (IN_COLS // N_DEV)
    n_steps = n_q * n_t
    n_chip = len(chip)
    sent = tuple(chip) + (() if halves is None else (halves,)) + (() if small is None else tuple(small))
    n_sent = len(sent)

    def body(ids_ref, a_ref, b_ref, *rest):
        srcs = rest[0:n_sent]
        o_ref = rest[n_sent]
        dsts = rest[n_sent + 1:2 * n_sent + 1]
        acc = rest[2 * n_sent + 1]
        sems = list(rest[2 * n_sent + 2:])
        j = pl.program_id(1)
        step = pl.program_id(0) * n_t + j
        if chip:
            _host_chip_exchange(step, n_steps, srcs[0:n_chip], dsts[0:n_chip], sems.pop(0), sems.pop(0))
        if halves is not None:
            _host_half_exchange(step, n_steps, srcs[n_chip], dsts[n_chip], sems.pop(0), sems.pop(0))
        if small is not None:
            _host_small_exchange(step, n_steps, *srcs[n_sent - 3:], *dsts[n_sent - 3:], *sems)

        @pl.when(j == 0)
        def _():
            acc[...] = jnp.zeros(acc.shape, F32)

        acc[...] += _dot_tn(a_ref[...], b_ref[...])

        @pl.when(j == n_t - 1)
        def _():
            o_ref[0] = acc[...].astype(BF16)

    landed = [jax.ShapeDtypeStruct(s.shape, BF16) for s in chip]
    scratch = [pltpu.VMEM((width, D_MODEL), F32)]
    if chip:
        scratch += _exchange_scratch(len(chip), 3)
    if halves is not None:
        landed.append(jax.ShapeDtypeStruct((halves.shape[0], halves.shape[1] // 2, halves.shape[2]), BF16))
        scratch += [pltpu.SemaphoreType.DMA((halves.shape[0],)), pltpu.SemaphoreType.DMA((halves.shape[0],))]
    if small is not None:
        vec_m, vec_b, wab = small
        landed += [jax.ShapeDtypeStruct((N_DEV,) + vec_m.shape, F32), jax.ShapeDtypeStruct((N_DEV,) + vec_b.shape, F32),
                   jax.ShapeDtypeStruct((N_DEV, wab.shape[0] // N_DEV, wab.shape[1]), F32)]
        scratch += _exchange_scratch(3, N_DEV) + [pltpu.SemaphoreType.DMA((2,))]
    grid_spec = pltpu.PrefetchScalarGridSpec(
        num_scalar_prefetch=1, grid=(n_q, n_t),
        in_specs=[pl.BlockSpec((tk, width), lambda q, j, ids: (j, ids[q])),
                  pl.BlockSpec((tk, D_MODEL), lambda q, j, ids: (j, 0))] + [HBM_SPEC] * n_sent,
        out_specs=[pl.BlockSpec((1, width, D_MODEL), lambda q, j, ids: (q, 0, 0))] + [HBM_SPEC] * n_sent,
        scratch_shapes=scratch)
    return pl.pallas_call(
        body, grid_spec=grid_spec, out_shape=[jax.ShapeDtypeStruct((n_q, width, D_MODEL), BF16)] + landed,
        compiler_params=_params(("arbitrary", "arbitrary"), 40), name=name,
    )(chip_ids, du, h, *sent)


def _adamw(w, g, m, v):
    m = ADAM_B1 * m + (1.0 - ADAM_B1) * g
    v = ADAM_B2 * v + (1.0 - ADAM_B2) * (g * g)
    delta = -ADAM_LR * ((m / BC1) / (jnp.sqrt(v / BC2) + ADAM_EPS) + ADAM_WD * w)
    return delta, m, v


def _update_sharded(g, landed, w, m, v, rows_blk, name):
    rows, cols = w.shape

    def body(g_ref, l_ref, w_ref, m_ref, v_ref, og, od, om, ov):
        gv = g_ref[...]
        for j in range(3):
            gv = gv + l_ref[j].astype(F32)
        delta, mn, vn = _adamw(w_ref[...], gv, m_ref[...], v_ref[...])
        og[...] = gv
        od[...] = delta
        om[...] = mn
        ov[...] = vn

    blk = pl.BlockSpec((rows_blk, cols), lambda i: (i, 0))
    shape = pltpu.HBM((rows, cols), F32)
    return pl.pallas_call(
        body, grid=(rows // rows_blk,),
        in_specs=[blk, pl.BlockSpec((3, rows_blk, cols), lambda i: (0, i, 0)), blk, blk, blk],
        out_specs=[blk] * 4, out_shape=[shape] * 4,
        compiler_params=_params(("arbitrary",), 32), name=name,
    )(*_in_hbm(g, landed, w, m, v))


def _update_w_in(g_own, sib_own, landed, w_t, m_t, v_t, core, cols_blk):
    rows, cols = w_t.shape

    def body(core_ref, g_ref, s_ref, l_ref, w_ref, m_ref, v_ref, og, od, om, ov):
        gv = g_ref[0, 0].astype(F32) + s_ref[0].astype(F32)
        for j in range(3):
            gv = gv + l_ref[j].astype(F32)
        delta, mn, vn = _adamw(w_ref[...], gv, m_ref[...], v_ref[...])
        og[...] = gv
        od[...] = delta
        om[...] = mn
        ov[...] = vn

    blk = pl.BlockSpec((rows, cols_blk), lambda i, cr: (0, i))
    grid_spec = pltpu.PrefetchScalarGridSpec(
        num_scalar_prefetch=1, grid=(cols // cols_blk,),
        in_specs=[pl.BlockSpec((1, 1, rows, cols_blk), lambda i, cr: (0, cr[0], 0, i)),
                  pl.BlockSpec((1, rows, cols_blk), lambda i, cr: (0, 0, i)),
                  pl.BlockSpec((3, rows, cols_blk), lambda i, cr: (0, 0, i)), blk, blk, blk],
        out_specs=[blk] * 4)
    return pl.pallas_call(
        body, grid_spec=grid_spec, out_shape=[pltpu.HBM((rows, cols), F32)] * 4,
        compiler_params=_params(("arbitrary",), 32), name="update_w_in",
    )(core, *_in_hbm(g_own.reshape(1, 2, rows, cols), sib_own, landed, w_t, m_t, v_t))


def _update_small(vsum, wsum, g_cw, g_rw, weights, moments_m, moments_v):
    n = len(weights)

    def body(*refs):
        vs, ws, gcw, grw = refs[0:4]
        w_refs = refs[4:4 + n]
        m_refs = refs[4 + n:4 + 2 * n]
        v_refs = refs[4 + 2 * n:4 + 3 * n]
        outs = refs[4 + 3 * n:]
        loss_ref = outs[0]
        loss_ref[...] = jnp.sum(vs[ROW_LOSS:ROW_LOSS + 1, :], axis=1, keepdims=True)
        grads = [
            vs[ROW_GMIX:ROW_GMIX + 1, :], gcw[...], grw[...], vs[ROW_BR:ROW_BR + 1, :],
            ws[0:LRU_WIDTH, :], vs[ROW_BA:ROW_BA + 1, :], ws[LRU_WIDTH:2 * LRU_WIDTH, :], vs[ROW_BX:ROW_BX + 1, :],
            vs[ROW_LAM:ROW_LAM + 1, :], vs[ROW_GNC:ROW_GNC + 1, 0:CONV_WIDTH], vs[ROW_GNR:ROW_GNR + 1, :],
            vs[ROW_GMLP:ROW_GMLP + 1, :], vs[ROW_GF:ROW_GF + 1, :],
        ]
        for k in range(n):
            gk = grads[k]
            delta, mn, vn = _adamw(w_refs[k][...], gk, m_refs[k][...], v_refs[k][...])
            outs[1 + 4 * k][...] = gk
            outs[2 + 4 * k][...] = delta
            outs[3 + 4 * k][...] = mn
            outs[4 + 4 * k][...] = vn

    whole = lambda a: pl.BlockSpec(a.shape, lambda i: (0,) * len(a.shape))
    out_shape = [jax.ShapeDtypeStruct((1, 1), F32)]
    for w in weights:
        out_shape += [jax.ShapeDtypeStruct(w.shape, F32)] * 4
    args = (vsum, wsum, g_cw, g_rw, *weights, *moments_m, *moments_v)
    return pl.pallas_call(
        body, grid=(1,), out_shape=out_shape, in_specs=[whole(a) for a in args], out_specs=[whole(s) for s in out_shape],
        compiler_params=_params(("arbitrary",), 32), name="update_small",
    )(*args)


def kernel(x, norm_mix_g, w_in, conv_w, rnn_conv_w, rnn_conv_b, w_a, b_a, w_x, b_x, lru_lambda, g_norm_conv, g_norm_rnn, w_out, norm_mlp_g, w_mlp_in, w_mlp_out, final_norm_g, loss_target, m_norm_mix_g, m_w_in, m_conv_w, m_rnn_conv_w, m_rnn_conv_b, m_w_a, m_b_a, m_w_x, m_b_x, m_lru_lambda, m_g_norm_conv, m_g_norm_rnn, m_w_out, m_norm_mlp_g, m_w_mlp_in, m_w_mlp_out, m_final_norm_g, v_norm_mix_g, v_w_in, v_conv_w, v_rnn_conv_w, v_rnn_conv_b, v_w_a, v_b_a, v_w_x, v_b_x, v_lru_lambda, v_g_norm_conv, v_g_norm_rnn, v_w_out, v_norm_mlp_g, v_w_mlp_in, v_w_mlp_out, v_final_norm_g):
    t_len = x.shape[1]
    my_id = 4 * lax.axis_index("x") + 2 * lax.axis_index("y") + lax.axis_index("c")
    tm = min(256, t_len)
    tb = min(512, t_len)
    tk = min(512, t_len)

    xs = x.reshape(t_len, D_MODEL)
    tgt = loss_target.reshape(t_len, D_MODEL)
    flat = lambda a: a.reshape(a.shape[-2:]) if a.ndim == 3 else a.reshape(1, -1)
    heads = lambda a: a.reshape(LRU_WIDTH, HEAD_DIM)

    turned = lambda a: jnp.transpose(flat(a))
    win_shard, wout_shard, w1_shard, w2_shard, cp_shard = _prep_shards(
        turned(w_in), flat(w_out), flat(w_mlp_in), flat(w_mlp_out), flat(conv_w), flat(rnn_conv_w))

    u, h, win_t, cp_full = _in_proj(xs, flat(norm_mix_g), (win_shard, cp_shard), min(1024, t_len))
    cpack = cp_full.reshape(N_DEV, 8, 128)
    conv_full = jnp.transpose(cpack[:, 0:3, 0:64], (1, 0, 2)).reshape(3, CONV_WIDTH)
    rnn_full = jnp.transpose(cpack[:, 3:7, :], (1, 0, 2)).reshape(4, LRU_WIDTH)
    mixer_small = (conv_full, rnn_full, flat(rnn_conv_b), heads(w_a), flat(b_a), heads(w_x), flat(b_x),
                   flat(lru_lambda), flat(g_norm_conv), flat(g_norm_rnn))
    hs, y, xr, gate_r, gate_i, mult, w1_blk, wout_blk = _mixer_fwd(u, *mixer_small, (w1_shard, wout_shard), tm)
    wout_f = wout_blk.reshape(MIX_WIDTH, D_MODEL)
    x1, h2, z, w2_blk = _mlp_up(xs, y, flat(norm_mlp_g), wout_f, w1_blk, w2_shard, tb)
    dx1, dx2, vec_m, dpre = _mlp_down_bwd(x1, z, tgt, flat(norm_mlp_g), flat(final_norm_g), w1_blk,
                                          w2_blk.reshape(D_FF, D_MODEL), tb)
    (g_w1,) = _tn_weight_grad(h2, dpre, tk, "w_mlp_in_grad", col_blocks=N_DEV)
    g_w2, sib_w1 = _tn_weight_grad(z, dx2, tk, "w_mlp_out_grad", pair=(g_w1,))
    g_w2 = g_w2.reshape(N_DEV, D_FF // N_DEV, D_MODEL)
    g_wout, sib_w2 = _tn_weight_grad(y, dx1, tk, "w_out_grad", pair=(g_w2,))
    g_wout = g_wout.reshape(N_DEV, MIX_WIDTH // N_DEV, D_MODEL)
    hsend_w1, own_w1, hsend_w2, own_w2 = _pair_sum((g_w1, g_w2), (sib_w1, sib_w2), "pair_sum_w_mlp")
    du, vec_b, wab, landed_w1, landed_w2, sib_wout = _mixer_bwd(
        u, hs, dx1, (xr, gate_r, gate_i, mult), conv_full, rnn_full, flat(rnn_conv_b), heads(w_a), heads(w_x),
        flat(lru_lambda), flat(g_norm_conv), flat(g_norm_rnn), wout_f, (hsend_w1, hsend_w2), g_wout, tm)
    hsend_wout, own_wout = _pair_sum((g_wout,), (sib_wout,), "pair_sum_w_out")
    ax, ay, ac = lax.axis_index("x"), lax.axis_index("y"), lax.axis_index("c")
    chip_ids = jnp.stack([2 * cx + cy for cx, cy in [(ax, ay)] + _other_chips(ax, ay)]).astype(jnp.int32)
    core = jnp.reshape(ac, (1,)).astype(jnp.int32)
    tw = min(1024, t_len)
    g_others, landed_wout, vrecv_m, vrecv_b, wrecv = _w_in_grad_part(
        du, h, tw, "w_in_grad_others", chip_ids[1:4], chip=(hsend_wout,), small=(vec_m, vec_b, wab))
    g_own, sib_others = _w_in_grad_part(du, h, tw, "w_in_grad_own", chip_ids[0:1], halves=g_others)
    hsend_win = _pair_sum_parts(g_others, sib_others, core)
    grad_x, vec_x, landed_win, sib_own = _in_proj_bwd(du, dx1, xs, flat(norm_mix_g), win_t, tm, hsend_win, g_own)

    vsum, wsum = _final_small(vrecv_m, vrecv_b, wab, wrecv, vec_x)

    up_win = _update_w_in(g_own, sib_own, landed_win, turned(w_in), turned(m_w_in), turned(v_w_in), core, 256)
    up_win = [jnp.transpose(a) for a in up_win]
    up_wout = _update_sharded(own_wout, landed_wout, flat(w_out), flat(m_w_out), flat(v_w_out), 96, "update_w_out")
    up_w1 = _update_sharded(own_w1, landed_w1, flat(w_mlp_in), flat(m_w_mlp_in), flat(v_w_mlp_in), 256,
                            "update_w_mlp_in")
    up_w2 = _update_sharded(own_w2, landed_w2, flat(w_mlp_out), flat(m_w_mlp_out), flat(v_w_mlp_out), 256,
                            "update_w_mlp_out")

    g_cw = lax.dynamic_slice(vsum, (ROW_CW, 64 * my_id), (3, 64))
    g_rw = lax.dynamic_slice(vsum, (ROW_RW, 128 * my_id), (4, 128))
    small_w = (norm_mix_g, conv_w, rnn_conv_w, rnn_conv_b, w_a, b_a, w_x, b_x, lru_lambda, g_norm_conv, g_norm_rnn,
               norm_mlp_g, final_norm_g)
    small_m = (m_norm_mix_g, m_conv_w, m_rnn_conv_w, m_rnn_conv_b, m_w_a, m_b_a, m_w_x, m_b_x, m_lru_lambda,
               m_g_norm_conv, m_g_norm_rnn, m_norm_mlp_g, m_final_norm_g)
    small_v = (v_norm_mix_g, v_conv_w, v_rnn_conv_w, v_rnn_conv_b, v_w_a, v_b_a, v_w_x, v_b_x, v_lru_lambda,
               v_g_norm_conv, v_g_norm_rnn, v_norm_mlp_g, v_final_norm_g)
    is_heads = (False, False, False, False, True, False, True, False, False, False, False, False, False)
    as2d = lambda arrs: [heads(a) if hd else flat(a) for a, hd in zip(arrs, is_heads)]
    small_out = _update_small(vsum, wsum, g_cw, g_rw, as2d(small_w), as2d(small_m), as2d(small_v))
    loss = small_out[0].reshape(())

    names = ["norm_mix_g", "w_in", "conv_w", "rnn_conv_w", "rnn_conv_b", "w_a", "b_a", "w_x", "b_x", "lru_lambda",
             "g_norm_conv", "g_norm_rnn", "w_out", "norm_mlp_g", "w_mlp_in", "w_mlp_out", "final_norm_g"]
    originals = dict(zip(names, (norm_mix_g, w_in, conv_w, rnn_conv_w, rnn_conv_b, w_a, b_a, w_x, b_x, lru_lambda,
                                 g_norm_conv, g_norm_rnn, w_out, norm_mlp_g, w_mlp_in, w_mlp_out, final_norm_g)))
    results = {"w_in": up_win, "w_out": up_wout, "w_mlp_in": up_w1, "w_mlp_out": up_w2}
    small_names = ["norm_mix_g", "conv_w", "rnn_conv_w", "rnn_conv_b", "w_a", "b_a", "w_x", "b_x", "lru_lambda",
                   "g_norm_conv", "g_norm_rnn", "norm_mlp_g", "final_norm_g"]
    for k, nm in enumerate(small_names):
        results[nm] = small_out[1 + 4 * k:5 + 4 * k]
    out = [loss, grad_x.reshape(x.shape)]
    for kind in range(4):
        out += [results[nm][kind].reshape(originals[nm].shape) for nm in names]
    return tuple(out)
```

```python
import functools

import jax
import jax.numpy as jnp
from jax import lax
from jax.experimental import pallas as pl
from jax.experimental.pallas import tpu as pltpu

F32 = jnp.float32
BF16 = jnp.bfloat16

D_MODEL = 1024
HEAD_DIM = 64
CONV_WIDTH = 512
LRU_WIDTH = 1024
MIX_WIDTH = CONV_WIDTH + LRU_WIDTH
IN_COLS = 3 * CONV_WIDTH + 2 * LRU_WIDTH
D_FF = 4 * D_MODEL
GROUP = 256
EPS = 1e-6
LRU_C = 8.0
N_DEV = 8
SUB = 8

OFF_GB, OFF_GC, OFF_V, OFF_XR, OFF_G = 0, 512, 1024, 1536, 2560

ADAM_LR, ADAM_B1, ADAM_B2, ADAM_EPS, ADAM_WD, ADAM_STEP = 0.001, 0.9, 0.999, 1e-08, 0.01, 10
BC1 = 1.0 - ADAM_B1 ** ADAM_STEP
BC2 = 1.0 - ADAM_B2 ** ADAM_STEP

MIB = 1024 * 1024
MESH = pl.DeviceIdType.MESH

VEC_ROWS = 32
ROW_GF, ROW_GMLP, ROW_LOSS = 0, 1, 2
ROW_GNC, ROW_GNR, ROW_BR, ROW_BA, ROW_BX, ROW_LAM, ROW_CW, ROW_RW = 8, 9, 10, 11, 12, 13, 14, 17
ROW_GMIX = 24
ACC_GNC, ACC_GNR, ACC_BR, ACC_BA, ACC_BX, ACC_SP, ACC_CW, ACC_RW, N_ACC = 0, 1, 2, 3, 4, 5, 6, 9, 13


def _params(semantics=None, vmem_mib=48):
    return pltpu.CompilerParams(dimension_semantics=semantics, vmem_limit_bytes=vmem_mib * MIB)


def _rms(x):
    return lax.rsqrt(jnp.mean(x * x, axis=-1, keepdims=True) + EPS)


def _rms_bwd(dy, xhat, r, g):
    dyh = dy * g
    return r * (dyh - xhat * jnp.mean(dyh * xhat, axis=-1, keepdims=True))


def _sigmoid(x):
    return 0.5 + 0.5 * jnp.tanh(0.5 * x)


def _gelu(x):
    c0, c1 = 0.7978845608028654, 0.044715
    x2 = x * x
    t = jnp.tanh(x * (c0 + (c0 * c1) * x2))
    half = 0.5 + 0.5 * t
    ge = x * half
    dge = half + (ge - ge * half) * (2.0 * c0 + (6.0 * c0 * c1) * x2)
    return ge, dge


def _softplus_neg(lam):
    z = -lam
    e = jnp.exp(-jnp.abs(z))
    return jnp.maximum(z, 0.0) + jnp.where(e < 1e-4, e * (1.0 - 0.5 * e), jnp.log(1.0 + e))


def _lru_gates(pa, px, sp_c):
    ra = _sigmoid(pa)
    ii = _sigmoid(px)
    neg_la = ra * sp_c
    a = jnp.exp(-neg_la)
    m2 = jnp.tanh(neg_la) * (1.0 + a * a)
    mult = jnp.where(m2 > 0.0, m2 * lax.rsqrt(m2), 0.0)
    return ra, ii, a, mult


def _down(cur, prev, s, row):
    return pltpu.roll(jnp.where(row < SUB - s, cur, prev), s, 0)


def _up(cur, nxt, s, row):
    return pltpu.roll(jnp.where(row >= s, cur, nxt), SUB - s, 0)


def _scan8_fwd(a, b, row):
    for s in (1, 2, 4):
        m = row >= s
        a_sh = pltpu.roll(a, s, 0)
        b_sh = pltpu.roll(b, s, 0)
        b = jnp.where(m, a * b_sh + b, b)
        a = jnp.where(m, a * a_sh, a)
    return a, b


def _scan8_rev(a, b, row):
    for s in (1, 2, 4):
        m = row < SUB - s
        a_sh = pltpu.roll(a, SUB - s, 0)
        b_sh = pltpu.roll(b, SUB - s, 0)
        b = jnp.where(m, a * b_sh + b, b)
        a = jnp.where(m, a * a_sh, a)
    return a, b


def _group_mask(shape):
    r = lax.broadcasted_iota(jnp.int32, shape, 0)
    c = lax.broadcasted_iota(jnp.int32, shape, 1)
    return ((r % GROUP) // HEAD_DIM) == (c // HEAD_DIM)


def _expand_heads(w):
    j = lax.broadcasted_iota(jnp.int32, (HEAD_DIM, GROUP), 0)
    c = lax.broadcasted_iota(jnp.int32, (HEAD_DIM, GROUP), 1)
    spread = (c % HEAD_DIM == j).astype(BF16)
    e = jnp.dot(w.astype(BF16), spread, preferred_element_type=F32)
    return jnp.where(_group_mask(e.shape), e, 0.0).astype(BF16)


def _fold_heads(p):
    p = jnp.where(_group_mask(p.shape), p, 0.0)
    c = lax.broadcasted_iota(jnp.int32, (GROUP, HEAD_DIM), 0)
    j = lax.broadcasted_iota(jnp.int32, (GROUP, HEAD_DIM), 1)
    fold = (c % HEAD_DIM == j).astype(BF16)
    hi = p.astype(BF16)
    rest = p - hi.astype(F32)
    mid = rest.astype(BF16)
    lo = (rest - mid.astype(F32)).astype(BF16)
    dot = functools.partial(jnp.dot, preferred_element_type=F32)
    return dot(hi, fold) + dot(mid, fold) + dot(lo, fold)


def _block_diag_apply(xb, wbd_ref):
    parts = [jnp.dot(xb[:, g * GROUP:(g + 1) * GROUP], wbd_ref[g * GROUP:(g + 1) * GROUP, :],
                     preferred_element_type=F32) for g in range(LRU_WIDTH // GROUP)]
    return jnp.concatenate(parts, axis=1)


def _block_diag_apply_t(db, wbd_ref):
    parts = [lax.dot_general(db[:, g * GROUP:(g + 1) * GROUP], wbd_ref[g * GROUP:(g + 1) * GROUP, :],
                             (((1,), (1,)), ((), ())), preferred_element_type=F32)
             for g in range(LRU_WIDTH // GROUP)]
    return jnp.concatenate(parts, axis=1)


def _dot_nt(a, b):
    return lax.dot_general(a, b, (((1,), (1,)), ((), ())), preferred_element_type=F32)


def _dot_tn(a, b):
    return lax.dot_general(a, b, (((0,), (0,)), ((), ())), preferred_element_type=F32)


CHUNKS_IN_FLIGHT = 8


def _chunk_loop(n_chunks, chunk, init):
    def body(k, carry):
        for j in range(CHUNKS_IN_FLIGHT):
            carry = chunk(k * CHUNKS_IN_FLIGHT + j, carry)
        return carry

    return lax.fori_loop(0, n_chunks // CHUNKS_IN_FLIGHT, body, init)


def _place():
    x, y, c = lax.axis_index("x"), lax.axis_index("y"), lax.axis_index("c")
    return x, y, c


def _block_id(chip, core):
    return 4 * chip[0] + 2 * chip[1] + core


def _other_chips(x, y):
    return [(1 - x, y), (x, 1 - y), (1 - x, 1 - y)]


def _remote_copy(src, dst, send_sem, recv_sem, to):
    return pltpu.make_async_remote_copy(src_ref=src, dst_ref=dst, send_sem=send_sem, recv_sem=recv_sem,
                                        device_id=to, device_id_type=MESH)


HBM_SPEC = pl.BlockSpec(memory_space=pl.ANY)


def _in_hbm(*arrays):
    return [pltpu.with_memory_space_constraint(a, pltpu.HBM) for a in arrays]


def _prep_shards(w_in_t, w_out, w_mlp_in, w_mlp_out, conv_w, rnn_conv_w):
    def body(win_ref, wout_ref, w1_ref, w2_ref, cw_ref, rw_ref, o_win, o_wout, o_w1, o_w2, o_cp):
        o_win[...] = win_ref[...].astype(BF16)
        o_wout[...] = wout_ref[...].astype(BF16)
        o_w1[...] = w1_ref[...].astype(BF16)
        o_w2[...] = w2_ref[...].astype(BF16)
        o_cp[...] = jnp.zeros(o_cp.shape, F32)
        o_cp[0:3, 0:64] = cw_ref[...]
        o_cp[3:7, :] = rw_ref[...]

    whole = lambda shape: pl.BlockSpec(shape, lambda i: (0,) * len(shape))
    args = (w_in_t, w_out, w_mlp_in, w_mlp_out, conv_w, rnn_conv_w)
    shapes = [(w_in_t.shape, BF16), (w_out.shape, BF16), (w_mlp_in.shape, BF16), (w_mlp_out.shape, BF16),
              ((8, 128), F32)]
    return pl.pallas_call(
        body, grid=(1,), out_shape=[jax.ShapeDtypeStruct(s, d) for s, d in shapes],
        in_specs=[whole(a.shape) for a in args], out_specs=[whole(s) for s, _ in shapes],
        compiler_params=_params(("arbitrary",), 40), name="prep_shards",
    )(*args)


def _host_all_gather(step, n_steps, shards, fulls, send_sems, recv_sems, local_sems):
    x, y, c = _place()
    me = (x, y, c)
    my_id = _block_id((x, y), c)
    sibling = (x, y, 1 - c)
    chips = _other_chips(x, y)
    n_arr = len(shards)

    def copy(arr, k, block, to, src=None):
        dst = fulls[arr].at[block]
        return _remote_copy(dst if src is None else src, dst, send_sems.at[arr, k], recv_sems.at[arr, k], to)

    def local(arr):
        return pltpu.make_async_copy(shards[arr], fulls[arr].at[my_id], local_sems.at[arr])

    @pl.when(step == 0)
    def _():
        for arr in range(n_arr):
            local(arr).start()
            copy(arr, 0, my_id, sibling, shards[arr]).start()
            for j, chip in enumerate(chips):
                copy(arr, 1 + j, my_id, (*chip, c), shards[arr]).start()

    @pl.when(step == max(n_steps - 2, 0))
    def _():
        for j, chip in enumerate(chips):
            for arr in range(n_arr):
                copy(arr, 1 + j, _block_id(chip, c), me).wait_recv()
                copy(arr, 4 + j, _block_id(chip, c), sibling).start()

    @pl.when(step == n_steps - 1)
    def _():
        for arr in range(n_arr):
            copy(arr, 0, _block_id((x, y), 1 - c), me).wait_recv()
            for j, chip in enumerate(chips):
                copy(arr, 4 + j, _block_id(chip, 1 - c), me).wait_recv()
            for k in range(4):
                copy(arr, k, my_id, me, shards[arr]).wait_send()
            for j, chip in enumerate(chips):
                copy(arr, 4 + j, _block_id(chip, c), me).wait_send()
            local(arr).wait()


def _host_pair_exchange(step, n_steps, gs, sibs, send_sems, recv_sems):
    x, y, c = _place()
    sibling = (x, y, 1 - c)
    chips = [(x, y)] + _other_chips(x, y)

    def d2d(arr, q):
        return _remote_copy(gs[arr].at[_block_id(chips[q], 1 - c)], sibs[arr].at[q],
                            send_sems.at[arr, q], recv_sems.at[arr, q], sibling)

    @pl.when(step == 0)
    def _():
        for arr in range(len(gs)):
            for q in (1, 2, 3, 0):
                d2d(arr, q).start()

    @pl.when(step == n_steps - 1)
    def _():
        for arr in range(len(gs)):
            for q in range(4):
                d2d(arr, q).wait()


def _host_chip_exchange(step, n_steps, hsends, hrecvs, send_sems, recv_sems):
    x, y, c = _place()
    chips = _other_chips(x, y)

    def ici(arr, j):
        return _remote_copy(hsends[arr].at[j], hrecvs[arr].at[j], send_sems.at[arr, j], recv_sems.at[arr, j],
                            (*chips[j], c))

    @pl.when(step == 0)
    def _():
        for arr in range(len(hsends)):
            for j in range(3):
                ici(arr, j).start()

    @pl.when(step == n_steps - 1)
    def _():
        for arr in range(len(hsends)):
            for j in range(3):
                ici(arr, j).wait()


def _host_half_exchange(step, n_steps, parts, sibs, send_sems, recv_sems):
    x, y, c = _place()
    n_q, rows2, _ = parts.shape
    half = rows2 // 2

    def d2d(q):
        src = parts.at[q, pl.ds(pl.multiple_of((1 - c) * half, 16), half), :]
        return _remote_copy(src, sibs.at[q], send_sems.at[q], recv_sems.at[q], (x, y, 1 - c))

    @pl.when(step == 0)
    def _():
        for q in range(n_q):
            d2d(q).start()

    @pl.when(step == n_steps - 1)
    def _():
        for q in range(n_q):
            d2d(q).wait()


def _peer(x, y, c, k):
    return (x ^ ((k >> 2) & 1), y ^ ((k >> 1) & 1), c ^ (k & 1))


def _host_small_exchange(step, n_steps, vec_m, vec_b, wab, vrecv_m, vrecv_b, wrecv, send_sems, recv_sems, local_sems):
    x, y, c = _place()
    my_id = _block_id((x, y), c)
    wrows = wab.shape[0] // N_DEV

    def copies(k):
        to = _peer(x, y, c, k)
        block = wab.at[pl.ds(pl.multiple_of(_block_id(to[0:2], to[2]) * wrows, SUB), wrows), :]
        return [_remote_copy(vec_m, vrecv_m.at[my_id], send_sems.at[0, k], recv_sems.at[0, k], to),
                _remote_copy(vec_b, vrecv_b.at[my_id], send_sems.at[1, k], recv_sems.at[1, k], to),
                _remote_copy(block, wrecv.at[k], send_sems.at[2, k], recv_sems.at[2, k], to)]

    mine = [pltpu.make_async_copy(vec_m, vrecv_m.at[my_id], local_sems.at[0]),
            pltpu.make_async_copy(vec_b, vrecv_b.at[my_id], local_sems.at[1])]

    @pl.when(step == 0)
    def _():
        for cp in mine:
            cp.start()
        for k in range(1, N_DEV):
            for cp in copies(k):
                cp.start()

    @pl.when(step == n_steps - 1)
    def _():
        for k in range(1, N_DEV):
            for cp in copies(k):
                cp.wait()
        for cp in mine:
            cp.wait()


def _pair_sum_parts(parts, sibs, core):
    n_q, rows2, cols = parts.shape
    half = rows2 // 2

    def body(core_ref, g_ref, s_ref, o_ref):
        o_ref[0] = (g_ref[0, 0].astype(F32) + s_ref[0].astype(F32)).astype(BF16)

    block = (1, half, cols)
    grid_spec = pltpu.PrefetchScalarGridSpec(
        num_scalar_prefetch=1, grid=(n_q,),
        in_specs=[pl.BlockSpec((1, 1, half, cols), lambda q, cr: (q, cr[0], 0, 0)),
                  pl.BlockSpec(block, lambda q, cr: (q, 0, 0))],
        out_specs=pl.BlockSpec(block, lambda q, cr: (q, 0, 0)))
    return pl.pallas_call(
        body, grid_spec=grid_spec, out_shape=pltpu.HBM((n_q, half, cols), BF16),
        compiler_params=_params(("arbitrary",), 32), name="pair_sum_w_in",
    )(core, *_in_hbm(parts.reshape(n_q, 2, half, cols), sibs))


def _pair_sum(gs, sibs, name):
    n_arr = len(gs)
    x, y, c = _place()
    slots = jnp.stack([_block_id(chip, c) for chip in [(x, y)] + _other_chips(x, y)]).astype(jnp.int32)

    def body(slots_ref, *refs):
        q = pl.program_id(0)
        for k in range(n_arr):
            g_ref, sib_ref = refs[2 * k:2 * k + 2]
            hs_ref, own_ref = refs[2 * n_arr + 2 * k:2 * n_arr + 2 * k + 2]
            both = g_ref[0].astype(F32) + sib_ref[0].astype(F32)

            @pl.when(q == 0)
            def _(own_ref=own_ref, both=both):
                own_ref[...] = both

            @pl.when(q > 0)
            def _(hs_ref=hs_ref, both=both):
                hs_ref[0] = both.astype(BF16)

    in_specs, out_specs, out_shape, args = [], [], [], []
    for g, sib in zip(gs, sibs):
        _, rows, cols = g.shape
        block = (1, rows, cols)
        in_specs += [pl.BlockSpec(block, lambda q, s: (s[q], 0, 0)), pl.BlockSpec(block, lambda q, s: (q, 0, 0))]
        out_specs += [pl.BlockSpec(block, lambda q, s: (jnp.maximum(q - 1, 0), 0, 0)),
                      pl.BlockSpec((rows, cols), lambda q, s: (0, 0))]
        out_shape += [pltpu.HBM((3, rows, cols), BF16), pltpu.HBM((rows, cols), F32)]
        args += _in_hbm(g, sib)
    grid_spec = pltpu.PrefetchScalarGridSpec(num_scalar_prefetch=1, grid=(4,), in_specs=in_specs, out_specs=out_specs)
    return pl.pallas_call(
        body, grid_spec=grid_spec, out_shape=out_shape,
        compiler_params=_params(("arbitrary",), 40), name=name,
    )(slots, *args)


def _exchange_scratch(n_arr, n_copies):
    return [pltpu.SemaphoreType.DMA((n_arr, n_copies)), pltpu.SemaphoreType.DMA((n_arr, n_copies))]


def _final_small(vrecv_m, vrecv_b, wab, wrecv, vec_x):
    wrows = wab.shape[0] // N_DEV

    def body(vm_ref, vb_ref, w_ref, wr_ref, vx_ref, o_vec, o_w, xrecv, wred, x_send, x_recv, b_send, b_recv):
        x, y, c = _place()
        my_id = _block_id((x, y), c)
        my_rows = pl.ds(pl.multiple_of(my_id * wrows, SUB), wrows)

        def xcopy(k):
            return _remote_copy(vx_ref, xrecv.at[my_id], x_send.at[k], x_recv.at[k], _peer(x, y, c, k))

        def bcopy(k):
            return _remote_copy(wred, o_w.at[my_rows, :], b_send.at[k], b_recv.at[k], _peer(x, y, c, k))

        xrecv[my_id] = vx_ref[...]
        for k in range(1, N_DEV):
            xcopy(k).start()
        red = w_ref[my_rows, :]
        for k in range(1, N_DEV):
            red = red + wr_ref[k]
        wred[...] = red
        o_w[my_rows, :] = red
        for k in range(1, N_DEV):
            bcopy(k).start()
        for k in range(1, N_DEV):
            xcopy(k).wait_recv()
        for rows, ref in ((slice(0, 8), vm_ref), (slice(8, 24), vb_ref), (slice(24, 32), xrecv)):
            tot = ref[0]
            for s in range(1, N_DEV):
                tot = tot + ref[s]
            o_vec[rows, :] = tot
        for k in range(1, N_DEV):
            bcopy(k).wait_recv()
        for k in range(1, N_DEV):
            xcopy(k).wait_send()
            bcopy(k).wait_send()

    vm = pl.BlockSpec(memory_space=pltpu.VMEM)
    dma8 = pltpu.SemaphoreType.DMA((N_DEV,))
    return pl.pallas_call(
        body, out_shape=(jax.ShapeDtypeStruct((VEC_ROWS, D_MODEL), F32), jax.ShapeDtypeStruct(wab.shape, F32)),
        in_specs=[vm] * 5, out_specs=[vm] * 2,
        scratch_shapes=[pltpu.VMEM((N_DEV, SUB, D_MODEL), F32), pltpu.VMEM((wrows, HEAD_DIM), F32),
                        dma8, dma8, dma8, dma8],
        compiler_params=_params(vmem_mib=32), name="final_small",
    )(vrecv_m, vrecv_b, wab, wrecv, vec_x)


def _in_proj(x, g_mix, shards, tm):
    t_len = x.shape[0]
    n_t = t_len // tm
    n_arr = len(shards)
    rows = [s.shape[0] for s in shards]
    width = 2 * rows[0]
    ax, ay = lax.axis_index("x"), lax.axis_index("y")
    order = jnp.stack([2 * cx + cy for cx, cy in [(ax, ay)] + _other_chips(ax, ay)]).astype(jnp.int32)

    def body(order_ref, x_ref, g_ref, *rest):
        shard_refs = rest[0:n_arr]
        u_ref, h_ref = rest[n_arr:n_arr + 2]
        fulls = rest[n_arr + 2:2 * n_arr + 2]
        h_s, wbuf, send_sems, recv_sems, local_sems, load_sem = rest[2 * n_arr + 2:]
        p = pl.program_id(0)
        i = pl.program_id(1)
        x_, y_, c = _place()
        me = (x_, y_, c)
        my_id = _block_id((x_, y_), c)
        sibling = (x_, y_, 1 - c)
        chips = _other_chips(x_, y_)

        def block(arr, blk):
            return fulls[arr].at[pl.ds(pl.multiple_of(blk * rows[arr], rows[arr]), rows[arr]), :]

        def copy(arr, k, blk, to, src=None):
            dst = block(arr, blk)
            return _remote_copy(dst if src is None else src, dst, send_sems.at[arr, k], recv_sems.at[arr, k], to)

        def local(arr):
            return pltpu.make_async_copy(shard_refs[arr], block(arr, my_id), local_sems.at[arr])

        def load_chip(chip, slot):
            start = pl.multiple_of((2 * chip[0] + chip[1]) * width, width)
            return pltpu.make_async_copy(fulls[0].at[pl.ds(start, width), :], wbuf.at[slot], load_sem.at[slot])

        def pass_on(j):
            for arr in range(n_arr):
                copy(arr, 1 + j, _block_id(chips[j], c), me).wait_recv()
                copy(arr, 4 + j, _block_id(chips[j], c), sibling).start()

        def complete(j):
            for arr in range(n_arr):
                copy(arr, 4 + j, _block_id(chips[j], 1 - c), me).wait_recv()

        @pl.when((p == 0) & (i == 0))
        def _():
            for arr in range(n_arr):
                local(arr).start()
                copy(arr, 0, my_id, sibling, shard_refs[arr]).start()
                for j in (0, 1):
                    copy(arr, 1 + j, my_id, (*chips[j], c), shard_refs[arr]).start()
            for arr in range(n_arr):
                local(arr).wait()
                copy(arr, 0, _block_id((x_, y_), 1 - c), me).wait_recv()
            load_chip((x_, y_), 0).start()
            load_chip((x_, y_), 0).wait()

        @pl.when((p == 1) & (i == 0))
        def _():
            pass_on(0)
            for arr in range(n_arr):
                copy(arr, 3, my_id, (*chips[2], c), shard_refs[arr]).start()
            pass_on(1)
            complete(0)
            load_chip(chips[0], 1).start()
            load_chip(chips[0], 1).wait()
            complete(1)
            load_chip(chips[1], 0).start()

        @pl.when((p == 2) & (i == 0))
        def _():
            load_chip(chips[1], 0).wait()

        @pl.when((p == 3) & (i == 0))
        def _():
            pass_on(2)
            complete(2)
            load_chip(chips[2], 1).start()
            load_chip(chips[2], 1).wait()

        @pl.when((p == 3) & (i == n_t - 1))
        def _():
            for arr in range(n_arr):
                for k in range(4):
                    copy(arr, k, my_id, me, shard_refs[arr]).wait_send()
                for j, chip in enumerate(chips):
                    copy(arr, 4 + j, _block_id(chip, c), me).wait_send()

        tile = pl.ds(pl.multiple_of(i * tm, tm), tm)

        @pl.when(p == 0)
        def _():
            xv = x_ref[...]
            h = (xv * _rms(xv) * g_ref[...]).astype(BF16)
            h_ref[...] = h
            h_s[tile, :] = h

        for slot in (0, 1):
            @pl.when(p % 2 == slot)
            def _(slot=slot):
                u_ref[...] = _dot_nt(h_s[tile, :], wbuf[slot])

    first_pass = lambda p, i, o: (jnp.where(p == 0, i, n_t - 1), 0)
    grid_spec = pltpu.PrefetchScalarGridSpec(
        num_scalar_prefetch=1, grid=(4, n_t),
        in_specs=[pl.BlockSpec((tm, D_MODEL), first_pass), pl.BlockSpec((1, D_MODEL), lambda p, i, o: (0, 0))]
        + [HBM_SPEC] * n_arr,
        out_specs=[pl.BlockSpec((tm, width), lambda p, i, o: (i, o[p])), pl.BlockSpec((tm, D_MODEL), first_pass)]
        + [HBM_SPEC] * n_arr,
        scratch_shapes=[pltpu.VMEM((t_len, D_MODEL), BF16), pltpu.VMEM((2, width, D_MODEL), BF16)]
        + _exchange_scratch(n_arr, 7) + [pltpu.SemaphoreType.DMA((n_arr,)), pltpu.SemaphoreType.DMA((2,))])
    return pl.pallas_call(
        body, grid_spec=grid_spec,
        out_shape=[jax.ShapeDtypeStruct((t_len, IN_COLS), F32), jax.ShapeDtypeStruct((t_len, D_MODEL), BF16)]
        + [jax.ShapeDtypeStruct((N_DEV * s.shape[0], s.shape[1]), s.dtype) for s in shards],
        compiler_params=_params(("arbitrary", "arbitrary"), 48), name="in_proj",
    )(order, x, g_mix, *shards)


def _conv3_chunk(u_ref, r, cv_prev, cw, row):
    gb = u_ref[pl.ds(r, SUB), OFF_GB:OFF_GB + CONV_WIDTH]
    gc = u_ref[pl.ds(r, SUB), OFF_GC:OFF_GC + CONV_WIDTH]
    v = u_ref[pl.ds(r, SUB), OFF_V:OFF_V + CONV_WIDTH]
    cv = gc * v
    cv_m1 = _down(cv, cv_prev, 1, row)
    cv_m2 = _down(cv, cv_prev, 2, row)
    cq = cw[2:3, :] * cv + cw[1:2, :] * cv_m1 + cw[0:1, :] * cv_m2
    return gb, gc, v, cv, cv_m1, cv_m2, cq


def _conv4_chunk(u_ref, r, xin_prev, rw, rb, row):
    xin = u_ref[pl.ds(r, SUB), OFF_XR:OFF_XR + LRU_WIDTH]
    m1 = _down(xin, xin_prev, 1, row)
    m2 = _down(xin, xin_prev, 2, row)
    m3 = _down(xin, xin_prev, 3, row)
    xr = rw[3:4, :] * xin + rw[2:3, :] * m1 + rw[1:2, :] * m2 + rw[0:1, :] * m3 + rb
    return xin, m1, m2, m3, xr


def _mixer_fwd(u, conv_w, rnn_conv_w, rnn_conv_b, wa, b_a, wx, b_x, lam, gnc, gnr, shards, tm):
    t_len = u.shape[0]
    n_steps = t_len // tm
    n_chunks = tm // SUB
    n_arr = len(shards)

    def body(u_ref, cw_ref, rw_ref, rb_ref, wa_ref, ba_ref, wx_ref, bx_ref, lam_ref, gnc_ref, gnr_ref, *rest):
        shard_refs = rest[0:n_arr]
        hs_ref, y_ref, xr_s, ra_ref, ii_ref, mult_ref = rest[n_arr:n_arr + 6]
        fulls = rest[n_arr + 6:2 * n_arr + 6]
        (y_s, pa_s, px_s, wabd, wxbd, cv_car, xin_car, h_car,
         send_sems, recv_sems, local_sems) = rest[2 * n_arr + 6:]
        _host_all_gather(pl.program_id(0), n_steps, shard_refs, fulls, send_sems, recv_sems, local_sems)

        @pl.when(pl.program_id(0) == 0)
        def _():
            cv_car[...] = jnp.zeros(cv_car.shape, F32)
            xin_car[...] = jnp.zeros(xin_car.shape, F32)
            h_car[...] = jnp.zeros(h_car.shape, F32)
            wabd[...] = _expand_heads(wa_ref[...])
            wxbd[...] = _expand_heads(wx_ref[...])

        row_c = lax.broadcasted_iota(jnp.int32, (SUB, CONV_WIDTH), 0)
        row_r = lax.broadcasted_iota(jnp.int32, (SUB, LRU_WIDTH), 0)
        cw = cw_ref[...]
        rw = rw_ref[...]
        rb = rb_ref[...]
        g_c = gnc_ref[...]
        g_r = gnr_ref[...]
        sp_c = LRU_C * _softplus_neg(lam_ref[...])

        def convs(i, carry):
            cv_prev, xin_prev = carry
            r = pl.multiple_of(i * SUB, SUB)
            gb, _, _, cv, _, _, cq = _conv3_chunk(u_ref, r, cv_prev, cw, row_c)
            y_c = gb * cq
            y_s[pl.ds(r, SUB), 0:CONV_WIDTH] = y_c * _rms(y_c) * g_c
            xin, _, _, _, xr = _conv4_chunk(u_ref, r, xin_prev, rw, rb, row_r)
            xr_s[pl.ds(r, SUB), :] = xr
            return cv, xin

        cv_last, xin_last = _chunk_loop(n_chunks, convs, (cv_car[...], xin_car[...]))
        cv_car[...] = cv_last
        xin_car[...] = xin_last

        xrb = xr_s[...].astype(BF16)
        pa_s[...] = _block_diag_apply(xrb, wabd) + ba_ref[...]
        px_s[...] = _block_diag_apply(xrb, wxbd) + bx_ref[...]

        def recur(i, h_prev):
            r = pl.multiple_of(i * SUB, SUB)
            xr = xr_s[pl.ds(r, SUB), :]
            ra, ii, a, mult = _lru_gates(pa_s[pl.ds(r, SUB), :], px_s[pl.ds(r, SUB), :], sp_c)
            ra_ref[pl.ds(r, SUB), :] = ra
            ii_ref[pl.ds(r, SUB), :] = ii
            mult_ref[pl.ds(r, SUB), :] = mult
            a_cum, b_cum = _scan8_fwd(a, mult * ii * xr, row_r)
            h = a_cum * h_prev + b_cum
            hs_ref[pl.ds(r, SUB), :] = h
            ge, _ = _gelu(u_ref[pl.ds(r, SUB), OFF_G:OFF_G + LRU_WIDTH])
            y_r = h * ge
            y_s[pl.ds(r, SUB), CONV_WIDTH:MIX_WIDTH] = y_r * _rms(y_r) * g_r
            return h[SUB - 1:SUB, :]

        h_car[...] = _chunk_loop(n_chunks, recur, h_car[...])

        y_ref[...] = y_s[...].astype(BF16)

    row_tile = lambda w: pl.BlockSpec((tm, w), lambda i: (i, 0))
    whole = lambda a: pl.BlockSpec(a.shape, lambda i: (0,) * a.ndim)
    smalls = (conv_w, rnn_conv_w, rnn_conv_b, wa, b_a, wx, b_x, lam, gnc, gnr)
    return pl.pallas_call(
        body, grid=(n_steps,),
        in_specs=[row_tile(IN_COLS)] + [whole(a) for a in smalls] + [HBM_SPEC] * n_arr,
        out_specs=[row_tile(LRU_WIDTH), row_tile(MIX_WIDTH)] + [row_tile(LRU_WIDTH)] * 4 + [HBM_SPEC] * n_arr,
        out_shape=[jax.ShapeDtypeStruct((t_len, LRU_WIDTH), F32), jax.ShapeDtypeStruct((t_len, MIX_WIDTH), BF16)]
        + [jax.ShapeDtypeStruct((t_len, LRU_WIDTH), F32)] * 4
        + [jax.ShapeDtypeStruct((N_DEV,) + s.shape, BF16) for s in shards],
        scratch_shapes=[pltpu.VMEM((tm, MIX_WIDTH), F32),
                        pltpu.VMEM((tm, LRU_WIDTH), F32), pltpu.VMEM((tm, LRU_WIDTH), F32),
                        pltpu.VMEM((LRU_WIDTH, GROUP), BF16), pltpu.VMEM((LRU_WIDTH, GROUP), BF16),
                        pltpu.VMEM((SUB, CONV_WIDTH), F32), pltpu.VMEM((SUB, LRU_WIDTH), F32),
                        pltpu.VMEM((1, LRU_WIDTH), F32)]
        + _exchange_scratch(n_arr, 7) + [pltpu.SemaphoreType.DMA((n_arr,))],
        compiler_params=_params(("arbitrary",), 56), name="mixer_fwd",
    )(u, *smalls, *shards)


def _mlp_up(x, y, g_mlp, w_out, w1, w2_shard, tm):
    t_len = x.shape[0]
    n_steps = t_len // tm
    n_blk, _, blk = w1.shape

    def body(x_ref, y_ref, gm_ref, wout_hbm, w1_hbm, w2_ref, x1_ref, h2_ref, z_ref, w2_full,
             wout_s, w1_s, sem, send_sems, recv_sems, local_sems):
        step = pl.program_id(0)
        _host_all_gather(step, n_steps, [w2_ref], [w2_full], send_sems, recv_sems, local_sems)

        load_wout = pltpu.make_async_copy(wout_hbm, wout_s, sem.at[0])
        load_w1 = pltpu.make_async_copy(w1_hbm, w1_s, sem.at[1])

        @pl.when(step == 0)
        def _():
            load_wout.start()
            load_w1.start()
            load_wout.wait()

        x1v = x_ref[...] + jnp.dot(y_ref[...], wout_s[...], preferred_element_type=F32)
        x1_ref[...] = x1v
        h2 = (x1v * _rms(x1v) * gm_ref[...]).astype(BF16)
        h2_ref[...] = h2

        @pl.when(step == 0)
        def _():
            load_w1.wait()

        for k in range(n_blk):
            rp = jnp.maximum(jnp.dot(h2, w1_s[k], preferred_element_type=F32), 0.0)
            z_ref[:, k * blk:(k + 1) * blk] = (rp * rp).astype(BF16)

    row_tile = lambda w: pl.BlockSpec((tm, w), lambda i: (i, 0))
    return pl.pallas_call(
        body, grid=(n_steps,),
        in_specs=[row_tile(D_MODEL), row_tile(MIX_WIDTH), pl.BlockSpec((1, D_MODEL), lambda i: (0, 0)),
                  HBM_SPEC, HBM_SPEC, HBM_SPEC],
        out_specs=[row_tile(D_MODEL), row_tile(D_MODEL), row_tile(D_FF), HBM_SPEC],
        out_shape=[jax.ShapeDtypeStruct((t_len, D_MODEL), F32), jax.ShapeDtypeStruct((t_len, D_MODEL), BF16),
                   jax.ShapeDtypeStruct((t_len, D_FF), BF16), jax.ShapeDtypeStruct((N_DEV,) + w2_shard.shape, BF16)],
        scratch_shapes=[pltpu.VMEM(w_out.shape, BF16), pltpu.VMEM(w1.shape, BF16), pltpu.SemaphoreType.DMA((2,))]
        + _exchange_scratch(1, 7) + [pltpu.SemaphoreType.DMA((1,))],
        compiler_params=_params(("arbitrary",), 48), name="mlp_up",
    )(x, y, g_mlp, w_out, w1, w2_shard)


def _mlp_down_bwd(x1, z, target, g_mlp, g_f, w1, w2, tm):
    t_len = x1.shape[0]
    n_steps = t_len // tm
    n_blk, _, blk = w1.shape

    def body(x1_ref, z_ref, tg_ref, gm_ref, gf_ref, w1_hbm, w2_hbm, dx1_ref, dx2_ref, vec_ref, dpre_hbm,
             w1_s, w2_s, dp_s, sem, out_sem):
        step = pl.program_id(0)
        rows = pl.ds(pl.multiple_of(step * tm, tm), tm)
        dp_out = pltpu.make_async_copy(dp_s, dpre_hbm.at[rows, :], out_sem.at[0])

        load_w1 = pltpu.make_async_copy(w1_hbm, w1_s, sem.at[0])
        load_w2 = pltpu.make_async_copy(w2_hbm, w2_s, sem.at[1])

        @pl.when(step == 0)
        def _():
            load_w2.start()
            load_w1.start()
            vec_ref[...] = jnp.zeros(vec_ref.shape, F32)
            load_w2.wait()

        x1v = x1_ref[...]
        g_m = gm_ref[...]
        g_o = gf_ref[...]
        r2 = _rms(x1v)
        x1h = x1v * r2
        x2 = x1v + jnp.dot(z_ref[...], w2_s[...], preferred_element_type=F32)
        r3 = _rms(x2)
        x2h = x2 * r3
        err = x2h * g_o - tg_ref[...]
        dout = err * (1.0 / D_MODEL)
        vec_ref[ROW_LOSS:ROW_LOSS + 1, :] += (0.5 / D_MODEL) * jnp.sum(err * err, axis=0, keepdims=True)
        vec_ref[ROW_GF:ROW_GF + 1, :] += jnp.sum(dout * x2h, axis=0, keepdims=True)
        dx2 = _rms_bwd(dout, x2h, r3, g_o)
        dx2b = dx2.astype(BF16)
        dx2_ref[...] = dx2b
        dh2 = jnp.zeros((tm, D_MODEL), F32)

        @pl.when(step > 0)
        def _():
            dp_out.wait()

        @pl.when(step == 0)
        def _():
            load_w1.wait()

        for k in range(n_blk):
            cols = slice(k * blk, (k + 1) * blk)
            dz = _dot_nt(dx2b, w2_s[cols, :])
            dpb = (dz * 2.0 * jnp.sqrt(z_ref[:, cols].astype(F32))).astype(BF16)
            dp_s[:, cols] = dpb
            dh2 = dh2 + _dot_nt(dpb, w1_s[k])
        dp_out.start()
        vec_ref[ROW_GMLP:ROW_GMLP + 1, :] += jnp.sum(dh2 * x1h, axis=0, keepdims=True)
        dx1_ref[...] = dx2 + _rms_bwd(dh2, x1h, r2, g_m)

        @pl.when(step == n_steps - 1)
        def _():
            dp_out.wait()

    row_tile = lambda w: pl.BlockSpec((tm, w), lambda i: (i, 0))
    vec_spec = pl.BlockSpec((1, D_MODEL), lambda i: (0, 0))
    return pl.pallas_call(
        body, grid=(n_steps,),
        in_specs=[row_tile(D_MODEL), row_tile(D_FF), row_tile(D_MODEL), vec_spec, vec_spec, HBM_SPEC, HBM_SPEC],
        out_specs=[row_tile(D_MODEL), row_tile(D_MODEL), pl.BlockSpec((SUB, D_MODEL), lambda i: (0, 0)), HBM_SPEC],
        out_shape=[jax.ShapeDtypeStruct((t_len, D_MODEL), F32), jax.ShapeDtypeStruct((t_len, D_MODEL), BF16),
                   jax.ShapeDtypeStruct((SUB, D_MODEL), F32), jax.ShapeDtypeStruct((t_len, D_FF), BF16)],
        scratch_shapes=[pltpu.VMEM(w1.shape, BF16), pltpu.VMEM(w2.shape, BF16), pltpu.VMEM((tm, D_FF), BF16),
                        pltpu.SemaphoreType.DMA((2,)), pltpu.SemaphoreType.DMA((1,))],
        compiler_params=_params(("arbitrary",), 56), name="mlp_down_bwd",
    )(x1, z, target, g_mlp, g_f, w1, w2)


def _mixer_bwd(u, hs, dx1, saved, conv_w, rnn_conv_w, rnn_conv_b, wa, wx, lam, gnc, gnr, w_out,
               chip_sums, g_wout, tm):
    t_len = u.shape[0]
    n_tiles = t_len // tm
    n_chunks = tm // SUB
    per_tile = tm // SUB
    n_sums = len(chip_sums)

    def body(u_ref, up_ref, hs_ref, hp_ref, dx1_ref, xr_ref, ra_ref, ii_ref, mult_ref,
             cw_ref, rw_ref, rb_ref, wa_ref, wx_ref, lam_ref, gnc_ref, gnr_ref, wout_ref, *rest):
        hsends = rest[0:n_sums]
        gwout_ref = rest[n_sums]
        du_ref, vec_ref, wab_ref = rest[n_sums + 1:n_sums + 4]
        hrecvs = rest[n_sums + 4:2 * n_sums + 4]
        sib_wout = rest[2 * n_sums + 4]
        (du_s, dy_s, dpa_s, dpx_s, dxr_s, wabd, wxbd, acc, dwa_acc, dwx_acc,
         a_car, dh_car, dcq_car, dxr_car, i_send, i_recv, d_send, d_recv) = rest[2 * n_sums + 5:]
        step = pl.program_id(0)
        _host_chip_exchange(step, n_tiles, hsends, hrecvs, i_send, i_recv)
        _host_pair_exchange(step, n_tiles, [gwout_ref], [sib_wout], d_send, d_recv)
        has_prev = (step < n_tiles - 1).astype(F32)

        @pl.when(step == 0)
        def _():
            acc[...] = jnp.zeros(acc.shape, F32)
            dwa_acc[...] = jnp.zeros(dwa_acc.shape, F32)
            dwx_acc[...] = jnp.zeros(dwx_acc.shape, F32)
            a_car[...] = jnp.ones(a_car.shape, F32)
            dh_car[...] = jnp.zeros(dh_car.shape, F32)
            dcq_car[...] = jnp.zeros(dcq_car.shape, F32)
            dxr_car[...] = jnp.zeros(dxr_car.shape, F32)
            wabd[...] = _expand_heads(wa_ref[...])
            wxbd[...] = _expand_heads(wx_ref[...])

        row_c = lax.broadcasted_iota(jnp.int32, (SUB, CONV_WIDTH), 0)
        row_r = lax.broadcasted_iota(jnp.int32, (SUB, LRU_WIDTH), 0)
        cw = cw_ref[...]
        rw = rw_ref[...]
        rb = rb_ref[...]
        g_c = gnc_ref[...]
        g_r = gnr_ref[...]
        sp_c = LRU_C * _softplus_neg(lam_ref[...])

        up = up_ref[...] * has_prev
        cv_before = up[:, OFF_GC:OFF_GC + CONV_WIDTH] * up[:, OFF_V:OFF_V + CONV_WIDTH]
        xin_before = up[:, OFF_XR:OFF_XR + LRU_WIDTH]
        hs_before = hp_ref[...] * has_prev

        dy_s[...] = _dot_nt(dx1_ref[...].astype(BF16), wout_ref[...])

        xrb = xr_ref[...].astype(BF16)

        def recur_bwd(j, carry):
            a_later, dh_later = carry
            i = n_chunks - 1 - j
            r = pl.multiple_of(i * SUB, SUB)
            rp = pl.multiple_of(jnp.maximum(i - 1, 0) * SUB, SUB)
            xr = xr_ref[pl.ds(r, SUB), :]
            hs_c = hs_ref[pl.ds(r, SUB), :]
            hs_prev = jnp.where(i == 0, hs_before, hs_ref[pl.ds(rp, SUB), :])
            h_m1 = _down(hs_c, hs_prev, 1, row_r)
            ra = ra_ref[pl.ds(r, SUB), :]
            ii = ii_ref[pl.ds(r, SUB), :]
            mult = mult_ref[pl.ds(r, SUB), :]
            a = jnp.exp(-ra * sp_c)
            inv_mult = lax.rsqrt(mult * mult)
            ge, dge = _gelu(u_ref[pl.ds(r, SUB), OFF_G:OFF_G + LRU_WIDTH])
            y_r = hs_c * ge
            rr = _rms(y_r)
            yhat = y_r * rr
            dyn = dy_s[pl.ds(r, SUB), CONV_WIDTH:MIX_WIDTH]
            acc[ACC_GNR] += dyn * yhat
            dy_r = _rms_bwd(dyn, yhat, rr, g_r)
            du_s[pl.ds(r, SUB), OFF_G:OFF_G + LRU_WIDTH] = dy_r * hs_c * dge
            a_cum, d_cum = _scan8_rev(_up(a, a_later, 1, row_r), dy_r * ge, row_r)
            dh = a_cum * dh_later + d_cum
            dm = dh * mult
            dii = dm * xr
            dxr_s[pl.ds(r, SUB), :] = dm * ii
            dla = a * dh * (h_m1 - (ii * xr) * a * inv_mult)
            dla_r = dla * ra
            acc[ACC_SP] -= dla_r
            dpa = dla_r * (sp_c * (ra - 1.0))
            dpx = dii * ii * (1.0 - ii)
            acc[ACC_BA] += dpa
            acc[ACC_BX] += dpx
            dpa_s[pl.ds(r, SUB), :] = dpa
            dpx_s[pl.ds(r, SUB), :] = dpx
            return a, dh[0:1, :]

        a_first, dh_first = _chunk_loop(n_chunks, recur_bwd, (a_car[...], dh_car[...]))
        a_car[...] = a_first
        dh_car[...] = dh_first

        dpab = dpa_s[...].astype(BF16)
        dpxb = dpx_s[...].astype(BF16)
        dxr_s[...] += _block_diag_apply_t(dpab, wabd) + _block_diag_apply_t(dpxb, wxbd)
        for g in range(LRU_WIDTH // GROUP):
            cols = slice(g * GROUP, (g + 1) * GROUP)
            dwa_acc[cols, :] += _dot_tn(xrb[:, cols], dpab[:, cols])
            dwx_acc[cols, :] += _dot_tn(xrb[:, cols], dpxb[:, cols])

        def convs_bwd(j, carry):
            dcq_later, dxr_later = carry
            i = n_chunks - 1 - j
            r = pl.multiple_of(i * SUB, SUB)
            rp = pl.multiple_of(jnp.maximum(i - 1, 0) * SUB, SUB)
            cv_prev = jnp.where(i == 0, cv_before,
                                u_ref[pl.ds(rp, SUB), OFF_GC:OFF_GC + CONV_WIDTH]
                                * u_ref[pl.ds(rp, SUB), OFF_V:OFF_V + CONV_WIDTH])
            gb, gc, v, cv, cv_m1, cv_m2, cq = _conv3_chunk(u_ref, r, cv_prev, cw, row_c)
            y_c = gb * cq
            rc = _rms(y_c)
            yhat = y_c * rc
            dyn = dy_s[pl.ds(r, SUB), 0:CONV_WIDTH]
            acc[ACC_GNC, :, 0:CONV_WIDTH] += dyn * yhat
            dy_c = _rms_bwd(dyn, yhat, rc, g_c)
            dcq = dy_c * gb
            dcv = (cw[2:3, :] * dcq + cw[1:2, :] * _up(dcq, dcq_later, 1, row_c)
                   + cw[0:1, :] * _up(dcq, dcq_later, 2, row_c))
            acc[ACC_CW + 2, :, 0:CONV_WIDTH] += dcq * cv
            acc[ACC_CW + 1, :, 0:CONV_WIDTH] += dcq * cv_m1
            acc[ACC_CW + 0, :, 0:CONV_WIDTH] += dcq * cv_m2
            du_s[pl.ds(r, SUB), OFF_GB:OFF_GB + CONV_WIDTH] = dy_c * cq
            du_s[pl.ds(r, SUB), OFF_GC:OFF_GC + CONV_WIDTH] = dcv * v
            du_s[pl.ds(r, SUB), OFF_V:OFF_V + CONV_WIDTH] = dcv * gc

            xin_prev = jnp.where(i == 0, xin_before, u_ref[pl.ds(rp, SUB), OFF_XR:OFF_XR + LRU_WIDTH])
            xin, m1, m2, m3, _ = _conv4_chunk(u_ref, r, xin_prev, rw, rb, row_r)
            dxr = dxr_s[pl.ds(r, SUB), :]
            du_s[pl.ds(r, SUB), OFF_XR:OFF_XR + LRU_WIDTH] = (
                rw[3:4, :] * dxr + rw[2:3, :] * _up(dxr, dxr_later, 1, row_r)
                + rw[1:2, :] * _up(dxr, dxr_later, 2, row_r) + rw[0:1, :] * _up(dxr, dxr_later, 3, row_r))
            acc[ACC_RW + 3] += dxr * xin
            acc[ACC_RW + 2] += dxr * m1
            acc[ACC_RW + 1] += dxr * m2
            acc[ACC_RW + 0] += dxr * m3
            acc[ACC_BR] += dxr
            return dcq, dxr

        dcq_first, dxr_first = _chunk_loop(n_chunks, convs_bwd, (dcq_car[...], dxr_car[...]))
        dcq_car[...] = dcq_first
        dxr_car[...] = dxr_first

        du_ref[...] = du_s[...].astype(BF16)

        @pl.when(step == n_tiles - 1)
        def _():
            vec_ref[...] = jnp.zeros(vec_ref.shape, F32)
            rows = {ACC_GNC: ROW_GNC, ACC_GNR: ROW_GNR, ACC_BR: ROW_BR, ACC_BA: ROW_BA, ACC_BX: ROW_BX}
            for k in range(3):
                rows[ACC_CW + k] = ROW_CW + k
            for k in range(4):
                rows[ACC_RW + k] = ROW_RW + k
            for slot, out_row in rows.items():
                o = out_row - ROW_GNC
                vec_ref[o:o + 1, :] = jnp.sum(acc[slot], axis=0, keepdims=True)
            lam_v = lam_ref[...]
            dsp = jnp.sum(acc[ACC_SP], axis=0, keepdims=True)
            o = ROW_LAM - ROW_GNC
            vec_ref[o:o + 1, :] = -dsp * LRU_C / (1.0 + jnp.exp(lam_v))
            wab_ref[0:LRU_WIDTH, :] = _fold_heads(dwa_acc[...])
            wab_ref[LRU_WIDTH:2 * LRU_WIDTH, :] = _fold_heads(dwx_acc[...])

    rev = lambda w: pl.BlockSpec((tm, w), lambda s: (n_tiles - 1 - s, 0))
    before = lambda w: pl.BlockSpec((SUB, w), lambda s: (jnp.maximum((n_tiles - 1 - s) * per_tile - 1, 0), 0))
    whole = lambda a: pl.BlockSpec(a.shape, lambda s: (0,) * a.ndim)
    smalls = (conv_w, rnn_conv_w, rnn_conv_b, wa, wx, lam, gnc, gnr, w_out)
    full = lambda w: pltpu.VMEM((tm, w), F32)
    return pl.pallas_call(
        body, grid=(n_tiles,),
        in_specs=[rev(IN_COLS), before(IN_COLS), rev(LRU_WIDTH), before(LRU_WIDTH), rev(D_MODEL)]
        + [rev(LRU_WIDTH)] * len(saved) + [whole(a) for a in smalls] + [HBM_SPEC] * (n_sums + 1),
        out_specs=[rev(IN_COLS), pl.BlockSpec((16, D_MODEL), lambda s: (0, 0)),
                   pl.BlockSpec((2 * LRU_WIDTH, HEAD_DIM), lambda s: (0, 0))] + [HBM_SPEC] * (n_sums + 1),
        out_shape=[jax.ShapeDtypeStruct((t_len, IN_COLS), BF16), jax.ShapeDtypeStruct((16, D_MODEL), F32),
                   jax.ShapeDtypeStruct((2 * LRU_WIDTH, HEAD_DIM), F32)]
        + [jax.ShapeDtypeStruct(s.shape, BF16) for s in chip_sums]
        + [jax.ShapeDtypeStruct((4,) + g_wout.shape[1:], BF16)],
        scratch_shapes=[full(IN_COLS), full(MIX_WIDTH), full(LRU_WIDTH), full(LRU_WIDTH), full(LRU_WIDTH),
                        pltpu.VMEM((LRU_WIDTH, GROUP), BF16), pltpu.VMEM((LRU_WIDTH, GROUP), BF16),
                        pltpu.VMEM((N_ACC, SUB, LRU_WIDTH), F32),
                        pltpu.VMEM((LRU_WIDTH, GROUP), F32), pltpu.VMEM((LRU_WIDTH, GROUP), F32),
                        pltpu.VMEM((SUB, LRU_WIDTH), F32), pltpu.VMEM((1, LRU_WIDTH), F32),
                        pltpu.VMEM((SUB, CONV_WIDTH), F32), pltpu.VMEM((SUB, LRU_WIDTH), F32)]
        + _exchange_scratch(n_sums, 3) + _exchange_scratch(1, 4),
        compiler_params=_params(("arbitrary",), 56), name="mixer_bwd",
    )(u, u, hs, hs, dx1, *saved, *smalls, *chip_sums, g_wout)


def _in_proj_bwd(du, dx1, x, g_mix, win_t, tm, chip_sums, g_own):
    t_len = x.shape[0]
    n_steps = t_len // tm

    def body(du_ref, dx1_ref, x_ref, g_ref, w_ref, hs_ref, gown_ref,
             dx_ref, vec_ref, landed_ref, sib_ref, i_send, i_recv, d_send, d_recv):
        step = pl.program_id(0)
        _host_chip_exchange(step, n_steps, [hs_ref], [landed_ref], i_send, i_recv)
        _host_half_exchange(step, n_steps, gown_ref, sib_ref, d_send, d_recv)

        @pl.when(step == 0)
        def _():
            vec_ref[...] = jnp.zeros(vec_ref.shape, F32)

        dh = jnp.dot(du_ref[...], w_ref[...], preferred_element_type=F32)
        xv = x_ref[...]
        r1 = _rms(xv)
        xh = xv * r1
        vec_ref[0:1, :] += jnp.sum(dh * xh, axis=0, keepdims=True)
        dx_ref[...] = dx1_ref[...] + _rms_bwd(dh, xh, r1, g_ref[...])

    row_tile = lambda w: pl.BlockSpec((tm, w), lambda i: (i, 0))
    half_shape = (g_own.shape[0], g_own.shape[1] // 2, g_own.shape[2])
    return pl.pallas_call(
        body, grid=(n_steps,),
        in_specs=[row_tile(IN_COLS), row_tile(D_MODEL), row_tile(D_MODEL), pl.BlockSpec((1, D_MODEL), lambda i: (0, 0)),
                  pl.BlockSpec((IN_COLS, D_MODEL), lambda i: (0, 0))] + [HBM_SPEC] * 2,
        out_specs=[row_tile(D_MODEL), pl.BlockSpec((SUB, D_MODEL), lambda i: (0, 0))] + [HBM_SPEC] * 2,
        out_shape=[jax.ShapeDtypeStruct((t_len, D_MODEL), F32), jax.ShapeDtypeStruct((SUB, D_MODEL), F32),
                   jax.ShapeDtypeStruct(chip_sums.shape, BF16), jax.ShapeDtypeStruct(half_shape, BF16)],
        scratch_shapes=_exchange_scratch(1, 3) + [pltpu.SemaphoreType.DMA((1,)), pltpu.SemaphoreType.DMA((1,))],
        compiler_params=_params(("arbitrary",), 56), name="in_proj_bwd",
    )(du, dx1, x, g_mix, win_t, chip_sums, g_own)


def _tn_weight_grad(a, b, tk, name, pair=(), col_blocks=1):
    t_len, m = a.shape
    n = b.shape[1]
    n_steps = t_len // tk
    sent = tuple(pair)
    n_sent = len(sent)

    def body(a_ref, b_ref, *rest):
        srcs = rest[0:n_sent]
        o_ref = rest[n_sent]
        dsts = rest[n_sent + 1:2 * n_sent + 1]
        acc = rest[2 * n_sent + 1]
        sems = rest[2 * n_sent + 2:]
        j = pl.program_id(0)
        if pair:
            _host_pair_exchange(j, n_steps, srcs, dsts, *sems)

        @pl.when(j == 0)
        def _():
            acc[...] = jnp.zeros(acc.shape, F32)

        acc[...] += _dot_tn(a_ref[...].astype(BF16), b_ref[...].astype(BF16))

        @pl.when(j == n_steps - 1)
        def _():
            if col_blocks == 1:
                o_ref[...] = acc[...].astype(BF16)
            else:
                for k in range(col_blocks):
                    o_ref[k] = acc[:, k * nb:(k + 1) * nb].astype(BF16)

    nb = n // col_blocks
    out_dims = (m, n) if col_blocks == 1 else (col_blocks, m, nb)
    landed = [jax.ShapeDtypeStruct((4,) + g.shape[1:], BF16) for g in pair]
    scratch = [pltpu.VMEM((m, n), F32)]
    if n_sent:
        scratch += _exchange_scratch(n_sent, 4)
    return pl.pallas_call(
        body, grid=(n_steps,),
        in_specs=[pl.BlockSpec((tk, m), lambda j: (j, 0)), pl.BlockSpec((tk, n), lambda j: (j, 0))]
        + [HBM_SPEC] * n_sent,
        out_specs=[pl.BlockSpec(out_dims, lambda j: (0,) * len(out_dims))] + [HBM_SPEC] * n_sent,
        out_shape=[jax.ShapeDtypeStruct(out_dims, BF16)] + landed,
        scratch_shapes=scratch,
        compiler_params=_params(("arbitrary",), 56), name=name,
    )(a, b, *sent)


def _w_in_grad_part(du, h, tk, name, chip_ids, chip=(), halves=None, small=None):
    t_len = du.shape[0]
    n_t = t_len // tk
    n_q = chip_ids.shape[0]
    width = 2 * (IN_COLS // N_DEV)
    n_steps = n_q * n_t
    n_chip = len(chip)
    sent = tuple(chip) + (() if halves is None else (halves,)) + (() if small is None else tuple(small))
    n_sent = len(sent)

    def body(ids_ref, a_ref, b_ref, *rest):
        srcs = rest[0:n_sent]
        o_ref = rest[n_sent]
        dsts = rest[n_sent + 1:2 * n_sent + 1]
        acc = rest[2 * n_sent + 1]
        sems = list(rest[2 * n_sent + 2:])
        j = pl.program_id(1)
        step = pl.program_id(0) * n_t + j
        if chip:
            _host_chip_exchange(step, n_steps, srcs[0:n_chip], dsts[0:n_chip], sems.pop(0), sems.pop(0))
        if halves is not None:
            _host_half_exchange(step, n_steps, srcs[n_chip], dsts[n_chip], sems.pop(0), sems.pop(0))
        if small is not None:
            _host_small_exchange(step, n_steps, *srcs[n_sent - 3:], *dsts[n_sent - 3:], *sems)

        @pl.when(j == 0)
        def _():
            acc[...] = jnp.zeros(acc.shape, F32)

        acc[...] += _dot_tn(a_ref[...], b_ref[...])

        @pl.when(j == n_t - 1)
        def _():
            o_ref[0] = acc[...].astype(BF16)

    landed = [jax.ShapeDtypeStruct(s.shape, BF16) for s in chip]
    scratch = [pltpu.VMEM((width, D_MODEL), F32)]
    if chip:
        scratch += _exchange_scratch(len(chip), 3)
    if halves is not None:
        landed.append(jax.ShapeDtypeStruct((halves.shape[0], halves.shape[1] // 2, halves.shape[2]), BF16))
        scratch += [pltpu.SemaphoreType.DMA((halves.shape[0],)), pltpu.SemaphoreType.DMA((halves.shape[0],))]
    if small is not None:
        vec_m, vec_b, wab = small
        landed += [jax.ShapeDtypeStruct((N_DEV,) + vec_m.shape, F32), jax.ShapeDtypeStruct((N_DEV,) + vec_b.shape, F32),
                   jax.ShapeDtypeStruct((N_DEV, wab.shape[0] // N_DEV, wab.shape[1]), F32)]
        scratch += _exchange_scratch(3, N_DEV) + [pltpu.SemaphoreType.DMA((2,))]
    grid_spec = pltpu.PrefetchScalarGridSpec(
        num_scalar_prefetch=1, grid=(n_q, n_t),
        in_specs=[pl.BlockSpec((tk, width), lambda q, j, ids: (j, ids[q])),
                  pl.BlockSpec((tk, D_MODEL), lambda q, j, ids: (j, 0))] + [HBM_SPEC] * n_sent,
        out_specs=[pl.BlockSpec((1, width, D_MODEL), lambda q, j, ids: (q, 0, 0))] + [HBM_SPEC] * n_sent,
        scratch_shapes=scratch)
    return pl.pallas_call(
        body, grid_spec=grid_spec, out_shape=[jax.ShapeDtypeStruct((n_q, width, D_MODEL), BF16)] + landed,
        compiler_params=_params(("arbitrary", "arbitrary"), 40), name=name,
    )(chip_ids, du, h, *sent)


def _adamw(w, g, m, v):
    m = ADAM_B1 * m + (1.0 - ADAM_B1) * g
    v = ADAM_B2 * v + (1.0 - ADAM_B2) * (g * g)
    delta = -ADAM_LR * ((m / BC1) / (jnp.sqrt(v / BC2) + ADAM_EPS) + ADAM_WD * w)
    return delta, m, v


def _update_sharded(g, landed, w, m, v, rows_blk, name):
    rows, cols = w.shape

    def body(g_ref, l_ref, w_ref, m_ref, v_ref, og, od, om, ov):
        gv = g_ref[...]
        for j in range(3):
            gv = gv + l_ref[j].astype(F32)
        delta, mn, vn = _adamw(w_ref[...], gv, m_ref[...], v_ref[...])
        og[...] = gv
        od[...] = delta
        om[...] = mn
        ov[...] = vn

    blk = pl.BlockSpec((rows_blk, cols), lambda i: (i, 0))
    shape = pltpu.HBM((rows, cols), F32)
    return pl.pallas_call(
        body, grid=(rows // rows_blk,),
        in_specs=[blk, pl.BlockSpec((3, rows_blk, cols), lambda i: (0, i, 0)), blk, blk, blk],
        out_specs=[blk] * 4, out_shape=[shape] * 4,
        compiler_params=_params(("arbitrary",), 32), name=name,
    )(*_in_hbm(g, landed, w, m, v))


def _update_w_in(g_own, sib_own, landed, w_t, m_t, v_t, core, cols_blk):
    rows, cols = w_t.shape

    def body(core_ref, g_ref, s_ref, l_ref, w_ref, m_ref, v_ref, og, od, om, ov):
        gv = g_ref[0, 0].astype(F32) + s_ref[0].astype(F32)
        for j in range(3):
            gv = gv + l_ref[j].astype(F32)
        delta, mn, vn = _adamw(w_ref[...], gv, m_ref[...], v_ref[...])
        og[...] = gv
        od[...] = delta
        om[...] = mn
        ov[...] = vn

    blk = pl.BlockSpec((rows, cols_blk), lambda i, cr: (0, i))
    grid_spec = pltpu.PrefetchScalarGridSpec(
        num_scalar_prefetch=1, grid=(cols // cols_blk,),
        in_specs=[pl.BlockSpec((1, 1, rows, cols_blk), lambda i, cr: (0, cr[0], 0, i)),
                  pl.BlockSpec((1, rows, cols_blk), lambda i, cr: (0, 0, i)),
                  pl.BlockSpec((3, rows, cols_blk), lambda i, cr: (0, 0, i)), blk, blk, blk],
        out_specs=[blk] * 4)
    return pl.pallas_call(
        body, grid_spec=grid_spec, out_shape=[pltpu.HBM((rows, cols), F32)] * 4,
        compiler_params=_params(("arbitrary",), 32), name="update_w_in",
    )(core, *_in_hbm(g_own.reshape(1, 2, rows, cols), sib_own, landed, w_t, m_t, v_t))


def _update_small(vsum, wsum, g_cw, g_rw, weights, moments_m, moments_v):
    n = len(weights)

    def body(*refs):
        vs, ws, gcw, grw = refs[0:4]
        w_refs = refs[4:4 + n]
        m_refs = refs[4 + n:4 + 2 * n]
        v_refs = refs[4 + 2 * n:4 + 3 * n]
        outs = refs[4 + 3 * n:]
        loss_ref = outs[0]
        loss_ref[...] = jnp.sum(vs[ROW_LOSS:ROW_LOSS + 1, :], axis=1, keepdims=True)
        grads = [
            vs[ROW_GMIX:ROW_GMIX + 1, :], gcw[...], grw[...], vs[ROW_BR:ROW_BR + 1, :],
            ws[0:LRU_WIDTH, :], vs[ROW_BA:ROW_BA + 1, :], ws[LRU_WIDTH:2 * LRU_WIDTH, :], vs[ROW_BX:ROW_BX + 1, :],
            vs[ROW_LAM:ROW_LAM + 1, :], vs[ROW_GNC:ROW_GNC + 1, 0:CONV_WIDTH], vs[ROW_GNR:ROW_GNR + 1, :],
            vs[ROW_GMLP:ROW_GMLP + 1, :], vs[ROW_GF:ROW_GF + 1, :],
        ]
        for k in range(n):
            gk = grads[k]
            delta, mn, vn = _adamw(w_refs[k][...], gk, m_refs[k][...], v_refs[k][...])
            outs[1 + 4 * k][...] = gk
            outs[2 + 4 * k][...] = delta
            outs[3 + 4 * k][...] = mn
            outs[4 + 4 * k][...] = vn

    whole = lambda a: pl.BlockSpec(a.shape, lambda i: (0,) * len(a.shape))
    out_shape = [jax.ShapeDtypeStruct((1, 1), F32)]
    for w in weights:
        out_shape += [jax.ShapeDtypeStruct(w.shape, F32)] * 4
    args = (vsum, wsum, g_cw, g_rw, *weights, *moments_m, *moments_v)
    return pl.pallas_call(
        body, grid=(1,), out_shape=out_shape, in_specs=[whole(a) for a in args], out_specs=[whole(s) for s in out_shape],
        compiler_params=_params(("arbitrary",), 32), name="update_small",
    )(*args)


def kernel(x, norm_mix_g, w_in, conv_w, rnn_conv_w, rnn_conv_b, w_a, b_a, w_x, b_x, lru_lambda, g_norm_conv, g_norm_rnn, w_out, norm_mlp_g, w_mlp_in, w_mlp_out, final_norm_g, loss_target, m_norm_mix_g, m_w_in, m_conv_w, m_rnn_conv_w, m_rnn_conv_b, m_w_a, m_b_a, m_w_x, m_b_x, m_lru_lambda, m_g_norm_conv, m_g_norm_rnn, m_w_out, m_norm_mlp_g, m_w_mlp_in, m_w_mlp_out, m_final_norm_g, v_norm_mix_g, v_w_in, v_conv_w, v_rnn_conv_w, v_rnn_conv_b, v_w_a, v_b_a, v_w_x, v_b_x, v_lru_lambda, v_g_norm_conv, v_g_norm_rnn, v_w_out, v_norm_mlp_g, v_w_mlp_in, v_w_mlp_out, v_final_norm_g):
    t_len = x.shape[1]
    my_id = 4 * lax.axis_index("x") + 2 * lax.axis_index("y") + lax.axis_index("c")
    tm = min(256, t_len)
    tb = min(512, t_len)
    tk = min(512, t_len)

    xs = x.reshape(t_len, D_MODEL)
    tgt = loss_target.reshape(t_len, D_MODEL)
    flat = lambda a: a.reshape(a.shape[-2:]) if a.ndim == 3 else a.reshape(1, -1)
    heads = lambda a: a.reshape(LRU_WIDTH, HEAD_DIM)

    turned = lambda a: jnp.transpose(flat(a))
    win_shard, wout_shard, w1_shard, w2_shard, cp_shard = _prep_shards(
        turned(w_in), flat(w_out), flat(w_mlp_in), flat(w_mlp_out), flat(conv_w), flat(rnn_conv_w))

    u, h, win_t, cp_full = _in_proj(xs, flat(norm_mix_g), (win_shard, cp_shard), min(1024, t_len))
    cpack = cp_full.reshape(N_DEV, 8, 128)
    conv_full = jnp.transpose(cpack[:, 0:3, 0:64], (1, 0, 2)).reshape(3, CONV_WIDTH)
    rnn_full = jnp.transpose(cpack[:, 3:7, :], (1, 0, 2)).reshape(4, LRU_WIDTH)
    mixer_small = (conv_full, rnn_full, flat(rnn_conv_b), heads(w_a), flat(b_a), heads(w_x), flat(b_x),
                   flat(lru_lambda), flat(g_norm_conv), flat(g_norm_rnn))
    hs, y, xr, gate_r, gate_i, mult, w1_blk, wout_blk = _mixer_fwd(u, *mixer_small, (w1_shard, wout_shard), tm)
    wout_f = wout_blk.reshape(MIX_WIDTH, D_MODEL)
    x1, h2, z, w2_blk = _mlp_up(xs, y, flat(norm_mlp_g), wout_f, w1_blk, w2_shard, tb)
    dx1, dx2, vec_m, dpre = _mlp_down_bwd(x1, z, tgt, flat(norm_mlp_g), flat(final_norm_g), w1_blk,
                                          w2_blk.reshape(D_FF, D_MODEL), tb)
    (g_w1,) = _tn_weight_grad(h2, dpre, tk, "w_mlp_in_grad", col_blocks=N_DEV)
    (g_w2,) = _tn_weight_grad(z, dx2, tk, "w_mlp_out_grad")
    g_w2 = g_w2.reshape(N_DEV, D_FF // N_DEV, D_MODEL)
    g_wout, sib_w1, sib_w2 = _tn_weight_grad(y, dx1, tk, "w_out_grad", pair=(g_w1, g_w2))
    g_wout = g_wout.reshape(N_DEV, MIX_WIDTH // N_DEV, D_MODEL)
    hsend_w1, own_w1, hsend_w2, own_w2 = _pair_sum((g_w1, g_w2), (sib_w1, sib_w2), "pair_sum_w_mlp")
    du, vec_b, wab, landed_w1, landed_w2, sib_wout = _mixer_bwd(
        u, hs, dx1, (xr, gate_r, gate_i, mult), conv_full, rnn_full, flat(rnn_conv_b), heads(w_a), heads(w_x),
        flat(lru_lambda), flat(g_norm_conv), flat(g_norm_rnn), wout_f, (hsend_w1, hsend_w2), g_wout, tm)
    hsend_wout, own_wout = _pair_sum((g_wout,), (sib_wout,), "pair_sum_w_out")
    ax, ay, ac = lax.axis_index("x"), lax.axis_index("y"), lax.axis_index("c")
    chip_ids = jnp.stack([2 * cx + cy for cx, cy in [(ax, ay)] + _other_chips(ax, ay)]).astype(jnp.int32)
    core = jnp.reshape(ac, (1,)).astype(jnp.int32)
    tw = min(1024, t_len)
    g_others, landed_wout, vrecv_m, vrecv_b, wrecv = _w_in_grad_part(
        du, h, tw, "w_in_grad_others", chip_ids[1:4], chip=(hsend_wout,), small=(vec_m, vec_b, wab))
    g_own, sib_others = _w_in_grad_part(du, h, tw, "w_in_grad_own", chip_ids[0:1], halves=g_others)
    hsend_win = _pair_sum_parts(g_others, sib_others, core)
    grad_x, vec_x, landed_win, sib_own = _in_proj_bwd(du, dx1, xs, flat(norm_mix_g), win_t, tm, hsend_win, g_own)

    vsum, wsum = _final_small(vrecv_m, vrecv_b, wab, wrecv, vec_x)

    up_win = _update_w_in(g_own, sib_own, landed_win, turned(w_in), turned(m_w_in), turned(v_w_in), core, 256)
    up_win = [jnp.transpose(a) for a in up_win]
    up_wout = _update_sharded(own_wout, landed_wout, flat(w_out), flat(m_w_out), flat(v_w_out), 96, "update_w_out")
    up_w1 = _update_sharded(own_w1, landed_w1, flat(w_mlp_in), flat(m_w_mlp_in), flat(v_w_mlp_in), 256,
                            "update_w_mlp_in")
    up_w2 = _update_sharded(own_w2, landed_w2, flat(w_mlp_out), flat(m_w_mlp_out), flat(v_w_mlp_out), 256,
                            "update_w_mlp_out")

    g_cw = lax.dynamic_slice(vsum, (ROW_CW, 64 * my_id), (3, 64))
    g_rw = lax.dynamic_slice(vsum, (ROW_RW, 128 * my_id), (4, 128))
    small_w = (norm_mix_g, conv_w, rnn_conv_w, rnn_conv_b, w_a, b_a, w_x, b_x, lru_lambda, g_norm_conv, g_norm_rnn,
               norm_mlp_g, final_norm_g)
    small_m = (m_norm_mix_g, m_conv_w, m_rnn_conv_w, m_rnn_conv_b, m_w_a, m_b_a, m_w_x, m_b_x, m_lru_lambda,
               m_g_norm_conv, m_g_norm_rnn, m_norm_mlp_g, m_final_norm_g)
    small_v = (v_norm_mix_g, v_conv_w, v_rnn_conv_w, v_rnn_conv_b, v_w_a, v_b_a, v_w_x, v_b_x, v_lru_lambda,
               v_g_norm_conv, v_g_norm_rnn, v_norm_mlp_g, v_final_norm_g)
    is_heads = (False, False, False, False, True, False, True, False, False, False, False, False, False)
    as2d = lambda arrs: [heads(a) if hd else flat(a) for a, hd in zip(arrs, is_heads)]
    small_out = _update_small(vsum, wsum, g_cw, g_rw, as2d(small_w), as2d(small_m), as2d(small_v))
    loss = small_out[0].reshape(())

    names = ["norm_mix_g", "w_in", "conv_w", "rnn_conv_w", "rnn_conv_b", "w_a", "b_a", "w_x", "b_x", "lru_lambda",
             "g_norm_conv", "g_norm_rnn", "w_out", "norm_mlp_g", "w_mlp_in", "w_mlp_out", "final_norm_g"]
    originals = dict(zip(names, (norm_mix_g, w_in, conv_w, rnn_conv_w, rnn_conv_b, w_a, b_a, w_x, b_x, lru_lambda,
                                 g_norm_conv, g_norm_rnn, w_out, norm_mlp_g, w_mlp_in, w_mlp_out, final_norm_g)))
    results = {"w_in": up_win, "w_out": up_wout, "w_mlp_in": up_w1, "w_mlp_out": up_w2}
    small_names = ["norm_mix_g", "conv_w", "rnn_conv_w", "rnn_conv_b", "w_a", "b_a", "w_x", "b_x", "lru_lambda",
                   "g_norm_conv", "g_norm_rnn", "norm_mlp_g", "final_norm_g"]
    for k, nm in enumerate(small_names):
        results[nm] = small_out[1 + 4 * k:5 + 4 * k]
    out = [loss, grad_x.reshape(x.shape)]
    for kind in range(4):
        out += [results[nm][kind].reshape(originals[nm].shape) for nm in names]
    return tuple(out)
```

```python
import functools

import jax
import jax.numpy as jnp
from jax import lax
from jax.experimental import pallas as pl
from jax.experimental.pallas import tpu as pltpu

F32 = jnp.float32
BF16 = jnp.bfloat16

D_MODEL = 1024
HEAD_DIM = 64
CONV_WIDTH = 512
LRU_WIDTH = 1024
MIX_WIDTH = CONV_WIDTH + LRU_WIDTH
IN_COLS = 3 * CONV_WIDTH + 2 * LRU_WIDTH
D_FF = 4 * D_MODEL
GROUP = 256
EPS = 1e-6
LRU_C = 8.0
N_DEV = 8
SUB = 8

OFF_GB, OFF_GC, OFF_V, OFF_XR, OFF_G = 0, 512, 1024, 1536, 2560

ADAM_LR, ADAM_B1, ADAM_B2, ADAM_EPS, ADAM_WD, ADAM_STEP = 0.001, 0.9, 0.999, 1e-08, 0.01, 10
BC1 = 1.0 - ADAM_B1 ** ADAM_STEP
BC2 = 1.0 - ADAM_B2 ** ADAM_STEP

MIB = 1024 * 1024
MESH = pl.DeviceIdType.MESH

VEC_ROWS = 32
ROW_GF, ROW_GMLP, ROW_LOSS = 0, 1, 2
ROW_GNC, ROW_GNR, ROW_BR, ROW_BA, ROW_BX, ROW_LAM, ROW_CW, ROW_RW = 8, 9, 10, 11, 12, 13, 14, 17
ROW_GMIX = 24
ACC_GNC, ACC_GNR, ACC_BR, ACC_BA, ACC_BX, ACC_SP, ACC_CW, ACC_RW, N_ACC = 0, 1, 2, 3, 4, 5, 6, 9, 13


def _params(semantics=None, vmem_mib=48):
    return pltpu.CompilerParams(dimension_semantics=semantics, vmem_limit_bytes=vmem_mib * MIB)


def _rms(x):
    return lax.rsqrt(jnp.mean(x * x, axis=-1, keepdims=True) + EPS)


def _rms_bwd(dy, xhat, r, g):
    dyh = dy * g
    return r * (dyh - xhat * jnp.mean(dyh * xhat, axis=-1, keepdims=True))


def _sigmoid(x):
    return 0.5 + 0.5 * jnp.tanh(0.5 * x)


def _gelu(x):
    c0, c1 = 0.7978845608028654, 0.044715
    x2 = x * x
    t = jnp.tanh(x * (c0 + (c0 * c1) * x2))
    half = 0.5 + 0.5 * t
    ge = x * half
    dge = half + (ge - ge * half) * (2.0 * c0 + (6.0 * c0 * c1) * x2)
    return ge, dge


def _softplus_neg(lam):
    z = -lam
    e = jnp.exp(-jnp.abs(z))
    return jnp.maximum(z, 0.0) + jnp.where(e < 1e-4, e * (1.0 - 0.5 * e), jnp.log(1.0 + e))


def _lru_gates(pa, px, sp_c):
    ra = _sigmoid(pa)
    ii = _sigmoid(px)
    neg_la = ra * sp_c
    a = jnp.exp(-neg_la)
    m2 = jnp.tanh(neg_la) * (1.0 + a * a)
    mult = jnp.where(m2 > 0.0, m2 * lax.rsqrt(m2), 0.0)
    return ra, ii, a, mult


def _down(cur, prev, s, row):
    return pltpu.roll(jnp.where(row < SUB - s, cur, prev), s, 0)


def _up(cur, nxt, s, row):
    return pltpu.roll(jnp.where(row >= s, cur, nxt), SUB - s, 0)


def _scan8_fwd(a, b, row):
    for s in (1, 2, 4):
        m = row >= s
        a_sh = pltpu.roll(a, s, 0)
        b_sh = pltpu.roll(b, s, 0)
        b = jnp.where(m, a * b_sh + b, b)
        a = jnp.where(m, a * a_sh, a)
    return a, b


def _scan8_rev(a, b, row):
    for s in (1, 2, 4):
        m = row < SUB - s
        a_sh = pltpu.roll(a, SUB - s, 0)
        b_sh = pltpu.roll(b, SUB - s, 0)
        b = jnp.where(m, a * b_sh + b, b)
        a = jnp.where(m, a * a_sh, a)
    return a, b


def _group_mask(shape):
    r = lax.broadcasted_iota(jnp.int32, shape, 0)
    c = lax.broadcasted_iota(jnp.int32, shape, 1)
    return ((r % GROUP) // HEAD_DIM) == (c // HEAD_DIM)


def _expand_heads(w):
    j = lax.broadcasted_iota(jnp.int32, (HEAD_DIM, GROUP), 0)
    c = lax.broadcasted_iota(jnp.int32, (HEAD_DIM, GROUP), 1)
    spread = (c % HEAD_DIM == j).astype(BF16)
    e = jnp.dot(w.astype(BF16), spread, preferred_element_type=F32)
    return jnp.where(_group_mask(e.shape), e, 0.0).astype(BF16)


def _fold_heads(p):
    p = jnp.where(_group_mask(p.shape), p, 0.0)
    c = lax.broadcasted_iota(jnp.int32, (GROUP, HEAD_DIM), 0)
    j = lax.broadcasted_iota(jnp.int32, (GROUP, HEAD_DIM), 1)
    fold = (c % HEAD_DIM == j).astype(BF16)
    hi = p.astype(BF16)
    rest = p - hi.astype(F32)
    mid = rest.astype(BF16)
    lo = (rest - mid.astype(F32)).astype(BF16)
    dot = functools.partial(jnp.dot, preferred_element_type=F32)
    return dot(hi, fold) + dot(mid, fold) + dot(lo, fold)


def _block_diag_apply(xb, wbd_ref):
    parts = [jnp.dot(xb[:, g * GROUP:(g + 1) * GROUP], wbd_ref[g * GROUP:(g + 1) * GROUP, :],
                     preferred_element_type=F32) for g in range(LRU_WIDTH // GROUP)]
    return jnp.concatenate(parts, axis=1)


def _block_diag_apply_t(db, wbd_ref):
    parts = [lax.dot_general(db[:, g * GROUP:(g + 1) * GROUP], wbd_ref[g * GROUP:(g + 1) * GROUP, :],
                             (((1,), (1,)), ((), ())), preferred_element_type=F32)
             for g in range(LRU_WIDTH // GROUP)]
    return jnp.concatenate(parts, axis=1)


def _dot_nt(a, b):
    return lax.dot_general(a, b, (((1,), (1,)), ((), ())), preferred_element_type=F32)


def _dot_tn(a, b):
    return lax.dot_general(a, b, (((0,), (0,)), ((), ())), preferred_element_type=F32)


CHUNKS_IN_FLIGHT = 8


def _chunk_loop(n_chunks, chunk, init):
    def body(k, carry):
        for j in range(CHUNKS_IN_FLIGHT):
            carry = chunk(k * CHUNKS_IN_FLIGHT + j, carry)
        return carry

    return lax.fori_loop(0, n_chunks // CHUNKS_IN_FLIGHT, body, init)


def _place():
    x, y, c = lax.axis_index("x"), lax.axis_index("y"), lax.axis_index("c")
    return x, y, c


def _block_id(chip, core):
    return 4 * chip[0] + 2 * chip[1] + core


def _other_chips(x, y):
    return [(1 - x, y), (x, 1 - y), (1 - x, 1 - y)]


def _remote_copy(src, dst, send_sem, recv_sem, to):
    return pltpu.make_async_remote_copy(src_ref=src, dst_ref=dst, send_sem=send_sem, recv_sem=recv_sem,
                                        device_id=to, device_id_type=MESH)


HBM_SPEC = pl.BlockSpec(memory_space=pl.ANY)


def _in_hbm(*arrays):
    return [pltpu.with_memory_space_constraint(a, pltpu.HBM) for a in arrays]


def _prep_shards(w_in_t, w_out, w_mlp_in, w_mlp_out, conv_w, rnn_conv_w):
    def body(win_ref, wout_ref, w1_ref, w2_ref, cw_ref, rw_ref, o_win, o_wout, o_w1, o_w2, o_cp):
        o_win[...] = win_ref[...].astype(BF16)
        o_wout[...] = wout_ref[...].astype(BF16)
        o_w1[...] = w1_ref[...].astype(BF16)
        o_w2[...] = w2_ref[...].astype(BF16)
        o_cp[...] = jnp.zeros(o_cp.shape, F32)
        o_cp[0:3, 0:64] = cw_ref[...]
        o_cp[3:7, :] = rw_ref[...]

    whole = lambda shape: pl.BlockSpec(shape, lambda i: (0,) * len(shape))
    args = (w_in_t, w_out, w_mlp_in, w_mlp_out, conv_w, rnn_conv_w)
    shapes = [(w_in_t.shape, BF16), (w_out.shape, BF16), (w_mlp_in.shape, BF16), (w_mlp_out.shape, BF16),
              ((8, 128), F32)]
    return pl.pallas_call(
        body, grid=(1,), out_shape=[jax.ShapeDtypeStruct(s, d) for s, d in shapes],
        in_specs=[whole(a.shape) for a in args], out_specs=[whole(s) for s, _ in shapes],
        compiler_params=_params(("arbitrary",), 40), name="prep_shards",
    )(*args)


def _host_all_gather(step, n_steps, shards, fulls, send_sems, recv_sems, local_sems):
    x, y, c = _place()
    me = (x, y, c)
    my_id = _block_id((x, y), c)
    sibling = (x, y, 1 - c)
    chips = _other_chips(x, y)
    n_arr = len(shards)

    def copy(arr, k, block, to, src=None):
        dst = fulls[arr].at[block]
        return _remote_copy(dst if src is None else src, dst, send_sems.at[arr, k], recv_sems.at[arr, k], to)

    def local(arr):
        return pltpu.make_async_copy(shards[arr], fulls[arr].at[my_id], local_sems.at[arr])

    @pl.when(step == 0)
    def _():
        for arr in range(n_arr):
            local(arr).start()
            copy(arr, 0, my_id, sibling, shards[arr]).start()
            for j, chip in enumerate(chips):
                copy(arr, 1 + j, my_id, (*chip, c), shards[arr]).start()

    @pl.when(step == max(n_steps - 2, 0))
    def _():
        for j, chip in enumerate(chips):
            for arr in range(n_arr):
                copy(arr, 1 + j, _block_id(chip, c), me).wait_recv()
                copy(arr, 4 + j, _block_id(chip, c), sibling).start()

    @pl.when(step == n_steps - 1)
    def _():
        for arr in range(n_arr):
            copy(arr, 0, _block_id((x, y), 1 - c), me).wait_recv()
            for j, chip in enumerate(chips):
                copy(arr, 4 + j, _block_id(chip, 1 - c), me).wait_recv()
            for k in range(4):
                copy(arr, k, my_id, me, shards[arr]).wait_send()
            for j, chip in enumerate(chips):
                copy(arr, 4 + j, _block_id(chip, c), me).wait_send()
            local(arr).wait()


def _host_pair_exchange(step, n_steps, gs, sibs, send_sems, recv_sems):
    x, y, c = _place()
    sibling = (x, y, 1 - c)
    chips = [(x, y)] + _other_chips(x, y)

    def d2d(arr, q):
        return _remote_copy(gs[arr].at[_block_id(chips[q], 1 - c)], sibs[arr].at[q],
                            send_sems.at[arr, q], recv_sems.at[arr, q], sibling)

    @pl.when(step == 0)
    def _():
        for arr in range(len(gs)):
            for q in (1, 2, 3, 0):
                d2d(arr, q).start()

    @pl.when(step == n_steps - 1)
    def _():
        for arr in range(len(gs)):
            for q in range(4):
                d2d(arr, q).wait()


def _host_chip_exchange(step, n_steps, hsends, hrecvs, send_sems, recv_sems):
    x, y, c = _place()
    chips = _other_chips(x, y)

    def ici(arr, j):
        return _remote_copy(hsends[arr].at[j], hrecvs[arr].at[j], send_sems.at[arr, j], recv_sems.at[arr, j],
                            (*chips[j], c))

    @pl.when(step == 0)
    def _():
        for arr in range(len(hsends)):
            for j in range(3):
                ici(arr, j).start()

    @pl.when(step == n_steps - 1)
    def _():
        for arr in range(len(hsends)):
            for j in range(3):
                ici(arr, j).wait()


def _host_half_exchange(step, n_steps, parts, sibs, send_sems, recv_sems):
    x, y, c = _place()
    n_q, rows2, _ = parts.shape
    half = rows2 // 2

    def d2d(q):
        src = parts.at[q, pl.ds(pl.multiple_of((1 - c) * half, 16), half), :]
        return _remote_copy(src, sibs.at[q], send_sems.at[q], recv_sems.at[q], (x, y, 1 - c))

    @pl.when(step == 0)
    def _():
        for q in range(n_q):
            d2d(q).start()

    @pl.when(step == n_steps - 1)
    def _():
        for q in range(n_q):
            d2d(q).wait()


def _peer(x, y, c, k):
    return (x ^ ((k >> 2) & 1), y ^ ((k >> 1) & 1), c ^ (k & 1))


def _host_small_exchange(step, n_steps, vec_m, vec_b, wab, vrecv_m, vrecv_b, wrecv, send_sems, recv_sems, local_sems):
    x, y, c = _place()
    my_id = _block_id((x, y), c)
    wrows = wab.shape[0] // N_DEV

    def copies(k):
        to = _peer(x, y, c, k)
        block = wab.at[pl.ds(pl.multiple_of(_block_id(to[0:2], to[2]) * wrows, SUB), wrows), :]
        return [_remote_copy(vec_m, vrecv_m.at[my_id], send_sems.at[0, k], recv_sems.at[0, k], to),
                _remote_copy(vec_b, vrecv_b.at[my_id], send_sems.at[1, k], recv_sems.at[1, k], to),
                _remote_copy(block, wrecv.at[k], send_sems.at[2, k], recv_sems.at[2, k], to)]

    mine = [pltpu.make_async_copy(vec_m, vrecv_m.at[my_id], local_sems.at[0]),
            pltpu.make_async_copy(vec_b, vrecv_b.at[my_id], local_sems.at[1])]

    @pl.when(step == 0)
    def _():
        for cp in mine:
            cp.start()
        for k in range(1, N_DEV):
            for cp in copies(k):
                cp.start()

    @pl.when(step == n_steps - 1)
    def _():
        for k in range(1, N_DEV):
            for cp in copies(k):
                cp.wait()
        for cp in mine:
            cp.wait()


def _pair_sum_parts(parts, sibs, core):
    n_q, rows2, cols = parts.shape
    half = rows2 // 2

    def body(core_ref, g_ref, s_ref, o_ref):
        o_ref[0] = (g_ref[0, 0].astype(F32) + s_ref[0].astype(F32)).astype(BF16)

    block = (1, half, cols)
    grid_spec = pltpu.PrefetchScalarGridSpec(
        num_scalar_prefetch=1, grid=(n_q,),
        in_specs=[pl.BlockSpec((1, 1, half, cols), lambda q, cr: (q, cr[0], 0, 0)),
                  pl.BlockSpec(block, lambda q, cr: (q, 0, 0))],
        out_specs=pl.BlockSpec(block, lambda q, cr: (q, 0, 0)))
    return pl.pallas_call(
        body, grid_spec=grid_spec, out_shape=pltpu.HBM((n_q, half, cols), BF16),
        compiler_params=_params(("arbitrary",), 32), name="pair_sum_w_in",
    )(core, *_in_hbm(parts.reshape(n_q, 2, half, cols), sibs))


def _pair_sum(gs, sibs, name):
    n_arr = len(gs)
    x, y, c = _place()
    slots = jnp.stack([_block_id(chip, c) for chip in [(x, y)] + _other_chips(x, y)]).astype(jnp.int32)

    def body(slots_ref, *refs):
        q = pl.program_id(0)
        for k in range(n_arr):
            g_ref, sib_ref = refs[2 * k:2 * k + 2]
            hs_ref, own_ref = refs[2 * n_arr + 2 * k:2 * n_arr + 2 * k + 2]
            both = g_ref[0].astype(F32) + sib_ref[0].astype(F32)

            @pl.when(q == 0)
            def _(own_ref=own_ref, both=both):
                own_ref[...] = both

            @pl.when(q > 0)
            def _(hs_ref=hs_ref, both=both):
                hs_ref[0] = both.astype(BF16)

    in_specs, out_specs, out_shape, args = [], [], [], []
    for g, sib in zip(gs, sibs):
        _, rows, cols = g.shape
        block = (1, rows, cols)
        in_specs += [pl.BlockSpec(block, lambda q, s: (s[q], 0, 0)), pl.BlockSpec(block, lambda q, s: (q, 0, 0))]
        out_specs += [pl.BlockSpec(block, lambda q, s: (jnp.maximum(q - 1, 0), 0, 0)),
                      pl.BlockSpec((rows, cols), lambda q, s: (0, 0))]
        out_shape += [pltpu.HBM((3, rows, cols), BF16), pltpu.HBM((rows, cols), F32)]
        args += _in_hbm(g, sib)
    grid_spec = pltpu.PrefetchScalarGridSpec(num_scalar_prefetch=1, grid=(4,), in_specs=in_specs, out_specs=out_specs)
    return pl.pallas_call(
        body, grid_spec=grid_spec, out_shape=out_shape,
        compiler_params=_params(("arbitrary",), 40), name=name,
    )(slots, *args)


def _exchange_scratch(n_arr, n_copies):
    return [pltpu.SemaphoreType.DMA((n_arr, n_copies)), pltpu.SemaphoreType.DMA((n_arr, n_copies))]


def _final_small(vrecv_m, vrecv_b, wab, wrecv, vec_x):
    wrows = wab.shape[0] // N_DEV

    def body(vm_ref, vb_ref, w_ref, wr_ref, vx_ref, o_vec, o_w, xrecv, wred, x_send, x_recv, b_send, b_recv):
        x, y, c = _place()
        my_id = _block_id((x, y), c)
        my_rows = pl.ds(pl.multiple_of(my_id * wrows, SUB), wrows)

        def xcopy(k):
            return _remote_copy(vx_ref, xrecv.at[my_id], x_send.at[k], x_recv.at[k], _peer(x, y, c, k))

        def bcopy(k):
            return _remote_copy(wred, o_w.at[my_rows, :], b_send.at[k], b_recv.at[k], _peer(x, y, c, k))

        xrecv[my_id] = vx_ref[...]
        for k in range(1, N_DEV):
            xcopy(k).start()
        red = w_ref[my_rows, :]
        for k in range(1, N_DEV):
            red = red + wr_ref[k]
        wred[...] = red
        o_w[my_rows, :] = red
        for k in range(1, N_DEV):
            bcopy(k).start()
        for k in range(1, N_DEV):
            xcopy(k).wait_recv()
        for rows, ref in ((slice(0, 8), vm_ref), (slice(8, 24), vb_ref), (slice(24, 32), xrecv)):
            tot = ref[0]
            for s in range(1, N_DEV):
                tot = tot + ref[s]
            o_vec[rows, :] = tot
        for k in range(1, N_DEV):
            bcopy(k).wait_recv()
        for k in range(1, N_DEV):
            xcopy(k).wait_send()
            bcopy(k).wait_send()

    vm = pl.BlockSpec(memory_space=pltpu.VMEM)
    dma8 = pltpu.SemaphoreType.DMA((N_DEV,))
    return pl.pallas_call(
        body, out_shape=(jax.ShapeDtypeStruct((VEC_ROWS, D_MODEL), F32), jax.ShapeDtypeStruct(wab.shape, F32)),
        in_specs=[vm] * 5, out_specs=[vm] * 2,
        scratch_shapes=[pltpu.VMEM((N_DEV, SUB, D_MODEL), F32), pltpu.VMEM((wrows, HEAD_DIM), F32),
                        dma8, dma8, dma8, dma8],
        compiler_params=_params(vmem_mib=32), name="final_small",
    )(vrecv_m, vrecv_b, wab, wrecv, vec_x)


def _in_proj(x, g_mix, shards, tm):
    t_len = x.shape[0]
    n_t = t_len // tm
    n_arr = len(shards)
    rows = [s.shape[0] for s in shards]
    width = 2 * rows[0]
    ax, ay = lax.axis_index("x"), lax.axis_index("y")
    order = jnp.stack([2 * cx + cy for cx, cy in [(ax, ay)] + _other_chips(ax, ay)]).astype(jnp.int32)

    def body(order_ref, x_ref, g_ref, *rest):
        shard_refs = rest[0:n_arr]
        u_ref, h_ref = rest[n_arr:n_arr + 2]
        fulls = rest[n_arr + 2:2 * n_arr + 2]
        h_s, wbuf, send_sems, recv_sems, local_sems, load_sem = rest[2 * n_arr + 2:]
        p = pl.program_id(0)
        i = pl.program_id(1)
        x_, y_, c = _place()
        me = (x_, y_, c)
        my_id = _block_id((x_, y_), c)
        sibling = (x_, y_, 1 - c)
        chips = _other_chips(x_, y_)

        def block(arr, blk):
            return fulls[arr].at[pl.ds(pl.multiple_of(blk * rows[arr], rows[arr]), rows[arr]), :]

        def copy(arr, k, blk, to, src=None):
            dst = block(arr, blk)
            return _remote_copy(dst if src is None else src, dst, send_sems.at[arr, k], recv_sems.at[arr, k], to)

        def local(arr):
            return pltpu.make_async_copy(shard_refs[arr], block(arr, my_id), local_sems.at[arr])

        def load_chip(chip, slot):
            start = pl.multiple_of((2 * chip[0] + chip[1]) * width, width)
            return pltpu.make_async_copy(fulls[0].at[pl.ds(start, width), :], wbuf.at[slot], load_sem.at[slot])

        def pass_on(j):
            for arr in range(n_arr):
                copy(arr, 1 + j, _block_id(chips[j], c), me).wait_recv()
                copy(arr, 4 + j, _block_id(chips[j], c), sibling).start()

        def complete(j):
            for arr in range(n_arr):
                copy(arr, 4 + j, _block_id(chips[j], 1 - c), me).wait_recv()

        @pl.when((p == 0) & (i == 0))
        def _():
            for arr in range(n_arr):
                local(arr).start()
                copy(arr, 0, my_id, sibling, shard_refs[arr]).start()
                for j in (0, 1):
                    copy(arr, 1 + j, my_id, (*chips[j], c), shard_refs[arr]).start()
            for arr in range(n_arr):
                local(arr).wait()
                copy(arr, 0, _block_id((x_, y_), 1 - c), me).wait_recv()
            load_chip((x_, y_), 0).start()
            load_chip((x_, y_), 0).wait()

        @pl.when((p == 1) & (i == 0))
        def _():
            pass_on(0)
            for arr in range(n_arr):
                copy(arr, 3, my_id, (*chips[2], c), shard_refs[arr]).start()
            pass_on(1)
            complete(0)
            load_chip(chips[0], 1).start()
            load_chip(chips[0], 1).wait()
            complete(1)
            load_chip(chips[1], 0).start()

        @pl.when((p == 2) & (i == 0))
        def _():
            load_chip(chips[1], 0).wait()

        @pl.when((p == 3) & (i == 0))
        def _():
            pass_on(2)
            complete(2)
            load_chip(chips[2], 1).start()
            load_chip(chips[2], 1).wait()

        @pl.when((p == 3) & (i == n_t - 1))
        def _():
            for arr in range(n_arr):
                for k in range(4):
                    copy(arr, k, my_id, me, shard_refs[arr]).wait_send()
                for j, chip in enumerate(chips):
                    copy(arr, 4 + j, _block_id(chip, c), me).wait_send()

        tile = pl.ds(pl.multiple_of(i * tm, tm), tm)

        @pl.when(p == 0)
        def _():
            xv = x_ref[...]
            h = (xv * _rms(xv) * g_ref[...]).astype(BF16)
            h_ref[...] = h
            h_s[tile, :] = h

        for slot in (0, 1):
            @pl.when(p % 2 == slot)
            def _(slot=slot):
                u_ref[...] = _dot_nt(h_s[tile, :], wbuf[slot])

    first_pass = lambda p, i, o: (jnp.where(p == 0, i, n_t - 1), 0)
    grid_spec = pltpu.PrefetchScalarGridSpec(
        num_scalar_prefetch=1, grid=(4, n_t),
        in_specs=[pl.BlockSpec((tm, D_MODEL), first_pass), pl.BlockSpec((1, D_MODEL), lambda p, i, o: (0, 0))]
        + [HBM_SPEC] * n_arr,
        out_specs=[pl.BlockSpec((tm, width), lambda p, i, o: (i, o[p])), pl.BlockSpec((tm, D_MODEL), first_pass)]
        + [HBM_SPEC] * n_arr,
        scratch_shapes=[pltpu.VMEM((t_len, D_MODEL), BF16), pltpu.VMEM((2, width, D_MODEL), BF16)]
        + _exchange_scratch(n_arr, 7) + [pltpu.SemaphoreType.DMA((n_arr,)), pltpu.SemaphoreType.DMA((2,))])
    return pl.pallas_call(
        body, grid_spec=grid_spec,
        out_shape=[jax.ShapeDtypeStruct((t_len, IN_COLS), F32), jax.ShapeDtypeStruct((t_len, D_MODEL), BF16)]
        + [jax.ShapeDtypeStruct((N_DEV * s.shape[0], s.shape[1]), s.dtype) for s in shards],
        compiler_params=_params(("arbitrary", "arbitrary"), 48), name="in_proj",
    )(order, x, g_mix, *shards)


def _conv3_chunk(u_ref, r, cv_prev, cw, row):
    gb = u_ref[pl.ds(r, SUB), OFF_GB:OFF_GB + CONV_WIDTH]
    gc = u_ref[pl.ds(r, SUB), OFF_GC:OFF_GC + CONV_WIDTH]
    v = u_ref[pl.ds(r, SUB), OFF_V:OFF_V + CONV_WIDTH]
    cv = gc * v
    cv_m1 = _down(cv, cv_prev, 1, row)
    cv_m2 = _down(cv, cv_prev, 2, row)
    cq = cw[2:3, :] * cv + cw[1:2, :] * cv_m1 + cw[0:1, :] * cv_m2
    return gb, gc, v, cv, cv_m1, cv_m2, cq


def _conv4_chunk(u_ref, r, xin_prev, rw, rb, row):
    xin = u_ref[pl.ds(r, SUB), OFF_XR:OFF_XR + LRU_WIDTH]
    m1 = _down(xin, xin_prev, 1, row)
    m2 = _down(xin, xin_prev, 2, row)
    m3 = _down(xin, xin_prev, 3, row)
    xr = rw[3:4, :] * xin + rw[2:3, :] * m1 + rw[1:2, :] * m2 + rw[0:1, :] * m3 + rb
    return xin, m1, m2, m3, xr


def _mixer_fwd(u, conv_w, rnn_conv_w, rnn_conv_b, wa, b_a, wx, b_x, lam, gnc, gnr, shards, tm):
    t_len = u.shape[0]
    n_steps = t_len // tm
    n_chunks = tm // SUB
    n_arr = len(shards)

    def body(u_ref, cw_ref, rw_ref, rb_ref, wa_ref, ba_ref, wx_ref, bx_ref, lam_ref, gnc_ref, gnr_ref, *rest):
        shard_refs = rest[0:n_arr]
        hs_ref, y_ref, xr_s, ra_ref, ii_ref, mult_ref = rest[n_arr:n_arr + 6]
        fulls = rest[n_arr + 6:2 * n_arr + 6]
        (y_s, pa_s, px_s, wabd, wxbd, cv_car, xin_car, h_car,
         send_sems, recv_sems, local_sems) = rest[2 * n_arr + 6:]
        _host_all_gather(pl.program_id(0), n_steps, shard_refs, fulls, send_sems, recv_sems, local_sems)

        @pl.when(pl.program_id(0) == 0)
        def _():
            cv_car[...] = jnp.zeros(cv_car.shape, F32)
            xin_car[...] = jnp.zeros(xin_car.shape, F32)
            h_car[...] = jnp.zeros(h_car.shape, F32)
            wabd[...] = _expand_heads(wa_ref[...])
            wxbd[...] = _expand_heads(wx_ref[...])

        row_c = lax.broadcasted_iota(jnp.int32, (SUB, CONV_WIDTH), 0)
        row_r = lax.broadcasted_iota(jnp.int32, (SUB, LRU_WIDTH), 0)
        cw = cw_ref[...]
        rw = rw_ref[...]
        rb = rb_ref[...]
        g_c = gnc_ref[...]
        g_r = gnr_ref[...]
        sp_c = LRU_C * _softplus_neg(lam_ref[...])

        def convs(i, carry):
            cv_prev, xin_prev = carry
            r = pl.multiple_of(i * SUB, SUB)
            gb, _, _, cv, _, _, cq = _conv3_chunk(u_ref, r, cv_prev, cw, row_c)
            y_c = gb * cq
            y_s[pl.ds(r, SUB), 0:CONV_WIDTH] = y_c * _rms(y_c) * g_c
            xin, _, _, _, xr = _conv4_chunk(u_ref, r, xin_prev, rw, rb, row_r)
            xr_s[pl.ds(r, SUB), :] = xr
            return cv, xin

        cv_last, xin_last = _chunk_loop(n_chunks, convs, (cv_car[...], xin_car[...]))
        cv_car[...] = cv_last
        xin_car[...] = xin_last

        xrb = xr_s[...].astype(BF16)
        pa_s[...] = _block_diag_apply(xrb, wabd) + ba_ref[...]
        px_s[...] = _block_diag_apply(xrb, wxbd) + bx_ref[...]

        def recur(i, h_prev):
            r = pl.multiple_of(i * SUB, SUB)
            xr = xr_s[pl.ds(r, SUB), :]
            ra, ii, a, mult = _lru_gates(pa_s[pl.ds(r, SUB), :], px_s[pl.ds(r, SUB), :], sp_c)
            ra_ref[pl.ds(r, SUB), :] = ra
            ii_ref[pl.ds(r, SUB), :] = ii
            mult_ref[pl.ds(r, SUB), :] = mult
            a_cum, b_cum = _scan8_fwd(a, mult * ii * xr, row_r)
            h = a_cum * h_prev + b_cum
            hs_ref[pl.ds(r, SUB), :] = h
            ge, _ = _gelu(u_ref[pl.ds(r, SUB), OFF_G:OFF_G + LRU_WIDTH])
            y_r = h * ge
            y_s[pl.ds(r, SUB), CONV_WIDTH:MIX_WIDTH] = y_r * _rms(y_r) * g_r
            return h[SUB - 1:SUB, :]

        h_car[...] = _chunk_loop(n_chunks, recur, h_car[...])

        y_ref[...] = y_s[...].astype(BF16)

    row_tile = lambda w: pl.BlockSpec((tm, w), lambda i: (i, 0))
    whole = lambda a: pl.BlockSpec(a.shape, lambda i: (0,) * a.ndim)
    smalls = (conv_w, rnn_conv_w, rnn_conv_b, wa, b_a, wx, b_x, lam, gnc, gnr)
    return pl.pallas_call(
        body, grid=(n_steps,),
        in_specs=[row_tile(IN_COLS)] + [whole(a) for a in smalls] + [HBM_SPEC] * n_arr,
        out_specs=[row_tile(LRU_WIDTH), row_tile(MIX_WIDTH)] + [row_tile(LRU_WIDTH)] * 4 + [HBM_SPEC] * n_arr,
        out_shape=[jax.ShapeDtypeStruct((t_len, LRU_WIDTH), F32), jax.ShapeDtypeStruct((t_len, MIX_WIDTH), BF16)]
        + [jax.ShapeDtypeStruct((t_len, LRU_WIDTH), F32)] * 4
        + [jax.ShapeDtypeStruct((N_DEV,) + s.shape, BF16) for s in shards],
        scratch_shapes=[pltpu.VMEM((tm, MIX_WIDTH), F32),
                        pltpu.VMEM((tm, LRU_WIDTH), F32), pltpu.VMEM((tm, LRU_WIDTH), F32),
                        pltpu.VMEM((LRU_WIDTH, GROUP), BF16), pltpu.VMEM((LRU_WIDTH, GROUP), BF16),
                        pltpu.VMEM((SUB, CONV_WIDTH), F32), pltpu.VMEM((SUB, LRU_WIDTH), F32),
                        pltpu.VMEM((1, LRU_WIDTH), F32)]
        + _exchange_scratch(n_arr, 7) + [pltpu.SemaphoreType.DMA((n_arr,))],
        compiler_params=_params(("arbitrary",), 56), name="mixer_fwd",
    )(u, *smalls, *shards)


def _mlp_up(x, y, g_mlp, w_out, w1, w2_shard, tm):
    t_len = x.shape[0]
    n_steps = t_len // tm
    n_blk, _, blk = w1.shape

    def body(x_ref, y_ref, gm_ref, wout_hbm, w1_hbm, w2_ref, x1_ref, h2_ref, z_ref, w2_full,
             wout_s, w1_s, sem, send_sems, recv_sems, local_sems):
        step = pl.program_id(0)
        _host_all_gather(step, n_steps, [w2_ref], [w2_full], send_sems, recv_sems, local_sems)

        load_wout = pltpu.make_async_copy(wout_hbm, wout_s, sem.at[0])
        load_w1 = pltpu.make_async_copy(w1_hbm, w1_s, sem.at[1])

        @pl.when(step == 0)
        def _():
            load_wout.start()
            load_w1.start()
            load_wout.wait()

        x1v = x_ref[...] + jnp.dot(y_ref[...], wout_s[...], preferred_element_type=F32)
        x1_ref[...] = x1v
        h2 = (x1v * _rms(x1v) * gm_ref[...]).astype(BF16)
        h2_ref[...] = h2

        @pl.when(step == 0)
        def _():
            load_w1.wait()

        for k in range(n_blk):
            rp = jnp.maximum(jnp.dot(h2, w1_s[k], preferred_element_type=F32), 0.0)
            z_ref[:, k * blk:(k + 1) * blk] = (rp * rp).astype(BF16)

    row_tile = lambda w: pl.BlockSpec((tm, w), lambda i: (i, 0))
    return pl.pallas_call(
        body, grid=(n_steps,),
        in_specs=[row_tile(D_MODEL), row_tile(MIX_WIDTH), pl.BlockSpec((1, D_MODEL), lambda i: (0, 0)),
                  HBM_SPEC, HBM_SPEC, HBM_SPEC],
        out_specs=[row_tile(D_MODEL), row_tile(D_MODEL), row_tile(D_FF), HBM_SPEC],
        out_shape=[jax.ShapeDtypeStruct((t_len, D_MODEL), F32), jax.ShapeDtypeStruct((t_len, D_MODEL), BF16),
                   jax.ShapeDtypeStruct((t_len, D_FF), BF16), jax.ShapeDtypeStruct((N_DEV,) + w2_shard.shape, BF16)],
        scratch_shapes=[pltpu.VMEM(w_out.shape, BF16), pltpu.VMEM(w1.shape, BF16), pltpu.SemaphoreType.DMA((2,))]
        + _exchange_scratch(1, 7) + [pltpu.SemaphoreType.DMA((1,))],
        compiler_params=_params(("arbitrary",), 48), name="mlp_up",
    )(x, y, g_mlp, w_out, w1, w2_shard)


def _mlp_down_bwd(x1, z, target, g_mlp, g_f, w1, w2, tm):
    t_len = x1.shape[0]
    n_steps = t_len // tm
    n_blk, _, blk = w1.shape

    def body(x1_ref, z_ref, tg_ref, gm_ref, gf_ref, w1_hbm, w2_hbm, dx1_ref, dx2_ref, vec_ref, dpre_hbm,
             w1_s, w2_s, dp_s, sem, out_sem):
        step = pl.program_id(0)
        rows = pl.ds(pl.multiple_of(step * tm, tm), tm)
        dp_out = pltpu.make_async_copy(dp_s, dpre_hbm.at[rows, :], out_sem.at[0])

        load_w1 = pltpu.make_async_copy(w1_hbm, w1_s, sem.at[0])
        load_w2 = pltpu.make_async_copy(w2_hbm, w2_s, sem.at[1])

        @pl.when(step == 0)
        def _():
            load_w2.start()
            load_w1.start()
            vec_ref[...] = jnp.zeros(vec_ref.shape, F32)
            load_w2.wait()

        x1v = x1_ref[...]
        g_m = gm_ref[...]
        g_o = gf_ref[...]
        r2 = _rms(x1v)
        x1h = x1v * r2
        x2 = x1v + jnp.dot(z_ref[...], w2_s[...], preferred_element_type=F32)
        r3 = _rms(x2)
        x2h = x2 * r3
        err = x2h * g_o - tg_ref[...]
        dout = err * (1.0 / D_MODEL)
        vec_ref[ROW_LOSS:ROW_LOSS + 1, :] += (0.5 / D_MODEL) * jnp.sum(err * err, axis=0, keepdims=True)
        vec_ref[ROW_GF:ROW_GF + 1, :] += jnp.sum(dout * x2h, axis=0, keepdims=True)
        dx2 = _rms_bwd(dout, x2h, r3, g_o)
        dx2b = dx2.astype(BF16)
        dx2_ref[...] = dx2b
        dh2 = jnp.zeros((tm, D_MODEL), F32)

        @pl.when(step > 0)
        def _():
            dp_out.wait()

        @pl.when(step == 0)
        def _():
            load_w1.wait()

        for k in range(n_blk):
            cols = slice(k * blk, (k + 1) * blk)
            dz = _dot_nt(dx2b, w2_s[cols, :])
            dpb = (dz * 2.0 * jnp.sqrt(z_ref[:, cols].astype(F32))).astype(BF16)
            dp_s[:, cols] = dpb
            dh2 = dh2 + _dot_nt(dpb, w1_s[k])
        dp_out.start()
        vec_ref[ROW_GMLP:ROW_GMLP + 1, :] += jnp.sum(dh2 * x1h, axis=0, keepdims=True)
        dx1_ref[...] = dx2 + _rms_bwd(dh2, x1h, r2, g_m)

        @pl.when(step == n_steps - 1)
        def _():
            dp_out.wait()

    row_tile = lambda w: pl.BlockSpec((tm, w), lambda i: (i, 0))
    vec_spec = pl.BlockSpec((1, D_MODEL), lambda i: (0, 0))
    return pl.pallas_call(
        body, grid=(n_steps,),
        in_specs=[row_tile(D_MODEL), row_tile(D_FF), row_tile(D_MODEL), vec_spec, vec_spec, HBM_SPEC, HBM_SPEC],
        out_specs=[row_tile(D_MODEL), row_tile(D_MODEL), pl.BlockSpec((SUB, D_MODEL), lambda i: (0, 0)), HBM_SPEC],
        out_shape=[jax.ShapeDtypeStruct((t_len, D_MODEL), F32), jax.ShapeDtypeStruct((t_len, D_MODEL), BF16),
                   jax.ShapeDtypeStruct((SUB, D_MODEL), F32), jax.ShapeDtypeStruct((t_len, D_FF), BF16)],
        scratch_shapes=[pltpu.VMEM(w1.shape, BF16), pltpu.VMEM(w2.shape, BF16), pltpu.VMEM((tm, D_FF), BF16),
                        pltpu.SemaphoreType.DMA((2,)), pltpu.SemaphoreType.DMA((1,))],
        compiler_params=_params(("arbitrary",), 56), name="mlp_down_bwd",
    )(x1, z, target, g_mlp, g_f, w1, w2)


def _mixer_bwd(u, hs, dx1, saved, conv_w, rnn_conv_w, wa, wx, lam, gnc, gnr, w_out, chip_sums, g_wout, tm):
    t_len = u.shape[0]
    n_tiles = t_len // tm
    n_chunks = tm // SUB
    per_tile = tm // SUB
    n_sums = len(chip_sums)

    def body(u_ref, up_ref, hs_ref, hp_ref, dx1_ref, xr_ref, ra_ref, ii_ref, mult_ref,
             cw_ref, rw_ref, wa_ref, wx_ref, lam_ref, gnc_ref, gnr_ref, wout_ref, *rest):
        hsends = rest[0:n_sums]
        gwout_ref = rest[n_sums]
        du_ref, vec_ref, wab_ref = rest[n_sums + 1:n_sums + 4]
        hrecvs = rest[n_sums + 4:2 * n_sums + 4]
        sib_wout = rest[2 * n_sums + 4]
        (du_s, dy_s, dpa_s, dpx_s, dxr_s, wabd, wxbd, acc, dwa_acc, dwx_acc,
         a_car, dh_car, dcq_car, dxr_car, i_send, i_recv, d_send, d_recv) = rest[2 * n_sums + 5:]
        step = pl.program_id(0)
        _host_chip_exchange(step, n_tiles, hsends, hrecvs, i_send, i_recv)
        _host_pair_exchange(step, n_tiles, [gwout_ref], [sib_wout], d_send, d_recv)
        has_prev = (step < n_tiles - 1).astype(F32)

        @pl.when(step == 0)
        def _():
            acc[...] = jnp.zeros(acc.shape, F32)
            dwa_acc[...] = jnp.zeros(dwa_acc.shape, F32)
            dwx_acc[...] = jnp.zeros(dwx_acc.shape, F32)
            a_car[...] = jnp.ones(a_car.shape, F32)
            dh_car[...] = jnp.zeros(dh_car.shape, F32)
            dcq_car[...] = jnp.zeros(dcq_car.shape, F32)
            dxr_car[...] = jnp.zeros(dxr_car.shape, F32)
            wabd[...] = _expand_heads(wa_ref[...])
            wxbd[...] = _expand_heads(wx_ref[...])

        row_c = lax.broadcasted_iota(jnp.int32, (SUB, CONV_WIDTH), 0)
        row_r = lax.broadcasted_iota(jnp.int32, (SUB, LRU_WIDTH), 0)
        cw = cw_ref[...]
        rw = rw_ref[...]
        g_c = gnc_ref[...]
        g_r = gnr_ref[...]
        sp_c = LRU_C * _softplus_neg(lam_ref[...])

        up = up_ref[...] * has_prev
        cv_before = up[:, OFF_GC:OFF_GC + CONV_WIDTH] * up[:, OFF_V:OFF_V + CONV_WIDTH]
        hs_before = hp_ref[...] * has_prev

        dy_s[...] = _dot_nt(dx1_ref[...].astype(BF16), wout_ref[...])

        xrb = xr_ref[...].astype(BF16)

        def recur_bwd(j, carry):
            a_later, dh_later = carry
            i = n_chunks - 1 - j
            r = pl.multiple_of(i * SUB, SUB)
            rp = pl.multiple_of(jnp.maximum(i - 1, 0) * SUB, SUB)
            xr = xr_ref[pl.ds(r, SUB), :]
            hs_c = hs_ref[pl.ds(r, SUB), :]
            hs_prev = jnp.where(i == 0, hs_before, hs_ref[pl.ds(rp, SUB), :])
            h_m1 = _down(hs_c, hs_prev, 1, row_r)
            ra = ra_ref[pl.ds(r, SUB), :]
            ii = ii_ref[pl.ds(r, SUB), :]
            mult = mult_ref[pl.ds(r, SUB), :]
            a = jnp.exp(-ra * sp_c)
            inv_mult = lax.rsqrt(mult * mult)
            ge, dge = _gelu(u_ref[pl.ds(r, SUB), OFF_G:OFF_G + LRU_WIDTH])
            y_r = hs_c * ge
            rr = _rms(y_r)
            yhat = y_r * rr
            dyn = dy_s[pl.ds(r, SUB), CONV_WIDTH:MIX_WIDTH]
            acc[ACC_GNR] += dyn * yhat
            dy_r = _rms_bwd(dyn, yhat, rr, g_r)
            du_s[pl.ds(r, SUB), OFF_G:OFF_G + LRU_WIDTH] = dy_r * hs_c * dge
            a_cum, d_cum = _scan8_rev(_up(a, a_later, 1, row_r), dy_r * ge, row_r)
            dh = a_cum * dh_later + d_cum
            dm = dh * mult
            dii = dm * xr
            dxr_s[pl.ds(r, SUB), :] = dm * ii
            dla = a * dh * (h_m1 - (ii * xr) * a * inv_mult)
            dla_r = dla * ra
            acc[ACC_SP] -= dla_r
            dpa = dla_r * (sp_c * (ra - 1.0))
            dpx = dii * ii * (1.0 - ii)
            acc[ACC_BA] += dpa
            acc[ACC_BX] += dpx
            dpa_s[pl.ds(r, SUB), :] = dpa
            dpx_s[pl.ds(r, SUB), :] = dpx
            return a, dh[0:1, :]

        a_first, dh_first = _chunk_loop(n_chunks, recur_bwd, (a_car[...], dh_car[...]))
        a_car[...] = a_first
        dh_car[...] = dh_first

        dpab = dpa_s[...].astype(BF16)
        dpxb = dpx_s[...].astype(BF16)
        dxr_s[...] += _block_diag_apply_t(dpab, wabd) + _block_diag_apply_t(dpxb, wxbd)
        for g in range(LRU_WIDTH // GROUP):
            cols = slice(g * GROUP, (g + 1) * GROUP)
            dwa_acc[cols, :] += _dot_tn(xrb[:, cols], dpab[:, cols])
            dwx_acc[cols, :] += _dot_tn(xrb[:, cols], dpxb[:, cols])

        def convs_bwd(j, carry):
            dcq_later, dxr_later = carry
            i = n_chunks - 1 - j
            r = pl.multiple_of(i * SUB, SUB)
            rp = pl.multiple_of(jnp.maximum(i - 1, 0) * SUB, SUB)
            cv_prev = jnp.where(i == 0, cv_before,
                                u_ref[pl.ds(rp, SUB), OFF_GC:OFF_GC + CONV_WIDTH]
                                * u_ref[pl.ds(rp, SUB), OFF_V:OFF_V + CONV_WIDTH])
            gb, gc, v, cv, cv_m1, cv_m2, cq = _conv3_chunk(u_ref, r, cv_prev, cw, row_c)
            y_c = gb * cq
            rc = _rms(y_c)
            yhat = y_c * rc
            dyn = dy_s[pl.ds(r, SUB), 0:CONV_WIDTH]
            acc[ACC_GNC, :, 0:CONV_WIDTH] += dyn * yhat
            dy_c = _rms_bwd(dyn, yhat, rc, g_c)
            dcq = dy_c * gb
            dcv = (cw[2:3, :] * dcq + cw[1:2, :] * _up(dcq, dcq_later, 1, row_c)
                   + cw[0:1, :] * _up(dcq, dcq_later, 2, row_c))
            acc[ACC_CW + 2, :, 0:CONV_WIDTH] += dcq * cv
            acc[ACC_CW + 1, :, 0:CONV_WIDTH] += dcq * cv_m1
            acc[ACC_CW + 0, :, 0:CONV_WIDTH] += dcq * cv_m2
            du_s[pl.ds(r, SUB), OFF_GB:OFF_GB + CONV_WIDTH] = dy_c * cq
            du_s[pl.ds(r, SUB), OFF_GC:OFF_GC + CONV_WIDTH] = dcv * v
            du_s[pl.ds(r, SUB), OFF_V:OFF_V + CONV_WIDTH] = dcv * gc

            xin = u_ref[pl.ds(r, SUB), OFF_XR:OFF_XR + LRU_WIDTH]
            dxr = dxr_s[pl.ds(r, SUB), :]
            ahead = [dxr] + [_up(dxr, dxr_later, k, row_r) for k in (1, 2, 3)]
            du_s[pl.ds(r, SUB), OFF_XR:OFF_XR + LRU_WIDTH] = (
                rw[3:4, :] * ahead[0] + rw[2:3, :] * ahead[1] + rw[1:2, :] * ahead[2] + rw[0:1, :] * ahead[3])
            for k in range(4):
                acc[ACC_RW + 3 - k] += ahead[k] * xin
            acc[ACC_BR] += dxr
            return dcq, dxr

        dcq_first, dxr_first = _chunk_loop(n_chunks, convs_bwd, (dcq_car[...], dxr_car[...]))
        dcq_car[...] = dcq_first
        dxr_car[...] = dxr_first

        du_ref[...] = du_s[...].astype(BF16)

        @pl.when(step == n_tiles - 1)
        def _():
            vec_ref[...] = jnp.zeros(vec_ref.shape, F32)
            rows = {ACC_GNC: ROW_GNC, ACC_GNR: ROW_GNR, ACC_BR: ROW_BR, ACC_BA: ROW_BA, ACC_BX: ROW_BX}
            for k in range(3):
                rows[ACC_CW + k] = ROW_CW + k
            for k in range(4):
                rows[ACC_RW + k] = ROW_RW + k
            for slot, out_row in rows.items():
                o = out_row - ROW_GNC
                vec_ref[o:o + 1, :] = jnp.sum(acc[slot], axis=0, keepdims=True)
            lam_v = lam_ref[...]
            dsp = jnp.sum(acc[ACC_SP], axis=0, keepdims=True)
            o = ROW_LAM - ROW_GNC
            vec_ref[o:o + 1, :] = -dsp * LRU_C / (1.0 + jnp.exp(lam_v))
            wab_ref[0:LRU_WIDTH, :] = _fold_heads(dwa_acc[...])
            wab_ref[LRU_WIDTH:2 * LRU_WIDTH, :] = _fold_heads(dwx_acc[...])

    rev = lambda w: pl.BlockSpec((tm, w), lambda s: (n_tiles - 1 - s, 0))
    before = lambda w: pl.BlockSpec((SUB, w), lambda s: (jnp.maximum((n_tiles - 1 - s) * per_tile - 1, 0), 0))
    whole = lambda a: pl.BlockSpec(a.shape, lambda s: (0,) * a.ndim)
    smalls = (conv_w, rnn_conv_w, wa, wx, lam, gnc, gnr, w_out)
    full = lambda w: pltpu.VMEM((tm, w), F32)
    return pl.pallas_call(
        body, grid=(n_tiles,),
        in_specs=[rev(IN_COLS), before(IN_COLS), rev(LRU_WIDTH), before(LRU_WIDTH), rev(D_MODEL)]
        + [rev(LRU_WIDTH)] * len(saved) + [whole(a) for a in smalls] + [HBM_SPEC] * (n_sums + 1),
        out_specs=[rev(IN_COLS), pl.BlockSpec((16, D_MODEL), lambda s: (0, 0)),
                   pl.BlockSpec((2 * LRU_WIDTH, HEAD_DIM), lambda s: (0, 0))] + [HBM_SPEC] * (n_sums + 1),
        out_shape=[jax.ShapeDtypeStruct((t_len, IN_COLS), BF16), jax.ShapeDtypeStruct((16, D_MODEL), F32),
                   jax.ShapeDtypeStruct((2 * LRU_WIDTH, HEAD_DIM), F32)]
        + [jax.ShapeDtypeStruct(s.shape, BF16) for s in chip_sums]
        + [jax.ShapeDtypeStruct((4,) + g_wout.shape[1:], BF16)],
        scratch_shapes=[full(IN_COLS), full(MIX_WIDTH), full(LRU_WIDTH), full(LRU_WIDTH), full(LRU_WIDTH),
                        pltpu.VMEM((LRU_WIDTH, GROUP), BF16), pltpu.VMEM((LRU_WIDTH, GROUP), BF16),
                        pltpu.VMEM((N_ACC, SUB, LRU_WIDTH), F32),
                        pltpu.VMEM((LRU_WIDTH, GROUP), F32), pltpu.VMEM((LRU_WIDTH, GROUP), F32),
                        pltpu.VMEM((SUB, LRU_WIDTH), F32), pltpu.VMEM((1, LRU_WIDTH), F32),
                        pltpu.VMEM((SUB, CONV_WIDTH), F32), pltpu.VMEM((SUB, LRU_WIDTH), F32)]
        + _exchange_scratch(n_sums, 3) + _exchange_scratch(1, 4),
        compiler_params=_params(("arbitrary",), 56), name="mixer_bwd",
    )(u, u, hs, hs, dx1, *saved, *smalls, *chip_sums, g_wout)


def _in_proj_bwd(du, dx1, x, g_mix, win_t, tm, chip_sums, g_own):
    t_len = x.shape[0]
    n_steps = t_len // tm

    def body(du_ref, dx1_ref, x_ref, g_ref, w_ref, hs_ref, gown_ref,
             dx_ref, vec_ref, landed_ref, sib_ref, i_send, i_recv, d_send, d_recv):
        step = pl.program_id(0)
        _host_chip_exchange(step, n_steps, [hs_ref], [landed_ref], i_send, i_recv)
        _host_half_exchange(step, n_steps, gown_ref, sib_ref, d_send, d_recv)

        @pl.when(step == 0)
        def _():
            vec_ref[...] = jnp.zeros(vec_ref.shape, F32)

        dh = jnp.dot(du_ref[...], w_ref[...], preferred_element_type=F32)
        xv = x_ref[...]
        r1 = _rms(xv)
        xh = xv * r1
        vec_ref[0:1, :] += jnp.sum(dh * xh, axis=0, keepdims=True)
        dx_ref[...] = dx1_ref[...] + _rms_bwd(dh, xh, r1, g_ref[...])

    row_tile = lambda w: pl.BlockSpec((tm, w), lambda i: (i, 0))
    half_shape = (g_own.shape[0], g_own.shape[1] // 2, g_own.shape[2])
    return pl.pallas_call(
        body, grid=(n_steps,),
        in_specs=[row_tile(IN_COLS), row_tile(D_MODEL), row_tile(D_MODEL), pl.BlockSpec((1, D_MODEL), lambda i: (0, 0)),
                  pl.BlockSpec((IN_COLS, D_MODEL), lambda i: (0, 0))] + [HBM_SPEC] * 2,
        out_specs=[row_tile(D_MODEL), pl.BlockSpec((SUB, D_MODEL), lambda i: (0, 0))] + [HBM_SPEC] * 2,
        out_shape=[jax.ShapeDtypeStruct((t_len, D_MODEL), F32), jax.ShapeDtypeStruct((SUB, D_MODEL), F32),
                   jax.ShapeDtypeStruct(chip_sums.shape, BF16), jax.ShapeDtypeStruct(half_shape, BF16)],
        scratch_shapes=_exchange_scratch(1, 3) + [pltpu.SemaphoreType.DMA((1,)), pltpu.SemaphoreType.DMA((1,))],
        compiler_params=_params(("arbitrary",), 56), name="in_proj_bwd",
    )(du, dx1, x, g_mix, win_t, chip_sums, g_own)


def _tn_weight_grad(a, b, tk, name, pair=(), col_blocks=1):
    t_len, m = a.shape
    n = b.shape[1]
    n_steps = t_len // tk
    sent = tuple(pair)
    n_sent = len(sent)

    def body(a_ref, b_ref, *rest):
        srcs = rest[0:n_sent]
        o_ref = rest[n_sent]
        dsts = rest[n_sent + 1:2 * n_sent + 1]
        acc = rest[2 * n_sent + 1]
        sems = rest[2 * n_sent + 2:]
        j = pl.program_id(0)
        if pair:
            _host_pair_exchange(j, n_steps, srcs, dsts, *sems)

        @pl.when(j == 0)
        def _():
            acc[...] = jnp.zeros(acc.shape, F32)

        acc[...] += _dot_tn(a_ref[...].astype(BF16), b_ref[...].astype(BF16))

        @pl.when(j == n_steps - 1)
        def _():
            if col_blocks == 1:
                o_ref[...] = acc[...].astype(BF16)
            else:
                for k in range(col_blocks):
                    o_ref[k] = acc[:, k * nb:(k + 1) * nb].astype(BF16)

    nb = n // col_blocks
    out_dims = (m, n) if col_blocks == 1 else (col_blocks, m, nb)
    landed = [jax.ShapeDtypeStruct((4,) + g.shape[1:], BF16) for g in pair]
    scratch = [pltpu.VMEM((m, n), F32)]
    if n_sent:
        scratch += _exchange_scratch(n_sent, 4)
    return pl.pallas_call(
        body, grid=(n_steps,),
        in_specs=[pl.BlockSpec((tk, m), lambda j: (j, 0)), pl.BlockSpec((tk, n), lambda j: (j, 0))]
        + [HBM_SPEC] * n_sent,
        out_specs=[pl.BlockSpec(out_dims, lambda j: (0,) * len(out_dims))] + [HBM_SPEC] * n_sent,
        out_shape=[jax.ShapeDtypeStruct(out_dims, BF16)] + landed,
        scratch_shapes=scratch,
        compiler_params=_params(("arbitrary",), 56), name=name,
    )(a, b, *sent)


def _w_in_grad_part(du, h, tk, name, chip_ids, chip=(), halves=None, small=None):
    t_len = du.shape[0]
    n_t = t_len // tk
    n_q = chip_ids.shape[0]
    width = 2 * (IN_COLS // N_DEV)
    n_steps = n_q * n_t
    n_chip = len(chip)
    sent = tuple(chip) + (() if halves is None else (halves,)) + (() if small is None else tuple(small))
    n_sent = len(sent)

    def body(ids_ref, a_ref, b_ref, *rest):
        srcs = rest[0:n_sent]
        o_ref = rest[n_sent]
        dsts = rest[n_sent + 1:2 * n_sent + 1]
        acc = rest[2 * n_sent + 1]
        sems = list(rest[2 * n_sent + 2:])
        j = pl.program_id(1)
        step = pl.program_id(0) * n_t + j
        if chip:
            _host_chip_exchange(step, n_steps, srcs[0:n_chip], dsts[0:n_chip], sems.pop(0), sems.pop(0))
        if halves is not None:
            _host_half_exchange(step, n_steps, srcs[n_chip], dsts[n_chip], sems.pop(0), sems.pop(0))
        if small is not None:
            _host_small_exchange(step, n_steps, *srcs[n_sent - 3:], *dsts[n_sent - 3:], *sems)

        @pl.when(j == 0)
        def _():
            acc[...] = jnp.zeros(acc.shape, F32)

        acc[...] += _dot_tn(a_ref[...], b_ref[...])

        @pl.when(j == n_t - 1)
        def _():
            o_ref[0] = acc[...].astype(BF16)

    landed = [jax.ShapeDtypeStruct(s.shape, BF16) for s in chip]
    scratch = [pltpu.VMEM((width, D_MODEL), F32)]
    if chip:
        scratch += _exchange_scratch(len(chip), 3)
    if halves is not None:
        landed.append(jax.ShapeDtypeStruct((halves.shape[0], halves.shape[1] // 2, halves.shape[2]), BF16))
        scratch += [pltpu.SemaphoreType.DMA((halves.shape[0],)), pltpu.SemaphoreType.DMA((halves.shape[0],))]
    if small is not None:
        vec_m, vec_b, wab = small
        landed += [jax.ShapeDtypeStruct((N_DEV,) + vec_m.shape, F32), jax.ShapeDtypeStruct((N_DEV,) + vec_b.shape, F32),
                   jax.ShapeDtypeStruct((N_DEV, wab.shape[0] // N_DEV, wab.shape[1]), F32)]
        scratch += _exchange_scratch(3, N_DEV) + [pltpu.SemaphoreType.DMA((2,))]
    grid_spec = pltpu.PrefetchScalarGridSpec(
        num_scalar_prefetch=1, grid=(n_q, n_t),
        in_specs=[pl.BlockSpec((tk, width), lambda q, j, ids: (j, ids[q])),
                  pl.BlockSpec((tk, D_MODEL), lambda q, j, ids: (j, 0))] + [HBM_SPEC] * n_sent,
        out_specs=[pl.BlockSpec((1, width, D_MODEL), lambda q, j, ids: (q, 0, 0))] + [HBM_SPEC] * n_sent,
        scratch_shapes=scratch)
    return pl.pallas_call(
        body, grid_spec=grid_spec, out_shape=[jax.ShapeDtypeStruct((n_q, width, D_MODEL), BF16)] + landed,
        compiler_params=_params(("arbitrary", "arbitrary"), 40), name=name,
    )(chip_ids, du, h, *sent)


def _adamw(w, g, m, v):
    m = ADAM_B1 * m + (1.0 - ADAM_B1) * g
    v = ADAM_B2 * v + (1.0 - ADAM_B2) * (g * g)
    delta = -ADAM_LR * ((m / BC1) / (jnp.sqrt(v / BC2) + ADAM_EPS) + ADAM_WD * w)
    return delta, m, v


def _update_sharded(g, landed, w, m, v, rows_blk, name):
    rows, cols = w.shape

    def body(g_ref, l_ref, w_ref, m_ref, v_ref, og, od, om, ov):
        gv = g_ref[...]
        for j in range(3):
            gv = gv + l_ref[j].astype(F32)
        delta, mn, vn = _adamw(w_ref[...], gv, m_ref[...], v_ref[...])
        og[...] = gv
        od[...] = delta
        om[...] = mn
        ov[...] = vn

    blk = pl.BlockSpec((rows_blk, cols), lambda i: (i, 0))
    shape = pltpu.HBM((rows, cols), F32)
    return pl.pallas_call(
        body, grid=(rows // rows_blk,),
        in_specs=[blk, pl.BlockSpec((3, rows_blk, cols), lambda i: (0, i, 0)), blk, blk, blk],
        out_specs=[blk] * 4, out_shape=[shape] * 4,
        compiler_params=_params(("arbitrary",), 32), name=name,
    )(*_in_hbm(g, landed, w, m, v))


def _update_w_in(g_own, sib_own, landed, w_t, m_t, v_t, core, cols_blk):
    rows, cols = w_t.shape

    def body(core_ref, g_ref, s_ref, l_ref, w_ref, m_ref, v_ref, og, od, om, ov):
        gv = g_ref[0, 0].astype(F32) + s_ref[0].astype(F32)
        for j in range(3):
            gv = gv + l_ref[j].astype(F32)
        delta, mn, vn = _adamw(w_ref[...], gv, m_ref[...], v_ref[...])
        og[...] = gv
        od[...] = delta
        om[...] = mn
        ov[...] = vn

    blk = pl.BlockSpec((rows, cols_blk), lambda i, cr: (0, i))
    grid_spec = pltpu.PrefetchScalarGridSpec(
        num_scalar_prefetch=1, grid=(cols // cols_blk,),
        in_specs=[pl.BlockSpec((1, 1, rows, cols_blk), lambda i, cr: (0, cr[0], 0, i)),
                  pl.BlockSpec((1, rows, cols_blk), lambda i, cr: (0, 0, i)),
                  pl.BlockSpec((3, rows, cols_blk), lambda i, cr: (0, 0, i)), blk, blk, blk],
        out_specs=[blk] * 4)
    return pl.pallas_call(
        body, grid_spec=grid_spec, out_shape=[pltpu.HBM((rows, cols), F32)] * 4,
        compiler_params=_params(("arbitrary",), 32), name="update_w_in",
    )(core, *_in_hbm(g_own.reshape(1, 2, rows, cols), sib_own, landed, w_t, m_t, v_t))


def _update_small(vsum, wsum, g_cw, g_rw, weights, moments_m, moments_v):
    n = len(weights)

    def body(*refs):
        vs, ws, gcw, grw = refs[0:4]
        w_refs = refs[4:4 + n]
        m_refs = refs[4 + n:4 + 2 * n]
        v_refs = refs[4 + 2 * n:4 + 3 * n]
        outs = refs[4 + 3 * n:]
        loss_ref = outs[0]
        loss_ref[...] = jnp.sum(vs[ROW_LOSS:ROW_LOSS + 1, :], axis=1, keepdims=True)
        grads = [
            vs[ROW_GMIX:ROW_GMIX + 1, :], gcw[...], grw[...], vs[ROW_BR:ROW_BR + 1, :],
            ws[0:LRU_WIDTH, :], vs[ROW_BA:ROW_BA + 1, :], ws[LRU_WIDTH:2 * LRU_WIDTH, :], vs[ROW_BX:ROW_BX + 1, :],
            vs[ROW_LAM:ROW_LAM + 1, :], vs[ROW_GNC:ROW_GNC + 1, 0:CONV_WIDTH], vs[ROW_GNR:ROW_GNR + 1, :],
            vs[ROW_GMLP:ROW_GMLP + 1, :], vs[ROW_GF:ROW_GF + 1, :],
        ]
        for k in range(n):
            gk = grads[k]
            delta, mn, vn = _adamw(w_refs[k][...], gk, m_refs[k][...], v_refs[k][...])
            outs[1 + 4 * k][...] = gk
            outs[2 + 4 * k][...] = delta
            outs[3 + 4 * k][...] = mn
            outs[4 + 4 * k][...] = vn

    whole = lambda a: pl.BlockSpec(a.shape, lambda i: (0,) * len(a.shape))
    out_shape = [jax.ShapeDtypeStruct((1, 1), F32)]
    for w in weights:
        out_shape += [jax.ShapeDtypeStruct(w.shape, F32)] * 4
    args = (vsum, wsum, g_cw, g_rw, *weights, *moments_m, *moments_v)
    return pl.pallas_call(
        body, grid=(1,), out_shape=out_shape, in_specs=[whole(a) for a in args], out_specs=[whole(s) for s in out_shape],
        compiler_params=_params(("arbitrary",), 32), name="update_small",
    )(*args)


def kernel(x, norm_mix_g, w_in, conv_w, rnn_conv_w, rnn_conv_b, w_a, b_a, w_x, b_x, lru_lambda, g_norm_conv, g_norm_rnn, w_out, norm_mlp_g, w_mlp_in, w_mlp_out, final_norm_g, loss_target, m_norm_mix_g, m_w_in, m_conv_w, m_rnn_conv_w, m_rnn_conv_b, m_w_a, m_b_a, m_w_x, m_b_x, m_lru_lambda, m_g_norm_conv, m_g_norm_rnn, m_w_out, m_norm_mlp_g, m_w_mlp_in, m_w_mlp_out, m_final_norm_g, v_norm_mix_g, v_w_in, v_conv_w, v_rnn_conv_w, v_rnn_conv_b, v_w_a, v_b_a, v_w_x, v_b_x, v_lru_lambda, v_g_norm_conv, v_g_norm_rnn, v_w_out, v_norm_mlp_g, v_w_mlp_in, v_w_mlp_out, v_final_norm_g):
    t_len = x.shape[1]
    my_id = 4 * lax.axis_index("x") + 2 * lax.axis_index("y") + lax.axis_index("c")
    tm = min(256, t_len)
    tb = min(512, t_len)
    tk = min(512, t_len)

    xs = x.reshape(t_len, D_MODEL)
    tgt = loss_target.reshape(t_len, D_MODEL)
    flat = lambda a: a.reshape(a.shape[-2:]) if a.ndim == 3 else a.reshape(1, -1)
    heads = lambda a: a.reshape(LRU_WIDTH, HEAD_DIM)

    turned = lambda a: jnp.transpose(flat(a))
    win_shard, wout_shard, w1_shard, w2_shard, cp_shard = _prep_shards(
        turned(w_in), flat(w_out), flat(w_mlp_in), flat(w_mlp_out), flat(conv_w), flat(rnn_conv_w))

    u, h, win_t, cp_full = _in_proj(xs, flat(norm_mix_g), (win_shard, cp_shard), min(1024, t_len))
    cpack = cp_full.reshape(N_DEV, 8, 128)
    conv_full = jnp.transpose(cpack[:, 0:3, 0:64], (1, 0, 2)).reshape(3, CONV_WIDTH)
    rnn_full = jnp.transpose(cpack[:, 3:7, :], (1, 0, 2)).reshape(4, LRU_WIDTH)
    mixer_small = (conv_full, rnn_full, flat(rnn_conv_b), heads(w_a), flat(b_a), heads(w_x), flat(b_x),
                   flat(lru_lambda), flat(g_norm_conv), flat(g_norm_rnn))
    hs, y, xr, gate_r, gate_i, mult, w1_blk, wout_blk = _mixer_fwd(u, *mixer_small, (w1_shard, wout_shard), tm)
    wout_f = wout_blk.reshape(MIX_WIDTH, D_MODEL)
    x1, h2, z, w2_blk = _mlp_up(xs, y, flat(norm_mlp_g), wout_f, w1_blk, w2_shard, tb)
    dx1, dx2, vec_m, dpre = _mlp_down_bwd(x1, z, tgt, flat(norm_mlp_g), flat(final_norm_g), w1_blk,
                                          w2_blk.reshape(D_FF, D_MODEL), tb)
    (g_w1,) = _tn_weight_grad(h2, dpre, tk, "w_mlp_in_grad", col_blocks=N_DEV)
    (g_w2,) = _tn_weight_grad(z, dx2, tk, "w_mlp_out_grad")
    g_w2 = g_w2.reshape(N_DEV, D_FF // N_DEV, D_MODEL)
    g_wout, sib_w1, sib_w2 = _tn_weight_grad(y, dx1, tk, "w_out_grad", pair=(g_w1, g_w2))
    g_wout = g_wout.reshape(N_DEV, MIX_WIDTH // N_DEV, D_MODEL)
    hsend_w1, own_w1, hsend_w2, own_w2 = _pair_sum((g_w1, g_w2), (sib_w1, sib_w2), "pair_sum_w_mlp")
    du, vec_b, wab, landed_w1, landed_w2, sib_wout = _mixer_bwd(
        u, hs, dx1, (xr, gate_r, gate_i, mult), conv_full, rnn_full, heads(w_a), heads(w_x),
        flat(lru_lambda), flat(g_norm_conv), flat(g_norm_rnn), wout_f, (hsend_w1, hsend_w2), g_wout, tm)
    hsend_wout, own_wout = _pair_sum((g_wout,), (sib_wout,), "pair_sum_w_out")
    ax, ay, ac = lax.axis_index("x"), lax.axis_index("y"), lax.axis_index("c")
    chip_ids = jnp.stack([2 * cx + cy for cx, cy in [(ax, ay)] + _other_chips(ax, ay)]).astype(jnp.int32)
    core = jnp.reshape(ac, (1,)).astype(jnp.int32)
    tw = min(1024, t_len)
    g_others, landed_wout, vrecv_m, vrecv_b, wrecv = _w_in_grad_part(
        du, h, tw, "w_in_grad_others", chip_ids[1:4], chip=(hsend_wout,), small=(vec_m, vec_b, wab))
    g_own, sib_others = _w_in_grad_part(du, h, tw, "w_in_grad_own", chip_ids[0:1], halves=g_others)
    hsend_win = _pair_sum_parts(g_others, sib_others, core)
    grad_x, vec_x, landed_win, sib_own = _in_proj_bwd(du, dx1, xs, flat(norm_mix_g), win_t, tm, hsend_win, g_own)

    vsum, wsum = _final_small(vrecv_m, vrecv_b, wab, wrecv, vec_x)

    up_win = _update_w_in(g_own, sib_own, landed_win, turned(w_in), turned(m_w_in), turned(v_w_in), core, 256)
    up_win = [jnp.transpose(a) for a in up_win]
    up_wout = _update_sharded(own_wout, landed_wout, flat(w_out), flat(m_w_out), flat(v_w_out), 96, "update_w_out")
    up_w1 = _update_sharded(own_w1, landed_w1, flat(w_mlp_in), flat(m_w_mlp_in), flat(v_w_mlp_in), 256,
                            "update_w_mlp_in")
    up_w2 = _update_sharded(own_w2, landed_w2, flat(w_mlp_out), flat(m_w_mlp_out), flat(v_w_mlp_out), 256,
                            "update_w_mlp_out")

    g_cw = lax.dynamic_slice(vsum, (ROW_CW, 64 * my_id), (3, 64))
    g_rw = lax.dynamic_slice(vsum, (ROW_RW, 128 * my_id), (4, 128))
    small_w = (norm_mix_g, conv_w, rnn_conv_w, rnn_conv_b, w_a, b_a, w_x, b_x, lru_lambda, g_norm_conv, g_norm_rnn,
               norm_mlp_g, final_norm_g)
    small_m = (m_norm_mix_g, m_conv_w, m_rnn_conv_w, m_rnn_conv_b, m_w_a, m_b_a, m_w_x, m_b_x, m_lru_lambda,
               m_g_norm_conv, m_g_norm_rnn, m_norm_mlp_g, m_final_norm_g)
    small_v = (v_norm_mix_g, v_conv_w, v_rnn_conv_w, v_rnn_conv_b, v_w_a, v_b_a, v_w_x, v_b_x, v_lru_lambda,
               v_g_norm_conv, v_g_norm_rnn, v_norm_mlp_g, v_final_norm_g)
    is_heads = (False, False, False, False, True, False, True, False, False, False, False, False, False)
    as2d = lambda arrs: [heads(a) if hd else flat(a) for a, hd in zip(arrs, is_heads)]
    small_out = _update_small(vsum, wsum, g_cw, g_rw, as2d(small_w), as2d(small_m), as2d(small_v))
    loss = small_out[0].reshape(())

    names = ["norm_mix_g", "w_in", "conv_w", "rnn_conv_w", "rnn_conv_b", "w_a", "b_a", "w_x", "b_x", "lru_lambda",
             "g_norm_conv", "g_norm_rnn", "w_out", "norm_mlp_g", "w_mlp_in", "w_mlp_out", "final_norm_g"]
    originals = dict(zip(names, (norm_mix_g, w_in, conv_w, rnn_conv_w, rnn_conv_b, w_a, b_a, w_x, b_x, lru_lambda,
                                 g_norm_conv, g_norm_rnn, w_out, norm_mlp_g, w_mlp_in, w_mlp_out, final_norm_g)))
    results = {"w_in": up_win, "w_out": up_wout, "w_mlp_in": up_w1, "w_mlp_out": up_w2}
    small_names = ["norm_mix_g", "conv_w", "rnn_conv_w", "rnn_conv_b", "w_a", "b_a", "w_x", "b_x", "lru_lambda",
                   "g_norm_conv", "g_norm_rnn", "norm_mlp_g", "final_norm_g"]
    for k, nm in enumerate(small_names):
        results[nm] = small_out[1 + 4 * k:5 + 4 * k]
    out = [loss, grad_x.reshape(x.shape)]
    for kind in range(4):
        out += [results[nm][kind].reshape(originals[nm].shape) for nm in names]
    return tuple(out)
```

```python
import functools

import jax
import jax.numpy as jnp
from jax import lax
from jax.experimental import pallas as pl
from jax.experimental.pallas import tpu as pltpu

F32 = jnp.float32
BF16 = jnp.bfloat16

D_MODEL = 1024
HEAD_DIM = 64
CONV_WIDTH = 512
LRU_WIDTH = 1024
MIX_WIDTH = CONV_WIDTH + LRU_WIDTH
IN_COLS = 3 * CONV_WIDTH + 2 * LRU_WIDTH
D_FF = 4 * D_MODEL
GROUP = 256
EPS = 1e-6
LRU_C = 8.0
N_DEV = 8
SUB = 8

OFF_GB, OFF_GC, OFF_V, OFF_XR, OFF_G = 0, 512, 1024, 1536, 2560

ADAM_LR, ADAM_B1, ADAM_B2, ADAM_EPS, ADAM_WD, ADAM_STEP = 0.001, 0.9, 0.999, 1e-08, 0.01, 10
BC1 = 1.0 - ADAM_B1 ** ADAM_STEP
BC2 = 1.0 - ADAM_B2 ** ADAM_STEP

MIB = 1024 * 1024
MESH = pl.DeviceIdType.MESH

VEC_ROWS = 32
ROW_GF, ROW_GMLP, ROW_LOSS = 0, 1, 2
ROW_GNC, ROW_GNR, ROW_BR, ROW_BA, ROW_BX, ROW_LAM, ROW_CW, ROW_RW = 8, 9, 10, 11, 12, 13, 14, 17
ROW_GMIX = 24
ACC_GNC, ACC_GNR, ACC_BR, ACC_BA, ACC_BX, ACC_SP, ACC_CW, ACC_RW, N_ACC = 0, 1, 2, 3, 4, 5, 6, 9, 13


def _params(semantics=None, vmem_mib=48):
    return pltpu.CompilerParams(dimension_semantics=semantics, vmem_limit_bytes=vmem_mib * MIB)


def _rms(x):
    return lax.rsqrt(jnp.mean(x * x, axis=-1, keepdims=True) + EPS)


def _rms_bwd(dy, xhat, r, g):
    dyh = dy * g
    return r * (dyh - xhat * jnp.mean(dyh * xhat, axis=-1, keepdims=True))


def _sigmoid(x):
    return 0.5 + 0.5 * jnp.tanh(0.5 * x)


def _gelu(x):
    c0, c1 = 0.7978845608028654, 0.044715
    x2 = x * x
    t = jnp.tanh(x * (c0 + (c0 * c1) * x2))
    half = 0.5 + 0.5 * t
    ge = x * half
    dge = half + (ge - ge * half) * (2.0 * c0 + (6.0 * c0 * c1) * x2)
    return ge, dge


def _softplus_neg(lam):
    z = -lam
    e = jnp.exp(-jnp.abs(z))
    return jnp.maximum(z, 0.0) + jnp.where(e < 1e-4, e * (1.0 - 0.5 * e), jnp.log(1.0 + e))


def _lru_gates(pa, px, sp_c):
    ra = _sigmoid(pa)
    ii = _sigmoid(px)
    neg_la = ra * sp_c
    a = jnp.exp(-neg_la)
    m2 = jnp.tanh(neg_la) * (1.0 + a * a)
    mult = jnp.where(m2 > 0.0, m2 * lax.rsqrt(m2), 0.0)
    return ra, ii, a, mult


def _down(cur, prev, s, row):
    return pltpu.roll(jnp.where(row < SUB - s, cur, prev), s, 0)


def _up(cur, nxt, s, row):
    return pltpu.roll(jnp.where(row >= s, cur, nxt), SUB - s, 0)


def _scan8_fwd(a, b, row):
    for s in (1, 2, 4):
        m = row >= s
        a_sh = pltpu.roll(a, s, 0)
        b_sh = pltpu.roll(b, s, 0)
        b = jnp.where(m, a * b_sh + b, b)
        a = jnp.where(m, a * a_sh, a)
    return a, b


def _scan8_rev(a, b, row):
    for s in (1, 2, 4):
        m = row < SUB - s
        a_sh = pltpu.roll(a, SUB - s, 0)
        b_sh = pltpu.roll(b, SUB - s, 0)
        b = jnp.where(m, a * b_sh + b, b)
        a = jnp.where(m, a * a_sh, a)
    return a, b


def _group_mask(shape):
    r = lax.broadcasted_iota(jnp.int32, shape, 0)
    c = lax.broadcasted_iota(jnp.int32, shape, 1)
    return ((r % GROUP) // HEAD_DIM) == (c // HEAD_DIM)


def _expand_heads(w):
    j = lax.broadcasted_iota(jnp.int32, (HEAD_DIM, GROUP), 0)
    c = lax.broadcasted_iota(jnp.int32, (HEAD_DIM, GROUP), 1)
    spread = (c % HEAD_DIM == j).astype(BF16)
    e = jnp.dot(w.astype(BF16), spread, preferred_element_type=F32)
    return jnp.where(_group_mask(e.shape), e, 0.0).astype(BF16)


def _fold_heads(p):
    p = jnp.where(_group_mask(p.shape), p, 0.0)
    c = lax.broadcasted_iota(jnp.int32, (GROUP, HEAD_DIM), 0)
    j = lax.broadcasted_iota(jnp.int32, (GROUP, HEAD_DIM), 1)
    fold = (c % HEAD_DIM == j).astype(BF16)
    hi = p.astype(BF16)
    rest = p - hi.astype(F32)
    mid = rest.astype(BF16)
    lo = (rest - mid.astype(F32)).astype(BF16)
    dot = functools.partial(jnp.dot, preferred_element_type=F32)
    return dot(hi, fold) + dot(mid, fold) + dot(lo, fold)


def _block_diag_apply(xb, wbd_ref):
    parts = [jnp.dot(xb[:, g * GROUP:(g + 1) * GROUP], wbd_ref[g * GROUP:(g + 1) * GROUP, :],
                     preferred_element_type=F32) for g in range(LRU_WIDTH // GROUP)]
    return jnp.concatenate(parts, axis=1)


def _block_diag_apply_t(db, wbd_ref):
    parts = [lax.dot_general(db[:, g * GROUP:(g + 1) * GROUP], wbd_ref[g * GROUP:(g + 1) * GROUP, :],
                             (((1,), (1,)), ((), ())), preferred_element_type=F32)
             for g in range(LRU_WIDTH // GROUP)]
    return jnp.concatenate(parts, axis=1)


def _dot_nt(a, b):
    return lax.dot_general(a, b, (((1,), (1,)), ((), ())), preferred_element_type=F32)


def _dot_tn(a, b):
    return lax.dot_general(a, b, (((0,), (0,)), ((), ())), preferred_element_type=F32)


CHUNKS_IN_FLIGHT = 8


def _chunk_loop(n_chunks, chunk, init):
    def body(k, carry):
        for j in range(CHUNKS_IN_FLIGHT):
            carry = chunk(k * CHUNKS_IN_FLIGHT + j, carry)
        return carry

    return lax.fori_loop(0, n_chunks // CHUNKS_IN_FLIGHT, body, init)


def _place():
    x, y, c = lax.axis_index("x"), lax.axis_index("y"), lax.axis_index("c")
    return x, y, c


def _block_id(chip, core):
    return 4 * chip[0] + 2 * chip[1] + core


def _other_chips(x, y):
    return [(1 - x, y), (x, 1 - y), (1 - x, 1 - y)]


def _remote_copy(src, dst, send_sem, recv_sem, to):
    return pltpu.make_async_remote_copy(src_ref=src, dst_ref=dst, send_sem=send_sem, recv_sem=recv_sem,
                                        device_id=to, device_id_type=MESH)


HBM_SPEC = pl.BlockSpec(memory_space=pl.ANY)


def _in_hbm(*arrays):
    return [pltpu.with_memory_space_constraint(a, pltpu.HBM) for a in arrays]


def _prep_shards(w_in_t, w_out, w_mlp_in, w_mlp_out, conv_w, rnn_conv_w):
    def body(win_ref, wout_ref, w1_ref, w2_ref, cw_ref, rw_ref, o_win, o_wout, o_w1, o_w2, o_cp):
        o_win[...] = win_ref[...].astype(BF16)
        o_wout[...] = wout_ref[...].astype(BF16)
        o_w1[...] = w1_ref[...].astype(BF16)
        o_w2[...] = w2_ref[...].astype(BF16)
        o_cp[...] = jnp.zeros(o_cp.shape, F32)
        o_cp[0:3, 0:64] = cw_ref[...]
        o_cp[3:7, :] = rw_ref[...]

    whole = lambda shape: pl.BlockSpec(shape, lambda i: (0,) * len(shape))
    args = (w_in_t, w_out, w_mlp_in, w_mlp_out, conv_w, rnn_conv_w)
    shapes = [(w_in_t.shape, BF16), (w_out.shape, BF16), (w_mlp_in.shape, BF16), (w_mlp_out.shape, BF16),
              ((8, 128), F32)]
    return pl.pallas_call(
        body, grid=(1,), out_shape=[jax.ShapeDtypeStruct(s, d) for s, d in shapes],
        in_specs=[whole(a.shape) for a in args], out_specs=[whole(s) for s, _ in shapes],
        compiler_params=_params(("arbitrary",), 40), name="prep_shards",
    )(*args)


def _host_all_gather(step, n_steps, shards, fulls, send_sems, recv_sems, local_sems):
    x, y, c = _place()
    me = (x, y, c)
    my_id = _block_id((x, y), c)
    sibling = (x, y, 1 - c)
    chips = _other_chips(x, y)
    n_arr = len(shards)

    def copy(arr, k, block, to, src=None):
        dst = fulls[arr].at[block]
        return _remote_copy(dst if src is None else src, dst, send_sems.at[arr, k], recv_sems.at[arr, k], to)

    def local(arr):
        return pltpu.make_async_copy(shards[arr], fulls[arr].at[my_id], local_sems.at[arr])

    @pl.when(step == 0)
    def _():
        for arr in range(n_arr):
            local(arr).start()
            copy(arr, 0, my_id, sibling, shards[arr]).start()
            for j, chip in enumerate(chips):
                copy(arr, 1 + j, my_id, (*chip, c), shards[arr]).start()

    @pl.when(step == max(n_steps - 2, 0))
    def _():
        for j, chip in enumerate(chips):
            for arr in range(n_arr):
                copy(arr, 1 + j, _block_id(chip, c), me).wait_recv()
                copy(arr, 4 + j, _block_id(chip, c), sibling).start()

    @pl.when(step == n_steps - 1)
    def _():
        for arr in range(n_arr):
            copy(arr, 0, _block_id((x, y), 1 - c), me).wait_recv()
            for j, chip in enumerate(chips):
                copy(arr, 4 + j, _block_id(chip, 1 - c), me).wait_recv()
            for k in range(4):
                copy(arr, k, my_id, me, shards[arr]).wait_send()
            for j, chip in enumerate(chips):
                copy(arr, 4 + j, _block_id(chip, c), me).wait_send()
            local(arr).wait()


def _host_pair_exchange(step, n_steps, gs, sibs, send_sems, recv_sems):
    x, y, c = _place()
    sibling = (x, y, 1 - c)
    chips = [(x, y)] + _other_chips(x, y)

    def d2d(arr, q):
        return _remote_copy(gs[arr].at[_block_id(chips[q], 1 - c)], sibs[arr].at[q],
                            send_sems.at[arr, q], recv_sems.at[arr, q], sibling)

    @pl.when(step == 0)
    def _():
        for arr in range(len(gs)):
            for q in (1, 2, 3, 0):
                d2d(arr, q).start()

    @pl.when(step == n_steps - 1)
    def _():
        for arr in range(len(gs)):
            for q in range(4):
                d2d(arr, q).wait()


def _host_chip_exchange(step, n_steps, hsends, hrecvs, send_sems, recv_sems):
    x, y, c = _place()
    chips = _other_chips(x, y)

    def ici(arr, j):
        return _remote_copy(hsends[arr].at[j], hrecvs[arr].at[j], send_sems.at[arr, j], recv_sems.at[arr, j],
                            (*chips[j], c))

    @pl.when(step == 0)
    def _():
        for arr in range(len(hsends)):
            for j in range(3):
                ici(arr, j).start()

    @pl.when(step == n_steps - 1)
    def _():
        for arr in range(len(hsends)):
            for j in range(3):
                ici(arr, j).wait()


def _host_half_exchange(step, n_steps, parts, sibs, send_sems, recv_sems):
    x, y, c = _place()
    n_q, rows2, _ = parts.shape
    half = rows2 // 2

    def d2d(q):
        src = parts.at[q, pl.ds(pl.multiple_of((1 - c) * half, 16), half), :]
        return _remote_copy(src, sibs.at[q], send_sems.at[q], recv_sems.at[q], (x, y, 1 - c))

    @pl.when(step == 0)
    def _():
        for q in range(n_q):
            d2d(q).start()

    @pl.when(step == n_steps - 1)
    def _():
        for q in range(n_q):
            d2d(q).wait()


def _peer(x, y, c, k):
    return (x ^ ((k >> 2) & 1), y ^ ((k >> 1) & 1), c ^ (k & 1))


def _host_small_exchange(step, n_steps, vec_m, vec_b, wab, vrecv_m, vrecv_b, wrecv, send_sems, recv_sems, local_sems):
    x, y, c = _place()
    my_id = _block_id((x, y), c)
    wrows = wab.shape[0] // N_DEV

    def copies(k):
        to = _peer(x, y, c, k)
        block = wab.at[pl.ds(pl.multiple_of(_block_id(to[0:2], to[2]) * wrows, SUB), wrows), :]
        return [_remote_copy(vec_m, vrecv_m.at[my_id], send_sems.at[0, k], recv_sems.at[0, k], to),
                _remote_copy(vec_b, vrecv_b.at[my_id], send_sems.at[1, k], recv_sems.at[1, k], to),
                _remote_copy(block, wrecv.at[k], send_sems.at[2, k], recv_sems.at[2, k], to)]

    mine = [pltpu.make_async_copy(vec_m, vrecv_m.at[my_id], local_sems.at[0]),
            pltpu.make_async_copy(vec_b, vrecv_b.at[my_id], local_sems.at[1])]

    @pl.when(step == 0)
    def _():
        for cp in mine:
            cp.start()
        for k in range(1, N_DEV):
            for cp in copies(k):
                cp.start()

    @pl.when(step == n_steps - 1)
    def _():
        for k in range(1, N_DEV):
            for cp in copies(k):
                cp.wait()
        for cp in mine:
            cp.wait()


def _pair_sum_parts(parts, sibs, core):
    n_q, rows2, cols = parts.shape
    half = rows2 // 2

    def body(core_ref, g_ref, s_ref, o_ref):
        o_ref[0] = (g_ref[0, 0].astype(F32) + s_ref[0].astype(F32)).astype(BF16)

    block = (1, half, cols)
    grid_spec = pltpu.PrefetchScalarGridSpec(
        num_scalar_prefetch=1, grid=(n_q,),
        in_specs=[pl.BlockSpec((1, 1, half, cols), lambda q, cr: (q, cr[0], 0, 0)),
                  pl.BlockSpec(block, lambda q, cr: (q, 0, 0))],
        out_specs=pl.BlockSpec(block, lambda q, cr: (q, 0, 0)))
    return pl.pallas_call(
        body, grid_spec=grid_spec, out_shape=pltpu.HBM((n_q, half, cols), BF16),
        compiler_params=_params(("arbitrary",), 32), name="pair_sum_w_in",
    )(core, *_in_hbm(parts.reshape(n_q, 2, half, cols), sibs))


def _pair_sum(gs, sibs, name):
    n_arr = len(gs)
    x, y, c = _place()
    slots = jnp.stack([_block_id(chip, c) for chip in [(x, y)] + _other_chips(x, y)]).astype(jnp.int32)

    def body(slots_ref, *refs):
        q = pl.program_id(0)
        for k in range(n_arr):
            g_ref, sib_ref = refs[2 * k:2 * k + 2]
            hs_ref, own_ref = refs[2 * n_arr + 2 * k:2 * n_arr + 2 * k + 2]
            both = g_ref[0].astype(F32) + sib_ref[0].astype(F32)

            @pl.when(q == 0)
            def _(own_ref=own_ref, both=both):
                own_ref[...] = both

            @pl.when(q > 0)
            def _(hs_ref=hs_ref, both=both):
                hs_ref[0] = both.astype(BF16)

    in_specs, out_specs, out_shape, args = [], [], [], []
    for g, sib in zip(gs, sibs):
        _, rows, cols = g.shape
        block = (1, rows, cols)
        in_specs += [pl.BlockSpec(block, lambda q, s: (s[q], 0, 0)), pl.BlockSpec(block, lambda q, s: (q, 0, 0))]
        out_specs += [pl.BlockSpec(block, lambda q, s: (jnp.maximum(q - 1, 0), 0, 0)),
                      pl.BlockSpec((rows, cols), lambda q, s: (0, 0))]
        out_shape += [pltpu.HBM((3, rows, cols), BF16), pltpu.HBM((rows, cols), F32)]
        args += _in_hbm(g, sib)
    grid_spec = pltpu.PrefetchScalarGridSpec(num_scalar_prefetch=1, grid=(4,), in_specs=in_specs, out_specs=out_specs)
    return pl.pallas_call(
        body, grid_spec=grid_spec, out_shape=out_shape,
        compiler_params=_params(("arbitrary",), 40), name=name,
    )(slots, *args)


def _exchange_scratch(n_arr, n_copies):
    return [pltpu.SemaphoreType.DMA((n_arr, n_copies)), pltpu.SemaphoreType.DMA((n_arr, n_copies))]


def _final_small(vrecv_m, vrecv_b, wab, wrecv, vec_x):
    wrows = wab.shape[0] // N_DEV

    def body(vm_ref, vb_ref, w_ref, wr_ref, vx_ref, o_vec, o_w, xrecv, wred, x_send, x_recv, b_send, b_recv):
        x, y, c = _place()
        my_id = _block_id((x, y), c)
        my_rows = pl.ds(pl.multiple_of(my_id * wrows, SUB), wrows)

        def xcopy(k):
            return _remote_copy(vx_ref, xrecv.at[my_id], x_send.at[k], x_recv.at[k], _peer(x, y, c, k))

        def bcopy(k):
            return _remote_copy(wred, o_w.at[my_rows, :], b_send.at[k], b_recv.at[k], _peer(x, y, c, k))

        xrecv[my_id] = vx_ref[...]
        for k in range(1, N_DEV):
            xcopy(k).start()
        red = w_ref[my_rows, :]
        for k in range(1, N_DEV):
            red = red + wr_ref[k]
        wred[...] = red
        o_w[my_rows, :] = red
        for k in range(1, N_DEV):
            bcopy(k).start()
        for k in range(1, N_DEV):
            xcopy(k).wait_recv()
        for rows, ref in ((slice(0, 8), vm_ref), (slice(8, 24), vb_ref), (slice(24, 32), xrecv)):
            tot = ref[0]
            for s in range(1, N_DEV):
                tot = tot + ref[s]
            o_vec[rows, :] = tot
        for k in range(1, N_DEV):
            bcopy(k).wait_recv()
        for k in range(1, N_DEV):
            xcopy(k).wait_send()
            bcopy(k).wait_send()

    vm = pl.BlockSpec(memory_space=pltpu.VMEM)
    dma8 = pltpu.SemaphoreType.DMA((N_DEV,))
    return pl.pallas_call(
        body, out_shape=(jax.ShapeDtypeStruct((VEC_ROWS, D_MODEL), F32), jax.ShapeDtypeStruct(wab.shape, F32)),
        in_specs=[vm] * 5, out_specs=[vm] * 2,
        scratch_shapes=[pltpu.VMEM((N_DEV, SUB, D_MODEL), F32), pltpu.VMEM((wrows, HEAD_DIM), F32),
                        dma8, dma8, dma8, dma8],
        compiler_params=_params(vmem_mib=32), name="final_small",
    )(vrecv_m, vrecv_b, wab, wrecv, vec_x)


def _in_proj(x, g_mix, shards, tm):
    t_len = x.shape[0]
    n_t = t_len // tm
    n_arr = len(shards)
    rows = [s.shape[0] for s in shards]
    width = 2 * rows[0]
    ax, ay = lax.axis_index("x"), lax.axis_index("y")
    order = jnp.stack([2 * cx + cy for cx, cy in [(ax, ay)] + _other_chips(ax, ay)]).astype(jnp.int32)

    def body(order_ref, x_ref, g_ref, *rest):
        shard_refs = rest[0:n_arr]
        u_ref, h_ref = rest[n_arr:n_arr + 2]
        fulls = rest[n_arr + 2:2 * n_arr + 2]
        h_s, wbuf, send_sems, recv_sems, local_sems, load_sem = rest[2 * n_arr + 2:]
        p = pl.program_id(0)
        i = pl.program_id(1)
        x_, y_, c = _place()
        me = (x_, y_, c)
        my_id = _block_id((x_, y_), c)
        sibling = (x_, y_, 1 - c)
        chips = _other_chips(x_, y_)

        def block(arr, blk):
            return fulls[arr].at[pl.ds(pl.multiple_of(blk * rows[arr], rows[arr]), rows[arr]), :]

        def copy(arr, k, blk, to, src=None):
            dst = block(arr, blk)
            return _remote_copy(dst if src is None else src, dst, send_sems.at[arr, k], recv_sems.at[arr, k], to)

        def local(arr):
            return pltpu.make_async_copy(shard_refs[arr], block(arr, my_id), local_sems.at[arr])

        def load_chip(chip, slot):
            start = pl.multiple_of((2 * chip[0] + chip[1]) * width, width)
            return pltpu.make_async_copy(fulls[0].at[pl.ds(start, width), :], wbuf.at[slot], load_sem.at[slot])

        def pass_on(j):
            for arr in range(n_arr):
                copy(arr, 1 + j, _block_id(chips[j], c), me).wait_recv()
                copy(arr, 4 + j, _block_id(chips[j], c), sibling).start()

        def complete(j):
            for arr in range(n_arr):
                copy(arr, 4 + j, _block_id(chips[j], 1 - c), me).wait_recv()

        @pl.when((p == 0) & (i == 0))
        def _():
            for arr in range(n_arr):
                local(arr).start()
                copy(arr, 0, my_id, sibling, shard_refs[arr]).start()
                for j in (0, 1):
                    copy(arr, 1 + j, my_id, (*chips[j], c), shard_refs[arr]).start()
            for arr in range(n_arr):
                local(arr).wait()
                copy(arr, 0, _block_id((x_, y_), 1 - c), me).wait_recv()
            load_chip((x_, y_), 0).start()
            load_chip((x_, y_), 0).wait()

        @pl.when((p == 1) & (i == 0))
        def _():
            pass_on(0)
            for arr in range(n_arr):
                copy(arr, 3, my_id, (*chips[2], c), shard_refs[arr]).start()
            pass_on(1)
            complete(0)
            load_chip(chips[0], 1).start()
            load_chip(chips[0], 1).wait()
            complete(1)
            load_chip(chips[1], 0).start()

        @pl.when((p == 2) & (i == 0))
        def _():
            load_chip(chips[1], 0).wait()

        @pl.when((p == 3) & (i == 0))
        def _():
            pass_on(2)
            complete(2)
            load_chip(chips[2], 1).start()
            load_chip(chips[2], 1).wait()

        @pl.when((p == 3) & (i == n_t - 1))
        def _():
            for arr in range(n_arr):
                for k in range(4):
                    copy(arr, k, my_id, me, shard_refs[arr]).wait_send()
                for j, chip in enumerate(chips):
                    copy(arr, 4 + j, _block_id(chip, c), me).wait_send()

        tile = pl.ds(pl.multiple_of(i * tm, tm), tm)

        @pl.when(p == 0)
        def _():
            xv = x_ref[...]
            h = (xv * _rms(xv) * g_ref[...]).astype(BF16)
            h_ref[...] = h
            h_s[tile, :] = h

        for slot in (0, 1):
            @pl.when(p % 2 == slot)
            def _(slot=slot):
                u_ref[...] = _dot_nt(h_s[tile, :], wbuf[slot])

    first_pass = lambda p, i, o: (jnp.where(p == 0, i, n_t - 1), 0)
    grid_spec = pltpu.PrefetchScalarGridSpec(
        num_scalar_prefetch=1, grid=(4, n_t),
        in_specs=[pl.BlockSpec((tm, D_MODEL), first_pass), pl.BlockSpec((1, D_MODEL), lambda p, i, o: (0, 0))]
        + [HBM_SPEC] * n_arr,
        out_specs=[pl.BlockSpec((tm, width), lambda p, i, o: (i, o[p])), pl.BlockSpec((tm, D_MODEL), first_pass)]
        + [HBM_SPEC] * n_arr,
        scratch_shapes=[pltpu.VMEM((t_len, D_MODEL), BF16), pltpu.VMEM((2, width, D_MODEL), BF16)]
        + _exchange_scratch(n_arr, 7) + [pltpu.SemaphoreType.DMA((n_arr,)), pltpu.SemaphoreType.DMA((2,))])
    return pl.pallas_call(
        body, grid_spec=grid_spec,
        out_shape=[jax.ShapeDtypeStruct((t_len, IN_COLS), F32), jax.ShapeDtypeStruct((t_len, D_MODEL), BF16)]
        + [jax.ShapeDtypeStruct((N_DEV * s.shape[0], s.shape[1]), s.dtype) for s in shards],
        compiler_params=_params(("arbitrary", "arbitrary"), 48), name="in_proj",
    )(order, x, g_mix, *shards)


def _conv3_chunk(u_ref, r, cv_prev, cw, row):
    gb = u_ref[pl.ds(r, SUB), OFF_GB:OFF_GB + CONV_WIDTH]
    gc = u_ref[pl.ds(r, SUB), OFF_GC:OFF_GC + CONV_WIDTH]
    v = u_ref[pl.ds(r, SUB), OFF_V:OFF_V + CONV_WIDTH]
    cv = gc * v
    cv_m1 = _down(cv, cv_prev, 1, row)
    cv_m2 = _down(cv, cv_prev, 2, row)
    cq = cw[2:3, :] * cv + cw[1:2, :] * cv_m1 + cw[0:1, :] * cv_m2
    return gb, gc, v, cv, cv_m1, cv_m2, cq


def _conv4_chunk(u_ref, r, xin_prev, rw, rb, row):
    xin = u_ref[pl.ds(r, SUB), OFF_XR:OFF_XR + LRU_WIDTH]
    m1 = _down(xin, xin_prev, 1, row)
    m2 = _down(xin, xin_prev, 2, row)
    m3 = _down(xin, xin_prev, 3, row)
    xr = rw[3:4, :] * xin + rw[2:3, :] * m1 + rw[1:2, :] * m2 + rw[0:1, :] * m3 + rb
    return xin, m1, m2, m3, xr


def _mixer_fwd(u, conv_w, rnn_conv_w, rnn_conv_b, wa, b_a, wx, b_x, lam, gnc, gnr, shards, tm):
    t_len = u.shape[0]
    n_steps = t_len // tm
    n_chunks = tm // SUB
    n_arr = len(shards)

    def body(u_ref, cw_ref, rw_ref, rb_ref, wa_ref, ba_ref, wx_ref, bx_ref, lam_ref, gnc_ref, gnr_ref, *rest):
        shard_refs = rest[0:n_arr]
        hs_ref, y_ref, xr_s, ra_ref, ii_ref, mult_ref, cq_ref = rest[n_arr:n_arr + 7]
        fulls = rest[n_arr + 7:2 * n_arr + 7]
        (y_s, pa_s, px_s, wabd, wxbd, cv_car, xin_car, h_car,
         send_sems, recv_sems, local_sems) = rest[2 * n_arr + 7:]
        _host_all_gather(pl.program_id(0), n_steps, shard_refs, fulls, send_sems, recv_sems, local_sems)

        @pl.when(pl.program_id(0) == 0)
        def _():
            cv_car[...] = jnp.zeros(cv_car.shape, F32)
            xin_car[...] = jnp.zeros(xin_car.shape, F32)
            h_car[...] = jnp.zeros(h_car.shape, F32)
            wabd[...] = _expand_heads(wa_ref[...])
            wxbd[...] = _expand_heads(wx_ref[...])

        row_c = lax.broadcasted_iota(jnp.int32, (SUB, CONV_WIDTH), 0)
        row_r = lax.broadcasted_iota(jnp.int32, (SUB, LRU_WIDTH), 0)
        cw = cw_ref[...]
        rw = rw_ref[...]
        rb = rb_ref[...]
        g_c = gnc_ref[...]
        g_r = gnr_ref[...]
        sp_c = LRU_C * _softplus_neg(lam_ref[...])

        def convs(i, carry):
            cv_prev, xin_prev = carry
            r = pl.multiple_of(i * SUB, SUB)
            gb, _, _, cv, _, _, cq = _conv3_chunk(u_ref, r, cv_prev, cw, row_c)
            cq_ref[pl.ds(r, SUB), :] = cq
            y_c = gb * cq
            y_s[pl.ds(r, SUB), 0:CONV_WIDTH] = y_c * _rms(y_c) * g_c
            xin, _, _, _, xr = _conv4_chunk(u_ref, r, xin_prev, rw, rb, row_r)
            xr_s[pl.ds(r, SUB), :] = xr
            return cv, xin

        cv_last, xin_last = _chunk_loop(n_chunks, convs, (cv_car[...], xin_car[...]))
        cv_car[...] = cv_last
        xin_car[...] = xin_last

        xrb = xr_s[...].astype(BF16)
        pa_s[...] = _block_diag_apply(xrb, wabd) + ba_ref[...]
        px_s[...] = _block_diag_apply(xrb, wxbd) + bx_ref[...]

        def recur(i, h_prev):
            r = pl.multiple_of(i * SUB, SUB)
            xr = xr_s[pl.ds(r, SUB), :]
            ra, ii, a, mult = _lru_gates(pa_s[pl.ds(r, SUB), :], px_s[pl.ds(r, SUB), :], sp_c)
            ra_ref[pl.ds(r, SUB), :] = ra
            ii_ref[pl.ds(r, SUB), :] = ii
            mult_ref[pl.ds(r, SUB), :] = mult
            a_cum, b_cum = _scan8_fwd(a, mult * ii * xr, row_r)
            h = a_cum * h_prev + b_cum
            hs_ref[pl.ds(r, SUB), :] = h
            ge, _ = _gelu(u_ref[pl.ds(r, SUB), OFF_G:OFF_G + LRU_WIDTH])
            y_r = h * ge
            y_s[pl.ds(r, SUB), CONV_WIDTH:MIX_WIDTH] = y_r * _rms(y_r) * g_r
            return h[SUB - 1:SUB, :]

        h_car[...] = _chunk_loop(n_chunks, recur, h_car[...])

        y_ref[...] = y_s[...].astype(BF16)

    row_tile = lambda w: pl.BlockSpec((tm, w), lambda i: (i, 0))
    whole = lambda a: pl.BlockSpec(a.shape, lambda i: (0,) * a.ndim)
    smalls = (conv_w, rnn_conv_w, rnn_conv_b, wa, b_a, wx, b_x, lam, gnc, gnr)
    return pl.pallas_call(
        body, grid=(n_steps,),
        in_specs=[row_tile(IN_COLS)] + [whole(a) for a in smalls] + [HBM_SPEC] * n_arr,
        out_specs=[row_tile(LRU_WIDTH), row_tile(MIX_WIDTH)] + [row_tile(LRU_WIDTH)] * 4 + [row_tile(CONV_WIDTH)]
        + [HBM_SPEC] * n_arr,
        out_shape=[jax.ShapeDtypeStruct((t_len, LRU_WIDTH), F32), jax.ShapeDtypeStruct((t_len, MIX_WIDTH), BF16)]
        + [jax.ShapeDtypeStruct((t_len, LRU_WIDTH), F32)] * 4 + [jax.ShapeDtypeStruct((t_len, CONV_WIDTH), F32)]
        + [jax.ShapeDtypeStruct((N_DEV,) + s.shape, BF16) for s in shards],
        scratch_shapes=[pltpu.VMEM((tm, MIX_WIDTH), F32),
                        pltpu.VMEM((tm, LRU_WIDTH), F32), pltpu.VMEM((tm, LRU_WIDTH), F32),
                        pltpu.VMEM((LRU_WIDTH, GROUP), BF16), pltpu.VMEM((LRU_WIDTH, GROUP), BF16),
                        pltpu.VMEM((SUB, CONV_WIDTH), F32), pltpu.VMEM((SUB, LRU_WIDTH), F32),
                        pltpu.VMEM((1, LRU_WIDTH), F32)]
        + _exchange_scratch(n_arr, 7) + [pltpu.SemaphoreType.DMA((n_arr,))],
        compiler_params=_params(("arbitrary",), 56), name="mixer_fwd",
    )(u, *smalls, *shards)


def _mlp_up(x, y, g_mlp, w_out, w1, w2_shard, tm):
    t_len = x.shape[0]
    n_steps = t_len // tm
    n_blk, _, blk = w1.shape

    def body(x_ref, y_ref, gm_ref, wout_hbm, w1_hbm, w2_ref, x1_ref, h2_ref, z_ref, w2_full,
             wout_s, w1_s, sem, send_sems, recv_sems, local_sems):
        step = pl.program_id(0)
        _host_all_gather(step, n_steps, [w2_ref], [w2_full], send_sems, recv_sems, local_sems)

        load_wout = pltpu.make_async_copy(wout_hbm, wout_s, sem.at[0])
        load_w1 = pltpu.make_async_copy(w1_hbm, w1_s, sem.at[1])

        @pl.when(step == 0)
        def _():
            load_wout.start()
            load_w1.start()
            load_wout.wait()

        x1v = x_ref[...] + jnp.dot(y_ref[...], wout_s[...], preferred_element_type=F32)
        x1_ref[...] = x1v
        h2 = (x1v * _rms(x1v) * gm_ref[...]).astype(BF16)
        h2_ref[...] = h2

        @pl.when(step == 0)
        def _():
            load_w1.wait()

        for k in range(n_blk):
            rp = jnp.maximum(jnp.dot(h2, w1_s[k], preferred_element_type=F32), 0.0)
            z_ref[:, k * blk:(k + 1) * blk] = (rp * rp).astype(BF16)

    row_tile = lambda w: pl.BlockSpec((tm, w), lambda i: (i, 0))
    return pl.pallas_call(
        body, grid=(n_steps,),
        in_specs=[row_tile(D_MODEL), row_tile(MIX_WIDTH), pl.BlockSpec((1, D_MODEL), lambda i: (0, 0)),
                  HBM_SPEC, HBM_SPEC, HBM_SPEC],
        out_specs=[row_tile(D_MODEL), row_tile(D_MODEL), row_tile(D_FF), HBM_SPEC],
        out_shape=[jax.ShapeDtypeStruct((t_len, D_MODEL), F32), jax.ShapeDtypeStruct((t_len, D_MODEL), BF16),
                   jax.ShapeDtypeStruct((t_len, D_FF), BF16), jax.ShapeDtypeStruct((N_DEV,) + w2_shard.shape, BF16)],
        scratch_shapes=[pltpu.VMEM(w_out.shape, BF16), pltpu.VMEM(w1.shape, BF16), pltpu.SemaphoreType.DMA((2,))]
        + _exchange_scratch(1, 7) + [pltpu.SemaphoreType.DMA((1,))],
        compiler_params=_params(("arbitrary",), 48), name="mlp_up",
    )(x, y, g_mlp, w_out, w1, w2_shard)


def _mlp_down_bwd(x1, z, target, g_mlp, g_f, w1, w2, tm):
    t_len = x1.shape[0]
    n_steps = t_len // tm
    n_blk, _, blk = w1.shape

    def body(x1_ref, z_ref, tg_ref, gm_ref, gf_ref, w1_hbm, w2_hbm, dx1_ref, dx2_ref, vec_ref, dpre_hbm,
             w1_s, w2_s, dp_s, sem, out_sem):
        step = pl.program_id(0)
        rows = pl.ds(pl.multiple_of(step * tm, tm), tm)
        dp_out = pltpu.make_async_copy(dp_s, dpre_hbm.at[rows, :], out_sem.at[0])

        load_w1 = pltpu.make_async_copy(w1_hbm, w1_s, sem.at[0])
        load_w2 = pltpu.make_async_copy(w2_hbm, w2_s, sem.at[1])

        @pl.when(step == 0)
        def _():
            load_w2.start()
            load_w1.start()
            vec_ref[...] = jnp.zeros(vec_ref.shape, F32)
            load_w2.wait()

        x1v = x1_ref[...]
        g_m = gm_ref[...]
        g_o = gf_ref[...]
        r2 = _rms(x1v)
        x1h = x1v * r2
        x2 = x1v + jnp.dot(z_ref[...], w2_s[...], preferred_element_type=F32)
        r3 = _rms(x2)
        x2h = x2 * r3
        err = x2h * g_o - tg_ref[...]
        dout = err * (1.0 / D_MODEL)
        vec_ref[ROW_LOSS:ROW_LOSS + 1, :] += (0.5 / D_MODEL) * jnp.sum(err * err, axis=0, keepdims=True)
        vec_ref[ROW_GF:ROW_GF + 1, :] += jnp.sum(dout * x2h, axis=0, keepdims=True)
        dx2 = _rms_bwd(dout, x2h, r3, g_o)
        dx2b = dx2.astype(BF16)
        dx2_ref[...] = dx2b
        dh2 = jnp.zeros((tm, D_MODEL), F32)

        @pl.when(step > 0)
        def _():
            dp_out.wait()

        @pl.when(step == 0)
        def _():
            load_w1.wait()

        for k in range(n_blk):
            cols = slice(k * blk, (k + 1) * blk)
            dz = _dot_nt(dx2b, w2_s[cols, :])
            dpb = (dz * 2.0 * jnp.sqrt(z_ref[:, cols].astype(F32))).astype(BF16)
            dp_s[:, cols] = dpb
            dh2 = dh2 + _dot_nt(dpb, w1_s[k])
        dp_out.start()
        vec_ref[ROW_GMLP:ROW_GMLP + 1, :] += jnp.sum(dh2 * x1h, axis=0, keepdims=True)
        dx1_ref[...] = dx2 + _rms_bwd(dh2, x1h, r2, g_m)

        @pl.when(step == n_steps - 1)
        def _():
            dp_out.wait()

    row_tile = lambda w: pl.BlockSpec((tm, w), lambda i: (i, 0))
    vec_spec = pl.BlockSpec((1, D_MODEL), lambda i: (0, 0))
    return pl.pallas_call(
        body, grid=(n_steps,),
        in_specs=[row_tile(D_MODEL), row_tile(D_FF), row_tile(D_MODEL), vec_spec, vec_spec, HBM_SPEC, HBM_SPEC],
        out_specs=[row_tile(D_MODEL), row_tile(D_MODEL), pl.BlockSpec((SUB, D_MODEL), lambda i: (0, 0)), HBM_SPEC],
        out_shape=[jax.ShapeDtypeStruct((t_len, D_MODEL), F32), jax.ShapeDtypeStruct((t_len, D_MODEL), BF16),
                   jax.ShapeDtypeStruct((SUB, D_MODEL), F32), jax.ShapeDtypeStruct((t_len, D_FF), BF16)],
        scratch_shapes=[pltpu.VMEM(w1.shape, BF16), pltpu.VMEM(w2.shape, BF16), pltpu.VMEM((tm, D_FF), BF16),
                        pltpu.SemaphoreType.DMA((2,)), pltpu.SemaphoreType.DMA((1,))],
        compiler_params=_params(("arbitrary",), 56), name="mlp_down_bwd",
    )(x1, z, target, g_mlp, g_f, w1, w2)


def _mixer_bwd(u, hs, dx1, saved, conv_w, rnn_conv_w, wa, wx, lam, gnc, gnr, w_out, chip_sums, g_wout, tm):
    t_len = u.shape[0]
    n_tiles = t_len // tm
    n_chunks = tm // SUB
    per_tile = tm // SUB
    n_sums = len(chip_sums)

    def body(u_ref, hs_ref, hp_ref, dx1_ref, xr_ref, ra_ref, ii_ref, mult_ref, cq_ref,
             cw_ref, rw_ref, wa_ref, wx_ref, lam_ref, gnc_ref, gnr_ref, wout_ref, *rest):
        hsends = rest[0:n_sums]
        gwout_ref = rest[n_sums]
        du_ref, vec_ref, wab_ref = rest[n_sums + 1:n_sums + 4]
        hrecvs = rest[n_sums + 4:2 * n_sums + 4]
        sib_wout = rest[2 * n_sums + 4]
        (du_s, dy_s, dpa_s, dpx_s, dxr_s, wabd, wxbd, acc, dwa_acc, dwx_acc,
         a_car, dh_car, dcq_car, dxr_car, i_send, i_recv, d_send, d_recv) = rest[2 * n_sums + 5:]
        step = pl.program_id(0)
        _host_chip_exchange(step, n_tiles, hsends, hrecvs, i_send, i_recv)
        _host_pair_exchange(step, n_tiles, [gwout_ref], [sib_wout], d_send, d_recv)
        has_prev = (step < n_tiles - 1).astype(F32)

        @pl.when(step == 0)
        def _():
            acc[...] = jnp.zeros(acc.shape, F32)
            dwa_acc[...] = jnp.zeros(dwa_acc.shape, F32)
            dwx_acc[...] = jnp.zeros(dwx_acc.shape, F32)
            a_car[...] = jnp.ones(a_car.shape, F32)
            dh_car[...] = jnp.zeros(dh_car.shape, F32)
            dcq_car[...] = jnp.zeros(dcq_car.shape, F32)
            dxr_car[...] = jnp.zeros(dxr_car.shape, F32)
            wabd[...] = _expand_heads(wa_ref[...])
            wxbd[...] = _expand_heads(wx_ref[...])

        row_c = lax.broadcasted_iota(jnp.int32, (SUB, CONV_WIDTH), 0)
        row_r = lax.broadcasted_iota(jnp.int32, (SUB, LRU_WIDTH), 0)
        cw = cw_ref[...]
        rw = rw_ref[...]
        g_c = gnc_ref[...]
        g_r = gnr_ref[...]
        sp_c = LRU_C * _softplus_neg(lam_ref[...])

        hs_before = hp_ref[...] * has_prev

        dy_s[...] = _dot_nt(dx1_ref[...].astype(BF16), wout_ref[...])

        xrb = xr_ref[...].astype(BF16)

        def recur_bwd(j, carry):
            a_later, dh_later = carry
            i = n_chunks - 1 - j
            r = pl.multiple_of(i * SUB, SUB)
            rp = pl.multiple_of(jnp.maximum(i - 1, 0) * SUB, SUB)
            xr = xr_ref[pl.ds(r, SUB), :]
            hs_c = hs_ref[pl.ds(r, SUB), :]
            hs_prev = jnp.where(i == 0, hs_before, hs_ref[pl.ds(rp, SUB), :])
            h_m1 = _down(hs_c, hs_prev, 1, row_r)
            ra = ra_ref[pl.ds(r, SUB), :]
            ii = ii_ref[pl.ds(r, SUB), :]
            mult = mult_ref[pl.ds(r, SUB), :]
            a = jnp.exp(-ra * sp_c)
            inv_mult = lax.rsqrt(mult * mult)
            ge, dge = _gelu(u_ref[pl.ds(r, SUB), OFF_G:OFF_G + LRU_WIDTH])
            y_r = hs_c * ge
            rr = _rms(y_r)
            yhat = y_r * rr
            dyn = dy_s[pl.ds(r, SUB), CONV_WIDTH:MIX_WIDTH]
            acc[ACC_GNR] += dyn * yhat
            dy_r = _rms_bwd(dyn, yhat, rr, g_r)
            du_s[pl.ds(r, SUB), OFF_G:OFF_G + LRU_WIDTH] = dy_r * hs_c * dge
            a_cum, d_cum = _scan8_rev(_up(a, a_later, 1, row_r), dy_r * ge, row_r)
            dh = a_cum * dh_later + d_cum
            dm = dh * mult
            dii = dm * xr
            dxr_s[pl.ds(r, SUB), :] = dm * ii
            dla = a * dh * (h_m1 - (ii * xr) * a * inv_mult)
            dla_r = dla * ra
            acc[ACC_SP] -= dla_r
            dpa = dla_r * (sp_c * (ra - 1.0))
            dpx = dii * ii * (1.0 - ii)
            acc[ACC_BA] += dpa
            acc[ACC_BX] += dpx
            dpa_s[pl.ds(r, SUB), :] = dpa
            dpx_s[pl.ds(r, SUB), :] = dpx
            return a, dh[0:1, :]

        a_first, dh_first = _chunk_loop(n_chunks, recur_bwd, (a_car[...], dh_car[...]))
        a_car[...] = a_first
        dh_car[...] = dh_first

        dpab = dpa_s[...].astype(BF16)
        dpxb = dpx_s[...].astype(BF16)
        dxr_s[...] += _block_diag_apply_t(dpab, wabd) + _block_diag_apply_t(dpxb, wxbd)
        for g in range(LRU_WIDTH // GROUP):
            cols = slice(g * GROUP, (g + 1) * GROUP)
            dwa_acc[cols, :] += _dot_tn(xrb[:, cols], dpab[:, cols])
            dwx_acc[cols, :] += _dot_tn(xrb[:, cols], dpxb[:, cols])

        def convs_bwd(j, carry):
            dcq_later, dxr_later = carry
            i = n_chunks - 1 - j
            r = pl.multiple_of(i * SUB, SUB)
            gb = u_ref[pl.ds(r, SUB), OFF_GB:OFF_GB + CONV_WIDTH]
            gc = u_ref[pl.ds(r, SUB), OFF_GC:OFF_GC + CONV_WIDTH]
            v = u_ref[pl.ds(r, SUB), OFF_V:OFF_V + CONV_WIDTH]
            cv = gc * v
            cq = cq_ref[pl.ds(r, SUB), :]
            y_c = gb * cq
            rc = _rms(y_c)
            yhat = y_c * rc
            dyn = dy_s[pl.ds(r, SUB), 0:CONV_WIDTH]
            acc[ACC_GNC, :, 0:CONV_WIDTH] += dyn * yhat
            dy_c = _rms_bwd(dyn, yhat, rc, g_c)
            dcq = dy_c * gb
            ahead3 = [dcq, _up(dcq, dcq_later, 1, row_c), _up(dcq, dcq_later, 2, row_c)]
            dcv = cw[2:3, :] * ahead3[0] + cw[1:2, :] * ahead3[1] + cw[0:1, :] * ahead3[2]
            for k in range(3):
                acc[ACC_CW + 2 - k, :, 0:CONV_WIDTH] += ahead3[k] * cv
            du_s[pl.ds(r, SUB), OFF_GB:OFF_GB + CONV_WIDTH] = dy_c * cq
            du_s[pl.ds(r, SUB), OFF_GC:OFF_GC + CONV_WIDTH] = dcv * v
            du_s[pl.ds(r, SUB), OFF_V:OFF_V + CONV_WIDTH] = dcv * gc

            xin = u_ref[pl.ds(r, SUB), OFF_XR:OFF_XR + LRU_WIDTH]
            dxr = dxr_s[pl.ds(r, SUB), :]
            ahead = [dxr] + [_up(dxr, dxr_later, k, row_r) for k in (1, 2, 3)]
            du_s[pl.ds(r, SUB), OFF_XR:OFF_XR + LRU_WIDTH] = (
                rw[3:4, :] * ahead[0] + rw[2:3, :] * ahead[1] + rw[1:2, :] * ahead[2] + rw[0:1, :] * ahead[3])
            for k in range(4):
                acc[ACC_RW + 3 - k] += ahead[k] * xin
            acc[ACC_BR] += dxr
            return dcq, dxr

        dcq_first, dxr_first = _chunk_loop(n_chunks, convs_bwd, (dcq_car[...], dxr_car[...]))
        dcq_car[...] = dcq_first
        dxr_car[...] = dxr_first

        du_ref[...] = du_s[...].astype(BF16)

        @pl.when(step == n_tiles - 1)
        def _():
            vec_ref[...] = jnp.zeros(vec_ref.shape, F32)
            rows = {ACC_GNC: ROW_GNC, ACC_GNR: ROW_GNR, ACC_BR: ROW_BR, ACC_BA: ROW_BA, ACC_BX: ROW_BX}
            for k in range(3):
                rows[ACC_CW + k] = ROW_CW + k
            for k in range(4):
                rows[ACC_RW + k] = ROW_RW + k
            for slot, out_row in rows.items():
                o = out_row - ROW_GNC
                vec_ref[o:o + 1, :] = jnp.sum(acc[slot], axis=0, keepdims=True)
            lam_v = lam_ref[...]
            dsp = jnp.sum(acc[ACC_SP], axis=0, keepdims=True)
            o = ROW_LAM - ROW_GNC
            vec_ref[o:o + 1, :] = -dsp * LRU_C / (1.0 + jnp.exp(lam_v))
            wab_ref[0:LRU_WIDTH, :] = _fold_heads(dwa_acc[...])
            wab_ref[LRU_WIDTH:2 * LRU_WIDTH, :] = _fold_heads(dwx_acc[...])

    rev = lambda w: pl.BlockSpec((tm, w), lambda s: (n_tiles - 1 - s, 0))
    before = lambda w: pl.BlockSpec((SUB, w), lambda s: (jnp.maximum((n_tiles - 1 - s) * per_tile - 1, 0), 0))
    whole = lambda a: pl.BlockSpec(a.shape, lambda s: (0,) * a.ndim)
    smalls = (conv_w, rnn_conv_w, wa, wx, lam, gnc, gnr, w_out)
    full = lambda w: pltpu.VMEM((tm, w), F32)
    return pl.pallas_call(
        body, grid=(n_tiles,),
        in_specs=[rev(IN_COLS), rev(LRU_WIDTH), before(LRU_WIDTH), rev(D_MODEL)]
        + [rev(a.shape[1]) for a in saved] + [whole(a) for a in smalls] + [HBM_SPEC] * (n_sums + 1),
        out_specs=[rev(IN_COLS), pl.BlockSpec((16, D_MODEL), lambda s: (0, 0)),
                   pl.BlockSpec((2 * LRU_WIDTH, HEAD_DIM), lambda s: (0, 0))] + [HBM_SPEC] * (n_sums + 1),
        out_shape=[jax.ShapeDtypeStruct((t_len, IN_COLS), BF16), jax.ShapeDtypeStruct((16, D_MODEL), F32),
                   jax.ShapeDtypeStruct((2 * LRU_WIDTH, HEAD_DIM), F32)]
        + [jax.ShapeDtypeStruct(s.shape, BF16) for s in chip_sums]
        + [jax.ShapeDtypeStruct((4,) + g_wout.shape[1:], BF16)],
        scratch_shapes=[full(IN_COLS), full(MIX_WIDTH), full(LRU_WIDTH), full(LRU_WIDTH), full(LRU_WIDTH),
                        pltpu.VMEM((LRU_WIDTH, GROUP), BF16), pltpu.VMEM((LRU_WIDTH, GROUP), BF16),
                        pltpu.VMEM((N_ACC, SUB, LRU_WIDTH), F32),
                        pltpu.VMEM((LRU_WIDTH, GROUP), F32), pltpu.VMEM((LRU_WIDTH, GROUP), F32),
                        pltpu.VMEM((SUB, LRU_WIDTH), F32), pltpu.VMEM((1, LRU_WIDTH), F32),
                        pltpu.VMEM((SUB, CONV_WIDTH), F32), pltpu.VMEM((SUB, LRU_WIDTH), F32)]
        + _exchange_scratch(n_sums, 3) + _exchange_scratch(1, 4),
        compiler_params=_params(("arbitrary",), 56), name="mixer_bwd",
    )(u, hs, hs, dx1, *saved, *smalls, *chip_sums, g_wout)


def _in_proj_bwd(du, dx1, x, g_mix, win_t, tm, chip_sums, g_own):
    t_len = x.shape[0]
    n_steps = t_len // tm

    def body(du_ref, dx1_ref, x_ref, g_ref, w_ref, hs_ref, gown_ref,
             dx_ref, vec_ref, landed_ref, sib_ref, i_send, i_recv, d_send, d_recv):
        step = pl.program_id(0)
        _host_chip_exchange(step, n_steps, [hs_ref], [landed_ref], i_send, i_recv)
        _host_half_exchange(step, n_steps, gown_ref, sib_ref, d_send, d_recv)

        @pl.when(step == 0)
        def _():
            vec_ref[...] = jnp.zeros(vec_ref.shape, F32)

        dh = jnp.dot(du_ref[...], w_ref[...], preferred_element_type=F32)
        xv = x_ref[...]
        r1 = _rms(xv)
        xh = xv * r1
        vec_ref[0:1, :] += jnp.sum(dh * xh, axis=0, keepdims=True)
        dx_ref[...] = dx1_ref[...] + _rms_bwd(dh, xh, r1, g_ref[...])

    row_tile = lambda w: pl.BlockSpec((tm, w), lambda i: (i, 0))
    half_shape = (g_own.shape[0], g_own.shape[1] // 2, g_own.shape[2])
    return pl.pallas_call(
        body, grid=(n_steps,),
        in_specs=[row_tile(IN_COLS), row_tile(D_MODEL), row_tile(D_MODEL), pl.BlockSpec((1, D_MODEL), lambda i: (0, 0)),
                  pl.BlockSpec((IN_COLS, D_MODEL), lambda i: (0, 0))] + [HBM_SPEC] * 2,
        out_specs=[row_tile(D_MODEL), pl.BlockSpec((SUB, D_MODEL), lambda i: (0, 0))] + [HBM_SPEC] * 2,
        out_shape=[jax.ShapeDtypeStruct((t_len, D_MODEL), F32), jax.ShapeDtypeStruct((SUB, D_MODEL), F32),
                   jax.ShapeDtypeStruct(chip_sums.shape, BF16), jax.ShapeDtypeStruct(half_shape, BF16)],
        scratch_shapes=_exchange_scratch(1, 3) + [pltpu.SemaphoreType.DMA((1,)), pltpu.SemaphoreType.DMA((1,))],
        compiler_params=_params(("arbitrary",), 56), name="in_proj_bwd",
    )(du, dx1, x, g_mix, win_t, chip_sums, g_own)


def _tn_weight_grad(a, b, tk, name, pair=(), col_blocks=1):
    t_len, m = a.shape
    n = b.shape[1]
    n_steps = t_len // tk
    sent = tuple(pair)
    n_sent = len(sent)

    def body(a_ref, b_ref, *rest):
        srcs = rest[0:n_sent]
        o_ref = rest[n_sent]
        dsts = rest[n_sent + 1:2 * n_sent + 1]
        acc = rest[2 * n_sent + 1]
        sems = rest[2 * n_sent + 2:]
        j = pl.program_id(0)
        if pair:
            _host_pair_exchange(j, n_steps, srcs, dsts, *sems)

        @pl.when(j == 0)
        def _():
            acc[...] = jnp.zeros(acc.shape, F32)

        acc[...] += _dot_tn(a_ref[...].astype(BF16), b_ref[...].astype(BF16))

        @pl.when(j == n_steps - 1)
        def _():
            if col_blocks == 1:
                o_ref[...] = acc[...].astype(BF16)
            else:
                for k in range(col_blocks):
                    o_ref[k] = acc[:, k * nb:(k + 1) * nb].astype(BF16)

    nb = n // col_blocks
    out_dims = (m, n) if col_blocks == 1 else (col_blocks, m, nb)
    landed = [jax.ShapeDtypeStruct((4,) + g.shape[1:], BF16) for g in pair]
    scratch = [pltpu.VMEM((m, n), F32)]
    if n_sent:
        scratch += _exchange_scratch(n_sent, 4)
    return pl.pallas_call(
        body, grid=(n_steps,),
        in_specs=[pl.BlockSpec((tk, m), lambda j: (j, 0)), pl.BlockSpec((tk, n), lambda j: (j, 0))]
        + [HBM_SPEC] * n_sent,
        out_specs=[pl.BlockSpec(out_dims, lambda j: (0,) * len(out_dims))] + [HBM_SPEC] * n_sent,
        out_shape=[jax.ShapeDtypeStruct(out_dims, BF16)] + landed,
        scratch_shapes=scratch,
        compiler_params=_params(("arbitrary",), 56), name=name,
    )(a, b, *sent)


def _w_in_grad_part(du, h, tk, name, chip_ids, chip=(), halves=None, small=None):
    t_len = du.shape[0]
    n_t = t_len // tk
    n_q = chip_ids.shape[0]
    width = 2 * (IN_COLS // N_DEV)
    n_steps = n_q * n_t
    n_chip = len(chip)
    sent = tuple(chip) + (() if halves is None else (halves,)) + (() if small is None else tuple(small))
    n_sent = len(sent)

    def body(ids_ref, a_ref, b_ref, *rest):
        srcs = rest[0:n_sent]
        o_ref = rest[n_sent]
        dsts = rest[n_sent + 1:2 * n_sent + 1]
        acc = rest[2 * n_sent + 1]
        sems = list(rest[2 * n_sent + 2:])
        j = pl.program_id(1)
        step = pl.program_id(0) * n_t + j
        if chip:
            _host_chip_exchange(step, n_steps, srcs[0:n_chip], dsts[0:n_chip], sems.pop(0), sems.pop(0))
        if halves is not None:
            _host_half_exchange(step, n_steps, srcs[n_chip], dsts[n_chip], sems.pop(0), sems.pop(0))
        if small is not None:
            _host_small_exchange(step, n_steps, *srcs[n_sent - 3:], *dsts[n_sent - 3:], *sems)

        @pl.when(j == 0)
        def _():
            acc[...] = jnp.zeros(acc.shape, F32)

        acc[...] += _dot_tn(a_ref[...], b_ref[...])

        @pl.when(j == n_t - 1)
        def _():
            o_ref[0] = acc[...].astype(BF16)

    landed = [jax.ShapeDtypeStruct(s.shape, BF16) for s in chip]
    scratch = [pltpu.VMEM((width, D_MODEL), F32)]
    if chip:
        scratch += _exchange_scratch(len(chip), 3)
    if halves is not None:
        landed.append(jax.ShapeDtypeStruct((halves.shape[0], halves.shape[1] // 2, halves.shape[2]), BF16))
        scratch += [pltpu.SemaphoreType.DMA((halves.shape[0],)), pltpu.SemaphoreType.DMA((halves.shape[0],))]
    if small is not None:
        vec_m, vec_b, wab = small
        landed += [jax.ShapeDtypeStruct((N_DEV,) + vec_m.shape, F32), jax.ShapeDtypeStruct((N_DEV,) + vec_b.shape, F32),
                   jax.ShapeDtypeStruct((N_DEV, wab.shape[0] // N_DEV, wab.shape[1]), F32)]
        scratch += _exchange_scratch(3, N_DEV) + [pltpu.SemaphoreType.DMA((2,))]
    grid_spec = pltpu.PrefetchScalarGridSpec(
        num_scalar_prefetch=1, grid=(n_q, n_t),
        in_specs=[pl.BlockSpec((tk, width), lambda q, j, ids: (j, ids[q])),
                  pl.BlockSpec((tk, D_MODEL), lambda q, j, ids: (j, 0))] + [HBM_SPEC] * n_sent,
        out_specs=[pl.BlockSpec((1, width, D_MODEL), lambda q, j, ids: (q, 0, 0))] + [HBM_SPEC] * n_sent,
        scratch_shapes=scratch)
    return pl.pallas_call(
        body, grid_spec=grid_spec, out_shape=[jax.ShapeDtypeStruct((n_q, width, D_MODEL), BF16)] + landed,
        compiler_params=_params(("arbitrary", "arbitrary"), 40), name=name,
    )(chip_ids, du, h, *sent)


def _adamw(w, g, m, v):
    m = ADAM_B1 * m + (1.0 - ADAM_B1) * g
    v = ADAM_B2 * v + (1.0 - ADAM_B2) * (g * g)
    delta = -ADAM_LR * ((m / BC1) / (jnp.sqrt(v / BC2) + ADAM_EPS) + ADAM_WD * w)
    return delta, m, v


def _update_sharded(g, landed, w, m, v, rows_blk, name):
    rows, cols = w.shape

    def body(g_ref, l_ref, w_ref, m_ref, v_ref, og, od, om, ov):
        gv = g_ref[...]
        for j in range(3):
            gv = gv + l_ref[j].astype(F32)
        delta, mn, vn = _adamw(w_ref[...], gv, m_ref[...], v_ref[...])
        og[...] = gv
        od[...] = delta
        om[...] = mn
        ov[...] = vn

    blk = pl.BlockSpec((rows_blk, cols), lambda i: (i, 0))
    shape = pltpu.HBM((rows, cols), F32)
    return pl.pallas_call(
        body, grid=(rows // rows_blk,),
        in_specs=[blk, pl.BlockSpec((3, rows_blk, cols), lambda i: (0, i, 0)), blk, blk, blk],
        out_specs=[blk] * 4, out_shape=[shape] * 4,
        compiler_params=_params(("arbitrary",), 32), name=name,
    )(*_in_hbm(g, landed, w, m, v))


def _update_w_in(g_own, sib_own, landed, w_t, m_t, v_t, core, cols_blk):
    rows, cols = w_t.shape

    def body(core_ref, g_ref, s_ref, l_ref, w_ref, m_ref, v_ref, og, od, om, ov):
        gv = g_ref[0, 0].astype(F32) + s_ref[0].astype(F32)
        for j in range(3):
            gv = gv + l_ref[j].astype(F32)
        delta, mn, vn = _adamw(w_ref[...], gv, m_ref[...], v_ref[...])
        og[...] = gv
        od[...] = delta
        om[...] = mn
        ov[...] = vn

    blk = pl.BlockSpec((rows, cols_blk), lambda i, cr: (0, i))
    grid_spec = pltpu.PrefetchScalarGridSpec(
        num_scalar_prefetch=1, grid=(cols // cols_blk,),
        in_specs=[pl.BlockSpec((1, 1, rows, cols_blk), lambda i, cr: (0, cr[0], 0, i)),
                  pl.BlockSpec((1, rows, cols_blk), lambda i, cr: (0, 0, i)),
                  pl.BlockSpec((3, rows, cols_blk), lambda i, cr: (0, 0, i)), blk, blk, blk],
        out_specs=[blk] * 4)
    return pl.pallas_call(
        body, grid_spec=grid_spec, out_shape=[pltpu.HBM((rows, cols), F32)] * 4,
        compiler_params=_params(("arbitrary",), 32), name="update_w_in",
    )(core, *_in_hbm(g_own.reshape(1, 2, rows, cols), sib_own, landed, w_t, m_t, v_t))


def _update_small(vsum, wsum, g_cw, g_rw, weights, moments_m, moments_v):
    n = len(weights)

    def body(*refs):
        vs, ws, gcw, grw = refs[0:4]
        w_refs = refs[4:4 + n]
        m_refs = refs[4 + n:4 + 2 * n]
        v_refs = refs[4 + 2 * n:4 + 3 * n]
        outs = refs[4 + 3 * n:]
        loss_ref = outs[0]
        loss_ref[...] = jnp.sum(vs[ROW_LOSS:ROW_LOSS + 1, :], axis=1, keepdims=True)
        grads = [
            vs[ROW_GMIX:ROW_GMIX + 1, :], gcw[...], grw[...], vs[ROW_BR:ROW_BR + 1, :],
            ws[0:LRU_WIDTH, :], vs[ROW_BA:ROW_BA + 1, :], ws[LRU_WIDTH:2 * LRU_WIDTH, :], vs[ROW_BX:ROW_BX + 1, :],
            vs[ROW_LAM:ROW_LAM + 1, :], vs[ROW_GNC:ROW_GNC + 1, 0:CONV_WIDTH], vs[ROW_GNR:ROW_GNR + 1, :],
            vs[ROW_GMLP:ROW_GMLP + 1, :], vs[ROW_GF:ROW_GF + 1, :],
        ]
        for k in range(n):
            gk = grads[k]
            delta, mn, vn = _adamw(w_refs[k][...], gk, m_refs[k][...], v_refs[k][...])
            outs[1 + 4 * k][...] = gk
            outs[2 + 4 * k][...] = delta
            outs[3 + 4 * k][...] = mn
            outs[4 + 4 * k][...] = vn

    whole = lambda a: pl.BlockSpec(a.shape, lambda i: (0,) * len(a.shape))
    out_shape = [jax.ShapeDtypeStruct((1, 1), F32)]
    for w in weights:
        out_shape += [jax.ShapeDtypeStruct(w.shape, F32)] * 4
    args = (vsum, wsum, g_cw, g_rw, *weights, *moments_m, *moments_v)
    return pl.pallas_call(
        body, grid=(1,), out_shape=out_shape, in_specs=[whole(a) for a in args], out_specs=[whole(s) for s in out_shape],
        compiler_params=_params(("arbitrary",), 32), name="update_small",
    )(*args)


def kernel(x, norm_mix_g, w_in, conv_w, rnn_conv_w, rnn_conv_b, w_a, b_a, w_x, b_x, lru_lambda, g_norm_conv, g_norm_rnn, w_out, norm_mlp_g, w_mlp_in, w_mlp_out, final_norm_g, loss_target, m_norm_mix_g, m_w_in, m_conv_w, m_rnn_conv_w, m_rnn_conv_b, m_w_a, m_b_a, m_w_x, m_b_x, m_lru_lambda, m_g_norm_conv, m_g_norm_rnn, m_w_out, m_norm_mlp_g, m_w_mlp_in, m_w_mlp_out, m_final_norm_g, v_norm_mix_g, v_w_in, v_conv_w, v_rnn_conv_w, v_rnn_conv_b, v_w_a, v_b_a, v_w_x, v_b_x, v_lru_lambda, v_g_norm_conv, v_g_norm_rnn, v_w_out, v_norm_mlp_g, v_w_mlp_in, v_w_mlp_out, v_final_norm_g):
    t_len = x.shape[1]
    my_id = 4 * lax.axis_index("x") + 2 * lax.axis_index("y") + lax.axis_index("c")
    tm = min(256, t_len)
    tb = min(512, t_len)
    tk = min(512, t_len)

    xs = x.reshape(t_len, D_MODEL)
    tgt = loss_target.reshape(t_len, D_MODEL)
    flat = lambda a: a.reshape(a.shape[-2:]) if a.ndim == 3 else a.reshape(1, -1)
    heads = lambda a: a.reshape(LRU_WIDTH, HEAD_DIM)

    turned = lambda a: jnp.transpose(flat(a))
    win_shard, wout_shard, w1_shard, w2_shard, cp_shard = _prep_shards(
        turned(w_in), flat(w_out), flat(w_mlp_in), flat(w_mlp_out), flat(conv_w), flat(rnn_conv_w))

    u, h, win_t, cp_full = _in_proj(xs, flat(norm_mix_g), (win_shard, cp_shard), min(1024, t_len))
    cpack = cp_full.reshape(N_DEV, 8, 128)
    conv_full = jnp.transpose(cpack[:, 0:3, 0:64], (1, 0, 2)).reshape(3, CONV_WIDTH)
    rnn_full = jnp.transpose(cpack[:, 3:7, :], (1, 0, 2)).reshape(4, LRU_WIDTH)
    mixer_small = (conv_full, rnn_full, flat(rnn_conv_b), heads(w_a), flat(b_a), heads(w_x), flat(b_x),
                   flat(lru_lambda), flat(g_norm_conv), flat(g_norm_rnn))
    hs, y, xr, gate_r, gate_i, mult, cq, w1_blk, wout_blk = _mixer_fwd(u, *mixer_small, (w1_shard, wout_shard), tm)
    wout_f = wout_blk.reshape(MIX_WIDTH, D_MODEL)
    x1, h2, z, w2_blk = _mlp_up(xs, y, flat(norm_mlp_g), wout_f, w1_blk, w2_shard, tb)
    dx1, dx2, vec_m, dpre = _mlp_down_bwd(x1, z, tgt, flat(norm_mlp_g), flat(final_norm_g), w1_blk,
                                          w2_blk.reshape(D_FF, D_MODEL), tb)
    (g_w1,) = _tn_weight_grad(h2, dpre, tk, "w_mlp_in_grad", col_blocks=N_DEV)
    (g_w2,) = _tn_weight_grad(z, dx2, tk, "w_mlp_out_grad")
    g_w2 = g_w2.reshape(N_DEV, D_FF // N_DEV, D_MODEL)
    g_wout, sib_w1, sib_w2 = _tn_weight_grad(y, dx1, tk, "w_out_grad", pair=(g_w1, g_w2))
    g_wout = g_wout.reshape(N_DEV, MIX_WIDTH // N_DEV, D_MODEL)
    hsend_w1, own_w1, hsend_w2, own_w2 = _pair_sum((g_w1, g_w2), (sib_w1, sib_w2), "pair_sum_w_mlp")
    du, vec_b, wab, landed_w1, landed_w2, sib_wout = _mixer_bwd(
        u, hs, dx1, (xr, gate_r, gate_i, mult, cq), conv_full, rnn_full, heads(w_a), heads(w_x),
        flat(lru_lambda), flat(g_norm_conv), flat(g_norm_rnn), wout_f, (hsend_w1, hsend_w2), g_wout, tm)
    hsend_wout, own_wout = _pair_sum((g_wout,), (sib_wout,), "pair_sum_w_out")
    ax, ay, ac = lax.axis_index("x"), lax.axis_index("y"), lax.axis_index("c")
    chip_ids = jnp.stack([2 * cx + cy for cx, cy in [(ax, ay)] + _other_chips(ax, ay)]).astype(jnp.int32)
    core = jnp.reshape(ac, (1,)).astype(jnp.int32)
    tw = min(1024, t_len)
    g_others, landed_wout, vrecv_m, vrecv_b, wrecv = _w_in_grad_part(
        du, h, tw, "w_in_grad_others", chip_ids[1:4], chip=(hsend_wout,), small=(vec_m, vec_b, wab))
    g_own, sib_others = _w_in_grad_part(du, h, tw, "w_in_grad_own", chip_ids[0:1], halves=g_others)
    hsend_win = _pair_sum_parts(g_others, sib_others, core)
    grad_x, vec_x, landed_win, sib_own = _in_proj_bwd(du, dx1, xs, flat(norm_mix_g), win_t, tm, hsend_win, g_own)

    vsum, wsum = _final_small(vrecv_m, vrecv_b, wab, wrecv, vec_x)

    up_win = _update_w_in(g_own, sib_own, landed_win, turned(w_in), turned(m_w_in), turned(v_w_in), core, 256)
    up_win = [jnp.transpose(a) for a in up_win]
    up_wout = _update_sharded(own_wout, landed_wout, flat(w_out), flat(m_w_out), flat(v_w_out), 96, "update_w_out")
    up_w1 = _update_sharded(own_w1, landed_w1, flat(w_mlp_in), flat(m_w_mlp_in), flat(v_w_mlp_in), 256,
                            "update_w_mlp_in")
    up_w2 = _update_sharded(own_w2, landed_w2, flat(w_mlp_out), flat(m_w_mlp_out), flat(v_w_mlp_out), 256,
                            "update_w_mlp_out")

    g_cw = lax.dynamic_slice(vsum, (ROW_CW, 64 * my_id), (3, 64))
    g_rw = lax.dynamic_slice(vsum, (ROW_RW, 128 * my_id), (4, 128))
    small_w = (norm_mix_g, conv_w, rnn_conv_w, rnn_conv_b, w_a, b_a, w_x, b_x, lru_lambda, g_norm_conv, g_norm_rnn,
               norm_mlp_g, final_norm_g)
    small_m = (m_norm_mix_g, m_conv_w, m_rnn_conv_w, m_rnn_conv_b, m_w_a, m_b_a, m_w_x, m_b_x, m_lru_lambda,
               m_g_norm_conv, m_g_norm_rnn, m_norm_mlp_g, m_final_norm_g)
    small_v = (v_norm_mix_g, v_conv_w, v_rnn_conv_w, v_rnn_conv_b, v_w_a, v_b_a, v_w_x, v_b_x, v_lru_lambda,
               v_g_norm_conv, v_g_norm_rnn, v_norm_mlp_g, v_final_norm_g)
    is_heads = (False, False, False, False, True, False, True, False, False, False, False, False, False)
    as2d = lambda arrs: [heads(a) if hd else flat(a) for a, hd in zip(arrs, is_heads)]
    small_out = _update_small(vsum, wsum, g_cw, g_rw, as2d(small_w), as2d(small_m), as2d(small_v))
    loss = small_out[0].reshape(())

    names = ["norm_mix_g", "w_in", "conv_w", "rnn_conv_w", "rnn_conv_b", "w_a", "b_a", "w_x", "b_x", "lru_lambda",
             "g_norm_conv", "g_norm_rnn", "w_out", "norm_mlp_g", "w_mlp_in", "w_mlp_out", "final_norm_g"]
    originals = dict(zip(names, (norm_mix_g, w_in, conv_w, rnn_conv_w, rnn_conv_b, w_a, b_a, w_x, b_x, lru_lambda,
                                 g_norm_conv, g_norm_rnn, w_out, norm_mlp_g, w_mlp_in, w_mlp_out, final_norm_g)))
    results = {"w_in": up_win, "w_out": up_wout, "w_mlp_in": up_w1, "w_mlp_out": up_w2}
    small_names = ["norm_mix_g", "conv_w", "rnn_conv_w", "rnn_conv_b", "w_a", "b_a", "w_x", "b_x", "lru_lambda",
                   "g_norm_conv", "g_norm_rnn", "norm_mlp_g", "final_norm_g"]
    for k, nm in enumerate(small_names):
        results[nm] = small_out[1 + 4 * k:5 + 4 * k]
    out = [loss, grad_x.reshape(x.shape)]
    for kind in range(4):
        out += [results[nm][kind].reshape(originals[nm].shape) for nm in names]
    return tuple(out)
```

```python
import functools

import jax
import jax.numpy as jnp
from jax import lax
from jax.experimental import pallas as pl
from jax.experimental.pallas import tpu as pltpu

F32 = jnp.float32
BF16 = jnp.bfloat16

D_MODEL = 1024
HEAD_DIM = 64
CONV_WIDTH = 512
LRU_WIDTH = 1024
MIX_WIDTH = CONV_WIDTH + LRU_WIDTH
IN_COLS = 3 * CONV_WIDTH + 2 * LRU_WIDTH
D_FF = 4 * D_MODEL
GROUP = 256
EPS = 1e-6
LRU_C = 8.0
N_DEV = 8
SUB = 8

OFF_GB, OFF_GC, OFF_V, OFF_XR, OFF_G = 0, 512, 1024, 1536, 2560

ADAM_LR, ADAM_B1, ADAM_B2, ADAM_EPS, ADAM_WD, ADAM_STEP = 0.001, 0.9, 0.999, 1e-08, 0.01, 10
BC1 = 1.0 - ADAM_B1 ** ADAM_STEP
BC2 = 1.0 - ADAM_B2 ** ADAM_STEP

MIB = 1024 * 1024
MESH = pl.DeviceIdType.MESH

VEC_ROWS = 32
ROW_GF, ROW_GMLP, ROW_LOSS = 0, 1, 2
ROW_GNC, ROW_GNR, ROW_BR, ROW_BA, ROW_BX, ROW_LAM, ROW_CW, ROW_RW = 8, 9, 10, 11, 12, 13, 14, 17
ROW_GMIX = 24
ACC_GNC, ACC_GNR, ACC_BR, ACC_BA, ACC_BX, ACC_SP, ACC_CW, ACC_RW, N_ACC = 0, 1, 2, 3, 4, 5, 6, 9, 13


def _params(semantics=None, vmem_mib=48):
    return pltpu.CompilerParams(dimension_semantics=semantics, vmem_limit_bytes=vmem_mib * MIB)


def _rms(x):
    return lax.rsqrt(jnp.mean(x * x, axis=-1, keepdims=True) + EPS)


def _rms_bwd(dy, xhat, r, g):
    dyh = dy * g
    return r * (dyh - xhat * jnp.mean(dyh * xhat, axis=-1, keepdims=True))


def _sigmoid(x):
    return 0.5 + 0.5 * jnp.tanh(0.5 * x)


def _gelu(x):
    c0, c1 = 0.7978845608028654, 0.044715
    x2 = x * x
    t = jnp.tanh(x * (c0 + (c0 * c1) * x2))
    half = 0.5 + 0.5 * t
    ge = x * half
    dge = half + (ge - ge * half) * (2.0 * c0 + (6.0 * c0 * c1) * x2)
    return ge, dge


def _softplus_neg(lam):
    z = -lam
    e = jnp.exp(-jnp.abs(z))
    return jnp.maximum(z, 0.0) + jnp.where(e < 1e-4, e * (1.0 - 0.5 * e), jnp.log(1.0 + e))


def _lru_gates(pa, px, sp_c):
    ra = _sigmoid(pa)
    ii = _sigmoid(px)
    neg_la = ra * sp_c
    a = jnp.exp(-neg_la)
    m2 = jnp.tanh(neg_la) * (1.0 + a * a)
    mult = jnp.where(m2 > 0.0, m2 * lax.rsqrt(m2), 0.0)
    return ra, ii, a, mult


def _down(cur, prev, s, row):
    return pltpu.roll(jnp.where(row < SUB - s, cur, prev), s, 0)


def _up(cur, nxt, s, row):
    return pltpu.roll(jnp.where(row >= s, cur, nxt), SUB - s, 0)


def _scan8_fwd(a, b, row):
    for s in (1, 2, 4):
        m = row >= s
        a_sh = pltpu.roll(a, s, 0)
        b_sh = pltpu.roll(b, s, 0)
        b = jnp.where(m, a * b_sh + b, b)
        a = jnp.where(m, a * a_sh, a)
    return a, b


def _scan8_rev(a, b, row):
    for s in (1, 2, 4):
        m = row < SUB - s
        a_sh = pltpu.roll(a, SUB - s, 0)
        b_sh = pltpu.roll(b, SUB - s, 0)
        b = jnp.where(m, a * b_sh + b, b)
        a = jnp.where(m, a * a_sh, a)
    return a, b


def _group_mask(shape):
    r = lax.broadcasted_iota(jnp.int32, shape, 0)
    c = lax.broadcasted_iota(jnp.int32, shape, 1)
    return ((r % GROUP) // HEAD_DIM) == (c // HEAD_DIM)


def _expand_heads(w):
    j = lax.broadcasted_iota(jnp.int32, (HEAD_DIM, GROUP), 0)
    c = lax.broadcasted_iota(jnp.int32, (HEAD_DIM, GROUP), 1)
    spread = (c % HEAD_DIM == j).astype(BF16)
    e = jnp.dot(w.astype(BF16), spread, preferred_element_type=F32)
    return jnp.where(_group_mask(e.shape), e, 0.0).astype(BF16)


def _fold_heads(p):
    p = jnp.where(_group_mask(p.shape), p, 0.0)
    c = lax.broadcasted_iota(jnp.int32, (GROUP, HEAD_DIM), 0)
    j = lax.broadcasted_iota(jnp.int32, (GROUP, HEAD_DIM), 1)
    fold = (c % HEAD_DIM == j).astype(BF16)
    hi = p.astype(BF16)
    rest = p - hi.astype(F32)
    mid = rest.astype(BF16)
    lo = (rest - mid.astype(F32)).astype(BF16)
    dot = functools.partial(jnp.dot, preferred_element_type=F32)
    return dot(hi, fold) + dot(mid, fold) + dot(lo, fold)


def _block_diag_apply(xb, wbd_ref):
    parts = [jnp.dot(xb[:, g * GROUP:(g + 1) * GROUP], wbd_ref[g * GROUP:(g + 1) * GROUP, :],
                     preferred_element_type=F32) for g in range(LRU_WIDTH // GROUP)]
    return jnp.concatenate(parts, axis=1)


def _block_diag_apply_t(db, wbd_ref):
    parts = [lax.dot_general(db[:, g * GROUP:(g + 1) * GROUP], wbd_ref[g * GROUP:(g + 1) * GROUP, :],
                             (((1,), (1,)), ((), ())), preferred_element_type=F32)
             for g in range(LRU_WIDTH // GROUP)]
    return jnp.concatenate(parts, axis=1)


def _dot_nt(a, b):
    return lax.dot_general(a, b, (((1,), (1,)), ((), ())), preferred_element_type=F32)


def _dot_tn(a, b):
    return lax.dot_general(a, b, (((0,), (0,)), ((), ())), preferred_element_type=F32)


CHUNKS_IN_FLIGHT = 8


def _chunk_loop(n_chunks, chunk, init):
    def body(k, carry):
        for j in range(CHUNKS_IN_FLIGHT):
            carry = chunk(k * CHUNKS_IN_FLIGHT + j, carry)
        return carry

    return lax.fori_loop(0, n_chunks // CHUNKS_IN_FLIGHT, body, init)


def _place():
    x, y, c = lax.axis_index("x"), lax.axis_index("y"), lax.axis_index("c")
    return x, y, c


def _block_id(chip, core):
    return 4 * chip[0] + 2 * chip[1] + core


def _other_chips(x, y):
    return [(1 - x, y), (x, 1 - y), (1 - x, 1 - y)]


def _remote_copy(src, dst, send_sem, recv_sem, to):
    return pltpu.make_async_remote_copy(src_ref=src, dst_ref=dst, send_sem=send_sem, recv_sem=recv_sem,
                                        device_id=to, device_id_type=MESH)


HBM_SPEC = pl.BlockSpec(memory_space=pl.ANY)


def _in_hbm(*arrays):
    return [pltpu.with_memory_space_constraint(a, pltpu.HBM) for a in arrays]


def _prep_shards(w_in_t, w_out, w_mlp_in, w_mlp_out, conv_w, rnn_conv_w):
    def body(win_ref, wout_ref, w1_ref, w2_ref, cw_ref, rw_ref, o_win, o_wout, o_w1, o_w2, o_cp):
        o_win[...] = win_ref[...].astype(BF16)
        o_wout[...] = wout_ref[...].astype(BF16)
        o_w1[...] = w1_ref[...].astype(BF16)
        o_w2[...] = w2_ref[...].astype(BF16)
        o_cp[...] = jnp.zeros(o_cp.shape, F32)
        o_cp[0:3, 0:64] = cw_ref[...]
        o_cp[3:7, :] = rw_ref[...]

    whole = lambda shape: pl.BlockSpec(shape, lambda i: (0,) * len(shape))
    args = (w_in_t, w_out, w_mlp_in, w_mlp_out, conv_w, rnn_conv_w)
    shapes = [(w_in_t.shape, BF16), (w_out.shape, BF16), (w_mlp_in.shape, BF16), (w_mlp_out.shape, BF16),
              ((8, 128), F32)]
    return pl.pallas_call(
        body, grid=(1,), out_shape=[jax.ShapeDtypeStruct(s, d) for s, d in shapes],
        in_specs=[whole(a.shape) for a in args], out_specs=[whole(s) for s, _ in shapes],
        compiler_params=_params(("arbitrary",), 40), name="prep_shards",
    )(*args)


def _host_all_gather(step, n_steps, shards, fulls, send_sems, recv_sems, local_sems):
    x, y, c = _place()
    me = (x, y, c)
    my_id = _block_id((x, y), c)
    sibling = (x, y, 1 - c)
    chips = _other_chips(x, y)
    n_arr = len(shards)

    def copy(arr, k, block, to, src=None):
        dst = fulls[arr].at[block]
        return _remote_copy(dst if src is None else src, dst, send_sems.at[arr, k], recv_sems.at[arr, k], to)

    def local(arr):
        return pltpu.make_async_copy(shards[arr], fulls[arr].at[my_id], local_sems.at[arr])

    @pl.when(step == 0)
    def _():
        for arr in range(n_arr):
            local(arr).start()
            copy(arr, 0, my_id, sibling, shards[arr]).start()
            for j, chip in enumerate(chips):
                copy(arr, 1 + j, my_id, (*chip, c), shards[arr]).start()

    @pl.when(step == max(n_steps - 2, 0))
    def _():
        for j, chip in enumerate(chips):
            for arr in range(n_arr):
                copy(arr, 1 + j, _block_id(chip, c), me).wait_recv()
                copy(arr, 4 + j, _block_id(chip, c), sibling).start()

    @pl.when(step == n_steps - 1)
    def _():
        for arr in range(n_arr):
            copy(arr, 0, _block_id((x, y), 1 - c), me).wait_recv()
            for j, chip in enumerate(chips):
                copy(arr, 4 + j, _block_id(chip, 1 - c), me).wait_recv()
            for k in range(4):
                copy(arr, k, my_id, me, shards[arr]).wait_send()
            for j, chip in enumerate(chips):
                copy(arr, 4 + j, _block_id(chip, c), me).wait_send()
            local(arr).wait()


def _host_pair_exchange(step, n_steps, gs, sibs, send_sems, recv_sems):
    x, y, c = _place()
    sibling = (x, y, 1 - c)
    chips = [(x, y)] + _other_chips(x, y)

    def d2d(arr, q):
        return _remote_copy(gs[arr].at[_block_id(chips[q], 1 - c)], sibs[arr].at[q],
                            send_sems.at[arr, q], recv_sems.at[arr, q], sibling)

    @pl.when(step == 0)
    def _():
        for arr in range(len(gs)):
            for q in (1, 2, 3, 0):
                d2d(arr, q).start()

    @pl.when(step == n_steps - 1)
    def _():
        for arr in range(len(gs)):
            for q in range(4):
                d2d(arr, q).wait()


def _host_chip_exchange(step, n_steps, hsends, hrecvs, send_sems, recv_sems):
    x, y, c = _place()
    chips = _other_chips(x, y)

    def ici(arr, j):
        return _remote_copy(hsends[arr].at[j], hrecvs[arr].at[j], send_sems.at[arr, j], recv_sems.at[arr, j],
                            (*chips[j], c))

    @pl.when(step == 0)
    def _():
        for arr in range(len(hsends)):
            for j in range(3):
                ici(arr, j).start()

    @pl.when(step == n_steps - 1)
    def _():
        for arr in range(len(hsends)):
            for j in range(3):
                ici(arr, j).wait()


def _host_half_exchange(step, n_steps, parts, sibs, send_sems, recv_sems):
    x, y, c = _place()
    n_q, rows2, _ = parts.shape
    half = rows2 // 2

    def d2d(q):
        src = parts.at[q, pl.ds(pl.multiple_of((1 - c) * half, 16), half), :]
        return _remote_copy(src, sibs.at[q], send_sems.at[q], recv_sems.at[q], (x, y, 1 - c))

    @pl.when(step == 0)
    def _():
        for q in range(n_q):
            d2d(q).start()

    @pl.when(step == n_steps - 1)
    def _():
        for q in range(n_q):
            d2d(q).wait()


def _peer(x, y, c, k):
    return (x ^ ((k >> 2) & 1), y ^ ((k >> 1) & 1), c ^ (k & 1))


def _host_small_exchange(step, n_steps, vec_m, vec_b, wab, vrecv_m, vrecv_b, wrecv, send_sems, recv_sems, local_sems):
    x, y, c = _place()
    my_id = _block_id((x, y), c)
    wrows = wab.shape[0] // N_DEV

    def copies(k):
        to = _peer(x, y, c, k)
        block = wab.at[pl.ds(pl.multiple_of(_block_id(to[0:2], to[2]) * wrows, SUB), wrows), :]
        return [_remote_copy(vec_m, vrecv_m.at[my_id], send_sems.at[0, k], recv_sems.at[0, k], to),
                _remote_copy(vec_b, vrecv_b.at[my_id], send_sems.at[1, k], recv_sems.at[1, k], to),
                _remote_copy(block, wrecv.at[k], send_sems.at[2, k], recv_sems.at[2, k], to)]

    mine = [pltpu.make_async_copy(vec_m, vrecv_m.at[my_id], local_sems.at[0]),
            pltpu.make_async_copy(vec_b, vrecv_b.at[my_id], local_sems.at[1])]

    @pl.when(step == 0)
    def _():
        for cp in mine:
            cp.start()
        for k in range(1, N_DEV):
            for cp in copies(k):
                cp.start()

    @pl.when(step == n_steps - 1)
    def _():
        for k in range(1, N_DEV):
            for cp in copies(k):
                cp.wait()
        for cp in mine:
            cp.wait()


def _pair_sum_parts(parts, sibs, core):
    n_q, rows2, cols = parts.shape
    half = rows2 // 2

    def body(core_ref, g_ref, s_ref, o_ref):
        o_ref[0] = (g_ref[0, 0].astype(F32) + s_ref[0].astype(F32)).astype(BF16)

    block = (1, half, cols)
    grid_spec = pltpu.PrefetchScalarGridSpec(
        num_scalar_prefetch=1, grid=(n_q,),
        in_specs=[pl.BlockSpec((1, 1, half, cols), lambda q, cr: (q, cr[0], 0, 0)),
                  pl.BlockSpec(block, lambda q, cr: (q, 0, 0))],
        out_specs=pl.BlockSpec(block, lambda q, cr: (q, 0, 0)))
    return pl.pallas_call(
        body, grid_spec=grid_spec, out_shape=pltpu.HBM((n_q, half, cols), BF16),
        compiler_params=_params(("arbitrary",), 32), name="pair_sum_w_in",
    )(core, *_in_hbm(parts.reshape(n_q, 2, half, cols), sibs))


def _pair_sum(gs, sibs, name):
    n_arr = len(gs)
    x, y, c = _place()
    slots = jnp.stack([_block_id(chip, c) for chip in [(x, y)] + _other_chips(x, y)]).astype(jnp.int32)

    def body(slots_ref, *refs):
        q = pl.program_id(0)
        for k in range(n_arr):
            g_ref, sib_ref = refs[2 * k:2 * k + 2]
            hs_ref, own_ref = refs[2 * n_arr + 2 * k:2 * n_arr + 2 * k + 2]
            both = g_ref[0].astype(F32) + sib_ref[0].astype(F32)

            @pl.when(q == 0)
            def _(own_ref=own_ref, both=both):
                own_ref[...] = both

            @pl.when(q > 0)
            def _(hs_ref=hs_ref, both=both):
                hs_ref[0] = both.astype(BF16)

    in_specs, out_specs, out_shape, args = [], [], [], []
    for g, sib in zip(gs, sibs):
        _, rows, cols = g.shape
        block = (1, rows, cols)
        in_specs += [pl.BlockSpec(block, lambda q, s: (s[q], 0, 0)), pl.BlockSpec(block, lambda q, s: (q, 0, 0))]
        out_specs += [pl.BlockSpec(block, lambda q, s: (jnp.maximum(q - 1, 0), 0, 0)),
                      pl.BlockSpec((rows, cols), lambda q, s: (0, 0))]
        out_shape += [pltpu.HBM((3, rows, cols), BF16), pltpu.HBM((rows, cols), F32)]
        args += _in_hbm(g, sib)
    grid_spec = pltpu.PrefetchScalarGridSpec(num_scalar_prefetch=1, grid=(4,), in_specs=in_specs, out_specs=out_specs)
    return pl.pallas_call(
        body, grid_spec=grid_spec, out_shape=out_shape,
        compiler_params=_params(("arbitrary",), 40), name=name,
    )(slots, *args)


def _exchange_scratch(n_arr, n_copies):
    return [pltpu.SemaphoreType.DMA((n_arr, n_copies)), pltpu.SemaphoreType.DMA((n_arr, n_copies))]


def _final_small(vrecv_m, vrecv_b, wab, wrecv, vec_x):
    wrows = wab.shape[0] // N_DEV

    def body(vm_ref, vb_ref, w_ref, wr_ref, vx_ref, o_vec, o_w, xrecv, wred, x_send, x_recv, b_send, b_recv):
        x, y, c = _place()
        my_id = _block_id((x, y), c)
        my_rows = pl.ds(pl.multiple_of(my_id * wrows, SUB), wrows)

        def xcopy(k):
            return _remote_copy(vx_ref, xrecv.at[my_id], x_send.at[k], x_recv.at[k], _peer(x, y, c, k))

        def bcopy(k):
            return _remote_copy(wred, o_w.at[my_rows, :], b_send.at[k], b_recv.at[k], _peer(x, y, c, k))

        xrecv[my_id] = vx_ref[...]
        for k in range(1, N_DEV):
            xcopy(k).start()
        red = w_ref[my_rows, :]
        for k in range(1, N_DEV):
            red = red + wr_ref[k]
        wred[...] = red
        o_w[my_rows, :] = red
        for k in range(1, N_DEV):
            bcopy(k).start()
        for k in range(1, N_DEV):
            xcopy(k).wait_recv()
        for rows, ref in ((slice(0, 8), vm_ref), (slice(8, 24), vb_ref), (slice(24, 32), xrecv)):
            tot = ref[0]
            for s in range(1, N_DEV):
                tot = tot + ref[s]
            o_vec[rows, :] = tot
        for k in range(1, N_DEV):
            bcopy(k).wait_recv()
        for k in range(1, N_DEV):
            xcopy(k).wait_send()
            bcopy(k).wait_send()

    vm = pl.BlockSpec(memory_space=pltpu.VMEM)
    dma8 = pltpu.SemaphoreType.DMA((N_DEV,))
    return pl.pallas_call(
        body, out_shape=(jax.ShapeDtypeStruct((VEC_ROWS, D_MODEL), F32), jax.ShapeDtypeStruct(wab.shape, F32)),
        in_specs=[vm] * 5, out_specs=[vm] * 2,
        scratch_shapes=[pltpu.VMEM((N_DEV, SUB, D_MODEL), F32), pltpu.VMEM((wrows, HEAD_DIM), F32),
                        dma8, dma8, dma8, dma8],
        compiler_params=_params(vmem_mib=32), name="final_small",
    )(vrecv_m, vrecv_b, wab, wrecv, vec_x)


def _in_proj(x, g_mix, shards, tm):
    t_len = x.shape[0]
    n_t = t_len // tm
    n_arr = len(shards)
    rows = [s.shape[0] for s in shards]
    width = 2 * rows[0]
    ax, ay = lax.axis_index("x"), lax.axis_index("y")
    order = jnp.stack([2 * cx + cy for cx, cy in [(ax, ay)] + _other_chips(ax, ay)]).astype(jnp.int32)

    def body(order_ref, x_ref, g_ref, *rest):
        shard_refs = rest[0:n_arr]
        u_ref, h_ref = rest[n_arr:n_arr + 2]
        fulls = rest[n_arr + 2:2 * n_arr + 2]
        h_s, wbuf, send_sems, recv_sems, local_sems, load_sem = rest[2 * n_arr + 2:]
        p = pl.program_id(0)
        i = pl.program_id(1)
        x_, y_, c = _place()
        me = (x_, y_, c)
        my_id = _block_id((x_, y_), c)
        sibling = (x_, y_, 1 - c)
        chips = _other_chips(x_, y_)

        def block(arr, blk):
            return fulls[arr].at[pl.ds(pl.multiple_of(blk * rows[arr], rows[arr]), rows[arr]), :]

        def copy(arr, k, blk, to, src=None):
            dst = block(arr, blk)
            return _remote_copy(dst if src is None else src, dst, send_sems.at[arr, k], recv_sems.at[arr, k], to)

        def local(arr):
            return pltpu.make_async_copy(shard_refs[arr], block(arr, my_id), local_sems.at[arr])

        def load_chip(chip, slot):
            start = pl.multiple_of((2 * chip[0] + chip[1]) * width, width)
            return pltpu.make_async_copy(fulls[0].at[pl.ds(start, width), :], wbuf.at[slot], load_sem.at[slot])

        def pass_on(j):
            for arr in range(n_arr):
                copy(arr, 1 + j, _block_id(chips[j], c), me).wait_recv()
                copy(arr, 4 + j, _block_id(chips[j], c), sibling).start()

        def complete(j):
            for arr in range(n_arr):
                copy(arr, 4 + j, _block_id(chips[j], 1 - c), me).wait_recv()

        @pl.when((p == 0) & (i == 0))
        def _():
            for arr in range(n_arr):
                local(arr).start()
                copy(arr, 0, my_id, sibling, shard_refs[arr]).start()
                for j in (0, 1):
                    copy(arr, 1 + j, my_id, (*chips[j], c), shard_refs[arr]).start()
            for arr in range(n_arr):
                local(arr).wait()
                copy(arr, 0, _block_id((x_, y_), 1 - c), me).wait_recv()
            load_chip((x_, y_), 0).start()
            load_chip((x_, y_), 0).wait()

        @pl.when((p == 1) & (i == 0))
        def _():
            pass_on(0)
            for arr in range(n_arr):
                copy(arr, 3, my_id, (*chips[2], c), shard_refs[arr]).start()
            pass_on(1)
            complete(0)
            load_chip(chips[0], 1).start()
            load_chip(chips[0], 1).wait()
            complete(1)
            load_chip(chips[1], 0).start()

        @pl.when((p == 2) & (i == 0))
        def _():
            load_chip(chips[1], 0).wait()

        @pl.when((p == 3) & (i == 0))
        def _():
            pass_on(2)
            complete(2)
            load_chip(chips[2], 1).start()
            load_chip(chips[2], 1).wait()

        @pl.when((p == 3) & (i == n_t - 1))
        def _():
            for arr in range(n_arr):
                for k in range(4):
                    copy(arr, k, my_id, me, shard_refs[arr]).wait_send()
                for j, chip in enumerate(chips):
                    copy(arr, 4 + j, _block_id(chip, c), me).wait_send()

        tile = pl.ds(pl.multiple_of(i * tm, tm), tm)

        @pl.when(p == 0)
        def _():
            xv = x_ref[...]
            h = (xv * _rms(xv) * g_ref[...]).astype(BF16)
            h_ref[...] = h
            h_s[tile, :] = h

        for slot in (0, 1):
            @pl.when(p % 2 == slot)
            def _(slot=slot):
                u_ref[...] = _dot_nt(h_s[tile, :], wbuf[slot])

    first_pass = lambda p, i, o: (jnp.where(p == 0, i, n_t - 1), 0)
    grid_spec = pltpu.PrefetchScalarGridSpec(
        num_scalar_prefetch=1, grid=(4, n_t),
        in_specs=[pl.BlockSpec((tm, D_MODEL), first_pass), pl.BlockSpec((1, D_MODEL), lambda p, i, o: (0, 0))]
        + [HBM_SPEC] * n_arr,
        out_specs=[pl.BlockSpec((tm, width), lambda p, i, o: (i, o[p])), pl.BlockSpec((tm, D_MODEL), first_pass)]
        + [HBM_SPEC] * n_arr,
        scratch_shapes=[pltpu.VMEM((t_len, D_MODEL), BF16), pltpu.VMEM((2, width, D_MODEL), BF16)]
        + _exchange_scratch(n_arr, 7) + [pltpu.SemaphoreType.DMA((n_arr,)), pltpu.SemaphoreType.DMA((2,))])
    return pl.pallas_call(
        body, grid_spec=grid_spec,
        out_shape=[jax.ShapeDtypeStruct((t_len, IN_COLS), F32), jax.ShapeDtypeStruct((t_len, D_MODEL), BF16)]
        + [jax.ShapeDtypeStruct((N_DEV * s.shape[0], s.shape[1]), s.dtype) for s in shards],
        compiler_params=_params(("arbitrary", "arbitrary"), 48), name="in_proj",
    )(order, x, g_mix, *shards)


def _conv3_chunk(u_ref, r, cv_prev, cw, row):
    gb = u_ref[pl.ds(r, SUB), OFF_GB:OFF_GB + CONV_WIDTH]
    gc = u_ref[pl.ds(r, SUB), OFF_GC:OFF_GC + CONV_WIDTH]
    v = u_ref[pl.ds(r, SUB), OFF_V:OFF_V + CONV_WIDTH]
    cv = gc * v
    cv_m1 = _down(cv, cv_prev, 1, row)
    cv_m2 = _down(cv, cv_prev, 2, row)
    cq = cw[2:3, :] * cv + cw[1:2, :] * cv_m1 + cw[0:1, :] * cv_m2
    return gb, gc, v, cv, cv_m1, cv_m2, cq


def _conv4_chunk(u_ref, r, xin_prev, rw, rb, row):
    xin = u_ref[pl.ds(r, SUB), OFF_XR:OFF_XR + LRU_WIDTH]
    m1 = _down(xin, xin_prev, 1, row)
    m2 = _down(xin, xin_prev, 2, row)
    m3 = _down(xin, xin_prev, 3, row)
    xr = rw[3:4, :] * xin + rw[2:3, :] * m1 + rw[1:2, :] * m2 + rw[0:1, :] * m3 + rb
    return xin, m1, m2, m3, xr


def _mixer_fwd(u, conv_w, rnn_conv_w, rnn_conv_b, wa, b_a, wx, b_x, lam, gnc, gnr, shards, tm):
    t_len = u.shape[0]
    n_steps = t_len // tm
    n_chunks = tm // SUB
    n_arr = len(shards)

    def body(u_ref, cw_ref, rw_ref, rb_ref, wa_ref, ba_ref, wx_ref, bx_ref, lam_ref, gnc_ref, gnr_ref, *rest):
        shard_refs = rest[0:n_arr]
        hs_ref, y_ref, xr_s, ra_ref, ii_ref, mult_ref, cq_ref = rest[n_arr:n_arr + 7]
        fulls = rest[n_arr + 7:2 * n_arr + 7]
        (y_s, pa_s, px_s, wabd, wxbd, cv_car, xin_car, h_car,
         send_sems, recv_sems, local_sems) = rest[2 * n_arr + 7:]
        _host_all_gather(pl.program_id(0), n_steps, shard_refs, fulls, send_sems, recv_sems, local_sems)

        @pl.when(pl.program_id(0) == 0)
        def _():
            cv_car[...] = jnp.zeros(cv_car.shape, F32)
            xin_car[...] = jnp.zeros(xin_car.shape, F32)
            h_car[...] = jnp.zeros(h_car.shape, F32)
            wabd[...] = _expand_heads(wa_ref[...])
            wxbd[...] = _expand_heads(wx_ref[...])

        row_c = lax.broadcasted_iota(jnp.int32, (SUB, CONV_WIDTH), 0)
        row_r = lax.broadcasted_iota(jnp.int32, (SUB, LRU_WIDTH), 0)
        cw = cw_ref[...]
        rw = rw_ref[...]
        rb = rb_ref[...]
        g_c = gnc_ref[...]
        g_r = gnr_ref[...]
        sp_c = LRU_C * _softplus_neg(lam_ref[...])

        def convs(i, carry):
            cv_prev, xin_prev = carry
            r = pl.multiple_of(i * SUB, SUB)
            gb, _, _, cv, _, _, cq = _conv3_chunk(u_ref, r, cv_prev, cw, row_c)
            cq_ref[pl.ds(r, SUB), :] = cq
            y_c = gb * cq
            y_s[pl.ds(r, SUB), 0:CONV_WIDTH] = y_c * _rms(y_c) * g_c
            xin, _, _, _, xr = _conv4_chunk(u_ref, r, xin_prev, rw, rb, row_r)
            xr_s[pl.ds(r, SUB), :] = xr
            return cv, xin

        cv_last, xin_last = _chunk_loop(n_chunks, convs, (cv_car[...], xin_car[...]))
        cv_car[...] = cv_last
        xin_car[...] = xin_last

        xrb = xr_s[...].astype(BF16)
        pa_s[...] = _block_diag_apply(xrb, wabd) + ba_ref[...]
        px_s[...] = _block_diag_apply(xrb, wxbd) + bx_ref[...]

        def recur(i, h_prev):
            r = pl.multiple_of(i * SUB, SUB)
            xr = xr_s[pl.ds(r, SUB), :]
            ra, ii, a, mult = _lru_gates(pa_s[pl.ds(r, SUB), :], px_s[pl.ds(r, SUB), :], sp_c)
            ra_ref[pl.ds(r, SUB), :] = ra
            ii_ref[pl.ds(r, SUB), :] = ii
            mult_ref[pl.ds(r, SUB), :] = mult
            a_cum, b_cum = _scan8_fwd(a, mult * ii * xr, row_r)
            h = a_cum * h_prev + b_cum
            hs_ref[pl.ds(r, SUB), :] = h
            ge, _ = _gelu(u_ref[pl.ds(r, SUB), OFF_G:OFF_G + LRU_WIDTH])
            y_r = h * ge
            y_s[pl.ds(r, SUB), CONV_WIDTH:MIX_WIDTH] = y_r * _rms(y_r) * g_r
            return h[SUB - 1:SUB, :]

        h_car[...] = _chunk_loop(n_chunks, recur, h_car[...])

        y_ref[...] = y_s[...].astype(BF16)

    row_tile = lambda w: pl.BlockSpec((tm, w), lambda i: (i, 0))
    whole = lambda a: pl.BlockSpec(a.shape, lambda i: (0,) * a.ndim)
    smalls = (conv_w, rnn_conv_w, rnn_conv_b, wa, b_a, wx, b_x, lam, gnc, gnr)
    return pl.pallas_call(
        body, grid=(n_steps,),
        in_specs=[row_tile(IN_COLS)] + [whole(a) for a in smalls] + [HBM_SPEC] * n_arr,
        out_specs=[row_tile(LRU_WIDTH), row_tile(MIX_WIDTH)] + [row_tile(LRU_WIDTH)] * 4 + [row_tile(CONV_WIDTH)]
        + [HBM_SPEC] * n_arr,
        out_shape=[jax.ShapeDtypeStruct((t_len, LRU_WIDTH), F32), jax.ShapeDtypeStruct((t_len, MIX_WIDTH), BF16)]
        + [jax.ShapeDtypeStruct((t_len, LRU_WIDTH), F32)] * 4 + [jax.ShapeDtypeStruct((t_len, CONV_WIDTH), F32)]
        + [jax.ShapeDtypeStruct((N_DEV,) + s.shape, BF16) for s in shards],
        scratch_shapes=[pltpu.VMEM((tm, MIX_WIDTH), F32),
                        pltpu.VMEM((tm, LRU_WIDTH), F32), pltpu.VMEM((tm, LRU_WIDTH), F32),
                        pltpu.VMEM((LRU_WIDTH, GROUP), BF16), pltpu.VMEM((LRU_WIDTH, GROUP), BF16),
                        pltpu.VMEM((SUB, CONV_WIDTH), F32), pltpu.VMEM((SUB, LRU_WIDTH), F32),
                        pltpu.VMEM((1, LRU_WIDTH), F32)]
        + _exchange_scratch(n_arr, 7) + [pltpu.SemaphoreType.DMA((n_arr,))],
        compiler_params=_params(("arbitrary",), 56), name="mixer_fwd",
    )(u, *smalls, *shards)


def _mlp_up(x, y, g_mlp, w_out, w1, w2_shard, tm):
    t_len = x.shape[0]
    n_steps = t_len // tm
    n_blk, _, blk = w1.shape

    def body(x_ref, y_ref, gm_ref, wout_hbm, w1_hbm, w2_ref, x1_ref, h2_ref, z_ref, w2_full,
             wout_s, w1_s, sem, send_sems, recv_sems, local_sems):
        step = pl.program_id(0)
        _host_all_gather(step, n_steps, [w2_ref], [w2_full], send_sems, recv_sems, local_sems)

        load_wout = pltpu.make_async_copy(wout_hbm, wout_s, sem.at[0])
        load_w1 = pltpu.make_async_copy(w1_hbm, w1_s, sem.at[1])

        @pl.when(step == 0)
        def _():
            load_wout.start()
            load_w1.start()
            load_wout.wait()

        x1v = x_ref[...] + jnp.dot(y_ref[...], wout_s[...], preferred_element_type=F32)
        x1_ref[...] = x1v
        h2 = (x1v * _rms(x1v) * gm_ref[...]).astype(BF16)
        h2_ref[...] = h2

        @pl.when(step == 0)
        def _():
            load_w1.wait()

        for k in range(n_blk):
            rp = jnp.maximum(jnp.dot(h2, w1_s[k], preferred_element_type=F32), 0.0)
            z_ref[:, k * blk:(k + 1) * blk] = (rp * rp).astype(BF16)

    row_tile = lambda w: pl.BlockSpec((tm, w), lambda i: (i, 0))
    return pl.pallas_call(
        body, grid=(n_steps,),
        in_specs=[row_tile(D_MODEL), row_tile(MIX_WIDTH), pl.BlockSpec((1, D_MODEL), lambda i: (0, 0)),
                  HBM_SPEC, HBM_SPEC, HBM_SPEC],
        out_specs=[row_tile(D_MODEL), row_tile(D_MODEL), row_tile(D_FF), HBM_SPEC],
        out_shape=[jax.ShapeDtypeStruct((t_len, D_MODEL), F32), jax.ShapeDtypeStruct((t_len, D_MODEL), BF16),
                   jax.ShapeDtypeStruct((t_len, D_FF), BF16), jax.ShapeDtypeStruct((N_DEV,) + w2_shard.shape, BF16)],
        scratch_shapes=[pltpu.VMEM(w_out.shape, BF16), pltpu.VMEM(w1.shape, BF16), pltpu.SemaphoreType.DMA((2,))]
        + _exchange_scratch(1, 7) + [pltpu.SemaphoreType.DMA((1,))],
        compiler_params=_params(("arbitrary",), 48), name="mlp_up",
    )(x, y, g_mlp, w_out, w1, w2_shard)


def _mlp_down_bwd(x1, z, target, g_mlp, g_f, w1, w2, tm):
    t_len = x1.shape[0]
    n_steps = t_len // tm
    n_blk, _, blk = w1.shape

    def body(x1_ref, z_ref, tg_ref, gm_ref, gf_ref, w1_hbm, w2_hbm, dx1_ref, dx2_ref, vec_ref, dpre_hbm,
             w1_s, w2_s, dp_s, sem, out_sem):
        step = pl.program_id(0)
        rows = pl.ds(pl.multiple_of(step * tm, tm), tm)
        dp_out = pltpu.make_async_copy(dp_s, dpre_hbm.at[rows, :], out_sem.at[0])

        load_w1 = pltpu.make_async_copy(w1_hbm, w1_s, sem.at[0])
        load_w2 = pltpu.make_async_copy(w2_hbm, w2_s, sem.at[1])

        @pl.when(step == 0)
        def _():
            load_w2.start()
            load_w1.start()
            vec_ref[...] = jnp.zeros(vec_ref.shape, F32)
            load_w2.wait()

        x1v = x1_ref[...]
        g_m = gm_ref[...]
        g_o = gf_ref[...]
        r2 = _rms(x1v)
        x1h = x1v * r2
        x2 = x1v + jnp.dot(z_ref[...], w2_s[...], preferred_element_type=F32)
        r3 = _rms(x2)
        x2h = x2 * r3
        err = x2h * g_o - tg_ref[...]
        dout = err * (1.0 / D_MODEL)
        vec_ref[ROW_LOSS:ROW_LOSS + 1, :] += (0.5 / D_MODEL) * jnp.sum(err * err, axis=0, keepdims=True)
        vec_ref[ROW_GF:ROW_GF + 1, :] += jnp.sum(dout * x2h, axis=0, keepdims=True)
        dx2 = _rms_bwd(dout, x2h, r3, g_o)
        dx2b = dx2.astype(BF16)
        dx2_ref[...] = dx2b
        dh2 = jnp.zeros((tm, D_MODEL), F32)

        @pl.when(step > 0)
        def _():
            dp_out.wait()

        @pl.when(step == 0)
        def _():
            load_w1.wait()

        for k in range(n_blk):
            cols = slice(k * blk, (k + 1) * blk)
            dz = _dot_nt(dx2b, w2_s[cols, :])
            dpb = (dz * 2.0 * jnp.sqrt(z_ref[:, cols].astype(F32))).astype(BF16)
            dp_s[:, cols] = dpb
            dh2 = dh2 + _dot_nt(dpb, w1_s[k])
        dp_out.start()
        vec_ref[ROW_GMLP:ROW_GMLP + 1, :] += jnp.sum(dh2 * x1h, axis=0, keepdims=True)
        dx1_ref[...] = dx2 + _rms_bwd(dh2, x1h, r2, g_m)

        @pl.when(step == n_steps - 1)
        def _():
            dp_out.wait()

    row_tile = lambda w: pl.BlockSpec((tm, w), lambda i: (i, 0))
    vec_spec = pl.BlockSpec((1, D_MODEL), lambda i: (0, 0))
    return pl.pallas_call(
        body, grid=(n_steps,),
        in_specs=[row_tile(D_MODEL), row_tile(D_FF), row_tile(D_MODEL), vec_spec, vec_spec, HBM_SPEC, HBM_SPEC],
        out_specs=[row_tile(D_MODEL), row_tile(D_MODEL), pl.BlockSpec((SUB, D_MODEL), lambda i: (0, 0)), HBM_SPEC],
        out_shape=[jax.ShapeDtypeStruct((t_len, D_MODEL), F32), jax.ShapeDtypeStruct((t_len, D_MODEL), BF16),
                   jax.ShapeDtypeStruct((SUB, D_MODEL), F32), jax.ShapeDtypeStruct((t_len, D_FF), BF16)],
        scratch_shapes=[pltpu.VMEM(w1.shape, BF16), pltpu.VMEM(w2.shape, BF16), pltpu.VMEM((tm, D_FF), BF16),
                        pltpu.SemaphoreType.DMA((2,)), pltpu.SemaphoreType.DMA((1,))],
        compiler_params=_params(("arbitrary",), 56), name="mlp_down_bwd",
    )(x1, z, target, g_mlp, g_f, w1, w2)


def _mixer_bwd(u, hs, dx1, saved, conv_w, rnn_conv_w, wa, wx, lam, gnc, gnr, w_out, chip_sums, g_wout, tm):
    t_len = u.shape[0]
    n_tiles = t_len // tm
    n_chunks = tm // SUB
    per_tile = tm // SUB
    n_sums = len(chip_sums)

    def body(u_ref, hs_ref, hp_ref, dx1_ref, xr_ref, ra_ref, ii_ref, mult_ref, cq_ref,
             cw_ref, rw_ref, wa_ref, wx_ref, lam_ref, gnc_ref, gnr_ref, wout_ref, *rest):
        hsends = rest[0:n_sums]
        gwout_ref = rest[n_sums]
        du_ref, vec_ref, wab_ref = rest[n_sums + 1:n_sums + 4]
        hrecvs = rest[n_sums + 4:2 * n_sums + 4]
        sib_wout = rest[2 * n_sums + 4]
        (du_s, dy_s, dpa_s, dpx_s, dxr_s, wabd, wxbd, acc, dwa_acc, dwx_acc,
         a_car, dh_car, dcq_car, dxr_car, i_send, i_recv, d_send, d_recv) = rest[2 * n_sums + 5:]
        step = pl.program_id(0)
        _host_chip_exchange(step, n_tiles, hsends, hrecvs, i_send, i_recv)
        _host_pair_exchange(step, n_tiles, [gwout_ref], [sib_wout], d_send, d_recv)
        has_prev = (step < n_tiles - 1).astype(F32)

        @pl.when(step == 0)
        def _():
            acc[...] = jnp.zeros(acc.shape, F32)
            dwa_acc[...] = jnp.zeros(dwa_acc.shape, F32)
            dwx_acc[...] = jnp.zeros(dwx_acc.shape, F32)
            a_car[...] = jnp.ones(a_car.shape, F32)
            dh_car[...] = jnp.zeros(dh_car.shape, F32)
            dcq_car[...] = jnp.zeros(dcq_car.shape, F32)
            dxr_car[...] = jnp.zeros(dxr_car.shape, F32)
            wabd[...] = _expand_heads(wa_ref[...])
            wxbd[...] = _expand_heads(wx_ref[...])

        row_c = lax.broadcasted_iota(jnp.int32, (SUB, CONV_WIDTH), 0)
        row_r = lax.broadcasted_iota(jnp.int32, (SUB, LRU_WIDTH), 0)
        cw = cw_ref[...]
        rw = rw_ref[...]
        g_c = gnc_ref[...]
        g_r = gnr_ref[...]
        sp_c = LRU_C * _softplus_neg(lam_ref[...])

        hs_before = hp_ref[...] * has_prev

        dy_s[...] = _dot_nt(dx1_ref[...].astype(BF16), wout_ref[...])

        xrb = xr_ref[...].astype(BF16)

        def recur_bwd(j, carry):
            a_later, dh_later = carry
            i = n_chunks - 1 - j
            r = pl.multiple_of(i * SUB, SUB)
            rp = pl.multiple_of(jnp.maximum(i - 1, 0) * SUB, SUB)
            xr = xr_ref[pl.ds(r, SUB), :]
            hs_c = hs_ref[pl.ds(r, SUB), :]
            hs_prev = jnp.where(i == 0, hs_before, hs_ref[pl.ds(rp, SUB), :])
            h_m1 = _down(hs_c, hs_prev, 1, row_r)
            ra = ra_ref[pl.ds(r, SUB), :]
            ii = ii_ref[pl.ds(r, SUB), :]
            mult = mult_ref[pl.ds(r, SUB), :]
            a = jnp.exp(-ra * sp_c)
            inv_mult = lax.rsqrt(mult * mult)
            ge, dge = _gelu(u_ref[pl.ds(r, SUB), OFF_G:OFF_G + LRU_WIDTH])
            y_r = hs_c * ge
            rr = _rms(y_r)
            yhat = y_r * rr
            dyn = dy_s[pl.ds(r, SUB), CONV_WIDTH:MIX_WIDTH]
            acc[ACC_GNR] += dyn * yhat
            dy_r = _rms_bwd(dyn, yhat, rr, g_r)
            du_s[pl.ds(r, SUB), OFF_G:OFF_G + LRU_WIDTH] = dy_r * hs_c * dge
            a_cum, d_cum = _scan8_rev(_up(a, a_later, 1, row_r), dy_r * ge, row_r)
            dh = a_cum * dh_later + d_cum
            dm = dh * mult
            dii = dm * xr
            dxr_s[pl.ds(r, SUB), :] = dm * ii
            dla = a * dh * (h_m1 - (ii * xr) * a * inv_mult)
            dla_r = dla * ra
            acc[ACC_SP] -= dla_r
            dpa = dla_r * (sp_c * (ra - 1.0))
            dpx = dii * ii * (1.0 - ii)
            acc[ACC_BA] += dpa
            acc[ACC_BX] += dpx
            dpa_s[pl.ds(r, SUB), :] = dpa
            dpx_s[pl.ds(r, SUB), :] = dpx
            return a, dh[0:1, :]

        a_first, dh_first = _chunk_loop(n_chunks, recur_bwd, (a_car[...], dh_car[...]))
        a_car[...] = a_first
        dh_car[...] = dh_first

        dpab = dpa_s[...].astype(BF16)
        dpxb = dpx_s[...].astype(BF16)
        dxr_s[...] += _block_diag_apply_t(dpab, wabd) + _block_diag_apply_t(dpxb, wxbd)
        for g in range(LRU_WIDTH // GROUP):
            cols = slice(g * GROUP, (g + 1) * GROUP)
            dwa_acc[cols, :] += _dot_tn(xrb[:, cols], dpab[:, cols])
            dwx_acc[cols, :] += _dot_tn(xrb[:, cols], dpxb[:, cols])

        def convs_bwd(j, carry):
            dcq_later, dxr_later = carry
            i = n_chunks - 1 - j
            r = pl.multiple_of(i * SUB, SUB)
            gb = u_ref[pl.ds(r, SUB), OFF_GB:OFF_GB + CONV_WIDTH]
            gc = u_ref[pl.ds(r, SUB), OFF_GC:OFF_GC + CONV_WIDTH]
            v = u_ref[pl.ds(r, SUB), OFF_V:OFF_V + CONV_WIDTH]
            cv = gc * v
            cq = cq_ref[pl.ds(r, SUB), :]
            y_c = gb * cq
            rc = _rms(y_c)
            yhat = y_c * rc
            dyn = dy_s[pl.ds(r, SUB), 0:CONV_WIDTH]
            acc[ACC_GNC, :, 0:CONV_WIDTH] += dyn * yhat
            dy_c = _rms_bwd(dyn, yhat, rc, g_c)
            dcq = dy_c * gb
            ahead3 = [dcq, _up(dcq, dcq_later, 1, row_c), _up(dcq, dcq_later, 2, row_c)]
            dcv = cw[2:3, :] * ahead3[0] + cw[1:2, :] * ahead3[1] + cw[0:1, :] * ahead3[2]
            for k in range(3):
                acc[ACC_CW + 2 - k, :, 0:CONV_WIDTH] += ahead3[k] * cv
            du_s[pl.ds(r, SUB), OFF_GB:OFF_GB + CONV_WIDTH] = dy_c * cq
            du_s[pl.ds(r, SUB), OFF_GC:OFF_GC + CONV_WIDTH] = dcv * v
            du_s[pl.ds(r, SUB), OFF_V:OFF_V + CONV_WIDTH] = dcv * gc

            xin = u_ref[pl.ds(r, SUB), OFF_XR:OFF_XR + LRU_WIDTH]
            dxr = dxr_s[pl.ds(r, SUB), :]
            ahead = [dxr] + [_up(dxr, dxr_later, k, row_r) for k in (1, 2, 3)]
            du_s[pl.ds(r, SUB), OFF_XR:OFF_XR + LRU_WIDTH] = (
                rw[3:4, :] * ahead[0] + rw[2:3, :] * ahead[1] + rw[1:2, :] * ahead[2] + rw[0:1, :] * ahead[3])
            for k in range(4):
                acc[ACC_RW + 3 - k] += ahead[k] * xin
            acc[ACC_BR] += dxr
            return dcq, dxr

        dcq_first, dxr_first = _chunk_loop(n_chunks, convs_bwd, (dcq_car[...], dxr_car[...]))
        dcq_car[...] = dcq_first
        dxr_car[...] = dxr_first

        du_ref[...] = du_s[...].astype(BF16)

        @pl.when(step == n_tiles - 1)
        def _():
            vec_ref[...] = jnp.zeros(vec_ref.shape, F32)
            rows = {ACC_GNC: ROW_GNC, ACC_GNR: ROW_GNR, ACC_BR: ROW_BR, ACC_BA: ROW_BA, ACC_BX: ROW_BX}
            for k in range(3):
                rows[ACC_CW + k] = ROW_CW + k
            for k in range(4):
                rows[ACC_RW + k] = ROW_RW + k
            for slot, out_row in rows.items():
                o = out_row - ROW_GNC
                vec_ref[o:o + 1, :] = jnp.sum(acc[slot], axis=0, keepdims=True)
            lam_v = lam_ref[...]
            dsp = jnp.sum(acc[ACC_SP], axis=0, keepdims=True)
            o = ROW_LAM - ROW_GNC
            vec_ref[o:o + 1, :] = -dsp * LRU_C / (1.0 + jnp.exp(lam_v))
            wab_ref[0:LRU_WIDTH, :] = _fold_heads(dwa_acc[...])
            wab_ref[LRU_WIDTH:2 * LRU_WIDTH, :] = _fold_heads(dwx_acc[...])

    rev = lambda w: pl.BlockSpec((tm, w), lambda s: (n_tiles - 1 - s, 0))
    before = lambda w: pl.BlockSpec((SUB, w), lambda s: (jnp.maximum((n_tiles - 1 - s) * per_tile - 1, 0), 0))
    whole = lambda a: pl.BlockSpec(a.shape, lambda s: (0,) * a.ndim)
    smalls = (conv_w, rnn_conv_w, wa, wx, lam, gnc, gnr, w_out)
    full = lambda w: pltpu.VMEM((tm, w), F32)
    return pl.pallas_call(
        body, grid=(n_tiles,),
        in_specs=[rev(IN_COLS), rev(LRU_WIDTH), before(LRU_WIDTH), rev(D_MODEL)]
        + [rev(a.shape[1]) for a in saved] + [whole(a) for a in smalls] + [HBM_SPEC] * (n_sums + 1),
        out_specs=[rev(IN_COLS), pl.BlockSpec((16, D_MODEL), lambda s: (0, 0)),
                   pl.BlockSpec((2 * LRU_WIDTH, HEAD_DIM), lambda s: (0, 0))] + [HBM_SPEC] * (n_sums + 1),
        out_shape=[jax.ShapeDtypeStruct((t_len, IN_COLS), BF16), jax.ShapeDtypeStruct((16, D_MODEL), F32),
                   jax.ShapeDtypeStruct((2 * LRU_WIDTH, HEAD_DIM), F32)]
        + [jax.ShapeDtypeStruct(s.shape, BF16) for s in chip_sums]
        + [jax.ShapeDtypeStruct((4,) + g_wout.shape[1:], BF16)],
        scratch_shapes=[full(IN_COLS), full(MIX_WIDTH), full(LRU_WIDTH), full(LRU_WIDTH), full(LRU_WIDTH),
                        pltpu.VMEM((LRU_WIDTH, GROUP), BF16), pltpu.VMEM((LRU_WIDTH, GROUP), BF16),
                        pltpu.VMEM((N_ACC, SUB, LRU_WIDTH), F32),
                        pltpu.VMEM((LRU_WIDTH, GROUP), F32), pltpu.VMEM((LRU_WIDTH, GROUP), F32),
                        pltpu.VMEM((SUB, LRU_WIDTH), F32), pltpu.VMEM((1, LRU_WIDTH), F32),
                        pltpu.VMEM((SUB, CONV_WIDTH), F32), pltpu.VMEM((SUB, LRU_WIDTH), F32)]
        + _exchange_scratch(n_sums, 3) + _exchange_scratch(1, 4),
        compiler_params=_params(("arbitrary",), 56), name="mixer_bwd",
    )(u, hs, hs, dx1, *saved, *smalls, *chip_sums, g_wout)


def _in_proj_bwd(du, dx1, x, g_mix, win_t, tm, chip_sums, g_own):
    t_len = x.shape[0]
    n_steps = t_len // tm

    def body(du_ref, dx1_ref, x_ref, g_ref, w_ref, hs_ref, gown_ref,
             dx_ref, vec_ref, landed_ref, sib_ref, i_send, i_recv, d_send, d_recv):
        step = pl.program_id(0)
        _host_chip_exchange(step, n_steps, [hs_ref], [landed_ref], i_send, i_recv)
        _host_half_exchange(step, n_steps, gown_ref, sib_ref, d_send, d_recv)

        @pl.when(step == 0)
        def _():
            vec_ref[...] = jnp.zeros(vec_ref.shape, F32)

        dh = jnp.dot(du_ref[...], w_ref[...], preferred_element_type=F32)
        xv = x_ref[...]
        r1 = _rms(xv)
        xh = xv * r1
        vec_ref[0:1, :] += jnp.sum(dh * xh, axis=0, keepdims=True)
        dx_ref[...] = dx1_ref[...] + _rms_bwd(dh, xh, r1, g_ref[...])

    row_tile = lambda w: pl.BlockSpec((tm, w), lambda i: (i, 0))
    half_shape = (g_own.shape[0], g_own.shape[1] // 2, g_own.shape[2])
    return pl.pallas_call(
        body, grid=(n_steps,),
        in_specs=[row_tile(IN_COLS), row_tile(D_MODEL), row_tile(D_MODEL), pl.BlockSpec((1, D_MODEL), lambda i: (0, 0)),
                  pl.BlockSpec((IN_COLS, D_MODEL), lambda i: (0, 0))] + [HBM_SPEC] * 2,
        out_specs=[row_tile(D_MODEL), pl.BlockSpec((SUB, D_MODEL), lambda i: (0, 0))] + [HBM_SPEC] * 2,
        out_shape=[jax.ShapeDtypeStruct((t_len, D_MODEL), F32), jax.ShapeDtypeStruct((SUB, D_MODEL), F32),
                   jax.ShapeDtypeStruct(chip_sums.shape, BF16), jax.ShapeDtypeStruct(half_shape, BF16)],
        scratch_shapes=_exchange_scratch(1, 3) + [pltpu.SemaphoreType.DMA((1,)), pltpu.SemaphoreType.DMA((1,))],
        compiler_params=_params(("arbitrary",), 56), name="in_proj_bwd",
    )(du, dx1, x, g_mix, win_t, chip_sums, g_own)


def _tn_weight_grad(a, b, tk, name, pair=(), col_blocks=1):
    t_len, m = a.shape
    n = b.shape[1]
    n_steps = t_len // tk
    sent = tuple(pair)
    n_sent = len(sent)

    def body(a_ref, b_ref, *rest):
        srcs = rest[0:n_sent]
        o_ref = rest[n_sent]
        dsts = rest[n_sent + 1:2 * n_sent + 1]
        acc = rest[2 * n_sent + 1]
        sems = rest[2 * n_sent + 2:]
        j = pl.program_id(0)
        if pair:
            _host_pair_exchange(j, n_steps, srcs, dsts, *sems)

        @pl.when(j == 0)
        def _():
            acc[...] = jnp.zeros(acc.shape, F32)

        acc[...] += _dot_tn(a_ref[...].astype(BF16), b_ref[...].astype(BF16))

        @pl.when(j == n_steps - 1)
        def _():
            if col_blocks == 1:
                o_ref[...] = acc[...].astype(BF16)
            else:
                for k in range(col_blocks):
                    o_ref[k] = acc[:, k * nb:(k + 1) * nb].astype(BF16)

    nb = n // col_blocks
    out_dims = (m, n) if col_blocks == 1 else (col_blocks, m, nb)
    landed = [jax.ShapeDtypeStruct((4,) + g.shape[1:], BF16) for g in pair]
    scratch = [pltpu.VMEM((m, n), F32)]
    if n_sent:
        scratch += _exchange_scratch(n_sent, 4)
    return pl.pallas_call(
        body, grid=(n_steps,),
        in_specs=[pl.BlockSpec((tk, m), lambda j: (j, 0)), pl.BlockSpec((tk, n), lambda j: (j, 0))]
        + [HBM_SPEC] * n_sent,
        out_specs=[pl.BlockSpec(out_dims, lambda j: (0,) * len(out_dims))] + [HBM_SPEC] * n_sent,
        out_shape=[jax.ShapeDtypeStruct(out_dims, BF16)] + landed,
        scratch_shapes=scratch,
        compiler_params=_params(("arbitrary",), 56), name=name,
    )(a, b, *sent)


def _w_in_grad_part(du, h, tk, name, chip_ids, chip=(), halves=None, small=None):
    t_len = du.shape[0]
    n_t = t_len // tk
    n_q = chip_ids.shape[0]
    width = 2 * (IN_COLS // N_DEV)
    n_steps = n_q * n_t
    n_chip = len(chip)
    sent = tuple(chip) + (() if halves is None else (halves,)) + (() if small is None else tuple(small))
    n_sent = len(sent)

    def body(ids_ref, a_ref, b_ref, *rest):
        srcs = rest[0:n_sent]
        o_ref = rest[n_sent]
        dsts = rest[n_sent + 1:2 * n_sent + 1]
        acc = rest[2 * n_sent + 1]
        sems = list(rest[2 * n_sent + 2:])
        j = pl.program_id(1)
        step = pl.program_id(0) * n_t + j
        if chip:
            _host_chip_exchange(step, n_steps, srcs[0:n_chip], dsts[0:n_chip], sems.pop(0), sems.pop(0))
        if halves is not None:
            _host_half_exchange(step, n_steps, srcs[n_chip], dsts[n_chip], sems.pop(0), sems.pop(0))
        if small is not None:
            _host_small_exchange(step, n_steps, *srcs[n_sent - 3:], *dsts[n_sent - 3:], *sems)

        @pl.when(j == 0)
        def _():
            acc[...] = jnp.zeros(acc.shape, F32)

        acc[...] += _dot_tn(a_ref[...], b_ref[...])

        @pl.when(j == n_t - 1)
        def _():
            o_ref[0] = acc[...].astype(BF16)

    landed = [jax.ShapeDtypeStruct(s.shape, BF16) for s in chip]
    scratch = [pltpu.VMEM((width, D_MODEL), F32)]
    if chip:
        scratch += _exchange_scratch(len(chip), 3)
    if halves is not None:
        landed.append(jax.ShapeDtypeStruct((halves.shape[0], halves.shape[1] // 2, halves.shape[2]), BF16))
        scratch += [pltpu.SemaphoreType.DMA((halves.shape[0],)), pltpu.SemaphoreType.DMA((halves.shape[0],))]
    if small is not None:
        vec_m, vec_b, wab = small
        landed += [jax.ShapeDtypeStruct((N_DEV,) + vec_m.shape, F32), jax.ShapeDtypeStruct((N_DEV,) + vec_b.shape, F32),
                   jax.ShapeDtypeStruct((N_DEV, wab.shape[0] // N_DEV, wab.shape[1]), F32)]
        scratch += _exchange_scratch(3, N_DEV) + [pltpu.SemaphoreType.DMA((2,))]
    grid_spec = pltpu.PrefetchScalarGridSpec(
        num_scalar_prefetch=1, grid=(n_q, n_t),
        in_specs=[pl.BlockSpec((tk, width), lambda q, j, ids: (j, ids[q])),
                  pl.BlockSpec((tk, D_MODEL), lambda q, j, ids: (j, 0))] + [HBM_SPEC] * n_sent,
        out_specs=[pl.BlockSpec((1, width, D_MODEL), lambda q, j, ids: (q, 0, 0))] + [HBM_SPEC] * n_sent,
        scratch_shapes=scratch)
    return pl.pallas_call(
        body, grid_spec=grid_spec, out_shape=[jax.ShapeDtypeStruct((n_q, width, D_MODEL), BF16)] + landed,
        compiler_params=_params(("arbitrary", "arbitrary"), 40), name=name,
    )(chip_ids, du, h, *sent)


def _adamw(w, g, m, v):
    m = ADAM_B1 * m + (1.0 - ADAM_B1) * g
    v = ADAM_B2 * v + (1.0 - ADAM_B2) * (g * g)
    delta = -ADAM_LR * ((m / BC1) / (jnp.sqrt(v / BC2) + ADAM_EPS) + ADAM_WD * w)
    return delta, m, v


def _update_sharded(g, landed, w, m, v, rows_blk, name):
    rows, cols = w.shape

    def body(g_ref, l_ref, w_ref, m_ref, v_ref, og, od, om, ov):
        gv = g_ref[...]
        for j in range(3):
            gv = gv + l_ref[j].astype(F32)
        delta, mn, vn = _adamw(w_ref[...], gv, m_ref[...], v_ref[...])
        og[...] = gv
        od[...] = delta
        om[...] = mn
        ov[...] = vn

    blk = pl.BlockSpec((rows_blk, cols), lambda i: (i, 0))
    shape = pltpu.HBM((rows, cols), F32)
    return pl.pallas_call(
        body, grid=(rows // rows_blk,),
        in_specs=[blk, pl.BlockSpec((3, rows_blk, cols), lambda i: (0, i, 0)), blk, blk, blk],
        out_specs=[blk] * 4, out_shape=[shape] * 4,
        compiler_params=_params(("arbitrary",), 32), name=name,
    )(*_in_hbm(g, landed, w, m, v))


def _update_w_in(g_own, sib_own, landed, w_t, m_t, v_t, core, cols_blk):
    rows, cols = w_t.shape

    def body(core_ref, g_ref, s_ref, l_ref, w_ref, m_ref, v_ref, og, od, om, ov):
        gv = g_ref[0, 0].astype(F32) + s_ref[0].astype(F32)
        for j in range(3):
            gv = gv + l_ref[j].astype(F32)
        delta, mn, vn = _adamw(w_ref[...], gv, m_ref[...], v_ref[...])
        og[...] = gv
        od[...] = delta
        om[...] = mn
        ov[...] = vn

    blk = pl.BlockSpec((rows, cols_blk), lambda i, cr: (0, i))
    grid_spec = pltpu.PrefetchScalarGridSpec(
        num_scalar_prefetch=1, grid=(cols // cols_blk,),
        in_specs=[pl.BlockSpec((1, 1, rows, cols_blk), lambda i, cr: (0, cr[0], 0, i)),
                  pl.BlockSpec((1, rows, cols_blk), lambda i, cr: (0, 0, i)),
                  pl.BlockSpec((3, rows, cols_blk), lambda i, cr: (0, 0, i)), blk, blk, blk],
        out_specs=[blk] * 4)
    return pl.pallas_call(
        body, grid_spec=grid_spec, out_shape=[pltpu.HBM((rows, cols), F32)] * 4,
        compiler_params=_params(("arbitrary",), 32), name="update_w_in",
    )(core, *_in_hbm(g_own.reshape(1, 2, rows, cols), sib_own, landed, w_t, m_t, v_t))


def _update_small(vsum, wsum, g_cw, g_rw, weights, moments_m, moments_v):
    n = len(weights)

    def body(*refs):
        vs, ws, gcw, grw = refs[0:4]
        w_refs = refs[4:4 + n]
        m_refs = refs[4 + n:4 + 2 * n]
        v_refs = refs[4 + 2 * n:4 + 3 * n]
        outs = refs[4 + 3 * n:]
        loss_ref = outs[0]
        loss_ref[...] = jnp.sum(vs[ROW_LOSS:ROW_LOSS + 1, :], axis=1, keepdims=True)
        grads = [
            vs[ROW_GMIX:ROW_GMIX + 1, :], gcw[...], grw[...], vs[ROW_BR:ROW_BR + 1, :],
            ws[0:LRU_WIDTH, :], vs[ROW_BA:ROW_BA + 1, :], ws[LRU_WIDTH:2 * LRU_WIDTH, :], vs[ROW_BX:ROW_BX + 1, :],
            vs[ROW_LAM:ROW_LAM + 1, :], vs[ROW_GNC:ROW_GNC + 1, 0:CONV_WIDTH], vs[ROW_GNR:ROW_GNR + 1, :],
            vs[ROW_GMLP:ROW_GMLP + 1, :], vs[ROW_GF:ROW_GF + 1, :],
        ]
        for k in range(n):
            gk = grads[k]
            delta, mn, vn = _adamw(w_refs[k][...], gk, m_refs[k][...], v_refs[k][...])
            outs[1 + 4 * k][...] = gk
            outs[2 + 4 * k][...] = delta
            outs[3 + 4 * k][...] = mn
            outs[4 + 4 * k][...] = vn

    whole = lambda a: pl.BlockSpec(a.shape, lambda i: (0,) * len(a.shape))
    out_shape = [jax.ShapeDtypeStruct((1, 1), F32)]
    for w in weights:
        out_shape += [jax.ShapeDtypeStruct(w.shape, F32)] * 4
    args = (vsum, wsum, g_cw, g_rw, *weights, *moments_m, *moments_v)
    return pl.pallas_call(
        body, grid=(1,), out_shape=out_shape, in_specs=[whole(a) for a in args], out_specs=[whole(s) for s in out_shape],
        compiler_params=_params(("arbitrary",), 32), name="update_small",
    )(*args)


def kernel(x, norm_mix_g, w_in, conv_w, rnn_conv_w, rnn_conv_b, w_a, b_a, w_x, b_x, lru_lambda, g_norm_conv, g_norm_rnn, w_out, norm_mlp_g, w_mlp_in, w_mlp_out, final_norm_g, loss_target, m_norm_mix_g, m_w_in, m_conv_w, m_rnn_conv_w, m_rnn_conv_b, m_w_a, m_b_a, m_w_x, m_b_x, m_lru_lambda, m_g_norm_conv, m_g_norm_rnn, m_w_out, m_norm_mlp_g, m_w_mlp_in, m_w_mlp_out, m_final_norm_g, v_norm_mix_g, v_w_in, v_conv_w, v_rnn_conv_w, v_rnn_conv_b, v_w_a, v_b_a, v_w_x, v_b_x, v_lru_lambda, v_g_norm_conv, v_g_norm_rnn, v_w_out, v_norm_mlp_g, v_w_mlp_in, v_w_mlp_out, v_final_norm_g):
    t_len = x.shape[1]
    my_id = 4 * lax.axis_index("x") + 2 * lax.axis_index("y") + lax.axis_index("c")
    tm = min(256, t_len)
    tb = min(512, t_len)
    tk = min(512, t_len)

    xs = x.reshape(t_len, D_MODEL)
    tgt = loss_target.reshape(t_len, D_MODEL)
    flat = lambda a: a.reshape(a.shape[-2:]) if a.ndim == 3 else a.reshape(1, -1)
    heads = lambda a: a.reshape(LRU_WIDTH, HEAD_DIM)

    turned = lambda a: jnp.transpose(flat(a))
    win_shard, wout_shard, w1_shard, w2_shard, cp_shard = _prep_shards(
        turned(w_in), flat(w_out), flat(w_mlp_in), flat(w_mlp_out), flat(conv_w), flat(rnn_conv_w))

    u, h, win_t, cp_full = _in_proj(xs, flat(norm_mix_g), (win_shard, cp_shard), min(1024, t_len))
    cpack = cp_full.reshape(N_DEV, 8, 128)
    conv_full = jnp.transpose(cpack[:, 0:3, 0:64], (1, 0, 2)).reshape(3, CONV_WIDTH)
    rnn_full = jnp.transpose(cpack[:, 3:7, :], (1, 0, 2)).reshape(4, LRU_WIDTH)
    mixer_small = (conv_full, rnn_full, flat(rnn_conv_b), heads(w_a), flat(b_a), heads(w_x), flat(b_x),
                   flat(lru_lambda), flat(g_norm_conv), flat(g_norm_rnn))
    hs, y, xr, gate_r, gate_i, mult, cq, w1_blk, wout_blk = _mixer_fwd(u, *mixer_small, (w1_shard, wout_shard), tb)
    wout_f = wout_blk.reshape(MIX_WIDTH, D_MODEL)
    x1, h2, z, w2_blk = _mlp_up(xs, y, flat(norm_mlp_g), wout_f, w1_blk, w2_shard, tb)
    dx1, dx2, vec_m, dpre = _mlp_down_bwd(x1, z, tgt, flat(norm_mlp_g), flat(final_norm_g), w1_blk,
                                          w2_blk.reshape(D_FF, D_MODEL), tb)
    (g_w1,) = _tn_weight_grad(h2, dpre, tk, "w_mlp_in_grad", col_blocks=N_DEV)
    (g_w2,) = _tn_weight_grad(z, dx2, tk, "w_mlp_out_grad")
    g_w2 = g_w2.reshape(N_DEV, D_FF // N_DEV, D_MODEL)
    g_wout, sib_w1, sib_w2 = _tn_weight_grad(y, dx1, tk, "w_out_grad", pair=(g_w1, g_w2))
    g_wout = g_wout.reshape(N_DEV, MIX_WIDTH // N_DEV, D_MODEL)
    hsend_w1, own_w1, hsend_w2, own_w2 = _pair_sum((g_w1, g_w2), (sib_w1, sib_w2), "pair_sum_w_mlp")
    du, vec_b, wab, landed_w1, landed_w2, sib_wout = _mixer_bwd(
        u, hs, dx1, (xr, gate_r, gate_i, mult, cq), conv_full, rnn_full, heads(w_a), heads(w_x),
        flat(lru_lambda), flat(g_norm_conv), flat(g_norm_rnn), wout_f, (hsend_w1, hsend_w2), g_wout, tm)
    hsend_wout, own_wout = _pair_sum((g_wout,), (sib_wout,), "pair_sum_w_out")
    ax, ay, ac = lax.axis_index("x"), lax.axis_index("y"), lax.axis_index("c")
    chip_ids = jnp.stack([2 * cx + cy for cx, cy in [(ax, ay)] + _other_chips(ax, ay)]).astype(jnp.int32)
    core = jnp.reshape(ac, (1,)).astype(jnp.int32)
    tw = min(1024, t_len)
    g_others, landed_wout, vrecv_m, vrecv_b, wrecv = _w_in_grad_part(
        du, h, tw, "w_in_grad_others", chip_ids[1:4], chip=(hsend_wout,), small=(vec_m, vec_b, wab))
    g_own, sib_others = _w_in_grad_part(du, h, tw, "w_in_grad_own", chip_ids[0:1], halves=g_others)
    hsend_win = _pair_sum_parts(g_others, sib_others, core)
    grad_x, vec_x, landed_win, sib_own = _in_proj_bwd(du, dx1, xs, flat(norm_mix_g), win_t, tm, hsend_win, g_own)

    vsum, wsum = _final_small(vrecv_m, vrecv_b, wab, wrecv, vec_x)

    up_win = _update_w_in(g_own, sib_own, landed_win, turned(w_in), turned(m_w_in), turned(v_w_in), core, 256)
    up_win = [jnp.transpose(a) for a in up_win]
    up_wout = _update_sharded(own_wout, landed_wout, flat(w_out), flat(m_w_out), flat(v_w_out), 96, "update_w_out")
    up_w1 = _update_sharded(own_w1, landed_w1, flat(w_mlp_in), flat(m_w_mlp_in), flat(v_w_mlp_in), 256,
                            "update_w_mlp_in")
    up_w2 = _update_sharded(own_w2, landed_w2, flat(w_mlp_out), flat(m_w_mlp_out), flat(v_w_mlp_out), 256,
                            "update_w_mlp_out")

    g_cw = lax.dynamic_slice(vsum, (ROW_CW, 64 * my_id), (3, 64))
    g_rw = lax.dynamic_slice(vsum, (ROW_RW, 128 * my_id), (4, 128))
    small_w = (norm_mix_g, conv_w, rnn_conv_w, rnn_conv_b, w_a, b_a, w_x, b_x, lru_lambda, g_norm_conv, g_norm_rnn,
               norm_mlp_g, final_norm_g)
    small_m = (m_norm_mix_g, m_conv_w, m_rnn_conv_w, m_rnn_conv_b, m_w_a, m_b_a, m_w_x, m_b_x, m_lru_lambda,
               m_g_norm_conv, m_g_norm_rnn, m_norm_mlp_g, m_final_norm_g)
    small_v = (v_norm_mix_g, v_conv_w, v_rnn_conv_w, v_rnn_conv_b, v_w_a, v_b_a, v_w_x, v_b_x, v_lru_lambda,
               v_g_norm_conv, v_g_norm_rnn, v_norm_mlp_g, v_final_norm_g)
    is_heads = (False, False, False, False, True, False, True, False, False, False, False, False, False)
    as2d = lambda arrs: [heads(a) if hd else flat(a) for a, hd in zip(arrs, is_heads)]
    small_out = _update_small(vsum, wsum, g_cw, g_rw, as2d(small_w), as2d(small_m), as2d(small_v))
    loss = small_out[0].reshape(())

    names = ["norm_mix_g", "w_in", "conv_w", "rnn_conv_w", "rnn_conv_b", "w_a", "b_a", "w_x", "b_x", "lru_lambda",
             "g_norm_conv", "g_norm_rnn", "w_out", "norm_mlp_g", "w_mlp_in", "w_mlp_out", "final_norm_g"]
    originals = dict(zip(names, (norm_mix_g, w_in, conv_w, rnn_conv_w, rnn_conv_b, w_a, b_a, w_x, b_x, lru_lambda,
                                 g_norm_conv, g_norm_rnn, w_out, norm_mlp_g, w_mlp_in, w_mlp_out, final_norm_g)))
    results = {"w_in": up_win, "w_out": up_wout, "w_mlp_in": up_w1, "w_mlp_out": up_w2}
    small_names = ["norm_mix_g", "conv_w", "rnn_conv_w", "rnn_conv_b", "w_a", "b_a", "w_x", "b_x", "lru_lambda",
                   "g_norm_conv", "g_norm_rnn", "norm_mlp_g", "final_norm_g"]
    for k, nm in enumerate(small_names):
        results[nm] = small_out[1 + 4 * k:5 + 4 * k]
    out = [loss, grad_x.reshape(x.shape)]
    for kind in range(4):
        out += [results[nm][kind].reshape(originals[nm].shape) for nm in names]
    return tuple(out)
```

```python
import functools

import jax
import jax.numpy as jnp
from jax import lax
from jax.experimental import pallas as pl
from jax.experimental.pallas import tpu as pltpu

F32 = jnp.float32
BF16 = jnp.bfloat16

D_MODEL = 1024
HEAD_DIM = 64
CONV_WIDTH = 512
LRU_WIDTH = 1024
MIX_WIDTH = CONV_WIDTH + LRU_WIDTH
IN_COLS = 3 * CONV_WIDTH + 2 * LRU_WIDTH
D_FF = 4 * D_MODEL
GROUP = 256
EPS = 1e-6
LRU_C = 8.0
N_DEV = 8
SUB = 8

OFF_GB, OFF_GC, OFF_V, OFF_XR, OFF_G = 0, 512, 1024, 1536, 2560

ADAM_LR, ADAM_B1, ADAM_B2, ADAM_EPS, ADAM_WD, ADAM_STEP = 0.001, 0.9, 0.999, 1e-08, 0.01, 10
BC1 = 1.0 - ADAM_B1 ** ADAM_STEP
BC2 = 1.0 - ADAM_B2 ** ADAM_STEP

MIB = 1024 * 1024
MESH = pl.DeviceIdType.MESH

VEC_ROWS = 32
ROW_GF, ROW_GMLP, ROW_LOSS = 0, 1, 2
ROW_GNC, ROW_GNR, ROW_BR, ROW_BA, ROW_BX, ROW_LAM, ROW_CW, ROW_RW = 8, 9, 10, 11, 12, 13, 14, 17
ROW_GMIX = 24
ACC_GNC, ACC_GNR, ACC_BR, ACC_BA, ACC_BX, ACC_SP, ACC_CW, ACC_RW, N_ACC = 0, 1, 2, 3, 4, 5, 6, 9, 13


def _params(semantics=None, vmem_mib=48):
    return pltpu.CompilerParams(dimension_semantics=semantics, vmem_limit_bytes=vmem_mib * MIB)


def _rms(x):
    return lax.rsqrt(jnp.mean(x * x, axis=-1, keepdims=True) + EPS)


def _rms_bwd(dy, xhat, r, g):
    dyh = dy * g
    return r * (dyh - xhat * jnp.mean(dyh * xhat, axis=-1, keepdims=True))


def _sigmoid(x):
    return 0.5 + 0.5 * jnp.tanh(0.5 * x)


def _gelu(x):
    c0, c1 = 0.7978845608028654, 0.044715
    x2 = x * x
    t = jnp.tanh(x * (c0 + (c0 * c1) * x2))
    half = 0.5 + 0.5 * t
    ge = x * half
    dge = half + (ge - ge * half) * (2.0 * c0 + (6.0 * c0 * c1) * x2)
    return ge, dge


def _softplus_neg(lam):
    z = -lam
    e = jnp.exp(-jnp.abs(z))
    return jnp.maximum(z, 0.0) + jnp.where(e < 1e-4, e * (1.0 - 0.5 * e), jnp.log(1.0 + e))


def _lru_gates(pa, px, sp_c):
    ra = _sigmoid(pa)
    ii = _sigmoid(px)
    neg_la = ra * sp_c
    a = jnp.exp(-neg_la)
    m2 = jnp.tanh(neg_la) * (1.0 + a * a)
    mult = jnp.where(m2 > 0.0, m2 * lax.rsqrt(m2), 0.0)
    return ra, ii, a, mult


def _down(cur, prev, s, row):
    return pltpu.roll(jnp.where(row < SUB - s, cur, prev), s, 0)


def _up(cur, nxt, s, row):
    return pltpu.roll(jnp.where(row >= s, cur, nxt), SUB - s, 0)


def _scan8_fwd(a, b, row):
    for s in (1, 2, 4):
        m = row >= s
        a_sh = pltpu.roll(a, s, 0)
        b_sh = pltpu.roll(b, s, 0)
        b = jnp.where(m, a * b_sh + b, b)
        a = jnp.where(m, a * a_sh, a)
    return a, b


def _scan8_rev(a, b, row):
    for s in (1, 2, 4):
        m = row < SUB - s
        a_sh = pltpu.roll(a, SUB - s, 0)
        b_sh = pltpu.roll(b, SUB - s, 0)
        b = jnp.where(m, a * b_sh + b, b)
        a = jnp.where(m, a * a_sh, a)
    return a, b


def _group_mask(shape):
    r = lax.broadcasted_iota(jnp.int32, shape, 0)
    c = lax.broadcasted_iota(jnp.int32, shape, 1)
    return ((r % GROUP) // HEAD_DIM) == (c // HEAD_DIM)


def _expand_heads(w):
    j = lax.broadcasted_iota(jnp.int32, (HEAD_DIM, GROUP), 0)
    c = lax.broadcasted_iota(jnp.int32, (HEAD_DIM, GROUP), 1)
    spread = (c % HEAD_DIM == j).astype(BF16)
    e = jnp.dot(w.astype(BF16), spread, preferred_element_type=F32)
    return jnp.where(_group_mask(e.shape), e, 0.0).astype(BF16)


def _fold_heads(p):
    p = jnp.where(_group_mask(p.shape), p, 0.0)
    c = lax.broadcasted_iota(jnp.int32, (GROUP, HEAD_DIM), 0)
    j = lax.broadcasted_iota(jnp.int32, (GROUP, HEAD_DIM), 1)
    fold = (c % HEAD_DIM == j).astype(BF16)
    hi = p.astype(BF16)
    rest = p - hi.astype(F32)
    mid = rest.astype(BF16)
    lo = (rest - mid.astype(F32)).astype(BF16)
    dot = functools.partial(jnp.dot, preferred_element_type=F32)
    return dot(hi, fold) + dot(mid, fold) + dot(lo, fold)


def _block_diag_apply(xb, wbd_ref):
    parts = [jnp.dot(xb[:, g * GROUP:(g + 1) * GROUP], wbd_ref[g * GROUP:(g + 1) * GROUP, :],
                     preferred_element_type=F32) for g in range(LRU_WIDTH // GROUP)]
    return jnp.concatenate(parts, axis=1)


def _block_diag_apply_t(db, wbd_ref):
    parts = [lax.dot_general(db[:, g * GROUP:(g + 1) * GROUP], wbd_ref[g * GROUP:(g + 1) * GROUP, :],
                             (((1,), (1,)), ((), ())), preferred_element_type=F32)
             for g in range(LRU_WIDTH // GROUP)]
    return jnp.concatenate(parts, axis=1)


def _dot_nt(a, b):
    return lax.dot_general(a, b, (((1,), (1,)), ((), ())), preferred_element_type=F32)


def _dot_tn(a, b):
    return lax.dot_general(a, b, (((0,), (0,)), ((), ())), preferred_element_type=F32)


CHUNKS_IN_FLIGHT = 8


def _chunk_loop(n_chunks, chunk, init):
    def body(k, carry):
        for j in range(CHUNKS_IN_FLIGHT):
            carry = chunk(k * CHUNKS_IN_FLIGHT + j, carry)
        return carry

    return lax.fori_loop(0, n_chunks // CHUNKS_IN_FLIGHT, body, init)


def _place():
    x, y, c = lax.axis_index("x"), lax.axis_index("y"), lax.axis_index("c")
    return x, y, c


def _block_id(chip, core):
    return 4 * chip[0] + 2 * chip[1] + core


def _other_chips(x, y):
    return [(1 - x, y), (x, 1 - y), (1 - x, 1 - y)]


def _remote_copy(src, dst, send_sem, recv_sem, to):
    return pltpu.make_async_remote_copy(src_ref=src, dst_ref=dst, send_sem=send_sem, recv_sem=recv_sem,
                                        device_id=to, device_id_type=MESH)


HBM_SPEC = pl.BlockSpec(memory_space=pl.ANY)


def _in_hbm(*arrays):
    return [pltpu.with_memory_space_constraint(a, pltpu.HBM) for a in arrays]


def _prep_shards(w_in_t, w_out, w_mlp_in, w_mlp_out, conv_w, rnn_conv_w):
    def body(win_ref, wout_ref, w1_ref, w2_ref, cw_ref, rw_ref, o_win, o_wout, o_w1, o_w2, o_cp):
        o_win[...] = win_ref[...].astype(BF16)
        o_wout[...] = wout_ref[...].astype(BF16)
        o_w1[...] = w1_ref[...].astype(BF16)
        o_w2[...] = w2_ref[...].astype(BF16)
        o_cp[...] = jnp.zeros(o_cp.shape, F32)
        o_cp[0:3, 0:64] = cw_ref[...]
        o_cp[3:7, :] = rw_ref[...]

    whole = lambda shape: pl.BlockSpec(shape, lambda i: (0,) * len(shape))
    args = (w_in_t, w_out, w_mlp_in, w_mlp_out, conv_w, rnn_conv_w)
    shapes = [(w_in_t.shape, BF16), (w_out.shape, BF16), (w_mlp_in.shape, BF16), (w_mlp_out.shape, BF16),
              ((8, 128), F32)]
    return pl.pallas_call(
        body, grid=(1,), out_shape=[jax.ShapeDtypeStruct(s, d) for s, d in shapes],
        in_specs=[whole(a.shape) for a in args], out_specs=[whole(s) for s, _ in shapes],
        compiler_params=_params(("arbitrary",), 40), name="prep_shards",
    )(*args)


def _host_all_gather(step, n_steps, shards, fulls, send_sems, recv_sems, local_sems):
    x, y, c = _place()
    me = (x, y, c)
    my_id = _block_id((x, y), c)
    sibling = (x, y, 1 - c)
    chips = _other_chips(x, y)
    n_arr = len(shards)

    def copy(arr, k, block, to, src=None):
        dst = fulls[arr].at[block]
        return _remote_copy(dst if src is None else src, dst, send_sems.at[arr, k], recv_sems.at[arr, k], to)

    def local(arr):
        return pltpu.make_async_copy(shards[arr], fulls[arr].at[my_id], local_sems.at[arr])

    @pl.when(step == 0)
    def _():
        for arr in range(n_arr):
            local(arr).start()
            copy(arr, 0, my_id, sibling, shards[arr]).start()
            for j, chip in enumerate(chips):
                copy(arr, 1 + j, my_id, (*chip, c), shards[arr]).start()

    @pl.when(step == max(n_steps - 2, 0))
    def _():
        for j, chip in enumerate(chips):
            for arr in range(n_arr):
                copy(arr, 1 + j, _block_id(chip, c), me).wait_recv()
                copy(arr, 4 + j, _block_id(chip, c), sibling).start()

    @pl.when(step == n_steps - 1)
    def _():
        for arr in range(n_arr):
            copy(arr, 0, _block_id((x, y), 1 - c), me).wait_recv()
            for j, chip in enumerate(chips):
                copy(arr, 4 + j, _block_id(chip, 1 - c), me).wait_recv()
            for k in range(4):
                copy(arr, k, my_id, me, shards[arr]).wait_send()
            for j, chip in enumerate(chips):
                copy(arr, 4 + j, _block_id(chip, c), me).wait_send()
            local(arr).wait()


def _host_pair_exchange(step, n_steps, gs, sibs, send_sems, recv_sems):
    x, y, c = _place()
    sibling = (x, y, 1 - c)
    chips = [(x, y)] + _other_chips(x, y)

    def d2d(arr, q):
        return _remote_copy(gs[arr].at[_block_id(chips[q], 1 - c)], sibs[arr].at[q],
                            send_sems.at[arr, q], recv_sems.at[arr, q], sibling)

    @pl.when(step == 0)
    def _():
        for arr in range(len(gs)):
            for q in (1, 2, 3, 0):
                d2d(arr, q).start()

    @pl.when(step == n_steps - 1)
    def _():
        for arr in range(len(gs)):
            for q in range(4):
                d2d(arr, q).wait()


def _host_chip_exchange(step, n_steps, hsends, hrecvs, send_sems, recv_sems):
    x, y, c = _place()
    chips = _other_chips(x, y)

    def ici(arr, j):
        return _remote_copy(hsends[arr].at[j], hrecvs[arr].at[j], send_sems.at[arr, j], recv_sems.at[arr, j],
                            (*chips[j], c))

    @pl.when(step == 0)
    def _():
        for arr in range(len(hsends)):
            for j in range(3):
                ici(arr, j).start()

    @pl.when(step == n_steps - 1)
    def _():
        for arr in range(len(hsends)):
            for j in range(3):
                ici(arr, j).wait()


def _host_half_exchange(step, n_steps, parts, sibs, send_sems, recv_sems):
    x, y, c = _place()
    n_q, rows2, _ = parts.shape
    half = rows2 // 2

    def d2d(q):
        src = parts.at[q, pl.ds(pl.multiple_of((1 - c) * half, 16), half), :]
        return _remote_copy(src, sibs.at[q], send_sems.at[q], recv_sems.at[q], (x, y, 1 - c))

    @pl.when(step == 0)
    def _():
        for q in range(n_q):
            d2d(q).start()

    @pl.when(step == n_steps - 1)
    def _():
        for q in range(n_q):
            d2d(q).wait()


def _peer(x, y, c, k):
    return (x ^ ((k >> 2) & 1), y ^ ((k >> 1) & 1), c ^ (k & 1))


def _host_small_exchange(step, n_steps, vec_m, vec_b, wab, vrecv_m, vrecv_b, wrecv, send_sems, recv_sems, local_sems):
    x, y, c = _place()
    my_id = _block_id((x, y), c)
    wrows = wab.shape[0] // N_DEV

    def copies(k):
        to = _peer(x, y, c, k)
        block = wab.at[pl.ds(pl.multiple_of(_block_id(to[0:2], to[2]) * wrows, SUB), wrows), :]
        return [_remote_copy(vec_m, vrecv_m.at[my_id], send_sems.at[0, k], recv_sems.at[0, k], to),
                _remote_copy(vec_b, vrecv_b.at[my_id], send_sems.at[1, k], recv_sems.at[1, k], to),
                _remote_copy(block, wrecv.at[k], send_sems.at[2, k], recv_sems.at[2, k], to)]

    mine = [pltpu.make_async_copy(vec_m, vrecv_m.at[my_id], local_sems.at[0]),
            pltpu.make_async_copy(vec_b, vrecv_b.at[my_id], local_sems.at[1])]

    @pl.when(step == 0)
    def _():
        for cp in mine:
            cp.start()
        for k in range(1, N_DEV):
            for cp in copies(k):
                cp.start()

    @pl.when(step == n_steps - 1)
    def _():
        for k in range(1, N_DEV):
            for cp in copies(k):
                cp.wait()
        for cp in mine:
            cp.wait()


def _pair_sum_parts(parts, sibs, core):
    n_q, rows2, cols = parts.shape
    half = rows2 // 2

    def body(core_ref, g_ref, s_ref, o_ref):
        o_ref[0] = (g_ref[0, 0].astype(F32) + s_ref[0].astype(F32)).astype(BF16)

    block = (1, half, cols)
    grid_spec = pltpu.PrefetchScalarGridSpec(
        num_scalar_prefetch=1, grid=(n_q,),
        in_specs=[pl.BlockSpec((1, 1, half, cols), lambda q, cr: (q, cr[0], 0, 0)),
                  pl.BlockSpec(block, lambda q, cr: (q, 0, 0))],
        out_specs=pl.BlockSpec(block, lambda q, cr: (q, 0, 0)))
    return pl.pallas_call(
        body, grid_spec=grid_spec, out_shape=pltpu.HBM((n_q, half, cols), BF16),
        compiler_params=_params(("arbitrary",), 32), name="pair_sum_w_in",
    )(core, *_in_hbm(parts.reshape(n_q, 2, half, cols), sibs))


def _pair_sum(gs, sibs, name):
    n_arr = len(gs)
    x, y, c = _place()
    slots = jnp.stack([_block_id(chip, c) for chip in [(x, y)] + _other_chips(x, y)]).astype(jnp.int32)

    def body(slots_ref, *refs):
        q = pl.program_id(0)
        for k in range(n_arr):
            g_ref, sib_ref = refs[2 * k:2 * k + 2]
            hs_ref, own_ref = refs[2 * n_arr + 2 * k:2 * n_arr + 2 * k + 2]
            both = g_ref[0].astype(F32) + sib_ref[0].astype(F32)

            @pl.when(q == 0)
            def _(own_ref=own_ref, both=both):
                own_ref[...] = both

            @pl.when(q > 0)
            def _(hs_ref=hs_ref, both=both):
                hs_ref[0] = both.astype(BF16)

    in_specs, out_specs, out_shape, args = [], [], [], []
    for g, sib in zip(gs, sibs):
        _, rows, cols = g.shape
        block = (1, rows, cols)
        in_specs += [pl.BlockSpec(block, lambda q, s: (s[q], 0, 0)), pl.BlockSpec(block, lambda q, s: (q, 0, 0))]
        out_specs += [pl.BlockSpec(block, lambda q, s: (jnp.maximum(q - 1, 0), 0, 0)),
                      pl.BlockSpec((rows, cols), lambda q, s: (0, 0))]
        out_shape += [pltpu.HBM((3, rows, cols), BF16), pltpu.HBM((rows, cols), F32)]
        args += _in_hbm(g, sib)
    grid_spec = pltpu.PrefetchScalarGridSpec(num_scalar_prefetch=1, grid=(4,), in_specs=in_specs, out_specs=out_specs)
    return pl.pallas_call(
        body, grid_spec=grid_spec, out_shape=out_shape,
        compiler_params=_params(("arbitrary",), 40), name=name,
    )(slots, *args)


def _exchange_scratch(n_arr, n_copies):
    return [pltpu.SemaphoreType.DMA((n_arr, n_copies)), pltpu.SemaphoreType.DMA((n_arr, n_copies))]


def _final_small(vrecv_m, vrecv_b, wab, wrecv, vec_x):
    wrows = wab.shape[0] // N_DEV

    def body(vm_ref, vb_ref, w_ref, wr_ref, vx_ref, o_vec, o_w, xrecv, wred, x_send, x_recv, b_send, b_recv):
        x, y, c = _place()
        my_id = _block_id((x, y), c)
        my_rows = pl.ds(pl.multiple_of(my_id * wrows, SUB), wrows)

        def xcopy(k):
            return _remote_copy(vx_ref, xrecv.at[my_id], x_send.at[k], x_recv.at[k], _peer(x, y, c, k))

        def bcopy(k):
            return _remote_copy(wred, o_w.at[my_rows, :], b_send.at[k], b_recv.at[k], _peer(x, y, c, k))

        xrecv[my_id] = vx_ref[...]
        for k in range(1, N_DEV):
            xcopy(k).start()
        red = w_ref[my_rows, :]
        for k in range(1, N_DEV):
            red = red + wr_ref[k]
        wred[...] = red
        o_w[my_rows, :] = red
        for k in range(1, N_DEV):
            bcopy(k).start()
        for k in range(1, N_DEV):
            xcopy(k).wait_recv()
        for rows, ref in ((slice(0, 8), vm_ref), (slice(8, 24), vb_ref), (slice(24, 32), xrecv)):
            tot = ref[0]
            for s in range(1, N_DEV):
                tot = tot + ref[s]
            o_vec[rows, :] = tot
        for k in range(1, N_DEV):
            bcopy(k).wait_recv()
        for k in range(1, N_DEV):
            xcopy(k).wait_send()
            bcopy(k).wait_send()

    vm = pl.BlockSpec(memory_space=pltpu.VMEM)
    dma8 = pltpu.SemaphoreType.DMA((N_DEV,))
    return pl.pallas_call(
        body, out_shape=(jax.ShapeDtypeStruct((VEC_ROWS, D_MODEL), F32), jax.ShapeDtypeStruct(wab.shape, F32)),
        in_specs=[vm] * 5, out_specs=[vm] * 2,
        scratch_shapes=[pltpu.VMEM((N_DEV, SUB, D_MODEL), F32), pltpu.VMEM((wrows, HEAD_DIM), F32),
                        dma8, dma8, dma8, dma8],
        compiler_params=_params(vmem_mib=32), name="final_small",
    )(vrecv_m, vrecv_b, wab, wrecv, vec_x)


def _in_proj(x, g_mix, shards, tm):
    t_len = x.shape[0]
    n_t = t_len // tm
    n_arr = len(shards)
    rows = [s.shape[0] for s in shards]
    width = 2 * rows[0]
    ax, ay = lax.axis_index("x"), lax.axis_index("y")
    order = jnp.stack([2 * cx + cy for cx, cy in [(ax, ay)] + _other_chips(ax, ay)]).astype(jnp.int32)

    def body(order_ref, x_ref, g_ref, *rest):
        shard_refs = rest[0:n_arr]
        u_ref, h_ref = rest[n_arr:n_arr + 2]
        fulls = rest[n_arr + 2:2 * n_arr + 2]
        h_s, wbuf, send_sems, recv_sems, local_sems, load_sem = rest[2 * n_arr + 2:]
        p = pl.program_id(0)
        i = pl.program_id(1)
        x_, y_, c = _place()
        me = (x_, y_, c)
        my_id = _block_id((x_, y_), c)
        sibling = (x_, y_, 1 - c)
        chips = _other_chips(x_, y_)

        def block(arr, blk):
            return fulls[arr].at[pl.ds(pl.multiple_of(blk * rows[arr], rows[arr]), rows[arr]), :]

        def copy(arr, k, blk, to, src=None):
            dst = block(arr, blk)
            return _remote_copy(dst if src is None else src, dst, send_sems.at[arr, k], recv_sems.at[arr, k], to)

        def local(arr):
            return pltpu.make_async_copy(shard_refs[arr], block(arr, my_id), local_sems.at[arr])

        def load_chip(chip, slot):
            start = pl.multiple_of((2 * chip[0] + chip[1]) * width, width)
            return pltpu.make_async_copy(fulls[0].at[pl.ds(start, width), :], wbuf.at[slot], load_sem.at[slot])

        def pass_on(j):
            for arr in range(n_arr):
                copy(arr, 1 + j, _block_id(chips[j], c), me).wait_recv()
                copy(arr, 4 + j, _block_id(chips[j], c), sibling).start()

        def complete(j):
            for arr in range(n_arr):
                copy(arr, 4 + j, _block_id(chips[j], 1 - c), me).wait_recv()

        @pl.when((p == 0) & (i == 0))
        def _():
            for arr in range(n_arr):
                local(arr).start()
                copy(arr, 0, my_id, sibling, shard_refs[arr]).start()
                for j in (0, 1):
                    copy(arr, 1 + j, my_id, (*chips[j], c), shard_refs[arr]).start()
            for arr in range(n_arr):
                local(arr).wait()
                copy(arr, 0, _block_id((x_, y_), 1 - c), me).wait_recv()
            load_chip((x_, y_), 0).start()
            load_chip((x_, y_), 0).wait()

        @pl.when((p == 1) & (i == 0))
        def _():
            pass_on(0)
            for arr in range(n_arr):
                copy(arr, 3, my_id, (*chips[2], c), shard_refs[arr]).start()
            pass_on(1)
            complete(0)
            load_chip(chips[0], 1).start()
            load_chip(chips[0], 1).wait()
            complete(1)
            load_chip(chips[1], 0).start()

        @pl.when((p == 2) & (i == 0))
        def _():
            load_chip(chips[1], 0).wait()

        @pl.when((p == 3) & (i == 0))
        def _():
            pass_on(2)
            complete(2)
            load_chip(chips[2], 1).start()
            load_chip(chips[2], 1).wait()

        @pl.when((p == 3) & (i == n_t - 1))
        def _():
            for arr in range(n_arr):
                for k in range(4):
                    copy(arr, k, my_id, me, shard_refs[arr]).wait_send()
                for j, chip in enumerate(chips):
                    copy(arr, 4 + j, _block_id(chip, c), me).wait_send()

        tile = pl.ds(pl.multiple_of(i * tm, tm), tm)

        @pl.when(p == 0)
        def _():
            xv = x_ref[...]
            h = (xv * _rms(xv) * g_ref[...]).astype(BF16)
            h_ref[...] = h
            h_s[tile, :] = h

        for slot in (0, 1):
            @pl.when(p % 2 == slot)
            def _(slot=slot):
                u_ref[...] = _dot_nt(h_s[tile, :], wbuf[slot])

    first_pass = lambda p, i, o: (jnp.where(p == 0, i, n_t - 1), 0)
    grid_spec = pltpu.PrefetchScalarGridSpec(
        num_scalar_prefetch=1, grid=(4, n_t),
        in_specs=[pl.BlockSpec((tm, D_MODEL), first_pass), pl.BlockSpec((1, D_MODEL), lambda p, i, o: (0, 0))]
        + [HBM_SPEC] * n_arr,
        out_specs=[pl.BlockSpec((tm, width), lambda p, i, o: (i, o[p])), pl.BlockSpec((tm, D_MODEL), first_pass)]
        + [HBM_SPEC] * n_arr,
        scratch_shapes=[pltpu.VMEM((t_len, D_MODEL), BF16), pltpu.VMEM((2, width, D_MODEL), BF16)]
        + _exchange_scratch(n_arr, 7) + [pltpu.SemaphoreType.DMA((n_arr,)), pltpu.SemaphoreType.DMA((2,))])
    return pl.pallas_call(
        body, grid_spec=grid_spec,
        out_shape=[jax.ShapeDtypeStruct((t_len, IN_COLS), F32), jax.ShapeDtypeStruct((t_len, D_MODEL), BF16)]
        + [jax.ShapeDtypeStruct((N_DEV * s.shape[0], s.shape[1]), s.dtype) for s in shards],
        compiler_params=_params(("arbitrary", "arbitrary"), 48), name="in_proj",
    )(order, x, g_mix, *shards)


def _conv3_chunk(u_ref, r, cv_prev, cw, row):
    gb = u_ref[pl.ds(r, SUB), OFF_GB:OFF_GB + CONV_WIDTH]
    gc = u_ref[pl.ds(r, SUB), OFF_GC:OFF_GC + CONV_WIDTH]
    v = u_ref[pl.ds(r, SUB), OFF_V:OFF_V + CONV_WIDTH]
    cv = gc * v
    cv_m1 = _down(cv, cv_prev, 1, row)
    cv_m2 = _down(cv, cv_prev, 2, row)
    cq = cw[2:3, :] * cv + cw[1:2, :] * cv_m1 + cw[0:1, :] * cv_m2
    return gb, gc, v, cv, cv_m1, cv_m2, cq


def _conv4_chunk(u_ref, r, xin_prev, rw, rb, row):
    xin = u_ref[pl.ds(r, SUB), OFF_XR:OFF_XR + LRU_WIDTH]
    m1 = _down(xin, xin_prev, 1, row)
    m2 = _down(xin, xin_prev, 2, row)
    m3 = _down(xin, xin_prev, 3, row)
    xr = rw[3:4, :] * xin + rw[2:3, :] * m1 + rw[1:2, :] * m2 + rw[0:1, :] * m3 + rb
    return xin, m1, m2, m3, xr


def _mixer_fwd(u, conv_w, rnn_conv_w, rnn_conv_b, wa, b_a, wx, b_x, lam, gnc, gnr, shards, tm):
    t_len = u.shape[0]
    n_steps = t_len // tm
    n_chunks = tm // SUB
    n_arr = len(shards)

    def body(u_ref, cw_ref, rw_ref, rb_ref, wa_ref, ba_ref, wx_ref, bx_ref, lam_ref, gnc_ref, gnr_ref, *rest):
        shard_refs = rest[0:n_arr]
        hs_ref, y_ref, xr_s, ra_ref, ii_ref, mult_ref, cq_ref = rest[n_arr:n_arr + 7]
        fulls = rest[n_arr + 7:2 * n_arr + 7]
        (y_s, pa_s, px_s, wabd, wxbd, cv_car, xin_car, h_car,
         send_sems, recv_sems, local_sems) = rest[2 * n_arr + 7:]
        _host_all_gather(pl.program_id(0), n_steps, shard_refs, fulls, send_sems, recv_sems, local_sems)

        @pl.when(pl.program_id(0) == 0)
        def _():
            cv_car[...] = jnp.zeros(cv_car.shape, F32)
            xin_car[...] = jnp.zeros(xin_car.shape, F32)
            h_car[...] = jnp.zeros(h_car.shape, F32)
            wabd[...] = _expand_heads(wa_ref[...])
            wxbd[...] = _expand_heads(wx_ref[...])

        row_c = lax.broadcasted_iota(jnp.int32, (SUB, CONV_WIDTH), 0)
        row_r = lax.broadcasted_iota(jnp.int32, (SUB, LRU_WIDTH), 0)
        cw = cw_ref[...]
        rw = rw_ref[...]
        rb = rb_ref[...]
        g_c = gnc_ref[...]
        g_r = gnr_ref[...]
        sp_c = LRU_C * _softplus_neg(lam_ref[...])

        def convs(i, carry):
            cv_prev, xin_prev = carry
            r = pl.multiple_of(i * SUB, SUB)
            gb, _, _, cv, _, _, cq = _conv3_chunk(u_ref, r, cv_prev, cw, row_c)
            cq_ref[pl.ds(r, SUB), :] = cq
            y_c = gb * cq
            y_s[pl.ds(r, SUB), 0:CONV_WIDTH] = y_c * _rms(y_c) * g_c
            xin, _, _, _, xr = _conv4_chunk(u_ref, r, xin_prev, rw, rb, row_r)
            xr_s[pl.ds(r, SUB), :] = xr
            return cv, xin

        cv_last, xin_last = _chunk_loop(n_chunks, convs, (cv_car[...], xin_car[...]))
        cv_car[...] = cv_last
        xin_car[...] = xin_last

        xrb = xr_s[...].astype(BF16)
        pa_s[...] = _block_diag_apply(xrb, wabd) + ba_ref[...]
        px_s[...] = _block_diag_apply(xrb, wxbd) + bx_ref[...]

        def recur(i, h_prev):
            r = pl.multiple_of(i * SUB, SUB)
            xr = xr_s[pl.ds(r, SUB), :]
            ra, ii, a, mult = _lru_gates(pa_s[pl.ds(r, SUB), :], px_s[pl.ds(r, SUB), :], sp_c)
            ra_ref[pl.ds(r, SUB), :] = ra
            ii_ref[pl.ds(r, SUB), :] = ii
            mult_ref[pl.ds(r, SUB), :] = mult
            a_cum, b_cum = _scan8_fwd(a, mult * ii * xr, row_r)
            h = a_cum * h_prev + b_cum
            hs_ref[pl.ds(r, SUB), :] = h
            ge, _ = _gelu(u_ref[pl.ds(r, SUB), OFF_G:OFF_G + LRU_WIDTH])
            y_r = h * ge
            y_s[pl.ds(r, SUB), CONV_WIDTH:MIX_WIDTH] = y_r * _rms(y_r) * g_r
            return h[SUB - 1:SUB, :]

        h_car[...] = _chunk_loop(n_chunks, recur, h_car[...])

        y_ref[...] = y_s[...].astype(BF16)

    row_tile = lambda w: pl.BlockSpec((tm, w), lambda i: (i, 0))
    whole = lambda a: pl.BlockSpec(a.shape, lambda i: (0,) * a.ndim)
    smalls = (conv_w, rnn_conv_w, rnn_conv_b, wa, b_a, wx, b_x, lam, gnc, gnr)
    return pl.pallas_call(
        body, grid=(n_steps,),
        in_specs=[row_tile(IN_COLS)] + [whole(a) for a in smalls] + [HBM_SPEC] * n_arr,
        out_specs=[row_tile(LRU_WIDTH), row_tile(MIX_WIDTH)] + [row_tile(LRU_WIDTH)] * 4 + [row_tile(CONV_WIDTH)]
        + [HBM_SPEC] * n_arr,
        out_shape=[jax.ShapeDtypeStruct((t_len, LRU_WIDTH), F32), jax.ShapeDtypeStruct((t_len, MIX_WIDTH), BF16)]
        + [jax.ShapeDtypeStruct((t_len, LRU_WIDTH), F32)] * 4 + [jax.ShapeDtypeStruct((t_len, CONV_WIDTH), F32)]
        + [jax.ShapeDtypeStruct((N_DEV,) + s.shape, BF16) for s in shards],
        scratch_shapes=[pltpu.VMEM((tm, MIX_WIDTH), F32),
                        pltpu.VMEM((tm, LRU_WIDTH), F32), pltpu.VMEM((tm, LRU_WIDTH), F32),
                        pltpu.VMEM((LRU_WIDTH, GROUP), BF16), pltpu.VMEM((LRU_WIDTH, GROUP), BF16),
                        pltpu.VMEM((SUB, CONV_WIDTH), F32), pltpu.VMEM((SUB, LRU_WIDTH), F32),
                        pltpu.VMEM((1, LRU_WIDTH), F32)]
        + _exchange_scratch(n_arr, 7) + [pltpu.SemaphoreType.DMA((n_arr,))],
        compiler_params=_params(("arbitrary",), 56), name="mixer_fwd",
    )(u, *smalls, *shards)


def _mlp_up(x, y, g_mlp, w_out, w1, w2_shard, tm):
    t_len = x.shape[0]
    n_steps = t_len // tm
    n_blk, _, blk = w1.shape

    def body(x_ref, y_ref, gm_ref, wout_hbm, w1_hbm, w2_ref, x1_ref, h2_ref, z_ref, w2_full,
             wout_s, w1_s, sem, send_sems, recv_sems, local_sems):
        step = pl.program_id(0)
        _host_all_gather(step, n_steps, [w2_ref], [w2_full], send_sems, recv_sems, local_sems)

        load_wout = pltpu.make_async_copy(wout_hbm, wout_s, sem.at[0])
        load_w1 = pltpu.make_async_copy(w1_hbm, w1_s, sem.at[1])

        @pl.when(step == 0)
        def _():
            load_wout.start()
            load_w1.start()
            load_wout.wait()

        x1v = x_ref[...] + jnp.dot(y_ref[...], wout_s[...], preferred_element_type=F32)
        x1_ref[...] = x1v
        h2 = (x1v * _rms(x1v) * gm_ref[...]).astype(BF16)
        h2_ref[...] = h2

        @pl.when(step == 0)
        def _():
            load_w1.wait()

        for k in range(n_blk):
            rp = jnp.maximum(jnp.dot(h2, w1_s[k], preferred_element_type=F32), 0.0)
            z_ref[:, k * blk:(k + 1) * blk] = (rp * rp).astype(BF16)

    row_tile = lambda w: pl.BlockSpec((tm, w), lambda i: (i, 0))
    return pl.pallas_call(
        body, grid=(n_steps,),
        in_specs=[row_tile(D_MODEL), row_tile(MIX_WIDTH), pl.BlockSpec((1, D_MODEL), lambda i: (0, 0)),
                  HBM_SPEC, HBM_SPEC, HBM_SPEC],
        out_specs=[row_tile(D_MODEL), row_tile(D_MODEL), row_tile(D_FF), HBM_SPEC],
        out_shape=[jax.ShapeDtypeStruct((t_len, D_MODEL), F32), jax.ShapeDtypeStruct((t_len, D_MODEL), BF16),
                   jax.ShapeDtypeStruct((t_len, D_FF), BF16), jax.ShapeDtypeStruct((N_DEV,) + w2_shard.shape, BF16)],
        scratch_shapes=[pltpu.VMEM(w_out.shape, BF16), pltpu.VMEM(w1.shape, BF16), pltpu.SemaphoreType.DMA((2,))]
        + _exchange_scratch(1, 7) + [pltpu.SemaphoreType.DMA((1,))],
        compiler_params=_params(("arbitrary",), 48), name="mlp_up",
    )(x, y, g_mlp, w_out, w1, w2_shard)


def _mlp_down_bwd(x1, z, target, g_mlp, g_f, w1, w2, tm):
    t_len = x1.shape[0]
    n_steps = t_len // tm
    n_blk, _, blk = w1.shape

    def body(x1_ref, z_ref, tg_ref, gm_ref, gf_ref, w1_hbm, w2_hbm, dx1_ref, dx2_ref, vec_ref, dpre_hbm,
             w1_s, w2_s, dp_s, sem, out_sem):
        step = pl.program_id(0)
        rows = pl.ds(pl.multiple_of(step * tm, tm), tm)
        dp_out = pltpu.make_async_copy(dp_s, dpre_hbm.at[rows, :], out_sem.at[0])

        load_w1 = pltpu.make_async_copy(w1_hbm, w1_s, sem.at[0])
        load_w2 = pltpu.make_async_copy(w2_hbm, w2_s, sem.at[1])

        @pl.when(step == 0)
        def _():
            load_w2.start()
            load_w1.start()
            vec_ref[...] = jnp.zeros(vec_ref.shape, F32)
            load_w2.wait()

        x1v = x1_ref[...]
        g_m = gm_ref[...]
        g_o = gf_ref[...]
        r2 = _rms(x1v)
        x1h = x1v * r2
        x2 = x1v + jnp.dot(z_ref[...], w2_s[...], preferred_element_type=F32)
        r3 = _rms(x2)
        x2h = x2 * r3
        err = x2h * g_o - tg_ref[...]
        dout = err * (1.0 / D_MODEL)
        vec_ref[ROW_LOSS:ROW_LOSS + 1, :] += (0.5 / D_MODEL) * jnp.sum(err * err, axis=0, keepdims=True)
        vec_ref[ROW_GF:ROW_GF + 1, :] += jnp.sum(dout * x2h, axis=0, keepdims=True)
        dx2 = _rms_bwd(dout, x2h, r3, g_o)
        dx2b = dx2.astype(BF16)
        dx2_ref[...] = dx2b
        dh2 = jnp.zeros((tm, D_MODEL), F32)

        @pl.when(step > 0)
        def _():
            dp_out.wait()

        @pl.when(step == 0)
        def _():
            load_w1.wait()

        for k in range(n_blk):
            cols = slice(k * blk, (k + 1) * blk)
            dz = _dot_nt(dx2b, w2_s[cols, :])
            dpb = (dz * 2.0 * jnp.sqrt(z_ref[:, cols].astype(F32))).astype(BF16)
            dp_s[:, cols] = dpb
            dh2 = dh2 + _dot_nt(dpb, w1_s[k])
        dp_out.start()
        vec_ref[ROW_GMLP:ROW_GMLP + 1, :] += jnp.sum(dh2 * x1h, axis=0, keepdims=True)
        dx1_ref[...] = dx2 + _rms_bwd(dh2, x1h, r2, g_m)

        @pl.when(step == n_steps - 1)
        def _():
            dp_out.wait()

    row_tile = lambda w: pl.BlockSpec((tm, w), lambda i: (i, 0))
    vec_spec = pl.BlockSpec((1, D_MODEL), lambda i: (0, 0))
    return pl.pallas_call(
        body, grid=(n_steps,),
        in_specs=[row_tile(D_MODEL), row_tile(D_FF), row_tile(D_MODEL), vec_spec, vec_spec, HBM_SPEC, HBM_SPEC],
        out_specs=[row_tile(D_MODEL), row_tile(D_MODEL), pl.BlockSpec((SUB, D_MODEL), lambda i: (0, 0)), HBM_SPEC],
        out_shape=[jax.ShapeDtypeStruct((t_len, D_MODEL), F32), jax.ShapeDtypeStruct((t_len, D_MODEL), BF16),
                   jax.ShapeDtypeStruct((SUB, D_MODEL), F32), jax.ShapeDtypeStruct((t_len, D_FF), BF16)],
        scratch_shapes=[pltpu.VMEM(w1.shape, BF16), pltpu.VMEM(w2.shape, BF16), pltpu.VMEM((tm, D_FF), BF16),
                        pltpu.SemaphoreType.DMA((2,)), pltpu.SemaphoreType.DMA((1,))],
        compiler_params=_params(("arbitrary",), 56), name="mlp_down_bwd",
    )(x1, z, target, g_mlp, g_f, w1, w2)


def _mixer_bwd(u, hs, dx1, saved, conv_w, rnn_conv_w, wa, wx, lam, gnc, gnr, w_out, chip_sums, g_wout, tm):
    t_len = u.shape[0]
    n_tiles = t_len // tm
    n_chunks = tm // SUB
    per_tile = tm // SUB
    n_sums = len(chip_sums)

    def body(u_ref, hs_ref, hp_ref, dx1_ref, xr_ref, ra_ref, ii_ref, mult_ref, cq_ref,
             cw_ref, rw_ref, wa_ref, wx_ref, lam_ref, gnc_ref, gnr_ref, wout_ref, *rest):
        hsends = rest[0:n_sums]
        gwout_ref = rest[n_sums]
        du_ref, vec_ref, wab_ref = rest[n_sums + 1:n_sums + 4]
        hrecvs = rest[n_sums + 4:2 * n_sums + 4]
        sib_wout = rest[2 * n_sums + 4]
        (du_s, dy_s, dpa_s, dpx_s, dxr_s, wabd, wxbd, acc, dwa_acc, dwx_acc,
         a_car, dh_car, dcq_car, dxr_car, i_send, i_recv, d_send, d_recv) = rest[2 * n_sums + 5:]
        step = pl.program_id(0)
        _host_chip_exchange(step, n_tiles, hsends, hrecvs, i_send, i_recv)
        _host_pair_exchange(step, n_tiles, [gwout_ref], [sib_wout], d_send, d_recv)
        has_prev = (step < n_tiles - 1).astype(F32)

        @pl.when(step == 0)
        def _():
            acc[...] = jnp.zeros(acc.shape, F32)
            dwa_acc[...] = jnp.zeros(dwa_acc.shape, F32)
            dwx_acc[...] = jnp.zeros(dwx_acc.shape, F32)
            a_car[...] = jnp.ones(a_car.shape, F32)
            dh_car[...] = jnp.zeros(dh_car.shape, F32)
            dcq_car[...] = jnp.zeros(dcq_car.shape, F32)
            dxr_car[...] = jnp.zeros(dxr_car.shape, F32)
            wabd[...] = _expand_heads(wa_ref[...])
            wxbd[...] = _expand_heads(wx_ref[...])

        row_c = lax.broadcasted_iota(jnp.int32, (SUB, CONV_WIDTH), 0)
        row_r = lax.broadcasted_iota(jnp.int32, (SUB, LRU_WIDTH), 0)
        cw = cw_ref[...]
        rw = rw_ref[...]
        g_c = gnc_ref[...]
        g_r = gnr_ref[...]
        sp_c = LRU_C * _softplus_neg(lam_ref[...])

        hs_before = hp_ref[...] * has_prev

        dy_s[...] = _dot_nt(dx1_ref[...].astype(BF16), wout_ref[...])

        def recur_bwd(j, carry):
            a_later, dh_later = carry
            i = n_chunks - 1 - j
            r = pl.multiple_of(i * SUB, SUB)
            rp = pl.multiple_of(jnp.maximum(i - 1, 0) * SUB, SUB)
            xr = xr_ref[pl.ds(r, SUB), :]
            hs_c = hs_ref[pl.ds(r, SUB), :]
            hs_prev = jnp.where(i == 0, hs_before, hs_ref[pl.ds(rp, SUB), :])
            h_m1 = _down(hs_c, hs_prev, 1, row_r)
            ra = ra_ref[pl.ds(r, SUB), :]
            ii = ii_ref[pl.ds(r, SUB), :]
            mult = mult_ref[pl.ds(r, SUB), :]
            a = jnp.exp(-ra * sp_c)
            inv_mult = lax.rsqrt(mult * mult)
            ge, dge = _gelu(u_ref[pl.ds(r, SUB), OFF_G:OFF_G + LRU_WIDTH])
            y_r = hs_c * ge
            rr = _rms(y_r)
            yhat = y_r * rr
            dyn = dy_s[pl.ds(r, SUB), CONV_WIDTH:MIX_WIDTH]
            acc[ACC_GNR] += dyn * yhat
            dy_r = _rms_bwd(dyn, yhat, rr, g_r)
            du_s[pl.ds(r, SUB), OFF_G:OFF_G + LRU_WIDTH] = dy_r * hs_c * dge
            a_cum, d_cum = _scan8_rev(_up(a, a_later, 1, row_r), dy_r * ge, row_r)
            dh = a_cum * dh_later + d_cum
            dm = dh * mult
            dii = dm * xr
            dxr_s[pl.ds(r, SUB), :] = dm * ii
            dla = a * dh * (h_m1 - (ii * xr) * a * inv_mult)
            dla_r = dla * ra
            acc[ACC_SP] -= dla_r
            dpa = dla_r * (sp_c * (ra - 1.0))
            dpx = dii * ii * (1.0 - ii)
            acc[ACC_BA] += dpa
            acc[ACC_BX] += dpx
            dpa_s[pl.ds(r, SUB), :] = dpa
            dpx_s[pl.ds(r, SUB), :] = dpx
            return a, dh[0:1, :]

        a_first, dh_first = _chunk_loop(n_chunks, recur_bwd, (a_car[...], dh_car[...]))
        a_car[...] = a_first
        dh_car[...] = dh_first

        for g in range(LRU_WIDTH // GROUP):
            cols = slice(g * GROUP, (g + 1) * GROUP)
            dpab = dpa_s[:, cols].astype(BF16)
            dpxb = dpx_s[:, cols].astype(BF16)
            xrb = xr_ref[:, cols].astype(BF16)
            dxr_s[:, cols] += _dot_nt(dpab, wabd[cols, :]) + _dot_nt(dpxb, wxbd[cols, :])
            dwa_acc[cols, :] += _dot_tn(xrb, dpab)
            dwx_acc[cols, :] += _dot_tn(xrb, dpxb)

        def convs_bwd(j, carry):
            dcq_later, dxr_later = carry
            i = n_chunks - 1 - j
            r = pl.multiple_of(i * SUB, SUB)
            gb = u_ref[pl.ds(r, SUB), OFF_GB:OFF_GB + CONV_WIDTH]
            gc = u_ref[pl.ds(r, SUB), OFF_GC:OFF_GC + CONV_WIDTH]
            v = u_ref[pl.ds(r, SUB), OFF_V:OFF_V + CONV_WIDTH]
            cv = gc * v
            cq = cq_ref[pl.ds(r, SUB), :]
            y_c = gb * cq
            rc = _rms(y_c)
            yhat = y_c * rc
            dyn = dy_s[pl.ds(r, SUB), 0:CONV_WIDTH]
            acc[ACC_GNC, :, 0:CONV_WIDTH] += dyn * yhat
            dy_c = _rms_bwd(dyn, yhat, rc, g_c)
            dcq = dy_c * gb
            ahead3 = [dcq, _up(dcq, dcq_later, 1, row_c), _up(dcq, dcq_later, 2, row_c)]
            dcv = cw[2:3, :] * ahead3[0] + cw[1:2, :] * ahead3[1] + cw[0:1, :] * ahead3[2]
            for k in range(3):
                acc[ACC_CW + 2 - k, :, 0:CONV_WIDTH] += ahead3[k] * cv
            du_s[pl.ds(r, SUB), OFF_GB:OFF_GB + CONV_WIDTH] = dy_c * cq
            du_s[pl.ds(r, SUB), OFF_GC:OFF_GC + CONV_WIDTH] = dcv * v
            du_s[pl.ds(r, SUB), OFF_V:OFF_V + CONV_WIDTH] = dcv * gc

            xin = u_ref[pl.ds(r, SUB), OFF_XR:OFF_XR + LRU_WIDTH]
            dxr = dxr_s[pl.ds(r, SUB), :]
            ahead = [dxr] + [_up(dxr, dxr_later, k, row_r) for k in (1, 2, 3)]
            du_s[pl.ds(r, SUB), OFF_XR:OFF_XR + LRU_WIDTH] = (
                rw[3:4, :] * ahead[0] + rw[2:3, :] * ahead[1] + rw[1:2, :] * ahead[2] + rw[0:1, :] * ahead[3])
            for k in range(4):
                acc[ACC_RW + 3 - k] += ahead[k] * xin
            acc[ACC_BR] += dxr
            return dcq, dxr

        dcq_first, dxr_first = _chunk_loop(n_chunks, convs_bwd, (dcq_car[...], dxr_car[...]))
        dcq_car[...] = dcq_first
        dxr_car[...] = dxr_first

        du_ref[...] = du_s[...].astype(BF16)

        @pl.when(step == n_tiles - 1)
        def _():
            vec_ref[...] = jnp.zeros(vec_ref.shape, F32)
            rows = {ACC_GNC: ROW_GNC, ACC_GNR: ROW_GNR, ACC_BR: ROW_BR, ACC_BA: ROW_BA, ACC_BX: ROW_BX}
            for k in range(3):
                rows[ACC_CW + k] = ROW_CW + k
            for k in range(4):
                rows[ACC_RW + k] = ROW_RW + k
            for slot, out_row in rows.items():
                o = out_row - ROW_GNC
                vec_ref[o:o + 1, :] = jnp.sum(acc[slot], axis=0, keepdims=True)
            lam_v = lam_ref[...]
            dsp = jnp.sum(acc[ACC_SP], axis=0, keepdims=True)
            o = ROW_LAM - ROW_GNC
            vec_ref[o:o + 1, :] = -dsp * LRU_C / (1.0 + jnp.exp(lam_v))
            wab_ref[0:LRU_WIDTH, :] = _fold_heads(dwa_acc[...])
            wab_ref[LRU_WIDTH:2 * LRU_WIDTH, :] = _fold_heads(dwx_acc[...])

    rev = lambda w: pl.BlockSpec((tm, w), lambda s: (n_tiles - 1 - s, 0))
    before = lambda w: pl.BlockSpec((SUB, w), lambda s: (jnp.maximum((n_tiles - 1 - s) * per_tile - 1, 0), 0))
    whole = lambda a: pl.BlockSpec(a.shape, lambda s: (0,) * a.ndim)
    smalls = (conv_w, rnn_conv_w, wa, wx, lam, gnc, gnr, w_out)
    full = lambda w: pltpu.VMEM((tm, w), F32)
    return pl.pallas_call(
        body, grid=(n_tiles,),
        in_specs=[rev(IN_COLS), rev(LRU_WIDTH), before(LRU_WIDTH), rev(D_MODEL)]
        + [rev(a.shape[1]) for a in saved] + [whole(a) for a in smalls] + [HBM_SPEC] * (n_sums + 1),
        out_specs=[rev(IN_COLS), pl.BlockSpec((16, D_MODEL), lambda s: (0, 0)),
                   pl.BlockSpec((2 * LRU_WIDTH, HEAD_DIM), lambda s: (0, 0))] + [HBM_SPEC] * (n_sums + 1),
        out_shape=[jax.ShapeDtypeStruct((t_len, IN_COLS), BF16), jax.ShapeDtypeStruct((16, D_MODEL), F32),
                   jax.ShapeDtypeStruct((2 * LRU_WIDTH, HEAD_DIM), F32)]
        + [jax.ShapeDtypeStruct(s.shape, BF16) for s in chip_sums]
        + [jax.ShapeDtypeStruct((4,) + g_wout.shape[1:], BF16)],
        scratch_shapes=[full(IN_COLS), full(MIX_WIDTH), full(LRU_WIDTH), full(LRU_WIDTH), full(LRU_WIDTH),
                        pltpu.VMEM((LRU_WIDTH, GROUP), BF16), pltpu.VMEM((LRU_WIDTH, GROUP), BF16),
                        pltpu.VMEM((N_ACC, SUB, LRU_WIDTH), F32),
                        pltpu.VMEM((LRU_WIDTH, GROUP), F32), pltpu.VMEM((LRU_WIDTH, GROUP), F32),
                        pltpu.VMEM((SUB, LRU_WIDTH), F32), pltpu.VMEM((1, LRU_WIDTH), F32),
                        pltpu.VMEM((SUB, CONV_WIDTH), F32), pltpu.VMEM((SUB, LRU_WIDTH), F32)]
        + _exchange_scratch(n_sums, 3) + _exchange_scratch(1, 4),
        compiler_params=_params(("arbitrary",), 56), name="mixer_bwd",
    )(u, hs, hs, dx1, *saved, *smalls, *chip_sums, g_wout)


def _in_proj_bwd(du, dx1, x, g_mix, win_t, tm, chip_sums, g_own):
    t_len = x.shape[0]
    n_steps = t_len // tm

    def body(du_ref, dx1_ref, x_ref, g_ref, w_ref, hs_ref, gown_ref,
             dx_ref, vec_ref, landed_ref, sib_ref, i_send, i_recv, d_send, d_recv):
        step = pl.program_id(0)
        _host_chip_exchange(step, n_steps, [hs_ref], [landed_ref], i_send, i_recv)
        _host_half_exchange(step, n_steps, gown_ref, sib_ref, d_send, d_recv)

        @pl.when(step == 0)
        def _():
            vec_ref[...] = jnp.zeros(vec_ref.shape, F32)

        dh = jnp.dot(du_ref[...], w_ref[...], preferred_element_type=F32)
        xv = x_ref[...]
        r1 = _rms(xv)
        xh = xv * r1
        vec_ref[0:1, :] += jnp.sum(dh * xh, axis=0, keepdims=True)
        dx_ref[...] = dx1_ref[...] + _rms_bwd(dh, xh, r1, g_ref[...])

    row_tile = lambda w: pl.BlockSpec((tm, w), lambda i: (i, 0))
    half_shape = (g_own.shape[0], g_own.shape[1] // 2, g_own.shape[2])
    return pl.pallas_call(
        body, grid=(n_steps,),
        in_specs=[row_tile(IN_COLS), row_tile(D_MODEL), row_tile(D_MODEL), pl.BlockSpec((1, D_MODEL), lambda i: (0, 0)),
                  pl.BlockSpec((IN_COLS, D_MODEL), lambda i: (0, 0))] + [HBM_SPEC] * 2,
        out_specs=[row_tile(D_MODEL), pl.BlockSpec((SUB, D_MODEL), lambda i: (0, 0))] + [HBM_SPEC] * 2,
        out_shape=[jax.ShapeDtypeStruct((t_len, D_MODEL), F32), jax.ShapeDtypeStruct((SUB, D_MODEL), F32),
                   jax.ShapeDtypeStruct(chip_sums.shape, BF16), jax.ShapeDtypeStruct(half_shape, BF16)],
        scratch_shapes=_exchange_scratch(1, 3) + [pltpu.SemaphoreType.DMA((1,)), pltpu.SemaphoreType.DMA((1,))],
        compiler_params=_params(("arbitrary",), 56), name="in_proj_bwd",
    )(du, dx1, x, g_mix, win_t, chip_sums, g_own)


def _tn_weight_grad(a, b, tk, name, pair=(), col_blocks=1):
    t_len, m = a.shape
    n = b.shape[1]
    n_steps = t_len // tk
    sent = tuple(pair)
    n_sent = len(sent)

    def body(a_ref, b_ref, *rest):
        srcs = rest[0:n_sent]
        o_ref = rest[n_sent]
        dsts = rest[n_sent + 1:2 * n_sent + 1]
        acc = rest[2 * n_sent + 1]
        sems = rest[2 * n_sent + 2:]
        j = pl.program_id(0)
        if pair:
            _host_pair_exchange(j, n_steps, srcs, dsts, *sems)

        @pl.when(j == 0)
        def _():
            acc[...] = jnp.zeros(acc.shape, F32)

        acc[...] += _dot_tn(a_ref[...].astype(BF16), b_ref[...].astype(BF16))

        @pl.when(j == n_steps - 1)
        def _():
            if col_blocks == 1:
                o_ref[...] = acc[...].astype(BF16)
            else:
                for k in range(col_blocks):
                    o_ref[k] = acc[:, k * nb:(k + 1) * nb].astype(BF16)

    nb = n // col_blocks
    out_dims = (m, n) if col_blocks == 1 else (col_blocks, m, nb)
    landed = [jax.ShapeDtypeStruct((4,) + g.shape[1:], BF16) for g in pair]
    scratch = [pltpu.VMEM((m, n), F32)]
    if n_sent:
        scratch += _exchange_scratch(n_sent, 4)
    return pl.pallas_call(
        body, grid=(n_steps,),
        in_specs=[pl.BlockSpec((tk, m), lambda j: (j, 0)), pl.BlockSpec((tk, n), lambda j: (j, 0))]
        + [HBM_SPEC] * n_sent,
        out_specs=[pl.BlockSpec(out_dims, lambda j: (0,) * len(out_dims))] + [HBM_SPEC] * n_sent,
        out_shape=[jax.ShapeDtypeStruct(out_dims, BF16)] + landed,
        scratch_shapes=scratch,
        compiler_params=_params(("arbitrary",), 56), name=name,
    )(a, b, *sent)


def _w_in_grad_part(du, h, tk, name, chip_ids, chip=(), halves=None, small=None):
    t_len = du.shape[0]
    n_t = t_len // tk
    n_q = chip_ids.shape[0]
    width = 2 * (IN_COLS // N_DEV)
    n_steps = n_q * n_t
    n_chip = len(chip)
    sent = tuple(chip) + (() if halves is None else (halves,)) + (() if small is None else tuple(small))
    n_sent = len(sent)

    def body(ids_ref, a_ref, b_ref, *rest):
        srcs = rest[0:n_sent]
        o_ref = rest[n_sent]
        dsts = rest[n_sent + 1:2 * n_sent + 1]
        acc = rest[2 * n_sent + 1]
        sems = list(rest[2 * n_sent + 2:])
        j = pl.program_id(1)
        step = pl.program_id(0) * n_t + j
        if chip:
            _host_chip_exchange(step, n_steps, srcs[0:n_chip], dsts[0:n_chip], sems.pop(0), sems.pop(0))
        if halves is not None:
            _host_half_exchange(step, n_steps, srcs[n_chip], dsts[n_chip], sems.pop(0), sems.pop(0))
        if small is not None:
            _host_small_exchange(step, n_steps, *srcs[n_sent - 3:], *dsts[n_sent - 3:], *sems)

        @pl.when(j == 0)
        def _():
            acc[...] = jnp.zeros(acc.shape, F32)

        acc[...] += _dot_tn(a_ref[...], b_ref[...])

        @pl.when(j == n_t - 1)
        def _():
            o_ref[0] = acc[...].astype(BF16)

    landed = [jax.ShapeDtypeStruct(s.shape, BF16) for s in chip]
    scratch = [pltpu.VMEM((width, D_MODEL), F32)]
    if chip:
        scratch += _exchange_scratch(len(chip), 3)
    if halves is not None:
        landed.append(jax.ShapeDtypeStruct((halves.shape[0], halves.shape[1] // 2, halves.shape[2]), BF16))
        scratch += [pltpu.SemaphoreType.DMA((halves.shape[0],)), pltpu.SemaphoreType.DMA((halves.shape[0],))]
    if small is not None:
        vec_m, vec_b, wab = small
        landed += [jax.ShapeDtypeStruct((N_DEV,) + vec_m.shape, F32), jax.ShapeDtypeStruct((N_DEV,) + vec_b.shape, F32),
                   jax.ShapeDtypeStruct((N_DEV, wab.shape[0] // N_DEV, wab.shape[1]), F32)]
        scratch += _exchange_scratch(3, N_DEV) + [pltpu.SemaphoreType.DMA((2,))]
    grid_spec = pltpu.PrefetchScalarGridSpec(
        num_scalar_prefetch=1, grid=(n_q, n_t),
        in_specs=[pl.BlockSpec((tk, width), lambda q, j, ids: (j, ids[q])),
                  pl.BlockSpec((tk, D_MODEL), lambda q, j, ids: (j, 0))] + [HBM_SPEC] * n_sent,
        out_specs=[pl.BlockSpec((1, width, D_MODEL), lambda q, j, ids: (q, 0, 0))] + [HBM_SPEC] * n_sent,
        scratch_shapes=scratch)
    return pl.pallas_call(
        body, grid_spec=grid_spec, out_shape=[jax.ShapeDtypeStruct((n_q, width, D_MODEL), BF16)] + landed,
        compiler_params=_params(("arbitrary", "arbitrary"), 40), name=name,
    )(chip_ids, du, h, *sent)


def _adamw(w, g, m, v):
    m = ADAM_B1 * m + (1.0 - ADAM_B1) * g
    v = ADAM_B2 * v + (1.0 - ADAM_B2) * (g * g)
    delta = -ADAM_LR * ((m / BC1) / (jnp.sqrt(v / BC2) + ADAM_EPS) + ADAM_WD * w)
    return delta, m, v


def _update_sharded(g, landed, w, m, v, rows_blk, name):
    rows, cols = w.shape

    def body(g_ref, l_ref, w_ref, m_ref, v_ref, og, od, om, ov):
        gv = g_ref[...]
        for j in range(3):
            gv = gv + l_ref[j].astype(F32)
        delta, mn, vn = _adamw(w_ref[...], gv, m_ref[...], v_ref[...])
        og[...] = gv
        od[...] = delta
        om[...] = mn
        ov[...] = vn

    blk = pl.BlockSpec((rows_blk, cols), lambda i: (i, 0))
    shape = pltpu.HBM((rows, cols), F32)
    return pl.pallas_call(
        body, grid=(rows // rows_blk,),
        in_specs=[blk, pl.BlockSpec((3, rows_blk, cols), lambda i: (0, i, 0)), blk, blk, blk],
        out_specs=[blk] * 4, out_shape=[shape] * 4,
        compiler_params=_params(("arbitrary",), 32), name=name,
    )(*_in_hbm(g, landed, w, m, v))


def _update_w_in(g_own, sib_own, landed, w_t, m_t, v_t, core, cols_blk):
    rows, cols = w_t.shape

    def body(core_ref, g_ref, s_ref, l_ref, w_ref, m_ref, v_ref, og, od, om, ov):
        gv = g_ref[0, 0].astype(F32) + s_ref[0].astype(F32)
        for j in range(3):
            gv = gv + l_ref[j].astype(F32)
        delta, mn, vn = _adamw(w_ref[...], gv, m_ref[...], v_ref[...])
        og[...] = gv
        od[...] = delta
        om[...] = mn
        ov[...] = vn

    blk = pl.BlockSpec((rows, cols_blk), lambda i, cr: (0, i))
    grid_spec = pltpu.PrefetchScalarGridSpec(
        num_scalar_prefetch=1, grid=(cols // cols_blk,),
        in_specs=[pl.BlockSpec((1, 1, rows, cols_blk), lambda i, cr: (0, cr[0], 0, i)),
                  pl.BlockSpec((1, rows, cols_blk), lambda i, cr: (0, 0, i)),
                  pl.BlockSpec((3, rows, cols_blk), lambda i, cr: (0, 0, i)), blk, blk, blk],
        out_specs=[blk] * 4)
    return pl.pallas_call(
        body, grid_spec=grid_spec, out_shape=[pltpu.HBM((rows, cols), F32)] * 4,
        compiler_params=_params(("arbitrary",), 32), name="update_w_in",
    )(core, *_in_hbm(g_own.reshape(1, 2, rows, cols), sib_own, landed, w_t, m_t, v_t))


def _update_small(vsum, wsum, g_cw, g_rw, weights, moments_m, moments_v):
    n = len(weights)

    def body(*refs):
        vs, ws, gcw, grw = refs[0:4]
        w_refs = refs[4:4 + n]
        m_refs = refs[4 + n:4 + 2 * n]
        v_refs = refs[4 + 2 * n:4 + 3 * n]
        outs = refs[4 + 3 * n:]
        loss_ref = outs[0]
        loss_ref[...] = jnp.sum(vs[ROW_LOSS:ROW_LOSS + 1, :], axis=1, keepdims=True)
        grads = [
            vs[ROW_GMIX:ROW_GMIX + 1, :], gcw[...], grw[...], vs[ROW_BR:ROW_BR + 1, :],
            ws[0:LRU_WIDTH, :], vs[ROW_BA:ROW_BA + 1, :], ws[LRU_WIDTH:2 * LRU_WIDTH, :], vs[ROW_BX:ROW_BX + 1, :],
            vs[ROW_LAM:ROW_LAM + 1, :], vs[ROW_GNC:ROW_GNC + 1, 0:CONV_WIDTH], vs[ROW_GNR:ROW_GNR + 1, :],
            vs[ROW_GMLP:ROW_GMLP + 1, :], vs[ROW_GF:ROW_GF + 1, :],
        ]
        for k in range(n):
            gk = grads[k]
            delta, mn, vn = _adamw(w_refs[k][...], gk, m_refs[k][...], v_refs[k][...])
            outs[1 + 4 * k][...] = gk
            outs[2 + 4 * k][...] = delta
            outs[3 + 4 * k][...] = mn
            outs[4 + 4 * k][...] = vn

    whole = lambda a: pl.BlockSpec(a.shape, lambda i: (0,) * len(a.shape))
    out_shape = [jax.ShapeDtypeStruct((1, 1), F32)]
    for w in weights:
        out_shape += [jax.ShapeDtypeStruct(w.shape, F32)] * 4
    args = (vsum, wsum, g_cw, g_rw, *weights, *moments_m, *moments_v)
    return pl.pallas_call(
        body, grid=(1,), out_shape=out_shape, in_specs=[whole(a) for a in args], out_specs=[whole(s) for s in out_shape],
        compiler_params=_params(("arbitrary",), 32), name="update_small",
    )(*args)


def kernel(x, norm_mix_g, w_in, conv_w, rnn_conv_w, rnn_conv_b, w_a, b_a, w_x, b_x, lru_lambda, g_norm_conv, g_norm_rnn, w_out, norm_mlp_g, w_mlp_in, w_mlp_out, final_norm_g, loss_target, m_norm_mix_g, m_w_in, m_conv_w, m_rnn_conv_w, m_rnn_conv_b, m_w_a, m_b_a, m_w_x, m_b_x, m_lru_lambda, m_g_norm_conv, m_g_norm_rnn, m_w_out, m_norm_mlp_g, m_w_mlp_in, m_w_mlp_out, m_final_norm_g, v_norm_mix_g, v_w_in, v_conv_w, v_rnn_conv_w, v_rnn_conv_b, v_w_a, v_b_a, v_w_x, v_b_x, v_lru_lambda, v_g_norm_conv, v_g_norm_rnn, v_w_out, v_norm_mlp_g, v_w_mlp_in, v_w_mlp_out, v_final_norm_g):
    t_len = x.shape[1]
    my_id = 4 * lax.axis_index("x") + 2 * lax.axis_index("y") + lax.axis_index("c")
    tm = min(256, t_len)
    tb = min(512, t_len)
    tk = min(512, t_len)

    xs = x.reshape(t_len, D_MODEL)
    tgt = loss_target.reshape(t_len, D_MODEL)
    flat = lambda a: a.reshape(a.shape[-2:]) if a.ndim == 3 else a.reshape(1, -1)
    heads = lambda a: a.reshape(LRU_WIDTH, HEAD_DIM)

    turned = lambda a: jnp.transpose(flat(a))
    win_shard, wout_shard, w1_shard, w2_shard, cp_shard = _prep_shards(
        turned(w_in), flat(w_out), flat(w_mlp_in), flat(w_mlp_out), flat(conv_w), flat(rnn_conv_w))

    u, h, win_t, cp_full = _in_proj(xs, flat(norm_mix_g), (win_shard, cp_shard), min(1024, t_len))
    cpack = cp_full.reshape(N_DEV, 8, 128)
    conv_full = jnp.transpose(cpack[:, 0:3, 0:64], (1, 0, 2)).reshape(3, CONV_WIDTH)
    rnn_full = jnp.transpose(cpack[:, 3:7, :], (1, 0, 2)).reshape(4, LRU_WIDTH)
    mixer_small = (conv_full, rnn_full, flat(rnn_conv_b), heads(w_a), flat(b_a), heads(w_x), flat(b_x),
                   flat(lru_lambda), flat(g_norm_conv), flat(g_norm_rnn))
    hs, y, xr, gate_r, gate_i, mult, cq, w1_blk, wout_blk = _mixer_fwd(u, *mixer_small, (w1_shard, wout_shard), tm)
    wout_f = wout_blk.reshape(MIX_WIDTH, D_MODEL)
    x1, h2, z, w2_blk = _mlp_up(xs, y, flat(norm_mlp_g), wout_f, w1_blk, w2_shard, tb)
    dx1, dx2, vec_m, dpre = _mlp_down_bwd(x1, z, tgt, flat(norm_mlp_g), flat(final_norm_g), w1_blk,
                                          w2_blk.reshape(D_FF, D_MODEL), tb)
    (g_w1,) = _tn_weight_grad(h2, dpre, tk, "w_mlp_in_grad", col_blocks=N_DEV)
    (g_w2,) = _tn_weight_grad(z, dx2, tk, "w_mlp_out_grad")
    g_w2 = g_w2.reshape(N_DEV, D_FF // N_DEV, D_MODEL)
    g_wout, sib_w1, sib_w2 = _tn_weight_grad(y, dx1, tk, "w_out_grad", pair=(g_w1, g_w2))
    g_wout = g_wout.reshape(N_DEV, MIX_WIDTH // N_DEV, D_MODEL)
    hsend_w1, own_w1, hsend_w2, own_w2 = _pair_sum((g_w1, g_w2), (sib_w1, sib_w2), "pair_sum_w_mlp")
    du, vec_b, wab, landed_w1, landed_w2, sib_wout = _mixer_bwd(
        u, hs, dx1, (xr, gate_r, gate_i, mult, cq), conv_full, rnn_full, heads(w_a), heads(w_x),
        flat(lru_lambda), flat(g_norm_conv), flat(g_norm_rnn), wout_f, (hsend_w1, hsend_w2), g_wout, tm)
    hsend_wout, own_wout = _pair_sum((g_wout,), (sib_wout,), "pair_sum_w_out")
    ax, ay, ac = lax.axis_index("x"), lax.axis_index("y"), lax.axis_index("c")
    chip_ids = jnp.stack([2 * cx + cy for cx, cy in [(ax, ay)] + _other_chips(ax, ay)]).astype(jnp.int32)
    core = jnp.reshape(ac, (1,)).astype(jnp.int32)
    tw = min(1024, t_len)
    g_others, landed_wout, vrecv_m, vrecv_b, wrecv = _w_in_grad_part(
        du, h, tw, "w_in_grad_others", chip_ids[1:4], chip=(hsend_wout,), small=(vec_m, vec_b, wab))
    g_own, sib_others = _w_in_grad_part(du, h, tw, "w_in_grad_own", chip_ids[0:1], halves=g_others)
    hsend_win = _pair_sum_parts(g_others, sib_others, core)
    grad_x, vec_x, landed_win, sib_own = _in_proj_bwd(du, dx1, xs, flat(norm_mix_g), win_t, tm, hsend_win, g_own)

    vsum, wsum = _final_small(vrecv_m, vrecv_b, wab, wrecv, vec_x)

    up_win = _update_w_in(g_own, sib_own, landed_win, turned(w_in), turned(m_w_in), turned(v_w_in), core, 256)
    up_win = [jnp.transpose(a) for a in up_win]
    up_wout = _update_sharded(own_wout, landed_wout, flat(w_out), flat(m_w_out), flat(v_w_out), 96, "update_w_out")
    up_w1 = _update_sharded(own_w1, landed_w1, flat(w_mlp_in), flat(m_w_mlp_in), flat(v_w_mlp_in), 256,
                            "update_w_mlp_in")
    up_w2 = _update_sharded(own_w2, landed_w2, flat(w_mlp_out), flat(m_w_mlp_out), flat(v_w_mlp_out), 256,
                            "update_w_mlp_out")

    g_cw = lax.dynamic_slice(vsum, (ROW_CW, 64 * my_id), (3, 64))
    g_rw = lax.dynamic_slice(vsum, (ROW_RW, 128 * my_id), (4, 128))
    small_w = (norm_mix_g, conv_w, rnn_conv_w, rnn_conv_b, w_a, b_a, w_x, b_x, lru_lambda, g_norm_conv, g_norm_rnn,
               norm_mlp_g, final_norm_g)
    small_m = (m_norm_mix_g, m_conv_w, m_rnn_conv_w, m_rnn_conv_b, m_w_a, m_b_a, m_w_x, m_b_x, m_lru_lambda,
               m_g_norm_conv, m_g_norm_rnn, m_norm_mlp_g, m_final_norm_g)
    small_v = (v_norm_mix_g, v_conv_w, v_rnn_conv_w, v_rnn_conv_b, v_w_a, v_b_a, v_w_x, v_b_x, v_lru_lambda,
               v_g_norm_conv, v_g_norm_rnn, v_norm_mlp_g, v_final_norm_g)
    is_heads = (False, False, False, False, True, False, True, False, False, False, False, False, False)
    as2d = lambda arrs: [heads(a) if hd else flat(a) for a, hd in zip(arrs, is_heads)]
    small_out = _update_small(vsum, wsum, g_cw, g_rw, as2d(small_w), as2d(small_m), as2d(small_v))
    loss = small_out[0].reshape(())

    names = ["norm_mix_g", "w_in", "conv_w", "rnn_conv_w", "rnn_conv_b", "w_a", "b_a", "w_x", "b_x", "lru_lambda",
             "g_norm_conv", "g_norm_rnn", "w_out", "norm_mlp_g", "w_mlp_in", "w_mlp_out", "final_norm_g"]
    originals = dict(zip(names, (norm_mix_g, w_in, conv_w, rnn_conv_w, rnn_conv_b, w_a, b_a, w_x, b_x, lru_lambda,
                                 g_norm_conv, g_norm_rnn, w_out, norm_mlp_g, w_mlp_in, w_mlp_out, final_norm_g)))
    results = {"w_in": up_win, "w_out": up_wout, "w_mlp_in": up_w1, "w_mlp_out": up_w2}
    small_names = ["norm_mix_g", "conv_w", "rnn_conv_w", "rnn_conv_b", "w_a", "b_a", "w_x", "b_x", "lru_lambda",
                   "g_norm_conv", "g_norm_rnn", "norm_mlp_g", "final_norm_g"]
    for k, nm in enumerate(small_names):
        results[nm] = small_out[1 + 4 * k:5 + 4 * k]
    out = [loss, grad_x.reshape(x.shape)]
    for kind in range(4):
        out += [results[nm][kind].reshape(originals[nm].shape) for nm in names]
    return tuple(out)
```

```python
import functools

import jax
import jax.numpy as jnp
from jax import lax
from jax.experimental import pallas as pl
from jax.experimental.pallas import tpu as pltpu

F32 = jnp.float32
BF16 = jnp.bfloat16

D_MODEL = 1024
HEAD_DIM = 64
CONV_WIDTH = 512
LRU_WIDTH = 1024
MIX_WIDTH = CONV_WIDTH + LRU_WIDTH
IN_COLS = 3 * CONV_WIDTH + 2 * LRU_WIDTH
D_FF = 4 * D_MODEL
GROUP = 256
EPS = 1e-6
LRU_C = 8.0
N_DEV = 8
SUB = 8

OFF_GB, OFF_GC, OFF_V, OFF_XR, OFF_G = 0, 512, 1024, 1536, 2560

ADAM_LR, ADAM_B1, ADAM_B2, ADAM_EPS, ADAM_WD, ADAM_STEP = 0.001, 0.9, 0.999, 1e-08, 0.01, 10
BC1 = 1.0 - ADAM_B1 ** ADAM_STEP
BC2 = 1.0 - ADAM_B2 ** ADAM_STEP

MIB = 1024 * 1024
MESH = pl.DeviceIdType.MESH

VEC_ROWS = 32
ROW_GF, ROW_GMLP, ROW_LOSS = 0, 1, 2
ROW_GNC, ROW_GNR, ROW_BR, ROW_BA, ROW_BX, ROW_LAM, ROW_CW, ROW_RW = 8, 9, 10, 11, 12, 13, 14, 17
ROW_GMIX = 24
ACC_GNC, ACC_GNR, ACC_BR, ACC_BA, ACC_BX, ACC_SP, ACC_CW, ACC_RW, N_ACC = 0, 1, 2, 3, 4, 5, 6, 9, 13


def _params(semantics=None, vmem_mib=48):
    return pltpu.CompilerParams(dimension_semantics=semantics, vmem_limit_bytes=vmem_mib * MIB)


def _rms(x):
    return lax.rsqrt(jnp.mean(x * x, axis=-1, keepdims=True) + EPS)


def _rms_bwd(dy, xhat, r, g):
    dyh = dy * g
    return r * (dyh - xhat * jnp.mean(dyh * xhat, axis=-1, keepdims=True))


def _sigmoid(x):
    return 0.5 + 0.5 * jnp.tanh(0.5 * x)


def _gelu(x):
    c0, c1 = 0.7978845608028654, 0.044715
    x2 = x * x
    t = jnp.tanh(x * (c0 + (c0 * c1) * x2))
    half = 0.5 + 0.5 * t
    ge = x * half
    dge = half + (ge - ge * half) * (2.0 * c0 + (6.0 * c0 * c1) * x2)
    return ge, dge


def _softplus_neg(lam):
    z = -lam
    e = jnp.exp(-jnp.abs(z))
    return jnp.maximum(z, 0.0) + jnp.where(e < 1e-4, e * (1.0 - 0.5 * e), jnp.log(1.0 + e))


def _lru_gates(pa, px, sp_c):
    ra = _sigmoid(pa)
    ii = _sigmoid(px)
    neg_la = ra * sp_c
    a = jnp.exp(-neg_la)
    m2 = jnp.tanh(neg_la) * (1.0 + a * a)
    mult = jnp.where(m2 > 0.0, m2 * lax.rsqrt(m2), 0.0)
    return ra, ii, a, mult


def _down(cur, prev, s, row):
    return pltpu.roll(jnp.where(row < SUB - s, cur, prev), s, 0)


def _up(cur, nxt, s, row):
    return pltpu.roll(jnp.where(row >= s, cur, nxt), SUB - s, 0)


def _scan8_fwd(a, b, row):
    for s in (1, 2, 4):
        m = row >= s
        a_sh = pltpu.roll(a, s, 0)
        b_sh = pltpu.roll(b, s, 0)
        b = jnp.where(m, a * b_sh + b, b)
        a = jnp.where(m, a * a_sh, a)
    return a, b


def _scan8_rev(a, b, row):
    for s in (1, 2, 4):
        m = row < SUB - s
        a_sh = pltpu.roll(a, SUB - s, 0)
        b_sh = pltpu.roll(b, SUB - s, 0)
        b = jnp.where(m, a * b_sh + b, b)
        a = jnp.where(m, a * a_sh, a)
    return a, b


def _group_mask(shape):
    r = lax.broadcasted_iota(jnp.int32, shape, 0)
    c = lax.broadcasted_iota(jnp.int32, shape, 1)
    return ((r % GROUP) // HEAD_DIM) == (c // HEAD_DIM)


def _expand_heads(w):
    j = lax.broadcasted_iota(jnp.int32, (HEAD_DIM, GROUP), 0)
    c = lax.broadcasted_iota(jnp.int32, (HEAD_DIM, GROUP), 1)
    spread = (c % HEAD_DIM == j).astype(BF16)
    e = jnp.dot(w.astype(BF16), spread, preferred_element_type=F32)
    return jnp.where(_group_mask(e.shape), e, 0.0).astype(BF16)


def _fold_heads(p):
    p = jnp.where(_group_mask(p.shape), p, 0.0)
    c = lax.broadcasted_iota(jnp.int32, (GROUP, HEAD_DIM), 0)
    j = lax.broadcasted_iota(jnp.int32, (GROUP, HEAD_DIM), 1)
    fold = (c % HEAD_DIM == j).astype(BF16)
    hi = p.astype(BF16)
    rest = p - hi.astype(F32)
    mid = rest.astype(BF16)
    lo = (rest - mid.astype(F32)).astype(BF16)
    dot = functools.partial(jnp.dot, preferred_element_type=F32)
    return dot(hi, fold) + dot(mid, fold) + dot(lo, fold)


def _block_diag_apply(xb, wbd_ref):
    parts = [jnp.dot(xb[:, g * GROUP:(g + 1) * GROUP], wbd_ref[g * GROUP:(g + 1) * GROUP, :],
                     preferred_element_type=F32) for g in range(LRU_WIDTH // GROUP)]
    return jnp.concatenate(parts, axis=1)


def _block_diag_apply_t(db, wbd_ref):
    parts = [lax.dot_general(db[:, g * GROUP:(g + 1) * GROUP], wbd_ref[g * GROUP:(g + 1) * GROUP, :],
                             (((1,), (1,)), ((), ())), preferred_element_type=F32)
             for g in range(LRU_WIDTH // GROUP)]
    return jnp.concatenate(parts, axis=1)


def _dot_nt(a, b):
    return lax.dot_general(a, b, (((1,), (1,)), ((), ())), preferred_element_type=F32)


def _dot_tn(a, b):
    return lax.dot_general(a, b, (((0,), (0,)), ((), ())), preferred_element_type=F32)


CHUNKS_IN_FLIGHT = 8


def _chunk_loop(n_chunks, chunk, init):
    def body(k, carry):
        for j in range(CHUNKS_IN_FLIGHT):
            carry = chunk(k * CHUNKS_IN_FLIGHT + j, carry)
        return carry

    return lax.fori_loop(0, n_chunks // CHUNKS_IN_FLIGHT, body, init)


def _place():
    x, y, c = lax.axis_index("x"), lax.axis_index("y"), lax.axis_index("c")
    return x, y, c


def _block_id(chip, core):
    return 4 * chip[0] + 2 * chip[1] + core


def _other_chips(x, y):
    return [(1 - x, y), (x, 1 - y), (1 - x, 1 - y)]


def _remote_copy(src, dst, send_sem, recv_sem, to):
    return pltpu.make_async_remote_copy(src_ref=src, dst_ref=dst, send_sem=send_sem, recv_sem=recv_sem,
                                        device_id=to, device_id_type=MESH)


HBM_SPEC = pl.BlockSpec(memory_space=pl.ANY)


def _in_hbm(*arrays):
    return [pltpu.with_memory_space_constraint(a, pltpu.HBM) for a in arrays]


def _prep_shards(w_in_t, w_out, w_mlp_in, w_mlp_out, conv_w, rnn_conv_w):
    def body(win_ref, wout_ref, w1_ref, w2_ref, cw_ref, rw_ref, o_win, o_wout, o_w1, o_w2, o_cp):
        o_win[...] = win_ref[...].astype(BF16)
        o_wout[...] = wout_ref[...].astype(BF16)
        o_w1[...] = w1_ref[...].astype(BF16)
        o_w2[...] = w2_ref[...].astype(BF16)
        o_cp[...] = jnp.zeros(o_cp.shape, F32)
        o_cp[0:3, 0:64] = cw_ref[...]
        o_cp[3:7, :] = rw_ref[...]

    whole = lambda shape: pl.BlockSpec(shape, lambda i: (0,) * len(shape))
    args = (w_in_t, w_out, w_mlp_in, w_mlp_out, conv_w, rnn_conv_w)
    shapes = [(w_in_t.shape, BF16), (w_out.shape, BF16), (w_mlp_in.shape, BF16), (w_mlp_out.shape, BF16),
              ((8, 128), F32)]
    return pl.pallas_call(
        body, grid=(1,), out_shape=[jax.ShapeDtypeStruct(s, d) for s, d in shapes],
        in_specs=[whole(a.shape) for a in args], out_specs=[whole(s) for s, _ in shapes],
        compiler_params=_params(("arbitrary",), 40), name="prep_shards",
    )(*args)


def _host_all_gather(step, n_steps, shards, fulls, send_sems, recv_sems, local_sems):
    x, y, c = _place()
    me = (x, y, c)
    my_id = _block_id((x, y), c)
    sibling = (x, y, 1 - c)
    chips = _other_chips(x, y)
    n_arr = len(shards)

    def copy(arr, k, block, to, src=None):
        dst = fulls[arr].at[block]
        return _remote_copy(dst if src is None else src, dst, send_sems.at[arr, k], recv_sems.at[arr, k], to)

    def local(arr):
        return pltpu.make_async_copy(shards[arr], fulls[arr].at[my_id], local_sems.at[arr])

    @pl.when(step == 0)
    def _():
        for arr in range(n_arr):
            local(arr).start()
            copy(arr, 0, my_id, sibling, shards[arr]).start()
            for j, chip in enumerate(chips):
                copy(arr, 1 + j, my_id, (*chip, c), shards[arr]).start()

    @pl.when(step == max(n_steps - 2, 0))
    def _():
        for j, chip in enumerate(chips):
            for arr in range(n_arr):
                copy(arr, 1 + j, _block_id(chip, c), me).wait_recv()
                copy(arr, 4 + j, _block_id(chip, c), sibling).start()

    @pl.when(step == n_steps - 1)
    def _():
        for arr in range(n_arr):
            copy(arr, 0, _block_id((x, y), 1 - c), me).wait_recv()
            for j, chip in enumerate(chips):
                copy(arr, 4 + j, _block_id(chip, 1 - c), me).wait_recv()
            for k in range(4):
                copy(arr, k, my_id, me, shards[arr]).wait_send()
            for j, chip in enumerate(chips):
                copy(arr, 4 + j, _block_id(chip, c), me).wait_send()
            local(arr).wait()


def _host_pair_exchange(step, n_steps, gs, sibs, send_sems, recv_sems):
    x, y, c = _place()
    sibling = (x, y, 1 - c)
    chips = [(x, y)] + _other_chips(x, y)

    def d2d(arr, q):
        return _remote_copy(gs[arr].at[_block_id(chips[q], 1 - c)], sibs[arr].at[q],
                            send_sems.at[arr, q], recv_sems.at[arr, q], sibling)

    @pl.when(step == 0)
    def _():
        for arr in range(len(gs)):
            for q in (1, 2, 3, 0):
                d2d(arr, q).start()

    @pl.when(step == n_steps - 1)
    def _():
        for arr in range(len(gs)):
            for q in range(4):
                d2d(arr, q).wait()


def _host_chip_exchange(step, n_steps, hsends, hrecvs, send_sems, recv_sems):
    x, y, c = _place()
    chips = _other_chips(x, y)

    def ici(arr, j):
        return _remote_copy(hsends[arr].at[j], hrecvs[arr].at[j], send_sems.at[arr, j], recv_sems.at[arr, j],
                            (*chips[j], c))

    @pl.when(step == 0)
    def _():
        for arr in range(len(hsends)):
            for j in range(3):
                ici(arr, j).start()

    @pl.when(step == n_steps - 1)
    def _():
        for arr in range(len(hsends)):
            for j in range(3):
                ici(arr, j).wait()


def _host_half_exchange(step, n_steps, parts, sibs, send_sems, recv_sems):
    x, y, c = _place()
    n_q, rows2, _ = parts.shape
    half = rows2 // 2

    def d2d(q):
        src = parts.at[q, pl.ds(pl.multiple_of((1 - c) * half, 16), half), :]
        return _remote_copy(src, sibs.at[q], send_sems.at[q], recv_sems.at[q], (x, y, 1 - c))

    @pl.when(step == 0)
    def _():
        for q in range(n_q):
            d2d(q).start()

    @pl.when(step == n_steps - 1)
    def _():
        for q in range(n_q):
            d2d(q).wait()


def _peer(x, y, c, k):
    return (x ^ ((k >> 2) & 1), y ^ ((k >> 1) & 1), c ^ (k & 1))


def _host_small_exchange(step, n_steps, vec_m, vec_b, wab, vrecv_m, vrecv_b, wrecv, send_sems, recv_sems, local_sems):
    x, y, c = _place()
    my_id = _block_id((x, y), c)
    wrows = wab.shape[0] // N_DEV

    def copies(k):
        to = _peer(x, y, c, k)
        block = wab.at[pl.ds(pl.multiple_of(_block_id(to[0:2], to[2]) * wrows, SUB), wrows), :]
        return [_remote_copy(vec_m, vrecv_m.at[my_id], send_sems.at[0, k], recv_sems.at[0, k], to),
                _remote_copy(vec_b, vrecv_b.at[my_id], send_sems.at[1, k], recv_sems.at[1, k], to),
                _remote_copy(block, wrecv.at[k], send_sems.at[2, k], recv_sems.at[2, k], to)]

    mine = [pltpu.make_async_copy(vec_m, vrecv_m.at[my_id], local_sems.at[0]),
            pltpu.make_async_copy(vec_b, vrecv_b.at[my_id], local_sems.at[1])]

    @pl.when(step == 0)
    def _():
        for cp in mine:
            cp.start()
        for k in range(1, N_DEV):
            for cp in copies(k):
                cp.start()

    @pl.when(step == n_steps - 1)
    def _():
        for k in range(1, N_DEV):
            for cp in copies(k):
                cp.wait()
        for cp in mine:
            cp.wait()


def _pair_sum_parts(parts, sibs, core):
    n_q, rows2, cols = parts.shape
    half = rows2 // 2

    def body(core_ref, g_ref, s_ref, o_ref):
        o_ref[0] = (g_ref[0, 0].astype(F32) + s_ref[0].astype(F32)).astype(BF16)

    block = (1, half, cols)
    grid_spec = pltpu.PrefetchScalarGridSpec(
        num_scalar_prefetch=1, grid=(n_q,),
        in_specs=[pl.BlockSpec((1, 1, half, cols), lambda q, cr: (q, cr[0], 0, 0)),
                  pl.BlockSpec(block, lambda q, cr: (q, 0, 0))],
        out_specs=pl.BlockSpec(block, lambda q, cr: (q, 0, 0)))
    return pl.pallas_call(
        body, grid_spec=grid_spec, out_shape=pltpu.HBM((n_q, half, cols), BF16),
        compiler_params=_params(("arbitrary",), 32), name="pair_sum_w_in",
    )(core, *_in_hbm(parts.reshape(n_q, 2, half, cols), sibs))


def _pair_sum(gs, sibs, name):
    n_arr = len(gs)
    x, y, c = _place()
    slots = jnp.stack([_block_id(chip, c) for chip in [(x, y)] + _other_chips(x, y)]).astype(jnp.int32)

    def body(slots_ref, *refs):
        q = pl.program_id(0)
        for k in range(n_arr):
            g_ref, sib_ref = refs[2 * k:2 * k + 2]
            hs_ref, own_ref = refs[2 * n_arr + 2 * k:2 * n_arr + 2 * k + 2]
            both = g_ref[0].astype(F32) + sib_ref[0].astype(F32)

            @pl.when(q == 0)
            def _(own_ref=own_ref, both=both):
                own_ref[...] = both

            @pl.when(q > 0)
            def _(hs_ref=hs_ref, both=both):
                hs_ref[0] = both.astype(BF16)

    in_specs, out_specs, out_shape, args = [], [], [], []
    for g, sib in zip(gs, sibs):
        _, rows, cols = g.shape
        block = (1, rows, cols)
        in_specs += [pl.BlockSpec(block, lambda q, s: (s[q], 0, 0)), pl.BlockSpec(block, lambda q, s: (q, 0, 0))]
        out_specs += [pl.BlockSpec(block, lambda q, s: (jnp.maximum(q - 1, 0), 0, 0)),
                      pl.BlockSpec((rows, cols), lambda q, s: (0, 0))]
        out_shape += [pltpu.HBM((3, rows, cols), BF16), pltpu.HBM((rows, cols), F32)]
        args += _in_hbm(g, sib)
    grid_spec = pltpu.PrefetchScalarGridSpec(num_scalar_prefetch=1, grid=(4,), in_specs=in_specs, out_specs=out_specs)
    return pl.pallas_call(
        body, grid_spec=grid_spec, out_shape=out_shape,
        compiler_params=_params(("arbitrary",), 40), name=name,
    )(slots, *args)


def _exchange_scratch(n_arr, n_copies):
    return [pltpu.SemaphoreType.DMA((n_arr, n_copies)), pltpu.SemaphoreType.DMA((n_arr, n_copies))]


def _final_small(vrecv_m, vrecv_b, wab, wrecv, vec_x):
    wrows = wab.shape[0] // N_DEV

    def body(vm_ref, vb_ref, w_ref, wr_ref, vx_ref, o_vec, o_w, xrecv, wred, x_send, x_recv, b_send, b_recv):
        x, y, c = _place()
        my_id = _block_id((x, y), c)
        my_rows = pl.ds(pl.multiple_of(my_id * wrows, SUB), wrows)

        def xcopy(k):
            return _remote_copy(vx_ref, xrecv.at[my_id], x_send.at[k], x_recv.at[k], _peer(x, y, c, k))

        def bcopy(k):
            return _remote_copy(wred, o_w.at[my_rows, :], b_send.at[k], b_recv.at[k], _peer(x, y, c, k))

        xrecv[my_id] = vx_ref[...]
        for k in range(1, N_DEV):
            xcopy(k).start()
        red = w_ref[my_rows, :]
        for k in range(1, N_DEV):
            red = red + wr_ref[k]
        wred[...] = red
        o_w[my_rows, :] = red
        for k in range(1, N_DEV):
            bcopy(k).start()
        for k in range(1, N_DEV):
            xcopy(k).wait_recv()
        for rows, ref in ((slice(0, 8), vm_ref), (slice(8, 24), vb_ref), (slice(24, 32), xrecv)):
            tot = ref[0]
            for s in range(1, N_DEV):
                tot = tot + ref[s]
            o_vec[rows, :] = tot
        for k in range(1, N_DEV):
            bcopy(k).wait_recv()
        for k in range(1, N_DEV):
            xcopy(k).wait_send()
            bcopy(k).wait_send()

    vm = pl.BlockSpec(memory_space=pltpu.VMEM)
    dma8 = pltpu.SemaphoreType.DMA((N_DEV,))
    return pl.pallas_call(
        body, out_shape=(jax.ShapeDtypeStruct((VEC_ROWS, D_MODEL), F32), jax.ShapeDtypeStruct(wab.shape, F32)),
        in_specs=[vm] * 5, out_specs=[vm] * 2,
        scratch_shapes=[pltpu.VMEM((N_DEV, SUB, D_MODEL), F32), pltpu.VMEM((wrows, HEAD_DIM), F32),
                        dma8, dma8, dma8, dma8],
        compiler_params=_params(vmem_mib=32), name="final_small",
    )(vrecv_m, vrecv_b, wab, wrecv, vec_x)


def _in_proj(x, g_mix, shards, tm):
    t_len = x.shape[0]
    n_t = t_len // tm
    n_arr = len(shards)
    rows = [s.shape[0] for s in shards]
    width = 2 * rows[0]
    ax, ay = lax.axis_index("x"), lax.axis_index("y")
    order = jnp.stack([2 * cx + cy for cx, cy in [(ax, ay)] + _other_chips(ax, ay)]).astype(jnp.int32)

    def body(order_ref, x_ref, g_ref, *rest):
        shard_refs = rest[0:n_arr]
        u_ref, h_ref = rest[n_arr:n_arr + 2]
        fulls = rest[n_arr + 2:2 * n_arr + 2]
        h_s, wbuf, send_sems, recv_sems, local_sems, load_sem = rest[2 * n_arr + 2:]
        p = pl.program_id(0)
        i = pl.program_id(1)
        x_, y_, c = _place()
        me = (x_, y_, c)
        my_id = _block_id((x_, y_), c)
        sibling = (x_, y_, 1 - c)
        chips = _other_chips(x_, y_)

        def block(arr, blk):
            return fulls[arr].at[pl.ds(pl.multiple_of(blk * rows[arr], rows[arr]), rows[arr]), :]

        def copy(arr, k, blk, to, src=None):
            dst = block(arr, blk)
            return _remote_copy(dst if src is None else src, dst, send_sems.at[arr, k], recv_sems.at[arr, k], to)

        def local(arr):
            return pltpu.make_async_copy(shard_refs[arr], block(arr, my_id), local_sems.at[arr])

        def load_chip(chip, slot):
            start = pl.multiple_of((2 * chip[0] + chip[1]) * width, width)
            return pltpu.make_async_copy(fulls[0].at[pl.ds(start, width), :], wbuf.at[slot], load_sem.at[slot])

        def pass_on(j):
            for arr in range(n_arr):
                copy(arr, 1 + j, _block_id(chips[j], c), me).wait_recv()
                copy(arr, 4 + j, _block_id(chips[j], c), sibling).start()

        def complete(j):
            for arr in range(n_arr):
                copy(arr, 4 + j, _block_id(chips[j], 1 - c), me).wait_recv()

        @pl.when((p == 0) & (i == 0))
        def _():
            for arr in range(n_arr):
                local(arr).start()
                copy(arr, 0, my_id, sibling, shard_refs[arr]).start()
                for j in (0, 1):
                    copy(arr, 1 + j, my_id, (*chips[j], c), shard_refs[arr]).start()
            for arr in range(n_arr):
                local(arr).wait()
                copy(arr, 0, _block_id((x_, y_), 1 - c), me).wait_recv()
            load_chip((x_, y_), 0).start()
            load_chip((x_, y_), 0).wait()

        @pl.when((p == 1) & (i == 0))
        def _():
            pass_on(0)
            for arr in range(n_arr):
                copy(arr, 3, my_id, (*chips[2], c), shard_refs[arr]).start()
            pass_on(1)
            complete(0)
            load_chip(chips[0], 1).start()
            load_chip(chips[0], 1).wait()
            complete(1)
            load_chip(chips[1], 0).start()

        @pl.when((p == 2) & (i == 0))
        def _():
            load_chip(chips[1], 0).wait()

        @pl.when((p == 3) & (i == 0))
        def _():
            pass_on(2)
            complete(2)
            load_chip(chips[2], 1).start()
            load_chip(chips[2], 1).wait()

        @pl.when((p == 3) & (i == n_t - 1))
        def _():
            for arr in range(n_arr):
                for k in range(4):
                    copy(arr, k, my_id, me, shard_refs[arr]).wait_send()
                for j, chip in enumerate(chips):
                    copy(arr, 4 + j, _block_id(chip, c), me).wait_send()

        tile = pl.ds(pl.multiple_of(i * tm, tm), tm)

        @pl.when(p == 0)
        def _():
            xv = x_ref[...]
            h = (xv * _rms(xv) * g_ref[...]).astype(BF16)
            h_ref[...] = h
            h_s[tile, :] = h

        for slot in (0, 1):
            @pl.when(p % 2 == slot)
            def _(slot=slot):
                u_ref[...] = _dot_nt(h_s[tile, :], wbuf[slot])

    first_pass = lambda p, i, o: (jnp.where(p == 0, i, n_t - 1), 0)
    grid_spec = pltpu.PrefetchScalarGridSpec(
        num_scalar_prefetch=1, grid=(4, n_t),
        in_specs=[pl.BlockSpec((tm, D_MODEL), first_pass), pl.BlockSpec((1, D_MODEL), lambda p, i, o: (0, 0))]
        + [HBM_SPEC] * n_arr,
        out_specs=[pl.BlockSpec((tm, width), lambda p, i, o: (i, o[p])), pl.BlockSpec((tm, D_MODEL), first_pass)]
        + [HBM_SPEC] * n_arr,
        scratch_shapes=[pltpu.VMEM((t_len, D_MODEL), BF16), pltpu.VMEM((2, width, D_MODEL), BF16)]
        + _exchange_scratch(n_arr, 7) + [pltpu.SemaphoreType.DMA((n_arr,)), pltpu.SemaphoreType.DMA((2,))])
    return pl.pallas_call(
        body, grid_spec=grid_spec,
        out_shape=[jax.ShapeDtypeStruct((t_len, IN_COLS), F32), jax.ShapeDtypeStruct((t_len, D_MODEL), BF16)]
        + [jax.ShapeDtypeStruct((N_DEV * s.shape[0], s.shape[1]), s.dtype) for s in shards],
        compiler_params=_params(("arbitrary", "arbitrary"), 48), name="in_proj",
    )(order, x, g_mix, *shards)


def _conv3_chunk(u_ref, r, cv_prev, cw, row):
    gb = u_ref[pl.ds(r, SUB), OFF_GB:OFF_GB + CONV_WIDTH]
    gc = u_ref[pl.ds(r, SUB), OFF_GC:OFF_GC + CONV_WIDTH]
    v = u_ref[pl.ds(r, SUB), OFF_V:OFF_V + CONV_WIDTH]
    cv = gc * v
    cv_m1 = _down(cv, cv_prev, 1, row)
    cv_m2 = _down(cv, cv_prev, 2, row)
    cq = cw[2:3, :] * cv + cw[1:2, :] * cv_m1 + cw[0:1, :] * cv_m2
    return gb, gc, v, cv, cv_m1, cv_m2, cq


def _conv4_chunk(u_ref, r, xin_prev, rw, rb, row):
    xin = u_ref[pl.ds(r, SUB), OFF_XR:OFF_XR + LRU_WIDTH]
    m1 = _down(xin, xin_prev, 1, row)
    m2 = _down(xin, xin_prev, 2, row)
    m3 = _down(xin, xin_prev, 3, row)
    xr = rw[3:4, :] * xin + rw[2:3, :] * m1 + rw[1:2, :] * m2 + rw[0:1, :] * m3 + rb
    return xin, m1, m2, m3, xr


def _mixer_fwd(u, conv_w, rnn_conv_w, rnn_conv_b, wa, b_a, wx, b_x, lam, gnc, gnr, shards, tm):
    t_len = u.shape[0]
    n_steps = t_len // tm
    n_chunks = tm // SUB
    n_arr = len(shards)

    def body(u_ref, cw_ref, rw_ref, rb_ref, wa_ref, ba_ref, wx_ref, bx_ref, lam_ref, gnc_ref, gnr_ref, *rest):
        shard_refs = rest[0:n_arr]
        hs_ref, y_ref, xr_s, ra_ref, ii_ref, mult_ref, cq_ref = rest[n_arr:n_arr + 7]
        fulls = rest[n_arr + 7:2 * n_arr + 7]
        (y_s, pa_s, px_s, wabd, wxbd, cv_car, xin_car, h_car,
         send_sems, recv_sems, local_sems) = rest[2 * n_arr + 7:]
        _host_all_gather(pl.program_id(0), n_steps, shard_refs, fulls, send_sems, recv_sems, local_sems)

        @pl.when(pl.program_id(0) == 0)
        def _():
            cv_car[...] = jnp.zeros(cv_car.shape, F32)
            xin_car[...] = jnp.zeros(xin_car.shape, F32)
            h_car[...] = jnp.zeros(h_car.shape, F32)
            wabd[...] = _expand_heads(wa_ref[...])
            wxbd[...] = _expand_heads(wx_ref[...])

        row_c = lax.broadcasted_iota(jnp.int32, (SUB, CONV_WIDTH), 0)
        row_r = lax.broadcasted_iota(jnp.int32, (SUB, LRU_WIDTH), 0)
        cw = cw_ref[...]
        rw = rw_ref[...]
        rb = rb_ref[...]
        g_c = gnc_ref[...]
        g_r = gnr_ref[...]
        sp_c = LRU_C * _softplus_neg(lam_ref[...])

        def convs(i, carry):
            cv_prev, xin_prev = carry
            r = pl.multiple_of(i * SUB, SUB)
            gb, _, _, cv, _, _, cq = _conv3_chunk(u_ref, r, cv_prev, cw, row_c)
            cq_ref[pl.ds(r, SUB), :] = cq
            y_c = gb * cq
            y_s[pl.ds(r, SUB), 0:CONV_WIDTH] = y_c * _rms(y_c) * g_c
            xin, _, _, _, xr = _conv4_chunk(u_ref, r, xin_prev, rw, rb, row_r)
            xr_s[pl.ds(r, SUB), :] = xr
            return cv, xin

        cv_last, xin_last = _chunk_loop(n_chunks, convs, (cv_car[...], xin_car[...]))
        cv_car[...] = cv_last
        xin_car[...] = xin_last

        for g in range(LRU_WIDTH // GROUP):
            cols = slice(g * GROUP, (g + 1) * GROUP)
            xrb = xr_s[:, cols].astype(BF16)
            pa_s[:, cols] = jnp.dot(xrb, wabd[cols, :], preferred_element_type=F32) + ba_ref[:, cols]
            px_s[:, cols] = jnp.dot(xrb, wxbd[cols, :], preferred_element_type=F32) + bx_ref[:, cols]

        def recur(i, h_prev):
            r = pl.multiple_of(i * SUB, SUB)
            xr = xr_s[pl.ds(r, SUB), :]
            ra, ii, a, mult = _lru_gates(pa_s[pl.ds(r, SUB), :], px_s[pl.ds(r, SUB), :], sp_c)
            ra_ref[pl.ds(r, SUB), :] = ra
            ii_ref[pl.ds(r, SUB), :] = ii
            mult_ref[pl.ds(r, SUB), :] = mult
            a_cum, b_cum = _scan8_fwd(a, mult * ii * xr, row_r)
            h = a_cum * h_prev + b_cum
            hs_ref[pl.ds(r, SUB), :] = h
            ge, _ = _gelu(u_ref[pl.ds(r, SUB), OFF_G:OFF_G + LRU_WIDTH])
            y_r = h * ge
            y_s[pl.ds(r, SUB), CONV_WIDTH:MIX_WIDTH] = y_r * _rms(y_r) * g_r
            return h[SUB - 1:SUB, :]

        h_car[...] = _chunk_loop(n_chunks, recur, h_car[...])

        y_ref[...] = y_s[...].astype(BF16)

    row_tile = lambda w: pl.BlockSpec((tm, w), lambda i: (i, 0))
    whole = lambda a: pl.BlockSpec(a.shape, lambda i: (0,) * a.ndim)
    smalls = (conv_w, rnn_conv_w, rnn_conv_b, wa, b_a, wx, b_x, lam, gnc, gnr)
    return pl.pallas_call(
        body, grid=(n_steps,),
        in_specs=[row_tile(IN_COLS)] + [whole(a) for a in smalls] + [HBM_SPEC] * n_arr,
        out_specs=[row_tile(LRU_WIDTH), row_tile(MIX_WIDTH)] + [row_tile(LRU_WIDTH)] * 4 + [row_tile(CONV_WIDTH)]
        + [HBM_SPEC] * n_arr,
        out_shape=[jax.ShapeDtypeStruct((t_len, LRU_WIDTH), F32), jax.ShapeDtypeStruct((t_len, MIX_WIDTH), BF16)]
        + [jax.ShapeDtypeStruct((t_len, LRU_WIDTH), F32)] * 4 + [jax.ShapeDtypeStruct((t_len, CONV_WIDTH), F32)]
        + [jax.ShapeDtypeStruct((N_DEV,) + s.shape, BF16) for s in shards],
        scratch_shapes=[pltpu.VMEM((tm, MIX_WIDTH), F32),
                        pltpu.VMEM((tm, LRU_WIDTH), F32), pltpu.VMEM((tm, LRU_WIDTH), F32),
                        pltpu.VMEM((LRU_WIDTH, GROUP), BF16), pltpu.VMEM((LRU_WIDTH, GROUP), BF16),
                        pltpu.VMEM((SUB, CONV_WIDTH), F32), pltpu.VMEM((SUB, LRU_WIDTH), F32),
                        pltpu.VMEM((1, LRU_WIDTH), F32)]
        + _exchange_scratch(n_arr, 7) + [pltpu.SemaphoreType.DMA((n_arr,))],
        compiler_params=_params(("arbitrary",), 56), name="mixer_fwd",
    )(u, *smalls, *shards)


def _mlp_up(x, y, g_mlp, w_out, w1, w2_shard, tm):
    t_len = x.shape[0]
    n_steps = t_len // tm
    n_blk, _, blk = w1.shape

    def body(x_ref, y_ref, gm_ref, wout_hbm, w1_hbm, w2_ref, x1_ref, h2_ref, z_ref, w2_full,
             wout_s, w1_s, sem, send_sems, recv_sems, local_sems):
        step = pl.program_id(0)
        _host_all_gather(step, n_steps, [w2_ref], [w2_full], send_sems, recv_sems, local_sems)

        load_wout = pltpu.make_async_copy(wout_hbm, wout_s, sem.at[0])
        load_w1 = pltpu.make_async_copy(w1_hbm, w1_s, sem.at[1])

        @pl.when(step == 0)
        def _():
            load_wout.start()
            load_w1.start()
            load_wout.wait()

        x1v = x_ref[...] + jnp.dot(y_ref[...], wout_s[...], preferred_element_type=F32)
        x1_ref[...] = x1v
        h2 = (x1v * _rms(x1v) * gm_ref[...]).astype(BF16)
        h2_ref[...] = h2

        @pl.when(step == 0)
        def _():
            load_w1.wait()

        for k in range(n_blk):
            rp = jnp.maximum(jnp.dot(h2, w1_s[k], preferred_element_type=F32), 0.0)
            z_ref[:, k * blk:(k + 1) * blk] = (rp * rp).astype(BF16)

    row_tile = lambda w: pl.BlockSpec((tm, w), lambda i: (i, 0))
    return pl.pallas_call(
        body, grid=(n_steps,),
        in_specs=[row_tile(D_MODEL), row_tile(MIX_WIDTH), pl.BlockSpec((1, D_MODEL), lambda i: (0, 0)),
                  HBM_SPEC, HBM_SPEC, HBM_SPEC],
        out_specs=[row_tile(D_MODEL), row_tile(D_MODEL), row_tile(D_FF), HBM_SPEC],
        out_shape=[jax.ShapeDtypeStruct((t_len, D_MODEL), F32), jax.ShapeDtypeStruct((t_len, D_MODEL), BF16),
                   jax.ShapeDtypeStruct((t_len, D_FF), BF16), jax.ShapeDtypeStruct((N_DEV,) + w2_shard.shape, BF16)],
        scratch_shapes=[pltpu.VMEM(w_out.shape, BF16), pltpu.VMEM(w1.shape, BF16), pltpu.SemaphoreType.DMA((2,))]
        + _exchange_scratch(1, 7) + [pltpu.SemaphoreType.DMA((1,))],
        compiler_params=_params(("arbitrary",), 48), name="mlp_up",
    )(x, y, g_mlp, w_out, w1, w2_shard)


def _mlp_down_bwd(x1, z, target, g_mlp, g_f, w1, w2, tm):
    t_len = x1.shape[0]
    n_steps = t_len // tm
    n_blk, _, blk = w1.shape

    def body(x1_ref, z_ref, tg_ref, gm_ref, gf_ref, w1_hbm, w2_hbm, dx1_ref, dx2_ref, vec_ref, dpre_hbm,
             w1_s, w2_s, dp_s, sem, out_sem):
        step = pl.program_id(0)
        rows = pl.ds(pl.multiple_of(step * tm, tm), tm)
        dp_out = pltpu.make_async_copy(dp_s, dpre_hbm.at[rows, :], out_sem.at[0])

        load_w1 = pltpu.make_async_copy(w1_hbm, w1_s, sem.at[0])
        load_w2 = pltpu.make_async_copy(w2_hbm, w2_s, sem.at[1])

        @pl.when(step == 0)
        def _():
            load_w2.start()
            load_w1.start()
            vec_ref[...] = jnp.zeros(vec_ref.shape, F32)
            load_w2.wait()

        x1v = x1_ref[...]
        g_m = gm_ref[...]
        g_o = gf_ref[...]
        r2 = _rms(x1v)
        x1h = x1v * r2
        x2 = x1v + jnp.dot(z_ref[...], w2_s[...], preferred_element_type=F32)
        r3 = _rms(x2)
        x2h = x2 * r3
        err = x2h * g_o - tg_ref[...]
        dout = err * (1.0 / D_MODEL)
        vec_ref[ROW_LOSS:ROW_LOSS + 1, :] += (0.5 / D_MODEL) * jnp.sum(err * err, axis=0, keepdims=True)
        vec_ref[ROW_GF:ROW_GF + 1, :] += jnp.sum(dout * x2h, axis=0, keepdims=True)
        dx2 = _rms_bwd(dout, x2h, r3, g_o)
        dx2b = dx2.astype(BF16)
        dx2_ref[...] = dx2b
        dh2 = jnp.zeros((tm, D_MODEL), F32)

        @pl.when(step > 0)
        def _():
            dp_out.wait()

        @pl.when(step == 0)
        def _():
            load_w1.wait()

        for k in range(n_blk):
            cols = slice(k * blk, (k + 1) * blk)
            dz = _dot_nt(dx2b, w2_s[cols, :])
            dpb = (dz * 2.0 * jnp.sqrt(z_ref[:, cols].astype(F32))).astype(BF16)
            dp_s[:, cols] = dpb
            dh2 = dh2 + _dot_nt(dpb, w1_s[k])
        dp_out.start()
        vec_ref[ROW_GMLP:ROW_GMLP + 1, :] += jnp.sum(dh2 * x1h, axis=0, keepdims=True)
        dx1_ref[...] = dx2 + _rms_bwd(dh2, x1h, r2, g_m)

        @pl.when(step == n_steps - 1)
        def _():
            dp_out.wait()

    row_tile = lambda w: pl.BlockSpec((tm, w), lambda i: (i, 0))
    vec_spec = pl.BlockSpec((1, D_MODEL), lambda i: (0, 0))
    return pl.pallas_call(
        body, grid=(n_steps,),
        in_specs=[row_tile(D_MODEL), row_tile(D_FF), row_tile(D_MODEL), vec_spec, vec_spec, HBM_SPEC, HBM_SPEC],
        out_specs=[row_tile(D_MODEL), row_tile(D_MODEL), pl.BlockSpec((SUB, D_MODEL), lambda i: (0, 0)), HBM_SPEC],
        out_shape=[jax.ShapeDtypeStruct((t_len, D_MODEL), F32), jax.ShapeDtypeStruct((t_len, D_MODEL), BF16),
                   jax.ShapeDtypeStruct((SUB, D_MODEL), F32), jax.ShapeDtypeStruct((t_len, D_FF), BF16)],
        scratch_shapes=[pltpu.VMEM(w1.shape, BF16), pltpu.VMEM(w2.shape, BF16), pltpu.VMEM((tm, D_FF), BF16),
                        pltpu.SemaphoreType.DMA((2,)), pltpu.SemaphoreType.DMA((1,))],
        compiler_params=_params(("arbitrary",), 56), name="mlp_down_bwd",
    )(x1, z, target, g_mlp, g_f, w1, w2)


def _mixer_bwd(u, hs, dx1, saved, conv_w, rnn_conv_w, wa, wx, lam, gnc, gnr, w_out, chip_sums, g_wout, tm):
    t_len = u.shape[0]
    n_tiles = t_len // tm
    n_chunks = tm // SUB
    per_tile = tm // SUB
    n_sums = len(chip_sums)

    def body(u_ref, hs_ref, hp_ref, dx1_ref, xr_ref, ra_ref, ii_ref, mult_ref, cq_ref,
             cw_ref, rw_ref, wa_ref, wx_ref, lam_ref, gnc_ref, gnr_ref, wout_ref, *rest):
        hsends = rest[0:n_sums]
        gwout_ref = rest[n_sums]
        du_ref, vec_ref, wab_ref = rest[n_sums + 1:n_sums + 4]
        hrecvs = rest[n_sums + 4:2 * n_sums + 4]
        sib_wout = rest[2 * n_sums + 4]
        (du_s, dy_s, dpa_s, dpx_s, dxr_s, wabd, wxbd, acc, dwa_acc, dwx_acc,
         a_car, dh_car, dcq_car, dxr_car, i_send, i_recv, d_send, d_recv) = rest[2 * n_sums + 5:]
        step = pl.program_id(0)
        _host_chip_exchange(step, n_tiles, hsends, hrecvs, i_send, i_recv)
        _host_pair_exchange(step, n_tiles, [gwout_ref], [sib_wout], d_send, d_recv)
        has_prev = (step < n_tiles - 1).astype(F32)

        @pl.when(step == 0)
        def _():
            acc[...] = jnp.zeros(acc.shape, F32)
            dwa_acc[...] = jnp.zeros(dwa_acc.shape, F32)
            dwx_acc[...] = jnp.zeros(dwx_acc.shape, F32)
            a_car[...] = jnp.ones(a_car.shape, F32)
            dh_car[...] = jnp.zeros(dh_car.shape, F32)
            dcq_car[...] = jnp.zeros(dcq_car.shape, F32)
            dxr_car[...] = jnp.zeros(dxr_car.shape, F32)
            wabd[...] = _expand_heads(wa_ref[...])
            wxbd[...] = _expand_heads(wx_ref[...])

        row_c = lax.broadcasted_iota(jnp.int32, (SUB, CONV_WIDTH), 0)
        row_r = lax.broadcasted_iota(jnp.int32, (SUB, LRU_WIDTH), 0)
        cw = cw_ref[...]
        rw = rw_ref[...]
        g_c = gnc_ref[...]
        g_r = gnr_ref[...]
        sp_c = LRU_C * _softplus_neg(lam_ref[...])

        hs_before = hp_ref[...] * has_prev

        dy_s[...] = _dot_nt(dx1_ref[...].astype(BF16), wout_ref[...])

        def recur_bwd(j, carry):
            a_later, dh_later = carry
            i = n_chunks - 1 - j
            r = pl.multiple_of(i * SUB, SUB)
            rp = pl.multiple_of(jnp.maximum(i - 1, 0) * SUB, SUB)
            xr = xr_ref[pl.ds(r, SUB), :]
            hs_c = hs_ref[pl.ds(r, SUB), :]
            hs_prev = jnp.where(i == 0, hs_before, hs_ref[pl.ds(rp, SUB), :])
            h_m1 = _down(hs_c, hs_prev, 1, row_r)
            ra = ra_ref[pl.ds(r, SUB), :]
            ii = ii_ref[pl.ds(r, SUB), :]
            mult = mult_ref[pl.ds(r, SUB), :]
            a = jnp.exp(-ra * sp_c)
            inv_mult = lax.rsqrt(mult * mult)
            ge, dge = _gelu(u_ref[pl.ds(r, SUB), OFF_G:OFF_G + LRU_WIDTH])
            y_r = hs_c * ge
            rr = _rms(y_r)
            yhat = y_r * rr
            dyn = dy_s[pl.ds(r, SUB), CONV_WIDTH:MIX_WIDTH]
            acc[ACC_GNR] += dyn * yhat
            dy_r = _rms_bwd(dyn, yhat, rr, g_r)
            du_s[pl.ds(r, SUB), OFF_G:OFF_G + LRU_WIDTH] = dy_r * hs_c * dge
            a_cum, d_cum = _scan8_rev(_up(a, a_later, 1, row_r), dy_r * ge, row_r)
            dh = a_cum * dh_later + d_cum
            dm = dh * mult
            dii = dm * xr
            dxr_s[pl.ds(r, SUB), :] = dm * ii
            dla = a * dh * (h_m1 - (ii * xr) * a * inv_mult)
            dla_r = dla * ra
            acc[ACC_SP] -= dla_r
            dpa = dla_r * (sp_c * (ra - 1.0))
            dpx = dii * ii * (1.0 - ii)
            acc[ACC_BA] += dpa
            acc[ACC_BX] += dpx
            dpa_s[pl.ds(r, SUB), :] = dpa
            dpx_s[pl.ds(r, SUB), :] = dpx
            return a, dh[0:1, :]

        a_first, dh_first = _chunk_loop(n_chunks, recur_bwd, (a_car[...], dh_car[...]))
        a_car[...] = a_first
        dh_car[...] = dh_first

        for g in range(LRU_WIDTH // GROUP):
            cols = slice(g * GROUP, (g + 1) * GROUP)
            dpab = dpa_s[:, cols].astype(BF16)
            dpxb = dpx_s[:, cols].astype(BF16)
            xrb = xr_ref[:, cols].astype(BF16)
            dxr_s[:, cols] += _dot_nt(dpab, wabd[cols, :]) + _dot_nt(dpxb, wxbd[cols, :])
            dwa_acc[cols, :] += _dot_tn(xrb, dpab)
            dwx_acc[cols, :] += _dot_tn(xrb, dpxb)

        def convs_bwd(j, carry):
            dcq_later, dxr_later = carry
            i = n_chunks - 1 - j
            r = pl.multiple_of(i * SUB, SUB)
            gb = u_ref[pl.ds(r, SUB), OFF_GB:OFF_GB + CONV_WIDTH]
            gc = u_ref[pl.ds(r, SUB), OFF_GC:OFF_GC + CONV_WIDTH]
            v = u_ref[pl.ds(r, SUB), OFF_V:OFF_V + CONV_WIDTH]
            cv = gc * v
            cq = cq_ref[pl.ds(r, SUB), :]
            y_c = gb * cq
            rc = _rms(y_c)
            yhat = y_c * rc
            dyn = dy_s[pl.ds(r, SUB), 0:CONV_WIDTH]
            acc[ACC_GNC, :, 0:CONV_WIDTH] += dyn * yhat
            dy_c = _rms_bwd(dyn, yhat, rc, g_c)
            dcq = dy_c * gb
            ahead3 = [dcq, _up(dcq, dcq_later, 1, row_c), _up(dcq, dcq_later, 2, row_c)]
            dcv = cw[2:3, :] * ahead3[0] + cw[1:2, :] * ahead3[1] + cw[0:1, :] * ahead3[2]
            for k in range(3):
                acc[ACC_CW + 2 - k, :, 0:CONV_WIDTH] += ahead3[k] * cv
            du_s[pl.ds(r, SUB), OFF_GB:OFF_GB + CONV_WIDTH] = dy_c * cq
            du_s[pl.ds(r, SUB), OFF_GC:OFF_GC + CONV_WIDTH] = dcv * v
            du_s[pl.ds(r, SUB), OFF_V:OFF_V + CONV_WIDTH] = dcv * gc

            xin = u_ref[pl.ds(r, SUB), OFF_XR:OFF_XR + LRU_WIDTH]
            dxr = dxr_s[pl.ds(r, SUB), :]
            ahead = [dxr] + [_up(dxr, dxr_later, k, row_r) for k in (1, 2, 3)]
            du_s[pl.ds(r, SUB), OFF_XR:OFF_XR + LRU_WIDTH] = (
                rw[3:4, :] * ahead[0] + rw[2:3, :] * ahead[1] + rw[1:2, :] * ahead[2] + rw[0:1, :] * ahead[3])
            for k in range(4):
                acc[ACC_RW + 3 - k] += ahead[k] * xin
            acc[ACC_BR] += dxr
            return dcq, dxr

        dcq_first, dxr_first = _chunk_loop(n_chunks, convs_bwd, (dcq_car[...], dxr_car[...]))
        dcq_car[...] = dcq_first
        dxr_car[...] = dxr_first

        du_ref[...] = du_s[...].astype(BF16)

        @pl.when(step == n_tiles - 1)
        def _():
            vec_ref[...] = jnp.zeros(vec_ref.shape, F32)
            rows = {ACC_GNC: ROW_GNC, ACC_GNR: ROW_GNR, ACC_BR: ROW_BR, ACC_BA: ROW_BA, ACC_BX: ROW_BX}
            for k in range(3):
                rows[ACC_CW + k] = ROW_CW + k
            for k in range(4):
                rows[ACC_RW + k] = ROW_RW + k
            for slot, out_row in rows.items():
                o = out_row - ROW_GNC
                vec_ref[o:o + 1, :] = jnp.sum(acc[slot], axis=0, keepdims=True)
            lam_v = lam_ref[...]
            dsp = jnp.sum(acc[ACC_SP], axis=0, keepdims=True)
            o = ROW_LAM - ROW_GNC
            vec_ref[o:o + 1, :] = -dsp * LRU_C / (1.0 + jnp.exp(lam_v))
            wab_ref[0:LRU_WIDTH, :] = _fold_heads(dwa_acc[...])
            wab_ref[LRU_WIDTH:2 * LRU_WIDTH, :] = _fold_heads(dwx_acc[...])

    rev = lambda w: pl.BlockSpec((tm, w), lambda s: (n_tiles - 1 - s, 0))
    before = lambda w: pl.BlockSpec((SUB, w), lambda s: (jnp.maximum((n_tiles - 1 - s) * per_tile - 1, 0), 0))
    whole = lambda a: pl.BlockSpec(a.shape, lambda s: (0,) * a.ndim)
    smalls = (conv_w, rnn_conv_w, wa, wx, lam, gnc, gnr, w_out)
    full = lambda w: pltpu.VMEM((tm, w), F32)
    return pl.pallas_call(
        body, grid=(n_tiles,),
        in_specs=[rev(IN_COLS), rev(LRU_WIDTH), before(LRU_WIDTH), rev(D_MODEL)]
        + [rev(a.shape[1]) for a in saved] + [whole(a) for a in smalls] + [HBM_SPEC] * (n_sums + 1),
        out_specs=[rev(IN_COLS), pl.BlockSpec((16, D_MODEL), lambda s: (0, 0)),
                   pl.BlockSpec((2 * LRU_WIDTH, HEAD_DIM), lambda s: (0, 0))] + [HBM_SPEC] * (n_sums + 1),
        out_shape=[jax.ShapeDtypeStruct((t_len, IN_COLS), BF16), jax.ShapeDtypeStruct((16, D_MODEL), F32),
                   jax.ShapeDtypeStruct((2 * LRU_WIDTH, HEAD_DIM), F32)]
        + [jax.ShapeDtypeStruct(s.shape, BF16) for s in chip_sums]
        + [jax.ShapeDtypeStruct((4,) + g_wout.shape[1:], BF16)],
        scratch_shapes=[full(IN_COLS), full(MIX_WIDTH), full(LRU_WIDTH), full(LRU_WIDTH), full(LRU_WIDTH),
                        pltpu.VMEM((LRU_WIDTH, GROUP), BF16), pltpu.VMEM((LRU_WIDTH, GROUP), BF16),
                        pltpu.VMEM((N_ACC, SUB, LRU_WIDTH), F32),
                        pltpu.VMEM((LRU_WIDTH, GROUP), F32), pltpu.VMEM((LRU_WIDTH, GROUP), F32),
                        pltpu.VMEM((SUB, LRU_WIDTH), F32), pltpu.VMEM((1, LRU_WIDTH), F32),
                        pltpu.VMEM((SUB, CONV_WIDTH), F32), pltpu.VMEM((SUB, LRU_WIDTH), F32)]
        + _exchange_scratch(n_sums, 3) + _exchange_scratch(1, 4),
        compiler_params=_params(("arbitrary",), 56), name="mixer_bwd",
    )(u, hs, hs, dx1, *saved, *smalls, *chip_sums, g_wout)


def _in_proj_bwd(du, dx1, x, g_mix, win_t, tm, chip_sums, g_own):
    t_len = x.shape[0]
    n_steps = t_len // tm

    def body(du_ref, dx1_ref, x_ref, g_ref, w_ref, hs_ref, gown_ref,
             dx_ref, vec_ref, landed_ref, sib_ref, i_send, i_recv, d_send, d_recv):
        step = pl.program_id(0)
        _host_chip_exchange(step, n_steps, [hs_ref], [landed_ref], i_send, i_recv)
        _host_half_exchange(step, n_steps, gown_ref, sib_ref, d_send, d_recv)

        @pl.when(step == 0)
        def _():
            vec_ref[...] = jnp.zeros(vec_ref.shape, F32)

        dh = jnp.dot(du_ref[...], w_ref[...], preferred_element_type=F32)
        xv = x_ref[...]
        r1 = _rms(xv)
        xh = xv * r1
        vec_ref[0:1, :] += jnp.sum(dh * xh, axis=0, keepdims=True)
        dx_ref[...] = dx1_ref[...] + _rms_bwd(dh, xh, r1, g_ref[...])

    row_tile = lambda w: pl.BlockSpec((tm, w), lambda i: (i, 0))
    half_shape = (g_own.shape[0], g_own.shape[1] // 2, g_own.shape[2])
    return pl.pallas_call(
        body, grid=(n_steps,),
        in_specs=[row_tile(IN_COLS), row_tile(D_MODEL), row_tile(D_MODEL), pl.BlockSpec((1, D_MODEL), lambda i: (0, 0)),
                  pl.BlockSpec((IN_COLS, D_MODEL), lambda i: (0, 0))] + [HBM_SPEC] * 2,
        out_specs=[row_tile(D_MODEL), pl.BlockSpec((SUB, D_MODEL), lambda i: (0, 0))] + [HBM_SPEC] * 2,
        out_shape=[jax.ShapeDtypeStruct((t_len, D_MODEL), F32), jax.ShapeDtypeStruct((SUB, D_MODEL), F32),
                   jax.ShapeDtypeStruct(chip_sums.shape, BF16), jax.ShapeDtypeStruct(half_shape, BF16)],
        scratch_shapes=_exchange_scratch(1, 3) + [pltpu.SemaphoreType.DMA((1,)), pltpu.SemaphoreType.DMA((1,))],
        compiler_params=_params(("arbitrary",), 56), name="in_proj_bwd",
    )(du, dx1, x, g_mix, win_t, chip_sums, g_own)


def _tn_weight_grad(a, b, tk, name, pair=(), col_blocks=1):
    t_len, m = a.shape
    n = b.shape[1]
    n_steps = t_len // tk
    sent = tuple(pair)
    n_sent = len(sent)

    def body(a_ref, b_ref, *rest):
        srcs = rest[0:n_sent]
        o_ref = rest[n_sent]
        dsts = rest[n_sent + 1:2 * n_sent + 1]
        acc = rest[2 * n_sent + 1]
        sems = rest[2 * n_sent + 2:]
        j = pl.program_id(0)
        if pair:
            _host_pair_exchange(j, n_steps, srcs, dsts, *sems)

        @pl.when(j == 0)
        def _():
            acc[...] = jnp.zeros(acc.shape, F32)

        acc[...] += _dot_tn(a_ref[...].astype(BF16), b_ref[...].astype(BF16))

        @pl.when(j == n_steps - 1)
        def _():
            if col_blocks == 1:
                o_ref[...] = acc[...].astype(BF16)
            else:
                for k in range(col_blocks):
                    o_ref[k] = acc[:, k * nb:(k + 1) * nb].astype(BF16)

    nb = n // col_blocks
    out_dims = (m, n) if col_blocks == 1 else (col_blocks, m, nb)
    landed = [jax.ShapeDtypeStruct((4,) + g.shape[1:], BF16) for g in pair]
    scratch = [pltpu.VMEM((m, n), F32)]
    if n_sent:
        scratch += _exchange_scratch(n_sent, 4)
    return pl.pallas_call(
        body, grid=(n_steps,),
        in_specs=[pl.BlockSpec((tk, m), lambda j: (j, 0)), pl.BlockSpec((tk, n), lambda j: (j, 0))]
        + [HBM_SPEC] * n_sent,
        out_specs=[pl.BlockSpec(out_dims, lambda j: (0,) * len(out_dims))] + [HBM_SPEC] * n_sent,
        out_shape=[jax.ShapeDtypeStruct(out_dims, BF16)] + landed,
        scratch_shapes=scratch,
        compiler_params=_params(("arbitrary",), 56), name=name,
    )(a, b, *sent)


def _w_in_grad_part(du, h, tk, name, chip_ids, chip=(), halves=None, small=None):
    t_len = du.shape[0]
    n_t = t_len // tk
    n_q = chip_ids.shape[0]
    width = 2 * (IN_COLS // N_DEV)
    n_steps = n_q * n_t
    n_chip = len(chip)
    sent = tuple(chip) + (() if halves is None else (halves,)) + (() if small is None else tuple(small))
    n_sent = len(sent)

    def body(ids_ref, a_ref, b_ref, *rest):
        srcs = rest[0:n_sent]
        o_ref = rest[n_sent]
        dsts = rest[n_sent + 1:2 * n_sent + 1]
        acc = rest[2 * n_sent + 1]
        sems = list(rest[2 * n_sent + 2:])
        j = pl.program_id(1)
        step = pl.program_id(0) * n_t + j
        if chip:
            _host_chip_exchange(step, n_steps, srcs[0:n_chip], dsts[0:n_chip], sems.pop(0), sems.pop(0))
        if halves is not None:
            _host_half_exchange(step, n_steps, srcs[n_chip], dsts[n_chip], sems.pop(0), sems.pop(0))
        if small is not None:
            _host_small_exchange(step, n_steps, *srcs[n_sent - 3:], *dsts[n_sent - 3:], *sems)

        @pl.when(j == 0)
        def _():
            acc[...] = jnp.zeros(acc.shape, F32)

        acc[...] += _dot_tn(a_ref[...], b_ref[...])

        @pl.when(j == n_t - 1)
        def _():
            o_ref[0] = acc[...].astype(BF16)

    landed = [jax.ShapeDtypeStruct(s.shape, BF16) for s in chip]
    scratch = [pltpu.VMEM((width, D_MODEL), F32)]
    if chip:
        scratch += _exchange_scratch(len(chip), 3)
    if halves is not None:
        landed.append(jax.ShapeDtypeStruct((halves.shape[0], halves.shape[1] // 2, halves.shape[2]), BF16))
        scratch += [pltpu.SemaphoreType.DMA((halves.shape[0],)), pltpu.SemaphoreType.DMA((halves.shape[0],))]
    if small is not None:
        vec_m, vec_b, wab = small
        landed += [jax.ShapeDtypeStruct((N_DEV,) + vec_m.shape, F32), jax.ShapeDtypeStruct((N_DEV,) + vec_b.shape, F32),
                   jax.ShapeDtypeStruct((N_DEV, wab.shape[0] // N_DEV, wab.shape[1]), F32)]
        scratch += _exchange_scratch(3, N_DEV) + [pltpu.SemaphoreType.DMA((2,))]
    grid_spec = pltpu.PrefetchScalarGridSpec(
        num_scalar_prefetch=1, grid=(n_q, n_t),
        in_specs=[pl.BlockSpec((tk, width), lambda q, j, ids: (j, ids[q])),
                  pl.BlockSpec((tk, D_MODEL), lambda q, j, ids: (j, 0))] + [HBM_SPEC] * n_sent,
        out_specs=[pl.BlockSpec((1, width, D_MODEL), lambda q, j, ids: (q, 0, 0))] + [HBM_SPEC] * n_sent,
        scratch_shapes=scratch)
    return pl.pallas_call(
        body, grid_spec=grid_spec, out_shape=[jax.ShapeDtypeStruct((n_q, width, D_MODEL), BF16)] + landed,
        compiler_params=_params(("arbitrary", "arbitrary"), 40), name=name,
    )(chip_ids, du, h, *sent)


def _adamw(w, g, m, v):
    m = ADAM_B1 * m + (1.0 - ADAM_B1) * g
    v = ADAM_B2 * v + (1.0 - ADAM_B2) * (g * g)
    delta = -ADAM_LR * ((m / BC1) / (jnp.sqrt(v / BC2) + ADAM_EPS) + ADAM_WD * w)
    return delta, m, v


def _update_sharded(g, landed, w, m, v, rows_blk, name):
    rows, cols = w.shape

    def body(g_ref, l_ref, w_ref, m_ref, v_ref, og, od, om, ov):
        gv = g_ref[...]
        for j in range(3):
            gv = gv + l_ref[j].astype(F32)
        delta, mn, vn = _adamw(w_ref[...], gv, m_ref[...], v_ref[...])
        og[...] = gv
        od[...] = delta
        om[...] = mn
        ov[...] = vn

    blk = pl.BlockSpec((rows_blk, cols), lambda i: (i, 0))
    shape = pltpu.HBM((rows, cols), F32)
    return pl.pallas_call(
        body, grid=(rows // rows_blk,),
        in_specs=[blk, pl.BlockSpec((3, rows_blk, cols), lambda i: (0, i, 0)), blk, blk, blk],
        out_specs=[blk] * 4, out_shape=[shape] * 4,
        compiler_params=_params(("arbitrary",), 32), name=name,
    )(*_in_hbm(g, landed, w, m, v))


def _update_w_in(g_own, sib_own, landed, w_t, m_t, v_t, core, cols_blk):
    rows, cols = w_t.shape

    def body(core_ref, g_ref, s_ref, l_ref, w_ref, m_ref, v_ref, og, od, om, ov):
        gv = g_ref[0, 0].astype(F32) + s_ref[0].astype(F32)
        for j in range(3):
            gv = gv + l_ref[j].astype(F32)
        delta, mn, vn = _adamw(w_ref[...], gv, m_ref[...], v_ref[...])
        og[...] = gv
        od[...] = delta
        om[...] = mn
        ov[...] = vn

    blk = pl.BlockSpec((rows, cols_blk), lambda i, cr: (0, i))
    grid_spec = pltpu.PrefetchScalarGridSpec(
        num_scalar_prefetch=1, grid=(cols // cols_blk,),
        in_specs=[pl.BlockSpec((1, 1, rows, cols_blk), lambda i, cr: (0, cr[0], 0, i)),
                  pl.BlockSpec((1, rows, cols_blk), lambda i, cr: (0, 0, i)),
                  pl.BlockSpec((3, rows, cols_blk), lambda i, cr: (0, 0, i)), blk, blk, blk],
        out_specs=[blk] * 4)
    return pl.pallas_call(
        body, grid_spec=grid_spec, out_shape=[pltpu.HBM((rows, cols), F32)] * 4,
        compiler_params=_params(("arbitrary",), 32), name="update_w_in",
    )(core, *_in_hbm(g_own.reshape(1, 2, rows, cols), sib_own, landed, w_t, m_t, v_t))


def _update_small(vsum, wsum, g_cw, g_rw, weights, moments_m, moments_v):
    n = len(weights)

    def body(*refs):
        vs, ws, gcw, grw = refs[0:4]
        w_refs = refs[4:4 + n]
        m_refs = refs[4 + n:4 + 2 * n]
        v_refs = refs[4 + 2 * n:4 + 3 * n]
        outs = refs[4 + 3 * n:]
        loss_ref = outs[0]
        loss_ref[...] = jnp.sum(vs[ROW_LOSS:ROW_LOSS + 1, :], axis=1, keepdims=True)
        grads = [
            vs[ROW_GMIX:ROW_GMIX + 1, :], gcw[...], grw[...], vs[ROW_BR:ROW_BR + 1, :],
            ws[0:LRU_WIDTH, :], vs[ROW_BA:ROW_BA + 1, :], ws[LRU_WIDTH:2 * LRU_WIDTH, :], vs[ROW_BX:ROW_BX + 1, :],
            vs[ROW_LAM:ROW_LAM + 1, :], vs[ROW_GNC:ROW_GNC + 1, 0:CONV_WIDTH], vs[ROW_GNR:ROW_GNR + 1, :],
            vs[ROW_GMLP:ROW_GMLP + 1, :], vs[ROW_GF:ROW_GF + 1, :],
        ]
        for k in range(n):
            gk = grads[k]
            delta, mn, vn = _adamw(w_refs[k][...], gk, m_refs[k][...], v_refs[k][...])
            outs[1 + 4 * k][...] = gk
            outs[2 + 4 * k][...] = delta
            outs[3 + 4 * k][...] = mn
            outs[4 + 4 * k][...] = vn

    whole = lambda a: pl.BlockSpec(a.shape, lambda i: (0,) * len(a.shape))
    out_shape = [jax.ShapeDtypeStruct((1, 1), F32)]
    for w in weights:
        out_shape += [jax.ShapeDtypeStruct(w.shape, F32)] * 4
    args = (vsum, wsum, g_cw, g_rw, *weights, *moments_m, *moments_v)
    return pl.pallas_call(
        body, grid=(1,), out_shape=out_shape, in_specs=[whole(a) for a in args], out_specs=[whole(s) for s in out_shape],
        compiler_params=_params(("arbitrary",), 32), name="update_small",
    )(*args)


def kernel(x, norm_mix_g, w_in, conv_w, rnn_conv_w, rnn_conv_b, w_a, b_a, w_x, b_x, lru_lambda, g_norm_conv, g_norm_rnn, w_out, norm_mlp_g, w_mlp_in, w_mlp_out, final_norm_g, loss_target, m_norm_mix_g, m_w_in, m_conv_w, m_rnn_conv_w, m_rnn_conv_b, m_w_a, m_b_a, m_w_x, m_b_x, m_lru_lambda, m_g_norm_conv, m_g_norm_rnn, m_w_out, m_norm_mlp_g, m_w_mlp_in, m_w_mlp_out, m_final_norm_g, v_norm_mix_g, v_w_in, v_conv_w, v_rnn_conv_w, v_rnn_conv_b, v_w_a, v_b_a, v_w_x, v_b_x, v_lru_lambda, v_g_norm_conv, v_g_norm_rnn, v_w_out, v_norm_mlp_g, v_w_mlp_in, v_w_mlp_out, v_final_norm_g):
    t_len = x.shape[1]
    my_id = 4 * lax.axis_index("x") + 2 * lax.axis_index("y") + lax.axis_index("c")
    tm = min(256, t_len)
    tb = min(512, t_len)
    tk = min(512, t_len)

    xs = x.reshape(t_len, D_MODEL)
    tgt = loss_target.reshape(t_len, D_MODEL)
    flat = lambda a: a.reshape(a.shape[-2:]) if a.ndim == 3 else a.reshape(1, -1)
    heads = lambda a: a.reshape(LRU_WIDTH, HEAD_DIM)

    turned = lambda a: jnp.transpose(flat(a))
    win_shard, wout_shard, w1_shard, w2_shard, cp_shard = _prep_shards(
        turned(w_in), flat(w_out), flat(w_mlp_in), flat(w_mlp_out), flat(conv_w), flat(rnn_conv_w))

    u, h, win_t, cp_full = _in_proj(xs, flat(norm_mix_g), (win_shard, cp_shard), min(1024, t_len))
    cpack = cp_full.reshape(N_DEV, 8, 128)
    conv_full = jnp.transpose(cpack[:, 0:3, 0:64], (1, 0, 2)).reshape(3, CONV_WIDTH)
    rnn_full = jnp.transpose(cpack[:, 3:7, :], (1, 0, 2)).reshape(4, LRU_WIDTH)
    mixer_small = (conv_full, rnn_full, flat(rnn_conv_b), heads(w_a), flat(b_a), heads(w_x), flat(b_x),
                   flat(lru_lambda), flat(g_norm_conv), flat(g_norm_rnn))
    hs, y, xr, gate_r, gate_i, mult, cq, w1_blk, wout_blk = _mixer_fwd(u, *mixer_small, (w1_shard, wout_shard), tm)
    wout_f = wout_blk.reshape(MIX_WIDTH, D_MODEL)
    x1, h2, z, w2_blk = _mlp_up(xs, y, flat(norm_mlp_g), wout_f, w1_blk, w2_shard, tb)
    dx1, dx2, vec_m, dpre = _mlp_down_bwd(x1, z, tgt, flat(norm_mlp_g), flat(final_norm_g), w1_blk,
                                          w2_blk.reshape(D_FF, D_MODEL), tb)
    (g_w1,) = _tn_weight_grad(h2, dpre, tk, "w_mlp_in_grad", col_blocks=N_DEV)
    (g_w2,) = _tn_weight_grad(z, dx2, tk, "w_mlp_out_grad")
    g_w2 = g_w2.reshape(N_DEV, D_FF // N_DEV, D_MODEL)
    g_wout, sib_w1, sib_w2 = _tn_weight_grad(y, dx1, tk, "w_out_grad", pair=(g_w1, g_w2))
    g_wout = g_wout.reshape(N_DEV, MIX_WIDTH // N_DEV, D_MODEL)
    hsend_w1, own_w1, hsend_w2, own_w2 = _pair_sum((g_w1, g_w2), (sib_w1, sib_w2), "pair_sum_w_mlp")
    du, vec_b, wab, landed_w1, landed_w2, sib_wout = _mixer_bwd(
        u, hs, dx1, (xr, gate_r, gate_i, mult, cq), conv_full, rnn_full, heads(w_a), heads(w_x),
        flat(lru_lambda), flat(g_norm_conv), flat(g_norm_rnn), wout_f, (hsend_w1, hsend_w2), g_wout, tm)
    hsend_wout, own_wout = _pair_sum((g_wout,), (sib_wout,), "pair_sum_w_out")
    ax, ay, ac = lax.axis_index("x"), lax.axis_index("y"), lax.axis_index("c")
    chip_ids = jnp.stack([2 * cx + cy for cx, cy in [(ax, ay)] + _other_chips(ax, ay)]).astype(jnp.int32)
    core = jnp.reshape(ac, (1,)).astype(jnp.int32)
    tw = min(1024, t_len)
    g_others, landed_wout, vrecv_m, vrecv_b, wrecv = _w_in_grad_part(
        du, h, tw, "w_in_grad_others", chip_ids[1:4], chip=(hsend_wout,), small=(vec_m, vec_b, wab))
    g_own, sib_others = _w_in_grad_part(du, h, tw, "w_in_grad_own", chip_ids[0:1], halves=g_others)
    hsend_win = _pair_sum_parts(g_others, sib_others, core)
    grad_x, vec_x, landed_win, sib_own = _in_proj_bwd(du, dx1, xs, flat(norm_mix_g), win_t, tm, hsend_win, g_own)

    vsum, wsum = _final_small(vrecv_m, vrecv_b, wab, wrecv, vec_x)

    up_win = _update_w_in(g_own, sib_own, landed_win, turned(w_in), turned(m_w_in), turned(v_w_in), core, 256)
    up_win = [jnp.transpose(a) for a in up_win]
    up_wout = _update_sharded(own_wout, landed_wout, flat(w_out), flat(m_w_out), flat(v_w_out), 96, "update_w_out")
    up_w1 = _update_sharded(own_w1, landed_w1, flat(w_mlp_in), flat(m_w_mlp_in), flat(v_w_mlp_in), 256,
                            "update_w_mlp_in")
    up_w2 = _update_sharded(own_w2, landed_w2, flat(w_mlp_out), flat(m_w_mlp_out), flat(v_w_mlp_out), 256,
                            "update_w_mlp_out")

    g_cw = lax.dynamic_slice(vsum, (ROW_CW, 64 * my_id), (3, 64))
    g_rw = lax.dynamic_slice(vsum, (ROW_RW, 128 * my_id), (4, 128))
    small_w = (norm_mix_g, conv_w, rnn_conv_w, rnn_conv_b, w_a, b_a, w_x, b_x, lru_lambda, g_norm_conv, g_norm_rnn,
               norm_mlp_g, final_norm_g)
    small_m = (m_norm_mix_g, m_conv_w, m_rnn_conv_w, m_rnn_conv_b, m_w_a, m_b_a, m_w_x, m_b_x, m_lru_lambda,
               m_g_norm_conv, m_g_norm_rnn, m_norm_mlp_g, m_final_norm_g)
    small_v = (v_norm_mix_g, v_conv_w, v_rnn_conv_w, v_rnn_conv_b, v_w_a, v_b_a, v_w_x, v_b_x, v_lru_lambda,
               v_g_norm_conv, v_g_norm_rnn, v_norm_mlp_g, v_final_norm_g)
    is_heads = (False, False, False, False, True, False, True, False, False, False, False, False, False)
    as2d = lambda arrs: [heads(a) if hd else flat(a) for a, hd in zip(arrs, is_heads)]
    small_out = _update_small(vsum, wsum, g_cw, g_rw, as2d(small_w), as2d(small_m), as2d(small_v))
    loss = small_out[0].reshape(())

    names = ["norm_mix_g", "w_in", "conv_w", "rnn_conv_w", "rnn_conv_b", "w_a", "b_a", "w_x", "b_x", "lru_lambda",
             "g_norm_conv", "g_norm_rnn", "w_out", "norm_mlp_g", "w_mlp_in", "w_mlp_out", "final_norm_g"]
    originals = dict(zip(names, (norm_mix_g, w_in, conv_w, rnn_conv_w, rnn_conv_b, w_a, b_a, w_x, b_x, lru_lambda,
                                 g_norm_conv, g_norm_rnn, w_out, norm_mlp_g, w_mlp_in, w_mlp_out, final_norm_g)))
    results = {"w_in": up_win, "w_out": up_wout, "w_mlp_in": up_w1, "w_mlp_out": up_w2}
    small_names = ["norm_mix_g", "conv_w", "rnn_conv_w", "rnn_conv_b", "w_a", "b_a", "w_x", "b_x", "lru_lambda",
                   "g_norm_conv", "g_norm_rnn", "norm_mlp_g", "final_norm_g"]
    for k, nm in enumerate(small_names):
        results[nm] = small_out[1 + 4 * k:5 + 4 * k]
    out = [loss, grad_x.reshape(x.shape)]
    for kind in range(4):
        out += [results[nm][kind].reshape(originals[nm].shape) for nm in names]
    return tuple(out)
```

```python
import functools

import jax
import jax.numpy as jnp
from jax import lax
from jax.experimental import pallas as pl
from jax.experimental.pallas import tpu as pltpu

F32 = jnp.float32
BF16 = jnp.bfloat16

D_MODEL = 1024
HEAD_DIM = 64
CONV_WIDTH = 512
LRU_WIDTH = 1024
MIX_WIDTH = CONV_WIDTH + LRU_WIDTH
IN_COLS = 3 * CONV_WIDTH + 2 * LRU_WIDTH
D_FF = 4 * D_MODEL
GROUP = 256
EPS = 1e-6
LRU_C = 8.0
N_DEV = 8
SUB = 8

OFF_GB, OFF_GC, OFF_V, OFF_XR, OFF_G = 0, 512, 1024, 1536, 2560

ADAM_LR, ADAM_B1, ADAM_B2, ADAM_EPS, ADAM_WD, ADAM_STEP = 0.001, 0.9, 0.999, 1e-08, 0.01, 10
BC1 = 1.0 - ADAM_B1 ** ADAM_STEP
BC2 = 1.0 - ADAM_B2 ** ADAM_STEP

MIB = 1024 * 1024
MESH = pl.DeviceIdType.MESH

VEC_ROWS = 32
ROW_GF, ROW_GMLP, ROW_LOSS = 0, 1, 2
ROW_GNC, ROW_GNR, ROW_BR, ROW_BA, ROW_BX, ROW_LAM, ROW_CW, ROW_RW = 8, 9, 10, 11, 12, 13, 14, 17
ROW_GMIX = 24
ACC_GNC, ACC_GNR, ACC_BR, ACC_BA, ACC_BX, ACC_SP, ACC_CW, ACC_RW, N_ACC = 0, 1, 2, 3, 4, 5, 6, 9, 13


def _params(semantics=None, vmem_mib=48):
    return pltpu.CompilerParams(dimension_semantics=semantics, vmem_limit_bytes=vmem_mib * MIB)


def _rms(x):
    return lax.rsqrt(jnp.mean(x * x, axis=-1, keepdims=True) + EPS)


def _rms_bwd(dy, xhat, r, g):
    dyh = dy * g
    return r * (dyh - xhat * jnp.mean(dyh * xhat, axis=-1, keepdims=True))


def _sigmoid(x):
    return 0.5 + 0.5 * jnp.tanh(0.5 * x)


def _gelu(x):
    c0, c1 = 0.7978845608028654, 0.044715
    x2 = x * x
    t = jnp.tanh(x * (c0 + (c0 * c1) * x2))
    half = 0.5 + 0.5 * t
    ge = x * half
    dge = half + (ge - ge * half) * (2.0 * c0 + (6.0 * c0 * c1) * x2)
    return ge, dge


def _softplus_neg(lam):
    z = -lam
    e = jnp.exp(-jnp.abs(z))
    return jnp.maximum(z, 0.0) + jnp.where(e < 1e-4, e * (1.0 - 0.5 * e), jnp.log(1.0 + e))


def _lru_gates(pa, px, sp_c):
    ra = _sigmoid(pa)
    ii = _sigmoid(px)
    neg_la = ra * sp_c
    a = jnp.exp(-neg_la)
    m2 = jnp.tanh(neg_la) * (1.0 + a * a)
    mult = jnp.where(m2 > 0.0, m2 * lax.rsqrt(m2), 0.0)
    return ra, ii, a, mult


def _down(cur, prev, s, row):
    return pltpu.roll(jnp.where(row < SUB - s, cur, prev), s, 0)


def _up(cur, nxt, s, row):
    return pltpu.roll(jnp.where(row >= s, cur, nxt), SUB - s, 0)


def _scan8_fwd(a, b, row):
    for s in (1, 2, 4):
        m = row >= s
        a_sh = pltpu.roll(a, s, 0)
        b_sh = pltpu.roll(b, s, 0)
        b = jnp.where(m, a * b_sh + b, b)
        a = jnp.where(m, a * a_sh, a)
    return a, b


def _scan8_rev(a, b, row):
    for s in (1, 2, 4):
        m = row < SUB - s
        a_sh = pltpu.roll(a, SUB - s, 0)
        b_sh = pltpu.roll(b, SUB - s, 0)
        b = jnp.where(m, a * b_sh + b, b)
        a = jnp.where(m, a * a_sh, a)
    return a, b


def _group_mask(shape):
    r = lax.broadcasted_iota(jnp.int32, shape, 0)
    c = lax.broadcasted_iota(jnp.int32, shape, 1)
    return ((r % GROUP) // HEAD_DIM) == (c // HEAD_DIM)


def _expand_heads(w):
    j = lax.broadcasted_iota(jnp.int32, (HEAD_DIM, GROUP), 0)
    c = lax.broadcasted_iota(jnp.int32, (HEAD_DIM, GROUP), 1)
    spread = (c % HEAD_DIM == j).astype(BF16)
    e = jnp.dot(w.astype(BF16), spread, preferred_element_type=F32)
    return jnp.where(_group_mask(e.shape), e, 0.0).astype(BF16)


def _fold_heads(p):
    p = jnp.where(_group_mask(p.shape), p, 0.0)
    c = lax.broadcasted_iota(jnp.int32, (GROUP, HEAD_DIM), 0)
    j = lax.broadcasted_iota(jnp.int32, (GROUP, HEAD_DIM), 1)
    fold = (c % HEAD_DIM == j).astype(BF16)
    hi = p.astype(BF16)
    rest = p - hi.astype(F32)
    mid = rest.astype(BF16)
    lo = (rest - mid.astype(F32)).astype(BF16)
    dot = functools.partial(jnp.dot, preferred_element_type=F32)
    return dot(hi, fold) + dot(mid, fold) + dot(lo, fold)


def _block_diag_apply(xb, wbd_ref):
    parts = [jnp.dot(xb[:, g * GROUP:(g + 1) * GROUP], wbd_ref[g * GROUP:(g + 1) * GROUP, :],
                     preferred_element_type=F32) for g in range(LRU_WIDTH // GROUP)]
    return jnp.concatenate(parts, axis=1)


def _block_diag_apply_t(db, wbd_ref):
    parts = [lax.dot_general(db[:, g * GROUP:(g + 1) * GROUP], wbd_ref[g * GROUP:(g + 1) * GROUP, :],
                             (((1,), (1,)), ((), ())), preferred_element_type=F32)
             for g in range(LRU_WIDTH // GROUP)]
    return jnp.concatenate(parts, axis=1)


def _dot_nt(a, b):
    return lax.dot_general(a, b, (((1,), (1,)), ((), ())), preferred_element_type=F32)


def _dot_tn(a, b):
    return lax.dot_general(a, b, (((0,), (0,)), ((), ())), preferred_element_type=F32)


CHUNKS_IN_FLIGHT = 8


def _chunk_loop(n_chunks, chunk, init):
    def body(k, carry):
        for j in range(CHUNKS_IN_FLIGHT):
            carry = chunk(k * CHUNKS_IN_FLIGHT + j, carry)
        return carry

    return lax.fori_loop(0, n_chunks // CHUNKS_IN_FLIGHT, body, init)


def _place():
    x, y, c = lax.axis_index("x"), lax.axis_index("y"), lax.axis_index("c")
    return x, y, c


def _block_id(chip, core):
    return 4 * chip[0] + 2 * chip[1] + core


def _other_chips(x, y):
    return [(1 - x, y), (x, 1 - y), (1 - x, 1 - y)]


def _remote_copy(src, dst, send_sem, recv_sem, to):
    return pltpu.make_async_remote_copy(src_ref=src, dst_ref=dst, send_sem=send_sem, recv_sem=recv_sem,
                                        device_id=to, device_id_type=MESH)


HBM_SPEC = pl.BlockSpec(memory_space=pl.ANY)


def _in_hbm(*arrays):
    return [pltpu.with_memory_space_constraint(a, pltpu.HBM) for a in arrays]


def _prep_shards(w_in_t, w_out, w_mlp_in, w_mlp_out, conv_w, rnn_conv_w):
    def body(win_ref, wout_ref, w1_ref, w2_ref, cw_ref, rw_ref, o_win, o_wout, o_w1, o_w2, o_cp):
        o_win[...] = win_ref[...].astype(BF16)
        o_wout[...] = wout_ref[...].astype(BF16)
        o_w1[...] = w1_ref[...].astype(BF16)
        o_w2[...] = w2_ref[...].astype(BF16)
        o_cp[...] = jnp.zeros(o_cp.shape, F32)
        o_cp[0:3, 0:64] = cw_ref[...]
        o_cp[3:7, :] = rw_ref[...]

    whole = lambda shape: pl.BlockSpec(shape, lambda i: (0,) * len(shape))
    args = (w_in_t, w_out, w_mlp_in, w_mlp_out, conv_w, rnn_conv_w)
    shapes = [(w_in_t.shape, BF16), (w_out.shape, BF16), (w_mlp_in.shape, BF16), (w_mlp_out.shape, BF16),
              ((8, 128), F32)]
    return pl.pallas_call(
        body, grid=(1,), out_shape=[jax.ShapeDtypeStruct(s, d) for s, d in shapes],
        in_specs=[whole(a.shape) for a in args], out_specs=[whole(s) for s, _ in shapes],
        compiler_params=_params(("arbitrary",), 40), name="prep_shards",
    )(*args)


def _host_all_gather(step, n_steps, shards, fulls, send_sems, recv_sems, local_sems):
    x, y, c = _place()
    me = (x, y, c)
    my_id = _block_id((x, y), c)
    sibling = (x, y, 1 - c)
    chips = _other_chips(x, y)
    n_arr = len(shards)

    def copy(arr, k, block, to, src=None):
        dst = fulls[arr].at[block]
        return _remote_copy(dst if src is None else src, dst, send_sems.at[arr, k], recv_sems.at[arr, k], to)

    def local(arr):
        return pltpu.make_async_copy(shards[arr], fulls[arr].at[my_id], local_sems.at[arr])

    @pl.when(step == 0)
    def _():
        for arr in range(n_arr):
            local(arr).start()
            copy(arr, 0, my_id, sibling, shards[arr]).start()
            for j, chip in enumerate(chips):
                copy(arr, 1 + j, my_id, (*chip, c), shards[arr]).start()

    @pl.when(step == max(n_steps - 2, 0))
    def _():
        for j, chip in enumerate(chips):
            for arr in range(n_arr):
                copy(arr, 1 + j, _block_id(chip, c), me).wait_recv()
                copy(arr, 4 + j, _block_id(chip, c), sibling).start()

    @pl.when(step == n_steps - 1)
    def _():
        for arr in range(n_arr):
            copy(arr, 0, _block_id((x, y), 1 - c), me).wait_recv()
            for j, chip in enumerate(chips):
                copy(arr, 4 + j, _block_id(chip, 1 - c), me).wait_recv()
            for k in range(4):
                copy(arr, k, my_id, me, shards[arr]).wait_send()
            for j, chip in enumerate(chips):
                copy(arr, 4 + j, _block_id(chip, c), me).wait_send()
            local(arr).wait()


def _host_pair_exchange(step, n_steps, gs, sibs, send_sems, recv_sems):
    x, y, c = _place()
    sibling = (x, y, 1 - c)
    chips = [(x, y)] + _other_chips(x, y)

    def d2d(arr, q):
        return _remote_copy(gs[arr].at[_block_id(chips[q], 1 - c)], sibs[arr].at[q],
                            send_sems.at[arr, q], recv_sems.at[arr, q], sibling)

    @pl.when(step == 0)
    def _():
        for arr in range(len(gs)):
            for q in (1, 2, 3, 0):
                d2d(arr, q).start()

    @pl.when(step == n_steps - 1)
    def _():
        for arr in range(len(gs)):
            for q in range(4):
                d2d(arr, q).wait()


def _host_chip_exchange(step, n_steps, hsends, hrecvs, send_sems, recv_sems):
    x, y, c = _place()
    chips = _other_chips(x, y)

    def ici(arr, j):
        return _remote_copy(hsends[arr].at[j], hrecvs[arr].at[j], send_sems.at[arr, j], recv_sems.at[arr, j],
                            (*chips[j], c))

    @pl.when(step == 0)
    def _():
        for arr in range(len(hsends)):
            for j in range(3):
                ici(arr, j).start()

    @pl.when(step == n_steps - 1)
    def _():
        for arr in range(len(hsends)):
            for j in range(3):
                ici(arr, j).wait()


def _host_half_exchange(step, n_steps, parts, sibs, send_sems, recv_sems):
    x, y, c = _place()
    n_q, rows2, _ = parts.shape
    half = rows2 // 2

    def d2d(q):
        src = parts.at[q, pl.ds(pl.multiple_of((1 - c) * half, 16), half), :]
        return _remote_copy(src, sibs.at[q], send_sems.at[q], recv_sems.at[q], (x, y, 1 - c))

    @pl.when(step == 0)
    def _():
        for q in range(n_q):
            d2d(q).start()

    @pl.when(step == n_steps - 1)
    def _():
        for q in range(n_q):
            d2d(q).wait()


def _peer(x, y, c, k):
    return (x ^ ((k >> 2) & 1), y ^ ((k >> 1) & 1), c ^ (k & 1))


def _host_small_exchange(step, n_steps, vec_m, vec_b, wab, vrecv_m, vrecv_b, wrecv, send_sems, recv_sems, local_sems):
    x, y, c = _place()
    my_id = _block_id((x, y), c)
    wrows = wab.shape[0] // N_DEV

    def copies(k):
        to = _peer(x, y, c, k)
        block = wab.at[pl.ds(pl.multiple_of(_block_id(to[0:2], to[2]) * wrows, SUB), wrows), :]
        return [_remote_copy(vec_m, vrecv_m.at[my_id], send_sems.at[0, k], recv_sems.at[0, k], to),
                _remote_copy(vec_b, vrecv_b.at[my_id], send_sems.at[1, k], recv_sems.at[1, k], to),
                _remote_copy(block, wrecv.at[k], send_sems.at[2, k], recv_sems.at[2, k], to)]

    mine = [pltpu.make_async_copy(vec_m, vrecv_m.at[my_id], local_sems.at[0]),
            pltpu.make_async_copy(vec_b, vrecv_b.at[my_id], local_sems.at[1])]

    @pl.when(step == 0)
    def _():
        for cp in mine:
            cp.start()
        for k in range(1, N_DEV):
            for cp in copies(k):
                cp.start()

    @pl.when(step == n_steps - 1)
    def _():
        for k in range(1, N_DEV):
            for cp in copies(k):
                cp.wait()
        for cp in mine:
            cp.wait()


def _pair_sum_parts(parts, sibs, core):
    n_q, rows2, cols = parts.shape
    half = rows2 // 2

    def body(core_ref, g_ref, s_ref, o_ref):
        o_ref[0] = (g_ref[0, 0].astype(F32) + s_ref[0].astype(F32)).astype(BF16)

    block = (1, half, cols)
    grid_spec = pltpu.PrefetchScalarGridSpec(
        num_scalar_prefetch=1, grid=(n_q,),
        in_specs=[pl.BlockSpec((1, 1, half, cols), lambda q, cr: (q, cr[0], 0, 0)),
                  pl.BlockSpec(block, lambda q, cr: (q, 0, 0))],
        out_specs=pl.BlockSpec(block, lambda q, cr: (q, 0, 0)))
    return pl.pallas_call(
        body, grid_spec=grid_spec, out_shape=pltpu.HBM((n_q, half, cols), BF16),
        compiler_params=_params(("arbitrary",), 32), name="pair_sum_w_in",
    )(core, *_in_hbm(parts.reshape(n_q, 2, half, cols), sibs))


def _pair_sum(gs, sibs, name):
    n_arr = len(gs)
    x, y, c = _place()
    slots = jnp.stack([_block_id(chip, c) for chip in [(x, y)] + _other_chips(x, y)]).astype(jnp.int32)

    def body(slots_ref, *refs):
        q = pl.program_id(0)
        for k in range(n_arr):
            g_ref, sib_ref = refs[2 * k:2 * k + 2]
            hs_ref, own_ref = refs[2 * n_arr + 2 * k:2 * n_arr + 2 * k + 2]
            both = g_ref[0].astype(F32) + sib_ref[0].astype(F32)

            @pl.when(q == 0)
            def _(own_ref=own_ref, both=both):
                own_ref[...] = both

            @pl.when(q > 0)
            def _(hs_ref=hs_ref, both=both):
                hs_ref[0] = both.astype(BF16)

    in_specs, out_specs, out_shape, args = [], [], [], []
    for g, sib in zip(gs, sibs):
        _, rows, cols = g.shape
        block = (1, rows, cols)
        in_specs += [pl.BlockSpec(block, lambda q, s: (s[q], 0, 0)), pl.BlockSpec(block, lambda q, s: (q, 0, 0))]
        out_specs += [pl.BlockSpec(block, lambda q, s: (jnp.maximum(q - 1, 0), 0, 0)),
                      pl.BlockSpec((rows, cols), lambda q, s: (0, 0))]
        out_shape += [pltpu.HBM((3, rows, cols), BF16), pltpu.HBM((rows, cols), F32)]
        args += _in_hbm(g, sib)
    grid_spec = pltpu.PrefetchScalarGridSpec(num_scalar_prefetch=1, grid=(4,), in_specs=in_specs, out_specs=out_specs)
    return pl.pallas_call(
        body, grid_spec=grid_spec, out_shape=out_shape,
        compiler_params=_params(("arbitrary",), 40), name=name,
    )(slots, *args)


def _exchange_scratch(n_arr, n_copies):
    return [pltpu.SemaphoreType.DMA((n_arr, n_copies)), pltpu.SemaphoreType.DMA((n_arr, n_copies))]


def _final_small(vrecv_m, vrecv_b, wab, wrecv, vec_x):
    wrows = wab.shape[0] // N_DEV

    def body(vm_ref, vb_ref, w_ref, wr_ref, vx_ref, o_vec, o_w, xrecv, wred, x_send, x_recv, b_send, b_recv):
        x, y, c = _place()
        my_id = _block_id((x, y), c)
        my_rows = pl.ds(pl.multiple_of(my_id * wrows, SUB), wrows)

        def xcopy(k):
            return _remote_copy(vx_ref, xrecv.at[my_id], x_send.at[k], x_recv.at[k], _peer(x, y, c, k))

        def bcopy(k):
            return _remote_copy(wred, o_w.at[my_rows, :], b_send.at[k], b_recv.at[k], _peer(x, y, c, k))

        xrecv[my_id] = vx_ref[...]
        for k in range(1, N_DEV):
            xcopy(k).start()
        red = w_ref[my_rows, :]
        for k in range(1, N_DEV):
            red = red + wr_ref[k]
        wred[...] = red
        o_w[my_rows, :] = red
        for k in range(1, N_DEV):
            bcopy(k).start()
        for k in range(1, N_DEV):
            xcopy(k).wait_recv()
        for rows, ref in ((slice(0, 8), vm_ref), (slice(8, 24), vb_ref), (slice(24, 32), xrecv)):
            tot = ref[0]
            for s in range(1, N_DEV):
                tot = tot + ref[s]
            o_vec[rows, :] = tot
        for k in range(1, N_DEV):
            bcopy(k).wait_recv()
        for k in range(1, N_DEV):
            xcopy(k).wait_send()
            bcopy(k).wait_send()

    vm = pl.BlockSpec(memory_space=pltpu.VMEM)
    dma8 = pltpu.SemaphoreType.DMA((N_DEV,))
    return pl.pallas_call(
        body, out_shape=(jax.ShapeDtypeStruct((VEC_ROWS, D_MODEL), F32), jax.ShapeDtypeStruct(wab.shape, F32)),
        in_specs=[vm] * 5, out_specs=[vm] * 2,
        scratch_shapes=[pltpu.VMEM((N_DEV, SUB, D_MODEL), F32), pltpu.VMEM((wrows, HEAD_DIM), F32),
                        dma8, dma8, dma8, dma8],
        compiler_params=_params(vmem_mib=32), name="final_small",
    )(vrecv_m, vrecv_b, wab, wrecv, vec_x)


def _in_proj(x, g_mix, shards, tm):
    t_len = x.shape[0]
    n_t = t_len // tm
    n_arr = len(shards)
    rows = [s.shape[0] for s in shards]
    width = 2 * rows[0]
    ax, ay = lax.axis_index("x"), lax.axis_index("y")
    order = jnp.stack([2 * cx + cy for cx, cy in [(ax, ay)] + _other_chips(ax, ay)]).astype(jnp.int32)

    def body(order_ref, x_ref, g_ref, *rest):
        shard_refs = rest[0:n_arr]
        u_ref, h_ref = rest[n_arr:n_arr + 2]
        fulls = rest[n_arr + 2:2 * n_arr + 2]
        h_s, wbuf, send_sems, recv_sems, local_sems, load_sem = rest[2 * n_arr + 2:]
        p = pl.program_id(0)
        i = pl.program_id(1)
        x_, y_, c = _place()
        me = (x_, y_, c)
        my_id = _block_id((x_, y_), c)
        sibling = (x_, y_, 1 - c)
        chips = _other_chips(x_, y_)

        def block(arr, blk):
            return fulls[arr].at[pl.ds(pl.multiple_of(blk * rows[arr], rows[arr]), rows[arr]), :]

        def copy(arr, k, blk, to, src=None):
            dst = block(arr, blk)
            return _remote_copy(dst if src is None else src, dst, send_sems.at[arr, k], recv_sems.at[arr, k], to)

        def local(arr):
            return pltpu.make_async_copy(shard_refs[arr], block(arr, my_id), local_sems.at[arr])

        def load_chip(chip, slot):
            start = pl.multiple_of((2 * chip[0] + chip[1]) * width, width)
            return pltpu.make_async_copy(fulls[0].at[pl.ds(start, width), :], wbuf.at[slot], load_sem.at[slot])

        def pass_on(j):
            for arr in range(n_arr):
                copy(arr, 1 + j, _block_id(chips[j], c), me).wait_recv()
                copy(arr, 4 + j, _block_id(chips[j], c), sibling).start()

        def complete(j):
            for arr in range(n_arr):
                copy(arr, 4 + j, _block_id(chips[j], 1 - c), me).wait_recv()

        @pl.when((p == 0) & (i == 0))
        def _():
            for arr in range(n_arr):
                local(arr).start()
                copy(arr, 0, my_id, sibling, shard_refs[arr]).start()
                for j in (0, 1):
                    copy(arr, 1 + j, my_id, (*chips[j], c), shard_refs[arr]).start()
            for arr in range(n_arr):
                local(arr).wait()
                copy(arr, 0, _block_id((x_, y_), 1 - c), me).wait_recv()
            load_chip((x_, y_), 0).start()
            load_chip((x_, y_), 0).wait()

        @pl.when((p == 1) & (i == 0))
        def _():
            pass_on(0)
            for arr in range(n_arr):
                copy(arr, 3, my_id, (*chips[2], c), shard_refs[arr]).start()
            pass_on(1)
            complete(0)
            load_chip(chips[0], 1).start()
            load_chip(chips[0], 1).wait()
            complete(1)
            load_chip(chips[1], 0).start()

        @pl.when((p == 2) & (i == 0))
        def _():
            load_chip(chips[1], 0).wait()

        @pl.when((p == 3) & (i == 0))
        def _():
            pass_on(2)
            complete(2)
            load_chip(chips[2], 1).start()
            load_chip(chips[2], 1).wait()

        @pl.when((p == 3) & (i == n_t - 1))
        def _():
            for arr in range(n_arr):
                for k in range(4):
                    copy(arr, k, my_id, me, shard_refs[arr]).wait_send()
                for j, chip in enumerate(chips):
                    copy(arr, 4 + j, _block_id(chip, c), me).wait_send()

        tile = pl.ds(pl.multiple_of(i * tm, tm), tm)

        @pl.when(p == 0)
        def _():
            xv = x_ref[...]
            h = (xv * _rms(xv) * g_ref[...]).astype(BF16)
            h_ref[...] = h
            h_s[tile, :] = h

        for slot in (0, 1):
            @pl.when(p % 2 == slot)
            def _(slot=slot):
                u_ref[...] = _dot_nt(h_s[tile, :], wbuf[slot])

    first_pass = lambda p, i, o: (jnp.where(p == 0, i, n_t - 1), 0)
    grid_spec = pltpu.PrefetchScalarGridSpec(
        num_scalar_prefetch=1, grid=(4, n_t),
        in_specs=[pl.BlockSpec((tm, D_MODEL), first_pass), pl.BlockSpec((1, D_MODEL), lambda p, i, o: (0, 0))]
        + [HBM_SPEC] * n_arr,
        out_specs=[pl.BlockSpec((tm, width), lambda p, i, o: (i, o[p])), pl.BlockSpec((tm, D_MODEL), first_pass)]
        + [HBM_SPEC] * n_arr,
        scratch_shapes=[pltpu.VMEM((t_len, D_MODEL), BF16), pltpu.VMEM((2, width, D_MODEL), BF16)]
        + _exchange_scratch(n_arr, 7) + [pltpu.SemaphoreType.DMA((n_arr,)), pltpu.SemaphoreType.DMA((2,))])
    return pl.pallas_call(
        body, grid_spec=grid_spec,
        out_shape=[jax.ShapeDtypeStruct((t_len, IN_COLS), F32), jax.ShapeDtypeStruct((t_len, D_MODEL), BF16)]
        + [jax.ShapeDtypeStruct((N_DEV * s.shape[0], s.shape[1]), s.dtype) for s in shards],
        compiler_params=_params(("arbitrary", "arbitrary"), 48), name="in_proj",
    )(order, x, g_mix, *shards)


def _conv3_chunk(u_ref, r, cv_prev, cw, row):
    gb = u_ref[pl.ds(r, SUB), OFF_GB:OFF_GB + CONV_WIDTH]
    gc = u_ref[pl.ds(r, SUB), OFF_GC:OFF_GC + CONV_WIDTH]
    v = u_ref[pl.ds(r, SUB), OFF_V:OFF_V + CONV_WIDTH]
    cv = gc * v
    cv_m1 = _down(cv, cv_prev, 1, row)
    cv_m2 = _down(cv, cv_prev, 2, row)
    cq = cw[2:3, :] * cv + cw[1:2, :] * cv_m1 + cw[0:1, :] * cv_m2
    return gb, gc, v, cv, cv_m1, cv_m2, cq


def _conv4_chunk(u_ref, r, xin_prev, rw, rb, row):
    xin = u_ref[pl.ds(r, SUB), OFF_XR:OFF_XR + LRU_WIDTH]
    m1 = _down(xin, xin_prev, 1, row)
    m2 = _down(xin, xin_prev, 2, row)
    m3 = _down(xin, xin_prev, 3, row)
    xr = rw[3:4, :] * xin + rw[2:3, :] * m1 + rw[1:2, :] * m2 + rw[0:1, :] * m3 + rb
    return xin, m1, m2, m3, xr


def _mixer_fwd(u, conv_w, rnn_conv_w, rnn_conv_b, wa, b_a, wx, b_x, lam, gnc, gnr, shards, tm):
    t_len = u.shape[0]
    n_steps = t_len // tm
    n_chunks = tm // SUB
    n_arr = len(shards)

    def body(u_ref, cw_ref, rw_ref, rb_ref, wa_ref, ba_ref, wx_ref, bx_ref, lam_ref, gnc_ref, gnr_ref, *rest):
        shard_refs = rest[0:n_arr]
        hs_ref, y_ref, xr_s, ra_ref, ii_ref, mult_ref, cq_ref = rest[n_arr:n_arr + 7]
        fulls = rest[n_arr + 7:2 * n_arr + 7]
        (pa_s, px_s, wabd, wxbd, cv_car, xin_car, h_car,
         send_sems, recv_sems, local_sems) = rest[2 * n_arr + 7:]
        _host_all_gather(pl.program_id(0), n_steps, shard_refs, fulls, send_sems, recv_sems, local_sems)

        @pl.when(pl.program_id(0) == 0)
        def _():
            cv_car[...] = jnp.zeros(cv_car.shape, F32)
            xin_car[...] = jnp.zeros(xin_car.shape, F32)
            h_car[...] = jnp.zeros(h_car.shape, F32)
            wabd[...] = _expand_heads(wa_ref[...])
            wxbd[...] = _expand_heads(wx_ref[...])

        row_c = lax.broadcasted_iota(jnp.int32, (SUB, CONV_WIDTH), 0)
        row_r = lax.broadcasted_iota(jnp.int32, (SUB, LRU_WIDTH), 0)
        cw = cw_ref[...]
        rw = rw_ref[...]
        rb = rb_ref[...]
        g_c = gnc_ref[...]
        g_r = gnr_ref[...]
        sp_c = LRU_C * _softplus_neg(lam_ref[...])

        def convs(i, carry):
            cv_prev, xin_prev = carry
            r = pl.multiple_of(i * SUB, SUB)
            gb, _, _, cv, _, _, cq = _conv3_chunk(u_ref, r, cv_prev, cw, row_c)
            cq_ref[pl.ds(r, SUB), :] = cq
            y_c = gb * cq
            y_ref[pl.ds(r, SUB), 0:CONV_WIDTH] = (y_c * _rms(y_c) * g_c).astype(BF16)
            xin, _, _, _, xr = _conv4_chunk(u_ref, r, xin_prev, rw, rb, row_r)
            xr_s[pl.ds(r, SUB), :] = xr
            return cv, xin

        cv_last, xin_last = _chunk_loop(n_chunks, convs, (cv_car[...], xin_car[...]))
        cv_car[...] = cv_last
        xin_car[...] = xin_last

        for g in range(LRU_WIDTH // GROUP):
            cols = slice(g * GROUP, (g + 1) * GROUP)
            xrb = xr_s[:, cols].astype(BF16)
            pa_s[:, cols] = jnp.dot(xrb, wabd[cols, :], preferred_element_type=F32) + ba_ref[:, cols]
            px_s[:, cols] = jnp.dot(xrb, wxbd[cols, :], preferred_element_type=F32) + bx_ref[:, cols]

        def recur(i, h_prev):
            r = pl.multiple_of(i * SUB, SUB)
            xr = xr_s[pl.ds(r, SUB), :]
            ra, ii, a, mult = _lru_gates(pa_s[pl.ds(r, SUB), :], px_s[pl.ds(r, SUB), :], sp_c)
            ra_ref[pl.ds(r, SUB), :] = ra
            ii_ref[pl.ds(r, SUB), :] = ii
            mult_ref[pl.ds(r, SUB), :] = mult
            a_cum, b_cum = _scan8_fwd(a, mult * ii * xr, row_r)
            h = a_cum * h_prev + b_cum
            hs_ref[pl.ds(r, SUB), :] = h
            ge, _ = _gelu(u_ref[pl.ds(r, SUB), OFF_G:OFF_G + LRU_WIDTH])
            y_r = h * ge
            y_ref[pl.ds(r, SUB), CONV_WIDTH:MIX_WIDTH] = (y_r * _rms(y_r) * g_r).astype(BF16)
            return h[SUB - 1:SUB, :]

        h_car[...] = _chunk_loop(n_chunks, recur, h_car[...])

    row_tile = lambda w: pl.BlockSpec((tm, w), lambda i: (i, 0))
    whole = lambda a: pl.BlockSpec(a.shape, lambda i: (0,) * a.ndim)
    smalls = (conv_w, rnn_conv_w, rnn_conv_b, wa, b_a, wx, b_x, lam, gnc, gnr)
    return pl.pallas_call(
        body, grid=(n_steps,),
        in_specs=[row_tile(IN_COLS)] + [whole(a) for a in smalls] + [HBM_SPEC] * n_arr,
        out_specs=[row_tile(LRU_WIDTH), row_tile(MIX_WIDTH)] + [row_tile(LRU_WIDTH)] * 4 + [row_tile(CONV_WIDTH)]
        + [HBM_SPEC] * n_arr,
        out_shape=[jax.ShapeDtypeStruct((t_len, LRU_WIDTH), F32), jax.ShapeDtypeStruct((t_len, MIX_WIDTH), BF16)]
        + [jax.ShapeDtypeStruct((t_len, LRU_WIDTH), F32)] * 4 + [jax.ShapeDtypeStruct((t_len, CONV_WIDTH), F32)]
        + [jax.ShapeDtypeStruct((N_DEV,) + s.shape, BF16) for s in shards],
        scratch_shapes=[pltpu.VMEM((tm, LRU_WIDTH), F32), pltpu.VMEM((tm, LRU_WIDTH), F32),
                        pltpu.VMEM((LRU_WIDTH, GROUP), BF16), pltpu.VMEM((LRU_WIDTH, GROUP), BF16),
                        pltpu.VMEM((SUB, CONV_WIDTH), F32), pltpu.VMEM((SUB, LRU_WIDTH), F32),
                        pltpu.VMEM((1, LRU_WIDTH), F32)]
        + _exchange_scratch(n_arr, 7) + [pltpu.SemaphoreType.DMA((n_arr,))],
        compiler_params=_params(("arbitrary",), 56), name="mixer_fwd",
    )(u, *smalls, *shards)


def _mlp_up(x, y, g_mlp, w_out, w1, w2_shard, tm):
    t_len = x.shape[0]
    n_steps = t_len // tm
    n_blk, _, blk = w1.shape

    def body(x_ref, y_ref, gm_ref, wout_hbm, w1_hbm, w2_ref, x1_ref, h2_ref, z_ref, w2_full,
             wout_s, w1_s, sem, send_sems, recv_sems, local_sems):
        step = pl.program_id(0)
        _host_all_gather(step, n_steps, [w2_ref], [w2_full], send_sems, recv_sems, local_sems)

        load_wout = pltpu.make_async_copy(wout_hbm, wout_s, sem.at[0])
        load_w1 = pltpu.make_async_copy(w1_hbm, w1_s, sem.at[1])

        @pl.when(step == 0)
        def _():
            load_wout.start()
            load_w1.start()
            load_wout.wait()

        x1v = x_ref[...] + jnp.dot(y_ref[...], wout_s[...], preferred_element_type=F32)
        x1_ref[...] = x1v
        h2 = (x1v * _rms(x1v) * gm_ref[...]).astype(BF16)
        h2_ref[...] = h2

        @pl.when(step == 0)
        def _():
            load_w1.wait()

        for k in range(n_blk):
            rp = jnp.maximum(jnp.dot(h2, w1_s[k], preferred_element_type=F32), 0.0)
            z_ref[:, k * blk:(k + 1) * blk] = (rp * rp).astype(BF16)

    row_tile = lambda w: pl.BlockSpec((tm, w), lambda i: (i, 0))
    return pl.pallas_call(
        body, grid=(n_steps,),
        in_specs=[row_tile(D_MODEL), row_tile(MIX_WIDTH), pl.BlockSpec((1, D_MODEL), lambda i: (0, 0)),
                  HBM_SPEC, HBM_SPEC, HBM_SPEC],
        out_specs=[row_tile(D_MODEL), row_tile(D_MODEL), row_tile(D_FF), HBM_SPEC],
        out_shape=[jax.ShapeDtypeStruct((t_len, D_MODEL), F32), jax.ShapeDtypeStruct((t_len, D_MODEL), BF16),
                   jax.ShapeDtypeStruct((t_len, D_FF), BF16), jax.ShapeDtypeStruct((N_DEV,) + w2_shard.shape, BF16)],
        scratch_shapes=[pltpu.VMEM(w_out.shape, BF16), pltpu.VMEM(w1.shape, BF16), pltpu.SemaphoreType.DMA((2,))]
        + _exchange_scratch(1, 7) + [pltpu.SemaphoreType.DMA((1,))],
        compiler_params=_params(("arbitrary",), 48), name="mlp_up",
    )(x, y, g_mlp, w_out, w1, w2_shard)


def _mlp_down_bwd(x1, z, target, g_mlp, g_f, w1, w2, tm):
    t_len = x1.shape[0]
    n_steps = t_len // tm
    n_blk, _, blk = w1.shape

    def body(x1_ref, z_ref, tg_ref, gm_ref, gf_ref, w1_hbm, w2_hbm, dx1_ref, dx2_ref, vec_ref, dpre_hbm,
             w1_s, w2_s, dp_s, sem, out_sem):
        step = pl.program_id(0)
        rows = pl.ds(pl.multiple_of(step * tm, tm), tm)
        dp_out = pltpu.make_async_copy(dp_s, dpre_hbm.at[rows, :], out_sem.at[0])

        load_w1 = pltpu.make_async_copy(w1_hbm, w1_s, sem.at[0])
        load_w2 = pltpu.make_async_copy(w2_hbm, w2_s, sem.at[1])

        @pl.when(step == 0)
        def _():
            load_w2.start()
            load_w1.start()
            vec_ref[...] = jnp.zeros(vec_ref.shape, F32)
            load_w2.wait()

        x1v = x1_ref[...]
        g_m = gm_ref[...]
        g_o = gf_ref[...]
        r2 = _rms(x1v)
        x1h = x1v * r2
        x2 = x1v + jnp.dot(z_ref[...], w2_s[...], preferred_element_type=F32)
        r3 = _rms(x2)
        x2h = x2 * r3
        err = x2h * g_o - tg_ref[...]
        dout = err * (1.0 / D_MODEL)
        vec_ref[ROW_LOSS:ROW_LOSS + 1, :] += (0.5 / D_MODEL) * jnp.sum(err * err, axis=0, keepdims=True)
        vec_ref[ROW_GF:ROW_GF + 1, :] += jnp.sum(dout * x2h, axis=0, keepdims=True)
        dx2 = _rms_bwd(dout, x2h, r3, g_o)
        dx2b = dx2.astype(BF16)
        dx2_ref[...] = dx2b
        dh2 = jnp.zeros((tm, D_MODEL), F32)

        @pl.when(step > 0)
        def _():
            dp_out.wait()

        @pl.when(step == 0)
        def _():
            load_w1.wait()

        for k in range(n_blk):
            cols = slice(k * blk, (k + 1) * blk)
            dz = _dot_nt(dx2b, w2_s[cols, :])
            dpb = (dz * 2.0 * jnp.sqrt(z_ref[:, cols].astype(F32))).astype(BF16)
            dp_s[:, cols] = dpb
            dh2 = dh2 + _dot_nt(dpb, w1_s[k])
        dp_out.start()
        vec_ref[ROW_GMLP:ROW_GMLP + 1, :] += jnp.sum(dh2 * x1h, axis=0, keepdims=True)
        dx1_ref[...] = dx2 + _rms_bwd(dh2, x1h, r2, g_m)

        @pl.when(step == n_steps - 1)
        def _():
            dp_out.wait()

    row_tile = lambda w: pl.BlockSpec((tm, w), lambda i: (i, 0))
    vec_spec = pl.BlockSpec((1, D_MODEL), lambda i: (0, 0))
    return pl.pallas_call(
        body, grid=(n_steps,),
        in_specs=[row_tile(D_MODEL), row_tile(D_FF), row_tile(D_MODEL), vec_spec, vec_spec, HBM_SPEC, HBM_SPEC],
        out_specs=[row_tile(D_MODEL), row_tile(D_MODEL), pl.BlockSpec((SUB, D_MODEL), lambda i: (0, 0)), HBM_SPEC],
        out_shape=[jax.ShapeDtypeStruct((t_len, D_MODEL), F32), jax.ShapeDtypeStruct((t_len, D_MODEL), BF16),
                   jax.ShapeDtypeStruct((SUB, D_MODEL), F32), jax.ShapeDtypeStruct((t_len, D_FF), BF16)],
        scratch_shapes=[pltpu.VMEM(w1.shape, BF16), pltpu.VMEM(w2.shape, BF16), pltpu.VMEM((tm, D_FF), BF16),
                        pltpu.SemaphoreType.DMA((2,)), pltpu.SemaphoreType.DMA((1,))],
        compiler_params=_params(("arbitrary",), 56), name="mlp_down_bwd",
    )(x1, z, target, g_mlp, g_f, w1, w2)


def _mixer_bwd(u, hs, dx1, saved, conv_w, rnn_conv_w, wa, wx, lam, gnc, gnr, w_out, chip_sums, g_wout, tm):
    t_len = u.shape[0]
    n_tiles = t_len // tm
    n_chunks = tm // SUB
    per_tile = tm // SUB
    n_sums = len(chip_sums)

    def body(u_ref, hs_ref, hp_ref, dx1_ref, xr_ref, ra_ref, ii_ref, mult_ref, cq_ref,
             cw_ref, rw_ref, wa_ref, wx_ref, lam_ref, gnc_ref, gnr_ref, wout_ref, *rest):
        hsends = rest[0:n_sums]
        gwout_ref = rest[n_sums]
        du_ref, vec_ref, wab_ref = rest[n_sums + 1:n_sums + 4]
        hrecvs = rest[n_sums + 4:2 * n_sums + 4]
        sib_wout = rest[2 * n_sums + 4]
        (du_s, dy_s, dpa_s, dpx_s, dxr_s, wabd, wxbd, acc, dwa_acc, dwx_acc,
         a_car, dh_car, dcq_car, dxr_car, i_send, i_recv, d_send, d_recv) = rest[2 * n_sums + 5:]
        step = pl.program_id(0)
        _host_chip_exchange(step, n_tiles, hsends, hrecvs, i_send, i_recv)
        _host_pair_exchange(step, n_tiles, [gwout_ref], [sib_wout], d_send, d_recv)
        has_prev = (step < n_tiles - 1).astype(F32)

        @pl.when(step == 0)
        def _():
            acc[...] = jnp.zeros(acc.shape, F32)
            dwa_acc[...] = jnp.zeros(dwa_acc.shape, F32)
            dwx_acc[...] = jnp.zeros(dwx_acc.shape, F32)
            a_car[...] = jnp.ones(a_car.shape, F32)
            dh_car[...] = jnp.zeros(dh_car.shape, F32)
            dcq_car[...] = jnp.zeros(dcq_car.shape, F32)
            dxr_car[...] = jnp.zeros(dxr_car.shape, F32)
            wabd[...] = _expand_heads(wa_ref[...])
            wxbd[...] = _expand_heads(wx_ref[...])

        row_c = lax.broadcasted_iota(jnp.int32, (SUB, CONV_WIDTH), 0)
        row_r = lax.broadcasted_iota(jnp.int32, (SUB, LRU_WIDTH), 0)
        cw = cw_ref[...]
        rw = rw_ref[...]
        g_c = gnc_ref[...]
        g_r = gnr_ref[...]
        sp_c = LRU_C * _softplus_neg(lam_ref[...])

        hs_before = hp_ref[...] * has_prev

        dy_s[...] = _dot_nt(dx1_ref[...].astype(BF16), wout_ref[...])

        def recur_bwd(j, carry):
            a_later, dh_later = carry
            i = n_chunks - 1 - j
            r = pl.multiple_of(i * SUB, SUB)
            rp = pl.multiple_of(jnp.maximum(i - 1, 0) * SUB, SUB)
            xr = xr_ref[pl.ds(r, SUB), :]
            hs_c = hs_ref[pl.ds(r, SUB), :]
            hs_prev = jnp.where(i == 0, hs_before, hs_ref[pl.ds(rp, SUB), :])
            h_m1 = _down(hs_c, hs_prev, 1, row_r)
            ra = ra_ref[pl.ds(r, SUB), :]
            ii = ii_ref[pl.ds(r, SUB), :]
            mult = mult_ref[pl.ds(r, SUB), :]
            a = jnp.exp(-ra * sp_c)
            inv_mult = lax.rsqrt(mult * mult)
            ge, dge = _gelu(u_ref[pl.ds(r, SUB), OFF_G:OFF_G + LRU_WIDTH])
            y_r = hs_c * ge
            rr = _rms(y_r)
            yhat = y_r * rr
            dyn = dy_s[pl.ds(r, SUB), CONV_WIDTH:MIX_WIDTH]
            acc[ACC_GNR] += dyn * yhat
            dy_r = _rms_bwd(dyn, yhat, rr, g_r)
            du_s[pl.ds(r, SUB), OFF_G:OFF_G + LRU_WIDTH] = dy_r * hs_c * dge
            a_cum, d_cum = _scan8_rev(_up(a, a_later, 1, row_r), dy_r * ge, row_r)
            dh = a_cum * dh_later + d_cum
            dm = dh * mult
            dii = dm * xr
            dxr_s[pl.ds(r, SUB), :] = dm * ii
            dla = a * dh * (h_m1 - (ii * xr) * a * inv_mult)
            dla_r = dla * ra
            acc[ACC_SP] -= dla_r
            dpa = dla_r * (sp_c * (ra - 1.0))
            dpx = dii * ii * (1.0 - ii)
            acc[ACC_BA] += dpa
            acc[ACC_BX] += dpx
            dpa_s[pl.ds(r, SUB), :] = dpa
            dpx_s[pl.ds(r, SUB), :] = dpx
            return a, dh[0:1, :]

        a_first, dh_first = _chunk_loop(n_chunks, recur_bwd, (a_car[...], dh_car[...]))
        a_car[...] = a_first
        dh_car[...] = dh_first

        for g in range(LRU_WIDTH // GROUP):
            cols = slice(g * GROUP, (g + 1) * GROUP)
            dpab = dpa_s[:, cols].astype(BF16)
            dpxb = dpx_s[:, cols].astype(BF16)
            xrb = xr_ref[:, cols].astype(BF16)
            dxr_s[:, cols] += _dot_nt(dpab, wabd[cols, :]) + _dot_nt(dpxb, wxbd[cols, :])
            dwa_acc[cols, :] += _dot_tn(xrb, dpab)
            dwx_acc[cols, :] += _dot_tn(xrb, dpxb)

        def convs_bwd(j, carry):
            dcq_later, dxr_later = carry
            i = n_chunks - 1 - j
            r = pl.multiple_of(i * SUB, SUB)
            gb = u_ref[pl.ds(r, SUB), OFF_GB:OFF_GB + CONV_WIDTH]
            gc = u_ref[pl.ds(r, SUB), OFF_GC:OFF_GC + CONV_WIDTH]
            v = u_ref[pl.ds(r, SUB), OFF_V:OFF_V + CONV_WIDTH]
            cv = gc * v
            cq = cq_ref[pl.ds(r, SUB), :]
            y_c = gb * cq
            rc = _rms(y_c)
            yhat = y_c * rc
            dyn = dy_s[pl.ds(r, SUB), 0:CONV_WIDTH]
            acc[ACC_GNC, :, 0:CONV_WIDTH] += dyn * yhat
            dy_c = _rms_bwd(dyn, yhat, rc, g_c)
            dcq = dy_c * gb
            ahead3 = [dcq, _up(dcq, dcq_later, 1, row_c), _up(dcq, dcq_later, 2, row_c)]
            dcv = cw[2:3, :] * ahead3[0] + cw[1:2, :] * ahead3[1] + cw[0:1, :] * ahead3[2]
            for k in range(3):
                acc[ACC_CW + 2 - k, :, 0:CONV_WIDTH] += ahead3[k] * cv
            du_s[pl.ds(r, SUB), OFF_GB:OFF_GB + CONV_WIDTH] = dy_c * cq
            du_s[pl.ds(r, SUB), OFF_GC:OFF_GC + CONV_WIDTH] = dcv * v
            du_s[pl.ds(r, SUB), OFF_V:OFF_V + CONV_WIDTH] = dcv * gc

            xin = u_ref[pl.ds(r, SUB), OFF_XR:OFF_XR + LRU_WIDTH]
            dxr = dxr_s[pl.ds(r, SUB), :]
            ahead = [dxr] + [_up(dxr, dxr_later, k, row_r) for k in (1, 2, 3)]
            du_s[pl.ds(r, SUB), OFF_XR:OFF_XR + LRU_WIDTH] = (
                rw[3:4, :] * ahead[0] + rw[2:3, :] * ahead[1] + rw[1:2, :] * ahead[2] + rw[0:1, :] * ahead[3])
            for k in range(4):
                acc[ACC_RW + 3 - k] += ahead[k] * xin
            acc[ACC_BR] += dxr
            return dcq, dxr

        dcq_first, dxr_first = _chunk_loop(n_chunks, convs_bwd, (dcq_car[...], dxr_car[...]))
        dcq_car[...] = dcq_first
        dxr_car[...] = dxr_first

        du_ref[...] = du_s[...].astype(BF16)

        @pl.when(step == n_tiles - 1)
        def _():
            vec_ref[...] = jnp.zeros(vec_ref.shape, F32)
            rows = {ACC_GNC: ROW_GNC, ACC_GNR: ROW_GNR, ACC_BR: ROW_BR, ACC_BA: ROW_BA, ACC_BX: ROW_BX}
            for k in range(3):
                rows[ACC_CW + k] = ROW_CW + k
            for k in range(4):
                rows[ACC_RW + k] = ROW_RW + k
            for slot, out_row in rows.items():
                o = out_row - ROW_GNC
                vec_ref[o:o + 1, :] = jnp.sum(acc[slot], axis=0, keepdims=True)
            lam_v = lam_ref[...]
            dsp = jnp.sum(acc[ACC_SP], axis=0, keepdims=True)
            o = ROW_LAM - ROW_GNC
            vec_ref[o:o + 1, :] = -dsp * LRU_C / (1.0 + jnp.exp(lam_v))
            wab_ref[0:LRU_WIDTH, :] = _fold_heads(dwa_acc[...])
            wab_ref[LRU_WIDTH:2 * LRU_WIDTH, :] = _fold_heads(dwx_acc[...])

    rev = lambda w: pl.BlockSpec((tm, w), lambda s: (n_tiles - 1 - s, 0))
    before = lambda w: pl.BlockSpec((SUB, w), lambda s: (jnp.maximum((n_tiles - 1 - s) * per_tile - 1, 0), 0))
    whole = lambda a: pl.BlockSpec(a.shape, lambda s: (0,) * a.ndim)
    smalls = (conv_w, rnn_conv_w, wa, wx, lam, gnc, gnr, w_out)
    full = lambda w: pltpu.VMEM((tm, w), F32)
    return pl.pallas_call(
        body, grid=(n_tiles,),
        in_specs=[rev(IN_COLS), rev(LRU_WIDTH), before(LRU_WIDTH), rev(D_MODEL)]
        + [rev(a.shape[1]) for a in saved] + [whole(a) for a in smalls] + [HBM_SPEC] * (n_sums + 1),
        out_specs=[rev(IN_COLS), pl.BlockSpec((16, D_MODEL), lambda s: (0, 0)),
                   pl.BlockSpec((2 * LRU_WIDTH, HEAD_DIM), lambda s: (0, 0))] + [HBM_SPEC] * (n_sums + 1),
        out_shape=[jax.ShapeDtypeStruct((t_len, IN_COLS), BF16), jax.ShapeDtypeStruct((16, D_MODEL), F32),
                   jax.ShapeDtypeStruct((2 * LRU_WIDTH, HEAD_DIM), F32)]
        + [jax.ShapeDtypeStruct(s.shape, BF16) for s in chip_sums]
        + [jax.ShapeDtypeStruct((4,) + g_wout.shape[1:], BF16)],
        scratch_shapes=[full(IN_COLS), full(MIX_WIDTH), full(LRU_WIDTH), full(LRU_WIDTH), full(LRU_WIDTH),
                        pltpu.VMEM((LRU_WIDTH, GROUP), BF16), pltpu.VMEM((LRU_WIDTH, GROUP), BF16),
                        pltpu.VMEM((N_ACC, SUB, LRU_WIDTH), F32),
                        pltpu.VMEM((LRU_WIDTH, GROUP), F32), pltpu.VMEM((LRU_WIDTH, GROUP), F32),
                        pltpu.VMEM((SUB, LRU_WIDTH), F32), pltpu.VMEM((1, LRU_WIDTH), F32),
                        pltpu.VMEM((SUB, CONV_WIDTH), F32), pltpu.VMEM((SUB, LRU_WIDTH), F32)]
        + _exchange_scratch(n_sums, 3) + _exchange_scratch(1, 4),
        compiler_params=_params(("arbitrary",), 56), name="mixer_bwd",
    )(u, hs, hs, dx1, *saved, *smalls, *chip_sums, g_wout)


def _in_proj_bwd(du, dx1, x, g_mix, win_t, tm, chip_sums, g_own):
    t_len = x.shape[0]
    n_steps = t_len // tm

    def body(du_ref, dx1_ref, x_ref, g_ref, w_ref, hs_ref, gown_ref,
             dx_ref, vec_ref, landed_ref, sib_ref, i_send, i_recv, d_send, d_recv):
        step = pl.program_id(0)
        _host_chip_exchange(step, n_steps, [hs_ref], [landed_ref], i_send, i_recv)
        _host_half_exchange(step, n_steps, gown_ref, sib_ref, d_send, d_recv)

        @pl.when(step == 0)
        def _():
            vec_ref[...] = jnp.zeros(vec_ref.shape, F32)

        dh = jnp.dot(du_ref[...], w_ref[...], preferred_element_type=F32)
        xv = x_ref[...]
        r1 = _rms(xv)
        xh = xv * r1
        vec_ref[0:1, :] += jnp.sum(dh * xh, axis=0, keepdims=True)
        dx_ref[...] = dx1_ref[...] + _rms_bwd(dh, xh, r1, g_ref[...])

    row_tile = lambda w: pl.BlockSpec((tm, w), lambda i: (i, 0))
    half_shape = (g_own.shape[0], g_own.shape[1] // 2, g_own.shape[2])
    return pl.pallas_call(
        body, grid=(n_steps,),
        in_specs=[row_tile(IN_COLS), row_tile(D_MODEL), row_tile(D_MODEL), pl.BlockSpec((1, D_MODEL), lambda i: (0, 0)),
                  pl.BlockSpec((IN_COLS, D_MODEL), lambda i: (0, 0))] + [HBM_SPEC] * 2,
        out_specs=[row_tile(D_MODEL), pl.BlockSpec((SUB, D_MODEL), lambda i: (0, 0))] + [HBM_SPEC] * 2,
        out_shape=[jax.ShapeDtypeStruct((t_len, D_MODEL), F32), jax.ShapeDtypeStruct((SUB, D_MODEL), F32),
                   jax.ShapeDtypeStruct(chip_sums.shape, BF16), jax.ShapeDtypeStruct(half_shape, BF16)],
        scratch_shapes=_exchange_scratch(1, 3) + [pltpu.SemaphoreType.DMA((1,)), pltpu.SemaphoreType.DMA((1,))],
        compiler_params=_params(("arbitrary",), 56), name="in_proj_bwd",
    )(du, dx1, x, g_mix, win_t, chip_sums, g_own)


def _tn_weight_grad(a, b, tk, name, pair=(), col_blocks=1):
    t_len, m = a.shape
    n = b.shape[1]
    n_steps = t_len // tk
    sent = tuple(pair)
    n_sent = len(sent)

    def body(a_ref, b_ref, *rest):
        srcs = rest[0:n_sent]
        o_ref = rest[n_sent]
        dsts = rest[n_sent + 1:2 * n_sent + 1]
        acc = rest[2 * n_sent + 1]
        sems = rest[2 * n_sent + 2:]
        j = pl.program_id(0)
        if pair:
            _host_pair_exchange(j, n_steps, srcs, dsts, *sems)

        @pl.when(j == 0)
        def _():
            acc[...] = jnp.zeros(acc.shape, F32)

        acc[...] += _dot_tn(a_ref[...].astype(BF16), b_ref[...].astype(BF16))

        @pl.when(j == n_steps - 1)
        def _():
            if col_blocks == 1:
                o_ref[...] = acc[...].astype(BF16)
            else:
                for k in range(col_blocks):
                    o_ref[k] = acc[:, k * nb:(k + 1) * nb].astype(BF16)

    nb = n // col_blocks
    out_dims = (m, n) if col_blocks == 1 else (col_blocks, m, nb)
    landed = [jax.ShapeDtypeStruct((4,) + g.shape[1:], BF16) for g in pair]
    scratch = [pltpu.VMEM((m, n), F32)]
    if n_sent:
        scratch += _exchange_scratch(n_sent, 4)
    return pl.pallas_call(
        body, grid=(n_steps,),
        in_specs=[pl.BlockSpec((tk, m), lambda j: (j, 0)), pl.BlockSpec((tk, n), lambda j: (j, 0))]
        + [HBM_SPEC] * n_sent,
        out_specs=[pl.BlockSpec(out_dims, lambda j: (0,) * len(out_dims))] + [HBM_SPEC] * n_sent,
        out_shape=[jax.ShapeDtypeStruct(out_dims, BF16)] + landed,
        scratch_shapes=scratch,
        compiler_params=_params(("arbitrary",), 56), name=name,
    )(a, b, *sent)


def _w_in_grad_part(du, h, tk, name, chip_ids, chip=(), halves=None, small=None):
    t_len = du.shape[0]
    n_t = t_len // tk
    n_q = chip_ids.shape[0]
    width = 2 * (IN_COLS // N_DEV)
    n_steps = n_q * n_t
    n_chip = len(chip)
    sent = tuple(chip) + (() if halves is None else (halves,)) + (() if small is None else tuple(small))
    n_sent = len(sent)

    def body(ids_ref, a_ref, b_ref, *rest):
        srcs = rest[0:n_sent]
        o_ref = rest[n_sent]
        dsts = rest[n_sent + 1:2 * n_sent + 1]
        acc = rest[2 * n_sent + 1]
        sems = list(rest[2 * n_sent + 2:])
        j = pl.program_id(1)
        step = pl.program_id(0) * n_t + j
        if chip:
            _host_chip_exchange(step, n_steps, srcs[0:n_chip], dsts[0:n_chip], sems.pop(0), sems.pop(0))
        if halves is not None:
            _host_half_exchange(step, n_steps, srcs[n_chip], dsts[n_chip], sems.pop(0), sems.pop(0))
        if small is not None:
            _host_small_exchange(step, n_steps, *srcs[n_sent - 3:], *dsts[n_sent - 3:], *sems)

        @pl.when(j == 0)
        def _():
            acc[...] = jnp.zeros(acc.shape, F32)

        acc[...] += _dot_tn(a_ref[...], b_ref[...])

        @pl.when(j == n_t - 1)
        def _():
            o_ref[0] = acc[...].astype(BF16)

    landed = [jax.ShapeDtypeStruct(s.shape, BF16) for s in chip]
    scratch = [pltpu.VMEM((width, D_MODEL), F32)]
    if chip:
        scratch += _exchange_scratch(len(chip), 3)
    if halves is not None:
        landed.append(jax.ShapeDtypeStruct((halves.shape[0], halves.shape[1] // 2, halves.shape[2]), BF16))
        scratch += [pltpu.SemaphoreType.DMA((halves.shape[0],)), pltpu.SemaphoreType.DMA((halves.shape[0],))]
    if small is not None:
        vec_m, vec_b, wab = small
        landed += [jax.ShapeDtypeStruct((N_DEV,) + vec_m.shape, F32), jax.ShapeDtypeStruct((N_DEV,) + vec_b.shape, F32),
                   jax.ShapeDtypeStruct((N_DEV, wab.shape[0] // N_DEV, wab.shape[1]), F32)]
        scratch += _exchange_scratch(3, N_DEV) + [pltpu.SemaphoreType.DMA((2,))]
    grid_spec = pltpu.PrefetchScalarGridSpec(
        num_scalar_prefetch=1, grid=(n_q, n_t),
        in_specs=[pl.BlockSpec((tk, width), lambda q, j, ids: (j, ids[q])),
                  pl.BlockSpec((tk, D_MODEL), lambda q, j, ids: (j, 0))] + [HBM_SPEC] * n_sent,
        out_specs=[pl.BlockSpec((1, width, D_MODEL), lambda q, j, ids: (q, 0, 0))] + [HBM_SPEC] * n_sent,
        scratch_shapes=scratch)
    return pl.pallas_call(
        body, grid_spec=grid_spec, out_shape=[jax.ShapeDtypeStruct((n_q, width, D_MODEL), BF16)] + landed,
        compiler_params=_params(("arbitrary", "arbitrary"), 40), name=name,
    )(chip_ids, du, h, *sent)


def _adamw(w, g, m, v):
    m = ADAM_B1 * m + (1.0 - ADAM_B1) * g
    v = ADAM_B2 * v + (1.0 - ADAM_B2) * (g * g)
    delta = -ADAM_LR * ((m / BC1) / (jnp.sqrt(v / BC2) + ADAM_EPS) + ADAM_WD * w)
    return delta, m, v


def _update_sharded(g, landed, w, m, v, rows_blk, name):
    rows, cols = w.shape

    def body(g_ref, l_ref, w_ref, m_ref, v_ref, og, od, om, ov):
        gv = g_ref[...]
        for j in range(3):
            gv = gv + l_ref[j].astype(F32)
        delta, mn, vn = _adamw(w_ref[...], gv, m_ref[...], v_ref[...])
        og[...] = gv
        od[...] = delta
        om[...] = mn
        ov[...] = vn

    blk = pl.BlockSpec((rows_blk, cols), lambda i: (i, 0))
    shape = pltpu.HBM((rows, cols), F32)
    return pl.pallas_call(
        body, grid=(rows // rows_blk,),
        in_specs=[blk, pl.BlockSpec((3, rows_blk, cols), lambda i: (0, i, 0)), blk, blk, blk],
        out_specs=[blk] * 4, out_shape=[shape] * 4,
        compiler_params=_params(("arbitrary",), 32), name=name,
    )(*_in_hbm(g, landed, w, m, v))


def _update_w_in(g_own, sib_own, landed, w_t, m_t, v_t, core, cols_blk):
    rows, cols = w_t.shape

    def body(core_ref, g_ref, s_ref, l_ref, w_ref, m_ref, v_ref, og, od, om, ov):
        gv = g_ref[0, 0].astype(F32) + s_ref[0].astype(F32)
        for j in range(3):
            gv = gv + l_ref[j].astype(F32)
        delta, mn, vn = _adamw(w_ref[...], gv, m_ref[...], v_ref[...])
        og[...] = gv
        od[...] = delta
        om[...] = mn
        ov[...] = vn

    blk = pl.BlockSpec((rows, cols_blk), lambda i, cr: (0, i))
    grid_spec = pltpu.PrefetchScalarGridSpec(
        num_scalar_prefetch=1, grid=(cols // cols_blk,),
        in_specs=[pl.BlockSpec((1, 1, rows, cols_blk), lambda i, cr: (0, cr[0], 0, i)),
                  pl.BlockSpec((1, rows, cols_blk), lambda i, cr: (0, 0, i)),
                  pl.BlockSpec((3, rows, cols_blk), lambda i, cr: (0, 0, i)), blk, blk, blk],
        out_specs=[blk] * 4)
    return pl.pallas_call(
        body, grid_spec=grid_spec, out_shape=[pltpu.HBM((rows, cols), F32)] * 4,
        compiler_params=_params(("arbitrary",), 32), name="update_w_in",
    )(core, *_in_hbm(g_own.reshape(1, 2, rows, cols), sib_own, landed, w_t, m_t, v_t))


def _update_small(vsum, wsum, g_cw, g_rw, weights, moments_m, moments_v):
    n = len(weights)

    def body(*refs):
        vs, ws, gcw, grw = refs[0:4]
        w_refs = refs[4:4 + n]
        m_refs = refs[4 + n:4 + 2 * n]
        v_refs = refs[4 + 2 * n:4 + 3 * n]
        outs = refs[4 + 3 * n:]
        loss_ref = outs[0]
        loss_ref[...] = jnp.sum(vs[ROW_LOSS:ROW_LOSS + 1, :], axis=1, keepdims=True)
        grads = [
            vs[ROW_GMIX:ROW_GMIX + 1, :], gcw[...], grw[...], vs[ROW_BR:ROW_BR + 1, :],
            ws[0:LRU_WIDTH, :], vs[ROW_BA:ROW_BA + 1, :], ws[LRU_WIDTH:2 * LRU_WIDTH, :], vs[ROW_BX:ROW_BX + 1, :],
            vs[ROW_LAM:ROW_LAM + 1, :], vs[ROW_GNC:ROW_GNC + 1, 0:CONV_WIDTH], vs[ROW_GNR:ROW_GNR + 1, :],
            vs[ROW_GMLP:ROW_GMLP + 1, :], vs[ROW_GF:ROW_GF + 1, :],
        ]
        for k in range(n):
            gk = grads[k]
            delta, mn, vn = _adamw(w_refs[k][...], gk, m_refs[k][...], v_refs[k][...])
            outs[1 + 4 * k][...] = gk
            outs[2 + 4 * k][...] = delta
            outs[3 + 4 * k][...] = mn
            outs[4 + 4 * k][...] = vn

    whole = lambda a: pl.BlockSpec(a.shape, lambda i: (0,) * len(a.shape))
    out_shape = [jax.ShapeDtypeStruct((1, 1), F32)]
    for w in weights:
        out_shape += [jax.ShapeDtypeStruct(w.shape, F32)] * 4
    args = (vsum, wsum, g_cw, g_rw, *weights, *moments_m, *moments_v)
    return pl.pallas_call(
        body, grid=(1,), out_shape=out_shape, in_specs=[whole(a) for a in args], out_specs=[whole(s) for s in out_shape],
        compiler_params=_params(("arbitrary",), 32), name="update_small",
    )(*args)


def kernel(x, norm_mix_g, w_in, conv_w, rnn_conv_w, rnn_conv_b, w_a, b_a, w_x, b_x, lru_lambda, g_norm_conv, g_norm_rnn, w_out, norm_mlp_g, w_mlp_in, w_mlp_out, final_norm_g, loss_target, m_norm_mix_g, m_w_in, m_conv_w, m_rnn_conv_w, m_rnn_conv_b, m_w_a, m_b_a, m_w_x, m_b_x, m_lru_lambda, m_g_norm_conv, m_g_norm_rnn, m_w_out, m_norm_mlp_g, m_w_mlp_in, m_w_mlp_out, m_final_norm_g, v_norm_mix_g, v_w_in, v_conv_w, v_rnn_conv_w, v_rnn_conv_b, v_w_a, v_b_a, v_w_x, v_b_x, v_lru_lambda, v_g_norm_conv, v_g_norm_rnn, v_w_out, v_norm_mlp_g, v_w_mlp_in, v_w_mlp_out, v_final_norm_g):
    t_len = x.shape[1]
    my_id = 4 * lax.axis_index("x") + 2 * lax.axis_index("y") + lax.axis_index("c")
    tm = min(256, t_len)
    tb = min(512, t_len)
    tk = min(512, t_len)

    xs = x.reshape(t_len, D_MODEL)
    tgt = loss_target.reshape(t_len, D_MODEL)
    flat = lambda a: a.reshape(a.shape[-2:]) if a.ndim == 3 else a.reshape(1, -1)
    heads = lambda a: a.reshape(LRU_WIDTH, HEAD_DIM)

    turned = lambda a: jnp.transpose(flat(a))
    win_shard, wout_shard, w1_shard, w2_shard, cp_shard = _prep_shards(
        turned(w_in), flat(w_out), flat(w_mlp_in), flat(w_mlp_out), flat(conv_w), flat(rnn_conv_w))

    u, h, win_t, cp_full = _in_proj(xs, flat(norm_mix_g), (win_shard, cp_shard), min(1024, t_len))
    cpack = cp_full.reshape(N_DEV, 8, 128)
    conv_full = jnp.transpose(cpack[:, 0:3, 0:64], (1, 0, 2)).reshape(3, CONV_WIDTH)
    rnn_full = jnp.transpose(cpack[:, 3:7, :], (1, 0, 2)).reshape(4, LRU_WIDTH)
    mixer_small = (conv_full, rnn_full, flat(rnn_conv_b), heads(w_a), flat(b_a), heads(w_x), flat(b_x),
                   flat(lru_lambda), flat(g_norm_conv), flat(g_norm_rnn))
    hs, y, xr, gate_r, gate_i, mult, cq, w1_blk, wout_blk = _mixer_fwd(u, *mixer_small, (w1_shard, wout_shard), tm)
    wout_f = wout_blk.reshape(MIX_WIDTH, D_MODEL)
    x1, h2, z, w2_blk = _mlp_up(xs, y, flat(norm_mlp_g), wout_f, w1_blk, w2_shard, tb)
    dx1, dx2, vec_m, dpre = _mlp_down_bwd(x1, z, tgt, flat(norm_mlp_g), flat(final_norm_g), w1_blk,
                                          w2_blk.reshape(D_FF, D_MODEL), tb)
    (g_w1,) = _tn_weight_grad(h2, dpre, tk, "w_mlp_in_grad", col_blocks=N_DEV)
    (g_w2,) = _tn_weight_grad(z, dx2, tk, "w_mlp_out_grad")
    g_w2 = g_w2.reshape(N_DEV, D_FF // N_DEV, D_MODEL)
    g_wout, sib_w1, sib_w2 = _tn_weight_grad(y, dx1, tk, "w_out_grad", pair=(g_w1, g_w2))
    g_wout = g_wout.reshape(N_DEV, MIX_WIDTH // N_DEV, D_MODEL)
    hsend_w1, own_w1, hsend_w2, own_w2 = _pair_sum((g_w1, g_w2), (sib_w1, sib_w2), "pair_sum_w_mlp")
    du, vec_b, wab, landed_w1, landed_w2, sib_wout = _mixer_bwd(
        u, hs, dx1, (xr, gate_r, gate_i, mult, cq), conv_full, rnn_full, heads(w_a), heads(w_x),
        flat(lru_lambda), flat(g_norm_conv), flat(g_norm_rnn), wout_f, (hsend_w1, hsend_w2), g_wout, tm)
    hsend_wout, own_wout = _pair_sum((g_wout,), (sib_wout,), "pair_sum_w_out")
    ax, ay, ac = lax.axis_index("x"), lax.axis_index("y"), lax.axis_index("c")
    chip_ids = jnp.stack([2 * cx + cy for cx, cy in [(ax, ay)] + _other_chips(ax, ay)]).astype(jnp.int32)
    core = jnp.reshape(ac, (1,)).astype(jnp.int32)
    tw = min(1024, t_len)
    g_others, landed_wout, vrecv_m, vrecv_b, wrecv = _w_in_grad_part(
        du, h, tw, "w_in_grad_others", chip_ids[1:4], chip=(hsend_wout,), small=(vec_m, vec_b, wab))
    g_own, sib_others = _w_in_grad_part(du, h, tw, "w_in_grad_own", chip_ids[0:1], halves=g_others)
    hsend_win = _pair_sum_parts(g_others, sib_others, core)
    grad_x, vec_x, landed_win, sib_own = _in_proj_bwd(du, dx1, xs, flat(norm_mix_g), win_t, tm, hsend_win, g_own)

    vsum, wsum = _final_small(vrecv_m, vrecv_b, wab, wrecv, vec_x)

    up_win = _update_w_in(g_own, sib_own, landed_win, turned(w_in), turned(m_w_in), turned(v_w_in), core, 256)
    up_win = [jnp.transpose(a) for a in up_win]
    up_wout = _update_sharded(own_wout, landed_wout, flat(w_out), flat(m_w_out), flat(v_w_out), 96, "update_w_out")
    up_w1 = _update_sharded(own_w1, landed_w1, flat(w_mlp_in), flat(m_w_mlp_in), flat(v_w_mlp_in), 256,
                            "update_w_mlp_in")
    up_w2 = _update_sharded(own_w2, landed_w2, flat(w_mlp_out), flat(m_w_mlp_out), flat(v_w_mlp_out), 256,
                            "update_w_mlp_out")

    g_cw = lax.dynamic_slice(vsum, (ROW_CW, 64 * my_id), (3, 64))
    g_rw = lax.dynamic_slice(vsum, (ROW_RW, 128 * my_id), (4, 128))
    small_w = (norm_mix_g, conv_w, rnn_conv_w, rnn_conv_b, w_a, b_a, w_x, b_x, lru_lambda, g_norm_conv, g_norm_rnn,
               norm_mlp_g, final_norm_g)
    small_m = (m_norm_mix_g, m_conv_w, m_rnn_conv_w, m_rnn_conv_b, m_w_a, m_b_a, m_w_x, m_b_x, m_lru_lambda,
               m_g_norm_conv, m_g_norm_rnn, m_norm_mlp_g, m_final_norm_g)
    small_v = (v_norm_mix_g, v_conv_w, v_rnn_conv_w, v_rnn_conv_b, v_w_a, v_b_a, v_w_x, v_b_x, v_lru_lambda,
               v_g_norm_conv, v_g_norm_rnn, v_norm_mlp_g, v_final_norm_g)
    is_heads = (False, False, False, False, True, False, True, False, False, False, False, False, False)
    as2d = lambda arrs: [heads(a) if hd else flat(a) for a, hd in zip(arrs, is_heads)]
    small_out = _update_small(vsum, wsum, g_cw, g_rw, as2d(small_w), as2d(small_m), as2d(small_v))
    loss = small_out[0].reshape(())

    names = ["norm_mix_g", "w_in", "conv_w", "rnn_conv_w", "rnn_conv_b", "w_a", "b_a", "w_x", "b_x", "lru_lambda",
             "g_norm_conv", "g_norm_rnn", "w_out", "norm_mlp_g", "w_mlp_in", "w_mlp_out", "final_norm_g"]
    originals = dict(zip(names, (norm_mix_g, w_in, conv_w, rnn_conv_w, rnn_conv_b, w_a, b_a, w_x, b_x, lru_lambda,
                                 g_norm_conv, g_norm_rnn, w_out, norm_mlp_g, w_mlp_in, w_mlp_out, final_norm_g)))
    results = {"w_in": up_win, "w_out": up_wout, "w_mlp_in": up_w1, "w_mlp_out": up_w2}
    small_names = ["norm_mix_g", "conv_w", "rnn_conv_w", "rnn_conv_b", "w_a", "b_a", "w_x", "b_x", "lru_lambda",
                   "g_norm_conv", "g_norm_rnn", "norm_mlp_g", "final_norm_g"]
    for k, nm in enumerate(small_names):
        results[nm] = small_out[1 + 4 * k:5 + 4 * k]
    out = [loss, grad_x.reshape(x.shape)]
    for kind in range(4):
        out += [results[nm][kind].reshape(originals[nm].shape) for nm in names]
    return tuple(out)
```
